```python
import jax, jax.numpy as jnp
from jax import lax
import numpy as np

D_MODEL = 1024
BATCH = 8
SEQ = 4096
DEPTH = 1

GLA_HEADS = 4
GLA_DK = 128
GLA_DV = 256
GLA_QK_W = GLA_HEADS * GLA_DK
GLA_V_W = GLA_HEADS * GLA_DV
GATE_RANK = 16
GATE_NORM = 16.0
CHUNK = 64
CONV_W = 1024
CONV_WIDTH = 3
MIX_W = GLA_V_W + CONV_W
SPLITS = [GLA_QK_W, GLA_QK_W, GLA_V_W, GLA_V_W, GATE_RANK, GATE_RANK,
          CONV_W, CONV_W, CONV_W, CONV_W]
IN_W = sum(SPLITS)
EPS = 1e-6

kernel_name = "hybrid_gla_shortconv_parallel_heads"


def rmsnorm(x, g):
    xf = x.astype(jnp.float32)
    y = xf * lax.rsqrt(jnp.mean(xf * xf, axis=-1, keepdims=True) + EPS)
    return (y * g.astype(jnp.float32)).astype(x.dtype)


def gla_direction(q, k, v, g, strict):
    bsz, nh, s, dk = q.shape
    dv = v.shape[-1]
    n = s // CHUNK
    q = q.reshape(bsz, nh, n, CHUNK, dk)
    k = k.reshape(bsz, nh, n, CHUNK, dk)
    v = v.reshape(bsz, nh, n, CHUNK, dv)
    g = g.reshape(bsz, nh, n, CHUNK, dk)
    b = jnp.cumsum(g, axis=3)
    b_ref = b[:, :, :, CHUNK // 2:CHUNK // 2 + 1, :]
    att = jnp.einsum('bhncd,bhnjd->bhncj', q * jnp.exp(b - b_ref), k * jnp.exp(b_ref - b))
    mask = jnp.tril(jnp.ones((CHUNK, CHUNK), dtype=bool), k=-1 if strict else 0)
    att = jnp.where(mask, att, 0.0)
    o_intra = jnp.einsum('bhncj,bhnjv->bhncv', att, v)
    b_last = b[:, :, :, -1:, :]
    q_in = q * jnp.exp(b)
    k_out = k * jnp.exp(b_last - b)
    decay_chunk = jnp.exp(b_last[:, :, :, 0, :])
    xs = (jnp.moveaxis(q_in, 2, 0), jnp.moveaxis(k_out, 2, 0),
          jnp.moveaxis(v, 2, 0), jnp.moveaxis(decay_chunk, 2, 0))

    def step(state, inp):
        qc, kc, vc, dc = inp
        o = jnp.einsum('bhcd,bhdv->bhcv', qc, state)
        state = dc[..., None] * state + jnp.einsum('bhcd,bhcv->bhdv', kc, vc)
        return state, o

    s0 = jnp.zeros((bsz, nh, dk, dv), jnp.float32)
    _, o_inter = lax.scan(step, s0, xs)
    o = o_intra + jnp.moveaxis(o_inter, 0, 2)
    return o.reshape(bsz, nh, s, dv)


def to_heads(t, d):
    bsz, s, _ = t.shape
    return t.reshape(bsz, s, -1, d).transpose(0, 2, 1, 3)


def hybrid_mixer(h, w_in, w_gk_f, b_gk_f, w_gk_b, b_gk_b, gla_norm_g, conv_w, conv_b, w_out):
    bsz, s, _ = h.shape
    proj = jnp.einsum('bsd,de->bse', h, w_in)
    idx = np.cumsum(SPLITS)[:-1].tolist()
    (q, k, v, z_a, lr_f, lr_b, b_gate, c_gate, h_c, z_c) = jnp.split(proj, idx, axis=-1)
    f32 = jnp.float32
    q = to_heads(q.astype(f32), GLA_DK) * (GLA_DK ** -0.5)
    k = to_heads(k.astype(f32), GLA_DK)
    v = to_heads(v.astype(f32), GLA_DV)
    g_f = jax.nn.log_sigmoid(jnp.einsum('bsr,re->bse', lr_f.astype(f32), w_gk_f.astype(f32))
                             + b_gk_f.astype(f32)) / GATE_NORM
    g_b = jax.nn.log_sigmoid(jnp.einsum('bsr,re->bse', lr_b.astype(f32), w_gk_b.astype(f32))
                             + b_gk_b.astype(f32)) / GATE_NORM
    g_f = to_heads(g_f, GLA_DK)
    g_b = to_heads(g_b, GLA_DK)
    o_fwd = gla_direction(q, k, v, g_f, strict=False)
    flip = lambda t: jnp.flip(t, axis=2)
    o_bwd = flip(gla_direction(flip(q), flip(k), flip(v), flip(g_b), strict=True))
    o = o_fwd + o_bwd
    o = o * lax.rsqrt(jnp.mean(o * o, axis=-1, keepdims=True) + EPS) * gla_norm_g.astype(f32)
    y_a = o.transpose(0, 2, 1, 3).reshape(bsz, s, GLA_V_W)
    y_a = (y_a * jax.nn.silu(z_a.astype(f32))).astype(h.dtype)
    u = c_gate * h_c
    up = jnp.pad(u, ((0, 0), (1, 1), (0, 0)))
    conv = (conv_w[0] * up[:, :-2] + conv_w[1] * up[:, 1:-1] + conv_w[2] * up[:, 2:]) + conv_b
    y_c = b_gate * conv * jax.nn.silu(z_c)
    y = jnp.concatenate([y_a, y_c.astype(h.dtype)], axis=-1)
    return jnp.einsum('bse,ed->bsd', y, w_out)


def _fwd_setup_inputs(seed: int = 0) -> dict:
    key = jax.random.key(seed)
    ks = jax.random.split(key, 14)
    nrm = lambda k_, shp, sc: jax.random.normal(k_, shp, jnp.float32) * sc
    return {
        "x": nrm(ks[0], (BATCH, SEQ, D_MODEL), 1.0),
        "norm_g": 1.0 + nrm(ks[1], (DEPTH, D_MODEL), 0.02),
        "w_in": nrm(ks[2], (DEPTH, D_MODEL, IN_W), D_MODEL ** -0.5),
        "w_gk_f": nrm(ks[3], (DEPTH, GATE_RANK, GLA_QK_W), GATE_RANK ** -0.5),
        "b_gk_f": nrm(ks[4], (DEPTH, GLA_QK_W), 0.1),
        "w_gk_b": nrm(ks[5], (DEPTH, GATE_RANK, GLA_QK_W), GATE_RANK ** -0.5),
        "b_gk_b": nrm(ks[6], (DEPTH, GLA_QK_W), 0.1),
        "gla_norm_g": 1.0 + nrm(ks[7], (DEPTH, GLA_DV), 0.02),
        "conv_w": nrm(ks[8], (DEPTH, CONV_WIDTH, CONV_W), CONV_WIDTH ** -0.5),
        "conv_b": nrm(ks[9], (DEPTH, CONV_W), 0.02),
        "w_out": nrm(ks[10], (DEPTH, MIX_W, D_MODEL), MIX_W ** -0.5),
        "final_g": 1.0 + nrm(ks[11], (D_MODEL,), 0.02),
    }


def _fwd_reference(x, norm_g, w_in, w_gk_f, b_gk_f, w_gk_b, b_gk_b, gla_norm_g, conv_w, conv_b, w_out, final_g):
    for layer in range(DEPTH):
        h = rmsnorm(x, norm_g[layer])
        x = x + hybrid_mixer(h, w_in[layer], w_gk_f[layer], b_gk_f[layer], w_gk_b[layer],
                             b_gk_b[layer], gla_norm_g[layer], conv_w[layer], conv_b[layer],
                             w_out[layer])
    return rmsnorm(x, final_g)


import jax as _jax
import jax.numpy as _jnp

TWIN_FORMAT = 'train_step'
FWD_PARAMS = ['x', 'norm_g', 'w_in', 'w_gk_f', 'b_gk_f', 'w_gk_b', 'b_gk_b', 'gla_norm_g', 'conv_w', 'conv_b', 'w_out', 'final_g']
TWIN_WEIGHTS = ['norm_g', 'w_in', 'w_gk_f', 'b_gk_f', 'w_gk_b', 'b_gk_b', 'gla_norm_g', 'conv_w', 'conv_b', 'w_out', 'final_g']
TWIN_DIFF_INPUT = 'x'
TWIN_INPUTS = ['x', 'norm_g', 'w_in', 'w_gk_f', 'b_gk_f', 'w_gk_b', 'b_gk_b', 'gla_norm_g', 'conv_w', 'conv_b', 'w_out', 'final_g', 'loss_target', 'm_norm_g', 'm_w_in', 'm_w_gk_f', 'm_b_gk_f', 'm_w_gk_b', 'm_b_gk_b', 'm_gla_norm_g', 'm_conv_w', 'm_conv_b', 'm_w_out', 'm_final_g', 'v_norm_g', 'v_w_in', 'v_w_gk_f', 'v_b_gk_f', 'v_w_gk_b', 'v_b_gk_b', 'v_gla_norm_g', 'v_conv_w', 'v_conv_b', 'v_w_out', 'v_final_g']
TWIN_OUTPUTS = ['loss', 'grad_x', 'grad_norm_g', 'grad_w_in', 'grad_w_gk_f', 'grad_b_gk_f', 'grad_w_gk_b', 'grad_b_gk_b', 'grad_gla_norm_g', 'grad_conv_w', 'grad_conv_b', 'grad_w_out', 'grad_final_g', 'delta_norm_g', 'delta_w_in', 'delta_w_gk_f', 'delta_b_gk_f', 'delta_w_gk_b', 'delta_b_gk_b', 'delta_gla_norm_g', 'delta_conv_w', 'delta_conv_b', 'delta_w_out', 'delta_final_g', 'new_m_norm_g', 'new_m_w_in', 'new_m_w_gk_f', 'new_m_b_gk_f', 'new_m_w_gk_b', 'new_m_b_gk_b', 'new_m_gla_norm_g', 'new_m_conv_w', 'new_m_conv_b', 'new_m_w_out', 'new_m_final_g', 'new_v_norm_g', 'new_v_w_in', 'new_v_w_gk_f', 'new_v_b_gk_f', 'new_v_w_gk_b', 'new_v_b_gk_b', 'new_v_gla_norm_g', 'new_v_conv_w', 'new_v_conv_b', 'new_v_w_out', 'new_v_final_g']
TWIN_LEAF_KINDS = {'loss': 'loss', 'grad_x': 'grad_x', 'grad_norm_g': 'grad_w', 'grad_w_in': 'grad_w', 'grad_w_gk_f': 'grad_w', 'grad_b_gk_f': 'grad_w', 'grad_w_gk_b': 'grad_w', 'grad_b_gk_b': 'grad_w', 'grad_gla_norm_g': 'grad_w', 'grad_conv_w': 'grad_w', 'grad_conv_b': 'grad_w', 'grad_w_out': 'grad_w', 'grad_final_g': 'grad_w', 'delta_norm_g': 'delta_w', 'delta_w_in': 'delta_w', 'delta_w_gk_f': 'delta_w', 'delta_b_gk_f': 'delta_w', 'delta_w_gk_b': 'delta_w', 'delta_b_gk_b': 'delta_w', 'delta_gla_norm_g': 'delta_w', 'delta_conv_w': 'delta_w', 'delta_conv_b': 'delta_w', 'delta_w_out': 'delta_w', 'delta_final_g': 'delta_w', 'new_m_norm_g': 'new_m', 'new_m_w_in': 'new_m', 'new_m_w_gk_f': 'new_m', 'new_m_b_gk_f': 'new_m', 'new_m_w_gk_b': 'new_m', 'new_m_b_gk_b': 'new_m', 'new_m_gla_norm_g': 'new_m', 'new_m_conv_w': 'new_m', 'new_m_conv_b': 'new_m', 'new_m_w_out': 'new_m', 'new_m_final_g': 'new_m', 'new_v_norm_g': 'new_v', 'new_v_w_in': 'new_v', 'new_v_w_gk_f': 'new_v', 'new_v_b_gk_f': 'new_v', 'new_v_w_gk_b': 'new_v', 'new_v_b_gk_b': 'new_v', 'new_v_gla_norm_g': 'new_v', 'new_v_conv_w': 'new_v', 'new_v_conv_b': 'new_v', 'new_v_w_out': 'new_v', 'new_v_final_g': 'new_v'}


def _forward(args):
    return _fwd_reference(*[args[k] for k in FWD_PARAMS])


def _output_shape():
    out = _jax.eval_shape(lambda: _forward(_fwd_setup_inputs(0)))
    return out.shape, out.dtype

N_MICROBATCH = 1
ADAM_LR = 0.001
ADAM_B1 = 0.9
ADAM_B2 = 0.999
ADAM_EPS = 1e-08
ADAM_WD = 0.01
ADAM_STEP = 10
PER_EXAMPLE_BATCH_AXIS = {'x': 0, 'loss_target': 0}
SHARED_INPUTS = []
_WEIGHT_DTYPES = {'norm_g': _jnp.float32, 'w_in': _jnp.float32, 'w_gk_f': _jnp.float32, 'b_gk_f': _jnp.float32, 'w_gk_b': _jnp.float32, 'b_gk_b': _jnp.float32, 'gla_norm_g': _jnp.float32, 'conv_w': _jnp.float32, 'conv_b': _jnp.float32, 'w_out': _jnp.float32, 'final_g': _jnp.float32}
MOMENT_SCALE = {'norm_g': 1.966965e-01, 'w_in': 6.956783e-02, 'w_gk_f': 7.557302e-03, 'b_gk_f': 3.105335e-02, 'w_gk_b': 7.678942e-03, 'b_gk_b': 3.293219e-02, 'gla_norm_g': 1.618539e-01, 'conv_w': 6.599263e-02, 'conv_b': 6.415089e-02, 'w_out': 8.996999e-02, 'final_g': 3.200818e+01}


def _to_microbatches(a, axis):
    t = _jnp.moveaxis(a, axis, 0)
    t = t.reshape((N_MICROBATCH, t.shape[0] // N_MICROBATCH) + t.shape[1:])
    return _jnp.moveaxis(t, 1, axis + 1)


def setup_inputs(seed: int = 0) -> dict:
    inp = _fwd_setup_inputs(seed)
    key = _jax.random.fold_in(_jax.random.key(seed), 7919)
    shape, _ = _output_shape()
    out = dict(inp)
    out["loss_target"] = _jax.random.normal(_jax.random.fold_in(key, 0), shape, _jnp.float32)
    for i, name in enumerate(TWIN_WEIGHTS):
        w = inp[name].astype(_jnp.float32)
        if MOMENT_SCALE is None:
            s = _jnp.sqrt(_jnp.mean(_jnp.square(w)) + 1e-30)
        else:
            s = MOMENT_SCALE[name]
        km, kv = _jax.random.split(_jax.random.fold_in(key, i + 1))
        out[name] = w
        out["m_" + name] = s * _jax.random.normal(km, w.shape, _jnp.float32)
        out["v_" + name] = (s * s) * _jax.random.uniform(kv, w.shape, _jnp.float32, 0.5, 1.5)
    if N_MICROBATCH > 1:
        for name, axis in PER_EXAMPLE_BATCH_AXIS.items():
            out[name] = _to_microbatches(out[name], axis)
    return {'x': out['x'], 'norm_g': out['norm_g'], 'w_in': out['w_in'], 'w_gk_f': out['w_gk_f'], 'b_gk_f': out['b_gk_f'], 'w_gk_b': out['w_gk_b'], 'b_gk_b': out['b_gk_b'], 'gla_norm_g': out['gla_norm_g'], 'conv_w': out['conv_w'], 'conv_b': out['conv_b'], 'w_out': out['w_out'], 'final_g': out['final_g'], 'loss_target': out['loss_target'], 'm_norm_g': out['m_norm_g'], 'm_w_in': out['m_w_in'], 'm_w_gk_f': out['m_w_gk_f'], 'm_b_gk_f': out['m_b_gk_f'], 'm_w_gk_b': out['m_w_gk_b'], 'm_b_gk_b': out['m_b_gk_b'], 'm_gla_norm_g': out['m_gla_norm_g'], 'm_conv_w': out['m_conv_w'], 'm_conv_b': out['m_conv_b'], 'm_w_out': out['m_w_out'], 'm_final_g': out['m_final_g'], 'v_norm_g': out['v_norm_g'], 'v_w_in': out['v_w_in'], 'v_w_gk_f': out['v_w_gk_f'], 'v_b_gk_f': out['v_b_gk_f'], 'v_w_gk_b': out['v_w_gk_b'], 'v_b_gk_b': out['v_b_gk_b'], 'v_gla_norm_g': out['v_gla_norm_g'], 'v_conv_w': out['v_conv_w'], 'v_conv_b': out['v_conv_b'], 'v_w_out': out['v_w_out'], 'v_final_g': out['v_final_g']}


def _loss(weights, diff, rest, loss_target):
    with _jax.named_scope("forward"):
        args = {**rest, TWIN_DIFF_INPUT: diff, **{k: w.astype(_WEIGHT_DTYPES[k]) for k, w in weights.items()}}
        y = _forward(args)
    with _jax.named_scope("loss_head"):
        err = _jnp.square(y.astype(_jnp.float32) - loss_target)
        return 0.5 * _jnp.sum(_jnp.mean(err, axis=-1)) if err.ndim else 0.5 * err


def _adamw(w, g, m, v):
    m = ADAM_B1 * m + (1.0 - ADAM_B1) * g
    v = ADAM_B2 * v + (1.0 - ADAM_B2) * _jnp.square(g)
    m_hat = m / (1.0 - ADAM_B1 ** ADAM_STEP)
    v_hat = v / (1.0 - ADAM_B2 ** ADAM_STEP)
    delta = -ADAM_LR * (m_hat / (_jnp.sqrt(v_hat) + ADAM_EPS) + ADAM_WD * w)
    return delta, m, v


def reference(x, norm_g, w_in, w_gk_f, b_gk_f, w_gk_b, b_gk_b, gla_norm_g, conv_w, conv_b, w_out, final_g, loss_target, m_norm_g, m_w_in, m_w_gk_f, m_b_gk_f, m_w_gk_b, m_b_gk_b, m_gla_norm_g, m_conv_w, m_conv_b, m_w_out, m_final_g, v_norm_g, v_w_in, v_w_gk_f, v_b_gk_f, v_w_gk_b, v_b_gk_b, v_gla_norm_g, v_conv_w, v_conv_b, v_w_out, v_final_g):
    given = dict(x=x, norm_g=norm_g, w_in=w_in, w_gk_f=w_gk_f, b_gk_f=b_gk_f, w_gk_b=w_gk_b, b_gk_b=b_gk_b, gla_norm_g=gla_norm_g, conv_w=conv_w, conv_b=conv_b, w_out=w_out, final_g=final_g, loss_target=loss_target, m_norm_g=m_norm_g, m_w_in=m_w_in, m_w_gk_f=m_w_gk_f, m_b_gk_f=m_b_gk_f, m_w_gk_b=m_w_gk_b, m_b_gk_b=m_b_gk_b, m_gla_norm_g=m_gla_norm_g, m_conv_w=m_conv_w, m_conv_b=m_conv_b, m_w_out=m_w_out, m_final_g=m_final_g, v_norm_g=v_norm_g, v_w_in=v_w_in, v_w_gk_f=v_w_gk_f, v_b_gk_f=v_b_gk_f, v_w_gk_b=v_w_gk_b, v_b_gk_b=v_b_gk_b, v_gla_norm_g=v_gla_norm_g, v_conv_w=v_conv_w, v_conv_b=v_conv_b, v_w_out=v_w_out, v_final_g=v_final_g)
    weights = {n: given[n] for n in TWIN_WEIGHTS}
    shared = {n: given[n] for n in SHARED_INPUTS}
    per_example = {n: given[n] for n in ['x']}
    grad_fn = _jax.value_and_grad(_loss, argnums=(0, 1))

    def one_microbatch(ex, loss_target):
        ex = dict(ex)
        diff = ex.pop(TWIN_DIFF_INPUT)
        return grad_fn(weights, diff, {**shared, **ex}, loss_target)

    if N_MICROBATCH == 1:
        loss, (grad_w, grad_x) = one_microbatch(per_example, given["loss_target"])
    else:
        def body(carry, xs):
            loss_sum, grad_sum = carry
            l_k, (gw_k, gx_k) = one_microbatch(xs[0], xs[1])
            with _jax.named_scope("update"):
                return (loss_sum + l_k, _jax.tree.map(_jnp.add, grad_sum, gw_k)), gx_k

        init = (_jnp.zeros((), _jnp.float32), _jax.tree.map(_jnp.zeros_like, weights))
        (loss, grad_w), grad_x = _jax.lax.scan(body, init, (per_example, given["loss_target"]))
    with _jax.named_scope("update"):
        delta_w, new_m, new_v = {}, {}, {}
        for n in TWIN_WEIGHTS:
            delta_w[n], new_m[n], new_v[n] = _adamw(weights[n], grad_w[n], given["m_" + n], given["v_" + n])
    return (loss, grad_x, *[grad_w[n] for n in TWIN_WEIGHTS], *[delta_w[n] for n in TWIN_WEIGHTS],
            *[new_m[n] for n in TWIN_WEIGHTS], *[new_v[n] for n in TWIN_WEIGHTS])
```

```python
import jax
import jax.numpy as jnp
from jax import lax
from jax.experimental import pallas as pl
from jax.experimental.pallas import tpu as pltpu

F32 = jnp.float32
BF16 = jnp.bfloat16
MESH = pl.DeviceIdType.MESH

N_DEV = 8
D_MODEL = 1024
HEADS = 4
DK = 128
DV = 256
QK_W = HEADS * DK
V_W = HEADS * DV
CONV_W = 1024
MIX_W = V_W + CONV_W
CHUNK = 64
RANK = 16
IN_W = 7200
SHARD_W = IN_W // N_DEV
MAIN_W = 7168
LR_W = 128
OFF_Q, OFF_K, OFF_V, OFF_ZA, OFF_B, OFF_ZC, OFF_C, OFF_H = 0, 512, 1024, 2048, 3072, 4096, 5120, 6144
QKV_W, GATES_W, CH_W = 2048, 3072, 2048
NAT_ZA, NAT_LR, NAT_B, NAT_C, NAT_ZC = 2048, 3072, 3104, 4128, 6176
EPS = 1e-6
GATE_SCALE = 1.0 / 16.0
QSCALE = DK ** -0.5
REF_F, LAST_F = CHUNK // 2, CHUNK - 1
REF_B, LAST_B = CHUNK - 1 - CHUNK // 2, 0

ADAM_LR = 0.001
ADAM_B1 = 0.9
ADAM_B2 = 0.999
ADAM_EPS = 1e-08
ADAM_WD = 0.01
ADAM_STEP = 10

VMEM_LIMIT = 56 * 1024 * 1024


def _cparams(*sem):
    return pltpu.CompilerParams(dimension_semantics=sem, vmem_limit_bytes=VMEM_LIMIT)


def _dot(a, b):
    return jnp.dot(a, b, preferred_element_type=F32)


def _dot_nt(a, b):
    return lax.dot_general(a, b, (((1,), (1,)), ((), ())), preferred_element_type=F32)


def _dot_tn(a, b):
    return lax.dot_general(a, b, (((0,), (0,)), ((), ())), preferred_element_type=F32)


def _dot_exact(a, b):
    return jnp.dot(a, b, preferred_element_type=F32, precision=lax.Precision.HIGHEST)


def _sigmoid(z):
    return jax.nn.sigmoid(z)


def _position():
    return lax.axis_index("x"), lax.axis_index("y"), lax.axis_index("c")


def _blk(px, py, pc):
    return 4 * px + 2 * py + pc


def _two_level_gather(outs, send_sems, recv_sems):
    x, y, c = _position()
    me, sibling = (x, y, c), (x, y, 1 - c)
    chips = [(1 - x, y), (x, 1 - y), (1 - x, 1 - y)]
    n = len(outs)

    def copy(a, k, block, to):
        ref = outs[a].at[_blk(*block)]
        return pltpu.make_async_remote_copy(src_ref=ref, dst_ref=ref, send_sem=send_sems.at[a * 7 + k],
                                            recv_sem=recv_sems.at[a * 7 + k], device_id=to, device_id_type=MESH)

    first = []
    for a in range(n):
        first.append(copy(a, 0, me, sibling))
        first += [copy(a, 1 + j, me, (*chip, c)) for j, chip in enumerate(chips)]
    for cp in first:
        cp.start()
    passed = []
    for j, chip in enumerate(chips):
        for a in range(n):
            copy(a, 1 + j, (*chip, c), me).wait_recv()
            fwd = copy(a, 4 + j, (*chip, c), sibling)
            fwd.start()
            passed.append(fwd)
    for a in range(n):
        copy(a, 0, sibling, me).wait_recv()
    for j, chip in enumerate(chips):
        for a in range(n):
            copy(a, 4 + j, (*chip, 1 - c), me).wait_recv()
    for cp in first + passed:
        cp.wait_send()


def _allgather_weights(w_in_s, w_out_s, small_s):
    def body(win_ref, wout_ref, sm_ref, win_all, wout_all, sm_all, send_sems, recv_sems):
        mine = _blk(*_position())
        win_all[mine] = win_ref[...].astype(BF16)
        wout_all[mine] = wout_ref[...].astype(BF16)
        sm_all[mine] = sm_ref[...]
        _two_level_gather((win_all, wout_all, sm_all), send_sems, recv_sems)

    vmem = pl.BlockSpec(memory_space=pltpu.VMEM)
    return pl.pallas_call(
        body, name="allgather_weights",
        out_shape=(jax.ShapeDtypeStruct((N_DEV,) + w_in_s.shape, BF16),
                   jax.ShapeDtypeStruct((N_DEV,) + w_out_s.shape, BF16),
                   jax.ShapeDtypeStruct((N_DEV,) + small_s.shape, F32)),
        in_specs=[vmem, vmem, vmem], out_specs=(vmem, vmem, vmem),
        scratch_shapes=[pltpu.SemaphoreType.DMA((21,)), pltpu.SemaphoreType.DMA((21,))],
        compiler_params=pltpu.CompilerParams(vmem_limit_bytes=VMEM_LIMIT),
    )(w_in_s, w_out_s, small_s)


def _allreduce_small(part):
    def body(p_ref, tot_ref, all_ref, send_sems, recv_sems):
        mine = _blk(*_position())
        all_ref[mine] = p_ref[...]
        _two_level_gather((all_ref,), send_sems, recv_sems)
        acc = all_ref[0]
        for d in range(1, N_DEV):
            acc = acc + all_ref[d]
        tot_ref[...] = acc

    vmem = pl.BlockSpec(memory_space=pltpu.VMEM)
    return pl.pallas_call(
        body, name="allreduce_small",
        out_shape=jax.ShapeDtypeStruct(part.shape, F32),
        in_specs=[vmem], out_specs=vmem,
        scratch_shapes=[pltpu.VMEM((N_DEV,) + part.shape, F32),
                        pltpu.SemaphoreType.DMA((7,)), pltpu.SemaphoreType.DMA((7,))],
        compiler_params=pltpu.CompilerParams(vmem_limit_bytes=VMEM_LIMIT),
    )(part)


def _exchange_sibling(parts):
    n = len(parts)

    def body(*refs):
        ins, outs = refs[:n], refs[n:2 * n]
        send_sems, recv_sems = refs[2 * n], refs[2 * n + 1]
        x, y, c = _position()
        sibling = (x, y, 1 - c)
        copies = []
        for a in range(n):
            for k in range(4):
                copies.append(pltpu.make_async_remote_copy(
                    src_ref=ins[a].at[2 * k + (1 - c)], dst_ref=outs[a].at[k],
                    send_sem=send_sems.at[a * 4 + k], recv_sem=recv_sems.at[a * 4 + k],
                    device_id=sibling, device_id_type=MESH))
        for cp in copies:
            cp.start()
        for cp in copies:
            cp.wait_recv()
        for cp in copies:
            cp.wait_send()

    hbm = pl.BlockSpec(memory_space=pl.ANY)
    return pl.pallas_call(
        body, name="exchange_sibling",
        out_shape=tuple(jax.ShapeDtypeStruct((4,) + p.shape[1:], F32) for p in parts),
        in_specs=[hbm] * n, out_specs=tuple([hbm] * n),
        scratch_shapes=[pltpu.SemaphoreType.DMA((4 * n,)), pltpu.SemaphoreType.DMA((4 * n,))],
    )(*parts)


def _exchange_chips(sums):
    n = len(sums)

    def body(*refs):
        ins, outs = refs[:n], refs[n:2 * n]
        send_sems, recv_sems = refs[2 * n], refs[2 * n + 1]
        x, y, c = _position()
        chips = [(1 - x, y), (x, 1 - y), (1 - x, 1 - y)]
        copies = []
        for a in range(n):
            for j, (px, py) in enumerate(chips):
                copies.append(pltpu.make_async_remote_copy(
                    src_ref=ins[a].at[2 * px + py], dst_ref=outs[a].at[j],
                    send_sem=send_sems.at[a * 3 + j], recv_sem=recv_sems.at[a * 3 + j],
                    device_id=(px, py, c), device_id_type=MESH))
        for cp in copies:
            cp.start()
        for cp in copies:
            cp.wait_recv()
        for cp in copies:
            cp.wait_send()

    hbm = pl.BlockSpec(memory_space=pl.ANY)
    return pl.pallas_call(
        body, name="exchange_chips",
        out_shape=tuple(jax.ShapeDtypeStruct((3,) + s.shape[1:], F32) for s in sums),
        in_specs=[hbm] * n, out_specs=tuple([hbm] * n),
        scratch_shapes=[pltpu.SemaphoreType.DMA((3 * n,)), pltpu.SemaphoreType.DMA((3 * n,))],
    )(*sums)


def _chip_sums(part, from_sibling, core, tr, name):
    _, rows, cols = part.shape

    def body(core_ref, p_ref, s_ref, o_ref):
        o_ref[...] = p_ref[...] + s_ref[...]

    return pl.pallas_call(
        body, name=name,
        out_shape=jax.ShapeDtypeStruct((4, rows, cols), F32),
        grid_spec=pltpu.PrefetchScalarGridSpec(
            num_scalar_prefetch=1, grid=(4, rows // tr),
            in_specs=[pl.BlockSpec((1, tr, cols), lambda k, r, core_ref: (2 * k + core_ref[0], r, 0)),
                      pl.BlockSpec((1, tr, cols), lambda k, r, core_ref: (k, r, 0))],
            out_specs=pl.BlockSpec((1, tr, cols), lambda k, r, core_ref: (k, r, 0))),
        compiler_params=_cparams("arbitrary", "arbitrary"),
    )(core, part, from_sibling)


def _adamw(w, g, m, v):
    m = ADAM_B1 * m + (1.0 - ADAM_B1) * g
    v = ADAM_B2 * v + (1.0 - ADAM_B2) * (g * g)
    m_hat = m / (1.0 - ADAM_B1 ** ADAM_STEP)
    v_hat = v / (1.0 - ADAM_B2 ** ADAM_STEP)
    delta = -ADAM_LR * (m_hat / (jnp.sqrt(v_hat) + ADAM_EPS) + ADAM_WD * w)
    return delta, m, v


def _final_sum_adamw(sums, from_chips, chip, w, m, v, tr, name):
    rows, cols = w.shape

    def body(chip_ref, s_ref, r_ref, w_ref, m_ref, v_ref, g_out, d_out, m_out, v_out):
        g = ((s_ref[0] + r_ref[0]) + r_ref[1]) + r_ref[2]
        delta, m_new, v_new = _adamw(w_ref[...], g, m_ref[...], v_ref[...])
        g_out[...] = g
        d_out[...] = delta
        m_out[...] = m_new
        v_out[...] = v_new

    tile = pl.BlockSpec((tr, cols), lambda r, chip_ref: (r, 0))
    shp = jax.ShapeDtypeStruct((rows, cols), F32)
    return pl.pallas_call(
        body, name=name,
        out_shape=(shp, shp, shp, shp),
        grid_spec=pltpu.PrefetchScalarGridSpec(
            num_scalar_prefetch=1, grid=(rows // tr,),
            in_specs=[pl.BlockSpec((1, tr, cols), lambda r, chip_ref: (chip_ref[0], r, 0)),
                      pl.BlockSpec((3, tr, cols), lambda r, chip_ref: (0, r, 0)),
                      tile, tile, tile],
            out_specs=(tile, tile, tile, tile)),
        compiler_params=_cparams("arbitrary"),
    )(chip, sums, from_chips, w, m, v)


def _adamw_small(g, w, m, v):
    def body(g_ref, w_ref, m_ref, v_ref, d_out, m_out, v_out):
        delta, m_new, v_new = _adamw(w_ref[...], g_ref[...], m_ref[...], v_ref[...])
        d_out[...] = delta
        m_out[...] = m_new
        v_out[...] = v_new

    vmem = pl.BlockSpec(memory_space=pltpu.VMEM)
    shp = jax.ShapeDtypeStruct(g.shape, F32)
    return pl.pallas_call(body, name="adamw_small", out_shape=(shp, shp, shp),
                          in_specs=[vmem] * 4, out_specs=(vmem, vmem, vmem))(g, w, m, v)


def _inproj(x2d, norm_g, w_main, w_lr, tm, tn):
    seq = x2d.shape[0]

    def body(x_ref, g_ref, w_ref, wlr_ref, proj_ref, lr_ref, ht_ref, hb_scr):
        @pl.when(pl.program_id(1) == 0)
        def _():
            xv = x_ref[...]
            r = lax.rsqrt(jnp.mean(xv * xv, axis=-1, keepdims=True) + EPS)
            h = (xv * r) * g_ref[...]
            hb = h.astype(BF16)
            hb_scr[...] = hb
            ht_ref[...] = h.T.astype(BF16)
            lr_ref[...] = _dot(hb, wlr_ref[...])

        proj_ref[...] = _dot(hb_scr[...], w_ref[...])

    return pl.pallas_call(
        body, name="inproj",
        out_shape=(jax.ShapeDtypeStruct((seq, MAIN_W), F32), jax.ShapeDtypeStruct((seq, LR_W), F32),
                   jax.ShapeDtypeStruct((D_MODEL, seq), BF16)),
        grid=(seq // tm, MAIN_W // tn),
        in_specs=[pl.BlockSpec((tm, D_MODEL), lambda i, j: (i, 0)),
                  pl.BlockSpec((1, D_MODEL), lambda i, j: (0, 0)),
                  pl.BlockSpec((D_MODEL, tn), lambda i, j: (0, j)),
                  pl.BlockSpec((D_MODEL, LR_W), lambda i, j: (0, 0))],
        out_specs=(pl.BlockSpec((tm, tn), lambda i, j: (i, j)),
                   pl.BlockSpec((tm, LR_W), lambda i, j: (i, 0)),
                   pl.BlockSpec((D_MODEL, tm), lambda i, j: (0, i))),
        scratch_shapes=[pltpu.VMEM((tm, D_MODEL), BF16)],
        compiler_params=_cparams("arbitrary", "arbitrary"),
    )(x2d, norm_g, w_main, w_lr)


def _chunk_consts():
    row = lax.broadcasted_iota(jnp.int32, (CHUNK, CHUNK), 0)
    col = lax.broadcasted_iota(jnp.int32, (CHUNK, CHUNK), 1)
    return col <= row, col >= row, col > row, col < row


def _decay_cols(dec_row):
    sq = jnp.broadcast_to(dec_row, (DK, DK)).T
    return jnp.concatenate([sq, sq], axis=1)


def _gate_terms(logit, cum, ref, last):
    g = (jnp.minimum(logit, 0.0) - jnp.log1p(jnp.exp(-jnp.abs(logit)))) * GATE_SCALE
    b = _dot_exact(cum, g)
    b_ref = b[ref:ref + 1, :]
    b_last = b[last:last + 1, :]
    return (jnp.exp(b - b_ref), jnp.exp(b_ref - b), jnp.exp(b), jnp.exp(b_last - b), jnp.exp(b_last))


def _gla_fwd(proj, lr, wgk_f, wgk_b, bgk_f, bgk_b, tt):
    seq = proj.shape[0]
    nb, nc, nch = seq // tt, tt // CHUNK, seq // CHUNK

    def body(qf, kf, vf, lrf, qb, kb, vb, lrb, wf, wb, bf, bb, of, ob, stf, stb, s_scr):
        @pl.when(pl.program_id(0) == 0)
        def _():
            s_scr[...] = jnp.zeros(s_scr.shape, F32)

        tril, triu, sup, _ = _chunk_consts()
        dirs = ((qf, kf, vf, lrf, wf, bf, of, stf, tril.astype(F32), tril, REF_F, LAST_F),
                (qb, kb, vb, lrb, wb, bb, ob, stb, triu.astype(F32), sup, REF_B, LAST_B))

        def chunk(cc, carry):
            for d, (q_r, k_r, v_r, lr_r, w_r, b_r, o_r, st_r, cum, mask, ref, last) in enumerate(dirs):
                ci = cc if d == 0 else nc - 1 - cc
                rows = pl.ds(pl.multiple_of(ci * CHUNK, CHUNK), CHUNK)
                logits = _dot(lr_r[rows, :].astype(BF16), w_r[...]) + b_r[...]
                for h in range(HEADS):
                    ksl = slice(h * DK, (h + 1) * DK)
                    vsl = slice(h * DV, (h + 1) * DV)
                    e_q, e_k, e_in, e_out, dec = _gate_terms(logits[:, ksl], cum, ref, last)
                    q = q_r[rows, ksl] * QSCALE
                    k = k_r[rows, ksl]
                    v = v_r[rows, vsl].astype(BF16)
                    state = s_scr[d * HEADS + h]
                    st_r[ci, h] = state
                    att = _dot_nt((q * e_q).astype(BF16), (k * e_k).astype(BF16))
                    att = jnp.where(mask, att, 0.0)
                    o = _dot(att.astype(BF16), v) + _dot((q * e_in).astype(BF16), state.astype(BF16))
                    o_r[rows, vsl] = o
                    s_scr[d * HEADS + h] = _decay_cols(dec) * state + _dot_tn((k * e_out).astype(BF16), v)
            return carry

        lax.fori_loop(0, nc, chunk, 0)

    fw = lambda i: (i, 0)
    bw = lambda i: (nb - 1 - i, 0)
    const = lambda i: (0, 0)

    def tok_specs(m):
        return [pl.BlockSpec((tt, QK_W), lambda i: (m(i)[0], OFF_Q // QK_W)),
                pl.BlockSpec((tt, QK_W), lambda i: (m(i)[0], OFF_K // QK_W)),
                pl.BlockSpec((tt, V_W), lambda i: (m(i)[0], OFF_V // V_W)),
                pl.BlockSpec((tt, LR_W), m)]

    st_shape = jax.ShapeDtypeStruct((nch, HEADS, DK, DV), F32)
    o_shape = jax.ShapeDtypeStruct((seq, V_W), F32)
    return pl.pallas_call(
        body, name="gla_fwd",
        out_shape=(o_shape, o_shape, st_shape, st_shape),
        grid=(nb,),
        in_specs=tok_specs(fw) + tok_specs(bw) + [
            pl.BlockSpec((LR_W, QK_W), const), pl.BlockSpec((LR_W, QK_W), const),
            pl.BlockSpec((1, QK_W), const), pl.BlockSpec((1, QK_W), const)],
        out_specs=(pl.BlockSpec((tt, V_W), fw), pl.BlockSpec((tt, V_W), bw),
                   pl.BlockSpec((nc, HEADS, DK, DV), lambda i: (i, 0, 0, 0)),
                   pl.BlockSpec((nc, HEADS, DK, DV), lambda i: (nb - 1 - i, 0, 0, 0))),
        scratch_shapes=[pltpu.VMEM((2 * HEADS, DK, DV), F32)],
        compiler_params=_cparams("arbitrary"),
    )(proj, proj, proj, lr, proj, proj, proj, lr, wgk_f, wgk_b, bgk_f, bgk_b)


def _head_norm(o, gain):
    outs, rinv = [], []
    for h in range(HEADS):
        oh = o[:, h * DV:(h + 1) * DV]
        r = lax.rsqrt(jnp.mean(oh * oh, axis=-1, keepdims=True) + EPS)
        outs.append((oh * r) * gain)
        rinv.append(r)
    return jnp.concatenate(outs, axis=1), rinv


def _shift_rows(u, prev_row, next_row):
    n = u.shape[0]
    row = lax.broadcasted_iota(jnp.int32, (n, 1), 0)
    up = jnp.where(row == 0, prev_row, pltpu.roll(u, 1, 0))
    un = jnp.where(row == n - 1, next_row, pltpu.roll(u, n - 1, 0))
    return up, un


def _halo_specs(tm, seq, col_block):
    per = tm // 8
    last = seq // 8 - 1
    return [pl.BlockSpec((8, CONV_W), lambda i: (jnp.maximum(i * per - 1, 0), col_block)),
            pl.BlockSpec((8, CONV_W), lambda i: (jnp.minimum((i + 1) * per, last), col_block))]


def _mix_out_loss(o_f, o_b, proj, x2d, tgt, gla_g, conv_w, conv_b, w_out, final_g, tm):
    seq = x2d.shape[0]
    nt = seq // tm

    def body(of, ob, za, bg, cg, hc, zc, cprev, cnext, hprev, hnext, x_ref, t_ref, gg, cw, cb, wo, fg,
             yt_ref, conv_ref, dx2_ref, dx2b_ref, loss_ref, dfg_ref):
        i = pl.program_id(0)

        @pl.when(i == 0)
        def _():
            loss_ref[...] = jnp.zeros(loss_ref.shape, F32)
            dfg_ref[...] = jnp.zeros(dfg_ref.shape, F32)

        on, _ = _head_norm(of[...] + ob[...], gg[...])
        zav = za[...]
        y_a = on * (zav * _sigmoid(zav))
        u = cg[...] * hc[...]
        prev_row = jnp.where(i > 0, cprev[7:8, :] * hprev[7:8, :], 0.0)
        next_row = jnp.where(i < nt - 1, cnext[0:1, :] * hnext[0:1, :], 0.0)
        up, un = _shift_rows(u, prev_row, next_row)
        conv = (cw[0:1, :] * up + cw[1:2, :] * u + cw[2:3, :] * un) + cb[...]
        conv_ref[...] = conv
        zcv = zc[...]
        y_c = bg[...] * conv * (zcv * _sigmoid(zcv))
        y = jnp.concatenate([y_a, y_c], axis=1)
        yt_ref[...] = y.T.astype(BF16)
        x2 = x_ref[...] + _dot(y.astype(BF16), wo[...])
        r = lax.rsqrt(jnp.mean(x2 * x2, axis=-1, keepdims=True) + EPS)
        xn = x2 * r
        err = xn * fg[...] - t_ref[...]
        loss_ref[...] += 0.5 * jnp.sum(jnp.mean(err * err, axis=-1, keepdims=True))
        dyf = err * (1.0 / D_MODEL)
        dfg_ref[...] += jnp.sum(dyf * xn, axis=0, keepdims=True)
        dxn = dyf * fg[...]
        dx2 = r * dxn - xn * (r * jnp.mean(dxn * xn, axis=-1, keepdims=True))
        dx2_ref[...] = dx2
        dx2b_ref[...] = dx2.astype(BF16)

    def col(off):
        return pl.BlockSpec((tm, CONV_W), lambda i: (i, off // CONV_W))

    rowt = pl.BlockSpec((tm, D_MODEL), lambda i: (i, 0))
    const = lambda shape: pl.BlockSpec(shape, lambda i: (0, 0))
    return pl.pallas_call(
        body, name="mix_out_loss",
        out_shape=(jax.ShapeDtypeStruct((MIX_W, seq), BF16), jax.ShapeDtypeStruct((seq, CONV_W), F32),
                   jax.ShapeDtypeStruct((seq, D_MODEL), F32), jax.ShapeDtypeStruct((seq, D_MODEL), BF16),
                   jax.ShapeDtypeStruct((8, 128), F32), jax.ShapeDtypeStruct((1, D_MODEL), F32)),
        grid=(nt,),
        in_specs=[rowt, rowt, col(OFF_ZA), col(OFF_B), col(OFF_C), col(OFF_H), col(OFF_ZC)]
        + _halo_specs(tm, seq, OFF_C // CONV_W) + _halo_specs(tm, seq, OFF_H // CONV_W)
        + [rowt, rowt, const((1, DV)), const((8, CONV_W)), const((1, CONV_W)), const((MIX_W, D_MODEL)),
           const((1, D_MODEL))],
        out_specs=(pl.BlockSpec((MIX_W, tm), lambda i: (0, i)), rowt, rowt, rowt, const((8, 128)),
                   const((1, D_MODEL))),
        compiler_params=_cparams("arbitrary"),
    )(o_f, o_b, proj, proj, proj, proj, proj, proj, proj, proj, proj, x2d, tgt, gla_g, conv_w, conv_b, w_out, final_g)


def _dsilu(z, s):
    return s * (1.0 + z * (1.0 - s))


def _mix_bwd(dx2b, o_f, o_b, proj, conv, gla_g, w_out, tm):
    seq = dx2b.shape[0]

    def body(dx, of, ob, za, bg, zc, cv, gg, wo, dg_ref, do_ref, dconv_ref, dgg_ref, dcb_ref):
        @pl.when(pl.program_id(0) == 0)
        def _():
            dgg_ref[...] = jnp.zeros(dgg_ref.shape, F32)
            dcb_ref[...] = jnp.zeros(dcb_ref.shape, F32)

        dy = _dot_nt(dx[...], wo[...])
        dy_a, dy_c = dy[:, :V_W], dy[:, V_W:]
        zcv, bgv, convv = zc[...], bg[...], cv[...]
        sc = _sigmoid(zcv)
        szc = zcv * sc
        dg_ref[:, CONV_W:2 * CONV_W] = (dy_c * convv * szc).astype(BF16)
        dconv = dy_c * bgv * szc
        dconv_ref[...] = dconv
        dcb_ref[...] += jnp.sum(dconv, axis=0, keepdims=True)
        dg_ref[:, 2 * CONV_W:] = (dy_c * bgv * convv * _dsilu(zcv, sc)).astype(BF16)

        o = of[...] + ob[...]
        gain = gg[...]
        on, rinv = _head_norm(o, gain)
        zav = za[...]
        sa = _sigmoid(zav)
        dg_ref[:, :CONV_W] = (dy_a * on * _dsilu(zav, sa)).astype(BF16)
        don = dy_a * (zav * sa)
        dgg = jnp.zeros((1, DV), F32)
        dos = []
        for h in range(HEADS):
            sl = slice(h * DV, (h + 1) * DV)
            oh, r, dh = o[:, sl], rinv[h], don[:, sl]
            ohn = oh * r
            dgg = dgg + jnp.sum(dh * ohn, axis=0, keepdims=True)
            dn = dh * gain
            dos.append(r * dn - ohn * (r * jnp.mean(dn * ohn, axis=-1, keepdims=True)))
        dgg_ref[...] += dgg
        do_ref[...] = jnp.concatenate(dos, axis=1)

    def col(off):
        return pl.BlockSpec((tm, CONV_W), lambda i: (i, off // CONV_W))

    rowt = pl.BlockSpec((tm, D_MODEL), lambda i: (i, 0))
    const = lambda shape: pl.BlockSpec(shape, lambda i: (0, 0))
    return pl.pallas_call(
        body, name="mix_bwd",
        out_shape=(jax.ShapeDtypeStruct((seq, GATES_W), BF16), jax.ShapeDtypeStruct((seq, V_W), F32),
                   jax.ShapeDtypeStruct((seq, CONV_W), F32),
                   jax.ShapeDtypeStruct((1, DV), F32), jax.ShapeDtypeStruct((1, CONV_W), F32)),
        grid=(seq // tm,),
        in_specs=[rowt, rowt, rowt, col(OFF_ZA), col(OFF_B), col(OFF_ZC), rowt, const((1, DV)),
                  const((MIX_W, D_MODEL))],
        out_specs=(pl.BlockSpec((tm, GATES_W), lambda i: (i, 0)), rowt, rowt, const((1, DV)), const((1, CONV_W))),
        compiler_params=_cparams("arbitrary"),
    )(dx2b, o_f, o_b, proj, proj, proj, conv, gla_g, w_out)


def _conv_bwd(dconv, proj, conv_w, tm):
    seq = dconv.shape[0]
    nt = seq // tm

    def body(dc_in, dprev, dnext, cg, hc, cprev, cnext, hprev, hnext, cw, dch_ref, dcw_ref):
        i = pl.program_id(0)

        @pl.when(i == 0)
        def _():
            dcw_ref[...] = jnp.zeros(dcw_ref.shape, F32)

        first, lastt = i > 0, i < nt - 1
        dcv = dc_in[...]
        d_up, d_un = _shift_rows(dcv, jnp.where(first, dprev[7:8, :], 0.0), jnp.where(lastt, dnext[0:1, :], 0.0))
        cgv, hcv = cg[...], hc[...]
        u = cgv * hcv
        u_up, u_un = _shift_rows(u, jnp.where(first, cprev[7:8, :] * hprev[7:8, :], 0.0),
                                 jnp.where(lastt, cnext[0:1, :] * hnext[0:1, :], 0.0))
        du = cw[0:1, :] * d_un + cw[1:2, :] * dcv + cw[2:3, :] * d_up
        dch_ref[:, :CONV_W] = (du * hcv).astype(BF16)
        dch_ref[:, CONV_W:] = (du * cgv).astype(BF16)
        dcw_ref[0:1, :] += jnp.sum(dcv * u_up, axis=0, keepdims=True)
        dcw_ref[1:2, :] += jnp.sum(dcv * u, axis=0, keepdims=True)
        dcw_ref[2:3, :] += jnp.sum(dcv * u_un, axis=0, keepdims=True)

    def col(off):
        return pl.BlockSpec((tm, CONV_W), lambda i: (i, off // CONV_W))

    rowt = pl.BlockSpec((tm, CONV_W), lambda i: (i, 0))
    const = lambda shape: pl.BlockSpec(shape, lambda i: (0, 0))
    return pl.pallas_call(
        body, name="conv_bwd",
        out_shape=(jax.ShapeDtypeStruct((seq, CH_W), BF16), jax.ShapeDtypeStruct((8, CONV_W), F32)),
        grid=(nt,),
        in_specs=[rowt] + _halo_specs(tm, seq, 0) + [col(OFF_C), col(OFF_H)]
        + _halo_specs(tm, seq, OFF_C // CONV_W) + _halo_specs(tm, seq, OFF_H // CONV_W) + [const((8, CONV_W))],
        out_specs=(pl.BlockSpec((tm, CH_W), lambda i: (i, 0)), const((8, CONV_W))),
        compiler_params=_cparams("arbitrary"),
    )(dconv, dconv, dconv, proj, proj, proj, proj, proj, proj, conv_w)


def _gla_bwd(proj, lr, do, st_f, st_b, wgk_f, wgk_b, bgk_f, bgk_b, tt):
    seq = proj.shape[0]
    nb, nc = seq // tt, tt // CHUNK

    def body(qf, kf, vf, lrf, dof, stf, qb, kb, vb, lrb, dob, stb, wf, wb, bf, bb,
             dqkv_f, dlr_f, dqkv_b, dlr_b, dwf, dwb, dbf, dbb, ds_scr):
        @pl.when(pl.program_id(0) == 0)
        def _():
            ds_scr[...] = jnp.zeros(ds_scr.shape, F32)
            for r in (dwf, dwb, dbf, dbb):
                r[...] = jnp.zeros(r.shape, F32)

        tril, triu, sup, _ = _chunk_consts()
        row = lax.broadcasted_iota(jnp.int32, (CHUNK, 1), 0)
        ones = jnp.ones((8, DV), F32)
        dirs = ((qf, kf, vf, lrf, dof, stf, wf, bf, dqkv_f, dlr_f, dwf, dbf,
                 tril.astype(F32), triu.astype(F32), tril, REF_F, LAST_F),
                (qb, kb, vb, lrb, dob, stb, wb, bb, dqkv_b, dlr_b, dwb, dbb,
                 triu.astype(F32), tril.astype(F32), sup, REF_B, LAST_B))

        def chunk(cc, carry):
            for d, (q_r, k_r, v_r, lr_r, do_r, st_r, w_r, b_r, dqkv_r, dlr_r, dw_r, db_r,
                    cum, cum_t, mask, ref, last) in enumerate(dirs):
                ci = nc - 1 - cc if d == 0 else cc
                rows = pl.ds(pl.multiple_of(ci * CHUNK, CHUNK), CHUNK)
                lrv = lr_r[rows, :].astype(BF16)
                wv = w_r[...]
                logits = _dot(lrv, wv) + b_r[...]
                dlogits = []
                for h in range(HEADS):
                    ksl = slice(h * DK, (h + 1) * DK)
                    vsl = slice(h * DV, (h + 1) * DV)
                    logit = logits[:, ksl]
                    e_q, e_k, e_in, e_out, dec = _gate_terms(logit, cum, ref, last)
                    q = q_r[rows, ksl] * QSCALE
                    k = k_r[rows, ksl]
                    v = v_r[rows, vsl].astype(BF16)
                    dov = do_r[rows, vsl].astype(BF16)
                    s_prev = st_r[ci, h]
                    ds = ds_scr[d * HEADS + h]
                    dsb = ds.astype(BF16)
                    qs, ks, q_in, k_out = q * e_q, k * e_k, q * e_in, k * e_out
                    qsb, ksb = qs.astype(BF16), ks.astype(BF16)
                    att = jnp.where(mask, _dot_nt(qsb, ksb), 0.0).astype(BF16)
                    datt = jnp.where(mask, _dot_nt(dov, v), 0.0).astype(BF16)
                    dqs = _dot(datt, ksb)
                    dks = _dot_tn(datt, qsb)
                    dv = _dot_tn(att, dov) + _dot(k_out.astype(BF16), dsb)
                    dq_in = _dot_nt(dov, s_prev.astype(BF16))
                    dk_out = _dot_nt(v, dsb)
                    dqkv_r[rows, OFF_Q + h * DK:OFF_Q + (h + 1) * DK] = (dqs * e_q + dq_in * e_in) * QSCALE
                    dqkv_r[rows, OFF_K + h * DK:OFF_K + (h + 1) * DK] = dks * e_k + dk_out * e_out
                    dqkv_r[rows, OFF_V + h * DV:OFF_V + (h + 1) * DV] = dv
                    kk = dk_out * k_out
                    db = dqs * qs - dks * ks + dq_in * q_in - kk
                    ddec = _dot_nt_exact(ones, ds * s_prev)[0:1, :]
                    tail = jnp.sum(kk, axis=0, keepdims=True) + ddec * dec
                    db = db + jnp.where(row == last, tail, 0.0)
                    dg = _dot_exact(cum_t, db)
                    dlogits.append((dg * GATE_SCALE) * _sigmoid(-logit))
                    ds_scr[d * HEADS + h] = _decay_cols(dec) * ds + _dot_tn(q_in.astype(BF16), dov)
                dlogit = jnp.concatenate(dlogits, axis=1)
                dlb = dlogit.astype(BF16)
                dlr_r[rows, :] = _dot_nt(dlb, wv)
                dw_r[...] += _dot_tn(lrv, dlb)
                db_r[...] += jnp.sum(dlogit, axis=0, keepdims=True)
            return carry

        lax.fori_loop(0, nc, chunk, 0)

    fw = lambda i: (nb - 1 - i, 0)
    bw = lambda i: (i, 0)
    const = lambda i: (0, 0)

    def tok_specs(m):
        return [pl.BlockSpec((tt, QK_W), lambda i: (m(i)[0], OFF_Q // QK_W)),
                pl.BlockSpec((tt, QK_W), lambda i: (m(i)[0], OFF_K // QK_W)),
                pl.BlockSpec((tt, V_W), lambda i: (m(i)[0], OFF_V // V_W)),
                pl.BlockSpec((tt, LR_W), m),
                pl.BlockSpec((tt, V_W), m),
                pl.BlockSpec((nc, HEADS, DK, DV), lambda i: (m(i)[0], 0, 0, 0))]

    dqkv = jax.ShapeDtypeStruct((seq, QK_W + QK_W + V_W), F32)
    dlr = jax.ShapeDtypeStruct((seq, LR_W), F32)
    dw = jax.ShapeDtypeStruct((LR_W, QK_W), F32)
    dbias = jax.ShapeDtypeStruct((1, QK_W), F32)
    return pl.pallas_call(
        body, name="gla_bwd",
        out_shape=(dqkv, dlr, dqkv, dlr, dw, dw, dbias, dbias),
        grid=(nb,),
        in_specs=tok_specs(fw) + tok_specs(bw) + [
            pl.BlockSpec((LR_W, QK_W), const), pl.BlockSpec((LR_W, QK_W), const),
            pl.BlockSpec((1, QK_W), const), pl.BlockSpec((1, QK_W), const)],
        out_specs=(pl.BlockSpec((tt, QK_W + QK_W + V_W), fw), pl.BlockSpec((tt, LR_W), fw),
                   pl.BlockSpec((tt, QK_W + QK_W + V_W), bw), pl.BlockSpec((tt, LR_W), bw),
                   pl.BlockSpec((LR_W, QK_W), const), pl.BlockSpec((LR_W, QK_W), const),
                   pl.BlockSpec((1, QK_W), const), pl.BlockSpec((1, QK_W), const)),
        scratch_shapes=[pltpu.VMEM((2 * HEADS, DK, DV), F32)],
        compiler_params=_cparams("arbitrary"),
    )(proj, proj, proj, lr, do, st_f, proj, proj, proj, lr, do, st_b, wgk_f, wgk_b, bgk_f, bgk_b)


def _dot_nt_exact(a, b):
    return lax.dot_general(a, b, (((1,), (1,)), ((), ())), preferred_element_type=F32,
                           precision=lax.Precision.HIGHEST)


def _sum_directions(dqkv_f, dqkv_b, dlr_f, dlr_b, tm):
    seq = dqkv_f.shape[0]

    def body(a, b, la, lb, dp_out, dlr_out):
        dp_out[...] = (a[...] + b[...]).astype(BF16)
        dlr_out[...] = (la[...] + lb[...]).astype(BF16)

    rowt = pl.BlockSpec((tm, QKV_W), lambda i: (i, 0))
    lrt = pl.BlockSpec((tm, LR_W), lambda i: (i, 0))
    return pl.pallas_call(
        body, name="sum_directions",
        out_shape=(jax.ShapeDtypeStruct((seq, QKV_W), BF16), jax.ShapeDtypeStruct((seq, LR_W), BF16)),
        grid=(seq // tm,),
        in_specs=[rowt, rowt, lrt, lrt],
        out_specs=(rowt, lrt),
        compiler_params=_cparams("arbitrary"),
    )(dqkv_f, dqkv_b, dlr_f, dlr_b)


def _input_grad(dp_qkv, dp_gates, dp_ch, dlr, w_main, w_lr, x2d, norm_g, dx2, tm):
    seq = x2d.shape[0]
    tk = 1024
    k_gates, k_ch, nk = QKV_W // tk, (QKV_W + GATES_W) // tk, MAIN_W // tk

    def body(dq, dg, dc, dl, w, wl, x_ref, g_ref, dx2_ref, gx_ref, dng_ref, acc):
        i, k = pl.program_id(0), pl.program_id(1)

        @pl.when(jnp.logical_and(i == 0, k == 0))
        def _():
            dng_ref[...] = jnp.zeros(dng_ref.shape, F32)

        @pl.when(k == 0)
        def _():
            acc[...] = _dot_nt(dl[...], wl[...])

        @pl.when(k < k_gates)
        def _():
            acc[...] += _dot_nt(dq[...], w[...])

        @pl.when(jnp.logical_and(k >= k_gates, k < k_ch))
        def _():
            acc[...] += _dot_nt(dg[...], w[...])

        @pl.when(k >= k_ch)
        def _():
            acc[...] += _dot_nt(dc[...], w[...])

        @pl.when(k == nk - 1)
        def _():
            dh = acc[...]
            xv = x_ref[...]
            r = lax.rsqrt(jnp.mean(xv * xv, axis=-1, keepdims=True) + EPS)
            xn = xv * r
            dng_ref[...] += jnp.sum(dh * xn, axis=0, keepdims=True)
            dn = dh * g_ref[...]
            gx_ref[...] = (r * dn - xn * (r * jnp.mean(dn * xn, axis=-1, keepdims=True))) + dx2_ref[...]

    rowt = pl.BlockSpec((tm, D_MODEL), lambda i, k: (i, 0))
    return pl.pallas_call(
        body, name="input_grad",
        out_shape=(jax.ShapeDtypeStruct((seq, D_MODEL), F32), jax.ShapeDtypeStruct((1, D_MODEL), F32)),
        grid=(seq // tm, nk),
        in_specs=[pl.BlockSpec((tm, tk), lambda i, k: (i, jnp.minimum(k, k_gates - 1))),
                  pl.BlockSpec((tm, tk), lambda i, k: (i, jnp.clip(k - k_gates, 0, k_ch - k_gates - 1))),
                  pl.BlockSpec((tm, tk), lambda i, k: (i, jnp.clip(k - k_ch, 0, nk - k_ch - 1))),
                  pl.BlockSpec((tm, LR_W), lambda i, k: (i, 0)),
                  pl.BlockSpec((D_MODEL, tk), lambda i, k: (0, k)),
                  pl.BlockSpec((D_MODEL, LR_W), lambda i, k: (0, 0)),
                  rowt, pl.BlockSpec((1, D_MODEL), lambda i, k: (0, 0)), rowt],
        out_specs=(rowt, pl.BlockSpec((1, D_MODEL), lambda i, k: (0, 0))),
        scratch_shapes=[pltpu.VMEM((tm, D_MODEL), F32)],
        compiler_params=_cparams("arbitrary", "arbitrary"),
    )(dp_qkv, dp_gates, dp_ch, dlr, w_main, w_lr, x2d, norm_g, dx2)


def _weight_grad(at, b, tn, tk, name):
    m, seq = at.shape
    n = b.shape[1]

    def body(a_ref, b_ref, o_ref):
        @pl.when(pl.program_id(1) == 0)
        def _():
            o_ref[...] = jnp.zeros(o_ref.shape, F32)

        o_ref[...] += _dot(a_ref[...], b_ref[...])

    return pl.pallas_call(
        body, name=name,
        out_shape=jax.ShapeDtypeStruct((m, n), F32),
        grid=(n // tn, seq // tk),
        in_specs=[pl.BlockSpec((m, tk), lambda j, k: (0, k)), pl.BlockSpec((tk, tn), lambda j, k: (k, j))],
        out_specs=pl.BlockSpec((m, tn), lambda j, k: (0, j)),
        compiler_params=_cparams("arbitrary", "arbitrary"),
    )(at, b)


def _rows128(a):
    return a.reshape(-1, 128)


def _pad_rows(a, rows):
    return jnp.concatenate([a, jnp.zeros((rows - a.shape[0], a.shape[1]), a.dtype)], axis=0)


def kernel(x, norm_g, w_in, w_gk_f, b_gk_f, w_gk_b, b_gk_b, gla_norm_g, conv_w, conv_b, w_out, final_g, loss_target, m_norm_g, m_w_in, m_w_gk_f, m_b_gk_f, m_w_gk_b, m_b_gk_b, m_gla_norm_g, m_conv_w, m_conv_b, m_w_out, m_final_g, v_norm_g, v_w_in, v_w_gk_f, v_b_gk_f, v_w_gk_b, v_b_gk_b, v_gla_norm_g, v_conv_w, v_conv_b, v_w_out, v_final_g):
    px, py, pc = _position()
    me = _blk(px, py, pc)
    seq = x.shape[1]
    x2d, tgt = x[0], loss_target[0]
    tm = min(512, seq)
    tt = min(256, seq)

    small_s = _pad_rows(jnp.concatenate([jnp.concatenate([w_gk_f[0], w_gk_b[0]], axis=1), conv_w[0]], axis=0), 24)
    win_all, wout_all, small_all = _allgather_weights(w_in[0], w_out[0], small_s)
    w_nat = win_all.transpose(1, 0, 2).reshape(D_MODEL, IN_W)
    w_main = jnp.concatenate([w_nat[:, :NAT_LR], w_nat[:, NAT_B:NAT_C], w_nat[:, NAT_ZC:], w_nat[:, NAT_C:NAT_ZC]], axis=1)
    w_lr = jnp.concatenate([w_nat[:, NAT_LR:NAT_B], jnp.zeros((D_MODEL, LR_W - 2 * RANK), BF16)], axis=1)
    w_out_full = wout_all.reshape(MIX_W, D_MODEL)
    wgk_cols = 512 // N_DEV
    wgk_f_full = small_all[:, 0:RANK, 0:wgk_cols].transpose(1, 0, 2).reshape(RANK, QK_W)
    wgk_b_full = small_all[:, 0:RANK, wgk_cols:2 * wgk_cols].transpose(1, 0, 2).reshape(RANK, QK_W)
    conv_w_full = _pad_rows(small_all[:, RANK:RANK + 3, :].transpose(1, 0, 2).reshape(3, CONV_W), 8)
    zr = lambda n: jnp.zeros((n, QK_W), F32)
    wgk_f_pad = jnp.concatenate([wgk_f_full, zr(LR_W - RANK)], axis=0).astype(BF16)
    wgk_b_pad = jnp.concatenate([zr(RANK), wgk_b_full, zr(LR_W - 2 * RANK)], axis=0).astype(BF16)

    proj, lr, h_t = _inproj(x2d, norm_g, w_main, w_lr, tm, 1024)
    o_f, o_b, st_f, st_b = _gla_fwd(proj, lr, wgk_f_pad, wgk_b_pad, b_gk_f, b_gk_b, tt)
    tmix = min(256, seq)
    y_t, conv, dx2, dx2b, loss_p, dfg_p = _mix_out_loss(o_f, o_b, proj, x2d, tgt, gla_norm_g, conv_w_full, conv_b,
                                                        w_out_full, final_g.reshape(1, D_MODEL), tmix)

    dp_gates, do, dconv, dgg_p, dcb_p = _mix_bwd(dx2b, o_f, o_b, proj, conv, gla_norm_g, w_out_full, tmix)
    dp_ch, dcw_p = _conv_bwd(dconv, proj, conv_w_full, tmix)
    dqkv_f, dlr_f, dqkv_b, dlr_b, dwf_p, dwb_p, dbf_p, dbb_p = _gla_bwd(
        proj, lr, do, st_f, st_b, wgk_f_pad, wgk_b_pad, b_gk_f, b_gk_b, tt)
    dp_qkv, dlr = _sum_directions(dqkv_f, dqkv_b, dlr_f, dlr_b, tm)
    grad_x2d, dng_p = _input_grad(dp_qkv, dp_gates, dp_ch, dlr, w_main, w_lr, x2d, norm_g, dx2, tm)
    dw_qkv = _weight_grad(h_t, dp_qkv, 1024, tm, "wgrad_qkv")
    dw_gates = _weight_grad(h_t, dp_gates, 1024, tm, "wgrad_gates")
    dw_ch = _weight_grad(h_t, dp_ch, 1024, tm, "wgrad_ch")
    dw_lr = _weight_grad(h_t, dlr, LR_W, tm, "wgrad_lr")
    dw_out = _weight_grad(y_t, dx2b, D_MODEL, tm, "wgrad_out")

    dw_nat = jnp.concatenate([dw_qkv, dw_gates[:, :CONV_W], dw_lr[:, :2 * RANK], dw_gates[:, CONV_W:2 * CONV_W], dw_ch,
                              dw_gates[:, 2 * CONV_W:]], axis=1)
    part_in = dw_nat.reshape(D_MODEL, N_DEV, SHARD_W).transpose(1, 0, 2)
    part_out = dw_out.reshape(N_DEV, MIX_W // N_DEV, D_MODEL)
    sib_in, sib_out = _exchange_sibling([part_in, part_out])
    core = jnp.reshape(pc, (1,)).astype(jnp.int32)
    chip = jnp.reshape(2 * px + py, (1,)).astype(jnp.int32)
    sums_in = _chip_sums(part_in, sib_in, core, 256, "chip_sums_in")
    sums_out = _chip_sums(part_out, sib_out, core, 256, "chip_sums_out")
    far_in, far_out = _exchange_chips([sums_in, sums_out])
    g_w_in, d_w_in, nm_w_in, nv_w_in = _final_sum_adamw(sums_in, far_in, chip, w_in[0], m_w_in[0], v_w_in[0], 256,
                                                        "adamw_in")
    g_w_out, d_w_out, nm_w_out, nv_w_out = _final_sum_adamw(sums_out, far_out, chip, w_out[0], m_w_out[0], v_w_out[0],
                                                            256, "adamw_out")

    pieces = [dng_p, dbf_p, dbb_p, dgg_p, dcb_p, dfg_p, dwf_p[0:RANK], dwb_p[RANK:2 * RANK], dcw_p[0:3], loss_p[0:1]]
    small_p = _pad_rows(jnp.concatenate([_rows128(p) for p in pieces], axis=0), 192)
    tot = _allreduce_small(small_p)
    sizes = [p.size // 128 for p in pieces]
    starts = [sum(sizes[:k]) for k in range(len(sizes))]
    take = lambda k, shape: tot[starts[k]:starts[k] + sizes[k]].reshape(shape)
    g_norm_g, g_b_gk_f, g_b_gk_b = take(0, (1, D_MODEL)), take(1, (1, QK_W)), take(2, (1, QK_W))
    g_gla, g_conv_b, g_final = take(3, (1, DV)), take(4, (1, CONV_W)), take(5, (D_MODEL,))
    g_wgk_f = lax.dynamic_slice_in_dim(take(6, (RANK, QK_W)), me * wgk_cols, wgk_cols, axis=1)[None]
    g_wgk_b = lax.dynamic_slice_in_dim(take(7, (RANK, QK_W)), me * wgk_cols, wgk_cols, axis=1)[None]
    g_conv_w = lax.dynamic_slice_in_dim(take(8, (3, CONV_W)), me * 128, 128, axis=1)[None]
    loss = tot[starts[9], 0]

    small_g = [g_norm_g, g_b_gk_f, g_b_gk_b, g_gla, g_conv_b, g_final, g_wgk_f, g_wgk_b, g_conv_w]
    small_w = [norm_g, b_gk_f, b_gk_b, gla_norm_g, conv_b, final_g, w_gk_f, w_gk_b, conv_w]
    small_m = [m_norm_g, m_b_gk_f, m_b_gk_b, m_gla_norm_g, m_conv_b, m_final_g, m_w_gk_f, m_w_gk_b, m_conv_w]
    small_v = [v_norm_g, v_b_gk_f, v_b_gk_b, v_gla_norm_g, v_conv_b, v_final_g, v_w_gk_f, v_w_gk_b, v_conv_w]
    pack = lambda arrs: _pad_rows(jnp.concatenate([_rows128(a) for a in arrs], axis=0), 56)
    d_s, m_s, v_s = _adamw_small(pack(small_g), pack(small_w), pack(small_m), pack(small_v))
    ssz = [a.size // 128 for a in small_w]
    sst = [sum(ssz[:k]) for k in range(len(ssz))]
    unpack = lambda buf: [buf[sst[k]:sst[k] + ssz[k]].reshape(small_w[k].shape) for k in range(len(small_w))]
    d_l, m_l, v_l = unpack(d_s), unpack(m_s), unpack(v_s)

    def ordered(sm, big_in, big_out):
        return [sm[0], big_in[None], sm[6], sm[1], sm[7], sm[2], sm[3], sm[8], sm[4], big_out[None], sm[5]]

    grads = ordered(small_g, g_w_in, g_w_out)
    deltas = ordered(d_l, d_w_in, d_w_out)
    new_m = ordered(m_l, nm_w_in, nm_w_out)
    new_v = ordered(v_l, nv_w_in, nv_w_out)
    return (loss, grad_x2d[None], *grads, *deltas, *new_m, *new_v)
```

```python
import jax
import jax.numpy as jnp
from jax import lax
from jax.experimental import pallas as pl
from jax.experimental.pallas import tpu as pltpu

F32 = jnp.float32
BF16 = jnp.bfloat16
MESH = pl.DeviceIdType.MESH

N_DEV = 8
D_MODEL = 1024
HEADS = 4
DK = 128
DV = 256
QK_W = HEADS * DK
V_W = HEADS * DV
CONV_W = 1024
MIX_W = V_W + CONV_W
CHUNK = 64
RANK = 16
IN_W = 7200
SHARD_W = IN_W // N_DEV
MAIN_W = 7168
LR_W = 128
OFF_Q, OFF_K, OFF_V, OFF_ZA, OFF_B, OFF_ZC, OFF_C, OFF_H = 0, 512, 1024, 2048, 3072, 4096, 5120, 6144
QKV_W, GATES_W, CH_W = 2048, 3072, 2048
NAT_ZA, NAT_LR, NAT_B, NAT_C, NAT_ZC = 2048, 3072, 3104, 4128, 6176
EPS = 1e-6
GATE_SCALE = 1.0 / 16.0
QSCALE = DK ** -0.5
REF_F, LAST_F = CHUNK // 2, CHUNK - 1
REF_B, LAST_B = CHUNK - 1 - CHUNK // 2, 0

ADAM_LR = 0.001
ADAM_B1 = 0.9
ADAM_B2 = 0.999
ADAM_EPS = 1e-08
ADAM_WD = 0.01
ADAM_STEP = 10

VMEM_LIMIT = 56 * 1024 * 1024


def _cparams(*sem):
    return pltpu.CompilerParams(dimension_semantics=sem, vmem_limit_bytes=VMEM_LIMIT)


def _dot(a, b):
    return jnp.dot(a, b, preferred_element_type=F32)


def _dot_nt(a, b):
    return lax.dot_general(a, b, (((1,), (1,)), ((), ())), preferred_element_type=F32)


def _dot_tn(a, b):
    return lax.dot_general(a, b, (((0,), (0,)), ((), ())), preferred_element_type=F32)


def _sigmoid(z):
    return jax.nn.sigmoid(z)


def _position():
    return lax.axis_index("x"), lax.axis_index("y"), lax.axis_index("c")


def _blk(px, py, pc):
    return 4 * px + 2 * py + pc


def _two_level_gather(outs, send_sems, recv_sems):
    x, y, c = _position()
    me, sibling = (x, y, c), (x, y, 1 - c)
    chips = [(1 - x, y), (x, 1 - y), (1 - x, 1 - y)]
    n = len(outs)

    def copy(a, k, block, to):
        ref = outs[a].at[_blk(*block)]
        return pltpu.make_async_remote_copy(src_ref=ref, dst_ref=ref, send_sem=send_sems.at[a * 7 + k],
                                            recv_sem=recv_sems.at[a * 7 + k], device_id=to, device_id_type=MESH)

    first = []
    for a in range(n):
        first.append(copy(a, 0, me, sibling))
        first += [copy(a, 1 + j, me, (*chip, c)) for j, chip in enumerate(chips)]
    for cp in first:
        cp.start()
    passed = []
    for j, chip in enumerate(chips):
        for a in range(n):
            copy(a, 1 + j, (*chip, c), me).wait_recv()
            fwd = copy(a, 4 + j, (*chip, c), sibling)
            fwd.start()
            passed.append(fwd)
    for a in range(n):
        copy(a, 0, sibling, me).wait_recv()
    for j, chip in enumerate(chips):
        for a in range(n):
            copy(a, 4 + j, (*chip, 1 - c), me).wait_recv()
    for cp in first + passed:
        cp.wait_send()


def _allgather_weights(w_in_s, w_out_s, small_s):
    def body(win_ref, wout_ref, sm_ref, win_all, wout_all, sm_all, send_sems, recv_sems):
        mine = _blk(*_position())
        win_all[mine] = win_ref[...].astype(BF16)
        wout_all[mine] = wout_ref[...].astype(BF16)
        sm_all[mine] = sm_ref[...]
        _two_level_gather((win_all, wout_all, sm_all), send_sems, recv_sems)

    vmem = pl.BlockSpec(memory_space=pltpu.VMEM)
    return pl.pallas_call(
        body, name="allgather_weights",
        out_shape=(jax.ShapeDtypeStruct((N_DEV,) + w_in_s.shape, BF16),
                   jax.ShapeDtypeStruct((N_DEV,) + w_out_s.shape, BF16),
                   jax.ShapeDtypeStruct((N_DEV,) + small_s.shape, F32)),
        in_specs=[vmem, vmem, vmem], out_specs=(vmem, vmem, vmem),
        scratch_shapes=[pltpu.SemaphoreType.DMA((21,)), pltpu.SemaphoreType.DMA((21,))],
        compiler_params=pltpu.CompilerParams(vmem_limit_bytes=VMEM_LIMIT),
    )(w_in_s, w_out_s, small_s)


def _allreduce_small(part):
    def body(p_ref, tot_ref, all_ref, send_sems, recv_sems):
        mine = _blk(*_position())
        all_ref[mine] = p_ref[...]
        _two_level_gather((all_ref,), send_sems, recv_sems)
        acc = all_ref[0]
        for d in range(1, N_DEV):
            acc = acc + all_ref[d]
        tot_ref[...] = acc

    vmem = pl.BlockSpec(memory_space=pltpu.VMEM)
    return pl.pallas_call(
        body, name="allreduce_small",
        out_shape=jax.ShapeDtypeStruct(part.shape, F32),
        in_specs=[vmem], out_specs=vmem,
        scratch_shapes=[pltpu.VMEM((N_DEV,) + part.shape, F32),
                        pltpu.SemaphoreType.DMA((7,)), pltpu.SemaphoreType.DMA((7,))],
        compiler_params=pltpu.CompilerParams(vmem_limit_bytes=VMEM_LIMIT),
    )(part)


def _exchange_sibling(parts):
    n = len(parts)

    def body(*refs):
        ins, outs = refs[:n], refs[n:2 * n]
        send_sems, recv_sems = refs[2 * n], refs[2 * n + 1]
        x, y, c = _position()
        sibling = (x, y, 1 - c)
        copies = []
        for a in range(n):
            for k in range(4):
                copies.append(pltpu.make_async_remote_copy(
                    src_ref=ins[a].at[2 * k + (1 - c)], dst_ref=outs[a].at[k],
                    send_sem=send_sems.at[a * 4 + k], recv_sem=recv_sems.at[a * 4 + k],
                    device_id=sibling, device_id_type=MESH))
        for cp in copies:
            cp.start()
        for cp in copies:
            cp.wait_recv()
        for cp in copies:
            cp.wait_send()

    hbm = pl.BlockSpec(memory_space=pl.ANY)
    return pl.pallas_call(
        body, name="exchange_sibling",
        out_shape=tuple(jax.ShapeDtypeStruct((4,) + p.shape[1:], F32) for p in parts),
        in_specs=[hbm] * n, out_specs=tuple([hbm] * n),
        scratch_shapes=[pltpu.SemaphoreType.DMA((4 * n,)), pltpu.SemaphoreType.DMA((4 * n,))],
    )(*parts)


def _exchange_chips(sums):
    n = len(sums)

    def body(*refs):
        ins, outs = refs[:n], refs[n:2 * n]
        send_sems, recv_sems = refs[2 * n], refs[2 * n + 1]
        x, y, c = _position()
        chips = [(1 - x, y), (x, 1 - y), (1 - x, 1 - y)]
        copies = []
        for a in range(n):
            for j, (px, py) in enumerate(chips):
                copies.append(pltpu.make_async_remote_copy(
                    src_ref=ins[a].at[2 * px + py], dst_ref=outs[a].at[j],
                    send_sem=send_sems.at[a * 3 + j], recv_sem=recv_sems.at[a * 3 + j],
                    device_id=(px, py, c), device_id_type=MESH))
        for cp in copies:
            cp.start()
        for cp in copies:
            cp.wait_recv()
        for cp in copies:
            cp.wait_send()

    hbm = pl.BlockSpec(memory_space=pl.ANY)
    return pl.pallas_call(
        body, name="exchange_chips",
        out_shape=tuple(jax.ShapeDtypeStruct((3,) + s.shape[1:], F32) for s in sums),
        in_specs=[hbm] * n, out_specs=tuple([hbm] * n),
        scratch_shapes=[pltpu.SemaphoreType.DMA((3 * n,)), pltpu.SemaphoreType.DMA((3 * n,))],
    )(*sums)


def _chip_sums(part, from_sibling, core, tr, name):
    _, rows, cols = part.shape

    def body(core_ref, p_ref, s_ref, o_ref):
        o_ref[...] = p_ref[...] + s_ref[...]

    return pl.pallas_call(
        body, name=name,
        out_shape=jax.ShapeDtypeStruct((4, rows, cols), F32),
        grid_spec=pltpu.PrefetchScalarGridSpec(
            num_scalar_prefetch=1, grid=(4, rows // tr),
            in_specs=[pl.BlockSpec((1, tr, cols), lambda k, r, core_ref: (2 * k + core_ref[0], r, 0)),
                      pl.BlockSpec((1, tr, cols), lambda k, r, core_ref: (k, r, 0))],
            out_specs=pl.BlockSpec((1, tr, cols), lambda k, r, core_ref: (k, r, 0))),
        compiler_params=_cparams("arbitrary", "arbitrary"),
    )(core, part, from_sibling)


def _adamw(w, g, m, v):
    m = ADAM_B1 * m + (1.0 - ADAM_B1) * g
    v = ADAM_B2 * v + (1.0 - ADAM_B2) * (g * g)
    m_hat = m / (1.0 - ADAM_B1 ** ADAM_STEP)
    v_hat = v / (1.0 - ADAM_B2 ** ADAM_STEP)
    delta = -ADAM_LR * (m_hat / (jnp.sqrt(v_hat) + ADAM_EPS) + ADAM_WD * w)
    return delta, m, v


def _final_sum_adamw(sums, from_chips, chip, w, m, v, tr, name):
    rows, cols = w.shape

    def body(chip_ref, s_ref, r_ref, w_ref, m_ref, v_ref, g_out, d_out, m_out, v_out):
        g = ((s_ref[0] + r_ref[0]) + r_ref[1]) + r_ref[2]
        delta, m_new, v_new = _adamw(w_ref[...], g, m_ref[...], v_ref[...])
        g_out[...] = g
        d_out[...] = delta
        m_out[...] = m_new
        v_out[...] = v_new

    tile = pl.BlockSpec((tr, cols), lambda r, chip_ref: (r, 0))
    shp = jax.ShapeDtypeStruct((rows, cols), F32)
    return pl.pallas_call(
        body, name=name,
        out_shape=(shp, shp, shp, shp),
        grid_spec=pltpu.PrefetchScalarGridSpec(
            num_scalar_prefetch=1, grid=(rows // tr,),
            in_specs=[pl.BlockSpec((1, tr, cols), lambda r, chip_ref: (chip_ref[0], r, 0)),
                      pl.BlockSpec((3, tr, cols), lambda r, chip_ref: (0, r, 0)),
                      tile, tile, tile],
            out_specs=(tile, tile, tile, tile)),
        compiler_params=_cparams("arbitrary"),
    )(chip, sums, from_chips, w, m, v)


def _adamw_small(g, w, m, v):
    def body(g_ref, w_ref, m_ref, v_ref, d_out, m_out, v_out):
        delta, m_new, v_new = _adamw(w_ref[...], g_ref[...], m_ref[...], v_ref[...])
        d_out[...] = delta
        m_out[...] = m_new
        v_out[...] = v_new

    vmem = pl.BlockSpec(memory_space=pltpu.VMEM)
    shp = jax.ShapeDtypeStruct(g.shape, F32)
    return pl.pallas_call(body, name="adamw_small", out_shape=(shp, shp, shp),
                          in_specs=[vmem] * 4, out_specs=(vmem, vmem, vmem))(g, w, m, v)


def _inproj(x2d, norm_g, w_main, w_lr, tm, tn):
    seq = x2d.shape[0]

    def body(x_ref, g_ref, w_ref, wlr_ref, proj_ref, lr_ref, ht_ref, hb_scr):
        @pl.when(pl.program_id(1) == 0)
        def _():
            xv = x_ref[...]
            r = lax.rsqrt(jnp.mean(xv * xv, axis=-1, keepdims=True) + EPS)
            h = (xv * r) * g_ref[...]
            hb = h.astype(BF16)
            hb_scr[...] = hb
            ht_ref[...] = h.T.astype(BF16)
            lr_ref[...] = _dot(hb, wlr_ref[...])

        proj_ref[...] = _dot(hb_scr[...], w_ref[...])

    return pl.pallas_call(
        body, name="inproj",
        out_shape=(jax.ShapeDtypeStruct((seq, MAIN_W), F32), jax.ShapeDtypeStruct((seq, LR_W), F32),
                   jax.ShapeDtypeStruct((D_MODEL, seq), BF16)),
        grid=(seq // tm, MAIN_W // tn),
        in_specs=[pl.BlockSpec((tm, D_MODEL), lambda i, j: (i, 0)),
                  pl.BlockSpec((1, D_MODEL), lambda i, j: (0, 0)),
                  pl.BlockSpec((D_MODEL, tn), lambda i, j: (0, j)),
                  pl.BlockSpec((D_MODEL, LR_W), lambda i, j: (0, 0))],
        out_specs=(pl.BlockSpec((tm, tn), lambda i, j: (i, j)),
                   pl.BlockSpec((tm, LR_W), lambda i, j: (i, 0)),
                   pl.BlockSpec((D_MODEL, tm), lambda i, j: (0, i))),
        scratch_shapes=[pltpu.VMEM((tm, D_MODEL), BF16)],
        compiler_params=_cparams("arbitrary", "arbitrary"),
    )(x2d, norm_g, w_main, w_lr)


def _block_masks(tt):
    row = lax.broadcasted_iota(jnp.int32, (tt, tt), 0)
    col = lax.broadcasted_iota(jnp.int32, (tt, tt), 1)
    same = jnp.right_shift(row, 6) == jnp.right_shift(col, 6)
    return (jnp.logical_and(same, col <= row), jnp.logical_and(same, col >= row), jnp.logical_and(same, col > row))


def _chunk_column_mask(tt):
    nc = tt // CHUNK
    row = lax.broadcasted_iota(jnp.int32, (tt, nc * DK), 0)
    col = lax.broadcasted_iota(jnp.int32, (tt, nc * DK), 1)
    return jnp.right_shift(row, 6) == jnp.right_shift(col, 7)


def _dot_split3(ones_mat, x):
    x1 = x.astype(BF16)
    r1 = x - x1.astype(F32)
    x2 = r1.astype(BF16)
    x3 = (r1 - x2.astype(F32)).astype(BF16)
    return (_dot(ones_mat, x3) + _dot(ones_mat, x2)) + _dot(ones_mat, x1)


def _log_gate(logits):
    return (jnp.minimum(logits, 0.0) - jnp.log1p(jnp.exp(-jnp.abs(logits)))) * GATE_SCALE


def _chunked(mask, x, nc):
    wide = jnp.concatenate([x] * nc, axis=1)
    return jnp.where(mask, wide, jnp.zeros_like(wide))


def _gla_fwd(proj, lr, wgk_f, wgk_b, bgk_f, bgk_b, tt):
    seq = proj.shape[0]
    nb, nc, nch = seq // tt, tt // CHUNK, seq // CHUNK

    def body(qf, kf, vf, lrf, qb, kb, vb, lrb, wf, wb, bf, bb, of, ob, stf, stb, s_scr, qs_s, ks_s, qin_s, kout_s):
        @pl.when(pl.program_id(0) == 0)
        def _():
            s_scr[...] = jnp.zeros(s_scr.shape, F32)

        low, upp, sup = _block_masks(tt)
        kmask = _chunk_column_mask(tt)
        dirs = ((qf, kf, vf, lrf, wf, bf, of, stf, low, low, REF_F, LAST_F, list(range(nc))),
                (qb, kb, vb, lrb, wb, bb, ob, stb, upp, sup, REF_B, LAST_B, list(reversed(range(nc)))))
        for d, (q_r, k_r, v_r, lr_r, w_r, b_r, o_r, st_r, cum, mask, ref, last, order) in enumerate(dirs):
            logits = _dot(lr_r[...].astype(BF16), w_r[...]) + b_r[...]
            b = _dot_split3(cum.astype(BF16), _log_gate(logits))
            decs = []
            for c in range(nc):
                rows = slice(c * CHUNK, (c + 1) * CHUNK)
                bc = b[rows]
                b_ref, b_last = bc[ref:ref + 1], bc[last:last + 1]
                qc = q_r[rows, :] * QSCALE
                kc = k_r[rows, :]
                qs_s[rows, :] = (qc * jnp.exp(bc - b_ref)).astype(BF16)
                ks_s[rows, :] = (kc * jnp.exp(b_ref - bc)).astype(BF16)
                qin_s[rows, :] = (qc * jnp.exp(bc)).astype(BF16)
                kout_s[rows, :] = (kc * jnp.exp(b_last - bc)).astype(BF16)
                decs.append(jnp.exp(b_last))
            for h in range(HEADS):
                ksl = slice(h * DK, (h + 1) * DK)
                vsl = slice(h * DV, (h + 1) * DV)
                v = v_r[:, vsl].astype(BF16)
                att = jnp.where(mask, _dot_nt(qs_s[:, ksl], ks_s[:, ksl]), 0.0).astype(BF16)
                o_intra = _dot(att, v)
                kv_t = _dot_tn(v, _chunked(kmask, kout_s[:, ksl], nc))
                st = s_scr[d * HEADS + h]
                for c in order:
                    rows = slice(c * CHUNK, (c + 1) * CHUNK)
                    st_r[c, h] = st
                    o_r[rows, vsl] = o_intra[rows] + _dot_nt(qin_s[rows, ksl], st.astype(BF16))
                    st = st * decs[c][:, ksl] + kv_t[:, c * DK:(c + 1) * DK]
                s_scr[d * HEADS + h] = st

    fw = lambda i: (i, 0)
    bw = lambda i: (nb - 1 - i, 0)
    const = lambda i: (0, 0)

    def tok_specs(m):
        return [pl.BlockSpec((tt, QK_W), lambda i: (m(i)[0], OFF_Q // QK_W)),
                pl.BlockSpec((tt, QK_W), lambda i: (m(i)[0], OFF_K // QK_W)),
                pl.BlockSpec((tt, V_W), lambda i: (m(i)[0], OFF_V // V_W)),
                pl.BlockSpec((tt, LR_W), m)]

    st_shape = jax.ShapeDtypeStruct((nch, HEADS, DV, DK), F32)
    o_shape = jax.ShapeDtypeStruct((seq, V_W), F32)
    operand = pltpu.VMEM((tt, QK_W), BF16)
    return pl.pallas_call(
        body, name="gla_fwd",
        out_shape=(o_shape, o_shape, st_shape, st_shape),
        grid=(nb,),
        in_specs=tok_specs(fw) + tok_specs(bw) + [
            pl.BlockSpec((LR_W, QK_W), const), pl.BlockSpec((LR_W, QK_W), const),
            pl.BlockSpec((1, QK_W), const), pl.BlockSpec((1, QK_W), const)],
        out_specs=(pl.BlockSpec((tt, V_W), fw), pl.BlockSpec((tt, V_W), bw),
                   pl.BlockSpec((nc, HEADS, DV, DK), lambda i: (i, 0, 0, 0)),
                   pl.BlockSpec((nc, HEADS, DV, DK), lambda i: (nb - 1 - i, 0, 0, 0))),
        scratch_shapes=[pltpu.VMEM((2 * HEADS, DV, DK), F32), operand, operand, operand, operand],
        compiler_params=_cparams("arbitrary"),
    )(proj, proj, proj, lr, proj, proj, proj, lr, wgk_f, wgk_b, bgk_f, bgk_b)


def _head_norm(o, gain):
    outs, rinv = [], []
    for h in range(HEADS):
        oh = o[:, h * DV:(h + 1) * DV]
        r = lax.rsqrt(jnp.mean(oh * oh, axis=-1, keepdims=True) + EPS)
        outs.append((oh * r) * gain)
        rinv.append(r)
    return jnp.concatenate(outs, axis=1), rinv


def _shift_rows(u, prev_row, next_row):
    n = u.shape[0]
    row = lax.broadcasted_iota(jnp.int32, (n, 1), 0)
    up = jnp.where(row == 0, prev_row, pltpu.roll(u, 1, 0))
    un = jnp.where(row == n - 1, next_row, pltpu.roll(u, n - 1, 0))
    return up, un


def _halo_specs(tm, seq, col_block):
    per = tm // 8
    last = seq // 8 - 1
    return [pl.BlockSpec((8, CONV_W), lambda i: (jnp.maximum(i * per - 1, 0), col_block)),
            pl.BlockSpec((8, CONV_W), lambda i: (jnp.minimum((i + 1) * per, last), col_block))]


def _mix_out_loss(o_f, o_b, proj, x2d, tgt, gla_g, conv_w, conv_b, w_out, final_g, tm):
    seq = x2d.shape[0]
    nt = seq // tm

    def body(of, ob, za, bg, cg, hc, zc, cprev, cnext, hprev, hnext, x_ref, t_ref, gg, cw, cb, wo, fg,
             yt_ref, conv_ref, dx2_ref, dx2b_ref, loss_ref, dfg_ref):
        i = pl.program_id(0)

        @pl.when(i == 0)
        def _():
            loss_ref[...] = jnp.zeros(loss_ref.shape, F32)
            dfg_ref[...] = jnp.zeros(dfg_ref.shape, F32)

        on, _ = _head_norm(of[...] + ob[...], gg[...])
        zav = za[...]
        y_a = on * (zav * _sigmoid(zav))
        u = cg[...] * hc[...]
        prev_row = jnp.where(i > 0, cprev[7:8, :] * hprev[7:8, :], 0.0)
        next_row = jnp.where(i < nt - 1, cnext[0:1, :] * hnext[0:1, :], 0.0)
        up, un = _shift_rows(u, prev_row, next_row)
        conv = (cw[0:1, :] * up + cw[1:2, :] * u + cw[2:3, :] * un) + cb[...]
        conv_ref[...] = conv
        zcv = zc[...]
        y_c = bg[...] * conv * (zcv * _sigmoid(zcv))
        y = jnp.concatenate([y_a, y_c], axis=1)
        yt_ref[...] = y.T.astype(BF16)
        x2 = x_ref[...] + _dot(y.astype(BF16), wo[...])
        r = lax.rsqrt(jnp.mean(x2 * x2, axis=-1, keepdims=True) + EPS)
        xn = x2 * r
        err = xn * fg[...] - t_ref[...]
        loss_ref[...] += 0.5 * jnp.sum(jnp.mean(err * err, axis=-1, keepdims=True))
        dyf = err * (1.0 / D_MODEL)
        dfg_ref[...] += jnp.sum(dyf * xn, axis=0, keepdims=True)
        dxn = dyf * fg[...]
        dx2 = r * dxn - xn * (r * jnp.mean(dxn * xn, axis=-1, keepdims=True))
        dx2_ref[...] = dx2
        dx2b_ref[...] = dx2.astype(BF16)

    def col(off):
        return pl.BlockSpec((tm, CONV_W), lambda i: (i, off // CONV_W))

    rowt = pl.BlockSpec((tm, D_MODEL), lambda i: (i, 0))
    const = lambda shape: pl.BlockSpec(shape, lambda i: (0, 0))
    return pl.pallas_call(
        body, name="mix_out_loss",
        out_shape=(jax.ShapeDtypeStruct((MIX_W, seq), BF16), jax.ShapeDtypeStruct((seq, CONV_W), F32),
                   jax.ShapeDtypeStruct((seq, D_MODEL), F32), jax.ShapeDtypeStruct((seq, D_MODEL), BF16),
                   jax.ShapeDtypeStruct((8, 128), F32), jax.ShapeDtypeStruct((1, D_MODEL), F32)),
        grid=(nt,),
        in_specs=[rowt, rowt, col(OFF_ZA), col(OFF_B), col(OFF_C), col(OFF_H), col(OFF_ZC)]
        + _halo_specs(tm, seq, OFF_C // CONV_W) + _halo_specs(tm, seq, OFF_H // CONV_W)
        + [rowt, rowt, const((1, DV)), const((8, CONV_W)), const((1, CONV_W)), const((MIX_W, D_MODEL)),
           const((1, D_MODEL))],
        out_specs=(pl.BlockSpec((MIX_W, tm), lambda i: (0, i)), rowt, rowt, rowt, const((8, 128)),
                   const((1, D_MODEL))),
        compiler_params=_cparams("arbitrary"),
    )(o_f, o_b, proj, proj, proj, proj, proj, proj, proj, proj, proj, x2d, tgt, gla_g, conv_w, conv_b, w_out, final_g)


def _dsilu(z, s):
    return s * (1.0 + z * (1.0 - s))


def _mix_bwd(dx2b, o_f, o_b, proj, conv, gla_g, w_out, tm):
    seq = dx2b.shape[0]

    def body(dx, of, ob, za, bg, zc, cv, gg, wo, dg_ref, do_ref, dconv_ref, dgg_ref, dcb_ref):
        @pl.when(pl.program_id(0) == 0)
        def _():
            dgg_ref[...] = jnp.zeros(dgg_ref.shape, F32)
            dcb_ref[...] = jnp.zeros(dcb_ref.shape, F32)

        dy = _dot_nt(dx[...], wo[...])
        dy_a, dy_c = dy[:, :V_W], dy[:, V_W:]
        zcv, bgv, convv = zc[...], bg[...], cv[...]
        sc = _sigmoid(zcv)
        szc = zcv * sc
        dg_ref[:, CONV_W:2 * CONV_W] = (dy_c * convv * szc).astype(BF16)
        dconv = dy_c * bgv * szc
        dconv_ref[...] = dconv
        dcb_ref[...] += jnp.sum(dconv, axis=0, keepdims=True)
        dg_ref[:, 2 * CONV_W:] = (dy_c * bgv * convv * _dsilu(zcv, sc)).astype(BF16)

        o = of[...] + ob[...]
        gain = gg[...]
        on, rinv = _head_norm(o, gain)
        zav = za[...]
        sa = _sigmoid(zav)
        dg_ref[:, :CONV_W] = (dy_a * on * _dsilu(zav, sa)).astype(BF16)
        don = dy_a * (zav * sa)
        dgg = jnp.zeros((1, DV), F32)
        dos = []
        for h in range(HEADS):
            sl = slice(h * DV, (h + 1) * DV)
            oh, r, dh = o[:, sl], rinv[h], don[:, sl]
            ohn = oh * r
            dgg = dgg + jnp.sum(dh * ohn, axis=0, keepdims=True)
            dn = dh * gain
            dos.append(r * dn - ohn * (r * jnp.mean(dn * ohn, axis=-1, keepdims=True)))
        dgg_ref[...] += dgg
        do_ref[...] = jnp.concatenate(dos, axis=1)

    def col(off):
        return pl.BlockSpec((tm, CONV_W), lambda i: (i, off // CONV_W))

    rowt = pl.BlockSpec((tm, D_MODEL), lambda i: (i, 0))
    const = lambda shape: pl.BlockSpec(shape, lambda i: (0, 0))
    return pl.pallas_call(
        body, name="mix_bwd",
        out_shape=(jax.ShapeDtypeStruct((seq, GATES_W), BF16), jax.ShapeDtypeStruct((seq, V_W), F32),
                   jax.ShapeDtypeStruct((seq, CONV_W), F32),
                   jax.ShapeDtypeStruct((1, DV), F32), jax.ShapeDtypeStruct((1, CONV_W), F32)),
        grid=(seq // tm,),
        in_specs=[rowt, rowt, rowt, col(OFF_ZA), col(OFF_B), col(OFF_ZC), rowt, const((1, DV)),
                  const((MIX_W, D_MODEL))],
        out_specs=(pl.BlockSpec((tm, GATES_W), lambda i: (i, 0)), rowt, rowt, const((1, DV)), const((1, CONV_W))),
        compiler_params=_cparams("arbitrary"),
    )(dx2b, o_f, o_b, proj, proj, proj, conv, gla_g, w_out)


def _conv_bwd(dconv, proj, conv_w, tm):
    seq = dconv.shape[0]
    nt = seq // tm

    def body(dc_in, dprev, dnext, cg, hc, cprev, cnext, hprev, hnext, cw, dch_ref, dcw_ref):
        i = pl.program_id(0)

        @pl.when(i == 0)
        def _():
            dcw_ref[...] = jnp.zeros(dcw_ref.shape, F32)

        first, lastt = i > 0, i < nt - 1
        dcv = dc_in[...]
        d_up, d_un = _shift_rows(dcv, jnp.where(first, dprev[7:8, :], 0.0), jnp.where(lastt, dnext[0:1, :], 0.0))
        cgv, hcv = cg[...], hc[...]
        u = cgv * hcv
        u_up, u_un = _shift_rows(u, jnp.where(first, cprev[7:8, :] * hprev[7:8, :], 0.0),
                                 jnp.where(lastt, cnext[0:1, :] * hnext[0:1, :], 0.0))
        du = cw[0:1, :] * d_un + cw[1:2, :] * dcv + cw[2:3, :] * d_up
        dch_ref[:, :CONV_W] = (du * hcv).astype(BF16)
        dch_ref[:, CONV_W:] = (du * cgv).astype(BF16)
        dcw_ref[0:1, :] += jnp.sum(dcv * u_up, axis=0, keepdims=True)
        dcw_ref[1:2, :] += jnp.sum(dcv * u, axis=0, keepdims=True)
        dcw_ref[2:3, :] += jnp.sum(dcv * u_un, axis=0, keepdims=True)

    def col(off):
        return pl.BlockSpec((tm, CONV_W), lambda i: (i, off // CONV_W))

    rowt = pl.BlockSpec((tm, CONV_W), lambda i: (i, 0))
    const = lambda shape: pl.BlockSpec(shape, lambda i: (0, 0))
    return pl.pallas_call(
        body, name="conv_bwd",
        out_shape=(jax.ShapeDtypeStruct((seq, CH_W), BF16), jax.ShapeDtypeStruct((8, CONV_W), F32)),
        grid=(nt,),
        in_specs=[rowt] + _halo_specs(tm, seq, 0) + [col(OFF_C), col(OFF_H)]
        + _halo_specs(tm, seq, OFF_C // CONV_W) + _halo_specs(tm, seq, OFF_H // CONV_W) + [const((8, CONV_W))],
        out_specs=(pl.BlockSpec((tm, CH_W), lambda i: (i, 0)), const((8, CONV_W))),
        compiler_params=_cparams("arbitrary"),
    )(dconv, dconv, dconv, proj, proj, proj, proj, proj, proj, conv_w)


def _gla_bwd(proj, lr, do, st_f, st_b, wgk_f, wgk_b, bgk_f, bgk_b, tt):
    seq = proj.shape[0]
    nb, nc = seq // tt, tt // CHUNK

    def body(qf, kf, vf, lrf, dof, stf, qb, kb, vb, lrb, dob, stb, wf, wb, bf, bb,
             dqkv_f, dlr_f, dqkv_b, dlr_b, dwf, dwb, dbf, dbb,
             ds_scr, eq_s, ek_s, ein_s, eout_s, qs_s, ks_s, qin_s, kout_s, db_s, lg_s):
        @pl.when(pl.program_id(0) == 0)
        def _():
            ds_scr[...] = jnp.zeros(ds_scr.shape, F32)
            for r in (dwf, dwb, dbf, dbb):
                r[...] = jnp.zeros(r.shape, F32)

        low, upp, sup = _block_masks(tt)
        kmask = _chunk_column_mask(tt)
        row = lax.broadcasted_iota(jnp.int32, (CHUNK, 1), 0)
        dirs = ((qf, kf, vf, lrf, dof, stf, wf, bf, dqkv_f, dlr_f, dwf, dbf,
                 low, upp, low, REF_F, LAST_F, list(reversed(range(nc)))),
                (qb, kb, vb, lrb, dob, stb, wb, bb, dqkv_b, dlr_b, dwb, dbb,
                 upp, low, sup, REF_B, LAST_B, list(range(nc))))
        for d, (q_r, k_r, v_r, lr_r, do_r, st_r, w_r, b_r, dqkv_r, dlr_r, dw_r, db_r,
                cum, cum_t, mask, ref, last, order) in enumerate(dirs):
            lrv = lr_r[...].astype(BF16)
            wv = w_r[...]
            logits = _dot(lrv, wv) + b_r[...]
            lg_s[...] = logits
            b = _dot_split3(cum.astype(BF16), _log_gate(logits))
            decs = []
            for c in range(nc):
                rows = slice(c * CHUNK, (c + 1) * CHUNK)
                bc = b[rows]
                b_ref, b_last = bc[ref:ref + 1], bc[last:last + 1]
                qc = q_r[rows, :] * QSCALE
                kc = k_r[rows, :]
                e_q, e_k, e_in, e_out = jnp.exp(bc - b_ref), jnp.exp(b_ref - bc), jnp.exp(bc), jnp.exp(b_last - bc)
                eq_s[rows, :], ek_s[rows, :], ein_s[rows, :], eout_s[rows, :] = e_q, e_k, e_in, e_out
                qs_s[rows, :] = (qc * e_q).astype(BF16)
                ks_s[rows, :] = (kc * e_k).astype(BF16)
                qin_s[rows, :] = (qc * e_in).astype(BF16)
                kout_s[rows, :] = (kc * e_out).astype(BF16)
                decs.append(jnp.exp(b_last))
            for h in range(HEADS):
                ksl = slice(h * DK, (h + 1) * DK)
                vsl = slice(h * DV, (h + 1) * DV)
                v = v_r[:, vsl].astype(BF16)
                dov = do_r[:, vsl].astype(BF16)
                qsb, ksb = qs_s[:, ksl], ks_s[:, ksl]
                att = jnp.where(mask, _dot_nt(qsb, ksb), 0.0).astype(BF16)
                datt = jnp.where(mask, _dot_nt(dov, v), 0.0).astype(BF16)
                dqs = _dot(datt, ksb)
                dks = _dot_tn(datt, qsb)
                dv_intra = _dot_tn(att, dov)
                g_t = _dot_tn(dov, _chunked(kmask, qin_s[:, ksl], nc))
                ds = ds_scr[d * HEADS + h]
                for c in order:
                    rows = slice(c * CHUNK, (c + 1) * CHUNK)
                    dsb = ds.astype(BF16)
                    s_prev = st_r[c, h]
                    dk_out = _dot(v[rows], dsb)
                    dq_in = _dot(dov[rows], s_prev.astype(BF16))
                    dqkv_r[rows, OFF_V + h * DV:OFF_V + (h + 1) * DV] = dv_intra[rows] + _dot_nt(kout_s[rows, ksl], dsb)
                    dec = decs[c][:, ksl]
                    ddec = jnp.sum(ds * s_prev, axis=0, keepdims=True)
                    e_q, e_k, e_in, e_out = eq_s[rows, ksl], ek_s[rows, ksl], ein_s[rows, ksl], eout_s[rows, ksl]
                    qc = q_r[rows, ksl] * QSCALE
                    kc = k_r[rows, ksl]
                    dqs_c, dks_c = dqs[rows], dks[rows]
                    dqkv_r[rows, OFF_Q + h * DK:OFF_Q + (h + 1) * DK] = (dqs_c * e_q + dq_in * e_in) * QSCALE
                    dqkv_r[rows, OFF_K + h * DK:OFF_K + (h + 1) * DK] = dks_c * e_k + dk_out * e_out
                    kk = dk_out * (kc * e_out)
                    db = dqs_c * (qc * e_q) - dks_c * (kc * e_k) + dq_in * (qc * e_in) - kk
                    tail = jnp.sum(kk, axis=0, keepdims=True) + ddec * dec
                    db_s[rows, ksl] = db + jnp.where(row == last, tail, 0.0)
                    ds = ds * dec + g_t[:, c * DK:(c + 1) * DK]
                ds_scr[d * HEADS + h] = ds
            dg = _dot_split3(cum_t.astype(BF16), db_s[...])
            dlogit = (dg * GATE_SCALE) * _sigmoid(-lg_s[...])
            dlb = dlogit.astype(BF16)
            dlr_r[...] = _dot_nt(dlb, wv)
            dw_r[...] += _dot_tn(lrv, dlb)
            db_r[...] += jnp.sum(dlogit, axis=0, keepdims=True)

    fw = lambda i: (nb - 1 - i, 0)
    bw = lambda i: (i, 0)
    const = lambda i: (0, 0)

    def tok_specs(m):
        return [pl.BlockSpec((tt, QK_W), lambda i: (m(i)[0], OFF_Q // QK_W)),
                pl.BlockSpec((tt, QK_W), lambda i: (m(i)[0], OFF_K // QK_W)),
                pl.BlockSpec((tt, V_W), lambda i: (m(i)[0], OFF_V // V_W)),
                pl.BlockSpec((tt, LR_W), m),
                pl.BlockSpec((tt, V_W), m),
                pl.BlockSpec((nc, HEADS, DV, DK), lambda i: (m(i)[0], 0, 0, 0))]

    dqkv = jax.ShapeDtypeStruct((seq, QK_W + QK_W + V_W), F32)
    dlr = jax.ShapeDtypeStruct((seq, LR_W), F32)
    dw = jax.ShapeDtypeStruct((LR_W, QK_W), F32)
    dbias = jax.ShapeDtypeStruct((1, QK_W), F32)
    return pl.pallas_call(
        body, name="gla_bwd",
        out_shape=(dqkv, dlr, dqkv, dlr, dw, dw, dbias, dbias),
        grid=(nb,),
        in_specs=tok_specs(fw) + tok_specs(bw) + [
            pl.BlockSpec((LR_W, QK_W), const), pl.BlockSpec((LR_W, QK_W), const),
            pl.BlockSpec((1, QK_W), const), pl.BlockSpec((1, QK_W), const)],
        out_specs=(pl.BlockSpec((tt, QK_W + QK_W + V_W), fw), pl.BlockSpec((tt, LR_W), fw),
                   pl.BlockSpec((tt, QK_W + QK_W + V_W), bw), pl.BlockSpec((tt, LR_W), bw),
                   pl.BlockSpec((LR_W, QK_W), const), pl.BlockSpec((LR_W, QK_W), const),
                   pl.BlockSpec((1, QK_W), const), pl.BlockSpec((1, QK_W), const)),
        scratch_shapes=[pltpu.VMEM((2 * HEADS, DV, DK), F32)] + [pltpu.VMEM((tt, QK_W), F32)] * 4
        + [pltpu.VMEM((tt, QK_W), BF16)] * 4 + [pltpu.VMEM((tt, QK_W), F32)] * 2,
        compiler_params=_cparams("arbitrary"),
    )(proj, proj, proj, lr, do, st_f, proj, proj, proj, lr, do, st_b, wgk_f, wgk_b, bgk_f, bgk_b)


def _sum_directions(dqkv_f, dqkv_b, dlr_f, dlr_b, tm):
    seq = dqkv_f.shape[0]

    def body(a, b, la, lb, dp_out, dlr_out):
        dp_out[...] = (a[...] + b[...]).astype(BF16)
        dlr_out[...] = (la[...] + lb[...]).astype(BF16)

    rowt = pl.BlockSpec((tm, QKV_W), lambda i: (i, 0))
    lrt = pl.BlockSpec((tm, LR_W), lambda i: (i, 0))
    return pl.pallas_call(
        body, name="sum_directions",
        out_shape=(jax.ShapeDtypeStruct((seq, QKV_W), BF16), jax.ShapeDtypeStruct((seq, LR_W), BF16)),
        grid=(seq // tm,),
        in_specs=[rowt, rowt, lrt, lrt],
        out_specs=(rowt, lrt),
        compiler_params=_cparams("arbitrary"),
    )(dqkv_f, dqkv_b, dlr_f, dlr_b)


def _input_grad(dp_qkv, dp_gates, dp_ch, dlr, w_main, w_lr, x2d, norm_g, dx2, tm):
    seq = x2d.shape[0]
    tk = 1024
    k_gates, k_ch, nk = QKV_W // tk, (QKV_W + GATES_W) // tk, MAIN_W // tk

    def body(dq, dg, dc, dl, w, wl, x_ref, g_ref, dx2_ref, gx_ref, dng_ref, acc):
        i, k = pl.program_id(0), pl.program_id(1)

        @pl.when(jnp.logical_and(i == 0, k == 0))
        def _():
            dng_ref[...] = jnp.zeros(dng_ref.shape, F32)

        @pl.when(k == 0)
        def _():
            acc[...] = _dot_nt(dl[...], wl[...])

        @pl.when(k < k_gates)
        def _():
            acc[...] += _dot_nt(dq[...], w[...])

        @pl.when(jnp.logical_and(k >= k_gates, k < k_ch))
        def _():
            acc[...] += _dot_nt(dg[...], w[...])

        @pl.when(k >= k_ch)
        def _():
            acc[...] += _dot_nt(dc[...], w[...])

        @pl.when(k == nk - 1)
        def _():
            dh = acc[...]
            xv = x_ref[...]
            r = lax.rsqrt(jnp.mean(xv * xv, axis=-1, keepdims=True) + EPS)
            xn = xv * r
            dng_ref[...] += jnp.sum(dh * xn, axis=0, keepdims=True)
            dn = dh * g_ref[...]
            gx_ref[...] = (r * dn - xn * (r * jnp.mean(dn * xn, axis=-1, keepdims=True))) + dx2_ref[...]

    rowt = pl.BlockSpec((tm, D_MODEL), lambda i, k: (i, 0))
    return pl.pallas_call(
        body, name="input_grad",
        out_shape=(jax.ShapeDtypeStruct((seq, D_MODEL), F32), jax.ShapeDtypeStruct((1, D_MODEL), F32)),
        grid=(seq // tm, nk),
        in_specs=[pl.BlockSpec((tm, tk), lambda i, k: (i, jnp.minimum(k, k_gates - 1))),
                  pl.BlockSpec((tm, tk), lambda i, k: (i, jnp.clip(k - k_gates, 0, k_ch - k_gates - 1))),
                  pl.BlockSpec((tm, tk), lambda i, k: (i, jnp.clip(k - k_ch, 0, nk - k_ch - 1))),
                  pl.BlockSpec((tm, LR_W), lambda i, k: (i, 0)),
                  pl.BlockSpec((D_MODEL, tk), lambda i, k: (0, k)),
                  pl.BlockSpec((D_MODEL, LR_W), lambda i, k: (0, 0)),
                  rowt, pl.BlockSpec((1, D_MODEL), lambda i, k: (0, 0)), rowt],
        out_specs=(rowt, pl.BlockSpec((1, D_MODEL), lambda i, k: (0, 0))),
        scratch_shapes=[pltpu.VMEM((tm, D_MODEL), F32)],
        compiler_params=_cparams("arbitrary", "arbitrary"),
    )(dp_qkv, dp_gates, dp_ch, dlr, w_main, w_lr, x2d, norm_g, dx2)


def _weight_grad(at, b, tn, tk, name):
    m, seq = at.shape
    n = b.shape[1]

    def body(a_ref, b_ref, o_ref):
        @pl.when(pl.program_id(1) == 0)
        def _():
            o_ref[...] = jnp.zeros(o_ref.shape, F32)

        o_ref[...] += _dot(a_ref[...], b_ref[...])

    return pl.pallas_call(
        body, name=name,
        out_shape=jax.ShapeDtypeStruct((m, n), F32),
        grid=(n // tn, seq // tk),
        in_specs=[pl.BlockSpec((m, tk), lambda j, k: (0, k)), pl.BlockSpec((tk, tn), lambda j, k: (k, j))],
        out_specs=pl.BlockSpec((m, tn), lambda j, k: (0, j)),
        compiler_params=_cparams("arbitrary", "arbitrary"),
    )(at, b)


def _pad_rows(a, rows):
    return jnp.pad(a, ((0, rows - a.shape[0]), (0, 0)))


def _rows128(a):
    a = a.reshape(-1, 128)
    return _pad_rows(a, -(-a.shape[0] // 8) * 8)


def _pack(arrs):
    return jnp.concatenate([_rows128(a) for a in arrs], axis=0)


def _unpack(buf, like):
    out, start = [], 0
    for a in like:
        rows = a.size // 128
        out.append(buf[start:start + rows].reshape(a.shape))
        start += -(-rows // 8) * 8
    return out


def kernel(x, norm_g, w_in, w_gk_f, b_gk_f, w_gk_b, b_gk_b, gla_norm_g, conv_w, conv_b, w_out, final_g, loss_target, m_norm_g, m_w_in, m_w_gk_f, m_b_gk_f, m_w_gk_b, m_b_gk_b, m_gla_norm_g, m_conv_w, m_conv_b, m_w_out, m_final_g, v_norm_g, v_w_in, v_w_gk_f, v_b_gk_f, v_w_gk_b, v_b_gk_b, v_gla_norm_g, v_conv_w, v_conv_b, v_w_out, v_final_g):
    px, py, pc = _position()
    me = _blk(px, py, pc)
    seq = x.shape[1]
    x2d, tgt = x[0], loss_target[0]
    tm = min(512, seq)
    tt = min(256, seq)

    small_s = jnp.concatenate([jnp.concatenate([w_gk_f[0], w_gk_b[0]], axis=1), _pad_rows(conv_w[0], 8)], axis=0)
    win_all, wout_all, small_all = _allgather_weights(w_in[0], w_out[0], small_s)
    w_nat = win_all.transpose(1, 0, 2).reshape(D_MODEL, IN_W)
    w_main = jnp.concatenate([w_nat[:, :NAT_LR], w_nat[:, NAT_B:NAT_C], w_nat[:, NAT_ZC:], w_nat[:, NAT_C:NAT_ZC]], axis=1)
    w_lr = jnp.concatenate([w_nat[:, NAT_LR:NAT_B], jnp.zeros((D_MODEL, LR_W - 2 * RANK), BF16)], axis=1)
    w_out_full = wout_all.reshape(MIX_W, D_MODEL)
    wgk_cols = 512 // N_DEV
    wgk_f_full = small_all[:, 0:RANK, 0:wgk_cols].transpose(1, 0, 2).reshape(RANK, QK_W)
    wgk_b_full = small_all[:, 0:RANK, wgk_cols:2 * wgk_cols].transpose(1, 0, 2).reshape(RANK, QK_W)
    conv_w_full = _pad_rows(small_all[:, RANK:RANK + 3, :].transpose(1, 0, 2).reshape(3, CONV_W), 8)
    zr = lambda n: jnp.zeros((n, QK_W), F32)
    wgk_f_pad = jnp.concatenate([wgk_f_full, zr(LR_W - RANK)], axis=0).astype(BF16)
    wgk_b_pad = jnp.concatenate([zr(RANK), wgk_b_full, zr(LR_W - 2 * RANK)], axis=0).astype(BF16)

    proj, lr, h_t = _inproj(x2d, norm_g, w_main, w_lr, tm, 1024)
    o_f, o_b, st_f, st_b = _gla_fwd(proj, lr, wgk_f_pad, wgk_b_pad, b_gk_f, b_gk_b, tt)
    tmix = min(256, seq)
    y_t, conv, dx2, dx2b, loss_p, dfg_p = _mix_out_loss(o_f, o_b, proj, x2d, tgt, gla_norm_g, conv_w_full, conv_b,
                                                        w_out_full, final_g.reshape(1, D_MODEL), tmix)

    dp_gates, do, dconv, dgg_p, dcb_p = _mix_bwd(dx2b, o_f, o_b, proj, conv, gla_norm_g, w_out_full, tmix)
    dp_ch, dcw_p = _conv_bwd(dconv, proj, conv_w_full, tmix)
    dqkv_f, dlr_f, dqkv_b, dlr_b, dwf_p, dwb_p, dbf_p, dbb_p = _gla_bwd(
        proj, lr, do, st_f, st_b, wgk_f_pad, wgk_b_pad, b_gk_f, b_gk_b, tt)
    dp_qkv, dlr = _sum_directions(dqkv_f, dqkv_b, dlr_f, dlr_b, tm)
    grad_x2d, dng_p = _input_grad(dp_qkv, dp_gates, dp_ch, dlr, w_main, w_lr, x2d, norm_g, dx2, tm)
    dw_qkv = _weight_grad(h_t, dp_qkv, 1024, tm, "wgrad_qkv")
    dw_gates = _weight_grad(h_t, dp_gates, 1024, tm, "wgrad_gates")
    dw_ch = _weight_grad(h_t, dp_ch, 1024, tm, "wgrad_ch")
    dw_lr = _weight_grad(h_t, dlr, LR_W, tm, "wgrad_lr")
    dw_out = _weight_grad(y_t, dx2b, D_MODEL, tm, "wgrad_out")

    dw_nat = jnp.concatenate([dw_qkv, dw_gates[:, :CONV_W], dw_lr[:, :2 * RANK], dw_gates[:, CONV_W:2 * CONV_W], dw_ch,
                              dw_gates[:, 2 * CONV_W:]], axis=1)
    part_in = dw_nat.reshape(D_MODEL, N_DEV, SHARD_W).transpose(1, 0, 2)
    part_out = dw_out.reshape(N_DEV, MIX_W // N_DEV, D_MODEL)
    sib_in, sib_out = _exchange_sibling([part_in, part_out])
    core = jnp.reshape(pc, (1,)).astype(jnp.int32)
    chip = jnp.reshape(2 * px + py, (1,)).astype(jnp.int32)
    sums_in = _chip_sums(part_in, sib_in, core, 256, "chip_sums_in")
    sums_out = _chip_sums(part_out, sib_out, core, 256, "chip_sums_out")
    far_in, far_out = _exchange_chips([sums_in, sums_out])
    g_w_in, d_w_in, nm_w_in, nv_w_in = _final_sum_adamw(sums_in, far_in, chip, w_in[0], m_w_in[0], v_w_in[0], 256,
                                                        "adamw_in")
    g_w_out, d_w_out, nm_w_out, nv_w_out = _final_sum_adamw(sums_out, far_out, chip, w_out[0], m_w_out[0], v_w_out[0],
                                                            256, "adamw_out")

    pieces = [dng_p, dbf_p, dbb_p, dgg_p, dcb_p, dfg_p[0], dwf_p[0:RANK], dwb_p[RANK:2 * RANK], dcw_p[0:3], loss_p[0]]
    tot = _unpack(_allreduce_small(_pack(pieces)), pieces)
    g_norm_g, g_b_gk_f, g_b_gk_b, g_gla, g_conv_b, g_final = tot[:6]
    g_wgk_f = lax.dynamic_slice_in_dim(tot[6], me * wgk_cols, wgk_cols, axis=1)[None]
    g_wgk_b = lax.dynamic_slice_in_dim(tot[7], me * wgk_cols, wgk_cols, axis=1)[None]
    g_conv_w = lax.dynamic_slice_in_dim(tot[8], me * 128, 128, axis=1)[None]
    loss = tot[9][0]

    small_g = [g_norm_g, g_b_gk_f, g_b_gk_b, g_gla, g_conv_b, g_final, g_wgk_f, g_wgk_b, g_conv_w]
    small_w = [norm_g, b_gk_f, b_gk_b, gla_norm_g, conv_b, final_g, w_gk_f, w_gk_b, conv_w]
    small_m = [m_norm_g, m_b_gk_f, m_b_gk_b, m_gla_norm_g, m_conv_b, m_final_g, m_w_gk_f, m_w_gk_b, m_conv_w]
    small_v = [v_norm_g, v_b_gk_f, v_b_gk_b, v_gla_norm_g, v_conv_b, v_final_g, v_w_gk_f, v_w_gk_b, v_conv_w]
    d_s, m_s, v_s = _adamw_small(_pack(small_g), _pack(small_w), _pack(small_m), _pack(small_v))
    d_l, m_l, v_l = _unpack(d_s, small_w), _unpack(m_s, small_w), _unpack(v_s, small_w)

    def ordered(sm, big_in, big_out):
        return [sm[0], big_in[None], sm[6], sm[1], sm[7], sm[2], sm[3], sm[8], sm[4], big_out[None], sm[5]]

    grads = ordered(small_g, g_w_in, g_w_out)
    deltas = ordered(d_l, d_w_in, d_w_out)
    new_m = ordered(m_l, nm_w_in, nm_w_out)
    new_v = ordered(v_l, nv_w_in, nv_w_out)
    return (loss, grad_x2d[None], *grads, *deltas, *new_m, *new_v)
```

```python
import jax
import jax.numpy as jnp
from jax import lax
from jax.experimental import pallas as pl
from jax.experimental.pallas import tpu as pltpu

F32 = jnp.float32
BF16 = jnp.bfloat16
MESH = pl.DeviceIdType.MESH

N_DEV = 8
D_MODEL = 1024
HEADS = 4
DK = 128
DV = 256
QK_W = HEADS * DK
V_W = HEADS * DV
CONV_W = 1024
MIX_W = V_W + CONV_W
CHUNK = 64
RANK = 16
IN_W = 7200
SHARD_W = IN_W // N_DEV
MAIN_W = 7168
LR_W = 128
OFF_Q, OFF_K, OFF_V, OFF_ZA, OFF_B, OFF_ZC, OFF_C, OFF_H = 0, 512, 1024, 2048, 3072, 4096, 5120, 6144
QKV_W, GATES_W, CH_W = 2048, 3072, 2048
NAT_ZA, NAT_LR, NAT_B, NAT_C, NAT_ZC = 2048, 3072, 3104, 4128, 6176
EPS = 1e-6
GATE_SCALE = 1.0 / 16.0
QSCALE = DK ** -0.5
REF_F, LAST_F = CHUNK // 2, CHUNK - 1
REF_B, LAST_B = CHUNK - 1 - CHUNK // 2, 0

ADAM_LR = 0.001
ADAM_B1 = 0.9
ADAM_B2 = 0.999
ADAM_EPS = 1e-08
ADAM_WD = 0.01
ADAM_STEP = 10

VMEM_LIMIT = 56 * 1024 * 1024


def _cparams(*sem):
    return pltpu.CompilerParams(dimension_semantics=sem, vmem_limit_bytes=VMEM_LIMIT)


def _dot(a, b):
    return jnp.dot(a, b, preferred_element_type=F32)


def _dot_nt(a, b):
    return lax.dot_general(a, b, (((1,), (1,)), ((), ())), preferred_element_type=F32)


def _dot_tn(a, b):
    return lax.dot_general(a, b, (((0,), (0,)), ((), ())), preferred_element_type=F32)


def _sigmoid(z):
    return jax.nn.sigmoid(z)


def _position():
    return lax.axis_index("x"), lax.axis_index("y"), lax.axis_index("c")


def _blk(px, py, pc):
    return 4 * px + 2 * py + pc


def _two_level_gather(outs, send_sems, recv_sems):
    x, y, c = _position()
    me, sibling = (x, y, c), (x, y, 1 - c)
    chips = [(1 - x, y), (x, 1 - y), (1 - x, 1 - y)]
    n = len(outs)

    def copy(a, k, block, to):
        ref = outs[a].at[_blk(*block)]
        return pltpu.make_async_remote_copy(src_ref=ref, dst_ref=ref, send_sem=send_sems.at[a * 7 + k],
                                            recv_sem=recv_sems.at[a * 7 + k], device_id=to, device_id_type=MESH)

    first = []
    for a in range(n):
        first.append(copy(a, 0, me, sibling))
        first += [copy(a, 1 + j, me, (*chip, c)) for j, chip in enumerate(chips)]
    for cp in first:
        cp.start()
    passed = []
    for j, chip in enumerate(chips):
        for a in range(n):
            copy(a, 1 + j, (*chip, c), me).wait_recv()
            fwd = copy(a, 4 + j, (*chip, c), sibling)
            fwd.start()
            passed.append(fwd)
    for a in range(n):
        copy(a, 0, sibling, me).wait_recv()
    for j, chip in enumerate(chips):
        for a in range(n):
            copy(a, 4 + j, (*chip, 1 - c), me).wait_recv()
    for cp in first + passed:
        cp.wait_send()


def _allgather_weights(w_in_s, w_out_s, small_s):
    def body(win_ref, wout_ref, sm_ref, win_all, wout_all, sm_all, send_sems, recv_sems):
        mine = _blk(*_position())
        win_all[mine] = win_ref[...].astype(BF16)
        wout_all[mine] = wout_ref[...].astype(BF16)
        sm_all[mine] = sm_ref[...]
        _two_level_gather((win_all, wout_all, sm_all), send_sems, recv_sems)

    vmem = pl.BlockSpec(memory_space=pltpu.VMEM)
    return pl.pallas_call(
        body, name="allgather_weights",
        out_shape=(jax.ShapeDtypeStruct((N_DEV,) + w_in_s.shape, BF16),
                   jax.ShapeDtypeStruct((N_DEV,) + w_out_s.shape, BF16),
                   jax.ShapeDtypeStruct((N_DEV,) + small_s.shape, F32)),
        in_specs=[vmem, vmem, vmem], out_specs=(vmem, vmem, vmem),
        scratch_shapes=[pltpu.SemaphoreType.DMA((21,)), pltpu.SemaphoreType.DMA((21,))],
        compiler_params=pltpu.CompilerParams(vmem_limit_bytes=VMEM_LIMIT),
    )(w_in_s, w_out_s, small_s)


def _allreduce_small(part):
    def body(p_ref, tot_ref, all_ref, send_sems, recv_sems):
        mine = _blk(*_position())
        all_ref[mine] = p_ref[...]
        _two_level_gather((all_ref,), send_sems, recv_sems)
        acc = all_ref[0]
        for d in range(1, N_DEV):
            acc = acc + all_ref[d]
        tot_ref[...] = acc

    vmem = pl.BlockSpec(memory_space=pltpu.VMEM)
    return pl.pallas_call(
        body, name="allreduce_small",
        out_shape=jax.ShapeDtypeStruct(part.shape, F32),
        in_specs=[vmem], out_specs=vmem,
        scratch_shapes=[pltpu.VMEM((N_DEV,) + part.shape, F32),
                        pltpu.SemaphoreType.DMA((7,)), pltpu.SemaphoreType.DMA((7,))],
        compiler_params=pltpu.CompilerParams(vmem_limit_bytes=VMEM_LIMIT),
    )(part)


def _exchange_sibling(parts):
    n = len(parts)

    def body(*refs):
        ins, outs = refs[:n], refs[n:2 * n]
        send_sems, recv_sems = refs[2 * n], refs[2 * n + 1]
        x, y, c = _position()
        sibling = (x, y, 1 - c)
        copies = []
        for a in range(n):
            for k in range(4):
                copies.append(pltpu.make_async_remote_copy(
                    src_ref=ins[a].at[2 * k + (1 - c)], dst_ref=outs[a].at[k],
                    send_sem=send_sems.at[a * 4 + k], recv_sem=recv_sems.at[a * 4 + k],
                    device_id=sibling, device_id_type=MESH))
        for cp in copies:
            cp.start()
        for cp in copies:
            cp.wait_recv()
        for cp in copies:
            cp.wait_send()

    hbm = pl.BlockSpec(memory_space=pl.ANY)
    return pl.pallas_call(
        body, name="exchange_sibling",
        out_shape=tuple(jax.ShapeDtypeStruct((4,) + p.shape[1:], F32) for p in parts),
        in_specs=[hbm] * n, out_specs=tuple([hbm] * n),
        scratch_shapes=[pltpu.SemaphoreType.DMA((4 * n,)), pltpu.SemaphoreType.DMA((4 * n,))],
    )(*parts)


def _exchange_chips(sums):
    n = len(sums)

    def body(*refs):
        ins, outs = refs[:n], refs[n:2 * n]
        send_sems, recv_sems = refs[2 * n], refs[2 * n + 1]
        x, y, c = _position()
        chips = [(1 - x, y), (x, 1 - y), (1 - x, 1 - y)]
        copies = []
        for a in range(n):
            for j, (px, py) in enumerate(chips):
                copies.append(pltpu.make_async_remote_copy(
                    src_ref=ins[a].at[2 * px + py], dst_ref=outs[a].at[j],
                    send_sem=send_sems.at[a * 3 + j], recv_sem=recv_sems.at[a * 3 + j],
                    device_id=(px, py, c), device_id_type=MESH))
        for cp in copies:
            cp.start()
        for cp in copies:
            cp.wait_recv()
        for cp in copies:
            cp.wait_send()

    hbm = pl.BlockSpec(memory_space=pl.ANY)
    return pl.pallas_call(
        body, name="exchange_chips",
        out_shape=tuple(jax.ShapeDtypeStruct((3,) + s.shape[1:], s.dtype) for s in sums),
        in_specs=[hbm] * n, out_specs=tuple([hbm] * n),
        scratch_shapes=[pltpu.SemaphoreType.DMA((3 * n,)), pltpu.SemaphoreType.DMA((3 * n,))],
    )(*sums)


def _chip_sums(part, from_sibling, core, tc, name):
    _, rows, cols = part.shape

    def body(core_ref, p_ref, s_ref, o_ref):
        o_ref[...] = (p_ref[...] + s_ref[...]).astype(BF16)

    return pl.pallas_call(
        body, name=name,
        out_shape=jax.ShapeDtypeStruct((4, rows, cols), BF16),
        grid_spec=pltpu.PrefetchScalarGridSpec(
            num_scalar_prefetch=1, grid=(4, cols // tc),
            in_specs=[pl.BlockSpec((1, rows, tc), lambda k, j, core_ref: (2 * k + core_ref[0], 0, j)),
                      pl.BlockSpec((1, rows, tc), lambda k, j, core_ref: (k, 0, j))],
            out_specs=pl.BlockSpec((1, rows, tc), lambda k, j, core_ref: (k, 0, j))),
        compiler_params=_cparams("arbitrary", "arbitrary"),
    )(core, part, from_sibling)


def _sum_chips(s_ref, r_ref):
    f = lambda a: a.astype(F32)
    return ((f(s_ref[0]) + f(r_ref[0])) + f(r_ref[1])) + f(r_ref[2])


def _final_sum(sums, from_chips, chip, tc, name):
    _, rows, cols = sums.shape

    def body(chip_ref, s_ref, r_ref, g_out):
        g_out[...] = _sum_chips(s_ref, r_ref)

    return pl.pallas_call(
        body, name=name,
        out_shape=jax.ShapeDtypeStruct((rows, cols), F32),
        grid_spec=pltpu.PrefetchScalarGridSpec(
            num_scalar_prefetch=1, grid=(cols // tc,),
            in_specs=[pl.BlockSpec((1, rows, tc), lambda j, chip_ref: (chip_ref[0], 0, j)),
                      pl.BlockSpec((3, rows, tc), lambda j, chip_ref: (0, 0, j))],
            out_specs=pl.BlockSpec((rows, tc), lambda j, chip_ref: (0, j))),
        compiler_params=_cparams("arbitrary"),
    )(chip, sums, from_chips)


def _adamw_rows(g, w, m, v, tr, name):
    rows, cols = g.shape

    def body(g_ref, w_ref, m_ref, v_ref, d_out, m_out, v_out):
        delta, m_new, v_new = _adamw(w_ref[...], g_ref[...], m_ref[...], v_ref[...])
        d_out[...] = delta
        m_out[...] = m_new
        v_out[...] = v_new

    tile = pl.BlockSpec((tr, cols), lambda r: (r, 0))
    shp = jax.ShapeDtypeStruct((rows, cols), F32)
    return pl.pallas_call(
        body, name=name, out_shape=(shp, shp, shp), grid=(rows // tr,),
        in_specs=[tile] * 4, out_specs=(tile, tile, tile),
        compiler_params=_cparams("arbitrary"),
    )(g, w, m, v)


def _adamw(w, g, m, v):
    m = ADAM_B1 * m + (1.0 - ADAM_B1) * g
    v = ADAM_B2 * v + (1.0 - ADAM_B2) * (g * g)
    m_hat = m / (1.0 - ADAM_B1 ** ADAM_STEP)
    v_hat = v / (1.0 - ADAM_B2 ** ADAM_STEP)
    delta = -ADAM_LR * (m_hat / (jnp.sqrt(v_hat) + ADAM_EPS) + ADAM_WD * w)
    return delta, m, v


def _final_sum_adamw(sums, from_chips, chip, w, m, v, tr, name):
    rows, cols = w.shape

    def body(chip_ref, s_ref, r_ref, w_ref, m_ref, v_ref, g_out, d_out, m_out, v_out):
        g = _sum_chips(s_ref, r_ref)
        delta, m_new, v_new = _adamw(w_ref[...], g, m_ref[...], v_ref[...])
        g_out[...] = g
        d_out[...] = delta
        m_out[...] = m_new
        v_out[...] = v_new

    tile = pl.BlockSpec((tr, cols), lambda r, chip_ref: (r, 0))
    shp = jax.ShapeDtypeStruct((rows, cols), F32)
    return pl.pallas_call(
        body, name=name,
        out_shape=(shp, shp, shp, shp),
        grid_spec=pltpu.PrefetchScalarGridSpec(
            num_scalar_prefetch=1, grid=(rows // tr,),
            in_specs=[pl.BlockSpec((1, tr, cols), lambda r, chip_ref: (chip_ref[0], r, 0)),
                      pl.BlockSpec((3, tr, cols), lambda r, chip_ref: (0, r, 0)),
                      tile, tile, tile],
            out_specs=(tile, tile, tile, tile)),
        compiler_params=_cparams("arbitrary"),
    )(chip, sums, from_chips, w, m, v)


def _adamw_small(g, w, m, v):
    def body(g_ref, w_ref, m_ref, v_ref, d_out, m_out, v_out):
        delta, m_new, v_new = _adamw(w_ref[...], g_ref[...], m_ref[...], v_ref[...])
        d_out[...] = delta
        m_out[...] = m_new
        v_out[...] = v_new

    vmem = pl.BlockSpec(memory_space=pltpu.VMEM)
    shp = jax.ShapeDtypeStruct(g.shape, F32)
    return pl.pallas_call(body, name="adamw_small", out_shape=(shp, shp, shp),
                          in_specs=[vmem] * 4, out_specs=(vmem, vmem, vmem))(g, w, m, v)


def _inproj(x2d, norm_g, w_main, w_lr, tm, tn):
    seq = x2d.shape[0]

    def body(x_ref, g_ref, w_ref, wlr_ref, proj_ref, lr_ref, ht_ref, hb_scr):
        @pl.when(pl.program_id(1) == 0)
        def _():
            xv = x_ref[...]
            r = lax.rsqrt(jnp.mean(xv * xv, axis=-1, keepdims=True) + EPS)
            h = (xv * r) * g_ref[...]
            hb = h.astype(BF16)
            hb_scr[...] = hb
            ht_ref[...] = h.T.astype(BF16)
            lr_ref[...] = _dot_nt(hb, wlr_ref[...])

        proj_ref[...] = _dot_nt(hb_scr[...], w_ref[...])

    return pl.pallas_call(
        body, name="inproj",
        out_shape=(jax.ShapeDtypeStruct((seq, MAIN_W), F32), jax.ShapeDtypeStruct((seq, LR_W), F32),
                   jax.ShapeDtypeStruct((D_MODEL, seq), BF16)),
        grid=(seq // tm, MAIN_W // tn),
        in_specs=[pl.BlockSpec((tm, D_MODEL), lambda i, j: (i, 0)),
                  pl.BlockSpec((1, D_MODEL), lambda i, j: (0, 0)),
                  pl.BlockSpec((tn, D_MODEL), lambda i, j: (j, 0)),
                  pl.BlockSpec((LR_W, D_MODEL), lambda i, j: (0, 0))],
        out_specs=(pl.BlockSpec((tm, tn), lambda i, j: (i, j)),
                   pl.BlockSpec((tm, LR_W), lambda i, j: (i, 0)),
                   pl.BlockSpec((D_MODEL, tm), lambda i, j: (0, i))),
        scratch_shapes=[pltpu.VMEM((tm, D_MODEL), BF16)],
        compiler_params=_cparams("arbitrary", "arbitrary"),
    )(x2d, norm_g, w_main, w_lr)


def _block_masks(tt):
    row = lax.broadcasted_iota(jnp.int32, (tt, tt), 0)
    col = lax.broadcasted_iota(jnp.int32, (tt, tt), 1)
    same = jnp.right_shift(row, 6) == jnp.right_shift(col, 6)
    return (jnp.logical_and(same, col <= row), jnp.logical_and(same, col >= row), jnp.logical_and(same, col > row))


def _chunk_column_mask(tt):
    nc = tt // CHUNK
    row = lax.broadcasted_iota(jnp.int32, (tt, nc * DK), 0)
    col = lax.broadcasted_iota(jnp.int32, (tt, nc * DK), 1)
    return jnp.right_shift(row, 6) == jnp.right_shift(col, 7)


def _dot_split3(ones_mat, x):
    x1 = x.astype(BF16)
    r1 = x - x1.astype(F32)
    x2 = r1.astype(BF16)
    x3 = (r1 - x2.astype(F32)).astype(BF16)
    return (_dot(ones_mat, x3) + _dot(ones_mat, x2)) + _dot(ones_mat, x1)


def _log_gate(logits):
    return (jnp.minimum(logits, 0.0) - jnp.log1p(jnp.exp(-jnp.abs(logits)))) * GATE_SCALE


def _chunked(mask, x, nc):
    wide = jnp.concatenate([x] * nc, axis=1)
    return jnp.where(mask, wide, jnp.zeros_like(wide))


def _gla_fwd(proj, lr, wgk_f, wgk_b, bgk_f, bgk_b, tt):
    seq = proj.shape[0]
    nb, nc, nch = seq // tt, tt // CHUNK, seq // CHUNK

    def body(qf, kf, vf, lrf, qb, kb, vb, lrb, wf, wb, bf, bb, of, ob, stf, stb, s_scr, qs_s, ks_s, qin_s, kout_s):
        @pl.when(pl.program_id(0) == 0)
        def _():
            s_scr[...] = jnp.zeros(s_scr.shape, F32)

        low, upp, sup = _block_masks(tt)
        kmask = _chunk_column_mask(tt)
        dirs = ((qf, kf, vf, lrf, wf, bf, of, stf, low, low, REF_F, LAST_F, list(range(nc))),
                (qb, kb, vb, lrb, wb, bb, ob, stb, upp, sup, REF_B, LAST_B, list(reversed(range(nc)))))
        for d, (q_r, k_r, v_r, lr_r, w_r, b_r, o_r, st_r, cum, mask, ref, last, order) in enumerate(dirs):
            logits = _dot(lr_r[...].astype(BF16), w_r[...]) + b_r[...]
            b = _dot_split3(cum.astype(BF16), _log_gate(logits))
            decs = []
            for c in range(nc):
                rows = slice(c * CHUNK, (c + 1) * CHUNK)
                bc = b[rows]
                b_ref, b_last = bc[ref:ref + 1], bc[last:last + 1]
                qc = q_r[rows, :] * QSCALE
                kc = k_r[rows, :]
                qs_s[rows, :] = (qc * jnp.exp(bc - b_ref)).astype(BF16)
                ks_s[rows, :] = (kc * jnp.exp(b_ref - bc)).astype(BF16)
                qin_s[rows, :] = (qc * jnp.exp(bc)).astype(BF16)
                kout_s[rows, :] = (kc * jnp.exp(b_last - bc)).astype(BF16)
                decs.append(jnp.exp(b_last))
            for h in range(HEADS):
                ksl = slice(h * DK, (h + 1) * DK)
                vsl = slice(h * DV, (h + 1) * DV)
                v = v_r[:, vsl].astype(BF16)
                att = jnp.where(mask, _dot_nt(qs_s[:, ksl], ks_s[:, ksl]), 0.0).astype(BF16)
                o_intra = _dot(att, v)
                kv_t = _dot_tn(v, _chunked(kmask, kout_s[:, ksl], nc))
                st = s_scr[d * HEADS + h]
                for c in order:
                    rows = slice(c * CHUNK, (c + 1) * CHUNK)
                    st_r[c, h] = st
                    o_r[rows, vsl] = o_intra[rows] + _dot_nt(qin_s[rows, ksl], st.astype(BF16))
                    st = st * decs[c][:, ksl] + kv_t[:, c * DK:(c + 1) * DK]
                s_scr[d * HEADS + h] = st

    fw = lambda i: (i, 0)
    bw = lambda i: (nb - 1 - i, 0)
    const = lambda i: (0, 0)

    def tok_specs(m):
        return [pl.BlockSpec((tt, QK_W), lambda i: (m(i)[0], OFF_Q // QK_W)),
                pl.BlockSpec((tt, QK_W), lambda i: (m(i)[0], OFF_K // QK_W)),
                pl.BlockSpec((tt, V_W), lambda i: (m(i)[0], OFF_V // V_W)),
                pl.BlockSpec((tt, LR_W), m)]

    st_shape = jax.ShapeDtypeStruct((nch, HEADS, DV, DK), F32)
    o_shape = jax.ShapeDtypeStruct((seq, V_W), F32)
    operand = pltpu.VMEM((tt, QK_W), BF16)
    return pl.pallas_call(
        body, name="gla_fwd",
        out_shape=(o_shape, o_shape, st_shape, st_shape),
        grid=(nb,),
        in_specs=tok_specs(fw) + tok_specs(bw) + [
            pl.BlockSpec((LR_W, QK_W), const), pl.BlockSpec((LR_W, QK_W), const),
            pl.BlockSpec((1, QK_W), const), pl.BlockSpec((1, QK_W), const)],
        out_specs=(pl.BlockSpec((tt, V_W), fw), pl.BlockSpec((tt, V_W), bw),
                   pl.BlockSpec((nc, HEADS, DV, DK), lambda i: (i, 0, 0, 0)),
                   pl.BlockSpec((nc, HEADS, DV, DK), lambda i: (nb - 1 - i, 0, 0, 0))),
        scratch_shapes=[pltpu.VMEM((2 * HEADS, DV, DK), F32), operand, operand, operand, operand],
        compiler_params=_cparams("arbitrary"),
    )(proj, proj, proj, lr, proj, proj, proj, lr, wgk_f, wgk_b, bgk_f, bgk_b)


def _head_norm(o, gain):
    outs, rinv = [], []
    for h in range(HEADS):
        oh = o[:, h * DV:(h + 1) * DV]
        r = lax.rsqrt(jnp.mean(oh * oh, axis=-1, keepdims=True) + EPS)
        outs.append((oh * r) * gain)
        rinv.append(r)
    return jnp.concatenate(outs, axis=1), rinv


def _shift_rows(u, prev_row, next_row):
    n = u.shape[0]
    row = lax.broadcasted_iota(jnp.int32, (n, 1), 0)
    up = jnp.where(row == 0, prev_row, pltpu.roll(u, 1, 0))
    un = jnp.where(row == n - 1, next_row, pltpu.roll(u, n - 1, 0))
    return up, un


def _halo_specs(tm, seq, col_block):
    per = tm // 8
    last = seq // 8 - 1
    return [pl.BlockSpec((8, CONV_W), lambda i: (jnp.maximum(i * per - 1, 0), col_block)),
            pl.BlockSpec((8, CONV_W), lambda i: (jnp.minimum((i + 1) * per, last), col_block))]


def _mix_out_loss(o_f, o_b, proj, x2d, tgt, gla_g, conv_w, conv_b, w_out, final_g, tm):
    seq = x2d.shape[0]
    nt = seq // tm

    def body(of, ob, za, bg, cg, hc, zc, cprev, cnext, hprev, hnext, x_ref, t_ref, gg, cw, cb, wo, fg,
             yt_ref, conv_ref, dx2_ref, dx2b_ref, loss_ref, dfg_ref):
        i = pl.program_id(0)

        @pl.when(i == 0)
        def _():
            loss_ref[...] = jnp.zeros(loss_ref.shape, F32)
            dfg_ref[...] = jnp.zeros(dfg_ref.shape, F32)

        on, _ = _head_norm(of[...] + ob[...], gg[...])
        zav = za[...]
        y_a = on * (zav * _sigmoid(zav))
        u = cg[...] * hc[...]
        prev_row = jnp.where(i > 0, cprev[7:8, :] * hprev[7:8, :], 0.0)
        next_row = jnp.where(i < nt - 1, cnext[0:1, :] * hnext[0:1, :], 0.0)
        up, un = _shift_rows(u, prev_row, next_row)
        conv = (cw[0:1, :] * up + cw[1:2, :] * u + cw[2:3, :] * un) + cb[...]
        conv_ref[...] = conv
        zcv = zc[...]
        y_c = bg[...] * conv * (zcv * _sigmoid(zcv))
        y = jnp.concatenate([y_a, y_c], axis=1)
        yt_ref[...] = y.T.astype(BF16)
        x2 = x_ref[...] + _dot(y.astype(BF16), wo[...])
        r = lax.rsqrt(jnp.mean(x2 * x2, axis=-1, keepdims=True) + EPS)
        xn = x2 * r
        err = xn * fg[...] - t_ref[...]
        loss_ref[...] += 0.5 * jnp.sum(jnp.mean(err * err, axis=-1, keepdims=True))
        dyf = err * (1.0 / D_MODEL)
        dfg_ref[...] += jnp.sum(dyf * xn, axis=0, keepdims=True)
        dxn = dyf * fg[...]
        dx2 = r * dxn - xn * (r * jnp.mean(dxn * xn, axis=-1, keepdims=True))
        dx2_ref[...] = dx2
        dx2b_ref[...] = dx2.astype(BF16)

    def col(off):
        return pl.BlockSpec((tm, CONV_W), lambda i: (i, off // CONV_W))

    rowt = pl.BlockSpec((tm, D_MODEL), lambda i: (i, 0))
    const = lambda shape: pl.BlockSpec(shape, lambda i: (0, 0))
    return pl.pallas_call(
        body, name="mix_out_loss",
        out_shape=(jax.ShapeDtypeStruct((MIX_W, seq), BF16), jax.ShapeDtypeStruct((seq, CONV_W), F32),
                   jax.ShapeDtypeStruct((seq, D_MODEL), F32), jax.ShapeDtypeStruct((seq, D_MODEL), BF16),
                   jax.ShapeDtypeStruct((8, 128), F32), jax.ShapeDtypeStruct((1, D_MODEL), F32)),
        grid=(nt,),
        in_specs=[rowt, rowt, col(OFF_ZA), col(OFF_B), col(OFF_C), col(OFF_H), col(OFF_ZC)]
        + _halo_specs(tm, seq, OFF_C // CONV_W) + _halo_specs(tm, seq, OFF_H // CONV_W)
        + [rowt, rowt, const((1, DV)), const((8, CONV_W)), const((1, CONV_W)), const((MIX_W, D_MODEL)),
           const((1, D_MODEL))],
        out_specs=(pl.BlockSpec((MIX_W, tm), lambda i: (0, i)), rowt, rowt, rowt, const((8, 128)),
                   const((1, D_MODEL))),
        compiler_params=_cparams("arbitrary"),
    )(o_f, o_b, proj, proj, proj, proj, proj, proj, proj, proj, proj, x2d, tgt, gla_g, conv_w, conv_b, w_out, final_g)


def _dsilu(z, s):
    return s * (1.0 + z * (1.0 - s))


def _mix_bwd(dx2b, o_f, o_b, proj, conv, gla_g, w_out, tm):
    seq = dx2b.shape[0]

    def body(dx, of, ob, za, bg, zc, cv, gg, wo, dg_ref, do_ref, dconv_ref, dgg_ref, dcb_ref):
        @pl.when(pl.program_id(0) == 0)
        def _():
            dgg_ref[...] = jnp.zeros(dgg_ref.shape, F32)
            dcb_ref[...] = jnp.zeros(dcb_ref.shape, F32)

        dy = _dot_nt(dx[...], wo[...])
        dy_a, dy_c = dy[:, :V_W], dy[:, V_W:]
        zcv, bgv, convv = zc[...], bg[...], cv[...]
        sc = _sigmoid(zcv)
        szc = zcv * sc
        dg_ref[:, CONV_W:2 * CONV_W] = (dy_c * convv * szc).astype(BF16)
        dconv = dy_c * bgv * szc
        dconv_ref[...] = dconv
        dcb_ref[...] += jnp.sum(dconv, axis=0, keepdims=True)
        dg_ref[:, 2 * CONV_W:] = (dy_c * bgv * convv * _dsilu(zcv, sc)).astype(BF16)

        o = of[...] + ob[...]
        gain = gg[...]
        on, rinv = _head_norm(o, gain)
        zav = za[...]
        sa = _sigmoid(zav)
        dg_ref[:, :CONV_W] = (dy_a * on * _dsilu(zav, sa)).astype(BF16)
        don = dy_a * (zav * sa)
        dgg = jnp.zeros((1, DV), F32)
        dos = []
        for h in range(HEADS):
            sl = slice(h * DV, (h + 1) * DV)
            oh, r, dh = o[:, sl], rinv[h], don[:, sl]
            ohn = oh * r
            dgg = dgg + jnp.sum(dh * ohn, axis=0, keepdims=True)
            dn = dh * gain
            dos.append(r * dn - ohn * (r * jnp.mean(dn * ohn, axis=-1, keepdims=True)))
        dgg_ref[...] += dgg
        do_ref[...] = jnp.concatenate(dos, axis=1)

    def col(off):
        return pl.BlockSpec((tm, CONV_W), lambda i: (i, off // CONV_W))

    rowt = pl.BlockSpec((tm, D_MODEL), lambda i: (i, 0))
    const = lambda shape: pl.BlockSpec(shape, lambda i: (0, 0))
    return pl.pallas_call(
        body, name="mix_bwd",
        out_shape=(jax.ShapeDtypeStruct((seq, GATES_W), BF16), jax.ShapeDtypeStruct((seq, V_W), F32),
                   jax.ShapeDtypeStruct((seq, CONV_W), F32),
                   jax.ShapeDtypeStruct((1, DV), F32), jax.ShapeDtypeStruct((1, CONV_W), F32)),
        grid=(seq // tm,),
        in_specs=[rowt, rowt, rowt, col(OFF_ZA), col(OFF_B), col(OFF_ZC), rowt, const((1, DV)),
                  const((MIX_W, D_MODEL))],
        out_specs=(pl.BlockSpec((tm, GATES_W), lambda i: (i, 0)), rowt, rowt, const((1, DV)), const((1, CONV_W))),
        compiler_params=_cparams("arbitrary"),
    )(dx2b, o_f, o_b, proj, proj, proj, conv, gla_g, w_out)


def _conv_bwd(dconv, proj, conv_w, tm):
    seq = dconv.shape[0]
    nt = seq // tm

    def body(dc_in, dprev, dnext, cg, hc, cprev, cnext, hprev, hnext, cw, dch_ref, dcw_ref):
        i = pl.program_id(0)

        @pl.when(i == 0)
        def _():
            dcw_ref[...] = jnp.zeros(dcw_ref.shape, F32)

        first, lastt = i > 0, i < nt - 1
        dcv = dc_in[...]
        d_up, d_un = _shift_rows(dcv, jnp.where(first, dprev[7:8, :], 0.0), jnp.where(lastt, dnext[0:1, :], 0.0))
        cgv, hcv = cg[...], hc[...]
        u = cgv * hcv
        u_up, u_un = _shift_rows(u, jnp.where(first, cprev[7:8, :] * hprev[7:8, :], 0.0),
                                 jnp.where(lastt, cnext[0:1, :] * hnext[0:1, :], 0.0))
        du = cw[0:1, :] * d_un + cw[1:2, :] * dcv + cw[2:3, :] * d_up
        dch_ref[:, :CONV_W] = (du * hcv).astype(BF16)
        dch_ref[:, CONV_W:] = (du * cgv).astype(BF16)
        dcw_ref[0:1, :] += jnp.sum(dcv * u_up, axis=0, keepdims=True)
        dcw_ref[1:2, :] += jnp.sum(dcv * u, axis=0, keepdims=True)
        dcw_ref[2:3, :] += jnp.sum(dcv * u_un, axis=0, keepdims=True)

    def col(off):
        return pl.BlockSpec((tm, CONV_W), lambda i: (i, off // CONV_W))

    rowt = pl.BlockSpec((tm, CONV_W), lambda i: (i, 0))
    const = lambda shape: pl.BlockSpec(shape, lambda i: (0, 0))
    return pl.pallas_call(
        body, name="conv_bwd",
        out_shape=(jax.ShapeDtypeStruct((seq, CH_W), BF16), jax.ShapeDtypeStruct((8, CONV_W), F32)),
        grid=(nt,),
        in_specs=[rowt] + _halo_specs(tm, seq, 0) + [col(OFF_C), col(OFF_H)]
        + _halo_specs(tm, seq, OFF_C // CONV_W) + _halo_specs(tm, seq, OFF_H // CONV_W) + [const((8, CONV_W))],
        out_specs=(pl.BlockSpec((tm, CH_W), lambda i: (i, 0)), const((8, CONV_W))),
        compiler_params=_cparams("arbitrary"),
    )(dconv, dconv, dconv, proj, proj, proj, proj, proj, proj, conv_w)


def _gla_bwd(proj, lr, do, st_f, st_b, wgk_f, wgk_b, bgk_f, bgk_b, tt):
    seq = proj.shape[0]
    nb, nc = seq // tt, tt // CHUNK

    def body(qf, kf, vf, lrf, dof, stf, qb, kb, vb, lrb, dob, stb, wf, wb, bf, bb,
             dqkv_f, dlr_f, dqkv_b, dlr_b, dwf, dwb, dbf, dbb,
             ds_scr, eq_s, ek_s, ein_s, eout_s, qs_s, ks_s, qin_s, kout_s, db_s, lg_s):
        @pl.when(pl.program_id(0) == 0)
        def _():
            ds_scr[...] = jnp.zeros(ds_scr.shape, F32)
            for r in (dwf, dwb, dbf, dbb):
                r[...] = jnp.zeros(r.shape, F32)

        low, upp, sup = _block_masks(tt)
        kmask = _chunk_column_mask(tt)
        row = lax.broadcasted_iota(jnp.int32, (CHUNK, 1), 0)
        dirs = ((qf, kf, vf, lrf, dof, stf, wf, bf, dqkv_f, dlr_f, dwf, dbf,
                 low, upp, low, REF_F, LAST_F, list(reversed(range(nc)))),
                (qb, kb, vb, lrb, dob, stb, wb, bb, dqkv_b, dlr_b, dwb, dbb,
                 upp, low, sup, REF_B, LAST_B, list(range(nc))))
        for d, (q_r, k_r, v_r, lr_r, do_r, st_r, w_r, b_r, dqkv_r, dlr_r, dw_r, db_r,
                cum, cum_t, mask, ref, last, order) in enumerate(dirs):
            lrv = lr_r[...].astype(BF16)
            wv = w_r[...]
            logits = _dot(lrv, wv) + b_r[...]
            lg_s[...] = logits
            b = _dot_split3(cum.astype(BF16), _log_gate(logits))
            decs = []
            for c in range(nc):
                rows = slice(c * CHUNK, (c + 1) * CHUNK)
                bc = b[rows]
                b_ref, b_last = bc[ref:ref + 1], bc[last:last + 1]
                qc = q_r[rows, :] * QSCALE
                kc = k_r[rows, :]
                e_q, e_k, e_in, e_out = jnp.exp(bc - b_ref), jnp.exp(b_ref - bc), jnp.exp(bc), jnp.exp(b_last - bc)
                eq_s[rows, :], ek_s[rows, :], ein_s[rows, :], eout_s[rows, :] = e_q, e_k, e_in, e_out
                qs_s[rows, :] = (qc * e_q).astype(BF16)
                ks_s[rows, :] = (kc * e_k).astype(BF16)
                qin_s[rows, :] = (qc * e_in).astype(BF16)
                kout_s[rows, :] = (kc * e_out).astype(BF16)
                decs.append(jnp.exp(b_last))
            for h in range(HEADS):
                ksl = slice(h * DK, (h + 1) * DK)
                vsl = slice(h * DV, (h + 1) * DV)
                v = v_r[:, vsl].astype(BF16)
                dov = do_r[:, vsl].astype(BF16)
                qsb, ksb = qs_s[:, ksl], ks_s[:, ksl]
                att = jnp.where(mask, _dot_nt(qsb, ksb), 0.0).astype(BF16)
                datt = jnp.where(mask, _dot_nt(dov, v), 0.0).astype(BF16)
                dqs = _dot(datt, ksb)
                dks = _dot_tn(datt, qsb)
                dv_intra = _dot_tn(att, dov)
                g_t = _dot_tn(dov, _chunked(kmask, qin_s[:, ksl], nc))
                ds = ds_scr[d * HEADS + h]
                for c in order:
                    rows = slice(c * CHUNK, (c + 1) * CHUNK)
                    dsb = ds.astype(BF16)
                    s_prev = st_r[c, h]
                    dk_out = _dot(v[rows], dsb)
                    dq_in = _dot(dov[rows], s_prev.astype(BF16))
                    dqkv_r[rows, OFF_V + h * DV:OFF_V + (h + 1) * DV] = dv_intra[rows] + _dot_nt(kout_s[rows, ksl], dsb)
                    dec = decs[c][:, ksl]
                    ddec = jnp.sum(ds * s_prev, axis=0, keepdims=True)
                    e_q, e_k, e_in, e_out = eq_s[rows, ksl], ek_s[rows, ksl], ein_s[rows, ksl], eout_s[rows, ksl]
                    qc = q_r[rows, ksl] * QSCALE
                    kc = k_r[rows, ksl]
                    dqs_c, dks_c = dqs[rows], dks[rows]
                    dqkv_r[rows, OFF_Q + h * DK:OFF_Q + (h + 1) * DK] = (dqs_c * e_q + dq_in * e_in) * QSCALE
                    dqkv_r[rows, OFF_K + h * DK:OFF_K + (h + 1) * DK] = dks_c * e_k + dk_out * e_out
                    kk = dk_out * (kc * e_out)
                    db = dqs_c * (qc * e_q) - dks_c * (kc * e_k) + dq_in * (qc * e_in) - kk
                    tail = jnp.sum(kk, axis=0, keepdims=True) + ddec * dec
                    db_s[rows, ksl] = db + jnp.where(row == last, tail, 0.0)
                    ds = ds * dec + g_t[:, c * DK:(c + 1) * DK]
                ds_scr[d * HEADS + h] = ds
            dg = _dot_split3(cum_t.astype(BF16), db_s[...])
            dlogit = (dg * GATE_SCALE) * _sigmoid(-lg_s[...])
            dlb = dlogit.astype(BF16)
            dlr_r[...] = _dot_nt(dlb, wv)
            dw_r[...] += _dot_tn(lrv, dlb)
            db_r[...] += jnp.sum(dlogit, axis=0, keepdims=True)

    fw = lambda i: (nb - 1 - i, 0)
    bw = lambda i: (i, 0)
    const = lambda i: (0, 0)

    def tok_specs(m):
        return [pl.BlockSpec((tt, QK_W), lambda i: (m(i)[0], OFF_Q // QK_W)),
                pl.BlockSpec((tt, QK_W), lambda i: (m(i)[0], OFF_K // QK_W)),
                pl.BlockSpec((tt, V_W), lambda i: (m(i)[0], OFF_V // V_W)),
                pl.BlockSpec((tt, LR_W), m),
                pl.BlockSpec((tt, V_W), m),
                pl.BlockSpec((nc, HEADS, DV, DK), lambda i: (m(i)[0], 0, 0, 0))]

    dqkv = jax.ShapeDtypeStruct((seq, QK_W + QK_W + V_W), F32)
    dlr = jax.ShapeDtypeStruct((seq, LR_W), F32)
    dw = jax.ShapeDtypeStruct((LR_W, QK_W), F32)
    dbias = jax.ShapeDtypeStruct((1, QK_W), F32)
    return pl.pallas_call(
        body, name="gla_bwd",
        out_shape=(dqkv, dlr, dqkv, dlr, dw, dw, dbias, dbias),
        grid=(nb,),
        in_specs=tok_specs(fw) + tok_specs(bw) + [
            pl.BlockSpec((LR_W, QK_W), const), pl.BlockSpec((LR_W, QK_W), const),
            pl.BlockSpec((1, QK_W), const), pl.BlockSpec((1, QK_W), const)],
        out_specs=(pl.BlockSpec((tt, QK_W + QK_W + V_W), fw), pl.BlockSpec((tt, LR_W), fw),
                   pl.BlockSpec((tt, QK_W + QK_W + V_W), bw), pl.BlockSpec((tt, LR_W), bw),
                   pl.BlockSpec((LR_W, QK_W), const), pl.BlockSpec((LR_W, QK_W), const),
                   pl.BlockSpec((1, QK_W), const), pl.BlockSpec((1, QK_W), const)),
        scratch_shapes=[pltpu.VMEM((2 * HEADS, DV, DK), F32)] + [pltpu.VMEM((tt, QK_W), F32)] * 4
        + [pltpu.VMEM((tt, QK_W), BF16)] * 4 + [pltpu.VMEM((tt, QK_W), F32)] * 2,
        compiler_params=_cparams("arbitrary"),
    )(proj, proj, proj, lr, do, st_f, proj, proj, proj, lr, do, st_b, wgk_f, wgk_b, bgk_f, bgk_b)


def _sum_directions(dqkv_f, dqkv_b, dlr_f, dlr_b, tm):
    seq = dqkv_f.shape[0]

    def body(a, b, la, lb, dp_out, dlr_out):
        dp_out[...] = (a[...] + b[...]).astype(BF16)
        dlr_out[...] = (la[...] + lb[...]).astype(BF16)

    rowt = pl.BlockSpec((tm, QKV_W), lambda i: (i, 0))
    lrt = pl.BlockSpec((tm, LR_W), lambda i: (i, 0))
    return pl.pallas_call(
        body, name="sum_directions",
        out_shape=(jax.ShapeDtypeStruct((seq, QKV_W), BF16), jax.ShapeDtypeStruct((seq, LR_W), BF16)),
        grid=(seq // tm,),
        in_specs=[rowt, rowt, lrt, lrt],
        out_specs=(rowt, lrt),
        compiler_params=_cparams("arbitrary"),
    )(dqkv_f, dqkv_b, dlr_f, dlr_b)


def _input_grad(dp_qkv, dp_gates, dp_ch, dlr, w_main, w_lr, x2d, norm_g, dx2, tm):
    seq = x2d.shape[0]
    tk = 1024
    k_gates, k_ch, nk = QKV_W // tk, (QKV_W + GATES_W) // tk, MAIN_W // tk

    def body(dq, dg, dc, dl, w, wl, x_ref, g_ref, dx2_ref, gx_ref, dng_ref, acc):
        i, k = pl.program_id(0), pl.program_id(1)

        @pl.when(jnp.logical_and(i == 0, k == 0))
        def _():
            dng_ref[...] = jnp.zeros(dng_ref.shape, F32)

        @pl.when(k == 0)
        def _():
            acc[...] = _dot(dl[...], wl[...])

        @pl.when(k < k_gates)
        def _():
            acc[...] += _dot(dq[...], w[...])

        @pl.when(jnp.logical_and(k >= k_gates, k < k_ch))
        def _():
            acc[...] += _dot(dg[...], w[...])

        @pl.when(k >= k_ch)
        def _():
            acc[...] += _dot(dc[...], w[...])

        @pl.when(k == nk - 1)
        def _():
            dh = acc[...]
            xv = x_ref[...]
            r = lax.rsqrt(jnp.mean(xv * xv, axis=-1, keepdims=True) + EPS)
            xn = xv * r
            dng_ref[...] += jnp.sum(dh * xn, axis=0, keepdims=True)
            dn = dh * g_ref[...]
            gx_ref[...] = (r * dn - xn * (r * jnp.mean(dn * xn, axis=-1, keepdims=True))) + dx2_ref[...]

    rowt = pl.BlockSpec((tm, D_MODEL), lambda i, k: (i, 0))
    return pl.pallas_call(
        body, name="input_grad",
        out_shape=(jax.ShapeDtypeStruct((seq, D_MODEL), F32), jax.ShapeDtypeStruct((1, D_MODEL), F32)),
        grid=(seq // tm, nk),
        in_specs=[pl.BlockSpec((tm, tk), lambda i, k: (i, jnp.minimum(k, k_gates - 1))),
                  pl.BlockSpec((tm, tk), lambda i, k: (i, jnp.clip(k - k_gates, 0, k_ch - k_gates - 1))),
                  pl.BlockSpec((tm, tk), lambda i, k: (i, jnp.clip(k - k_ch, 0, nk - k_ch - 1))),
                  pl.BlockSpec((tm, LR_W), lambda i, k: (i, 0)),
                  pl.BlockSpec((tk, D_MODEL), lambda i, k: (k, 0)),
                  pl.BlockSpec((LR_W, D_MODEL), lambda i, k: (0, 0)),
                  rowt, pl.BlockSpec((1, D_MODEL), lambda i, k: (0, 0)), rowt],
        out_specs=(rowt, pl.BlockSpec((1, D_MODEL), lambda i, k: (0, 0))),
        scratch_shapes=[pltpu.VMEM((tm, D_MODEL), F32)],
        compiler_params=_cparams("arbitrary", "arbitrary"),
    )(dp_qkv, dp_gates, dp_ch, dlr, w_main, w_lr, x2d, norm_g, dx2)


def _weight_grad(at, b, tn, tk, name):
    m, seq = at.shape
    n = b.shape[1]

    def body(a_ref, b_ref, o_ref):
        @pl.when(pl.program_id(1) == 0)
        def _():
            o_ref[...] = jnp.zeros(o_ref.shape, F32)

        o_ref[...] += _dot(a_ref[...], b_ref[...])

    return pl.pallas_call(
        body, name=name,
        out_shape=jax.ShapeDtypeStruct((m, n), F32),
        grid=(n // tn, seq // tk),
        in_specs=[pl.BlockSpec((m, tk), lambda j, k: (0, k)), pl.BlockSpec((tk, tn), lambda j, k: (k, j))],
        out_specs=pl.BlockSpec((m, tn), lambda j, k: (0, j)),
        compiler_params=_cparams("arbitrary", "arbitrary"),
    )(at, b)


def _weight_grad_t(at, b, tn, tk, name):
    m, seq = at.shape
    n = b.shape[1]
    nk = seq // tk

    def body(a_ref, b_ref, o_ref, acc):
        k = pl.program_id(1)

        @pl.when(k == 0)
        def _():
            acc[...] = jnp.zeros(acc.shape, F32)

        acc[...] += _dot(a_ref[...], b_ref[...])

        @pl.when(k == nk - 1)
        def _():
            o_ref[...] = acc[...].T

    return pl.pallas_call(
        body, name=name,
        out_shape=jax.ShapeDtypeStruct((n, m), F32),
        grid=(n // tn, nk),
        in_specs=[pl.BlockSpec((m, tk), lambda j, k: (0, k)), pl.BlockSpec((tk, tn), lambda j, k: (k, j))],
        out_specs=pl.BlockSpec((tn, m), lambda j, k: (j, 0)),
        scratch_shapes=[pltpu.VMEM((m, tn), F32)],
        compiler_params=_cparams("arbitrary", "arbitrary"),
    )(at, b)


def _pad_rows(a, rows):
    return jnp.pad(a, ((0, rows - a.shape[0]), (0, 0)))


def _rows128(a):
    a = a.reshape(-1, 128)
    return _pad_rows(a, -(-a.shape[0] // 8) * 8)


def _pack(arrs):
    return jnp.concatenate([_rows128(a) for a in arrs], axis=0)


def _unpack(buf, like):
    out, start = [], 0
    for a in like:
        rows = a.size // 128
        out.append(buf[start:start + rows].reshape(a.shape))
        start += -(-rows // 8) * 8
    return out


def kernel(x, norm_g, w_in, w_gk_f, b_gk_f, w_gk_b, b_gk_b, gla_norm_g, conv_w, conv_b, w_out, final_g, loss_target, m_norm_g, m_w_in, m_w_gk_f, m_b_gk_f, m_w_gk_b, m_b_gk_b, m_gla_norm_g, m_conv_w, m_conv_b, m_w_out, m_final_g, v_norm_g, v_w_in, v_w_gk_f, v_b_gk_f, v_w_gk_b, v_b_gk_b, v_gla_norm_g, v_conv_w, v_conv_b, v_w_out, v_final_g):
    px, py, pc = _position()
    me = _blk(px, py, pc)
    seq = x.shape[1]
    x2d, tgt = x[0], loss_target[0]
    tm = min(512, seq)
    tt = min(256, seq)

    small_s = jnp.concatenate([jnp.concatenate([w_gk_f[0], w_gk_b[0]], axis=1), _pad_rows(conv_w[0], 8)], axis=0)
    win_all, wout_all, small_all = _allgather_weights(w_in[0].T, w_out[0], small_s)
    w_nat = win_all.reshape(IN_W, D_MODEL)
    w_main = jnp.concatenate([w_nat[:NAT_LR], w_nat[NAT_B:NAT_C], w_nat[NAT_ZC:], w_nat[NAT_C:NAT_ZC]], axis=0)
    w_lr = w_nat[NAT_LR:NAT_LR + LR_W]
    w_out_full = wout_all.reshape(MIX_W, D_MODEL)
    wgk_cols = 512 // N_DEV
    wgk_f_full = small_all[:, 0:RANK, 0:wgk_cols].transpose(1, 0, 2).reshape(RANK, QK_W)
    wgk_b_full = small_all[:, 0:RANK, wgk_cols:2 * wgk_cols].transpose(1, 0, 2).reshape(RANK, QK_W)
    conv_w_full = _pad_rows(small_all[:, RANK:RANK + 3, :].transpose(1, 0, 2).reshape(3, CONV_W), 8)
    zr = lambda n: jnp.zeros((n, QK_W), F32)
    wgk_f_pad = jnp.concatenate([wgk_f_full, zr(LR_W - RANK)], axis=0).astype(BF16)
    wgk_b_pad = jnp.concatenate([zr(RANK), wgk_b_full, zr(LR_W - 2 * RANK)], axis=0).astype(BF16)

    proj, lr, h_t = _inproj(x2d, norm_g, w_main, w_lr, tm, 1024)
    o_f, o_b, st_f, st_b = _gla_fwd(proj, lr, wgk_f_pad, wgk_b_pad, b_gk_f, b_gk_b, tt)
    tmix = min(256, seq)
    y_t, conv, dx2, dx2b, loss_p, dfg_p = _mix_out_loss(o_f, o_b, proj, x2d, tgt, gla_norm_g, conv_w_full, conv_b,
                                                        w_out_full, final_g.reshape(1, D_MODEL), tmix)

    dp_gates, do, dconv, dgg_p, dcb_p = _mix_bwd(dx2b, o_f, o_b, proj, conv, gla_norm_g, w_out_full, tmix)
    dp_ch, dcw_p = _conv_bwd(dconv, proj, conv_w_full, tmix)
    dqkv_f, dlr_f, dqkv_b, dlr_b, dwf_p, dwb_p, dbf_p, dbb_p = _gla_bwd(
        proj, lr, do, st_f, st_b, wgk_f_pad, wgk_b_pad, b_gk_f, b_gk_b, tt)
    dp_qkv, dlr = _sum_directions(dqkv_f, dqkv_b, dlr_f, dlr_b, tm)
    grad_x2d, dng_p = _input_grad(dp_qkv, dp_gates, dp_ch, dlr, w_main, w_lr, x2d, norm_g, dx2, tm)
    dw_qkv = _weight_grad_t(h_t, dp_qkv, 1024, tm, "wgrad_qkv")
    dw_gates = _weight_grad_t(h_t, dp_gates, 1024, tm, "wgrad_gates")
    dw_ch = _weight_grad_t(h_t, dp_ch, 1024, tm, "wgrad_ch")
    dw_lr = _weight_grad_t(h_t, dlr, LR_W, tm, "wgrad_lr")
    dw_out = _weight_grad(y_t, dx2b, D_MODEL, tm, "wgrad_out")

    dw_nat = jnp.concatenate([dw_qkv, dw_gates[:CONV_W], dw_lr[:2 * RANK], dw_gates[CONV_W:2 * CONV_W], dw_ch,
                              dw_gates[2 * CONV_W:]], axis=0)
    part_in = dw_nat.reshape(N_DEV, SHARD_W, D_MODEL)
    part_out = dw_out.reshape(N_DEV, MIX_W // N_DEV, D_MODEL)
    sib_in, sib_out = _exchange_sibling([part_in, part_out])
    core = jnp.reshape(pc, (1,)).astype(jnp.int32)
    chip = jnp.reshape(2 * px + py, (1,)).astype(jnp.int32)
    sums_in = _chip_sums(part_in, sib_in, core, 256, "chip_sums_in")
    sums_out = _chip_sums(part_out, sib_out, core, 256, "chip_sums_out")
    far_in, far_out = _exchange_chips([sums_in, sums_out])
    g_in_t = _final_sum(sums_in, far_in, chip, 256, "final_sum_in")
    g_w_out, d_w_out, nm_w_out, nv_w_out = _final_sum_adamw(sums_out, far_out, chip, w_out[0], m_w_out[0], v_w_out[0],
                                                            256, "adamw_out")
    flat = lambda a: a[0].T.reshape(SHARD_W * D_MODEL // 128, 128)
    unflat = lambda a: a.reshape(SHARD_W, D_MODEL).T
    d_flat, m_flat, v_flat = _adamw_rows(g_in_t.reshape(SHARD_W * D_MODEL // 128, 128), flat(w_in), flat(m_w_in),
                                         flat(v_w_in), 720, "adamw_in")
    g_w_in, d_w_in, nm_w_in, nv_w_in = g_in_t.T, unflat(d_flat), unflat(m_flat), unflat(v_flat)

    pieces = [dng_p, dbf_p, dbb_p, dgg_p, dcb_p, dfg_p[0], dwf_p[0:RANK], dwb_p[RANK:2 * RANK], dcw_p[0:3], loss_p[0]]
    tot = _unpack(_allreduce_small(_pack(pieces)), pieces)
    g_norm_g, g_b_gk_f, g_b_gk_b, g_gla, g_conv_b, g_final = tot[:6]
    g_wgk_f = lax.dynamic_slice_in_dim(tot[6], me * wgk_cols, wgk_cols, axis=1)[None]
    g_wgk_b = lax.dynamic_slice_in_dim(tot[7], me * wgk_cols, wgk_cols, axis=1)[None]
    g_conv_w = lax.dynamic_slice_in_dim(tot[8], me * 128, 128, axis=1)[None]
    loss = tot[9][0]

    small_g = [g_norm_g, g_b_gk_f, g_b_gk_b, g_gla, g_conv_b, g_final, g_wgk_f, g_wgk_b, g_conv_w]
    small_w = [norm_g, b_gk_f, b_gk_b, gla_norm_g, conv_b, final_g, w_gk_f, w_gk_b, conv_w]
    small_m = [m_norm_g, m_b_gk_f, m_b_gk_b, m_gla_norm_g, m_conv_b, m_final_g, m_w_gk_f, m_w_gk_b, m_conv_w]
    small_v = [v_norm_g, v_b_gk_f, v_b_gk_b, v_gla_norm_g, v_conv_b, v_final_g, v_w_gk_f, v_w_gk_b, v_conv_w]
    d_s, m_s, v_s = _adamw_small(_pack(small_g), _pack(small_w), _pack(small_m), _pack(small_v))
    d_l, m_l, v_l = _unpack(d_s, small_w), _unpack(m_s, small_w), _unpack(v_s, small_w)

    def ordered(sm, big_in, big_out):
        return [sm[0], big_in[None], sm[6], sm[1], sm[7], sm[2], sm[3], sm[8], sm[4], big_out[None], sm[5]]

    grads = ordered(small_g, g_w_in, g_w_out)
    deltas = ordered(d_l, d_w_in, d_w_out)
    new_m = ordered(m_l, nm_w_in, nm_w_out)
    new_v = ordered(v_l, nv_w_in, nv_w_out)
    return (loss, grad_x2d[None], *grads, *deltas, *new_m, *new_v)
```

```python
import jax
import jax.numpy as jnp
from jax import lax
from jax.experimental import pallas as pl
from jax.experimental.pallas import tpu as pltpu

F32 = jnp.float32
BF16 = jnp.bfloat16
MESH = pl.DeviceIdType.MESH

N_DEV = 8
D_MODEL = 1024
HEADS = 4
DK = 128
DV = 256
QK_W = HEADS * DK
V_W = HEADS * DV
CONV_W = 1024
MIX_W = V_W + CONV_W
CHUNK = 64
RANK = 16
IN_W = 7200
SHARD_W = IN_W // N_DEV
MAIN_W = 7168
LR_W = 128
OFF_Q, OFF_K, OFF_V, OFF_ZA, OFF_B, OFF_ZC, OFF_C, OFF_H = 0, 512, 1024, 2048, 3072, 4096, 5120, 6144
QKV_W, GATES_W, CH_W = 2048, 3072, 2048
NAT_ZA, NAT_LR, NAT_B, NAT_C, NAT_ZC = 2048, 3072, 3104, 4128, 6176
EPS = 1e-6
GATE_SCALE = 1.0 / 16.0
QSCALE = DK ** -0.5
REF_F, LAST_F = CHUNK // 2, CHUNK - 1
REF_B, LAST_B = CHUNK - 1 - CHUNK // 2, 0

ADAM_LR = 0.001
ADAM_B1 = 0.9
ADAM_B2 = 0.999
ADAM_EPS = 1e-08
ADAM_WD = 0.01
ADAM_STEP = 10

VMEM_LIMIT = 56 * 1024 * 1024


def _cparams(*sem):
    return pltpu.CompilerParams(dimension_semantics=sem, vmem_limit_bytes=VMEM_LIMIT)


def _dot(a, b):
    return jnp.dot(a, b, preferred_element_type=F32)


def _dot_nt(a, b):
    return lax.dot_general(a, b, (((1,), (1,)), ((), ())), preferred_element_type=F32)


def _dot_tn(a, b):
    return lax.dot_general(a, b, (((0,), (0,)), ((), ())), preferred_element_type=F32)


def _sigmoid(z):
    return jax.nn.sigmoid(z)


def _position():
    return lax.axis_index("x"), lax.axis_index("y"), lax.axis_index("c")


def _blk(px, py, pc):
    return 4 * px + 2 * py + pc


def _two_level_gather(outs, send_sems, recv_sems):
    x, y, c = _position()
    me, sibling = (x, y, c), (x, y, 1 - c)
    chips = [(1 - x, y), (x, 1 - y), (1 - x, 1 - y)]
    n = len(outs)

    def copy(a, k, block, to):
        ref = outs[a].at[_blk(*block)]
        return pltpu.make_async_remote_copy(src_ref=ref, dst_ref=ref, send_sem=send_sems.at[a * 7 + k],
                                            recv_sem=recv_sems.at[a * 7 + k], device_id=to, device_id_type=MESH)

    first = []
    for a in range(n):
        first.append(copy(a, 0, me, sibling))
        first += [copy(a, 1 + j, me, (*chip, c)) for j, chip in enumerate(chips)]
    for cp in first:
        cp.start()
    passed = []
    for j, chip in enumerate(chips):
        for a in range(n):
            copy(a, 1 + j, (*chip, c), me).wait_recv()
            fwd = copy(a, 4 + j, (*chip, c), sibling)
            fwd.start()
            passed.append(fwd)
    for a in range(n):
        copy(a, 0, sibling, me).wait_recv()
    for j, chip in enumerate(chips):
        for a in range(n):
            copy(a, 4 + j, (*chip, 1 - c), me).wait_recv()
    for cp in first + passed:
        cp.wait_send()


def _allgather_weights(w_in_s, w_out_s, small_s):
    def body(win_ref, wout_ref, sm_ref, win_all, wout_all, sm_all, send_sems, recv_sems):
        mine = _blk(*_position())
        win_all[mine] = win_ref[...].astype(BF16)
        wout_all[mine] = wout_ref[...].astype(BF16)
        sm_all[mine] = sm_ref[...]
        _two_level_gather((win_all, wout_all, sm_all), send_sems, recv_sems)

    vmem = pl.BlockSpec(memory_space=pltpu.VMEM)
    return pl.pallas_call(
        body, name="allgather_weights",
        out_shape=(jax.ShapeDtypeStruct((N_DEV,) + w_in_s.shape, BF16),
                   jax.ShapeDtypeStruct((N_DEV,) + w_out_s.shape, BF16),
                   jax.ShapeDtypeStruct((N_DEV,) + small_s.shape, F32)),
        in_specs=[vmem, vmem, vmem], out_specs=(vmem, vmem, vmem),
        scratch_shapes=[pltpu.SemaphoreType.DMA((21,)), pltpu.SemaphoreType.DMA((21,))],
        compiler_params=pltpu.CompilerParams(vmem_limit_bytes=VMEM_LIMIT),
    )(w_in_s, w_out_s, small_s)


def _allreduce_small(part):
    def body(p_ref, tot_ref, all_ref, send_sems, recv_sems):
        mine = _blk(*_position())
        all_ref[mine] = p_ref[...]
        _two_level_gather((all_ref,), send_sems, recv_sems)
        acc = all_ref[0]
        for d in range(1, N_DEV):
            acc = acc + all_ref[d]
        tot_ref[...] = acc

    vmem = pl.BlockSpec(memory_space=pltpu.VMEM)
    return pl.pallas_call(
        body, name="allreduce_small",
        out_shape=jax.ShapeDtypeStruct(part.shape, F32),
        in_specs=[vmem], out_specs=vmem,
        scratch_shapes=[pltpu.VMEM((N_DEV,) + part.shape, F32),
                        pltpu.SemaphoreType.DMA((7,)), pltpu.SemaphoreType.DMA((7,))],
        compiler_params=pltpu.CompilerParams(vmem_limit_bytes=VMEM_LIMIT),
    )(part)


def _exchange_sibling(parts):
    n = len(parts)

    def body(*refs):
        ins, outs = refs[:n], refs[n:2 * n]
        send_sems, recv_sems = refs[2 * n], refs[2 * n + 1]
        x, y, c = _position()
        sibling = (x, y, 1 - c)
        copies = []
        for a in range(n):
            for k in range(4):
                copies.append(pltpu.make_async_remote_copy(
                    src_ref=ins[a].at[2 * k + (1 - c)], dst_ref=outs[a].at[k],
                    send_sem=send_sems.at[a * 4 + k], recv_sem=recv_sems.at[a * 4 + k],
                    device_id=sibling, device_id_type=MESH))
        for cp in copies:
            cp.start()
        for cp in copies:
            cp.wait_recv()
        for cp in copies:
            cp.wait_send()

    hbm = pl.BlockSpec(memory_space=pl.ANY)
    return pl.pallas_call(
        body, name="exchange_sibling",
        out_shape=tuple(jax.ShapeDtypeStruct((4,) + p.shape[1:], F32) for p in parts),
        in_specs=[hbm] * n, out_specs=tuple([hbm] * n),
        scratch_shapes=[pltpu.SemaphoreType.DMA((4 * n,)), pltpu.SemaphoreType.DMA((4 * n,))],
    )(*parts)


def _exchange_chips(sums):
    n = len(sums)

    def body(*refs):
        ins, outs = refs[:n], refs[n:2 * n]
        send_sems, recv_sems = refs[2 * n], refs[2 * n + 1]
        x, y, c = _position()
        chips = [(1 - x, y), (x, 1 - y), (1 - x, 1 - y)]
        copies = []
        for a in range(n):
            for j, (px, py) in enumerate(chips):
                copies.append(pltpu.make_async_remote_copy(
                    src_ref=ins[a].at[2 * px + py], dst_ref=outs[a].at[j],
                    send_sem=send_sems.at[a * 3 + j], recv_sem=recv_sems.at[a * 3 + j],
                    device_id=(px, py, c), device_id_type=MESH))
        for cp in copies:
            cp.start()
        for cp in copies:
            cp.wait_recv()
        for cp in copies:
            cp.wait_send()

    hbm = pl.BlockSpec(memory_space=pl.ANY)
    return pl.pallas_call(
        body, name="exchange_chips",
        out_shape=tuple(jax.ShapeDtypeStruct((3,) + s.shape[1:], s.dtype) for s in sums),
        in_specs=[hbm] * n, out_specs=tuple([hbm] * n),
        scratch_shapes=[pltpu.SemaphoreType.DMA((3 * n,)), pltpu.SemaphoreType.DMA((3 * n,))],
    )(*sums)


def _chip_sums(part, from_sibling, core, tc, name):
    _, rows, cols = part.shape

    def body(core_ref, p_ref, s_ref, o_ref):
        o_ref[...] = (p_ref[...] + s_ref[...]).astype(BF16)

    return pl.pallas_call(
        body, name=name,
        out_shape=jax.ShapeDtypeStruct((4, rows, cols), BF16),
        grid_spec=pltpu.PrefetchScalarGridSpec(
            num_scalar_prefetch=1, grid=(4, cols // tc),
            in_specs=[pl.BlockSpec((1, rows, tc), lambda k, j, core_ref: (2 * k + core_ref[0], 0, j)),
                      pl.BlockSpec((1, rows, tc), lambda k, j, core_ref: (k, 0, j))],
            out_specs=pl.BlockSpec((1, rows, tc), lambda k, j, core_ref: (k, 0, j))),
        compiler_params=_cparams("arbitrary", "arbitrary"),
    )(core, part, from_sibling)


def _sum_chips(s_ref, r_ref):
    f = lambda a: a.astype(F32)
    return ((f(s_ref[0]) + f(r_ref[0])) + f(r_ref[1])) + f(r_ref[2])


def _final_sum(sums, from_chips, chip, tc, name):
    _, rows, cols = sums.shape

    def body(chip_ref, s_ref, r_ref, g_out):
        g_out[...] = _sum_chips(s_ref, r_ref)

    return pl.pallas_call(
        body, name=name,
        out_shape=jax.ShapeDtypeStruct((rows, cols), F32),
        grid_spec=pltpu.PrefetchScalarGridSpec(
            num_scalar_prefetch=1, grid=(cols // tc,),
            in_specs=[pl.BlockSpec((1, rows, tc), lambda j, chip_ref: (chip_ref[0], 0, j)),
                      pl.BlockSpec((3, rows, tc), lambda j, chip_ref: (0, 0, j))],
            out_specs=pl.BlockSpec((rows, tc), lambda j, chip_ref: (0, j))),
        compiler_params=_cparams("arbitrary"),
    )(chip, sums, from_chips)


def _adamw_rows(g, w, m, v, tr, name):
    rows, cols = g.shape

    def body(g_ref, w_ref, m_ref, v_ref, d_out, m_out, v_out):
        delta, m_new, v_new = _adamw(w_ref[...], g_ref[...], m_ref[...], v_ref[...])
        d_out[...] = delta
        m_out[...] = m_new
        v_out[...] = v_new

    tile = pl.BlockSpec((tr, cols), lambda r: (r, 0))
    shp = jax.ShapeDtypeStruct((rows, cols), F32)
    return pl.pallas_call(
        body, name=name, out_shape=(shp, shp, shp), grid=(rows // tr,),
        in_specs=[tile] * 4, out_specs=(tile, tile, tile),
        compiler_params=_cparams("arbitrary"),
    )(g, w, m, v)


def _adamw(w, g, m, v):
    m = ADAM_B1 * m + (1.0 - ADAM_B1) * g
    v = ADAM_B2 * v + (1.0 - ADAM_B2) * (g * g)
    m_hat = m / (1.0 - ADAM_B1 ** ADAM_STEP)
    v_hat = v / (1.0 - ADAM_B2 ** ADAM_STEP)
    delta = -ADAM_LR * (m_hat / (jnp.sqrt(v_hat) + ADAM_EPS) + ADAM_WD * w)
    return delta, m, v


def _final_sum_adamw(sums, from_chips, chip, w, m, v, tr, name):
    rows, cols = w.shape

    def body(chip_ref, s_ref, r_ref, w_ref, m_ref, v_ref, g_out, d_out, m_out, v_out):
        g = _sum_chips(s_ref, r_ref)
        delta, m_new, v_new = _adamw(w_ref[...], g, m_ref[...], v_ref[...])
        g_out[...] = g
        d_out[...] = delta
        m_out[...] = m_new
        v_out[...] = v_new

    tile = pl.BlockSpec((tr, cols), lambda r, chip_ref: (r, 0))
    shp = jax.ShapeDtypeStruct((rows, cols), F32)
    return pl.pallas_call(
        body, name=name,
        out_shape=(shp, shp, shp, shp),
        grid_spec=pltpu.PrefetchScalarGridSpec(
            num_scalar_prefetch=1, grid=(rows // tr,),
            in_specs=[pl.BlockSpec((1, tr, cols), lambda r, chip_ref: (chip_ref[0], r, 0)),
                      pl.BlockSpec((3, tr, cols), lambda r, chip_ref: (0, r, 0)),
                      tile, tile, tile],
            out_specs=(tile, tile, tile, tile)),
        compiler_params=_cparams("arbitrary"),
    )(chip, sums, from_chips, w, m, v)


def _adamw_small(g, w, m, v):
    def body(g_ref, w_ref, m_ref, v_ref, d_out, m_out, v_out):
        delta, m_new, v_new = _adamw(w_ref[...], g_ref[...], m_ref[...], v_ref[...])
        d_out[...] = delta
        m_out[...] = m_new
        v_out[...] = v_new

    vmem = pl.BlockSpec(memory_space=pltpu.VMEM)
    shp = jax.ShapeDtypeStruct(g.shape, F32)
    return pl.pallas_call(body, name="adamw_small", out_shape=(shp, shp, shp),
                          in_specs=[vmem] * 4, out_specs=(vmem, vmem, vmem))(g, w, m, v)


def _inproj(x2d, norm_g, w_main, w_lr, tm, tn):
    seq = x2d.shape[0]
    ni = seq // tm
    first_sweep = lambda j, i: jnp.where(j == 0, i, ni - 1)

    def body(x_ref, g_ref, w_ref, wlr_ref, proj_ref, lr_ref, ht_ref, h_all):
        rows = pl.ds(pl.multiple_of(pl.program_id(1) * tm, tm), tm)

        @pl.when(pl.program_id(0) == 0)
        def _():
            xv = x_ref[...]
            r = lax.rsqrt(jnp.mean(xv * xv, axis=-1, keepdims=True) + EPS)
            h = (xv * r) * g_ref[...]
            hb = h.astype(BF16)
            h_all[rows, :] = hb
            ht_ref[...] = h.T.astype(BF16)
            lr_ref[...] = _dot_nt(hb, wlr_ref[...])

        proj_ref[...] = _dot_nt(h_all[rows, :], w_ref[...]).astype(BF16)

    return pl.pallas_call(
        body, name="inproj",
        out_shape=(jax.ShapeDtypeStruct((seq, MAIN_W), BF16), jax.ShapeDtypeStruct((seq, LR_W), F32),
                   jax.ShapeDtypeStruct((D_MODEL, seq), BF16)),
        grid=(MAIN_W // tn, ni),
        in_specs=[pl.BlockSpec((tm, D_MODEL), lambda j, i: (first_sweep(j, i), 0)),
                  pl.BlockSpec((1, D_MODEL), lambda j, i: (0, 0)),
                  pl.BlockSpec((tn, D_MODEL), lambda j, i: (j, 0)),
                  pl.BlockSpec((LR_W, D_MODEL), lambda j, i: (0, 0))],
        out_specs=(pl.BlockSpec((tm, tn), lambda j, i: (i, j)),
                   pl.BlockSpec((tm, LR_W), lambda j, i: (first_sweep(j, i), 0)),
                   pl.BlockSpec((D_MODEL, tm), lambda j, i: (0, first_sweep(j, i)))),
        scratch_shapes=[pltpu.VMEM((seq, D_MODEL), BF16)],
        compiler_params=_cparams("arbitrary", "arbitrary"),
    )(x2d, norm_g, w_main, w_lr)


def _block_masks(tt):
    row = lax.broadcasted_iota(jnp.int32, (tt, tt), 0)
    col = lax.broadcasted_iota(jnp.int32, (tt, tt), 1)
    same = jnp.right_shift(row, 6) == jnp.right_shift(col, 6)
    return (jnp.logical_and(same, col <= row), jnp.logical_and(same, col >= row), jnp.logical_and(same, col > row))


def _chunk_column_mask(tt):
    nc = tt // CHUNK
    row = lax.broadcasted_iota(jnp.int32, (tt, nc * DK), 0)
    col = lax.broadcasted_iota(jnp.int32, (tt, nc * DK), 1)
    return jnp.right_shift(row, 6) == jnp.right_shift(col, 7)


def _dot_split3(ones_mat, x):
    x1 = x.astype(BF16)
    r1 = x - x1.astype(F32)
    x2 = r1.astype(BF16)
    x3 = (r1 - x2.astype(F32)).astype(BF16)
    return (_dot(ones_mat, x3) + _dot(ones_mat, x2)) + _dot(ones_mat, x1)


def _log_gate(logits):
    return (jnp.minimum(logits, 0.0) - jnp.log1p(jnp.exp(-jnp.abs(logits)))) * GATE_SCALE


def _chunked(mask, x, nc):
    wide = jnp.concatenate([x] * nc, axis=1)
    return jnp.where(mask, wide, jnp.zeros_like(wide))


def _gla_fwd(proj, lr, wgk_f, wgk_b, bgk_f, bgk_b, tt):
    seq = proj.shape[0]
    nb, nc, nch = seq // tt, tt // CHUNK, seq // CHUNK

    def body(qf, kf, vf, lrf, qb, kb, vb, lrb, wf, wb, bf, bb, of, ob, stf, stb, s_scr, qs_s, ks_s, qin_s, kout_s):
        @pl.when(pl.program_id(0) == 0)
        def _():
            s_scr[...] = jnp.zeros(s_scr.shape, F32)

        low, upp, sup = _block_masks(tt)
        kmask = _chunk_column_mask(tt)
        dirs = ((qf, kf, vf, lrf, wf, bf, of, stf, low, low, REF_F, LAST_F, list(range(nc))),
                (qb, kb, vb, lrb, wb, bb, ob, stb, upp, sup, REF_B, LAST_B, list(reversed(range(nc)))))
        for d, (q_r, k_r, v_r, lr_r, w_r, b_r, o_r, st_r, cum, mask, ref, last, order) in enumerate(dirs):
            logits = _dot(lr_r[...].astype(BF16), w_r[...]) + b_r[...]
            b = _dot_split3(cum.astype(BF16), _log_gate(logits))
            decs = []
            for c in range(nc):
                rows = slice(c * CHUNK, (c + 1) * CHUNK)
                bc = b[rows]
                b_ref, b_last = bc[ref:ref + 1], bc[last:last + 1]
                qc = q_r[rows, :].astype(F32) * QSCALE
                kc = k_r[rows, :].astype(F32)
                qs_s[rows, :] = (qc * jnp.exp(bc - b_ref)).astype(BF16)
                ks_s[rows, :] = (kc * jnp.exp(b_ref - bc)).astype(BF16)
                qin_s[rows, :] = (qc * jnp.exp(bc)).astype(BF16)
                kout_s[rows, :] = (kc * jnp.exp(b_last - bc)).astype(BF16)
                decs.append(jnp.exp(b_last))
            for h in range(HEADS):
                ksl = slice(h * DK, (h + 1) * DK)
                vsl = slice(h * DV, (h + 1) * DV)
                v = v_r[:, vsl].astype(BF16)
                att = jnp.where(mask, _dot_nt(qs_s[:, ksl], ks_s[:, ksl]), 0.0).astype(BF16)
                o_intra = _dot(att, v)
                kv_t = _dot_tn(v, _chunked(kmask, kout_s[:, ksl], nc))
                st = s_scr[d * HEADS + h]
                for c in order:
                    rows = slice(c * CHUNK, (c + 1) * CHUNK)
                    stb = st.astype(BF16)
                    st_r[c, h] = stb
                    o_r[rows, vsl] = (o_intra[rows] + _dot_nt(qin_s[rows, ksl], stb)).astype(BF16)
                    st = st * decs[c][:, ksl] + kv_t[:, c * DK:(c + 1) * DK]
                s_scr[d * HEADS + h] = st

    fw = lambda i: (i, 0)
    bw = lambda i: (nb - 1 - i, 0)
    const = lambda i: (0, 0)

    def tok_specs(m):
        return [pl.BlockSpec((tt, QK_W), lambda i: (m(i)[0], OFF_Q // QK_W)),
                pl.BlockSpec((tt, QK_W), lambda i: (m(i)[0], OFF_K // QK_W)),
                pl.BlockSpec((tt, V_W), lambda i: (m(i)[0], OFF_V // V_W)),
                pl.BlockSpec((tt, LR_W), m)]

    st_shape = jax.ShapeDtypeStruct((nch, HEADS, DV, DK), BF16)
    o_shape = jax.ShapeDtypeStruct((seq, V_W), BF16)
    operand = pltpu.VMEM((tt, QK_W), BF16)
    return pl.pallas_call(
        body, name="gla_fwd",
        out_shape=(o_shape, o_shape, st_shape, st_shape),
        grid=(nb,),
        in_specs=tok_specs(fw) + tok_specs(bw) + [
            pl.BlockSpec((LR_W, QK_W), const), pl.BlockSpec((LR_W, QK_W), const),
            pl.BlockSpec((1, QK_W), const), pl.BlockSpec((1, QK_W), const)],
        out_specs=(pl.BlockSpec((tt, V_W), fw), pl.BlockSpec((tt, V_W), bw),
                   pl.BlockSpec((nc, HEADS, DV, DK), lambda i: (i, 0, 0, 0)),
                   pl.BlockSpec((nc, HEADS, DV, DK), lambda i: (nb - 1 - i, 0, 0, 0))),
        scratch_shapes=[pltpu.VMEM((2 * HEADS, DV, DK), F32), operand, operand, operand, operand],
        compiler_params=_cparams("arbitrary"),
    )(proj, proj, proj, lr, proj, proj, proj, lr, wgk_f, wgk_b, bgk_f, bgk_b)


def _head_norm(o, gain):
    outs, rinv = [], []
    for h in range(HEADS):
        oh = o[:, h * DV:(h + 1) * DV]
        r = lax.rsqrt(jnp.mean(oh * oh, axis=-1, keepdims=True) + EPS)
        outs.append((oh * r) * gain)
        rinv.append(r)
    return jnp.concatenate(outs, axis=1), rinv


def _shift_rows(u, prev_row, next_row):
    n = u.shape[0]
    row = lax.broadcasted_iota(jnp.int32, (n, 1), 0)
    up = jnp.where(row == 0, prev_row, pltpu.roll(u, 1, 0))
    un = jnp.where(row == n - 1, next_row, pltpu.roll(u, n - 1, 0))
    return up, un


HALO = 16


def _halo_specs(tm, seq, col_block):
    per = tm // HALO
    last = seq // HALO - 1
    return [pl.BlockSpec((HALO, CONV_W), lambda i: (jnp.maximum(i * per - 1, 0), col_block)),
            pl.BlockSpec((HALO, CONV_W), lambda i: (jnp.minimum((i + 1) * per, last), col_block))]


def _f32(ref):
    return ref[...].astype(F32)


def _last_row(ref):
    return ref[HALO - 1:HALO, :].astype(F32)


def _first_row(ref):
    return ref[0:1, :].astype(F32)


def _mix_out_loss(o_f, o_b, proj, x2d, tgt, gla_g, conv_w, conv_b, w_out, final_g, tm):
    seq = x2d.shape[0]
    nt = seq // tm

    def body(of, ob, za, bg, cg, hc, zc, cprev, cnext, hprev, hnext, x_ref, t_ref, gg, cw, cb, wo, fg,
             yt_ref, conv_ref, dx2_ref, dx2b_ref, loss_ref, dfg_ref):
        i = pl.program_id(0)

        @pl.when(i == 0)
        def _():
            loss_ref[...] = jnp.zeros(loss_ref.shape, F32)
            dfg_ref[...] = jnp.zeros(dfg_ref.shape, F32)

        on, _ = _head_norm(_f32(of) + _f32(ob), gg[...])
        zav = _f32(za)
        y_a = on * (zav * _sigmoid(zav))
        u = _f32(cg) * _f32(hc)
        prev_row = jnp.where(i > 0, _last_row(cprev) * _last_row(hprev), 0.0)
        next_row = jnp.where(i < nt - 1, _first_row(cnext) * _first_row(hnext), 0.0)
        up, un = _shift_rows(u, prev_row, next_row)
        conv = (cw[0:1, :] * up + cw[1:2, :] * u + cw[2:3, :] * un) + cb[...]
        conv_ref[...] = conv.astype(BF16)
        zcv = _f32(zc)
        y_c = _f32(bg) * conv * (zcv * _sigmoid(zcv))
        y = jnp.concatenate([y_a, y_c], axis=1)
        yt_ref[...] = y.T.astype(BF16)
        x2 = x_ref[...] + _dot(y.astype(BF16), wo[...])
        r = lax.rsqrt(jnp.mean(x2 * x2, axis=-1, keepdims=True) + EPS)
        xn = x2 * r
        err = xn * fg[...] - t_ref[...]
        loss_ref[...] += 0.5 * jnp.sum(jnp.mean(err * err, axis=-1, keepdims=True))
        dyf = err * (1.0 / D_MODEL)
        dfg_ref[...] += jnp.sum(dyf * xn, axis=0, keepdims=True)
        dxn = dyf * fg[...]
        dx2 = r * dxn - xn * (r * jnp.mean(dxn * xn, axis=-1, keepdims=True))
        dx2_ref[...] = dx2
        dx2b_ref[...] = dx2.astype(BF16)

    def col(off):
        return pl.BlockSpec((tm, CONV_W), lambda i: (i, off // CONV_W))

    rowt = pl.BlockSpec((tm, D_MODEL), lambda i: (i, 0))
    const = lambda shape: pl.BlockSpec(shape, lambda i: (0, 0))
    return pl.pallas_call(
        body, name="mix_out_loss",
        out_shape=(jax.ShapeDtypeStruct((MIX_W, seq), BF16), jax.ShapeDtypeStruct((seq, CONV_W), BF16),
                   jax.ShapeDtypeStruct((seq, D_MODEL), F32), jax.ShapeDtypeStruct((seq, D_MODEL), BF16),
                   jax.ShapeDtypeStruct((8, 128), F32), jax.ShapeDtypeStruct((1, D_MODEL), F32)),
        grid=(nt,),
        in_specs=[rowt, rowt, col(OFF_ZA), col(OFF_B), col(OFF_C), col(OFF_H), col(OFF_ZC)]
        + _halo_specs(tm, seq, OFF_C // CONV_W) + _halo_specs(tm, seq, OFF_H // CONV_W)
        + [rowt, rowt, const((1, DV)), const((8, CONV_W)), const((1, CONV_W)), const((MIX_W, D_MODEL)),
           const((1, D_MODEL))],
        out_specs=(pl.BlockSpec((MIX_W, tm), lambda i: (0, i)), rowt, rowt, rowt, const((8, 128)),
                   const((1, D_MODEL))),
        compiler_params=_cparams("arbitrary"),
    )(o_f, o_b, proj, proj, proj, proj, proj, proj, proj, proj, proj, x2d, tgt, gla_g, conv_w, conv_b, w_out, final_g)


def _dsilu(z, s):
    return s * (1.0 + z * (1.0 - s))


def _mix_bwd(dx2b, o_f, o_b, proj, conv, gla_g, w_out, tm):
    seq = dx2b.shape[0]

    def body(dx, of, ob, za, bg, zc, cv, gg, wo, dg_ref, do_ref, dconv_ref, dgg_ref, dcb_ref):
        @pl.when(pl.program_id(0) == 0)
        def _():
            dgg_ref[...] = jnp.zeros(dgg_ref.shape, F32)
            dcb_ref[...] = jnp.zeros(dcb_ref.shape, F32)

        dy = _dot_nt(dx[...], wo[...])
        dy_a, dy_c = dy[:, :V_W], dy[:, V_W:]
        zcv, bgv, convv = _f32(zc), _f32(bg), _f32(cv)
        sc = _sigmoid(zcv)
        szc = zcv * sc
        dg_ref[:, CONV_W:2 * CONV_W] = (dy_c * convv * szc).astype(BF16)
        dconv = dy_c * bgv * szc
        dconv_ref[...] = dconv.astype(BF16)
        dcb_ref[...] += jnp.sum(dconv, axis=0, keepdims=True)
        dg_ref[:, 2 * CONV_W:] = (dy_c * bgv * convv * _dsilu(zcv, sc)).astype(BF16)

        o = _f32(of) + _f32(ob)
        gain = gg[...]
        on, rinv = _head_norm(o, gain)
        zav = _f32(za)
        sa = _sigmoid(zav)
        dg_ref[:, :CONV_W] = (dy_a * on * _dsilu(zav, sa)).astype(BF16)
        don = dy_a * (zav * sa)
        dgg = jnp.zeros((1, DV), F32)
        dos = []
        for h in range(HEADS):
            sl = slice(h * DV, (h + 1) * DV)
            oh, r, dh = o[:, sl], rinv[h], don[:, sl]
            ohn = oh * r
            dgg = dgg + jnp.sum(dh * ohn, axis=0, keepdims=True)
            dn = dh * gain
            dos.append(r * dn - ohn * (r * jnp.mean(dn * ohn, axis=-1, keepdims=True)))
        dgg_ref[...] += dgg
        do_ref[...] = jnp.concatenate(dos, axis=1).astype(BF16)

    def col(off):
        return pl.BlockSpec((tm, CONV_W), lambda i: (i, off // CONV_W))

    rowt = pl.BlockSpec((tm, D_MODEL), lambda i: (i, 0))
    const = lambda shape: pl.BlockSpec(shape, lambda i: (0, 0))
    return pl.pallas_call(
        body, name="mix_bwd",
        out_shape=(jax.ShapeDtypeStruct((seq, GATES_W), BF16), jax.ShapeDtypeStruct((seq, V_W), BF16),
                   jax.ShapeDtypeStruct((seq, CONV_W), BF16),
                   jax.ShapeDtypeStruct((1, DV), F32), jax.ShapeDtypeStruct((1, CONV_W), F32)),
        grid=(seq // tm,),
        in_specs=[rowt, rowt, rowt, col(OFF_ZA), col(OFF_B), col(OFF_ZC), rowt, const((1, DV)),
                  const((MIX_W, D_MODEL))],
        out_specs=(pl.BlockSpec((tm, GATES_W), lambda i: (i, 0)), rowt, rowt, const((1, DV)), const((1, CONV_W))),
        compiler_params=_cparams("arbitrary"),
    )(dx2b, o_f, o_b, proj, proj, proj, conv, gla_g, w_out)


def _conv_bwd(dconv, proj, conv_w, tm):
    seq = dconv.shape[0]
    nt = seq // tm

    def body(dc_in, dprev, dnext, cg, hc, cprev, cnext, hprev, hnext, cw, dch_ref, dcw_ref):
        i = pl.program_id(0)

        @pl.when(i == 0)
        def _():
            dcw_ref[...] = jnp.zeros(dcw_ref.shape, F32)

        first, lastt = i > 0, i < nt - 1
        dcv = _f32(dc_in)
        d_up, d_un = _shift_rows(dcv, jnp.where(first, _last_row(dprev), 0.0), jnp.where(lastt, _first_row(dnext), 0.0))
        cgv, hcv = _f32(cg), _f32(hc)
        u = cgv * hcv
        u_up, u_un = _shift_rows(u, jnp.where(first, _last_row(cprev) * _last_row(hprev), 0.0),
                                 jnp.where(lastt, _first_row(cnext) * _first_row(hnext), 0.0))
        du = cw[0:1, :] * d_un + cw[1:2, :] * dcv + cw[2:3, :] * d_up
        dch_ref[:, :CONV_W] = (du * hcv).astype(BF16)
        dch_ref[:, CONV_W:] = (du * cgv).astype(BF16)
        dcw_ref[0:1, :] += jnp.sum(dcv * u_up, axis=0, keepdims=True)
        dcw_ref[1:2, :] += jnp.sum(dcv * u, axis=0, keepdims=True)
        dcw_ref[2:3, :] += jnp.sum(dcv * u_un, axis=0, keepdims=True)

    def col(off):
        return pl.BlockSpec((tm, CONV_W), lambda i: (i, off // CONV_W))

    rowt = pl.BlockSpec((tm, CONV_W), lambda i: (i, 0))
    const = lambda shape: pl.BlockSpec(shape, lambda i: (0, 0))
    return pl.pallas_call(
        body, name="conv_bwd",
        out_shape=(jax.ShapeDtypeStruct((seq, CH_W), BF16), jax.ShapeDtypeStruct((8, CONV_W), F32)),
        grid=(nt,),
        in_specs=[rowt] + _halo_specs(tm, seq, 0) + [col(OFF_C), col(OFF_H)]
        + _halo_specs(tm, seq, OFF_C // CONV_W) + _halo_specs(tm, seq, OFF_H // CONV_W) + [const((8, CONV_W))],
        out_specs=(pl.BlockSpec((tm, CH_W), lambda i: (i, 0)), const((8, CONV_W))),
        compiler_params=_cparams("arbitrary"),
    )(dconv, dconv, dconv, proj, proj, proj, proj, proj, proj, conv_w)


def _gla_bwd(proj, lr, do, st_f, st_b, wgk_f, wgk_b, bgk_f, bgk_b, tt):
    seq = proj.shape[0]
    nb, nc = seq // tt, tt // CHUNK

    def body(qf, kf, vf, lrf, dof, stf, qb, kb, vb, lrb, dob, stb, wf, wb, bf, bb,
             dqkv_f, dlr_f, dqkv_b, dlr_b, dwf, dwb, dbf, dbb,
             ds_scr, eq_s, ek_s, ein_s, eout_s, qs_s, ks_s, qin_s, kout_s, db_s, lg_s):
        @pl.when(pl.program_id(0) == 0)
        def _():
            ds_scr[...] = jnp.zeros(ds_scr.shape, F32)
            for r in (dwf, dwb, dbf, dbb):
                r[...] = jnp.zeros(r.shape, F32)

        low, upp, sup = _block_masks(tt)
        kmask = _chunk_column_mask(tt)
        row = lax.broadcasted_iota(jnp.int32, (CHUNK, 1), 0)
        dirs = ((qf, kf, vf, lrf, dof, stf, wf, bf, dqkv_f, dlr_f, dwf, dbf,
                 low, upp, low, REF_F, LAST_F, list(reversed(range(nc)))),
                (qb, kb, vb, lrb, dob, stb, wb, bb, dqkv_b, dlr_b, dwb, dbb,
                 upp, low, sup, REF_B, LAST_B, list(range(nc))))
        for d, (q_r, k_r, v_r, lr_r, do_r, st_r, w_r, b_r, dqkv_r, dlr_r, dw_r, db_r,
                cum, cum_t, mask, ref, last, order) in enumerate(dirs):
            lrv = lr_r[...].astype(BF16)
            wv = w_r[...]
            logits = _dot(lrv, wv) + b_r[...]
            lg_s[...] = logits
            b = _dot_split3(cum.astype(BF16), _log_gate(logits))
            decs = []
            for c in range(nc):
                rows = slice(c * CHUNK, (c + 1) * CHUNK)
                bc = b[rows]
                b_ref, b_last = bc[ref:ref + 1], bc[last:last + 1]
                qc = q_r[rows, :].astype(F32) * QSCALE
                kc = k_r[rows, :].astype(F32)
                e_q, e_k, e_in, e_out = jnp.exp(bc - b_ref), jnp.exp(b_ref - bc), jnp.exp(bc), jnp.exp(b_last - bc)
                eq_s[rows, :], ek_s[rows, :], ein_s[rows, :], eout_s[rows, :] = e_q, e_k, e_in, e_out
                qs_s[rows, :] = (qc * e_q).astype(BF16)
                ks_s[rows, :] = (kc * e_k).astype(BF16)
                qin_s[rows, :] = (qc * e_in).astype(BF16)
                kout_s[rows, :] = (kc * e_out).astype(BF16)
                decs.append(jnp.exp(b_last))
            for h in range(HEADS):
                ksl = slice(h * DK, (h + 1) * DK)
                vsl = slice(h * DV, (h + 1) * DV)
                v = v_r[:, vsl].astype(BF16)
                dov = do_r[:, vsl].astype(BF16)
                qsb, ksb = qs_s[:, ksl], ks_s[:, ksl]
                att = jnp.where(mask, _dot_nt(qsb, ksb), 0.0).astype(BF16)
                datt = jnp.where(mask, _dot_nt(dov, v), 0.0).astype(BF16)
                dqs = _dot(datt, ksb)
                dks = _dot_tn(datt, qsb)
                dv_intra = _dot_tn(att, dov)
                g_t = _dot_tn(dov, _chunked(kmask, qin_s[:, ksl], nc))
                ds = ds_scr[d * HEADS + h]
                for c in order:
                    rows = slice(c * CHUNK, (c + 1) * CHUNK)
                    dsb = ds.astype(BF16)
                    s_prev = st_r[c, h]
                    dk_out = _dot(v[rows], dsb)
                    dq_in = _dot(dov[rows], s_prev)
                    dqkv_r[rows, OFF_V + h * DV:OFF_V + (h + 1) * DV] = dv_intra[rows] + _dot_nt(kout_s[rows, ksl], dsb)
                    dec = decs[c][:, ksl]
                    ddec = jnp.sum(ds * s_prev.astype(F32), axis=0, keepdims=True)
                    e_q, e_k, e_in, e_out = eq_s[rows, ksl], ek_s[rows, ksl], ein_s[rows, ksl], eout_s[rows, ksl]
                    qc = q_r[rows, ksl].astype(F32) * QSCALE
                    kc = k_r[rows, ksl].astype(F32)
                    dqs_c, dks_c = dqs[rows], dks[rows]
                    dqkv_r[rows, OFF_Q + h * DK:OFF_Q + (h + 1) * DK] = (dqs_c * e_q + dq_in * e_in) * QSCALE
                    dqkv_r[rows, OFF_K + h * DK:OFF_K + (h + 1) * DK] = dks_c * e_k + dk_out * e_out
                    kk = dk_out * (kc * e_out)
                    db = dqs_c * (qc * e_q) - dks_c * (kc * e_k) + dq_in * (qc * e_in) - kk
                    tail = jnp.sum(kk, axis=0, keepdims=True) + ddec * dec
                    db_s[rows, ksl] = db + jnp.where(row == last, tail, 0.0)
                    ds = ds * dec + g_t[:, c * DK:(c + 1) * DK]
                ds_scr[d * HEADS + h] = ds
            dg = _dot_split3(cum_t.astype(BF16), db_s[...])
            dlogit = (dg * GATE_SCALE) * _sigmoid(-lg_s[...])
            dlb = dlogit.astype(BF16)
            dlr_r[...] = _dot_nt(dlb, wv)
            dw_r[...] += _dot_tn(lrv, dlb)
            db_r[...] += jnp.sum(dlogit, axis=0, keepdims=True)

    fw = lambda i: (nb - 1 - i, 0)
    bw = lambda i: (i, 0)
    const = lambda i: (0, 0)

    def tok_specs(m):
        return [pl.BlockSpec((tt, QK_W), lambda i: (m(i)[0], OFF_Q // QK_W)),
                pl.BlockSpec((tt, QK_W), lambda i: (m(i)[0], OFF_K // QK_W)),
                pl.BlockSpec((tt, V_W), lambda i: (m(i)[0], OFF_V // V_W)),
                pl.BlockSpec((tt, LR_W), m),
                pl.BlockSpec((tt, V_W), m),
                pl.BlockSpec((nc, HEADS, DV, DK), lambda i: (m(i)[0], 0, 0, 0))]

    dqkv = jax.ShapeDtypeStruct((seq, QK_W + QK_W + V_W), F32)
    dlr = jax.ShapeDtypeStruct((seq, LR_W), F32)
    dw = jax.ShapeDtypeStruct((LR_W, QK_W), F32)
    dbias = jax.ShapeDtypeStruct((1, QK_W), F32)
    return pl.pallas_call(
        body, name="gla_bwd",
        out_shape=(dqkv, dlr, dqkv, dlr, dw, dw, dbias, dbias),
        grid=(nb,),
        in_specs=tok_specs(fw) + tok_specs(bw) + [
            pl.BlockSpec((LR_W, QK_W), const), pl.BlockSpec((LR_W, QK_W), const),
            pl.BlockSpec((1, QK_W), const), pl.BlockSpec((1, QK_W), const)],
        out_specs=(pl.BlockSpec((tt, QK_W + QK_W + V_W), fw), pl.BlockSpec((tt, LR_W), fw),
                   pl.BlockSpec((tt, QK_W + QK_W + V_W), bw), pl.BlockSpec((tt, LR_W), bw),
                   pl.BlockSpec((LR_W, QK_W), const), pl.BlockSpec((LR_W, QK_W), const),
                   pl.BlockSpec((1, QK_W), const), pl.BlockSpec((1, QK_W), const)),
        scratch_shapes=[pltpu.VMEM((2 * HEADS, DV, DK), F32)] + [pltpu.VMEM((tt, QK_W), F32)] * 4
        + [pltpu.VMEM((tt, QK_W), BF16)] * 4 + [pltpu.VMEM((tt, QK_W), F32)] * 2,
        compiler_params=_cparams("arbitrary"),
    )(proj, proj, proj, lr, do, st_f, proj, proj, proj, lr, do, st_b, wgk_f, wgk_b, bgk_f, bgk_b)


def _sum_directions(dqkv_f, dqkv_b, dlr_f, dlr_b, tm):
    seq = dqkv_f.shape[0]

    def body(a, b, la, lb, dp_out, dlr_out):
        dp_out[...] = (a[...] + b[...]).astype(BF16)
        dlr_out[...] = (la[...] + lb[...]).astype(BF16)

    rowt = pl.BlockSpec((tm, QKV_W), lambda i: (i, 0))
    lrt = pl.BlockSpec((tm, LR_W), lambda i: (i, 0))
    return pl.pallas_call(
        body, name="sum_directions",
        out_shape=(jax.ShapeDtypeStruct((seq, QKV_W), BF16), jax.ShapeDtypeStruct((seq, LR_W), BF16)),
        grid=(seq // tm,),
        in_specs=[rowt, rowt, lrt, lrt],
        out_specs=(rowt, lrt),
        compiler_params=_cparams("arbitrary"),
    )(dqkv_f, dqkv_b, dlr_f, dlr_b)


def _input_grad(dp_qkv, dp_gates, dp_ch, dlr, w_main, w_lr, x2d, norm_g, dx2, tm):
    seq = x2d.shape[0]
    o_gates, o_ch = QKV_W, QKV_W + GATES_W

    def body(dq, dg, dc, dl, w, wl, x_ref, g_ref, dx2_ref, gx_ref, dng_ref):
        @pl.when(pl.program_id(0) == 0)
        def _():
            dng_ref[...] = jnp.zeros(dng_ref.shape, F32)

        dh = (_dot(dl[...], wl[...]) + _dot(dq[...], w[0:o_gates, :]) + _dot(dg[...], w[o_gates:o_ch, :])
              + _dot(dc[...], w[o_ch:MAIN_W, :]))
        xv = x_ref[...]
        r = lax.rsqrt(jnp.mean(xv * xv, axis=-1, keepdims=True) + EPS)
        xn = xv * r
        dng_ref[...] += jnp.sum(dh * xn, axis=0, keepdims=True)
        dn = dh * g_ref[...]
        gx_ref[...] = (r * dn - xn * (r * jnp.mean(dn * xn, axis=-1, keepdims=True))) + dx2_ref[...]

    rowt = pl.BlockSpec((tm, D_MODEL), lambda i: (i, 0))
    seg = lambda width: pl.BlockSpec((tm, width), lambda i: (i, 0))
    resident = lambda rows: pl.BlockSpec((rows, D_MODEL), lambda i: (0, 0), pipeline_mode=pl.Buffered(1))
    return pl.pallas_call(
        body, name="input_grad",
        out_shape=(jax.ShapeDtypeStruct((seq, D_MODEL), F32), jax.ShapeDtypeStruct((1, D_MODEL), F32)),
        grid=(seq // tm,),
        in_specs=[seg(QKV_W), seg(GATES_W), seg(CH_W), seg(LR_W), resident(MAIN_W), resident(LR_W),
                  rowt, pl.BlockSpec((1, D_MODEL), lambda i: (0, 0)), rowt],
        out_specs=(rowt, pl.BlockSpec((1, D_MODEL), lambda i: (0, 0))),
        compiler_params=_cparams("arbitrary"),
    )(dp_qkv, dp_gates, dp_ch, dlr, w_main, w_lr, x2d, norm_g, dx2)


def _weight_grad(at, b, tn, tk, name):
    m, seq = at.shape
    n = b.shape[1]

    def body(a_ref, b_ref, o_ref):
        @pl.when(pl.program_id(1) == 0)
        def _():
            o_ref[...] = jnp.zeros(o_ref.shape, F32)

        o_ref[...] += _dot(a_ref[...], b_ref[...])

    return pl.pallas_call(
        body, name=name,
        out_shape=jax.ShapeDtypeStruct((m, n), F32),
        grid=(n // tn, seq // tk),
        in_specs=[pl.BlockSpec((m, tk), lambda j, k: (0, k)), pl.BlockSpec((tk, tn), lambda j, k: (k, j))],
        out_specs=pl.BlockSpec((m, tn), lambda j, k: (0, j)),
        compiler_params=_cparams("arbitrary", "arbitrary"),
    )(at, b)


def _weight_grad_t(at, b, tn, name):
    m, seq = at.shape
    n = b.shape[1]

    def body(a_ref, b_ref, o_ref, acc):
        acc[...] = _dot(a_ref[...], b_ref[...])
        o_ref[...] = acc[...].T

    return pl.pallas_call(
        body, name=name,
        out_shape=jax.ShapeDtypeStruct((n, m), F32),
        grid=(n // tn,),
        in_specs=[pl.BlockSpec((m, seq), lambda j: (0, 0), pipeline_mode=pl.Buffered(1)),
                  pl.BlockSpec((seq, tn), lambda j: (0, j))],
        out_specs=pl.BlockSpec((tn, m), lambda j: (j, 0)),
        scratch_shapes=[pltpu.VMEM((m, tn), F32)],
        compiler_params=_cparams("arbitrary"),
    )(at, b)


def _pad_rows(a, rows):
    return jnp.pad(a, ((0, rows - a.shape[0]), (0, 0)))


def _rows128(a):
    a = a.reshape(-1, 128)
    return _pad_rows(a, -(-a.shape[0] // 8) * 8)


def _pack(arrs):
    return jnp.concatenate([_rows128(a) for a in arrs], axis=0)


def _unpack(buf, like):
    out, start = [], 0
    for a in like:
        rows = a.size // 128
        out.append(buf[start:start + rows].reshape(a.shape))
        start += -(-rows // 8) * 8
    return out


def kernel(x, norm_g, w_in, w_gk_f, b_gk_f, w_gk_b, b_gk_b, gla_norm_g, conv_w, conv_b, w_out, final_g, loss_target, m_norm_g, m_w_in, m_w_gk_f, m_b_gk_f, m_w_gk_b, m_b_gk_b, m_gla_norm_g, m_conv_w, m_conv_b, m_w_out, m_final_g, v_norm_g, v_w_in, v_w_gk_f, v_b_gk_f, v_w_gk_b, v_b_gk_b, v_gla_norm_g, v_conv_w, v_conv_b, v_w_out, v_final_g):
    px, py, pc = _position()
    me = _blk(px, py, pc)
    seq = x.shape[1]
    x2d, tgt = x[0], loss_target[0]
    tm = min(512, seq)
    tt = min(256, seq)

    small_s = jnp.concatenate([jnp.concatenate([w_gk_f[0], w_gk_b[0]], axis=1), _pad_rows(conv_w[0], 8)], axis=0)
    win_all, wout_all, small_all = _allgather_weights(w_in[0].T, w_out[0], small_s)
    w_nat = win_all.reshape(IN_W, D_MODEL)
    w_main = jnp.concatenate([w_nat[:NAT_LR], w_nat[NAT_B:NAT_C], w_nat[NAT_ZC:], w_nat[NAT_C:NAT_ZC]], axis=0)
    w_lr = w_nat[NAT_LR:NAT_LR + LR_W]
    w_out_full = wout_all.reshape(MIX_W, D_MODEL)
    wgk_cols = 512 // N_DEV
    wgk_f_full = small_all[:, 0:RANK, 0:wgk_cols].transpose(1, 0, 2).reshape(RANK, QK_W)
    wgk_b_full = small_all[:, 0:RANK, wgk_cols:2 * wgk_cols].transpose(1, 0, 2).reshape(RANK, QK_W)
    conv_w_full = _pad_rows(small_all[:, RANK:RANK + 3, :].transpose(1, 0, 2).reshape(3, CONV_W), 8)
    zr = lambda n: jnp.zeros((n, QK_W), F32)
    wgk_f_pad = jnp.concatenate([wgk_f_full, zr(LR_W - RANK)], axis=0).astype(BF16)
    wgk_b_pad = jnp.concatenate([zr(RANK), wgk_b_full, zr(LR_W - 2 * RANK)], axis=0).astype(BF16)

    proj, lr, h_t = _inproj(x2d, norm_g, w_main, w_lr, tm, 1024)
    o_f, o_b, st_f, st_b = _gla_fwd(proj, lr, wgk_f_pad, wgk_b_pad, b_gk_f, b_gk_b, tt)
    tmix = min(256, seq)
    y_t, conv, dx2, dx2b, loss_p, dfg_p = _mix_out_loss(o_f, o_b, proj, x2d, tgt, gla_norm_g, conv_w_full, conv_b,
                                                        w_out_full, final_g.reshape(1, D_MODEL), tmix)

    dp_gates, do, dconv, dgg_p, dcb_p = _mix_bwd(dx2b, o_f, o_b, proj, conv, gla_norm_g, w_out_full, tmix)
    dp_ch, dcw_p = _conv_bwd(dconv, proj, conv_w_full, tmix)
    dqkv_f, dlr_f, dqkv_b, dlr_b, dwf_p, dwb_p, dbf_p, dbb_p = _gla_bwd(
        proj, lr, do, st_f, st_b, wgk_f_pad, wgk_b_pad, b_gk_f, b_gk_b, tt)
    dp_qkv, dlr = _sum_directions(dqkv_f, dqkv_b, dlr_f, dlr_b, tm)
    grad_x2d, dng_p = _input_grad(dp_qkv, dp_gates, dp_ch, dlr, w_main, w_lr, x2d, norm_g, dx2, tmix)
    dw_qkv = _weight_grad_t(h_t, dp_qkv, 512, "wgrad_qkv")
    dw_gates = _weight_grad_t(h_t, dp_gates, 512, "wgrad_gates")
    dw_ch = _weight_grad_t(h_t, dp_ch, 512, "wgrad_ch")
    dw_lr = _weight_grad_t(h_t, dlr, LR_W, "wgrad_lr")
    dw_out = _weight_grad(y_t, dx2b, D_MODEL, tm, "wgrad_out")

    dw_nat = jnp.concatenate([dw_qkv, dw_gates[:CONV_W], dw_lr[:2 * RANK], dw_gates[CONV_W:2 * CONV_W], dw_ch,
                              dw_gates[2 * CONV_W:]], axis=0)
    part_in = dw_nat.reshape(N_DEV, SHARD_W, D_MODEL)
    part_out = dw_out.reshape(N_DEV, MIX_W // N_DEV, D_MODEL)
    sib_in, sib_out = _exchange_sibling([part_in, part_out])
    core = jnp.reshape(pc, (1,)).astype(jnp.int32)
    chip = jnp.reshape(2 * px + py, (1,)).astype(jnp.int32)
    sums_in = _chip_sums(part_in, sib_in, core, 256, "chip_sums_in")
    sums_out = _chip_sums(part_out, sib_out, core, 256, "chip_sums_out")
    far_in, far_out = _exchange_chips([sums_in, sums_out])
    g_in_t = _final_sum(sums_in, far_in, chip, 256, "final_sum_in")
    g_w_out, d_w_out, nm_w_out, nv_w_out = _final_sum_adamw(sums_out, far_out, chip, w_out[0], m_w_out[0], v_w_out[0],
                                                            256, "adamw_out")
    flat = lambda a: a[0].T.reshape(SHARD_W * D_MODEL // 128, 128)
    unflat = lambda a: a.reshape(SHARD_W, D_MODEL).T
    d_flat, m_flat, v_flat = _adamw_rows(g_in_t.reshape(SHARD_W * D_MODEL // 128, 128), flat(w_in), flat(m_w_in),
                                         flat(v_w_in), 720, "adamw_in")
    g_w_in, d_w_in, nm_w_in, nv_w_in = g_in_t.T, unflat(d_flat), unflat(m_flat), unflat(v_flat)

    pieces = [dng_p, dbf_p, dbb_p, dgg_p, dcb_p, dfg_p[0], dwf_p[0:RANK], dwb_p[RANK:2 * RANK], dcw_p[0:3], loss_p[0]]
    tot = _unpack(_allreduce_small(_pack(pieces)), pieces)
    g_norm_g, g_b_gk_f, g_b_gk_b, g_gla, g_conv_b, g_final = tot[:6]
    g_wgk_f = lax.dynamic_slice_in_dim(tot[6], me * wgk_cols, wgk_cols, axis=1)[None]
    g_wgk_b = lax.dynamic_slice_in_dim(tot[7], me * wgk_cols, wgk_cols, axis=1)[None]
    g_conv_w = lax.dynamic_slice_in_dim(tot[8], me * 128, 128, axis=1)[None]
    loss = tot[9][0]

    small_g = [g_norm_g, g_b_gk_f, g_b_gk_b, g_gla, g_conv_b, g_final, g_wgk_f, g_wgk_b, g_conv_w]
    small_w = [norm_g, b_gk_f, b_gk_b, gla_norm_g, conv_b, final_g, w_gk_f, w_gk_b, conv_w]
    small_m = [m_norm_g, m_b_gk_f, m_b_gk_b, m_gla_norm_g, m_conv_b, m_final_g, m_w_gk_f, m_w_gk_b, m_conv_w]
    small_v = [v_norm_g, v_b_gk_f, v_b_gk_b, v_gla_norm_g, v_conv_b, v_final_g, v_w_gk_f, v_w_gk_b, v_conv_w]
    d_s, m_s, v_s = _adamw_small(_pack(small_g), _pack(small_w), _pack(small_m), _pack(small_v))
    d_l, m_l, v_l = _unpack(d_s, small_w), _unpack(m_s, small_w), _unpack(v_s, small_w)

    def ordered(sm, big_in, big_out):
        return [sm[0], big_in[None], sm[6], sm[1], sm[7], sm[2], sm[3], sm[8], sm[4], big_out[None], sm[5]]

    grads = ordered(small_g, g_w_in, g_w_out)
    deltas = ordered(d_l, d_w_in, d_w_out)
    new_m = ordered(m_l, nm_w_in, nm_w_out)
    new_v = ordered(v_l, nv_w_in, nv_w_out)
    return (loss, grad_x2d[None], *grads, *deltas, *new_m, *new_v)
```

```python
import jax
import jax.numpy as jnp
from jax import lax
from jax.experimental import pallas as pl
from jax.experimental.pallas import tpu as pltpu

F32 = jnp.float32
BF16 = jnp.bfloat16
MESH = pl.DeviceIdType.MESH

N_DEV = 8
D_MODEL = 1024
HEADS = 4
DK = 128
DV = 256
QK_W = HEADS * DK
V_W = HEADS * DV
CONV_W = 1024
MIX_W = V_W + CONV_W
CHUNK = 64
RANK = 16
IN_W = 7200
SHARD_W = IN_W // N_DEV
MAIN_W = 7168
LR_W = 128
OFF_Q, OFF_K, OFF_V, OFF_ZA, OFF_B, OFF_ZC, OFF_C, OFF_H = 0, 512, 1024, 2048, 3072, 4096, 5120, 6144
QKV_W, GATES_W, CH_W = 2048, 3072, 2048
NAT_ZA, NAT_LR, NAT_B, NAT_C, NAT_ZC = 2048, 3072, 3104, 4128, 6176
EPS = 1e-6
GATE_SCALE = 1.0 / 16.0
QSCALE = DK ** -0.5
REF_F, LAST_F = CHUNK // 2, CHUNK - 1
REF_B, LAST_B = CHUNK - 1 - CHUNK // 2, 0

ADAM_LR = 0.001
ADAM_B1 = 0.9
ADAM_B2 = 0.999
ADAM_EPS = 1e-08
ADAM_WD = 0.01
ADAM_STEP = 10

VMEM_LIMIT = 56 * 1024 * 1024


def _cparams(*sem):
    return pltpu.CompilerParams(dimension_semantics=sem, vmem_limit_bytes=VMEM_LIMIT)


def _dot(a, b):
    return jnp.dot(a, b, preferred_element_type=F32)


def _dot_nt(a, b):
    return lax.dot_general(a, b, (((1,), (1,)), ((), ())), preferred_element_type=F32)


def _dot_tn(a, b):
    return lax.dot_general(a, b, (((0,), (0,)), ((), ())), preferred_element_type=F32)


def _sigmoid(z):
    return jax.nn.sigmoid(z)


def _position():
    return lax.axis_index("x"), lax.axis_index("y"), lax.axis_index("c")


def _blk(px, py, pc):
    return 4 * px + 2 * py + pc


def _two_level_gather(outs, send_sems, recv_sems):
    x, y, c = _position()
    me, sibling = (x, y, c), (x, y, 1 - c)
    chips = [(1 - x, y), (x, 1 - y), (1 - x, 1 - y)]
    n = len(outs)

    def copy(a, k, block, to):
        ref = outs[a].at[_blk(*block)]
        return pltpu.make_async_remote_copy(src_ref=ref, dst_ref=ref, send_sem=send_sems.at[a * 7 + k],
                                            recv_sem=recv_sems.at[a * 7 + k], device_id=to, device_id_type=MESH)

    first = []
    for a in range(n):
        first.append(copy(a, 0, me, sibling))
        first += [copy(a, 1 + j, me, (*chip, c)) for j, chip in enumerate(chips)]
    for cp in first:
        cp.start()
    passed = []
    for j, chip in enumerate(chips):
        for a in range(n):
            copy(a, 1 + j, (*chip, c), me).wait_recv()
            fwd = copy(a, 4 + j, (*chip, c), sibling)
            fwd.start()
            passed.append(fwd)
    for a in range(n):
        copy(a, 0, sibling, me).wait_recv()
    for j, chip in enumerate(chips):
        for a in range(n):
            copy(a, 4 + j, (*chip, 1 - c), me).wait_recv()
    for cp in first + passed:
        cp.wait_send()


def _allgather_w_in(w_in_t):
    def body(win_ref, win_all, send_sems, recv_sems):
        win_all[_blk(*_position())] = win_ref[...].astype(BF16)
        _two_level_gather((win_all,), send_sems, recv_sems)

    vmem = pl.BlockSpec(memory_space=pltpu.VMEM)
    return pl.pallas_call(
        body, name="allgather_w_in",
        out_shape=jax.ShapeDtypeStruct((N_DEV,) + w_in_t.shape, BF16),
        in_specs=[vmem], out_specs=vmem,
        scratch_shapes=[pltpu.SemaphoreType.DMA((7,)), pltpu.SemaphoreType.DMA((7,))],
        compiler_params=pltpu.CompilerParams(vmem_limit_bytes=VMEM_LIMIT),
    )(w_in_t)


def _peer_copies(srcs, outs, send_sems, recv_sems):
    x, y, c = _position()
    me = _blk(x, y, c)
    copies = []
    for a, (src, out) in enumerate(zip(srcs, outs)):
        k = 0
        for dx in (0, 1):
            for dy in (0, 1):
                for dc in (0, 1):
                    if dx + dy + dc == 0:
                        continue
                    peer = (1 - x if dx else x, 1 - y if dy else y, 1 - c if dc else c)
                    copies.append(pltpu.make_async_remote_copy(
                        src_ref=src, dst_ref=out.at[me], send_sem=send_sems.at[a * 7 + k],
                        recv_sem=recv_sems.at[a * 7 + k], device_id=peer, device_id_type=MESH))
                    k += 1
    return copies


def _chip_copies(ins, outs, send_sems, recv_sems):
    x, y, c = _position()
    chips = [(1 - x, y), (x, 1 - y), (1 - x, 1 - y)]
    copies = []
    for a in range(len(ins)):
        for j, (px, py) in enumerate(chips):
            copies.append(pltpu.make_async_remote_copy(
                src_ref=ins[a].at[2 * px + py], dst_ref=outs[a].at[j],
                send_sem=send_sems.at[a * 3 + j], recv_sem=recv_sems.at[a * 3 + j],
                device_id=(px, py, c), device_id_type=MESH))
    return copies


def _allreduce_small(part):
    def body(p_ref, tot_ref, all_ref, send_sems, recv_sems):
        mine = _blk(*_position())
        all_ref[mine] = p_ref[...]
        _two_level_gather((all_ref,), send_sems, recv_sems)
        acc = all_ref[0]
        for d in range(1, N_DEV):
            acc = acc + all_ref[d]
        tot_ref[...] = acc

    vmem = pl.BlockSpec(memory_space=pltpu.VMEM)
    return pl.pallas_call(
        body, name="allreduce_small",
        out_shape=jax.ShapeDtypeStruct(part.shape, F32),
        in_specs=[vmem], out_specs=vmem,
        scratch_shapes=[pltpu.VMEM((N_DEV,) + part.shape, F32),
                        pltpu.SemaphoreType.DMA((7,)), pltpu.SemaphoreType.DMA((7,))],
        compiler_params=pltpu.CompilerParams(vmem_limit_bytes=VMEM_LIMIT),
    )(part)


def _exchange_sibling(parts):
    n = len(parts)

    def body(*refs):
        ins, outs = refs[:n], refs[n:2 * n]
        send_sems, recv_sems = refs[2 * n], refs[2 * n + 1]
        x, y, c = _position()
        sibling = (x, y, 1 - c)
        copies = []
        for a in range(n):
            for k in range(4):
                copies.append(pltpu.make_async_remote_copy(
                    src_ref=ins[a].at[2 * k + (1 - c)], dst_ref=outs[a].at[k],
                    send_sem=send_sems.at[a * 4 + k], recv_sem=recv_sems.at[a * 4 + k],
                    device_id=sibling, device_id_type=MESH))
        for cp in copies:
            cp.start()
        for cp in copies:
            cp.wait_recv()
        for cp in copies:
            cp.wait_send()

    hbm = pl.BlockSpec(memory_space=pl.ANY)
    return pl.pallas_call(
        body, name="exchange_sibling",
        out_shape=tuple(jax.ShapeDtypeStruct((4,) + p.shape[1:], F32) for p in parts),
        in_specs=[hbm] * n, out_specs=tuple([hbm] * n),
        scratch_shapes=[pltpu.SemaphoreType.DMA((4 * n,)), pltpu.SemaphoreType.DMA((4 * n,))],
    )(*parts)


def _chip_sums(part, from_sibling, core, tc, name):
    _, rows, cols = part.shape

    def body(core_ref, p_ref, s_ref, o_ref):
        o_ref[...] = (p_ref[...] + s_ref[...]).astype(BF16)

    return pl.pallas_call(
        body, name=name,
        out_shape=jax.ShapeDtypeStruct((4, rows, cols), BF16),
        grid_spec=pltpu.PrefetchScalarGridSpec(
            num_scalar_prefetch=1, grid=(4, cols // tc),
            in_specs=[pl.BlockSpec((1, rows, tc), lambda k, j, core_ref: (2 * k + core_ref[0], 0, j)),
                      pl.BlockSpec((1, rows, tc), lambda k, j, core_ref: (k, 0, j))],
            out_specs=pl.BlockSpec((1, rows, tc), lambda k, j, core_ref: (k, 0, j))),
        compiler_params=_cparams("arbitrary", "arbitrary"),
    )(core, part, from_sibling)


def _sum_chips(s_ref, r_ref):
    f = lambda a: a.astype(F32)
    return ((f(s_ref[0]) + f(r_ref[0])) + f(r_ref[1])) + f(r_ref[2])


def _final_sum(sums, from_chips, chip, tc, name):
    _, rows, cols = sums.shape

    def body(chip_ref, s_ref, r_ref, g_out):
        g_out[...] = _sum_chips(s_ref, r_ref)

    return pl.pallas_call(
        body, name=name,
        out_shape=jax.ShapeDtypeStruct((rows, cols), F32),
        grid_spec=pltpu.PrefetchScalarGridSpec(
            num_scalar_prefetch=1, grid=(cols // tc,),
            in_specs=[pl.BlockSpec((1, rows, tc), lambda j, chip_ref: (chip_ref[0], 0, j)),
                      pl.BlockSpec((3, rows, tc), lambda j, chip_ref: (0, 0, j))],
            out_specs=pl.BlockSpec((rows, tc), lambda j, chip_ref: (0, j))),
        compiler_params=_cparams("arbitrary"),
    )(chip, sums, from_chips)


def _adamw_rows(g, w, m, v, tr, name):
    rows, cols = g.shape

    def body(g_ref, w_ref, m_ref, v_ref, d_out, m_out, v_out):
        delta, m_new, v_new = _adamw(w_ref[...], g_ref[...], m_ref[...], v_ref[...])
        d_out[...] = delta
        m_out[...] = m_new
        v_out[...] = v_new

    tile = pl.BlockSpec((tr, cols), lambda r: (r, 0))
    shp = jax.ShapeDtypeStruct((rows, cols), F32)
    return pl.pallas_call(
        body, name=name, out_shape=(shp, shp, shp), grid=(rows // tr,),
        in_specs=[tile] * 4, out_specs=(tile, tile, tile),
        compiler_params=_cparams("arbitrary"),
    )(g, w, m, v)


def _adamw(w, g, m, v):
    m = ADAM_B1 * m + (1.0 - ADAM_B1) * g
    v = ADAM_B2 * v + (1.0 - ADAM_B2) * (g * g)
    m_hat = m / (1.0 - ADAM_B1 ** ADAM_STEP)
    v_hat = v / (1.0 - ADAM_B2 ** ADAM_STEP)
    delta = -ADAM_LR * (m_hat / (jnp.sqrt(v_hat) + ADAM_EPS) + ADAM_WD * w)
    return delta, m, v


def _final_sum_adamw(sums, from_chips, chip, w, m, v, tr, name):
    rows, cols = w.shape

    def body(chip_ref, s_ref, r_ref, w_ref, m_ref, v_ref, g_out, d_out, m_out, v_out):
        g = _sum_chips(s_ref, r_ref)
        delta, m_new, v_new = _adamw(w_ref[...], g, m_ref[...], v_ref[...])
        g_out[...] = g
        d_out[...] = delta
        m_out[...] = m_new
        v_out[...] = v_new

    tile = pl.BlockSpec((tr, cols), lambda r, chip_ref: (r, 0))
    shp = jax.ShapeDtypeStruct((rows, cols), F32)
    return pl.pallas_call(
        body, name=name,
        out_shape=(shp, shp, shp, shp),
        grid_spec=pltpu.PrefetchScalarGridSpec(
            num_scalar_prefetch=1, grid=(rows // tr,),
            in_specs=[pl.BlockSpec((1, tr, cols), lambda r, chip_ref: (chip_ref[0], r, 0)),
                      pl.BlockSpec((3, tr, cols), lambda r, chip_ref: (0, r, 0)),
                      tile, tile, tile],
            out_specs=(tile, tile, tile, tile)),
        compiler_params=_cparams("arbitrary"),
    )(chip, sums, from_chips, w, m, v)


def _adamw_small(g, w, m, v):
    def body(g_ref, w_ref, m_ref, v_ref, d_out, m_out, v_out):
        delta, m_new, v_new = _adamw(w_ref[...], g_ref[...], m_ref[...], v_ref[...])
        d_out[...] = delta
        m_out[...] = m_new
        v_out[...] = v_new

    vmem = pl.BlockSpec(memory_space=pltpu.VMEM)
    shp = jax.ShapeDtypeStruct(g.shape, F32)
    return pl.pallas_call(body, name="adamw_small", out_shape=(shp, shp, shp),
                          in_specs=[vmem] * 4, out_specs=(vmem, vmem, vmem))(g, w, m, v)


def _inproj(x2d, norm_g, w_main, w_lr, w_out_s, small_s, tm, tn):
    seq = x2d.shape[0]
    ni, nj = seq // tm, MAIN_W // tn
    first_sweep = lambda j, i: jnp.where(j == 0, i, ni - 1)

    def body(x_ref, g_ref, w_ref, wlr_ref, wout_ref, sm_ref, proj_ref, lr_ref, ht_ref, wout_all, sm_all,
             h_all, wout_b, sm_b, send_sems, recv_sems, local_sems):
        j, i = pl.program_id(0), pl.program_id(1)
        rows = pl.ds(pl.multiple_of(i * tm, tm), tm)
        me = _blk(*_position())

        def gather_copies():
            mine = [pltpu.make_async_copy(wout_b, wout_all.at[me], local_sems.at[0]),
                    pltpu.make_async_copy(sm_b, sm_all.at[me], local_sems.at[1])]
            return mine, _peer_copies((wout_b, sm_b), (wout_all, sm_all), send_sems, recv_sems)

        @pl.when(jnp.logical_and(j == 0, i == 0))
        def _():
            wout_b[...] = wout_ref[...].astype(BF16)
            sm_b[...] = sm_ref[...]
            mine, remote = gather_copies()
            for cp in mine + remote:
                cp.start()

        @pl.when(j == 0)
        def _():
            xv = x_ref[...]
            r = lax.rsqrt(jnp.mean(xv * xv, axis=-1, keepdims=True) + EPS)
            h = (xv * r) * g_ref[...]
            hb = h.astype(BF16)
            h_all[rows, :] = hb
            ht_ref[...] = h.T.astype(BF16)
            lr_ref[...] = _dot_nt(hb, wlr_ref[...])

        proj_ref[...] = _dot_nt(h_all[rows, :], w_ref[...]).astype(BF16)

        @pl.when(jnp.logical_and(j == nj - 1, i == ni - 1))
        def _():
            mine, remote = gather_copies()
            for cp in remote:
                cp.wait_recv()
            for cp in remote:
                cp.wait_send()
            for cp in mine:
                cp.wait()

    const = lambda shape: pl.BlockSpec(shape, lambda j, i: (0,) * len(shape))
    hbm = pl.BlockSpec(memory_space=pl.ANY)
    return pl.pallas_call(
        body, name="inproj",
        out_shape=(jax.ShapeDtypeStruct((seq, MAIN_W), BF16), jax.ShapeDtypeStruct((seq, LR_W), F32),
                   jax.ShapeDtypeStruct((D_MODEL, seq), BF16),
                   jax.ShapeDtypeStruct((N_DEV,) + w_out_s.shape, BF16),
                   jax.ShapeDtypeStruct((N_DEV,) + small_s.shape, F32)),
        grid=(nj, ni),
        in_specs=[pl.BlockSpec((tm, D_MODEL), lambda j, i: (first_sweep(j, i), 0)),
                  const((1, D_MODEL)),
                  pl.BlockSpec((tn, D_MODEL), lambda j, i: (j, 0)),
                  const((LR_W, D_MODEL)), const(w_out_s.shape), const(small_s.shape)],
        out_specs=(pl.BlockSpec((tm, tn), lambda j, i: (i, j)),
                   pl.BlockSpec((tm, LR_W), lambda j, i: (first_sweep(j, i), 0)),
                   pl.BlockSpec((D_MODEL, tm), lambda j, i: (0, first_sweep(j, i))), hbm, hbm),
        scratch_shapes=[pltpu.VMEM((seq, D_MODEL), BF16), pltpu.VMEM(w_out_s.shape, BF16),
                        pltpu.VMEM(small_s.shape, F32), pltpu.SemaphoreType.DMA((14,)),
                        pltpu.SemaphoreType.DMA((14,)), pltpu.SemaphoreType.DMA((2,))],
        compiler_params=_cparams("arbitrary", "arbitrary"),
    )(x2d, norm_g, w_main, w_lr, w_out_s, small_s)


def _block_masks(tt):
    row = lax.broadcasted_iota(jnp.int32, (tt, tt), 0)
    col = lax.broadcasted_iota(jnp.int32, (tt, tt), 1)
    same = jnp.right_shift(row, 6) == jnp.right_shift(col, 6)
    return (jnp.logical_and(same, col <= row), jnp.logical_and(same, col >= row), jnp.logical_and(same, col > row))


def _chunk_column_mask(tt):
    nc = tt // CHUNK
    row = lax.broadcasted_iota(jnp.int32, (tt, nc * DK), 0)
    col = lax.broadcasted_iota(jnp.int32, (tt, nc * DK), 1)
    return jnp.right_shift(row, 6) == jnp.right_shift(col, 7)


def _dot_split3(ones_mat, x):
    x1 = x.astype(BF16)
    r1 = x - x1.astype(F32)
    x2 = r1.astype(BF16)
    x3 = (r1 - x2.astype(F32)).astype(BF16)
    return (_dot(ones_mat, x3) + _dot(ones_mat, x2)) + _dot(ones_mat, x1)


def _log_gate(logits):
    return (jnp.minimum(logits, 0.0) - jnp.log1p(jnp.exp(-jnp.abs(logits)))) * GATE_SCALE


def _chunked(mask, x, nc):
    wide = jnp.concatenate([x] * nc, axis=1)
    return jnp.where(mask, wide, jnp.zeros_like(wide))


def _gla_fwd(proj, lr, wgk_f, wgk_b, bgk_f, bgk_b, tt):
    seq = proj.shape[0]
    nb, nc, nch = seq // tt, tt // CHUNK, seq // CHUNK

    def body(qf, kf, vf, lrf, qb, kb, vb, lrb, wf, wb, bf, bb, of, ob, stf, stb, s_scr, qs_s, ks_s, qin_s, kout_s):
        @pl.when(pl.program_id(0) == 0)
        def _():
            s_scr[...] = jnp.zeros(s_scr.shape, F32)

        low, upp, sup = _block_masks(tt)
        kmask = _chunk_column_mask(tt)
        dirs = ((qf, kf, vf, lrf, wf, bf, of, stf, low, low, REF_F, LAST_F, list(range(nc))),
                (qb, kb, vb, lrb, wb, bb, ob, stb, upp, sup, REF_B, LAST_B, list(reversed(range(nc)))))
        for d, (q_r, k_r, v_r, lr_r, w_r, b_r, o_r, st_r, cum, mask, ref, last, order) in enumerate(dirs):
            logits = _dot(lr_r[...].astype(BF16), w_r[...]) + b_r[...]
            b = _dot_split3(cum.astype(BF16), _log_gate(logits))
            decs = []
            for c in range(nc):
                rows = slice(c * CHUNK, (c + 1) * CHUNK)
                bc = b[rows]
                b_ref, b_last = bc[ref:ref + 1], bc[last:last + 1]
                qc = q_r[rows, :].astype(F32) * QSCALE
                kc = k_r[rows, :].astype(F32)
                qs_s[rows, :] = (qc * jnp.exp(bc - b_ref)).astype(BF16)
                ks_s[rows, :] = (kc * jnp.exp(b_ref - bc)).astype(BF16)
                qin_s[rows, :] = (qc * jnp.exp(bc)).astype(BF16)
                kout_s[rows, :] = (kc * jnp.exp(b_last - bc)).astype(BF16)
                decs.append(jnp.exp(b_last))
            for h in range(HEADS):
                ksl = slice(h * DK, (h + 1) * DK)
                vsl = slice(h * DV, (h + 1) * DV)
                v = v_r[:, vsl].astype(BF16)
                att = jnp.where(mask, _dot_nt(qs_s[:, ksl], ks_s[:, ksl]), 0.0).astype(BF16)
                o_intra = _dot(att, v)
                kv_t = _dot_tn(v, _chunked(kmask, kout_s[:, ksl], nc))
                st = s_scr[d * HEADS + h]
                for c in order:
                    rows = slice(c * CHUNK, (c + 1) * CHUNK)
                    stb = st.astype(BF16)
                    st_r[c, h] = stb
                    o_r[rows, vsl] = (o_intra[rows] + _dot_nt(qin_s[rows, ksl], stb)).astype(BF16)
                    st = st * decs[c][:, ksl] + kv_t[:, c * DK:(c + 1) * DK]
                s_scr[d * HEADS + h] = st

    fw = lambda i: (i, 0)
    bw = lambda i: (nb - 1 - i, 0)
    const = lambda i: (0, 0)

    def tok_specs(m):
        return [pl.BlockSpec((tt, QK_W), lambda i: (m(i)[0], OFF_Q // QK_W)),
                pl.BlockSpec((tt, QK_W), lambda i: (m(i)[0], OFF_K // QK_W)),
                pl.BlockSpec((tt, V_W), lambda i: (m(i)[0], OFF_V // V_W)),
                pl.BlockSpec((tt, LR_W), m)]

    st_shape = jax.ShapeDtypeStruct((nch, HEADS, DV, DK), BF16)
    o_shape = jax.ShapeDtypeStruct((seq, V_W), BF16)
    operand = pltpu.VMEM((tt, QK_W), BF16)
    return pl.pallas_call(
        body, name="gla_fwd",
        out_shape=(o_shape, o_shape, st_shape, st_shape),
        grid=(nb,),
        in_specs=tok_specs(fw) + tok_specs(bw) + [
            pl.BlockSpec((LR_W, QK_W), const), pl.BlockSpec((LR_W, QK_W), const),
            pl.BlockSpec((1, QK_W), const), pl.BlockSpec((1, QK_W), const)],
        out_specs=(pl.BlockSpec((tt, V_W), fw), pl.BlockSpec((tt, V_W), bw),
                   pl.BlockSpec((nc, HEADS, DV, DK), lambda i: (i, 0, 0, 0)),
                   pl.BlockSpec((nc, HEADS, DV, DK), lambda i: (nb - 1 - i, 0, 0, 0))),
        scratch_shapes=[pltpu.VMEM((2 * HEADS, DV, DK), F32), operand, operand, operand, operand],
        compiler_params=_cparams("arbitrary"),
    )(proj, proj, proj, lr, proj, proj, proj, lr, wgk_f, wgk_b, bgk_f, bgk_b)


def _head_norm(o, gain):
    outs, rinv = [], []
    for h in range(HEADS):
        oh = o[:, h * DV:(h + 1) * DV]
        r = lax.rsqrt(jnp.mean(oh * oh, axis=-1, keepdims=True) + EPS)
        outs.append((oh * r) * gain)
        rinv.append(r)
    return jnp.concatenate(outs, axis=1), rinv


def _shift_rows(u, prev_row, next_row):
    n = u.shape[0]
    row = lax.broadcasted_iota(jnp.int32, (n, 1), 0)
    up = jnp.where(row == 0, prev_row, pltpu.roll(u, 1, 0))
    un = jnp.where(row == n - 1, next_row, pltpu.roll(u, n - 1, 0))
    return up, un


HALO = 16


def _halo_specs(tm, seq, col_block):
    per = tm // HALO
    last = seq // HALO - 1
    return [pl.BlockSpec((HALO, CONV_W), lambda i: (jnp.maximum(i * per - 1, 0), col_block)),
            pl.BlockSpec((HALO, CONV_W), lambda i: (jnp.minimum((i + 1) * per, last), col_block))]


def _f32(ref):
    return ref[...].astype(F32)


def _last_row(ref):
    return ref[HALO - 1:HALO, :].astype(F32)


def _first_row(ref):
    return ref[0:1, :].astype(F32)


def _mix_out_loss(o_f, o_b, proj, x2d, tgt, gla_g, conv_w, conv_b, w_out, final_g, tm):
    seq = x2d.shape[0]
    nt = seq // tm

    def body(of, ob, za, bg, cg, hc, zc, cprev, cnext, hprev, hnext, x_ref, t_ref, gg, cw, cb, wo, fg,
             yt_ref, conv_ref, dx2_ref, dx2b_ref, loss_ref, dfg_ref):
        i = pl.program_id(0)

        @pl.when(i == 0)
        def _():
            loss_ref[...] = jnp.zeros(loss_ref.shape, F32)
            dfg_ref[...] = jnp.zeros(dfg_ref.shape, F32)

        on, _ = _head_norm(_f32(of) + _f32(ob), gg[...])
        zav = _f32(za)
        y_a = on * (zav * _sigmoid(zav))
        u = _f32(cg) * _f32(hc)
        prev_row = jnp.where(i > 0, _last_row(cprev) * _last_row(hprev), 0.0)
        next_row = jnp.where(i < nt - 1, _first_row(cnext) * _first_row(hnext), 0.0)
        up, un = _shift_rows(u, prev_row, next_row)
        conv = (cw[0:1, :] * up + cw[1:2, :] * u + cw[2:3, :] * un) + cb[...]
        conv_ref[...] = conv.astype(BF16)
        zcv = _f32(zc)
        y_c = _f32(bg) * conv * (zcv * _sigmoid(zcv))
        y = jnp.concatenate([y_a, y_c], axis=1)
        yt_ref[...] = y.T.astype(BF16)
        x2 = x_ref[...] + _dot(y.astype(BF16), wo[...])
        r = lax.rsqrt(jnp.mean(x2 * x2, axis=-1, keepdims=True) + EPS)
        xn = x2 * r
        err = xn * fg[...] - t_ref[...]
        loss_ref[...] += 0.5 * jnp.sum(jnp.mean(err * err, axis=-1, keepdims=True))
        dyf = err * (1.0 / D_MODEL)
        dfg_ref[...] += jnp.sum(dyf * xn, axis=0, keepdims=True)
        dxn = dyf * fg[...]
        dx2 = r * dxn - xn * (r * jnp.mean(dxn * xn, axis=-1, keepdims=True))
        dx2_ref[...] = dx2
        dx2b_ref[...] = dx2.astype(BF16)

    def col(off):
        return pl.BlockSpec((tm, CONV_W), lambda i: (i, off // CONV_W))

    rowt = pl.BlockSpec((tm, D_MODEL), lambda i: (i, 0))
    const = lambda shape: pl.BlockSpec(shape, lambda i: (0, 0))
    return pl.pallas_call(
        body, name="mix_out_loss",
        out_shape=(jax.ShapeDtypeStruct((MIX_W, seq), BF16), jax.ShapeDtypeStruct((seq, CONV_W), BF16),
                   jax.ShapeDtypeStruct((seq, D_MODEL), F32), jax.ShapeDtypeStruct((seq, D_MODEL), BF16),
                   jax.ShapeDtypeStruct((8, 128), F32), jax.ShapeDtypeStruct((1, D_MODEL), F32)),
        grid=(nt,),
        in_specs=[rowt, rowt, col(OFF_ZA), col(OFF_B), col(OFF_C), col(OFF_H), col(OFF_ZC)]
        + _halo_specs(tm, seq, OFF_C // CONV_W) + _halo_specs(tm, seq, OFF_H // CONV_W)
        + [rowt, rowt, const((1, DV)), const((8, CONV_W)), const((1, CONV_W)), const((MIX_W, D_MODEL)),
           const((1, D_MODEL))],
        out_specs=(pl.BlockSpec((MIX_W, tm), lambda i: (0, i)), rowt, rowt, rowt, const((8, 128)),
                   const((1, D_MODEL))),
        compiler_params=_cparams("arbitrary"),
    )(o_f, o_b, proj, proj, proj, proj, proj, proj, proj, proj, proj, x2d, tgt, gla_g, conv_w, conv_b, w_out, final_g)


def _dsilu(z, s):
    return s * (1.0 + z * (1.0 - s))


def _mix_bwd(dx2b, o_f, o_b, proj, conv, gla_g, w_out, tm):
    seq = dx2b.shape[0]

    def body(dx, of, ob, za, bg, zc, cv, gg, wo, dg_ref, do_ref, dconv_ref, dgg_ref, dcb_ref):
        @pl.when(pl.program_id(0) == 0)
        def _():
            dgg_ref[...] = jnp.zeros(dgg_ref.shape, F32)
            dcb_ref[...] = jnp.zeros(dcb_ref.shape, F32)

        dy = _dot_nt(dx[...], wo[...])
        dy_a, dy_c = dy[:, :V_W], dy[:, V_W:]
        zcv, bgv, convv = _f32(zc), _f32(bg), _f32(cv)
        sc = _sigmoid(zcv)
        szc = zcv * sc
        dg_ref[:, CONV_W:2 * CONV_W] = (dy_c * convv * szc).astype(BF16)
        dconv = dy_c * bgv * szc
        dconv_ref[...] = dconv.astype(BF16)
        dcb_ref[...] += jnp.sum(dconv, axis=0, keepdims=True)
        dg_ref[:, 2 * CONV_W:] = (dy_c * bgv * convv * _dsilu(zcv, sc)).astype(BF16)

        o = _f32(of) + _f32(ob)
        gain = gg[...]
        on, rinv = _head_norm(o, gain)
        zav = _f32(za)
        sa = _sigmoid(zav)
        dg_ref[:, :CONV_W] = (dy_a * on * _dsilu(zav, sa)).astype(BF16)
        don = dy_a * (zav * sa)
        dgg = jnp.zeros((1, DV), F32)
        dos = []
        for h in range(HEADS):
            sl = slice(h * DV, (h + 1) * DV)
            oh, r, dh = o[:, sl], rinv[h], don[:, sl]
            ohn = oh * r
            dgg = dgg + jnp.sum(dh * ohn, axis=0, keepdims=True)
            dn = dh * gain
            dos.append(r * dn - ohn * (r * jnp.mean(dn * ohn, axis=-1, keepdims=True)))
        dgg_ref[...] += dgg
        do_ref[...] = jnp.concatenate(dos, axis=1).astype(BF16)

    def col(off):
        return pl.BlockSpec((tm, CONV_W), lambda i: (i, off // CONV_W))

    rowt = pl.BlockSpec((tm, D_MODEL), lambda i: (i, 0))
    const = lambda shape: pl.BlockSpec(shape, lambda i: (0, 0))
    return pl.pallas_call(
        body, name="mix_bwd",
        out_shape=(jax.ShapeDtypeStruct((seq, GATES_W), BF16), jax.ShapeDtypeStruct((seq, V_W), BF16),
                   jax.ShapeDtypeStruct((seq, CONV_W), BF16),
                   jax.ShapeDtypeStruct((1, DV), F32), jax.ShapeDtypeStruct((1, CONV_W), F32)),
        grid=(seq // tm,),
        in_specs=[rowt, rowt, rowt, col(OFF_ZA), col(OFF_B), col(OFF_ZC), rowt, const((1, DV)),
                  const((MIX_W, D_MODEL))],
        out_specs=(pl.BlockSpec((tm, GATES_W), lambda i: (i, 0)), rowt, rowt, const((1, DV)), const((1, CONV_W))),
        compiler_params=_cparams("arbitrary"),
    )(dx2b, o_f, o_b, proj, proj, proj, conv, gla_g, w_out)


def _conv_bwd(dconv, proj, conv_w, tm):
    seq = dconv.shape[0]
    nt = seq // tm

    def body(dc_in, dprev, dnext, cg, hc, cprev, cnext, hprev, hnext, cw, dch_ref, dcw_ref):
        i = pl.program_id(0)

        @pl.when(i == 0)
        def _():
            dcw_ref[...] = jnp.zeros(dcw_ref.shape, F32)

        first, lastt = i > 0, i < nt - 1
        dcv = _f32(dc_in)
        d_up, d_un = _shift_rows(dcv, jnp.where(first, _last_row(dprev), 0.0), jnp.where(lastt, _first_row(dnext), 0.0))
        cgv, hcv = _f32(cg), _f32(hc)
        u = cgv * hcv
        u_up, u_un = _shift_rows(u, jnp.where(first, _last_row(cprev) * _last_row(hprev), 0.0),
                                 jnp.where(lastt, _first_row(cnext) * _first_row(hnext), 0.0))
        du = cw[0:1, :] * d_un + cw[1:2, :] * dcv + cw[2:3, :] * d_up
        dch_ref[:, :CONV_W] = (du * hcv).astype(BF16)
        dch_ref[:, CONV_W:] = (du * cgv).astype(BF16)
        dcw_ref[0:1, :] += jnp.sum(dcv * u_up, axis=0, keepdims=True)
        dcw_ref[1:2, :] += jnp.sum(dcv * u, axis=0, keepdims=True)
        dcw_ref[2:3, :] += jnp.sum(dcv * u_un, axis=0, keepdims=True)

    def col(off):
        return pl.BlockSpec((tm, CONV_W), lambda i: (i, off // CONV_W))

    rowt = pl.BlockSpec((tm, CONV_W), lambda i: (i, 0))
    const = lambda shape: pl.BlockSpec(shape, lambda i: (0, 0))
    return pl.pallas_call(
        body, name="conv_bwd",
        out_shape=(jax.ShapeDtypeStruct((seq, CH_W), BF16), jax.ShapeDtypeStruct((8, CONV_W), F32)),
        grid=(nt,),
        in_specs=[rowt] + _halo_specs(tm, seq, 0) + [col(OFF_C), col(OFF_H)]
        + _halo_specs(tm, seq, OFF_C // CONV_W) + _halo_specs(tm, seq, OFF_H // CONV_W) + [const((8, CONV_W))],
        out_specs=(pl.BlockSpec((tm, CH_W), lambda i: (i, 0)), const((8, CONV_W))),
        compiler_params=_cparams("arbitrary"),
    )(dconv, dconv, dconv, proj, proj, proj, proj, proj, proj, conv_w)


def _gla_bwd(proj, lr, do, st_f, st_b, wgk_f, wgk_b, bgk_f, bgk_b, tt):
    seq = proj.shape[0]
    nb, nc = seq // tt, tt // CHUNK

    def body(qf, kf, vf, lrf, dof, stf, qb, kb, vb, lrb, dob, stb, wf, wb, bf, bb,
             dqkv_f, dlr_f, dqkv_b, dlr_b, dwf, dwb, dbf, dbb,
             ds_scr, eq_s, ek_s, ein_s, eout_s, qs_s, ks_s, qin_s, kout_s, db_s, lg_s):
        @pl.when(pl.program_id(0) == 0)
        def _():
            ds_scr[...] = jnp.zeros(ds_scr.shape, F32)
            for r in (dwf, dwb, dbf, dbb):
                r[...] = jnp.zeros(r.shape, F32)

        low, upp, sup = _block_masks(tt)
        kmask = _chunk_column_mask(tt)
        row = lax.broadcasted_iota(jnp.int32, (CHUNK, 1), 0)
        dirs = ((qf, kf, vf, lrf, dof, stf, wf, bf, dqkv_f, dlr_f, dwf, dbf,
                 low, upp, low, REF_F, LAST_F, list(reversed(range(nc)))),
                (qb, kb, vb, lrb, dob, stb, wb, bb, dqkv_b, dlr_b, dwb, dbb,
                 upp, low, sup, REF_B, LAST_B, list(range(nc))))
        for d, (q_r, k_r, v_r, lr_r, do_r, st_r, w_r, b_r, dqkv_r, dlr_r, dw_r, db_r,
                cum, cum_t, mask, ref, last, order) in enumerate(dirs):
            lrv = lr_r[...].astype(BF16)
            wv = w_r[...]
            logits = _dot(lrv, wv) + b_r[...]
            lg_s[...] = logits
            b = _dot_split3(cum.astype(BF16), _log_gate(logits))
            decs = []
            for c in range(nc):
                rows = slice(c * CHUNK, (c + 1) * CHUNK)
                bc = b[rows]
                b_ref, b_last = bc[ref:ref + 1], bc[last:last + 1]
                qc = q_r[rows, :].astype(F32) * QSCALE
                kc = k_r[rows, :].astype(F32)
                e_q, e_k, e_in, e_out = jnp.exp(bc - b_ref), jnp.exp(b_ref - bc), jnp.exp(bc), jnp.exp(b_last - bc)
                eq_s[rows, :], ek_s[rows, :], ein_s[rows, :], eout_s[rows, :] = e_q, e_k, e_in, e_out
                qs_s[rows, :] = (qc * e_q).astype(BF16)
                ks_s[rows, :] = (kc * e_k).astype(BF16)
                qin_s[rows, :] = (qc * e_in).astype(BF16)
                kout_s[rows, :] = (kc * e_out).astype(BF16)
                decs.append(jnp.exp(b_last))
            for h in range(HEADS):
                ksl = slice(h * DK, (h + 1) * DK)
                vsl = slice(h * DV, (h + 1) * DV)
                v = v_r[:, vsl].astype(BF16)
                dov = do_r[:, vsl].astype(BF16)
                qsb, ksb = qs_s[:, ksl], ks_s[:, ksl]
                att = jnp.where(mask, _dot_nt(qsb, ksb), 0.0).astype(BF16)
                datt = jnp.where(mask, _dot_nt(dov, v), 0.0).astype(BF16)
                dqs = _dot(datt, ksb)
                dks = _dot_tn(datt, qsb)
                dv_intra = _dot_tn(att, dov)
                g_t = _dot_tn(dov, _chunked(kmask, qin_s[:, ksl], nc))
                ds = ds_scr[d * HEADS + h]
                for c in order:
                    rows = slice(c * CHUNK, (c + 1) * CHUNK)
                    dsb = ds.astype(BF16)
                    s_prev = st_r[c, h]
                    dk_out = _dot(v[rows], dsb)
                    dq_in = _dot(dov[rows], s_prev)
                    dqkv_r[rows, OFF_V + h * DV:OFF_V + (h + 1) * DV] = dv_intra[rows] + _dot_nt(kout_s[rows, ksl], dsb)
                    dec = decs[c][:, ksl]
                    ddec = jnp.sum(ds * s_prev.astype(F32), axis=0, keepdims=True)
                    e_q, e_k, e_in, e_out = eq_s[rows, ksl], ek_s[rows, ksl], ein_s[rows, ksl], eout_s[rows, ksl]
                    qc = q_r[rows, ksl].astype(F32) * QSCALE
                    kc = k_r[rows, ksl].astype(F32)
                    dqs_c, dks_c = dqs[rows], dks[rows]
                    dqkv_r[rows, OFF_Q + h * DK:OFF_Q + (h + 1) * DK] = (dqs_c * e_q + dq_in * e_in) * QSCALE
                    dqkv_r[rows, OFF_K + h * DK:OFF_K + (h + 1) * DK] = dks_c * e_k + dk_out * e_out
                    kk = dk_out * (kc * e_out)
                    db = dqs_c * (qc * e_q) - dks_c * (kc * e_k) + dq_in * (qc * e_in) - kk
                    tail = jnp.sum(kk, axis=0, keepdims=True) + ddec * dec
                    db_s[rows, ksl] = db + jnp.where(row == last, tail, 0.0)
                    ds = ds * dec + g_t[:, c * DK:(c + 1) * DK]
                ds_scr[d * HEADS + h] = ds
            dg = _dot_split3(cum_t.astype(BF16), db_s[...])
            dlogit = (dg * GATE_SCALE) * _sigmoid(-lg_s[...])
            dlb = dlogit.astype(BF16)
            dlr_r[...] = _dot_nt(dlb, wv)
            dw_r[...] += _dot_tn(lrv, dlb)
            db_r[...] += jnp.sum(dlogit, axis=0, keepdims=True)

    fw = lambda i: (nb - 1 - i, 0)
    bw = lambda i: (i, 0)
    const = lambda i: (0, 0)

    def tok_specs(m):
        return [pl.BlockSpec((tt, QK_W), lambda i: (m(i)[0], OFF_Q // QK_W)),
                pl.BlockSpec((tt, QK_W), lambda i: (m(i)[0], OFF_K // QK_W)),
                pl.BlockSpec((tt, V_W), lambda i: (m(i)[0], OFF_V // V_W)),
                pl.BlockSpec((tt, LR_W), m),
                pl.BlockSpec((tt, V_W), m),
                pl.BlockSpec((nc, HEADS, DV, DK), lambda i: (m(i)[0], 0, 0, 0))]

    dqkv = jax.ShapeDtypeStruct((seq, QK_W + QK_W + V_W), F32)
    dlr = jax.ShapeDtypeStruct((seq, LR_W), F32)
    dw = jax.ShapeDtypeStruct((LR_W, QK_W), F32)
    dbias = jax.ShapeDtypeStruct((1, QK_W), F32)
    return pl.pallas_call(
        body, name="gla_bwd",
        out_shape=(dqkv, dlr, dqkv, dlr, dw, dw, dbias, dbias),
        grid=(nb,),
        in_specs=tok_specs(fw) + tok_specs(bw) + [
            pl.BlockSpec((LR_W, QK_W), const), pl.BlockSpec((LR_W, QK_W), const),
            pl.BlockSpec((1, QK_W), const), pl.BlockSpec((1, QK_W), const)],
        out_specs=(pl.BlockSpec((tt, QK_W + QK_W + V_W), fw), pl.BlockSpec((tt, LR_W), fw),
                   pl.BlockSpec((tt, QK_W + QK_W + V_W), bw), pl.BlockSpec((tt, LR_W), bw),
                   pl.BlockSpec((LR_W, QK_W), const), pl.BlockSpec((LR_W, QK_W), const),
                   pl.BlockSpec((1, QK_W), const), pl.BlockSpec((1, QK_W), const)),
        scratch_shapes=[pltpu.VMEM((2 * HEADS, DV, DK), F32)] + [pltpu.VMEM((tt, QK_W), F32)] * 4
        + [pltpu.VMEM((tt, QK_W), BF16)] * 4 + [pltpu.VMEM((tt, QK_W), F32)] * 2,
        compiler_params=_cparams("arbitrary"),
    )(proj, proj, proj, lr, do, st_f, proj, proj, proj, lr, do, st_b, wgk_f, wgk_b, bgk_f, bgk_b)


def _sum_directions(dqkv_f, dqkv_b, dlr_f, dlr_b, tm):
    seq = dqkv_f.shape[0]

    def body(a, b, la, lb, dp_out, dlr_out):
        dp_out[...] = (a[...] + b[...]).astype(BF16)
        dlr_out[...] = (la[...] + lb[...]).astype(BF16)

    rowt = pl.BlockSpec((tm, QKV_W), lambda i: (i, 0))
    lrt = pl.BlockSpec((tm, LR_W), lambda i: (i, 0))
    return pl.pallas_call(
        body, name="sum_directions",
        out_shape=(jax.ShapeDtypeStruct((seq, QKV_W), BF16), jax.ShapeDtypeStruct((seq, LR_W), BF16)),
        grid=(seq // tm,),
        in_specs=[rowt, rowt, lrt, lrt],
        out_specs=(rowt, lrt),
        compiler_params=_cparams("arbitrary"),
    )(dqkv_f, dqkv_b, dlr_f, dlr_b)


def _input_grad(dp_qkv, dp_gates, dp_ch, dlr, w_main, w_lr, x2d, norm_g, dx2, sums, tm):
    seq = x2d.shape[0]
    nt, n = seq // tm, len(sums)
    o_gates, o_ch = QKV_W, QKV_W + GATES_W

    def body(dq, dg, dc, dl, w, wl, x_ref, g_ref, dx2_ref, *rest):
        ins, (gx_ref, dng_ref), outs = rest[:n], rest[n:n + 2], rest[n + 2:2 * n + 2]
        send_sems, recv_sems = rest[2 * n + 2:]
        i = pl.program_id(0)

        @pl.when(i == 0)
        def _():
            for cp in _chip_copies(ins, outs, send_sems, recv_sems):
                cp.start()
            dng_ref[...] = jnp.zeros(dng_ref.shape, F32)

        dh = (_dot(dl[...], wl[...]) + _dot(dq[...], w[0:o_gates, :]) + _dot(dg[...], w[o_gates:o_ch, :])
              + _dot(dc[...], w[o_ch:MAIN_W, :]))
        xv = x_ref[...]
        r = lax.rsqrt(jnp.mean(xv * xv, axis=-1, keepdims=True) + EPS)
        xn = xv * r
        dng_ref[...] += jnp.sum(dh * xn, axis=0, keepdims=True)
        dn = dh * g_ref[...]
        gx_ref[...] = (r * dn - xn * (r * jnp.mean(dn * xn, axis=-1, keepdims=True))) + dx2_ref[...]

        @pl.when(i == nt - 1)
        def _():
            copies = _chip_copies(ins, outs, send_sems, recv_sems)
            for cp in copies:
                cp.wait_recv()
            for cp in copies:
                cp.wait_send()

    rowt = pl.BlockSpec((tm, D_MODEL), lambda i: (i, 0))
    seg = lambda width: pl.BlockSpec((tm, width), lambda i: (i, 0))
    resident = lambda rows: pl.BlockSpec((rows, D_MODEL), lambda i: (0, 0), pipeline_mode=pl.Buffered(1))
    hbm = pl.BlockSpec(memory_space=pl.ANY)
    return pl.pallas_call(
        body, name="input_grad",
        out_shape=(jax.ShapeDtypeStruct((seq, D_MODEL), F32), jax.ShapeDtypeStruct((1, D_MODEL), F32))
        + tuple(jax.ShapeDtypeStruct((3,) + s.shape[1:], s.dtype) for s in sums),
        grid=(nt,),
        in_specs=[seg(QKV_W), seg(GATES_W), seg(CH_W), seg(LR_W), resident(MAIN_W), resident(LR_W),
                  rowt, pl.BlockSpec((1, D_MODEL), lambda i: (0, 0)), rowt] + [hbm] * n,
        out_specs=(rowt, pl.BlockSpec((1, D_MODEL), lambda i: (0, 0))) + (hbm,) * n,
        scratch_shapes=[pltpu.SemaphoreType.DMA((3 * n,)), pltpu.SemaphoreType.DMA((3 * n,))],
        compiler_params=_cparams("arbitrary"),
    )(dp_qkv, dp_gates, dp_ch, dlr, w_main, w_lr, x2d, norm_g, dx2, *sums)


def _weight_grad(at, b, tn, tk, name):
    m, seq = at.shape
    n = b.shape[1]

    def body(a_ref, b_ref, o_ref):
        @pl.when(pl.program_id(1) == 0)
        def _():
            o_ref[...] = jnp.zeros(o_ref.shape, F32)

        o_ref[...] += _dot(a_ref[...], b_ref[...])

    return pl.pallas_call(
        body, name=name,
        out_shape=jax.ShapeDtypeStruct((m, n), F32),
        grid=(n // tn, seq // tk),
        in_specs=[pl.BlockSpec((m, tk), lambda j, k: (0, k)), pl.BlockSpec((tk, tn), lambda j, k: (k, j))],
        out_specs=pl.BlockSpec((m, tn), lambda j, k: (0, j)),
        compiler_params=_cparams("arbitrary", "arbitrary"),
    )(at, b)


def _weight_grad_t(at, b, tn, name):
    m, seq = at.shape
    n = b.shape[1]

    def body(a_ref, b_ref, o_ref, acc):
        acc[...] = _dot(a_ref[...], b_ref[...])
        o_ref[...] = acc[...].T

    return pl.pallas_call(
        body, name=name,
        out_shape=jax.ShapeDtypeStruct((n, m), F32),
        grid=(n // tn,),
        in_specs=[pl.BlockSpec((m, seq), lambda j: (0, 0), pipeline_mode=pl.Buffered(1)),
                  pl.BlockSpec((seq, tn), lambda j: (0, j))],
        out_specs=pl.BlockSpec((tn, m), lambda j: (j, 0)),
        scratch_shapes=[pltpu.VMEM((m, tn), F32)],
        compiler_params=_cparams("arbitrary"),
    )(at, b)


def _pad_rows(a, rows):
    return jnp.pad(a, ((0, rows - a.shape[0]), (0, 0)))


def _rows128(a):
    a = a.reshape(-1, 128)
    return _pad_rows(a, -(-a.shape[0] // 8) * 8)


def _pack(arrs):
    return jnp.concatenate([_rows128(a) for a in arrs], axis=0)


def _unpack(buf, like):
    out, start = [], 0
    for a in like:
        rows = a.size // 128
        out.append(buf[start:start + rows].reshape(a.shape))
        start += -(-rows // 8) * 8
    return out


def kernel(x, norm_g, w_in, w_gk_f, b_gk_f, w_gk_b, b_gk_b, gla_norm_g, conv_w, conv_b, w_out, final_g, loss_target, m_norm_g, m_w_in, m_w_gk_f, m_b_gk_f, m_w_gk_b, m_b_gk_b, m_gla_norm_g, m_conv_w, m_conv_b, m_w_out, m_final_g, v_norm_g, v_w_in, v_w_gk_f, v_b_gk_f, v_w_gk_b, v_b_gk_b, v_gla_norm_g, v_conv_w, v_conv_b, v_w_out, v_final_g):
    px, py, pc = _position()
    me = _blk(px, py, pc)
    seq = x.shape[1]
    x2d, tgt = x[0], loss_target[0]
    tm = min(512, seq)
    tt = min(256, seq)

    small_s = jnp.concatenate([jnp.concatenate([w_gk_f[0], w_gk_b[0]], axis=1), _pad_rows(conv_w[0], 8)], axis=0)
    w_nat = _allgather_w_in(w_in[0].T).reshape(IN_W, D_MODEL)
    w_main = jnp.concatenate([w_nat[:NAT_LR], w_nat[NAT_B:NAT_C], w_nat[NAT_ZC:], w_nat[NAT_C:NAT_ZC]], axis=0)
    w_lr = w_nat[NAT_LR:NAT_LR + LR_W]

    proj, lr, h_t, wout_all, small_all = _inproj(x2d, norm_g, w_main, w_lr, w_out[0], small_s, tm, 1024)
    w_out_full = wout_all.reshape(MIX_W, D_MODEL)
    wgk_cols = 512 // N_DEV
    wgk_f_full = small_all[:, 0:RANK, 0:wgk_cols].transpose(1, 0, 2).reshape(RANK, QK_W)
    wgk_b_full = small_all[:, 0:RANK, wgk_cols:2 * wgk_cols].transpose(1, 0, 2).reshape(RANK, QK_W)
    conv_w_full = _pad_rows(small_all[:, RANK:RANK + 3, :].transpose(1, 0, 2).reshape(3, CONV_W), 8)
    zr = lambda n: jnp.zeros((n, QK_W), F32)
    wgk_f_pad = jnp.concatenate([wgk_f_full, zr(LR_W - RANK)], axis=0).astype(BF16)
    wgk_b_pad = jnp.concatenate([zr(RANK), wgk_b_full, zr(LR_W - 2 * RANK)], axis=0).astype(BF16)

    o_f, o_b, st_f, st_b = _gla_fwd(proj, lr, wgk_f_pad, wgk_b_pad, b_gk_f, b_gk_b, tt)
    tmix = min(256, seq)
    y_t, conv, dx2, dx2b, loss_p, dfg_p = _mix_out_loss(o_f, o_b, proj, x2d, tgt, gla_norm_g, conv_w_full, conv_b,
                                                        w_out_full, final_g.reshape(1, D_MODEL), tmix)

    dp_gates, do, dconv, dgg_p, dcb_p = _mix_bwd(dx2b, o_f, o_b, proj, conv, gla_norm_g, w_out_full, tmix)
    dp_ch, dcw_p = _conv_bwd(dconv, proj, conv_w_full, tmix)
    dqkv_f, dlr_f, dqkv_b, dlr_b, dwf_p, dwb_p, dbf_p, dbb_p = _gla_bwd(
        proj, lr, do, st_f, st_b, wgk_f_pad, wgk_b_pad, b_gk_f, b_gk_b, tt)
    dp_qkv, dlr = _sum_directions(dqkv_f, dqkv_b, dlr_f, dlr_b, tm)
    dw_qkv = _weight_grad_t(h_t, dp_qkv, 512, "wgrad_qkv")
    dw_gates = _weight_grad_t(h_t, dp_gates, 512, "wgrad_gates")
    dw_ch = _weight_grad_t(h_t, dp_ch, 512, "wgrad_ch")
    dw_lr = _weight_grad_t(h_t, dlr, LR_W, "wgrad_lr")
    dw_out = _weight_grad(y_t, dx2b, D_MODEL, tm, "wgrad_out")

    dw_nat = jnp.concatenate([dw_qkv, dw_gates[:CONV_W], dw_lr[:2 * RANK], dw_gates[CONV_W:2 * CONV_W], dw_ch,
                              dw_gates[2 * CONV_W:]], axis=0)
    part_in = dw_nat.reshape(N_DEV, SHARD_W, D_MODEL)
    part_out = dw_out.reshape(N_DEV, MIX_W // N_DEV, D_MODEL)
    sib_in, sib_out = _exchange_sibling([part_in, part_out])
    core = jnp.reshape(pc, (1,)).astype(jnp.int32)
    chip = jnp.reshape(2 * px + py, (1,)).astype(jnp.int32)
    sums_in = _chip_sums(part_in, sib_in, core, 256, "chip_sums_in")
    sums_out = _chip_sums(part_out, sib_out, core, 256, "chip_sums_out")
    grad_x2d, dng_p, far_in, far_out = _input_grad(dp_qkv, dp_gates, dp_ch, dlr, w_main, w_lr, x2d, norm_g, dx2,
                                                   [sums_in, sums_out], tmix)
    g_in_t = _final_sum(sums_in, far_in, chip, 256, "final_sum_in")
    g_w_out, d_w_out, nm_w_out, nv_w_out = _final_sum_adamw(sums_out, far_out, chip, w_out[0], m_w_out[0], v_w_out[0],
                                                            256, "adamw_out")
    flat = lambda a: a[0].T.reshape(SHARD_W * D_MODEL // 128, 128)
    unflat = lambda a: a.reshape(SHARD_W, D_MODEL).T
    d_flat, m_flat, v_flat = _adamw_rows(g_in_t.reshape(SHARD_W * D_MODEL // 128, 128), flat(w_in), flat(m_w_in),
                                         flat(v_w_in), 720, "adamw_in")
    g_w_in, d_w_in, nm_w_in, nv_w_in = g_in_t.T, unflat(d_flat), unflat(m_flat), unflat(v_flat)

    pieces = [dng_p, dbf_p, dbb_p, dgg_p, dcb_p, dfg_p[0], dwf_p[0:RANK], dwb_p[RANK:2 * RANK], dcw_p[0:3], loss_p[0]]
    tot = _unpack(_allreduce_small(_pack(pieces)), pieces)
    g_norm_g, g_b_gk_f, g_b_gk_b, g_gla, g_conv_b, g_final = tot[:6]
    g_wgk_f = lax.dynamic_slice_in_dim(tot[6], me * wgk_cols, wgk_cols, axis=1)[None]
    g_wgk_b = lax.dynamic_slice_in_dim(tot[7], me * wgk_cols, wgk_cols, axis=1)[None]
    g_conv_w = lax.dynamic_slice_in_dim(tot[8], me * 128, 128, axis=1)[None]
    loss = tot[9][0]

    small_g = [g_norm_g, g_b_gk_f, g_b_gk_b, g_gla, g_conv_b, g_final, g_wgk_f, g_wgk_b, g_conv_w]
    small_w = [norm_g, b_gk_f, b_gk_b, gla_norm_g, conv_b, final_g, w_gk_f, w_gk_b, conv_w]
    small_m = [m_norm_g, m_b_gk_f, m_b_gk_b, m_gla_norm_g, m_conv_b, m_final_g, m_w_gk_f, m_w_gk_b, m_conv_w]
    small_v = [v_norm_g, v_b_gk_f, v_b_gk_b, v_gla_norm_g, v_conv_b, v_final_g, v_w_gk_f, v_w_gk_b, v_conv_w]
    d_s, m_s, v_s = _adamw_small(_pack(small_g), _pack(small_w), _pack(small_m), _pack(small_v))
    d_l, m_l, v_l = _unpack(d_s, small_w), _unpack(m_s, small_w), _unpack(v_s, small_w)

    def ordered(sm, big_in, big_out):
        return [sm[0], big_in[None], sm[6], sm[1], sm[7], sm[2], sm[3], sm[8], sm[4], big_out[None], sm[5]]

    grads = ordered(small_g, g_w_in, g_w_out)
    deltas = ordered(d_l, d_w_in, d_w_out)
    new_m = ordered(m_l, nm_w_in, nm_w_out)
    new_v = ordered(v_l, nv_w_in, nv_w_out)
    return (loss, grad_x2d[None], *grads, *deltas, *new_m, *new_v)
```

```python
import jax
import jax.numpy as jnp
from jax import lax
from jax.experimental import pallas as pl
from jax.experimental.pallas import tpu as pltpu

F32 = jnp.float32
BF16 = jnp.bfloat16
MESH = pl.DeviceIdType.MESH

N_DEV = 8
D_MODEL = 1024
HEADS = 4
DK = 128
DV = 256
QK_W = HEADS * DK
V_W = HEADS * DV
CONV_W = 1024
MIX_W = V_W + CONV_W
CHUNK = 64
RANK = 16
IN_W = 7200
SHARD_W = IN_W // N_DEV
MAIN_W = 7168
LR_W = 128
OFF_Q, OFF_K, OFF_V, OFF_ZA, OFF_B, OFF_ZC, OFF_C, OFF_H = 0, 512, 1024, 2048, 3072, 4096, 5120, 6144
QKV_W, GATES_W, CH_W = 2048, 3072, 2048
NAT_ZA, NAT_LR, NAT_B, NAT_C, NAT_ZC = 2048, 3072, 3104, 4128, 6176
EPS = 1e-6
GATE_SCALE = 1.0 / 16.0
QSCALE = DK ** -0.5
REF_F, LAST_F = CHUNK // 2, CHUNK - 1
REF_B, LAST_B = CHUNK - 1 - CHUNK // 2, 0

ADAM_LR = 0.001
ADAM_B1 = 0.9
ADAM_B2 = 0.999
ADAM_EPS = 1e-08
ADAM_WD = 0.01
ADAM_STEP = 10

VMEM_LIMIT = 56 * 1024 * 1024


def _cparams(*sem):
    return pltpu.CompilerParams(dimension_semantics=sem, vmem_limit_bytes=VMEM_LIMIT)


def _dot(a, b):
    return jnp.dot(a, b, preferred_element_type=F32)


def _dot_nt(a, b):
    return lax.dot_general(a, b, (((1,), (1,)), ((), ())), preferred_element_type=F32)


def _dot_tn(a, b):
    return lax.dot_general(a, b, (((0,), (0,)), ((), ())), preferred_element_type=F32)


def _sigmoid(z):
    return jax.nn.sigmoid(z)


def _position():
    return lax.axis_index("x"), lax.axis_index("y"), lax.axis_index("c")


def _blk(px, py, pc):
    return 4 * px + 2 * py + pc


def _two_level_gather(outs, send_sems, recv_sems):
    x, y, c = _position()
    me, sibling = (x, y, c), (x, y, 1 - c)
    chips = [(1 - x, y), (x, 1 - y), (1 - x, 1 - y)]
    n = len(outs)

    def copy(a, k, block, to):
        ref = outs[a].at[_blk(*block)]
        return pltpu.make_async_remote_copy(src_ref=ref, dst_ref=ref, send_sem=send_sems.at[a * 7 + k],
                                            recv_sem=recv_sems.at[a * 7 + k], device_id=to, device_id_type=MESH)

    first = []
    for a in range(n):
        first.append(copy(a, 0, me, sibling))
        first += [copy(a, 1 + j, me, (*chip, c)) for j, chip in enumerate(chips)]
    for cp in first:
        cp.start()
    passed = []
    for j, chip in enumerate(chips):
        for a in range(n):
            copy(a, 1 + j, (*chip, c), me).wait_recv()
            fwd = copy(a, 4 + j, (*chip, c), sibling)
            fwd.start()
            passed.append(fwd)
    for a in range(n):
        copy(a, 0, sibling, me).wait_recv()
    for j, chip in enumerate(chips):
        for a in range(n):
            copy(a, 4 + j, (*chip, 1 - c), me).wait_recv()
    for cp in first + passed:
        cp.wait_send()


def _allgather_w_in(w_in_t):
    def body(win_ref, win_all, send_sems, recv_sems):
        win_all[_blk(*_position())] = win_ref[...].astype(BF16)
        _two_level_gather((win_all,), send_sems, recv_sems)

    vmem = pl.BlockSpec(memory_space=pltpu.VMEM)
    return pl.pallas_call(
        body, name="allgather_w_in",
        out_shape=jax.ShapeDtypeStruct((N_DEV,) + w_in_t.shape, BF16),
        in_specs=[vmem], out_specs=vmem,
        scratch_shapes=[pltpu.SemaphoreType.DMA((7,)), pltpu.SemaphoreType.DMA((7,))],
        compiler_params=pltpu.CompilerParams(vmem_limit_bytes=VMEM_LIMIT),
    )(w_in_t)


def _peer_copies(srcs, outs, send_sems, recv_sems):
    x, y, c = _position()
    me = _blk(x, y, c)
    copies = []
    for a, (src, out) in enumerate(zip(srcs, outs)):
        k = 0
        for dx in (0, 1):
            for dy in (0, 1):
                for dc in (0, 1):
                    if dx + dy + dc == 0:
                        continue
                    peer = (1 - x if dx else x, 1 - y if dy else y, 1 - c if dc else c)
                    copies.append(pltpu.make_async_remote_copy(
                        src_ref=src, dst_ref=out.at[me], send_sem=send_sems.at[a * 7 + k],
                        recv_sem=recv_sems.at[a * 7 + k], device_id=peer, device_id_type=MESH))
                    k += 1
    return copies


def _chip_copies(ins, outs, send_sems, recv_sems):
    x, y, c = _position()
    chips = [(1 - x, y), (x, 1 - y), (1 - x, 1 - y)]
    copies = []
    for a in range(len(ins)):
        for j, (px, py) in enumerate(chips):
            copies.append(pltpu.make_async_remote_copy(
                src_ref=ins[a].at[2 * px + py], dst_ref=outs[a].at[j],
                send_sem=send_sems.at[a * 3 + j], recv_sem=recv_sems.at[a * 3 + j],
                device_id=(px, py, c), device_id_type=MESH))
    return copies


def _allreduce_small(part):
    def body(p_ref, tot_ref, all_ref, send_sems, recv_sems):
        mine = _blk(*_position())
        all_ref[mine] = p_ref[...]
        _two_level_gather((all_ref,), send_sems, recv_sems)
        acc = all_ref[0]
        for d in range(1, N_DEV):
            acc = acc + all_ref[d]
        tot_ref[...] = acc

    vmem = pl.BlockSpec(memory_space=pltpu.VMEM)
    return pl.pallas_call(
        body, name="allreduce_small",
        out_shape=jax.ShapeDtypeStruct(part.shape, F32),
        in_specs=[vmem], out_specs=vmem,
        scratch_shapes=[pltpu.VMEM((N_DEV,) + part.shape, F32),
                        pltpu.SemaphoreType.DMA((7,)), pltpu.SemaphoreType.DMA((7,))],
        compiler_params=pltpu.CompilerParams(vmem_limit_bytes=VMEM_LIMIT),
    )(part)


def _exchange_sibling(parts):
    n = len(parts)

    def body(*refs):
        ins, outs = refs[:n], refs[n:2 * n]
        send_sems, recv_sems = refs[2 * n], refs[2 * n + 1]
        x, y, c = _position()
        sibling = (x, y, 1 - c)
        copies = []
        for a in range(n):
            for k in range(4):
                copies.append(pltpu.make_async_remote_copy(
                    src_ref=ins[a].at[2 * k + (1 - c)], dst_ref=outs[a].at[k],
                    send_sem=send_sems.at[a * 4 + k], recv_sem=recv_sems.at[a * 4 + k],
                    device_id=sibling, device_id_type=MESH))
        for cp in copies:
            cp.start()
        for cp in copies:
            cp.wait_recv()
        for cp in copies:
            cp.wait_send()

    hbm = pl.BlockSpec(memory_space=pl.ANY)
    return pl.pallas_call(
        body, name="exchange_sibling",
        out_shape=tuple(jax.ShapeDtypeStruct((4,) + p.shape[1:], F32) for p in parts),
        in_specs=[hbm] * n, out_specs=tuple([hbm] * n),
        scratch_shapes=[pltpu.SemaphoreType.DMA((4 * n,)), pltpu.SemaphoreType.DMA((4 * n,))],
    )(*parts)


def _chip_sums(part, from_sibling, core, tc, name):
    _, rows, cols = part.shape

    def body(core_ref, p_ref, s_ref, o_ref):
        o_ref[...] = (p_ref[...] + s_ref[...]).astype(BF16)

    return pl.pallas_call(
        body, name=name,
        out_shape=jax.ShapeDtypeStruct((4, rows, cols), BF16),
        grid_spec=pltpu.PrefetchScalarGridSpec(
            num_scalar_prefetch=1, grid=(4, cols // tc),
            in_specs=[pl.BlockSpec((1, rows, tc), lambda k, j, core_ref: (2 * k + core_ref[0], 0, j)),
                      pl.BlockSpec((1, rows, tc), lambda k, j, core_ref: (k, 0, j))],
            out_specs=pl.BlockSpec((1, rows, tc), lambda k, j, core_ref: (k, 0, j))),
        compiler_params=_cparams("arbitrary", "arbitrary"),
    )(core, part, from_sibling)


def _sum_chips(s_ref, r_ref):
    f = lambda a: a.astype(F32)
    return ((f(s_ref[0]) + f(r_ref[0])) + f(r_ref[1])) + f(r_ref[2])


def _final_sum(sums, from_chips, chip, tc, name):
    _, rows, cols = sums.shape

    def body(chip_ref, s_ref, r_ref, g_out):
        g_out[...] = _sum_chips(s_ref, r_ref)

    return pl.pallas_call(
        body, name=name,
        out_shape=jax.ShapeDtypeStruct((rows, cols), F32),
        grid_spec=pltpu.PrefetchScalarGridSpec(
            num_scalar_prefetch=1, grid=(cols // tc,),
            in_specs=[pl.BlockSpec((1, rows, tc), lambda j, chip_ref: (chip_ref[0], 0, j)),
                      pl.BlockSpec((3, rows, tc), lambda j, chip_ref: (0, 0, j))],
            out_specs=pl.BlockSpec((rows, tc), lambda j, chip_ref: (0, j))),
        compiler_params=_cparams("arbitrary"),
    )(chip, sums, from_chips)


def _adamw_rows(g, w, m, v, tr, name):
    rows, cols = g.shape

    def body(g_ref, w_ref, m_ref, v_ref, d_out, m_out, v_out):
        delta, m_new, v_new = _adamw(w_ref[...], g_ref[...], m_ref[...], v_ref[...])
        d_out[...] = delta
        m_out[...] = m_new
        v_out[...] = v_new

    tile = pl.BlockSpec((tr, cols), lambda r: (r, 0))
    shp = jax.ShapeDtypeStruct((rows, cols), F32)
    return pl.pallas_call(
        body, name=name, out_shape=(shp, shp, shp), grid=(rows // tr,),
        in_specs=[tile] * 4, out_specs=(tile, tile, tile),
        compiler_params=_cparams("arbitrary"),
    )(g, w, m, v)


def _adamw(w, g, m, v):
    m = ADAM_B1 * m + (1.0 - ADAM_B1) * g
    v = ADAM_B2 * v + (1.0 - ADAM_B2) * (g * g)
    m_hat = m / (1.0 - ADAM_B1 ** ADAM_STEP)
    v_hat = v / (1.0 - ADAM_B2 ** ADAM_STEP)
    delta = -ADAM_LR * (m_hat / (jnp.sqrt(v_hat) + ADAM_EPS) + ADAM_WD * w)
    return delta, m, v


def _final_sum_adamw(sums, from_chips, chip, w, m, v, tr, name):
    rows, cols = w.shape

    def body(chip_ref, s_ref, r_ref, w_ref, m_ref, v_ref, g_out, d_out, m_out, v_out):
        g = _sum_chips(s_ref, r_ref)
        delta, m_new, v_new = _adamw(w_ref[...], g, m_ref[...], v_ref[...])
        g_out[...] = g
        d_out[...] = delta
        m_out[...] = m_new
        v_out[...] = v_new

    tile = pl.BlockSpec((tr, cols), lambda r, chip_ref: (r, 0))
    shp = jax.ShapeDtypeStruct((rows, cols), F32)
    return pl.pallas_call(
        body, name=name,
        out_shape=(shp, shp, shp, shp),
        grid_spec=pltpu.PrefetchScalarGridSpec(
            num_scalar_prefetch=1, grid=(rows // tr,),
            in_specs=[pl.BlockSpec((1, tr, cols), lambda r, chip_ref: (chip_ref[0], r, 0)),
                      pl.BlockSpec((3, tr, cols), lambda r, chip_ref: (0, r, 0)),
                      tile, tile, tile],
            out_specs=(tile, tile, tile, tile)),
        compiler_params=_cparams("arbitrary"),
    )(chip, sums, from_chips, w, m, v)


def _adamw_small(g, w, m, v):
    def body(g_ref, w_ref, m_ref, v_ref, d_out, m_out, v_out):
        delta, m_new, v_new = _adamw(w_ref[...], g_ref[...], m_ref[...], v_ref[...])
        d_out[...] = delta
        m_out[...] = m_new
        v_out[...] = v_new

    vmem = pl.BlockSpec(memory_space=pltpu.VMEM)
    shp = jax.ShapeDtypeStruct(g.shape, F32)
    return pl.pallas_call(body, name="adamw_small", out_shape=(shp, shp, shp),
                          in_specs=[vmem] * 4, out_specs=(vmem, vmem, vmem))(g, w, m, v)


TILE_ROWS = (0, 1024, NAT_ZA, NAT_B, NAT_ZC, NAT_C, NAT_C + CONV_W)


def _inproj(x2d, norm_g, w_nat, w_out_s, small_s, tm):
    seq = x2d.shape[0]
    tn = CONV_W
    ni, nj = seq // tm, MAIN_W // tn
    first_sweep = lambda j, i: jnp.where(j == 0, i, ni - 1)

    def tile_row(j, i):
        row = 0
        for k, start in enumerate(TILE_ROWS):
            row = row + jnp.where(j == k, start // 32, 0)
        return pl.multiple_of(row * 32, 32), 0

    def body(x_ref, g_ref, w_ref, wlr_ref, wout_ref, sm_ref, proj_ref, lr_ref, ht_ref, wout_all, sm_all,
             h_all, wout_b, sm_b, send_sems, recv_sems, local_sems):
        j, i = pl.program_id(0), pl.program_id(1)
        rows = pl.ds(pl.multiple_of(i * tm, tm), tm)
        me = _blk(*_position())

        def gather_copies():
            mine = [pltpu.make_async_copy(wout_b, wout_all.at[me], local_sems.at[0]),
                    pltpu.make_async_copy(sm_b, sm_all.at[me], local_sems.at[1])]
            return mine, _peer_copies((wout_b, sm_b), (wout_all, sm_all), send_sems, recv_sems)

        @pl.when(jnp.logical_and(j == 0, i == 0))
        def _():
            wout_b[...] = wout_ref[...].astype(BF16)
            sm_b[...] = sm_ref[...]
            mine, remote = gather_copies()
            for cp in mine + remote:
                cp.start()

        @pl.when(j == 0)
        def _():
            xv = x_ref[...]
            r = lax.rsqrt(jnp.mean(xv * xv, axis=-1, keepdims=True) + EPS)
            h = (xv * r) * g_ref[...]
            hb = h.astype(BF16)
            h_all[rows, :] = hb
            ht_ref[...] = h.T.astype(BF16)
            lr_ref[...] = _dot_nt(hb, wlr_ref[...])

        proj_ref[...] = _dot_nt(h_all[rows, :], w_ref[...]).astype(BF16)

        @pl.when(jnp.logical_and(j == nj - 1, i == ni - 1))
        def _():
            mine, remote = gather_copies()
            for cp in remote:
                cp.wait_recv()
            for cp in remote:
                cp.wait_send()
            for cp in mine:
                cp.wait()

    const = lambda shape: pl.BlockSpec(shape, lambda j, i: (0,) * len(shape))
    hbm = pl.BlockSpec(memory_space=pl.ANY)
    return pl.pallas_call(
        body, name="inproj",
        out_shape=(jax.ShapeDtypeStruct((seq, MAIN_W), BF16), jax.ShapeDtypeStruct((seq, LR_W), F32),
                   jax.ShapeDtypeStruct((D_MODEL, seq), BF16),
                   jax.ShapeDtypeStruct((N_DEV,) + w_out_s.shape, BF16),
                   jax.ShapeDtypeStruct((N_DEV,) + small_s.shape, F32)),
        grid=(nj, ni),
        in_specs=[pl.BlockSpec((tm, D_MODEL), lambda j, i: (first_sweep(j, i), 0)),
                  const((1, D_MODEL)),
                  pl.BlockSpec((pl.Element(tn), pl.Element(D_MODEL)), tile_row),
                  pl.BlockSpec((pl.Element(LR_W), pl.Element(D_MODEL)), lambda j, i: (NAT_LR, 0)),
                  const(w_out_s.shape), const(small_s.shape)],
        out_specs=(pl.BlockSpec((tm, tn), lambda j, i: (i, j)),
                   pl.BlockSpec((tm, LR_W), lambda j, i: (first_sweep(j, i), 0)),
                   pl.BlockSpec((D_MODEL, tm), lambda j, i: (0, first_sweep(j, i))), hbm, hbm),
        scratch_shapes=[pltpu.VMEM((seq, D_MODEL), BF16), pltpu.VMEM(w_out_s.shape, BF16),
                        pltpu.VMEM(small_s.shape, F32), pltpu.SemaphoreType.DMA((14,)),
                        pltpu.SemaphoreType.DMA((14,)), pltpu.SemaphoreType.DMA((2,))],
        compiler_params=_cparams("arbitrary", "arbitrary"),
    )(x2d, norm_g, w_nat, w_nat, w_out_s, small_s)


def _block_masks(tt):
    row = lax.broadcasted_iota(jnp.int32, (tt, tt), 0)
    col = lax.broadcasted_iota(jnp.int32, (tt, tt), 1)
    same = jnp.right_shift(row, 6) == jnp.right_shift(col, 6)
    return (jnp.logical_and(same, col <= row), jnp.logical_and(same, col >= row), jnp.logical_and(same, col > row))


def _dot_split3(ones_mat, x):
    x1 = x.astype(BF16)
    r1 = x - x1.astype(F32)
    x2 = r1.astype(BF16)
    x3 = (r1 - x2.astype(F32)).astype(BF16)
    return (_dot(ones_mat, x3) + _dot(ones_mat, x2)) + _dot(ones_mat, x1)


def _log_gate(logits):
    return (jnp.minimum(logits, 0.0) - jnp.log1p(jnp.exp(-jnp.abs(logits)))) * GATE_SCALE


def _gla_fwd(proj, lr, wgk_f, wgk_b, bgk_f, bgk_b, tt):
    seq = proj.shape[0]
    nb, nc, nch = seq // tt, tt // CHUNK, seq // CHUNK

    def body(qf, kf, vf, lrf, qb, kb, vb, lrb, wf, wb, bf, bb, of, ob, stf, stb, s_scr, qs_s, ks_s, qin_s, kout_s):
        @pl.when(pl.program_id(0) == 0)
        def _():
            s_scr[...] = jnp.zeros(s_scr.shape, F32)

        low, upp, sup = _block_masks(tt)
        dirs = ((qf, kf, vf, lrf, wf, bf, of, stf, low, low, REF_F, LAST_F, list(range(nc))),
                (qb, kb, vb, lrb, wb, bb, ob, stb, upp, sup, REF_B, LAST_B, list(reversed(range(nc)))))
        for d, (q_r, k_r, v_r, lr_r, w_r, b_r, o_r, st_r, cum, mask, ref, last, order) in enumerate(dirs):
            logits = _dot(lr_r[...].astype(BF16), w_r[...]) + b_r[...]
            b = _dot_split3(cum.astype(BF16), _log_gate(logits))
            decs = []
            for c in range(nc):
                rows = slice(c * CHUNK, (c + 1) * CHUNK)
                bc = b[rows]
                b_ref, b_last = bc[ref:ref + 1], bc[last:last + 1]
                qc = q_r[rows, :].astype(F32) * QSCALE
                kc = k_r[rows, :].astype(F32)
                qs_s[rows, :] = (qc * jnp.exp(bc - b_ref)).astype(BF16)
                ks_s[rows, :] = (kc * jnp.exp(b_ref - bc)).astype(BF16)
                qin_s[rows, :] = (qc * jnp.exp(bc)).astype(BF16)
                kout_s[rows, :] = (kc * jnp.exp(b_last - bc)).astype(BF16)
                decs.append(jnp.exp(b_last))
            for h in range(HEADS):
                ksl = slice(h * DK, (h + 1) * DK)
                vsl = slice(h * DV, (h + 1) * DV)
                v = v_r[:, vsl].astype(BF16)
                att = jnp.where(mask, _dot_nt(qs_s[:, ksl], ks_s[:, ksl]), 0.0).astype(BF16)
                o_intra = _dot(att, v)
                st = s_scr[d * HEADS + h]
                for c in order:
                    rows = slice(c * CHUNK, (c + 1) * CHUNK)
                    stb = st.astype(BF16)
                    st_r[c, h] = stb
                    o_r[rows, vsl] = (o_intra[rows] + _dot_nt(qin_s[rows, ksl], stb)).astype(BF16)
                    st = st * decs[c][:, ksl] + _dot_tn(v[rows], kout_s[rows, ksl])
                s_scr[d * HEADS + h] = st

    fw = lambda i: (i, 0)
    bw = lambda i: (nb - 1 - i, 0)
    const = lambda i: (0, 0)

    def tok_specs(m):
        return [pl.BlockSpec((tt, QK_W), lambda i: (m(i)[0], OFF_Q // QK_W)),
                pl.BlockSpec((tt, QK_W), lambda i: (m(i)[0], OFF_K // QK_W)),
                pl.BlockSpec((tt, V_W), lambda i: (m(i)[0], OFF_V // V_W)),
                pl.BlockSpec((tt, LR_W), m)]

    st_shape = jax.ShapeDtypeStruct((nch, HEADS, DV, DK), BF16)
    o_shape = jax.ShapeDtypeStruct((seq, V_W), BF16)
    operand = pltpu.VMEM((tt, QK_W), BF16)
    return pl.pallas_call(
        body, name="gla_fwd",
        out_shape=(o_shape, o_shape, st_shape, st_shape),
        grid=(nb,),
        in_specs=tok_specs(fw) + tok_specs(bw) + [
            pl.BlockSpec((LR_W, QK_W), const), pl.BlockSpec((LR_W, QK_W), const),
            pl.BlockSpec((1, QK_W), const), pl.BlockSpec((1, QK_W), const)],
        out_specs=(pl.BlockSpec((tt, V_W), fw), pl.BlockSpec((tt, V_W), bw),
                   pl.BlockSpec((nc, HEADS, DV, DK), lambda i: (i, 0, 0, 0)),
                   pl.BlockSpec((nc, HEADS, DV, DK), lambda i: (nb - 1 - i, 0, 0, 0))),
        scratch_shapes=[pltpu.VMEM((2 * HEADS, DV, DK), F32), operand, operand, operand, operand],
        compiler_params=_cparams("arbitrary"),
    )(proj, proj, proj, lr, proj, proj, proj, lr, wgk_f, wgk_b, bgk_f, bgk_b)


def _head_norm(o, gain):
    outs, rinv = [], []
    for h in range(HEADS):
        oh = o[:, h * DV:(h + 1) * DV]
        r = lax.rsqrt(jnp.mean(oh * oh, axis=-1, keepdims=True) + EPS)
        outs.append((oh * r) * gain)
        rinv.append(r)
    return jnp.concatenate(outs, axis=1), rinv


def _shift_rows(u, prev_row, next_row):
    n = u.shape[0]
    row = lax.broadcasted_iota(jnp.int32, (n, 1), 0)
    up = jnp.where(row == 0, prev_row, pltpu.roll(u, 1, 0))
    un = jnp.where(row == n - 1, next_row, pltpu.roll(u, n - 1, 0))
    return up, un


HALO = 16


def _halo_specs(tm, seq, col_block):
    per = tm // HALO
    last = seq // HALO - 1
    return [pl.BlockSpec((HALO, CONV_W), lambda i: (jnp.maximum(i * per - 1, 0), col_block)),
            pl.BlockSpec((HALO, CONV_W), lambda i: (jnp.minimum((i + 1) * per, last), col_block))]


def _f32(ref):
    return ref[...].astype(F32)


def _last_row(ref):
    return ref[HALO - 1:HALO, :].astype(F32)


def _first_row(ref):
    return ref[0:1, :].astype(F32)


def _mix_out_loss(o_f, o_b, proj, x2d, tgt, gla_g, conv_w, conv_b, w_out, final_g, tm):
    seq = x2d.shape[0]
    nt = seq // tm

    def body(of, ob, za, bg, cg, hc, zc, cprev, cnext, hprev, hnext, x_ref, t_ref, gg, cw, cb, wo, fg,
             yt_ref, conv_ref, dx2_ref, dx2b_ref, loss_ref, dfg_ref):
        i = pl.program_id(0)

        @pl.when(i == 0)
        def _():
            loss_ref[...] = jnp.zeros(loss_ref.shape, F32)
            dfg_ref[...] = jnp.zeros(dfg_ref.shape, F32)

        on, _ = _head_norm(_f32(of) + _f32(ob), gg[...])
        zav = _f32(za)
        y_a = on * (zav * _sigmoid(zav))
        u = _f32(cg) * _f32(hc)
        prev_row = jnp.where(i > 0, _last_row(cprev) * _last_row(hprev), 0.0)
        next_row = jnp.where(i < nt - 1, _first_row(cnext) * _first_row(hnext), 0.0)
        up, un = _shift_rows(u, prev_row, next_row)
        conv = (cw[0:1, :] * up + cw[1:2, :] * u + cw[2:3, :] * un) + cb[...]
        conv_ref[...] = conv.astype(BF16)
        zcv = _f32(zc)
        y_c = _f32(bg) * conv * (zcv * _sigmoid(zcv))
        y = jnp.concatenate([y_a, y_c], axis=1)
        yt_ref[...] = y.T.astype(BF16)
        x2 = x_ref[...] + _dot(y.astype(BF16), wo[...])
        r = lax.rsqrt(jnp.mean(x2 * x2, axis=-1, keepdims=True) + EPS)
        xn = x2 * r
        err = xn * fg[...] - t_ref[...]
        loss_ref[...] += 0.5 * jnp.sum(jnp.mean(err * err, axis=-1, keepdims=True))
        dyf = err * (1.0 / D_MODEL)
        dfg_ref[...] += jnp.sum(dyf * xn, axis=0, keepdims=True)
        dxn = dyf * fg[...]
        dx2 = r * dxn - xn * (r * jnp.mean(dxn * xn, axis=-1, keepdims=True))
        dx2_ref[...] = dx2
        dx2b_ref[...] = dx2.astype(BF16)

    def col(off):
        return pl.BlockSpec((tm, CONV_W), lambda i: (i, off // CONV_W))

    rowt = pl.BlockSpec((tm, D_MODEL), lambda i: (i, 0))
    const = lambda shape: pl.BlockSpec(shape, lambda i: (0, 0))
    return pl.pallas_call(
        body, name="mix_out_loss",
        out_shape=(jax.ShapeDtypeStruct((MIX_W, seq), BF16), jax.ShapeDtypeStruct((seq, CONV_W), BF16),
                   jax.ShapeDtypeStruct((seq, D_MODEL), F32), jax.ShapeDtypeStruct((seq, D_MODEL), BF16),
                   jax.ShapeDtypeStruct((8, 128), F32), jax.ShapeDtypeStruct((1, D_MODEL), F32)),
        grid=(nt,),
        in_specs=[rowt, rowt, col(OFF_ZA), col(OFF_B), col(OFF_C), col(OFF_H), col(OFF_ZC)]
        + _halo_specs(tm, seq, OFF_C // CONV_W) + _halo_specs(tm, seq, OFF_H // CONV_W)
        + [rowt, rowt, const((1, DV)), const((8, CONV_W)), const((1, CONV_W)), const((MIX_W, D_MODEL)),
           const((1, D_MODEL))],
        out_specs=(pl.BlockSpec((MIX_W, tm), lambda i: (0, i)), rowt, rowt, rowt, const((8, 128)),
                   const((1, D_MODEL))),
        compiler_params=_cparams("arbitrary"),
    )(o_f, o_b, proj, proj, proj, proj, proj, proj, proj, proj, proj, x2d, tgt, gla_g, conv_w, conv_b, w_out, final_g)


def _dsilu(z, s):
    return s * (1.0 + z * (1.0 - s))


def _mix_bwd(dx2b, o_f, o_b, proj, conv, gla_g, w_out, tm):
    seq = dx2b.shape[0]

    def body(dx, of, ob, za, bg, zc, cv, gg, wo, dg_ref, do_ref, dconv_ref, dgg_ref, dcb_ref):
        @pl.when(pl.program_id(0) == 0)
        def _():
            dgg_ref[...] = jnp.zeros(dgg_ref.shape, F32)
            dcb_ref[...] = jnp.zeros(dcb_ref.shape, F32)

        dy = _dot_nt(dx[...], wo[...])
        dy_a, dy_c = dy[:, :V_W], dy[:, V_W:]
        zcv, bgv, convv = _f32(zc), _f32(bg), _f32(cv)
        sc = _sigmoid(zcv)
        szc = zcv * sc
        dg_ref[:, CONV_W:2 * CONV_W] = (dy_c * convv * szc).astype(BF16)
        dconv = dy_c * bgv * szc
        dconv_ref[...] = dconv.astype(BF16)
        dcb_ref[...] += jnp.sum(dconv, axis=0, keepdims=True)
        dg_ref[:, 2 * CONV_W:] = (dy_c * bgv * convv * _dsilu(zcv, sc)).astype(BF16)

        o = _f32(of) + _f32(ob)
        gain = gg[...]
        on, rinv = _head_norm(o, gain)
        zav = _f32(za)
        sa = _sigmoid(zav)
        dg_ref[:, :CONV_W] = (dy_a * on * _dsilu(zav, sa)).astype(BF16)
        don = dy_a * (zav * sa)
        dgg = jnp.zeros((1, DV), F32)
        dos = []
        for h in range(HEADS):
            sl = slice(h * DV, (h + 1) * DV)
            oh, r, dh = o[:, sl], rinv[h], don[:, sl]
            ohn = oh * r
            dgg = dgg + jnp.sum(dh * ohn, axis=0, keepdims=True)
            dn = dh * gain
            dos.append(r * dn - ohn * (r * jnp.mean(dn * ohn, axis=-1, keepdims=True)))
        dgg_ref[...] += dgg
        do_ref[...] = jnp.concatenate(dos, axis=1).astype(BF16)

    def col(off):
        return pl.BlockSpec((tm, CONV_W), lambda i: (i, off // CONV_W))

    rowt = pl.BlockSpec((tm, D_MODEL), lambda i: (i, 0))
    const = lambda shape: pl.BlockSpec(shape, lambda i: (0, 0))
    return pl.pallas_call(
        body, name="mix_bwd",
        out_shape=(jax.ShapeDtypeStruct((seq, GATES_W), BF16), jax.ShapeDtypeStruct((seq, V_W), BF16),
                   jax.ShapeDtypeStruct((seq, CONV_W), BF16),
                   jax.ShapeDtypeStruct((1, DV), F32), jax.ShapeDtypeStruct((1, CONV_W), F32)),
        grid=(seq // tm,),
        in_specs=[rowt, rowt, rowt, col(OFF_ZA), col(OFF_B), col(OFF_ZC), rowt, const((1, DV)),
                  const((MIX_W, D_MODEL))],
        out_specs=(pl.BlockSpec((tm, GATES_W), lambda i: (i, 0)), rowt, rowt, const((1, DV)), const((1, CONV_W))),
        compiler_params=_cparams("arbitrary"),
    )(dx2b, o_f, o_b, proj, proj, proj, conv, gla_g, w_out)


def _conv_bwd(dconv, proj, conv_w, tm):
    seq = dconv.shape[0]
    nt = seq // tm

    def body(dc_in, dprev, dnext, cg, hc, cprev, cnext, hprev, hnext, cw, dch_ref, dcw_ref):
        i = pl.program_id(0)

        @pl.when(i == 0)
        def _():
            dcw_ref[...] = jnp.zeros(dcw_ref.shape, F32)

        first, lastt = i > 0, i < nt - 1
        dcv = _f32(dc_in)
        d_up, d_un = _shift_rows(dcv, jnp.where(first, _last_row(dprev), 0.0), jnp.where(lastt, _first_row(dnext), 0.0))
        cgv, hcv = _f32(cg), _f32(hc)
        u = cgv * hcv
        u_up, u_un = _shift_rows(u, jnp.where(first, _last_row(cprev) * _last_row(hprev), 0.0),
                                 jnp.where(lastt, _first_row(cnext) * _first_row(hnext), 0.0))
        du = cw[0:1, :] * d_un + cw[1:2, :] * dcv + cw[2:3, :] * d_up
        dch_ref[:, :CONV_W] = (du * hcv).astype(BF16)
        dch_ref[:, CONV_W:] = (du * cgv).astype(BF16)
        dcw_ref[0:1, :] += jnp.sum(dcv * u_up, axis=0, keepdims=True)
        dcw_ref[1:2, :] += jnp.sum(dcv * u, axis=0, keepdims=True)
        dcw_ref[2:3, :] += jnp.sum(dcv * u_un, axis=0, keepdims=True)

    def col(off):
        return pl.BlockSpec((tm, CONV_W), lambda i: (i, off // CONV_W))

    rowt = pl.BlockSpec((tm, CONV_W), lambda i: (i, 0))
    const = lambda shape: pl.BlockSpec(shape, lambda i: (0, 0))
    return pl.pallas_call(
        body, name="conv_bwd",
        out_shape=(jax.ShapeDtypeStruct((seq, CH_W), BF16), jax.ShapeDtypeStruct((8, CONV_W), F32)),
        grid=(nt,),
        in_specs=[rowt] + _halo_specs(tm, seq, 0) + [col(OFF_C), col(OFF_H)]
        + _halo_specs(tm, seq, OFF_C // CONV_W) + _halo_specs(tm, seq, OFF_H // CONV_W) + [const((8, CONV_W))],
        out_specs=(pl.BlockSpec((tm, CH_W), lambda i: (i, 0)), const((8, CONV_W))),
        compiler_params=_cparams("arbitrary"),
    )(dconv, dconv, dconv, proj, proj, proj, proj, proj, proj, conv_w)


def _gla_bwd(proj, lr, do, st_f, st_b, wgk_f, wgk_b, bgk_f, bgk_b, tt):
    seq = proj.shape[0]
    nb, nc = seq // tt, tt // CHUNK

    def body(qf, kf, vf, lrf, dof, stf, qb, kb, vb, lrb, dob, stb, wf, wb, bf, bb,
             dqkv_f, dlr_f, dqkv_b, dlr_b, dwf, dwb, dbf, dbb,
             ds_scr, eq_s, ek_s, ein_s, eout_s, qs_s, ks_s, qin_s, kout_s, db_s, lg_s):
        @pl.when(pl.program_id(0) == 0)
        def _():
            ds_scr[...] = jnp.zeros(ds_scr.shape, F32)
            for r in (dwf, dwb, dbf, dbb):
                r[...] = jnp.zeros(r.shape, F32)

        low, upp, sup = _block_masks(tt)
        row = lax.broadcasted_iota(jnp.int32, (CHUNK, 1), 0)
        dirs = ((qf, kf, vf, lrf, dof, stf, wf, bf, dqkv_f, dlr_f, dwf, dbf,
                 low, upp, low, REF_F, LAST_F, list(reversed(range(nc)))),
                (qb, kb, vb, lrb, dob, stb, wb, bb, dqkv_b, dlr_b, dwb, dbb,
                 upp, low, sup, REF_B, LAST_B, list(range(nc))))
        for d, (q_r, k_r, v_r, lr_r, do_r, st_r, w_r, b_r, dqkv_r, dlr_r, dw_r, db_r,
                cum, cum_t, mask, ref, last, order) in enumerate(dirs):
            lrv = lr_r[...].astype(BF16)
            wv = w_r[...]
            logits = _dot(lrv, wv) + b_r[...]
            lg_s[...] = logits
            b = _dot_split3(cum.astype(BF16), _log_gate(logits))
            decs = []
            for c in range(nc):
                rows = slice(c * CHUNK, (c + 1) * CHUNK)
                bc = b[rows]
                b_ref, b_last = bc[ref:ref + 1], bc[last:last + 1]
                qc = q_r[rows, :].astype(F32) * QSCALE
                kc = k_r[rows, :].astype(F32)
                e_q, e_k, e_in, e_out = jnp.exp(bc - b_ref), jnp.exp(b_ref - bc), jnp.exp(bc), jnp.exp(b_last - bc)
                eq_s[rows, :], ek_s[rows, :], ein_s[rows, :], eout_s[rows, :] = e_q, e_k, e_in, e_out
                qs_s[rows, :] = (qc * e_q).astype(BF16)
                ks_s[rows, :] = (kc * e_k).astype(BF16)
                qin_s[rows, :] = (qc * e_in).astype(BF16)
                kout_s[rows, :] = (kc * e_out).astype(BF16)
                decs.append(jnp.exp(b_last))
            for h in range(HEADS):
                ksl = slice(h * DK, (h + 1) * DK)
                vsl = slice(h * DV, (h + 1) * DV)
                v = v_r[:, vsl].astype(BF16)
                dov = do_r[:, vsl].astype(BF16)
                qsb, ksb = qs_s[:, ksl], ks_s[:, ksl]
                att = jnp.where(mask, _dot_nt(qsb, ksb), 0.0).astype(BF16)
                datt = jnp.where(mask, _dot_nt(dov, v), 0.0).astype(BF16)
                dqs = _dot(datt, ksb)
                dks = _dot_tn(datt, qsb)
                dv_intra = _dot_tn(att, dov)
                ds = ds_scr[d * HEADS + h]
                for c in order:
                    rows = slice(c * CHUNK, (c + 1) * CHUNK)
                    dsb = ds.astype(BF16)
                    s_prev = st_r[c, h]
                    dk_out = _dot(v[rows], dsb)
                    dq_in = _dot(dov[rows], s_prev)
                    dqkv_r[rows, OFF_V + h * DV:OFF_V + (h + 1) * DV] = dv_intra[rows] + _dot_nt(kout_s[rows, ksl], dsb)
                    dec = decs[c][:, ksl]
                    ddec = jnp.sum(ds * s_prev.astype(F32), axis=0, keepdims=True)
                    e_q, e_k, e_in, e_out = eq_s[rows, ksl], ek_s[rows, ksl], ein_s[rows, ksl], eout_s[rows, ksl]
                    qc = q_r[rows, ksl].astype(F32) * QSCALE
                    kc = k_r[rows, ksl].astype(F32)
                    dqs_c, dks_c = dqs[rows], dks[rows]
                    dqkv_r[rows, OFF_Q + h * DK:OFF_Q + (h + 1) * DK] = (dqs_c * e_q + dq_in * e_in) * QSCALE
                    dqkv_r[rows, OFF_K + h * DK:OFF_K + (h + 1) * DK] = dks_c * e_k + dk_out * e_out
                    kk = dk_out * (kc * e_out)
                    db = dqs_c * (qc * e_q) - dks_c * (kc * e_k) + dq_in * (qc * e_in) - kk
                    tail = jnp.sum(kk, axis=0, keepdims=True) + ddec * dec
                    db_s[rows, ksl] = db + jnp.where(row == last, tail, 0.0)
                    ds = ds * dec + _dot_tn(dov[rows], qin_s[rows, ksl])
                ds_scr[d * HEADS + h] = ds
            dg = _dot_split3(cum_t.astype(BF16), db_s[...])
            dlogit = (dg * GATE_SCALE) * _sigmoid(-lg_s[...])
            dlb = dlogit.astype(BF16)
            dlr_r[...] = _dot_nt(dlb, wv)
            dw_r[...] += _dot_tn(lrv, dlb)
            db_r[...] += jnp.sum(dlogit, axis=0, keepdims=True)

    fw = lambda i: (nb - 1 - i, 0)
    bw = lambda i: (i, 0)
    const = lambda i: (0, 0)

    def tok_specs(m):
        return [pl.BlockSpec((tt, QK_W), lambda i: (m(i)[0], OFF_Q // QK_W)),
                pl.BlockSpec((tt, QK_W), lambda i: (m(i)[0], OFF_K // QK_W)),
                pl.BlockSpec((tt, V_W), lambda i: (m(i)[0], OFF_V // V_W)),
                pl.BlockSpec((tt, LR_W), m),
                pl.BlockSpec((tt, V_W), m),
                pl.BlockSpec((nc, HEADS, DV, DK), lambda i: (m(i)[0], 0, 0, 0))]

    dqkv = jax.ShapeDtypeStruct((seq, QK_W + QK_W + V_W), F32)
    dlr = jax.ShapeDtypeStruct((seq, LR_W), F32)
    dw = jax.ShapeDtypeStruct((LR_W, QK_W), F32)
    dbias = jax.ShapeDtypeStruct((1, QK_W), F32)
    return pl.pallas_call(
        body, name="gla_bwd",
        out_shape=(dqkv, dlr, dqkv, dlr, dw, dw, dbias, dbias),
        grid=(nb,),
        in_specs=tok_specs(fw) + tok_specs(bw) + [
            pl.BlockSpec((LR_W, QK_W), const), pl.BlockSpec((LR_W, QK_W), const),
            pl.BlockSpec((1, QK_W), const), pl.BlockSpec((1, QK_W), const)],
        out_specs=(pl.BlockSpec((tt, QK_W + QK_W + V_W), fw), pl.BlockSpec((tt, LR_W), fw),
                   pl.BlockSpec((tt, QK_W + QK_W + V_W), bw), pl.BlockSpec((tt, LR_W), bw),
                   pl.BlockSpec((LR_W, QK_W), const), pl.BlockSpec((LR_W, QK_W), const),
                   pl.BlockSpec((1, QK_W), const), pl.BlockSpec((1, QK_W), const)),
        scratch_shapes=[pltpu.VMEM((2 * HEADS, DV, DK), F32)] + [pltpu.VMEM((tt, QK_W), F32)] * 4
        + [pltpu.VMEM((tt, QK_W), BF16)] * 4 + [pltpu.VMEM((tt, QK_W), F32)] * 2,
        compiler_params=_cparams("arbitrary"),
    )(proj, proj, proj, lr, do, st_f, proj, proj, proj, lr, do, st_b, wgk_f, wgk_b, bgk_f, bgk_b)


def _sum_directions(dqkv_f, dqkv_b, dlr_f, dlr_b, tm):
    seq = dqkv_f.shape[0]

    def body(a, b, la, lb, dp_out, dlr_out):
        dp_out[...] = (a[...] + b[...]).astype(BF16)
        dlr_out[...] = (la[...] + lb[...]).astype(BF16)

    rowt = pl.BlockSpec((tm, QKV_W), lambda i: (i, 0))
    lrt = pl.BlockSpec((tm, LR_W), lambda i: (i, 0))
    return pl.pallas_call(
        body, name="sum_directions",
        out_shape=(jax.ShapeDtypeStruct((seq, QKV_W), BF16), jax.ShapeDtypeStruct((seq, LR_W), BF16)),
        grid=(seq // tm,),
        in_specs=[rowt, rowt, lrt, lrt],
        out_specs=(rowt, lrt),
        compiler_params=_cparams("arbitrary"),
    )(dqkv_f, dqkv_b, dlr_f, dlr_b)


def _input_grad(dp_qkv, dp_gates, dp_ch, dlr, w_nat, x2d, norm_g, dx2, sums, tm):
    seq = x2d.shape[0]
    nt, n = seq // tm, len(sums)

    def body(dq, dg, dc, dl, w, x_ref, g_ref, dx2_ref, *rest):
        ins, (gx_ref, dng_ref), outs = rest[:n], rest[n:n + 2], rest[n + 2:2 * n + 2]
        send_sems, recv_sems = rest[2 * n + 2:]
        i = pl.program_id(0)

        @pl.when(i == 0)
        def _():
            for cp in _chip_copies(ins, outs, send_sems, recv_sems):
                cp.start()
            dng_ref[...] = jnp.zeros(dng_ref.shape, F32)

        dh = (_dot(dl[...], w[NAT_LR:NAT_LR + LR_W, :]) + _dot(dq[...], w[0:NAT_ZA, :])
              + _dot(dg[:, 0:CONV_W], w[NAT_ZA:NAT_LR, :]) + _dot(dg[:, CONV_W:2 * CONV_W], w[NAT_B:NAT_C, :])
              + _dot(dg[:, 2 * CONV_W:], w[NAT_ZC:IN_W, :]) + _dot(dc[...], w[NAT_C:NAT_ZC, :]))
        xv = x_ref[...]
        r = lax.rsqrt(jnp.mean(xv * xv, axis=-1, keepdims=True) + EPS)
        xn = xv * r
        dng_ref[...] += jnp.sum(dh * xn, axis=0, keepdims=True)
        dn = dh * g_ref[...]
        gx_ref[...] = (r * dn - xn * (r * jnp.mean(dn * xn, axis=-1, keepdims=True))) + dx2_ref[...]

        @pl.when(i == nt - 1)
        def _():
            copies = _chip_copies(ins, outs, send_sems, recv_sems)
            for cp in copies:
                cp.wait_recv()
            for cp in copies:
                cp.wait_send()

    rowt = pl.BlockSpec((tm, D_MODEL), lambda i: (i, 0))
    seg = lambda width: pl.BlockSpec((tm, width), lambda i: (i, 0))
    resident = lambda rows: pl.BlockSpec((rows, D_MODEL), lambda i: (0, 0), pipeline_mode=pl.Buffered(1))
    hbm = pl.BlockSpec(memory_space=pl.ANY)
    return pl.pallas_call(
        body, name="input_grad",
        out_shape=(jax.ShapeDtypeStruct((seq, D_MODEL), F32), jax.ShapeDtypeStruct((1, D_MODEL), F32))
        + tuple(jax.ShapeDtypeStruct((3,) + s.shape[1:], s.dtype) for s in sums),
        grid=(nt,),
        in_specs=[seg(QKV_W), seg(GATES_W), seg(CH_W), seg(LR_W), resident(IN_W),
                  rowt, pl.BlockSpec((1, D_MODEL), lambda i: (0, 0)), rowt] + [hbm] * n,
        out_specs=(rowt, pl.BlockSpec((1, D_MODEL), lambda i: (0, 0))) + (hbm,) * n,
        scratch_shapes=[pltpu.SemaphoreType.DMA((3 * n,)), pltpu.SemaphoreType.DMA((3 * n,))],
        compiler_params=_cparams("arbitrary"),
    )(dp_qkv, dp_gates, dp_ch, dlr, w_nat, x2d, norm_g, dx2, *sums)


def _weight_grad(at, b, tn, tk, name):
    m, seq = at.shape
    n = b.shape[1]

    def body(a_ref, b_ref, o_ref):
        @pl.when(pl.program_id(1) == 0)
        def _():
            o_ref[...] = jnp.zeros(o_ref.shape, F32)

        o_ref[...] += _dot(a_ref[...], b_ref[...])

    return pl.pallas_call(
        body, name=name,
        out_shape=jax.ShapeDtypeStruct((m, n), F32),
        grid=(n // tn, seq // tk),
        in_specs=[pl.BlockSpec((m, tk), lambda j, k: (0, k)), pl.BlockSpec((tk, tn), lambda j, k: (k, j))],
        out_specs=pl.BlockSpec((m, tn), lambda j, k: (0, j)),
        compiler_params=_cparams("arbitrary", "arbitrary"),
    )(at, b)


def _weight_grad_t(at, b, tn, name):
    m, seq = at.shape
    n = b.shape[1]

    def body(a_ref, b_ref, o_ref, acc):
        acc[...] = _dot(a_ref[...], b_ref[...])
        o_ref[...] = acc[...].T

    return pl.pallas_call(
        body, name=name,
        out_shape=jax.ShapeDtypeStruct((n, m), F32),
        grid=(n // tn,),
        in_specs=[pl.BlockSpec((m, seq), lambda j: (0, 0), pipeline_mode=pl.Buffered(1)),
                  pl.BlockSpec((seq, tn), lambda j: (0, j))],
        out_specs=pl.BlockSpec((tn, m), lambda j: (j, 0)),
        scratch_shapes=[pltpu.VMEM((m, tn), F32)],
        compiler_params=_cparams("arbitrary"),
    )(at, b)


def _pad_rows(a, rows):
    return jnp.pad(a, ((0, rows - a.shape[0]), (0, 0)))


def _rows128(a):
    a = a.reshape(-1, 128)
    return _pad_rows(a, -(-a.shape[0] // 8) * 8)


def _pack(arrs):
    return jnp.concatenate([_rows128(a) for a in arrs], axis=0)


def _unpack(buf, like):
    out, start = [], 0
    for a in like:
        rows = a.size // 128
        out.append(buf[start:start + rows].reshape(a.shape))
        start += -(-rows // 8) * 8
    return out


def kernel(x, norm_g, w_in, w_gk_f, b_gk_f, w_gk_b, b_gk_b, gla_norm_g, conv_w, conv_b, w_out, final_g, loss_target, m_norm_g, m_w_in, m_w_gk_f, m_b_gk_f, m_w_gk_b, m_b_gk_b, m_gla_norm_g, m_conv_w, m_conv_b, m_w_out, m_final_g, v_norm_g, v_w_in, v_w_gk_f, v_b_gk_f, v_w_gk_b, v_b_gk_b, v_gla_norm_g, v_conv_w, v_conv_b, v_w_out, v_final_g):
    px, py, pc = _position()
    me = _blk(px, py, pc)
    seq = x.shape[1]
    x2d, tgt = x[0], loss_target[0]
    tm = min(512, seq)
    tt = min(256, seq)

    small_s = jnp.concatenate([jnp.concatenate([w_gk_f[0], w_gk_b[0]], axis=1), _pad_rows(conv_w[0], 8)], axis=0)
    w_nat = _allgather_w_in(w_in[0].T).reshape(IN_W, D_MODEL)

    proj, lr, h_t, wout_all, small_all = _inproj(x2d, norm_g, w_nat, w_out[0], small_s, tm)
    w_out_full = wout_all.reshape(MIX_W, D_MODEL)
    wgk_cols = 512 // N_DEV
    wgk_f_full = small_all[:, 0:RANK, 0:wgk_cols].transpose(1, 0, 2).reshape(RANK, QK_W)
    wgk_b_full = small_all[:, 0:RANK, wgk_cols:2 * wgk_cols].transpose(1, 0, 2).reshape(RANK, QK_W)
    conv_w_full = _pad_rows(small_all[:, RANK:RANK + 3, :].transpose(1, 0, 2).reshape(3, CONV_W), 8)
    zr = lambda n: jnp.zeros((n, QK_W), F32)
    wgk_f_pad = jnp.concatenate([wgk_f_full, zr(LR_W - RANK)], axis=0).astype(BF16)
    wgk_b_pad = jnp.concatenate([zr(RANK), wgk_b_full, zr(LR_W - 2 * RANK)], axis=0).astype(BF16)

    o_f, o_b, st_f, st_b = _gla_fwd(proj, lr, wgk_f_pad, wgk_b_pad, b_gk_f, b_gk_b, tt)
    tmix = min(256, seq)
    y_t, conv, dx2, dx2b, loss_p, dfg_p = _mix_out_loss(o_f, o_b, proj, x2d, tgt, gla_norm_g, conv_w_full, conv_b,
                                                        w_out_full, final_g.reshape(1, D_MODEL), tmix)

    dp_gates, do, dconv, dgg_p, dcb_p = _mix_bwd(dx2b, o_f, o_b, proj, conv, gla_norm_g, w_out_full, tmix)
    dp_ch, dcw_p = _conv_bwd(dconv, proj, conv_w_full, tmix)
    dqkv_f, dlr_f, dqkv_b, dlr_b, dwf_p, dwb_p, dbf_p, dbb_p = _gla_bwd(
        proj, lr, do, st_f, st_b, wgk_f_pad, wgk_b_pad, b_gk_f, b_gk_b, tt)
    dp_qkv, dlr = _sum_directions(dqkv_f, dqkv_b, dlr_f, dlr_b, tm)
    dw_qkv = _weight_grad_t(h_t, dp_qkv, 512, "wgrad_qkv")
    dw_gates = _weight_grad_t(h_t, dp_gates, 512, "wgrad_gates")
    dw_ch = _weight_grad_t(h_t, dp_ch, 512, "wgrad_ch")
    dw_lr = _weight_grad_t(h_t, dlr, LR_W, "wgrad_lr")
    dw_out = _weight_grad(y_t, dx2b, D_MODEL, tm, "wgrad_out")

    dw_nat = jnp.concatenate([dw_qkv, dw_gates[:CONV_W], dw_lr[:2 * RANK], dw_gates[CONV_W:2 * CONV_W], dw_ch,
                              dw_gates[2 * CONV_W:]], axis=0)
    part_in = dw_nat.reshape(N_DEV, SHARD_W, D_MODEL)
    part_out = dw_out.reshape(N_DEV, MIX_W // N_DEV, D_MODEL)
    sib_in, sib_out = _exchange_sibling([part_in, part_out])
    core = jnp.reshape(pc, (1,)).astype(jnp.int32)
    chip = jnp.reshape(2 * px + py, (1,)).astype(jnp.int32)
    sums_in = _chip_sums(part_in, sib_in, core, 256, "chip_sums_in")
    sums_out = _chip_sums(part_out, sib_out, core, 256, "chip_sums_out")
    grad_x2d, dng_p, far_in, far_out = _input_grad(dp_qkv, dp_gates, dp_ch, dlr, w_nat, x2d, norm_g, dx2,
                                                   [sums_in, sums_out], tmix)
    g_in_t = _final_sum(sums_in, far_in, chip, 256, "final_sum_in")
    g_w_out, d_w_out, nm_w_out, nv_w_out = _final_sum_adamw(sums_out, far_out, chip, w_out[0], m_w_out[0], v_w_out[0],
                                                            256, "adamw_out")
    flat = lambda a: a[0].T.reshape(SHARD_W * D_MODEL // 128, 128)
    unflat = lambda a: a.reshape(SHARD_W, D_MODEL).T
    d_flat, m_flat, v_flat = _adamw_rows(g_in_t.reshape(SHARD_W * D_MODEL // 128, 128), flat(w_in), flat(m_w_in),
                                         flat(v_w_in), 720, "adamw_in")
    g_w_in, d_w_in, nm_w_in, nv_w_in = g_in_t.T, unflat(d_flat), unflat(m_flat), unflat(v_flat)

    pieces = [dng_p, dbf_p, dbb_p, dgg_p, dcb_p, dfg_p[0], dwf_p[0:RANK], dwb_p[RANK:2 * RANK], dcw_p[0:3], loss_p[0]]
    tot = _unpack(_allreduce_small(_pack(pieces)), pieces)
    g_norm_g, g_b_gk_f, g_b_gk_b, g_gla, g_conv_b, g_final = tot[:6]
    g_wgk_f = lax.dynamic_slice_in_dim(tot[6], me * wgk_cols, wgk_cols, axis=1)[None]
    g_wgk_b = lax.dynamic_slice_in_dim(tot[7], me * wgk_cols, wgk_cols, axis=1)[None]
    g_conv_w = lax.dynamic_slice_in_dim(tot[8], me * 128, 128, axis=1)[None]
    loss = tot[9][0]

    small_g = [g_norm_g, g_b_gk_f, g_b_gk_b, g_gla, g_conv_b, g_final, g_wgk_f, g_wgk_b, g_conv_w]
    small_w = [norm_g, b_gk_f, b_gk_b, gla_norm_g, conv_b, final_g, w_gk_f, w_gk_b, conv_w]
    small_m = [m_norm_g, m_b_gk_f, m_b_gk_b, m_gla_norm_g, m_conv_b, m_final_g, m_w_gk_f, m_w_gk_b, m_conv_w]
    small_v = [v_norm_g, v_b_gk_f, v_b_gk_b, v_gla_norm_g, v_conv_b, v_final_g, v_w_gk_f, v_w_gk_b, v_conv_w]
    d_s, m_s, v_s = _adamw_small(_pack(small_g), _pack(small_w), _pack(small_m), _pack(small_v))
    d_l, m_l, v_l = _unpack(d_s, small_w), _unpack(m_s, small_w), _unpack(v_s, small_w)

    def ordered(sm, big_in, big_out):
        return [sm[0], big_in[None], sm[6], sm[1], sm[7], sm[2], sm[3], sm[8], sm[4], big_out[None], sm[5]]

    grads = ordered(small_g, g_w_in, g_w_out)
    deltas = ordered(d_l, d_w_in, d_w_out)
    new_m = ordered(m_l, nm_w_in, nm_w_out)
    new_v = ordered(v_l, nv_w_in, nv_w_out)
    return (loss, grad_x2d[None], *grads, *deltas, *new_m, *new_v)
```

```python
import jax
import jax.numpy as jnp
from jax import lax
from jax.experimental import pallas as pl
from jax.experimental.pallas import tpu as pltpu

F32 = jnp.float32
BF16 = jnp.bfloat16
MESH = pl.DeviceIdType.MESH

N_DEV = 8
D_MODEL = 1024
HEADS = 4
DK = 128
DV = 256
QK_W = HEADS * DK
V_W = HEADS * DV
CONV_W = 1024
MIX_W = V_W + CONV_W
CHUNK = 64
RANK = 16
IN_W = 7200
SHARD_W = IN_W // N_DEV
MAIN_W = 7168
LR_W = 128
OFF_Q, OFF_K, OFF_V, OFF_ZA, OFF_B, OFF_ZC, OFF_C, OFF_H = 0, 512, 1024, 2048, 3072, 4096, 5120, 6144
QKV_W, GATES_W, CH_W = 2048, 3072, 2048
NAT_ZA, NAT_LR, NAT_B, NAT_C, NAT_ZC = 2048, 3072, 3104, 4128, 6176
EPS = 1e-6
GATE_SCALE = 1.0 / 16.0
QSCALE = DK ** -0.5
REF_F, LAST_F = CHUNK // 2, CHUNK - 1
REF_B, LAST_B = CHUNK - 1 - CHUNK // 2, 0

ADAM_LR = 0.001
ADAM_B1 = 0.9
ADAM_B2 = 0.999
ADAM_EPS = 1e-08
ADAM_WD = 0.01
ADAM_STEP = 10

VMEM_LIMIT = 56 * 1024 * 1024


def _cparams(*sem):
    return pltpu.CompilerParams(dimension_semantics=sem, vmem_limit_bytes=VMEM_LIMIT)


def _dot(a, b):
    return jnp.dot(a, b, preferred_element_type=F32)


def _dot_nt(a, b):
    return lax.dot_general(a, b, (((1,), (1,)), ((), ())), preferred_element_type=F32)


def _dot_tn(a, b):
    return lax.dot_general(a, b, (((0,), (0,)), ((), ())), preferred_element_type=F32)


def _sigmoid(z):
    return jax.nn.sigmoid(z)


def _position():
    return lax.axis_index("x"), lax.axis_index("y"), lax.axis_index("c")


def _blk(px, py, pc):
    return 4 * px + 2 * py + pc


def _two_level_gather(outs, send_sems, recv_sems):
    x, y, c = _position()
    me, sibling = (x, y, c), (x, y, 1 - c)
    chips = [(1 - x, y), (x, 1 - y), (1 - x, 1 - y)]
    n = len(outs)

    def copy(a, k, block, to):
        ref = outs[a].at[_blk(*block)]
        return pltpu.make_async_remote_copy(src_ref=ref, dst_ref=ref, send_sem=send_sems.at[a * 7 + k],
                                            recv_sem=recv_sems.at[a * 7 + k], device_id=to, device_id_type=MESH)

    first = []
    for a in range(n):
        first.append(copy(a, 0, me, sibling))
        first += [copy(a, 1 + j, me, (*chip, c)) for j, chip in enumerate(chips)]
    for cp in first:
        cp.start()
    passed = []
    for j, chip in enumerate(chips):
        for a in range(n):
            copy(a, 1 + j, (*chip, c), me).wait_recv()
            fwd = copy(a, 4 + j, (*chip, c), sibling)
            fwd.start()
            passed.append(fwd)
    for a in range(n):
        copy(a, 0, sibling, me).wait_recv()
    for j, chip in enumerate(chips):
        for a in range(n):
            copy(a, 4 + j, (*chip, 1 - c), me).wait_recv()
    for cp in first + passed:
        cp.wait_send()


def _allgather_w_in(w_in_t):
    def body(win_ref, win_all, send_sems, recv_sems):
        win_all[_blk(*_position())] = win_ref[...].astype(BF16)
        _two_level_gather((win_all,), send_sems, recv_sems)

    vmem = pl.BlockSpec(memory_space=pltpu.VMEM)
    return pl.pallas_call(
        body, name="allgather_w_in",
        out_shape=jax.ShapeDtypeStruct((N_DEV,) + w_in_t.shape, BF16),
        in_specs=[vmem], out_specs=vmem,
        scratch_shapes=[pltpu.SemaphoreType.DMA((7,)), pltpu.SemaphoreType.DMA((7,))],
        compiler_params=pltpu.CompilerParams(vmem_limit_bytes=VMEM_LIMIT),
    )(w_in_t)


def _peer_copies(srcs, outs, send_sems, recv_sems):
    x, y, c = _position()
    me = _blk(x, y, c)
    copies = []
    for a, (src, out) in enumerate(zip(srcs, outs)):
        k = 0
        for dx in (0, 1):
            for dy in (0, 1):
                for dc in (0, 1):
                    if dx + dy + dc == 0:
                        continue
                    peer = (1 - x if dx else x, 1 - y if dy else y, 1 - c if dc else c)
                    copies.append(pltpu.make_async_remote_copy(
                        src_ref=src, dst_ref=out.at[me], send_sem=send_sems.at[a * 7 + k],
                        recv_sem=recv_sems.at[a * 7 + k], device_id=peer, device_id_type=MESH))
                    k += 1
    return copies


def _chip_copies(ins, outs, send_sems, recv_sems):
    x, y, c = _position()
    chips = [(1 - x, y), (x, 1 - y), (1 - x, 1 - y)]
    copies = []
    for a in range(len(ins)):
        for j, (px, py) in enumerate(chips):
            copies.append(pltpu.make_async_remote_copy(
                src_ref=ins[a].at[2 * px + py], dst_ref=outs[a].at[j],
                send_sem=send_sems.at[a * 3 + j], recv_sem=recv_sems.at[a * 3 + j],
                device_id=(px, py, c), device_id_type=MESH))
    return copies


def _allreduce_small(part):
    def body(p_ref, tot_ref, all_ref, send_sems, recv_sems):
        mine = _blk(*_position())
        all_ref[mine] = p_ref[...]
        _two_level_gather((all_ref,), send_sems, recv_sems)
        acc = all_ref[0]
        for d in range(1, N_DEV):
            acc = acc + all_ref[d]
        tot_ref[...] = acc

    vmem = pl.BlockSpec(memory_space=pltpu.VMEM)
    return pl.pallas_call(
        body, name="allreduce_small",
        out_shape=jax.ShapeDtypeStruct(part.shape, F32),
        in_specs=[vmem], out_specs=vmem,
        scratch_shapes=[pltpu.VMEM((N_DEV,) + part.shape, F32),
                        pltpu.SemaphoreType.DMA((7,)), pltpu.SemaphoreType.DMA((7,))],
        compiler_params=pltpu.CompilerParams(vmem_limit_bytes=VMEM_LIMIT),
    )(part)


def _exchange_sibling(parts):
    n = len(parts)

    def body(*refs):
        ins, outs = refs[:n], refs[n:2 * n]
        send_sems, recv_sems = refs[2 * n], refs[2 * n + 1]
        x, y, c = _position()
        sibling = (x, y, 1 - c)
        copies = []
        for a in range(n):
            for k in range(4):
                copies.append(pltpu.make_async_remote_copy(
                    src_ref=ins[a].at[2 * k + (1 - c)], dst_ref=outs[a].at[k],
                    send_sem=send_sems.at[a * 4 + k], recv_sem=recv_sems.at[a * 4 + k],
                    device_id=sibling, device_id_type=MESH))
        for cp in copies:
            cp.start()
        for cp in copies:
            cp.wait_recv()
        for cp in copies:
            cp.wait_send()

    hbm = pl.BlockSpec(memory_space=pl.ANY)
    return pl.pallas_call(
        body, name="exchange_sibling",
        out_shape=tuple(jax.ShapeDtypeStruct((4,) + p.shape[1:], F32) for p in parts),
        in_specs=[hbm] * n, out_specs=tuple([hbm] * n),
        scratch_shapes=[pltpu.SemaphoreType.DMA((4 * n,)), pltpu.SemaphoreType.DMA((4 * n,))],
    )(*parts)


def _chip_sums(part, from_sibling, core, tc, name):
    _, rows, cols = part.shape

    def body(core_ref, p_ref, s_ref, o_ref):
        o_ref[...] = (p_ref[...] + s_ref[...]).astype(BF16)

    return pl.pallas_call(
        body, name=name,
        out_shape=jax.ShapeDtypeStruct((4, rows, cols), BF16),
        grid_spec=pltpu.PrefetchScalarGridSpec(
            num_scalar_prefetch=1, grid=(4, cols // tc),
            in_specs=[pl.BlockSpec((1, rows, tc), lambda k, j, core_ref: (2 * k + core_ref[0], 0, j)),
                      pl.BlockSpec((1, rows, tc), lambda k, j, core_ref: (k, 0, j))],
            out_specs=pl.BlockSpec((1, rows, tc), lambda k, j, core_ref: (k, 0, j))),
        compiler_params=_cparams("arbitrary", "arbitrary"),
    )(core, part, from_sibling)


def _sum_chips(s_ref, r_ref):
    f = lambda a: a.astype(F32)
    return ((f(s_ref[0]) + f(r_ref[0])) + f(r_ref[1])) + f(r_ref[2])


def _final_sum(sums, from_chips, chip, tc, name):
    _, rows, cols = sums.shape

    def body(chip_ref, s_ref, r_ref, g_out):
        g_out[...] = _sum_chips(s_ref, r_ref)

    return pl.pallas_call(
        body, name=name,
        out_shape=jax.ShapeDtypeStruct((rows, cols), F32),
        grid_spec=pltpu.PrefetchScalarGridSpec(
            num_scalar_prefetch=1, grid=(cols // tc,),
            in_specs=[pl.BlockSpec((1, rows, tc), lambda j, chip_ref: (chip_ref[0], 0, j)),
                      pl.BlockSpec((3, rows, tc), lambda j, chip_ref: (0, 0, j))],
            out_specs=pl.BlockSpec((rows, tc), lambda j, chip_ref: (0, j))),
        compiler_params=_cparams("arbitrary"),
    )(chip, sums, from_chips)


def _adamw_rows(g, w, m, v, tr, name):
    rows = g.shape[0]

    def body(g_ref, w_ref, m_ref, v_ref, d_out, m_out, v_out):
        delta, m_new, v_new = _adamw(w_ref[...], g_ref[...], m_ref[...], v_ref[...])
        d_out[...] = delta
        m_out[...] = m_new
        v_out[...] = v_new

    tile = pl.BlockSpec((tr,) + g.shape[1:], lambda r: (r, 0, 0))
    shp = jax.ShapeDtypeStruct(g.shape, F32)
    return pl.pallas_call(
        body, name=name, out_shape=(shp, shp, shp), grid=(rows // tr,),
        in_specs=[tile] * 4, out_specs=(tile, tile, tile),
        compiler_params=_cparams("arbitrary"),
    )(g, w, m, v)


def _adamw(w, g, m, v):
    m = ADAM_B1 * m + (1.0 - ADAM_B1) * g
    v = ADAM_B2 * v + (1.0 - ADAM_B2) * (g * g)
    m_hat = m / (1.0 - ADAM_B1 ** ADAM_STEP)
    v_hat = v / (1.0 - ADAM_B2 ** ADAM_STEP)
    delta = -ADAM_LR * (m_hat / (jnp.sqrt(v_hat) + ADAM_EPS) + ADAM_WD * w)
    return delta, m, v


def _final_sum_adamw(sums, from_chips, chip, w, m, v, tr, name):
    rows, cols = w.shape

    def body(chip_ref, s_ref, r_ref, w_ref, m_ref, v_ref, g_out, d_out, m_out, v_out):
        g = _sum_chips(s_ref, r_ref)
        delta, m_new, v_new = _adamw(w_ref[...], g, m_ref[...], v_ref[...])
        g_out[...] = g
        d_out[...] = delta
        m_out[...] = m_new
        v_out[...] = v_new

    tile = pl.BlockSpec((tr, cols), lambda r, chip_ref: (r, 0))
    shp = jax.ShapeDtypeStruct((rows, cols), F32)
    return pl.pallas_call(
        body, name=name,
        out_shape=(shp, shp, shp, shp),
        grid_spec=pltpu.PrefetchScalarGridSpec(
            num_scalar_prefetch=1, grid=(rows // tr,),
            in_specs=[pl.BlockSpec((1, tr, cols), lambda r, chip_ref: (chip_ref[0], r, 0)),
                      pl.BlockSpec((3, tr, cols), lambda r, chip_ref: (0, r, 0)),
                      tile, tile, tile],
            out_specs=(tile, tile, tile, tile)),
        compiler_params=_cparams("arbitrary"),
    )(chip, sums, from_chips, w, m, v)


def _adamw_small(g, w, m, v):
    def body(g_ref, w_ref, m_ref, v_ref, d_out, m_out, v_out):
        delta, m_new, v_new = _adamw(w_ref[...], g_ref[...], m_ref[...], v_ref[...])
        d_out[...] = delta
        m_out[...] = m_new
        v_out[...] = v_new

    vmem = pl.BlockSpec(memory_space=pltpu.VMEM)
    shp = jax.ShapeDtypeStruct(g.shape, F32)
    return pl.pallas_call(body, name="adamw_small", out_shape=(shp, shp, shp),
                          in_specs=[vmem] * 4, out_specs=(vmem, vmem, vmem))(g, w, m, v)


TILE_ROWS = (0, 1024, NAT_ZA, NAT_B, NAT_ZC, NAT_C, NAT_C + CONV_W)


def _inproj(x2d, norm_g, w_nat, w_out_s, small_s, tm):
    seq = x2d.shape[0]
    tn = CONV_W
    ni, nj = seq // tm, MAIN_W // tn
    first_sweep = lambda j, i: jnp.where(j == 0, i, ni - 1)

    def tile_row(j, i):
        row = 0
        for k, start in enumerate(TILE_ROWS):
            row = row + jnp.where(j == k, start // 32, 0)
        return pl.multiple_of(row * 32, 32), 0

    def body(x_ref, g_ref, w_ref, wlr_ref, wout_ref, sm_ref, proj_ref, lr_ref, ht_ref, wout_all, sm_all,
             h_all, wout_b, sm_b, send_sems, recv_sems, local_sems):
        j, i = pl.program_id(0), pl.program_id(1)
        rows = pl.ds(pl.multiple_of(i * tm, tm), tm)
        me = _blk(*_position())

        def gather_copies():
            mine = [pltpu.make_async_copy(wout_b, wout_all.at[me], local_sems.at[0]),
                    pltpu.make_async_copy(sm_b, sm_all.at[me], local_sems.at[1])]
            return mine, _peer_copies((wout_b, sm_b), (wout_all, sm_all), send_sems, recv_sems)

        @pl.when(jnp.logical_and(j == 0, i == 0))
        def _():
            wout_b[...] = wout_ref[...].astype(BF16)
            sm_b[...] = sm_ref[...]
            mine, remote = gather_copies()
            for cp in mine + remote:
                cp.start()

        @pl.when(j == 0)
        def _():
            xv = x_ref[...]
            r = lax.rsqrt(jnp.mean(xv * xv, axis=-1, keepdims=True) + EPS)
            h = (xv * r) * g_ref[...]
            hb = h.astype(BF16)
            h_all[rows, :] = hb
            ht_ref[...] = h.T.astype(BF16)
            lr_ref[...] = _dot_nt(hb, wlr_ref[...])

        proj_ref[...] = _dot_nt(h_all[rows, :], w_ref[...]).astype(BF16)

        @pl.when(jnp.logical_and(j == nj - 1, i == ni - 1))
        def _():
            mine, remote = gather_copies()
            for cp in remote:
                cp.wait_recv()
            for cp in remote:
                cp.wait_send()
            for cp in mine:
                cp.wait()

    const = lambda shape: pl.BlockSpec(shape, lambda j, i: (0,) * len(shape))
    hbm = pl.BlockSpec(memory_space=pl.ANY)
    return pl.pallas_call(
        body, name="inproj",
        out_shape=(jax.ShapeDtypeStruct((seq, MAIN_W), BF16), jax.ShapeDtypeStruct((seq, LR_W), F32),
                   jax.ShapeDtypeStruct((D_MODEL, seq), BF16),
                   jax.ShapeDtypeStruct((N_DEV,) + w_out_s.shape, BF16),
                   jax.ShapeDtypeStruct((N_DEV,) + small_s.shape, F32)),
        grid=(nj, ni),
        in_specs=[pl.BlockSpec((tm, D_MODEL), lambda j, i: (first_sweep(j, i), 0)),
                  const((1, D_MODEL)),
                  pl.BlockSpec((pl.Element(tn), pl.Element(D_MODEL)), tile_row),
                  pl.BlockSpec((pl.Element(LR_W), pl.Element(D_MODEL)), lambda j, i: (NAT_LR, 0)),
                  const(w_out_s.shape), const(small_s.shape)],
        out_specs=(pl.BlockSpec((tm, tn), lambda j, i: (i, j)),
                   pl.BlockSpec((tm, LR_W), lambda j, i: (first_sweep(j, i), 0)),
                   pl.BlockSpec((D_MODEL, tm), lambda j, i: (0, first_sweep(j, i))), hbm, hbm),
        scratch_shapes=[pltpu.VMEM((seq, D_MODEL), BF16), pltpu.VMEM(w_out_s.shape, BF16),
                        pltpu.VMEM(small_s.shape, F32), pltpu.SemaphoreType.DMA((14,)),
                        pltpu.SemaphoreType.DMA((14,)), pltpu.SemaphoreType.DMA((2,))],
        compiler_params=_cparams("arbitrary", "arbitrary"),
    )(x2d, norm_g, w_nat, w_nat, w_out_s, small_s)


def _block_masks(tt):
    row = lax.broadcasted_iota(jnp.int32, (tt, tt), 0)
    col = lax.broadcasted_iota(jnp.int32, (tt, tt), 1)
    same = jnp.right_shift(row, 6) == jnp.right_shift(col, 6)
    return (jnp.logical_and(same, col <= row), jnp.logical_and(same, col >= row), jnp.logical_and(same, col > row))


def _dot_split3(ones_mat, x):
    x1 = x.astype(BF16)
    r1 = x - x1.astype(F32)
    x2 = r1.astype(BF16)
    x3 = (r1 - x2.astype(F32)).astype(BF16)
    return (_dot(ones_mat, x3) + _dot(ones_mat, x2)) + _dot(ones_mat, x1)


def _log_gate(logits):
    return (jnp.minimum(logits, 0.0) - jnp.log1p(jnp.exp(-jnp.abs(logits)))) * GATE_SCALE


def _chunk_column_mask(tt):
    nc = tt // CHUNK
    row = lax.broadcasted_iota(jnp.int32, (tt, nc * DK), 0)
    col = lax.broadcasted_iota(jnp.int32, (tt, nc * DK), 1)
    return jnp.right_shift(row, 6) == jnp.right_shift(col, 7)


def _chunked(mask, x, nc):
    wide = jnp.concatenate([x] * nc, axis=1)
    return jnp.where(mask, wide, jnp.zeros_like(wide))


def _gla_fwd(proj, lr, wgk_f, wgk_b, bgk_f, bgk_b, tt):
    seq = proj.shape[0]
    nb, nc, nch = seq // tt, tt // CHUNK, seq // CHUNK

    def body(qf, kf, vf, lrf, qb, kb, vb, lrb, wf, wb, bf, bb, of, ob, stf, stb, s_scr, qs_s, ks_s, qin_s, kout_s):
        @pl.when(pl.program_id(0) == 0)
        def _():
            s_scr[...] = jnp.zeros(s_scr.shape, F32)

        low, upp, sup = _block_masks(tt)
        dirs = ((qf, kf, vf, lrf, wf, bf, of, stf, low, low, REF_F, LAST_F, list(range(nc))),
                (qb, kb, vb, lrb, wb, bb, ob, stb, upp, sup, REF_B, LAST_B, list(reversed(range(nc)))))
        for d, (q_r, k_r, v_r, lr_r, w_r, b_r, o_r, st_r, cum, mask, ref, last, order) in enumerate(dirs):
            logits = _dot(lr_r[...].astype(BF16), w_r[...]) + b_r[...]
            b = _dot_split3(cum.astype(BF16), _log_gate(logits))
            decs = []
            for c in range(nc):
                rows = slice(c * CHUNK, (c + 1) * CHUNK)
                bc = b[rows]
                b_ref, b_last = bc[ref:ref + 1], bc[last:last + 1]
                qc = q_r[rows, :].astype(F32) * QSCALE
                kc = k_r[rows, :].astype(F32)
                qs_s[rows, :] = (qc * jnp.exp(bc - b_ref)).astype(BF16)
                ks_s[rows, :] = (kc * jnp.exp(b_ref - bc)).astype(BF16)
                qin_s[rows, :] = (qc * jnp.exp(bc)).astype(BF16)
                kout_s[rows, :] = (kc * jnp.exp(b_last - bc)).astype(BF16)
                decs.append(jnp.exp(b_last))
            for h in range(HEADS):
                ksl = slice(h * DK, (h + 1) * DK)
                vsl = slice(h * DV, (h + 1) * DV)
                v = v_r[:, vsl].astype(BF16)
                att = jnp.where(mask, _dot_nt(qs_s[:, ksl], ks_s[:, ksl]), 0.0).astype(BF16)
                o_intra = _dot(att, v)
                st = s_scr[d * HEADS + h]
                for c in order:
                    rows = slice(c * CHUNK, (c + 1) * CHUNK)
                    stb = st.astype(BF16)
                    st_r[c, h] = stb
                    o_r[rows, vsl] = (o_intra[rows] + _dot_nt(qin_s[rows, ksl], stb)).astype(BF16)
                    st = st * decs[c][:, ksl] + _dot_tn(v[rows], kout_s[rows, ksl])
                s_scr[d * HEADS + h] = st

    fw = lambda i: (i, 0)
    bw = lambda i: (nb - 1 - i, 0)
    const = lambda i: (0, 0)

    def tok_specs(m):
        return [pl.BlockSpec((tt, QK_W), lambda i: (m(i)[0], OFF_Q // QK_W)),
                pl.BlockSpec((tt, QK_W), lambda i: (m(i)[0], OFF_K // QK_W)),
                pl.BlockSpec((tt, V_W), lambda i: (m(i)[0], OFF_V // V_W)),
                pl.BlockSpec((tt, LR_W), m)]

    st_shape = jax.ShapeDtypeStruct((nch, HEADS, DV, DK), BF16)
    o_shape = jax.ShapeDtypeStruct((seq, V_W), BF16)
    operand = pltpu.VMEM((tt, QK_W), BF16)
    return pl.pallas_call(
        body, name="gla_fwd",
        out_shape=(o_shape, o_shape, st_shape, st_shape),
        grid=(nb,),
        in_specs=tok_specs(fw) + tok_specs(bw) + [
            pl.BlockSpec((LR_W, QK_W), const), pl.BlockSpec((LR_W, QK_W), const),
            pl.BlockSpec((1, QK_W), const), pl.BlockSpec((1, QK_W), const)],
        out_specs=(pl.BlockSpec((tt, V_W), fw), pl.BlockSpec((tt, V_W), bw),
                   pl.BlockSpec((nc, HEADS, DV, DK), lambda i: (i, 0, 0, 0)),
                   pl.BlockSpec((nc, HEADS, DV, DK), lambda i: (nb - 1 - i, 0, 0, 0))),
        scratch_shapes=[pltpu.VMEM((2 * HEADS, DV, DK), F32), operand, operand, operand, operand],
        compiler_params=_cparams("arbitrary"),
    )(proj, proj, proj, lr, proj, proj, proj, lr, wgk_f, wgk_b, bgk_f, bgk_b)


def _head_norm(o, gain):
    outs, rinv = [], []
    for h in range(HEADS):
        oh = o[:, h * DV:(h + 1) * DV]
        r = lax.rsqrt(jnp.mean(oh * oh, axis=-1, keepdims=True) + EPS)
        outs.append((oh * r) * gain)
        rinv.append(r)
    return jnp.concatenate(outs, axis=1), rinv


def _shift_rows(u, prev_row, next_row):
    n = u.shape[0]
    row = lax.broadcasted_iota(jnp.int32, (n, 1), 0)
    up = jnp.where(row == 0, prev_row, pltpu.roll(u, 1, 0))
    un = jnp.where(row == n - 1, next_row, pltpu.roll(u, n - 1, 0))
    return up, un


HALO = 16


def _halo_specs(tm, seq, col_block):
    per = tm // HALO
    last = seq // HALO - 1
    return [pl.BlockSpec((HALO, CONV_W), lambda i: (jnp.maximum(i * per - 1, 0), col_block)),
            pl.BlockSpec((HALO, CONV_W), lambda i: (jnp.minimum((i + 1) * per, last), col_block))]


def _f32(ref):
    return ref[...].astype(F32)


def _last_row(ref):
    return ref[HALO - 1:HALO, :].astype(F32)


def _first_row(ref):
    return ref[0:1, :].astype(F32)


def _mix_out_loss(o_f, o_b, proj, x2d, tgt, gla_g, conv_w, conv_b, w_out, final_g, tm):
    seq = x2d.shape[0]
    nt = seq // tm

    def body(of, ob, za, bg, cg, hc, zc, cprev, cnext, hprev, hnext, x_ref, t_ref, gg, cw, cb, wo, fg,
             yt_ref, conv_ref, dx2_ref, dx2b_ref, loss_ref, dfg_ref):
        i = pl.program_id(0)

        @pl.when(i == 0)
        def _():
            loss_ref[...] = jnp.zeros(loss_ref.shape, F32)
            dfg_ref[...] = jnp.zeros(dfg_ref.shape, F32)

        on, _ = _head_norm(_f32(of) + _f32(ob), gg[...])
        zav = _f32(za)
        y_a = on * (zav * _sigmoid(zav))
        u = _f32(cg) * _f32(hc)
        prev_row = jnp.where(i > 0, _last_row(cprev) * _last_row(hprev), 0.0)
        next_row = jnp.where(i < nt - 1, _first_row(cnext) * _first_row(hnext), 0.0)
        up, un = _shift_rows(u, prev_row, next_row)
        conv = (cw[0:1, :] * up + cw[1:2, :] * u + cw[2:3, :] * un) + cb[...]
        conv_ref[...] = conv.astype(BF16)
        zcv = _f32(zc)
        y_c = _f32(bg) * conv * (zcv * _sigmoid(zcv))
        y = jnp.concatenate([y_a, y_c], axis=1)
        yt_ref[...] = y.T.astype(BF16)
        x2 = x_ref[...] + _dot(y.astype(BF16), wo[...])
        r = lax.rsqrt(jnp.mean(x2 * x2, axis=-1, keepdims=True) + EPS)
        xn = x2 * r
        err = xn * fg[...] - t_ref[...]
        loss_ref[...] += 0.5 * jnp.sum(jnp.mean(err * err, axis=-1, keepdims=True))
        dyf = err * (1.0 / D_MODEL)
        dfg_ref[...] += jnp.sum(dyf * xn, axis=0, keepdims=True)
        dxn = dyf * fg[...]
        dx2 = r * dxn - xn * (r * jnp.mean(dxn * xn, axis=-1, keepdims=True))
        dx2_ref[...] = dx2
        dx2b_ref[...] = dx2.astype(BF16)

    def col(off):
        return pl.BlockSpec((tm, CONV_W), lambda i: (i, off // CONV_W))

    rowt = pl.BlockSpec((tm, D_MODEL), lambda i: (i, 0))
    const = lambda shape: pl.BlockSpec(shape, lambda i: (0, 0))
    return pl.pallas_call(
        body, name="mix_out_loss",
        out_shape=(jax.ShapeDtypeStruct((MIX_W, seq), BF16), jax.ShapeDtypeStruct((seq, CONV_W), BF16),
                   jax.ShapeDtypeStruct((seq, D_MODEL), F32), jax.ShapeDtypeStruct((seq, D_MODEL), BF16),
                   jax.ShapeDtypeStruct((8, 128), F32), jax.ShapeDtypeStruct((1, D_MODEL), F32)),
        grid=(nt,),
        in_specs=[rowt, rowt, col(OFF_ZA), col(OFF_B), col(OFF_C), col(OFF_H), col(OFF_ZC)]
        + _halo_specs(tm, seq, OFF_C // CONV_W) + _halo_specs(tm, seq, OFF_H // CONV_W)
        + [rowt, rowt, const((1, DV)), const((8, CONV_W)), const((1, CONV_W)), const((MIX_W, D_MODEL)),
           const((1, D_MODEL))],
        out_specs=(pl.BlockSpec((MIX_W, tm), lambda i: (0, i)), rowt, rowt, rowt, const((8, 128)),
                   const((1, D_MODEL))),
        compiler_params=_cparams("arbitrary"),
    )(o_f, o_b, proj, proj, proj, proj, proj, proj, proj, proj, proj, x2d, tgt, gla_g, conv_w, conv_b, w_out, final_g)


def _dsilu(z, s):
    return s * (1.0 + z * (1.0 - s))


def _mix_bwd(dx2b, o_f, o_b, proj, conv, gla_g, w_out, tm):
    seq = dx2b.shape[0]

    def body(dx, of, ob, za, bg, zc, cv, gg, wo, dg_ref, do_ref, dconv_ref, dgg_ref, dcb_ref):
        @pl.when(pl.program_id(0) == 0)
        def _():
            dgg_ref[...] = jnp.zeros(dgg_ref.shape, F32)
            dcb_ref[...] = jnp.zeros(dcb_ref.shape, F32)

        dy = _dot_nt(dx[...], wo[...])
        dy_a, dy_c = dy[:, :V_W], dy[:, V_W:]
        zcv, bgv, convv = _f32(zc), _f32(bg), _f32(cv)
        sc = _sigmoid(zcv)
        szc = zcv * sc
        dg_ref[:, CONV_W:2 * CONV_W] = (dy_c * convv * szc).astype(BF16)
        dconv = dy_c * bgv * szc
        dconv_ref[...] = dconv.astype(BF16)
        dcb_ref[...] += jnp.sum(dconv, axis=0, keepdims=True)
        dg_ref[:, 2 * CONV_W:] = (dy_c * bgv * convv * _dsilu(zcv, sc)).astype(BF16)

        o = _f32(of) + _f32(ob)
        gain = gg[...]
        on, rinv = _head_norm(o, gain)
        zav = _f32(za)
        sa = _sigmoid(zav)
        dg_ref[:, :CONV_W] = (dy_a * on * _dsilu(zav, sa)).astype(BF16)
        don = dy_a * (zav * sa)
        dgg = jnp.zeros((1, DV), F32)
        dos = []
        for h in range(HEADS):
            sl = slice(h * DV, (h + 1) * DV)
            oh, r, dh = o[:, sl], rinv[h], don[:, sl]
            ohn = oh * r
            dgg = dgg + jnp.sum(dh * ohn, axis=0, keepdims=True)
            dn = dh * gain
            dos.append(r * dn - ohn * (r * jnp.mean(dn * ohn, axis=-1, keepdims=True)))
        dgg_ref[...] += dgg
        do_ref[...] = jnp.concatenate(dos, axis=1).astype(BF16)

    def col(off):
        return pl.BlockSpec((tm, CONV_W), lambda i: (i, off // CONV_W))

    rowt = pl.BlockSpec((tm, D_MODEL), lambda i: (i, 0))
    const = lambda shape: pl.BlockSpec(shape, lambda i: (0, 0))
    return pl.pallas_call(
        body, name="mix_bwd",
        out_shape=(jax.ShapeDtypeStruct((seq, GATES_W), BF16), jax.ShapeDtypeStruct((seq, V_W), BF16),
                   jax.ShapeDtypeStruct((seq, CONV_W), BF16),
                   jax.ShapeDtypeStruct((1, DV), F32), jax.ShapeDtypeStruct((1, CONV_W), F32)),
        grid=(seq // tm,),
        in_specs=[rowt, rowt, rowt, col(OFF_ZA), col(OFF_B), col(OFF_ZC), rowt, const((1, DV)),
                  const((MIX_W, D_MODEL))],
        out_specs=(pl.BlockSpec((tm, GATES_W), lambda i: (i, 0)), rowt, rowt, const((1, DV)), const((1, CONV_W))),
        compiler_params=_cparams("arbitrary"),
    )(dx2b, o_f, o_b, proj, proj, proj, conv, gla_g, w_out)


def _conv_bwd(dconv, proj, conv_w, tm):
    seq = dconv.shape[0]
    nt = seq // tm

    def body(dc_in, dprev, dnext, cg, hc, cprev, cnext, hprev, hnext, cw, dch_ref, dcw_ref):
        i = pl.program_id(0)

        @pl.when(i == 0)
        def _():
            dcw_ref[...] = jnp.zeros(dcw_ref.shape, F32)

        first, lastt = i > 0, i < nt - 1
        dcv = _f32(dc_in)
        d_up, d_un = _shift_rows(dcv, jnp.where(first, _last_row(dprev), 0.0), jnp.where(lastt, _first_row(dnext), 0.0))
        cgv, hcv = _f32(cg), _f32(hc)
        u = cgv * hcv
        u_up, u_un = _shift_rows(u, jnp.where(first, _last_row(cprev) * _last_row(hprev), 0.0),
                                 jnp.where(lastt, _first_row(cnext) * _first_row(hnext), 0.0))
        du = cw[0:1, :] * d_un + cw[1:2, :] * dcv + cw[2:3, :] * d_up
        dch_ref[:, :CONV_W] = (du * hcv).astype(BF16)
        dch_ref[:, CONV_W:] = (du * cgv).astype(BF16)
        dcw_ref[0:1, :] += jnp.sum(dcv * u_up, axis=0, keepdims=True)
        dcw_ref[1:2, :] += jnp.sum(dcv * u, axis=0, keepdims=True)
        dcw_ref[2:3, :] += jnp.sum(dcv * u_un, axis=0, keepdims=True)

    def col(off):
        return pl.BlockSpec((tm, CONV_W), lambda i: (i, off // CONV_W))

    rowt = pl.BlockSpec((tm, CONV_W), lambda i: (i, 0))
    const = lambda shape: pl.BlockSpec(shape, lambda i: (0, 0))
    return pl.pallas_call(
        body, name="conv_bwd",
        out_shape=(jax.ShapeDtypeStruct((seq, CH_W), BF16), jax.ShapeDtypeStruct((8, CONV_W), F32)),
        grid=(nt,),
        in_specs=[rowt] + _halo_specs(tm, seq, 0) + [col(OFF_C), col(OFF_H)]
        + _halo_specs(tm, seq, OFF_C // CONV_W) + _halo_specs(tm, seq, OFF_H // CONV_W) + [const((8, CONV_W))],
        out_specs=(pl.BlockSpec((tm, CH_W), lambda i: (i, 0)), const((8, CONV_W))),
        compiler_params=_cparams("arbitrary"),
    )(dconv, dconv, dconv, proj, proj, proj, proj, proj, proj, conv_w)


def _gla_bwd(proj, lr, do, st_f, st_b, wgk_f, wgk_b, bgk_f, bgk_b, tt):
    seq = proj.shape[0]
    nb, nc = seq // tt, tt // CHUNK

    def body(qf, kf, vf, lrf, dof, stf, qb, kb, vb, lrb, dob, stb, wf, wb, bf, bb,
             dqkv_f, dlr_f, dqkv_b, dlr_b, dwf, dwb, dbf, dbb,
             ds_scr, eq_s, ek_s, ein_s, eout_s, qs_s, ks_s, qin_s, kout_s, db_s, lg_s):
        @pl.when(pl.program_id(0) == 0)
        def _():
            ds_scr[...] = jnp.zeros(ds_scr.shape, F32)
            for r in (dwf, dwb, dbf, dbb):
                r[...] = jnp.zeros(r.shape, F32)

        low, upp, sup = _block_masks(tt)
        row = lax.broadcasted_iota(jnp.int32, (CHUNK, 1), 0)
        kmask = _chunk_column_mask(tt)
        dirs = ((qf, kf, vf, lrf, dof, stf, wf, bf, dqkv_f, dlr_f, dwf, dbf,
                 low, upp, low, REF_F, LAST_F, list(reversed(range(nc)))),
                (qb, kb, vb, lrb, dob, stb, wb, bb, dqkv_b, dlr_b, dwb, dbb,
                 upp, low, sup, REF_B, LAST_B, list(range(nc))))
        for d, (q_r, k_r, v_r, lr_r, do_r, st_r, w_r, b_r, dqkv_r, dlr_r, dw_r, db_r,
                cum, cum_t, mask, ref, last, order) in enumerate(dirs):
            lrv = lr_r[...].astype(BF16)
            wv = w_r[...]
            logits = _dot(lrv, wv) + b_r[...]
            lg_s[...] = logits
            b = _dot_split3(cum.astype(BF16), _log_gate(logits))
            decs = []
            for c in range(nc):
                rows = slice(c * CHUNK, (c + 1) * CHUNK)
                bc = b[rows]
                b_ref, b_last = bc[ref:ref + 1], bc[last:last + 1]
                qc = q_r[rows, :].astype(F32) * QSCALE
                kc = k_r[rows, :].astype(F32)
                e_q, e_k, e_in, e_out = jnp.exp(bc - b_ref), jnp.exp(b_ref - bc), jnp.exp(bc), jnp.exp(b_last - bc)
                eq_s[rows, :], ek_s[rows, :], ein_s[rows, :], eout_s[rows, :] = e_q, e_k, e_in, e_out
                qs_s[rows, :] = (qc * e_q).astype(BF16)
                ks_s[rows, :] = (kc * e_k).astype(BF16)
                qin_s[rows, :] = (qc * e_in).astype(BF16)
                kout_s[rows, :] = (kc * e_out).astype(BF16)
                decs.append(jnp.exp(b_last))
            for h in range(HEADS):
                ksl = slice(h * DK, (h + 1) * DK)
                vsl = slice(h * DV, (h + 1) * DV)
                v = v_r[:, vsl].astype(BF16)
                dov = do_r[:, vsl].astype(BF16)
                qsb, ksb = qs_s[:, ksl], ks_s[:, ksl]
                att = jnp.where(mask, _dot_nt(qsb, ksb), 0.0).astype(BF16)
                datt = jnp.where(mask, _dot_nt(dov, v), 0.0).astype(BF16)
                dqs = _dot(datt, ksb)
                dks = _dot_tn(datt, qsb)
                dv_intra = _dot_tn(att, dov)
                g_t = _dot_tn(dov, _chunked(kmask, qin_s[:, ksl], nc))
                ds = ds_scr[d * HEADS + h]
                for c in order:
                    rows = slice(c * CHUNK, (c + 1) * CHUNK)
                    dsb = ds.astype(BF16)
                    s_prev = st_r[c, h]
                    dk_out = _dot(v[rows], dsb)
                    dq_in = _dot(dov[rows], s_prev)
                    dv = dv_intra[rows] + _dot_nt(kout_s[rows, ksl], dsb)
                    dqkv_r[rows, OFF_V + h * DV:OFF_V + (h + 1) * DV] = dv.astype(BF16)
                    dec = decs[c][:, ksl]
                    ddec = jnp.sum(ds * s_prev.astype(F32), axis=0, keepdims=True)
                    e_out = eout_s[rows, ksl]
                    qc = q_r[rows, ksl].astype(F32) * QSCALE
                    kc = k_r[rows, ksl].astype(F32)
                    dq = dqs[rows] * eq_s[rows, ksl] + dq_in * ein_s[rows, ksl]
                    dk = dks[rows] * ek_s[rows, ksl] + dk_out * e_out
                    dqkv_r[rows, OFF_Q + h * DK:OFF_Q + (h + 1) * DK] = (dq * QSCALE).astype(BF16)
                    dqkv_r[rows, OFF_K + h * DK:OFF_K + (h + 1) * DK] = dk.astype(BF16)
                    tail = jnp.sum(dk_out * (kc * e_out), axis=0, keepdims=True) + ddec * dec
                    db_s[rows, ksl] = (qc * dq - kc * dk) + jnp.where(row == last, tail, 0.0)
                    ds = ds * dec + g_t[:, c * DK:(c + 1) * DK]
                ds_scr[d * HEADS + h] = ds
            dg = _dot_split3(cum_t.astype(BF16), db_s[...])
            dlogit = (dg * GATE_SCALE) * _sigmoid(-lg_s[...])
            dlb = dlogit.astype(BF16)
            dlr_r[...] = _dot_nt(dlb, wv)
            dw_r[...] += _dot_tn(lrv, dlb)
            db_r[...] += jnp.sum(dlogit, axis=0, keepdims=True)

    fw = lambda i: (nb - 1 - i, 0)
    bw = lambda i: (i, 0)
    const = lambda i: (0, 0)

    def tok_specs(m):
        return [pl.BlockSpec((tt, QK_W), lambda i: (m(i)[0], OFF_Q // QK_W)),
                pl.BlockSpec((tt, QK_W), lambda i: (m(i)[0], OFF_K // QK_W)),
                pl.BlockSpec((tt, V_W), lambda i: (m(i)[0], OFF_V // V_W)),
                pl.BlockSpec((tt, LR_W), m),
                pl.BlockSpec((tt, V_W), m),
                pl.BlockSpec((nc, HEADS, DV, DK), lambda i: (m(i)[0], 0, 0, 0))]

    dqkv = jax.ShapeDtypeStruct((seq, QK_W + QK_W + V_W), BF16)
    dlr = jax.ShapeDtypeStruct((seq, LR_W), F32)
    dw = jax.ShapeDtypeStruct((LR_W, QK_W), F32)
    dbias = jax.ShapeDtypeStruct((1, QK_W), F32)
    return pl.pallas_call(
        body, name="gla_bwd",
        out_shape=(dqkv, dlr, dqkv, dlr, dw, dw, dbias, dbias),
        grid=(nb,),
        in_specs=tok_specs(fw) + tok_specs(bw) + [
            pl.BlockSpec((LR_W, QK_W), const), pl.BlockSpec((LR_W, QK_W), const),
            pl.BlockSpec((1, QK_W), const), pl.BlockSpec((1, QK_W), const)],
        out_specs=(pl.BlockSpec((tt, QK_W + QK_W + V_W), fw), pl.BlockSpec((tt, LR_W), fw),
                   pl.BlockSpec((tt, QK_W + QK_W + V_W), bw), pl.BlockSpec((tt, LR_W), bw),
                   pl.BlockSpec((LR_W, QK_W), const), pl.BlockSpec((LR_W, QK_W), const),
                   pl.BlockSpec((1, QK_W), const), pl.BlockSpec((1, QK_W), const)),
        scratch_shapes=[pltpu.VMEM((2 * HEADS, DV, DK), F32)] + [pltpu.VMEM((tt, QK_W), F32)] * 4
        + [pltpu.VMEM((tt, QK_W), BF16)] * 4 + [pltpu.VMEM((tt, QK_W), F32)] * 2,
        compiler_params=_cparams("arbitrary"),
    )(proj, proj, proj, lr, do, st_f, proj, proj, proj, lr, do, st_b, wgk_f, wgk_b, bgk_f, bgk_b)


def _sum_directions(dqkv_f, dqkv_b, dlr_f, dlr_b, tm):
    seq = dqkv_f.shape[0]

    def body(a, b, la, lb, dp_out, dlr_out):
        dp_out[...] = (_f32(a) + _f32(b)).astype(BF16)
        dlr_out[...] = (la[...] + lb[...]).astype(BF16)

    rowt = pl.BlockSpec((tm, QKV_W), lambda i: (i, 0))
    lrt = pl.BlockSpec((tm, LR_W), lambda i: (i, 0))
    return pl.pallas_call(
        body, name="sum_directions",
        out_shape=(jax.ShapeDtypeStruct((seq, QKV_W), BF16), jax.ShapeDtypeStruct((seq, LR_W), BF16)),
        grid=(seq // tm,),
        in_specs=[rowt, rowt, lrt, lrt],
        out_specs=(rowt, lrt),
        compiler_params=_cparams("arbitrary"),
    )(dqkv_f, dqkv_b, dlr_f, dlr_b)


def _input_grad(dp_qkv, dp_gates, dp_ch, dlr, w_nat, x2d, norm_g, dx2, sums, tm):
    seq = x2d.shape[0]
    nt, n = seq // tm, len(sums)

    def body(dq, dg, dc, dl, w, x_ref, g_ref, dx2_ref, *rest):
        ins, (gx_ref, dng_ref), outs = rest[:n], rest[n:n + 2], rest[n + 2:2 * n + 2]
        send_sems, recv_sems = rest[2 * n + 2:]
        i = pl.program_id(0)

        @pl.when(i == 0)
        def _():
            for cp in _chip_copies(ins, outs, send_sems, recv_sems):
                cp.start()
            dng_ref[...] = jnp.zeros(dng_ref.shape, F32)

        dh = (_dot(dl[...], w[NAT_LR:NAT_LR + LR_W, :]) + _dot(dq[...], w[0:NAT_ZA, :])
              + _dot(dg[:, 0:CONV_W], w[NAT_ZA:NAT_LR, :]) + _dot(dg[:, CONV_W:2 * CONV_W], w[NAT_B:NAT_C, :])
              + _dot(dg[:, 2 * CONV_W:], w[NAT_ZC:IN_W, :]) + _dot(dc[...], w[NAT_C:NAT_ZC, :]))
        xv = x_ref[...]
        r = lax.rsqrt(jnp.mean(xv * xv, axis=-1, keepdims=True) + EPS)
        xn = xv * r
        dng_ref[...] += jnp.sum(dh * xn, axis=0, keepdims=True)
        dn = dh * g_ref[...]
        gx_ref[...] = (r * dn - xn * (r * jnp.mean(dn * xn, axis=-1, keepdims=True))) + dx2_ref[...]

        @pl.when(i == nt - 1)
        def _():
            copies = _chip_copies(ins, outs, send_sems, recv_sems)
            for cp in copies:
                cp.wait_recv()
            for cp in copies:
                cp.wait_send()

    rowt = pl.BlockSpec((tm, D_MODEL), lambda i: (i, 0))
    seg = lambda width: pl.BlockSpec((tm, width), lambda i: (i, 0))
    resident = lambda rows: pl.BlockSpec((rows, D_MODEL), lambda i: (0, 0), pipeline_mode=pl.Buffered(1))
    hbm = pl.BlockSpec(memory_space=pl.ANY)
    return pl.pallas_call(
        body, name="input_grad",
        out_shape=(jax.ShapeDtypeStruct((seq, D_MODEL), F32), jax.ShapeDtypeStruct((1, D_MODEL), F32))
        + tuple(jax.ShapeDtypeStruct((3,) + s.shape[1:], s.dtype) for s in sums),
        grid=(nt,),
        in_specs=[seg(QKV_W), seg(GATES_W), seg(CH_W), seg(LR_W), resident(IN_W),
                  rowt, pl.BlockSpec((1, D_MODEL), lambda i: (0, 0)), rowt] + [hbm] * n,
        out_specs=(rowt, pl.BlockSpec((1, D_MODEL), lambda i: (0, 0))) + (hbm,) * n,
        scratch_shapes=[pltpu.SemaphoreType.DMA((3 * n,)), pltpu.SemaphoreType.DMA((3 * n,))],
        compiler_params=_cparams("arbitrary"),
    )(dp_qkv, dp_gates, dp_ch, dlr, w_nat, x2d, norm_g, dx2, *sums)


def _weight_grad(at, b, tn, tk, name):
    m, seq = at.shape
    n = b.shape[1]

    def body(a_ref, b_ref, o_ref):
        @pl.when(pl.program_id(1) == 0)
        def _():
            o_ref[...] = jnp.zeros(o_ref.shape, F32)

        o_ref[...] += _dot(a_ref[...], b_ref[...])

    return pl.pallas_call(
        body, name=name,
        out_shape=jax.ShapeDtypeStruct((m, n), F32),
        grid=(n // tn, seq // tk),
        in_specs=[pl.BlockSpec((m, tk), lambda j, k: (0, k)), pl.BlockSpec((tk, tn), lambda j, k: (k, j))],
        out_specs=pl.BlockSpec((m, tn), lambda j, k: (0, j)),
        compiler_params=_cparams("arbitrary", "arbitrary"),
    )(at, b)


def _weight_grad_in(h_t, dp_qkv, dp_gates, dp_ch, dlr):
    m, seq = h_t.shape
    tn = 512
    n_qkv, n_gates, n_ch = QKV_W // tn, GATES_W // tn, CH_W // tn
    starts = ([k * tn for k in range(n_qkv)] + [NAT_ZA, NAT_ZA + tn, NAT_B, NAT_B + tn, NAT_ZC, NAT_ZC + tn]
              + [NAT_C + k * tn for k in range(n_ch)])

    def out_row(j):
        row = 0
        for k, start in enumerate(starts):
            row = row + jnp.where(j == k, start // 32, 0)
        return pl.multiple_of(row * 32, 32), 0

    def body(a_ref, bq, bg, bc, o_ref, acc):
        j = pl.program_id(0)

        @pl.when(j < n_qkv)
        def _():
            acc[...] = _dot(a_ref[...], bq[...])

        @pl.when(jnp.logical_and(j >= n_qkv, j < n_qkv + n_gates))
        def _():
            acc[...] = _dot(a_ref[...], bg[...])

        @pl.when(j >= n_qkv + n_gates)
        def _():
            acc[...] = _dot(a_ref[...], bc[...])

        o_ref[...] = acc[...].T

    resident = pl.BlockSpec((m, seq), lambda j: (0, 0), pipeline_mode=pl.Buffered(1))
    seg = lambda first, count: pl.BlockSpec((seq, tn), lambda j: (0, jnp.clip(j - first, 0, count - 1)))
    main = pl.pallas_call(
        body, name="wgrad_in",
        out_shape=jax.ShapeDtypeStruct((IN_W, m), F32),
        grid=(n_qkv + n_gates + n_ch,),
        in_specs=[resident, seg(0, n_qkv), seg(n_qkv, n_gates), seg(n_qkv + n_gates, n_ch)],
        out_specs=pl.BlockSpec((pl.Element(tn), pl.Element(m)), out_row),
        scratch_shapes=[pltpu.VMEM((m, tn), F32)],
        compiler_params=_cparams("arbitrary"),
    )(h_t, dp_qkv, dp_gates, dp_ch)

    def lr_body(a_ref, b_ref, full_ref, o_ref, acc):
        acc[...] = _dot(a_ref[...], b_ref[...])
        o_ref[...] = acc[...].T[0:2 * RANK, :]

    whole = lambda shape: pl.BlockSpec(shape, lambda j: (0, 0))
    return pl.pallas_call(
        lr_body, name="wgrad_lr",
        out_shape=jax.ShapeDtypeStruct((IN_W, m), F32),
        grid=(1,),
        in_specs=[whole((m, seq)), whole((seq, LR_W)), pl.BlockSpec(memory_space=pl.ANY)],
        out_specs=pl.BlockSpec((pl.Element(2 * RANK), pl.Element(m)), lambda j: (NAT_LR, 0)),
        scratch_shapes=[pltpu.VMEM((m, LR_W), F32)],
        input_output_aliases={2: 0},
        compiler_params=_cparams("arbitrary"),
    )(h_t, dlr, main)


def _pad_rows(a, rows):
    return jnp.pad(a, ((0, rows - a.shape[0]), (0, 0)))


def _rows128(a):
    a = a.reshape(-1, 128)
    return _pad_rows(a, -(-a.shape[0] // 8) * 8)


def _pack(arrs):
    return jnp.concatenate([_rows128(a) for a in arrs], axis=0)


def _unpack(buf, like):
    out, start = [], 0
    for a in like:
        rows = a.size // 128
        out.append(buf[start:start + rows].reshape(a.shape))
        start += -(-rows // 8) * 8
    return out


def kernel(x, norm_g, w_in, w_gk_f, b_gk_f, w_gk_b, b_gk_b, gla_norm_g, conv_w, conv_b, w_out, final_g, loss_target, m_norm_g, m_w_in, m_w_gk_f, m_b_gk_f, m_w_gk_b, m_b_gk_b, m_gla_norm_g, m_conv_w, m_conv_b, m_w_out, m_final_g, v_norm_g, v_w_in, v_w_gk_f, v_b_gk_f, v_w_gk_b, v_b_gk_b, v_gla_norm_g, v_conv_w, v_conv_b, v_w_out, v_final_g):
    px, py, pc = _position()
    me = _blk(px, py, pc)
    seq = x.shape[1]
    x2d, tgt = x[0], loss_target[0]
    tm = min(512, seq)
    tt = min(256, seq)

    small_s = jnp.concatenate([jnp.concatenate([w_gk_f[0], w_gk_b[0]], axis=1), _pad_rows(conv_w[0], 8)], axis=0)
    w_nat = _allgather_w_in(w_in[0].T).reshape(IN_W, D_MODEL)

    proj, lr, h_t, wout_all, small_all = _inproj(x2d, norm_g, w_nat, w_out[0], small_s, min(1024, seq))
    w_out_full = wout_all.reshape(MIX_W, D_MODEL)
    wgk_cols = 512 // N_DEV
    wgk_f_full = small_all[:, 0:RANK, 0:wgk_cols].transpose(1, 0, 2).reshape(RANK, QK_W)
    wgk_b_full = small_all[:, 0:RANK, wgk_cols:2 * wgk_cols].transpose(1, 0, 2).reshape(RANK, QK_W)
    conv_w_full = _pad_rows(small_all[:, RANK:RANK + 3, :].transpose(1, 0, 2).reshape(3, CONV_W), 8)
    zr = lambda n: jnp.zeros((n, QK_W), F32)
    wgk_f_pad = jnp.concatenate([wgk_f_full, zr(LR_W - RANK)], axis=0).astype(BF16)
    wgk_b_pad = jnp.concatenate([zr(RANK), wgk_b_full, zr(LR_W - 2 * RANK)], axis=0).astype(BF16)

    o_f, o_b, st_f, st_b = _gla_fwd(proj, lr, wgk_f_pad, wgk_b_pad, b_gk_f, b_gk_b, tt)
    tmix = min(256, seq)
    y_t, conv, dx2, dx2b, loss_p, dfg_p = _mix_out_loss(o_f, o_b, proj, x2d, tgt, gla_norm_g, conv_w_full, conv_b,
                                                        w_out_full, final_g.reshape(1, D_MODEL), tmix)

    dp_gates, do, dconv, dgg_p, dcb_p = _mix_bwd(dx2b, o_f, o_b, proj, conv, gla_norm_g, w_out_full, tmix)
    dp_ch, dcw_p = _conv_bwd(dconv, proj, conv_w_full, tmix)
    dqkv_f, dlr_f, dqkv_b, dlr_b, dwf_p, dwb_p, dbf_p, dbb_p = _gla_bwd(
        proj, lr, do, st_f, st_b, wgk_f_pad, wgk_b_pad, b_gk_f, b_gk_b, tt)
    dp_qkv, dlr = _sum_directions(dqkv_f, dqkv_b, dlr_f, dlr_b, tm)
    dw_nat = _weight_grad_in(h_t, dp_qkv, dp_gates, dp_ch, dlr)
    dw_out = _weight_grad(y_t, dx2b, D_MODEL, tm, "wgrad_out")

    part_in = dw_nat.reshape(N_DEV, SHARD_W, D_MODEL)
    part_out = dw_out.reshape(N_DEV, MIX_W // N_DEV, D_MODEL)
    sib_in, sib_out = _exchange_sibling([part_in, part_out])
    core = jnp.reshape(pc, (1,)).astype(jnp.int32)
    chip = jnp.reshape(2 * px + py, (1,)).astype(jnp.int32)
    sums_in = _chip_sums(part_in, sib_in, core, 256, "chip_sums_in")
    sums_out = _chip_sums(part_out, sib_out, core, 256, "chip_sums_out")
    grad_x2d, dng_p, far_in, far_out = _input_grad(dp_qkv, dp_gates, dp_ch, dlr, w_nat, x2d, norm_g, dx2,
                                                   [sums_in, sums_out], tmix)
    g_in_t = _final_sum(sums_in, far_in, chip, 256, "final_sum_in")
    g_w_out, d_w_out, nm_w_out, nv_w_out = _final_sum_adamw(sums_out, far_out, chip, w_out[0], m_w_out[0], v_w_out[0],
                                                            256, "adamw_out")
    flat = lambda a: a[0].T.reshape(SHARD_W, D_MODEL // 128, 128)
    unflat = lambda a: a.reshape(SHARD_W, D_MODEL).T
    d_flat, m_flat, v_flat = _adamw_rows(g_in_t.reshape(SHARD_W, D_MODEL // 128, 128), flat(w_in), flat(m_w_in),
                                         flat(v_w_in), 90, "adamw_in")
    g_w_in, d_w_in, nm_w_in, nv_w_in = g_in_t.T, unflat(d_flat), unflat(m_flat), unflat(v_flat)

    pieces = [dng_p, dbf_p, dbb_p, dgg_p, dcb_p, dfg_p[0], dwf_p[0:RANK], dwb_p[RANK:2 * RANK], dcw_p[0:3], loss_p[0]]
    tot = _unpack(_allreduce_small(_pack(pieces)), pieces)
    g_norm_g, g_b_gk_f, g_b_gk_b, g_gla, g_conv_b, g_final = tot[:6]
    g_wgk_f = lax.dynamic_slice_in_dim(tot[6], me * wgk_cols, wgk_cols, axis=1)[None]
    g_wgk_b = lax.dynamic_slice_in_dim(tot[7], me * wgk_cols, wgk_cols, axis=1)[None]
    g_conv_w = lax.dynamic_slice_in_dim(tot[8], me * 128, 128, axis=1)[None]
    loss = tot[9][0]

    small_g = [g_norm_g, g_b_gk_f, g_b_gk_b, g_gla, g_conv_b, g_final, g_wgk_f, g_wgk_b, g_conv_w]
    small_w = [norm_g, b_gk_f, b_gk_b, gla_norm_g, conv_b, final_g, w_gk_f, w_gk_b, conv_w]
    small_m = [m_norm_g, m_b_gk_f, m_b_gk_b, m_gla_norm_g, m_conv_b, m_final_g, m_w_gk_f, m_w_gk_b, m_conv_w]
    small_v = [v_norm_g, v_b_gk_f, v_b_gk_b, v_gla_norm_g, v_conv_b, v_final_g, v_w_gk_f, v_w_gk_b, v_conv_w]
    d_s, m_s, v_s = _adamw_small(_pack(small_g), _pack(small_w), _pack(small_m), _pack(small_v))
    d_l, m_l, v_l = _unpack(d_s, small_w), _unpack(m_s, small_w), _unpack(v_s, small_w)

    def ordered(sm, big_in, big_out):
        return [sm[0], big_in[None], sm[6], sm[1], sm[7], sm[2], sm[3], sm[8], sm[4], big_out[None], sm[5]]

    grads = ordered(small_g, g_w_in, g_w_out)
    deltas = ordered(d_l, d_w_in, d_w_out)
    new_m = ordered(m_l, nm_w_in, nm_w_out)
    new_v = ordered(v_l, nv_w_in, nv_w_out)
    return (loss, grad_x2d[None], *grads, *deltas, *new_m, *new_v)
```

```python
import jax
import jax.numpy as jnp
from jax import lax
from jax.experimental import pallas as pl
from jax.experimental.pallas import tpu as pltpu

F32 = jnp.float32
BF16 = jnp.bfloat16
MESH = pl.DeviceIdType.MESH

N_DEV = 8
D_MODEL = 1024
HEADS = 4
DK = 128
DV = 256
QK_W = HEADS * DK
V_W = HEADS * DV
CONV_W = 1024
MIX_W = V_W + CONV_W
CHUNK = 64
RANK = 16
IN_W = 7200
SHARD_W = IN_W // N_DEV
MAIN_W = 7168
LR_W = 128
OFF_Q, OFF_K, OFF_V, OFF_ZA, OFF_B, OFF_ZC, OFF_C, OFF_H = 0, 512, 1024, 2048, 3072, 4096, 5120, 6144
QKV_W, GATES_W, CH_W = 2048, 3072, 2048
NAT_ZA, NAT_LR, NAT_B, NAT_C, NAT_ZC = 2048, 3072, 3104, 4128, 6176
EPS = 1e-6
GATE_SCALE = 1.0 / 16.0
QSCALE = DK ** -0.5
REF_F, LAST_F = CHUNK // 2, CHUNK - 1
REF_B, LAST_B = CHUNK - 1 - CHUNK // 2, 0

ADAM_LR = 0.001
ADAM_B1 = 0.9
ADAM_B2 = 0.999
ADAM_EPS = 1e-08
ADAM_WD = 0.01
ADAM_STEP = 10

VMEM_LIMIT = 56 * 1024 * 1024


def _cparams(*sem):
    return pltpu.CompilerParams(dimension_semantics=sem, vmem_limit_bytes=VMEM_LIMIT)


def _dot(a, b):
    return jnp.dot(a, b, preferred_element_type=F32)


def _dot_nt(a, b):
    return lax.dot_general(a, b, (((1,), (1,)), ((), ())), preferred_element_type=F32)


def _dot_tn(a, b):
    return lax.dot_general(a, b, (((0,), (0,)), ((), ())), preferred_element_type=F32)


def _sigmoid(z):
    return jax.nn.sigmoid(z)


def _position():
    return lax.axis_index("x"), lax.axis_index("y"), lax.axis_index("c")


def _blk(px, py, pc):
    return 4 * px + 2 * py + pc


def _two_level_gather(outs, send_sems, recv_sems):
    x, y, c = _position()
    me, sibling = (x, y, c), (x, y, 1 - c)
    chips = [(1 - x, y), (x, 1 - y), (1 - x, 1 - y)]
    n = len(outs)

    def copy(a, k, block, to):
        ref = outs[a].at[_blk(*block)]
        return pltpu.make_async_remote_copy(src_ref=ref, dst_ref=ref, send_sem=send_sems.at[a * 7 + k],
                                            recv_sem=recv_sems.at[a * 7 + k], device_id=to, device_id_type=MESH)

    first = []
    for a in range(n):
        first.append(copy(a, 0, me, sibling))
        first += [copy(a, 1 + j, me, (*chip, c)) for j, chip in enumerate(chips)]
    for cp in first:
        cp.start()
    passed = []
    for j, chip in enumerate(chips):
        for a in range(n):
            copy(a, 1 + j, (*chip, c), me).wait_recv()
            fwd = copy(a, 4 + j, (*chip, c), sibling)
            fwd.start()
            passed.append(fwd)
    for a in range(n):
        copy(a, 0, sibling, me).wait_recv()
    for j, chip in enumerate(chips):
        for a in range(n):
            copy(a, 4 + j, (*chip, 1 - c), me).wait_recv()
    for cp in first + passed:
        cp.wait_send()


def _allgather_w_in(w_in_t):
    def body(win_ref, win_all, send_sems, recv_sems):
        win_all[_blk(*_position())] = win_ref[...].astype(BF16)
        _two_level_gather((win_all,), send_sems, recv_sems)

    vmem = pl.BlockSpec(memory_space=pltpu.VMEM)
    return pl.pallas_call(
        body, name="allgather_w_in",
        out_shape=jax.ShapeDtypeStruct((N_DEV,) + w_in_t.shape, BF16),
        in_specs=[vmem], out_specs=vmem,
        scratch_shapes=[pltpu.SemaphoreType.DMA((7,)), pltpu.SemaphoreType.DMA((7,))],
        compiler_params=pltpu.CompilerParams(vmem_limit_bytes=VMEM_LIMIT),
    )(w_in_t)


def _peer_copies(srcs, outs, send_sems, recv_sems):
    x, y, c = _position()
    me = _blk(x, y, c)
    copies = []
    for a, (src, out) in enumerate(zip(srcs, outs)):
        k = 0
        for dx in (0, 1):
            for dy in (0, 1):
                for dc in (0, 1):
                    if dx + dy + dc == 0:
                        continue
                    peer = (1 - x if dx else x, 1 - y if dy else y, 1 - c if dc else c)
                    copies.append(pltpu.make_async_remote_copy(
                        src_ref=src, dst_ref=out.at[me], send_sem=send_sems.at[a * 7 + k],
                        recv_sem=recv_sems.at[a * 7 + k], device_id=peer, device_id_type=MESH))
                    k += 1
    return copies


def _chip_copies(ins, outs, send_sems, recv_sems):
    x, y, c = _position()
    chips = [(1 - x, y), (x, 1 - y), (1 - x, 1 - y)]
    copies = []
    for a in range(len(ins)):
        for j, (px, py) in enumerate(chips):
            copies.append(pltpu.make_async_remote_copy(
                src_ref=ins[a].at[2 * px + py], dst_ref=outs[a].at[j],
                send_sem=send_sems.at[a * 3 + j], recv_sem=recv_sems.at[a * 3 + j],
                device_id=(px, py, c), device_id_type=MESH))
    return copies


def _sibling_copies(part, out, send_sems, recv_sems):
    x, y, c = _position()
    return [pltpu.make_async_remote_copy(src_ref=part.at[2 * k + (1 - c)], dst_ref=out.at[k], send_sem=send_sems.at[k],
                                         recv_sem=recv_sems.at[k], device_id=(x, y, 1 - c), device_id_type=MESH)
            for k in range(4)]


def _start_all(copies):
    for cp in copies:
        cp.start()


def _wait_all(copies):
    for cp in copies:
        cp.wait_recv()
    for cp in copies:
        cp.wait_send()


def _chip_sums(part, from_sibling, core, tc, name, riding=None):
    _, rows, cols = part.shape
    nj = cols // tc

    def body(core_ref, p_ref, s_ref, *rest):
        if riding is None:
            (o_ref,) = rest
        else:
            ride_in, o_ref, ride_out, send_sems, recv_sems = rest
            k, j = pl.program_id(0), pl.program_id(1)

            @pl.when(jnp.logical_and(k == 0, j == 0))
            def _():
                _start_all(_sibling_copies(ride_in, ride_out, send_sems, recv_sems))

        o_ref[...] = (p_ref[...] + s_ref[...]).astype(BF16)

        if riding is not None:
            @pl.when(jnp.logical_and(k == 3, j == nj - 1))
            def _():
                _wait_all(_sibling_copies(ride_in, ride_out, send_sems, recv_sems))

    hbm = pl.BlockSpec(memory_space=pl.ANY)
    sums = jax.ShapeDtypeStruct((4, rows, cols), BF16)
    tile_out = pl.BlockSpec((1, rows, tc), lambda k, j, core_ref: (k, 0, j))
    in_specs = [pl.BlockSpec((1, rows, tc), lambda k, j, core_ref: (2 * k + core_ref[0], 0, j)),
                pl.BlockSpec((1, rows, tc), lambda k, j, core_ref: (k, 0, j))]
    if riding is None:
        out_shape, out_specs, scratch, args = sums, tile_out, [], (core, part, from_sibling)
    else:
        out_shape = (sums, jax.ShapeDtypeStruct((4,) + riding.shape[1:], F32))
        out_specs, in_specs = (tile_out, hbm), in_specs + [hbm]
        scratch = [pltpu.SemaphoreType.DMA((4,)), pltpu.SemaphoreType.DMA((4,))]
        args = (core, part, from_sibling, riding)
    return pl.pallas_call(
        body, name=name, out_shape=out_shape,
        grid_spec=pltpu.PrefetchScalarGridSpec(num_scalar_prefetch=1, grid=(4, nj), in_specs=in_specs,
                                               out_specs=out_specs, scratch_shapes=scratch),
        compiler_params=_cparams("arbitrary", "arbitrary"),
    )(*args)


def _sum_chips(s_ref, r_ref):
    f = lambda a: a.astype(F32)
    return ((f(s_ref[0]) + f(r_ref[0])) + f(r_ref[1])) + f(r_ref[2])


def _final_sum(sums, from_chips, chip, small, tc, name):
    _, rows, cols = sums.shape
    nj = cols // tc

    def body(chip_ref, s_ref, r_ref, sm_ref, g_out, tot_ref, all_ref, send_sems, recv_sems):
        j = pl.program_id(0)
        me = _blk(*_position())

        @pl.when(j == 0)
        def _():
            all_ref[me] = sm_ref[...]
            _start_all(_peer_copies((all_ref.at[me],), (all_ref,), send_sems, recv_sems))

        g_out[...] = _sum_chips(s_ref, r_ref)

        @pl.when(j == nj - 1)
        def _():
            _wait_all(_peer_copies((all_ref.at[me],), (all_ref,), send_sems, recv_sems))
            acc = all_ref[0]
            for d in range(1, N_DEV):
                acc = acc + all_ref[d]
            tot_ref[...] = acc

    whole = pl.BlockSpec(small.shape, lambda j, chip_ref: (0, 0))
    return pl.pallas_call(
        body, name=name,
        out_shape=(jax.ShapeDtypeStruct((rows, cols), F32), jax.ShapeDtypeStruct(small.shape, F32)),
        grid_spec=pltpu.PrefetchScalarGridSpec(
            num_scalar_prefetch=1, grid=(nj,),
            in_specs=[pl.BlockSpec((1, rows, tc), lambda j, chip_ref: (chip_ref[0], 0, j)),
                      pl.BlockSpec((3, rows, tc), lambda j, chip_ref: (0, 0, j)), whole],
            out_specs=(pl.BlockSpec((rows, tc), lambda j, chip_ref: (0, j)), whole),
            scratch_shapes=[pltpu.VMEM((N_DEV,) + small.shape, F32), pltpu.SemaphoreType.DMA((7,)),
                            pltpu.SemaphoreType.DMA((7,))]),
        compiler_params=_cparams("arbitrary"),
    )(chip, sums, from_chips, small)


def _adamw_rows(g, w, m, v, tr, name):
    rows = g.shape[0]

    def body(g_ref, w_ref, m_ref, v_ref, d_out, m_out, v_out):
        delta, m_new, v_new = _adamw(w_ref[...], g_ref[...], m_ref[...], v_ref[...])
        d_out[...] = delta
        m_out[...] = m_new
        v_out[...] = v_new

    tile = pl.BlockSpec((tr,) + g.shape[1:], lambda r: (r, 0, 0))
    shp = jax.ShapeDtypeStruct(g.shape, F32)
    return pl.pallas_call(
        body, name=name, out_shape=(shp, shp, shp), grid=(rows // tr,),
        in_specs=[tile] * 4, out_specs=(tile, tile, tile),
        compiler_params=_cparams("arbitrary"),
    )(g, w, m, v)


def _adamw(w, g, m, v):
    m = ADAM_B1 * m + (1.0 - ADAM_B1) * g
    v = ADAM_B2 * v + (1.0 - ADAM_B2) * (g * g)
    m_hat = m / (1.0 - ADAM_B1 ** ADAM_STEP)
    v_hat = v / (1.0 - ADAM_B2 ** ADAM_STEP)
    delta = -ADAM_LR * (m_hat / (jnp.sqrt(v_hat) + ADAM_EPS) + ADAM_WD * w)
    return delta, m, v


def _final_sum_adamw(sums, from_chips, chip, w, m, v, tr, name):
    rows, cols = w.shape

    def body(chip_ref, s_ref, r_ref, w_ref, m_ref, v_ref, g_out, d_out, m_out, v_out):
        g = _sum_chips(s_ref, r_ref)
        delta, m_new, v_new = _adamw(w_ref[...], g, m_ref[...], v_ref[...])
        g_out[...] = g
        d_out[...] = delta
        m_out[...] = m_new
        v_out[...] = v_new

    tile = pl.BlockSpec((tr, cols), lambda r, chip_ref: (r, 0))
    shp = jax.ShapeDtypeStruct((rows, cols), F32)
    return pl.pallas_call(
        body, name=name,
        out_shape=(shp, shp, shp, shp),
        grid_spec=pltpu.PrefetchScalarGridSpec(
            num_scalar_prefetch=1, grid=(rows // tr,),
            in_specs=[pl.BlockSpec((1, tr, cols), lambda r, chip_ref: (chip_ref[0], r, 0)),
                      pl.BlockSpec((3, tr, cols), lambda r, chip_ref: (0, r, 0)),
                      tile, tile, tile],
            out_specs=(tile, tile, tile, tile)),
        compiler_params=_cparams("arbitrary"),
    )(chip, sums, from_chips, w, m, v)


def _adamw_small(g, w, m, v):
    def body(g_ref, w_ref, m_ref, v_ref, d_out, m_out, v_out):
        delta, m_new, v_new = _adamw(w_ref[...], g_ref[...], m_ref[...], v_ref[...])
        d_out[...] = delta
        m_out[...] = m_new
        v_out[...] = v_new

    vmem = pl.BlockSpec(memory_space=pltpu.VMEM)
    shp = jax.ShapeDtypeStruct(g.shape, F32)
    return pl.pallas_call(body, name="adamw_small", out_shape=(shp, shp, shp),
                          in_specs=[vmem] * 4, out_specs=(vmem, vmem, vmem))(g, w, m, v)


TILE_ROWS = (0, 1024, NAT_ZA, NAT_B, NAT_ZC, NAT_C, NAT_C + CONV_W)


def _inproj(x2d, norm_g, w_nat, w_out_s, small_s, tm):
    seq = x2d.shape[0]
    tn = CONV_W
    ni, nj = seq // tm, MAIN_W // tn
    first_sweep = lambda j, i: jnp.where(j == 0, i, ni - 1)

    def tile_row(j, i):
        row = 0
        for k, start in enumerate(TILE_ROWS):
            row = row + jnp.where(j == k, start // 32, 0)
        return pl.multiple_of(row * 32, 32), 0

    def body(x_ref, g_ref, w_ref, wlr_ref, wout_ref, sm_ref, proj_ref, lr_ref, ht_ref, wout_all, sm_all,
             h_all, wout_b, sm_b, send_sems, recv_sems, local_sems):
        j, i = pl.program_id(0), pl.program_id(1)
        rows = pl.ds(pl.multiple_of(i * tm, tm), tm)
        me = _blk(*_position())

        def gather_copies():
            mine = [pltpu.make_async_copy(wout_b, wout_all.at[me], local_sems.at[0]),
                    pltpu.make_async_copy(sm_b, sm_all.at[me], local_sems.at[1])]
            return mine, _peer_copies((wout_b, sm_b), (wout_all, sm_all), send_sems, recv_sems)

        @pl.when(jnp.logical_and(j == 0, i == 0))
        def _():
            wout_b[...] = wout_ref[...].astype(BF16)
            sm_b[...] = sm_ref[...]
            mine, remote = gather_copies()
            for cp in mine + remote:
                cp.start()

        @pl.when(j == 0)
        def _():
            xv = x_ref[...]
            r = lax.rsqrt(jnp.mean(xv * xv, axis=-1, keepdims=True) + EPS)
            h = (xv * r) * g_ref[...]
            hb = h.astype(BF16)
            h_all[rows, :] = hb
            ht_ref[...] = h.T.astype(BF16)
            lr_ref[...] = _dot_nt(hb, wlr_ref[...])

        proj_ref[...] = _dot_nt(h_all[rows, :], w_ref[...]).astype(BF16)

        @pl.when(jnp.logical_and(j == nj - 1, i == ni - 1))
        def _():
            mine, remote = gather_copies()
            for cp in remote:
                cp.wait_recv()
            for cp in remote:
                cp.wait_send()
            for cp in mine:
                cp.wait()

    const = lambda shape: pl.BlockSpec(shape, lambda j, i: (0,) * len(shape))
    hbm = pl.BlockSpec(memory_space=pl.ANY)
    return pl.pallas_call(
        body, name="inproj",
        out_shape=(jax.ShapeDtypeStruct((seq, MAIN_W), BF16), jax.ShapeDtypeStruct((seq, LR_W), F32),
                   jax.ShapeDtypeStruct((D_MODEL, seq), BF16),
                   jax.ShapeDtypeStruct((N_DEV,) + w_out_s.shape, BF16),
                   jax.ShapeDtypeStruct((N_DEV,) + small_s.shape, F32)),
        grid=(nj, ni),
        in_specs=[pl.BlockSpec((tm, D_MODEL), lambda j, i: (first_sweep(j, i), 0)),
                  const((1, D_MODEL)),
                  pl.BlockSpec((pl.Element(tn), pl.Element(D_MODEL)), tile_row),
                  pl.BlockSpec((pl.Element(LR_W), pl.Element(D_MODEL)), lambda j, i: (NAT_LR, 0)),
                  const(w_out_s.shape), const(small_s.shape)],
        out_specs=(pl.BlockSpec((tm, tn), lambda j, i: (i, j)),
                   pl.BlockSpec((tm, LR_W), lambda j, i: (first_sweep(j, i), 0)),
                   pl.BlockSpec((D_MODEL, tm), lambda j, i: (0, first_sweep(j, i))), hbm, hbm),
        scratch_shapes=[pltpu.VMEM((seq, D_MODEL), BF16), pltpu.VMEM(w_out_s.shape, BF16),
                        pltpu.VMEM(small_s.shape, F32), pltpu.SemaphoreType.DMA((14,)),
                        pltpu.SemaphoreType.DMA((14,)), pltpu.SemaphoreType.DMA((2,))],
        compiler_params=_cparams("arbitrary", "arbitrary"),
    )(x2d, norm_g, w_nat, w_nat, w_out_s, small_s)


def _block_masks(tt):
    row = lax.broadcasted_iota(jnp.int32, (tt, tt), 0)
    col = lax.broadcasted_iota(jnp.int32, (tt, tt), 1)
    same = jnp.right_shift(row, 6) == jnp.right_shift(col, 6)
    return (jnp.logical_and(same, col <= row), jnp.logical_and(same, col >= row), jnp.logical_and(same, col > row))


def _dot_split3(ones_mat, x):
    x1 = x.astype(BF16)
    r1 = x - x1.astype(F32)
    x2 = r1.astype(BF16)
    x3 = (r1 - x2.astype(F32)).astype(BF16)
    return (_dot(ones_mat, x3) + _dot(ones_mat, x2)) + _dot(ones_mat, x1)


def _log_gate(logits):
    return (jnp.minimum(logits, 0.0) - jnp.log1p(jnp.exp(-jnp.abs(logits)))) * GATE_SCALE


def _chunk_column_mask(tt):
    nc = tt // CHUNK
    row = lax.broadcasted_iota(jnp.int32, (tt, nc * DK), 0)
    col = lax.broadcasted_iota(jnp.int32, (tt, nc * DK), 1)
    return jnp.right_shift(row, 6) == jnp.right_shift(col, 7)


def _chunked(mask, x, nc):
    wide = jnp.concatenate([x] * nc, axis=1)
    return jnp.where(mask, wide, jnp.zeros_like(wide))


def _gla_fwd(proj, lr, wgk_f, wgk_b, bgk_f, bgk_b, tt):
    seq = proj.shape[0]
    nb, nc, nch = seq // tt, tt // CHUNK, seq // CHUNK

    def body(qf, kf, vf, lrf, qb, kb, vb, lrb, wf, wb, bf, bb, of, ob, stf, stb, s_scr, qs_s, ks_s, qin_s, kout_s):
        @pl.when(pl.program_id(0) == 0)
        def _():
            s_scr[...] = jnp.zeros(s_scr.shape, F32)

        low, upp, sup = _block_masks(tt)
        dirs = ((qf, kf, vf, lrf, wf, bf, of, stf, low, low, REF_F, LAST_F, list(range(nc))),
                (qb, kb, vb, lrb, wb, bb, ob, stb, upp, sup, REF_B, LAST_B, list(reversed(range(nc)))))
        for d, (q_r, k_r, v_r, lr_r, w_r, b_r, o_r, st_r, cum, mask, ref, last, order) in enumerate(dirs):
            logits = _dot(lr_r[...].astype(BF16), w_r[...]) + b_r[...]
            b = _dot_split3(cum.astype(BF16), _log_gate(logits))
            decs = []
            for c in range(nc):
                rows = slice(c * CHUNK, (c + 1) * CHUNK)
                bc = b[rows]
                b_ref, b_last = bc[ref:ref + 1], bc[last:last + 1]
                qc = q_r[rows, :].astype(F32) * QSCALE
                kc = k_r[rows, :].astype(F32)
                qs_s[rows, :] = (qc * jnp.exp(bc - b_ref)).astype(BF16)
                ks_s[rows, :] = (kc * jnp.exp(b_ref - bc)).astype(BF16)
                qin_s[rows, :] = (qc * jnp.exp(bc)).astype(BF16)
                kout_s[rows, :] = (kc * jnp.exp(b_last - bc)).astype(BF16)
                decs.append(jnp.exp(b_last))
            for h in range(HEADS):
                ksl = slice(h * DK, (h + 1) * DK)
                vsl = slice(h * DV, (h + 1) * DV)
                v = v_r[:, vsl].astype(BF16)
                att = jnp.where(mask, _dot_nt(qs_s[:, ksl], ks_s[:, ksl]), 0.0).astype(BF16)
                o_intra = _dot(att, v)
                st = s_scr[d * HEADS + h]
                for c in order:
                    rows = slice(c * CHUNK, (c + 1) * CHUNK)
                    stb = st.astype(BF16)
                    st_r[c, h] = stb
                    o_r[rows, vsl] = (o_intra[rows] + _dot_nt(qin_s[rows, ksl], stb)).astype(BF16)
                    st = st * decs[c][:, ksl] + _dot_tn(v[rows], kout_s[rows, ksl])
                s_scr[d * HEADS + h] = st

    fw = lambda i: (i, 0)
    bw = lambda i: (nb - 1 - i, 0)
    const = lambda i: (0, 0)

    def tok_specs(m):
        return [pl.BlockSpec((tt, QK_W), lambda i: (m(i)[0], OFF_Q // QK_W)),
                pl.BlockSpec((tt, QK_W), lambda i: (m(i)[0], OFF_K // QK_W)),
                pl.BlockSpec((tt, V_W), lambda i: (m(i)[0], OFF_V // V_W)),
                pl.BlockSpec((tt, LR_W), m)]

    st_shape = jax.ShapeDtypeStruct((nch, HEADS, DV, DK), BF16)
    o_shape = jax.ShapeDtypeStruct((seq, V_W), BF16)
    operand = pltpu.VMEM((tt, QK_W), BF16)
    return pl.pallas_call(
        body, name="gla_fwd",
        out_shape=(o_shape, o_shape, st_shape, st_shape),
        grid=(nb,),
        in_specs=tok_specs(fw) + tok_specs(bw) + [
            pl.BlockSpec((LR_W, QK_W), const), pl.BlockSpec((LR_W, QK_W), const),
            pl.BlockSpec((1, QK_W), const), pl.BlockSpec((1, QK_W), const)],
        out_specs=(pl.BlockSpec((tt, V_W), fw), pl.BlockSpec((tt, V_W), bw),
                   pl.BlockSpec((nc, HEADS, DV, DK), lambda i: (i, 0, 0, 0)),
                   pl.BlockSpec((nc, HEADS, DV, DK), lambda i: (nb - 1 - i, 0, 0, 0))),
        scratch_shapes=[pltpu.VMEM((2 * HEADS, DV, DK), F32), operand, operand, operand, operand],
        compiler_params=_cparams("arbitrary"),
    )(proj, proj, proj, lr, proj, proj, proj, lr, wgk_f, wgk_b, bgk_f, bgk_b)


def _head_norm(o, gain):
    outs, rinv = [], []
    for h in range(HEADS):
        oh = o[:, h * DV:(h + 1) * DV]
        r = lax.rsqrt(jnp.mean(oh * oh, axis=-1, keepdims=True) + EPS)
        outs.append((oh * r) * gain)
        rinv.append(r)
    return jnp.concatenate(outs, axis=1), rinv


def _shift_rows(u, prev_row, next_row):
    n = u.shape[0]
    row = lax.broadcasted_iota(jnp.int32, (n, 1), 0)
    up = jnp.where(row == 0, prev_row, pltpu.roll(u, 1, 0))
    un = jnp.where(row == n - 1, next_row, pltpu.roll(u, n - 1, 0))
    return up, un


HALO = 16


def _halo_specs(tm, seq, col_block):
    per = tm // HALO
    last = seq // HALO - 1
    return [pl.BlockSpec((HALO, CONV_W), lambda i: (jnp.maximum(i * per - 1, 0), col_block)),
            pl.BlockSpec((HALO, CONV_W), lambda i: (jnp.minimum((i + 1) * per, last), col_block))]


def _f32(ref):
    return ref[...].astype(F32)


def _last_row(ref):
    return ref[HALO - 1:HALO, :].astype(F32)


def _first_row(ref):
    return ref[0:1, :].astype(F32)


def _mix_out_loss(o_f, o_b, proj, x2d, tgt, gla_g, conv_w, conv_b, w_out, final_g, tm):
    seq = x2d.shape[0]
    nt = seq // tm

    def body(of, ob, za, bg, cg, hc, zc, cprev, cnext, hprev, hnext, x_ref, t_ref, gg, cw, cb, wo, fg,
             yt_ref, conv_ref, dx2_ref, dx2b_ref, loss_ref, dfg_ref):
        i = pl.program_id(0)

        @pl.when(i == 0)
        def _():
            loss_ref[...] = jnp.zeros(loss_ref.shape, F32)
            dfg_ref[...] = jnp.zeros(dfg_ref.shape, F32)

        on, _ = _head_norm(_f32(of) + _f32(ob), gg[...])
        zav = _f32(za)
        y_a = on * (zav * _sigmoid(zav))
        u = _f32(cg) * _f32(hc)
        prev_row = jnp.where(i > 0, _last_row(cprev) * _last_row(hprev), 0.0)
        next_row = jnp.where(i < nt - 1, _first_row(cnext) * _first_row(hnext), 0.0)
        up, un = _shift_rows(u, prev_row, next_row)
        conv = (cw[0:1, :] * up + cw[1:2, :] * u + cw[2:3, :] * un) + cb[...]
        conv_ref[...] = conv.astype(BF16)
        zcv = _f32(zc)
        y_c = _f32(bg) * conv * (zcv * _sigmoid(zcv))
        y = jnp.concatenate([y_a, y_c], axis=1)
        yt_ref[...] = y.T.astype(BF16)
        x2 = x_ref[...] + _dot(y.astype(BF16), wo[...])
        r = lax.rsqrt(jnp.mean(x2 * x2, axis=-1, keepdims=True) + EPS)
        xn = x2 * r
        err = xn * fg[...] - t_ref[...]
        loss_ref[...] += 0.5 * jnp.sum(jnp.mean(err * err, axis=-1, keepdims=True))
        dyf = err * (1.0 / D_MODEL)
        dfg_ref[...] += jnp.sum(dyf * xn, axis=0, keepdims=True)
        dxn = dyf * fg[...]
        dx2 = r * dxn - xn * (r * jnp.mean(dxn * xn, axis=-1, keepdims=True))
        dx2_ref[...] = dx2
        dx2b_ref[...] = dx2.astype(BF16)

    def col(off):
        return pl.BlockSpec((tm, CONV_W), lambda i: (i, off // CONV_W))

    rowt = pl.BlockSpec((tm, D_MODEL), lambda i: (i, 0))
    const = lambda shape: pl.BlockSpec(shape, lambda i: (0, 0))
    return pl.pallas_call(
        body, name="mix_out_loss",
        out_shape=(jax.ShapeDtypeStruct((MIX_W, seq), BF16), jax.ShapeDtypeStruct((seq, CONV_W), BF16),
                   jax.ShapeDtypeStruct((seq, D_MODEL), F32), jax.ShapeDtypeStruct((seq, D_MODEL), BF16),
                   jax.ShapeDtypeStruct((8, 128), F32), jax.ShapeDtypeStruct((1, D_MODEL), F32)),
        grid=(nt,),
        in_specs=[rowt, rowt, col(OFF_ZA), col(OFF_B), col(OFF_C), col(OFF_H), col(OFF_ZC)]
        + _halo_specs(tm, seq, OFF_C // CONV_W) + _halo_specs(tm, seq, OFF_H // CONV_W)
        + [rowt, rowt, const((1, DV)), const((8, CONV_W)), const((1, CONV_W)), const((MIX_W, D_MODEL)),
           const((1, D_MODEL))],
        out_specs=(pl.BlockSpec((MIX_W, tm), lambda i: (0, i)), rowt, rowt, rowt, const((8, 128)),
                   const((1, D_MODEL))),
        compiler_params=_cparams("arbitrary"),
    )(o_f, o_b, proj, proj, proj, proj, proj, proj, proj, proj, proj, x2d, tgt, gla_g, conv_w, conv_b, w_out, final_g)


def _dsilu(z, s):
    return s * (1.0 + z * (1.0 - s))


def _mix_bwd(dx2b, o_f, o_b, proj, conv, gla_g, w_out, tm):
    seq = dx2b.shape[0]

    def body(dx, of, ob, za, bg, zc, cv, gg, wo, dg_ref, do_ref, dconv_ref, dgg_ref, dcb_ref):
        @pl.when(pl.program_id(0) == 0)
        def _():
            dgg_ref[...] = jnp.zeros(dgg_ref.shape, F32)
            dcb_ref[...] = jnp.zeros(dcb_ref.shape, F32)

        dy = _dot_nt(dx[...], wo[...])
        dy_a, dy_c = dy[:, :V_W], dy[:, V_W:]
        zcv, bgv, convv = _f32(zc), _f32(bg), _f32(cv)
        sc = _sigmoid(zcv)
        szc = zcv * sc
        dg_ref[:, CONV_W:2 * CONV_W] = (dy_c * convv * szc).astype(BF16)
        dconv = dy_c * bgv * szc
        dconv_ref[...] = dconv.astype(BF16)
        dcb_ref[...] += jnp.sum(dconv, axis=0, keepdims=True)
        dg_ref[:, 2 * CONV_W:] = (dy_c * bgv * convv * _dsilu(zcv, sc)).astype(BF16)

        o = _f32(of) + _f32(ob)
        gain = gg[...]
        on, rinv = _head_norm(o, gain)
        zav = _f32(za)
        sa = _sigmoid(zav)
        dg_ref[:, :CONV_W] = (dy_a * on * _dsilu(zav, sa)).astype(BF16)
        don = dy_a * (zav * sa)
        dgg = jnp.zeros((1, DV), F32)
        dos = []
        for h in range(HEADS):
            sl = slice(h * DV, (h + 1) * DV)
            oh, r, dh = o[:, sl], rinv[h], don[:, sl]
            ohn = oh * r
            dgg = dgg + jnp.sum(dh * ohn, axis=0, keepdims=True)
            dn = dh * gain
            dos.append(r * dn - ohn * (r * jnp.mean(dn * ohn, axis=-1, keepdims=True)))
        dgg_ref[...] += dgg
        do_ref[...] = jnp.concatenate(dos, axis=1).astype(BF16)

    def col(off):
        return pl.BlockSpec((tm, CONV_W), lambda i: (i, off // CONV_W))

    rowt = pl.BlockSpec((tm, D_MODEL), lambda i: (i, 0))
    const = lambda shape: pl.BlockSpec(shape, lambda i: (0, 0))
    return pl.pallas_call(
        body, name="mix_bwd",
        out_shape=(jax.ShapeDtypeStruct((seq, GATES_W), BF16), jax.ShapeDtypeStruct((seq, V_W), BF16),
                   jax.ShapeDtypeStruct((seq, CONV_W), BF16),
                   jax.ShapeDtypeStruct((1, DV), F32), jax.ShapeDtypeStruct((1, CONV_W), F32)),
        grid=(seq // tm,),
        in_specs=[rowt, rowt, rowt, col(OFF_ZA), col(OFF_B), col(OFF_ZC), rowt, const((1, DV)),
                  const((MIX_W, D_MODEL))],
        out_specs=(pl.BlockSpec((tm, GATES_W), lambda i: (i, 0)), rowt, rowt, const((1, DV)), const((1, CONV_W))),
        compiler_params=_cparams("arbitrary"),
    )(dx2b, o_f, o_b, proj, proj, proj, conv, gla_g, w_out)


def _conv_bwd(dconv, proj, conv_w, tm):
    seq = dconv.shape[0]
    nt = seq // tm

    def body(dc_in, dprev, dnext, cg, hc, cprev, cnext, hprev, hnext, cw, dch_ref, dcw_ref):
        i = pl.program_id(0)

        @pl.when(i == 0)
        def _():
            dcw_ref[...] = jnp.zeros(dcw_ref.shape, F32)

        first, lastt = i > 0, i < nt - 1
        dcv = _f32(dc_in)
        d_up, d_un = _shift_rows(dcv, jnp.where(first, _last_row(dprev), 0.0), jnp.where(lastt, _first_row(dnext), 0.0))
        cgv, hcv = _f32(cg), _f32(hc)
        u = cgv * hcv
        u_up, u_un = _shift_rows(u, jnp.where(first, _last_row(cprev) * _last_row(hprev), 0.0),
                                 jnp.where(lastt, _first_row(cnext) * _first_row(hnext), 0.0))
        du = cw[0:1, :] * d_un + cw[1:2, :] * dcv + cw[2:3, :] * d_up
        dch_ref[:, :CONV_W] = (du * hcv).astype(BF16)
        dch_ref[:, CONV_W:] = (du * cgv).astype(BF16)
        dcw_ref[0:1, :] += jnp.sum(dcv * u_up, axis=0, keepdims=True)
        dcw_ref[1:2, :] += jnp.sum(dcv * u, axis=0, keepdims=True)
        dcw_ref[2:3, :] += jnp.sum(dcv * u_un, axis=0, keepdims=True)

    def col(off):
        return pl.BlockSpec((tm, CONV_W), lambda i: (i, off // CONV_W))

    rowt = pl.BlockSpec((tm, CONV_W), lambda i: (i, 0))
    const = lambda shape: pl.BlockSpec(shape, lambda i: (0, 0))
    return pl.pallas_call(
        body, name="conv_bwd",
        out_shape=(jax.ShapeDtypeStruct((seq, CH_W), BF16), jax.ShapeDtypeStruct((8, CONV_W), F32)),
        grid=(nt,),
        in_specs=[rowt] + _halo_specs(tm, seq, 0) + [col(OFF_C), col(OFF_H)]
        + _halo_specs(tm, seq, OFF_C // CONV_W) + _halo_specs(tm, seq, OFF_H // CONV_W) + [const((8, CONV_W))],
        out_specs=(pl.BlockSpec((tm, CH_W), lambda i: (i, 0)), const((8, CONV_W))),
        compiler_params=_cparams("arbitrary"),
    )(dconv, dconv, dconv, proj, proj, proj, proj, proj, proj, conv_w)


def _gla_bwd(proj, lr, do, st_f, st_b, wgk_f, wgk_b, bgk_f, bgk_b, tt):
    seq = proj.shape[0]
    nb, nc = seq // tt, tt // CHUNK

    def body(qf, kf, vf, lrf, dof, stf, qb, kb, vb, lrb, dob, stb, wf, wb, bf, bb,
             dqkv_f, dlr_f, dqkv_b, dlr_b, dwf, dwb, dbf, dbb,
             ds_scr, eq_s, ek_s, ein_s, eout_s, qs_s, ks_s, qin_s, kout_s, db_s, lg_s):
        @pl.when(pl.program_id(0) == 0)
        def _():
            ds_scr[...] = jnp.zeros(ds_scr.shape, F32)
            for r in (dwf, dwb, dbf, dbb):
                r[...] = jnp.zeros(r.shape, F32)

        low, upp, sup = _block_masks(tt)
        row = lax.broadcasted_iota(jnp.int32, (CHUNK, 1), 0)
        kmask = _chunk_column_mask(tt)
        dirs = ((qf, kf, vf, lrf, dof, stf, wf, bf, dqkv_f, dlr_f, dwf, dbf,
                 low, upp, low, REF_F, LAST_F, list(reversed(range(nc)))),
                (qb, kb, vb, lrb, dob, stb, wb, bb, dqkv_b, dlr_b, dwb, dbb,
                 upp, low, sup, REF_B, LAST_B, list(range(nc))))
        for d, (q_r, k_r, v_r, lr_r, do_r, st_r, w_r, b_r, dqkv_r, dlr_r, dw_r, db_r,
                cum, cum_t, mask, ref, last, order) in enumerate(dirs):
            lrv = lr_r[...].astype(BF16)
            wv = w_r[...]
            logits = _dot(lrv, wv) + b_r[...]
            lg_s[...] = logits
            b = _dot_split3(cum.astype(BF16), _log_gate(logits))
            decs = []
            for c in range(nc):
                rows = slice(c * CHUNK, (c + 1) * CHUNK)
                bc = b[rows]
                b_ref, b_last = bc[ref:ref + 1], bc[last:last + 1]
                qc = q_r[rows, :].astype(F32) * QSCALE
                kc = k_r[rows, :].astype(F32)
                e_q, e_k, e_in, e_out = jnp.exp(bc - b_ref), jnp.exp(b_ref - bc), jnp.exp(bc), jnp.exp(b_last - bc)
                eq_s[rows, :], ek_s[rows, :], ein_s[rows, :], eout_s[rows, :] = e_q, e_k, e_in, e_out
                qs_s[rows, :] = (qc * e_q).astype(BF16)
                ks_s[rows, :] = (kc * e_k).astype(BF16)
                qin_s[rows, :] = (qc * e_in).astype(BF16)
                kout_s[rows, :] = (kc * e_out).astype(BF16)
                decs.append(jnp.exp(b_last))
            for h in range(HEADS):
                ksl = slice(h * DK, (h + 1) * DK)
                vsl = slice(h * DV, (h + 1) * DV)
                v = v_r[:, vsl].astype(BF16)
                dov = do_r[:, vsl].astype(BF16)
                qsb, ksb = qs_s[:, ksl], ks_s[:, ksl]
                att = jnp.where(mask, _dot_nt(qsb, ksb), 0.0).astype(BF16)
                datt = jnp.where(mask, _dot_nt(dov, v), 0.0).astype(BF16)
                dqs = _dot(datt, ksb)
                dks = _dot_tn(datt, qsb)
                dv_intra = _dot_tn(att, dov)
                g_t = _dot_tn(dov, _chunked(kmask, qin_s[:, ksl], nc))
                ds = ds_scr[d * HEADS + h]
                for c in order:
                    rows = slice(c * CHUNK, (c + 1) * CHUNK)
                    dsb = ds.astype(BF16)
                    s_prev = st_r[c, h]
                    dk_out = _dot(v[rows], dsb)
                    dq_in = _dot(dov[rows], s_prev)
                    dv = dv_intra[rows] + _dot_nt(kout_s[rows, ksl], dsb)
                    dqkv_r[rows, OFF_V + h * DV:OFF_V + (h + 1) * DV] = dv.astype(BF16)
                    dec = decs[c][:, ksl]
                    ddec = jnp.sum(ds * s_prev.astype(F32), axis=0, keepdims=True)
                    e_out = eout_s[rows, ksl]
                    qc = q_r[rows, ksl].astype(F32) * QSCALE
                    kc = k_r[rows, ksl].astype(F32)
                    dq = dqs[rows] * eq_s[rows, ksl] + dq_in * ein_s[rows, ksl]
                    dk = dks[rows] * ek_s[rows, ksl] + dk_out * e_out
                    dqkv_r[rows, OFF_Q + h * DK:OFF_Q + (h + 1) * DK] = (dq * QSCALE).astype(BF16)
                    dqkv_r[rows, OFF_K + h * DK:OFF_K + (h + 1) * DK] = dk.astype(BF16)
                    tail = jnp.sum(dk_out * (kc * e_out), axis=0, keepdims=True) + ddec * dec
                    db_s[rows, ksl] = (qc * dq - kc * dk) + jnp.where(row == last, tail, 0.0)
                    ds = ds * dec + g_t[:, c * DK:(c + 1) * DK]
                ds_scr[d * HEADS + h] = ds
            dg = _dot_split3(cum_t.astype(BF16), db_s[...])
            dlogit = (dg * GATE_SCALE) * _sigmoid(-lg_s[...])
            dlb = dlogit.astype(BF16)
            dlr_r[...] = _dot_nt(dlb, wv)
            dw_r[...] += _dot_tn(lrv, dlb)
            db_r[...] += jnp.sum(dlogit, axis=0, keepdims=True)

    fw = lambda i: (nb - 1 - i, 0)
    bw = lambda i: (i, 0)
    const = lambda i: (0, 0)

    def tok_specs(m):
        return [pl.BlockSpec((tt, QK_W), lambda i: (m(i)[0], OFF_Q // QK_W)),
                pl.BlockSpec((tt, QK_W), lambda i: (m(i)[0], OFF_K // QK_W)),
                pl.BlockSpec((tt, V_W), lambda i: (m(i)[0], OFF_V // V_W)),
                pl.BlockSpec((tt, LR_W), m),
                pl.BlockSpec((tt, V_W), m),
                pl.BlockSpec((nc, HEADS, DV, DK), lambda i: (m(i)[0], 0, 0, 0))]

    dqkv = jax.ShapeDtypeStruct((seq, QK_W + QK_W + V_W), BF16)
    dlr = jax.ShapeDtypeStruct((seq, LR_W), F32)
    dw = jax.ShapeDtypeStruct((LR_W, QK_W), F32)
    dbias = jax.ShapeDtypeStruct((1, QK_W), F32)
    return pl.pallas_call(
        body, name="gla_bwd",
        out_shape=(dqkv, dlr, dqkv, dlr, dw, dw, dbias, dbias),
        grid=(nb,),
        in_specs=tok_specs(fw) + tok_specs(bw) + [
            pl.BlockSpec((LR_W, QK_W), const), pl.BlockSpec((LR_W, QK_W), const),
            pl.BlockSpec((1, QK_W), const), pl.BlockSpec((1, QK_W), const)],
        out_specs=(pl.BlockSpec((tt, QK_W + QK_W + V_W), fw), pl.BlockSpec((tt, LR_W), fw),
                   pl.BlockSpec((tt, QK_W + QK_W + V_W), bw), pl.BlockSpec((tt, LR_W), bw),
                   pl.BlockSpec((LR_W, QK_W), const), pl.BlockSpec((LR_W, QK_W), const),
                   pl.BlockSpec((1, QK_W), const), pl.BlockSpec((1, QK_W), const)),
        scratch_shapes=[pltpu.VMEM((2 * HEADS, DV, DK), F32)] + [pltpu.VMEM((tt, QK_W), F32)] * 4
        + [pltpu.VMEM((tt, QK_W), BF16)] * 4 + [pltpu.VMEM((tt, QK_W), F32)] * 2,
        compiler_params=_cparams("arbitrary"),
    )(proj, proj, proj, lr, do, st_f, proj, proj, proj, lr, do, st_b, wgk_f, wgk_b, bgk_f, bgk_b)


def _sum_directions(dqkv_f, dqkv_b, dlr_f, dlr_b, tm):
    seq = dqkv_f.shape[0]

    def body(a, b, la, lb, dp_out, dlr_out):
        dp_out[...] = (_f32(a) + _f32(b)).astype(BF16)
        dlr_out[...] = (la[...] + lb[...]).astype(BF16)

    rowt = pl.BlockSpec((tm, QKV_W), lambda i: (i, 0))
    lrt = pl.BlockSpec((tm, LR_W), lambda i: (i, 0))
    return pl.pallas_call(
        body, name="sum_directions",
        out_shape=(jax.ShapeDtypeStruct((seq, QKV_W), BF16), jax.ShapeDtypeStruct((seq, LR_W), BF16)),
        grid=(seq // tm,),
        in_specs=[rowt, rowt, lrt, lrt],
        out_specs=(rowt, lrt),
        compiler_params=_cparams("arbitrary"),
    )(dqkv_f, dqkv_b, dlr_f, dlr_b)


def _input_grad(dp_qkv, dp_gates, dp_ch, dlr, w_nat, x2d, norm_g, dx2, sums, tm):
    seq = x2d.shape[0]
    nt, n = seq // tm, len(sums)

    def body(dq, dg, dc, dl, w, x_ref, g_ref, dx2_ref, *rest):
        ins, (gx_ref, dng_ref), outs = rest[:n], rest[n:n + 2], rest[n + 2:2 * n + 2]
        send_sems, recv_sems = rest[2 * n + 2:]
        i = pl.program_id(0)

        @pl.when(i == 0)
        def _():
            for cp in _chip_copies(ins, outs, send_sems, recv_sems):
                cp.start()
            dng_ref[...] = jnp.zeros(dng_ref.shape, F32)

        dh = (_dot(dl[...], w[NAT_LR:NAT_LR + LR_W, :]) + _dot(dq[...], w[0:NAT_ZA, :])
              + _dot(dg[:, 0:CONV_W], w[NAT_ZA:NAT_LR, :]) + _dot(dg[:, CONV_W:2 * CONV_W], w[NAT_B:NAT_C, :])
              + _dot(dg[:, 2 * CONV_W:], w[NAT_ZC:IN_W, :]) + _dot(dc[...], w[NAT_C:NAT_ZC, :]))
        xv = x_ref[...]
        r = lax.rsqrt(jnp.mean(xv * xv, axis=-1, keepdims=True) + EPS)
        xn = xv * r
        dng_ref[...] += jnp.sum(dh * xn, axis=0, keepdims=True)
        dn = dh * g_ref[...]
        gx_ref[...] = (r * dn - xn * (r * jnp.mean(dn * xn, axis=-1, keepdims=True))) + dx2_ref[...]

        @pl.when(i == nt - 1)
        def _():
            copies = _chip_copies(ins, outs, send_sems, recv_sems)
            for cp in copies:
                cp.wait_recv()
            for cp in copies:
                cp.wait_send()

    rowt = pl.BlockSpec((tm, D_MODEL), lambda i: (i, 0))
    seg = lambda width: pl.BlockSpec((tm, width), lambda i: (i, 0))
    resident = lambda rows: pl.BlockSpec((rows, D_MODEL), lambda i: (0, 0), pipeline_mode=pl.Buffered(1))
    hbm = pl.BlockSpec(memory_space=pl.ANY)
    return pl.pallas_call(
        body, name="input_grad",
        out_shape=(jax.ShapeDtypeStruct((seq, D_MODEL), F32), jax.ShapeDtypeStruct((1, D_MODEL), F32))
        + tuple(jax.ShapeDtypeStruct((3,) + s.shape[1:], s.dtype) for s in sums),
        grid=(nt,),
        in_specs=[seg(QKV_W), seg(GATES_W), seg(CH_W), seg(LR_W), resident(IN_W),
                  rowt, pl.BlockSpec((1, D_MODEL), lambda i: (0, 0)), rowt] + [hbm] * n,
        out_specs=(rowt, pl.BlockSpec((1, D_MODEL), lambda i: (0, 0))) + (hbm,) * n,
        scratch_shapes=[pltpu.SemaphoreType.DMA((3 * n,)), pltpu.SemaphoreType.DMA((3 * n,))],
        compiler_params=_cparams("arbitrary"),
    )(dp_qkv, dp_gates, dp_ch, dlr, w_nat, x2d, norm_g, dx2, *sums)


def _weight_grad_out(y_t, dx2b, tk, riding):
    m, seq = y_t.shape
    n = dx2b.shape[1]
    nk = seq // tk

    def body(a_ref, b_ref, ride_in, o_ref, ride_out, send_sems, recv_sems):
        k = pl.program_id(0)

        @pl.when(k == 0)
        def _():
            _start_all(_sibling_copies(ride_in, ride_out, send_sems, recv_sems))
            o_ref[...] = jnp.zeros(o_ref.shape, F32)

        o_ref[...] += _dot(a_ref[...], b_ref[...])

        @pl.when(k == nk - 1)
        def _():
            _wait_all(_sibling_copies(ride_in, ride_out, send_sems, recv_sems))

    hbm = pl.BlockSpec(memory_space=pl.ANY)
    return pl.pallas_call(
        body, name="wgrad_out",
        out_shape=(jax.ShapeDtypeStruct((m, n), F32), jax.ShapeDtypeStruct((4,) + riding.shape[1:], F32)),
        grid=(nk,),
        in_specs=[pl.BlockSpec((m, tk), lambda k: (0, k)), pl.BlockSpec((tk, n), lambda k: (k, 0)), hbm],
        out_specs=(pl.BlockSpec((m, n), lambda k: (0, 0)), hbm),
        scratch_shapes=[pltpu.SemaphoreType.DMA((4,)), pltpu.SemaphoreType.DMA((4,))],
        compiler_params=_cparams("arbitrary"),
    )(y_t, dx2b, riding)


def _weight_grad_in(h_t, dp_qkv, dp_gates, dp_ch, dlr):
    m, seq = h_t.shape
    tn = 512
    n_qkv, n_gates, n_ch = QKV_W // tn, GATES_W // tn, CH_W // tn
    starts = ([k * tn for k in range(n_qkv)] + [NAT_ZA, NAT_ZA + tn, NAT_B, NAT_B + tn, NAT_ZC, NAT_ZC + tn]
              + [NAT_C + k * tn for k in range(n_ch)])

    def out_row(j):
        row = 0
        for k, start in enumerate(starts):
            row = row + jnp.where(j == k, start // 32, 0)
        return pl.multiple_of(row * 32, 32), 0

    def body(a_ref, bq, bg, bc, o_ref, acc):
        j = pl.program_id(0)

        @pl.when(j < n_qkv)
        def _():
            acc[...] = _dot(a_ref[...], bq[...])

        @pl.when(jnp.logical_and(j >= n_qkv, j < n_qkv + n_gates))
        def _():
            acc[...] = _dot(a_ref[...], bg[...])

        @pl.when(j >= n_qkv + n_gates)
        def _():
            acc[...] = _dot(a_ref[...], bc[...])

        o_ref[...] = acc[...].T

    resident = pl.BlockSpec((m, seq), lambda j: (0, 0), pipeline_mode=pl.Buffered(1))
    seg = lambda first, count: pl.BlockSpec((seq, tn), lambda j: (0, jnp.clip(j - first, 0, count - 1)))
    main = pl.pallas_call(
        body, name="wgrad_in",
        out_shape=jax.ShapeDtypeStruct((IN_W, m), F32),
        grid=(n_qkv + n_gates + n_ch,),
        in_specs=[resident, seg(0, n_qkv), seg(n_qkv, n_gates), seg(n_qkv + n_gates, n_ch)],
        out_specs=pl.BlockSpec((pl.Element(tn), pl.Element(m)), out_row),
        scratch_shapes=[pltpu.VMEM((m, tn), F32)],
        compiler_params=_cparams("arbitrary"),
    )(h_t, dp_qkv, dp_gates, dp_ch)

    def lr_body(a_ref, b_ref, full_ref, o_ref, acc):
        acc[...] = _dot(a_ref[...], b_ref[...])
        o_ref[...] = acc[...].T[0:2 * RANK, :]

    whole = lambda shape: pl.BlockSpec(shape, lambda j: (0, 0))
    return pl.pallas_call(
        lr_body, name="wgrad_lr",
        out_shape=jax.ShapeDtypeStruct((IN_W, m), F32),
        grid=(1,),
        in_specs=[whole((m, seq)), whole((seq, LR_W)), pl.BlockSpec(memory_space=pl.ANY)],
        out_specs=pl.BlockSpec((pl.Element(2 * RANK), pl.Element(m)), lambda j: (NAT_LR, 0)),
        scratch_shapes=[pltpu.VMEM((m, LR_W), F32)],
        input_output_aliases={2: 0},
        compiler_params=_cparams("arbitrary"),
    )(h_t, dlr, main)


def _pad_rows(a, rows):
    return jnp.pad(a, ((0, rows - a.shape[0]), (0, 0)))


def _rows128(a):
    a = a.reshape(-1, 128)
    return _pad_rows(a, -(-a.shape[0] // 8) * 8)


def _pack(arrs):
    return jnp.concatenate([_rows128(a) for a in arrs], axis=0)


def _unpack(buf, like):
    out, start = [], 0
    for a in like:
        rows = a.size // 128
        out.append(buf[start:start + rows].reshape(a.shape))
        start += -(-rows // 8) * 8
    return out


def kernel(x, norm_g, w_in, w_gk_f, b_gk_f, w_gk_b, b_gk_b, gla_norm_g, conv_w, conv_b, w_out, final_g, loss_target, m_norm_g, m_w_in, m_w_gk_f, m_b_gk_f, m_w_gk_b, m_b_gk_b, m_gla_norm_g, m_conv_w, m_conv_b, m_w_out, m_final_g, v_norm_g, v_w_in, v_w_gk_f, v_b_gk_f, v_w_gk_b, v_b_gk_b, v_gla_norm_g, v_conv_w, v_conv_b, v_w_out, v_final_g):
    px, py, pc = _position()
    me = _blk(px, py, pc)
    seq = x.shape[1]
    x2d, tgt = x[0], loss_target[0]
    tm = min(512, seq)
    tt = min(256, seq)

    small_s = jnp.concatenate([jnp.concatenate([w_gk_f[0], w_gk_b[0]], axis=1), _pad_rows(conv_w[0], 8)], axis=0)
    w_nat = _allgather_w_in(w_in[0].T).reshape(IN_W, D_MODEL)

    proj, lr, h_t, wout_all, small_all = _inproj(x2d, norm_g, w_nat, w_out[0], small_s, min(1024, seq))
    w_out_full = wout_all.reshape(MIX_W, D_MODEL)
    wgk_cols = 512 // N_DEV
    wgk_f_full = small_all[:, 0:RANK, 0:wgk_cols].transpose(1, 0, 2).reshape(RANK, QK_W)
    wgk_b_full = small_all[:, 0:RANK, wgk_cols:2 * wgk_cols].transpose(1, 0, 2).reshape(RANK, QK_W)
    conv_w_full = _pad_rows(small_all[:, RANK:RANK + 3, :].transpose(1, 0, 2).reshape(3, CONV_W), 8)
    zr = lambda n: jnp.zeros((n, QK_W), F32)
    wgk_f_pad = jnp.concatenate([wgk_f_full, zr(LR_W - RANK)], axis=0).astype(BF16)
    wgk_b_pad = jnp.concatenate([zr(RANK), wgk_b_full, zr(LR_W - 2 * RANK)], axis=0).astype(BF16)

    o_f, o_b, st_f, st_b = _gla_fwd(proj, lr, wgk_f_pad, wgk_b_pad, b_gk_f, b_gk_b, tt)
    tmix = min(256, seq)
    y_t, conv, dx2, dx2b, loss_p, dfg_p = _mix_out_loss(o_f, o_b, proj, x2d, tgt, gla_norm_g, conv_w_full, conv_b,
                                                        w_out_full, final_g.reshape(1, D_MODEL), tmix)

    dp_gates, do, dconv, dgg_p, dcb_p = _mix_bwd(dx2b, o_f, o_b, proj, conv, gla_norm_g, w_out_full, tmix)
    dp_ch, dcw_p = _conv_bwd(dconv, proj, conv_w_full, tmix)
    dqkv_f, dlr_f, dqkv_b, dlr_b, dwf_p, dwb_p, dbf_p, dbb_p = _gla_bwd(
        proj, lr, do, st_f, st_b, wgk_f_pad, wgk_b_pad, b_gk_f, b_gk_b, tt)
    dp_qkv, dlr = _sum_directions(dqkv_f, dqkv_b, dlr_f, dlr_b, tm)
    dw_nat = _weight_grad_in(h_t, dp_qkv, dp_gates, dp_ch, dlr)

    part_in = dw_nat.reshape(N_DEV, SHARD_W, D_MODEL)
    dw_out, sib_in = _weight_grad_out(y_t, dx2b, tm, part_in)
    part_out = dw_out.reshape(N_DEV, MIX_W // N_DEV, D_MODEL)
    core = jnp.reshape(pc, (1,)).astype(jnp.int32)
    chip = jnp.reshape(2 * px + py, (1,)).astype(jnp.int32)
    sums_in, sib_out = _chip_sums(part_in, sib_in, core, 256, "chip_sums_in", riding=part_out)
    sums_out = _chip_sums(part_out, sib_out, core, 256, "chip_sums_out")
    grad_x2d, dng_p, far_in, far_out = _input_grad(dp_qkv, dp_gates, dp_ch, dlr, w_nat, x2d, norm_g, dx2,
                                                   [sums_in, sums_out], tmix)
    pieces = [dng_p, dbf_p, dbb_p, dgg_p, dcb_p, dfg_p[0], dwf_p[0:RANK], dwb_p[RANK:2 * RANK], dcw_p[0:3], loss_p[0]]
    g_in_t, small_tot = _final_sum(sums_in, far_in, chip, _pack(pieces), 256, "final_sum_in")
    g_w_out, d_w_out, nm_w_out, nv_w_out = _final_sum_adamw(sums_out, far_out, chip, w_out[0], m_w_out[0], v_w_out[0],
                                                            256, "adamw_out")
    flat = lambda a: a[0].T.reshape(SHARD_W, D_MODEL // 128, 128)
    unflat = lambda a: a.reshape(SHARD_W, D_MODEL).T
    d_flat, m_flat, v_flat = _adamw_rows(g_in_t.reshape(SHARD_W, D_MODEL // 128, 128), flat(w_in), flat(m_w_in),
                                         flat(v_w_in), 90, "adamw_in")
    g_w_in, d_w_in, nm_w_in, nv_w_in = g_in_t.T, unflat(d_flat), unflat(m_flat), unflat(v_flat)

    tot = _unpack(small_tot, pieces)
    g_norm_g, g_b_gk_f, g_b_gk_b, g_gla, g_conv_b, g_final = tot[:6]
    g_wgk_f = lax.dynamic_slice_in_dim(tot[6], me * wgk_cols, wgk_cols, axis=1)[None]
    g_wgk_b = lax.dynamic_slice_in_dim(tot[7], me * wgk_cols, wgk_cols, axis=1)[None]
    g_conv_w = lax.dynamic_slice_in_dim(tot[8], me * 128, 128, axis=1)[None]
    loss = tot[9][0]

    small_g = [g_norm_g, g_b_gk_f, g_b_gk_b, g_gla, g_conv_b, g_final, g_wgk_f, g_wgk_b, g_conv_w]
    small_w = [norm_g, b_gk_f, b_gk_b, gla_norm_g, conv_b, final_g, w_gk_f, w_gk_b, conv_w]
    small_m = [m_norm_g, m_b_gk_f, m_b_gk_b, m_gla_norm_g, m_conv_b, m_final_g, m_w_gk_f, m_w_gk_b, m_conv_w]
    small_v = [v_norm_g, v_b_gk_f, v_b_gk_b, v_gla_norm_g, v_conv_b, v_final_g, v_w_gk_f, v_w_gk_b, v_conv_w]
    d_s, m_s, v_s = _adamw_small(_pack(small_g), _pack(small_w), _pack(small_m), _pack(small_v))
    d_l, m_l, v_l = _unpack(d_s, small_w), _unpack(m_s, small_w), _unpack(v_s, small_w)

    def ordered(sm, big_in, big_out):
        return [sm[0], big_in[None], sm[6], sm[1], sm[7], sm[2], sm[3], sm[8], sm[4], big_out[None], sm[5]]

    grads = ordered(small_g, g_w_in, g_w_out)
    deltas = ordered(d_l, d_w_in, d_w_out)
    new_m = ordered(m_l, nm_w_in, nm_w_out)
    new_v = ordered(v_l, nv_w_in, nv_w_out)
    return (loss, grad_x2d[None], *grads, *deltas, *new_m, *new_v)
```

```python
import jax
import jax.numpy as jnp
from jax import lax
from jax.experimental import pallas as pl
from jax.experimental.pallas import tpu as pltpu

F32 = jnp.float32
BF16 = jnp.bfloat16
MESH = pl.DeviceIdType.MESH

N_DEV = 8
D_MODEL = 1024
HEADS = 4
DK = 128
DV = 256
QK_W = HEADS * DK
V_W = HEADS * DV
CONV_W = 1024
MIX_W = V_W + CONV_W
CHUNK = 64
RANK = 16
IN_W = 7200
SHARD_W = IN_W // N_DEV
MAIN_W = 7168
LR_W = 128
OFF_Q, OFF_K, OFF_V, OFF_ZA, OFF_B, OFF_ZC, OFF_C, OFF_H = 0, 512, 1024, 2048, 3072, 4096, 5120, 6144
QKV_W, GATES_W, CH_W = 2048, 3072, 2048
NAT_ZA, NAT_LR, NAT_B, NAT_C, NAT_ZC = 2048, 3072, 3104, 4128, 6176
EPS = 1e-6
GATE_SCALE = 1.0 / 16.0
QSCALE = DK ** -0.5
REF_F, LAST_F = CHUNK // 2, CHUNK - 1
REF_B, LAST_B = CHUNK - 1 - CHUNK // 2, 0

ADAM_LR = 0.001
ADAM_B1 = 0.9
ADAM_B2 = 0.999
ADAM_EPS = 1e-08
ADAM_WD = 0.01
ADAM_STEP = 10

VMEM_LIMIT = 56 * 1024 * 1024


def _cparams(*sem):
    return pltpu.CompilerParams(dimension_semantics=sem, vmem_limit_bytes=VMEM_LIMIT)


def _dot(a, b):
    return jnp.dot(a, b, preferred_element_type=F32)


def _dot_nt(a, b):
    return lax.dot_general(a, b, (((1,), (1,)), ((), ())), preferred_element_type=F32)


def _dot_tn(a, b):
    return lax.dot_general(a, b, (((0,), (0,)), ((), ())), preferred_element_type=F32)


def _sigmoid(z):
    return jax.nn.sigmoid(z)


def _position():
    return lax.axis_index("x"), lax.axis_index("y"), lax.axis_index("c")


def _blk(px, py, pc):
    return 4 * px + 2 * py + pc


def _two_level_gather(outs, send_sems, recv_sems):
    x, y, c = _position()
    me, sibling = (x, y, c), (x, y, 1 - c)
    chips = [(1 - x, y), (x, 1 - y), (1 - x, 1 - y)]
    n = len(outs)

    def copy(a, k, block, to):
        ref = outs[a].at[_blk(*block)]
        return pltpu.make_async_remote_copy(src_ref=ref, dst_ref=ref, send_sem=send_sems.at[a * 7 + k],
                                            recv_sem=recv_sems.at[a * 7 + k], device_id=to, device_id_type=MESH)

    first = []
    for a in range(n):
        first.append(copy(a, 0, me, sibling))
        first += [copy(a, 1 + j, me, (*chip, c)) for j, chip in enumerate(chips)]
    for cp in first:
        cp.start()
    passed = []
    for j, chip in enumerate(chips):
        for a in range(n):
            copy(a, 1 + j, (*chip, c), me).wait_recv()
            fwd = copy(a, 4 + j, (*chip, c), sibling)
            fwd.start()
            passed.append(fwd)
    for a in range(n):
        copy(a, 0, sibling, me).wait_recv()
    for j, chip in enumerate(chips):
        for a in range(n):
            copy(a, 4 + j, (*chip, 1 - c), me).wait_recv()
    for cp in first + passed:
        cp.wait_send()


def _allgather_w_in(w_in_t):
    def body(win_ref, win_all, send_sems, recv_sems):
        win_all[_blk(*_position())] = win_ref[...].astype(BF16)
        _two_level_gather((win_all,), send_sems, recv_sems)

    vmem = pl.BlockSpec(memory_space=pltpu.VMEM)
    return pl.pallas_call(
        body, name="allgather_w_in",
        out_shape=jax.ShapeDtypeStruct((N_DEV,) + w_in_t.shape, BF16),
        in_specs=[vmem], out_specs=vmem,
        scratch_shapes=[pltpu.SemaphoreType.DMA((7,)), pltpu.SemaphoreType.DMA((7,))],
        compiler_params=pltpu.CompilerParams(vmem_limit_bytes=VMEM_LIMIT),
    )(w_in_t)


def _peer_copies(srcs, outs, send_sems, recv_sems):
    x, y, c = _position()
    me = _blk(x, y, c)
    copies = []
    for a, (src, out) in enumerate(zip(srcs, outs)):
        k = 0
        for dx in (0, 1):
            for dy in (0, 1):
                for dc in (0, 1):
                    if dx + dy + dc == 0:
                        continue
                    peer = (1 - x if dx else x, 1 - y if dy else y, 1 - c if dc else c)
                    copies.append(pltpu.make_async_remote_copy(
                        src_ref=src, dst_ref=out.at[me], send_sem=send_sems.at[a * 7 + k],
                        recv_sem=recv_sems.at[a * 7 + k], device_id=peer, device_id_type=MESH))
                    k += 1
    return copies


def _chip_copies(ins, outs, send_sems, recv_sems):
    x, y, c = _position()
    chips = [(1 - x, y), (x, 1 - y), (1 - x, 1 - y)]
    copies = []
    for a in range(len(ins)):
        for j, (px, py) in enumerate(chips):
            copies.append(pltpu.make_async_remote_copy(
                src_ref=ins[a].at[2 * px + py], dst_ref=outs[a].at[j],
                send_sem=send_sems.at[a * 3 + j], recv_sem=recv_sems.at[a * 3 + j],
                device_id=(px, py, c), device_id_type=MESH))
    return copies


WINDOW_ROWS = SHARD_W + 4


def _window_start(k, parity):
    return 2 * SHARD_W * k + (SHARD_W - 4) * parity


def _owner_block(part, k, parity):
    if part.ndim == 3:
        return part.at[2 * k + parity]
    return part.at[pl.ds(pl.multiple_of(_window_start(k, parity), 8), WINDOW_ROWS)]


def _block_shape(part):
    return part.shape[1:] if part.ndim == 3 else (WINDOW_ROWS, part.shape[1])


def _sibling_copies(part, out, send_sems, recv_sems):
    x, y, c = _position()
    return [pltpu.make_async_remote_copy(src_ref=_owner_block(part, k, 1 - c), dst_ref=out.at[k],
                                         send_sem=send_sems.at[k], recv_sem=recv_sems.at[k],
                                         device_id=(x, y, 1 - c), device_id_type=MESH)
            for k in range(4)]


def _start_all(copies):
    for cp in copies:
        cp.start()


def _wait_all(copies):
    for cp in copies:
        cp.wait_recv()
    for cp in copies:
        cp.wait_send()


def _chip_sums(part, from_sibling, core, tc, name, riding=None):
    rows, cols = _block_shape(part)
    nj = cols // tc

    def body(core_ref, p_ref, s_ref, *rest):
        if riding is None:
            (o_ref,) = rest
        else:
            ride_in, o_ref, ride_out, send_sems, recv_sems = rest
            k, j = pl.program_id(0), pl.program_id(1)

            @pl.when(jnp.logical_and(k == 0, j == 0))
            def _():
                _start_all(_sibling_copies(ride_in, ride_out, send_sems, recv_sems))

        o_ref[0] = (p_ref[...].reshape(rows, tc) + s_ref[0]).astype(BF16)

        if riding is not None:
            @pl.when(jnp.logical_and(k == 3, j == nj - 1))
            def _():
                _wait_all(_sibling_copies(ride_in, ride_out, send_sems, recv_sems))

    hbm = pl.BlockSpec(memory_space=pl.ANY)
    sums = jax.ShapeDtypeStruct((4, rows, cols), BF16)
    tile_out = pl.BlockSpec((1, rows, tc), lambda k, j, core_ref: (k, 0, j))
    if part.ndim == 3:
        mine = pl.BlockSpec((1, rows, tc), lambda k, j, core_ref: (2 * k + core_ref[0], 0, j))
    else:
        mine = pl.BlockSpec((pl.Element(rows), pl.Element(tc)),
                            lambda k, j, core_ref: (pl.multiple_of(_window_start(k, core_ref[0]), 8),
                                                    pl.multiple_of(j * tc, 128)))
    in_specs = [mine, pl.BlockSpec((1, rows, tc), lambda k, j, core_ref: (k, 0, j))]
    if riding is None:
        out_shape, out_specs, scratch, args = sums, tile_out, [], (core, part, from_sibling)
    else:
        out_shape = (sums, jax.ShapeDtypeStruct((4,) + _block_shape(riding), F32))
        out_specs, in_specs = (tile_out, hbm), in_specs + [hbm]
        scratch = [pltpu.SemaphoreType.DMA((4,)), pltpu.SemaphoreType.DMA((4,))]
        args = (core, part, from_sibling, riding)
    return pl.pallas_call(
        body, name=name, out_shape=out_shape,
        grid_spec=pltpu.PrefetchScalarGridSpec(num_scalar_prefetch=1, grid=(4, nj), in_specs=in_specs,
                                               out_specs=out_specs, scratch_shapes=scratch),
        compiler_params=_cparams("arbitrary", "arbitrary"),
    )(*args)


def _sum_chips(s_ref, r_ref):
    f = lambda a: a.astype(F32)
    return ((f(s_ref[0]) + f(r_ref[0])) + f(r_ref[1])) + f(r_ref[2])


def _final_sum(sums, from_chips, chip, small, tc, name):
    _, rows, cols = sums.shape
    nj = cols // tc

    def body(chip_ref, s_ref, r_ref, sm_ref, g_out, tot_ref, all_ref, send_sems, recv_sems):
        j = pl.program_id(0)
        me = _blk(*_position())

        @pl.when(j == 0)
        def _():
            all_ref[me] = sm_ref[...]
            _start_all(_peer_copies((all_ref.at[me],), (all_ref,), send_sems, recv_sems))

        g_out[...] = _sum_chips(s_ref, r_ref)

        @pl.when(j == nj - 1)
        def _():
            _wait_all(_peer_copies((all_ref.at[me],), (all_ref,), send_sems, recv_sems))
            acc = all_ref[0]
            for d in range(1, N_DEV):
                acc = acc + all_ref[d]
            tot_ref[...] = acc

    whole = pl.BlockSpec(small.shape, lambda j, chip_ref: (0, 0))
    return pl.pallas_call(
        body, name=name,
        out_shape=(jax.ShapeDtypeStruct((rows, cols), F32), jax.ShapeDtypeStruct(small.shape, F32)),
        grid_spec=pltpu.PrefetchScalarGridSpec(
            num_scalar_prefetch=1, grid=(nj,),
            in_specs=[pl.BlockSpec((1, rows, tc), lambda j, chip_ref: (chip_ref[0], 0, j)),
                      pl.BlockSpec((3, rows, tc), lambda j, chip_ref: (0, 0, j)), whole],
            out_specs=(pl.BlockSpec((rows, tc), lambda j, chip_ref: (0, j)), whole),
            scratch_shapes=[pltpu.VMEM((N_DEV,) + small.shape, F32), pltpu.SemaphoreType.DMA((7,)),
                            pltpu.SemaphoreType.DMA((7,))]),
        compiler_params=_cparams("arbitrary"),
    )(chip, sums, from_chips, small)


def _adamw_rows(g, w, m, v, tr, name):
    rows = g.shape[0]

    def body(g_ref, w_ref, m_ref, v_ref, d_out, m_out, v_out):
        delta, m_new, v_new = _adamw(w_ref[...], g_ref[...], m_ref[...], v_ref[...])
        d_out[...] = delta
        m_out[...] = m_new
        v_out[...] = v_new

    tile = pl.BlockSpec((tr,) + g.shape[1:], lambda r: (r, 0, 0))
    shp = jax.ShapeDtypeStruct(g.shape, F32)
    return pl.pallas_call(
        body, name=name, out_shape=(shp, shp, shp), grid=(rows // tr,),
        in_specs=[tile] * 4, out_specs=(tile, tile, tile),
        compiler_params=_cparams("arbitrary"),
    )(g, w, m, v)


def _adamw(w, g, m, v):
    m = ADAM_B1 * m + (1.0 - ADAM_B1) * g
    v = ADAM_B2 * v + (1.0 - ADAM_B2) * (g * g)
    m_hat = m / (1.0 - ADAM_B1 ** ADAM_STEP)
    v_hat = v / (1.0 - ADAM_B2 ** ADAM_STEP)
    delta = -ADAM_LR * (m_hat / (jnp.sqrt(v_hat) + ADAM_EPS) + ADAM_WD * w)
    return delta, m, v


def _final_sum_adamw(sums, from_chips, chip, w, m, v, tr, name):
    rows, cols = w.shape

    def body(chip_ref, s_ref, r_ref, w_ref, m_ref, v_ref, g_out, d_out, m_out, v_out):
        g = _sum_chips(s_ref, r_ref)
        delta, m_new, v_new = _adamw(w_ref[...], g, m_ref[...], v_ref[...])
        g_out[...] = g
        d_out[...] = delta
        m_out[...] = m_new
        v_out[...] = v_new

    tile = pl.BlockSpec((tr, cols), lambda r, chip_ref: (r, 0))
    shp = jax.ShapeDtypeStruct((rows, cols), F32)
    return pl.pallas_call(
        body, name=name,
        out_shape=(shp, shp, shp, shp),
        grid_spec=pltpu.PrefetchScalarGridSpec(
            num_scalar_prefetch=1, grid=(rows // tr,),
            in_specs=[pl.BlockSpec((1, tr, cols), lambda r, chip_ref: (chip_ref[0], r, 0)),
                      pl.BlockSpec((3, tr, cols), lambda r, chip_ref: (0, r, 0)),
                      tile, tile, tile],
            out_specs=(tile, tile, tile, tile)),
        compiler_params=_cparams("arbitrary"),
    )(chip, sums, from_chips, w, m, v)


def _adamw_small(g, w, m, v):
    def body(g_ref, w_ref, m_ref, v_ref, d_out, m_out, v_out):
        delta, m_new, v_new = _adamw(w_ref[...], g_ref[...], m_ref[...], v_ref[...])
        d_out[...] = delta
        m_out[...] = m_new
        v_out[...] = v_new

    vmem = pl.BlockSpec(memory_space=pltpu.VMEM)
    shp = jax.ShapeDtypeStruct(g.shape, F32)
    return pl.pallas_call(body, name="adamw_small", out_shape=(shp, shp, shp),
                          in_specs=[vmem] * 4, out_specs=(vmem, vmem, vmem))(g, w, m, v)


TILE_ROWS = (0, 1024, NAT_ZA, NAT_B, NAT_ZC, NAT_C, NAT_C + CONV_W)


def _inproj(x2d, norm_g, w_nat, w_out_s, small_s, tm):
    seq = x2d.shape[0]
    tn = CONV_W
    ni, nj = seq // tm, MAIN_W // tn
    first_sweep = lambda j, i: jnp.where(j == 0, i, ni - 1)

    def tile_row(j, i):
        row = 0
        for k, start in enumerate(TILE_ROWS):
            row = row + jnp.where(j == k, start // 32, 0)
        return pl.multiple_of(row * 32, 32), 0

    def body(x_ref, g_ref, w_ref, wlr_ref, wout_ref, sm_ref, proj_ref, lr_ref, ht_ref, wout_all, sm_all,
             h_all, wout_b, sm_b, send_sems, recv_sems, local_sems):
        j, i = pl.program_id(0), pl.program_id(1)
        rows = pl.ds(pl.multiple_of(i * tm, tm), tm)
        me = _blk(*_position())

        def gather_copies():
            mine = [pltpu.make_async_copy(wout_b, wout_all.at[me], local_sems.at[0]),
                    pltpu.make_async_copy(sm_b, sm_all.at[me], local_sems.at[1])]
            return mine, _peer_copies((wout_b, sm_b), (wout_all, sm_all), send_sems, recv_sems)

        @pl.when(jnp.logical_and(j == 0, i == 0))
        def _():
            wout_b[...] = wout_ref[...].astype(BF16)
            sm_b[...] = sm_ref[...]
            mine, remote = gather_copies()
            for cp in mine + remote:
                cp.start()

        @pl.when(j == 0)
        def _():
            xv = x_ref[...]
            r = lax.rsqrt(jnp.mean(xv * xv, axis=-1, keepdims=True) + EPS)
            h = (xv * r) * g_ref[...]
            hb = h.astype(BF16)
            h_all[rows, :] = hb
            ht_ref[...] = h.T.astype(BF16)
            lr_ref[...] = _dot_nt(hb, wlr_ref[...])

        proj_ref[...] = _dot_nt(h_all[rows, :], w_ref[...]).astype(BF16)

        @pl.when(jnp.logical_and(j == nj - 1, i == ni - 1))
        def _():
            mine, remote = gather_copies()
            for cp in remote:
                cp.wait_recv()
            for cp in remote:
                cp.wait_send()
            for cp in mine:
                cp.wait()

    const = lambda shape: pl.BlockSpec(shape, lambda j, i: (0,) * len(shape))
    hbm = pl.BlockSpec(memory_space=pl.ANY)
    return pl.pallas_call(
        body, name="inproj",
        out_shape=(jax.ShapeDtypeStruct((seq, MAIN_W), BF16), jax.ShapeDtypeStruct((seq, LR_W), F32),
                   jax.ShapeDtypeStruct((D_MODEL, seq), BF16),
                   jax.ShapeDtypeStruct((N_DEV,) + w_out_s.shape, BF16),
                   jax.ShapeDtypeStruct((N_DEV,) + small_s.shape, F32)),
        grid=(nj, ni),
        in_specs=[pl.BlockSpec((tm, D_MODEL), lambda j, i: (first_sweep(j, i), 0)),
                  const((1, D_MODEL)),
                  pl.BlockSpec((pl.Element(tn), pl.Element(D_MODEL)), tile_row),
                  pl.BlockSpec((pl.Element(LR_W), pl.Element(D_MODEL)), lambda j, i: (NAT_LR, 0)),
                  const(w_out_s.shape), const(small_s.shape)],
        out_specs=(pl.BlockSpec((tm, tn), lambda j, i: (i, j)),
                   pl.BlockSpec((tm, LR_W), lambda j, i: (first_sweep(j, i), 0)),
                   pl.BlockSpec((D_MODEL, tm), lambda j, i: (0, first_sweep(j, i))), hbm, hbm),
        scratch_shapes=[pltpu.VMEM((seq, D_MODEL), BF16), pltpu.VMEM(w_out_s.shape, BF16),
                        pltpu.VMEM(small_s.shape, F32), pltpu.SemaphoreType.DMA((14,)),
                        pltpu.SemaphoreType.DMA((14,)), pltpu.SemaphoreType.DMA((2,))],
        compiler_params=_cparams("arbitrary", "arbitrary"),
    )(x2d, norm_g, w_nat, w_nat, w_out_s, small_s)


def _block_masks(tt):
    row = lax.broadcasted_iota(jnp.int32, (tt, tt), 0)
    col = lax.broadcasted_iota(jnp.int32, (tt, tt), 1)
    same = jnp.right_shift(row, 6) == jnp.right_shift(col, 6)
    return (jnp.logical_and(same, col <= row), jnp.logical_and(same, col >= row), jnp.logical_and(same, col > row))


def _dot_split3(ones_mat, x):
    x1 = x.astype(BF16)
    r1 = x - x1.astype(F32)
    x2 = r1.astype(BF16)
    x3 = (r1 - x2.astype(F32)).astype(BF16)
    return (_dot(ones_mat, x3) + _dot(ones_mat, x2)) + _dot(ones_mat, x1)


def _log_gate(logits):
    return (jnp.minimum(logits, 0.0) - jnp.log(1.0 + jnp.exp(-jnp.abs(logits)))) * GATE_SCALE


def _chunk_column_mask(tt):
    nc = tt // CHUNK
    row = lax.broadcasted_iota(jnp.int32, (tt, nc * DK), 0)
    col = lax.broadcasted_iota(jnp.int32, (tt, nc * DK), 1)
    return jnp.right_shift(row, 6) == jnp.right_shift(col, 7)


def _chunked(mask, x, nc):
    wide = jnp.concatenate([x] * nc, axis=1)
    return jnp.where(mask, wide, jnp.zeros_like(wide))


def _gla_fwd(proj, lr, wgk_f, wgk_b, bgk_f, bgk_b, tt):
    seq = proj.shape[0]
    nb, nc, nch = seq // tt, tt // CHUNK, seq // CHUNK

    def body(qf, kf, vf, lrf, qb, kb, vb, lrb, wf, wb, bf, bb, of, ob, stf, stb, s_scr, qs_s, ks_s, qin_s, kout_s):
        @pl.when(pl.program_id(0) == 0)
        def _():
            s_scr[...] = jnp.zeros(s_scr.shape, F32)

        low, upp, sup = _block_masks(tt)
        dirs = ((qf, kf, vf, lrf, wf, bf, of, stf, low, low, REF_F, LAST_F, list(range(nc))),
                (qb, kb, vb, lrb, wb, bb, ob, stb, upp, sup, REF_B, LAST_B, list(reversed(range(nc)))))
        for d, (q_r, k_r, v_r, lr_r, w_r, b_r, o_r, st_r, cum, mask, ref, last, order) in enumerate(dirs):
            logits = _dot(lr_r[...].astype(BF16), w_r[...]) + b_r[...]
            b = _dot_split3(cum.astype(BF16), _log_gate(logits))
            decs = []
            for c in range(nc):
                rows = slice(c * CHUNK, (c + 1) * CHUNK)
                bc = b[rows]
                b_ref, b_last = bc[ref:ref + 1], bc[last:last + 1]
                qc = q_r[rows, :].astype(F32) * QSCALE
                kc = k_r[rows, :].astype(F32)
                qs_s[rows, :] = (qc * jnp.exp(bc - b_ref)).astype(BF16)
                ks_s[rows, :] = (kc * jnp.exp(b_ref - bc)).astype(BF16)
                qin_s[rows, :] = (qc * jnp.exp(bc)).astype(BF16)
                kout_s[rows, :] = (kc * jnp.exp(b_last - bc)).astype(BF16)
                decs.append(jnp.exp(b_last))
            for h in range(HEADS):
                ksl = slice(h * DK, (h + 1) * DK)
                vsl = slice(h * DV, (h + 1) * DV)
                v = v_r[:, vsl].astype(BF16)
                att = jnp.where(mask, _dot_nt(qs_s[:, ksl], ks_s[:, ksl]), 0.0).astype(BF16)
                o_intra = _dot(att, v)
                st = s_scr[d * HEADS + h]
                for c in order:
                    rows = slice(c * CHUNK, (c + 1) * CHUNK)
                    stb = st.astype(BF16)
                    st_r[c, h] = stb
                    o_r[rows, vsl] = (o_intra[rows] + _dot_nt(qin_s[rows, ksl], stb)).astype(BF16)
                    st = st * decs[c][:, ksl] + _dot_tn(v[rows], kout_s[rows, ksl])
                s_scr[d * HEADS + h] = st

    fw = lambda i: (i, 0)
    bw = lambda i: (nb - 1 - i, 0)
    const = lambda i: (0, 0)

    def tok_specs(m):
        return [pl.BlockSpec((tt, QK_W), lambda i: (m(i)[0], OFF_Q // QK_W)),
                pl.BlockSpec((tt, QK_W), lambda i: (m(i)[0], OFF_K // QK_W)),
                pl.BlockSpec((tt, V_W), lambda i: (m(i)[0], OFF_V // V_W)),
                pl.BlockSpec((tt, LR_W), m)]

    st_shape = jax.ShapeDtypeStruct((nch, HEADS, DV, DK), BF16)
    o_shape = jax.ShapeDtypeStruct((seq, V_W), BF16)
    operand = pltpu.VMEM((tt, QK_W), BF16)
    return pl.pallas_call(
        body, name="gla_fwd",
        out_shape=(o_shape, o_shape, st_shape, st_shape),
        grid=(nb,),
        in_specs=tok_specs(fw) + tok_specs(bw) + [
            pl.BlockSpec((LR_W, QK_W), const), pl.BlockSpec((LR_W, QK_W), const),
            pl.BlockSpec((1, QK_W), const), pl.BlockSpec((1, QK_W), const)],
        out_specs=(pl.BlockSpec((tt, V_W), fw), pl.BlockSpec((tt, V_W), bw),
                   pl.BlockSpec((nc, HEADS, DV, DK), lambda i: (i, 0, 0, 0)),
                   pl.BlockSpec((nc, HEADS, DV, DK), lambda i: (nb - 1 - i, 0, 0, 0))),
        scratch_shapes=[pltpu.VMEM((2 * HEADS, DV, DK), F32), operand, operand, operand, operand],
        compiler_params=_cparams("arbitrary"),
    )(proj, proj, proj, lr, proj, proj, proj, lr, wgk_f, wgk_b, bgk_f, bgk_b)


def _head_norm(o, gain):
    outs, rinv = [], []
    for h in range(HEADS):
        oh = o[:, h * DV:(h + 1) * DV]
        r = lax.rsqrt(jnp.mean(oh * oh, axis=-1, keepdims=True) + EPS)
        outs.append((oh * r) * gain)
        rinv.append(r)
    return jnp.concatenate(outs, axis=1), rinv


def _shift_rows(u, prev_row, next_row):
    n = u.shape[0]
    row = lax.broadcasted_iota(jnp.int32, (n, 1), 0)
    up = jnp.where(row == 0, prev_row, pltpu.roll(u, 1, 0))
    un = jnp.where(row == n - 1, next_row, pltpu.roll(u, n - 1, 0))
    return up, un


HALO = 16


def _halo_specs(tm, seq, col_block):
    per = tm // HALO
    last = seq // HALO - 1
    return [pl.BlockSpec((HALO, CONV_W), lambda i: (jnp.maximum(i * per - 1, 0), col_block)),
            pl.BlockSpec((HALO, CONV_W), lambda i: (jnp.minimum((i + 1) * per, last), col_block))]


def _f32(ref):
    return ref[...].astype(F32)


def _last_row(ref):
    return ref[HALO - 1:HALO, :].astype(F32)


def _first_row(ref):
    return ref[0:1, :].astype(F32)


def _mix_out_loss(o_f, o_b, proj, x2d, tgt, gla_g, conv_w, conv_b, w_out, final_g, tm):
    seq = x2d.shape[0]
    nt = seq // tm

    def body(of, ob, za, bg, cg, hc, zc, cprev, cnext, hprev, hnext, x_ref, t_ref, gg, cw, cb, wo, fg,
             yt_ref, conv_ref, dx2_ref, dx2b_ref, loss_ref, dfg_ref):
        i = pl.program_id(0)

        @pl.when(i == 0)
        def _():
            loss_ref[...] = jnp.zeros(loss_ref.shape, F32)
            dfg_ref[...] = jnp.zeros(dfg_ref.shape, F32)

        on, _ = _head_norm(_f32(of) + _f32(ob), gg[...])
        zav = _f32(za)
        y_a = on * (zav * _sigmoid(zav))
        u = _f32(cg) * _f32(hc)
        prev_row = jnp.where(i > 0, _last_row(cprev) * _last_row(hprev), 0.0)
        next_row = jnp.where(i < nt - 1, _first_row(cnext) * _first_row(hnext), 0.0)
        up, un = _shift_rows(u, prev_row, next_row)
        conv = (cw[0:1, :] * up + cw[1:2, :] * u + cw[2:3, :] * un) + cb[...]
        conv_ref[...] = conv.astype(BF16)
        zcv = _f32(zc)
        y_c = _f32(bg) * conv * (zcv * _sigmoid(zcv))
        y = jnp.concatenate([y_a, y_c], axis=1)
        yt_ref[...] = y.T.astype(BF16)
        x2 = x_ref[...] + _dot(y.astype(BF16), wo[...])
        r = lax.rsqrt(jnp.mean(x2 * x2, axis=-1, keepdims=True) + EPS)
        xn = x2 * r
        err = xn * fg[...] - t_ref[...]
        loss_ref[...] += 0.5 * jnp.sum(jnp.mean(err * err, axis=-1, keepdims=True))
        dyf = err * (1.0 / D_MODEL)
        dfg_ref[...] += jnp.sum(dyf * xn, axis=0, keepdims=True)
        dxn = dyf * fg[...]
        dx2 = r * dxn - xn * (r * jnp.mean(dxn * xn, axis=-1, keepdims=True))
        dx2_ref[...] = dx2
        dx2b_ref[...] = dx2.astype(BF16)

    def col(off):
        return pl.BlockSpec((tm, CONV_W), lambda i: (i, off // CONV_W))

    rowt = pl.BlockSpec((tm, D_MODEL), lambda i: (i, 0))
    const = lambda shape: pl.BlockSpec(shape, lambda i: (0, 0))
    return pl.pallas_call(
        body, name="mix_out_loss",
        out_shape=(jax.ShapeDtypeStruct((MIX_W, seq), BF16), jax.ShapeDtypeStruct((seq, CONV_W), BF16),
                   jax.ShapeDtypeStruct((seq, D_MODEL), F32), jax.ShapeDtypeStruct((seq, D_MODEL), BF16),
                   jax.ShapeDtypeStruct((8, 128), F32), jax.ShapeDtypeStruct((1, D_MODEL), F32)),
        grid=(nt,),
        in_specs=[rowt, rowt, col(OFF_ZA), col(OFF_B), col(OFF_C), col(OFF_H), col(OFF_ZC)]
        + _halo_specs(tm, seq, OFF_C // CONV_W) + _halo_specs(tm, seq, OFF_H // CONV_W)
        + [rowt, rowt, const((1, DV)), const((8, CONV_W)), const((1, CONV_W)), const((MIX_W, D_MODEL)),
           const((1, D_MODEL))],
        out_specs=(pl.BlockSpec((MIX_W, tm), lambda i: (0, i)), rowt, rowt, rowt, const((8, 128)),
                   const((1, D_MODEL))),
        compiler_params=_cparams("arbitrary"),
    )(o_f, o_b, proj, proj, proj, proj, proj, proj, proj, proj, proj, x2d, tgt, gla_g, conv_w, conv_b, w_out, final_g)


def _dsilu(z, s):
    return s * (1.0 + z * (1.0 - s))


def _mix_bwd(dx2b, o_f, o_b, proj, conv, gla_g, w_out, tm):
    seq = dx2b.shape[0]

    def body(dx, of, ob, za, bg, zc, cv, gg, wo, dg_ref, do_ref, dconv_ref, dgg_ref, dcb_ref):
        @pl.when(pl.program_id(0) == 0)
        def _():
            dgg_ref[...] = jnp.zeros(dgg_ref.shape, F32)
            dcb_ref[...] = jnp.zeros(dcb_ref.shape, F32)

        dy = _dot_nt(dx[...], wo[...])
        dy_a, dy_c = dy[:, :V_W], dy[:, V_W:]
        zcv, bgv, convv = _f32(zc), _f32(bg), _f32(cv)
        sc = _sigmoid(zcv)
        szc = zcv * sc
        dg_ref[:, CONV_W:2 * CONV_W] = (dy_c * convv * szc).astype(BF16)
        dconv = dy_c * bgv * szc
        dconv_ref[...] = dconv.astype(BF16)
        dcb_ref[...] += jnp.sum(dconv, axis=0, keepdims=True)
        dg_ref[:, 2 * CONV_W:] = (dy_c * bgv * convv * _dsilu(zcv, sc)).astype(BF16)

        o = _f32(of) + _f32(ob)
        gain = gg[...]
        on, rinv = _head_norm(o, gain)
        zav = _f32(za)
        sa = _sigmoid(zav)
        dg_ref[:, :CONV_W] = (dy_a * on * _dsilu(zav, sa)).astype(BF16)
        don = dy_a * (zav * sa)
        dgg = jnp.zeros((1, DV), F32)
        dos = []
        for h in range(HEADS):
            sl = slice(h * DV, (h + 1) * DV)
            oh, r, dh = o[:, sl], rinv[h], don[:, sl]
            ohn = oh * r
            dgg = dgg + jnp.sum(dh * ohn, axis=0, keepdims=True)
            dn = dh * gain
            dos.append(r * dn - ohn * (r * jnp.mean(dn * ohn, axis=-1, keepdims=True)))
        dgg_ref[...] += dgg
        do_ref[...] = jnp.concatenate(dos, axis=1).astype(BF16)

    def col(off):
        return pl.BlockSpec((tm, CONV_W), lambda i: (i, off // CONV_W))

    rowt = pl.BlockSpec((tm, D_MODEL), lambda i: (i, 0))
    const = lambda shape: pl.BlockSpec(shape, lambda i: (0, 0))
    return pl.pallas_call(
        body, name="mix_bwd",
        out_shape=(jax.ShapeDtypeStruct((seq, GATES_W), BF16), jax.ShapeDtypeStruct((seq, V_W), BF16),
                   jax.ShapeDtypeStruct((seq, CONV_W), BF16),
                   jax.ShapeDtypeStruct((1, DV), F32), jax.ShapeDtypeStruct((1, CONV_W), F32)),
        grid=(seq // tm,),
        in_specs=[rowt, rowt, rowt, col(OFF_ZA), col(OFF_B), col(OFF_ZC), rowt, const((1, DV)),
                  const((MIX_W, D_MODEL))],
        out_specs=(pl.BlockSpec((tm, GATES_W), lambda i: (i, 0)), rowt, rowt, const((1, DV)), const((1, CONV_W))),
        compiler_params=_cparams("arbitrary"),
    )(dx2b, o_f, o_b, proj, proj, proj, conv, gla_g, w_out)


def _conv_bwd(dconv, proj, conv_w, tm):
    seq = dconv.shape[0]
    nt = seq // tm

    def body(dc_in, dprev, dnext, cg, hc, cprev, cnext, hprev, hnext, cw, dch_ref, dcw_ref):
        i = pl.program_id(0)

        @pl.when(i == 0)
        def _():
            dcw_ref[...] = jnp.zeros(dcw_ref.shape, F32)

        first, lastt = i > 0, i < nt - 1
        dcv = _f32(dc_in)
        d_up, d_un = _shift_rows(dcv, jnp.where(first, _last_row(dprev), 0.0), jnp.where(lastt, _first_row(dnext), 0.0))
        cgv, hcv = _f32(cg), _f32(hc)
        u = cgv * hcv
        u_up, u_un = _shift_rows(u, jnp.where(first, _last_row(cprev) * _last_row(hprev), 0.0),
                                 jnp.where(lastt, _first_row(cnext) * _first_row(hnext), 0.0))
        du = cw[0:1, :] * d_un + cw[1:2, :] * dcv + cw[2:3, :] * d_up
        dch_ref[:, :CONV_W] = (du * hcv).astype(BF16)
        dch_ref[:, CONV_W:] = (du * cgv).astype(BF16)
        dcw_ref[0:1, :] += jnp.sum(dcv * u_up, axis=0, keepdims=True)
        dcw_ref[1:2, :] += jnp.sum(dcv * u, axis=0, keepdims=True)
        dcw_ref[2:3, :] += jnp.sum(dcv * u_un, axis=0, keepdims=True)

    def col(off):
        return pl.BlockSpec((tm, CONV_W), lambda i: (i, off // CONV_W))

    rowt = pl.BlockSpec((tm, CONV_W), lambda i: (i, 0))
    const = lambda shape: pl.BlockSpec(shape, lambda i: (0, 0))
    return pl.pallas_call(
        body, name="conv_bwd",
        out_shape=(jax.ShapeDtypeStruct((seq, CH_W), BF16), jax.ShapeDtypeStruct((8, CONV_W), F32)),
        grid=(nt,),
        in_specs=[rowt] + _halo_specs(tm, seq, 0) + [col(OFF_C), col(OFF_H)]
        + _halo_specs(tm, seq, OFF_C // CONV_W) + _halo_specs(tm, seq, OFF_H // CONV_W) + [const((8, CONV_W))],
        out_specs=(pl.BlockSpec((tm, CH_W), lambda i: (i, 0)), const((8, CONV_W))),
        compiler_params=_cparams("arbitrary"),
    )(dconv, dconv, dconv, proj, proj, proj, proj, proj, proj, conv_w)


def _gla_bwd(proj, lr, do, st_f, st_b, wgk_f, wgk_b, bgk_f, bgk_b, tt):
    seq = proj.shape[0]
    nb, nc = seq // tt, tt // CHUNK

    def body(qf, kf, vf, lrf, dof, stf, qb, kb, vb, lrb, dob, stb, wf, wb, bf, bb,
             dqkv_f, dlr_f, dqkv_b, dlr_b, dwf, dwb, dbf, dbb,
             ds_scr, eq_s, ek_s, ein_s, eout_s, qs_s, ks_s, qin_s, kout_s, db_s, lg_s):
        @pl.when(pl.program_id(0) == 0)
        def _():
            ds_scr[...] = jnp.zeros(ds_scr.shape, F32)
            for r in (dwf, dwb, dbf, dbb):
                r[...] = jnp.zeros(r.shape, F32)

        low, upp, sup = _block_masks(tt)
        row = lax.broadcasted_iota(jnp.int32, (CHUNK, 1), 0)
        kmask = _chunk_column_mask(tt)
        dirs = ((qf, kf, vf, lrf, dof, stf, wf, bf, dqkv_f, dlr_f, dwf, dbf,
                 low, upp, low, REF_F, LAST_F, list(reversed(range(nc)))),
                (qb, kb, vb, lrb, dob, stb, wb, bb, dqkv_b, dlr_b, dwb, dbb,
                 upp, low, sup, REF_B, LAST_B, list(range(nc))))
        for d, (q_r, k_r, v_r, lr_r, do_r, st_r, w_r, b_r, dqkv_r, dlr_r, dw_r, db_r,
                cum, cum_t, mask, ref, last, order) in enumerate(dirs):
            lrv = lr_r[...].astype(BF16)
            wv = w_r[...]
            logits = _dot(lrv, wv) + b_r[...]
            lg_s[...] = logits
            b = _dot_split3(cum.astype(BF16), _log_gate(logits))
            decs = []
            for c in range(nc):
                rows = slice(c * CHUNK, (c + 1) * CHUNK)
                bc = b[rows]
                b_ref, b_last = bc[ref:ref + 1], bc[last:last + 1]
                qc = q_r[rows, :].astype(F32) * QSCALE
                kc = k_r[rows, :].astype(F32)
                e_q, e_k, e_in, e_out = jnp.exp(bc - b_ref), jnp.exp(b_ref - bc), jnp.exp(bc), jnp.exp(b_last - bc)
                eq_s[rows, :], ek_s[rows, :], ein_s[rows, :], eout_s[rows, :] = e_q, e_k, e_in, e_out
                qs_s[rows, :] = (qc * e_q).astype(BF16)
                ks_s[rows, :] = (kc * e_k).astype(BF16)
                qin_s[rows, :] = (qc * e_in).astype(BF16)
                kout_s[rows, :] = (kc * e_out).astype(BF16)
                decs.append(jnp.exp(b_last))
            for h in range(HEADS):
                ksl = slice(h * DK, (h + 1) * DK)
                vsl = slice(h * DV, (h + 1) * DV)
                v = v_r[:, vsl].astype(BF16)
                dov = do_r[:, vsl].astype(BF16)
                qsb, ksb = qs_s[:, ksl], ks_s[:, ksl]
                att = jnp.where(mask, _dot_nt(qsb, ksb), 0.0).astype(BF16)
                datt = jnp.where(mask, _dot_nt(dov, v), 0.0).astype(BF16)
                dqs = _dot(datt, ksb)
                dks = _dot_tn(datt, qsb)
                dv_intra = _dot_tn(att, dov)
                g_t = _dot_tn(dov, _chunked(kmask, qin_s[:, ksl], nc))
                ds = ds_scr[d * HEADS + h]
                for c in order:
                    rows = slice(c * CHUNK, (c + 1) * CHUNK)
                    dsb = ds.astype(BF16)
                    s_prev = st_r[c, h]
                    dk_out = _dot(v[rows], dsb)
                    dq_in = _dot(dov[rows], s_prev)
                    dv = dv_intra[rows] + _dot_nt(kout_s[rows, ksl], dsb)
                    dqkv_r[rows, OFF_V + h * DV:OFF_V + (h + 1) * DV] = dv.astype(BF16)
                    dec = decs[c][:, ksl]
                    ddec = jnp.sum(ds * s_prev.astype(F32), axis=0, keepdims=True)
                    e_out = eout_s[rows, ksl]
                    qc = q_r[rows, ksl].astype(F32) * QSCALE
                    kc = k_r[rows, ksl].astype(F32)
                    dq = dqs[rows] * eq_s[rows, ksl] + dq_in * ein_s[rows, ksl]
                    dk = dks[rows] * ek_s[rows, ksl] + dk_out * e_out
                    dqkv_r[rows, OFF_Q + h * DK:OFF_Q + (h + 1) * DK] = (dq * QSCALE).astype(BF16)
                    dqkv_r[rows, OFF_K + h * DK:OFF_K + (h + 1) * DK] = dk.astype(BF16)
                    tail = jnp.sum(dk_out * (kc * e_out), axis=0, keepdims=True) + ddec * dec
                    db_s[rows, ksl] = (qc * dq - kc * dk) + jnp.where(row == last, tail, 0.0)
                    ds = ds * dec + g_t[:, c * DK:(c + 1) * DK]
                ds_scr[d * HEADS + h] = ds
            dg = _dot_split3(cum_t.astype(BF16), db_s[...])
            dlogit = (dg * GATE_SCALE) * _sigmoid(-lg_s[...])
            dlb = dlogit.astype(BF16)
            dlr_r[...] = _dot_nt(dlb, wv)
            dw_r[...] += _dot_tn(lrv, dlb)
            db_r[...] += jnp.sum(dlogit, axis=0, keepdims=True)

    fw = lambda i: (nb - 1 - i, 0)
    bw = lambda i: (i, 0)
    const = lambda i: (0, 0)

    def tok_specs(m):
        return [pl.BlockSpec((tt, QK_W), lambda i: (m(i)[0], OFF_Q // QK_W)),
                pl.BlockSpec((tt, QK_W), lambda i: (m(i)[0], OFF_K // QK_W)),
                pl.BlockSpec((tt, V_W), lambda i: (m(i)[0], OFF_V // V_W)),
                pl.BlockSpec((tt, LR_W), m),
                pl.BlockSpec((tt, V_W), m),
                pl.BlockSpec((nc, HEADS, DV, DK), lambda i: (m(i)[0], 0, 0, 0))]

    dqkv = jax.ShapeDtypeStruct((seq, QK_W + QK_W + V_W), BF16)
    dlr = jax.ShapeDtypeStruct((seq, LR_W), F32)
    dw = jax.ShapeDtypeStruct((LR_W, QK_W), F32)
    dbias = jax.ShapeDtypeStruct((1, QK_W), F32)
    return pl.pallas_call(
        body, name="gla_bwd",
        out_shape=(dqkv, dlr, dqkv, dlr, dw, dw, dbias, dbias),
        grid=(nb,),
        in_specs=tok_specs(fw) + tok_specs(bw) + [
            pl.BlockSpec((LR_W, QK_W), const), pl.BlockSpec((LR_W, QK_W), const),
            pl.BlockSpec((1, QK_W), const), pl.BlockSpec((1, QK_W), const)],
        out_specs=(pl.BlockSpec((tt, QK_W + QK_W + V_W), fw), pl.BlockSpec((tt, LR_W), fw),
                   pl.BlockSpec((tt, QK_W + QK_W + V_W), bw), pl.BlockSpec((tt, LR_W), bw),
                   pl.BlockSpec((LR_W, QK_W), const), pl.BlockSpec((LR_W, QK_W), const),
                   pl.BlockSpec((1, QK_W), const), pl.BlockSpec((1, QK_W), const)),
        scratch_shapes=[pltpu.VMEM((2 * HEADS, DV, DK), F32)] + [pltpu.VMEM((tt, QK_W), F32)] * 4
        + [pltpu.VMEM((tt, QK_W), BF16)] * 4 + [pltpu.VMEM((tt, QK_W), F32)] * 2,
        compiler_params=_cparams("arbitrary"),
    )(proj, proj, proj, lr, do, st_f, proj, proj, proj, lr, do, st_b, wgk_f, wgk_b, bgk_f, bgk_b)


def _sum_directions(dqkv_f, dqkv_b, dlr_f, dlr_b, tm):
    seq = dqkv_f.shape[0]

    def body(a, b, la, lb, dp_out, dlr_out):
        dp_out[...] = (_f32(a) + _f32(b)).astype(BF16)
        dlr_out[...] = (la[...] + lb[...]).astype(BF16)

    rowt = pl.BlockSpec((tm, QKV_W), lambda i: (i, 0))
    lrt = pl.BlockSpec((tm, LR_W), lambda i: (i, 0))
    return pl.pallas_call(
        body, name="sum_directions",
        out_shape=(jax.ShapeDtypeStruct((seq, QKV_W), BF16), jax.ShapeDtypeStruct((seq, LR_W), BF16)),
        grid=(seq // tm,),
        in_specs=[rowt, rowt, lrt, lrt],
        out_specs=(rowt, lrt),
        compiler_params=_cparams("arbitrary"),
    )(dqkv_f, dqkv_b, dlr_f, dlr_b)


def _input_grad(dp_qkv, dp_gates, dp_ch, dlr, w_nat, x2d, norm_g, dx2, sums, tm):
    seq = x2d.shape[0]
    nt, n = seq // tm, len(sums)

    def body(dq, dg, dc, dl, w, x_ref, g_ref, dx2_ref, *rest):
        ins, (gx_ref, dng_ref), outs = rest[:n], rest[n:n + 2], rest[n + 2:2 * n + 2]
        send_sems, recv_sems = rest[2 * n + 2:]
        i = pl.program_id(0)

        @pl.when(i == 0)
        def _():
            for cp in _chip_copies(ins, outs, send_sems, recv_sems):
                cp.start()
            dng_ref[...] = jnp.zeros(dng_ref.shape, F32)

        dh = (_dot(dl[...], w[NAT_LR:NAT_LR + LR_W, :]) + _dot(dq[...], w[0:NAT_ZA, :])
              + _dot(dg[:, 0:CONV_W], w[NAT_ZA:NAT_LR, :]) + _dot(dg[:, CONV_W:2 * CONV_W], w[NAT_B:NAT_C, :])
              + _dot(dg[:, 2 * CONV_W:], w[NAT_ZC:IN_W, :]) + _dot(dc[...], w[NAT_C:NAT_ZC, :]))
        xv = x_ref[...]
        r = lax.rsqrt(jnp.mean(xv * xv, axis=-1, keepdims=True) + EPS)
        xn = xv * r
        dng_ref[...] += jnp.sum(dh * xn, axis=0, keepdims=True)
        dn = dh * g_ref[...]
        gx_ref[...] = (r * dn - xn * (r * jnp.mean(dn * xn, axis=-1, keepdims=True))) + dx2_ref[...]

        @pl.when(i == nt - 1)
        def _():
            copies = _chip_copies(ins, outs, send_sems, recv_sems)
            for cp in copies:
                cp.wait_recv()
            for cp in copies:
                cp.wait_send()

    rowt = pl.BlockSpec((tm, D_MODEL), lambda i: (i, 0))
    seg = lambda width: pl.BlockSpec((tm, width), lambda i: (i, 0))
    resident = lambda rows: pl.BlockSpec((rows, D_MODEL), lambda i: (0, 0), pipeline_mode=pl.Buffered(1))
    hbm = pl.BlockSpec(memory_space=pl.ANY)
    return pl.pallas_call(
        body, name="input_grad",
        out_shape=(jax.ShapeDtypeStruct((seq, D_MODEL), F32), jax.ShapeDtypeStruct((1, D_MODEL), F32))
        + tuple(jax.ShapeDtypeStruct((3,) + s.shape[1:], s.dtype) for s in sums),
        grid=(nt,),
        in_specs=[seg(QKV_W), seg(GATES_W), seg(CH_W), seg(LR_W), resident(IN_W),
                  rowt, pl.BlockSpec((1, D_MODEL), lambda i: (0, 0)), rowt] + [hbm] * n,
        out_specs=(rowt, pl.BlockSpec((1, D_MODEL), lambda i: (0, 0))) + (hbm,) * n,
        scratch_shapes=[pltpu.SemaphoreType.DMA((3 * n,)), pltpu.SemaphoreType.DMA((3 * n,))],
        compiler_params=_cparams("arbitrary"),
    )(dp_qkv, dp_gates, dp_ch, dlr, w_nat, x2d, norm_g, dx2, *sums)


def _weight_grad_out(y_t, dx2b, tk, riding):
    m, seq = y_t.shape
    n = dx2b.shape[1]
    nk = seq // tk

    def body(a_ref, b_ref, ride_in, o_ref, ride_out, send_sems, recv_sems):
        k = pl.program_id(0)

        @pl.when(k == 0)
        def _():
            _start_all(_sibling_copies(ride_in, ride_out, send_sems, recv_sems))
            o_ref[...] = jnp.zeros(o_ref.shape, F32)

        o_ref[...] += _dot(a_ref[...], b_ref[...])

        @pl.when(k == nk - 1)
        def _():
            _wait_all(_sibling_copies(ride_in, ride_out, send_sems, recv_sems))

    hbm = pl.BlockSpec(memory_space=pl.ANY)
    return pl.pallas_call(
        body, name="wgrad_out",
        out_shape=(jax.ShapeDtypeStruct((m, n), F32), jax.ShapeDtypeStruct((4,) + _block_shape(riding), F32)),
        grid=(nk,),
        in_specs=[pl.BlockSpec((m, tk), lambda k: (0, k)), pl.BlockSpec((tk, n), lambda k: (k, 0)), hbm],
        out_specs=(pl.BlockSpec((m, n), lambda k: (0, 0)), hbm),
        scratch_shapes=[pltpu.SemaphoreType.DMA((4,)), pltpu.SemaphoreType.DMA((4,))],
        compiler_params=_cparams("arbitrary"),
    )(y_t, dx2b, riding)


def _weight_grad_in(h_t, dp_qkv, dp_gates, dp_ch, dlr):
    m, seq = h_t.shape
    tn = 512
    n_qkv, n_gates, n_ch = QKV_W // tn, GATES_W // tn, CH_W // tn
    starts = ([k * tn for k in range(n_qkv)] + [NAT_ZA, NAT_ZA + tn, NAT_B, NAT_B + tn, NAT_ZC, NAT_ZC + tn]
              + [NAT_C + k * tn for k in range(n_ch)])

    def out_row(j):
        row = 0
        for k, start in enumerate(starts):
            row = row + jnp.where(j == k, start // 32, 0)
        return pl.multiple_of(row * 32, 32), 0

    def body(a_ref, bq, bg, bc, o_ref, acc):
        j = pl.program_id(0)

        @pl.when(j < n_qkv)
        def _():
            acc[...] = _dot(a_ref[...], bq[...])

        @pl.when(jnp.logical_and(j >= n_qkv, j < n_qkv + n_gates))
        def _():
            acc[...] = _dot(a_ref[...], bg[...])

        @pl.when(j >= n_qkv + n_gates)
        def _():
            acc[...] = _dot(a_ref[...], bc[...])

        o_ref[...] = acc[...].T

    resident = pl.BlockSpec((m, seq), lambda j: (0, 0), pipeline_mode=pl.Buffered(1))
    seg = lambda first, count: pl.BlockSpec((seq, tn), lambda j: (0, jnp.clip(j - first, 0, count - 1)))
    main = pl.pallas_call(
        body, name="wgrad_in",
        out_shape=jax.ShapeDtypeStruct((IN_W, m), F32),
        grid=(n_qkv + n_gates + n_ch,),
        in_specs=[resident, seg(0, n_qkv), seg(n_qkv, n_gates), seg(n_qkv + n_gates, n_ch)],
        out_specs=pl.BlockSpec((pl.Element(tn), pl.Element(m)), out_row),
        scratch_shapes=[pltpu.VMEM((m, tn), F32)],
        compiler_params=_cparams("arbitrary"),
    )(h_t, dp_qkv, dp_gates, dp_ch)

    def lr_body(a_ref, b_ref, full_ref, o_ref, acc):
        acc[...] = _dot(a_ref[...], b_ref[...])
        o_ref[...] = acc[...].T[0:2 * RANK, :]

    whole = lambda shape: pl.BlockSpec(shape, lambda j: (0, 0))
    return pl.pallas_call(
        lr_body, name="wgrad_lr",
        out_shape=jax.ShapeDtypeStruct((IN_W, m), F32),
        grid=(1,),
        in_specs=[whole((m, seq)), whole((seq, LR_W)), pl.BlockSpec(memory_space=pl.ANY)],
        out_specs=pl.BlockSpec((pl.Element(2 * RANK), pl.Element(m)), lambda j: (NAT_LR, 0)),
        scratch_shapes=[pltpu.VMEM((m, LR_W), F32)],
        input_output_aliases={2: 0},
        compiler_params=_cparams("arbitrary"),
    )(h_t, dlr, main)


def _pad_rows(a, rows):
    return jnp.pad(a, ((0, rows - a.shape[0]), (0, 0)))


def _rows128(a):
    a = a.reshape(-1, 128)
    return _pad_rows(a, -(-a.shape[0] // 8) * 8)


def _pack(arrs):
    return jnp.concatenate([_rows128(a) for a in arrs], axis=0)


def _unpack(buf, like):
    out, start = [], 0
    for a in like:
        rows = a.size // 128
        out.append(buf[start:start + rows].reshape(a.shape))
        start += -(-rows // 8) * 8
    return out


def kernel(x, norm_g, w_in, w_gk_f, b_gk_f, w_gk_b, b_gk_b, gla_norm_g, conv_w, conv_b, w_out, final_g, loss_target, m_norm_g, m_w_in, m_w_gk_f, m_b_gk_f, m_w_gk_b, m_b_gk_b, m_gla_norm_g, m_conv_w, m_conv_b, m_w_out, m_final_g, v_norm_g, v_w_in, v_w_gk_f, v_b_gk_f, v_w_gk_b, v_b_gk_b, v_gla_norm_g, v_conv_w, v_conv_b, v_w_out, v_final_g):
    px, py, pc = _position()
    me = _blk(px, py, pc)
    seq = x.shape[1]
    x2d, tgt = x[0], loss_target[0]
    tm = min(512, seq)
    tt = min(256, seq)

    small_s = jnp.concatenate([jnp.concatenate([w_gk_f[0], w_gk_b[0]], axis=1), _pad_rows(conv_w[0], 8)], axis=0)
    w_nat = _allgather_w_in(w_in[0].T).reshape(IN_W, D_MODEL)

    proj, lr, h_t, wout_all, small_all = _inproj(x2d, norm_g, w_nat, w_out[0], small_s, min(1024, seq))
    w_out_full = wout_all.reshape(MIX_W, D_MODEL)
    wgk_cols = 512 // N_DEV
    wgk_f_full = small_all[:, 0:RANK, 0:wgk_cols].transpose(1, 0, 2).reshape(RANK, QK_W)
    wgk_b_full = small_all[:, 0:RANK, wgk_cols:2 * wgk_cols].transpose(1, 0, 2).reshape(RANK, QK_W)
    conv_w_full = _pad_rows(small_all[:, RANK:RANK + 3, :].transpose(1, 0, 2).reshape(3, CONV_W), 8)
    zr = lambda n: jnp.zeros((n, QK_W), F32)
    wgk_f_pad = jnp.concatenate([wgk_f_full, zr(LR_W - RANK)], axis=0).astype(BF16)
    wgk_b_pad = jnp.concatenate([zr(RANK), wgk_b_full, zr(LR_W - 2 * RANK)], axis=0).astype(BF16)

    o_f, o_b, st_f, st_b = _gla_fwd(proj, lr, wgk_f_pad, wgk_b_pad, b_gk_f, b_gk_b, tt)
    tmix = min(256, seq)
    y_t, conv, dx2, dx2b, loss_p, dfg_p = _mix_out_loss(o_f, o_b, proj, x2d, tgt, gla_norm_g, conv_w_full, conv_b,
                                                        w_out_full, final_g.reshape(1, D_MODEL), tmix)

    dp_gates, do, dconv, dgg_p, dcb_p = _mix_bwd(dx2b, o_f, o_b, proj, conv, gla_norm_g, w_out_full, tmix)
    dp_ch, dcw_p = _conv_bwd(dconv, proj, conv_w_full, tmix)
    dqkv_f, dlr_f, dqkv_b, dlr_b, dwf_p, dwb_p, dbf_p, dbb_p = _gla_bwd(
        proj, lr, do, st_f, st_b, wgk_f_pad, wgk_b_pad, b_gk_f, b_gk_b, tt)
    dp_qkv, dlr = _sum_directions(dqkv_f, dqkv_b, dlr_f, dlr_b, tm)
    dw_nat = _weight_grad_in(h_t, dp_qkv, dp_gates, dp_ch, dlr)

    dw_out, sib_in = _weight_grad_out(y_t, dx2b, tm, dw_nat)
    part_out = dw_out.reshape(N_DEV, MIX_W // N_DEV, D_MODEL)
    core = jnp.reshape(pc, (1,)).astype(jnp.int32)
    chip = jnp.reshape(2 * px + py, (1,)).astype(jnp.int32)
    sums_in, sib_out = _chip_sums(dw_nat, sib_in, core, 256, "chip_sums_in", riding=part_out)
    sums_out = _chip_sums(part_out, sib_out, core, 256, "chip_sums_out")
    grad_x2d, dng_p, far_in, far_out = _input_grad(dp_qkv, dp_gates, dp_ch, dlr, w_nat, x2d, norm_g, dx2,
                                                   [sums_in, sums_out], tmix)
    pieces = [dng_p, dbf_p, dbb_p, dgg_p, dcb_p, dfg_p[0], dwf_p[0:RANK], dwb_p[RANK:2 * RANK], dcw_p[0:3], loss_p[0]]
    g_window, small_tot = _final_sum(sums_in, far_in, chip, _pack(pieces), 256, "final_sum_in")
    g_in_t = lax.dynamic_slice_in_dim(g_window, 4 * pc, SHARD_W, axis=0)
    g_w_out, d_w_out, nm_w_out, nv_w_out = _final_sum_adamw(sums_out, far_out, chip, w_out[0], m_w_out[0], v_w_out[0],
                                                            256, "adamw_out")
    flat = lambda a: a[0].T.reshape(SHARD_W, D_MODEL // 128, 128)
    unflat = lambda a: a.reshape(SHARD_W, D_MODEL).T
    d_flat, m_flat, v_flat = _adamw_rows(g_in_t.reshape(SHARD_W, D_MODEL // 128, 128), flat(w_in), flat(m_w_in),
                                         flat(v_w_in), 90, "adamw_in")
    g_w_in, d_w_in, nm_w_in, nv_w_in = g_in_t.T, unflat(d_flat), unflat(m_flat), unflat(v_flat)

    tot = _unpack(small_tot, pieces)
    g_norm_g, g_b_gk_f, g_b_gk_b, g_gla, g_conv_b, g_final = tot[:6]
    g_wgk_f = lax.dynamic_slice_in_dim(tot[6], me * wgk_cols, wgk_cols, axis=1)[None]
    g_wgk_b = lax.dynamic_slice_in_dim(tot[7], me * wgk_cols, wgk_cols, axis=1)[None]
    g_conv_w = lax.dynamic_slice_in_dim(tot[8], me * 128, 128, axis=1)[None]
    loss = tot[9][0]

    small_g = [g_norm_g, g_b_gk_f, g_b_gk_b, g_gla, g_conv_b, g_final, g_wgk_f, g_wgk_b, g_conv_w]
    small_w = [norm_g, b_gk_f, b_gk_b, gla_norm_g, conv_b, final_g, w_gk_f, w_gk_b, conv_w]
    small_m = [m_norm_g, m_b_gk_f, m_b_gk_b, m_gla_norm_g, m_conv_b, m_final_g, m_w_gk_f, m_w_gk_b, m_conv_w]
    small_v = [v_norm_g, v_b_gk_f, v_b_gk_b, v_gla_norm_g, v_conv_b, v_final_g, v_w_gk_f, v_w_gk_b, v_conv_w]
    d_s, m_s, v_s = _adamw_small(_pack(small_g), _pack(small_w), _pack(small_m), _pack(small_v))
    d_l, m_l, v_l = _unpack(d_s, small_w), _unpack(m_s, small_w), _unpack(v_s, small_w)

    def ordered(sm, big_in, big_out):
        return [sm[0], big_in[None], sm[6], sm[1], sm[7], sm[2], sm[3], sm[8], sm[4], big_out[None], sm[5]]

    grads = ordered(small_g, g_w_in, g_w_out)
    deltas = ordered(d_l, d_w_in, d_w_out)
    new_m = ordered(m_l, nm_w_in, nm_w_out)
    new_v = ordered(v_l, nv_w_in, nv_w_out)
    return (loss, grad_x2d[None], *grads, *deltas, *new_m, *new_v)
```

```python
import jax
import jax.numpy as jnp
from jax import lax
from jax.experimental import pallas as pl
from jax.experimental.pallas import tpu as pltpu

F32 = jnp.float32
BF16 = jnp.bfloat16
MESH = pl.DeviceIdType.MESH

N_DEV = 8
D_MODEL = 1024
HEADS = 4
DK = 128
DV = 256
QK_W = HEADS * DK
V_W = HEADS * DV
CONV_W = 1024
MIX_W = V_W + CONV_W
CHUNK = 64
RANK = 16
IN_W = 7200
SHARD_W = IN_W // N_DEV
MAIN_W = 7168
LR_W = 128
OFF_Q, OFF_K, OFF_V, OFF_ZA, OFF_B, OFF_ZC, OFF_C, OFF_H = 0, 512, 1024, 2048, 3072, 4096, 5120, 6144
QKV_W, GATES_W, CH_W = 2048, 3072, 2048
NAT_ZA, NAT_LR, NAT_B, NAT_C, NAT_ZC = 2048, 3072, 3104, 4128, 6176
EPS = 1e-6
GATE_SCALE = 1.0 / 16.0
QSCALE = DK ** -0.5
REF_F, LAST_F = CHUNK // 2, CHUNK - 1
REF_B, LAST_B = CHUNK - 1 - CHUNK // 2, 0

ADAM_LR = 0.001
ADAM_B1 = 0.9
ADAM_B2 = 0.999
ADAM_EPS = 1e-08
ADAM_WD = 0.01
ADAM_STEP = 10

VMEM_LIMIT = 56 * 1024 * 1024


def _cparams(*sem):
    return pltpu.CompilerParams(dimension_semantics=sem, vmem_limit_bytes=VMEM_LIMIT)


def _dot(a, b):
    return jnp.dot(a, b, preferred_element_type=F32)


def _dot_nt(a, b):
    return lax.dot_general(a, b, (((1,), (1,)), ((), ())), preferred_element_type=F32)


def _dot_tn(a, b):
    return lax.dot_general(a, b, (((0,), (0,)), ((), ())), preferred_element_type=F32)


def _sigmoid(z):
    return jax.nn.sigmoid(z)


def _position():
    return lax.axis_index("x"), lax.axis_index("y"), lax.axis_index("c")


def _blk(px, py, pc):
    return 4 * px + 2 * py + pc


def _two_level_gather(pieces, n, send_sems, recv_sems):
    x, y, c = _position()
    me, sibling = (x, y, c), (x, y, 1 - c)
    chips = [(1 - x, y), (x, 1 - y), (1 - x, 1 - y)]

    def copy(a, k, block, to):
        ref = pieces(*block)[a]
        return pltpu.make_async_remote_copy(src_ref=ref, dst_ref=ref, send_sem=send_sems.at[a * 7 + k],
                                            recv_sem=recv_sems.at[a * 7 + k], device_id=to, device_id_type=MESH)

    first = []
    for a in range(n):
        first.append(copy(a, 0, me, sibling))
        first += [copy(a, 1 + j, me, (*chip, c)) for j, chip in enumerate(chips)]
    for cp in first:
        cp.start()
    passed = []
    for j, chip in enumerate(chips):
        for a in range(n):
            copy(a, 1 + j, (*chip, c), me).wait_recv()
            fwd = copy(a, 4 + j, (*chip, c), sibling)
            fwd.start()
            passed.append(fwd)
    for a in range(n):
        copy(a, 0, sibling, me).wait_recv()
    for j, chip in enumerate(chips):
        for a in range(n):
            copy(a, 4 + j, (*chip, 1 - c), me).wait_recv()
    for cp in first + passed:
        cp.wait_send()


EDGE = 16
SHIFTED_ROWS = 912
BODY_ROWS = SHIFTED_ROWS - 2 * EDGE


def _first_tile_row(blk, px):
    return EDGE * (56 * blk + px)


def _allgather_w_in(shifted):
    d = shifted.shape[1]

    def body(sh_ref, w_all, edges, send_sems, recv_sems):
        x, y, c = _position()
        me = _blk(x, y, c)

        def pieces(px, py, pc):
            blk = _blk(px, py, pc)
            body_rows = pl.ds(pl.multiple_of(_first_tile_row(blk, px) + EDGE, EDGE), BODY_ROWS)
            return [w_all.at[body_rows], edges.at[blk]]

        mine = pieces(x, y, c)
        mine[0][...] = sh_ref[EDGE:EDGE + BODY_ROWS, :].astype(BF16)
        edges[me, 0] = sh_ref[0:EDGE, :].astype(BF16)
        edges[me, 1] = sh_ref[EDGE + BODY_ROWS:, :].astype(BF16)
        _two_level_gather(pieces, 2, send_sems, recv_sems)
        shared = {}
        for blk in range(N_DEV):
            first = _first_tile_row(blk, blk // 4)
            shared.setdefault(first, []).append((blk, 0))
            shared.setdefault(first + EDGE + BODY_ROWS, []).append((blk, 1))
        for row, parts in shared.items():
            tile = edges[parts[0][0], parts[0][1]].astype(F32)
            for blk, side in parts[1:]:
                tile = tile + edges[blk, side].astype(F32)
            w_all[row:row + EDGE, :] = tile.astype(BF16)

    vmem = pl.BlockSpec(memory_space=pltpu.VMEM)
    return pl.pallas_call(
        body, name="allgather_w_in",
        out_shape=jax.ShapeDtypeStruct((IN_W, d), BF16),
        in_specs=[vmem], out_specs=vmem,
        scratch_shapes=[pltpu.VMEM((N_DEV, 2, EDGE, d), BF16), pltpu.SemaphoreType.DMA((14,)),
                        pltpu.SemaphoreType.DMA((14,))],
        compiler_params=pltpu.CompilerParams(vmem_limit_bytes=VMEM_LIMIT),
    )(shifted)


def _peer_copies(srcs, outs, send_sems, recv_sems):
    x, y, c = _position()
    me = _blk(x, y, c)
    copies = []
    for a, (src, out) in enumerate(zip(srcs, outs)):
        k = 0
        for dx in (0, 1):
            for dy in (0, 1):
                for dc in (0, 1):
                    if dx + dy + dc == 0:
                        continue
                    peer = (1 - x if dx else x, 1 - y if dy else y, 1 - c if dc else c)
                    copies.append(pltpu.make_async_remote_copy(
                        src_ref=src, dst_ref=out.at[me], send_sem=send_sems.at[a * 7 + k],
                        recv_sem=recv_sems.at[a * 7 + k], device_id=peer, device_id_type=MESH))
                    k += 1
    return copies


def _chip_copies(ins, outs, send_sems, recv_sems):
    x, y, c = _position()
    chips = [(1 - x, y), (x, 1 - y), (1 - x, 1 - y)]
    copies = []
    for a in range(len(ins)):
        for j, (px, py) in enumerate(chips):
            copies.append(pltpu.make_async_remote_copy(
                src_ref=ins[a].at[2 * px + py], dst_ref=outs[a].at[j],
                send_sem=send_sems.at[a * 3 + j], recv_sem=recv_sems.at[a * 3 + j],
                device_id=(px, py, c), device_id_type=MESH))
    return copies


WINDOW_ROWS = SHARD_W + 4


def _window_start(k, parity):
    return 2 * SHARD_W * k + (SHARD_W - 4) * parity


def _owner_block(part, k, parity):
    if part.ndim == 3:
        return part.at[2 * k + parity]
    return part.at[pl.ds(pl.multiple_of(_window_start(k, parity), 8), WINDOW_ROWS)]


def _block_shape(part):
    return part.shape[1:] if part.ndim == 3 else (WINDOW_ROWS, part.shape[1])


def _sibling_copies(part, out, send_sems, recv_sems):
    x, y, c = _position()
    return [pltpu.make_async_remote_copy(src_ref=_owner_block(part, k, 1 - c), dst_ref=out.at[k],
                                         send_sem=send_sems.at[k], recv_sem=recv_sems.at[k],
                                         device_id=(x, y, 1 - c), device_id_type=MESH)
            for k in range(4)]


def _start_all(copies):
    for cp in copies:
        cp.start()


def _wait_all(copies):
    for cp in copies:
        cp.wait_recv()
    for cp in copies:
        cp.wait_send()


def _chip_sums(part, from_sibling, core, tc, name, riding=None):
    rows, cols = _block_shape(part)
    nj = cols // tc

    def body(core_ref, p_ref, s_ref, *rest):
        if riding is None:
            (o_ref,) = rest
        else:
            ride_in, o_ref, ride_out, send_sems, recv_sems = rest
            k, j = pl.program_id(0), pl.program_id(1)

            @pl.when(jnp.logical_and(k == 0, j == 0))
            def _():
                _start_all(_sibling_copies(ride_in, ride_out, send_sems, recv_sems))

        o_ref[0] = (p_ref[...].reshape(rows, tc) + s_ref[0]).astype(BF16)

        if riding is not None:
            @pl.when(jnp.logical_and(k == 3, j == nj - 1))
            def _():
                _wait_all(_sibling_copies(ride_in, ride_out, send_sems, recv_sems))

    hbm = pl.BlockSpec(memory_space=pl.ANY)
    sums = jax.ShapeDtypeStruct((4, rows, cols), BF16)
    tile_out = pl.BlockSpec((1, rows, tc), lambda k, j, core_ref: (k, 0, j))
    if part.ndim == 3:
        mine = pl.BlockSpec((1, rows, tc), lambda k, j, core_ref: (2 * k + core_ref[0], 0, j))
    else:
        mine = pl.BlockSpec((pl.Element(rows), pl.Element(tc)),
                            lambda k, j, core_ref: (pl.multiple_of(_window_start(k, core_ref[0]), 8),
                                                    pl.multiple_of(j * tc, 128)))
    in_specs = [mine, pl.BlockSpec((1, rows, tc), lambda k, j, core_ref: (k, 0, j))]
    if riding is None:
        out_shape, out_specs, scratch, args = sums, tile_out, [], (core, part, from_sibling)
    else:
        out_shape = (sums, jax.ShapeDtypeStruct((4,) + _block_shape(riding), F32))
        out_specs, in_specs = (tile_out, hbm), in_specs + [hbm]
        scratch = [pltpu.SemaphoreType.DMA((4,)), pltpu.SemaphoreType.DMA((4,))]
        args = (core, part, from_sibling, riding)
    return pl.pallas_call(
        body, name=name, out_shape=out_shape,
        grid_spec=pltpu.PrefetchScalarGridSpec(num_scalar_prefetch=1, grid=(4, nj), in_specs=in_specs,
                                               out_specs=out_specs, scratch_shapes=scratch),
        compiler_params=_cparams("arbitrary", "arbitrary"),
    )(*args)


def _sum_chips(s_ref, r_ref):
    f = lambda a: a.astype(F32)
    return ((f(s_ref[0]) + f(r_ref[0])) + f(r_ref[1])) + f(r_ref[2])


def _final_sum(sums, from_chips, chip, small, tc, name):
    _, rows, cols = sums.shape
    nj = cols // tc

    def body(chip_ref, s_ref, r_ref, sm_ref, g_out, tot_ref, all_ref, send_sems, recv_sems):
        j = pl.program_id(0)
        me = _blk(*_position())

        @pl.when(j == 0)
        def _():
            all_ref[me] = sm_ref[...]
            _start_all(_peer_copies((all_ref.at[me],), (all_ref,), send_sems, recv_sems))

        g_out[...] = _sum_chips(s_ref, r_ref)

        @pl.when(j == nj - 1)
        def _():
            _wait_all(_peer_copies((all_ref.at[me],), (all_ref,), send_sems, recv_sems))
            acc = all_ref[0]
            for d in range(1, N_DEV):
                acc = acc + all_ref[d]
            tot_ref[...] = acc

    whole = pl.BlockSpec(small.shape, lambda j, chip_ref: (0, 0))
    return pl.pallas_call(
        body, name=name,
        out_shape=(jax.ShapeDtypeStruct((rows, cols), F32), jax.ShapeDtypeStruct(small.shape, F32)),
        grid_spec=pltpu.PrefetchScalarGridSpec(
            num_scalar_prefetch=1, grid=(nj,),
            in_specs=[pl.BlockSpec((1, rows, tc), lambda j, chip_ref: (chip_ref[0], 0, j)),
                      pl.BlockSpec((3, rows, tc), lambda j, chip_ref: (0, 0, j)), whole],
            out_specs=(pl.BlockSpec((rows, tc), lambda j, chip_ref: (0, j)), whole),
            scratch_shapes=[pltpu.VMEM((N_DEV,) + small.shape, F32), pltpu.SemaphoreType.DMA((7,)),
                            pltpu.SemaphoreType.DMA((7,))]),
        compiler_params=_cparams("arbitrary"),
    )(chip, sums, from_chips, small)


def _adamw_rows(g, w, m, v, tr, name):
    rows = g.shape[0]

    def body(g_ref, w_ref, m_ref, v_ref, d_out, m_out, v_out):
        delta, m_new, v_new = _adamw(w_ref[...], g_ref[...], m_ref[...], v_ref[...])
        d_out[...] = delta
        m_out[...] = m_new
        v_out[...] = v_new

    tile = pl.BlockSpec((tr,) + g.shape[1:], lambda r: (r, 0, 0))
    shp = jax.ShapeDtypeStruct(g.shape, F32)
    return pl.pallas_call(
        body, name=name, out_shape=(shp, shp, shp), grid=(rows // tr,),
        in_specs=[tile] * 4, out_specs=(tile, tile, tile),
        compiler_params=_cparams("arbitrary"),
    )(g, w, m, v)


def _adamw(w, g, m, v):
    m = ADAM_B1 * m + (1.0 - ADAM_B1) * g
    v = ADAM_B2 * v + (1.0 - ADAM_B2) * (g * g)
    m_hat = m / (1.0 - ADAM_B1 ** ADAM_STEP)
    v_hat = v / (1.0 - ADAM_B2 ** ADAM_STEP)
    delta = -ADAM_LR * (m_hat / (jnp.sqrt(v_hat) + ADAM_EPS) + ADAM_WD * w)
    return delta, m, v


def _final_sum_adamw(sums, from_chips, chip, w, m, v, tr, name):
    rows, cols = w.shape

    def body(chip_ref, s_ref, r_ref, w_ref, m_ref, v_ref, g_out, d_out, m_out, v_out):
        g = _sum_chips(s_ref, r_ref)
        delta, m_new, v_new = _adamw(w_ref[...], g, m_ref[...], v_ref[...])
        g_out[...] = g
        d_out[...] = delta
        m_out[...] = m_new
        v_out[...] = v_new

    tile = pl.BlockSpec((tr, cols), lambda r, chip_ref: (r, 0))
    shp = jax.ShapeDtypeStruct((rows, cols), F32)
    return pl.pallas_call(
        body, name=name,
        out_shape=(shp, shp, shp, shp),
        grid_spec=pltpu.PrefetchScalarGridSpec(
            num_scalar_prefetch=1, grid=(rows // tr,),
            in_specs=[pl.BlockSpec((1, tr, cols), lambda r, chip_ref: (chip_ref[0], r, 0)),
                      pl.BlockSpec((3, tr, cols), lambda r, chip_ref: (0, r, 0)),
                      tile, tile, tile],
            out_specs=(tile, tile, tile, tile)),
        compiler_params=_cparams("arbitrary"),
    )(chip, sums, from_chips, w, m, v)


def _adamw_small(g, w, m, v):
    def body(g_ref, w_ref, m_ref, v_ref, d_out, m_out, v_out):
        delta, m_new, v_new = _adamw(w_ref[...], g_ref[...], m_ref[...], v_ref[...])
        d_out[...] = delta
        m_out[...] = m_new
        v_out[...] = v_new

    vmem = pl.BlockSpec(memory_space=pltpu.VMEM)
    shp = jax.ShapeDtypeStruct(g.shape, F32)
    return pl.pallas_call(body, name="adamw_small", out_shape=(shp, shp, shp),
                          in_specs=[vmem] * 4, out_specs=(vmem, vmem, vmem))(g, w, m, v)


TILE_ROWS = (0, 1024, NAT_ZA, NAT_B, NAT_ZC, NAT_C, NAT_C + CONV_W)


def _inproj(x2d, norm_g, w_nat, w_out_s, small_s, tm):
    seq = x2d.shape[0]
    tn = CONV_W
    ni, nj = seq // tm, MAIN_W // tn
    first_sweep = lambda j, i: jnp.where(j == 0, i, ni - 1)

    def tile_row(j, i):
        row = 0
        for k, start in enumerate(TILE_ROWS):
            row = row + jnp.where(j == k, start // 32, 0)
        return pl.multiple_of(row * 32, 32), 0

    def body(x_ref, g_ref, w_ref, wlr_ref, wout_ref, sm_ref, proj_ref, lr_ref, ht_ref, wout_all, sm_all,
             h_all, wout_b, sm_b, send_sems, recv_sems, local_sems):
        j, i = pl.program_id(0), pl.program_id(1)
        rows = pl.ds(pl.multiple_of(i * tm, tm), tm)
        me = _blk(*_position())

        def gather_copies():
            mine = [pltpu.make_async_copy(wout_b, wout_all.at[me], local_sems.at[0]),
                    pltpu.make_async_copy(sm_b, sm_all.at[me], local_sems.at[1])]
            return mine, _peer_copies((wout_b, sm_b), (wout_all, sm_all), send_sems, recv_sems)

        @pl.when(jnp.logical_and(j == 0, i == 0))
        def _():
            wout_b[...] = wout_ref[...].astype(BF16)
            sm_b[...] = sm_ref[...]
            mine, remote = gather_copies()
            for cp in mine + remote:
                cp.start()

        @pl.when(j == 0)
        def _():
            xv = x_ref[...]
            r = lax.rsqrt(jnp.mean(xv * xv, axis=-1, keepdims=True) + EPS)
            h = (xv * r) * g_ref[...]
            hb = h.astype(BF16)
            h_all[rows, :] = hb
            ht_ref[...] = h.T.astype(BF16)
            lr_ref[...] = _dot_nt(hb, wlr_ref[...])

        proj_ref[...] = _dot_nt(h_all[rows, :], w_ref[...]).astype(BF16)

        @pl.when(jnp.logical_and(j == nj - 1, i == ni - 1))
        def _():
            mine, remote = gather_copies()
            for cp in remote:
                cp.wait_recv()
            for cp in remote:
                cp.wait_send()
            for cp in mine:
                cp.wait()

    const = lambda shape: pl.BlockSpec(shape, lambda j, i: (0,) * len(shape))
    hbm = pl.BlockSpec(memory_space=pl.ANY)
    return pl.pallas_call(
        body, name="inproj",
        out_shape=(jax.ShapeDtypeStruct((seq, MAIN_W), BF16), jax.ShapeDtypeStruct((seq, LR_W), F32),
                   jax.ShapeDtypeStruct((D_MODEL, seq), BF16),
                   jax.ShapeDtypeStruct((N_DEV,) + w_out_s.shape, BF16),
                   jax.ShapeDtypeStruct((N_DEV,) + small_s.shape, F32)),
        grid=(nj, ni),
        in_specs=[pl.BlockSpec((tm, D_MODEL), lambda j, i: (first_sweep(j, i), 0)),
                  const((1, D_MODEL)),
                  pl.BlockSpec((pl.Element(tn), pl.Element(D_MODEL)), tile_row),
                  pl.BlockSpec((pl.Element(LR_W), pl.Element(D_MODEL)), lambda j, i: (NAT_LR, 0)),
                  const(w_out_s.shape), const(small_s.shape)],
        out_specs=(pl.BlockSpec((tm, tn), lambda j, i: (i, j)),
                   pl.BlockSpec((tm, LR_W), lambda j, i: (first_sweep(j, i), 0)),
                   pl.BlockSpec((D_MODEL, tm), lambda j, i: (0, first_sweep(j, i))), hbm, hbm),
        scratch_shapes=[pltpu.VMEM((seq, D_MODEL), BF16), pltpu.VMEM(w_out_s.shape, BF16),
                        pltpu.VMEM(small_s.shape, F32), pltpu.SemaphoreType.DMA((14,)),
                        pltpu.SemaphoreType.DMA((14,)), pltpu.SemaphoreType.DMA((2,))],
        compiler_params=_cparams("arbitrary", "arbitrary"),
    )(x2d, norm_g, w_nat, w_nat, w_out_s, small_s)


def _block_masks(tt):
    row = lax.broadcasted_iota(jnp.int32, (tt, tt), 0)
    col = lax.broadcasted_iota(jnp.int32, (tt, tt), 1)
    same = jnp.right_shift(row, 6) == jnp.right_shift(col, 6)
    return (jnp.logical_and(same, col <= row), jnp.logical_and(same, col >= row), jnp.logical_and(same, col > row))


def _dot_split3(ones_mat, x):
    x1 = x.astype(BF16)
    r1 = x - x1.astype(F32)
    x2 = r1.astype(BF16)
    x3 = (r1 - x2.astype(F32)).astype(BF16)
    return (_dot(ones_mat, x3) + _dot(ones_mat, x2)) + _dot(ones_mat, x1)


def _log_gate(logits):
    return (jnp.minimum(logits, 0.0) - jnp.log(1.0 + jnp.exp(-jnp.abs(logits)))) * GATE_SCALE


def _chunk_column_mask(tt):
    nc = tt // CHUNK
    row = lax.broadcasted_iota(jnp.int32, (tt, nc * DK), 0)
    col = lax.broadcasted_iota(jnp.int32, (tt, nc * DK), 1)
    return jnp.right_shift(row, 6) == jnp.right_shift(col, 7)


def _chunked(mask, x, nc):
    wide = jnp.concatenate([x] * nc, axis=1)
    return jnp.where(mask, wide, jnp.zeros_like(wide))


def _gla_fwd(proj, lr, wgk_f, wgk_b, bgk_f, bgk_b, tt):
    seq = proj.shape[0]
    nb, nc, nch = seq // tt, tt // CHUNK, seq // CHUNK

    def body(qf, kf, vf, lrf, qb, kb, vb, lrb, wf, wb, bf, bb, of, ob, stf, stb, s_scr, qs_s, ks_s, qin_s, kout_s):
        @pl.when(pl.program_id(0) == 0)
        def _():
            s_scr[...] = jnp.zeros(s_scr.shape, F32)

        low, upp, sup = _block_masks(tt)
        dirs = ((qf, kf, vf, lrf, wf, bf, of, stf, low, low, REF_F, LAST_F, list(range(nc))),
                (qb, kb, vb, lrb, wb, bb, ob, stb, upp, sup, REF_B, LAST_B, list(reversed(range(nc)))))
        for d, (q_r, k_r, v_r, lr_r, w_r, b_r, o_r, st_r, cum, mask, ref, last, order) in enumerate(dirs):
            logits = _dot(lr_r[...].astype(BF16), w_r[...]) + b_r[...]
            b = _dot_split3(cum.astype(BF16), _log_gate(logits))
            decs = []
            for c in range(nc):
                rows = slice(c * CHUNK, (c + 1) * CHUNK)
                bc = b[rows]
                b_ref, b_last = bc[ref:ref + 1], bc[last:last + 1]
                qc = q_r[rows, :].astype(F32) * QSCALE
                kc = k_r[rows, :].astype(F32)
                qs_s[rows, :] = (qc * jnp.exp(bc - b_ref)).astype(BF16)
                ks_s[rows, :] = (kc * jnp.exp(b_ref - bc)).astype(BF16)
                qin_s[rows, :] = (qc * jnp.exp(bc)).astype(BF16)
                kout_s[rows, :] = (kc * jnp.exp(b_last - bc)).astype(BF16)
                decs.append(jnp.exp(b_last))
            for h in range(HEADS):
                ksl = slice(h * DK, (h + 1) * DK)
                vsl = slice(h * DV, (h + 1) * DV)
                v = v_r[:, vsl].astype(BF16)
                att = jnp.where(mask, _dot_nt(qs_s[:, ksl], ks_s[:, ksl]), 0.0).astype(BF16)
                o_intra = _dot(att, v)
                st = s_scr[d * HEADS + h]
                for c in order:
                    rows = slice(c * CHUNK, (c + 1) * CHUNK)
                    stb = st.astype(BF16)
                    st_r[c, h] = stb
                    o_r[rows, vsl] = (o_intra[rows] + _dot_nt(qin_s[rows, ksl], stb)).astype(BF16)
                    st = st * decs[c][:, ksl] + _dot_tn(v[rows], kout_s[rows, ksl])
                s_scr[d * HEADS + h] = st

    fw = lambda i: (i, 0)
    bw = lambda i: (nb - 1 - i, 0)
    const = lambda i: (0, 0)

    def tok_specs(m):
        return [pl.BlockSpec((tt, QK_W), lambda i: (m(i)[0], OFF_Q // QK_W)),
                pl.BlockSpec((tt, QK_W), lambda i: (m(i)[0], OFF_K // QK_W)),
                pl.BlockSpec((tt, V_W), lambda i: (m(i)[0], OFF_V // V_W)),
                pl.BlockSpec((tt, LR_W), m)]

    st_shape = jax.ShapeDtypeStruct((nch, HEADS, DV, DK), BF16)
    o_shape = jax.ShapeDtypeStruct((seq, V_W), BF16)
    operand = pltpu.VMEM((tt, QK_W), BF16)
    return pl.pallas_call(
        body, name="gla_fwd",
        out_shape=(o_shape, o_shape, st_shape, st_shape),
        grid=(nb,),
        in_specs=tok_specs(fw) + tok_specs(bw) + [
            pl.BlockSpec((LR_W, QK_W), const), pl.BlockSpec((LR_W, QK_W), const),
            pl.BlockSpec((1, QK_W), const), pl.BlockSpec((1, QK_W), const)],
        out_specs=(pl.BlockSpec((tt, V_W), fw), pl.BlockSpec((tt, V_W), bw),
                   pl.BlockSpec((nc, HEADS, DV, DK), lambda i: (i, 0, 0, 0)),
                   pl.BlockSpec((nc, HEADS, DV, DK), lambda i: (nb - 1 - i, 0, 0, 0))),
        scratch_shapes=[pltpu.VMEM((2 * HEADS, DV, DK), F32), operand, operand, operand, operand],
        compiler_params=_cparams("arbitrary"),
    )(proj, proj, proj, lr, proj, proj, proj, lr, wgk_f, wgk_b, bgk_f, bgk_b)


def _head_norm(o, gain):
    outs, rinv = [], []
    for h in range(HEADS):
        oh = o[:, h * DV:(h + 1) * DV]
        r = lax.rsqrt(jnp.mean(oh * oh, axis=-1, keepdims=True) + EPS)
        outs.append((oh * r) * gain)
        rinv.append(r)
    return jnp.concatenate(outs, axis=1), rinv


def _shift_rows(u, prev_row, next_row):
    n = u.shape[0]
    row = lax.broadcasted_iota(jnp.int32, (n, 1), 0)
    up = jnp.where(row == 0, prev_row, pltpu.roll(u, 1, 0))
    un = jnp.where(row == n - 1, next_row, pltpu.roll(u, n - 1, 0))
    return up, un


HALO = 16


def _halo_specs(tm, seq, col_block):
    per = tm // HALO
    last = seq // HALO - 1
    return [pl.BlockSpec((HALO, CONV_W), lambda i: (jnp.maximum(i * per - 1, 0), col_block)),
            pl.BlockSpec((HALO, CONV_W), lambda i: (jnp.minimum((i + 1) * per, last), col_block))]


def _f32(ref):
    return ref[...].astype(F32)


def _last_row(ref):
    return ref[HALO - 1:HALO, :].astype(F32)


def _first_row(ref):
    return ref[0:1, :].astype(F32)


def _mix_out_loss(o_f, o_b, proj, x2d, tgt, gla_g, conv_w, conv_b, w_out, final_g, tm):
    seq = x2d.shape[0]
    nt = seq // tm

    def body(of, ob, za, bg, cg, hc, zc, cprev, cnext, hprev, hnext, x_ref, t_ref, gg, cw, cb, wo, fg,
             yt_ref, conv_ref, dx2_ref, dx2b_ref, loss_ref, dfg_ref):
        i = pl.program_id(0)

        @pl.when(i == 0)
        def _():
            loss_ref[...] = jnp.zeros(loss_ref.shape, F32)
            dfg_ref[...] = jnp.zeros(dfg_ref.shape, F32)

        on, _ = _head_norm(_f32(of) + _f32(ob), gg[...])
        zav = _f32(za)
        y_a = on * (zav * _sigmoid(zav))
        u = _f32(cg) * _f32(hc)
        prev_row = jnp.where(i > 0, _last_row(cprev) * _last_row(hprev), 0.0)
        next_row = jnp.where(i < nt - 1, _first_row(cnext) * _first_row(hnext), 0.0)
        up, un = _shift_rows(u, prev_row, next_row)
        conv = (cw[0:1, :] * up + cw[1:2, :] * u + cw[2:3, :] * un) + cb[...]
        conv_ref[...] = conv.astype(BF16)
        zcv = _f32(zc)
        y_c = _f32(bg) * conv * (zcv * _sigmoid(zcv))
        y = jnp.concatenate([y_a, y_c], axis=1)
        yt_ref[...] = y.T.astype(BF16)
        x2 = x_ref[...] + _dot(y.astype(BF16), wo[...])
        r = lax.rsqrt(jnp.mean(x2 * x2, axis=-1, keepdims=True) + EPS)
        xn = x2 * r
        err = xn * fg[...] - t_ref[...]
        loss_ref[...] += 0.5 * jnp.sum(jnp.mean(err * err, axis=-1, keepdims=True))
        dyf = err * (1.0 / D_MODEL)
        dfg_ref[...] += jnp.sum(dyf * xn, axis=0, keepdims=True)
        dxn = dyf * fg[...]
        dx2 = r * dxn - xn * (r * jnp.mean(dxn * xn, axis=-1, keepdims=True))
        dx2_ref[...] = dx2
        dx2b_ref[...] = dx2.astype(BF16)

    def col(off):
        return pl.BlockSpec((tm, CONV_W), lambda i: (i, off // CONV_W))

    rowt = pl.BlockSpec((tm, D_MODEL), lambda i: (i, 0))
    const = lambda shape: pl.BlockSpec(shape, lambda i: (0, 0))
    return pl.pallas_call(
        body, name="mix_out_loss",
        out_shape=(jax.ShapeDtypeStruct((MIX_W, seq), BF16), jax.ShapeDtypeStruct((seq, CONV_W), BF16),
                   jax.ShapeDtypeStruct((seq, D_MODEL), F32), jax.ShapeDtypeStruct((seq, D_MODEL), BF16),
                   jax.ShapeDtypeStruct((8, 128), F32), jax.ShapeDtypeStruct((1, D_MODEL), F32)),
        grid=(nt,),
        in_specs=[rowt, rowt, col(OFF_ZA), col(OFF_B), col(OFF_C), col(OFF_H), col(OFF_ZC)]
        + _halo_specs(tm, seq, OFF_C // CONV_W) + _halo_specs(tm, seq, OFF_H // CONV_W)
        + [rowt, rowt, const((1, DV)), const((8, CONV_W)), const((1, CONV_W)), const((MIX_W, D_MODEL)),
           const((1, D_MODEL))],
        out_specs=(pl.BlockSpec((MIX_W, tm), lambda i: (0, i)), rowt, rowt, rowt, const((8, 128)),
                   const((1, D_MODEL))),
        compiler_params=_cparams("arbitrary"),
    )(o_f, o_b, proj, proj, proj, proj, proj, proj, proj, proj, proj, x2d, tgt, gla_g, conv_w, conv_b, w_out, final_g)


def _dsilu(z, s):
    return s * (1.0 + z * (1.0 - s))


def _mix_bwd(dx2b, o_f, o_b, proj, conv, gla_g, w_out, tm):
    seq = dx2b.shape[0]

    def body(dx, of, ob, za, bg, zc, cv, gg, wo, dg_ref, do_ref, dconv_ref, dgg_ref, dcb_ref):
        @pl.when(pl.program_id(0) == 0)
        def _():
            dgg_ref[...] = jnp.zeros(dgg_ref.shape, F32)
            dcb_ref[...] = jnp.zeros(dcb_ref.shape, F32)

        dy = _dot_nt(dx[...], wo[...])
        dy_a, dy_c = dy[:, :V_W], dy[:, V_W:]
        zcv, bgv, convv = _f32(zc), _f32(bg), _f32(cv)
        sc = _sigmoid(zcv)
        szc = zcv * sc
        dg_ref[:, CONV_W:2 * CONV_W] = (dy_c * convv * szc).astype(BF16)
        dconv = dy_c * bgv * szc
        dconv_ref[...] = dconv.astype(BF16)
        dcb_ref[...] += jnp.sum(dconv, axis=0, keepdims=True)
        dg_ref[:, 2 * CONV_W:] = (dy_c * bgv * convv * _dsilu(zcv, sc)).astype(BF16)

        o = _f32(of) + _f32(ob)
        gain = gg[...]
        on, rinv = _head_norm(o, gain)
        zav = _f32(za)
        sa = _sigmoid(zav)
        dg_ref[:, :CONV_W] = (dy_a * on * _dsilu(zav, sa)).astype(BF16)
        don = dy_a * (zav * sa)
        dgg = jnp.zeros((1, DV), F32)
        dos = []
        for h in range(HEADS):
            sl = slice(h * DV, (h + 1) * DV)
            oh, r, dh = o[:, sl], rinv[h], don[:, sl]
            ohn = oh * r
            dgg = dgg + jnp.sum(dh * ohn, axis=0, keepdims=True)
            dn = dh * gain
            dos.append(r * dn - ohn * (r * jnp.mean(dn * ohn, axis=-1, keepdims=True)))
        dgg_ref[...] += dgg
        do_ref[...] = jnp.concatenate(dos, axis=1).astype(BF16)

    def col(off):
        return pl.BlockSpec((tm, CONV_W), lambda i: (i, off // CONV_W))

    rowt = pl.BlockSpec((tm, D_MODEL), lambda i: (i, 0))
    const = lambda shape: pl.BlockSpec(shape, lambda i: (0, 0))
    return pl.pallas_call(
        body, name="mix_bwd",
        out_shape=(jax.ShapeDtypeStruct((seq, GATES_W), BF16), jax.ShapeDtypeStruct((seq, V_W), BF16),
                   jax.ShapeDtypeStruct((seq, CONV_W), BF16),
                   jax.ShapeDtypeStruct((1, DV), F32), jax.ShapeDtypeStruct((1, CONV_W), F32)),
        grid=(seq // tm,),
        in_specs=[rowt, rowt, rowt, col(OFF_ZA), col(OFF_B), col(OFF_ZC), rowt, const((1, DV)),
                  const((MIX_W, D_MODEL))],
        out_specs=(pl.BlockSpec((tm, GATES_W), lambda i: (i, 0)), rowt, rowt, const((1, DV)), const((1, CONV_W))),
        compiler_params=_cparams("arbitrary"),
    )(dx2b, o_f, o_b, proj, proj, proj, conv, gla_g, w_out)


def _conv_bwd(dconv, proj, conv_w, tm):
    seq = dconv.shape[0]
    nt = seq // tm

    def body(dc_in, dprev, dnext, cg, hc, cprev, cnext, hprev, hnext, cw, dch_ref, dcw_ref):
        i = pl.program_id(0)

        @pl.when(i == 0)
        def _():
            dcw_ref[...] = jnp.zeros(dcw_ref.shape, F32)

        first, lastt = i > 0, i < nt - 1
        dcv = _f32(dc_in)
        d_up, d_un = _shift_rows(dcv, jnp.where(first, _last_row(dprev), 0.0), jnp.where(lastt, _first_row(dnext), 0.0))
        cgv, hcv = _f32(cg), _f32(hc)
        u = cgv * hcv
        u_up, u_un = _shift_rows(u, jnp.where(first, _last_row(cprev) * _last_row(hprev), 0.0),
                                 jnp.where(lastt, _first_row(cnext) * _first_row(hnext), 0.0))
        du = cw[0:1, :] * d_un + cw[1:2, :] * dcv + cw[2:3, :] * d_up
        dch_ref[:, :CONV_W] = (du * hcv).astype(BF16)
        dch_ref[:, CONV_W:] = (du * cgv).astype(BF16)
        dcw_ref[0:1, :] += jnp.sum(dcv * u_up, axis=0, keepdims=True)
        dcw_ref[1:2, :] += jnp.sum(dcv * u, axis=0, keepdims=True)
        dcw_ref[2:3, :] += jnp.sum(dcv * u_un, axis=0, keepdims=True)

    def col(off):
        return pl.BlockSpec((tm, CONV_W), lambda i: (i, off // CONV_W))

    rowt = pl.BlockSpec((tm, CONV_W), lambda i: (i, 0))
    const = lambda shape: pl.BlockSpec(shape, lambda i: (0, 0))
    return pl.pallas_call(
        body, name="conv_bwd",
        out_shape=(jax.ShapeDtypeStruct((seq, CH_W), BF16), jax.ShapeDtypeStruct((8, CONV_W), F32)),
        grid=(nt,),
        in_specs=[rowt] + _halo_specs(tm, seq, 0) + [col(OFF_C), col(OFF_H)]
        + _halo_specs(tm, seq, OFF_C // CONV_W) + _halo_specs(tm, seq, OFF_H // CONV_W) + [const((8, CONV_W))],
        out_specs=(pl.BlockSpec((tm, CH_W), lambda i: (i, 0)), const((8, CONV_W))),
        compiler_params=_cparams("arbitrary"),
    )(dconv, dconv, dconv, proj, proj, proj, proj, proj, proj, conv_w)


def _gla_bwd(proj, lr, do, st_f, st_b, wgk_f, wgk_b, bgk_f, bgk_b, tt):
    seq = proj.shape[0]
    nb, nc = seq // tt, tt // CHUNK

    def body(qf, kf, vf, lrf, dof, stf, qb, kb, vb, lrb, dob, stb, wf, wb, bf, bb,
             dqkv_f, dlr_f, dqkv_b, dlr_b, dwf, dwb, dbf, dbb,
             ds_scr, eq_s, ek_s, ein_s, eout_s, qs_s, ks_s, qin_s, kout_s, db_s, lg_s):
        @pl.when(pl.program_id(0) == 0)
        def _():
            ds_scr[...] = jnp.zeros(ds_scr.shape, F32)
            for r in (dwf, dwb, dbf, dbb):
                r[...] = jnp.zeros(r.shape, F32)

        low, upp, sup = _block_masks(tt)
        row = lax.broadcasted_iota(jnp.int32, (CHUNK, 1), 0)
        kmask = _chunk_column_mask(tt)
        dirs = ((qf, kf, vf, lrf, dof, stf, wf, bf, dqkv_f, dlr_f, dwf, dbf,
                 low, upp, low, REF_F, LAST_F, list(reversed(range(nc)))),
                (qb, kb, vb, lrb, dob, stb, wb, bb, dqkv_b, dlr_b, dwb, dbb,
                 upp, low, sup, REF_B, LAST_B, list(range(nc))))
        for d, (q_r, k_r, v_r, lr_r, do_r, st_r, w_r, b_r, dqkv_r, dlr_r, dw_r, db_r,
                cum, cum_t, mask, ref, last, order) in enumerate(dirs):
            lrv = lr_r[...].astype(BF16)
            wv = w_r[...]
            logits = _dot(lrv, wv) + b_r[...]
            lg_s[...] = logits
            b = _dot_split3(cum.astype(BF16), _log_gate(logits))
            decs = []
            for c in range(nc):
                rows = slice(c * CHUNK, (c + 1) * CHUNK)
                bc = b[rows]
                b_ref, b_last = bc[ref:ref + 1], bc[last:last + 1]
                qc = q_r[rows, :].astype(F32) * QSCALE
                kc = k_r[rows, :].astype(F32)
                e_q, e_k, e_in, e_out = jnp.exp(bc - b_ref), jnp.exp(b_ref - bc), jnp.exp(bc), jnp.exp(b_last - bc)
                eq_s[rows, :], ek_s[rows, :], ein_s[rows, :], eout_s[rows, :] = e_q, e_k, e_in, e_out
                qs_s[rows, :] = (qc * e_q).astype(BF16)
                ks_s[rows, :] = (kc * e_k).astype(BF16)
                qin_s[rows, :] = (qc * e_in).astype(BF16)
                kout_s[rows, :] = (kc * e_out).astype(BF16)
                decs.append(jnp.exp(b_last))
            for h in range(HEADS):
                ksl = slice(h * DK, (h + 1) * DK)
                vsl = slice(h * DV, (h + 1) * DV)
                v = v_r[:, vsl].astype(BF16)
                dov = do_r[:, vsl].astype(BF16)
                qsb, ksb = qs_s[:, ksl], ks_s[:, ksl]
                att = jnp.where(mask, _dot_nt(qsb, ksb), 0.0).astype(BF16)
                datt = jnp.where(mask, _dot_nt(dov, v), 0.0).astype(BF16)
                dqs = _dot(datt, ksb)
                dks = _dot_tn(datt, qsb)
                dv_intra = _dot_tn(att, dov)
                g_t = _dot_tn(dov, _chunked(kmask, qin_s[:, ksl], nc))
                ds = ds_scr[d * HEADS + h]
                for c in order:
                    rows = slice(c * CHUNK, (c + 1) * CHUNK)
                    dsb = ds.astype(BF16)
                    s_prev = st_r[c, h]
                    dk_out = _dot(v[rows], dsb)
                    dq_in = _dot(dov[rows], s_prev)
                    dv = dv_intra[rows] + _dot_nt(kout_s[rows, ksl], dsb)
                    dqkv_r[rows, OFF_V + h * DV:OFF_V + (h + 1) * DV] = dv.astype(BF16)
                    dec = decs[c][:, ksl]
                    ddec = jnp.sum(ds * s_prev.astype(F32), axis=0, keepdims=True)
                    e_out = eout_s[rows, ksl]
                    qc = q_r[rows, ksl].astype(F32) * QSCALE
                    kc = k_r[rows, ksl].astype(F32)
                    dq = dqs[rows] * eq_s[rows, ksl] + dq_in * ein_s[rows, ksl]
                    dk = dks[rows] * ek_s[rows, ksl] + dk_out * e_out
                    dqkv_r[rows, OFF_Q + h * DK:OFF_Q + (h + 1) * DK] = (dq * QSCALE).astype(BF16)
                    dqkv_r[rows, OFF_K + h * DK:OFF_K + (h + 1) * DK] = dk.astype(BF16)
                    tail = jnp.sum(dk_out * (kc * e_out), axis=0, keepdims=True) + ddec * dec
                    db_s[rows, ksl] = (qc * dq - kc * dk) + jnp.where(row == last, tail, 0.0)
                    ds = ds * dec + g_t[:, c * DK:(c + 1) * DK]
                ds_scr[d * HEADS + h] = ds
            dg = _dot_split3(cum_t.astype(BF16), db_s[...])
            dlogit = (dg * GATE_SCALE) * _sigmoid(-lg_s[...])
            dlb = dlogit.astype(BF16)
            dlr_r[...] = _dot_nt(dlb, wv)
            dw_r[...] += _dot_tn(lrv, dlb)
            db_r[...] += jnp.sum(dlogit, axis=0, keepdims=True)

    fw = lambda i: (nb - 1 - i, 0)
    bw = lambda i: (i, 0)
    const = lambda i: (0, 0)

    def tok_specs(m):
        return [pl.BlockSpec((tt, QK_W), lambda i: (m(i)[0], OFF_Q // QK_W)),
                pl.BlockSpec((tt, QK_W), lambda i: (m(i)[0], OFF_K // QK_W)),
                pl.BlockSpec((tt, V_W), lambda i: (m(i)[0], OFF_V // V_W)),
                pl.BlockSpec((tt, LR_W), m),
                pl.BlockSpec((tt, V_W), m),
                pl.BlockSpec((nc, HEADS, DV, DK), lambda i: (m(i)[0], 0, 0, 0))]

    dqkv = jax.ShapeDtypeStruct((seq, QK_W + QK_W + V_W), BF16)
    dlr = jax.ShapeDtypeStruct((seq, LR_W), F32)
    dw = jax.ShapeDtypeStruct((LR_W, QK_W), F32)
    dbias = jax.ShapeDtypeStruct((1, QK_W), F32)
    return pl.pallas_call(
        body, name="gla_bwd",
        out_shape=(dqkv, dlr, dqkv, dlr, dw, dw, dbias, dbias),
        grid=(nb,),
        in_specs=tok_specs(fw) + tok_specs(bw) + [
            pl.BlockSpec((LR_W, QK_W), const), pl.BlockSpec((LR_W, QK_W), const),
            pl.BlockSpec((1, QK_W), const), pl.BlockSpec((1, QK_W), const)],
        out_specs=(pl.BlockSpec((tt, QK_W + QK_W + V_W), fw), pl.BlockSpec((tt, LR_W), fw),
                   pl.BlockSpec((tt, QK_W + QK_W + V_W), bw), pl.BlockSpec((tt, LR_W), bw),
                   pl.BlockSpec((LR_W, QK_W), const), pl.BlockSpec((LR_W, QK_W), const),
                   pl.BlockSpec((1, QK_W), const), pl.BlockSpec((1, QK_W), const)),
        scratch_shapes=[pltpu.VMEM((2 * HEADS, DV, DK), F32)] + [pltpu.VMEM((tt, QK_W), F32)] * 4
        + [pltpu.VMEM((tt, QK_W), BF16)] * 4 + [pltpu.VMEM((tt, QK_W), F32)] * 2,
        compiler_params=_cparams("arbitrary"),
    )(proj, proj, proj, lr, do, st_f, proj, proj, proj, lr, do, st_b, wgk_f, wgk_b, bgk_f, bgk_b)


def _sum_directions(dqkv_f, dqkv_b, dlr_f, dlr_b, tm):
    seq = dqkv_f.shape[0]

    def body(a, b, la, lb, dp_out, dlr_out):
        dp_out[...] = (_f32(a) + _f32(b)).astype(BF16)
        dlr_out[...] = (la[...] + lb[...]).astype(BF16)

    rowt = pl.BlockSpec((tm, QKV_W), lambda i: (i, 0))
    lrt = pl.BlockSpec((tm, LR_W), lambda i: (i, 0))
    return pl.pallas_call(
        body, name="sum_directions",
        out_shape=(jax.ShapeDtypeStruct((seq, QKV_W), BF16), jax.ShapeDtypeStruct((seq, LR_W), BF16)),
        grid=(seq // tm,),
        in_specs=[rowt, rowt, lrt, lrt],
        out_specs=(rowt, lrt),
        compiler_params=_cparams("arbitrary"),
    )(dqkv_f, dqkv_b, dlr_f, dlr_b)


def _input_grad(dp_qkv, dp_gates, dp_ch, dlr, w_nat, x2d, norm_g, dx2, sums, tm):
    seq = x2d.shape[0]
    nt, n = seq // tm, len(sums)

    def body(dq, dg, dc, dl, w, x_ref, g_ref, dx2_ref, *rest):
        ins, (gx_ref, dng_ref), outs = rest[:n], rest[n:n + 2], rest[n + 2:2 * n + 2]
        send_sems, recv_sems = rest[2 * n + 2:]
        i = pl.program_id(0)

        @pl.when(i == 0)
        def _():
            for cp in _chip_copies(ins, outs, send_sems, recv_sems):
                cp.start()
            dng_ref[...] = jnp.zeros(dng_ref.shape, F32)

        dh = (_dot(dl[...], w[NAT_LR:NAT_LR + LR_W, :]) + _dot(dq[...], w[0:NAT_ZA, :])
              + _dot(dg[:, 0:CONV_W], w[NAT_ZA:NAT_LR, :]) + _dot(dg[:, CONV_W:2 * CONV_W], w[NAT_B:NAT_C, :])
              + _dot(dg[:, 2 * CONV_W:], w[NAT_ZC:IN_W, :]) + _dot(dc[...], w[NAT_C:NAT_ZC, :]))
        xv = x_ref[...]
        r = lax.rsqrt(jnp.mean(xv * xv, axis=-1, keepdims=True) + EPS)
        xn = xv * r
        dng_ref[...] += jnp.sum(dh * xn, axis=0, keepdims=True)
        dn = dh * g_ref[...]
        gx_ref[...] = (r * dn - xn * (r * jnp.mean(dn * xn, axis=-1, keepdims=True))) + dx2_ref[...]

        @pl.when(i == nt - 1)
        def _():
            copies = _chip_copies(ins, outs, send_sems, recv_sems)
            for cp in copies:
                cp.wait_recv()
            for cp in copies:
                cp.wait_send()

    rowt = pl.BlockSpec((tm, D_MODEL), lambda i: (i, 0))
    seg = lambda width: pl.BlockSpec((tm, width), lambda i: (i, 0))
    resident = lambda rows: pl.BlockSpec((rows, D_MODEL), lambda i: (0, 0), pipeline_mode=pl.Buffered(1))
    hbm = pl.BlockSpec(memory_space=pl.ANY)
    return pl.pallas_call(
        body, name="input_grad",
        out_shape=(jax.ShapeDtypeStruct((seq, D_MODEL), F32), jax.ShapeDtypeStruct((1, D_MODEL), F32))
        + tuple(jax.ShapeDtypeStruct((3,) + s.shape[1:], s.dtype) for s in sums),
        grid=(nt,),
        in_specs=[seg(QKV_W), seg(GATES_W), seg(CH_W), seg(LR_W), resident(IN_W),
                  rowt, pl.BlockSpec((1, D_MODEL), lambda i: (0, 0)), rowt] + [hbm] * n,
        out_specs=(rowt, pl.BlockSpec((1, D_MODEL), lambda i: (0, 0))) + (hbm,) * n,
        scratch_shapes=[pltpu.SemaphoreType.DMA((3 * n,)), pltpu.SemaphoreType.DMA((3 * n,))],
        compiler_params=_cparams("arbitrary"),
    )(dp_qkv, dp_gates, dp_ch, dlr, w_nat, x2d, norm_g, dx2, *sums)


def _weight_grad_out(y_t, dx2b, tk, riding):
    m, seq = y_t.shape
    n = dx2b.shape[1]
    nk = seq // tk

    def body(a_ref, b_ref, ride_in, o_ref, ride_out, send_sems, recv_sems):
        k = pl.program_id(0)

        @pl.when(k == 0)
        def _():
            _start_all(_sibling_copies(ride_in, ride_out, send_sems, recv_sems))
            o_ref[...] = jnp.zeros(o_ref.shape, F32)

        o_ref[...] += _dot(a_ref[...], b_ref[...])

        @pl.when(k == nk - 1)
        def _():
            _wait_all(_sibling_copies(ride_in, ride_out, send_sems, recv_sems))

    hbm = pl.BlockSpec(memory_space=pl.ANY)
    return pl.pallas_call(
        body, name="wgrad_out",
        out_shape=(jax.ShapeDtypeStruct((m, n), F32), jax.ShapeDtypeStruct((4,) + _block_shape(riding), F32)),
        grid=(nk,),
        in_specs=[pl.BlockSpec((m, tk), lambda k: (0, k)), pl.BlockSpec((tk, n), lambda k: (k, 0)), hbm],
        out_specs=(pl.BlockSpec((m, n), lambda k: (0, 0)), hbm),
        scratch_shapes=[pltpu.SemaphoreType.DMA((4,)), pltpu.SemaphoreType.DMA((4,))],
        compiler_params=_cparams("arbitrary"),
    )(y_t, dx2b, riding)


def _weight_grad_in(h_t, dp_qkv, dp_gates, dp_ch, dlr):
    m, seq = h_t.shape
    tn = 512
    n_qkv, n_gates, n_ch = QKV_W // tn, GATES_W // tn, CH_W // tn
    starts = ([k * tn for k in range(n_qkv)] + [NAT_ZA, NAT_ZA + tn, NAT_B, NAT_B + tn, NAT_ZC, NAT_ZC + tn]
              + [NAT_C + k * tn for k in range(n_ch)])

    def out_row(j):
        row = 0
        for k, start in enumerate(starts):
            row = row + jnp.where(j == k, start // 32, 0)
        return pl.multiple_of(row * 32, 32), 0

    def body(a_ref, bq, bg, bc, o_ref, acc):
        j = pl.program_id(0)

        @pl.when(j < n_qkv)
        def _():
            acc[...] = _dot(a_ref[...], bq[...])

        @pl.when(jnp.logical_and(j >= n_qkv, j < n_qkv + n_gates))
        def _():
            acc[...] = _dot(a_ref[...], bg[...])

        @pl.when(j >= n_qkv + n_gates)
        def _():
            acc[...] = _dot(a_ref[...], bc[...])

        o_ref[...] = acc[...].T

    resident = pl.BlockSpec((m, seq), lambda j: (0, 0), pipeline_mode=pl.Buffered(1))
    seg = lambda first, count: pl.BlockSpec((seq, tn), lambda j: (0, jnp.clip(j - first, 0, count - 1)))
    main = pl.pallas_call(
        body, name="wgrad_in",
        out_shape=jax.ShapeDtypeStruct((IN_W, m), F32),
        grid=(n_qkv + n_gates + n_ch,),
        in_specs=[resident, seg(0, n_qkv), seg(n_qkv, n_gates), seg(n_qkv + n_gates, n_ch)],
        out_specs=pl.BlockSpec((pl.Element(tn), pl.Element(m)), out_row),
        scratch_shapes=[pltpu.VMEM((m, tn), F32)],
        compiler_params=_cparams("arbitrary"),
    )(h_t, dp_qkv, dp_gates, dp_ch)

    def lr_body(a_ref, b_ref, full_ref, o_ref, acc):
        acc[...] = _dot(a_ref[...], b_ref[...])
        o_ref[...] = acc[...].T[0:2 * RANK, :]

    whole = lambda shape: pl.BlockSpec(shape, lambda j: (0, 0))
    return pl.pallas_call(
        lr_body, name="wgrad_lr",
        out_shape=jax.ShapeDtypeStruct((IN_W, m), F32),
        grid=(1,),
        in_specs=[whole((m, seq)), whole((seq, LR_W)), pl.BlockSpec(memory_space=pl.ANY)],
        out_specs=pl.BlockSpec((pl.Element(2 * RANK), pl.Element(m)), lambda j: (NAT_LR, 0)),
        scratch_shapes=[pltpu.VMEM((m, LR_W), F32)],
        input_output_aliases={2: 0},
        compiler_params=_cparams("arbitrary"),
    )(h_t, dlr, main)


def _pad_rows(a, rows):
    return jnp.pad(a, ((0, rows - a.shape[0]), (0, 0)))


def _rows128(a):
    a = a.reshape(-1, 128)
    return _pad_rows(a, -(-a.shape[0] // 8) * 8)


def _pack(arrs):
    return jnp.concatenate([_rows128(a) for a in arrs], axis=0)


def _unpack(buf, like):
    out, start = [], 0
    for a in like:
        rows = a.size // 128
        out.append(buf[start:start + rows].reshape(a.shape))
        start += -(-rows // 8) * 8
    return out


def kernel(x, norm_g, w_in, w_gk_f, b_gk_f, w_gk_b, b_gk_b, gla_norm_g, conv_w, conv_b, w_out, final_g, loss_target, m_norm_g, m_w_in, m_w_gk_f, m_b_gk_f, m_w_gk_b, m_b_gk_b, m_gla_norm_g, m_conv_w, m_conv_b, m_w_out, m_final_g, v_norm_g, v_w_in, v_w_gk_f, v_b_gk_f, v_w_gk_b, v_b_gk_b, v_gla_norm_g, v_conv_w, v_conv_b, v_w_out, v_final_g):
    px, py, pc = _position()
    me = _blk(px, py, pc)
    seq = x.shape[1]
    x2d, tgt = x[0], loss_target[0]
    tm = min(512, seq)
    tt = min(256, seq)

    small_s = jnp.concatenate([jnp.concatenate([w_gk_f[0], w_gk_b[0]], axis=1), _pad_rows(conv_w[0], 8)], axis=0)
    shifted = lax.dynamic_update_slice(jnp.zeros((SHIFTED_ROWS, D_MODEL), F32), w_in[0].T, (4 * (me % 4), 0))
    w_nat = _allgather_w_in(shifted)

    proj, lr, h_t, wout_all, small_all = _inproj(x2d, norm_g, w_nat, w_out[0], small_s, min(1024, seq))
    w_out_full = wout_all.reshape(MIX_W, D_MODEL)
    wgk_cols = 512 // N_DEV
    wgk_f_full = small_all[:, 0:RANK, 0:wgk_cols].transpose(1, 0, 2).reshape(RANK, QK_W)
    wgk_b_full = small_all[:, 0:RANK, wgk_cols:2 * wgk_cols].transpose(1, 0, 2).reshape(RANK, QK_W)
    conv_w_full = _pad_rows(small_all[:, RANK:RANK + 3, :].transpose(1, 0, 2).reshape(3, CONV_W), 8)
    zr = lambda n: jnp.zeros((n, QK_W), F32)
    wgk_f_pad = jnp.concatenate([wgk_f_full, zr(LR_W - RANK)], axis=0).astype(BF16)
    wgk_b_pad = jnp.concatenate([zr(RANK), wgk_b_full, zr(LR_W - 2 * RANK)], axis=0).astype(BF16)

    o_f, o_b, st_f, st_b = _gla_fwd(proj, lr, wgk_f_pad, wgk_b_pad, b_gk_f, b_gk_b, tt)
    tmix = min(256, seq)
    y_t, conv, dx2, dx2b, loss_p, dfg_p = _mix_out_loss(o_f, o_b, proj, x2d, tgt, gla_norm_g, conv_w_full, conv_b,
                                                        w_out_full, final_g.reshape(1, D_MODEL), tmix)

    dp_gates, do, dconv, dgg_p, dcb_p = _mix_bwd(dx2b, o_f, o_b, proj, conv, gla_norm_g, w_out_full, tmix)
    dp_ch, dcw_p = _conv_bwd(dconv, proj, conv_w_full, tmix)
    dqkv_f, dlr_f, dqkv_b, dlr_b, dwf_p, dwb_p, dbf_p, dbb_p = _gla_bwd(
        proj, lr, do, st_f, st_b, wgk_f_pad, wgk_b_pad, b_gk_f, b_gk_b, tt)
    dp_qkv, dlr = _sum_directions(dqkv_f, dqkv_b, dlr_f, dlr_b, tm)
    dw_nat = _weight_grad_in(h_t, dp_qkv, dp_gates, dp_ch, dlr)

    dw_out, sib_in = _weight_grad_out(y_t, dx2b, tm, dw_nat)
    part_out = dw_out.reshape(N_DEV, MIX_W // N_DEV, D_MODEL)
    core = jnp.reshape(pc, (1,)).astype(jnp.int32)
    chip = jnp.reshape(2 * px + py, (1,)).astype(jnp.int32)
    sums_in, sib_out = _chip_sums(dw_nat, sib_in, core, 256, "chip_sums_in", riding=part_out)
    sums_out = _chip_sums(part_out, sib_out, core, 256, "chip_sums_out")
    grad_x2d, dng_p, far_in, far_out = _input_grad(dp_qkv, dp_gates, dp_ch, dlr, w_nat, x2d, norm_g, dx2,
                                                   [sums_in, sums_out], tmix)
    pieces = [dng_p, dbf_p, dbb_p, dgg_p, dcb_p, dfg_p[0], dwf_p[0:RANK], dwb_p[RANK:2 * RANK], dcw_p[0:3], loss_p[0]]
    g_window, small_tot = _final_sum(sums_in, far_in, chip, _pack(pieces), 256, "final_sum_in")
    g_in_t = lax.dynamic_slice_in_dim(g_window, 4 * pc, SHARD_W, axis=0)
    g_w_out, d_w_out, nm_w_out, nv_w_out = _final_sum_adamw(sums_out, far_out, chip, w_out[0], m_w_out[0], v_w_out[0],
                                                            256, "adamw_out")
    flat = lambda a: a[0].T.reshape(SHARD_W, D_MODEL // 128, 128)
    unflat = lambda a: a.reshape(SHARD_W, D_MODEL).T
    d_flat, m_flat, v_flat = _adamw_rows(g_in_t.reshape(SHARD_W, D_MODEL // 128, 128), flat(w_in), flat(m_w_in),
                                         flat(v_w_in), 90, "adamw_in")
    g_w_in, d_w_in, nm_w_in, nv_w_in = g_in_t.T, unflat(d_flat), unflat(m_flat), unflat(v_flat)

    tot = _unpack(small_tot, pieces)
    g_norm_g, g_b_gk_f, g_b_gk_b, g_gla, g_conv_b, g_final = tot[:6]
    g_wgk_f = lax.dynamic_slice_in_dim(tot[6], me * wgk_cols, wgk_cols, axis=1)[None]
    g_wgk_b = lax.dynamic_slice_in_dim(tot[7], me * wgk_cols, wgk_cols, axis=1)[None]
    g_conv_w = lax.dynamic_slice_in_dim(tot[8], me * 128, 128, axis=1)[None]
    loss = tot[9][0]

    small_g = [g_norm_g, g_b_gk_f, g_b_gk_b, g_gla, g_conv_b, g_final, g_wgk_f, g_wgk_b, g_conv_w]
    small_w = [norm_g, b_gk_f, b_gk_b, gla_norm_g, conv_b, final_g, w_gk_f, w_gk_b, conv_w]
    small_m = [m_norm_g, m_b_gk_f, m_b_gk_b, m_gla_norm_g, m_conv_b, m_final_g, m_w_gk_f, m_w_gk_b, m_conv_w]
    small_v = [v_norm_g, v_b_gk_f, v_b_gk_b, v_gla_norm_g, v_conv_b, v_final_g, v_w_gk_f, v_w_gk_b, v_conv_w]
    d_s, m_s, v_s = _adamw_small(_pack(small_g), _pack(small_w), _pack(small_m), _pack(small_v))
    d_l, m_l, v_l = _unpack(d_s, small_w), _unpack(m_s, small_w), _unpack(v_s, small_w)

    def ordered(sm, big_in, big_out):
        return [sm[0], big_in[None], sm[6], sm[1], sm[7], sm[2], sm[3], sm[8], sm[4], big_out[None], sm[5]]

    grads = ordered(small_g, g_w_in, g_w_out)
    deltas = ordered(d_l, d_w_in, d_w_out)
    new_m = ordered(m_l, nm_w_in, nm_w_out)
    new_v = ordered(v_l, nv_w_in, nv_w_out)
    return (loss, grad_x2d[None], *grads, *deltas, *new_m, *new_v)
```

```python
import jax
import jax.numpy as jnp
from jax import lax
from jax.experimental import pallas as pl
from jax.experimental.pallas import tpu as pltpu

F32 = jnp.float32
BF16 = jnp.bfloat16
MESH = pl.DeviceIdType.MESH

N_DEV = 8
D_MODEL = 1024
HEADS = 4
DK = 128
DV = 256
QK_W = HEADS * DK
V_W = HEADS * DV
CONV_W = 1024
MIX_W = V_W + CONV_W
CHUNK = 64
RANK = 16
IN_W = 7200
SHARD_W = IN_W // N_DEV
MAIN_W = 7168
LR_W = 128
OFF_Q, OFF_K, OFF_V, OFF_ZA, OFF_B, OFF_ZC, OFF_C, OFF_H = 0, 512, 1024, 2048, 3072, 4096, 5120, 6144
QKV_W, GATES_W, CH_W = 2048, 3072, 2048
NAT_ZA, NAT_LR, NAT_B, NAT_C, NAT_ZC = 2048, 3072, 3104, 4128, 6176
EPS = 1e-6
GATE_SCALE = 1.0 / 16.0
QSCALE = DK ** -0.5
REF_F, LAST_F = CHUNK // 2, CHUNK - 1
REF_B, LAST_B = CHUNK - 1 - CHUNK // 2, 0

ADAM_LR = 0.001
ADAM_B1 = 0.9
ADAM_B2 = 0.999
ADAM_EPS = 1e-08
ADAM_WD = 0.01
ADAM_STEP = 10

VMEM_LIMIT = 56 * 1024 * 1024


def _cparams(*sem):
    return pltpu.CompilerParams(dimension_semantics=sem, vmem_limit_bytes=VMEM_LIMIT)


def _dot(a, b):
    return jnp.dot(a, b, preferred_element_type=F32)


def _dot_nt(a, b):
    return lax.dot_general(a, b, (((1,), (1,)), ((), ())), preferred_element_type=F32)


def _dot_tn(a, b):
    return lax.dot_general(a, b, (((0,), (0,)), ((), ())), preferred_element_type=F32)


def _sigmoid(z):
    return jax.nn.sigmoid(z)


def _position():
    return lax.axis_index("x"), lax.axis_index("y"), lax.axis_index("c")


def _blk(px, py, pc):
    return 4 * px + 2 * py + pc


EDGE = 16
SHIFTED_ROWS = 912
BODY_ROWS = SHIFTED_ROWS - 2 * EDGE


def _first_tile_row(blk, px):
    return EDGE * (56 * blk + px)


def _edge_tiles():
    tiles = {}
    for blk in range(N_DEV):
        first = _first_tile_row(blk, blk // 4)
        tiles.setdefault(first, []).append((blk, 0))
        tiles.setdefault(first + EDGE + BODY_ROWS, []).append((blk, 1))
    return tiles


def _peer_copies(srcs, outs, send_sems, recv_sems):
    x, y, c = _position()
    me = _blk(x, y, c)
    copies = []
    for a, (src, out) in enumerate(zip(srcs, outs)):
        k = 0
        for dx in (0, 1):
            for dy in (0, 1):
                for dc in (0, 1):
                    if dx + dy + dc == 0:
                        continue
                    peer = (1 - x if dx else x, 1 - y if dy else y, 1 - c if dc else c)
                    copies.append(pltpu.make_async_remote_copy(
                        src_ref=src, dst_ref=out.at[me], send_sem=send_sems.at[a * 7 + k],
                        recv_sem=recv_sems.at[a * 7 + k], device_id=peer, device_id_type=MESH))
                    k += 1
    return copies


def _chip_copies(ins, outs, send_sems, recv_sems):
    x, y, c = _position()
    chips = [(1 - x, y), (x, 1 - y), (1 - x, 1 - y)]
    copies = []
    for a in range(len(ins)):
        for j, (px, py) in enumerate(chips):
            copies.append(pltpu.make_async_remote_copy(
                src_ref=ins[a].at[2 * px + py], dst_ref=outs[a].at[j],
                send_sem=send_sems.at[a * 3 + j], recv_sem=recv_sems.at[a * 3 + j],
                device_id=(px, py, c), device_id_type=MESH))
    return copies


WINDOW_ROWS = SHARD_W + 4


def _window_start(k, parity):
    return 2 * SHARD_W * k + (SHARD_W - 4) * parity


def _owner_block(part, k, parity):
    if part.ndim == 3:
        return part.at[2 * k + parity]
    return part.at[pl.ds(pl.multiple_of(_window_start(k, parity), 8), WINDOW_ROWS)]


def _block_shape(part):
    return part.shape[1:] if part.ndim == 3 else (WINDOW_ROWS, part.shape[1])


def _sibling_copies(part, out, send_sems, recv_sems):
    x, y, c = _position()
    return [pltpu.make_async_remote_copy(src_ref=_owner_block(part, k, 1 - c), dst_ref=out.at[k],
                                         send_sem=send_sems.at[k], recv_sem=recv_sems.at[k],
                                         device_id=(x, y, 1 - c), device_id_type=MESH)
            for k in range(4)]


def _start_all(copies):
    for cp in copies:
        cp.start()


def _wait_all(copies):
    for cp in copies:
        cp.wait_recv()
    for cp in copies:
        cp.wait_send()


def _chip_sums(part, from_sibling, core, tc, name, riding=None):
    rows, cols = _block_shape(part)
    nj = cols // tc

    def body(core_ref, p_ref, s_ref, *rest):
        if riding is None:
            (o_ref,) = rest
        else:
            ride_in, o_ref, ride_out, send_sems, recv_sems = rest
            k, j = pl.program_id(0), pl.program_id(1)

            @pl.when(jnp.logical_and(k == 0, j == 0))
            def _():
                _start_all(_sibling_copies(ride_in, ride_out, send_sems, recv_sems))

        o_ref[0] = (p_ref[...].reshape(rows, tc) + s_ref[0]).astype(BF16)

        if riding is not None:
            @pl.when(jnp.logical_and(k == 3, j == nj - 1))
            def _():
                _wait_all(_sibling_copies(ride_in, ride_out, send_sems, recv_sems))

    hbm = pl.BlockSpec(memory_space=pl.ANY)
    sums = jax.ShapeDtypeStruct((4, rows, cols), BF16)
    tile_out = pl.BlockSpec((1, rows, tc), lambda k, j, core_ref: (k, 0, j))
    if part.ndim == 3:
        mine = pl.BlockSpec((1, rows, tc), lambda k, j, core_ref: (2 * k + core_ref[0], 0, j))
    else:
        mine = pl.BlockSpec((pl.Element(rows), pl.Element(tc)),
                            lambda k, j, core_ref: (pl.multiple_of(_window_start(k, core_ref[0]), 8),
                                                    pl.multiple_of(j * tc, 128)))
    in_specs = [mine, pl.BlockSpec((1, rows, tc), lambda k, j, core_ref: (k, 0, j))]
    if riding is None:
        out_shape, out_specs, scratch, args = sums, tile_out, [], (core, part, from_sibling)
    else:
        out_shape = (sums, jax.ShapeDtypeStruct((4,) + _block_shape(riding), F32))
        out_specs, in_specs = (tile_out, hbm), in_specs + [hbm]
        scratch = [pltpu.SemaphoreType.DMA((4,)), pltpu.SemaphoreType.DMA((4,))]
        args = (core, part, from_sibling, riding)
    return pl.pallas_call(
        body, name=name, out_shape=out_shape,
        grid_spec=pltpu.PrefetchScalarGridSpec(num_scalar_prefetch=1, grid=(4, nj), in_specs=in_specs,
                                               out_specs=out_specs, scratch_shapes=scratch),
        compiler_params=_cparams("arbitrary", "arbitrary"),
    )(*args)


def _sum_chips(s_ref, r_ref):
    f = lambda a: a.astype(F32)
    return ((f(s_ref[0]) + f(r_ref[0])) + f(r_ref[1])) + f(r_ref[2])


def _final_sum(sums, from_chips, chip, small, tc, name):
    _, rows, cols = sums.shape
    nj = cols // tc

    def body(chip_ref, s_ref, r_ref, sm_ref, g_out, tot_ref, all_ref, send_sems, recv_sems):
        j = pl.program_id(0)
        me = _blk(*_position())

        @pl.when(j == 0)
        def _():
            all_ref[me] = sm_ref[...]
            _start_all(_peer_copies((all_ref.at[me],), (all_ref,), send_sems, recv_sems))

        g_out[...] = _sum_chips(s_ref, r_ref)

        @pl.when(j == nj - 1)
        def _():
            _wait_all(_peer_copies((all_ref.at[me],), (all_ref,), send_sems, recv_sems))
            acc = all_ref[0]
            for d in range(1, N_DEV):
                acc = acc + all_ref[d]
            tot_ref[...] = acc

    whole = pl.BlockSpec(small.shape, lambda j, chip_ref: (0, 0))
    return pl.pallas_call(
        body, name=name,
        out_shape=(jax.ShapeDtypeStruct((rows, cols), F32), jax.ShapeDtypeStruct(small.shape, F32)),
        grid_spec=pltpu.PrefetchScalarGridSpec(
            num_scalar_prefetch=1, grid=(nj,),
            in_specs=[pl.BlockSpec((1, rows, tc), lambda j, chip_ref: (chip_ref[0], 0, j)),
                      pl.BlockSpec((3, rows, tc), lambda j, chip_ref: (0, 0, j)), whole],
            out_specs=(pl.BlockSpec((rows, tc), lambda j, chip_ref: (0, j)), whole),
            scratch_shapes=[pltpu.VMEM((N_DEV,) + small.shape, F32), pltpu.SemaphoreType.DMA((7,)),
                            pltpu.SemaphoreType.DMA((7,))]),
        compiler_params=_cparams("arbitrary"),
    )(chip, sums, from_chips, small)


def _adamw_rows(g, w, m, v, tr, name):
    rows = g.shape[0]

    def body(g_ref, w_ref, m_ref, v_ref, d_out, m_out, v_out):
        delta, m_new, v_new = _adamw(w_ref[...], g_ref[...], m_ref[...], v_ref[...])
        d_out[...] = delta
        m_out[...] = m_new
        v_out[...] = v_new

    tile = pl.BlockSpec((tr,) + g.shape[1:], lambda r: (r, 0, 0))
    shp = jax.ShapeDtypeStruct(g.shape, F32)
    return pl.pallas_call(
        body, name=name, out_shape=(shp, shp, shp), grid=(rows // tr,),
        in_specs=[tile] * 4, out_specs=(tile, tile, tile),
        compiler_params=_cparams("arbitrary"),
    )(g, w, m, v)


def _adamw(w, g, m, v):
    m = ADAM_B1 * m + (1.0 - ADAM_B1) * g
    v = ADAM_B2 * v + (1.0 - ADAM_B2) * (g * g)
    m_hat = m / (1.0 - ADAM_B1 ** ADAM_STEP)
    v_hat = v / (1.0 - ADAM_B2 ** ADAM_STEP)
    delta = -ADAM_LR * (m_hat / (jnp.sqrt(v_hat) + ADAM_EPS) + ADAM_WD * w)
    return delta, m, v


def _final_sum_adamw(sums, from_chips, chip, w, m, v, tr, name):
    rows, cols = w.shape

    def body(chip_ref, s_ref, r_ref, w_ref, m_ref, v_ref, g_out, d_out, m_out, v_out):
        g = _sum_chips(s_ref, r_ref)
        delta, m_new, v_new = _adamw(w_ref[...], g, m_ref[...], v_ref[...])
        g_out[...] = g
        d_out[...] = delta
        m_out[...] = m_new
        v_out[...] = v_new

    tile = pl.BlockSpec((tr, cols), lambda r, chip_ref: (r, 0))
    shp = jax.ShapeDtypeStruct((rows, cols), F32)
    return pl.pallas_call(
        body, name=name,
        out_shape=(shp, shp, shp, shp),
        grid_spec=pltpu.PrefetchScalarGridSpec(
            num_scalar_prefetch=1, grid=(rows // tr,),
            in_specs=[pl.BlockSpec((1, tr, cols), lambda r, chip_ref: (chip_ref[0], r, 0)),
                      pl.BlockSpec((3, tr, cols), lambda r, chip_ref: (0, r, 0)),
                      tile, tile, tile],
            out_specs=(tile, tile, tile, tile)),
        compiler_params=_cparams("arbitrary"),
    )(chip, sums, from_chips, w, m, v)


def _adamw_small(g, w, m, v):
    def body(g_ref, w_ref, m_ref, v_ref, d_out, m_out, v_out):
        delta, m_new, v_new = _adamw(w_ref[...], g_ref[...], m_ref[...], v_ref[...])
        d_out[...] = delta
        m_out[...] = m_new
        v_out[...] = v_new

    vmem = pl.BlockSpec(memory_space=pltpu.VMEM)
    shp = jax.ShapeDtypeStruct(g.shape, F32)
    return pl.pallas_call(body, name="adamw_small", out_shape=(shp, shp, shp),
                          in_specs=[vmem] * 4, out_specs=(vmem, vmem, vmem))(g, w, m, v)


TILE_ROWS = (0, 1024, NAT_ZA, NAT_B, NAT_ZC, NAT_C, NAT_C + CONV_W)


TILE_ORDER = ((0, 1, 2, 3, 5, 6, 4), (2, 0, 1, 4, 3, 5, 6), (5, 0, 6, 4, 1, 2, 3), (4, 2, 3, 5, 6, 0, 1))
NEIGHBOUR_SWEEP, DIAGONAL_SWEEP = 1, 4


def _gather_inproj(x2d, norm_g, shifted, w_out_s, small_s, order, tm):
    seq = x2d.shape[0]
    tn = CONV_W
    ni, nj = seq // tm, MAIN_W // tn
    first_sweep = lambda j, i, order_ref: jnp.where(j == 0, i, ni - 1)
    last_sweep = lambda j, i, order_ref: jnp.where(j == nj - 1, i, 0)
    edge_tiles = _edge_tiles()

    def body(order_ref, x_ref, g_ref, sh_ref, wout_ref, sm_ref, proj_ref, lr_ref, ht_ref, w_nat, wout_all, sm_all,
             w_all, h_all, edges, wout_b, sm_b, send_sems, recv_sems, peer_send, peer_recv, local_sems):
        j, i = pl.program_id(0), pl.program_id(1)
        rows = pl.ds(pl.multiple_of(i * tm, tm), tm)
        x, y, c = _position()
        me, here, sibling = _blk(x, y, c), (x, y, c), (x, y, 1 - c)
        chips = [(1 - x, y), (x, 1 - y), (1 - x, 1 - y)]

        def pieces(px, py, pc):
            blk = _blk(px, py, pc)
            body_rows = pl.ds(pl.multiple_of(_first_tile_row(blk, px) + EDGE, EDGE), BODY_ROWS)
            return [w_all.at[body_rows], edges.at[blk]]

        def copy(a, k, block, to):
            ref = pieces(*block)[a]
            return pltpu.make_async_remote_copy(src_ref=ref, dst_ref=ref, send_sem=send_sems.at[a * 7 + k],
                                                recv_sem=recv_sems.at[a * 7 + k], device_id=to, device_id_type=MESH)

        def own_copies():
            targets = [(0, sibling)] + [(1 + n, (*chip, c)) for n, chip in enumerate(chips)]
            return [copy(a, k, here, to) for k, to in targets for a in range(2)]

        def forwards(n):
            return [copy(a, 4 + n, (*chips[n], c), sibling) for a in range(2)]

        def small_copies():
            mine = [pltpu.make_async_copy(wout_b, wout_all.at[me], local_sems.at[0]),
                    pltpu.make_async_copy(sm_b, sm_all.at[me], local_sems.at[1])]
            return mine, _peer_copies((wout_b, sm_b), (wout_all, sm_all), peer_send, peer_recv)

        def arrive(ns):
            for n in ns:
                for a in range(2):
                    copy(a, 1 + n, (*chips[n], c), here).wait_recv()
                _start_all(forwards(n))
            for n in ns:
                for a in range(2):
                    copy(a, 4 + n, (*chips[n], 1 - c), here).wait_recv()

        def add_edge_tiles(stage):
            for row, parts in edge_tiles.items():
                ready = 0
                for blk, _ in parts:
                    away = (x != blk // 4).astype(jnp.int32) + (y != (blk // 2) % 2).astype(jnp.int32)
                    ready = jnp.maximum(ready, away)

                @pl.when(ready == stage)
                def _(row=row, parts=parts):
                    tile = edges[parts[0][0], parts[0][1]].astype(F32)
                    for blk, side in parts[1:]:
                        tile = tile + edges[blk, side].astype(F32)
                    w_all[row:row + EDGE, :] = tile.astype(BF16)

        @pl.when(jnp.logical_and(j == 0, i == 0))
        def _():
            pieces(*here)[0][...] = sh_ref[EDGE:EDGE + BODY_ROWS, :].astype(BF16)
            edges[me, 0] = sh_ref[0:EDGE, :].astype(BF16)
            edges[me, 1] = sh_ref[EDGE + BODY_ROWS:, :].astype(BF16)
            _start_all(own_copies())
            wout_b[...] = wout_ref[...].astype(BF16)
            sm_b[...] = sm_ref[...]
            mine, remote = small_copies()
            _start_all(mine + remote)
            for a in range(2):
                copy(a, 0, sibling, here).wait_recv()
            add_edge_tiles(0)

        @pl.when(jnp.logical_and(j == NEIGHBOUR_SWEEP, i == 0))
        def _():
            arrive((0, 1))
            add_edge_tiles(1)

        @pl.when(jnp.logical_and(j == DIAGONAL_SWEEP, i == 0))
        def _():
            arrive((2,))
            add_edge_tiles(2)

        @pl.when(j == 0)
        def _():
            xv = x_ref[...]
            r = lax.rsqrt(jnp.mean(xv * xv, axis=-1, keepdims=True) + EPS)
            h = (xv * r) * g_ref[...]
            h_all[rows, :] = h.astype(BF16)
            ht_ref[...] = h.T.astype(BF16)

        tile = order_ref[j]
        row = 0
        for k, start in enumerate(TILE_ROWS):
            row = row + jnp.where(tile == k, start // 32, 0)
        w_tile = w_all[pl.ds(pl.multiple_of(row * 32, 32), tn), :]
        proj_ref[...] = _dot_nt(h_all[rows, :], w_tile).astype(BF16)

        @pl.when(j == nj - 1)
        def _():
            lr_ref[...] = _dot_nt(h_all[rows, :], w_all[NAT_LR:NAT_LR + LR_W, :])

        @pl.when(jnp.logical_and(j == nj - 1, i == ni - 1))
        def _():
            for cp in own_copies() + forwards(0) + forwards(1) + forwards(2):
                cp.wait_send()
            mine, remote = small_copies()
            _wait_all(remote)
            for cp in mine:
                cp.wait()
            keep = pltpu.make_async_copy(w_all, w_nat, local_sems.at[2])
            keep.start()
            keep.wait()

    const = lambda shape: pl.BlockSpec(shape, lambda j, i, order_ref: (0,) * len(shape))
    hbm = pl.BlockSpec(memory_space=pl.ANY)
    vmem = pl.BlockSpec(memory_space=pltpu.VMEM)
    return pl.pallas_call(
        body, name="gather_inproj",
        out_shape=(jax.ShapeDtypeStruct((seq, MAIN_W), BF16), jax.ShapeDtypeStruct((seq, LR_W), F32),
                   jax.ShapeDtypeStruct((D_MODEL, seq), BF16), jax.ShapeDtypeStruct((IN_W, D_MODEL), BF16),
                   jax.ShapeDtypeStruct((N_DEV,) + w_out_s.shape, BF16),
                   jax.ShapeDtypeStruct((N_DEV,) + small_s.shape, F32)),
        grid_spec=pltpu.PrefetchScalarGridSpec(
            num_scalar_prefetch=1, grid=(nj, ni),
            in_specs=[pl.BlockSpec((tm, D_MODEL), lambda j, i, order_ref: (first_sweep(j, i, order_ref), 0)),
                      const((1, D_MODEL)), vmem, const(w_out_s.shape), const(small_s.shape)],
            out_specs=(pl.BlockSpec((tm, tn), lambda j, i, order_ref: (i, order_ref[j])),
                       pl.BlockSpec((tm, LR_W), lambda j, i, order_ref: (last_sweep(j, i, order_ref), 0)),
                       pl.BlockSpec((D_MODEL, tm), lambda j, i, order_ref: (0, first_sweep(j, i, order_ref))),
                       hbm, hbm, hbm),
            scratch_shapes=[pltpu.VMEM((IN_W, D_MODEL), BF16), pltpu.VMEM((seq, D_MODEL), BF16),
                            pltpu.VMEM((N_DEV, 2, EDGE, D_MODEL), BF16),
                            pltpu.VMEM(w_out_s.shape, BF16), pltpu.VMEM(small_s.shape, F32),
                            pltpu.SemaphoreType.DMA((14,)), pltpu.SemaphoreType.DMA((14,)),
                            pltpu.SemaphoreType.DMA((14,)), pltpu.SemaphoreType.DMA((14,)),
                            pltpu.SemaphoreType.DMA((3,))]),
        compiler_params=_cparams("arbitrary", "arbitrary"),
    )(order, x2d, norm_g, shifted, w_out_s, small_s)


def _block_masks(tt):
    row = lax.broadcasted_iota(jnp.int32, (tt, tt), 0)
    col = lax.broadcasted_iota(jnp.int32, (tt, tt), 1)
    same = jnp.right_shift(row, 6) == jnp.right_shift(col, 6)
    return (jnp.logical_and(same, col <= row), jnp.logical_and(same, col >= row), jnp.logical_and(same, col > row))


def _dot_split3(ones_mat, x):
    x1 = x.astype(BF16)
    r1 = x - x1.astype(F32)
    x2 = r1.astype(BF16)
    x3 = (r1 - x2.astype(F32)).astype(BF16)
    return (_dot(ones_mat, x3) + _dot(ones_mat, x2)) + _dot(ones_mat, x1)


def _log_gate(logits):
    return (jnp.minimum(logits, 0.0) - jnp.log(1.0 + jnp.exp(-jnp.abs(logits)))) * GATE_SCALE


def _chunk_column_mask(tt):
    nc = tt // CHUNK
    row = lax.broadcasted_iota(jnp.int32, (tt, nc * DK), 0)
    col = lax.broadcasted_iota(jnp.int32, (tt, nc * DK), 1)
    return jnp.right_shift(row, 6) == jnp.right_shift(col, 7)


def _chunked(mask, x, nc):
    wide = jnp.concatenate([x] * nc, axis=1)
    return jnp.where(mask, wide, jnp.zeros_like(wide))


def _gla_fwd(proj, lr, wgk_f, wgk_b, bgk_f, bgk_b, tt):
    seq = proj.shape[0]
    nb, nc, nch = seq // tt, tt // CHUNK, seq // CHUNK

    def body(qf, kf, vf, lrf, qb, kb, vb, lrb, wf, wb, bf, bb, of, ob, stf, stb, s_scr, qs_s, ks_s, qin_s, kout_s):
        @pl.when(pl.program_id(0) == 0)
        def _():
            s_scr[...] = jnp.zeros(s_scr.shape, F32)

        low, upp, sup = _block_masks(tt)
        dirs = ((qf, kf, vf, lrf, wf, bf, of, stf, low, low, REF_F, LAST_F, list(range(nc))),
                (qb, kb, vb, lrb, wb, bb, ob, stb, upp, sup, REF_B, LAST_B, list(reversed(range(nc)))))
        for d, (q_r, k_r, v_r, lr_r, w_r, b_r, o_r, st_r, cum, mask, ref, last, order) in enumerate(dirs):
            logits = _dot(lr_r[...].astype(BF16), w_r[...]) + b_r[...]
            b = _dot_split3(cum.astype(BF16), _log_gate(logits))
            decs = []
            for c in range(nc):
                rows = slice(c * CHUNK, (c + 1) * CHUNK)
                bc = b[rows]
                b_ref, b_last = bc[ref:ref + 1], bc[last:last + 1]
                qc = q_r[rows, :].astype(F32) * QSCALE
                kc = k_r[rows, :].astype(F32)
                qs_s[rows, :] = (qc * jnp.exp(bc - b_ref)).astype(BF16)
                ks_s[rows, :] = (kc * jnp.exp(b_ref - bc)).astype(BF16)
                qin_s[rows, :] = (qc * jnp.exp(bc)).astype(BF16)
                kout_s[rows, :] = (kc * jnp.exp(b_last - bc)).astype(BF16)
                decs.append(jnp.exp(b_last))
            for h in range(HEADS):
                ksl = slice(h * DK, (h + 1) * DK)
                vsl = slice(h * DV, (h + 1) * DV)
                v = v_r[:, vsl].astype(BF16)
                att = jnp.where(mask, _dot_nt(qs_s[:, ksl], ks_s[:, ksl]), 0.0).astype(BF16)
                o_intra = _dot(att, v)
                st = s_scr[d * HEADS + h]
                for c in order:
                    rows = slice(c * CHUNK, (c + 1) * CHUNK)
                    stb = st.astype(BF16)
                    st_r[c, h] = stb
                    o_r[rows, vsl] = (o_intra[rows] + _dot_nt(qin_s[rows, ksl], stb)).astype(BF16)
                    st = st * decs[c][:, ksl] + _dot_tn(v[rows], kout_s[rows, ksl])
                s_scr[d * HEADS + h] = st

    fw = lambda i: (i, 0)
    bw = lambda i: (nb - 1 - i, 0)
    const = lambda i: (0, 0)

    def tok_specs(m):
        return [pl.BlockSpec((tt, QK_W), lambda i: (m(i)[0], OFF_Q // QK_W)),
                pl.BlockSpec((tt, QK_W), lambda i: (m(i)[0], OFF_K // QK_W)),
                pl.BlockSpec((tt, V_W), lambda i: (m(i)[0], OFF_V // V_W)),
                pl.BlockSpec((tt, LR_W), m)]

    st_shape = jax.ShapeDtypeStruct((nch, HEADS, DV, DK), BF16)
    o_shape = jax.ShapeDtypeStruct((seq, V_W), BF16)
    operand = pltpu.VMEM((tt, QK_W), BF16)
    return pl.pallas_call(
        body, name="gla_fwd",
        out_shape=(o_shape, o_shape, st_shape, st_shape),
        grid=(nb,),
        in_specs=tok_specs(fw) + tok_specs(bw) + [
            pl.BlockSpec((LR_W, QK_W), const), pl.BlockSpec((LR_W, QK_W), const),
            pl.BlockSpec((1, QK_W), const), pl.BlockSpec((1, QK_W), const)],
        out_specs=(pl.BlockSpec((tt, V_W), fw), pl.BlockSpec((tt, V_W), bw),
                   pl.BlockSpec((nc, HEADS, DV, DK), lambda i: (i, 0, 0, 0)),
                   pl.BlockSpec((nc, HEADS, DV, DK), lambda i: (nb - 1 - i, 0, 0, 0))),
        scratch_shapes=[pltpu.VMEM((2 * HEADS, DV, DK), F32), operand, operand, operand, operand],
        compiler_params=_cparams("arbitrary"),
    )(proj, proj, proj, lr, proj, proj, proj, lr, wgk_f, wgk_b, bgk_f, bgk_b)


def _head_norm(o, gain):
    outs, rinv = [], []
    for h in range(HEADS):
        oh = o[:, h * DV:(h + 1) * DV]
        r = lax.rsqrt(jnp.mean(oh * oh, axis=-1, keepdims=True) + EPS)
        outs.append((oh * r) * gain)
        rinv.append(r)
    return jnp.concatenate(outs, axis=1), rinv


def _shift_rows(u, prev_row, next_row):
    n = u.shape[0]
    row = lax.broadcasted_iota(jnp.int32, (n, 1), 0)
    up = jnp.where(row == 0, prev_row, pltpu.roll(u, 1, 0))
    un = jnp.where(row == n - 1, next_row, pltpu.roll(u, n - 1, 0))
    return up, un


HALO = 16


def _halo_specs(tm, seq, col_block):
    per = tm // HALO
    last = seq // HALO - 1
    return [pl.BlockSpec((HALO, CONV_W), lambda i: (jnp.maximum(i * per - 1, 0), col_block)),
            pl.BlockSpec((HALO, CONV_W), lambda i: (jnp.minimum((i + 1) * per, last), col_block))]


def _f32(ref):
    return ref[...].astype(F32)


def _last_row(ref):
    return ref[HALO - 1:HALO, :].astype(F32)


def _first_row(ref):
    return ref[0:1, :].astype(F32)


def _mix_out_loss(o_f, o_b, proj, x2d, tgt, gla_g, conv_w, conv_b, w_out, final_g, tm):
    seq = x2d.shape[0]
    nt = seq // tm

    def body(of, ob, za, bg, cg, hc, zc, cprev, cnext, hprev, hnext, x_ref, t_ref, gg, cw, cb, wo, fg,
             yt_ref, conv_ref, dx2_ref, dx2b_ref, loss_ref, dfg_ref):
        i = pl.program_id(0)

        @pl.when(i == 0)
        def _():
            loss_ref[...] = jnp.zeros(loss_ref.shape, F32)
            dfg_ref[...] = jnp.zeros(dfg_ref.shape, F32)

        on, _ = _head_norm(_f32(of) + _f32(ob), gg[...])
        zav = _f32(za)
        y_a = on * (zav * _sigmoid(zav))
        u = _f32(cg) * _f32(hc)
        prev_row = jnp.where(i > 0, _last_row(cprev) * _last_row(hprev), 0.0)
        next_row = jnp.where(i < nt - 1, _first_row(cnext) * _first_row(hnext), 0.0)
        up, un = _shift_rows(u, prev_row, next_row)
        conv = (cw[0:1, :] * up + cw[1:2, :] * u + cw[2:3, :] * un) + cb[...]
        conv_ref[...] = conv.astype(BF16)
        zcv = _f32(zc)
        y_c = _f32(bg) * conv * (zcv * _sigmoid(zcv))
        y = jnp.concatenate([y_a, y_c], axis=1)
        yt_ref[...] = y.T.astype(BF16)
        x2 = x_ref[...] + _dot(y.astype(BF16), wo[...])
        r = lax.rsqrt(jnp.mean(x2 * x2, axis=-1, keepdims=True) + EPS)
        xn = x2 * r
        err = xn * fg[...] - t_ref[...]
        loss_ref[...] += 0.5 * jnp.sum(jnp.mean(err * err, axis=-1, keepdims=True))
        dyf = err * (1.0 / D_MODEL)
        dfg_ref[...] += jnp.sum(dyf * xn, axis=0, keepdims=True)
        dxn = dyf * fg[...]
        dx2 = r * dxn - xn * (r * jnp.mean(dxn * xn, axis=-1, keepdims=True))
        dx2_ref[...] = dx2
        dx2b_ref[...] = dx2.astype(BF16)

    def col(off):
        return pl.BlockSpec((tm, CONV_W), lambda i: (i, off // CONV_W))

    rowt = pl.BlockSpec((tm, D_MODEL), lambda i: (i, 0))
    const = lambda shape: pl.BlockSpec(shape, lambda i: (0, 0))
    return pl.pallas_call(
        body, name="mix_out_loss",
        out_shape=(jax.ShapeDtypeStruct((MIX_W, seq), BF16), jax.ShapeDtypeStruct((seq, CONV_W), BF16),
                   jax.ShapeDtypeStruct((seq, D_MODEL), F32), jax.ShapeDtypeStruct((seq, D_MODEL), BF16),
                   jax.ShapeDtypeStruct((8, 128), F32), jax.ShapeDtypeStruct((1, D_MODEL), F32)),
        grid=(nt,),
        in_specs=[rowt, rowt, col(OFF_ZA), col(OFF_B), col(OFF_C), col(OFF_H), col(OFF_ZC)]
        + _halo_specs(tm, seq, OFF_C // CONV_W) + _halo_specs(tm, seq, OFF_H // CONV_W)
        + [rowt, rowt, const((1, DV)), const((8, CONV_W)), const((1, CONV_W)), const((MIX_W, D_MODEL)),
           const((1, D_MODEL))],
        out_specs=(pl.BlockSpec((MIX_W, tm), lambda i: (0, i)), rowt, rowt, rowt, const((8, 128)),
                   const((1, D_MODEL))),
        compiler_params=_cparams("arbitrary"),
    )(o_f, o_b, proj, proj, proj, proj, proj, proj, proj, proj, proj, x2d, tgt, gla_g, conv_w, conv_b, w_out, final_g)


def _dsilu(z, s):
    return s * (1.0 + z * (1.0 - s))


def _mix_bwd(dx2b, o_f, o_b, proj, conv, gla_g, w_out, tm):
    seq = dx2b.shape[0]

    def body(dx, of, ob, za, bg, zc, cv, gg, wo, dg_ref, do_ref, dconv_ref, dgg_ref, dcb_ref):
        @pl.when(pl.program_id(0) == 0)
        def _():
            dgg_ref[...] = jnp.zeros(dgg_ref.shape, F32)
            dcb_ref[...] = jnp.zeros(dcb_ref.shape, F32)

        dy = _dot_nt(dx[...], wo[...])
        dy_a, dy_c = dy[:, :V_W], dy[:, V_W:]
        zcv, bgv, convv = _f32(zc), _f32(bg), _f32(cv)
        sc = _sigmoid(zcv)
        szc = zcv * sc
        dg_ref[:, CONV_W:2 * CONV_W] = (dy_c * convv * szc).astype(BF16)
        dconv = dy_c * bgv * szc
        dconv_ref[...] = dconv.astype(BF16)
        dcb_ref[...] += jnp.sum(dconv, axis=0, keepdims=True)
        dg_ref[:, 2 * CONV_W:] = (dy_c * bgv * convv * _dsilu(zcv, sc)).astype(BF16)

        o = _f32(of) + _f32(ob)
        gain = gg[...]
        on, rinv = _head_norm(o, gain)
        zav = _f32(za)
        sa = _sigmoid(zav)
        dg_ref[:, :CONV_W] = (dy_a * on * _dsilu(zav, sa)).astype(BF16)
        don = dy_a * (zav * sa)
        dgg = jnp.zeros((1, DV), F32)
        dos = []
        for h in range(HEADS):
            sl = slice(h * DV, (h + 1) * DV)
            oh, r, dh = o[:, sl], rinv[h], don[:, sl]
            ohn = oh * r
            dgg = dgg + jnp.sum(dh * ohn, axis=0, keepdims=True)
            dn = dh * gain
            dos.append(r * dn - ohn * (r * jnp.mean(dn * ohn, axis=-1, keepdims=True)))
        dgg_ref[...] += dgg
        do_ref[...] = jnp.concatenate(dos, axis=1).astype(BF16)

    def col(off):
        return pl.BlockSpec((tm, CONV_W), lambda i: (i, off // CONV_W))

    rowt = pl.BlockSpec((tm, D_MODEL), lambda i: (i, 0))
    const = lambda shape: pl.BlockSpec(shape, lambda i: (0, 0))
    return pl.pallas_call(
        body, name="mix_bwd",
        out_shape=(jax.ShapeDtypeStruct((seq, GATES_W), BF16), jax.ShapeDtypeStruct((seq, V_W), BF16),
                   jax.ShapeDtypeStruct((seq, CONV_W), BF16),
                   jax.ShapeDtypeStruct((1, DV), F32), jax.ShapeDtypeStruct((1, CONV_W), F32)),
        grid=(seq // tm,),
        in_specs=[rowt, rowt, rowt, col(OFF_ZA), col(OFF_B), col(OFF_ZC), rowt, const((1, DV)),
                  const((MIX_W, D_MODEL))],
        out_specs=(pl.BlockSpec((tm, GATES_W), lambda i: (i, 0)), rowt, rowt, const((1, DV)), const((1, CONV_W))),
        compiler_params=_cparams("arbitrary"),
    )(dx2b, o_f, o_b, proj, proj, proj, conv, gla_g, w_out)


def _conv_bwd(dconv, proj, conv_w, tm):
    seq = dconv.shape[0]
    nt = seq // tm

    def body(dc_in, dprev, dnext, cg, hc, cprev, cnext, hprev, hnext, cw, dch_ref, dcw_ref):
        i = pl.program_id(0)

        @pl.when(i == 0)
        def _():
            dcw_ref[...] = jnp.zeros(dcw_ref.shape, F32)

        first, lastt = i > 0, i < nt - 1
        dcv = _f32(dc_in)
        d_up, d_un = _shift_rows(dcv, jnp.where(first, _last_row(dprev), 0.0), jnp.where(lastt, _first_row(dnext), 0.0))
        cgv, hcv = _f32(cg), _f32(hc)
        u = cgv * hcv
        u_up, u_un = _shift_rows(u, jnp.where(first, _last_row(cprev) * _last_row(hprev), 0.0),
                                 jnp.where(lastt, _first_row(cnext) * _first_row(hnext), 0.0))
        du = cw[0:1, :] * d_un + cw[1:2, :] * dcv + cw[2:3, :] * d_up
        dch_ref[:, :CONV_W] = (du * hcv).astype(BF16)
        dch_ref[:, CONV_W:] = (du * cgv).astype(BF16)
        dcw_ref[0:1, :] += jnp.sum(dcv * u_up, axis=0, keepdims=True)
        dcw_ref[1:2, :] += jnp.sum(dcv * u, axis=0, keepdims=True)
        dcw_ref[2:3, :] += jnp.sum(dcv * u_un, axis=0, keepdims=True)

    def col(off):
        return pl.BlockSpec((tm, CONV_W), lambda i: (i, off // CONV_W))

    rowt = pl.BlockSpec((tm, CONV_W), lambda i: (i, 0))
    const = lambda shape: pl.BlockSpec(shape, lambda i: (0, 0))
    return pl.pallas_call(
        body, name="conv_bwd",
        out_shape=(jax.ShapeDtypeStruct((seq, CH_W), BF16), jax.ShapeDtypeStruct((8, CONV_W), F32)),
        grid=(nt,),
        in_specs=[rowt] + _halo_specs(tm, seq, 0) + [col(OFF_C), col(OFF_H)]
        + _halo_specs(tm, seq, OFF_C // CONV_W) + _halo_specs(tm, seq, OFF_H // CONV_W) + [const((8, CONV_W))],
        out_specs=(pl.BlockSpec((tm, CH_W), lambda i: (i, 0)), const((8, CONV_W))),
        compiler_params=_cparams("arbitrary"),
    )(dconv, dconv, dconv, proj, proj, proj, proj, proj, proj, conv_w)


def _gla_bwd(proj, lr, do, st_f, st_b, wgk_f, wgk_b, bgk_f, bgk_b, tt):
    seq = proj.shape[0]
    nb, nc = seq // tt, tt // CHUNK

    def body(qf, kf, vf, lrf, dof, stf, qb, kb, vb, lrb, dob, stb, wf, wb, bf, bb,
             dqkv_f, dlr_f, dqkv_b, dlr_b, dwf, dwb, dbf, dbb,
             ds_scr, eq_s, ek_s, ein_s, eout_s, qs_s, ks_s, qin_s, kout_s, db_s, lg_s):
        @pl.when(pl.program_id(0) == 0)
        def _():
            ds_scr[...] = jnp.zeros(ds_scr.shape, F32)
            for r in (dwf, dwb, dbf, dbb):
                r[...] = jnp.zeros(r.shape, F32)

        low, upp, sup = _block_masks(tt)
        row = lax.broadcasted_iota(jnp.int32, (CHUNK, 1), 0)
        kmask = _chunk_column_mask(tt)
        dirs = ((qf, kf, vf, lrf, dof, stf, wf, bf, dqkv_f, dlr_f, dwf, dbf,
                 low, upp, low, REF_F, LAST_F, list(reversed(range(nc)))),
                (qb, kb, vb, lrb, dob, stb, wb, bb, dqkv_b, dlr_b, dwb, dbb,
                 upp, low, sup, REF_B, LAST_B, list(range(nc))))
        for d, (q_r, k_r, v_r, lr_r, do_r, st_r, w_r, b_r, dqkv_r, dlr_r, dw_r, db_r,
                cum, cum_t, mask, ref, last, order) in enumerate(dirs):
            lrv = lr_r[...].astype(BF16)
            wv = w_r[...]
            logits = _dot(lrv, wv) + b_r[...]
            lg_s[...] = logits
            b = _dot_split3(cum.astype(BF16), _log_gate(logits))
            decs = []
            for c in range(nc):
                rows = slice(c * CHUNK, (c + 1) * CHUNK)
                bc = b[rows]
                b_ref, b_last = bc[ref:ref + 1], bc[last:last + 1]
                qc = q_r[rows, :].astype(F32) * QSCALE
                kc = k_r[rows, :].astype(F32)
                e_q, e_k, e_in, e_out = jnp.exp(bc - b_ref), jnp.exp(b_ref - bc), jnp.exp(bc), jnp.exp(b_last - bc)
                eq_s[rows, :], ek_s[rows, :], ein_s[rows, :], eout_s[rows, :] = e_q, e_k, e_in, e_out
                qs_s[rows, :] = (qc * e_q).astype(BF16)
                ks_s[rows, :] = (kc * e_k).astype(BF16)
                qin_s[rows, :] = (qc * e_in).astype(BF16)
                kout_s[rows, :] = (kc * e_out).astype(BF16)
                decs.append(jnp.exp(b_last))
            for h in range(HEADS):
                ksl = slice(h * DK, (h + 1) * DK)
                vsl = slice(h * DV, (h + 1) * DV)
                v = v_r[:, vsl].astype(BF16)
                dov = do_r[:, vsl].astype(BF16)
                qsb, ksb = qs_s[:, ksl], ks_s[:, ksl]
                att = jnp.where(mask, _dot_nt(qsb, ksb), 0.0).astype(BF16)
                datt = jnp.where(mask, _dot_nt(dov, v), 0.0).astype(BF16)
                dqs = _dot(datt, ksb)
                dks = _dot_tn(datt, qsb)
                dv_intra = _dot_tn(att, dov)
                g_t = _dot_tn(dov, _chunked(kmask, qin_s[:, ksl], nc))
                ds = ds_scr[d * HEADS + h]
                for c in order:
                    rows = slice(c * CHUNK, (c + 1) * CHUNK)
                    dsb = ds.astype(BF16)
                    s_prev = st_r[c, h]
                    dk_out = _dot(v[rows], dsb)
                    dq_in = _dot(dov[rows], s_prev)
                    dv = dv_intra[rows] + _dot_nt(kout_s[rows, ksl], dsb)
                    dqkv_r[rows, OFF_V + h * DV:OFF_V + (h + 1) * DV] = dv.astype(BF16)
                    dec = decs[c][:, ksl]
                    ddec = jnp.sum(ds * s_prev.astype(F32), axis=0, keepdims=True)
                    e_out = eout_s[rows, ksl]
                    qc = q_r[rows, ksl].astype(F32) * QSCALE
                    kc = k_r[rows, ksl].astype(F32)
                    dq = dqs[rows] * eq_s[rows, ksl] + dq_in * ein_s[rows, ksl]
                    dk = dks[rows] * ek_s[rows, ksl] + dk_out * e_out
                    dqkv_r[rows, OFF_Q + h * DK:OFF_Q + (h + 1) * DK] = (dq * QSCALE).astype(BF16)
                    dqkv_r[rows, OFF_K + h * DK:OFF_K + (h + 1) * DK] = dk.astype(BF16)
                    tail = jnp.sum(dk_out * (kc * e_out), axis=0, keepdims=True) + ddec * dec
                    db_s[rows, ksl] = (qc * dq - kc * dk) + jnp.where(row == last, tail, 0.0)
                    ds = ds * dec + g_t[:, c * DK:(c + 1) * DK]
                ds_scr[d * HEADS + h] = ds
            dg = _dot_split3(cum_t.astype(BF16), db_s[...])
            dlogit = (dg * GATE_SCALE) * _sigmoid(-lg_s[...])
            dlb = dlogit.astype(BF16)
            dlr_r[...] = _dot_nt(dlb, wv)
            dw_r[...] += _dot_tn(lrv, dlb)
            db_r[...] += jnp.sum(dlogit, axis=0, keepdims=True)

    fw = lambda i: (nb - 1 - i, 0)
    bw = lambda i: (i, 0)
    const = lambda i: (0, 0)

    def tok_specs(m):
        return [pl.BlockSpec((tt, QK_W), lambda i: (m(i)[0], OFF_Q // QK_W)),
                pl.BlockSpec((tt, QK_W), lambda i: (m(i)[0], OFF_K // QK_W)),
                pl.BlockSpec((tt, V_W), lambda i: (m(i)[0], OFF_V // V_W)),
                pl.BlockSpec((tt, LR_W), m),
                pl.BlockSpec((tt, V_W), m),
                pl.BlockSpec((nc, HEADS, DV, DK), lambda i: (m(i)[0], 0, 0, 0))]

    dqkv = jax.ShapeDtypeStruct((seq, QK_W + QK_W + V_W), BF16)
    dlr = jax.ShapeDtypeStruct((seq, LR_W), F32)
    dw = jax.ShapeDtypeStruct((LR_W, QK_W), F32)
    dbias = jax.ShapeDtypeStruct((1, QK_W), F32)
    return pl.pallas_call(
        body, name="gla_bwd",
        out_shape=(dqkv, dlr, dqkv, dlr, dw, dw, dbias, dbias),
        grid=(nb,),
        in_specs=tok_specs(fw) + tok_specs(bw) + [
            pl.BlockSpec((LR_W, QK_W), const), pl.BlockSpec((LR_W, QK_W), const),
            pl.BlockSpec((1, QK_W), const), pl.BlockSpec((1, QK_W), const)],
        out_specs=(pl.BlockSpec((tt, QK_W + QK_W + V_W), fw), pl.BlockSpec((tt, LR_W), fw),
                   pl.BlockSpec((tt, QK_W + QK_W + V_W), bw), pl.BlockSpec((tt, LR_W), bw),
                   pl.BlockSpec((LR_W, QK_W), const), pl.BlockSpec((LR_W, QK_W), const),
                   pl.BlockSpec((1, QK_W), const), pl.BlockSpec((1, QK_W), const)),
        scratch_shapes=[pltpu.VMEM((2 * HEADS, DV, DK), F32)] + [pltpu.VMEM((tt, QK_W), F32)] * 4
        + [pltpu.VMEM((tt, QK_W), BF16)] * 4 + [pltpu.VMEM((tt, QK_W), F32)] * 2,
        compiler_params=_cparams("arbitrary"),
    )(proj, proj, proj, lr, do, st_f, proj, proj, proj, lr, do, st_b, wgk_f, wgk_b, bgk_f, bgk_b)


def _sum_directions(dqkv_f, dqkv_b, dlr_f, dlr_b, tm):
    seq = dqkv_f.shape[0]

    def body(a, b, la, lb, dp_out, dlr_out):
        dp_out[...] = (_f32(a) + _f32(b)).astype(BF16)
        dlr_out[...] = (la[...] + lb[...]).astype(BF16)

    rowt = pl.BlockSpec((tm, QKV_W), lambda i: (i, 0))
    lrt = pl.BlockSpec((tm, LR_W), lambda i: (i, 0))
    return pl.pallas_call(
        body, name="sum_directions",
        out_shape=(jax.ShapeDtypeStruct((seq, QKV_W), BF16), jax.ShapeDtypeStruct((seq, LR_W), BF16)),
        grid=(seq // tm,),
        in_specs=[rowt, rowt, lrt, lrt],
        out_specs=(rowt, lrt),
        compiler_params=_cparams("arbitrary"),
    )(dqkv_f, dqkv_b, dlr_f, dlr_b)


def _input_grad(dp_qkv, dp_gates, dp_ch, dlr, w_nat, x2d, norm_g, dx2, sums, tm):
    seq = x2d.shape[0]
    nt, n = seq // tm, len(sums)

    def body(dq, dg, dc, dl, w, x_ref, g_ref, dx2_ref, *rest):
        ins, (gx_ref, dng_ref), outs = rest[:n], rest[n:n + 2], rest[n + 2:2 * n + 2]
        send_sems, recv_sems = rest[2 * n + 2:]
        i = pl.program_id(0)

        @pl.when(i == 0)
        def _():
            for cp in _chip_copies(ins, outs, send_sems, recv_sems):
                cp.start()
            dng_ref[...] = jnp.zeros(dng_ref.shape, F32)

        dh = (_dot(dl[...], w[NAT_LR:NAT_LR + LR_W, :]) + _dot(dq[...], w[0:NAT_ZA, :])
              + _dot(dg[:, 0:CONV_W], w[NAT_ZA:NAT_LR, :]) + _dot(dg[:, CONV_W:2 * CONV_W], w[NAT_B:NAT_C, :])
              + _dot(dg[:, 2 * CONV_W:], w[NAT_ZC:IN_W, :]) + _dot(dc[...], w[NAT_C:NAT_ZC, :]))
        xv = x_ref[...]
        r = lax.rsqrt(jnp.mean(xv * xv, axis=-1, keepdims=True) + EPS)
        xn = xv * r
        dng_ref[...] += jnp.sum(dh * xn, axis=0, keepdims=True)
        dn = dh * g_ref[...]
        gx_ref[...] = (r * dn - xn * (r * jnp.mean(dn * xn, axis=-1, keepdims=True))) + dx2_ref[...]

        @pl.when(i == nt - 1)
        def _():
            copies = _chip_copies(ins, outs, send_sems, recv_sems)
            for cp in copies:
                cp.wait_recv()
            for cp in copies:
                cp.wait_send()

    rowt = pl.BlockSpec((tm, D_MODEL), lambda i: (i, 0))
    seg = lambda width: pl.BlockSpec((tm, width), lambda i: (i, 0))
    resident = lambda rows: pl.BlockSpec((rows, D_MODEL), lambda i: (0, 0), pipeline_mode=pl.Buffered(1))
    hbm = pl.BlockSpec(memory_space=pl.ANY)
    return pl.pallas_call(
        body, name="input_grad",
        out_shape=(jax.ShapeDtypeStruct((seq, D_MODEL), F32), jax.ShapeDtypeStruct((1, D_MODEL), F32))
        + tuple(jax.ShapeDtypeStruct((3,) + s.shape[1:], s.dtype) for s in sums),
        grid=(nt,),
        in_specs=[seg(QKV_W), seg(GATES_W), seg(CH_W), seg(LR_W), resident(IN_W),
                  rowt, pl.BlockSpec((1, D_MODEL), lambda i: (0, 0)), rowt] + [hbm] * n,
        out_specs=(rowt, pl.BlockSpec((1, D_MODEL), lambda i: (0, 0))) + (hbm,) * n,
        scratch_shapes=[pltpu.SemaphoreType.DMA((3 * n,)), pltpu.SemaphoreType.DMA((3 * n,))],
        compiler_params=_cparams("arbitrary"),
    )(dp_qkv, dp_gates, dp_ch, dlr, w_nat, x2d, norm_g, dx2, *sums)


def _weight_grad_out(y_t, dx2b, tk, riding):
    m, seq = y_t.shape
    n = dx2b.shape[1]
    nk = seq // tk

    def body(a_ref, b_ref, ride_in, o_ref, ride_out, send_sems, recv_sems):
        k = pl.program_id(0)

        @pl.when(k == 0)
        def _():
            _start_all(_sibling_copies(ride_in, ride_out, send_sems, recv_sems))
            o_ref[...] = jnp.zeros(o_ref.shape, F32)

        o_ref[...] += _dot(a_ref[...], b_ref[...])

        @pl.when(k == nk - 1)
        def _():
            _wait_all(_sibling_copies(ride_in, ride_out, send_sems, recv_sems))

    hbm = pl.BlockSpec(memory_space=pl.ANY)
    return pl.pallas_call(
        body, name="wgrad_out",
        out_shape=(jax.ShapeDtypeStruct((m, n), F32), jax.ShapeDtypeStruct((4,) + _block_shape(riding), F32)),
        grid=(nk,),
        in_specs=[pl.BlockSpec((m, tk), lambda k: (0, k)), pl.BlockSpec((tk, n), lambda k: (k, 0)), hbm],
        out_specs=(pl.BlockSpec((m, n), lambda k: (0, 0)), hbm),
        scratch_shapes=[pltpu.SemaphoreType.DMA((4,)), pltpu.SemaphoreType.DMA((4,))],
        compiler_params=_cparams("arbitrary"),
    )(y_t, dx2b, riding)


def _weight_grad_in(h_t, dp_qkv, dp_gates, dp_ch, dlr):
    m, seq = h_t.shape
    tn = 512
    n_qkv, n_gates, n_ch = QKV_W // tn, GATES_W // tn, CH_W // tn
    starts = ([k * tn for k in range(n_qkv)] + [NAT_ZA, NAT_ZA + tn, NAT_B, NAT_B + tn, NAT_ZC, NAT_ZC + tn]
              + [NAT_C + k * tn for k in range(n_ch)])

    def out_row(j):
        row = 0
        for k, start in enumerate(starts):
            row = row + jnp.where(j == k, start // 32, 0)
        return pl.multiple_of(row * 32, 32), 0

    def body(a_ref, bq, bg, bc, o_ref, acc):
        j = pl.program_id(0)

        @pl.when(j < n_qkv)
        def _():
            acc[...] = _dot(a_ref[...], bq[...])

        @pl.when(jnp.logical_and(j >= n_qkv, j < n_qkv + n_gates))
        def _():
            acc[...] = _dot(a_ref[...], bg[...])

        @pl.when(j >= n_qkv + n_gates)
        def _():
            acc[...] = _dot(a_ref[...], bc[...])

        o_ref[...] = acc[...].T

    resident = pl.BlockSpec((m, seq), lambda j: (0, 0), pipeline_mode=pl.Buffered(1))
    seg = lambda first, count: pl.BlockSpec((seq, tn), lambda j: (0, jnp.clip(j - first, 0, count - 1)))
    main = pl.pallas_call(
        body, name="wgrad_in",
        out_shape=jax.ShapeDtypeStruct((IN_W, m), F32),
        grid=(n_qkv + n_gates + n_ch,),
        in_specs=[resident, seg(0, n_qkv), seg(n_qkv, n_gates), seg(n_qkv + n_gates, n_ch)],
        out_specs=pl.BlockSpec((pl.Element(tn), pl.Element(m)), out_row),
        scratch_shapes=[pltpu.VMEM((m, tn), F32)],
        compiler_params=_cparams("arbitrary"),
    )(h_t, dp_qkv, dp_gates, dp_ch)

    def lr_body(a_ref, b_ref, full_ref, o_ref, acc):
        acc[...] = _dot(a_ref[...], b_ref[...])
        o_ref[...] = acc[...].T[0:2 * RANK, :]

    whole = lambda shape: pl.BlockSpec(shape, lambda j: (0, 0))
    return pl.pallas_call(
        lr_body, name="wgrad_lr",
        out_shape=jax.ShapeDtypeStruct((IN_W, m), F32),
        grid=(1,),
        in_specs=[whole((m, seq)), whole((seq, LR_W)), pl.BlockSpec(memory_space=pl.ANY)],
        out_specs=pl.BlockSpec((pl.Element(2 * RANK), pl.Element(m)), lambda j: (NAT_LR, 0)),
        scratch_shapes=[pltpu.VMEM((m, LR_W), F32)],
        input_output_aliases={2: 0},
        compiler_params=_cparams("arbitrary"),
    )(h_t, dlr, main)


def _pad_rows(a, rows):
    return jnp.pad(a, ((0, rows - a.shape[0]), (0, 0)))


def _rows128(a):
    a = a.reshape(-1, 128)
    return _pad_rows(a, -(-a.shape[0] // 8) * 8)


def _pack(arrs):
    return jnp.concatenate([_rows128(a) for a in arrs], axis=0)


def _unpack(buf, like):
    out, start = [], 0
    for a in like:
        rows = a.size // 128
        out.append(buf[start:start + rows].reshape(a.shape))
        start += -(-rows // 8) * 8
    return out


def kernel(x, norm_g, w_in, w_gk_f, b_gk_f, w_gk_b, b_gk_b, gla_norm_g, conv_w, conv_b, w_out, final_g, loss_target, m_norm_g, m_w_in, m_w_gk_f, m_b_gk_f, m_w_gk_b, m_b_gk_b, m_gla_norm_g, m_conv_w, m_conv_b, m_w_out, m_final_g, v_norm_g, v_w_in, v_w_gk_f, v_b_gk_f, v_w_gk_b, v_b_gk_b, v_gla_norm_g, v_conv_w, v_conv_b, v_w_out, v_final_g):
    px, py, pc = _position()
    me = _blk(px, py, pc)
    seq = x.shape[1]
    x2d, tgt = x[0], loss_target[0]
    tm = min(512, seq)
    tt = min(256, seq)

    small_s = jnp.concatenate([jnp.concatenate([w_gk_f[0], w_gk_b[0]], axis=1), _pad_rows(conv_w[0], 8)], axis=0)
    shifted = lax.dynamic_update_slice(jnp.zeros((SHIFTED_ROWS, D_MODEL), F32), w_in[0].T, (4 * (me % 4), 0))
    order = sum(jnp.where(2 * px + py == k, jnp.asarray(tiles + (0,), jnp.int32), 0) for k, tiles in enumerate(TILE_ORDER))
    proj, lr, h_t, w_nat, wout_all, small_all = _gather_inproj(x2d, norm_g, shifted, w_out[0], small_s, order,
                                                               min(1024, seq))
    w_out_full = wout_all.reshape(MIX_W, D_MODEL)
    wgk_cols = 512 // N_DEV
    wgk_f_full = small_all[:, 0:RANK, 0:wgk_cols].transpose(1, 0, 2).reshape(RANK, QK_W)
    wgk_b_full = small_all[:, 0:RANK, wgk_cols:2 * wgk_cols].transpose(1, 0, 2).reshape(RANK, QK_W)
    conv_w_full = _pad_rows(small_all[:, RANK:RANK + 3, :].transpose(1, 0, 2).reshape(3, CONV_W), 8)
    zr = lambda n: jnp.zeros((n, QK_W), F32)
    wgk_f_pad = jnp.concatenate([wgk_f_full, zr(LR_W - RANK)], axis=0).astype(BF16)
    wgk_b_pad = jnp.concatenate([zr(RANK), wgk_b_full, zr(LR_W - 2 * RANK)], axis=0).astype(BF16)

    o_f, o_b, st_f, st_b = _gla_fwd(proj, lr, wgk_f_pad, wgk_b_pad, b_gk_f, b_gk_b, tt)
    tmix = min(256, seq)
    y_t, conv, dx2, dx2b, loss_p, dfg_p = _mix_out_loss(o_f, o_b, proj, x2d, tgt, gla_norm_g, conv_w_full, conv_b,
                                                        w_out_full, final_g.reshape(1, D_MODEL), tmix)

    dp_gates, do, dconv, dgg_p, dcb_p = _mix_bwd(dx2b, o_f, o_b, proj, conv, gla_norm_g, w_out_full, tmix)
    dp_ch, dcw_p = _conv_bwd(dconv, proj, conv_w_full, tmix)
    dqkv_f, dlr_f, dqkv_b, dlr_b, dwf_p, dwb_p, dbf_p, dbb_p = _gla_bwd(
        proj, lr, do, st_f, st_b, wgk_f_pad, wgk_b_pad, b_gk_f, b_gk_b, tt)
    dp_qkv, dlr = _sum_directions(dqkv_f, dqkv_b, dlr_f, dlr_b, tm)
    dw_nat = _weight_grad_in(h_t, dp_qkv, dp_gates, dp_ch, dlr)

    dw_out, sib_in = _weight_grad_out(y_t, dx2b, tm, dw_nat)
    part_out = dw_out.reshape(N_DEV, MIX_W // N_DEV, D_MODEL)
    core = jnp.reshape(pc, (1,)).astype(jnp.int32)
    chip = jnp.reshape(2 * px + py, (1,)).astype(jnp.int32)
    sums_in, sib_out = _chip_sums(dw_nat, sib_in, core, 256, "chip_sums_in", riding=part_out)
    sums_out = _chip_sums(part_out, sib_out, core, 256, "chip_sums_out")
    grad_x2d, dng_p, far_in, far_out = _input_grad(dp_qkv, dp_gates, dp_ch, dlr, w_nat, x2d, norm_g, dx2,
                                                   [sums_in, sums_out], tmix)
    pieces = [dng_p, dbf_p, dbb_p, dgg_p, dcb_p, dfg_p[0], dwf_p[0:RANK], dwb_p[RANK:2 * RANK], dcw_p[0:3], loss_p[0]]
    g_window, small_tot = _final_sum(sums_in, far_in, chip, _pack(pieces), 256, "final_sum_in")
    g_in_t = lax.dynamic_slice_in_dim(g_window, 4 * pc, SHARD_W, axis=0)
    g_w_out, d_w_out, nm_w_out, nv_w_out = _final_sum_adamw(sums_out, far_out, chip, w_out[0], m_w_out[0], v_w_out[0],
                                                            256, "adamw_out")
    flat = lambda a: a[0].T.reshape(SHARD_W, D_MODEL // 128, 128)
    unflat = lambda a: a.reshape(SHARD_W, D_MODEL).T
    d_flat, m_flat, v_flat = _adamw_rows(g_in_t.reshape(SHARD_W, D_MODEL // 128, 128), flat(w_in), flat(m_w_in),
                                         flat(v_w_in), 90, "adamw_in")
    g_w_in, d_w_in, nm_w_in, nv_w_in = g_in_t.T, unflat(d_flat), unflat(m_flat), unflat(v_flat)

    tot = _unpack(small_tot, pieces)
    g_norm_g, g_b_gk_f, g_b_gk_b, g_gla, g_conv_b, g_final = tot[:6]
    g_wgk_f = lax.dynamic_slice_in_dim(tot[6], me * wgk_cols, wgk_cols, axis=1)[None]
    g_wgk_b = lax.dynamic_slice_in_dim(tot[7], me * wgk_cols, wgk_cols, axis=1)[None]
    g_conv_w = lax.dynamic_slice_in_dim(tot[8], me * 128, 128, axis=1)[None]
    loss = tot[9][0]

    small_g = [g_norm_g, g_b_gk_f, g_b_gk_b, g_gla, g_conv_b, g_final, g_wgk_f, g_wgk_b, g_conv_w]
    small_w = [norm_g, b_gk_f, b_gk_b, gla_norm_g, conv_b, final_g, w_gk_f, w_gk_b, conv_w]
    small_m = [m_norm_g, m_b_gk_f, m_b_gk_b, m_gla_norm_g, m_conv_b, m_final_g, m_w_gk_f, m_w_gk_b, m_conv_w]
    small_v = [v_norm_g, v_b_gk_f, v_b_gk_b, v_gla_norm_g, v_conv_b, v_final_g, v_w_gk_f, v_w_gk_b, v_conv_w]
    d_s, m_s, v_s = _adamw_small(_pack(small_g), _pack(small_w), _pack(small_m), _pack(small_v))
    d_l, m_l, v_l = _unpack(d_s, small_w), _unpack(m_s, small_w), _unpack(v_s, small_w)

    def ordered(sm, big_in, big_out):
        return [sm[0], big_in[None], sm[6], sm[1], sm[7], sm[2], sm[3], sm[8], sm[4], big_out[None], sm[5]]

    grads = ordered(small_g, g_w_in, g_w_out)
    deltas = ordered(d_l, d_w_in, d_w_out)
    new_m = ordered(m_l, nm_w_in, nm_w_out)
    new_v = ordered(v_l, nv_w_in, nv_w_out)
    return (loss, grad_x2d[None], *grads, *deltas, *new_m, *new_v)
```

```python
import jax
import jax.numpy as jnp
from jax import lax
from jax.experimental import pallas as pl
from jax.experimental.pallas import tpu as pltpu

F32 = jnp.float32
BF16 = jnp.bfloat16
MESH = pl.DeviceIdType.MESH

N_DEV = 8
D_MODEL = 1024
HEADS = 4
DK = 128
DV = 256
QK_W = HEADS * DK
V_W = HEADS * DV
CONV_W = 1024
MIX_W = V_W + CONV_W
CHUNK = 64
RANK = 16
IN_W = 7200
SHARD_W = IN_W // N_DEV
MAIN_W = 7168
LR_W = 128
OFF_Q, OFF_K, OFF_V, OFF_ZA, OFF_B, OFF_ZC, OFF_C, OFF_H = 0, 512, 1024, 2048, 3072, 4096, 5120, 6144
QKV_W, GATES_W, CH_W = 2048, 3072, 2048
NAT_ZA, NAT_LR, NAT_B, NAT_C, NAT_ZC = 2048, 3072, 3104, 4128, 6176
EPS = 1e-6
GATE_SCALE = 1.0 / 16.0
QSCALE = DK ** -0.5
REF_F, LAST_F = CHUNK // 2, CHUNK - 1
REF_B, LAST_B = CHUNK - 1 - CHUNK // 2, 0

ADAM_LR = 0.001
ADAM_B1 = 0.9
ADAM_B2 = 0.999
ADAM_EPS = 1e-08
ADAM_WD = 0.01
ADAM_STEP = 10

VMEM_LIMIT = 56 * 1024 * 1024


def _cparams(*sem):
    return pltpu.CompilerParams(dimension_semantics=sem, vmem_limit_bytes=VMEM_LIMIT)


def _dot(a, b):
    return jnp.dot(a, b, preferred_element_type=F32)


def _dot_nt(a, b):
    return lax.dot_general(a, b, (((1,), (1,)), ((), ())), preferred_element_type=F32)


def _dot_tn(a, b):
    return lax.dot_general(a, b, (((0,), (0,)), ((), ())), preferred_element_type=F32)


def _sigmoid(z):
    return jax.nn.sigmoid(z)


def _position():
    return lax.axis_index("x"), lax.axis_index("y"), lax.axis_index("c")


def _blk(px, py, pc):
    return 4 * px + 2 * py + pc


EDGE = 16
SHIFTED_ROWS = 912
BODY_ROWS = SHIFTED_ROWS - 2 * EDGE


def _first_tile_row(blk, px):
    return EDGE * (56 * blk + px)


def _edge_tiles():
    tiles = {}
    for blk in range(N_DEV):
        first = _first_tile_row(blk, blk // 4)
        tiles.setdefault(first, []).append((blk, 0))
        tiles.setdefault(first + EDGE + BODY_ROWS, []).append((blk, 1))
    return tiles


def _peer_copies(srcs, outs, send_sems, recv_sems):
    x, y, c = _position()
    me = _blk(x, y, c)
    copies = []
    for a, (src, out) in enumerate(zip(srcs, outs)):
        k = 0
        for dx in (0, 1):
            for dy in (0, 1):
                for dc in (0, 1):
                    if dx + dy + dc == 0:
                        continue
                    peer = (1 - x if dx else x, 1 - y if dy else y, 1 - c if dc else c)
                    copies.append(pltpu.make_async_remote_copy(
                        src_ref=src, dst_ref=out.at[me], send_sem=send_sems.at[a * 7 + k],
                        recv_sem=recv_sems.at[a * 7 + k], device_id=peer, device_id_type=MESH))
                    k += 1
    return copies


def _chip_copies(ins, outs, send_sems, recv_sems):
    x, y, c = _position()
    chips = [(1 - x, y), (x, 1 - y), (1 - x, 1 - y)]
    copies = []
    for a in range(len(ins)):
        for j, (px, py) in enumerate(chips):
            copies.append(pltpu.make_async_remote_copy(
                src_ref=ins[a].at[2 * px + py], dst_ref=outs[a].at[j],
                send_sem=send_sems.at[a * 3 + j], recv_sem=recv_sems.at[a * 3 + j],
                device_id=(px, py, c), device_id_type=MESH))
    return copies


WINDOW_ROWS = SHARD_W + 4


def _window_start(k, parity):
    return 2 * SHARD_W * k + (SHARD_W - 4) * parity


def _owner_block(part, k, parity):
    if part.ndim == 3:
        return part.at[2 * k + parity]
    return part.at[pl.ds(pl.multiple_of(_window_start(k, parity), 8), WINDOW_ROWS)]


def _block_shape(part):
    return part.shape[1:] if part.ndim == 3 else (WINDOW_ROWS, part.shape[1])


def _sibling_copies(part, out, send_sems, recv_sems):
    x, y, c = _position()
    return [pltpu.make_async_remote_copy(src_ref=_owner_block(part, k, 1 - c), dst_ref=out.at[k],
                                         send_sem=send_sems.at[k], recv_sem=recv_sems.at[k],
                                         device_id=(x, y, 1 - c), device_id_type=MESH)
            for k in range(4)]


def _start_all(copies):
    for cp in copies:
        cp.start()


def _wait_all(copies):
    for cp in copies:
        cp.wait_recv()
    for cp in copies:
        cp.wait_send()


def _chip_sums(part, from_sibling, core, tc, name, riding=None):
    rows, cols = _block_shape(part)
    nj = cols // tc

    def body(core_ref, p_ref, s_ref, *rest):
        if riding is None:
            (o_ref,) = rest
        else:
            ride_in, o_ref, ride_out, send_sems, recv_sems = rest
            k, j = pl.program_id(0), pl.program_id(1)

            @pl.when(jnp.logical_and(k == 0, j == 0))
            def _():
                _start_all(_sibling_copies(ride_in, ride_out, send_sems, recv_sems))

        o_ref[0] = (p_ref[...].reshape(rows, tc) + s_ref[0]).astype(BF16)

        if riding is not None:
            @pl.when(jnp.logical_and(k == 3, j == nj - 1))
            def _():
                _wait_all(_sibling_copies(ride_in, ride_out, send_sems, recv_sems))

    hbm = pl.BlockSpec(memory_space=pl.ANY)
    sums = jax.ShapeDtypeStruct((4, rows, cols), BF16)
    tile_out = pl.BlockSpec((1, rows, tc), lambda k, j, core_ref: (k, 0, j))
    if part.ndim == 3:
        mine = pl.BlockSpec((1, rows, tc), lambda k, j, core_ref: (2 * k + core_ref[0], 0, j))
    else:
        mine = pl.BlockSpec((pl.Element(rows), pl.Element(tc)),
                            lambda k, j, core_ref: (pl.multiple_of(_window_start(k, core_ref[0]), 8),
                                                    pl.multiple_of(j * tc, 128)))
    in_specs = [mine, pl.BlockSpec((1, rows, tc), lambda k, j, core_ref: (k, 0, j))]
    if riding is None:
        out_shape, out_specs, scratch, args = sums, tile_out, [], (core, part, from_sibling)
    else:
        out_shape = (sums, jax.ShapeDtypeStruct((4,) + _block_shape(riding), F32))
        out_specs, in_specs = (tile_out, hbm), in_specs + [hbm]
        scratch = [pltpu.SemaphoreType.DMA((4,)), pltpu.SemaphoreType.DMA((4,))]
        args = (core, part, from_sibling, riding)
    return pl.pallas_call(
        body, name=name, out_shape=out_shape,
        grid_spec=pltpu.PrefetchScalarGridSpec(num_scalar_prefetch=1, grid=(4, nj), in_specs=in_specs,
                                               out_specs=out_specs, scratch_shapes=scratch),
        compiler_params=_cparams("arbitrary", "arbitrary"),
    )(*args)


def _sum_chips(s_ref, r_ref):
    f = lambda a: a.astype(F32)
    return ((f(s_ref[0]) + f(r_ref[0])) + f(r_ref[1])) + f(r_ref[2])


def _final_sum(sums, from_chips, chip, small, tc, name):
    _, rows, cols = sums.shape
    nj = cols // tc

    def body(chip_ref, s_ref, r_ref, sm_ref, g_out, tot_ref, all_ref, send_sems, recv_sems):
        j = pl.program_id(0)
        me = _blk(*_position())

        @pl.when(j == 0)
        def _():
            all_ref[me] = sm_ref[...]
            _start_all(_peer_copies((all_ref.at[me],), (all_ref,), send_sems, recv_sems))

        g_out[...] = _sum_chips(s_ref, r_ref)

        @pl.when(j == nj - 1)
        def _():
            _wait_all(_peer_copies((all_ref.at[me],), (all_ref,), send_sems, recv_sems))
            acc = all_ref[0]
            for d in range(1, N_DEV):
                acc = acc + all_ref[d]
            tot_ref[...] = acc

    whole = pl.BlockSpec(small.shape, lambda j, chip_ref: (0, 0))
    return pl.pallas_call(
        body, name=name,
        out_shape=(jax.ShapeDtypeStruct((rows, cols), F32), jax.ShapeDtypeStruct(small.shape, F32)),
        grid_spec=pltpu.PrefetchScalarGridSpec(
            num_scalar_prefetch=1, grid=(nj,),
            in_specs=[pl.BlockSpec((1, rows, tc), lambda j, chip_ref: (chip_ref[0], 0, j)),
                      pl.BlockSpec((3, rows, tc), lambda j, chip_ref: (0, 0, j)), whole],
            out_specs=(pl.BlockSpec((rows, tc), lambda j, chip_ref: (0, j)), whole),
            scratch_shapes=[pltpu.VMEM((N_DEV,) + small.shape, F32), pltpu.SemaphoreType.DMA((7,)),
                            pltpu.SemaphoreType.DMA((7,))]),
        compiler_params=_cparams("arbitrary"),
    )(chip, sums, from_chips, small)


def _adamw_rows(g, w, m, v, tr, name):
    rows = g.shape[0]

    def body(g_ref, w_ref, m_ref, v_ref, d_out, m_out, v_out):
        delta, m_new, v_new = _adamw(w_ref[...], g_ref[...], m_ref[...], v_ref[...])
        d_out[...] = delta
        m_out[...] = m_new
        v_out[...] = v_new

    tile = pl.BlockSpec((tr,) + g.shape[1:], lambda r: (r, 0, 0))
    shp = jax.ShapeDtypeStruct(g.shape, F32)
    return pl.pallas_call(
        body, name=name, out_shape=(shp, shp, shp), grid=(rows // tr,),
        in_specs=[tile] * 4, out_specs=(tile, tile, tile),
        compiler_params=_cparams("arbitrary"),
    )(g, w, m, v)


def _adamw(w, g, m, v):
    m = ADAM_B1 * m + (1.0 - ADAM_B1) * g
    v = ADAM_B2 * v + (1.0 - ADAM_B2) * (g * g)
    m_hat = m / (1.0 - ADAM_B1 ** ADAM_STEP)
    v_hat = v / (1.0 - ADAM_B2 ** ADAM_STEP)
    delta = -ADAM_LR * (m_hat / (jnp.sqrt(v_hat) + ADAM_EPS) + ADAM_WD * w)
    return delta, m, v


def _final_sum_adamw(sums, from_chips, chip, w, m, v, tr, name):
    rows, cols = w.shape

    def body(chip_ref, s_ref, r_ref, w_ref, m_ref, v_ref, g_out, d_out, m_out, v_out):
        g = _sum_chips(s_ref, r_ref)
        delta, m_new, v_new = _adamw(w_ref[...], g, m_ref[...], v_ref[...])
        g_out[...] = g
        d_out[...] = delta
        m_out[...] = m_new
        v_out[...] = v_new

    tile = pl.BlockSpec((tr, cols), lambda r, chip_ref: (r, 0))
    shp = jax.ShapeDtypeStruct((rows, cols), F32)
    return pl.pallas_call(
        body, name=name,
        out_shape=(shp, shp, shp, shp),
        grid_spec=pltpu.PrefetchScalarGridSpec(
            num_scalar_prefetch=1, grid=(rows // tr,),
            in_specs=[pl.BlockSpec((1, tr, cols), lambda r, chip_ref: (chip_ref[0], r, 0)),
                      pl.BlockSpec((3, tr, cols), lambda r, chip_ref: (0, r, 0)),
                      tile, tile, tile],
            out_specs=(tile, tile, tile, tile)),
        compiler_params=_cparams("arbitrary"),
    )(chip, sums, from_chips, w, m, v)


def _adamw_small(g, w, m, v):
    def body(g_ref, w_ref, m_ref, v_ref, d_out, m_out, v_out):
        delta, m_new, v_new = _adamw(w_ref[...], g_ref[...], m_ref[...], v_ref[...])
        d_out[...] = delta
        m_out[...] = m_new
        v_out[...] = v_new

    vmem = pl.BlockSpec(memory_space=pltpu.VMEM)
    shp = jax.ShapeDtypeStruct(g.shape, F32)
    return pl.pallas_call(body, name="adamw_small", out_shape=(shp, shp, shp),
                          in_specs=[vmem] * 4, out_specs=(vmem, vmem, vmem))(g, w, m, v)


TILE_ROWS = (0, 1024, NAT_ZA, NAT_B, NAT_ZC, NAT_C, NAT_C + CONV_W)


TILE_ORDER = ((0, 1, 2, 3, 5, 6, 4), (2, 0, 1, 4, 3, 5, 6), (5, 0, 6, 4, 1, 2, 3), (4, 2, 3, 5, 6, 0, 1))
NEIGHBOUR_SWEEP, DIAGONAL_SWEEP = 1, 4
PIECES, W_IN_PIECES, OTHER_PIECES = 4, (0, 1), (2, 3)


def _gather_inproj(x2d, norm_g, shifted, w_out_s, small_s, order, tm):
    seq = x2d.shape[0]
    tn = CONV_W
    ni, nj = seq // tm, MAIN_W // tn
    first_sweep = lambda j, i, order_ref: jnp.where(j == 0, i, ni - 1)
    last_sweep = lambda j, i, order_ref: jnp.where(j == nj - 1, i, 0)
    edge_tiles = _edge_tiles()

    def body(order_ref, x_ref, g_ref, sh_ref, wout_ref, sm_ref, proj_ref, lr_ref, ht_ref, w_nat, wout_all, sm_all,
             w_all, h_all, edges, wout_b, sm_b, send_sems, recv_sems, local_sems):
        j, i = pl.program_id(0), pl.program_id(1)
        rows = pl.ds(pl.multiple_of(i * tm, tm), tm)
        x, y, c = _position()
        me, here, sibling = _blk(x, y, c), (x, y, c), (x, y, 1 - c)
        chips = [(1 - x, y), (x, 1 - y), (1 - x, 1 - y)]

        def pieces(px, py, pc):
            blk = _blk(px, py, pc)
            body_rows = pl.ds(pl.multiple_of(_first_tile_row(blk, px) + EDGE, EDGE), BODY_ROWS)
            return [w_all.at[body_rows], edges.at[blk], wout_all.at[blk], sm_all.at[blk]]

        def copy(a, k, block, to, staged=None):
            ref = pieces(*block)[a]
            return pltpu.make_async_remote_copy(src_ref=ref if staged is None else staged, dst_ref=ref,
                                                send_sem=send_sems.at[a * 7 + k], recv_sem=recv_sems.at[a * 7 + k],
                                                device_id=to, device_id_type=MESH)

        def own_copies(group):
            targets = [(0, sibling)] + [(1 + n, (*chip, c)) for n, chip in enumerate(chips)]
            staged = [None, None, wout_b, sm_b]
            return [copy(a, k, here, to, staged[a]) for k, to in targets for a in group]

        def forwards(n):
            return [copy(a, 4 + n, (*chips[n], c), sibling) for a in range(PIECES)]

        def keep_own():
            return [pltpu.make_async_copy(wout_b, wout_all.at[me], local_sems.at[0]),
                    pltpu.make_async_copy(sm_b, sm_all.at[me], local_sems.at[1])]

        def arrive(ns):
            for n in ns:
                for a in range(PIECES):
                    copy(a, 1 + n, (*chips[n], c), here).wait_recv()
                _start_all(forwards(n))
            for n in ns:
                for a in range(PIECES):
                    copy(a, 4 + n, (*chips[n], 1 - c), here).wait_recv()

        def add_edge_tiles(stage):
            for row, parts in edge_tiles.items():
                ready = 0
                for blk, _ in parts:
                    away = (x != blk // 4).astype(jnp.int32) + (y != (blk // 2) % 2).astype(jnp.int32)
                    ready = jnp.maximum(ready, away)

                @pl.when(ready == stage)
                def _(row=row, parts=parts):
                    tile = edges[parts[0][0], parts[0][1]].astype(F32)
                    for blk, side in parts[1:]:
                        tile = tile + edges[blk, side].astype(F32)
                    w_all[row:row + EDGE, :] = tile.astype(BF16)

        @pl.when(jnp.logical_and(j == 0, i == 0))
        def _():
            pieces(*here)[0][...] = sh_ref[EDGE:EDGE + BODY_ROWS, :].astype(BF16)
            edges[me, 0] = sh_ref[0:EDGE, :].astype(BF16)
            edges[me, 1] = sh_ref[EDGE + BODY_ROWS:, :].astype(BF16)
            _start_all(own_copies(W_IN_PIECES))
            wout_b[...] = wout_ref[...].astype(BF16)
            sm_b[...] = sm_ref[...]
            _start_all(own_copies(OTHER_PIECES) + keep_own())
            for a in W_IN_PIECES:
                copy(a, 0, sibling, here).wait_recv()
            add_edge_tiles(0)

        @pl.when(jnp.logical_and(j == NEIGHBOUR_SWEEP, i == 0))
        def _():
            arrive((0, 1))
            add_edge_tiles(1)

        @pl.when(jnp.logical_and(j == DIAGONAL_SWEEP, i == 0))
        def _():
            arrive((2,))
            add_edge_tiles(2)

        @pl.when(j == 0)
        def _():
            xv = x_ref[...]
            r = lax.rsqrt(jnp.mean(xv * xv, axis=-1, keepdims=True) + EPS)
            h = (xv * r) * g_ref[...]
            h_all[rows, :] = h.astype(BF16)
            ht_ref[...] = h.T.astype(BF16)

        tile = order_ref[j]
        row = 0
        for k, start in enumerate(TILE_ROWS):
            row = row + jnp.where(tile == k, start // 32, 0)
        w_tile = w_all[pl.ds(pl.multiple_of(row * 32, 32), tn), :]
        proj_ref[...] = _dot_nt(h_all[rows, :], w_tile).astype(BF16)

        @pl.when(j == nj - 1)
        def _():
            lr_ref[...] = _dot_nt(h_all[rows, :], w_all[NAT_LR:NAT_LR + LR_W, :])

        @pl.when(jnp.logical_and(j == nj - 1, i == ni - 1))
        def _():
            for cp in own_copies(range(PIECES)) + forwards(0) + forwards(1) + forwards(2):
                cp.wait_send()
            for a in OTHER_PIECES:
                copy(a, 0, sibling, here).wait_recv()
            for cp in keep_own():
                cp.wait()
            keep = pltpu.make_async_copy(w_all, w_nat, local_sems.at[2])
            keep.start()
            keep.wait()

    const = lambda shape: pl.BlockSpec(shape, lambda j, i, order_ref: (0,) * len(shape))
    hbm = pl.BlockSpec(memory_space=pl.ANY)
    vmem = pl.BlockSpec(memory_space=pltpu.VMEM)
    return pl.pallas_call(
        body, name="gather_inproj",
        out_shape=(jax.ShapeDtypeStruct((seq, MAIN_W), BF16), jax.ShapeDtypeStruct((seq, LR_W), F32),
                   jax.ShapeDtypeStruct((D_MODEL, seq), BF16), jax.ShapeDtypeStruct((IN_W, D_MODEL), BF16),
                   jax.ShapeDtypeStruct((N_DEV,) + w_out_s.shape, BF16),
                   jax.ShapeDtypeStruct((N_DEV,) + small_s.shape, F32)),
        grid_spec=pltpu.PrefetchScalarGridSpec(
            num_scalar_prefetch=1, grid=(nj, ni),
            in_specs=[pl.BlockSpec((tm, D_MODEL), lambda j, i, order_ref: (first_sweep(j, i, order_ref), 0)),
                      const((1, D_MODEL)), vmem, const(w_out_s.shape), const(small_s.shape)],
            out_specs=(pl.BlockSpec((tm, tn), lambda j, i, order_ref: (i, order_ref[j])),
                       pl.BlockSpec((tm, LR_W), lambda j, i, order_ref: (last_sweep(j, i, order_ref), 0)),
                       pl.BlockSpec((D_MODEL, tm), lambda j, i, order_ref: (0, first_sweep(j, i, order_ref))),
                       hbm, hbm, hbm),
            scratch_shapes=[pltpu.VMEM((IN_W, D_MODEL), BF16), pltpu.VMEM((seq, D_MODEL), BF16),
                            pltpu.VMEM((N_DEV, 2, EDGE, D_MODEL), BF16),
                            pltpu.VMEM(w_out_s.shape, BF16), pltpu.VMEM(small_s.shape, F32),
                            pltpu.SemaphoreType.DMA((7 * PIECES,)), pltpu.SemaphoreType.DMA((7 * PIECES,)),
                            pltpu.SemaphoreType.DMA((3,))]),
        compiler_params=_cparams("arbitrary", "arbitrary"),
    )(order, x2d, norm_g, shifted, w_out_s, small_s)


def _block_masks(tt):
    row = lax.broadcasted_iota(jnp.int32, (tt, tt), 0)
    col = lax.broadcasted_iota(jnp.int32, (tt, tt), 1)
    same = jnp.right_shift(row, 6) == jnp.right_shift(col, 6)
    return (jnp.logical_and(same, col <= row), jnp.logical_and(same, col >= row), jnp.logical_and(same, col > row))


def _dot_split3(ones_mat, x):
    x1 = x.astype(BF16)
    r1 = x - x1.astype(F32)
    x2 = r1.astype(BF16)
    x3 = (r1 - x2.astype(F32)).astype(BF16)
    return (_dot(ones_mat, x3) + _dot(ones_mat, x2)) + _dot(ones_mat, x1)


def _log_gate(logits):
    return (jnp.minimum(logits, 0.0) - jnp.log(1.0 + jnp.exp(-jnp.abs(logits)))) * GATE_SCALE


def _chunk_column_mask(tt):
    nc = tt // CHUNK
    row = lax.broadcasted_iota(jnp.int32, (tt, nc * DK), 0)
    col = lax.broadcasted_iota(jnp.int32, (tt, nc * DK), 1)
    return jnp.right_shift(row, 6) == jnp.right_shift(col, 7)


def _chunked(mask, x, nc):
    wide = jnp.concatenate([x] * nc, axis=1)
    return jnp.where(mask, wide, jnp.zeros_like(wide))


def _gla_fwd(proj, lr, wgk_f, wgk_b, bgk_f, bgk_b, tt):
    seq = proj.shape[0]
    nb, nc, nch = seq // tt, tt // CHUNK, seq // CHUNK

    def body(qf, kf, vf, lrf, qb, kb, vb, lrb, wf, wb, bf, bb, of, ob, stf, stb, s_scr, qs_s, ks_s, qin_s, kout_s):
        @pl.when(pl.program_id(0) == 0)
        def _():
            s_scr[...] = jnp.zeros(s_scr.shape, F32)

        low, upp, sup = _block_masks(tt)
        dirs = ((qf, kf, vf, lrf, wf, bf, of, stf, low, low, REF_F, LAST_F, list(range(nc))),
                (qb, kb, vb, lrb, wb, bb, ob, stb, upp, sup, REF_B, LAST_B, list(reversed(range(nc)))))
        for d, (q_r, k_r, v_r, lr_r, w_r, b_r, o_r, st_r, cum, mask, ref, last, order) in enumerate(dirs):
            logits = _dot(lr_r[...].astype(BF16), w_r[...]) + b_r[...]
            b = _dot_split3(cum.astype(BF16), _log_gate(logits))
            decs = []
            for c in range(nc):
                rows = slice(c * CHUNK, (c + 1) * CHUNK)
                bc = b[rows]
                b_ref, b_last = bc[ref:ref + 1], bc[last:last + 1]
                qc = q_r[rows, :].astype(F32) * QSCALE
                kc = k_r[rows, :].astype(F32)
                qs_s[rows, :] = (qc * jnp.exp(bc - b_ref)).astype(BF16)
                ks_s[rows, :] = (kc * jnp.exp(b_ref - bc)).astype(BF16)
                qin_s[rows, :] = (qc * jnp.exp(bc)).astype(BF16)
                kout_s[rows, :] = (kc * jnp.exp(b_last - bc)).astype(BF16)
                decs.append(jnp.exp(b_last))
            for h in range(HEADS):
                ksl = slice(h * DK, (h + 1) * DK)
                vsl = slice(h * DV, (h + 1) * DV)
                v = v_r[:, vsl].astype(BF16)
                att = jnp.where(mask, _dot_nt(qs_s[:, ksl], ks_s[:, ksl]), 0.0).astype(BF16)
                o_intra = _dot(att, v)
                st = s_scr[d * HEADS + h]
                for c in order:
                    rows = slice(c * CHUNK, (c + 1) * CHUNK)
                    stb = st.astype(BF16)
                    st_r[c, h] = stb
                    o_r[rows, vsl] = (o_intra[rows] + _dot_nt(qin_s[rows, ksl], stb)).astype(BF16)
                    st = st * decs[c][:, ksl] + _dot_tn(v[rows], kout_s[rows, ksl])
                s_scr[d * HEADS + h] = st

    fw = lambda i: (i, 0)
    bw = lambda i: (nb - 1 - i, 0)
    const = lambda i: (0, 0)

    def tok_specs(m):
        return [pl.BlockSpec((tt, QK_W), lambda i: (m(i)[0], OFF_Q // QK_W)),
                pl.BlockSpec((tt, QK_W), lambda i: (m(i)[0], OFF_K // QK_W)),
                pl.BlockSpec((tt, V_W), lambda i: (m(i)[0], OFF_V // V_W)),
                pl.BlockSpec((tt, LR_W), m)]

    st_shape = jax.ShapeDtypeStruct((nch, HEADS, DV, DK), BF16)
    o_shape = jax.ShapeDtypeStruct((seq, V_W), BF16)
    operand = pltpu.VMEM((tt, QK_W), BF16)
    return pl.pallas_call(
        body, name="gla_fwd",
        out_shape=(o_shape, o_shape, st_shape, st_shape),
        grid=(nb,),
        in_specs=tok_specs(fw) + tok_specs(bw) + [
            pl.BlockSpec((LR_W, QK_W), const), pl.BlockSpec((LR_W, QK_W), const),
            pl.BlockSpec((1, QK_W), const), pl.BlockSpec((1, QK_W), const)],
        out_specs=(pl.BlockSpec((tt, V_W), fw), pl.BlockSpec((tt, V_W), bw),
                   pl.BlockSpec((nc, HEADS, DV, DK), lambda i: (i, 0, 0, 0)),
                   pl.BlockSpec((nc, HEADS, DV, DK), lambda i: (nb - 1 - i, 0, 0, 0))),
        scratch_shapes=[pltpu.VMEM((2 * HEADS, DV, DK), F32), operand, operand, operand, operand],
        compiler_params=_cparams("arbitrary"),
    )(proj, proj, proj, lr, proj, proj, proj, lr, wgk_f, wgk_b, bgk_f, bgk_b)


def _head_norm(o, gain):
    outs, rinv = [], []
    for h in range(HEADS):
        oh = o[:, h * DV:(h + 1) * DV]
        r = lax.rsqrt(jnp.mean(oh * oh, axis=-1, keepdims=True) + EPS)
        outs.append((oh * r) * gain)
        rinv.append(r)
    return jnp.concatenate(outs, axis=1), rinv


def _shift_rows(u, prev_row, next_row):
    n = u.shape[0]
    row = lax.broadcasted_iota(jnp.int32, (n, 1), 0)
    up = jnp.where(row == 0, prev_row, pltpu.roll(u, 1, 0))
    un = jnp.where(row == n - 1, next_row, pltpu.roll(u, n - 1, 0))
    return up, un


HALO = 16


def _halo_specs(tm, seq, col_block):
    per = tm // HALO
    last = seq // HALO - 1
    return [pl.BlockSpec((HALO, CONV_W), lambda i: (jnp.maximum(i * per - 1, 0), col_block)),
            pl.BlockSpec((HALO, CONV_W), lambda i: (jnp.minimum((i + 1) * per, last), col_block))]


def _f32(ref):
    return ref[...].astype(F32)


def _last_row(ref):
    return ref[HALO - 1:HALO, :].astype(F32)


def _first_row(ref):
    return ref[0:1, :].astype(F32)


def _mix_out_loss(o_f, o_b, proj, x2d, tgt, gla_g, conv_w, conv_b, w_out, final_g, tm):
    seq = x2d.shape[0]
    nt = seq // tm

    def body(of, ob, za, bg, cg, hc, zc, cprev, cnext, hprev, hnext, x_ref, t_ref, gg, cw, cb, wo, fg,
             yt_ref, conv_ref, dx2_ref, dx2b_ref, loss_ref, dfg_ref):
        i = pl.program_id(0)

        @pl.when(i == 0)
        def _():
            loss_ref[...] = jnp.zeros(loss_ref.shape, F32)
            dfg_ref[...] = jnp.zeros(dfg_ref.shape, F32)

        on, _ = _head_norm(_f32(of) + _f32(ob), gg[...])
        zav = _f32(za)
        y_a = on * (zav * _sigmoid(zav))
        u = _f32(cg) * _f32(hc)
        prev_row = jnp.where(i > 0, _last_row(cprev) * _last_row(hprev), 0.0)
        next_row = jnp.where(i < nt - 1, _first_row(cnext) * _first_row(hnext), 0.0)
        up, un = _shift_rows(u, prev_row, next_row)
        conv = (cw[0:1, :] * up + cw[1:2, :] * u + cw[2:3, :] * un) + cb[...]
        conv_ref[...] = conv.astype(BF16)
        zcv = _f32(zc)
        y_c = _f32(bg) * conv * (zcv * _sigmoid(zcv))
        y = jnp.concatenate([y_a, y_c], axis=1)
        yt_ref[...] = y.T.astype(BF16)
        x2 = x_ref[...] + _dot(y.astype(BF16), wo[...])
        r = lax.rsqrt(jnp.mean(x2 * x2, axis=-1, keepdims=True) + EPS)
        xn = x2 * r
        err = xn * fg[...] - t_ref[...]
        loss_ref[...] += 0.5 * jnp.sum(jnp.mean(err * err, axis=-1, keepdims=True))
        dyf = err * (1.0 / D_MODEL)
        dfg_ref[...] += jnp.sum(dyf * xn, axis=0, keepdims=True)
        dxn = dyf * fg[...]
        dx2 = r * dxn - xn * (r * jnp.mean(dxn * xn, axis=-1, keepdims=True))
        dx2_ref[...] = dx2
        dx2b_ref[...] = dx2.astype(BF16)

    def col(off):
        return pl.BlockSpec((tm, CONV_W), lambda i: (i, off // CONV_W))

    rowt = pl.BlockSpec((tm, D_MODEL), lambda i: (i, 0))
    const = lambda shape: pl.BlockSpec(shape, lambda i: (0, 0))
    return pl.pallas_call(
        body, name="mix_out_loss",
        out_shape=(jax.ShapeDtypeStruct((MIX_W, seq), BF16), jax.ShapeDtypeStruct((seq, CONV_W), BF16),
                   jax.ShapeDtypeStruct((seq, D_MODEL), F32), jax.ShapeDtypeStruct((seq, D_MODEL), BF16),
                   jax.ShapeDtypeStruct((8, 128), F32), jax.ShapeDtypeStruct((1, D_MODEL), F32)),
        grid=(nt,),
        in_specs=[rowt, rowt, col(OFF_ZA), col(OFF_B), col(OFF_C), col(OFF_H), col(OFF_ZC)]
        + _halo_specs(tm, seq, OFF_C // CONV_W) + _halo_specs(tm, seq, OFF_H // CONV_W)
        + [rowt, rowt, const((1, DV)), const((8, CONV_W)), const((1, CONV_W)), const((MIX_W, D_MODEL)),
           const((1, D_MODEL))],
        out_specs=(pl.BlockSpec((MIX_W, tm), lambda i: (0, i)), rowt, rowt, rowt, const((8, 128)),
                   const((1, D_MODEL))),
        compiler_params=_cparams("arbitrary"),
    )(o_f, o_b, proj, proj, proj, proj, proj, proj, proj, proj, proj, x2d, tgt, gla_g, conv_w, conv_b, w_out, final_g)


def _dsilu(z, s):
    return s * (1.0 + z * (1.0 - s))


def _mix_bwd(dx2b, o_f, o_b, proj, conv, gla_g, w_out, tm):
    seq = dx2b.shape[0]

    def body(dx, of, ob, za, bg, zc, cv, gg, wo, dg_ref, do_ref, dconv_ref, dgg_ref, dcb_ref):
        @pl.when(pl.program_id(0) == 0)
        def _():
            dgg_ref[...] = jnp.zeros(dgg_ref.shape, F32)
            dcb_ref[...] = jnp.zeros(dcb_ref.shape, F32)

        dy = _dot_nt(dx[...], wo[...])
        dy_a, dy_c = dy[:, :V_W], dy[:, V_W:]
        zcv, bgv, convv = _f32(zc), _f32(bg), _f32(cv)
        sc = _sigmoid(zcv)
        szc = zcv * sc
        dg_ref[:, CONV_W:2 * CONV_W] = (dy_c * convv * szc).astype(BF16)
        dconv = dy_c * bgv * szc
        dconv_ref[...] = dconv.astype(BF16)
        dcb_ref[...] += jnp.sum(dconv, axis=0, keepdims=True)
        dg_ref[:, 2 * CONV_W:] = (dy_c * bgv * convv * _dsilu(zcv, sc)).astype(BF16)

        o = _f32(of) + _f32(ob)
        gain = gg[...]
        on, rinv = _head_norm(o, gain)
        zav = _f32(za)
        sa = _sigmoid(zav)
        dg_ref[:, :CONV_W] = (dy_a * on * _dsilu(zav, sa)).astype(BF16)
        don = dy_a * (zav * sa)
        dgg = jnp.zeros((1, DV), F32)
        dos = []
        for h in range(HEADS):
            sl = slice(h * DV, (h + 1) * DV)
            oh, r, dh = o[:, sl], rinv[h], don[:, sl]
            ohn = oh * r
            dgg = dgg + jnp.sum(dh * ohn, axis=0, keepdims=True)
            dn = dh * gain
            dos.append(r * dn - ohn * (r * jnp.mean(dn * ohn, axis=-1, keepdims=True)))
        dgg_ref[...] += dgg
        do_ref[...] = jnp.concatenate(dos, axis=1).astype(BF16)

    def col(off):
        return pl.BlockSpec((tm, CONV_W), lambda i: (i, off // CONV_W))

    rowt = pl.BlockSpec((tm, D_MODEL), lambda i: (i, 0))
    const = lambda shape: pl.BlockSpec(shape, lambda i: (0, 0))
    return pl.pallas_call(
        body, name="mix_bwd",
        out_shape=(jax.ShapeDtypeStruct((seq, GATES_W), BF16), jax.ShapeDtypeStruct((seq, V_W), BF16),
                   jax.ShapeDtypeStruct((seq, CONV_W), BF16),
                   jax.ShapeDtypeStruct((1, DV), F32), jax.ShapeDtypeStruct((1, CONV_W), F32)),
        grid=(seq // tm,),
        in_specs=[rowt, rowt, rowt, col(OFF_ZA), col(OFF_B), col(OFF_ZC), rowt, const((1, DV)),
                  const((MIX_W, D_MODEL))],
        out_specs=(pl.BlockSpec((tm, GATES_W), lambda i: (i, 0)), rowt, rowt, const((1, DV)), const((1, CONV_W))),
        compiler_params=_cparams("arbitrary"),
    )(dx2b, o_f, o_b, proj, proj, proj, conv, gla_g, w_out)


def _conv_bwd(dconv, proj, conv_w, tm):
    seq = dconv.shape[0]
    nt = seq // tm

    def body(dc_in, dprev, dnext, cg, hc, cprev, cnext, hprev, hnext, cw, dch_ref, dcw_ref):
        i = pl.program_id(0)

        @pl.when(i == 0)
        def _():
            dcw_ref[...] = jnp.zeros(dcw_ref.shape, F32)

        first, lastt = i > 0, i < nt - 1
        dcv = _f32(dc_in)
        d_up, d_un = _shift_rows(dcv, jnp.where(first, _last_row(dprev), 0.0), jnp.where(lastt, _first_row(dnext), 0.0))
        cgv, hcv = _f32(cg), _f32(hc)
        u = cgv * hcv
        u_up, u_un = _shift_rows(u, jnp.where(first, _last_row(cprev) * _last_row(hprev), 0.0),
                                 jnp.where(lastt, _first_row(cnext) * _first_row(hnext), 0.0))
        du = cw[0:1, :] * d_un + cw[1:2, :] * dcv + cw[2:3, :] * d_up
        dch_ref[:, :CONV_W] = (du * hcv).astype(BF16)
        dch_ref[:, CONV_W:] = (du * cgv).astype(BF16)
        dcw_ref[0:1, :] += jnp.sum(dcv * u_up, axis=0, keepdims=True)
        dcw_ref[1:2, :] += jnp.sum(dcv * u, axis=0, keepdims=True)
        dcw_ref[2:3, :] += jnp.sum(dcv * u_un, axis=0, keepdims=True)

    def col(off):
        return pl.BlockSpec((tm, CONV_W), lambda i: (i, off // CONV_W))

    rowt = pl.BlockSpec((tm, CONV_W), lambda i: (i, 0))
    const = lambda shape: pl.BlockSpec(shape, lambda i: (0, 0))
    return pl.pallas_call(
        body, name="conv_bwd",
        out_shape=(jax.ShapeDtypeStruct((seq, CH_W), BF16), jax.ShapeDtypeStruct((8, CONV_W), F32)),
        grid=(nt,),
        in_specs=[rowt] + _halo_specs(tm, seq, 0) + [col(OFF_C), col(OFF_H)]
        + _halo_specs(tm, seq, OFF_C // CONV_W) + _halo_specs(tm, seq, OFF_H // CONV_W) + [const((8, CONV_W))],
        out_specs=(pl.BlockSpec((tm, CH_W), lambda i: (i, 0)), const((8, CONV_W))),
        compiler_params=_cparams("arbitrary"),
    )(dconv, dconv, dconv, proj, proj, proj, proj, proj, proj, conv_w)


def _gla_bwd(proj, lr, do, st_f, st_b, wgk_f, wgk_b, bgk_f, bgk_b, tt):
    seq = proj.shape[0]
    nb, nc = seq // tt, tt // CHUNK

    def body(qf, kf, vf, lrf, dof, stf, qb, kb, vb, lrb, dob, stb, wf, wb, bf, bb,
             dqkv_f, dlr_f, dqkv_b, dlr_b, dwf, dwb, dbf, dbb,
             ds_scr, eq_s, ek_s, ein_s, eout_s, qs_s, ks_s, qin_s, kout_s, db_s, lg_s):
        @pl.when(pl.program_id(0) == 0)
        def _():
            ds_scr[...] = jnp.zeros(ds_scr.shape, F32)
            for r in (dwf, dwb, dbf, dbb):
                r[...] = jnp.zeros(r.shape, F32)

        low, upp, sup = _block_masks(tt)
        row = lax.broadcasted_iota(jnp.int32, (CHUNK, 1), 0)
        kmask = _chunk_column_mask(tt)
        dirs = ((qf, kf, vf, lrf, dof, stf, wf, bf, dqkv_f, dlr_f, dwf, dbf,
                 low, upp, low, REF_F, LAST_F, list(reversed(range(nc)))),
                (qb, kb, vb, lrb, dob, stb, wb, bb, dqkv_b, dlr_b, dwb, dbb,
                 upp, low, sup, REF_B, LAST_B, list(range(nc))))
        for d, (q_r, k_r, v_r, lr_r, do_r, st_r, w_r, b_r, dqkv_r, dlr_r, dw_r, db_r,
                cum, cum_t, mask, ref, last, order) in enumerate(dirs):
            lrv = lr_r[...].astype(BF16)
            wv = w_r[...]
            logits = _dot(lrv, wv) + b_r[...]
            lg_s[...] = logits
            b = _dot_split3(cum.astype(BF16), _log_gate(logits))
            decs = []
            for c in range(nc):
                rows = slice(c * CHUNK, (c + 1) * CHUNK)
                bc = b[rows]
                b_ref, b_last = bc[ref:ref + 1], bc[last:last + 1]
                qc = q_r[rows, :].astype(F32) * QSCALE
                kc = k_r[rows, :].astype(F32)
                e_q, e_k, e_in, e_out = jnp.exp(bc - b_ref), jnp.exp(b_ref - bc), jnp.exp(bc), jnp.exp(b_last - bc)
                eq_s[rows, :], ek_s[rows, :], ein_s[rows, :], eout_s[rows, :] = e_q, e_k, e_in, e_out
                qs_s[rows, :] = (qc * e_q).astype(BF16)
                ks_s[rows, :] = (kc * e_k).astype(BF16)
                qin_s[rows, :] = (qc * e_in).astype(BF16)
                kout_s[rows, :] = (kc * e_out).astype(BF16)
                decs.append(jnp.exp(b_last))
            for h in range(HEADS):
                ksl = slice(h * DK, (h + 1) * DK)
                vsl = slice(h * DV, (h + 1) * DV)
                v = v_r[:, vsl].astype(BF16)
                dov = do_r[:, vsl].astype(BF16)
                qsb, ksb = qs_s[:, ksl], ks_s[:, ksl]
                att = jnp.where(mask, _dot_nt(qsb, ksb), 0.0).astype(BF16)
                datt = jnp.where(mask, _dot_nt(dov, v), 0.0).astype(BF16)
                dqs = _dot(datt, ksb)
                dks = _dot_tn(datt, qsb)
                dv_intra = _dot_tn(att, dov)
                g_t = _dot_tn(dov, _chunked(kmask, qin_s[:, ksl], nc))
                ds = ds_scr[d * HEADS + h]
                for c in order:
                    rows = slice(c * CHUNK, (c + 1) * CHUNK)
                    dsb = ds.astype(BF16)
                    s_prev = st_r[c, h]
                    dk_out = _dot(v[rows], dsb)
                    dq_in = _dot(dov[rows], s_prev)
                    dv = dv_intra[rows] + _dot_nt(kout_s[rows, ksl], dsb)
                    dqkv_r[rows, OFF_V + h * DV:OFF_V + (h + 1) * DV] = dv.astype(BF16)
                    dec = decs[c][:, ksl]
                    ddec = jnp.sum(ds * s_prev.astype(F32), axis=0, keepdims=True)
                    e_out = eout_s[rows, ksl]
                    qc = q_r[rows, ksl].astype(F32) * QSCALE
                    kc = k_r[rows, ksl].astype(F32)
                    dq = dqs[rows] * eq_s[rows, ksl] + dq_in * ein_s[rows, ksl]
                    dk = dks[rows] * ek_s[rows, ksl] + dk_out * e_out
                    dqkv_r[rows, OFF_Q + h * DK:OFF_Q + (h + 1) * DK] = (dq * QSCALE).astype(BF16)
                    dqkv_r[rows, OFF_K + h * DK:OFF_K + (h + 1) * DK] = dk.astype(BF16)
                    tail = jnp.sum(dk_out * (kc * e_out), axis=0, keepdims=True) + ddec * dec
                    db_s[rows, ksl] = (qc * dq - kc * dk) + jnp.where(row == last, tail, 0.0)
                    ds = ds * dec + g_t[:, c * DK:(c + 1) * DK]
                ds_scr[d * HEADS + h] = ds
            dg = _dot_split3(cum_t.astype(BF16), db_s[...])
            dlogit = (dg * GATE_SCALE) * _sigmoid(-lg_s[...])
            dlb = dlogit.astype(BF16)
            dlr_r[...] = _dot_nt(dlb, wv)
            dw_r[...] += _dot_tn(lrv, dlb)
            db_r[...] += jnp.sum(dlogit, axis=0, keepdims=True)

    fw = lambda i: (nb - 1 - i, 0)
    bw = lambda i: (i, 0)
    const = lambda i: (0, 0)

    def tok_specs(m):
        return [pl.BlockSpec((tt, QK_W), lambda i: (m(i)[0], OFF_Q // QK_W)),
                pl.BlockSpec((tt, QK_W), lambda i: (m(i)[0], OFF_K // QK_W)),
                pl.BlockSpec((tt, V_W), lambda i: (m(i)[0], OFF_V // V_W)),
                pl.BlockSpec((tt, LR_W), m),
                pl.BlockSpec((tt, V_W), m),
                pl.BlockSpec((nc, HEADS, DV, DK), lambda i: (m(i)[0], 0, 0, 0))]

    dqkv = jax.ShapeDtypeStruct((seq, QK_W + QK_W + V_W), BF16)
    dlr = jax.ShapeDtypeStruct((seq, LR_W), F32)
    dw = jax.ShapeDtypeStruct((LR_W, QK_W), F32)
    dbias = jax.ShapeDtypeStruct((1, QK_W), F32)
    return pl.pallas_call(
        body, name="gla_bwd",
        out_shape=(dqkv, dlr, dqkv, dlr, dw, dw, dbias, dbias),
        grid=(nb,),
        in_specs=tok_specs(fw) + tok_specs(bw) + [
            pl.BlockSpec((LR_W, QK_W), const), pl.BlockSpec((LR_W, QK_W), const),
            pl.BlockSpec((1, QK_W), const), pl.BlockSpec((1, QK_W), const)],
        out_specs=(pl.BlockSpec((tt, QK_W + QK_W + V_W), fw), pl.BlockSpec((tt, LR_W), fw),
                   pl.BlockSpec((tt, QK_W + QK_W + V_W), bw), pl.BlockSpec((tt, LR_W), bw),
                   pl.BlockSpec((LR_W, QK_W), const), pl.BlockSpec((LR_W, QK_W), const),
                   pl.BlockSpec((1, QK_W), const), pl.BlockSpec((1, QK_W), const)),
        scratch_shapes=[pltpu.VMEM((2 * HEADS, DV, DK), F32)] + [pltpu.VMEM((tt, QK_W), F32)] * 4
        + [pltpu.VMEM((tt, QK_W), BF16)] * 4 + [pltpu.VMEM((tt, QK_W), F32)] * 2,
        compiler_params=_cparams("arbitrary"),
    )(proj, proj, proj, lr, do, st_f, proj, proj, proj, lr, do, st_b, wgk_f, wgk_b, bgk_f, bgk_b)


def _sum_directions(dqkv_f, dqkv_b, dlr_f, dlr_b, tm):
    seq = dqkv_f.shape[0]

    def body(a, b, la, lb, dp_out, dlr_out):
        dp_out[...] = (_f32(a) + _f32(b)).astype(BF16)
        dlr_out[...] = (la[...] + lb[...]).astype(BF16)

    rowt = pl.BlockSpec((tm, QKV_W), lambda i: (i, 0))
    lrt = pl.BlockSpec((tm, LR_W), lambda i: (i, 0))
    return pl.pallas_call(
        body, name="sum_directions",
        out_shape=(jax.ShapeDtypeStruct((seq, QKV_W), BF16), jax.ShapeDtypeStruct((seq, LR_W), BF16)),
        grid=(seq // tm,),
        in_specs=[rowt, rowt, lrt, lrt],
        out_specs=(rowt, lrt),
        compiler_params=_cparams("arbitrary"),
    )(dqkv_f, dqkv_b, dlr_f, dlr_b)


def _input_grad(dp_qkv, dp_gates, dp_ch, dlr, w_nat, x2d, norm_g, dx2, sums, tm):
    seq = x2d.shape[0]
    nt, n = seq // tm, len(sums)

    def body(dq, dg, dc, dl, w, x_ref, g_ref, dx2_ref, *rest):
        ins, (gx_ref, dng_ref), outs = rest[:n], rest[n:n + 2], rest[n + 2:2 * n + 2]
        send_sems, recv_sems = rest[2 * n + 2:]
        i = pl.program_id(0)

        @pl.when(i == 0)
        def _():
            for cp in _chip_copies(ins, outs, send_sems, recv_sems):
                cp.start()
            dng_ref[...] = jnp.zeros(dng_ref.shape, F32)

        dh = (_dot(dl[...], w[NAT_LR:NAT_LR + LR_W, :]) + _dot(dq[...], w[0:NAT_ZA, :])
              + _dot(dg[:, 0:CONV_W], w[NAT_ZA:NAT_LR, :]) + _dot(dg[:, CONV_W:2 * CONV_W], w[NAT_B:NAT_C, :])
              + _dot(dg[:, 2 * CONV_W:], w[NAT_ZC:IN_W, :]) + _dot(dc[...], w[NAT_C:NAT_ZC, :]))
        xv = x_ref[...]
        r = lax.rsqrt(jnp.mean(xv * xv, axis=-1, keepdims=True) + EPS)
        xn = xv * r
        dng_ref[...] += jnp.sum(dh * xn, axis=0, keepdims=True)
        dn = dh * g_ref[...]
        gx_ref[...] = (r * dn - xn * (r * jnp.mean(dn * xn, axis=-1, keepdims=True))) + dx2_ref[...]

        @pl.when(i == nt - 1)
        def _():
            copies = _chip_copies(ins, outs, send_sems, recv_sems)
            for cp in copies:
                cp.wait_recv()
            for cp in copies:
                cp.wait_send()

    rowt = pl.BlockSpec((tm, D_MODEL), lambda i: (i, 0))
    seg = lambda width: pl.BlockSpec((tm, width), lambda i: (i, 0))
    resident = lambda rows: pl.BlockSpec((rows, D_MODEL), lambda i: (0, 0), pipeline_mode=pl.Buffered(1))
    hbm = pl.BlockSpec(memory_space=pl.ANY)
    return pl.pallas_call(
        body, name="input_grad",
        out_shape=(jax.ShapeDtypeStruct((seq, D_MODEL), F32), jax.ShapeDtypeStruct((1, D_MODEL), F32))
        + tuple(jax.ShapeDtypeStruct((3,) + s.shape[1:], s.dtype) for s in sums),
        grid=(nt,),
        in_specs=[seg(QKV_W), seg(GATES_W), seg(CH_W), seg(LR_W), resident(IN_W),
                  rowt, pl.BlockSpec((1, D_MODEL), lambda i: (0, 0)), rowt] + [hbm] * n,
        out_specs=(rowt, pl.BlockSpec((1, D_MODEL), lambda i: (0, 0))) + (hbm,) * n,
        scratch_shapes=[pltpu.SemaphoreType.DMA((3 * n,)), pltpu.SemaphoreType.DMA((3 * n,))],
        compiler_params=_cparams("arbitrary"),
    )(dp_qkv, dp_gates, dp_ch, dlr, w_nat, x2d, norm_g, dx2, *sums)


def _weight_grad_out(y_t, dx2b, tk, riding):
    m, seq = y_t.shape
    n = dx2b.shape[1]
    nk = seq // tk

    def body(a_ref, b_ref, ride_in, o_ref, ride_out, send_sems, recv_sems):
        k = pl.program_id(0)

        @pl.when(k == 0)
        def _():
            _start_all(_sibling_copies(ride_in, ride_out, send_sems, recv_sems))
            o_ref[...] = jnp.zeros(o_ref.shape, F32)

        o_ref[...] += _dot(a_ref[...], b_ref[...])

        @pl.when(k == nk - 1)
        def _():
            _wait_all(_sibling_copies(ride_in, ride_out, send_sems, recv_sems))

    hbm = pl.BlockSpec(memory_space=pl.ANY)
    return pl.pallas_call(
        body, name="wgrad_out",
        out_shape=(jax.ShapeDtypeStruct((m, n), F32), jax.ShapeDtypeStruct((4,) + _block_shape(riding), F32)),
        grid=(nk,),
        in_specs=[pl.BlockSpec((m, tk), lambda k: (0, k)), pl.BlockSpec((tk, n), lambda k: (k, 0)), hbm],
        out_specs=(pl.BlockSpec((m, n), lambda k: (0, 0)), hbm),
        scratch_shapes=[pltpu.SemaphoreType.DMA((4,)), pltpu.SemaphoreType.DMA((4,))],
        compiler_params=_cparams("arbitrary"),
    )(y_t, dx2b, riding)


def _weight_grad_in(h_t, dp_qkv, dp_gates, dp_ch, dlr):
    m, seq = h_t.shape
    tn = 512
    n_qkv, n_gates, n_ch = QKV_W // tn, GATES_W // tn, CH_W // tn
    starts = ([k * tn for k in range(n_qkv)] + [NAT_ZA, NAT_ZA + tn, NAT_B, NAT_B + tn, NAT_ZC, NAT_ZC + tn]
              + [NAT_C + k * tn for k in range(n_ch)])

    def out_row(j):
        row = 0
        for k, start in enumerate(starts):
            row = row + jnp.where(j == k, start // 32, 0)
        return pl.multiple_of(row * 32, 32), 0

    def body(a_ref, bq, bg, bc, o_ref, acc):
        j = pl.program_id(0)

        @pl.when(j < n_qkv)
        def _():
            acc[...] = _dot(a_ref[...], bq[...])

        @pl.when(jnp.logical_and(j >= n_qkv, j < n_qkv + n_gates))
        def _():
            acc[...] = _dot(a_ref[...], bg[...])

        @pl.when(j >= n_qkv + n_gates)
        def _():
            acc[...] = _dot(a_ref[...], bc[...])

        o_ref[...] = acc[...].T

    resident = pl.BlockSpec((m, seq), lambda j: (0, 0), pipeline_mode=pl.Buffered(1))
    seg = lambda first, count: pl.BlockSpec((seq, tn), lambda j: (0, jnp.clip(j - first, 0, count - 1)))
    main = pl.pallas_call(
        body, name="wgrad_in",
        out_shape=jax.ShapeDtypeStruct((IN_W, m), F32),
        grid=(n_qkv + n_gates + n_ch,),
        in_specs=[resident, seg(0, n_qkv), seg(n_qkv, n_gates), seg(n_qkv + n_gates, n_ch)],
        out_specs=pl.BlockSpec((pl.Element(tn), pl.Element(m)), out_row),
        scratch_shapes=[pltpu.VMEM((m, tn), F32)],
        compiler_params=_cparams("arbitrary"),
    )(h_t, dp_qkv, dp_gates, dp_ch)

    def lr_body(a_ref, b_ref, full_ref, o_ref, acc):
        acc[...] = _dot(a_ref[...], b_ref[...])
        o_ref[...] = acc[...].T[0:2 * RANK, :]

    whole = lambda shape: pl.BlockSpec(shape, lambda j: (0, 0))
    return pl.pallas_call(
        lr_body, name="wgrad_lr",
        out_shape=jax.ShapeDtypeStruct((IN_W, m), F32),
        grid=(1,),
        in_specs=[whole((m, seq)), whole((seq, LR_W)), pl.BlockSpec(memory_space=pl.ANY)],
        out_specs=pl.BlockSpec((pl.Element(2 * RANK), pl.Element(m)), lambda j: (NAT_LR, 0)),
        scratch_shapes=[pltpu.VMEM((m, LR_W), F32)],
        input_output_aliases={2: 0},
        compiler_params=_cparams("arbitrary"),
    )(h_t, dlr, main)


def _pad_rows(a, rows):
    return jnp.pad(a, ((0, rows - a.shape[0]), (0, 0)))


def _rows128(a):
    a = a.reshape(-1, 128)
    return _pad_rows(a, -(-a.shape[0] // 8) * 8)


def _pack(arrs):
    return jnp.concatenate([_rows128(a) for a in arrs], axis=0)


def _unpack(buf, like):
    out, start = [], 0
    for a in like:
        rows = a.size // 128
        out.append(buf[start:start + rows].reshape(a.shape))
        start += -(-rows // 8) * 8
    return out


def kernel(x, norm_g, w_in, w_gk_f, b_gk_f, w_gk_b, b_gk_b, gla_norm_g, conv_w, conv_b, w_out, final_g, loss_target, m_norm_g, m_w_in, m_w_gk_f, m_b_gk_f, m_w_gk_b, m_b_gk_b, m_gla_norm_g, m_conv_w, m_conv_b, m_w_out, m_final_g, v_norm_g, v_w_in, v_w_gk_f, v_b_gk_f, v_w_gk_b, v_b_gk_b, v_gla_norm_g, v_conv_w, v_conv_b, v_w_out, v_final_g):
    px, py, pc = _position()
    me = _blk(px, py, pc)
    seq = x.shape[1]
    x2d, tgt = x[0], loss_target[0]
    tm = min(512, seq)
    tt = min(256, seq)

    small_s = jnp.concatenate([jnp.concatenate([w_gk_f[0], w_gk_b[0]], axis=1), _pad_rows(conv_w[0], 8)], axis=0)
    shifted = lax.dynamic_update_slice(jnp.zeros((SHIFTED_ROWS, D_MODEL), F32), w_in[0].T, (4 * (me % 4), 0))
    order = sum(jnp.where(2 * px + py == k, jnp.asarray(tiles + (0,), jnp.int32), 0) for k, tiles in enumerate(TILE_ORDER))
    proj, lr, h_t, w_nat, wout_all, small_all = _gather_inproj(x2d, norm_g, shifted, w_out[0], small_s, order,
                                                               min(1024, seq))
    w_out_full = wout_all.reshape(MIX_W, D_MODEL)
    wgk_cols = 512 // N_DEV
    wgk_f_full = small_all[:, 0:RANK, 0:wgk_cols].transpose(1, 0, 2).reshape(RANK, QK_W)
    wgk_b_full = small_all[:, 0:RANK, wgk_cols:2 * wgk_cols].transpose(1, 0, 2).reshape(RANK, QK_W)
    conv_w_full = _pad_rows(small_all[:, RANK:RANK + 3, :].transpose(1, 0, 2).reshape(3, CONV_W), 8)
    zr = lambda n: jnp.zeros((n, QK_W), F32)
    wgk_f_pad = jnp.concatenate([wgk_f_full, zr(LR_W - RANK)], axis=0).astype(BF16)
    wgk_b_pad = jnp.concatenate([zr(RANK), wgk_b_full, zr(LR_W - 2 * RANK)], axis=0).astype(BF16)

    o_f, o_b, st_f, st_b = _gla_fwd(proj, lr, wgk_f_pad, wgk_b_pad, b_gk_f, b_gk_b, tt)
    tmix = min(256, seq)
    y_t, conv, dx2, dx2b, loss_p, dfg_p = _mix_out_loss(o_f, o_b, proj, x2d, tgt, gla_norm_g, conv_w_full, conv_b,
                                                        w_out_full, final_g.reshape(1, D_MODEL), tmix)

    dp_gates, do, dconv, dgg_p, dcb_p = _mix_bwd(dx2b, o_f, o_b, proj, conv, gla_norm_g, w_out_full, tmix)
    dp_ch, dcw_p = _conv_bwd(dconv, proj, conv_w_full, tmix)
    dqkv_f, dlr_f, dqkv_b, dlr_b, dwf_p, dwb_p, dbf_p, dbb_p = _gla_bwd(
        proj, lr, do, st_f, st_b, wgk_f_pad, wgk_b_pad, b_gk_f, b_gk_b, tt)
    dp_qkv, dlr = _sum_directions(dqkv_f, dqkv_b, dlr_f, dlr_b, tm)
    dw_nat = _weight_grad_in(h_t, dp_qkv, dp_gates, dp_ch, dlr)

    dw_out, sib_in = _weight_grad_out(y_t, dx2b, tm, dw_nat)
    part_out = dw_out.reshape(N_DEV, MIX_W // N_DEV, D_MODEL)
    core = jnp.reshape(pc, (1,)).astype(jnp.int32)
    chip = jnp.reshape(2 * px + py, (1,)).astype(jnp.int32)
    sums_in, sib_out = _chip_sums(dw_nat, sib_in, core, 256, "chip_sums_in", riding=part_out)
    sums_out = _chip_sums(part_out, sib_out, core, 256, "chip_sums_out")
    grad_x2d, dng_p, far_in, far_out = _input_grad(dp_qkv, dp_gates, dp_ch, dlr, w_nat, x2d, norm_g, dx2,
                                                   [sums_in, sums_out], tmix)
    pieces = [dng_p, dbf_p, dbb_p, dgg_p, dcb_p, dfg_p[0], dwf_p[0:RANK], dwb_p[RANK:2 * RANK], dcw_p[0:3], loss_p[0]]
    g_window, small_tot = _final_sum(sums_in, far_in, chip, _pack(pieces), 256, "final_sum_in")
    g_in_t = lax.dynamic_slice_in_dim(g_window, 4 * pc, SHARD_W, axis=0)
    g_w_out, d_w_out, nm_w_out, nv_w_out = _final_sum_adamw(sums_out, far_out, chip, w_out[0], m_w_out[0], v_w_out[0],
                                                            256, "adamw_out")
    flat = lambda a: a[0].T.reshape(SHARD_W, D_MODEL // 128, 128)
    unflat = lambda a: a.reshape(SHARD_W, D_MODEL).T
    d_flat, m_flat, v_flat = _adamw_rows(g_in_t.reshape(SHARD_W, D_MODEL // 128, 128), flat(w_in), flat(m_w_in),
                                         flat(v_w_in), 90, "adamw_in")
    g_w_in, d_w_in, nm_w_in, nv_w_in = g_in_t.T, unflat(d_flat), unflat(m_flat), unflat(v_flat)

    tot = _unpack(small_tot, pieces)
    g_norm_g, g_b_gk_f, g_b_gk_b, g_gla, g_conv_b, g_final = tot[:6]
    g_wgk_f = lax.dynamic_slice_in_dim(tot[6], me * wgk_cols, wgk_cols, axis=1)[None]
    g_wgk_b = lax.dynamic_slice_in_dim(tot[7], me * wgk_cols, wgk_cols, axis=1)[None]
    g_conv_w = lax.dynamic_slice_in_dim(tot[8], me * 128, 128, axis=1)[None]
    loss = tot[9][0]

    small_g = [g_norm_g, g_b_gk_f, g_b_gk_b, g_gla, g_conv_b, g_final, g_wgk_f, g_wgk_b, g_conv_w]
    small_w = [norm_g, b_gk_f, b_gk_b, gla_norm_g, conv_b, final_g, w_gk_f, w_gk_b, conv_w]
    small_m = [m_norm_g, m_b_gk_f, m_b_gk_b, m_gla_norm_g, m_conv_b, m_final_g, m_w_gk_f, m_w_gk_b, m_conv_w]
    small_v = [v_norm_g, v_b_gk_f, v_b_gk_b, v_gla_norm_g, v_conv_b, v_final_g, v_w_gk_f, v_w_gk_b, v_conv_w]
    d_s, m_s, v_s = _adamw_small(_pack(small_g), _pack(small_w), _pack(small_m), _pack(small_v))
    d_l, m_l, v_l = _unpack(d_s, small_w), _unpack(m_s, small_w), _unpack(v_s, small_w)

    def ordered(sm, big_in, big_out):
        return [sm[0], big_in[None], sm[6], sm[1], sm[7], sm[2], sm[3], sm[8], sm[4], big_out[None], sm[5]]

    grads = ordered(small_g, g_w_in, g_w_out)
    deltas = ordered(d_l, d_w_in, d_w_out)
    new_m = ordered(m_l, nm_w_in, nm_w_out)
    new_v = ordered(v_l, nv_w_in, nv_w_out)
    return (loss, grad_x2d[None], *grads, *deltas, *new_m, *new_v)
```

```python
import jax
import jax.numpy as jnp
from jax import lax
from jax.experimental import pallas as pl
from jax.experimental.pallas import tpu as pltpu

F32 = jnp.float32
BF16 = jnp.bfloat16
MESH = pl.DeviceIdType.MESH

N_DEV = 8
D_MODEL = 1024
HEADS = 4
DK = 128
DV = 256
QK_W = HEADS * DK
V_W = HEADS * DV
CONV_W = 1024
MIX_W = V_W + CONV_W
CHUNK = 64
RANK = 16
IN_W = 7200
SHARD_W = IN_W // N_DEV
MAIN_W = 7168
LR_W = 128
OFF_Q, OFF_K, OFF_V, OFF_ZA, OFF_B, OFF_ZC, OFF_C, OFF_H = 0, 512, 1024, 2048, 3072, 4096, 5120, 6144
QKV_W, GATES_W, CH_W = 2048, 3072, 2048
NAT_ZA, NAT_LR, NAT_B, NAT_C, NAT_ZC = 2048, 3072, 3104, 4128, 6176
EPS = 1e-6
GATE_SCALE = 1.0 / 16.0
QSCALE = DK ** -0.5
REF_F, LAST_F = CHUNK // 2, CHUNK - 1
REF_B, LAST_B = CHUNK - 1 - CHUNK // 2, 0

ADAM_LR = 0.001
ADAM_B1 = 0.9
ADAM_B2 = 0.999
ADAM_EPS = 1e-08
ADAM_WD = 0.01
ADAM_STEP = 10

VMEM_LIMIT = 56 * 1024 * 1024


def _cparams(*sem):
    return pltpu.CompilerParams(dimension_semantics=sem, vmem_limit_bytes=VMEM_LIMIT)


def _dot(a, b):
    return jnp.dot(a, b, preferred_element_type=F32)


def _dot_nt(a, b):
    return lax.dot_general(a, b, (((1,), (1,)), ((), ())), preferred_element_type=F32)


def _dot_tn(a, b):
    return lax.dot_general(a, b, (((0,), (0,)), ((), ())), preferred_element_type=F32)


def _sigmoid(z):
    return jax.nn.sigmoid(z)


def _position():
    return lax.axis_index("x"), lax.axis_index("y"), lax.axis_index("c")


def _blk(px, py, pc):
    return 4 * px + 2 * py + pc


EDGE = 16
SHIFTED_ROWS = 912
BODY_ROWS = SHIFTED_ROWS - 2 * EDGE


def _first_tile_row(blk, px):
    return EDGE * (56 * blk + px)


def _edge_tiles():
    tiles = {}
    for blk in range(N_DEV):
        first = _first_tile_row(blk, blk // 4)
        tiles.setdefault(first, []).append((blk, 0))
        tiles.setdefault(first + EDGE + BODY_ROWS, []).append((blk, 1))
    return tiles


def _peer_copies(srcs, outs, send_sems, recv_sems):
    x, y, c = _position()
    me = _blk(x, y, c)
    copies = []
    for a, (src, out) in enumerate(zip(srcs, outs)):
        k = 0
        for dx in (0, 1):
            for dy in (0, 1):
                for dc in (0, 1):
                    if dx + dy + dc == 0:
                        continue
                    peer = (1 - x if dx else x, 1 - y if dy else y, 1 - c if dc else c)
                    copies.append(pltpu.make_async_remote_copy(
                        src_ref=src, dst_ref=out.at[me], send_sem=send_sems.at[a * 7 + k],
                        recv_sem=recv_sems.at[a * 7 + k], device_id=peer, device_id_type=MESH))
                    k += 1
    return copies


def _chip_copies(ins, outs, send_sems, recv_sems):
    x, y, c = _position()
    chips = [(1 - x, y), (x, 1 - y), (1 - x, 1 - y)]
    copies = []
    for a in range(len(ins)):
        for j, (px, py) in enumerate(chips):
            copies.append(pltpu.make_async_remote_copy(
                src_ref=ins[a].at[2 * px + py], dst_ref=outs[a].at[j],
                send_sem=send_sems.at[a * 3 + j], recv_sem=recv_sems.at[a * 3 + j],
                device_id=(px, py, c), device_id_type=MESH))
    return copies


WINDOW_ROWS = SHARD_W + 4


def _window_start(k, parity):
    return 2 * SHARD_W * k + (SHARD_W - 4) * parity


def _owner_block(part, k, parity):
    if part.ndim == 3:
        return part.at[2 * k + parity]
    return part.at[pl.ds(pl.multiple_of(_window_start(k, parity), 8), WINDOW_ROWS)]


def _block_shape(part):
    return part.shape[1:] if part.ndim == 3 else (WINDOW_ROWS, part.shape[1])


def _sibling_copies(part, out, send_sems, recv_sems):
    x, y, c = _position()
    return [pltpu.make_async_remote_copy(src_ref=_owner_block(part, k, 1 - c), dst_ref=out.at[k],
                                         send_sem=send_sems.at[k], recv_sem=recv_sems.at[k],
                                         device_id=(x, y, 1 - c), device_id_type=MESH)
            for k in range(4)]


def _start_all(copies):
    for cp in copies:
        cp.start()


def _wait_all(copies):
    for cp in copies:
        cp.wait_recv()
    for cp in copies:
        cp.wait_send()


def _chip_sums(part, from_sibling, core, tc, name, riding=None):
    rows, cols = _block_shape(part)
    nj = cols // tc

    def body(core_ref, p_ref, s_ref, *rest):
        if riding is None:
            (o_ref,) = rest
        else:
            ride_in, o_ref, ride_out, send_sems, recv_sems = rest
            k, j = pl.program_id(0), pl.program_id(1)

            @pl.when(jnp.logical_and(k == 0, j == 0))
            def _():
                _start_all(_sibling_copies(ride_in, ride_out, send_sems, recv_sems))

        o_ref[0] = (p_ref[...].reshape(rows, tc) + s_ref[0]).astype(BF16)

        if riding is not None:
            @pl.when(jnp.logical_and(k == 3, j == nj - 1))
            def _():
                _wait_all(_sibling_copies(ride_in, ride_out, send_sems, recv_sems))

    hbm = pl.BlockSpec(memory_space=pl.ANY)
    sums = jax.ShapeDtypeStruct((4, rows, cols), BF16)
    tile_out = pl.BlockSpec((1, rows, tc), lambda k, j, core_ref: (k, 0, j))
    if part.ndim == 3:
        mine = pl.BlockSpec((1, rows, tc), lambda k, j, core_ref: (2 * k + core_ref[0], 0, j))
    else:
        mine = pl.BlockSpec((pl.Element(rows), pl.Element(tc)),
                            lambda k, j, core_ref: (pl.multiple_of(_window_start(k, core_ref[0]), 8),
                                                    pl.multiple_of(j * tc, 128)))
    in_specs = [mine, pl.BlockSpec((1, rows, tc), lambda k, j, core_ref: (k, 0, j))]
    if riding is None:
        out_shape, out_specs, scratch, args = sums, tile_out, [], (core, part, from_sibling)
    else:
        out_shape = (sums, jax.ShapeDtypeStruct((4,) + _block_shape(riding), F32))
        out_specs, in_specs = (tile_out, hbm), in_specs + [hbm]
        scratch = [pltpu.SemaphoreType.DMA((4,)), pltpu.SemaphoreType.DMA((4,))]
        args = (core, part, from_sibling, riding)
    return pl.pallas_call(
        body, name=name, out_shape=out_shape,
        grid_spec=pltpu.PrefetchScalarGridSpec(num_scalar_prefetch=1, grid=(4, nj), in_specs=in_specs,
                                               out_specs=out_specs, scratch_shapes=scratch),
        compiler_params=_cparams("arbitrary", "arbitrary"),
    )(*args)


def _sum_chips(s_ref, r_ref):
    f = lambda a: a.astype(F32)
    return ((f(s_ref[0]) + f(r_ref[0])) + f(r_ref[1])) + f(r_ref[2])


def _final_sum(sums, from_chips, chip, small, tc, name):
    _, rows, cols = sums.shape
    nj = cols // tc

    def body(chip_ref, s_ref, r_ref, sm_ref, g_out, tot_ref, all_ref, send_sems, recv_sems):
        j = pl.program_id(0)
        me = _blk(*_position())

        @pl.when(j == 0)
        def _():
            all_ref[me] = sm_ref[...]
            _start_all(_peer_copies((all_ref.at[me],), (all_ref,), send_sems, recv_sems))

        g_out[...] = _sum_chips(s_ref, r_ref)

        @pl.when(j == nj - 1)
        def _():
            _wait_all(_peer_copies((all_ref.at[me],), (all_ref,), send_sems, recv_sems))
            acc = all_ref[0]
            for d in range(1, N_DEV):
                acc = acc + all_ref[d]
            tot_ref[...] = acc

    whole = pl.BlockSpec(small.shape, lambda j, chip_ref: (0, 0))
    return pl.pallas_call(
        body, name=name,
        out_shape=(jax.ShapeDtypeStruct((rows, cols), F32), jax.ShapeDtypeStruct(small.shape, F32)),
        grid_spec=pltpu.PrefetchScalarGridSpec(
            num_scalar_prefetch=1, grid=(nj,),
            in_specs=[pl.BlockSpec((1, rows, tc), lambda j, chip_ref: (chip_ref[0], 0, j)),
                      pl.BlockSpec((3, rows, tc), lambda j, chip_ref: (0, 0, j)), whole],
            out_specs=(pl.BlockSpec((rows, tc), lambda j, chip_ref: (0, j)), whole),
            scratch_shapes=[pltpu.VMEM((N_DEV,) + small.shape, F32), pltpu.SemaphoreType.DMA((7,)),
                            pltpu.SemaphoreType.DMA((7,))]),
        compiler_params=_cparams("arbitrary"),
    )(chip, sums, from_chips, small)


def _adamw_rows(g, w, m, v, tr, name):
    rows = g.shape[0]

    def body(g_ref, w_ref, m_ref, v_ref, d_out, m_out, v_out):
        delta, m_new, v_new = _adamw(w_ref[...], g_ref[...], m_ref[...], v_ref[...])
        d_out[...] = delta
        m_out[...] = m_new
        v_out[...] = v_new

    tile = pl.BlockSpec((tr,) + g.shape[1:], lambda r: (r, 0, 0))
    shp = jax.ShapeDtypeStruct(g.shape, F32)
    return pl.pallas_call(
        body, name=name, out_shape=(shp, shp, shp), grid=(rows // tr,),
        in_specs=[tile] * 4, out_specs=(tile, tile, tile),
        compiler_params=_cparams("arbitrary"),
    )(g, w, m, v)


def _adamw(w, g, m, v):
    m = ADAM_B1 * m + (1.0 - ADAM_B1) * g
    v = ADAM_B2 * v + (1.0 - ADAM_B2) * (g * g)
    m_hat = m / (1.0 - ADAM_B1 ** ADAM_STEP)
    v_hat = v / (1.0 - ADAM_B2 ** ADAM_STEP)
    delta = -ADAM_LR * (m_hat / (jnp.sqrt(v_hat) + ADAM_EPS) + ADAM_WD * w)
    return delta, m, v


def _final_sum_adamw(sums, from_chips, chip, w, m, v, tr, name):
    rows, cols = w.shape

    def body(chip_ref, s_ref, r_ref, w_ref, m_ref, v_ref, g_out, d_out, m_out, v_out):
        g = _sum_chips(s_ref, r_ref)
        delta, m_new, v_new = _adamw(w_ref[...], g, m_ref[...], v_ref[...])
        g_out[...] = g
        d_out[...] = delta
        m_out[...] = m_new
        v_out[...] = v_new

    tile = pl.BlockSpec((tr, cols), lambda r, chip_ref: (r, 0))
    shp = jax.ShapeDtypeStruct((rows, cols), F32)
    return pl.pallas_call(
        body, name=name,
        out_shape=(shp, shp, shp, shp),
        grid_spec=pltpu.PrefetchScalarGridSpec(
            num_scalar_prefetch=1, grid=(rows // tr,),
            in_specs=[pl.BlockSpec((1, tr, cols), lambda r, chip_ref: (chip_ref[0], r, 0)),
                      pl.BlockSpec((3, tr, cols), lambda r, chip_ref: (0, r, 0)),
                      tile, tile, tile],
            out_specs=(tile, tile, tile, tile)),
        compiler_params=_cparams("arbitrary"),
    )(chip, sums, from_chips, w, m, v)


def _adamw_small(g, w, m, v):
    def body(g_ref, w_ref, m_ref, v_ref, d_out, m_out, v_out):
        delta, m_new, v_new = _adamw(w_ref[...], g_ref[...], m_ref[...], v_ref[...])
        d_out[...] = delta
        m_out[...] = m_new
        v_out[...] = v_new

    vmem = pl.BlockSpec(memory_space=pltpu.VMEM)
    shp = jax.ShapeDtypeStruct(g.shape, F32)
    return pl.pallas_call(body, name="adamw_small", out_shape=(shp, shp, shp),
                          in_specs=[vmem] * 4, out_specs=(vmem, vmem, vmem))(g, w, m, v)


TILE_ROWS = (0, 1024, NAT_ZA, NAT_B, NAT_ZC, NAT_C, NAT_C + CONV_W)


TILE_ORDER = ((0, 1, 2, 3, 5, 6, 4), (2, 0, 1, 4, 3, 5, 6), (5, 0, 6, 4, 1, 2, 3), (4, 2, 3, 5, 6, 0, 1))
NEIGHBOUR_SWEEP, DIAGONAL_SWEEP = 1, 4
PIECES, W_IN_PIECES, OTHER_PIECES = 4, (0, 1), (2, 3)


def _gather_inproj(x2d, norm_g, shifted, w_out_s, small_s, order, tm):
    seq = x2d.shape[0]
    tn = CONV_W
    ni, nj = seq // tm, MAIN_W // tn
    first_sweep = lambda j, i, order_ref: jnp.where(j == 0, i, ni - 1)
    last_sweep = lambda j, i, order_ref: jnp.where(j == nj - 1, i, 0)
    edge_tiles = _edge_tiles()

    def body(order_ref, x_ref, g_ref, sh_ref, wout_ref, sm_ref, proj_ref, lr_ref, ht_ref, w_nat, wout_all, sm_all,
             w_all, h_all, edges, wout_b, sm_b, send_sems, recv_sems, local_sems):
        j, i = pl.program_id(0), pl.program_id(1)
        rows = pl.ds(pl.multiple_of(i * tm, tm), tm)
        x, y, c = _position()
        me, here, sibling = _blk(x, y, c), (x, y, c), (x, y, 1 - c)
        chips = [(1 - x, y), (x, 1 - y), (1 - x, 1 - y)]

        def pieces(px, py, pc):
            blk = _blk(px, py, pc)
            body_rows = pl.ds(pl.multiple_of(_first_tile_row(blk, px) + EDGE, EDGE), BODY_ROWS)
            return [w_all.at[body_rows], edges.at[blk], wout_all.at[blk], sm_all.at[blk]]

        def copy(a, k, block, to, staged=None):
            ref = pieces(*block)[a]
            return pltpu.make_async_remote_copy(src_ref=ref if staged is None else staged, dst_ref=ref,
                                                send_sem=send_sems.at[a * 7 + k], recv_sem=recv_sems.at[a * 7 + k],
                                                device_id=to, device_id_type=MESH)

        def own_copies(group):
            targets = [(0, sibling)] + [(1 + n, (*chip, c)) for n, chip in enumerate(chips)]
            staged = [None, None, wout_b, sm_b]
            return [copy(a, k, here, to, staged[a]) for k, to in targets for a in group]

        def forwards(n, group):
            return [copy(a, 4 + n, (*chips[n], c), sibling) for a in group]

        def keep_own():
            return [pltpu.make_async_copy(wout_b, wout_all.at[me], local_sems.at[0]),
                    pltpu.make_async_copy(sm_b, sm_all.at[me], local_sems.at[1])]

        def arrive(ns, group):
            for n in ns:
                for a in group:
                    copy(a, 1 + n, (*chips[n], c), here).wait_recv()
                _start_all(forwards(n, group))
            for n in ns:
                for a in group:
                    copy(a, 4 + n, (*chips[n], 1 - c), here).wait_recv()

        def add_edge_tiles(stage):
            for row, parts in edge_tiles.items():
                ready = 0
                for blk, _ in parts:
                    away = (x != blk // 4).astype(jnp.int32) + (y != (blk // 2) % 2).astype(jnp.int32)
                    ready = jnp.maximum(ready, away)

                @pl.when(ready == stage)
                def _(row=row, parts=parts):
                    tile = edges[parts[0][0], parts[0][1]].astype(F32)
                    for blk, side in parts[1:]:
                        tile = tile + edges[blk, side].astype(F32)
                    w_all[row:row + EDGE, :] = tile.astype(BF16)

        @pl.when(jnp.logical_and(j == 0, i == 0))
        def _():
            pieces(*here)[0][...] = sh_ref[EDGE:EDGE + BODY_ROWS, :].astype(BF16)
            edges[me, 0] = sh_ref[0:EDGE, :].astype(BF16)
            edges[me, 1] = sh_ref[EDGE + BODY_ROWS:, :].astype(BF16)
            _start_all(own_copies(W_IN_PIECES))
            wout_b[...] = wout_ref[...].astype(BF16)
            sm_b[...] = sm_ref[...]
            _start_all(own_copies(OTHER_PIECES) + keep_own())
            for a in W_IN_PIECES:
                copy(a, 0, sibling, here).wait_recv()
            add_edge_tiles(0)

        @pl.when(jnp.logical_and(j == NEIGHBOUR_SWEEP, i == 0))
        def _():
            arrive((0, 1), W_IN_PIECES)
            add_edge_tiles(1)

        @pl.when(jnp.logical_and(j == DIAGONAL_SWEEP, i == 0))
        def _():
            arrive((2,), W_IN_PIECES)
            add_edge_tiles(2)

        @pl.when(jnp.logical_and(j == nj - 1, i == 0))
        def _():
            arrive((0, 1, 2), OTHER_PIECES)

        @pl.when(j == 0)
        def _():
            xv = x_ref[...]
            r = lax.rsqrt(jnp.mean(xv * xv, axis=-1, keepdims=True) + EPS)
            h = (xv * r) * g_ref[...]
            h_all[rows, :] = h.astype(BF16)
            ht_ref[...] = h.T.astype(BF16)

        tile = order_ref[j]
        row = 0
        for k, start in enumerate(TILE_ROWS):
            row = row + jnp.where(tile == k, start // 32, 0)
        w_tile = w_all[pl.ds(pl.multiple_of(row * 32, 32), tn), :]
        proj_ref[...] = _dot_nt(h_all[rows, :], w_tile).astype(BF16)

        @pl.when(j == nj - 1)
        def _():
            lr_ref[...] = _dot_nt(h_all[rows, :], w_all[NAT_LR:NAT_LR + LR_W, :])

        @pl.when(jnp.logical_and(j == nj - 1, i == ni - 1))
        def _():
            for cp in own_copies(range(PIECES)) + [cp for n in range(3) for cp in forwards(n, range(PIECES))]:
                cp.wait_send()
            for a in OTHER_PIECES:
                copy(a, 0, sibling, here).wait_recv()
            for cp in keep_own():
                cp.wait()
            keep = pltpu.make_async_copy(w_all, w_nat, local_sems.at[2])
            keep.start()
            keep.wait()

    const = lambda shape: pl.BlockSpec(shape, lambda j, i, order_ref: (0,) * len(shape))
    hbm = pl.BlockSpec(memory_space=pl.ANY)
    vmem = pl.BlockSpec(memory_space=pltpu.VMEM)
    return pl.pallas_call(
        body, name="gather_inproj",
        out_shape=(jax.ShapeDtypeStruct((seq, MAIN_W), BF16), jax.ShapeDtypeStruct((seq, LR_W), F32),
                   jax.ShapeDtypeStruct((D_MODEL, seq), BF16), jax.ShapeDtypeStruct((IN_W, D_MODEL), BF16),
                   jax.ShapeDtypeStruct((N_DEV,) + w_out_s.shape, BF16),
                   jax.ShapeDtypeStruct((N_DEV,) + small_s.shape, F32)),
        grid_spec=pltpu.PrefetchScalarGridSpec(
            num_scalar_prefetch=1, grid=(nj, ni),
            in_specs=[pl.BlockSpec((tm, D_MODEL), lambda j, i, order_ref: (first_sweep(j, i, order_ref), 0)),
                      const((1, D_MODEL)), vmem, const(w_out_s.shape), const(small_s.shape)],
            out_specs=(pl.BlockSpec((tm, tn), lambda j, i, order_ref: (i, order_ref[j])),
                       pl.BlockSpec((tm, LR_W), lambda j, i, order_ref: (last_sweep(j, i, order_ref), 0)),
                       pl.BlockSpec((D_MODEL, tm), lambda j, i, order_ref: (0, first_sweep(j, i, order_ref))),
                       hbm, hbm, hbm),
            scratch_shapes=[pltpu.VMEM((IN_W, D_MODEL), BF16), pltpu.VMEM((seq, D_MODEL), BF16),
                            pltpu.VMEM((N_DEV, 2, EDGE, D_MODEL), BF16),
                            pltpu.VMEM(w_out_s.shape, BF16), pltpu.VMEM(small_s.shape, F32),
                            pltpu.SemaphoreType.DMA((7 * PIECES,)), pltpu.SemaphoreType.DMA((7 * PIECES,)),
                            pltpu.SemaphoreType.DMA((3,))]),
        compiler_params=_cparams("arbitrary", "arbitrary"),
    )(order, x2d, norm_g, shifted, w_out_s, small_s)


def _block_masks(tt):
    row = lax.broadcasted_iota(jnp.int32, (tt, tt), 0)
    col = lax.broadcasted_iota(jnp.int32, (tt, tt), 1)
    same = jnp.right_shift(row, 6) == jnp.right_shift(col, 6)
    return (jnp.logical_and(same, col <= row), jnp.logical_and(same, col >= row), jnp.logical_and(same, col > row))


def _dot_split3(ones_mat, x):
    x1 = x.astype(BF16)
    r1 = x - x1.astype(F32)
    x2 = r1.astype(BF16)
    x3 = (r1 - x2.astype(F32)).astype(BF16)
    return (_dot(ones_mat, x3) + _dot(ones_mat, x2)) + _dot(ones_mat, x1)


def _log_gate(logits):
    return (jnp.minimum(logits, 0.0) - jnp.log(1.0 + jnp.exp(-jnp.abs(logits)))) * GATE_SCALE


def _chunk_column_mask(tt):
    nc = tt // CHUNK
    row = lax.broadcasted_iota(jnp.int32, (tt, nc * DK), 0)
    col = lax.broadcasted_iota(jnp.int32, (tt, nc * DK), 1)
    return jnp.right_shift(row, 6) == jnp.right_shift(col, 7)


def _chunked(mask, x, nc):
    wide = jnp.concatenate([x] * nc, axis=1)
    return jnp.where(mask, wide, jnp.zeros_like(wide))


def _gla_fwd(proj, lr, wgk_f, wgk_b, bgk_f, bgk_b, tt):
    seq = proj.shape[0]
    nb, nc, nch = seq // tt, tt // CHUNK, seq // CHUNK

    def body(qf, kf, vf, lrf, qb, kb, vb, lrb, wf, wb, bf, bb, of, ob, stf, stb, s_scr, qs_s, ks_s, qin_s, kout_s):
        @pl.when(pl.program_id(0) == 0)
        def _():
            s_scr[...] = jnp.zeros(s_scr.shape, F32)

        low, upp, sup = _block_masks(tt)
        dirs = ((qf, kf, vf, lrf, wf, bf, of, stf, low, low, REF_F, LAST_F, list(range(nc))),
                (qb, kb, vb, lrb, wb, bb, ob, stb, upp, sup, REF_B, LAST_B, list(reversed(range(nc)))))
        for d, (q_r, k_r, v_r, lr_r, w_r, b_r, o_r, st_r, cum, mask, ref, last, order) in enumerate(dirs):
            logits = _dot(lr_r[...].astype(BF16), w_r[...]) + b_r[...]
            b = _dot_split3(cum.astype(BF16), _log_gate(logits))
            decs = []
            for c in range(nc):
                rows = slice(c * CHUNK, (c + 1) * CHUNK)
                bc = b[rows]
                b_ref, b_last = bc[ref:ref + 1], bc[last:last + 1]
                qc = q_r[rows, :].astype(F32) * QSCALE
                kc = k_r[rows, :].astype(F32)
                qs_s[rows, :] = (qc * jnp.exp(bc - b_ref)).astype(BF16)
                ks_s[rows, :] = (kc * jnp.exp(b_ref - bc)).astype(BF16)
                qin_s[rows, :] = (qc * jnp.exp(bc)).astype(BF16)
                kout_s[rows, :] = (kc * jnp.exp(b_last - bc)).astype(BF16)
                decs.append(jnp.exp(b_last))
            for h in range(HEADS):
                ksl = slice(h * DK, (h + 1) * DK)
                vsl = slice(h * DV, (h + 1) * DV)
                v = v_r[:, vsl].astype(BF16)
                att = jnp.where(mask, _dot_nt(qs_s[:, ksl], ks_s[:, ksl]), 0.0).astype(BF16)
                o_intra = _dot(att, v)
                st = s_scr[d * HEADS + h]
                for c in order:
                    rows = slice(c * CHUNK, (c + 1) * CHUNK)
                    stb = st.astype(BF16)
                    st_r[c, h] = stb
                    o_r[rows, vsl] = (o_intra[rows] + _dot_nt(qin_s[rows, ksl], stb)).astype(BF16)
                    st = st * decs[c][:, ksl] + _dot_tn(v[rows], kout_s[rows, ksl])
                s_scr[d * HEADS + h] = st

    fw = lambda i: (i, 0)
    bw = lambda i: (nb - 1 - i, 0)
    const = lambda i: (0, 0)

    def tok_specs(m):
        return [pl.BlockSpec((tt, QK_W), lambda i: (m(i)[0], OFF_Q // QK_W)),
                pl.BlockSpec((tt, QK_W), lambda i: (m(i)[0], OFF_K // QK_W)),
                pl.BlockSpec((tt, V_W), lambda i: (m(i)[0], OFF_V // V_W)),
                pl.BlockSpec((tt, LR_W), m)]

    st_shape = jax.ShapeDtypeStruct((nch, HEADS, DV, DK), BF16)
    o_shape = jax.ShapeDtypeStruct((seq, V_W), BF16)
    operand = pltpu.VMEM((tt, QK_W), BF16)
    return pl.pallas_call(
        body, name="gla_fwd",
        out_shape=(o_shape, o_shape, st_shape, st_shape),
        grid=(nb,),
        in_specs=tok_specs(fw) + tok_specs(bw) + [
            pl.BlockSpec((LR_W, QK_W), const), pl.BlockSpec((LR_W, QK_W), const),
            pl.BlockSpec((1, QK_W), const), pl.BlockSpec((1, QK_W), const)],
        out_specs=(pl.BlockSpec((tt, V_W), fw), pl.BlockSpec((tt, V_W), bw),
                   pl.BlockSpec((nc, HEADS, DV, DK), lambda i: (i, 0, 0, 0)),
                   pl.BlockSpec((nc, HEADS, DV, DK), lambda i: (nb - 1 - i, 0, 0, 0))),
        scratch_shapes=[pltpu.VMEM((2 * HEADS, DV, DK), F32), operand, operand, operand, operand],
        compiler_params=_cparams("arbitrary"),
    )(proj, proj, proj, lr, proj, proj, proj, lr, wgk_f, wgk_b, bgk_f, bgk_b)


def _head_norm(o, gain):
    outs, rinv = [], []
    for h in range(HEADS):
        oh = o[:, h * DV:(h + 1) * DV]
        r = lax.rsqrt(jnp.mean(oh * oh, axis=-1, keepdims=True) + EPS)
        outs.append((oh * r) * gain)
        rinv.append(r)
    return jnp.concatenate(outs, axis=1), rinv


def _shift_rows(u, prev_row, next_row):
    n = u.shape[0]
    row = lax.broadcasted_iota(jnp.int32, (n, 1), 0)
    up = jnp.where(row == 0, prev_row, pltpu.roll(u, 1, 0))
    un = jnp.where(row == n - 1, next_row, pltpu.roll(u, n - 1, 0))
    return up, un


HALO = 16


def _halo_specs(tm, seq, col_block):
    per = tm // HALO
    last = seq // HALO - 1
    return [pl.BlockSpec((HALO, CONV_W), lambda i: (jnp.maximum(i * per - 1, 0), col_block)),
            pl.BlockSpec((HALO, CONV_W), lambda i: (jnp.minimum((i + 1) * per, last), col_block))]


def _f32(ref):
    return ref[...].astype(F32)


def _last_row(ref):
    return ref[HALO - 1:HALO, :].astype(F32)


def _first_row(ref):
    return ref[0:1, :].astype(F32)


def _mix_out_loss(o_f, o_b, proj, x2d, tgt, gla_g, conv_w, conv_b, w_out, final_g, tm):
    seq = x2d.shape[0]
    nt = seq // tm

    def body(of, ob, za, bg, cg, hc, zc, cprev, cnext, hprev, hnext, x_ref, t_ref, gg, cw, cb, wo, fg,
             yt_ref, conv_ref, dx2_ref, dx2b_ref, loss_ref, dfg_ref):
        i = pl.program_id(0)

        @pl.when(i == 0)
        def _():
            loss_ref[...] = jnp.zeros(loss_ref.shape, F32)
            dfg_ref[...] = jnp.zeros(dfg_ref.shape, F32)

        on, _ = _head_norm(_f32(of) + _f32(ob), gg[...])
        zav = _f32(za)
        y_a = on * (zav * _sigmoid(zav))
        u = _f32(cg) * _f32(hc)
        prev_row = jnp.where(i > 0, _last_row(cprev) * _last_row(hprev), 0.0)
        next_row = jnp.where(i < nt - 1, _first_row(cnext) * _first_row(hnext), 0.0)
        up, un = _shift_rows(u, prev_row, next_row)
        conv = (cw[0:1, :] * up + cw[1:2, :] * u + cw[2:3, :] * un) + cb[...]
        conv_ref[...] = conv.astype(BF16)
        zcv = _f32(zc)
        y_c = _f32(bg) * conv * (zcv * _sigmoid(zcv))
        y = jnp.concatenate([y_a, y_c], axis=1)
        yt_ref[...] = y.T.astype(BF16)
        x2 = x_ref[...] + _dot(y.astype(BF16), wo[...])
        r = lax.rsqrt(jnp.mean(x2 * x2, axis=-1, keepdims=True) + EPS)
        xn = x2 * r
        err = xn * fg[...] - t_ref[...]
        loss_ref[...] += 0.5 * jnp.sum(jnp.mean(err * err, axis=-1, keepdims=True))
        dyf = err * (1.0 / D_MODEL)
        dfg_ref[...] += jnp.sum(dyf * xn, axis=0, keepdims=True)
        dxn = dyf * fg[...]
        dx2 = r * dxn - xn * (r * jnp.mean(dxn * xn, axis=-1, keepdims=True))
        dx2_ref[...] = dx2
        dx2b_ref[...] = dx2.astype(BF16)

    def col(off):
        return pl.BlockSpec((tm, CONV_W), lambda i: (i, off // CONV_W))

    rowt = pl.BlockSpec((tm, D_MODEL), lambda i: (i, 0))
    const = lambda shape: pl.BlockSpec(shape, lambda i: (0, 0))
    return pl.pallas_call(
        body, name="mix_out_loss",
        out_shape=(jax.ShapeDtypeStruct((MIX_W, seq), BF16), jax.ShapeDtypeStruct((seq, CONV_W), BF16),
                   jax.ShapeDtypeStruct((seq, D_MODEL), F32), jax.ShapeDtypeStruct((seq, D_MODEL), BF16),
                   jax.ShapeDtypeStruct((8, 128), F32), jax.ShapeDtypeStruct((1, D_MODEL), F32)),
        grid=(nt,),
        in_specs=[rowt, rowt, col(OFF_ZA), col(OFF_B), col(OFF_C), col(OFF_H), col(OFF_ZC)]
        + _halo_specs(tm, seq, OFF_C // CONV_W) + _halo_specs(tm, seq, OFF_H // CONV_W)
        + [rowt, rowt, const((1, DV)), const((8, CONV_W)), const((1, CONV_W)), const((MIX_W, D_MODEL)),
           const((1, D_MODEL))],
        out_specs=(pl.BlockSpec((MIX_W, tm), lambda i: (0, i)), rowt, rowt, rowt, const((8, 128)),
                   const((1, D_MODEL))),
        compiler_params=_cparams("arbitrary"),
    )(o_f, o_b, proj, proj, proj, proj, proj, proj, proj, proj, proj, x2d, tgt, gla_g, conv_w, conv_b, w_out, final_g)


def _dsilu(z, s):
    return s * (1.0 + z * (1.0 - s))


def _mix_bwd(dx2b, o_f, o_b, proj, conv, gla_g, w_out, tm):
    seq = dx2b.shape[0]

    def body(dx, of, ob, za, bg, zc, cv, gg, wo, dg_ref, do_ref, dconv_ref, dgg_ref, dcb_ref):
        @pl.when(pl.program_id(0) == 0)
        def _():
            dgg_ref[...] = jnp.zeros(dgg_ref.shape, F32)
            dcb_ref[...] = jnp.zeros(dcb_ref.shape, F32)

        dy = _dot_nt(dx[...], wo[...])
        dy_a, dy_c = dy[:, :V_W], dy[:, V_W:]
        zcv, bgv, convv = _f32(zc), _f32(bg), _f32(cv)
        sc = _sigmoid(zcv)
        szc = zcv * sc
        dg_ref[:, CONV_W:2 * CONV_W] = (dy_c * convv * szc).astype(BF16)
        dconv = dy_c * bgv * szc
        dconv_ref[...] = dconv.astype(BF16)
        dcb_ref[...] += jnp.sum(dconv, axis=0, keepdims=True)
        dg_ref[:, 2 * CONV_W:] = (dy_c * bgv * convv * _dsilu(zcv, sc)).astype(BF16)

        o = _f32(of) + _f32(ob)
        gain = gg[...]
        on, rinv = _head_norm(o, gain)
        zav = _f32(za)
        sa = _sigmoid(zav)
        dg_ref[:, :CONV_W] = (dy_a * on * _dsilu(zav, sa)).astype(BF16)
        don = dy_a * (zav * sa)
        dgg = jnp.zeros((1, DV), F32)
        dos = []
        for h in range(HEADS):
            sl = slice(h * DV, (h + 1) * DV)
            oh, r, dh = o[:, sl], rinv[h], don[:, sl]
            ohn = oh * r
            dgg = dgg + jnp.sum(dh * ohn, axis=0, keepdims=True)
            dn = dh * gain
            dos.append(r * dn - ohn * (r * jnp.mean(dn * ohn, axis=-1, keepdims=True)))
        dgg_ref[...] += dgg
        do_ref[...] = jnp.concatenate(dos, axis=1).astype(BF16)

    def col(off):
        return pl.BlockSpec((tm, CONV_W), lambda i: (i, off // CONV_W))

    rowt = pl.BlockSpec((tm, D_MODEL), lambda i: (i, 0))
    const = lambda shape: pl.BlockSpec(shape, lambda i: (0, 0))
    return pl.pallas_call(
        body, name="mix_bwd",
        out_shape=(jax.ShapeDtypeStruct((seq, GATES_W), BF16), jax.ShapeDtypeStruct((seq, V_W), BF16),
                   jax.ShapeDtypeStruct((seq, CONV_W), BF16),
                   jax.ShapeDtypeStruct((1, DV), F32), jax.ShapeDtypeStruct((1, CONV_W), F32)),
        grid=(seq // tm,),
        in_specs=[rowt, rowt, rowt, col(OFF_ZA), col(OFF_B), col(OFF_ZC), rowt, const((1, DV)),
                  const((MIX_W, D_MODEL))],
        out_specs=(pl.BlockSpec((tm, GATES_W), lambda i: (i, 0)), rowt, rowt, const((1, DV)), const((1, CONV_W))),
        compiler_params=_cparams("arbitrary"),
    )(dx2b, o_f, o_b, proj, proj, proj, conv, gla_g, w_out)


def _conv_bwd(dconv, proj, conv_w, tm):
    seq = dconv.shape[0]
    nt = seq // tm

    def body(dc_in, dprev, dnext, cg, hc, cprev, cnext, hprev, hnext, cw, dch_ref, dcw_ref):
        i = pl.program_id(0)

        @pl.when(i == 0)
        def _():
            dcw_ref[...] = jnp.zeros(dcw_ref.shape, F32)

        first, lastt = i > 0, i < nt - 1
        dcv = _f32(dc_in)
        d_up, d_un = _shift_rows(dcv, jnp.where(first, _last_row(dprev), 0.0), jnp.where(lastt, _first_row(dnext), 0.0))
        cgv, hcv = _f32(cg), _f32(hc)
        u = cgv * hcv
        u_up, u_un = _shift_rows(u, jnp.where(first, _last_row(cprev) * _last_row(hprev), 0.0),
                                 jnp.where(lastt, _first_row(cnext) * _first_row(hnext), 0.0))
        du = cw[0:1, :] * d_un + cw[1:2, :] * dcv + cw[2:3, :] * d_up
        dch_ref[:, :CONV_W] = (du * hcv).astype(BF16)
        dch_ref[:, CONV_W:] = (du * cgv).astype(BF16)
        dcw_ref[0:1, :] += jnp.sum(dcv * u_up, axis=0, keepdims=True)
        dcw_ref[1:2, :] += jnp.sum(dcv * u, axis=0, keepdims=True)
        dcw_ref[2:3, :] += jnp.sum(dcv * u_un, axis=0, keepdims=True)

    def col(off):
        return pl.BlockSpec((tm, CONV_W), lambda i: (i, off // CONV_W))

    rowt = pl.BlockSpec((tm, CONV_W), lambda i: (i, 0))
    const = lambda shape: pl.BlockSpec(shape, lambda i: (0, 0))
    return pl.pallas_call(
        body, name="conv_bwd",
        out_shape=(jax.ShapeDtypeStruct((seq, CH_W), BF16), jax.ShapeDtypeStruct((8, CONV_W), F32)),
        grid=(nt,),
        in_specs=[rowt] + _halo_specs(tm, seq, 0) + [col(OFF_C), col(OFF_H)]
        + _halo_specs(tm, seq, OFF_C // CONV_W) + _halo_specs(tm, seq, OFF_H // CONV_W) + [const((8, CONV_W))],
        out_specs=(pl.BlockSpec((tm, CH_W), lambda i: (i, 0)), const((8, CONV_W))),
        compiler_params=_cparams("arbitrary"),
    )(dconv, dconv, dconv, proj, proj, proj, proj, proj, proj, conv_w)


def _gla_bwd(proj, lr, do, st_f, st_b, wgk_f, wgk_b, bgk_f, bgk_b, tt):
    seq = proj.shape[0]
    nb, nc = seq // tt, tt // CHUNK

    def body(qf, kf, vf, lrf, dof, stf, qb, kb, vb, lrb, dob, stb, wf, wb, bf, bb,
             dqkv_f, dlr_f, dqkv_b, dlr_b, dwf, dwb, dbf, dbb,
             ds_scr, eq_s, ek_s, ein_s, eout_s, qs_s, ks_s, qin_s, kout_s, db_s, lg_s):
        @pl.when(pl.program_id(0) == 0)
        def _():
            ds_scr[...] = jnp.zeros(ds_scr.shape, F32)
            for r in (dwf, dwb, dbf, dbb):
                r[...] = jnp.zeros(r.shape, F32)

        low, upp, sup = _block_masks(tt)
        row = lax.broadcasted_iota(jnp.int32, (CHUNK, 1), 0)
        kmask = _chunk_column_mask(tt)
        dirs = ((qf, kf, vf, lrf, dof, stf, wf, bf, dqkv_f, dlr_f, dwf, dbf,
                 low, upp, low, REF_F, LAST_F, list(reversed(range(nc)))),
                (qb, kb, vb, lrb, dob, stb, wb, bb, dqkv_b, dlr_b, dwb, dbb,
                 upp, low, sup, REF_B, LAST_B, list(range(nc))))
        for d, (q_r, k_r, v_r, lr_r, do_r, st_r, w_r, b_r, dqkv_r, dlr_r, dw_r, db_r,
                cum, cum_t, mask, ref, last, order) in enumerate(dirs):
            lrv = lr_r[...].astype(BF16)
            wv = w_r[...]
            logits = _dot(lrv, wv) + b_r[...]
            lg_s[...] = logits
            b = _dot_split3(cum.astype(BF16), _log_gate(logits))
            decs = []
            for c in range(nc):
                rows = slice(c * CHUNK, (c + 1) * CHUNK)
                bc = b[rows]
                b_ref, b_last = bc[ref:ref + 1], bc[last:last + 1]
                qc = q_r[rows, :].astype(F32) * QSCALE
                kc = k_r[rows, :].astype(F32)
                e_q, e_k, e_in, e_out = jnp.exp(bc - b_ref), jnp.exp(b_ref - bc), jnp.exp(bc), jnp.exp(b_last - bc)
                eq_s[rows, :], ek_s[rows, :], ein_s[rows, :], eout_s[rows, :] = e_q, e_k, e_in, e_out
                qs_s[rows, :] = (qc * e_q).astype(BF16)
                ks_s[rows, :] = (kc * e_k).astype(BF16)
                qin_s[rows, :] = (qc * e_in).astype(BF16)
                kout_s[rows, :] = (kc * e_out).astype(BF16)
                decs.append(jnp.exp(b_last))
            for h in range(HEADS):
                ksl = slice(h * DK, (h + 1) * DK)
                vsl = slice(h * DV, (h + 1) * DV)
                v = v_r[:, vsl].astype(BF16)
                dov = do_r[:, vsl].astype(BF16)
                qsb, ksb = qs_s[:, ksl], ks_s[:, ksl]
                att = jnp.where(mask, _dot_nt(qsb, ksb), 0.0).astype(BF16)
                datt = jnp.where(mask, _dot_nt(dov, v), 0.0).astype(BF16)
                dqs = _dot(datt, ksb)
                dks = _dot_tn(datt, qsb)
                dv_intra = _dot_tn(att, dov)
                g_t = _dot_tn(dov, _chunked(kmask, qin_s[:, ksl], nc))
                ds = ds_scr[d * HEADS + h]
                for c in order:
                    rows = slice(c * CHUNK, (c + 1) * CHUNK)
                    dsb = ds.astype(BF16)
                    s_prev = st_r[c, h]
                    dk_out = _dot(v[rows], dsb)
                    dq_in = _dot(dov[rows], s_prev)
                    dv = dv_intra[rows] + _dot_nt(kout_s[rows, ksl], dsb)
                    dqkv_r[rows, OFF_V + h * DV:OFF_V + (h + 1) * DV] = dv.astype(BF16)
                    dec = decs[c][:, ksl]
                    ddec = jnp.sum(ds * s_prev.astype(F32), axis=0, keepdims=True)
                    e_out = eout_s[rows, ksl]
                    qc = q_r[rows, ksl].astype(F32) * QSCALE
                    kc = k_r[rows, ksl].astype(F32)
                    dq = dqs[rows] * eq_s[rows, ksl] + dq_in * ein_s[rows, ksl]
                    dk = dks[rows] * ek_s[rows, ksl] + dk_out * e_out
                    dqkv_r[rows, OFF_Q + h * DK:OFF_Q + (h + 1) * DK] = (dq * QSCALE).astype(BF16)
                    dqkv_r[rows, OFF_K + h * DK:OFF_K + (h + 1) * DK] = dk.astype(BF16)
                    tail = jnp.sum(dk_out * (kc * e_out), axis=0, keepdims=True) + ddec * dec
                    db_s[rows, ksl] = (qc * dq - kc * dk) + jnp.where(row == last, tail, 0.0)
                    ds = ds * dec + g_t[:, c * DK:(c + 1) * DK]
                ds_scr[d * HEADS + h] = ds
            dg = _dot_split3(cum_t.astype(BF16), db_s[...])
            dlogit = (dg * GATE_SCALE) * _sigmoid(-lg_s[...])
            dlb = dlogit.astype(BF16)
            dlr_r[...] = _dot_nt(dlb, wv)
            dw_r[...] += _dot_tn(lrv, dlb)
            db_r[...] += jnp.sum(dlogit, axis=0, keepdims=True)

    fw = lambda i: (nb - 1 - i, 0)
    bw = lambda i: (i, 0)
    const = lambda i: (0, 0)

    def tok_specs(m):
        return [pl.BlockSpec((tt, QK_W), lambda i: (m(i)[0], OFF_Q // QK_W)),
                pl.BlockSpec((tt, QK_W), lambda i: (m(i)[0], OFF_K // QK_W)),
                pl.BlockSpec((tt, V_W), lambda i: (m(i)[0], OFF_V // V_W)),
                pl.BlockSpec((tt, LR_W), m),
                pl.BlockSpec((tt, V_W), m),
                pl.BlockSpec((nc, HEADS, DV, DK), lambda i: (m(i)[0], 0, 0, 0))]

    dqkv = jax.ShapeDtypeStruct((seq, QK_W + QK_W + V_W), BF16)
    dlr = jax.ShapeDtypeStruct((seq, LR_W), F32)
    dw = jax.ShapeDtypeStruct((LR_W, QK_W), F32)
    dbias = jax.ShapeDtypeStruct((1, QK_W), F32)
    return pl.pallas_call(
        body, name="gla_bwd",
        out_shape=(dqkv, dlr, dqkv, dlr, dw, dw, dbias, dbias),
        grid=(nb,),
        in_specs=tok_specs(fw) + tok_specs(bw) + [
            pl.BlockSpec((LR_W, QK_W), const), pl.BlockSpec((LR_W, QK_W), const),
            pl.BlockSpec((1, QK_W), const), pl.BlockSpec((1, QK_W), const)],
        out_specs=(pl.BlockSpec((tt, QK_W + QK_W + V_W), fw), pl.BlockSpec((tt, LR_W), fw),
                   pl.BlockSpec((tt, QK_W + QK_W + V_W), bw), pl.BlockSpec((tt, LR_W), bw),
                   pl.BlockSpec((LR_W, QK_W), const), pl.BlockSpec((LR_W, QK_W), const),
                   pl.BlockSpec((1, QK_W), const), pl.BlockSpec((1, QK_W), const)),
        scratch_shapes=[pltpu.VMEM((2 * HEADS, DV, DK), F32)] + [pltpu.VMEM((tt, QK_W), F32)] * 4
        + [pltpu.VMEM((tt, QK_W), BF16)] * 4 + [pltpu.VMEM((tt, QK_W), F32)] * 2,
        compiler_params=_cparams("arbitrary"),
    )(proj, proj, proj, lr, do, st_f, proj, proj, proj, lr, do, st_b, wgk_f, wgk_b, bgk_f, bgk_b)


def _sum_directions(dqkv_f, dqkv_b, dlr_f, dlr_b, tm):
    seq = dqkv_f.shape[0]

    def body(a, b, la, lb, dp_out, dlr_out):
        dp_out[...] = (_f32(a) + _f32(b)).astype(BF16)
        dlr_out[...] = (la[...] + lb[...]).astype(BF16)

    rowt = pl.BlockSpec((tm, QKV_W), lambda i: (i, 0))
    lrt = pl.BlockSpec((tm, LR_W), lambda i: (i, 0))
    return pl.pallas_call(
        body, name="sum_directions",
        out_shape=(jax.ShapeDtypeStruct((seq, QKV_W), BF16), jax.ShapeDtypeStruct((seq, LR_W), BF16)),
        grid=(seq // tm,),
        in_specs=[rowt, rowt, lrt, lrt],
        out_specs=(rowt, lrt),
        compiler_params=_cparams("arbitrary"),
    )(dqkv_f, dqkv_b, dlr_f, dlr_b)


def _input_grad(dp_qkv, dp_gates, dp_ch, dlr, w_nat, x2d, norm_g, dx2, sums, tm):
    seq = x2d.shape[0]
    nt, n = seq // tm, len(sums)

    def body(dq, dg, dc, dl, w, x_ref, g_ref, dx2_ref, *rest):
        ins, (gx_ref, dng_ref), outs = rest[:n], rest[n:n + 2], rest[n + 2:2 * n + 2]
        send_sems, recv_sems = rest[2 * n + 2:]
        i = pl.program_id(0)

        @pl.when(i == 0)
        def _():
            for cp in _chip_copies(ins, outs, send_sems, recv_sems):
                cp.start()
            dng_ref[...] = jnp.zeros(dng_ref.shape, F32)

        dh = (_dot(dl[...], w[NAT_LR:NAT_LR + LR_W, :]) + _dot(dq[...], w[0:NAT_ZA, :])
              + _dot(dg[:, 0:CONV_W], w[NAT_ZA:NAT_LR, :]) + _dot(dg[:, CONV_W:2 * CONV_W], w[NAT_B:NAT_C, :])
              + _dot(dg[:, 2 * CONV_W:], w[NAT_ZC:IN_W, :]) + _dot(dc[...], w[NAT_C:NAT_ZC, :]))
        xv = x_ref[...]
        r = lax.rsqrt(jnp.mean(xv * xv, axis=-1, keepdims=True) + EPS)
        xn = xv * r
        dng_ref[...] += jnp.sum(dh * xn, axis=0, keepdims=True)
        dn = dh * g_ref[...]
        gx_ref[...] = (r * dn - xn * (r * jnp.mean(dn * xn, axis=-1, keepdims=True))) + dx2_ref[...]

        @pl.when(i == nt - 1)
        def _():
            copies = _chip_copies(ins, outs, send_sems, recv_sems)
            for cp in copies:
                cp.wait_recv()
            for cp in copies:
                cp.wait_send()

    rowt = pl.BlockSpec((tm, D_MODEL), lambda i: (i, 0))
    seg = lambda width: pl.BlockSpec((tm, width), lambda i: (i, 0))
    resident = lambda rows: pl.BlockSpec((rows, D_MODEL), lambda i: (0, 0), pipeline_mode=pl.Buffered(1))
    hbm = pl.BlockSpec(memory_space=pl.ANY)
    return pl.pallas_call(
        body, name="input_grad",
        out_shape=(jax.ShapeDtypeStruct((seq, D_MODEL), F32), jax.ShapeDtypeStruct((1, D_MODEL), F32))
        + tuple(jax.ShapeDtypeStruct((3,) + s.shape[1:], s.dtype) for s in sums),
        grid=(nt,),
        in_specs=[seg(QKV_W), seg(GATES_W), seg(CH_W), seg(LR_W), resident(IN_W),
                  rowt, pl.BlockSpec((1, D_MODEL), lambda i: (0, 0)), rowt] + [hbm] * n,
        out_specs=(rowt, pl.BlockSpec((1, D_MODEL), lambda i: (0, 0))) + (hbm,) * n,
        scratch_shapes=[pltpu.SemaphoreType.DMA((3 * n,)), pltpu.SemaphoreType.DMA((3 * n,))],
        compiler_params=_cparams("arbitrary"),
    )(dp_qkv, dp_gates, dp_ch, dlr, w_nat, x2d, norm_g, dx2, *sums)


def _weight_grad_out(y_t, dx2b, tk, riding):
    m, seq = y_t.shape
    n = dx2b.shape[1]
    nk = seq // tk

    def body(a_ref, b_ref, ride_in, o_ref, ride_out, send_sems, recv_sems):
        k = pl.program_id(0)

        @pl.when(k == 0)
        def _():
            _start_all(_sibling_copies(ride_in, ride_out, send_sems, recv_sems))
            o_ref[...] = jnp.zeros(o_ref.shape, F32)

        o_ref[...] += _dot(a_ref[...], b_ref[...])

        @pl.when(k == nk - 1)
        def _():
            _wait_all(_sibling_copies(ride_in, ride_out, send_sems, recv_sems))

    hbm = pl.BlockSpec(memory_space=pl.ANY)
    return pl.pallas_call(
        body, name="wgrad_out",
        out_shape=(jax.ShapeDtypeStruct((m, n), F32), jax.ShapeDtypeStruct((4,) + _block_shape(riding), F32)),
        grid=(nk,),
        in_specs=[pl.BlockSpec((m, tk), lambda k: (0, k)), pl.BlockSpec((tk, n), lambda k: (k, 0)), hbm],
        out_specs=(pl.BlockSpec((m, n), lambda k: (0, 0)), hbm),
        scratch_shapes=[pltpu.SemaphoreType.DMA((4,)), pltpu.SemaphoreType.DMA((4,))],
        compiler_params=_cparams("arbitrary"),
    )(y_t, dx2b, riding)


def _weight_grad_in(h_t, dp_qkv, dp_gates, dp_ch, dlr):
    m, seq = h_t.shape
    tn = 512
    n_qkv, n_gates, n_ch = QKV_W // tn, GATES_W // tn, CH_W // tn
    starts = ([k * tn for k in range(n_qkv)] + [NAT_ZA, NAT_ZA + tn, NAT_B, NAT_B + tn, NAT_ZC, NAT_ZC + tn]
              + [NAT_C + k * tn for k in range(n_ch)])

    def out_row(j):
        row = 0
        for k, start in enumerate(starts):
            row = row + jnp.where(j == k, start // 32, 0)
        return pl.multiple_of(row * 32, 32), 0

    def body(a_ref, bq, bg, bc, o_ref, acc):
        j = pl.program_id(0)

        @pl.when(j < n_qkv)
        def _():
            acc[...] = _dot(a_ref[...], bq[...])

        @pl.when(jnp.logical_and(j >= n_qkv, j < n_qkv + n_gates))
        def _():
            acc[...] = _dot(a_ref[...], bg[...])

        @pl.when(j >= n_qkv + n_gates)
        def _():
            acc[...] = _dot(a_ref[...], bc[...])

        o_ref[...] = acc[...].T

    resident = pl.BlockSpec((m, seq), lambda j: (0, 0), pipeline_mode=pl.Buffered(1))
    seg = lambda first, count: pl.BlockSpec((seq, tn), lambda j: (0, jnp.clip(j - first, 0, count - 1)))
    main = pl.pallas_call(
        body, name="wgrad_in",
        out_shape=jax.ShapeDtypeStruct((IN_W, m), F32),
        grid=(n_qkv + n_gates + n_ch,),
        in_specs=[resident, seg(0, n_qkv), seg(n_qkv, n_gates), seg(n_qkv + n_gates, n_ch)],
        out_specs=pl.BlockSpec((pl.Element(tn), pl.Element(m)), out_row),
        scratch_shapes=[pltpu.VMEM((m, tn), F32)],
        compiler_params=_cparams("arbitrary"),
    )(h_t, dp_qkv, dp_gates, dp_ch)

    def lr_body(a_ref, b_ref, full_ref, o_ref, acc):
        acc[...] = _dot(a_ref[...], b_ref[...])
        o_ref[...] = acc[...].T[0:2 * RANK, :]

    whole = lambda shape: pl.BlockSpec(shape, lambda j: (0, 0))
    return pl.pallas_call(
        lr_body, name="wgrad_lr",
        out_shape=jax.ShapeDtypeStruct((IN_W, m), F32),
        grid=(1,),
        in_specs=[whole((m, seq)), whole((seq, LR_W)), pl.BlockSpec(memory_space=pl.ANY)],
        out_specs=pl.BlockSpec((pl.Element(2 * RANK), pl.Element(m)), lambda j: (NAT_LR, 0)),
        scratch_shapes=[pltpu.VMEM((m, LR_W), F32)],
        input_output_aliases={2: 0},
        compiler_params=_cparams("arbitrary"),
    )(h_t, dlr, main)


def _pad_rows(a, rows):
    return jnp.pad(a, ((0, rows - a.shape[0]), (0, 0)))


def _rows128(a):
    a = a.reshape(-1, 128)
    return _pad_rows(a, -(-a.shape[0] // 8) * 8)


def _pack(arrs):
    return jnp.concatenate([_rows128(a) for a in arrs], axis=0)


def _unpack(buf, like):
    out, start = [], 0
    for a in like:
        rows = a.size // 128
        out.append(buf[start:start + rows].reshape(a.shape))
        start += -(-rows // 8) * 8
    return out


def kernel(x, norm_g, w_in, w_gk_f, b_gk_f, w_gk_b, b_gk_b, gla_norm_g, conv_w, conv_b, w_out, final_g, loss_target, m_norm_g, m_w_in, m_w_gk_f, m_b_gk_f, m_w_gk_b, m_b_gk_b, m_gla_norm_g, m_conv_w, m_conv_b, m_w_out, m_final_g, v_norm_g, v_w_in, v_w_gk_f, v_b_gk_f, v_w_gk_b, v_b_gk_b, v_gla_norm_g, v_conv_w, v_conv_b, v_w_out, v_final_g):
    px, py, pc = _position()
    me = _blk(px, py, pc)
    seq = x.shape[1]
    x2d, tgt = x[0], loss_target[0]
    tm = min(512, seq)
    tt = min(256, seq)

    small_s = jnp.concatenate([jnp.concatenate([w_gk_f[0], w_gk_b[0]], axis=1), _pad_rows(conv_w[0], 8)], axis=0)
    shifted = lax.dynamic_update_slice(jnp.zeros((SHIFTED_ROWS, D_MODEL), F32), w_in[0].T, (4 * (me % 4), 0))
    order = sum(jnp.where(2 * px + py == k, jnp.asarray(tiles + (0,), jnp.int32), 0) for k, tiles in enumerate(TILE_ORDER))
    proj, lr, h_t, w_nat, wout_all, small_all = _gather_inproj(x2d, norm_g, shifted, w_out[0], small_s, order,
                                                               min(1024, seq))
    w_out_full = wout_all.reshape(MIX_W, D_MODEL)
    wgk_cols = 512 // N_DEV
    wgk_f_full = small_all[:, 0:RANK, 0:wgk_cols].transpose(1, 0, 2).reshape(RANK, QK_W)
    wgk_b_full = small_all[:, 0:RANK, wgk_cols:2 * wgk_cols].transpose(1, 0, 2).reshape(RANK, QK_W)
    conv_w_full = _pad_rows(small_all[:, RANK:RANK + 3, :].transpose(1, 0, 2).reshape(3, CONV_W), 8)
    zr = lambda n: jnp.zeros((n, QK_W), F32)
    wgk_f_pad = jnp.concatenate([wgk_f_full, zr(LR_W - RANK)], axis=0).astype(BF16)
    wgk_b_pad = jnp.concatenate([zr(RANK), wgk_b_full, zr(LR_W - 2 * RANK)], axis=0).astype(BF16)

    o_f, o_b, st_f, st_b = _gla_fwd(proj, lr, wgk_f_pad, wgk_b_pad, b_gk_f, b_gk_b, tt)
    tmix = min(256, seq)
    y_t, conv, dx2, dx2b, loss_p, dfg_p = _mix_out_loss(o_f, o_b, proj, x2d, tgt, gla_norm_g, conv_w_full, conv_b,
                                                        w_out_full, final_g.reshape(1, D_MODEL), tmix)

    dp_gates, do, dconv, dgg_p, dcb_p = _mix_bwd(dx2b, o_f, o_b, proj, conv, gla_norm_g, w_out_full, tmix)
    dp_ch, dcw_p = _conv_bwd(dconv, proj, conv_w_full, tmix)
    dqkv_f, dlr_f, dqkv_b, dlr_b, dwf_p, dwb_p, dbf_p, dbb_p = _gla_bwd(
        proj, lr, do, st_f, st_b, wgk_f_pad, wgk_b_pad, b_gk_f, b_gk_b, tt)
    dp_qkv, dlr = _sum_directions(dqkv_f, dqkv_b, dlr_f, dlr_b, tm)
    dw_nat = _weight_grad_in(h_t, dp_qkv, dp_gates, dp_ch, dlr)

    dw_out, sib_in = _weight_grad_out(y_t, dx2b, tm, dw_nat)
    part_out = dw_out.reshape(N_DEV, MIX_W // N_DEV, D_MODEL)
    core = jnp.reshape(pc, (1,)).astype(jnp.int32)
    chip = jnp.reshape(2 * px + py, (1,)).astype(jnp.int32)
    sums_in, sib_out = _chip_sums(dw_nat, sib_in, core, 256, "chip_sums_in", riding=part_out)
    sums_out = _chip_sums(part_out, sib_out, core, 256, "chip_sums_out")
    grad_x2d, dng_p, far_in, far_out = _input_grad(dp_qkv, dp_gates, dp_ch, dlr, w_nat, x2d, norm_g, dx2,
                                                   [sums_in, sums_out], tmix)
    pieces = [dng_p, dbf_p, dbb_p, dgg_p, dcb_p, dfg_p[0], dwf_p[0:RANK], dwb_p[RANK:2 * RANK], dcw_p[0:3], loss_p[0]]
    g_window, small_tot = _final_sum(sums_in, far_in, chip, _pack(pieces), 256, "final_sum_in")
    g_in_t = lax.dynamic_slice_in_dim(g_window, 4 * pc, SHARD_W, axis=0)
    g_w_out, d_w_out, nm_w_out, nv_w_out = _final_sum_adamw(sums_out, far_out, chip, w_out[0], m_w_out[0], v_w_out[0],
                                                            256, "adamw_out")
    flat = lambda a: a[0].T.reshape(SHARD_W, D_MODEL // 128, 128)
    unflat = lambda a: a.reshape(SHARD_W, D_MODEL).T
    d_flat, m_flat, v_flat = _adamw_rows(g_in_t.reshape(SHARD_W, D_MODEL // 128, 128), flat(w_in), flat(m_w_in),
                                         flat(v_w_in), 90, "adamw_in")
    g_w_in, d_w_in, nm_w_in, nv_w_in = g_in_t.T, unflat(d_flat), unflat(m_flat), unflat(v_flat)

    tot = _unpack(small_tot, pieces)
    g_norm_g, g_b_gk_f, g_b_gk_b, g_gla, g_conv_b, g_final = tot[:6]
    g_wgk_f = lax.dynamic_slice_in_dim(tot[6], me * wgk_cols, wgk_cols, axis=1)[None]
    g_wgk_b = lax.dynamic_slice_in_dim(tot[7], me * wgk_cols, wgk_cols, axis=1)[None]
    g_conv_w = lax.dynamic_slice_in_dim(tot[8], me * 128, 128, axis=1)[None]
    loss = tot[9][0]

    small_g = [g_norm_g, g_b_gk_f, g_b_gk_b, g_gla, g_conv_b, g_final, g_wgk_f, g_wgk_b, g_conv_w]
    small_w = [norm_g, b_gk_f, b_gk_b, gla_norm_g, conv_b, final_g, w_gk_f, w_gk_b, conv_w]
    small_m = [m_norm_g, m_b_gk_f, m_b_gk_b, m_gla_norm_g, m_conv_b, m_final_g, m_w_gk_f, m_w_gk_b, m_conv_w]
    small_v = [v_norm_g, v_b_gk_f, v_b_gk_b, v_gla_norm_g, v_conv_b, v_final_g, v_w_gk_f, v_w_gk_b, v_conv_w]
    d_s, m_s, v_s = _adamw_small(_pack(small_g), _pack(small_w), _pack(small_m), _pack(small_v))
    d_l, m_l, v_l = _unpack(d_s, small_w), _unpack(m_s, small_w), _unpack(v_s, small_w)

    def ordered(sm, big_in, big_out):
        return [sm[0], big_in[None], sm[6], sm[1], sm[7], sm[2], sm[3], sm[8], sm[4], big_out[None], sm[5]]

    grads = ordered(small_g, g_w_in, g_w_out)
    deltas = ordered(d_l, d_w_in, d_w_out)
    new_m = ordered(m_l, nm_w_in, nm_w_out)
    new_v = ordered(v_l, nv_w_in, nv_w_out)
    return (loss, grad_x2d[None], *grads, *deltas, *new_m, *new_v)
```

```python
import jax
import jax.numpy as jnp
from jax import lax
from jax.experimental import pallas as pl
from jax.experimental.pallas import tpu as pltpu

F32 = jnp.float32
BF16 = jnp.bfloat16
MESH = pl.DeviceIdType.MESH

N_DEV = 8
D_MODEL = 1024
HEADS = 4
DK = 128
DV = 256
QK_W = HEADS * DK
V_W = HEADS * DV
CONV_W = 1024
MIX_W = V_W + CONV_W
CHUNK = 64
RANK = 16
IN_W = 7200
SHARD_W = IN_W // N_DEV
MAIN_W = 7168
LR_W = 128
OFF_Q, OFF_K, OFF_V, OFF_ZA, OFF_B, OFF_ZC, OFF_C, OFF_H = 0, 512, 1024, 2048, 3072, 4096, 5120, 6144
QKV_W, GATES_W, CH_W = 2048, 3072, 2048
NAT_ZA, NAT_LR, NAT_B, NAT_C, NAT_ZC = 2048, 3072, 3104, 4128, 6176
EPS = 1e-6
GATE_SCALE = 1.0 / 16.0
QSCALE = DK ** -0.5
REF_F, LAST_F = CHUNK // 2, CHUNK - 1
REF_B, LAST_B = CHUNK - 1 - CHUNK // 2, 0

ADAM_LR = 0.001
ADAM_B1 = 0.9
ADAM_B2 = 0.999
ADAM_EPS = 1e-08
ADAM_WD = 0.01
ADAM_STEP = 10

VMEM_LIMIT = 56 * 1024 * 1024


def _cparams(*sem):
    return pltpu.CompilerParams(dimension_semantics=sem, vmem_limit_bytes=VMEM_LIMIT)


def _dot(a, b):
    return jnp.dot(a, b, preferred_element_type=F32)


def _dot_nt(a, b):
    return lax.dot_general(a, b, (((1,), (1,)), ((), ())), preferred_element_type=F32)


def _dot_tn(a, b):
    return lax.dot_general(a, b, (((0,), (0,)), ((), ())), preferred_element_type=F32)


def _sigmoid(z):
    return jax.nn.sigmoid(z)


def _position():
    return lax.axis_index("x"), lax.axis_index("y"), lax.axis_index("c")


def _blk(px, py, pc):
    return 4 * px + 2 * py + pc


EDGE = 16
SHIFTED_ROWS = 912
BODY_ROWS = SHIFTED_ROWS - 2 * EDGE


def _first_tile_row(blk, px):
    return EDGE * (56 * blk + px)


def _edge_tiles():
    tiles = {}
    for blk in range(N_DEV):
        first = _first_tile_row(blk, blk // 4)
        tiles.setdefault(first, []).append((blk, 0))
        tiles.setdefault(first + EDGE + BODY_ROWS, []).append((blk, 1))
    return tiles


def _peer_copies(srcs, outs, send_sems, recv_sems):
    x, y, c = _position()
    me = _blk(x, y, c)
    copies = []
    for a, (src, out) in enumerate(zip(srcs, outs)):
        k = 0
        for dx in (0, 1):
            for dy in (0, 1):
                for dc in (0, 1):
                    if dx + dy + dc == 0:
                        continue
                    peer = (1 - x if dx else x, 1 - y if dy else y, 1 - c if dc else c)
                    copies.append(pltpu.make_async_remote_copy(
                        src_ref=src, dst_ref=out.at[me], send_sem=send_sems.at[a * 7 + k],
                        recv_sem=recv_sems.at[a * 7 + k], device_id=peer, device_id_type=MESH))
                    k += 1
    return copies


def _chip_copies(ins, outs, send_sems, recv_sems):
    x, y, c = _position()
    chips = [(1 - x, y), (x, 1 - y), (1 - x, 1 - y)]
    copies = []
    for a in range(len(ins)):
        for j, (px, py) in enumerate(chips):
            copies.append(pltpu.make_async_remote_copy(
                src_ref=ins[a].at[2 * px + py], dst_ref=outs[a].at[j],
                send_sem=send_sems.at[a * 3 + j], recv_sem=recv_sems.at[a * 3 + j],
                device_id=(px, py, c), device_id_type=MESH))
    return copies


WINDOW_ROWS = SHARD_W + 4


def _window_start(k, parity):
    return 2 * SHARD_W * k + (SHARD_W - 4) * parity


def _owner_block(part, k, parity):
    if part.ndim == 3:
        return part.at[2 * k + parity]
    return part.at[pl.ds(pl.multiple_of(_window_start(k, parity), 8), WINDOW_ROWS)]


def _block_shape(part):
    return part.shape[1:] if part.ndim == 3 else (WINDOW_ROWS, part.shape[1])


def _sibling_copies(part, out, send_sems, recv_sems):
    x, y, c = _position()
    return [pltpu.make_async_remote_copy(src_ref=_owner_block(part, k, 1 - c), dst_ref=out.at[k],
                                         send_sem=send_sems.at[k], recv_sem=recv_sems.at[k],
                                         device_id=(x, y, 1 - c), device_id_type=MESH)
            for k in range(4)]


def _start_all(copies):
    for cp in copies:
        cp.start()


def _wait_all(copies):
    for cp in copies:
        cp.wait_recv()
    for cp in copies:
        cp.wait_send()


def _chip_sums(part, from_sibling, core, tc, name, riding=None):
    rows, cols = _block_shape(part)
    nj = cols // tc

    def body(core_ref, p_ref, s_ref, *rest):
        if riding is None:
            (o_ref,) = rest
        else:
            ride_in, o_ref, ride_out, send_sems, recv_sems = rest
            k, j = pl.program_id(0), pl.program_id(1)

            @pl.when(jnp.logical_and(k == 0, j == 0))
            def _():
                _start_all(_sibling_copies(ride_in, ride_out, send_sems, recv_sems))

        o_ref[0] = (p_ref[...].reshape(rows, tc) + s_ref[0]).astype(BF16)

        if riding is not None:
            @pl.when(jnp.logical_and(k == 3, j == nj - 1))
            def _():
                _wait_all(_sibling_copies(ride_in, ride_out, send_sems, recv_sems))

    hbm = pl.BlockSpec(memory_space=pl.ANY)
    sums = jax.ShapeDtypeStruct((4, rows, cols), BF16)
    tile_out = pl.BlockSpec((1, rows, tc), lambda k, j, core_ref: (k, 0, j))
    if part.ndim == 3:
        mine = pl.BlockSpec((1, rows, tc), lambda k, j, core_ref: (2 * k + core_ref[0], 0, j))
    else:
        mine = pl.BlockSpec((pl.Element(rows), pl.Element(tc)),
                            lambda k, j, core_ref: (pl.multiple_of(_window_start(k, core_ref[0]), 8),
                                                    pl.multiple_of(j * tc, 128)))
    in_specs = [mine, pl.BlockSpec((1, rows, tc), lambda k, j, core_ref: (k, 0, j))]
    if riding is None:
        out_shape, out_specs, scratch, args = sums, tile_out, [], (core, part, from_sibling)
    else:
        out_shape = (sums, jax.ShapeDtypeStruct((4,) + _block_shape(riding), F32))
        out_specs, in_specs = (tile_out, hbm), in_specs + [hbm]
        scratch = [pltpu.SemaphoreType.DMA((4,)), pltpu.SemaphoreType.DMA((4,))]
        args = (core, part, from_sibling, riding)
    return pl.pallas_call(
        body, name=name, out_shape=out_shape,
        grid_spec=pltpu.PrefetchScalarGridSpec(num_scalar_prefetch=1, grid=(4, nj), in_specs=in_specs,
                                               out_specs=out_specs, scratch_shapes=scratch),
        compiler_params=_cparams("arbitrary", "arbitrary"),
    )(*args)


def _sum_chips(s_ref, r_ref):
    f = lambda a: a.astype(F32)
    return ((f(s_ref[0]) + f(r_ref[0])) + f(r_ref[1])) + f(r_ref[2])


def _final_sum(sums, from_chips, chip, small, tc, name):
    _, rows, cols = sums.shape
    nj = cols // tc

    def body(chip_ref, s_ref, r_ref, sm_ref, g_out, tot_ref, all_ref, send_sems, recv_sems):
        j = pl.program_id(0)
        me = _blk(*_position())

        @pl.when(j == 0)
        def _():
            all_ref[me] = sm_ref[...]
            _start_all(_peer_copies((all_ref.at[me],), (all_ref,), send_sems, recv_sems))

        g_out[...] = _sum_chips(s_ref, r_ref)

        @pl.when(j == nj - 1)
        def _():
            _wait_all(_peer_copies((all_ref.at[me],), (all_ref,), send_sems, recv_sems))
            acc = all_ref[0]
            for d in range(1, N_DEV):
                acc = acc + all_ref[d]
            tot_ref[...] = acc

    whole = pl.BlockSpec(small.shape, lambda j, chip_ref: (0, 0))
    return pl.pallas_call(
        body, name=name,
        out_shape=(jax.ShapeDtypeStruct((rows, cols), F32), jax.ShapeDtypeStruct(small.shape, F32)),
        grid_spec=pltpu.PrefetchScalarGridSpec(
            num_scalar_prefetch=1, grid=(nj,),
            in_specs=[pl.BlockSpec((1, rows, tc), lambda j, chip_ref: (chip_ref[0], 0, j)),
                      pl.BlockSpec((3, rows, tc), lambda j, chip_ref: (0, 0, j)), whole],
            out_specs=(pl.BlockSpec((rows, tc), lambda j, chip_ref: (0, j)), whole),
            scratch_shapes=[pltpu.VMEM((N_DEV,) + small.shape, F32), pltpu.SemaphoreType.DMA((7,)),
                            pltpu.SemaphoreType.DMA((7,))]),
        compiler_params=_cparams("arbitrary"),
    )(chip, sums, from_chips, small)


def _adamw_rows(g, w, m, v, tr, name):
    rows = g.shape[0]

    def body(g_ref, w_ref, m_ref, v_ref, d_out, m_out, v_out):
        delta, m_new, v_new = _adamw(w_ref[...], g_ref[...], m_ref[...], v_ref[...])
        d_out[...] = delta
        m_out[...] = m_new
        v_out[...] = v_new

    tile = pl.BlockSpec((tr,) + g.shape[1:], lambda r: (r, 0, 0))
    shp = jax.ShapeDtypeStruct(g.shape, F32)
    return pl.pallas_call(
        body, name=name, out_shape=(shp, shp, shp), grid=(rows // tr,),
        in_specs=[tile] * 4, out_specs=(tile, tile, tile),
        compiler_params=_cparams("arbitrary"),
    )(g, w, m, v)


def _adamw(w, g, m, v):
    m = ADAM_B1 * m + (1.0 - ADAM_B1) * g
    v = ADAM_B2 * v + (1.0 - ADAM_B2) * (g * g)
    m_hat = m / (1.0 - ADAM_B1 ** ADAM_STEP)
    v_hat = v / (1.0 - ADAM_B2 ** ADAM_STEP)
    delta = -ADAM_LR * (m_hat / (jnp.sqrt(v_hat) + ADAM_EPS) + ADAM_WD * w)
    return delta, m, v


def _final_sum_adamw(sums, from_chips, chip, w, m, v, tr, name):
    rows, cols = w.shape

    def body(chip_ref, s_ref, r_ref, w_ref, m_ref, v_ref, g_out, d_out, m_out, v_out):
        g = _sum_chips(s_ref, r_ref)
        delta, m_new, v_new = _adamw(w_ref[...], g, m_ref[...], v_ref[...])
        g_out[...] = g
        d_out[...] = delta
        m_out[...] = m_new
        v_out[...] = v_new

    tile = pl.BlockSpec((tr, cols), lambda r, chip_ref: (r, 0))
    shp = jax.ShapeDtypeStruct((rows, cols), F32)
    return pl.pallas_call(
        body, name=name,
        out_shape=(shp, shp, shp, shp),
        grid_spec=pltpu.PrefetchScalarGridSpec(
            num_scalar_prefetch=1, grid=(rows // tr,),
            in_specs=[pl.BlockSpec((1, tr, cols), lambda r, chip_ref: (chip_ref[0], r, 0)),
                      pl.BlockSpec((3, tr, cols), lambda r, chip_ref: (0, r, 0)),
                      tile, tile, tile],
            out_specs=(tile, tile, tile, tile)),
        compiler_params=_cparams("arbitrary"),
    )(chip, sums, from_chips, w, m, v)


def _adamw_small(g, w, m, v):
    def body(g_ref, w_ref, m_ref, v_ref, d_out, m_out, v_out):
        delta, m_new, v_new = _adamw(w_ref[...], g_ref[...], m_ref[...], v_ref[...])
        d_out[...] = delta
        m_out[...] = m_new
        v_out[...] = v_new

    vmem = pl.BlockSpec(memory_space=pltpu.VMEM)
    shp = jax.ShapeDtypeStruct(g.shape, F32)
    return pl.pallas_call(body, name="adamw_small", out_shape=(shp, shp, shp),
                          in_specs=[vmem] * 4, out_specs=(vmem, vmem, vmem))(g, w, m, v)


TILE_ROWS = (0, 1024, NAT_ZA, NAT_B, NAT_ZC, NAT_C, NAT_C + CONV_W)


TILE_ORDER = ((0, 1, 2, 3, 5, 6, 4), (2, 0, 1, 4, 3, 5, 6), (5, 0, 6, 4, 1, 2, 3), (4, 2, 3, 5, 6, 0, 1))
NEIGHBOUR_SWEEP, DIAGONAL_SWEEP = 1, 4
PIECES, W_IN_PIECES, OTHER_PIECES = 4, (0, 1), (2, 3)


def _gather_inproj(x2d, norm_g, shifted, w_out_s, small_s, order, tm):
    seq = x2d.shape[0]
    tn = CONV_W
    ni, nj = seq // tm, MAIN_W // tn
    first_sweep = lambda j, i, order_ref: jnp.where(j == 0, i, ni - 1)
    last_sweep = lambda j, i, order_ref: jnp.where(j == nj - 1, i, 0)
    edge_tiles = _edge_tiles()

    def body(order_ref, x_ref, g_ref, sh_ref, wout_ref, sm_ref, proj_ref, lr_ref, ht_ref, w_nat, wout_all, sm_all,
             w_all, h_all, edges, wout_b, sm_b, send_sems, recv_sems, local_sems):
        j, i = pl.program_id(0), pl.program_id(1)
        rows = pl.ds(pl.multiple_of(i * tm, tm), tm)
        x, y, c = _position()
        me, here, sibling = _blk(x, y, c), (x, y, c), (x, y, 1 - c)
        along_x = c == 0
        chips = [(jnp.where(along_x, 1 - x, x), jnp.where(along_x, y, 1 - y)),
                 (jnp.where(along_x, x, 1 - x), jnp.where(along_x, 1 - y, y)), (1 - x, 1 - y)]
        sibling_chips = [chips[1], chips[0], chips[2]]

        def pieces(px, py, pc):
            blk = _blk(px, py, pc)
            body_rows = pl.ds(pl.multiple_of(_first_tile_row(blk, px) + EDGE, EDGE), BODY_ROWS)
            return [w_all.at[body_rows], edges.at[blk], wout_all.at[blk], sm_all.at[blk]]

        def copy(a, k, block, to, staged=None):
            ref = pieces(*block)[a]
            return pltpu.make_async_remote_copy(src_ref=ref if staged is None else staged, dst_ref=ref,
                                                send_sem=send_sems.at[a * 7 + k], recv_sem=recv_sems.at[a * 7 + k],
                                                device_id=to, device_id_type=MESH)

        def own_copies(group):
            targets = [(0, sibling)] + [(1 + n, (*chips[n], c)) for n in range(2)]
            staged = [None, None, wout_b, sm_b]
            return [copy(a, k, here, to, staged[a]) for k, to in targets for a in group]

        def relays(group):
            return [copy(a, 3, (*chips[0], c), (*chips[1], c)) for a in group]

        def forwards(n, group):
            return [copy(a, 4 + n, (*chips[n], c), sibling) for a in group]

        def keep_own():
            return [pltpu.make_async_copy(wout_b, wout_all.at[me], local_sems.at[0]),
                    pltpu.make_async_copy(sm_b, sm_all.at[me], local_sems.at[1])]

        def arrive(ns, group):
            for n in ns:
                for a in group:
                    copy(a, 1 + n, (*chips[n], c), here).wait_recv()
                _start_all((relays(group) if n == 0 else []) + forwards(n, group))
            for n in ns:
                for a in group:
                    copy(a, 4 + n, (*sibling_chips[n], 1 - c), here).wait_recv()

        def add_edge_tiles(stage):
            for row, parts in edge_tiles.items():
                ready = 0
                for blk, _ in parts:
                    away = (x != blk // 4).astype(jnp.int32) + (y != (blk // 2) % 2).astype(jnp.int32)
                    ready = jnp.maximum(ready, away)

                @pl.when(ready == stage)
                def _(row=row, parts=parts):
                    tile = edges[parts[0][0], parts[0][1]].astype(F32)
                    for blk, side in parts[1:]:
                        tile = tile + edges[blk, side].astype(F32)
                    w_all[row:row + EDGE, :] = tile.astype(BF16)

        @pl.when(jnp.logical_and(j == 0, i == 0))
        def _():
            pieces(*here)[0][...] = sh_ref[EDGE:EDGE + BODY_ROWS, :].astype(BF16)
            edges[me, 0] = sh_ref[0:EDGE, :].astype(BF16)
            edges[me, 1] = sh_ref[EDGE + BODY_ROWS:, :].astype(BF16)
            _start_all(own_copies(W_IN_PIECES))
            wout_b[...] = wout_ref[...].astype(BF16)
            sm_b[...] = sm_ref[...]
            _start_all(own_copies(OTHER_PIECES) + keep_own())
            for a in W_IN_PIECES:
                copy(a, 0, sibling, here).wait_recv()
            add_edge_tiles(0)

        @pl.when(jnp.logical_and(j == NEIGHBOUR_SWEEP, i == 0))
        def _():
            arrive((0, 1), W_IN_PIECES)
            add_edge_tiles(1)

        @pl.when(jnp.logical_and(j == DIAGONAL_SWEEP, i == 0))
        def _():
            arrive((2,), W_IN_PIECES)
            add_edge_tiles(2)
            arrive((0, 1), OTHER_PIECES)

        @pl.when(jnp.logical_and(j == nj - 1, i == 0))
        def _():
            arrive((2,), OTHER_PIECES)

        @pl.when(j == 0)
        def _():
            xv = x_ref[...]
            r = lax.rsqrt(jnp.mean(xv * xv, axis=-1, keepdims=True) + EPS)
            h = (xv * r) * g_ref[...]
            h_all[rows, :] = h.astype(BF16)
            ht_ref[...] = h.T.astype(BF16)

        tile = order_ref[j]
        row = 0
        for k, start in enumerate(TILE_ROWS):
            row = row + jnp.where(tile == k, start // 32, 0)
        w_tile = w_all[pl.ds(pl.multiple_of(row * 32, 32), tn), :]
        proj_ref[...] = _dot_nt(h_all[rows, :], w_tile).astype(BF16)

        @pl.when(j == nj - 1)
        def _():
            lr_ref[...] = _dot_nt(h_all[rows, :], w_all[NAT_LR:NAT_LR + LR_W, :])

        @pl.when(jnp.logical_and(j == nj - 1, i == ni - 1))
        def _():
            everything = range(PIECES)
            passed_on = [cp for n in range(3) for cp in forwards(n, everything)]
            for cp in own_copies(everything) + relays(everything) + passed_on:
                cp.wait_send()
            for a in OTHER_PIECES:
                copy(a, 0, sibling, here).wait_recv()
            for cp in keep_own():
                cp.wait()
            keep = pltpu.make_async_copy(w_all, w_nat, local_sems.at[2])
            keep.start()
            keep.wait()

    const = lambda shape: pl.BlockSpec(shape, lambda j, i, order_ref: (0,) * len(shape))
    hbm = pl.BlockSpec(memory_space=pl.ANY)
    vmem = pl.BlockSpec(memory_space=pltpu.VMEM)
    return pl.pallas_call(
        body, name="gather_inproj",
        out_shape=(jax.ShapeDtypeStruct((seq, MAIN_W), BF16), jax.ShapeDtypeStruct((seq, LR_W), F32),
                   jax.ShapeDtypeStruct((D_MODEL, seq), BF16), jax.ShapeDtypeStruct((IN_W, D_MODEL), BF16),
                   jax.ShapeDtypeStruct((N_DEV,) + w_out_s.shape, BF16),
                   jax.ShapeDtypeStruct((N_DEV,) + small_s.shape, F32)),
        grid_spec=pltpu.PrefetchScalarGridSpec(
            num_scalar_prefetch=1, grid=(nj, ni),
            in_specs=[pl.BlockSpec((tm, D_MODEL), lambda j, i, order_ref: (first_sweep(j, i, order_ref), 0)),
                      const((1, D_MODEL)), vmem, const(w_out_s.shape), const(small_s.shape)],
            out_specs=(pl.BlockSpec((tm, tn), lambda j, i, order_ref: (i, order_ref[j])),
                       pl.BlockSpec((tm, LR_W), lambda j, i, order_ref: (last_sweep(j, i, order_ref), 0)),
                       pl.BlockSpec((D_MODEL, tm), lambda j, i, order_ref: (0, first_sweep(j, i, order_ref))),
                       hbm, hbm, hbm),
            scratch_shapes=[pltpu.VMEM((IN_W, D_MODEL), BF16), pltpu.VMEM((seq, D_MODEL), BF16),
                            pltpu.VMEM((N_DEV, 2, EDGE, D_MODEL), BF16),
                            pltpu.VMEM(w_out_s.shape, BF16), pltpu.VMEM(small_s.shape, F32),
                            pltpu.SemaphoreType.DMA((7 * PIECES,)), pltpu.SemaphoreType.DMA((7 * PIECES,)),
                            pltpu.SemaphoreType.DMA((3,))]),
        compiler_params=_cparams("arbitrary", "arbitrary"),
    )(order, x2d, norm_g, shifted, w_out_s, small_s)


def _block_masks(tt):
    row = lax.broadcasted_iota(jnp.int32, (tt, tt), 0)
    col = lax.broadcasted_iota(jnp.int32, (tt, tt), 1)
    same = jnp.right_shift(row, 6) == jnp.right_shift(col, 6)
    return (jnp.logical_and(same, col <= row), jnp.logical_and(same, col >= row), jnp.logical_and(same, col > row))


def _dot_split3(ones_mat, x):
    x1 = x.astype(BF16)
    r1 = x - x1.astype(F32)
    x2 = r1.astype(BF16)
    x3 = (r1 - x2.astype(F32)).astype(BF16)
    return (_dot(ones_mat, x3) + _dot(ones_mat, x2)) + _dot(ones_mat, x1)


def _log_gate(logits):
    return (jnp.minimum(logits, 0.0) - jnp.log(1.0 + jnp.exp(-jnp.abs(logits)))) * GATE_SCALE


def _chunk_column_mask(tt):
    nc = tt // CHUNK
    row = lax.broadcasted_iota(jnp.int32, (tt, nc * DK), 0)
    col = lax.broadcasted_iota(jnp.int32, (tt, nc * DK), 1)
    return jnp.right_shift(row, 6) == jnp.right_shift(col, 7)


def _chunked(mask, x, nc):
    wide = jnp.concatenate([x] * nc, axis=1)
    return jnp.where(mask, wide, jnp.zeros_like(wide))


def _gla_fwd(proj, lr, wgk_f, wgk_b, bgk_f, bgk_b, tt):
    seq = proj.shape[0]
    nb, nc, nch = seq // tt, tt // CHUNK, seq // CHUNK

    def body(qf, kf, vf, lrf, qb, kb, vb, lrb, wf, wb, bf, bb, of, ob, stf, stb, s_scr, qs_s, ks_s, qin_s, kout_s):
        @pl.when(pl.program_id(0) == 0)
        def _():
            s_scr[...] = jnp.zeros(s_scr.shape, F32)

        low, upp, sup = _block_masks(tt)
        dirs = ((qf, kf, vf, lrf, wf, bf, of, stf, low, low, REF_F, LAST_F, list(range(nc))),
                (qb, kb, vb, lrb, wb, bb, ob, stb, upp, sup, REF_B, LAST_B, list(reversed(range(nc)))))
        for d, (q_r, k_r, v_r, lr_r, w_r, b_r, o_r, st_r, cum, mask, ref, last, order) in enumerate(dirs):
            logits = _dot(lr_r[...].astype(BF16), w_r[...]) + b_r[...]
            b = _dot_split3(cum.astype(BF16), _log_gate(logits))
            decs = []
            for c in range(nc):
                rows = slice(c * CHUNK, (c + 1) * CHUNK)
                bc = b[rows]
                b_ref, b_last = bc[ref:ref + 1], bc[last:last + 1]
                qc = q_r[rows, :].astype(F32) * QSCALE
                kc = k_r[rows, :].astype(F32)
                qs_s[rows, :] = (qc * jnp.exp(bc - b_ref)).astype(BF16)
                ks_s[rows, :] = (kc * jnp.exp(b_ref - bc)).astype(BF16)
                qin_s[rows, :] = (qc * jnp.exp(bc)).astype(BF16)
                kout_s[rows, :] = (kc * jnp.exp(b_last - bc)).astype(BF16)
                decs.append(jnp.exp(b_last))
            for h in range(HEADS):
                ksl = slice(h * DK, (h + 1) * DK)
                vsl = slice(h * DV, (h + 1) * DV)
                v = v_r[:, vsl].astype(BF16)
                att = jnp.where(mask, _dot_nt(qs_s[:, ksl], ks_s[:, ksl]), 0.0).astype(BF16)
                o_intra = _dot(att, v)
                st = s_scr[d * HEADS + h]
                for c in order:
                    rows = slice(c * CHUNK, (c + 1) * CHUNK)
                    stb = st.astype(BF16)
                    st_r[c, h] = stb
                    o_r[rows, vsl] = (o_intra[rows] + _dot_nt(qin_s[rows, ksl], stb)).astype(BF16)
                    st = st * decs[c][:, ksl] + _dot_tn(v[rows], kout_s[rows, ksl])
                s_scr[d * HEADS + h] = st

    fw = lambda i: (i, 0)
    bw = lambda i: (nb - 1 - i, 0)
    const = lambda i: (0, 0)

    def tok_specs(m):
        return [pl.BlockSpec((tt, QK_W), lambda i: (m(i)[0], OFF_Q // QK_W)),
                pl.BlockSpec((tt, QK_W), lambda i: (m(i)[0], OFF_K // QK_W)),
                pl.BlockSpec((tt, V_W), lambda i: (m(i)[0], OFF_V // V_W)),
                pl.BlockSpec((tt, LR_W), m)]

    st_shape = jax.ShapeDtypeStruct((nch, HEADS, DV, DK), BF16)
    o_shape = jax.ShapeDtypeStruct((seq, V_W), BF16)
    operand = pltpu.VMEM((tt, QK_W), BF16)
    return pl.pallas_call(
        body, name="gla_fwd",
        out_shape=(o_shape, o_shape, st_shape, st_shape),
        grid=(nb,),
        in_specs=tok_specs(fw) + tok_specs(bw) + [
            pl.BlockSpec((LR_W, QK_W), const), pl.BlockSpec((LR_W, QK_W), const),
            pl.BlockSpec((1, QK_W), const), pl.BlockSpec((1, QK_W), const)],
        out_specs=(pl.BlockSpec((tt, V_W), fw), pl.BlockSpec((tt, V_W), bw),
                   pl.BlockSpec((nc, HEADS, DV, DK), lambda i: (i, 0, 0, 0)),
                   pl.BlockSpec((nc, HEADS, DV, DK), lambda i: (nb - 1 - i, 0, 0, 0))),
        scratch_shapes=[pltpu.VMEM((2 * HEADS, DV, DK), F32), operand, operand, operand, operand],
        compiler_params=_cparams("arbitrary"),
    )(proj, proj, proj, lr, proj, proj, proj, lr, wgk_f, wgk_b, bgk_f, bgk_b)


def _head_norm(o, gain):
    outs, rinv = [], []
    for h in range(HEADS):
        oh = o[:, h * DV:(h + 1) * DV]
        r = lax.rsqrt(jnp.mean(oh * oh, axis=-1, keepdims=True) + EPS)
        outs.append((oh * r) * gain)
        rinv.append(r)
    return jnp.concatenate(outs, axis=1), rinv


def _shift_rows(u, prev_row, next_row):
    n = u.shape[0]
    row = lax.broadcasted_iota(jnp.int32, (n, 1), 0)
    up = jnp.where(row == 0, prev_row, pltpu.roll(u, 1, 0))
    un = jnp.where(row == n - 1, next_row, pltpu.roll(u, n - 1, 0))
    return up, un


HALO = 16


def _halo_specs(tm, seq, col_block):
    per = tm // HALO
    last = seq // HALO - 1
    return [pl.BlockSpec((HALO, CONV_W), lambda i: (jnp.maximum(i * per - 1, 0), col_block)),
            pl.BlockSpec((HALO, CONV_W), lambda i: (jnp.minimum((i + 1) * per, last), col_block))]


def _f32(ref):
    return ref[...].astype(F32)


def _last_row(ref):
    return ref[HALO - 1:HALO, :].astype(F32)


def _first_row(ref):
    return ref[0:1, :].astype(F32)


def _mix_out_loss(o_f, o_b, proj, x2d, tgt, gla_g, conv_w, conv_b, w_out, final_g, tm):
    seq = x2d.shape[0]
    nt = seq // tm

    def body(of, ob, za, bg, cg, hc, zc, cprev, cnext, hprev, hnext, x_ref, t_ref, gg, cw, cb, wo, fg,
             yt_ref, conv_ref, dx2_ref, dx2b_ref, loss_ref, dfg_ref):
        i = pl.program_id(0)

        @pl.when(i == 0)
        def _():
            loss_ref[...] = jnp.zeros(loss_ref.shape, F32)
            dfg_ref[...] = jnp.zeros(dfg_ref.shape, F32)

        on, _ = _head_norm(_f32(of) + _f32(ob), gg[...])
        zav = _f32(za)
        y_a = on * (zav * _sigmoid(zav))
        u = _f32(cg) * _f32(hc)
        prev_row = jnp.where(i > 0, _last_row(cprev) * _last_row(hprev), 0.0)
        next_row = jnp.where(i < nt - 1, _first_row(cnext) * _first_row(hnext), 0.0)
        up, un = _shift_rows(u, prev_row, next_row)
        conv = (cw[0:1, :] * up + cw[1:2, :] * u + cw[2:3, :] * un) + cb[...]
        conv_ref[...] = conv.astype(BF16)
        zcv = _f32(zc)
        y_c = _f32(bg) * conv * (zcv * _sigmoid(zcv))
        y = jnp.concatenate([y_a, y_c], axis=1)
        yt_ref[...] = y.T.astype(BF16)
        x2 = x_ref[...] + _dot(y.astype(BF16), wo[...])
        r = lax.rsqrt(jnp.mean(x2 * x2, axis=-1, keepdims=True) + EPS)
        xn = x2 * r
        err = xn * fg[...] - t_ref[...]
        loss_ref[...] += 0.5 * jnp.sum(jnp.mean(err * err, axis=-1, keepdims=True))
        dyf = err * (1.0 / D_MODEL)
        dfg_ref[...] += jnp.sum(dyf * xn, axis=0, keepdims=True)
        dxn = dyf * fg[...]
        dx2 = r * dxn - xn * (r * jnp.mean(dxn * xn, axis=-1, keepdims=True))
        dx2_ref[...] = dx2
        dx2b_ref[...] = dx2.astype(BF16)

    def col(off):
        return pl.BlockSpec((tm, CONV_W), lambda i: (i, off // CONV_W))

    rowt = pl.BlockSpec((tm, D_MODEL), lambda i: (i, 0))
    const = lambda shape: pl.BlockSpec(shape, lambda i: (0, 0))
    return pl.pallas_call(
        body, name="mix_out_loss",
        out_shape=(jax.ShapeDtypeStruct((MIX_W, seq), BF16), jax.ShapeDtypeStruct((seq, CONV_W), BF16),
                   jax.ShapeDtypeStruct((seq, D_MODEL), F32), jax.ShapeDtypeStruct((seq, D_MODEL), BF16),
                   jax.ShapeDtypeStruct((8, 128), F32), jax.ShapeDtypeStruct((1, D_MODEL), F32)),
        grid=(nt,),
        in_specs=[rowt, rowt, col(OFF_ZA), col(OFF_B), col(OFF_C), col(OFF_H), col(OFF_ZC)]
        + _halo_specs(tm, seq, OFF_C // CONV_W) + _halo_specs(tm, seq, OFF_H // CONV_W)
        + [rowt, rowt, const((1, DV)), const((8, CONV_W)), const((1, CONV_W)), const((MIX_W, D_MODEL)),
           const((1, D_MODEL))],
        out_specs=(pl.BlockSpec((MIX_W, tm), lambda i: (0, i)), rowt, rowt, rowt, const((8, 128)),
                   const((1, D_MODEL))),
        compiler_params=_cparams("arbitrary"),
    )(o_f, o_b, proj, proj, proj, proj, proj, proj, proj, proj, proj, x2d, tgt, gla_g, conv_w, conv_b, w_out, final_g)


def _dsilu(z, s):
    return s * (1.0 + z * (1.0 - s))


def _mix_bwd(dx2b, o_f, o_b, proj, conv, gla_g, w_out, tm):
    seq = dx2b.shape[0]

    def body(dx, of, ob, za, bg, zc, cv, gg, wo, dg_ref, do_ref, dconv_ref, dgg_ref, dcb_ref):
        @pl.when(pl.program_id(0) == 0)
        def _():
            dgg_ref[...] = jnp.zeros(dgg_ref.shape, F32)
            dcb_ref[...] = jnp.zeros(dcb_ref.shape, F32)

        dy = _dot_nt(dx[...], wo[...])
        dy_a, dy_c = dy[:, :V_W], dy[:, V_W:]
        zcv, bgv, convv = _f32(zc), _f32(bg), _f32(cv)
        sc = _sigmoid(zcv)
        szc = zcv * sc
        dg_ref[:, CONV_W:2 * CONV_W] = (dy_c * convv * szc).astype(BF16)
        dconv = dy_c * bgv * szc
        dconv_ref[...] = dconv.astype(BF16)
        dcb_ref[...] += jnp.sum(dconv, axis=0, keepdims=True)
        dg_ref[:, 2 * CONV_W:] = (dy_c * bgv * convv * _dsilu(zcv, sc)).astype(BF16)

        o = _f32(of) + _f32(ob)
        gain = gg[...]
        on, rinv = _head_norm(o, gain)
        zav = _f32(za)
        sa = _sigmoid(zav)
        dg_ref[:, :CONV_W] = (dy_a * on * _dsilu(zav, sa)).astype(BF16)
        don = dy_a * (zav * sa)
        dgg = jnp.zeros((1, DV), F32)
        dos = []
        for h in range(HEADS):
            sl = slice(h * DV, (h + 1) * DV)
            oh, r, dh = o[:, sl], rinv[h], don[:, sl]
            ohn = oh * r
            dgg = dgg + jnp.sum(dh * ohn, axis=0, keepdims=True)
            dn = dh * gain
            dos.append(r * dn - ohn * (r * jnp.mean(dn * ohn, axis=-1, keepdims=True)))
        dgg_ref[...] += dgg
        do_ref[...] = jnp.concatenate(dos, axis=1).astype(BF16)

    def col(off):
        return pl.BlockSpec((tm, CONV_W), lambda i: (i, off // CONV_W))

    rowt = pl.BlockSpec((tm, D_MODEL), lambda i: (i, 0))
    const = lambda shape: pl.BlockSpec(shape, lambda i: (0, 0))
    return pl.pallas_call(
        body, name="mix_bwd",
        out_shape=(jax.ShapeDtypeStruct((seq, GATES_W), BF16), jax.ShapeDtypeStruct((seq, V_W), BF16),
                   jax.ShapeDtypeStruct((seq, CONV_W), BF16),
                   jax.ShapeDtypeStruct((1, DV), F32), jax.ShapeDtypeStruct((1, CONV_W), F32)),
        grid=(seq // tm,),
        in_specs=[rowt, rowt, rowt, col(OFF_ZA), col(OFF_B), col(OFF_ZC), rowt, const((1, DV)),
                  const((MIX_W, D_MODEL))],
        out_specs=(pl.BlockSpec((tm, GATES_W), lambda i: (i, 0)), rowt, rowt, const((1, DV)), const((1, CONV_W))),
        compiler_params=_cparams("arbitrary"),
    )(dx2b, o_f, o_b, proj, proj, proj, conv, gla_g, w_out)


def _conv_bwd(dconv, proj, conv_w, tm):
    seq = dconv.shape[0]
    nt = seq // tm

    def body(dc_in, dprev, dnext, cg, hc, cprev, cnext, hprev, hnext, cw, dch_ref, dcw_ref):
        i = pl.program_id(0)

        @pl.when(i == 0)
        def _():
            dcw_ref[...] = jnp.zeros(dcw_ref.shape, F32)

        first, lastt = i > 0, i < nt - 1
        dcv = _f32(dc_in)
        d_up, d_un = _shift_rows(dcv, jnp.where(first, _last_row(dprev), 0.0), jnp.where(lastt, _first_row(dnext), 0.0))
        cgv, hcv = _f32(cg), _f32(hc)
        u = cgv * hcv
        u_up, u_un = _shift_rows(u, jnp.where(first, _last_row(cprev) * _last_row(hprev), 0.0),
                                 jnp.where(lastt, _first_row(cnext) * _first_row(hnext), 0.0))
        du = cw[0:1, :] * d_un + cw[1:2, :] * dcv + cw[2:3, :] * d_up
        dch_ref[:, :CONV_W] = (du * hcv).astype(BF16)
        dch_ref[:, CONV_W:] = (du * cgv).astype(BF16)
        dcw_ref[0:1, :] += jnp.sum(dcv * u_up, axis=0, keepdims=True)
        dcw_ref[1:2, :] += jnp.sum(dcv * u, axis=0, keepdims=True)
        dcw_ref[2:3, :] += jnp.sum(dcv * u_un, axis=0, keepdims=True)

    def col(off):
        return pl.BlockSpec((tm, CONV_W), lambda i: (i, off // CONV_W))

    rowt = pl.BlockSpec((tm, CONV_W), lambda i: (i, 0))
    const = lambda shape: pl.BlockSpec(shape, lambda i: (0, 0))
    return pl.pallas_call(
        body, name="conv_bwd",
        out_shape=(jax.ShapeDtypeStruct((seq, CH_W), BF16), jax.ShapeDtypeStruct((8, CONV_W), F32)),
        grid=(nt,),
        in_specs=[rowt] + _halo_specs(tm, seq, 0) + [col(OFF_C), col(OFF_H)]
        + _halo_specs(tm, seq, OFF_C // CONV_W) + _halo_specs(tm, seq, OFF_H // CONV_W) + [const((8, CONV_W))],
        out_specs=(pl.BlockSpec((tm, CH_W), lambda i: (i, 0)), const((8, CONV_W))),
        compiler_params=_cparams("arbitrary"),
    )(dconv, dconv, dconv, proj, proj, proj, proj, proj, proj, conv_w)


def _gla_bwd(proj, lr, do, st_f, st_b, wgk_f, wgk_b, bgk_f, bgk_b, tt):
    seq = proj.shape[0]
    nb, nc = seq // tt, tt // CHUNK

    def body(qf, kf, vf, lrf, dof, stf, qb, kb, vb, lrb, dob, stb, wf, wb, bf, bb,
             dqkv_f, dlr_f, dqkv_b, dlr_b, dwf, dwb, dbf, dbb,
             ds_scr, eq_s, ek_s, ein_s, eout_s, qs_s, ks_s, qin_s, kout_s, db_s, lg_s):
        @pl.when(pl.program_id(0) == 0)
        def _():
            ds_scr[...] = jnp.zeros(ds_scr.shape, F32)
            for r in (dwf, dwb, dbf, dbb):
                r[...] = jnp.zeros(r.shape, F32)

        low, upp, sup = _block_masks(tt)
        row = lax.broadcasted_iota(jnp.int32, (CHUNK, 1), 0)
        kmask = _chunk_column_mask(tt)
        dirs = ((qf, kf, vf, lrf, dof, stf, wf, bf, dqkv_f, dlr_f, dwf, dbf,
                 low, upp, low, REF_F, LAST_F, list(reversed(range(nc)))),
                (qb, kb, vb, lrb, dob, stb, wb, bb, dqkv_b, dlr_b, dwb, dbb,
                 upp, low, sup, REF_B, LAST_B, list(range(nc))))
        for d, (q_r, k_r, v_r, lr_r, do_r, st_r, w_r, b_r, dqkv_r, dlr_r, dw_r, db_r,
                cum, cum_t, mask, ref, last, order) in enumerate(dirs):
            lrv = lr_r[...].astype(BF16)
            wv = w_r[...]
            logits = _dot(lrv, wv) + b_r[...]
            lg_s[...] = logits
            b = _dot_split3(cum.astype(BF16), _log_gate(logits))
            decs = []
            for c in range(nc):
                rows = slice(c * CHUNK, (c + 1) * CHUNK)
                bc = b[rows]
                b_ref, b_last = bc[ref:ref + 1], bc[last:last + 1]
                qc = q_r[rows, :].astype(F32) * QSCALE
                kc = k_r[rows, :].astype(F32)
                e_q, e_k, e_in, e_out = jnp.exp(bc - b_ref), jnp.exp(b_ref - bc), jnp.exp(bc), jnp.exp(b_last - bc)
                eq_s[rows, :], ek_s[rows, :], ein_s[rows, :], eout_s[rows, :] = e_q, e_k, e_in, e_out
                qs_s[rows, :] = (qc * e_q).astype(BF16)
                ks_s[rows, :] = (kc * e_k).astype(BF16)
                qin_s[rows, :] = (qc * e_in).astype(BF16)
                kout_s[rows, :] = (kc * e_out).astype(BF16)
                decs.append(jnp.exp(b_last))
            for h in range(HEADS):
                ksl = slice(h * DK, (h + 1) * DK)
                vsl = slice(h * DV, (h + 1) * DV)
                v = v_r[:, vsl].astype(BF16)
                dov = do_r[:, vsl].astype(BF16)
                qsb, ksb = qs_s[:, ksl], ks_s[:, ksl]
                att = jnp.where(mask, _dot_nt(qsb, ksb), 0.0).astype(BF16)
                datt = jnp.where(mask, _dot_nt(dov, v), 0.0).astype(BF16)
                dqs = _dot(datt, ksb)
                dks = _dot_tn(datt, qsb)
                dv_intra = _dot_tn(att, dov)
                g_t = _dot_tn(dov, _chunked(kmask, qin_s[:, ksl], nc))
                ds = ds_scr[d * HEADS + h]
                for c in order:
                    rows = slice(c * CHUNK, (c + 1) * CHUNK)
                    dsb = ds.astype(BF16)
                    s_prev = st_r[c, h]
                    dk_out = _dot(v[rows], dsb)
                    dq_in = _dot(dov[rows], s_prev)
                    dv = dv_intra[rows] + _dot_nt(kout_s[rows, ksl], dsb)
                    dqkv_r[rows, OFF_V + h * DV:OFF_V + (h + 1) * DV] = dv.astype(BF16)
                    dec = decs[c][:, ksl]
                    ddec = jnp.sum(ds * s_prev.astype(F32), axis=0, keepdims=True)
                    e_out = eout_s[rows, ksl]
                    qc = q_r[rows, ksl].astype(F32) * QSCALE
                    kc = k_r[rows, ksl].astype(F32)
                    dq = dqs[rows] * eq_s[rows, ksl] + dq_in * ein_s[rows, ksl]
                    dk = dks[rows] * ek_s[rows, ksl] + dk_out * e_out
                    dqkv_r[rows, OFF_Q + h * DK:OFF_Q + (h + 1) * DK] = (dq * QSCALE).astype(BF16)
                    dqkv_r[rows, OFF_K + h * DK:OFF_K + (h + 1) * DK] = dk.astype(BF16)
                    tail = jnp.sum(dk_out * (kc * e_out), axis=0, keepdims=True) + ddec * dec
                    db_s[rows, ksl] = (qc * dq - kc * dk) + jnp.where(row == last, tail, 0.0)
                    ds = ds * dec + g_t[:, c * DK:(c + 1) * DK]
                ds_scr[d * HEADS + h] = ds
            dg = _dot_split3(cum_t.astype(BF16), db_s[...])
            dlogit = (dg * GATE_SCALE) * _sigmoid(-lg_s[...])
            dlb = dlogit.astype(BF16)
            dlr_r[...] = _dot_nt(dlb, wv)
            dw_r[...] += _dot_tn(lrv, dlb)
            db_r[...] += jnp.sum(dlogit, axis=0, keepdims=True)

    fw = lambda i: (nb - 1 - i, 0)
    bw = lambda i: (i, 0)
    const = lambda i: (0, 0)

    def tok_specs(m):
        return [pl.BlockSpec((tt, QK_W), lambda i: (m(i)[0], OFF_Q // QK_W)),
                pl.BlockSpec((tt, QK_W), lambda i: (m(i)[0], OFF_K // QK_W)),
                pl.BlockSpec((tt, V_W), lambda i: (m(i)[0], OFF_V // V_W)),
                pl.BlockSpec((tt, LR_W), m),
                pl.BlockSpec((tt, V_W), m),
                pl.BlockSpec((nc, HEADS, DV, DK), lambda i: (m(i)[0], 0, 0, 0))]

    dqkv = jax.ShapeDtypeStruct((seq, QK_W + QK_W + V_W), BF16)
    dlr = jax.ShapeDtypeStruct((seq, LR_W), F32)
    dw = jax.ShapeDtypeStruct((LR_W, QK_W), F32)
    dbias = jax.ShapeDtypeStruct((1, QK_W), F32)
    return pl.pallas_call(
        body, name="gla_bwd",
        out_shape=(dqkv, dlr, dqkv, dlr, dw, dw, dbias, dbias),
        grid=(nb,),
        in_specs=tok_specs(fw) + tok_specs(bw) + [
            pl.BlockSpec((LR_W, QK_W), const), pl.BlockSpec((LR_W, QK_W), const),
            pl.BlockSpec((1, QK_W), const), pl.BlockSpec((1, QK_W), const)],
        out_specs=(pl.BlockSpec((tt, QK_W + QK_W + V_W), fw), pl.BlockSpec((tt, LR_W), fw),
                   pl.BlockSpec((tt, QK_W + QK_W + V_W), bw), pl.BlockSpec((tt, LR_W), bw),
                   pl.BlockSpec((LR_W, QK_W), const), pl.BlockSpec((LR_W, QK_W), const),
                   pl.BlockSpec((1, QK_W), const), pl.BlockSpec((1, QK_W), const)),
        scratch_shapes=[pltpu.VMEM((2 * HEADS, DV, DK), F32)] + [pltpu.VMEM((tt, QK_W), F32)] * 4
        + [pltpu.VMEM((tt, QK_W), BF16)] * 4 + [pltpu.VMEM((tt, QK_W), F32)] * 2,
        compiler_params=_cparams("arbitrary"),
    )(proj, proj, proj, lr, do, st_f, proj, proj, proj, lr, do, st_b, wgk_f, wgk_b, bgk_f, bgk_b)


def _sum_directions(dqkv_f, dqkv_b, dlr_f, dlr_b, tm):
    seq = dqkv_f.shape[0]

    def body(a, b, la, lb, dp_out, dlr_out):
        dp_out[...] = (_f32(a) + _f32(b)).astype(BF16)
        dlr_out[...] = (la[...] + lb[...]).astype(BF16)

    rowt = pl.BlockSpec((tm, QKV_W), lambda i: (i, 0))
    lrt = pl.BlockSpec((tm, LR_W), lambda i: (i, 0))
    return pl.pallas_call(
        body, name="sum_directions",
        out_shape=(jax.ShapeDtypeStruct((seq, QKV_W), BF16), jax.ShapeDtypeStruct((seq, LR_W), BF16)),
        grid=(seq // tm,),
        in_specs=[rowt, rowt, lrt, lrt],
        out_specs=(rowt, lrt),
        compiler_params=_cparams("arbitrary"),
    )(dqkv_f, dqkv_b, dlr_f, dlr_b)


def _input_grad(dp_qkv, dp_gates, dp_ch, dlr, w_nat, x2d, norm_g, dx2, sums, tm):
    seq = x2d.shape[0]
    nt, n = seq // tm, len(sums)

    def body(dq, dg, dc, dl, w, x_ref, g_ref, dx2_ref, *rest):
        ins, (gx_ref, dng_ref), outs = rest[:n], rest[n:n + 2], rest[n + 2:2 * n + 2]
        send_sems, recv_sems = rest[2 * n + 2:]
        i = pl.program_id(0)

        @pl.when(i == 0)
        def _():
            for cp in _chip_copies(ins, outs, send_sems, recv_sems):
                cp.start()
            dng_ref[...] = jnp.zeros(dng_ref.shape, F32)

        dh = (_dot(dl[...], w[NAT_LR:NAT_LR + LR_W, :]) + _dot(dq[...], w[0:NAT_ZA, :])
              + _dot(dg[:, 0:CONV_W], w[NAT_ZA:NAT_LR, :]) + _dot(dg[:, CONV_W:2 * CONV_W], w[NAT_B:NAT_C, :])
              + _dot(dg[:, 2 * CONV_W:], w[NAT_ZC:IN_W, :]) + _dot(dc[...], w[NAT_C:NAT_ZC, :]))
        xv = x_ref[...]
        r = lax.rsqrt(jnp.mean(xv * xv, axis=-1, keepdims=True) + EPS)
        xn = xv * r
        dng_ref[...] += jnp.sum(dh * xn, axis=0, keepdims=True)
        dn = dh * g_ref[...]
        gx_ref[...] = (r * dn - xn * (r * jnp.mean(dn * xn, axis=-1, keepdims=True))) + dx2_ref[...]

        @pl.when(i == nt - 1)
        def _():
            copies = _chip_copies(ins, outs, send_sems, recv_sems)
            for cp in copies:
                cp.wait_recv()
            for cp in copies:
                cp.wait_send()

    rowt = pl.BlockSpec((tm, D_MODEL), lambda i: (i, 0))
    seg = lambda width: pl.BlockSpec((tm, width), lambda i: (i, 0))
    resident = lambda rows: pl.BlockSpec((rows, D_MODEL), lambda i: (0, 0), pipeline_mode=pl.Buffered(1))
    hbm = pl.BlockSpec(memory_space=pl.ANY)
    return pl.pallas_call(
        body, name="input_grad",
        out_shape=(jax.ShapeDtypeStruct((seq, D_MODEL), F32), jax.ShapeDtypeStruct((1, D_MODEL), F32))
        + tuple(jax.ShapeDtypeStruct((3,) + s.shape[1:], s.dtype) for s in sums),
        grid=(nt,),
        in_specs=[seg(QKV_W), seg(GATES_W), seg(CH_W), seg(LR_W), resident(IN_W),
                  rowt, pl.BlockSpec((1, D_MODEL), lambda i: (0, 0)), rowt] + [hbm] * n,
        out_specs=(rowt, pl.BlockSpec((1, D_MODEL), lambda i: (0, 0))) + (hbm,) * n,
        scratch_shapes=[pltpu.SemaphoreType.DMA((3 * n,)), pltpu.SemaphoreType.DMA((3 * n,))],
        compiler_params=_cparams("arbitrary"),
    )(dp_qkv, dp_gates, dp_ch, dlr, w_nat, x2d, norm_g, dx2, *sums)


def _weight_grad_out(y_t, dx2b, tk, riding):
    m, seq = y_t.shape
    n = dx2b.shape[1]
    nk = seq // tk

    def body(a_ref, b_ref, ride_in, o_ref, ride_out, send_sems, recv_sems):
        k = pl.program_id(0)

        @pl.when(k == 0)
        def _():
            _start_all(_sibling_copies(ride_in, ride_out, send_sems, recv_sems))
            o_ref[...] = jnp.zeros(o_ref.shape, F32)

        o_ref[...] += _dot(a_ref[...], b_ref[...])

        @pl.when(k == nk - 1)
        def _():
            _wait_all(_sibling_copies(ride_in, ride_out, send_sems, recv_sems))

    hbm = pl.BlockSpec(memory_space=pl.ANY)
    return pl.pallas_call(
        body, name="wgrad_out",
        out_shape=(jax.ShapeDtypeStruct((m, n), F32), jax.ShapeDtypeStruct((4,) + _block_shape(riding), F32)),
        grid=(nk,),
        in_specs=[pl.BlockSpec((m, tk), lambda k: (0, k)), pl.BlockSpec((tk, n), lambda k: (k, 0)), hbm],
        out_specs=(pl.BlockSpec((m, n), lambda k: (0, 0)), hbm),
        scratch_shapes=[pltpu.SemaphoreType.DMA((4,)), pltpu.SemaphoreType.DMA((4,))],
        compiler_params=_cparams("arbitrary"),
    )(y_t, dx2b, riding)


def _weight_grad_in(h_t, dp_qkv, dp_gates, dp_ch, dlr):
    m, seq = h_t.shape
    tn = 512
    n_qkv, n_gates, n_ch = QKV_W // tn, GATES_W // tn, CH_W // tn
    starts = ([k * tn for k in range(n_qkv)] + [NAT_ZA, NAT_ZA + tn, NAT_B, NAT_B + tn, NAT_ZC, NAT_ZC + tn]
              + [NAT_C + k * tn for k in range(n_ch)])

    def out_row(j):
        row = 0
        for k, start in enumerate(starts):
            row = row + jnp.where(j == k, start // 32, 0)
        return pl.multiple_of(row * 32, 32), 0

    def body(a_ref, bq, bg, bc, o_ref, acc):
        j = pl.program_id(0)

        @pl.when(j < n_qkv)
        def _():
            acc[...] = _dot(a_ref[...], bq[...])

        @pl.when(jnp.logical_and(j >= n_qkv, j < n_qkv + n_gates))
        def _():
            acc[...] = _dot(a_ref[...], bg[...])

        @pl.when(j >= n_qkv + n_gates)
        def _():
            acc[...] = _dot(a_ref[...], bc[...])

        o_ref[...] = acc[...].T

    resident = pl.BlockSpec((m, seq), lambda j: (0, 0), pipeline_mode=pl.Buffered(1))
    seg = lambda first, count: pl.BlockSpec((seq, tn), lambda j: (0, jnp.clip(j - first, 0, count - 1)))
    main = pl.pallas_call(
        body, name="wgrad_in",
        out_shape=jax.ShapeDtypeStruct((IN_W, m), F32),
        grid=(n_qkv + n_gates + n_ch,),
        in_specs=[resident, seg(0, n_qkv), seg(n_qkv, n_gates), seg(n_qkv + n_gates, n_ch)],
        out_specs=pl.BlockSpec((pl.Element(tn), pl.Element(m)), out_row),
        scratch_shapes=[pltpu.VMEM((m, tn), F32)],
        compiler_params=_cparams("arbitrary"),
    )(h_t, dp_qkv, dp_gates, dp_ch)

    def lr_body(a_ref, b_ref, full_ref, o_ref, acc):
        acc[...] = _dot(a_ref[...], b_ref[...])
        o_ref[...] = acc[...].T[0:2 * RANK, :]

    whole = lambda shape: pl.BlockSpec(shape, lambda j: (0, 0))
    return pl.pallas_call(
        lr_body, name="wgrad_lr",
        out_shape=jax.ShapeDtypeStruct((IN_W, m), F32),
        grid=(1,),
        in_specs=[whole((m, seq)), whole((seq, LR_W)), pl.BlockSpec(memory_space=pl.ANY)],
        out_specs=pl.BlockSpec((pl.Element(2 * RANK), pl.Element(m)), lambda j: (NAT_LR, 0)),
        scratch_shapes=[pltpu.VMEM((m, LR_W), F32)],
        input_output_aliases={2: 0},
        compiler_params=_cparams("arbitrary"),
    )(h_t, dlr, main)


def _pad_rows(a, rows):
    return jnp.pad(a, ((0, rows - a.shape[0]), (0, 0)))


def _rows128(a):
    a = a.reshape(-1, 128)
    return _pad_rows(a, -(-a.shape[0] // 8) * 8)


def _pack(arrs):
    return jnp.concatenate([_rows128(a) for a in arrs], axis=0)


def _unpack(buf, like):
    out, start = [], 0
    for a in like:
        rows = a.size // 128
        out.append(buf[start:start + rows].reshape(a.shape))
        start += -(-rows // 8) * 8
    return out


def kernel(x, norm_g, w_in, w_gk_f, b_gk_f, w_gk_b, b_gk_b, gla_norm_g, conv_w, conv_b, w_out, final_g, loss_target, m_norm_g, m_w_in, m_w_gk_f, m_b_gk_f, m_w_gk_b, m_b_gk_b, m_gla_norm_g, m_conv_w, m_conv_b, m_w_out, m_final_g, v_norm_g, v_w_in, v_w_gk_f, v_b_gk_f, v_w_gk_b, v_b_gk_b, v_gla_norm_g, v_conv_w, v_conv_b, v_w_out, v_final_g):
    px, py, pc = _position()
    me = _blk(px, py, pc)
    seq = x.shape[1]
    x2d, tgt = x[0], loss_target[0]
    tm = min(512, seq)
    tt = min(256, seq)

    small_s = jnp.concatenate([jnp.concatenate([w_gk_f[0], w_gk_b[0]], axis=1), _pad_rows(conv_w[0], 8)], axis=0)
    shifted = lax.dynamic_update_slice(jnp.zeros((SHIFTED_ROWS, D_MODEL), F32), w_in[0].T, (4 * (me % 4), 0))
    order = sum(jnp.where(2 * px + py == k, jnp.asarray(tiles + (0,), jnp.int32), 0) for k, tiles in enumerate(TILE_ORDER))
    proj, lr, h_t, w_nat, wout_all, small_all = _gather_inproj(x2d, norm_g, shifted, w_out[0], small_s, order,
                                                               min(1024, seq))
    w_out_full = wout_all.reshape(MIX_W, D_MODEL)
    wgk_cols = 512 // N_DEV
    wgk_f_full = small_all[:, 0:RANK, 0:wgk_cols].transpose(1, 0, 2).reshape(RANK, QK_W)
    wgk_b_full = small_all[:, 0:RANK, wgk_cols:2 * wgk_cols].transpose(1, 0, 2).reshape(RANK, QK_W)
    conv_w_full = _pad_rows(small_all[:, RANK:RANK + 3, :].transpose(1, 0, 2).reshape(3, CONV_W), 8)
    zr = lambda n: jnp.zeros((n, QK_W), F32)
    wgk_f_pad = jnp.concatenate([wgk_f_full, zr(LR_W - RANK)], axis=0).astype(BF16)
    wgk_b_pad = jnp.concatenate([zr(RANK), wgk_b_full, zr(LR_W - 2 * RANK)], axis=0).astype(BF16)

    o_f, o_b, st_f, st_b = _gla_fwd(proj, lr, wgk_f_pad, wgk_b_pad, b_gk_f, b_gk_b, tt)
    tmix = min(256, seq)
    y_t, conv, dx2, dx2b, loss_p, dfg_p = _mix_out_loss(o_f, o_b, proj, x2d, tgt, gla_norm_g, conv_w_full, conv_b,
                                                        w_out_full, final_g.reshape(1, D_MODEL), tmix)

    dp_gates, do, dconv, dgg_p, dcb_p = _mix_bwd(dx2b, o_f, o_b, proj, conv, gla_norm_g, w_out_full, tmix)
    dp_ch, dcw_p = _conv_bwd(dconv, proj, conv_w_full, tmix)
    dqkv_f, dlr_f, dqkv_b, dlr_b, dwf_p, dwb_p, dbf_p, dbb_p = _gla_bwd(
        proj, lr, do, st_f, st_b, wgk_f_pad, wgk_b_pad, b_gk_f, b_gk_b, tt)
    dp_qkv, dlr = _sum_directions(dqkv_f, dqkv_b, dlr_f, dlr_b, tm)
    dw_nat = _weight_grad_in(h_t, dp_qkv, dp_gates, dp_ch, dlr)

    dw_out, sib_in = _weight_grad_out(y_t, dx2b, tm, dw_nat)
    part_out = dw_out.reshape(N_DEV, MIX_W // N_DEV, D_MODEL)
    core = jnp.reshape(pc, (1,)).astype(jnp.int32)
    chip = jnp.reshape(2 * px + py, (1,)).astype(jnp.int32)
    sums_in, sib_out = _chip_sums(dw_nat, sib_in, core, 256, "chip_sums_in", riding=part_out)
    sums_out = _chip_sums(part_out, sib_out, core, 256, "chip_sums_out")
    grad_x2d, dng_p, far_in, far_out = _input_grad(dp_qkv, dp_gates, dp_ch, dlr, w_nat, x2d, norm_g, dx2,
                                                   [sums_in, sums_out], tmix)
    pieces = [dng_p, dbf_p, dbb_p, dgg_p, dcb_p, dfg_p[0], dwf_p[0:RANK], dwb_p[RANK:2 * RANK], dcw_p[0:3], loss_p[0]]
    g_window, small_tot = _final_sum(sums_in, far_in, chip, _pack(pieces), 256, "final_sum_in")
    g_in_t = lax.dynamic_slice_in_dim(g_window, 4 * pc, SHARD_W, axis=0)
    g_w_out, d_w_out, nm_w_out, nv_w_out = _final_sum_adamw(sums_out, far_out, chip, w_out[0], m_w_out[0], v_w_out[0],
                                                            256, "adamw_out")
    flat = lambda a: a[0].T.reshape(SHARD_W, D_MODEL // 128, 128)
    unflat = lambda a: a.reshape(SHARD_W, D_MODEL).T
    d_flat, m_flat, v_flat = _adamw_rows(g_in_t.reshape(SHARD_W, D_MODEL // 128, 128), flat(w_in), flat(m_w_in),
                                         flat(v_w_in), 90, "adamw_in")
    g_w_in, d_w_in, nm_w_in, nv_w_in = g_in_t.T, unflat(d_flat), unflat(m_flat), unflat(v_flat)

    tot = _unpack(small_tot, pieces)
    g_norm_g, g_b_gk_f, g_b_gk_b, g_gla, g_conv_b, g_final = tot[:6]
    g_wgk_f = lax.dynamic_slice_in_dim(tot[6], me * wgk_cols, wgk_cols, axis=1)[None]
    g_wgk_b = lax.dynamic_slice_in_dim(tot[7], me * wgk_cols, wgk_cols, axis=1)[None]
    g_conv_w = lax.dynamic_slice_in_dim(tot[8], me * 128, 128, axis=1)[None]
    loss = tot[9][0]

    small_g = [g_norm_g, g_b_gk_f, g_b_gk_b, g_gla, g_conv_b, g_final, g_wgk_f, g_wgk_b, g_conv_w]
    small_w = [norm_g, b_gk_f, b_gk_b, gla_norm_g, conv_b, final_g, w_gk_f, w_gk_b, conv_w]
    small_m = [m_norm_g, m_b_gk_f, m_b_gk_b, m_gla_norm_g, m_conv_b, m_final_g, m_w_gk_f, m_w_gk_b, m_conv_w]
    small_v = [v_norm_g, v_b_gk_f, v_b_gk_b, v_gla_norm_g, v_conv_b, v_final_g, v_w_gk_f, v_w_gk_b, v_conv_w]
    d_s, m_s, v_s = _adamw_small(_pack(small_g), _pack(small_w), _pack(small_m), _pack(small_v))
    d_l, m_l, v_l = _unpack(d_s, small_w), _unpack(m_s, small_w), _unpack(v_s, small_w)

    def ordered(sm, big_in, big_out):
        return [sm[0], big_in[None], sm[6], sm[1], sm[7], sm[2], sm[3], sm[8], sm[4], big_out[None], sm[5]]

    grads = ordered(small_g, g_w_in, g_w_out)
    deltas = ordered(d_l, d_w_in, d_w_out)
    new_m = ordered(m_l, nm_w_in, nm_w_out)
    new_v = ordered(v_l, nv_w_in, nv_w_out)
    return (loss, grad_x2d[None], *grads, *deltas, *new_m, *new_v)
```

```python
import jax
import jax.numpy as jnp
from jax import lax
from jax.experimental import pallas as pl
from jax.experimental.pallas import tpu as pltpu

F32 = jnp.float32
BF16 = jnp.bfloat16
MESH = pl.DeviceIdType.MESH

N_DEV = 8
D_MODEL = 1024
HEADS = 4
DK = 128
DV = 256
QK_W = HEADS * DK
V_W = HEADS * DV
CONV_W = 1024
MIX_W = V_W + CONV_W
CHUNK = 64
RANK = 16
IN_W = 7200
SHARD_W = IN_W // N_DEV
MAIN_W = 7168
LR_W = 128
OFF_Q, OFF_K, OFF_V, OFF_ZA, OFF_B, OFF_ZC, OFF_C, OFF_H = 0, 512, 1024, 2048, 3072, 4096, 5120, 6144
QKV_W, GATES_W, CH_W = 2048, 3072, 2048
NAT_ZA, NAT_LR, NAT_B, NAT_C, NAT_ZC = 2048, 3072, 3104, 4128, 6176
EPS = 1e-6
GATE_SCALE = 1.0 / 16.0
QSCALE = DK ** -0.5
REF_F, LAST_F = CHUNK // 2, CHUNK - 1
REF_B, LAST_B = CHUNK - 1 - CHUNK // 2, 0

ADAM_LR = 0.001
ADAM_B1 = 0.9
ADAM_B2 = 0.999
ADAM_EPS = 1e-08
ADAM_WD = 0.01
ADAM_STEP = 10

VMEM_LIMIT = 56 * 1024 * 1024


def _cparams(*sem):
    return pltpu.CompilerParams(dimension_semantics=sem, vmem_limit_bytes=VMEM_LIMIT)


def _dot(a, b):
    return jnp.dot(a, b, preferred_element_type=F32)


def _dot_nt(a, b):
    return lax.dot_general(a, b, (((1,), (1,)), ((), ())), preferred_element_type=F32)


def _dot_tn(a, b):
    return lax.dot_general(a, b, (((0,), (0,)), ((), ())), preferred_element_type=F32)


def _sigmoid(z):
    return jax.nn.sigmoid(z)


def _position():
    return lax.axis_index("x"), lax.axis_index("y"), lax.axis_index("c")


def _blk(px, py, pc):
    return 4 * px + 2 * py + pc


EDGE = 16
SHIFTED_ROWS = 912
BODY_ROWS = SHIFTED_ROWS - 2 * EDGE


def _first_tile_row(blk, px):
    return EDGE * (56 * blk + px)


def _edge_tiles():
    tiles = {}
    for blk in range(N_DEV):
        first = _first_tile_row(blk, blk // 4)
        tiles.setdefault(first, []).append((blk, 0))
        tiles.setdefault(first + EDGE + BODY_ROWS, []).append((blk, 1))
    return tiles


def _peer_copies(srcs, outs, send_sems, recv_sems):
    x, y, c = _position()
    me = _blk(x, y, c)
    copies = []
    for a, (src, out) in enumerate(zip(srcs, outs)):
        k = 0
        for dx in (0, 1):
            for dy in (0, 1):
                for dc in (0, 1):
                    if dx + dy + dc == 0:
                        continue
                    peer = (1 - x if dx else x, 1 - y if dy else y, 1 - c if dc else c)
                    copies.append(pltpu.make_async_remote_copy(
                        src_ref=src, dst_ref=out.at[me], send_sem=send_sems.at[a * 7 + k],
                        recv_sem=recv_sems.at[a * 7 + k], device_id=peer, device_id_type=MESH))
                    k += 1
    return copies


def _route_chips():
    x, y, c = _position()
    along_x = c == 0
    return [(jnp.where(along_x, 1 - x, x), jnp.where(along_x, y, 1 - y)),
            (jnp.where(along_x, x, 1 - x), jnp.where(along_x, 1 - y, y)), (1 - x, 1 - y)]


WINDOW_ROWS = SHARD_W + 4


def _window_start(k, parity):
    return 2 * SHARD_W * k + (SHARD_W - 4) * parity


def _owner_block(part, k, parity):
    if part.ndim == 3:
        return part.at[2 * k + parity]
    return part.at[pl.ds(pl.multiple_of(_window_start(k, parity), 8), WINDOW_ROWS)]


def _block_shape(part):
    return part.shape[1:] if part.ndim == 3 else (WINDOW_ROWS, part.shape[1])


def _sibling_copies(part, out, send_sems, recv_sems):
    x, y, c = _position()
    return [pltpu.make_async_remote_copy(src_ref=_owner_block(part, k, 1 - c), dst_ref=out.at[k],
                                         send_sem=send_sems.at[k], recv_sem=recv_sems.at[k],
                                         device_id=(x, y, 1 - c), device_id_type=MESH)
            for k in range(4)]


def _start_all(copies):
    for cp in copies:
        cp.start()


def _wait_all(copies):
    for cp in copies:
        cp.wait_recv()
    for cp in copies:
        cp.wait_send()


def _chip_sums(part, from_sibling, core, tc, name, riding=None):
    rows, cols = _block_shape(part)
    nj = cols // tc

    def body(core_ref, p_ref, s_ref, *rest):
        if riding is None:
            (o_ref,) = rest
        else:
            ride_in, o_ref, ride_out, send_sems, recv_sems = rest
            k, j = pl.program_id(0), pl.program_id(1)

            @pl.when(jnp.logical_and(k == 0, j == 0))
            def _():
                _start_all(_sibling_copies(ride_in, ride_out, send_sems, recv_sems))

        o_ref[0] = (p_ref[...].reshape(rows, tc) + s_ref[0]).astype(BF16)

        if riding is not None:
            @pl.when(jnp.logical_and(k == 3, j == nj - 1))
            def _():
                _wait_all(_sibling_copies(ride_in, ride_out, send_sems, recv_sems))

    hbm = pl.BlockSpec(memory_space=pl.ANY)
    sums = jax.ShapeDtypeStruct((4, rows, cols), BF16)
    tile_out = pl.BlockSpec((1, rows, tc), lambda k, j, core_ref: (k, 0, j))
    if part.ndim == 3:
        mine = pl.BlockSpec((1, rows, tc), lambda k, j, core_ref: (2 * k + core_ref[0], 0, j))
    else:
        mine = pl.BlockSpec((pl.Element(rows), pl.Element(tc)),
                            lambda k, j, core_ref: (pl.multiple_of(_window_start(k, core_ref[0]), 8),
                                                    pl.multiple_of(j * tc, 128)))
    in_specs = [mine, pl.BlockSpec((1, rows, tc), lambda k, j, core_ref: (k, 0, j))]
    if riding is None:
        out_shape, out_specs, scratch, args = sums, tile_out, [], (core, part, from_sibling)
    else:
        out_shape = (sums, jax.ShapeDtypeStruct((4,) + _block_shape(riding), F32))
        out_specs, in_specs = (tile_out, hbm), in_specs + [hbm]
        scratch = [pltpu.SemaphoreType.DMA((4,)), pltpu.SemaphoreType.DMA((4,))]
        args = (core, part, from_sibling, riding)
    return pl.pallas_call(
        body, name=name, out_shape=out_shape,
        grid_spec=pltpu.PrefetchScalarGridSpec(num_scalar_prefetch=1, grid=(4, nj), in_specs=in_specs,
                                               out_specs=out_specs, scratch_shapes=scratch),
        compiler_params=_cparams("arbitrary", "arbitrary"),
    )(*args)


def _sum_chips(s_ref, r_ref):
    f = lambda a: a.astype(F32)
    return (f(s_ref[0]) + f(r_ref[0])) + f(r_ref[1])


def _final_sum(sums, from_chips, chip, small, tc, name):
    _, rows, cols = sums.shape
    nj = cols // tc

    def body(chip_ref, s_ref, r_ref, sm_ref, g_out, tot_ref, all_ref, send_sems, recv_sems):
        j = pl.program_id(0)
        me = _blk(*_position())

        @pl.when(j == 0)
        def _():
            all_ref[me] = sm_ref[...]
            _start_all(_peer_copies((all_ref.at[me],), (all_ref,), send_sems, recv_sems))

        g_out[...] = _sum_chips(s_ref, r_ref)

        @pl.when(j == nj - 1)
        def _():
            _wait_all(_peer_copies((all_ref.at[me],), (all_ref,), send_sems, recv_sems))
            acc = all_ref[0]
            for d in range(1, N_DEV):
                acc = acc + all_ref[d]
            tot_ref[...] = acc

    whole = pl.BlockSpec(small.shape, lambda j, chip_ref: (0, 0))
    return pl.pallas_call(
        body, name=name,
        out_shape=(jax.ShapeDtypeStruct((rows, cols), F32), jax.ShapeDtypeStruct(small.shape, F32)),
        grid_spec=pltpu.PrefetchScalarGridSpec(
            num_scalar_prefetch=1, grid=(nj,),
            in_specs=[pl.BlockSpec((1, rows, tc), lambda j, chip_ref: (chip_ref[0], 0, j)),
                      pl.BlockSpec((2, rows, tc), lambda j, chip_ref: (0, 0, j)), whole],
            out_specs=(pl.BlockSpec((rows, tc), lambda j, chip_ref: (0, j)), whole),
            scratch_shapes=[pltpu.VMEM((N_DEV,) + small.shape, F32), pltpu.SemaphoreType.DMA((7,)),
                            pltpu.SemaphoreType.DMA((7,))]),
        compiler_params=_cparams("arbitrary"),
    )(chip, sums, from_chips, small)


def _adamw_rows(g, w, m, v, tr, name):
    rows = g.shape[0]

    def body(g_ref, w_ref, m_ref, v_ref, d_out, m_out, v_out):
        delta, m_new, v_new = _adamw(w_ref[...], g_ref[...], m_ref[...], v_ref[...])
        d_out[...] = delta
        m_out[...] = m_new
        v_out[...] = v_new

    tile = pl.BlockSpec((tr,) + g.shape[1:], lambda r: (r, 0, 0))
    shp = jax.ShapeDtypeStruct(g.shape, F32)
    return pl.pallas_call(
        body, name=name, out_shape=(shp, shp, shp), grid=(rows // tr,),
        in_specs=[tile] * 4, out_specs=(tile, tile, tile),
        compiler_params=_cparams("arbitrary"),
    )(g, w, m, v)


def _adamw(w, g, m, v):
    m = ADAM_B1 * m + (1.0 - ADAM_B1) * g
    v = ADAM_B2 * v + (1.0 - ADAM_B2) * (g * g)
    m_hat = m / (1.0 - ADAM_B1 ** ADAM_STEP)
    v_hat = v / (1.0 - ADAM_B2 ** ADAM_STEP)
    delta = -ADAM_LR * (m_hat / (jnp.sqrt(v_hat) + ADAM_EPS) + ADAM_WD * w)
    return delta, m, v


def _final_sum_adamw(sums, from_chips, chip, w, m, v, tr, name):
    rows, cols = w.shape

    def body(chip_ref, s_ref, r_ref, w_ref, m_ref, v_ref, g_out, d_out, m_out, v_out):
        g = _sum_chips(s_ref, r_ref)
        delta, m_new, v_new = _adamw(w_ref[...], g, m_ref[...], v_ref[...])
        g_out[...] = g
        d_out[...] = delta
        m_out[...] = m_new
        v_out[...] = v_new

    tile = pl.BlockSpec((tr, cols), lambda r, chip_ref: (r, 0))
    shp = jax.ShapeDtypeStruct((rows, cols), F32)
    return pl.pallas_call(
        body, name=name,
        out_shape=(shp, shp, shp, shp),
        grid_spec=pltpu.PrefetchScalarGridSpec(
            num_scalar_prefetch=1, grid=(rows // tr,),
            in_specs=[pl.BlockSpec((1, tr, cols), lambda r, chip_ref: (chip_ref[0], r, 0)),
                      pl.BlockSpec((2, tr, cols), lambda r, chip_ref: (0, r, 0)),
                      tile, tile, tile],
            out_specs=(tile, tile, tile, tile)),
        compiler_params=_cparams("arbitrary"),
    )(chip, sums, from_chips, w, m, v)


def _adamw_small(g, w, m, v):
    def body(g_ref, w_ref, m_ref, v_ref, d_out, m_out, v_out):
        delta, m_new, v_new = _adamw(w_ref[...], g_ref[...], m_ref[...], v_ref[...])
        d_out[...] = delta
        m_out[...] = m_new
        v_out[...] = v_new

    vmem = pl.BlockSpec(memory_space=pltpu.VMEM)
    shp = jax.ShapeDtypeStruct(g.shape, F32)
    return pl.pallas_call(body, name="adamw_small", out_shape=(shp, shp, shp),
                          in_specs=[vmem] * 4, out_specs=(vmem, vmem, vmem))(g, w, m, v)


TILE_ROWS = (0, 1024, NAT_ZA, NAT_B, NAT_ZC, NAT_C, NAT_C + CONV_W)


TILE_ORDER = ((0, 1, 2, 3, 5, 6, 4), (2, 0, 1, 4, 3, 5, 6), (5, 0, 6, 4, 1, 2, 3), (4, 2, 3, 5, 6, 0, 1))
NEIGHBOUR_SWEEP, DIAGONAL_SWEEP = 1, 4
PIECES, W_IN_PIECES, OTHER_PIECES = 4, (0, 1), (2, 3)


def _gather_inproj(x2d, norm_g, shifted, w_out_s, small_s, order, tm):
    seq = x2d.shape[0]
    tn = CONV_W
    ni, nj = seq // tm, MAIN_W // tn
    first_sweep = lambda j, i, order_ref: jnp.where(j == 0, i, ni - 1)
    last_sweep = lambda j, i, order_ref: jnp.where(j == nj - 1, i, 0)
    edge_tiles = _edge_tiles()

    def body(order_ref, x_ref, g_ref, sh_ref, wout_ref, sm_ref, proj_ref, lr_ref, ht_ref, w_nat, wout_all, sm_all,
             w_all, h_all, edges, wout_b, sm_b, send_sems, recv_sems, local_sems):
        j, i = pl.program_id(0), pl.program_id(1)
        rows = pl.ds(pl.multiple_of(i * tm, tm), tm)
        x, y, c = _position()
        me, here, sibling = _blk(x, y, c), (x, y, c), (x, y, 1 - c)
        chips = _route_chips()
        sibling_chips = [chips[1], chips[0], chips[2]]

        def pieces(px, py, pc):
            blk = _blk(px, py, pc)
            body_rows = pl.ds(pl.multiple_of(_first_tile_row(blk, px) + EDGE, EDGE), BODY_ROWS)
            return [w_all.at[body_rows], edges.at[blk], wout_all.at[blk], sm_all.at[blk]]

        def copy(a, k, block, to, staged=None):
            ref = pieces(*block)[a]
            return pltpu.make_async_remote_copy(src_ref=ref if staged is None else staged, dst_ref=ref,
                                                send_sem=send_sems.at[a * 7 + k], recv_sem=recv_sems.at[a * 7 + k],
                                                device_id=to, device_id_type=MESH)

        def own_copies(group):
            targets = [(0, sibling)] + [(1 + n, (*chips[n], c)) for n in range(2)]
            staged = [None, None, wout_b, sm_b]
            return [copy(a, k, here, to, staged[a]) for k, to in targets for a in group]

        def relays(group):
            return [copy(a, 3, (*chips[0], c), (*chips[1], c)) for a in group]

        def forwards(n, group):
            return [copy(a, 4 + n, (*chips[n], c), sibling) for a in group]

        def keep_own():
            return [pltpu.make_async_copy(wout_b, wout_all.at[me], local_sems.at[0]),
                    pltpu.make_async_copy(sm_b, sm_all.at[me], local_sems.at[1])]

        def arrive(ns, group):
            for n in ns:
                for a in group:
                    copy(a, 1 + n, (*chips[n], c), here).wait_recv()
                _start_all((relays(group) if n == 0 else []) + forwards(n, group))
            for n in ns:
                for a in group:
                    copy(a, 4 + n, (*sibling_chips[n], 1 - c), here).wait_recv()

        def add_edge_tiles(stage):
            for row, parts in edge_tiles.items():
                ready = 0
                for blk, _ in parts:
                    away = (x != blk // 4).astype(jnp.int32) + (y != (blk // 2) % 2).astype(jnp.int32)
                    ready = jnp.maximum(ready, away)

                @pl.when(ready == stage)
                def _(row=row, parts=parts):
                    tile = edges[parts[0][0], parts[0][1]].astype(F32)
                    for blk, side in parts[1:]:
                        tile = tile + edges[blk, side].astype(F32)
                    w_all[row:row + EDGE, :] = tile.astype(BF16)

        @pl.when(jnp.logical_and(j == 0, i == 0))
        def _():
            pieces(*here)[0][...] = sh_ref[EDGE:EDGE + BODY_ROWS, :].astype(BF16)
            edges[me, 0] = sh_ref[0:EDGE, :].astype(BF16)
            edges[me, 1] = sh_ref[EDGE + BODY_ROWS:, :].astype(BF16)
            _start_all(own_copies(W_IN_PIECES))
            wout_b[...] = wout_ref[...].astype(BF16)
            sm_b[...] = sm_ref[...]
            _start_all(own_copies(OTHER_PIECES) + keep_own())
            for a in W_IN_PIECES:
                copy(a, 0, sibling, here).wait_recv()
            add_edge_tiles(0)

        @pl.when(jnp.logical_and(j == NEIGHBOUR_SWEEP, i == 0))
        def _():
            arrive((0, 1), W_IN_PIECES)
            add_edge_tiles(1)

        @pl.when(jnp.logical_and(j == DIAGONAL_SWEEP, i == 0))
        def _():
            arrive((2,), W_IN_PIECES)
            add_edge_tiles(2)
            arrive((0, 1), OTHER_PIECES)

        @pl.when(jnp.logical_and(j == nj - 1, i == 0))
        def _():
            arrive((2,), OTHER_PIECES)

        @pl.when(j == 0)
        def _():
            xv = x_ref[...]
            r = lax.rsqrt(jnp.mean(xv * xv, axis=-1, keepdims=True) + EPS)
            h = (xv * r) * g_ref[...]
            h_all[rows, :] = h.astype(BF16)
            ht_ref[...] = h.T.astype(BF16)

        tile = order_ref[j]
        row = 0
        for k, start in enumerate(TILE_ROWS):
            row = row + jnp.where(tile == k, start // 32, 0)
        w_tile = w_all[pl.ds(pl.multiple_of(row * 32, 32), tn), :]
        proj_ref[...] = _dot_nt(h_all[rows, :], w_tile).astype(BF16)

        @pl.when(j == nj - 1)
        def _():
            lr_ref[...] = _dot_nt(h_all[rows, :], w_all[NAT_LR:NAT_LR + LR_W, :])

        @pl.when(jnp.logical_and(j == nj - 1, i == ni - 1))
        def _():
            everything = range(PIECES)
            passed_on = [cp for n in range(3) for cp in forwards(n, everything)]
            for cp in own_copies(everything) + relays(everything) + passed_on:
                cp.wait_send()
            for a in OTHER_PIECES:
                copy(a, 0, sibling, here).wait_recv()
            for cp in keep_own():
                cp.wait()
            keep = pltpu.make_async_copy(w_all, w_nat, local_sems.at[2])
            keep.start()
            keep.wait()

    const = lambda shape: pl.BlockSpec(shape, lambda j, i, order_ref: (0,) * len(shape))
    hbm = pl.BlockSpec(memory_space=pl.ANY)
    vmem = pl.BlockSpec(memory_space=pltpu.VMEM)
    return pl.pallas_call(
        body, name="gather_inproj",
        out_shape=(jax.ShapeDtypeStruct((seq, MAIN_W), BF16), jax.ShapeDtypeStruct((seq, LR_W), F32),
                   jax.ShapeDtypeStruct((D_MODEL, seq), BF16), jax.ShapeDtypeStruct((IN_W, D_MODEL), BF16),
                   jax.ShapeDtypeStruct((N_DEV,) + w_out_s.shape, BF16),
                   jax.ShapeDtypeStruct((N_DEV,) + small_s.shape, F32)),
        grid_spec=pltpu.PrefetchScalarGridSpec(
            num_scalar_prefetch=1, grid=(nj, ni),
            in_specs=[pl.BlockSpec((tm, D_MODEL), lambda j, i, order_ref: (first_sweep(j, i, order_ref), 0)),
                      const((1, D_MODEL)), vmem, const(w_out_s.shape), const(small_s.shape)],
            out_specs=(pl.BlockSpec((tm, tn), lambda j, i, order_ref: (i, order_ref[j])),
                       pl.BlockSpec((tm, LR_W), lambda j, i, order_ref: (last_sweep(j, i, order_ref), 0)),
                       pl.BlockSpec((D_MODEL, tm), lambda j, i, order_ref: (0, first_sweep(j, i, order_ref))),
                       hbm, hbm, hbm),
            scratch_shapes=[pltpu.VMEM((IN_W, D_MODEL), BF16), pltpu.VMEM((seq, D_MODEL), BF16),
                            pltpu.VMEM((N_DEV, 2, EDGE, D_MODEL), BF16),
                            pltpu.VMEM(w_out_s.shape, BF16), pltpu.VMEM(small_s.shape, F32),
                            pltpu.SemaphoreType.DMA((7 * PIECES,)), pltpu.SemaphoreType.DMA((7 * PIECES,)),
                            pltpu.SemaphoreType.DMA((3,))]),
        compiler_params=_cparams("arbitrary", "arbitrary"),
    )(order, x2d, norm_g, shifted, w_out_s, small_s)


def _block_masks(tt):
    row = lax.broadcasted_iota(jnp.int32, (tt, tt), 0)
    col = lax.broadcasted_iota(jnp.int32, (tt, tt), 1)
    same = jnp.right_shift(row, 6) == jnp.right_shift(col, 6)
    return (jnp.logical_and(same, col <= row), jnp.logical_and(same, col >= row), jnp.logical_and(same, col > row))


def _dot_split3(ones_mat, x):
    x1 = x.astype(BF16)
    r1 = x - x1.astype(F32)
    x2 = r1.astype(BF16)
    x3 = (r1 - x2.astype(F32)).astype(BF16)
    return (_dot(ones_mat, x3) + _dot(ones_mat, x2)) + _dot(ones_mat, x1)


def _log_gate(logits):
    return (jnp.minimum(logits, 0.0) - jnp.log(1.0 + jnp.exp(-jnp.abs(logits)))) * GATE_SCALE


def _chunk_column_mask(tt):
    nc = tt // CHUNK
    row = lax.broadcasted_iota(jnp.int32, (tt, nc * DK), 0)
    col = lax.broadcasted_iota(jnp.int32, (tt, nc * DK), 1)
    return jnp.right_shift(row, 6) == jnp.right_shift(col, 7)


def _chunked(mask, x, nc):
    wide = jnp.concatenate([x] * nc, axis=1)
    return jnp.where(mask, wide, jnp.zeros_like(wide))


def _gla_fwd(proj, lr, wgk_f, wgk_b, bgk_f, bgk_b, tt):
    seq = proj.shape[0]
    nb, nc, nch = seq // tt, tt // CHUNK, seq // CHUNK

    def body(qf, kf, vf, lrf, qb, kb, vb, lrb, wf, wb, bf, bb, of, ob, stf, stb, s_scr, qs_s, ks_s, qin_s, kout_s):
        @pl.when(pl.program_id(0) == 0)
        def _():
            s_scr[...] = jnp.zeros(s_scr.shape, F32)

        low, upp, sup = _block_masks(tt)
        dirs = ((qf, kf, vf, lrf, wf, bf, of, stf, low, low, REF_F, LAST_F, list(range(nc))),
                (qb, kb, vb, lrb, wb, bb, ob, stb, upp, sup, REF_B, LAST_B, list(reversed(range(nc)))))
        for d, (q_r, k_r, v_r, lr_r, w_r, b_r, o_r, st_r, cum, mask, ref, last, order) in enumerate(dirs):
            logits = _dot(lr_r[...].astype(BF16), w_r[...]) + b_r[...]
            b = _dot_split3(cum.astype(BF16), _log_gate(logits))
            decs = []
            for c in range(nc):
                rows = slice(c * CHUNK, (c + 1) * CHUNK)
                bc = b[rows]
                b_ref, b_last = bc[ref:ref + 1], bc[last:last + 1]
                qc = q_r[rows, :].astype(F32) * QSCALE
                kc = k_r[rows, :].astype(F32)
                qs_s[rows, :] = (qc * jnp.exp(bc - b_ref)).astype(BF16)
                ks_s[rows, :] = (kc * jnp.exp(b_ref - bc)).astype(BF16)
                qin_s[rows, :] = (qc * jnp.exp(bc)).astype(BF16)
                kout_s[rows, :] = (kc * jnp.exp(b_last - bc)).astype(BF16)
                decs.append(jnp.exp(b_last))
            for h in range(HEADS):
                ksl = slice(h * DK, (h + 1) * DK)
                vsl = slice(h * DV, (h + 1) * DV)
                v = v_r[:, vsl].astype(BF16)
                att = jnp.where(mask, _dot_nt(qs_s[:, ksl], ks_s[:, ksl]), 0.0).astype(BF16)
                o_intra = _dot(att, v)
                st = s_scr[d * HEADS + h]
                for c in order:
                    rows = slice(c * CHUNK, (c + 1) * CHUNK)
                    stb = st.astype(BF16)
                    st_r[c, h] = stb
                    o_r[rows, vsl] = (o_intra[rows] + _dot_nt(qin_s[rows, ksl], stb)).astype(BF16)
                    st = st * decs[c][:, ksl] + _dot_tn(v[rows], kout_s[rows, ksl])
                s_scr[d * HEADS + h] = st

    fw = lambda i: (i, 0)
    bw = lambda i: (nb - 1 - i, 0)
    const = lambda i: (0, 0)

    def tok_specs(m):
        return [pl.BlockSpec((tt, QK_W), lambda i: (m(i)[0], OFF_Q // QK_W)),
                pl.BlockSpec((tt, QK_W), lambda i: (m(i)[0], OFF_K // QK_W)),
                pl.BlockSpec((tt, V_W), lambda i: (m(i)[0], OFF_V // V_W)),
                pl.BlockSpec((tt, LR_W), m)]

    st_shape = jax.ShapeDtypeStruct((nch, HEADS, DV, DK), BF16)
    o_shape = jax.ShapeDtypeStruct((seq, V_W), BF16)
    operand = pltpu.VMEM((tt, QK_W), BF16)
    return pl.pallas_call(
        body, name="gla_fwd",
        out_shape=(o_shape, o_shape, st_shape, st_shape),
        grid=(nb,),
        in_specs=tok_specs(fw) + tok_specs(bw) + [
            pl.BlockSpec((LR_W, QK_W), const), pl.BlockSpec((LR_W, QK_W), const),
            pl.BlockSpec((1, QK_W), const), pl.BlockSpec((1, QK_W), const)],
        out_specs=(pl.BlockSpec((tt, V_W), fw), pl.BlockSpec((tt, V_W), bw),
                   pl.BlockSpec((nc, HEADS, DV, DK), lambda i: (i, 0, 0, 0)),
                   pl.BlockSpec((nc, HEADS, DV, DK), lambda i: (nb - 1 - i, 0, 0, 0))),
        scratch_shapes=[pltpu.VMEM((2 * HEADS, DV, DK), F32), operand, operand, operand, operand],
        compiler_params=_cparams("arbitrary"),
    )(proj, proj, proj, lr, proj, proj, proj, lr, wgk_f, wgk_b, bgk_f, bgk_b)


def _head_norm(o, gain):
    outs, rinv = [], []
    for h in range(HEADS):
        oh = o[:, h * DV:(h + 1) * DV]
        r = lax.rsqrt(jnp.mean(oh * oh, axis=-1, keepdims=True) + EPS)
        outs.append((oh * r) * gain)
        rinv.append(r)
    return jnp.concatenate(outs, axis=1), rinv


def _shift_rows(u, prev_row, next_row):
    n = u.shape[0]
    row = lax.broadcasted_iota(jnp.int32, (n, 1), 0)
    up = jnp.where(row == 0, prev_row, pltpu.roll(u, 1, 0))
    un = jnp.where(row == n - 1, next_row, pltpu.roll(u, n - 1, 0))
    return up, un


HALO = 16


def _halo_specs(tm, seq, col_block):
    per = tm // HALO
    last = seq // HALO - 1
    return [pl.BlockSpec((HALO, CONV_W), lambda i: (jnp.maximum(i * per - 1, 0), col_block)),
            pl.BlockSpec((HALO, CONV_W), lambda i: (jnp.minimum((i + 1) * per, last), col_block))]


def _f32(ref):
    return ref[...].astype(F32)


def _last_row(ref):
    return ref[HALO - 1:HALO, :].astype(F32)


def _first_row(ref):
    return ref[0:1, :].astype(F32)


def _mix_out_loss(o_f, o_b, proj, x2d, tgt, gla_g, conv_w, conv_b, w_out, final_g, tm):
    seq = x2d.shape[0]
    nt = seq // tm

    def body(of, ob, za, bg, cg, hc, zc, cprev, cnext, hprev, hnext, x_ref, t_ref, gg, cw, cb, wo, fg,
             yt_ref, conv_ref, dx2_ref, dx2b_ref, loss_ref, dfg_ref):
        i = pl.program_id(0)

        @pl.when(i == 0)
        def _():
            loss_ref[...] = jnp.zeros(loss_ref.shape, F32)
            dfg_ref[...] = jnp.zeros(dfg_ref.shape, F32)

        on, _ = _head_norm(_f32(of) + _f32(ob), gg[...])
        zav = _f32(za)
        y_a = on * (zav * _sigmoid(zav))
        u = _f32(cg) * _f32(hc)
        prev_row = jnp.where(i > 0, _last_row(cprev) * _last_row(hprev), 0.0)
        next_row = jnp.where(i < nt - 1, _first_row(cnext) * _first_row(hnext), 0.0)
        up, un = _shift_rows(u, prev_row, next_row)
        conv = (cw[0:1, :] * up + cw[1:2, :] * u + cw[2:3, :] * un) + cb[...]
        conv_ref[...] = conv.astype(BF16)
        zcv = _f32(zc)
        y_c = _f32(bg) * conv * (zcv * _sigmoid(zcv))
        y = jnp.concatenate([y_a, y_c], axis=1)
        yt_ref[...] = y.T.astype(BF16)
        x2 = x_ref[...] + _dot(y.astype(BF16), wo[...])
        r = lax.rsqrt(jnp.mean(x2 * x2, axis=-1, keepdims=True) + EPS)
        xn = x2 * r
        err = xn * fg[...] - t_ref[...]
        loss_ref[...] += 0.5 * jnp.sum(jnp.mean(err * err, axis=-1, keepdims=True))
        dyf = err * (1.0 / D_MODEL)
        dfg_ref[...] += jnp.sum(dyf * xn, axis=0, keepdims=True)
        dxn = dyf * fg[...]
        dx2 = r * dxn - xn * (r * jnp.mean(dxn * xn, axis=-1, keepdims=True))
        dx2_ref[...] = dx2
        dx2b_ref[...] = dx2.astype(BF16)

    def col(off):
        return pl.BlockSpec((tm, CONV_W), lambda i: (i, off // CONV_W))

    rowt = pl.BlockSpec((tm, D_MODEL), lambda i: (i, 0))
    const = lambda shape: pl.BlockSpec(shape, lambda i: (0, 0))
    return pl.pallas_call(
        body, name="mix_out_loss",
        out_shape=(jax.ShapeDtypeStruct((MIX_W, seq), BF16), jax.ShapeDtypeStruct((seq, CONV_W), BF16),
                   jax.ShapeDtypeStruct((seq, D_MODEL), F32), jax.ShapeDtypeStruct((seq, D_MODEL), BF16),
                   jax.ShapeDtypeStruct((8, 128), F32), jax.ShapeDtypeStruct((1, D_MODEL), F32)),
        grid=(nt,),
        in_specs=[rowt, rowt, col(OFF_ZA), col(OFF_B), col(OFF_C), col(OFF_H), col(OFF_ZC)]
        + _halo_specs(tm, seq, OFF_C // CONV_W) + _halo_specs(tm, seq, OFF_H // CONV_W)
        + [rowt, rowt, const((1, DV)), const((8, CONV_W)), const((1, CONV_W)), const((MIX_W, D_MODEL)),
           const((1, D_MODEL))],
        out_specs=(pl.BlockSpec((MIX_W, tm), lambda i: (0, i)), rowt, rowt, rowt, const((8, 128)),
                   const((1, D_MODEL))),
        compiler_params=_cparams("arbitrary"),
    )(o_f, o_b, proj, proj, proj, proj, proj, proj, proj, proj, proj, x2d, tgt, gla_g, conv_w, conv_b, w_out, final_g)


def _dsilu(z, s):
    return s * (1.0 + z * (1.0 - s))


def _mix_bwd(dx2b, o_f, o_b, proj, conv, gla_g, w_out, tm):
    seq = dx2b.shape[0]

    def body(dx, of, ob, za, bg, zc, cv, gg, wo, dg_ref, do_ref, dconv_ref, dgg_ref, dcb_ref):
        @pl.when(pl.program_id(0) == 0)
        def _():
            dgg_ref[...] = jnp.zeros(dgg_ref.shape, F32)
            dcb_ref[...] = jnp.zeros(dcb_ref.shape, F32)

        dy = _dot_nt(dx[...], wo[...])
        dy_a, dy_c = dy[:, :V_W], dy[:, V_W:]
        zcv, bgv, convv = _f32(zc), _f32(bg), _f32(cv)
        sc = _sigmoid(zcv)
        szc = zcv * sc
        dg_ref[:, CONV_W:2 * CONV_W] = (dy_c * convv * szc).astype(BF16)
        dconv = dy_c * bgv * szc
        dconv_ref[...] = dconv.astype(BF16)
        dcb_ref[...] += jnp.sum(dconv, axis=0, keepdims=True)
        dg_ref[:, 2 * CONV_W:] = (dy_c * bgv * convv * _dsilu(zcv, sc)).astype(BF16)

        o = _f32(of) + _f32(ob)
        gain = gg[...]
        on, rinv = _head_norm(o, gain)
        zav = _f32(za)
        sa = _sigmoid(zav)
        dg_ref[:, :CONV_W] = (dy_a * on * _dsilu(zav, sa)).astype(BF16)
        don = dy_a * (zav * sa)
        dgg = jnp.zeros((1, DV), F32)
        dos = []
        for h in range(HEADS):
            sl = slice(h * DV, (h + 1) * DV)
            oh, r, dh = o[:, sl], rinv[h], don[:, sl]
            ohn = oh * r
            dgg = dgg + jnp.sum(dh * ohn, axis=0, keepdims=True)
            dn = dh * gain
            dos.append(r * dn - ohn * (r * jnp.mean(dn * ohn, axis=-1, keepdims=True)))
        dgg_ref[...] += dgg
        do_ref[...] = jnp.concatenate(dos, axis=1).astype(BF16)

    def col(off):
        return pl.BlockSpec((tm, CONV_W), lambda i: (i, off // CONV_W))

    rowt = pl.BlockSpec((tm, D_MODEL), lambda i: (i, 0))
    const = lambda shape: pl.BlockSpec(shape, lambda i: (0, 0))
    return pl.pallas_call(
        body, name="mix_bwd",
        out_shape=(jax.ShapeDtypeStruct((seq, GATES_W), BF16), jax.ShapeDtypeStruct((seq, V_W), BF16),
                   jax.ShapeDtypeStruct((seq, CONV_W), BF16),
                   jax.ShapeDtypeStruct((1, DV), F32), jax.ShapeDtypeStruct((1, CONV_W), F32)),
        grid=(seq // tm,),
        in_specs=[rowt, rowt, rowt, col(OFF_ZA), col(OFF_B), col(OFF_ZC), rowt, const((1, DV)),
                  const((MIX_W, D_MODEL))],
        out_specs=(pl.BlockSpec((tm, GATES_W), lambda i: (i, 0)), rowt, rowt, const((1, DV)), const((1, CONV_W))),
        compiler_params=_cparams("arbitrary"),
    )(dx2b, o_f, o_b, proj, proj, proj, conv, gla_g, w_out)


def _conv_bwd(dconv, proj, conv_w, tm):
    seq = dconv.shape[0]
    nt = seq // tm

    def body(dc_in, dprev, dnext, cg, hc, cprev, cnext, hprev, hnext, cw, dch_ref, dcw_ref):
        i = pl.program_id(0)

        @pl.when(i == 0)
        def _():
            dcw_ref[...] = jnp.zeros(dcw_ref.shape, F32)

        first, lastt = i > 0, i < nt - 1
        dcv = _f32(dc_in)
        d_up, d_un = _shift_rows(dcv, jnp.where(first, _last_row(dprev), 0.0), jnp.where(lastt, _first_row(dnext), 0.0))
        cgv, hcv = _f32(cg), _f32(hc)
        u = cgv * hcv
        u_up, u_un = _shift_rows(u, jnp.where(first, _last_row(cprev) * _last_row(hprev), 0.0),
                                 jnp.where(lastt, _first_row(cnext) * _first_row(hnext), 0.0))
        du = cw[0:1, :] * d_un + cw[1:2, :] * dcv + cw[2:3, :] * d_up
        dch_ref[:, :CONV_W] = (du * hcv).astype(BF16)
        dch_ref[:, CONV_W:] = (du * cgv).astype(BF16)
        dcw_ref[0:1, :] += jnp.sum(dcv * u_up, axis=0, keepdims=True)
        dcw_ref[1:2, :] += jnp.sum(dcv * u, axis=0, keepdims=True)
        dcw_ref[2:3, :] += jnp.sum(dcv * u_un, axis=0, keepdims=True)

    def col(off):
        return pl.BlockSpec((tm, CONV_W), lambda i: (i, off // CONV_W))

    rowt = pl.BlockSpec((tm, CONV_W), lambda i: (i, 0))
    const = lambda shape: pl.BlockSpec(shape, lambda i: (0, 0))
    return pl.pallas_call(
        body, name="conv_bwd",
        out_shape=(jax.ShapeDtypeStruct((seq, CH_W), BF16), jax.ShapeDtypeStruct((8, CONV_W), F32)),
        grid=(nt,),
        in_specs=[rowt] + _halo_specs(tm, seq, 0) + [col(OFF_C), col(OFF_H)]
        + _halo_specs(tm, seq, OFF_C // CONV_W) + _halo_specs(tm, seq, OFF_H // CONV_W) + [const((8, CONV_W))],
        out_specs=(pl.BlockSpec((tm, CH_W), lambda i: (i, 0)), const((8, CONV_W))),
        compiler_params=_cparams("arbitrary"),
    )(dconv, dconv, dconv, proj, proj, proj, proj, proj, proj, conv_w)


def _gla_bwd(proj, lr, do, st_f, st_b, wgk_f, wgk_b, bgk_f, bgk_b, tt):
    seq = proj.shape[0]
    nb, nc = seq // tt, tt // CHUNK

    def body(qf, kf, vf, lrf, dof, stf, qb, kb, vb, lrb, dob, stb, wf, wb, bf, bb,
             dqkv_f, dlr_f, dqkv_b, dlr_b, dwf, dwb, dbf, dbb,
             ds_scr, eq_s, ek_s, ein_s, eout_s, qs_s, ks_s, qin_s, kout_s, db_s, lg_s):
        @pl.when(pl.program_id(0) == 0)
        def _():
            ds_scr[...] = jnp.zeros(ds_scr.shape, F32)
            for r in (dwf, dwb, dbf, dbb):
                r[...] = jnp.zeros(r.shape, F32)

        low, upp, sup = _block_masks(tt)
        row = lax.broadcasted_iota(jnp.int32, (CHUNK, 1), 0)
        kmask = _chunk_column_mask(tt)
        dirs = ((qf, kf, vf, lrf, dof, stf, wf, bf, dqkv_f, dlr_f, dwf, dbf,
                 low, upp, low, REF_F, LAST_F, list(reversed(range(nc)))),
                (qb, kb, vb, lrb, dob, stb, wb, bb, dqkv_b, dlr_b, dwb, dbb,
                 upp, low, sup, REF_B, LAST_B, list(range(nc))))
        for d, (q_r, k_r, v_r, lr_r, do_r, st_r, w_r, b_r, dqkv_r, dlr_r, dw_r, db_r,
                cum, cum_t, mask, ref, last, order) in enumerate(dirs):
            lrv = lr_r[...].astype(BF16)
            wv = w_r[...]
            logits = _dot(lrv, wv) + b_r[...]
            lg_s[...] = logits
            b = _dot_split3(cum.astype(BF16), _log_gate(logits))
            decs = []
            for c in range(nc):
                rows = slice(c * CHUNK, (c + 1) * CHUNK)
                bc = b[rows]
                b_ref, b_last = bc[ref:ref + 1], bc[last:last + 1]
                qc = q_r[rows, :].astype(F32) * QSCALE
                kc = k_r[rows, :].astype(F32)
                e_q, e_k, e_in, e_out = jnp.exp(bc - b_ref), jnp.exp(b_ref - bc), jnp.exp(bc), jnp.exp(b_last - bc)
                eq_s[rows, :], ek_s[rows, :], ein_s[rows, :], eout_s[rows, :] = e_q, e_k, e_in, e_out
                qs_s[rows, :] = (qc * e_q).astype(BF16)
                ks_s[rows, :] = (kc * e_k).astype(BF16)
                qin_s[rows, :] = (qc * e_in).astype(BF16)
                kout_s[rows, :] = (kc * e_out).astype(BF16)
                decs.append(jnp.exp(b_last))
            for h in range(HEADS):
                ksl = slice(h * DK, (h + 1) * DK)
                vsl = slice(h * DV, (h + 1) * DV)
                v = v_r[:, vsl].astype(BF16)
                dov = do_r[:, vsl].astype(BF16)
                qsb, ksb = qs_s[:, ksl], ks_s[:, ksl]
                att = jnp.where(mask, _dot_nt(qsb, ksb), 0.0).astype(BF16)
                datt = jnp.where(mask, _dot_nt(dov, v), 0.0).astype(BF16)
                dqs = _dot(datt, ksb)
                dks = _dot_tn(datt, qsb)
                dv_intra = _dot_tn(att, dov)
                g_t = _dot_tn(dov, _chunked(kmask, qin_s[:, ksl], nc))
                ds = ds_scr[d * HEADS + h]
                for c in order:
                    rows = slice(c * CHUNK, (c + 1) * CHUNK)
                    dsb = ds.astype(BF16)
                    s_prev = st_r[c, h]
                    dk_out = _dot(v[rows], dsb)
                    dq_in = _dot(dov[rows], s_prev)
                    dv = dv_intra[rows] + _dot_nt(kout_s[rows, ksl], dsb)
                    dqkv_r[rows, OFF_V + h * DV:OFF_V + (h + 1) * DV] = dv.astype(BF16)
                    dec = decs[c][:, ksl]
                    ddec = jnp.sum(ds * s_prev.astype(F32), axis=0, keepdims=True)
                    e_out = eout_s[rows, ksl]
                    qc = q_r[rows, ksl].astype(F32) * QSCALE
                    kc = k_r[rows, ksl].astype(F32)
                    dq = dqs[rows] * eq_s[rows, ksl] + dq_in * ein_s[rows, ksl]
                    dk = dks[rows] * ek_s[rows, ksl] + dk_out * e_out
                    dqkv_r[rows, OFF_Q + h * DK:OFF_Q + (h + 1) * DK] = (dq * QSCALE).astype(BF16)
                    dqkv_r[rows, OFF_K + h * DK:OFF_K + (h + 1) * DK] = dk.astype(BF16)
                    tail = jnp.sum(dk_out * (kc * e_out), axis=0, keepdims=True) + ddec * dec
                    db_s[rows, ksl] = (qc * dq - kc * dk) + jnp.where(row == last, tail, 0.0)
                    ds = ds * dec + g_t[:, c * DK:(c + 1) * DK]
                ds_scr[d * HEADS + h] = ds
            dg = _dot_split3(cum_t.astype(BF16), db_s[...])
            dlogit = (dg * GATE_SCALE) * _sigmoid(-lg_s[...])
            dlb = dlogit.astype(BF16)
            dlr_r[...] = _dot_nt(dlb, wv)
            dw_r[...] += _dot_tn(lrv, dlb)
            db_r[...] += jnp.sum(dlogit, axis=0, keepdims=True)

    fw = lambda i: (nb - 1 - i, 0)
    bw = lambda i: (i, 0)
    const = lambda i: (0, 0)

    def tok_specs(m):
        return [pl.BlockSpec((tt, QK_W), lambda i: (m(i)[0], OFF_Q // QK_W)),
                pl.BlockSpec((tt, QK_W), lambda i: (m(i)[0], OFF_K // QK_W)),
                pl.BlockSpec((tt, V_W), lambda i: (m(i)[0], OFF_V // V_W)),
                pl.BlockSpec((tt, LR_W), m),
                pl.BlockSpec((tt, V_W), m),
                pl.BlockSpec((nc, HEADS, DV, DK), lambda i: (m(i)[0], 0, 0, 0))]

    dqkv = jax.ShapeDtypeStruct((seq, QK_W + QK_W + V_W), BF16)
    dlr = jax.ShapeDtypeStruct((seq, LR_W), F32)
    dw = jax.ShapeDtypeStruct((LR_W, QK_W), F32)
    dbias = jax.ShapeDtypeStruct((1, QK_W), F32)
    return pl.pallas_call(
        body, name="gla_bwd",
        out_shape=(dqkv, dlr, dqkv, dlr, dw, dw, dbias, dbias),
        grid=(nb,),
        in_specs=tok_specs(fw) + tok_specs(bw) + [
            pl.BlockSpec((LR_W, QK_W), const), pl.BlockSpec((LR_W, QK_W), const),
            pl.BlockSpec((1, QK_W), const), pl.BlockSpec((1, QK_W), const)],
        out_specs=(pl.BlockSpec((tt, QK_W + QK_W + V_W), fw), pl.BlockSpec((tt, LR_W), fw),
                   pl.BlockSpec((tt, QK_W + QK_W + V_W), bw), pl.BlockSpec((tt, LR_W), bw),
                   pl.BlockSpec((LR_W, QK_W), const), pl.BlockSpec((LR_W, QK_W), const),
                   pl.BlockSpec((1, QK_W), const), pl.BlockSpec((1, QK_W), const)),
        scratch_shapes=[pltpu.VMEM((2 * HEADS, DV, DK), F32)] + [pltpu.VMEM((tt, QK_W), F32)] * 4
        + [pltpu.VMEM((tt, QK_W), BF16)] * 4 + [pltpu.VMEM((tt, QK_W), F32)] * 2,
        compiler_params=_cparams("arbitrary"),
    )(proj, proj, proj, lr, do, st_f, proj, proj, proj, lr, do, st_b, wgk_f, wgk_b, bgk_f, bgk_b)


def _sum_directions(dqkv_f, dqkv_b, dlr_f, dlr_b, tm):
    seq = dqkv_f.shape[0]

    def body(a, b, la, lb, dp_out, dlr_out):
        dp_out[...] = (_f32(a) + _f32(b)).astype(BF16)
        dlr_out[...] = (la[...] + lb[...]).astype(BF16)

    rowt = pl.BlockSpec((tm, QKV_W), lambda i: (i, 0))
    lrt = pl.BlockSpec((tm, LR_W), lambda i: (i, 0))
    return pl.pallas_call(
        body, name="sum_directions",
        out_shape=(jax.ShapeDtypeStruct((seq, QKV_W), BF16), jax.ShapeDtypeStruct((seq, LR_W), BF16)),
        grid=(seq // tm,),
        in_specs=[rowt, rowt, lrt, lrt],
        out_specs=(rowt, lrt),
        compiler_params=_cparams("arbitrary"),
    )(dqkv_f, dqkv_b, dlr_f, dlr_b)


def _input_grad(dp_qkv, dp_gates, dp_ch, dlr, w_nat, x2d, norm_g, dx2, sums, tm):
    seq = x2d.shape[0]
    nt, n = seq // tm, len(sums)
    relay_step = (5 * nt) // 8

    def body(dq, dg, dc, dl, w, x_ref, g_ref, dx2_ref, *rest):
        ins, (gx_ref, dng_ref), outs = rest[:n], rest[n:n + 2], rest[n + 2:2 * n + 2]
        passing, joined = rest[2 * n + 2:3 * n + 2], rest[3 * n + 2:4 * n + 2]
        send_sems, recv_sems, local_sems = rest[4 * n + 2:]
        i = pl.program_id(0)
        c = lax.axis_index("c")
        first, second, diagonal = _route_chips()
        slot = lambda chip: 2 * chip[0] + chip[1]

        def remote(a, k, src, dst, to):
            return pltpu.make_async_remote_copy(src_ref=src, dst_ref=dst, send_sem=send_sems.at[3 * a + k],
                                                recv_sem=recv_sems.at[3 * a + k], device_id=(*to, c),
                                                device_id_type=MESH)

        direct = lambda a: remote(a, 0, ins[a].at[slot(first)], outs[a].at[0], first)
        for_second = lambda a: remote(a, 1, ins[a].at[slot(diagonal)], passing[a], first)
        joint = lambda a: remote(a, 2, joined[a], outs[a].at[1], second)
        own = lambda a: pltpu.make_async_copy(ins[a].at[slot(second)], joined[a], local_sems.at[a])

        @pl.when(i == 0)
        def _():
            for a in range(n):
                _start_all([direct(a), for_second(a), own(a)])
            dng_ref[...] = jnp.zeros(dng_ref.shape, F32)

        @pl.when(i == relay_step)
        def _():
            for a in range(n):
                for_second(a).wait_recv()
                own(a).wait()
                joined[a][...] = (joined[a][...].astype(F32) + passing[a][...].astype(F32)).astype(BF16)
                joint(a).start()

        dh = (_dot(dl[...], w[NAT_LR:NAT_LR + LR_W, :]) + _dot(dq[...], w[0:NAT_ZA, :])
              + _dot(dg[:, 0:CONV_W], w[NAT_ZA:NAT_LR, :]) + _dot(dg[:, CONV_W:2 * CONV_W], w[NAT_B:NAT_C, :])
              + _dot(dg[:, 2 * CONV_W:], w[NAT_ZC:IN_W, :]) + _dot(dc[...], w[NAT_C:NAT_ZC, :]))
        xv = x_ref[...]
        r = lax.rsqrt(jnp.mean(xv * xv, axis=-1, keepdims=True) + EPS)
        xn = xv * r
        dng_ref[...] += jnp.sum(dh * xn, axis=0, keepdims=True)
        dn = dh * g_ref[...]
        gx_ref[...] = (r * dn - xn * (r * jnp.mean(dn * xn, axis=-1, keepdims=True))) + dx2_ref[...]

        @pl.when(i == nt - 1)
        def _():
            for a in range(n):
                direct(a).wait_recv()
                joint(a).wait_recv()
            for a in range(n):
                for cp in (direct(a), for_second(a), joint(a)):
                    cp.wait_send()

    rowt = pl.BlockSpec((tm, D_MODEL), lambda i: (i, 0))
    seg = lambda width: pl.BlockSpec((tm, width), lambda i: (i, 0))
    resident = lambda rows: pl.BlockSpec((rows, D_MODEL), lambda i: (0, 0), pipeline_mode=pl.Buffered(1))
    hbm = pl.BlockSpec(memory_space=pl.ANY)
    blocks = [pltpu.VMEM(s.shape[1:], s.dtype) for s in sums]
    return pl.pallas_call(
        body, name="input_grad",
        out_shape=(jax.ShapeDtypeStruct((seq, D_MODEL), F32), jax.ShapeDtypeStruct((1, D_MODEL), F32))
        + tuple(jax.ShapeDtypeStruct((2,) + s.shape[1:], s.dtype) for s in sums),
        grid=(nt,),
        in_specs=[seg(QKV_W), seg(GATES_W), seg(CH_W), seg(LR_W), resident(IN_W),
                  rowt, pl.BlockSpec((1, D_MODEL), lambda i: (0, 0)), rowt] + [hbm] * n,
        out_specs=(rowt, pl.BlockSpec((1, D_MODEL), lambda i: (0, 0))) + (hbm,) * n,
        scratch_shapes=blocks + blocks + [pltpu.SemaphoreType.DMA((3 * n,)), pltpu.SemaphoreType.DMA((3 * n,)),
                                          pltpu.SemaphoreType.DMA((n,))],
        compiler_params=_cparams("arbitrary"),
    )(dp_qkv, dp_gates, dp_ch, dlr, w_nat, x2d, norm_g, dx2, *sums)


def _weight_grad_out(y_t, dx2b, tk, riding):
    m, seq = y_t.shape
    n = dx2b.shape[1]
    nk = seq // tk

    def body(a_ref, b_ref, ride_in, o_ref, ride_out, send_sems, recv_sems):
        k = pl.program_id(0)

        @pl.when(k == 0)
        def _():
            _start_all(_sibling_copies(ride_in, ride_out, send_sems, recv_sems))
            o_ref[...] = jnp.zeros(o_ref.shape, F32)

        o_ref[...] += _dot(a_ref[...], b_ref[...])

        @pl.when(k == nk - 1)
        def _():
            _wait_all(_sibling_copies(ride_in, ride_out, send_sems, recv_sems))

    hbm = pl.BlockSpec(memory_space=pl.ANY)
    return pl.pallas_call(
        body, name="wgrad_out",
        out_shape=(jax.ShapeDtypeStruct((m, n), F32), jax.ShapeDtypeStruct((4,) + _block_shape(riding), F32)),
        grid=(nk,),
        in_specs=[pl.BlockSpec((m, tk), lambda k: (0, k)), pl.BlockSpec((tk, n), lambda k: (k, 0)), hbm],
        out_specs=(pl.BlockSpec((m, n), lambda k: (0, 0)), hbm),
        scratch_shapes=[pltpu.SemaphoreType.DMA((4,)), pltpu.SemaphoreType.DMA((4,))],
        compiler_params=_cparams("arbitrary"),
    )(y_t, dx2b, riding)


def _weight_grad_in(h_t, dp_qkv, dp_gates, dp_ch, dlr):
    m, seq = h_t.shape
    tn = 512
    n_qkv, n_gates, n_ch = QKV_W // tn, GATES_W // tn, CH_W // tn
    starts = ([k * tn for k in range(n_qkv)] + [NAT_ZA, NAT_ZA + tn, NAT_B, NAT_B + tn, NAT_ZC, NAT_ZC + tn]
              + [NAT_C + k * tn for k in range(n_ch)])

    def out_row(j):
        row = 0
        for k, start in enumerate(starts):
            row = row + jnp.where(j == k, start // 32, 0)
        return pl.multiple_of(row * 32, 32), 0

    def body(a_ref, bq, bg, bc, o_ref, acc):
        j = pl.program_id(0)

        @pl.when(j < n_qkv)
        def _():
            acc[...] = _dot(a_ref[...], bq[...])

        @pl.when(jnp.logical_and(j >= n_qkv, j < n_qkv + n_gates))
        def _():
            acc[...] = _dot(a_ref[...], bg[...])

        @pl.when(j >= n_qkv + n_gates)
        def _():
            acc[...] = _dot(a_ref[...], bc[...])

        o_ref[...] = acc[...].T

    resident = pl.BlockSpec((m, seq), lambda j: (0, 0), pipeline_mode=pl.Buffered(1))
    seg = lambda first, count: pl.BlockSpec((seq, tn), lambda j: (0, jnp.clip(j - first, 0, count - 1)))
    main = pl.pallas_call(
        body, name="wgrad_in",
        out_shape=jax.ShapeDtypeStruct((IN_W, m), F32),
        grid=(n_qkv + n_gates + n_ch,),
        in_specs=[resident, seg(0, n_qkv), seg(n_qkv, n_gates), seg(n_qkv + n_gates, n_ch)],
        out_specs=pl.BlockSpec((pl.Element(tn), pl.Element(m)), out_row),
        scratch_shapes=[pltpu.VMEM((m, tn), F32)],
        compiler_params=_cparams("arbitrary"),
    )(h_t, dp_qkv, dp_gates, dp_ch)

    def lr_body(a_ref, b_ref, full_ref, o_ref, acc):
        acc[...] = _dot(a_ref[...], b_ref[...])
        o_ref[...] = acc[...].T[0:2 * RANK, :]

    whole = lambda shape: pl.BlockSpec(shape, lambda j: (0, 0))
    return pl.pallas_call(
        lr_body, name="wgrad_lr",
        out_shape=jax.ShapeDtypeStruct((IN_W, m), F32),
        grid=(1,),
        in_specs=[whole((m, seq)), whole((seq, LR_W)), pl.BlockSpec(memory_space=pl.ANY)],
        out_specs=pl.BlockSpec((pl.Element(2 * RANK), pl.Element(m)), lambda j: (NAT_LR, 0)),
        scratch_shapes=[pltpu.VMEM((m, LR_W), F32)],
        input_output_aliases={2: 0},
        compiler_params=_cparams("arbitrary"),
    )(h_t, dlr, main)


def _pad_rows(a, rows):
    return jnp.pad(a, ((0, rows - a.shape[0]), (0, 0)))


def _rows128(a):
    a = a.reshape(-1, 128)
    return _pad_rows(a, -(-a.shape[0] // 8) * 8)


def _pack(arrs):
    return jnp.concatenate([_rows128(a) for a in arrs], axis=0)


def _unpack(buf, like):
    out, start = [], 0
    for a in like:
        rows = a.size // 128
        out.append(buf[start:start + rows].reshape(a.shape))
        start += -(-rows // 8) * 8
    return out


def kernel(x, norm_g, w_in, w_gk_f, b_gk_f, w_gk_b, b_gk_b, gla_norm_g, conv_w, conv_b, w_out, final_g, loss_target, m_norm_g, m_w_in, m_w_gk_f, m_b_gk_f, m_w_gk_b, m_b_gk_b, m_gla_norm_g, m_conv_w, m_conv_b, m_w_out, m_final_g, v_norm_g, v_w_in, v_w_gk_f, v_b_gk_f, v_w_gk_b, v_b_gk_b, v_gla_norm_g, v_conv_w, v_conv_b, v_w_out, v_final_g):
    px, py, pc = _position()
    me = _blk(px, py, pc)
    seq = x.shape[1]
    x2d, tgt = x[0], loss_target[0]
    tm = min(512, seq)
    tt = min(256, seq)

    small_s = jnp.concatenate([jnp.concatenate([w_gk_f[0], w_gk_b[0]], axis=1), _pad_rows(conv_w[0], 8)], axis=0)
    shifted = lax.dynamic_update_slice(jnp.zeros((SHIFTED_ROWS, D_MODEL), F32), w_in[0].T, (4 * (me % 4), 0))
    order = sum(jnp.where(2 * px + py == k, jnp.asarray(tiles + (0,), jnp.int32), 0) for k, tiles in enumerate(TILE_ORDER))
    proj, lr, h_t, w_nat, wout_all, small_all = _gather_inproj(x2d, norm_g, shifted, w_out[0], small_s, order,
                                                               min(1024, seq))
    w_out_full = wout_all.reshape(MIX_W, D_MODEL)
    wgk_cols = 512 // N_DEV
    wgk_f_full = small_all[:, 0:RANK, 0:wgk_cols].transpose(1, 0, 2).reshape(RANK, QK_W)
    wgk_b_full = small_all[:, 0:RANK, wgk_cols:2 * wgk_cols].transpose(1, 0, 2).reshape(RANK, QK_W)
    conv_w_full = _pad_rows(small_all[:, RANK:RANK + 3, :].transpose(1, 0, 2).reshape(3, CONV_W), 8)
    zr = lambda n: jnp.zeros((n, QK_W), F32)
    wgk_f_pad = jnp.concatenate([wgk_f_full, zr(LR_W - RANK)], axis=0).astype(BF16)
    wgk_b_pad = jnp.concatenate([zr(RANK), wgk_b_full, zr(LR_W - 2 * RANK)], axis=0).astype(BF16)

    o_f, o_b, st_f, st_b = _gla_fwd(proj, lr, wgk_f_pad, wgk_b_pad, b_gk_f, b_gk_b, tt)
    tmix = min(256, seq)
    y_t, conv, dx2, dx2b, loss_p, dfg_p = _mix_out_loss(o_f, o_b, proj, x2d, tgt, gla_norm_g, conv_w_full, conv_b,
                                                        w_out_full, final_g.reshape(1, D_MODEL), tmix)

    dp_gates, do, dconv, dgg_p, dcb_p = _mix_bwd(dx2b, o_f, o_b, proj, conv, gla_norm_g, w_out_full, tmix)
    dp_ch, dcw_p = _conv_bwd(dconv, proj, conv_w_full, tmix)
    dqkv_f, dlr_f, dqkv_b, dlr_b, dwf_p, dwb_p, dbf_p, dbb_p = _gla_bwd(
        proj, lr, do, st_f, st_b, wgk_f_pad, wgk_b_pad, b_gk_f, b_gk_b, tt)
    dp_qkv, dlr = _sum_directions(dqkv_f, dqkv_b, dlr_f, dlr_b, tm)
    dw_nat = _weight_grad_in(h_t, dp_qkv, dp_gates, dp_ch, dlr)

    dw_out, sib_in = _weight_grad_out(y_t, dx2b, tm, dw_nat)
    part_out = dw_out.reshape(N_DEV, MIX_W // N_DEV, D_MODEL)
    core = jnp.reshape(pc, (1,)).astype(jnp.int32)
    chip = jnp.reshape(2 * px + py, (1,)).astype(jnp.int32)
    sums_in, sib_out = _chip_sums(dw_nat, sib_in, core, 256, "chip_sums_in", riding=part_out)
    sums_out = _chip_sums(part_out, sib_out, core, 256, "chip_sums_out")
    grad_x2d, dng_p, far_in, far_out = _input_grad(dp_qkv, dp_gates, dp_ch, dlr, w_nat, x2d, norm_g, dx2,
                                                   [sums_in, sums_out], tmix)
    pieces = [dng_p, dbf_p, dbb_p, dgg_p, dcb_p, dfg_p[0], dwf_p[0:RANK], dwb_p[RANK:2 * RANK], dcw_p[0:3], loss_p[0]]
    g_window, small_tot = _final_sum(sums_in, far_in, chip, _pack(pieces), 256, "final_sum_in")
    g_in_t = lax.dynamic_slice_in_dim(g_window, 4 * pc, SHARD_W, axis=0)
    g_w_out, d_w_out, nm_w_out, nv_w_out = _final_sum_adamw(sums_out, far_out, chip, w_out[0], m_w_out[0], v_w_out[0],
                                                            256, "adamw_out")
    flat = lambda a: a[0].T.reshape(SHARD_W, D_MODEL // 128, 128)
    unflat = lambda a: a.reshape(SHARD_W, D_MODEL).T
    d_flat, m_flat, v_flat = _adamw_rows(g_in_t.reshape(SHARD_W, D_MODEL // 128, 128), flat(w_in), flat(m_w_in),
                                         flat(v_w_in), 90, "adamw_in")
    g_w_in, d_w_in, nm_w_in, nv_w_in = g_in_t.T, unflat(d_flat), unflat(m_flat), unflat(v_flat)

    tot = _unpack(small_tot, pieces)
    g_norm_g, g_b_gk_f, g_b_gk_b, g_gla, g_conv_b, g_final = tot[:6]
    g_wgk_f = lax.dynamic_slice_in_dim(tot[6], me * wgk_cols, wgk_cols, axis=1)[None]
    g_wgk_b = lax.dynamic_slice_in_dim(tot[7], me * wgk_cols, wgk_cols, axis=1)[None]
    g_conv_w = lax.dynamic_slice_in_dim(tot[8], me * 128, 128, axis=1)[None]
    loss = tot[9][0]

    small_g = [g_norm_g, g_b_gk_f, g_b_gk_b, g_gla, g_conv_b, g_final, g_wgk_f, g_wgk_b, g_conv_w]
    small_w = [norm_g, b_gk_f, b_gk_b, gla_norm_g, conv_b, final_g, w_gk_f, w_gk_b, conv_w]
    small_m = [m_norm_g, m_b_gk_f, m_b_gk_b, m_gla_norm_g, m_conv_b, m_final_g, m_w_gk_f, m_w_gk_b, m_conv_w]
    small_v = [v_norm_g, v_b_gk_f, v_b_gk_b, v_gla_norm_g, v_conv_b, v_final_g, v_w_gk_f, v_w_gk_b, v_conv_w]
    d_s, m_s, v_s = _adamw_small(_pack(small_g), _pack(small_w), _pack(small_m), _pack(small_v))
    d_l, m_l, v_l = _unpack(d_s, small_w), _unpack(m_s, small_w), _unpack(v_s, small_w)

    def ordered(sm, big_in, big_out):
        return [sm[0], big_in[None], sm[6], sm[1], sm[7], sm[2], sm[3], sm[8], sm[4], big_out[None], sm[5]]

    grads = ordered(small_g, g_w_in, g_w_out)
    deltas = ordered(d_l, d_w_in, d_w_out)
    new_m = ordered(m_l, nm_w_in, nm_w_out)
    new_v = ordered(v_l, nv_w_in, nv_w_out)
    return (loss, grad_x2d[None], *grads, *deltas, *new_m, *new_v)
```

```python
import jax
import jax.numpy as jnp
from jax import lax
from jax.experimental import pallas as pl
from jax.experimental.pallas import tpu as pltpu

F32 = jnp.float32
BF16 = jnp.bfloat16
MESH = pl.DeviceIdType.MESH

N_DEV = 8
D_MODEL = 1024
HEADS = 4
DK = 128
DV = 256
QK_W = HEADS * DK
V_W = HEADS * DV
CONV_W = 1024
MIX_W = V_W + CONV_W
CHUNK = 64
RANK = 16
IN_W = 7200
SHARD_W = IN_W // N_DEV
MAIN_W = 7168
LR_W = 128
OFF_Q, OFF_K, OFF_V, OFF_ZA, OFF_B, OFF_ZC, OFF_C, OFF_H = 0, 512, 1024, 2048, 3072, 4096, 5120, 6144
QKV_W, GATES_W, CH_W = 2048, 3072, 2048
NAT_ZA, NAT_LR, NAT_B, NAT_C, NAT_ZC = 2048, 3072, 3104, 4128, 6176
EPS = 1e-6
GATE_SCALE = 1.0 / 16.0
QSCALE = DK ** -0.5
REF_F, LAST_F = CHUNK // 2, CHUNK - 1
REF_B, LAST_B = CHUNK - 1 - CHUNK // 2, 0

ADAM_LR = 0.001
ADAM_B1 = 0.9
ADAM_B2 = 0.999
ADAM_EPS = 1e-08
ADAM_WD = 0.01
ADAM_STEP = 10

VMEM_LIMIT = 56 * 1024 * 1024


def _cparams(*sem):
    return pltpu.CompilerParams(dimension_semantics=sem, vmem_limit_bytes=VMEM_LIMIT)


def _dot(a, b):
    return jnp.dot(a, b, preferred_element_type=F32)


def _dot_nt(a, b):
    return lax.dot_general(a, b, (((1,), (1,)), ((), ())), preferred_element_type=F32)


def _dot_tn(a, b):
    return lax.dot_general(a, b, (((0,), (0,)), ((), ())), preferred_element_type=F32)


def _sigmoid(z):
    return jax.nn.sigmoid(z)


def _position():
    return lax.axis_index("x"), lax.axis_index("y"), lax.axis_index("c")


def _blk(px, py, pc):
    return 4 * px + 2 * py + pc


EDGE = 16
SHIFTED_ROWS = 912
BODY_ROWS = SHIFTED_ROWS - 2 * EDGE


def _first_tile_row(blk, px):
    return EDGE * (56 * blk + px)


def _edge_tiles():
    tiles = {}
    for blk in range(N_DEV):
        first = _first_tile_row(blk, blk // 4)
        tiles.setdefault(first, []).append((blk, 0))
        tiles.setdefault(first + EDGE + BODY_ROWS, []).append((blk, 1))
    return tiles


def _peer_copies(srcs, outs, send_sems, recv_sems):
    x, y, c = _position()
    me = _blk(x, y, c)
    copies = []
    for a, (src, out) in enumerate(zip(srcs, outs)):
        k = 0
        for dx in (0, 1):
            for dy in (0, 1):
                for dc in (0, 1):
                    if dx + dy + dc == 0:
                        continue
                    peer = (1 - x if dx else x, 1 - y if dy else y, 1 - c if dc else c)
                    copies.append(pltpu.make_async_remote_copy(
                        src_ref=src, dst_ref=out.at[me], send_sem=send_sems.at[a * 7 + k],
                        recv_sem=recv_sems.at[a * 7 + k], device_id=peer, device_id_type=MESH))
                    k += 1
    return copies


def _route_chips():
    x, y, c = _position()
    along_x = c == 0
    return [(jnp.where(along_x, 1 - x, x), jnp.where(along_x, y, 1 - y)),
            (jnp.where(along_x, x, 1 - x), jnp.where(along_x, 1 - y, y)), (1 - x, 1 - y)]


WINDOW_ROWS = SHARD_W + 4


def _window_start(k, parity):
    return 2 * SHARD_W * k + (SHARD_W - 4) * parity


def _owner_block(part, k, parity):
    if part.ndim == 3:
        return part.at[2 * k + parity]
    return part.at[pl.ds(pl.multiple_of(_window_start(k, parity), 8), WINDOW_ROWS)]


def _block_shape(part):
    return part.shape[1:] if part.ndim == 3 else (WINDOW_ROWS, part.shape[1])


def _sibling_copies(part, out, send_sems, recv_sems):
    x, y, c = _position()
    return [pltpu.make_async_remote_copy(src_ref=_owner_block(part, k, 1 - c), dst_ref=out.at[k],
                                         send_sem=send_sems.at[k], recv_sem=recv_sems.at[k],
                                         device_id=(x, y, 1 - c), device_id_type=MESH)
            for k in range(4)]


def _start_all(copies):
    for cp in copies:
        cp.start()


def _wait_all(copies):
    for cp in copies:
        cp.wait_recv()
    for cp in copies:
        cp.wait_send()


def _chip_sums(part, from_sibling, core, tc, name, riding=None):
    rows, cols = _block_shape(part)
    nj = cols // tc

    def body(core_ref, p_ref, s_ref, *rest):
        if riding is None:
            (o_ref,) = rest
        else:
            ride_in, o_ref, ride_out, send_sems, recv_sems = rest
            k, j = pl.program_id(0), pl.program_id(1)

            @pl.when(jnp.logical_and(k == 0, j == 0))
            def _():
                _start_all(_sibling_copies(ride_in, ride_out, send_sems, recv_sems))

        o_ref[0] = (p_ref[...].reshape(rows, tc) + s_ref[0]).astype(BF16)

        if riding is not None:
            @pl.when(jnp.logical_and(k == 3, j == nj - 1))
            def _():
                _wait_all(_sibling_copies(ride_in, ride_out, send_sems, recv_sems))

    hbm = pl.BlockSpec(memory_space=pl.ANY)
    sums = jax.ShapeDtypeStruct((4, rows, cols), BF16)
    tile_out = pl.BlockSpec((1, rows, tc), lambda k, j, core_ref: (k, 0, j))
    if part.ndim == 3:
        mine = pl.BlockSpec((1, rows, tc), lambda k, j, core_ref: (2 * k + core_ref[0], 0, j))
    else:
        mine = pl.BlockSpec((pl.Element(rows), pl.Element(tc)),
                            lambda k, j, core_ref: (pl.multiple_of(_window_start(k, core_ref[0]), 8),
                                                    pl.multiple_of(j * tc, 128)))
    in_specs = [mine, pl.BlockSpec((1, rows, tc), lambda k, j, core_ref: (k, 0, j))]
    if riding is None:
        out_shape, out_specs, scratch, args = sums, tile_out, [], (core, part, from_sibling)
    else:
        out_shape = (sums, jax.ShapeDtypeStruct((4,) + _block_shape(riding), F32))
        out_specs, in_specs = (tile_out, hbm), in_specs + [hbm]
        scratch = [pltpu.SemaphoreType.DMA((4,)), pltpu.SemaphoreType.DMA((4,))]
        args = (core, part, from_sibling, riding)
    return pl.pallas_call(
        body, name=name, out_shape=out_shape,
        grid_spec=pltpu.PrefetchScalarGridSpec(num_scalar_prefetch=1, grid=(4, nj), in_specs=in_specs,
                                               out_specs=out_specs, scratch_shapes=scratch),
        compiler_params=_cparams("arbitrary", "arbitrary"),
    )(*args)


def _sum_chips(s_ref, r_ref):
    f = lambda a: a.astype(F32)
    return (f(s_ref[0]) + f(r_ref[0])) + f(r_ref[1])


def _final_sum(sums, from_chips, chip, small, tc, name):
    _, rows, cols = sums.shape
    nj = cols // tc

    def body(chip_ref, s_ref, r_ref, sm_ref, g_out, tot_ref, all_ref, send_sems, recv_sems):
        j = pl.program_id(0)
        me = _blk(*_position())

        @pl.when(j == 0)
        def _():
            all_ref[me] = sm_ref[...]
            _start_all(_peer_copies((all_ref.at[me],), (all_ref,), send_sems, recv_sems))

        g_out[...] = _sum_chips(s_ref, r_ref)

        @pl.when(j == nj - 1)
        def _():
            _wait_all(_peer_copies((all_ref.at[me],), (all_ref,), send_sems, recv_sems))
            acc = all_ref[0]
            for d in range(1, N_DEV):
                acc = acc + all_ref[d]
            tot_ref[...] = acc

    whole = pl.BlockSpec(small.shape, lambda j, chip_ref: (0, 0))
    return pl.pallas_call(
        body, name=name,
        out_shape=(jax.ShapeDtypeStruct((rows, cols), F32), jax.ShapeDtypeStruct(small.shape, F32)),
        grid_spec=pltpu.PrefetchScalarGridSpec(
            num_scalar_prefetch=1, grid=(nj,),
            in_specs=[pl.BlockSpec((1, rows, tc), lambda j, chip_ref: (chip_ref[0], 0, j)),
                      pl.BlockSpec((2, rows, tc), lambda j, chip_ref: (0, 0, j)), whole],
            out_specs=(pl.BlockSpec((rows, tc), lambda j, chip_ref: (0, j)), whole),
            scratch_shapes=[pltpu.VMEM((N_DEV,) + small.shape, F32), pltpu.SemaphoreType.DMA((7,)),
                            pltpu.SemaphoreType.DMA((7,))]),
        compiler_params=_cparams("arbitrary"),
    )(chip, sums, from_chips, small)


def _adamw_rows(g, w, m, v, tr, name):
    rows = g.shape[0]

    def body(g_ref, w_ref, m_ref, v_ref, d_out, m_out, v_out):
        delta, m_new, v_new = _adamw(w_ref[...], g_ref[...], m_ref[...], v_ref[...])
        d_out[...] = delta
        m_out[...] = m_new
        v_out[...] = v_new

    tile = pl.BlockSpec((tr,) + g.shape[1:], lambda r: (r, 0, 0))
    shp = jax.ShapeDtypeStruct(g.shape, F32)
    return pl.pallas_call(
        body, name=name, out_shape=(shp, shp, shp), grid=(rows // tr,),
        in_specs=[tile] * 4, out_specs=(tile, tile, tile),
        compiler_params=_cparams("arbitrary"),
    )(g, w, m, v)


def _adamw(w, g, m, v):
    m = ADAM_B1 * m + (1.0 - ADAM_B1) * g
    v = ADAM_B2 * v + (1.0 - ADAM_B2) * (g * g)
    m_hat = m / (1.0 - ADAM_B1 ** ADAM_STEP)
    v_hat = v / (1.0 - ADAM_B2 ** ADAM_STEP)
    delta = -ADAM_LR * (m_hat / (jnp.sqrt(v_hat) + ADAM_EPS) + ADAM_WD * w)
    return delta, m, v


def _final_sum_adamw(sums, from_chips, chip, w, m, v, tr, name):
    rows, cols = w.shape

    def body(chip_ref, s_ref, r_ref, w_ref, m_ref, v_ref, g_out, d_out, m_out, v_out):
        g = _sum_chips(s_ref, r_ref)
        delta, m_new, v_new = _adamw(w_ref[...], g, m_ref[...], v_ref[...])
        g_out[...] = g
        d_out[...] = delta
        m_out[...] = m_new
        v_out[...] = v_new

    tile = pl.BlockSpec((tr, cols), lambda r, chip_ref: (r, 0))
    shp = jax.ShapeDtypeStruct((rows, cols), F32)
    return pl.pallas_call(
        body, name=name,
        out_shape=(shp, shp, shp, shp),
        grid_spec=pltpu.PrefetchScalarGridSpec(
            num_scalar_prefetch=1, grid=(rows // tr,),
            in_specs=[pl.BlockSpec((1, tr, cols), lambda r, chip_ref: (chip_ref[0], r, 0)),
                      pl.BlockSpec((2, tr, cols), lambda r, chip_ref: (0, r, 0)),
                      tile, tile, tile],
            out_specs=(tile, tile, tile, tile)),
        compiler_params=_cparams("arbitrary"),
    )(chip, sums, from_chips, w, m, v)


def _adamw_small(g, w, m, v):
    def body(g_ref, w_ref, m_ref, v_ref, d_out, m_out, v_out):
        delta, m_new, v_new = _adamw(w_ref[...], g_ref[...], m_ref[...], v_ref[...])
        d_out[...] = delta
        m_out[...] = m_new
        v_out[...] = v_new

    vmem = pl.BlockSpec(memory_space=pltpu.VMEM)
    shp = jax.ShapeDtypeStruct(g.shape, F32)
    return pl.pallas_call(body, name="adamw_small", out_shape=(shp, shp, shp),
                          in_specs=[vmem] * 4, out_specs=(vmem, vmem, vmem))(g, w, m, v)


TILE_ROWS = (0, 1024, NAT_ZA, NAT_B, NAT_ZC, NAT_C, NAT_C + CONV_W)


TILE_ORDER = ((0, 1, 2, 3, 5, 6, 4), (2, 0, 1, 4, 3, 5, 6), (5, 0, 6, 4, 1, 2, 3), (4, 2, 3, 5, 6, 0, 1))
NEIGHBOUR_SWEEP, DIAGONAL_SWEEP = 1, 4
PIECES, W_IN_PIECES, OTHER_PIECES = 4, (0, 1), (2, 3)


def _gather_inproj(x2d, norm_g, shifted, w_out_s, small_s, order, tm):
    seq = x2d.shape[0]
    tn = CONV_W
    ni, nj = seq // tm, MAIN_W // tn
    first_sweep = lambda j, i, order_ref: jnp.where(j == 0, i, ni - 1)
    last_sweep = lambda j, i, order_ref: jnp.where(j == nj - 1, i, 0)
    edge_tiles = _edge_tiles()

    def body(order_ref, x_ref, g_ref, sh_ref, wout_ref, sm_ref, proj_ref, lr_ref, ht_ref, w_nat, wout_all, sm_all,
             w_all, h_all, edges, wout_b, sm_b, send_sems, recv_sems, local_sems):
        j, i = pl.program_id(0), pl.program_id(1)
        rows = pl.ds(pl.multiple_of(i * tm, tm), tm)
        x, y, c = _position()
        me, here, sibling = _blk(x, y, c), (x, y, c), (x, y, 1 - c)
        chips = _route_chips()
        sibling_chips = [chips[1], chips[0], chips[2]]

        def pieces(px, py, pc):
            blk = _blk(px, py, pc)
            body_rows = pl.ds(pl.multiple_of(_first_tile_row(blk, px) + EDGE, EDGE), BODY_ROWS)
            return [w_all.at[body_rows], edges.at[blk], wout_all.at[blk], sm_all.at[blk]]

        def copy(a, k, block, to, staged=None):
            ref = pieces(*block)[a]
            return pltpu.make_async_remote_copy(src_ref=ref if staged is None else staged, dst_ref=ref,
                                                send_sem=send_sems.at[a * 7 + k], recv_sem=recv_sems.at[a * 7 + k],
                                                device_id=to, device_id_type=MESH)

        def own_copies(group):
            targets = [(0, sibling)] + [(1 + n, (*chips[n], c)) for n in range(2)]
            staged = [None, None, wout_b, sm_b]
            return [copy(a, k, here, to, staged[a]) for k, to in targets for a in group]

        def relays(group):
            return [copy(a, 3, (*chips[0], c), (*chips[1], c)) for a in group]

        def forwards(n, group):
            return [copy(a, 4 + n, (*chips[n], c), sibling) for a in group]

        def keep_own():
            return [pltpu.make_async_copy(wout_b, wout_all.at[me], local_sems.at[0]),
                    pltpu.make_async_copy(sm_b, sm_all.at[me], local_sems.at[1])]

        def arrive(ns, group):
            for n in ns:
                for a in group:
                    copy(a, 1 + n, (*chips[n], c), here).wait_recv()
                _start_all((relays(group) if n == 0 else []) + forwards(n, group))
            for n in ns:
                for a in group:
                    copy(a, 4 + n, (*sibling_chips[n], 1 - c), here).wait_recv()

        def add_edge_tiles(stage):
            for row, parts in edge_tiles.items():
                ready = 0
                for blk, _ in parts:
                    away = (x != blk // 4).astype(jnp.int32) + (y != (blk // 2) % 2).astype(jnp.int32)
                    ready = jnp.maximum(ready, away)

                @pl.when(ready == stage)
                def _(row=row, parts=parts):
                    tile = edges[parts[0][0], parts[0][1]].astype(F32)
                    for blk, side in parts[1:]:
                        tile = tile + edges[blk, side].astype(F32)
                    w_all[row:row + EDGE, :] = tile.astype(BF16)

        @pl.when(jnp.logical_and(j == 0, i == 0))
        def _():
            pieces(*here)[0][...] = sh_ref[EDGE:EDGE + BODY_ROWS, :].astype(BF16)
            edges[me, 0] = sh_ref[0:EDGE, :].astype(BF16)
            edges[me, 1] = sh_ref[EDGE + BODY_ROWS:, :].astype(BF16)
            _start_all(own_copies(W_IN_PIECES))
            wout_b[...] = wout_ref[...].astype(BF16)
            sm_b[...] = sm_ref[...]
            _start_all(own_copies(OTHER_PIECES) + keep_own())
            for a in W_IN_PIECES:
                copy(a, 0, sibling, here).wait_recv()
            add_edge_tiles(0)

        @pl.when(jnp.logical_and(j == NEIGHBOUR_SWEEP, i == 0))
        def _():
            arrive((0, 1), W_IN_PIECES)
            add_edge_tiles(1)

        @pl.when(jnp.logical_and(j == DIAGONAL_SWEEP, i == 0))
        def _():
            arrive((2,), W_IN_PIECES)
            add_edge_tiles(2)
            arrive((0, 1), OTHER_PIECES)

        @pl.when(jnp.logical_and(j == nj - 1, i == 0))
        def _():
            arrive((2,), OTHER_PIECES)

        @pl.when(j == 0)
        def _():
            xv = x_ref[...]
            r = lax.rsqrt(jnp.mean(xv * xv, axis=-1, keepdims=True) + EPS)
            h = (xv * r) * g_ref[...]
            h_all[rows, :] = h.astype(BF16)
            ht_ref[...] = h.T.astype(BF16)

        tile = order_ref[j]
        row = 0
        for k, start in enumerate(TILE_ROWS):
            row = row + jnp.where(tile == k, start // 32, 0)
        w_tile = w_all[pl.ds(pl.multiple_of(row * 32, 32), tn), :]
        proj_ref[...] = _dot_nt(h_all[rows, :], w_tile).astype(BF16)

        @pl.when(j == nj - 1)
        def _():
            lr_ref[...] = _dot_nt(h_all[rows, :], w_all[NAT_LR:NAT_LR + LR_W, :])

        @pl.when(jnp.logical_and(j == nj - 1, i == ni - 1))
        def _():
            everything = range(PIECES)
            passed_on = [cp for n in range(3) for cp in forwards(n, everything)]
            for cp in own_copies(everything) + relays(everything) + passed_on:
                cp.wait_send()
            for a in OTHER_PIECES:
                copy(a, 0, sibling, here).wait_recv()
            for cp in keep_own():
                cp.wait()
            keep = pltpu.make_async_copy(w_all, w_nat, local_sems.at[2])
            keep.start()
            keep.wait()

    const = lambda shape: pl.BlockSpec(shape, lambda j, i, order_ref: (0,) * len(shape))
    hbm = pl.BlockSpec(memory_space=pl.ANY)
    vmem = pl.BlockSpec(memory_space=pltpu.VMEM)
    return pl.pallas_call(
        body, name="gather_inproj",
        out_shape=(jax.ShapeDtypeStruct((seq, MAIN_W), BF16), jax.ShapeDtypeStruct((seq, LR_W), F32),
                   jax.ShapeDtypeStruct((D_MODEL, seq), BF16), jax.ShapeDtypeStruct((IN_W, D_MODEL), BF16),
                   jax.ShapeDtypeStruct((N_DEV,) + w_out_s.shape, BF16),
                   jax.ShapeDtypeStruct((N_DEV,) + small_s.shape, F32)),
        grid_spec=pltpu.PrefetchScalarGridSpec(
            num_scalar_prefetch=1, grid=(nj, ni),
            in_specs=[pl.BlockSpec((tm, D_MODEL), lambda j, i, order_ref: (first_sweep(j, i, order_ref), 0)),
                      const((1, D_MODEL)), vmem, const(w_out_s.shape), const(small_s.shape)],
            out_specs=(pl.BlockSpec((tm, tn), lambda j, i, order_ref: (i, order_ref[j])),
                       pl.BlockSpec((tm, LR_W), lambda j, i, order_ref: (last_sweep(j, i, order_ref), 0)),
                       pl.BlockSpec((D_MODEL, tm), lambda j, i, order_ref: (0, first_sweep(j, i, order_ref))),
                       hbm, hbm, hbm),
            scratch_shapes=[pltpu.VMEM((IN_W, D_MODEL), BF16), pltpu.VMEM((seq, D_MODEL), BF16),
                            pltpu.VMEM((N_DEV, 2, EDGE, D_MODEL), BF16),
                            pltpu.VMEM(w_out_s.shape, BF16), pltpu.VMEM(small_s.shape, F32),
                            pltpu.SemaphoreType.DMA((7 * PIECES,)), pltpu.SemaphoreType.DMA((7 * PIECES,)),
                            pltpu.SemaphoreType.DMA((3,))]),
        compiler_params=_cparams("arbitrary", "arbitrary"),
    )(order, x2d, norm_g, shifted, w_out_s, small_s)


def _block_masks(tt):
    row = lax.broadcasted_iota(jnp.int32, (tt, tt), 0)
    col = lax.broadcasted_iota(jnp.int32, (tt, tt), 1)
    same = jnp.right_shift(row, 6) == jnp.right_shift(col, 6)
    return (jnp.logical_and(same, col <= row), jnp.logical_and(same, col >= row), jnp.logical_and(same, col > row))


def _dot_split3(ones_mat, x):
    x1 = x.astype(BF16)
    r1 = x - x1.astype(F32)
    x2 = r1.astype(BF16)
    x3 = (r1 - x2.astype(F32)).astype(BF16)
    return (_dot(ones_mat, x3) + _dot(ones_mat, x2)) + _dot(ones_mat, x1)


def _log_gate(logits):
    return (jnp.minimum(logits, 0.0) - jnp.log(1.0 + jnp.exp(-jnp.abs(logits)))) * GATE_SCALE


def _chunk_column_mask(tt):
    nc = tt // CHUNK
    row = lax.broadcasted_iota(jnp.int32, (tt, nc * DK), 0)
    col = lax.broadcasted_iota(jnp.int32, (tt, nc * DK), 1)
    return jnp.right_shift(row, 6) == jnp.right_shift(col, 7)


def _chunked(mask, x, nc):
    wide = jnp.concatenate([x] * nc, axis=1)
    return jnp.where(mask, wide, jnp.zeros_like(wide))


def _gla_fwd(proj, lr, wgk_f, wgk_b, bgk_f, bgk_b, tt):
    seq = proj.shape[0]
    nb, nc, nch = seq // tt, tt // CHUNK, seq // CHUNK

    def body(qf, kf, vf, lrf, qb, kb, vb, lrb, wf, wb, bf, bb, of, ob, stf, stb, s_scr, qs_s, ks_s, qin_s, kout_s):
        @pl.when(pl.program_id(0) == 0)
        def _():
            s_scr[...] = jnp.zeros(s_scr.shape, F32)

        low, upp, sup = _block_masks(tt)
        dirs = ((qf, kf, vf, lrf, wf, bf, of, stf, low, low, REF_F, LAST_F, list(range(nc))),
                (qb, kb, vb, lrb, wb, bb, ob, stb, upp, sup, REF_B, LAST_B, list(reversed(range(nc)))))
        for d, (q_r, k_r, v_r, lr_r, w_r, b_r, o_r, st_r, cum, mask, ref, last, order) in enumerate(dirs):
            logits = _dot(lr_r[...].astype(BF16), w_r[...]) + b_r[...]
            b = _dot_split3(cum.astype(BF16), _log_gate(logits))
            decs = []
            for c in range(nc):
                rows = slice(c * CHUNK, (c + 1) * CHUNK)
                bc = b[rows]
                b_ref, b_last = bc[ref:ref + 1], bc[last:last + 1]
                qc = q_r[rows, :].astype(F32) * QSCALE
                kc = k_r[rows, :].astype(F32)
                qs_s[rows, :] = (qc * jnp.exp(bc - b_ref)).astype(BF16)
                ks_s[rows, :] = (kc * jnp.exp(b_ref - bc)).astype(BF16)
                qin_s[rows, :] = (qc * jnp.exp(bc)).astype(BF16)
                kout_s[rows, :] = (kc * jnp.exp(b_last - bc)).astype(BF16)
                decs.append(jnp.exp(b_last))
            for h in range(HEADS):
                ksl = slice(h * DK, (h + 1) * DK)
                vsl = slice(h * DV, (h + 1) * DV)
                v = v_r[:, vsl].astype(BF16)
                att = jnp.where(mask, _dot_nt(qs_s[:, ksl], ks_s[:, ksl]), 0.0).astype(BF16)
                o_intra = _dot(att, v)
                st = s_scr[d * HEADS + h]
                for c in order:
                    rows = slice(c * CHUNK, (c + 1) * CHUNK)
                    stb = st.astype(BF16)
                    st_r[c, h] = stb
                    o_r[rows, vsl] = (o_intra[rows] + _dot_nt(qin_s[rows, ksl], stb)).astype(BF16)
                    st = st * decs[c][:, ksl] + _dot_tn(v[rows], kout_s[rows, ksl])
                s_scr[d * HEADS + h] = st

    fw = lambda i: (i, 0)
    bw = lambda i: (nb - 1 - i, 0)
    const = lambda i: (0, 0)

    def tok_specs(m):
        return [pl.BlockSpec((tt, QK_W), lambda i: (m(i)[0], OFF_Q // QK_W)),
                pl.BlockSpec((tt, QK_W), lambda i: (m(i)[0], OFF_K // QK_W)),
                pl.BlockSpec((tt, V_W), lambda i: (m(i)[0], OFF_V // V_W)),
                pl.BlockSpec((tt, LR_W), m)]

    st_shape = jax.ShapeDtypeStruct((nch, HEADS, DV, DK), BF16)
    o_shape = jax.ShapeDtypeStruct((seq, V_W), BF16)
    operand = pltpu.VMEM((tt, QK_W), BF16)
    return pl.pallas_call(
        body, name="gla_fwd",
        out_shape=(o_shape, o_shape, st_shape, st_shape),
        grid=(nb,),
        in_specs=tok_specs(fw) + tok_specs(bw) + [
            pl.BlockSpec((LR_W, QK_W), const), pl.BlockSpec((LR_W, QK_W), const),
            pl.BlockSpec((1, QK_W), const), pl.BlockSpec((1, QK_W), const)],
        out_specs=(pl.BlockSpec((tt, V_W), fw), pl.BlockSpec((tt, V_W), bw),
                   pl.BlockSpec((nc, HEADS, DV, DK), lambda i: (i, 0, 0, 0)),
                   pl.BlockSpec((nc, HEADS, DV, DK), lambda i: (nb - 1 - i, 0, 0, 0))),
        scratch_shapes=[pltpu.VMEM((2 * HEADS, DV, DK), F32), operand, operand, operand, operand],
        compiler_params=_cparams("arbitrary"),
    )(proj, proj, proj, lr, proj, proj, proj, lr, wgk_f, wgk_b, bgk_f, bgk_b)


def _head_norm(o, gain):
    outs, rinv = [], []
    for h in range(HEADS):
        oh = o[:, h * DV:(h + 1) * DV]
        r = lax.rsqrt(jnp.mean(oh * oh, axis=-1, keepdims=True) + EPS)
        outs.append((oh * r) * gain)
        rinv.append(r)
    return jnp.concatenate(outs, axis=1), rinv


def _shift_rows(u, prev_row, next_row):
    n = u.shape[0]
    row = lax.broadcasted_iota(jnp.int32, (n, 1), 0)
    up = jnp.where(row == 0, prev_row, pltpu.roll(u, 1, 0))
    un = jnp.where(row == n - 1, next_row, pltpu.roll(u, n - 1, 0))
    return up, un


HALO = 16


def _halo_specs(tm, seq, col_block):
    per = tm // HALO
    last = seq // HALO - 1
    return [pl.BlockSpec((HALO, CONV_W), lambda i: (jnp.maximum(i * per - 1, 0), col_block)),
            pl.BlockSpec((HALO, CONV_W), lambda i: (jnp.minimum((i + 1) * per, last), col_block))]


def _f32(ref):
    return ref[...].astype(F32)


def _last_row(ref):
    return ref[HALO - 1:HALO, :].astype(F32)


def _first_row(ref):
    return ref[0:1, :].astype(F32)


def _mix_out_loss(o_f, o_b, proj, x2d, tgt, gla_g, conv_w, conv_b, w_out, final_g, tm):
    seq = x2d.shape[0]
    nt = seq // tm

    def body(of, ob, za, bg, cg, hc, zc, cprev, cnext, hprev, hnext, x_ref, t_ref, gg, cw, cb, wo, fg,
             yt_ref, conv_ref, dx2_ref, dx2b_ref, loss_ref, dfg_ref):
        i = pl.program_id(0)

        @pl.when(i == 0)
        def _():
            loss_ref[...] = jnp.zeros(loss_ref.shape, F32)
            dfg_ref[...] = jnp.zeros(dfg_ref.shape, F32)

        on, _ = _head_norm(_f32(of) + _f32(ob), gg[...])
        zav = _f32(za)
        y_a = on * (zav * _sigmoid(zav))
        u = _f32(cg) * _f32(hc)
        prev_row = jnp.where(i > 0, _last_row(cprev) * _last_row(hprev), 0.0)
        next_row = jnp.where(i < nt - 1, _first_row(cnext) * _first_row(hnext), 0.0)
        up, un = _shift_rows(u, prev_row, next_row)
        conv = (cw[0:1, :] * up + cw[1:2, :] * u + cw[2:3, :] * un) + cb[...]
        conv_ref[...] = conv.astype(BF16)
        zcv = _f32(zc)
        y_c = _f32(bg) * conv * (zcv * _sigmoid(zcv))
        y = jnp.concatenate([y_a, y_c], axis=1)
        yt_ref[...] = y.T.astype(BF16)
        x2 = x_ref[...] + _dot(y.astype(BF16), wo[...])
        r = lax.rsqrt(jnp.mean(x2 * x2, axis=-1, keepdims=True) + EPS)
        xn = x2 * r
        err = xn * fg[...] - t_ref[...]
        loss_ref[...] += 0.5 * jnp.sum(jnp.mean(err * err, axis=-1, keepdims=True))
        dyf = err * (1.0 / D_MODEL)
        dfg_ref[...] += jnp.sum(dyf * xn, axis=0, keepdims=True)
        dxn = dyf * fg[...]
        dx2 = r * dxn - xn * (r * jnp.mean(dxn * xn, axis=-1, keepdims=True))
        dx2_ref[...] = dx2
        dx2b_ref[...] = dx2.astype(BF16)

    def col(off):
        return pl.BlockSpec((tm, CONV_W), lambda i: (i, off // CONV_W))

    rowt = pl.BlockSpec((tm, D_MODEL), lambda i: (i, 0))
    const = lambda shape: pl.BlockSpec(shape, lambda i: (0, 0))
    return pl.pallas_call(
        body, name="mix_out_loss",
        out_shape=(jax.ShapeDtypeStruct((MIX_W, seq), BF16), jax.ShapeDtypeStruct((seq, CONV_W), BF16),
                   jax.ShapeDtypeStruct((seq, D_MODEL), F32), jax.ShapeDtypeStruct((seq, D_MODEL), BF16),
                   jax.ShapeDtypeStruct((8, 128), F32), jax.ShapeDtypeStruct((1, D_MODEL), F32)),
        grid=(nt,),
        in_specs=[rowt, rowt, col(OFF_ZA), col(OFF_B), col(OFF_C), col(OFF_H), col(OFF_ZC)]
        + _halo_specs(tm, seq, OFF_C // CONV_W) + _halo_specs(tm, seq, OFF_H // CONV_W)
        + [rowt, rowt, const((1, DV)), const((8, CONV_W)), const((1, CONV_W)), const((MIX_W, D_MODEL)),
           const((1, D_MODEL))],
        out_specs=(pl.BlockSpec((MIX_W, tm), lambda i: (0, i)), rowt, rowt, rowt, const((8, 128)),
                   const((1, D_MODEL))),
        compiler_params=_cparams("arbitrary"),
    )(o_f, o_b, proj, proj, proj, proj, proj, proj, proj, proj, proj, x2d, tgt, gla_g, conv_w, conv_b, w_out, final_g)


def _dsilu(z, s):
    return s * (1.0 + z * (1.0 - s))


def _mix_bwd(dx2b, o_f, o_b, proj, conv, gla_g, w_out, tm):
    seq = dx2b.shape[0]

    def body(dx, of, ob, za, bg, zc, cv, gg, wo, dg_ref, do_ref, dconv_ref, dgg_ref, dcb_ref):
        @pl.when(pl.program_id(0) == 0)
        def _():
            dgg_ref[...] = jnp.zeros(dgg_ref.shape, F32)
            dcb_ref[...] = jnp.zeros(dcb_ref.shape, F32)

        dy = _dot_nt(dx[...], wo[...])
        dy_a, dy_c = dy[:, :V_W], dy[:, V_W:]
        zcv, bgv, convv = _f32(zc), _f32(bg), _f32(cv)
        sc = _sigmoid(zcv)
        szc = zcv * sc
        dg_ref[:, CONV_W:2 * CONV_W] = (dy_c * convv * szc).astype(BF16)
        dconv = dy_c * bgv * szc
        dconv_ref[...] = dconv.astype(BF16)
        dcb_ref[...] += jnp.sum(dconv, axis=0, keepdims=True)
        dg_ref[:, 2 * CONV_W:] = (dy_c * bgv * convv * _dsilu(zcv, sc)).astype(BF16)

        o = _f32(of) + _f32(ob)
        gain = gg[...]
        on, rinv = _head_norm(o, gain)
        zav = _f32(za)
        sa = _sigmoid(zav)
        dg_ref[:, :CONV_W] = (dy_a * on * _dsilu(zav, sa)).astype(BF16)
        don = dy_a * (zav * sa)
        dgg = jnp.zeros((1, DV), F32)
        dos = []
        for h in range(HEADS):
            sl = slice(h * DV, (h + 1) * DV)
            oh, r, dh = o[:, sl], rinv[h], don[:, sl]
            ohn = oh * r
            dgg = dgg + jnp.sum(dh * ohn, axis=0, keepdims=True)
            dn = dh * gain
            dos.append(r * dn - ohn * (r * jnp.mean(dn * ohn, axis=-1, keepdims=True)))
        dgg_ref[...] += dgg
        do_ref[...] = jnp.concatenate(dos, axis=1).astype(BF16)

    def col(off):
        return pl.BlockSpec((tm, CONV_W), lambda i: (i, off // CONV_W))

    rowt = pl.BlockSpec((tm, D_MODEL), lambda i: (i, 0))
    const = lambda shape: pl.BlockSpec(shape, lambda i: (0, 0))
    return pl.pallas_call(
        body, name="mix_bwd",
        out_shape=(jax.ShapeDtypeStruct((seq, GATES_W), BF16), jax.ShapeDtypeStruct((seq, V_W), BF16),
                   jax.ShapeDtypeStruct((seq, CONV_W), BF16),
                   jax.ShapeDtypeStruct((1, DV), F32), jax.ShapeDtypeStruct((1, CONV_W), F32)),
        grid=(seq // tm,),
        in_specs=[rowt, rowt, rowt, col(OFF_ZA), col(OFF_B), col(OFF_ZC), rowt, const((1, DV)),
                  const((MIX_W, D_MODEL))],
        out_specs=(pl.BlockSpec((tm, GATES_W), lambda i: (i, 0)), rowt, rowt, const((1, DV)), const((1, CONV_W))),
        compiler_params=_cparams("arbitrary"),
    )(dx2b, o_f, o_b, proj, proj, proj, conv, gla_g, w_out)


def _conv_bwd(dconv, proj, conv_w, tm):
    seq = dconv.shape[0]
    nt = seq // tm

    def body(dc_in, dprev, dnext, cg, hc, cprev, cnext, hprev, hnext, cw, dch_ref, dcw_ref):
        i = pl.program_id(0)

        @pl.when(i == 0)
        def _():
            dcw_ref[...] = jnp.zeros(dcw_ref.shape, F32)

        first, lastt = i > 0, i < nt - 1
        dcv = _f32(dc_in)
        d_up, d_un = _shift_rows(dcv, jnp.where(first, _last_row(dprev), 0.0), jnp.where(lastt, _first_row(dnext), 0.0))
        cgv, hcv = _f32(cg), _f32(hc)
        u = cgv * hcv
        u_up, u_un = _shift_rows(u, jnp.where(first, _last_row(cprev) * _last_row(hprev), 0.0),
                                 jnp.where(lastt, _first_row(cnext) * _first_row(hnext), 0.0))
        du = cw[0:1, :] * d_un + cw[1:2, :] * dcv + cw[2:3, :] * d_up
        dch_ref[:, :CONV_W] = (du * hcv).astype(BF16)
        dch_ref[:, CONV_W:] = (du * cgv).astype(BF16)
        dcw_ref[0:1, :] += jnp.sum(dcv * u_up, axis=0, keepdims=True)
        dcw_ref[1:2, :] += jnp.sum(dcv * u, axis=0, keepdims=True)
        dcw_ref[2:3, :] += jnp.sum(dcv * u_un, axis=0, keepdims=True)

    def col(off):
        return pl.BlockSpec((tm, CONV_W), lambda i: (i, off // CONV_W))

    rowt = pl.BlockSpec((tm, CONV_W), lambda i: (i, 0))
    const = lambda shape: pl.BlockSpec(shape, lambda i: (0, 0))
    return pl.pallas_call(
        body, name="conv_bwd",
        out_shape=(jax.ShapeDtypeStruct((seq, CH_W), BF16), jax.ShapeDtypeStruct((8, CONV_W), F32)),
        grid=(nt,),
        in_specs=[rowt] + _halo_specs(tm, seq, 0) + [col(OFF_C), col(OFF_H)]
        + _halo_specs(tm, seq, OFF_C // CONV_W) + _halo_specs(tm, seq, OFF_H // CONV_W) + [const((8, CONV_W))],
        out_specs=(pl.BlockSpec((tm, CH_W), lambda i: (i, 0)), const((8, CONV_W))),
        compiler_params=_cparams("arbitrary"),
    )(dconv, dconv, dconv, proj, proj, proj, proj, proj, proj, conv_w)


def _gla_bwd(proj, lr, do, st_f, st_b, wgk_f, wgk_b, bgk_f, bgk_b, tt):
    seq = proj.shape[0]
    nb, nc = seq // tt, tt // CHUNK

    def body(qf, kf, vf, lrf, dof, stf, qb, kb, vb, lrb, dob, stb, wf, wb, bf, bb,
             dqkv_f, dlr_f, dqkv_b, dlr_b, dwf, dwb, dbf, dbb,
             ds_scr, eq_s, ek_s, ein_s, eout_s, qs_s, ks_s, qin_s, kout_s, db_s, lg_s):
        @pl.when(pl.program_id(0) == 0)
        def _():
            ds_scr[...] = jnp.zeros(ds_scr.shape, F32)
            for r in (dwf, dwb, dbf, dbb):
                r[...] = jnp.zeros(r.shape, F32)

        low, upp, sup = _block_masks(tt)
        row = lax.broadcasted_iota(jnp.int32, (CHUNK, 1), 0)
        kmask = _chunk_column_mask(tt)
        dirs = ((qf, kf, vf, lrf, dof, stf, wf, bf, dqkv_f, dlr_f, dwf, dbf,
                 low, upp, low, REF_F, LAST_F, list(reversed(range(nc)))),
                (qb, kb, vb, lrb, dob, stb, wb, bb, dqkv_b, dlr_b, dwb, dbb,
                 upp, low, sup, REF_B, LAST_B, list(range(nc))))
        for d, (q_r, k_r, v_r, lr_r, do_r, st_r, w_r, b_r, dqkv_r, dlr_r, dw_r, db_r,
                cum, cum_t, mask, ref, last, order) in enumerate(dirs):
            lrv = lr_r[...].astype(BF16)
            wv = w_r[...]
            logits = _dot(lrv, wv) + b_r[...]
            lg_s[...] = logits
            b = _dot_split3(cum.astype(BF16), _log_gate(logits))
            decs = []
            for c in range(nc):
                rows = slice(c * CHUNK, (c + 1) * CHUNK)
                bc = b[rows]
                b_ref, b_last = bc[ref:ref + 1], bc[last:last + 1]
                qc = q_r[rows, :].astype(F32) * QSCALE
                kc = k_r[rows, :].astype(F32)
                e_q, e_k, e_in, e_out = jnp.exp(bc - b_ref), jnp.exp(b_ref - bc), jnp.exp(bc), jnp.exp(b_last - bc)
                eq_s[rows, :], ek_s[rows, :], ein_s[rows, :], eout_s[rows, :] = e_q, e_k, e_in, e_out
                qs_s[rows, :] = (qc * e_q).astype(BF16)
                ks_s[rows, :] = (kc * e_k).astype(BF16)
                qin_s[rows, :] = (qc * e_in).astype(BF16)
                kout_s[rows, :] = (kc * e_out).astype(BF16)
                decs.append(jnp.exp(b_last))
            for h in range(HEADS):
                ksl = slice(h * DK, (h + 1) * DK)
                vsl = slice(h * DV, (h + 1) * DV)
                v = v_r[:, vsl].astype(BF16)
                dov = do_r[:, vsl].astype(BF16)
                qsb, ksb = qs_s[:, ksl], ks_s[:, ksl]
                att = jnp.where(mask, _dot_nt(qsb, ksb), 0.0).astype(BF16)
                datt = jnp.where(mask, _dot_nt(dov, v), 0.0).astype(BF16)
                dqs = _dot(datt, ksb)
                dks = _dot_tn(datt, qsb)
                dv_intra = _dot_tn(att, dov)
                g_t = _dot_tn(dov, _chunked(kmask, qin_s[:, ksl], nc))
                ds = ds_scr[d * HEADS + h]
                for c in order:
                    rows = slice(c * CHUNK, (c + 1) * CHUNK)
                    dsb = ds.astype(BF16)
                    s_prev = st_r[c, h]
                    dk_out = _dot(v[rows], dsb)
                    dq_in = _dot(dov[rows], s_prev)
                    dv = dv_intra[rows] + _dot_nt(kout_s[rows, ksl], dsb)
                    dqkv_r[rows, OFF_V + h * DV:OFF_V + (h + 1) * DV] = dv.astype(BF16)
                    dec = decs[c][:, ksl]
                    ddec = jnp.sum(ds * s_prev.astype(F32), axis=0, keepdims=True)
                    e_out = eout_s[rows, ksl]
                    qc = q_r[rows, ksl].astype(F32) * QSCALE
                    kc = k_r[rows, ksl].astype(F32)
                    dq = dqs[rows] * eq_s[rows, ksl] + dq_in * ein_s[rows, ksl]
                    dk = dks[rows] * ek_s[rows, ksl] + dk_out * e_out
                    dqkv_r[rows, OFF_Q + h * DK:OFF_Q + (h + 1) * DK] = (dq * QSCALE).astype(BF16)
                    dqkv_r[rows, OFF_K + h * DK:OFF_K + (h + 1) * DK] = dk.astype(BF16)
                    tail = jnp.sum(dk_out * (kc * e_out), axis=0, keepdims=True) + ddec * dec
                    db_s[rows, ksl] = (qc * dq - kc * dk) + jnp.where(row == last, tail, 0.0)
                    ds = ds * dec + g_t[:, c * DK:(c + 1) * DK]
                ds_scr[d * HEADS + h] = ds
            dg = _dot_split3(cum_t.astype(BF16), db_s[...])
            dlogit = (dg * GATE_SCALE) * _sigmoid(-lg_s[...])
            dlb = dlogit.astype(BF16)
            dlr_r[...] = _dot_nt(dlb, wv)
            dw_r[...] += _dot_tn(lrv, dlb)
            db_r[...] += jnp.sum(dlogit, axis=0, keepdims=True)

    fw = lambda i: (nb - 1 - i, 0)
    bw = lambda i: (i, 0)
    const = lambda i: (0, 0)

    def tok_specs(m):
        return [pl.BlockSpec((tt, QK_W), lambda i: (m(i)[0], OFF_Q // QK_W)),
                pl.BlockSpec((tt, QK_W), lambda i: (m(i)[0], OFF_K // QK_W)),
                pl.BlockSpec((tt, V_W), lambda i: (m(i)[0], OFF_V // V_W)),
                pl.BlockSpec((tt, LR_W), m),
                pl.BlockSpec((tt, V_W), m),
                pl.BlockSpec((nc, HEADS, DV, DK), lambda i: (m(i)[0], 0, 0, 0))]

    dqkv = jax.ShapeDtypeStruct((seq, QK_W + QK_W + V_W), BF16)
    dlr = jax.ShapeDtypeStruct((seq, LR_W), F32)
    dw = jax.ShapeDtypeStruct((LR_W, QK_W), F32)
    dbias = jax.ShapeDtypeStruct((1, QK_W), F32)
    return pl.pallas_call(
        body, name="gla_bwd",
        out_shape=(dqkv, dlr, dqkv, dlr, dw, dw, dbias, dbias),
        grid=(nb,),
        in_specs=tok_specs(fw) + tok_specs(bw) + [
            pl.BlockSpec((LR_W, QK_W), const), pl.BlockSpec((LR_W, QK_W), const),
            pl.BlockSpec((1, QK_W), const), pl.BlockSpec((1, QK_W), const)],
        out_specs=(pl.BlockSpec((tt, QK_W + QK_W + V_W), fw), pl.BlockSpec((tt, LR_W), fw),
                   pl.BlockSpec((tt, QK_W + QK_W + V_W), bw), pl.BlockSpec((tt, LR_W), bw),
                   pl.BlockSpec((LR_W, QK_W), const), pl.BlockSpec((LR_W, QK_W), const),
                   pl.BlockSpec((1, QK_W), const), pl.BlockSpec((1, QK_W), const)),
        scratch_shapes=[pltpu.VMEM((2 * HEADS, DV, DK), F32)] + [pltpu.VMEM((tt, QK_W), F32)] * 4
        + [pltpu.VMEM((tt, QK_W), BF16)] * 4 + [pltpu.VMEM((tt, QK_W), F32)] * 2,
        compiler_params=_cparams("arbitrary"),
    )(proj, proj, proj, lr, do, st_f, proj, proj, proj, lr, do, st_b, wgk_f, wgk_b, bgk_f, bgk_b)


def _sum_directions(dqkv_f, dqkv_b, dlr_f, dlr_b, tm):
    seq = dqkv_f.shape[0]

    def body(a, b, la, lb, dp_out, dlr_out):
        dp_out[...] = (_f32(a) + _f32(b)).astype(BF16)
        dlr_out[...] = (la[...] + lb[...]).astype(BF16)

    rowt = pl.BlockSpec((tm, QKV_W), lambda i: (i, 0))
    lrt = pl.BlockSpec((tm, LR_W), lambda i: (i, 0))
    return pl.pallas_call(
        body, name="sum_directions",
        out_shape=(jax.ShapeDtypeStruct((seq, QKV_W), BF16), jax.ShapeDtypeStruct((seq, LR_W), BF16)),
        grid=(seq // tm,),
        in_specs=[rowt, rowt, lrt, lrt],
        out_specs=(rowt, lrt),
        compiler_params=_cparams("arbitrary"),
    )(dqkv_f, dqkv_b, dlr_f, dlr_b)


def _input_grad(dp_qkv, dp_gates, dp_ch, dlr, w_nat, x2d, norm_g, dx2, sums, tm):
    seq = x2d.shape[0]
    nt, n = seq // tm, len(sums)
    relay_step = (3 * nt) // 8

    def body(dq, dg, dc, dl, w, x_ref, g_ref, dx2_ref, *rest):
        ins, (gx_ref, dng_ref), outs = rest[:n], rest[n:n + 2], rest[n + 2:2 * n + 2]
        passing, joined = rest[2 * n + 2:3 * n + 2], rest[3 * n + 2:4 * n + 2]
        send_sems, recv_sems, local_sems = rest[4 * n + 2:]
        i = pl.program_id(0)
        c = lax.axis_index("c")
        first, second, diagonal = _route_chips()
        slot = lambda chip: 2 * chip[0] + chip[1]

        def remote(a, k, src, dst, to):
            return pltpu.make_async_remote_copy(src_ref=src, dst_ref=dst, send_sem=send_sems.at[3 * a + k],
                                                recv_sem=recv_sems.at[3 * a + k], device_id=(*to, c),
                                                device_id_type=MESH)

        direct = lambda a: remote(a, 0, ins[a].at[slot(first)], outs[a].at[0], first)
        for_second = lambda a: remote(a, 1, ins[a].at[slot(diagonal)], passing[a], first)
        joint = lambda a: remote(a, 2, joined[a], outs[a].at[1], second)
        own = lambda a: pltpu.make_async_copy(ins[a].at[slot(second)], joined[a], local_sems.at[a])

        @pl.when(i == 0)
        def _():
            _start_all([for_second(a) for a in range(n)] + [own(a) for a in range(n)] + [direct(a) for a in range(n)])
            dng_ref[...] = jnp.zeros(dng_ref.shape, F32)

        @pl.when(i == relay_step)
        def _():
            for a in range(n):
                for_second(a).wait_recv()
                own(a).wait()
                joined[a][...] = (joined[a][...].astype(F32) + passing[a][...].astype(F32)).astype(BF16)
                joint(a).start()

        dh = (_dot(dl[...], w[NAT_LR:NAT_LR + LR_W, :]) + _dot(dq[...], w[0:NAT_ZA, :])
              + _dot(dg[:, 0:CONV_W], w[NAT_ZA:NAT_LR, :]) + _dot(dg[:, CONV_W:2 * CONV_W], w[NAT_B:NAT_C, :])
              + _dot(dg[:, 2 * CONV_W:], w[NAT_ZC:IN_W, :]) + _dot(dc[...], w[NAT_C:NAT_ZC, :]))
        xv = x_ref[...]
        r = lax.rsqrt(jnp.mean(xv * xv, axis=-1, keepdims=True) + EPS)
        xn = xv * r
        dng_ref[...] += jnp.sum(dh * xn, axis=0, keepdims=True)
        dn = dh * g_ref[...]
        gx_ref[...] = (r * dn - xn * (r * jnp.mean(dn * xn, axis=-1, keepdims=True))) + dx2_ref[...]

        @pl.when(i == nt - 1)
        def _():
            for a in range(n):
                direct(a).wait_recv()
                joint(a).wait_recv()
            for a in range(n):
                for cp in (direct(a), for_second(a), joint(a)):
                    cp.wait_send()

    rowt = pl.BlockSpec((tm, D_MODEL), lambda i: (i, 0))
    seg = lambda width: pl.BlockSpec((tm, width), lambda i: (i, 0))
    resident = lambda rows: pl.BlockSpec((rows, D_MODEL), lambda i: (0, 0), pipeline_mode=pl.Buffered(1))
    hbm = pl.BlockSpec(memory_space=pl.ANY)
    blocks = [pltpu.VMEM(s.shape[1:], s.dtype) for s in sums]
    return pl.pallas_call(
        body, name="input_grad",
        out_shape=(jax.ShapeDtypeStruct((seq, D_MODEL), F32), jax.ShapeDtypeStruct((1, D_MODEL), F32))
        + tuple(jax.ShapeDtypeStruct((2,) + s.shape[1:], s.dtype) for s in sums),
        grid=(nt,),
        in_specs=[seg(QKV_W), seg(GATES_W), seg(CH_W), seg(LR_W), resident(IN_W),
                  rowt, pl.BlockSpec((1, D_MODEL), lambda i: (0, 0)), rowt] + [hbm] * n,
        out_specs=(rowt, pl.BlockSpec((1, D_MODEL), lambda i: (0, 0))) + (hbm,) * n,
        scratch_shapes=blocks + blocks + [pltpu.SemaphoreType.DMA((3 * n,)), pltpu.SemaphoreType.DMA((3 * n,)),
                                          pltpu.SemaphoreType.DMA((n,))],
        compiler_params=_cparams("arbitrary"),
    )(dp_qkv, dp_gates, dp_ch, dlr, w_nat, x2d, norm_g, dx2, *sums)


def _weight_grad_out(y_t, dx2b, tk, riding):
    m, seq = y_t.shape
    n = dx2b.shape[1]
    nk = seq // tk

    def body(a_ref, b_ref, ride_in, o_ref, ride_out, send_sems, recv_sems):
        k = pl.program_id(0)

        @pl.when(k == 0)
        def _():
            _start_all(_sibling_copies(ride_in, ride_out, send_sems, recv_sems))
            o_ref[...] = jnp.zeros(o_ref.shape, F32)

        o_ref[...] += _dot(a_ref[...], b_ref[...])

        @pl.when(k == nk - 1)
        def _():
            _wait_all(_sibling_copies(ride_in, ride_out, send_sems, recv_sems))

    hbm = pl.BlockSpec(memory_space=pl.ANY)
    return pl.pallas_call(
        body, name="wgrad_out",
        out_shape=(jax.ShapeDtypeStruct((m, n), F32), jax.ShapeDtypeStruct((4,) + _block_shape(riding), F32)),
        grid=(nk,),
        in_specs=[pl.BlockSpec((m, tk), lambda k: (0, k)), pl.BlockSpec((tk, n), lambda k: (k, 0)), hbm],
        out_specs=(pl.BlockSpec((m, n), lambda k: (0, 0)), hbm),
        scratch_shapes=[pltpu.SemaphoreType.DMA((4,)), pltpu.SemaphoreType.DMA((4,))],
        compiler_params=_cparams("arbitrary"),
    )(y_t, dx2b, riding)


def _weight_grad_in(h_t, dp_qkv, dp_gates, dp_ch, dlr):
    m, seq = h_t.shape
    tn = 512
    n_qkv, n_gates, n_ch = QKV_W // tn, GATES_W // tn, CH_W // tn
    starts = ([k * tn for k in range(n_qkv)] + [NAT_ZA, NAT_ZA + tn, NAT_B, NAT_B + tn, NAT_ZC, NAT_ZC + tn]
              + [NAT_C + k * tn for k in range(n_ch)])

    def out_row(j):
        row = 0
        for k, start in enumerate(starts):
            row = row + jnp.where(j == k, start // 32, 0)
        return pl.multiple_of(row * 32, 32), 0

    def body(a_ref, bq, bg, bc, o_ref, acc):
        j = pl.program_id(0)

        @pl.when(j < n_qkv)
        def _():
            acc[...] = _dot(a_ref[...], bq[...])

        @pl.when(jnp.logical_and(j >= n_qkv, j < n_qkv + n_gates))
        def _():
            acc[...] = _dot(a_ref[...], bg[...])

        @pl.when(j >= n_qkv + n_gates)
        def _():
            acc[...] = _dot(a_ref[...], bc[...])

        o_ref[...] = acc[...].T

    resident = pl.BlockSpec((m, seq), lambda j: (0, 0), pipeline_mode=pl.Buffered(1))
    seg = lambda first, count: pl.BlockSpec((seq, tn), lambda j: (0, jnp.clip(j - first, 0, count - 1)))
    main = pl.pallas_call(
        body, name="wgrad_in",
        out_shape=jax.ShapeDtypeStruct((IN_W, m), F32),
        grid=(n_qkv + n_gates + n_ch,),
        in_specs=[resident, seg(0, n_qkv), seg(n_qkv, n_gates), seg(n_qkv + n_gates, n_ch)],
        out_specs=pl.BlockSpec((pl.Element(tn), pl.Element(m)), out_row),
        scratch_shapes=[pltpu.VMEM((m, tn), F32)],
        compiler_params=_cparams("arbitrary"),
    )(h_t, dp_qkv, dp_gates, dp_ch)

    def lr_body(a_ref, b_ref, full_ref, o_ref, acc):
        acc[...] = _dot(a_ref[...], b_ref[...])
        o_ref[...] = acc[...].T[0:2 * RANK, :]

    whole = lambda shape: pl.BlockSpec(shape, lambda j: (0, 0))
    return pl.pallas_call(
        lr_body, name="wgrad_lr",
        out_shape=jax.ShapeDtypeStruct((IN_W, m), F32),
        grid=(1,),
        in_specs=[whole((m, seq)), whole((seq, LR_W)), pl.BlockSpec(memory_space=pl.ANY)],
        out_specs=pl.BlockSpec((pl.Element(2 * RANK), pl.Element(m)), lambda j: (NAT_LR, 0)),
        scratch_shapes=[pltpu.VMEM((m, LR_W), F32)],
        input_output_aliases={2: 0},
        compiler_params=_cparams("arbitrary"),
    )(h_t, dlr, main)


def _pad_rows(a, rows):
    return jnp.pad(a, ((0, rows - a.shape[0]), (0, 0)))


def _rows128(a):
    a = a.reshape(-1, 128)
    return _pad_rows(a, -(-a.shape[0] // 8) * 8)


def _pack(arrs):
    return jnp.concatenate([_rows128(a) for a in arrs], axis=0)


def _unpack(buf, like):
    out, start = [], 0
    for a in like:
        rows = a.size // 128
        out.append(buf[start:start + rows].reshape(a.shape))
        start += -(-rows // 8) * 8
    return out


def kernel(x, norm_g, w_in, w_gk_f, b_gk_f, w_gk_b, b_gk_b, gla_norm_g, conv_w, conv_b, w_out, final_g, loss_target, m_norm_g, m_w_in, m_w_gk_f, m_b_gk_f, m_w_gk_b, m_b_gk_b, m_gla_norm_g, m_conv_w, m_conv_b, m_w_out, m_final_g, v_norm_g, v_w_in, v_w_gk_f, v_b_gk_f, v_w_gk_b, v_b_gk_b, v_gla_norm_g, v_conv_w, v_conv_b, v_w_out, v_final_g):
    px, py, pc = _position()
    me = _blk(px, py, pc)
    seq = x.shape[1]
    x2d, tgt = x[0], loss_target[0]
    tm = min(512, seq)
    tt = min(256, seq)

    small_s = jnp.concatenate([jnp.concatenate([w_gk_f[0], w_gk_b[0]], axis=1), _pad_rows(conv_w[0], 8)], axis=0)
    shifted = lax.dynamic_update_slice(jnp.zeros((SHIFTED_ROWS, D_MODEL), F32), w_in[0].T, (4 * (me % 4), 0))
    order = sum(jnp.where(2 * px + py == k, jnp.asarray(tiles + (0,), jnp.int32), 0) for k, tiles in enumerate(TILE_ORDER))
    proj, lr, h_t, w_nat, wout_all, small_all = _gather_inproj(x2d, norm_g, shifted, w_out[0], small_s, order,
                                                               min(1024, seq))
    w_out_full = wout_all.reshape(MIX_W, D_MODEL)
    wgk_cols = 512 // N_DEV
    wgk_f_full = small_all[:, 0:RANK, 0:wgk_cols].transpose(1, 0, 2).reshape(RANK, QK_W)
    wgk_b_full = small_all[:, 0:RANK, wgk_cols:2 * wgk_cols].transpose(1, 0, 2).reshape(RANK, QK_W)
    conv_w_full = _pad_rows(small_all[:, RANK:RANK + 3, :].transpose(1, 0, 2).reshape(3, CONV_W), 8)
    zr = lambda n: jnp.zeros((n, QK_W), F32)
    wgk_f_pad = jnp.concatenate([wgk_f_full, zr(LR_W - RANK)], axis=0).astype(BF16)
    wgk_b_pad = jnp.concatenate([zr(RANK), wgk_b_full, zr(LR_W - 2 * RANK)], axis=0).astype(BF16)

    o_f, o_b, st_f, st_b = _gla_fwd(proj, lr, wgk_f_pad, wgk_b_pad, b_gk_f, b_gk_b, tt)
    tmix = min(256, seq)
    y_t, conv, dx2, dx2b, loss_p, dfg_p = _mix_out_loss(o_f, o_b, proj, x2d, tgt, gla_norm_g, conv_w_full, conv_b,
                                                        w_out_full, final_g.reshape(1, D_MODEL), tmix)

    dp_gates, do, dconv, dgg_p, dcb_p = _mix_bwd(dx2b, o_f, o_b, proj, conv, gla_norm_g, w_out_full, tmix)
    dp_ch, dcw_p = _conv_bwd(dconv, proj, conv_w_full, tmix)
    dqkv_f, dlr_f, dqkv_b, dlr_b, dwf_p, dwb_p, dbf_p, dbb_p = _gla_bwd(
        proj, lr, do, st_f, st_b, wgk_f_pad, wgk_b_pad, b_gk_f, b_gk_b, tt)
    dp_qkv, dlr = _sum_directions(dqkv_f, dqkv_b, dlr_f, dlr_b, tm)
    dw_nat = _weight_grad_in(h_t, dp_qkv, dp_gates, dp_ch, dlr)

    dw_out, sib_in = _weight_grad_out(y_t, dx2b, tm, dw_nat)
    part_out = dw_out.reshape(N_DEV, MIX_W // N_DEV, D_MODEL)
    core = jnp.reshape(pc, (1,)).astype(jnp.int32)
    chip = jnp.reshape(2 * px + py, (1,)).astype(jnp.int32)
    sums_in, sib_out = _chip_sums(dw_nat, sib_in, core, 256, "chip_sums_in", riding=part_out)
    sums_out = _chip_sums(part_out, sib_out, core, 256, "chip_sums_out")
    grad_x2d, dng_p, far_in, far_out = _input_grad(dp_qkv, dp_gates, dp_ch, dlr, w_nat, x2d, norm_g, dx2,
                                                   [sums_in, sums_out], tmix)
    pieces = [dng_p, dbf_p, dbb_p, dgg_p, dcb_p, dfg_p[0], dwf_p[0:RANK], dwb_p[RANK:2 * RANK], dcw_p[0:3], loss_p[0]]
    g_window, small_tot = _final_sum(sums_in, far_in, chip, _pack(pieces), 256, "final_sum_in")
    g_in_t = lax.dynamic_slice_in_dim(g_window, 4 * pc, SHARD_W, axis=0)
    g_w_out, d_w_out, nm_w_out, nv_w_out = _final_sum_adamw(sums_out, far_out, chip, w_out[0], m_w_out[0], v_w_out[0],
                                                            256, "adamw_out")
    flat = lambda a: a[0].T.reshape(SHARD_W, D_MODEL // 128, 128)
    unflat = lambda a: a.reshape(SHARD_W, D_MODEL).T
    d_flat, m_flat, v_flat = _adamw_rows(g_in_t.reshape(SHARD_W, D_MODEL // 128, 128), flat(w_in), flat(m_w_in),
                                         flat(v_w_in), 90, "adamw_in")
    g_w_in, d_w_in, nm_w_in, nv_w_in = g_in_t.T, unflat(d_flat), unflat(m_flat), unflat(v_flat)

    tot = _unpack(small_tot, pieces)
    g_norm_g, g_b_gk_f, g_b_gk_b, g_gla, g_conv_b, g_final = tot[:6]
    g_wgk_f = lax.dynamic_slice_in_dim(tot[6], me * wgk_cols, wgk_cols, axis=1)[None]
    g_wgk_b = lax.dynamic_slice_in_dim(tot[7], me * wgk_cols, wgk_cols, axis=1)[None]
    g_conv_w = lax.dynamic_slice_in_dim(tot[8], me * 128, 128, axis=1)[None]
    loss = tot[9][0]

    small_g = [g_norm_g, g_b_gk_f, g_b_gk_b, g_gla, g_conv_b, g_final, g_wgk_f, g_wgk_b, g_conv_w]
    small_w = [norm_g, b_gk_f, b_gk_b, gla_norm_g, conv_b, final_g, w_gk_f, w_gk_b, conv_w]
    small_m = [m_norm_g, m_b_gk_f, m_b_gk_b, m_gla_norm_g, m_conv_b, m_final_g, m_w_gk_f, m_w_gk_b, m_conv_w]
    small_v = [v_norm_g, v_b_gk_f, v_b_gk_b, v_gla_norm_g, v_conv_b, v_final_g, v_w_gk_f, v_w_gk_b, v_conv_w]
    d_s, m_s, v_s = _adamw_small(_pack(small_g), _pack(small_w), _pack(small_m), _pack(small_v))
    d_l, m_l, v_l = _unpack(d_s, small_w), _unpack(m_s, small_w), _unpack(v_s, small_w)

    def ordered(sm, big_in, big_out):
        return [sm[0], big_in[None], sm[6], sm[1], sm[7], sm[2], sm[3], sm[8], sm[4], big_out[None], sm[5]]

    grads = ordered(small_g, g_w_in, g_w_out)
    deltas = ordered(d_l, d_w_in, d_w_out)
    new_m = ordered(m_l, nm_w_in, nm_w_out)
    new_v = ordered(v_l, nv_w_in, nv_w_out)
    return (loss, grad_x2d[None], *grads, *deltas, *new_m, *new_v)
```

```python
import jax
import jax.numpy as jnp
from jax import lax
from jax.experimental import pallas as pl
from jax.experimental.pallas import tpu as pltpu

F32 = jnp.float32
BF16 = jnp.bfloat16
MESH = pl.DeviceIdType.MESH

N_DEV = 8
D_MODEL = 1024
HEADS = 4
DK = 128
DV = 256
QK_W = HEADS * DK
V_W = HEADS * DV
CONV_W = 1024
MIX_W = V_W + CONV_W
CHUNK = 64
RANK = 16
IN_W = 7200
SHARD_W = IN_W // N_DEV
MAIN_W = 7168
LR_W = 128
OFF_Q, OFF_K, OFF_V, OFF_ZA, OFF_B, OFF_ZC, OFF_C, OFF_H = 0, 512, 1024, 2048, 3072, 4096, 5120, 6144
QKV_W, GATES_W, CH_W = 2048, 3072, 2048
NAT_ZA, NAT_LR, NAT_B, NAT_C, NAT_ZC = 2048, 3072, 3104, 4128, 6176
EPS = 1e-6
GATE_SCALE = 1.0 / 16.0
QSCALE = DK ** -0.5
REF_F, LAST_F = CHUNK // 2, CHUNK - 1
REF_B, LAST_B = CHUNK - 1 - CHUNK // 2, 0

ADAM_LR = 0.001
ADAM_B1 = 0.9
ADAM_B2 = 0.999
ADAM_EPS = 1e-08
ADAM_WD = 0.01
ADAM_STEP = 10

VMEM_LIMIT = 56 * 1024 * 1024


def _cparams(*sem):
    return pltpu.CompilerParams(dimension_semantics=sem, vmem_limit_bytes=VMEM_LIMIT)


def _dot(a, b):
    return jnp.dot(a, b, preferred_element_type=F32)


def _dot_nt(a, b):
    return lax.dot_general(a, b, (((1,), (1,)), ((), ())), preferred_element_type=F32)


def _dot_tn(a, b):
    return lax.dot_general(a, b, (((0,), (0,)), ((), ())), preferred_element_type=F32)


def _sigmoid(z):
    return jax.nn.sigmoid(z)


def _position():
    return lax.axis_index("x"), lax.axis_index("y"), lax.axis_index("c")


def _blk(px, py, pc):
    return 4 * px + 2 * py + pc


EDGE = 16
SHIFTED_ROWS = 912
BODY_ROWS = SHIFTED_ROWS - 2 * EDGE


def _first_tile_row(blk, px):
    return EDGE * (56 * blk + px)


def _edge_tiles():
    tiles = {}
    for blk in range(N_DEV):
        first = _first_tile_row(blk, blk // 4)
        tiles.setdefault(first, []).append((blk, 0))
        tiles.setdefault(first + EDGE + BODY_ROWS, []).append((blk, 1))
    return tiles


def _peer_copies(srcs, outs, send_sems, recv_sems):
    x, y, c = _position()
    me = _blk(x, y, c)
    copies = []
    for a, (src, out) in enumerate(zip(srcs, outs)):
        k = 0
        for dx in (0, 1):
            for dy in (0, 1):
                for dc in (0, 1):
                    if dx + dy + dc == 0:
                        continue
                    peer = (1 - x if dx else x, 1 - y if dy else y, 1 - c if dc else c)
                    copies.append(pltpu.make_async_remote_copy(
                        src_ref=src, dst_ref=out.at[me], send_sem=send_sems.at[a * 7 + k],
                        recv_sem=recv_sems.at[a * 7 + k], device_id=peer, device_id_type=MESH))
                    k += 1
    return copies


def _route_chips():
    x, y, c = _position()
    along_x = c == 0
    return [(jnp.where(along_x, 1 - x, x), jnp.where(along_x, y, 1 - y)),
            (jnp.where(along_x, x, 1 - x), jnp.where(along_x, 1 - y, y)), (1 - x, 1 - y)]


WINDOW_ROWS = SHARD_W + 4


def _window_start(k, parity):
    return 2 * SHARD_W * k + (SHARD_W - 4) * parity


def _owner_block(part, k, parity):
    if part.ndim == 3:
        return part.at[2 * k + parity]
    return part.at[pl.ds(pl.multiple_of(_window_start(k, parity), 8), WINDOW_ROWS)]


def _block_shape(part):
    return part.shape[1:] if part.ndim == 3 else (WINDOW_ROWS, part.shape[1])


def _sibling_copies(part, out, send_sems, recv_sems):
    x, y, c = _position()
    return [pltpu.make_async_remote_copy(src_ref=_owner_block(part, k, 1 - c), dst_ref=out.at[k],
                                         send_sem=send_sems.at[k], recv_sem=recv_sems.at[k],
                                         device_id=(x, y, 1 - c), device_id_type=MESH)
            for k in range(4)]


def _start_all(copies):
    for cp in copies:
        cp.start()


def _wait_all(copies):
    for cp in copies:
        cp.wait_recv()
    for cp in copies:
        cp.wait_send()


def _chip_sums(part, from_sibling, core, tc, name, riding=None):
    rows, cols = _block_shape(part)
    nj = cols // tc

    def body(core_ref, p_ref, s_ref, *rest):
        if riding is None:
            (o_ref,) = rest
        else:
            ride_in, o_ref, ride_out, send_sems, recv_sems = rest
            k, j = pl.program_id(0), pl.program_id(1)

            @pl.when(jnp.logical_and(k == 0, j == 0))
            def _():
                _start_all(_sibling_copies(ride_in, ride_out, send_sems, recv_sems))

        o_ref[0] = (p_ref[...].reshape(rows, tc) + s_ref[0]).astype(BF16)

        if riding is not None:
            @pl.when(jnp.logical_and(k == 3, j == nj - 1))
            def _():
                _wait_all(_sibling_copies(ride_in, ride_out, send_sems, recv_sems))

    hbm = pl.BlockSpec(memory_space=pl.ANY)
    sums = jax.ShapeDtypeStruct((4, rows, cols), BF16)
    tile_out = pl.BlockSpec((1, rows, tc), lambda k, j, core_ref: (k, 0, j))
    if part.ndim == 3:
        mine = pl.BlockSpec((1, rows, tc), lambda k, j, core_ref: (2 * k + core_ref[0], 0, j))
    else:
        mine = pl.BlockSpec((pl.Element(rows), pl.Element(tc)),
                            lambda k, j, core_ref: (pl.multiple_of(_window_start(k, core_ref[0]), 8),
                                                    pl.multiple_of(j * tc, 128)))
    in_specs = [mine, pl.BlockSpec((1, rows, tc), lambda k, j, core_ref: (k, 0, j))]
    if riding is None:
        out_shape, out_specs, scratch, args = sums, tile_out, [], (core, part, from_sibling)
    else:
        out_shape = (sums, jax.ShapeDtypeStruct((4,) + _block_shape(riding), F32))
        out_specs, in_specs = (tile_out, hbm), in_specs + [hbm]
        scratch = [pltpu.SemaphoreType.DMA((4,)), pltpu.SemaphoreType.DMA((4,))]
        args = (core, part, from_sibling, riding)
    return pl.pallas_call(
        body, name=name, out_shape=out_shape,
        grid_spec=pltpu.PrefetchScalarGridSpec(num_scalar_prefetch=1, grid=(4, nj), in_specs=in_specs,
                                               out_specs=out_specs, scratch_shapes=scratch),
        compiler_params=_cparams("arbitrary", "arbitrary"),
    )(*args)


def _sum_chips(s_ref, r_ref):
    f = lambda a: a.astype(F32)
    return (f(s_ref[0]) + f(r_ref[0])) + f(r_ref[1])


def _final_sum(sums, from_chips, chip, small, tc, name):
    _, rows, cols = sums.shape
    nj = cols // tc

    def body(chip_ref, s_ref, r_ref, sm_ref, g_out, tot_ref, all_ref, send_sems, recv_sems):
        j = pl.program_id(0)
        me = _blk(*_position())

        @pl.when(j == 0)
        def _():
            all_ref[me] = sm_ref[...]
            _start_all(_peer_copies((all_ref.at[me],), (all_ref,), send_sems, recv_sems))

        g_out[...] = _sum_chips(s_ref, r_ref)

        @pl.when(j == nj - 1)
        def _():
            _wait_all(_peer_copies((all_ref.at[me],), (all_ref,), send_sems, recv_sems))
            acc = all_ref[0]
            for d in range(1, N_DEV):
                acc = acc + all_ref[d]
            tot_ref[...] = acc

    whole = pl.BlockSpec(small.shape, lambda j, chip_ref: (0, 0))
    return pl.pallas_call(
        body, name=name,
        out_shape=(jax.ShapeDtypeStruct((rows, cols), F32), jax.ShapeDtypeStruct(small.shape, F32)),
        grid_spec=pltpu.PrefetchScalarGridSpec(
            num_scalar_prefetch=1, grid=(nj,),
            in_specs=[pl.BlockSpec((1, rows, tc), lambda j, chip_ref: (chip_ref[0], 0, j)),
                      pl.BlockSpec((2, rows, tc), lambda j, chip_ref: (0, 0, j)), whole],
            out_specs=(pl.BlockSpec((rows, tc), lambda j, chip_ref: (0, j)), whole),
            scratch_shapes=[pltpu.VMEM((N_DEV,) + small.shape, F32), pltpu.SemaphoreType.DMA((7,)),
                            pltpu.SemaphoreType.DMA((7,))]),
        compiler_params=_cparams("arbitrary"),
    )(chip, sums, from_chips, small)


def _adamw_rows(g, w, m, v, tr, name):
    rows = g.shape[0]

    def body(g_ref, w_ref, m_ref, v_ref, d_out, m_out, v_out):
        delta, m_new, v_new = _adamw(w_ref[...], g_ref[...], m_ref[...], v_ref[...])
        d_out[...] = delta
        m_out[...] = m_new
        v_out[...] = v_new

    tile = pl.BlockSpec((tr,) + g.shape[1:], lambda r: (r, 0, 0))
    shp = jax.ShapeDtypeStruct(g.shape, F32)
    return pl.pallas_call(
        body, name=name, out_shape=(shp, shp, shp), grid=(rows // tr,),
        in_specs=[tile] * 4, out_specs=(tile, tile, tile),
        compiler_params=_cparams("arbitrary"),
    )(g, w, m, v)


def _adamw(w, g, m, v):
    m = ADAM_B1 * m + (1.0 - ADAM_B1) * g
    v = ADAM_B2 * v + (1.0 - ADAM_B2) * (g * g)
    m_hat = m / (1.0 - ADAM_B1 ** ADAM_STEP)
    v_hat = v / (1.0 - ADAM_B2 ** ADAM_STEP)
    delta = -ADAM_LR * (m_hat / (jnp.sqrt(v_hat) + ADAM_EPS) + ADAM_WD * w)
    return delta, m, v


def _final_sum_adamw(sums, from_chips, chip, w, m, v, tr, name):
    rows, cols = w.shape

    def body(chip_ref, s_ref, r_ref, w_ref, m_ref, v_ref, g_out, d_out, m_out, v_out):
        g = _sum_chips(s_ref, r_ref)
        delta, m_new, v_new = _adamw(w_ref[...], g, m_ref[...], v_ref[...])
        g_out[...] = g
        d_out[...] = delta
        m_out[...] = m_new
        v_out[...] = v_new

    tile = pl.BlockSpec((tr, cols), lambda r, chip_ref: (r, 0))
    shp = jax.ShapeDtypeStruct((rows, cols), F32)
    return pl.pallas_call(
        body, name=name,
        out_shape=(shp, shp, shp, shp),
        grid_spec=pltpu.PrefetchScalarGridSpec(
            num_scalar_prefetch=1, grid=(rows // tr,),
            in_specs=[pl.BlockSpec((1, tr, cols), lambda r, chip_ref: (chip_ref[0], r, 0)),
                      pl.BlockSpec((2, tr, cols), lambda r, chip_ref: (0, r, 0)),
                      tile, tile, tile],
            out_specs=(tile, tile, tile, tile)),
        compiler_params=_cparams("arbitrary"),
    )(chip, sums, from_chips, w, m, v)


def _adamw_small(g, w, m, v):
    def body(g_ref, w_ref, m_ref, v_ref, d_out, m_out, v_out):
        delta, m_new, v_new = _adamw(w_ref[...], g_ref[...], m_ref[...], v_ref[...])
        d_out[...] = delta
        m_out[...] = m_new
        v_out[...] = v_new

    vmem = pl.BlockSpec(memory_space=pltpu.VMEM)
    shp = jax.ShapeDtypeStruct(g.shape, F32)
    return pl.pallas_call(body, name="adamw_small", out_shape=(shp, shp, shp),
                          in_specs=[vmem] * 4, out_specs=(vmem, vmem, vmem))(g, w, m, v)


TILE_ROWS = (0, 1024, NAT_ZA, NAT_B, NAT_ZC, NAT_C, NAT_C + CONV_W)


TILE_ORDER = ((0, 1, 2, 3, 5, 6, 4), (2, 0, 1, 4, 3, 5, 6), (5, 0, 6, 4, 1, 2, 3), (4, 2, 3, 5, 6, 0, 1))
NEIGHBOUR_SWEEP, DIAGONAL_SWEEP = 1, 4
PIECES, W_IN_PIECES, OTHER_PIECES = 4, (0, 1), (2, 3)


def _gather_inproj(x2d, norm_g, shifted, w_out_s, small_s, order, tm):
    seq = x2d.shape[0]
    tn = CONV_W
    ni, nj = seq // tm, MAIN_W // tn
    first_sweep = lambda j, i, order_ref: jnp.where(j == 0, i, ni - 1)
    last_sweep = lambda j, i, order_ref: jnp.where(j == nj - 1, i, 0)
    edge_tiles = _edge_tiles()

    def body(order_ref, x_ref, g_ref, sh_ref, wout_ref, sm_ref, proj_ref, lr_ref, ht_ref, w_nat, wout_all, sm_all,
             w_all, h_all, edges, wout_b, sm_b, send_sems, recv_sems, local_sems):
        j, i = pl.program_id(0), pl.program_id(1)
        rows = pl.ds(pl.multiple_of(i * tm, tm), tm)
        x, y, c = _position()
        me, here, sibling = _blk(x, y, c), (x, y, c), (x, y, 1 - c)
        chips = _route_chips()
        sibling_chips = [chips[1], chips[0], chips[2]]

        def pieces(px, py, pc):
            blk = _blk(px, py, pc)
            body_rows = pl.ds(pl.multiple_of(_first_tile_row(blk, px) + EDGE, EDGE), BODY_ROWS)
            return [w_all.at[body_rows], edges.at[blk], wout_all.at[blk], sm_all.at[blk]]

        def copy(a, k, block, to, staged=None):
            ref = pieces(*block)[a]
            return pltpu.make_async_remote_copy(src_ref=ref if staged is None else staged, dst_ref=ref,
                                                send_sem=send_sems.at[a * 7 + k], recv_sem=recv_sems.at[a * 7 + k],
                                                device_id=to, device_id_type=MESH)

        def own_copies(group):
            targets = [(0, sibling)] + [(1 + n, (*chips[n], c)) for n in range(2)]
            staged = [None, None, wout_b, sm_b]
            return [copy(a, k, here, to, staged[a]) for k, to in targets for a in group]

        def relays(group):
            return [copy(a, 3, (*chips[0], c), (*chips[1], c)) for a in group]

        def forwards(n, group):
            return [copy(a, 4 + n, (*chips[n], c), sibling) for a in group]

        def keep_own():
            return [pltpu.make_async_copy(wout_b, wout_all.at[me], local_sems.at[0]),
                    pltpu.make_async_copy(sm_b, sm_all.at[me], local_sems.at[1])]

        def keep_weight():
            return pltpu.make_async_copy(w_all, w_nat, local_sems.at[2])

        def arrive(ns, group):
            for n in ns:
                for a in group:
                    copy(a, 1 + n, (*chips[n], c), here).wait_recv()
                _start_all((relays(group) if n == 0 else []) + forwards(n, group))
            for n in ns:
                for a in group:
                    copy(a, 4 + n, (*sibling_chips[n], 1 - c), here).wait_recv()

        def add_edge_tiles(stage):
            for row, parts in edge_tiles.items():
                ready = 0
                for blk, _ in parts:
                    away = (x != blk // 4).astype(jnp.int32) + (y != (blk // 2) % 2).astype(jnp.int32)
                    ready = jnp.maximum(ready, away)

                @pl.when(ready == stage)
                def _(row=row, parts=parts):
                    tile = edges[parts[0][0], parts[0][1]].astype(F32)
                    for blk, side in parts[1:]:
                        tile = tile + edges[blk, side].astype(F32)
                    w_all[row:row + EDGE, :] = tile.astype(BF16)

        @pl.when(jnp.logical_and(j == 0, i == 0))
        def _():
            pieces(*here)[0][...] = sh_ref[EDGE:EDGE + BODY_ROWS, :].astype(BF16)
            edges[me, 0] = sh_ref[0:EDGE, :].astype(BF16)
            edges[me, 1] = sh_ref[EDGE + BODY_ROWS:, :].astype(BF16)
            _start_all(own_copies(W_IN_PIECES))
            wout_b[...] = wout_ref[...].astype(BF16)
            sm_b[...] = sm_ref[...]
            _start_all(own_copies(OTHER_PIECES) + keep_own())
            for a in W_IN_PIECES:
                copy(a, 0, sibling, here).wait_recv()
            add_edge_tiles(0)

        @pl.when(jnp.logical_and(j == NEIGHBOUR_SWEEP, i == 0))
        def _():
            arrive((0, 1), W_IN_PIECES)
            add_edge_tiles(1)

        @pl.when(jnp.logical_and(j == DIAGONAL_SWEEP, i == 0))
        def _():
            arrive((2,), W_IN_PIECES)
            add_edge_tiles(2)
            keep_weight().start()
            arrive((0, 1), OTHER_PIECES)

        @pl.when(jnp.logical_and(j == nj - 1, i == 0))
        def _():
            arrive((2,), OTHER_PIECES)

        @pl.when(j == 0)
        def _():
            xv = x_ref[...]
            r = lax.rsqrt(jnp.mean(xv * xv, axis=-1, keepdims=True) + EPS)
            h = (xv * r) * g_ref[...]
            h_all[rows, :] = h.astype(BF16)
            ht_ref[...] = h.T.astype(BF16)

        tile = order_ref[j]
        row = 0
        for k, start in enumerate(TILE_ROWS):
            row = row + jnp.where(tile == k, start // 32, 0)
        w_tile = w_all[pl.ds(pl.multiple_of(row * 32, 32), tn), :]
        proj_ref[...] = _dot_nt(h_all[rows, :], w_tile).astype(BF16)

        @pl.when(j == nj - 1)
        def _():
            lr_ref[...] = _dot_nt(h_all[rows, :], w_all[NAT_LR:NAT_LR + LR_W, :])

        @pl.when(jnp.logical_and(j == nj - 1, i == ni - 1))
        def _():
            everything = range(PIECES)
            passed_on = [cp for n in range(3) for cp in forwards(n, everything)]
            for cp in own_copies(everything) + relays(everything) + passed_on:
                cp.wait_send()
            for a in OTHER_PIECES:
                copy(a, 0, sibling, here).wait_recv()
            for cp in keep_own() + [keep_weight()]:
                cp.wait()

    const = lambda shape: pl.BlockSpec(shape, lambda j, i, order_ref: (0,) * len(shape))
    hbm = pl.BlockSpec(memory_space=pl.ANY)
    vmem = pl.BlockSpec(memory_space=pltpu.VMEM)
    return pl.pallas_call(
        body, name="gather_inproj",
        out_shape=(jax.ShapeDtypeStruct((seq, MAIN_W), BF16), jax.ShapeDtypeStruct((seq, LR_W), F32),
                   jax.ShapeDtypeStruct((D_MODEL, seq), BF16), jax.ShapeDtypeStruct((IN_W, D_MODEL), BF16),
                   jax.ShapeDtypeStruct((N_DEV,) + w_out_s.shape, BF16),
                   jax.ShapeDtypeStruct((N_DEV,) + small_s.shape, F32)),
        grid_spec=pltpu.PrefetchScalarGridSpec(
            num_scalar_prefetch=1, grid=(nj, ni),
            in_specs=[pl.BlockSpec((tm, D_MODEL), lambda j, i, order_ref: (first_sweep(j, i, order_ref), 0)),
                      const((1, D_MODEL)), vmem, const(w_out_s.shape), const(small_s.shape)],
            out_specs=(pl.BlockSpec((tm, tn), lambda j, i, order_ref: (i, order_ref[j])),
                       pl.BlockSpec((tm, LR_W), lambda j, i, order_ref: (last_sweep(j, i, order_ref), 0)),
                       pl.BlockSpec((D_MODEL, tm), lambda j, i, order_ref: (0, first_sweep(j, i, order_ref))),
                       hbm, hbm, hbm),
            scratch_shapes=[pltpu.VMEM((IN_W, D_MODEL), BF16), pltpu.VMEM((seq, D_MODEL), BF16),
                            pltpu.VMEM((N_DEV, 2, EDGE, D_MODEL), BF16),
                            pltpu.VMEM(w_out_s.shape, BF16), pltpu.VMEM(small_s.shape, F32),
                            pltpu.SemaphoreType.DMA((7 * PIECES,)), pltpu.SemaphoreType.DMA((7 * PIECES,)),
                            pltpu.SemaphoreType.DMA((3,))]),
        compiler_params=_cparams("arbitrary", "arbitrary"),
    )(order, x2d, norm_g, shifted, w_out_s, small_s)


def _block_masks(tt):
    row = lax.broadcasted_iota(jnp.int32, (tt, tt), 0)
    col = lax.broadcasted_iota(jnp.int32, (tt, tt), 1)
    same = jnp.right_shift(row, 6) == jnp.right_shift(col, 6)
    return (jnp.logical_and(same, col <= row), jnp.logical_and(same, col >= row), jnp.logical_and(same, col > row))


def _dot_split3(ones_mat, x):
    x1 = x.astype(BF16)
    r1 = x - x1.astype(F32)
    x2 = r1.astype(BF16)
    x3 = (r1 - x2.astype(F32)).astype(BF16)
    return (_dot(ones_mat, x3) + _dot(ones_mat, x2)) + _dot(ones_mat, x1)


def _log_gate(logits):
    return (jnp.minimum(logits, 0.0) - jnp.log(1.0 + jnp.exp(-jnp.abs(logits)))) * GATE_SCALE


def _chunk_column_mask(tt):
    nc = tt // CHUNK
    row = lax.broadcasted_iota(jnp.int32, (tt, nc * DK), 0)
    col = lax.broadcasted_iota(jnp.int32, (tt, nc * DK), 1)
    return jnp.right_shift(row, 6) == jnp.right_shift(col, 7)


def _chunked(mask, x, nc):
    wide = jnp.concatenate([x] * nc, axis=1)
    return jnp.where(mask, wide, jnp.zeros_like(wide))


def _gla_fwd(proj, lr, wgk_f, wgk_b, bgk_f, bgk_b, tt):
    seq = proj.shape[0]
    nb, nc, nch = seq // tt, tt // CHUNK, seq // CHUNK

    def body(qf, kf, vf, lrf, qb, kb, vb, lrb, wf, wb, bf, bb, of, ob, stf, stb, s_scr, qs_s, ks_s, qin_s, kout_s):
        @pl.when(pl.program_id(0) == 0)
        def _():
            s_scr[...] = jnp.zeros(s_scr.shape, F32)

        low, upp, sup = _block_masks(tt)
        dirs = ((qf, kf, vf, lrf, wf, bf, of, stf, low, low, REF_F, LAST_F, list(range(nc))),
                (qb, kb, vb, lrb, wb, bb, ob, stb, upp, sup, REF_B, LAST_B, list(reversed(range(nc)))))
        for d, (q_r, k_r, v_r, lr_r, w_r, b_r, o_r, st_r, cum, mask, ref, last, order) in enumerate(dirs):
            logits = _dot(lr_r[...].astype(BF16), w_r[...]) + b_r[...]
            b = _dot_split3(cum.astype(BF16), _log_gate(logits))
            decs = []
            for c in range(nc):
                rows = slice(c * CHUNK, (c + 1) * CHUNK)
                bc = b[rows]
                b_ref, b_last = bc[ref:ref + 1], bc[last:last + 1]
                qc = q_r[rows, :].astype(F32) * QSCALE
                kc = k_r[rows, :].astype(F32)
                qs_s[rows, :] = (qc * jnp.exp(bc - b_ref)).astype(BF16)
                ks_s[rows, :] = (kc * jnp.exp(b_ref - bc)).astype(BF16)
                qin_s[rows, :] = (qc * jnp.exp(bc)).astype(BF16)
                kout_s[rows, :] = (kc * jnp.exp(b_last - bc)).astype(BF16)
                decs.append(jnp.exp(b_last))
            for h in range(HEADS):
                ksl = slice(h * DK, (h + 1) * DK)
                vsl = slice(h * DV, (h + 1) * DV)
                v = v_r[:, vsl].astype(BF16)
                att = jnp.where(mask, _dot_nt(qs_s[:, ksl], ks_s[:, ksl]), 0.0).astype(BF16)
                o_intra = _dot(att, v)
                st = s_scr[d * HEADS + h]
                for c in order:
                    rows = slice(c * CHUNK, (c + 1) * CHUNK)
                    stb = st.astype(BF16)
                    st_r[c, h] = stb
                    o_r[rows, vsl] = (o_intra[rows] + _dot_nt(qin_s[rows, ksl], stb)).astype(BF16)
                    st = st * decs[c][:, ksl] + _dot_tn(v[rows], kout_s[rows, ksl])
                s_scr[d * HEADS + h] = st

    fw = lambda i: (i, 0)
    bw = lambda i: (nb - 1 - i, 0)
    const = lambda i: (0, 0)

    def tok_specs(m):
        return [pl.BlockSpec((tt, QK_W), lambda i: (m(i)[0], OFF_Q // QK_W)),
                pl.BlockSpec((tt, QK_W), lambda i: (m(i)[0], OFF_K // QK_W)),
                pl.BlockSpec((tt, V_W), lambda i: (m(i)[0], OFF_V // V_W)),
                pl.BlockSpec((tt, LR_W), m)]

    st_shape = jax.ShapeDtypeStruct((nch, HEADS, DV, DK), BF16)
    o_shape = jax.ShapeDtypeStruct((seq, V_W), BF16)
    operand = pltpu.VMEM((tt, QK_W), BF16)
    return pl.pallas_call(
        body, name="gla_fwd",
        out_shape=(o_shape, o_shape, st_shape, st_shape),
        grid=(nb,),
        in_specs=tok_specs(fw) + tok_specs(bw) + [
            pl.BlockSpec((LR_W, QK_W), const), pl.BlockSpec((LR_W, QK_W), const),
            pl.BlockSpec((1, QK_W), const), pl.BlockSpec((1, QK_W), const)],
        out_specs=(pl.BlockSpec((tt, V_W), fw), pl.BlockSpec((tt, V_W), bw),
                   pl.BlockSpec((nc, HEADS, DV, DK), lambda i: (i, 0, 0, 0)),
                   pl.BlockSpec((nc, HEADS, DV, DK), lambda i: (nb - 1 - i, 0, 0, 0))),
        scratch_shapes=[pltpu.VMEM((2 * HEADS, DV, DK), F32), operand, operand, operand, operand],
        compiler_params=_cparams("arbitrary"),
    )(proj, proj, proj, lr, proj, proj, proj, lr, wgk_f, wgk_b, bgk_f, bgk_b)


def _head_norm(o, gain):
    outs, rinv = [], []
    for h in range(HEADS):
        oh = o[:, h * DV:(h + 1) * DV]
        r = lax.rsqrt(jnp.mean(oh * oh, axis=-1, keepdims=True) + EPS)
        outs.append((oh * r) * gain)
        rinv.append(r)
    return jnp.concatenate(outs, axis=1), rinv


def _shift_rows(u, prev_row, next_row):
    n = u.shape[0]
    row = lax.broadcasted_iota(jnp.int32, (n, 1), 0)
    up = jnp.where(row == 0, prev_row, pltpu.roll(u, 1, 0))
    un = jnp.where(row == n - 1, next_row, pltpu.roll(u, n - 1, 0))
    return up, un


HALO = 16


def _halo_specs(tm, seq, col_block):
    per = tm // HALO
    last = seq // HALO - 1
    return [pl.BlockSpec((HALO, CONV_W), lambda i: (jnp.maximum(i * per - 1, 0), col_block)),
            pl.BlockSpec((HALO, CONV_W), lambda i: (jnp.minimum((i + 1) * per, last), col_block))]


def _f32(ref):
    return ref[...].astype(F32)


def _last_row(ref):
    return ref[HALO - 1:HALO, :].astype(F32)


def _first_row(ref):
    return ref[0:1, :].astype(F32)


def _mix_out_loss(o_f, o_b, proj, x2d, tgt, gla_g, conv_w, conv_b, w_out, final_g, tm):
    seq = x2d.shape[0]
    nt = seq // tm

    def body(of, ob, za, bg, cg, hc, zc, cprev, cnext, hprev, hnext, x_ref, t_ref, gg, cw, cb, wo, fg,
             yt_ref, conv_ref, dx2_ref, dx2b_ref, loss_ref, dfg_ref):
        i = pl.program_id(0)

        @pl.when(i == 0)
        def _():
            loss_ref[...] = jnp.zeros(loss_ref.shape, F32)
            dfg_ref[...] = jnp.zeros(dfg_ref.shape, F32)

        on, _ = _head_norm(_f32(of) + _f32(ob), gg[...])
        zav = _f32(za)
        y_a = on * (zav * _sigmoid(zav))
        u = _f32(cg) * _f32(hc)
        prev_row = jnp.where(i > 0, _last_row(cprev) * _last_row(hprev), 0.0)
        next_row = jnp.where(i < nt - 1, _first_row(cnext) * _first_row(hnext), 0.0)
        up, un = _shift_rows(u, prev_row, next_row)
        conv = (cw[0:1, :] * up + cw[1:2, :] * u + cw[2:3, :] * un) + cb[...]
        conv_ref[...] = conv.astype(BF16)
        zcv = _f32(zc)
        y_c = _f32(bg) * conv * (zcv * _sigmoid(zcv))
        y = jnp.concatenate([y_a, y_c], axis=1)
        yt_ref[...] = y.T.astype(BF16)
        x2 = x_ref[...] + _dot(y.astype(BF16), wo[...])
        r = lax.rsqrt(jnp.mean(x2 * x2, axis=-1, keepdims=True) + EPS)
        xn = x2 * r
        err = xn * fg[...] - t_ref[...]
        loss_ref[...] += 0.5 * jnp.sum(jnp.mean(err * err, axis=-1, keepdims=True))
        dyf = err * (1.0 / D_MODEL)
        dfg_ref[...] += jnp.sum(dyf * xn, axis=0, keepdims=True)
        dxn = dyf * fg[...]
        dx2 = r * dxn - xn * (r * jnp.mean(dxn * xn, axis=-1, keepdims=True))
        dx2_ref[...] = dx2
        dx2b_ref[...] = dx2.astype(BF16)

    def col(off):
        return pl.BlockSpec((tm, CONV_W), lambda i: (i, off // CONV_W))

    rowt = pl.BlockSpec((tm, D_MODEL), lambda i: (i, 0))
    const = lambda shape: pl.BlockSpec(shape, lambda i: (0, 0))
    return pl.pallas_call(
        body, name="mix_out_loss",
        out_shape=(jax.ShapeDtypeStruct((MIX_W, seq), BF16), jax.ShapeDtypeStruct((seq, CONV_W), BF16),
                   jax.ShapeDtypeStruct((seq, D_MODEL), F32), jax.ShapeDtypeStruct((seq, D_MODEL), BF16),
                   jax.ShapeDtypeStruct((8, 128), F32), jax.ShapeDtypeStruct((1, D_MODEL), F32)),
        grid=(nt,),
        in_specs=[rowt, rowt, col(OFF_ZA), col(OFF_B), col(OFF_C), col(OFF_H), col(OFF_ZC)]
        + _halo_specs(tm, seq, OFF_C // CONV_W) + _halo_specs(tm, seq, OFF_H // CONV_W)
        + [rowt, rowt, const((1, DV)), const((8, CONV_W)), const((1, CONV_W)), const((MIX_W, D_MODEL)),
           const((1, D_MODEL))],
        out_specs=(pl.BlockSpec((MIX_W, tm), lambda i: (0, i)), rowt, rowt, rowt, const((8, 128)),
                   const((1, D_MODEL))),
        compiler_params=_cparams("arbitrary"),
    )(o_f, o_b, proj, proj, proj, proj, proj, proj, proj, proj, proj, x2d, tgt, gla_g, conv_w, conv_b, w_out, final_g)


def _dsilu(z, s):
    return s * (1.0 + z * (1.0 - s))


def _mix_bwd(dx2b, o_f, o_b, proj, conv, gla_g, w_out, tm):
    seq = dx2b.shape[0]

    def body(dx, of, ob, za, bg, zc, cv, gg, wo, dg_ref, do_ref, dconv_ref, dgg_ref, dcb_ref):
        @pl.when(pl.program_id(0) == 0)
        def _():
            dgg_ref[...] = jnp.zeros(dgg_ref.shape, F32)
            dcb_ref[...] = jnp.zeros(dcb_ref.shape, F32)

        dy = _dot_nt(dx[...], wo[...])
        dy_a, dy_c = dy[:, :V_W], dy[:, V_W:]
        zcv, bgv, convv = _f32(zc), _f32(bg), _f32(cv)
        sc = _sigmoid(zcv)
        szc = zcv * sc
        dg_ref[:, CONV_W:2 * CONV_W] = (dy_c * convv * szc).astype(BF16)
        dconv = dy_c * bgv * szc
        dconv_ref[...] = dconv.astype(BF16)
        dcb_ref[...] += jnp.sum(dconv, axis=0, keepdims=True)
        dg_ref[:, 2 * CONV_W:] = (dy_c * bgv * convv * _dsilu(zcv, sc)).astype(BF16)

        o = _f32(of) + _f32(ob)
        gain = gg[...]
        on, rinv = _head_norm(o, gain)
        zav = _f32(za)
        sa = _sigmoid(zav)
        dg_ref[:, :CONV_W] = (dy_a * on * _dsilu(zav, sa)).astype(BF16)
        don = dy_a * (zav * sa)
        dgg = jnp.zeros((1, DV), F32)
        dos = []
        for h in range(HEADS):
            sl = slice(h * DV, (h + 1) * DV)
            oh, r, dh = o[:, sl], rinv[h], don[:, sl]
            ohn = oh * r
            dgg = dgg + jnp.sum(dh * ohn, axis=0, keepdims=True)
            dn = dh * gain
            dos.append(r * dn - ohn * (r * jnp.mean(dn * ohn, axis=-1, keepdims=True)))
        dgg_ref[...] += dgg
        do_ref[...] = jnp.concatenate(dos, axis=1).astype(BF16)

    def col(off):
        return pl.BlockSpec((tm, CONV_W), lambda i: (i, off // CONV_W))

    rowt = pl.BlockSpec((tm, D_MODEL), lambda i: (i, 0))
    const = lambda shape: pl.BlockSpec(shape, lambda i: (0, 0))
    return pl.pallas_call(
        body, name="mix_bwd",
        out_shape=(jax.ShapeDtypeStruct((seq, GATES_W), BF16), jax.ShapeDtypeStruct((seq, V_W), BF16),
                   jax.ShapeDtypeStruct((seq, CONV_W), BF16),
                   jax.ShapeDtypeStruct((1, DV), F32), jax.ShapeDtypeStruct((1, CONV_W), F32)),
        grid=(seq // tm,),
        in_specs=[rowt, rowt, rowt, col(OFF_ZA), col(OFF_B), col(OFF_ZC), rowt, const((1, DV)),
                  const((MIX_W, D_MODEL))],
        out_specs=(pl.BlockSpec((tm, GATES_W), lambda i: (i, 0)), rowt, rowt, const((1, DV)), const((1, CONV_W))),
        compiler_params=_cparams("arbitrary"),
    )(dx2b, o_f, o_b, proj, proj, proj, conv, gla_g, w_out)


def _conv_bwd(dconv, proj, conv_w, tm):
    seq = dconv.shape[0]
    nt = seq // tm

    def body(dc_in, dprev, dnext, cg, hc, cprev, cnext, hprev, hnext, cw, dch_ref, dcw_ref):
        i = pl.program_id(0)

        @pl.when(i == 0)
        def _():
            dcw_ref[...] = jnp.zeros(dcw_ref.shape, F32)

        first, lastt = i > 0, i < nt - 1
        dcv = _f32(dc_in)
        d_up, d_un = _shift_rows(dcv, jnp.where(first, _last_row(dprev), 0.0), jnp.where(lastt, _first_row(dnext), 0.0))
        cgv, hcv = _f32(cg), _f32(hc)
        u = cgv * hcv
        u_up, u_un = _shift_rows(u, jnp.where(first, _last_row(cprev) * _last_row(hprev), 0.0),
                                 jnp.where(lastt, _first_row(cnext) * _first_row(hnext), 0.0))
        du = cw[0:1, :] * d_un + cw[1:2, :] * dcv + cw[2:3, :] * d_up
        dch_ref[:, :CONV_W] = (du * hcv).astype(BF16)
        dch_ref[:, CONV_W:] = (du * cgv).astype(BF16)
        dcw_ref[0:1, :] += jnp.sum(dcv * u_up, axis=0, keepdims=True)
        dcw_ref[1:2, :] += jnp.sum(dcv * u, axis=0, keepdims=True)
        dcw_ref[2:3, :] += jnp.sum(dcv * u_un, axis=0, keepdims=True)

    def col(off):
        return pl.BlockSpec((tm, CONV_W), lambda i: (i, off // CONV_W))

    rowt = pl.BlockSpec((tm, CONV_W), lambda i: (i, 0))
    const = lambda shape: pl.BlockSpec(shape, lambda i: (0, 0))
    return pl.pallas_call(
        body, name="conv_bwd",
        out_shape=(jax.ShapeDtypeStruct((seq, CH_W), BF16), jax.ShapeDtypeStruct((8, CONV_W), F32)),
        grid=(nt,),
        in_specs=[rowt] + _halo_specs(tm, seq, 0) + [col(OFF_C), col(OFF_H)]
        + _halo_specs(tm, seq, OFF_C // CONV_W) + _halo_specs(tm, seq, OFF_H // CONV_W) + [const((8, CONV_W))],
        out_specs=(pl.BlockSpec((tm, CH_W), lambda i: (i, 0)), const((8, CONV_W))),
        compiler_params=_cparams("arbitrary"),
    )(dconv, dconv, dconv, proj, proj, proj, proj, proj, proj, conv_w)


def _gla_bwd(proj, lr, do, st_f, st_b, wgk_f, wgk_b, bgk_f, bgk_b, tt):
    seq = proj.shape[0]
    nb, nc = seq // tt, tt // CHUNK

    def body(qf, kf, vf, lrf, dof, stf, qb, kb, vb, lrb, dob, stb, wf, wb, bf, bb,
             dqkv_f, dlr_f, dqkv_b, dlr_b, dwf, dwb, dbf, dbb,
             ds_scr, eq_s, ek_s, ein_s, eout_s, qs_s, ks_s, qin_s, kout_s, db_s, lg_s):
        @pl.when(pl.program_id(0) == 0)
        def _():
            ds_scr[...] = jnp.zeros(ds_scr.shape, F32)
            for r in (dwf, dwb, dbf, dbb):
                r[...] = jnp.zeros(r.shape, F32)

        low, upp, sup = _block_masks(tt)
        row = lax.broadcasted_iota(jnp.int32, (CHUNK, 1), 0)
        kmask = _chunk_column_mask(tt)
        dirs = ((qf, kf, vf, lrf, dof, stf, wf, bf, dqkv_f, dlr_f, dwf, dbf,
                 low, upp, low, REF_F, LAST_F, list(reversed(range(nc)))),
                (qb, kb, vb, lrb, dob, stb, wb, bb, dqkv_b, dlr_b, dwb, dbb,
                 upp, low, sup, REF_B, LAST_B, list(range(nc))))
        for d, (q_r, k_r, v_r, lr_r, do_r, st_r, w_r, b_r, dqkv_r, dlr_r, dw_r, db_r,
                cum, cum_t, mask, ref, last, order) in enumerate(dirs):
            lrv = lr_r[...].astype(BF16)
            wv = w_r[...]
            logits = _dot(lrv, wv) + b_r[...]
            lg_s[...] = logits
            b = _dot_split3(cum.astype(BF16), _log_gate(logits))
            decs = []
            for c in range(nc):
                rows = slice(c * CHUNK, (c + 1) * CHUNK)
                bc = b[rows]
                b_ref, b_last = bc[ref:ref + 1], bc[last:last + 1]
                qc = q_r[rows, :].astype(F32) * QSCALE
                kc = k_r[rows, :].astype(F32)
                e_q, e_k, e_in, e_out = jnp.exp(bc - b_ref), jnp.exp(b_ref - bc), jnp.exp(bc), jnp.exp(b_last - bc)
                eq_s[rows, :], ek_s[rows, :], ein_s[rows, :], eout_s[rows, :] = e_q, e_k, e_in, e_out
                qs_s[rows, :] = (qc * e_q).astype(BF16)
                ks_s[rows, :] = (kc * e_k).astype(BF16)
                qin_s[rows, :] = (qc * e_in).astype(BF16)
                kout_s[rows, :] = (kc * e_out).astype(BF16)
                decs.append(jnp.exp(b_last))
            for h in range(HEADS):
                ksl = slice(h * DK, (h + 1) * DK)
                vsl = slice(h * DV, (h + 1) * DV)
                v = v_r[:, vsl].astype(BF16)
                dov = do_r[:, vsl].astype(BF16)
                qsb, ksb = qs_s[:, ksl], ks_s[:, ksl]
                att = jnp.where(mask, _dot_nt(qsb, ksb), 0.0).astype(BF16)
                datt = jnp.where(mask, _dot_nt(dov, v), 0.0).astype(BF16)
                dqs = _dot(datt, ksb)
                dks = _dot_tn(datt, qsb)
                dv_intra = _dot_tn(att, dov)
                g_t = _dot_tn(dov, _chunked(kmask, qin_s[:, ksl], nc))
                ds = ds_scr[d * HEADS + h]
                for c in order:
                    rows = slice(c * CHUNK, (c + 1) * CHUNK)
                    dsb = ds.astype(BF16)
                    s_prev = st_r[c, h]
                    dk_out = _dot(v[rows], dsb)
                    dq_in = _dot(dov[rows], s_prev)
                    dv = dv_intra[rows] + _dot_nt(kout_s[rows, ksl], dsb)
                    dqkv_r[rows, OFF_V + h * DV:OFF_V + (h + 1) * DV] = dv.astype(BF16)
                    dec = decs[c][:, ksl]
                    ddec = jnp.sum(ds * s_prev.astype(F32), axis=0, keepdims=True)
                    e_out = eout_s[rows, ksl]
                    qc = q_r[rows, ksl].astype(F32) * QSCALE
                    kc = k_r[rows, ksl].astype(F32)
                    dq = dqs[rows] * eq_s[rows, ksl] + dq_in * ein_s[rows, ksl]
                    dk = dks[rows] * ek_s[rows, ksl] + dk_out * e_out
                    dqkv_r[rows, OFF_Q + h * DK:OFF_Q + (h + 1) * DK] = (dq * QSCALE).astype(BF16)
                    dqkv_r[rows, OFF_K + h * DK:OFF_K + (h + 1) * DK] = dk.astype(BF16)
                    tail = jnp.sum(dk_out * (kc * e_out), axis=0, keepdims=True) + ddec * dec
                    db_s[rows, ksl] = (qc * dq - kc * dk) + jnp.where(row == last, tail, 0.0)
                    ds = ds * dec + g_t[:, c * DK:(c + 1) * DK]
                ds_scr[d * HEADS + h] = ds
            dg = _dot_split3(cum_t.astype(BF16), db_s[...])
            dlogit = (dg * GATE_SCALE) * _sigmoid(-lg_s[...])
            dlb = dlogit.astype(BF16)
            dlr_r[...] = _dot_nt(dlb, wv)
            dw_r[...] += _dot_tn(lrv, dlb)
            db_r[...] += jnp.sum(dlogit, axis=0, keepdims=True)

    fw = lambda i: (nb - 1 - i, 0)
    bw = lambda i: (i, 0)
    const = lambda i: (0, 0)

    def tok_specs(m):
        return [pl.BlockSpec((tt, QK_W), lambda i: (m(i)[0], OFF_Q // QK_W)),
                pl.BlockSpec((tt, QK_W), lambda i: (m(i)[0], OFF_K // QK_W)),
                pl.BlockSpec((tt, V_W), lambda i: (m(i)[0], OFF_V // V_W)),
                pl.BlockSpec((tt, LR_W), m),
                pl.BlockSpec((tt, V_W), m),
                pl.BlockSpec((nc, HEADS, DV, DK), lambda i: (m(i)[0], 0, 0, 0))]

    dqkv = jax.ShapeDtypeStruct((seq, QK_W + QK_W + V_W), BF16)
    dlr = jax.ShapeDtypeStruct((seq, LR_W), F32)
    dw = jax.ShapeDtypeStruct((LR_W, QK_W), F32)
    dbias = jax.ShapeDtypeStruct((1, QK_W), F32)
    return pl.pallas_call(
        body, name="gla_bwd",
        out_shape=(dqkv, dlr, dqkv, dlr, dw, dw, dbias, dbias),
        grid=(nb,),
        in_specs=tok_specs(fw) + tok_specs(bw) + [
            pl.BlockSpec((LR_W, QK_W), const), pl.BlockSpec((LR_W, QK_W), const),
            pl.BlockSpec((1, QK_W), const), pl.BlockSpec((1, QK_W), const)],
        out_specs=(pl.BlockSpec((tt, QK_W + QK_W + V_W), fw), pl.BlockSpec((tt, LR_W), fw),
                   pl.BlockSpec((tt, QK_W + QK_W + V_W), bw), pl.BlockSpec((tt, LR_W), bw),
                   pl.BlockSpec((LR_W, QK_W), const), pl.BlockSpec((LR_W, QK_W), const),
                   pl.BlockSpec((1, QK_W), const), pl.BlockSpec((1, QK_W), const)),
        scratch_shapes=[pltpu.VMEM((2 * HEADS, DV, DK), F32)] + [pltpu.VMEM((tt, QK_W), F32)] * 4
        + [pltpu.VMEM((tt, QK_W), BF16)] * 4 + [pltpu.VMEM((tt, QK_W), F32)] * 2,
        compiler_params=_cparams("arbitrary"),
    )(proj, proj, proj, lr, do, st_f, proj, proj, proj, lr, do, st_b, wgk_f, wgk_b, bgk_f, bgk_b)


def _both_directions(f_ref, b_ref):
    return (_f32(f_ref) + _f32(b_ref)).astype(BF16)


def _input_grad(dqkv_f, dqkv_b, dp_gates, dp_ch, dlr_f, dlr_b, w_nat, x2d, norm_g, dx2, sums, tm):
    seq = x2d.shape[0]
    nt, n = seq // tm, len(sums)
    relay_step = (3 * nt) // 8

    def body(dqf, dqb, dg, dc, dlf, dlb, w, x_ref, g_ref, dx2_ref, *rest):
        ins, (gx_ref, dng_ref), outs = rest[:n], rest[n:n + 2], rest[n + 2:2 * n + 2]
        passing, joined = rest[2 * n + 2:3 * n + 2], rest[3 * n + 2:4 * n + 2]
        send_sems, recv_sems, local_sems = rest[4 * n + 2:]
        i = pl.program_id(0)
        c = lax.axis_index("c")
        first, second, diagonal = _route_chips()
        slot = lambda chip: 2 * chip[0] + chip[1]

        def remote(a, k, src, dst, to):
            return pltpu.make_async_remote_copy(src_ref=src, dst_ref=dst, send_sem=send_sems.at[3 * a + k],
                                                recv_sem=recv_sems.at[3 * a + k], device_id=(*to, c),
                                                device_id_type=MESH)

        direct = lambda a: remote(a, 0, ins[a].at[slot(first)], outs[a].at[0], first)
        for_second = lambda a: remote(a, 1, ins[a].at[slot(diagonal)], passing[a], first)
        joint = lambda a: remote(a, 2, joined[a], outs[a].at[1], second)
        own = lambda a: pltpu.make_async_copy(ins[a].at[slot(second)], joined[a], local_sems.at[a])

        @pl.when(i == 0)
        def _():
            _start_all([for_second(a) for a in range(n)] + [own(a) for a in range(n)] + [direct(a) for a in range(n)])
            dng_ref[...] = jnp.zeros(dng_ref.shape, F32)

        @pl.when(i == relay_step)
        def _():
            for a in range(n):
                for_second(a).wait_recv()
                own(a).wait()
                joined[a][...] = (joined[a][...].astype(F32) + passing[a][...].astype(F32)).astype(BF16)
                joint(a).start()

        dh = (_dot((dlf[...] + dlb[...]).astype(BF16), w[NAT_LR:NAT_LR + LR_W, :])
              + _dot(_both_directions(dqf, dqb), w[0:NAT_ZA, :])
              + _dot(dg[:, 0:CONV_W], w[NAT_ZA:NAT_LR, :]) + _dot(dg[:, CONV_W:2 * CONV_W], w[NAT_B:NAT_C, :])
              + _dot(dg[:, 2 * CONV_W:], w[NAT_ZC:IN_W, :]) + _dot(dc[...], w[NAT_C:NAT_ZC, :]))
        xv = x_ref[...]
        r = lax.rsqrt(jnp.mean(xv * xv, axis=-1, keepdims=True) + EPS)
        xn = xv * r
        dng_ref[...] += jnp.sum(dh * xn, axis=0, keepdims=True)
        dn = dh * g_ref[...]
        gx_ref[...] = (r * dn - xn * (r * jnp.mean(dn * xn, axis=-1, keepdims=True))) + dx2_ref[...]

        @pl.when(i == nt - 1)
        def _():
            for a in range(n):
                direct(a).wait_recv()
                joint(a).wait_recv()
            for a in range(n):
                for cp in (direct(a), for_second(a), joint(a)):
                    cp.wait_send()

    rowt = pl.BlockSpec((tm, D_MODEL), lambda i: (i, 0))
    seg = lambda width: pl.BlockSpec((tm, width), lambda i: (i, 0))
    resident = lambda rows: pl.BlockSpec((rows, D_MODEL), lambda i: (0, 0), pipeline_mode=pl.Buffered(1))
    hbm = pl.BlockSpec(memory_space=pl.ANY)
    blocks = [pltpu.VMEM(s.shape[1:], s.dtype) for s in sums]
    return pl.pallas_call(
        body, name="input_grad",
        out_shape=(jax.ShapeDtypeStruct((seq, D_MODEL), F32), jax.ShapeDtypeStruct((1, D_MODEL), F32))
        + tuple(jax.ShapeDtypeStruct((2,) + s.shape[1:], s.dtype) for s in sums),
        grid=(nt,),
        in_specs=[seg(QKV_W), seg(QKV_W), seg(GATES_W), seg(CH_W), seg(LR_W), seg(LR_W), resident(IN_W),
                  rowt, pl.BlockSpec((1, D_MODEL), lambda i: (0, 0)), rowt] + [hbm] * n,
        out_specs=(rowt, pl.BlockSpec((1, D_MODEL), lambda i: (0, 0))) + (hbm,) * n,
        scratch_shapes=blocks + blocks + [pltpu.SemaphoreType.DMA((3 * n,)), pltpu.SemaphoreType.DMA((3 * n,)),
                                          pltpu.SemaphoreType.DMA((n,))],
        compiler_params=_cparams("arbitrary"),
    )(dqkv_f, dqkv_b, dp_gates, dp_ch, dlr_f, dlr_b, w_nat, x2d, norm_g, dx2, *sums)


def _weight_grad_out(y_t, dx2b, tk, riding):
    m, seq = y_t.shape
    n = dx2b.shape[1]
    nk = seq // tk

    def body(a_ref, b_ref, ride_in, o_ref, ride_out, send_sems, recv_sems):
        k = pl.program_id(0)

        @pl.when(k == 0)
        def _():
            _start_all(_sibling_copies(ride_in, ride_out, send_sems, recv_sems))
            o_ref[...] = jnp.zeros(o_ref.shape, F32)

        o_ref[...] += _dot(a_ref[...], b_ref[...])

        @pl.when(k == nk - 1)
        def _():
            _wait_all(_sibling_copies(ride_in, ride_out, send_sems, recv_sems))

    hbm = pl.BlockSpec(memory_space=pl.ANY)
    return pl.pallas_call(
        body, name="wgrad_out",
        out_shape=(jax.ShapeDtypeStruct((m, n), F32), jax.ShapeDtypeStruct((4,) + _block_shape(riding), F32)),
        grid=(nk,),
        in_specs=[pl.BlockSpec((m, tk), lambda k: (0, k)), pl.BlockSpec((tk, n), lambda k: (k, 0)), hbm],
        out_specs=(pl.BlockSpec((m, n), lambda k: (0, 0)), hbm),
        scratch_shapes=[pltpu.SemaphoreType.DMA((4,)), pltpu.SemaphoreType.DMA((4,))],
        compiler_params=_cparams("arbitrary"),
    )(y_t, dx2b, riding)


def _weight_grad_in(h_t, dqkv_f, dqkv_b, dp_gates, dp_ch, dlr_f, dlr_b):
    m, seq = h_t.shape
    tn = 512
    n_qkv, n_gates, n_ch = QKV_W // tn, GATES_W // tn, CH_W // tn
    starts = ([k * tn for k in range(n_qkv)] + [NAT_ZA, NAT_ZA + tn, NAT_B, NAT_B + tn, NAT_ZC, NAT_ZC + tn]
              + [NAT_C + k * tn for k in range(n_ch)])

    def out_row(j):
        row = 0
        for k, start in enumerate(starts):
            row = row + jnp.where(j == k, start // 32, 0)
        return pl.multiple_of(row * 32, 32), 0

    def body(a_ref, bqf, bqb, bg, bc, o_ref, acc, bq):
        j = pl.program_id(0)

        @pl.when(j < n_qkv)
        def _():
            bq[...] = _both_directions(bqf, bqb)
            acc[...] = _dot(a_ref[...], bq[...])

        @pl.when(jnp.logical_and(j >= n_qkv, j < n_qkv + n_gates))
        def _():
            acc[...] = _dot(a_ref[...], bg[...])

        @pl.when(j >= n_qkv + n_gates)
        def _():
            acc[...] = _dot(a_ref[...], bc[...])

        o_ref[...] = acc[...].T

    resident = pl.BlockSpec((m, seq), lambda j: (0, 0), pipeline_mode=pl.Buffered(1))
    seg = lambda first, count: pl.BlockSpec((seq, tn), lambda j: (0, jnp.clip(j - first, 0, count - 1)))
    main = pl.pallas_call(
        body, name="wgrad_in",
        out_shape=jax.ShapeDtypeStruct((IN_W, m), F32),
        grid=(n_qkv + n_gates + n_ch,),
        in_specs=[resident, seg(0, n_qkv), seg(0, n_qkv), seg(n_qkv, n_gates), seg(n_qkv + n_gates, n_ch)],
        out_specs=pl.BlockSpec((pl.Element(tn), pl.Element(m)), out_row),
        scratch_shapes=[pltpu.VMEM((m, tn), F32), pltpu.VMEM((seq, tn), BF16)],
        compiler_params=_cparams("arbitrary"),
    )(h_t, dqkv_f, dqkv_b, dp_gates, dp_ch)

    def lr_body(a_ref, bf_ref, bb_ref, full_ref, o_ref, acc):
        acc[...] = _dot(a_ref[...], (bf_ref[...] + bb_ref[...]).astype(BF16))
        o_ref[...] = acc[...].T[0:2 * RANK, :]

    whole = lambda shape: pl.BlockSpec(shape, lambda j: (0, 0))
    return pl.pallas_call(
        lr_body, name="wgrad_lr",
        out_shape=jax.ShapeDtypeStruct((IN_W, m), F32),
        grid=(1,),
        in_specs=[whole((m, seq)), whole((seq, LR_W)), whole((seq, LR_W)), pl.BlockSpec(memory_space=pl.ANY)],
        out_specs=pl.BlockSpec((pl.Element(2 * RANK), pl.Element(m)), lambda j: (NAT_LR, 0)),
        scratch_shapes=[pltpu.VMEM((m, LR_W), F32)],
        input_output_aliases={3: 0},
        compiler_params=_cparams("arbitrary"),
    )(h_t, dlr_f, dlr_b, main)


def _pad_rows(a, rows):
    return jnp.pad(a, ((0, rows - a.shape[0]), (0, 0)))


def _rows128(a):
    a = a.reshape(-1, 128)
    return _pad_rows(a, -(-a.shape[0] // 8) * 8)


def _pack(arrs):
    return jnp.concatenate([_rows128(a) for a in arrs], axis=0)


def _unpack(buf, like):
    out, start = [], 0
    for a in like:
        rows = a.size // 128
        out.append(buf[start:start + rows].reshape(a.shape))
        start += -(-rows // 8) * 8
    return out


def kernel(x, norm_g, w_in, w_gk_f, b_gk_f, w_gk_b, b_gk_b, gla_norm_g, conv_w, conv_b, w_out, final_g, loss_target, m_norm_g, m_w_in, m_w_gk_f, m_b_gk_f, m_w_gk_b, m_b_gk_b, m_gla_norm_g, m_conv_w, m_conv_b, m_w_out, m_final_g, v_norm_g, v_w_in, v_w_gk_f, v_b_gk_f, v_w_gk_b, v_b_gk_b, v_gla_norm_g, v_conv_w, v_conv_b, v_w_out, v_final_g):
    px, py, pc = _position()
    me = _blk(px, py, pc)
    seq = x.shape[1]
    x2d, tgt = x[0], loss_target[0]
    tm = min(512, seq)
    tt = min(256, seq)

    small_s = jnp.concatenate([jnp.concatenate([w_gk_f[0], w_gk_b[0]], axis=1), _pad_rows(conv_w[0], 8)], axis=0)
    shifted = lax.dynamic_update_slice(jnp.zeros((SHIFTED_ROWS, D_MODEL), F32), w_in[0].T, (4 * (me % 4), 0))
    order = sum(jnp.where(2 * px + py == k, jnp.asarray(tiles + (0,), jnp.int32), 0) for k, tiles in enumerate(TILE_ORDER))
    proj, lr, h_t, w_nat, wout_all, small_all = _gather_inproj(x2d, norm_g, shifted, w_out[0], small_s, order,
                                                               min(1024, seq))
    w_out_full = wout_all.reshape(MIX_W, D_MODEL)
    wgk_cols = 512 // N_DEV
    wgk_f_full = small_all[:, 0:RANK, 0:wgk_cols].transpose(1, 0, 2).reshape(RANK, QK_W)
    wgk_b_full = small_all[:, 0:RANK, wgk_cols:2 * wgk_cols].transpose(1, 0, 2).reshape(RANK, QK_W)
    conv_w_full = _pad_rows(small_all[:, RANK:RANK + 3, :].transpose(1, 0, 2).reshape(3, CONV_W), 8)
    zr = lambda n: jnp.zeros((n, QK_W), F32)
    wgk_f_pad = jnp.concatenate([wgk_f_full, zr(LR_W - RANK)], axis=0).astype(BF16)
    wgk_b_pad = jnp.concatenate([zr(RANK), wgk_b_full, zr(LR_W - 2 * RANK)], axis=0).astype(BF16)

    o_f, o_b, st_f, st_b = _gla_fwd(proj, lr, wgk_f_pad, wgk_b_pad, b_gk_f, b_gk_b, tt)
    tmix = min(256, seq)
    y_t, conv, dx2, dx2b, loss_p, dfg_p = _mix_out_loss(o_f, o_b, proj, x2d, tgt, gla_norm_g, conv_w_full, conv_b,
                                                        w_out_full, final_g.reshape(1, D_MODEL), tmix)

    dp_gates, do, dconv, dgg_p, dcb_p = _mix_bwd(dx2b, o_f, o_b, proj, conv, gla_norm_g, w_out_full, tmix)
    dp_ch, dcw_p = _conv_bwd(dconv, proj, conv_w_full, tmix)
    dqkv_f, dlr_f, dqkv_b, dlr_b, dwf_p, dwb_p, dbf_p, dbb_p = _gla_bwd(
        proj, lr, do, st_f, st_b, wgk_f_pad, wgk_b_pad, b_gk_f, b_gk_b, tt)
    dw_nat = _weight_grad_in(h_t, dqkv_f, dqkv_b, dp_gates, dp_ch, dlr_f, dlr_b)

    dw_out, sib_in = _weight_grad_out(y_t, dx2b, tm, dw_nat)
    part_out = dw_out.reshape(N_DEV, MIX_W // N_DEV, D_MODEL)
    core = jnp.reshape(pc, (1,)).astype(jnp.int32)
    chip = jnp.reshape(2 * px + py, (1,)).astype(jnp.int32)
    sums_in, sib_out = _chip_sums(dw_nat, sib_in, core, 256, "chip_sums_in", riding=part_out)
    sums_out = _chip_sums(part_out, sib_out, core, 256, "chip_sums_out")
    grad_x2d, dng_p, far_in, far_out = _input_grad(dqkv_f, dqkv_b, dp_gates, dp_ch, dlr_f, dlr_b, w_nat, x2d, norm_g, dx2,
                                                   [sums_in, sums_out], tmix)
    pieces = [dng_p, dbf_p, dbb_p, dgg_p, dcb_p, dfg_p[0], dwf_p[0:RANK], dwb_p[RANK:2 * RANK], dcw_p[0:3], loss_p[0]]
    g_window, small_tot = _final_sum(sums_in, far_in, chip, _pack(pieces), 256, "final_sum_in")
    g_in_t = lax.dynamic_slice_in_dim(g_window, 4 * pc, SHARD_W, axis=0)
    g_w_out, d_w_out, nm_w_out, nv_w_out = _final_sum_adamw(sums_out, far_out, chip, w_out[0], m_w_out[0], v_w_out[0],
                                                            256, "adamw_out")
    flat = lambda a: a[0].T.reshape(SHARD_W, D_MODEL // 128, 128)
    unflat = lambda a: a.reshape(SHARD_W, D_MODEL).T
    d_flat, m_flat, v_flat = _adamw_rows(g_in_t.reshape(SHARD_W, D_MODEL // 128, 128), flat(w_in), flat(m_w_in),
                                         flat(v_w_in), 90, "adamw_in")
    g_w_in, d_w_in, nm_w_in, nv_w_in = g_in_t.T, unflat(d_flat), unflat(m_flat), unflat(v_flat)

    tot = _unpack(small_tot, pieces)
    g_norm_g, g_b_gk_f, g_b_gk_b, g_gla, g_conv_b, g_final = tot[:6]
    g_wgk_f = lax.dynamic_slice_in_dim(tot[6], me * wgk_cols, wgk_cols, axis=1)[None]
    g_wgk_b = lax.dynamic_slice_in_dim(tot[7], me * wgk_cols, wgk_cols, axis=1)[None]
    g_conv_w = lax.dynamic_slice_in_dim(tot[8], me * 128, 128, axis=1)[None]
    loss = tot[9][0]

    small_g = [g_norm_g, g_b_gk_f, g_b_gk_b, g_gla, g_conv_b, g_final, g_wgk_f, g_wgk_b, g_conv_w]
    small_w = [norm_g, b_gk_f, b_gk_b, gla_norm_g, conv_b, final_g, w_gk_f, w_gk_b, conv_w]
    small_m = [m_norm_g, m_b_gk_f, m_b_gk_b, m_gla_norm_g, m_conv_b, m_final_g, m_w_gk_f, m_w_gk_b, m_conv_w]
    small_v = [v_norm_g, v_b_gk_f, v_b_gk_b, v_gla_norm_g, v_conv_b, v_final_g, v_w_gk_f, v_w_gk_b, v_conv_w]
    d_s, m_s, v_s = _adamw_small(_pack(small_g), _pack(small_w), _pack(small_m), _pack(small_v))
    d_l, m_l, v_l = _unpack(d_s, small_w), _unpack(m_s, small_w), _unpack(v_s, small_w)

    def ordered(sm, big_in, big_out):
        return [sm[0], big_in[None], sm[6], sm[1], sm[7], sm[2], sm[3], sm[8], sm[4], big_out[None], sm[5]]

    grads = ordered(small_g, g_w_in, g_w_out)
    deltas = ordered(d_l, d_w_in, d_w_out)
    new_m = ordered(m_l, nm_w_in, nm_w_out)
    new_v = ordered(v_l, nv_w_in, nv_w_out)
    return (loss, grad_x2d[None], *grads, *deltas, *new_m, *new_v)
```

```python
import jax
import jax.numpy as jnp
from jax import lax
from jax.experimental import pallas as pl
from jax.experimental.pallas import tpu as pltpu

F32 = jnp.float32
BF16 = jnp.bfloat16
MESH = pl.DeviceIdType.MESH

N_DEV = 8
D_MODEL = 1024
HEADS = 4
DK = 128
DV = 256
QK_W = HEADS * DK
V_W = HEADS * DV
CONV_W = 1024
MIX_W = V_W + CONV_W
CHUNK = 64
RANK = 16
IN_W = 7200
SHARD_W = IN_W // N_DEV
MAIN_W = 7168
LR_W = 128
OFF_Q, OFF_K, OFF_V, OFF_ZA, OFF_B, OFF_ZC, OFF_C, OFF_H = 0, 512, 1024, 2048, 3072, 4096, 5120, 6144
QKV_W, GATES_W, CH_W = 2048, 3072, 2048
NAT_ZA, NAT_LR, NAT_B, NAT_C, NAT_ZC = 2048, 3072, 3104, 4128, 6176
EPS = 1e-6
GATE_SCALE = 1.0 / 16.0
QSCALE = DK ** -0.5
REF_F, LAST_F = CHUNK // 2, CHUNK - 1
REF_B, LAST_B = CHUNK - 1 - CHUNK // 2, 0

ADAM_LR = 0.001
ADAM_B1 = 0.9
ADAM_B2 = 0.999
ADAM_EPS = 1e-08
ADAM_WD = 0.01
ADAM_STEP = 10

VMEM_LIMIT = 56 * 1024 * 1024


def _cparams(*sem):
    return pltpu.CompilerParams(dimension_semantics=sem, vmem_limit_bytes=VMEM_LIMIT)


def _dot(a, b):
    return jnp.dot(a, b, preferred_element_type=F32)


def _dot_nt(a, b):
    return lax.dot_general(a, b, (((1,), (1,)), ((), ())), preferred_element_type=F32)


def _dot_tn(a, b):
    return lax.dot_general(a, b, (((0,), (0,)), ((), ())), preferred_element_type=F32)


def _sigmoid(z):
    return jax.nn.sigmoid(z)


def _position():
    return lax.axis_index("x"), lax.axis_index("y"), lax.axis_index("c")


def _blk(px, py, pc):
    return 4 * px + 2 * py + pc


EDGE = 16
SHIFTED_ROWS = 912
BODY_ROWS = SHIFTED_ROWS - 2 * EDGE


def _first_tile_row(blk, px):
    return EDGE * (56 * blk + px)


def _edge_tiles():
    tiles = {}
    for blk in range(N_DEV):
        first = _first_tile_row(blk, blk // 4)
        tiles.setdefault(first, []).append((blk, 0))
        tiles.setdefault(first + EDGE + BODY_ROWS, []).append((blk, 1))
    return tiles


def _peer_copies(srcs, outs, send_sems, recv_sems):
    x, y, c = _position()
    me = _blk(x, y, c)
    copies = []
    for a, (src, out) in enumerate(zip(srcs, outs)):
        k = 0
        for dx in (0, 1):
            for dy in (0, 1):
                for dc in (0, 1):
                    if dx + dy + dc == 0:
                        continue
                    peer = (1 - x if dx else x, 1 - y if dy else y, 1 - c if dc else c)
                    copies.append(pltpu.make_async_remote_copy(
                        src_ref=src, dst_ref=out.at[me], send_sem=send_sems.at[a * 7 + k],
                        recv_sem=recv_sems.at[a * 7 + k], device_id=peer, device_id_type=MESH))
                    k += 1
    return copies


def _route_chips():
    x, y, c = _position()
    along_x = c == 0
    return [(jnp.where(along_x, 1 - x, x), jnp.where(along_x, y, 1 - y)),
            (jnp.where(along_x, x, 1 - x), jnp.where(along_x, 1 - y, y)), (1 - x, 1 - y)]


WINDOW_ROWS = SHARD_W + 4


def _window_start(k, parity):
    return 2 * SHARD_W * k + (SHARD_W - 4) * parity


def _owner_block(part, k, parity):
    if part.ndim == 3:
        return part.at[2 * k + parity]
    return part.at[pl.ds(pl.multiple_of(_window_start(k, parity), 8), WINDOW_ROWS)]


def _block_shape(part):
    return part.shape[1:] if part.ndim == 3 else (WINDOW_ROWS, part.shape[1])


def _sibling_copies(part, out, send_sems, recv_sems):
    x, y, c = _position()
    return [pltpu.make_async_remote_copy(src_ref=_owner_block(part, k, 1 - c), dst_ref=out.at[k],
                                         send_sem=send_sems.at[k], recv_sem=recv_sems.at[k],
                                         device_id=(x, y, 1 - c), device_id_type=MESH)
            for k in range(4)]


def _start_all(copies):
    for cp in copies:
        cp.start()


def _wait_all(copies):
    for cp in copies:
        cp.wait_recv()
    for cp in copies:
        cp.wait_send()


def _chip_sums(part, from_sibling, core, tc, name, riding=None):
    rows, cols = _block_shape(part)
    nj = cols // tc

    def body(core_ref, p_ref, s_ref, *rest):
        if riding is None:
            (o_ref,) = rest
        else:
            ride_in, o_ref, ride_out, send_sems, recv_sems = rest
            k, j = pl.program_id(0), pl.program_id(1)

            @pl.when(jnp.logical_and(k == 0, j == 0))
            def _():
                _start_all(_sibling_copies(ride_in, ride_out, send_sems, recv_sems))

        o_ref[0] = (p_ref[...].reshape(rows, tc) + s_ref[0]).astype(BF16)

        if riding is not None:
            @pl.when(jnp.logical_and(k == 3, j == nj - 1))
            def _():
                _wait_all(_sibling_copies(ride_in, ride_out, send_sems, recv_sems))

    hbm = pl.BlockSpec(memory_space=pl.ANY)
    sums = jax.ShapeDtypeStruct((4, rows, cols), BF16)
    tile_out = pl.BlockSpec((1, rows, tc), lambda k, j, core_ref: (k, 0, j))
    if part.ndim == 3:
        mine = pl.BlockSpec((1, rows, tc), lambda k, j, core_ref: (2 * k + core_ref[0], 0, j))
    else:
        mine = pl.BlockSpec((pl.Element(rows), pl.Element(tc)),
                            lambda k, j, core_ref: (pl.multiple_of(_window_start(k, core_ref[0]), 8),
                                                    pl.multiple_of(j * tc, 128)))
    in_specs = [mine, pl.BlockSpec((1, rows, tc), lambda k, j, core_ref: (k, 0, j))]
    if riding is None:
        out_shape, out_specs, scratch, args = sums, tile_out, [], (core, part, from_sibling)
    else:
        out_shape = (sums, jax.ShapeDtypeStruct((4,) + _block_shape(riding), F32))
        out_specs, in_specs = (tile_out, hbm), in_specs + [hbm]
        scratch = [pltpu.SemaphoreType.DMA((4,)), pltpu.SemaphoreType.DMA((4,))]
        args = (core, part, from_sibling, riding)
    return pl.pallas_call(
        body, name=name, out_shape=out_shape,
        grid_spec=pltpu.PrefetchScalarGridSpec(num_scalar_prefetch=1, grid=(4, nj), in_specs=in_specs,
                                               out_specs=out_specs, scratch_shapes=scratch),
        compiler_params=_cparams("arbitrary", "arbitrary"),
    )(*args)


def _sum_chips(s_ref, r_ref):
    f = lambda a: a.astype(F32)
    return (f(s_ref[0]) + f(r_ref[0])) + f(r_ref[1])


def _final_sum(sums, from_chips, chip, small, tc, name):
    _, rows, cols = sums.shape
    nj = cols // tc

    def body(chip_ref, s_ref, r_ref, sm_ref, g_out, tot_ref, all_ref, send_sems, recv_sems):
        j = pl.program_id(0)
        me = _blk(*_position())

        @pl.when(j == 0)
        def _():
            all_ref[me] = sm_ref[...]
            _start_all(_peer_copies((all_ref.at[me],), (all_ref,), send_sems, recv_sems))

        g_out[...] = _sum_chips(s_ref, r_ref)

        @pl.when(j == nj - 1)
        def _():
            _wait_all(_peer_copies((all_ref.at[me],), (all_ref,), send_sems, recv_sems))
            acc = all_ref[0]
            for d in range(1, N_DEV):
                acc = acc + all_ref[d]
            tot_ref[...] = acc

    whole = pl.BlockSpec(small.shape, lambda j, chip_ref: (0, 0))
    return pl.pallas_call(
        body, name=name,
        out_shape=(jax.ShapeDtypeStruct((rows, cols), F32), jax.ShapeDtypeStruct(small.shape, F32)),
        grid_spec=pltpu.PrefetchScalarGridSpec(
            num_scalar_prefetch=1, grid=(nj,),
            in_specs=[pl.BlockSpec((1, rows, tc), lambda j, chip_ref: (chip_ref[0], 0, j)),
                      pl.BlockSpec((2, rows, tc), lambda j, chip_ref: (0, 0, j)), whole],
            out_specs=(pl.BlockSpec((rows, tc), lambda j, chip_ref: (0, j)), whole),
            scratch_shapes=[pltpu.VMEM((N_DEV,) + small.shape, F32), pltpu.SemaphoreType.DMA((7,)),
                            pltpu.SemaphoreType.DMA((7,))]),
        compiler_params=_cparams("arbitrary"),
    )(chip, sums, from_chips, small)


def _adamw_rows(g, w, m, v, tr, name):
    rows = g.shape[0]

    def body(g_ref, w_ref, m_ref, v_ref, d_out, m_out, v_out):
        delta, m_new, v_new = _adamw(w_ref[...], g_ref[...], m_ref[...], v_ref[...])
        d_out[...] = delta
        m_out[...] = m_new
        v_out[...] = v_new

    tile = pl.BlockSpec((tr,) + g.shape[1:], lambda r: (r, 0, 0))
    shp = jax.ShapeDtypeStruct(g.shape, F32)
    return pl.pallas_call(
        body, name=name, out_shape=(shp, shp, shp), grid=(rows // tr,),
        in_specs=[tile] * 4, out_specs=(tile, tile, tile),
        compiler_params=_cparams("arbitrary"),
    )(g, w, m, v)


def _adamw(w, g, m, v):
    m = ADAM_B1 * m + (1.0 - ADAM_B1) * g
    v = ADAM_B2 * v + (1.0 - ADAM_B2) * (g * g)
    m_hat = m / (1.0 - ADAM_B1 ** ADAM_STEP)
    v_hat = v / (1.0 - ADAM_B2 ** ADAM_STEP)
    delta = -ADAM_LR * (m_hat / (jnp.sqrt(v_hat) + ADAM_EPS) + ADAM_WD * w)
    return delta, m, v


def _final_sum_adamw(sums, from_chips, chip, w, m, v, tr, name):
    rows, cols = w.shape

    def body(chip_ref, s_ref, r_ref, w_ref, m_ref, v_ref, g_out, d_out, m_out, v_out):
        g = _sum_chips(s_ref, r_ref)
        delta, m_new, v_new = _adamw(w_ref[...], g, m_ref[...], v_ref[...])
        g_out[...] = g
        d_out[...] = delta
        m_out[...] = m_new
        v_out[...] = v_new

    tile = pl.BlockSpec((tr, cols), lambda r, chip_ref: (r, 0))
    shp = jax.ShapeDtypeStruct((rows, cols), F32)
    return pl.pallas_call(
        body, name=name,
        out_shape=(shp, shp, shp, shp),
        grid_spec=pltpu.PrefetchScalarGridSpec(
            num_scalar_prefetch=1, grid=(rows // tr,),
            in_specs=[pl.BlockSpec((1, tr, cols), lambda r, chip_ref: (chip_ref[0], r, 0)),
                      pl.BlockSpec((2, tr, cols), lambda r, chip_ref: (0, r, 0)),
                      tile, tile, tile],
            out_specs=(tile, tile, tile, tile)),
        compiler_params=_cparams("arbitrary"),
    )(chip, sums, from_chips, w, m, v)


def _adamw_small(g, w, m, v):
    def body(g_ref, w_ref, m_ref, v_ref, d_out, m_out, v_out):
        delta, m_new, v_new = _adamw(w_ref[...], g_ref[...], m_ref[...], v_ref[...])
        d_out[...] = delta
        m_out[...] = m_new
        v_out[...] = v_new

    vmem = pl.BlockSpec(memory_space=pltpu.VMEM)
    shp = jax.ShapeDtypeStruct(g.shape, F32)
    return pl.pallas_call(body, name="adamw_small", out_shape=(shp, shp, shp),
                          in_specs=[vmem] * 4, out_specs=(vmem, vmem, vmem))(g, w, m, v)


TILE_ROWS = (0, 1024, NAT_ZA, NAT_B, NAT_ZC, NAT_C, NAT_C + CONV_W)


TILE_ORDER = ((0, 1, 2, 3, 5, 6, 4), (2, 0, 1, 4, 3, 5, 6), (5, 0, 6, 4, 1, 2, 3), (4, 2, 3, 5, 6, 0, 1))
NEIGHBOUR_SWEEP, DIAGONAL_SWEEP = 1, 4
PIECES, W_IN_PIECES, OTHER_PIECES = 4, (0, 1), (2, 3)


def _gather_inproj(x2d, norm_g, shifted, w_out_s, small_s, order, tm):
    seq = x2d.shape[0]
    tn = CONV_W
    ni, nj = seq // tm, MAIN_W // tn
    first_sweep = lambda j, i, order_ref: jnp.where(j == 0, i, ni - 1)
    last_sweep = lambda j, i, order_ref: jnp.where(j == nj - 1, i, 0)
    edge_tiles = _edge_tiles()

    def body(order_ref, x_ref, g_ref, sh_ref, wout_ref, sm_ref, proj_ref, lr_ref, ht_ref, w_nat, wout_all, sm_all,
             w_all, h_all, edges, wout_b, sm_b, send_sems, recv_sems, local_sems):
        j, i = pl.program_id(0), pl.program_id(1)
        rows = pl.ds(pl.multiple_of(i * tm, tm), tm)
        x, y, c = _position()
        me, here, sibling = _blk(x, y, c), (x, y, c), (x, y, 1 - c)
        chips = _route_chips()
        sibling_chips = [chips[1], chips[0], chips[2]]

        def pieces(px, py, pc):
            blk = _blk(px, py, pc)
            body_rows = pl.ds(pl.multiple_of(_first_tile_row(blk, px) + EDGE, EDGE), BODY_ROWS)
            return [w_all.at[body_rows], edges.at[blk], wout_all.at[blk], sm_all.at[blk]]

        def copy(a, k, block, to, staged=None):
            ref = pieces(*block)[a]
            return pltpu.make_async_remote_copy(src_ref=ref if staged is None else staged, dst_ref=ref,
                                                send_sem=send_sems.at[a * 7 + k], recv_sem=recv_sems.at[a * 7 + k],
                                                device_id=to, device_id_type=MESH)

        def own_copies(group):
            targets = [(0, sibling)] + [(1 + n, (*chips[n], c)) for n in range(2)]
            staged = [None, None, wout_b, sm_b]
            return [copy(a, k, here, to, staged[a]) for k, to in targets for a in group]

        def relays(group):
            return [copy(a, 3, (*chips[0], c), (*chips[1], c)) for a in group]

        def forwards(n, group):
            return [copy(a, 4 + n, (*chips[n], c), sibling) for a in group]

        def keep_own():
            return [pltpu.make_async_copy(wout_b, wout_all.at[me], local_sems.at[0]),
                    pltpu.make_async_copy(sm_b, sm_all.at[me], local_sems.at[1])]

        def keep_weight():
            return pltpu.make_async_copy(w_all, w_nat, local_sems.at[2])

        def arrive(ns, group):
            for n in ns:
                for a in group:
                    copy(a, 1 + n, (*chips[n], c), here).wait_recv()
                _start_all((relays(group) if n == 0 else []) + forwards(n, group))
            for n in ns:
                for a in group:
                    copy(a, 4 + n, (*sibling_chips[n], 1 - c), here).wait_recv()

        def add_edge_tiles(stage):
            for row, parts in edge_tiles.items():
                ready = 0
                for blk, _ in parts:
                    away = (x != blk // 4).astype(jnp.int32) + (y != (blk // 2) % 2).astype(jnp.int32)
                    ready = jnp.maximum(ready, away)

                @pl.when(ready == stage)
                def _(row=row, parts=parts):
                    tile = edges[parts[0][0], parts[0][1]].astype(F32)
                    for blk, side in parts[1:]:
                        tile = tile + edges[blk, side].astype(F32)
                    w_all[row:row + EDGE, :] = tile.astype(BF16)

        @pl.when(jnp.logical_and(j == 0, i == 0))
        def _():
            pieces(*here)[0][...] = sh_ref[EDGE:EDGE + BODY_ROWS, :].astype(BF16)
            edges[me, 0] = sh_ref[0:EDGE, :].astype(BF16)
            edges[me, 1] = sh_ref[EDGE + BODY_ROWS:, :].astype(BF16)
            _start_all(own_copies(W_IN_PIECES))
            wout_b[...] = wout_ref[...].astype(BF16)
            sm_b[...] = sm_ref[...]
            _start_all(own_copies(OTHER_PIECES) + keep_own())
            for a in W_IN_PIECES:
                copy(a, 0, sibling, here).wait_recv()
            add_edge_tiles(0)

        @pl.when(jnp.logical_and(j == NEIGHBOUR_SWEEP, i == 0))
        def _():
            arrive((0, 1), W_IN_PIECES)
            add_edge_tiles(1)

        @pl.when(jnp.logical_and(j == DIAGONAL_SWEEP, i == 0))
        def _():
            arrive((2,), W_IN_PIECES)
            add_edge_tiles(2)
            keep_weight().start()
            arrive((0, 1), OTHER_PIECES)

        @pl.when(jnp.logical_and(j == nj - 1, i == 0))
        def _():
            arrive((2,), OTHER_PIECES)

        @pl.when(j == 0)
        def _():
            xv = x_ref[...]
            r = lax.rsqrt(jnp.mean(xv * xv, axis=-1, keepdims=True) + EPS)
            h = (xv * r) * g_ref[...]
            h_all[rows, :] = h.astype(BF16)
            ht_ref[...] = h.T.astype(BF16)

        tile = order_ref[j]
        row = 0
        for k, start in enumerate(TILE_ROWS):
            row = row + jnp.where(tile == k, start // 32, 0)
        w_tile = w_all[pl.ds(pl.multiple_of(row * 32, 32), tn), :]
        proj_ref[...] = _dot_nt(h_all[rows, :], w_tile).astype(BF16)

        @pl.when(j == nj - 1)
        def _():
            lr_ref[...] = _dot_nt(h_all[rows, :], w_all[NAT_LR:NAT_LR + LR_W, :])

        @pl.when(jnp.logical_and(j == nj - 1, i == ni - 1))
        def _():
            everything = range(PIECES)
            passed_on = [cp for n in range(3) for cp in forwards(n, everything)]
            for cp in own_copies(everything) + relays(everything) + passed_on:
                cp.wait_send()
            for a in OTHER_PIECES:
                copy(a, 0, sibling, here).wait_recv()
            for cp in keep_own() + [keep_weight()]:
                cp.wait()

    const = lambda shape: pl.BlockSpec(shape, lambda j, i, order_ref: (0,) * len(shape))
    hbm = pl.BlockSpec(memory_space=pl.ANY)
    vmem = pl.BlockSpec(memory_space=pltpu.VMEM)
    return pl.pallas_call(
        body, name="gather_inproj",
        out_shape=(jax.ShapeDtypeStruct((seq, MAIN_W), BF16), jax.ShapeDtypeStruct((seq, LR_W), F32),
                   jax.ShapeDtypeStruct((D_MODEL, seq), BF16), jax.ShapeDtypeStruct((IN_W, D_MODEL), BF16),
                   jax.ShapeDtypeStruct((N_DEV,) + w_out_s.shape, BF16),
                   jax.ShapeDtypeStruct((N_DEV,) + small_s.shape, F32)),
        grid_spec=pltpu.PrefetchScalarGridSpec(
            num_scalar_prefetch=1, grid=(nj, ni),
            in_specs=[pl.BlockSpec((tm, D_MODEL), lambda j, i, order_ref: (first_sweep(j, i, order_ref), 0)),
                      const((1, D_MODEL)), vmem, const(w_out_s.shape), const(small_s.shape)],
            out_specs=(pl.BlockSpec((tm, tn), lambda j, i, order_ref: (i, order_ref[j])),
                       pl.BlockSpec((tm, LR_W), lambda j, i, order_ref: (last_sweep(j, i, order_ref), 0)),
                       pl.BlockSpec((D_MODEL, tm), lambda j, i, order_ref: (0, first_sweep(j, i, order_ref))),
                       hbm, hbm, hbm),
            scratch_shapes=[pltpu.VMEM((IN_W, D_MODEL), BF16), pltpu.VMEM((seq, D_MODEL), BF16),
                            pltpu.VMEM((N_DEV, 2, EDGE, D_MODEL), BF16),
                            pltpu.VMEM(w_out_s.shape, BF16), pltpu.VMEM(small_s.shape, F32),
                            pltpu.SemaphoreType.DMA((7 * PIECES,)), pltpu.SemaphoreType.DMA((7 * PIECES,)),
                            pltpu.SemaphoreType.DMA((3,))]),
        compiler_params=_cparams("arbitrary", "arbitrary"),
    )(order, x2d, norm_g, shifted, w_out_s, small_s)


def _block_masks(tt):
    row = lax.broadcasted_iota(jnp.int32, (tt, tt), 0)
    col = lax.broadcasted_iota(jnp.int32, (tt, tt), 1)
    same = jnp.right_shift(row, 6) == jnp.right_shift(col, 6)
    return (jnp.logical_and(same, col <= row), jnp.logical_and(same, col >= row), jnp.logical_and(same, col > row))


def _dot_split3(ones_mat, x):
    x1 = x.astype(BF16)
    r1 = x - x1.astype(F32)
    x2 = r1.astype(BF16)
    x3 = (r1 - x2.astype(F32)).astype(BF16)
    return (_dot(ones_mat, x3) + _dot(ones_mat, x2)) + _dot(ones_mat, x1)


def _log_gate(logits):
    return (jnp.minimum(logits, 0.0) - jnp.log(1.0 + jnp.exp(-jnp.abs(logits)))) * GATE_SCALE


def _chunk_column_mask(tt):
    nc = tt // CHUNK
    row = lax.broadcasted_iota(jnp.int32, (tt, nc * DK), 0)
    col = lax.broadcasted_iota(jnp.int32, (tt, nc * DK), 1)
    return jnp.right_shift(row, 6) == jnp.right_shift(col, 7)


def _chunked(mask, x, nc):
    wide = jnp.concatenate([x] * nc, axis=1)
    return jnp.where(mask, wide, jnp.zeros_like(wide))


def _gla_fwd(proj, lr, wgk_f, wgk_b, bgk_f, bgk_b, tt):
    seq = proj.shape[0]
    nb, nc, nch = seq // tt, tt // CHUNK, seq // CHUNK

    def body(qf, kf, vf, lrf, qb, kb, vb, lrb, wf, wb, bf, bb, of, ob, stf, stb, s_scr, qs_s, ks_s, qin_s, kout_s):
        @pl.when(pl.program_id(0) == 0)
        def _():
            s_scr[...] = jnp.zeros(s_scr.shape, F32)

        low, upp, sup = _block_masks(tt)
        dirs = ((qf, kf, vf, lrf, wf, bf, of, stf, low, low, REF_F, LAST_F, list(range(nc))),
                (qb, kb, vb, lrb, wb, bb, ob, stb, upp, sup, REF_B, LAST_B, list(reversed(range(nc)))))
        for d, (q_r, k_r, v_r, lr_r, w_r, b_r, o_r, st_r, cum, mask, ref, last, order) in enumerate(dirs):
            logits = _dot(lr_r[...].astype(BF16), w_r[...]) + b_r[...]
            b = _dot_split3(cum.astype(BF16), _log_gate(logits))
            decs = []
            for c in range(nc):
                rows = slice(c * CHUNK, (c + 1) * CHUNK)
                bc = b[rows]
                b_ref, b_last = bc[ref:ref + 1], bc[last:last + 1]
                qc = q_r[rows, :].astype(F32) * QSCALE
                kc = k_r[rows, :].astype(F32)
                qs_s[rows, :] = (qc * jnp.exp(bc - b_ref)).astype(BF16)
                ks_s[rows, :] = (kc * jnp.exp(b_ref - bc)).astype(BF16)
                qin_s[rows, :] = (qc * jnp.exp(bc)).astype(BF16)
                kout_s[rows, :] = (kc * jnp.exp(b_last - bc)).astype(BF16)
                decs.append(jnp.exp(b_last))
            for h in range(HEADS):
                ksl = slice(h * DK, (h + 1) * DK)
                vsl = slice(h * DV, (h + 1) * DV)
                v = v_r[:, vsl].astype(BF16)
                att = jnp.where(mask, _dot_nt(qs_s[:, ksl], ks_s[:, ksl]), 0.0).astype(BF16)
                o_intra = _dot(att, v)
                st = s_scr[d * HEADS + h]
                for c in order:
                    rows = slice(c * CHUNK, (c + 1) * CHUNK)
                    stb = st.astype(BF16)
                    st_r[c, h] = stb
                    o_r[rows, vsl] = (o_intra[rows] + _dot_nt(qin_s[rows, ksl], stb)).astype(BF16)
                    st = st * decs[c][:, ksl] + _dot_tn(v[rows], kout_s[rows, ksl])
                s_scr[d * HEADS + h] = st

    fw = lambda i: (i, 0)
    bw = lambda i: (nb - 1 - i, 0)
    const = lambda i: (0, 0)

    def tok_specs(m):
        return [pl.BlockSpec((tt, QK_W), lambda i: (m(i)[0], OFF_Q // QK_W)),
                pl.BlockSpec((tt, QK_W), lambda i: (m(i)[0], OFF_K // QK_W)),
                pl.BlockSpec((tt, V_W), lambda i: (m(i)[0], OFF_V // V_W)),
                pl.BlockSpec((tt, LR_W), m)]

    st_shape = jax.ShapeDtypeStruct((nch, HEADS, DV, DK), BF16)
    o_shape = jax.ShapeDtypeStruct((seq, V_W), BF16)
    operand = pltpu.VMEM((tt, QK_W), BF16)
    return pl.pallas_call(
        body, name="gla_fwd",
        out_shape=(o_shape, o_shape, st_shape, st_shape),
        grid=(nb,),
        in_specs=tok_specs(fw) + tok_specs(bw) + [
            pl.BlockSpec((LR_W, QK_W), const), pl.BlockSpec((LR_W, QK_W), const),
            pl.BlockSpec((1, QK_W), const), pl.BlockSpec((1, QK_W), const)],
        out_specs=(pl.BlockSpec((tt, V_W), fw), pl.BlockSpec((tt, V_W), bw),
                   pl.BlockSpec((nc, HEADS, DV, DK), lambda i: (i, 0, 0, 0)),
                   pl.BlockSpec((nc, HEADS, DV, DK), lambda i: (nb - 1 - i, 0, 0, 0))),
        scratch_shapes=[pltpu.VMEM((2 * HEADS, DV, DK), F32), operand, operand, operand, operand],
        compiler_params=_cparams("arbitrary"),
    )(proj, proj, proj, lr, proj, proj, proj, lr, wgk_f, wgk_b, bgk_f, bgk_b)


def _head_norm(o, gain):
    outs, rinv = [], []
    for h in range(HEADS):
        oh = o[:, h * DV:(h + 1) * DV]
        r = lax.rsqrt(jnp.mean(oh * oh, axis=-1, keepdims=True) + EPS)
        outs.append((oh * r) * gain)
        rinv.append(r)
    return jnp.concatenate(outs, axis=1), rinv


def _shift_rows(u, prev_row, next_row):
    n = u.shape[0]
    row = lax.broadcasted_iota(jnp.int32, (n, 1), 0)
    up = jnp.where(row == 0, prev_row, pltpu.roll(u, 1, 0))
    un = jnp.where(row == n - 1, next_row, pltpu.roll(u, n - 1, 0))
    return up, un


HALO = 16


def _halo_specs(tm, seq, col_block):
    per = tm // HALO
    last = seq // HALO - 1
    return [pl.BlockSpec((HALO, CONV_W), lambda i: (jnp.maximum(i * per - 1, 0), col_block)),
            pl.BlockSpec((HALO, CONV_W), lambda i: (jnp.minimum((i + 1) * per, last), col_block))]


def _f32(ref):
    return ref[...].astype(F32)


def _last_row(ref):
    return ref[HALO - 1:HALO, :].astype(F32)


def _first_row(ref):
    return ref[0:1, :].astype(F32)


def _mix_out_loss(o_f, o_b, proj, x2d, tgt, gla_g, conv_w, conv_b, w_out, final_g, tm):
    seq = x2d.shape[0]
    nt = seq // tm

    def body(of, ob, za, bg, cg, hc, zc, cprev, cnext, hprev, hnext, x_ref, t_ref, gg, cw, cb, wo, fg,
             yt_ref, conv_ref, dx2_ref, dx2b_ref, loss_ref, dfg_ref):
        i = pl.program_id(0)

        @pl.when(i == 0)
        def _():
            loss_ref[...] = jnp.zeros(loss_ref.shape, F32)
            dfg_ref[...] = jnp.zeros(dfg_ref.shape, F32)

        on, _ = _head_norm(_f32(of) + _f32(ob), gg[...])
        zav = _f32(za)
        y_a = on * (zav * _sigmoid(zav))
        u = _f32(cg) * _f32(hc)
        prev_row = jnp.where(i > 0, _last_row(cprev) * _last_row(hprev), 0.0)
        next_row = jnp.where(i < nt - 1, _first_row(cnext) * _first_row(hnext), 0.0)
        up, un = _shift_rows(u, prev_row, next_row)
        conv = (cw[0:1, :] * up + cw[1:2, :] * u + cw[2:3, :] * un) + cb[...]
        conv_ref[...] = conv.astype(BF16)
        zcv = _f32(zc)
        y_c = _f32(bg) * conv * (zcv * _sigmoid(zcv))
        y = jnp.concatenate([y_a, y_c], axis=1)
        yt_ref[...] = y.T.astype(BF16)
        x2 = x_ref[...] + _dot(y.astype(BF16), wo[...])
        r = lax.rsqrt(jnp.mean(x2 * x2, axis=-1, keepdims=True) + EPS)
        xn = x2 * r
        err = xn * fg[...] - t_ref[...]
        loss_ref[...] += 0.5 * jnp.sum(jnp.mean(err * err, axis=-1, keepdims=True))
        dyf = err * (1.0 / D_MODEL)
        dfg_ref[...] += jnp.sum(dyf * xn, axis=0, keepdims=True)
        dxn = dyf * fg[...]
        dx2 = r * dxn - xn * (r * jnp.mean(dxn * xn, axis=-1, keepdims=True))
        dx2_ref[...] = dx2
        dx2b_ref[...] = dx2.astype(BF16)

    def col(off):
        return pl.BlockSpec((tm, CONV_W), lambda i: (i, off // CONV_W))

    rowt = pl.BlockSpec((tm, D_MODEL), lambda i: (i, 0))
    const = lambda shape: pl.BlockSpec(shape, lambda i: (0, 0))
    return pl.pallas_call(
        body, name="mix_out_loss",
        out_shape=(jax.ShapeDtypeStruct((MIX_W, seq), BF16), jax.ShapeDtypeStruct((seq, CONV_W), BF16),
                   jax.ShapeDtypeStruct((seq, D_MODEL), F32), jax.ShapeDtypeStruct((seq, D_MODEL), BF16),
                   jax.ShapeDtypeStruct((8, 128), F32), jax.ShapeDtypeStruct((1, D_MODEL), F32)),
        grid=(nt,),
        in_specs=[rowt, rowt, col(OFF_ZA), col(OFF_B), col(OFF_C), col(OFF_H), col(OFF_ZC)]
        + _halo_specs(tm, seq, OFF_C // CONV_W) + _halo_specs(tm, seq, OFF_H // CONV_W)
        + [rowt, rowt, const((1, DV)), const((8, CONV_W)), const((1, CONV_W)), const((MIX_W, D_MODEL)),
           const((1, D_MODEL))],
        out_specs=(pl.BlockSpec((MIX_W, tm), lambda i: (0, i)), rowt, rowt, rowt, const((8, 128)),
                   const((1, D_MODEL))),
        compiler_params=_cparams("arbitrary"),
    )(o_f, o_b, proj, proj, proj, proj, proj, proj, proj, proj, proj, x2d, tgt, gla_g, conv_w, conv_b, w_out, final_g)


def _dsilu(z, s):
    return s * (1.0 + z * (1.0 - s))


def _mix_bwd(dx2b, o_f, o_b, proj, conv, gla_g, w_out, tm):
    seq = dx2b.shape[0]

    def body(dx, of, ob, za, bg, zc, cv, gg, wo, dg_ref, do_ref, dconv_ref, dgg_ref, dcb_ref):
        @pl.when(pl.program_id(0) == 0)
        def _():
            dgg_ref[...] = jnp.zeros(dgg_ref.shape, F32)
            dcb_ref[...] = jnp.zeros(dcb_ref.shape, F32)

        dy = _dot_nt(dx[...], wo[...])
        dy_a, dy_c = dy[:, :V_W], dy[:, V_W:]
        zcv, bgv, convv = _f32(zc), _f32(bg), _f32(cv)
        sc = _sigmoid(zcv)
        szc = zcv * sc
        dg_ref[:, CONV_W:2 * CONV_W] = (dy_c * convv * szc).astype(BF16)
        dconv = dy_c * bgv * szc
        dconv_ref[...] = dconv.astype(BF16)
        dcb_ref[...] += jnp.sum(dconv, axis=0, keepdims=True)
        dg_ref[:, 2 * CONV_W:] = (dy_c * bgv * convv * _dsilu(zcv, sc)).astype(BF16)

        o = _f32(of) + _f32(ob)
        gain = gg[...]
        on, rinv = _head_norm(o, gain)
        zav = _f32(za)
        sa = _sigmoid(zav)
        dg_ref[:, :CONV_W] = (dy_a * on * _dsilu(zav, sa)).astype(BF16)
        don = dy_a * (zav * sa)
        dgg = jnp.zeros((1, DV), F32)
        dos = []
        for h in range(HEADS):
            sl = slice(h * DV, (h + 1) * DV)
            oh, r, dh = o[:, sl], rinv[h], don[:, sl]
            ohn = oh * r
            dgg = dgg + jnp.sum(dh * ohn, axis=0, keepdims=True)
            dn = dh * gain
            dos.append(r * dn - ohn * (r * jnp.mean(dn * ohn, axis=-1, keepdims=True)))
        dgg_ref[...] += dgg
        do_ref[...] = jnp.concatenate(dos, axis=1).astype(BF16)

    def col(off):
        return pl.BlockSpec((tm, CONV_W), lambda i: (i, off // CONV_W))

    rowt = pl.BlockSpec((tm, D_MODEL), lambda i: (i, 0))
    const = lambda shape: pl.BlockSpec(shape, lambda i: (0, 0))
    return pl.pallas_call(
        body, name="mix_bwd",
        out_shape=(jax.ShapeDtypeStruct((seq, GATES_W), BF16), jax.ShapeDtypeStruct((seq, V_W), BF16),
                   jax.ShapeDtypeStruct((seq, CONV_W), BF16),
                   jax.ShapeDtypeStruct((1, DV), F32), jax.ShapeDtypeStruct((1, CONV_W), F32)),
        grid=(seq // tm,),
        in_specs=[rowt, rowt, rowt, col(OFF_ZA), col(OFF_B), col(OFF_ZC), rowt, const((1, DV)),
                  const((MIX_W, D_MODEL))],
        out_specs=(pl.BlockSpec((tm, GATES_W), lambda i: (i, 0)), rowt, rowt, const((1, DV)), const((1, CONV_W))),
        compiler_params=_cparams("arbitrary"),
    )(dx2b, o_f, o_b, proj, proj, proj, conv, gla_g, w_out)


def _conv_bwd(dconv, proj, conv_w, tm):
    seq = dconv.shape[0]
    nt = seq // tm

    def body(dc_in, dprev, dnext, cg, hc, cprev, cnext, hprev, hnext, cw, dch_ref, dcw_ref):
        i = pl.program_id(0)

        @pl.when(i == 0)
        def _():
            dcw_ref[...] = jnp.zeros(dcw_ref.shape, F32)

        first, lastt = i > 0, i < nt - 1
        dcv = _f32(dc_in)
        d_up, d_un = _shift_rows(dcv, jnp.where(first, _last_row(dprev), 0.0), jnp.where(lastt, _first_row(dnext), 0.0))
        cgv, hcv = _f32(cg), _f32(hc)
        u = cgv * hcv
        u_up, u_un = _shift_rows(u, jnp.where(first, _last_row(cprev) * _last_row(hprev), 0.0),
                                 jnp.where(lastt, _first_row(cnext) * _first_row(hnext), 0.0))
        du = cw[0:1, :] * d_un + cw[1:2, :] * dcv + cw[2:3, :] * d_up
        dch_ref[:, :CONV_W] = (du * hcv).astype(BF16)
        dch_ref[:, CONV_W:] = (du * cgv).astype(BF16)
        dcw_ref[0:1, :] += jnp.sum(dcv * u_up, axis=0, keepdims=True)
        dcw_ref[1:2, :] += jnp.sum(dcv * u, axis=0, keepdims=True)
        dcw_ref[2:3, :] += jnp.sum(dcv * u_un, axis=0, keepdims=True)

    def col(off):
        return pl.BlockSpec((tm, CONV_W), lambda i: (i, off // CONV_W))

    rowt = pl.BlockSpec((tm, CONV_W), lambda i: (i, 0))
    const = lambda shape: pl.BlockSpec(shape, lambda i: (0, 0))
    return pl.pallas_call(
        body, name="conv_bwd",
        out_shape=(jax.ShapeDtypeStruct((seq, CH_W), BF16), jax.ShapeDtypeStruct((8, CONV_W), F32)),
        grid=(nt,),
        in_specs=[rowt] + _halo_specs(tm, seq, 0) + [col(OFF_C), col(OFF_H)]
        + _halo_specs(tm, seq, OFF_C // CONV_W) + _halo_specs(tm, seq, OFF_H // CONV_W) + [const((8, CONV_W))],
        out_specs=(pl.BlockSpec((tm, CH_W), lambda i: (i, 0)), const((8, CONV_W))),
        compiler_params=_cparams("arbitrary"),
    )(dconv, dconv, dconv, proj, proj, proj, proj, proj, proj, conv_w)


def _gla_bwd(proj, lr, do, st_f, st_b, wgk_f, wgk_b, bgk_f, bgk_b, tt):
    seq = proj.shape[0]
    nb, nc = seq // tt, tt // CHUNK

    def body(qf, kf, vf, lrf, dof, stf, qb, kb, vb, lrb, dob, stb, wf, wb, bf, bb,
             dqkv_f, dlr_f, dqkv_b, dlr_b, dwf, dwb, dbf, dbb,
             ds_scr, eq_s, ek_s, ein_s, eout_s, qs_s, ks_s, qin_s, kout_s, db_s, lg_s):
        @pl.when(pl.program_id(0) == 0)
        def _():
            ds_scr[...] = jnp.zeros(ds_scr.shape, F32)
            for r in (dwf, dwb, dbf, dbb):
                r[...] = jnp.zeros(r.shape, F32)

        low, upp, sup = _block_masks(tt)
        row = lax.broadcasted_iota(jnp.int32, (CHUNK, 1), 0)
        kmask = _chunk_column_mask(tt)
        dirs = ((qf, kf, vf, lrf, dof, stf, wf, bf, dqkv_f, dlr_f, dwf, dbf,
                 low, upp, low, REF_F, LAST_F, list(reversed(range(nc)))),
                (qb, kb, vb, lrb, dob, stb, wb, bb, dqkv_b, dlr_b, dwb, dbb,
                 upp, low, sup, REF_B, LAST_B, list(range(nc))))
        for d, (q_r, k_r, v_r, lr_r, do_r, st_r, w_r, b_r, dqkv_r, dlr_r, dw_r, db_r,
                cum, cum_t, mask, ref, last, order) in enumerate(dirs):
            lrv = lr_r[...].astype(BF16)
            wv = w_r[...]
            logits = _dot(lrv, wv) + b_r[...]
            lg_s[...] = logits
            b = _dot_split3(cum.astype(BF16), _log_gate(logits))
            decs = []
            for c in range(nc):
                rows = slice(c * CHUNK, (c + 1) * CHUNK)
                bc = b[rows]
                b_ref, b_last = bc[ref:ref + 1], bc[last:last + 1]
                qc = q_r[rows, :].astype(F32) * QSCALE
                kc = k_r[rows, :].astype(F32)
                e_q, e_k, e_in, e_out = jnp.exp(bc - b_ref), jnp.exp(b_ref - bc), jnp.exp(bc), jnp.exp(b_last - bc)
                eq_s[rows, :], ek_s[rows, :], ein_s[rows, :], eout_s[rows, :] = e_q, e_k, e_in, e_out
                qs_s[rows, :] = (qc * e_q).astype(BF16)
                ks_s[rows, :] = (kc * e_k).astype(BF16)
                qin_s[rows, :] = (qc * e_in).astype(BF16)
                kout_s[rows, :] = (kc * e_out).astype(BF16)
                decs.append(jnp.exp(b_last))
            for h in range(HEADS):
                ksl = slice(h * DK, (h + 1) * DK)
                vsl = slice(h * DV, (h + 1) * DV)
                v = v_r[:, vsl].astype(BF16)
                dov = do_r[:, vsl].astype(BF16)
                qsb, ksb = qs_s[:, ksl], ks_s[:, ksl]
                att = jnp.where(mask, _dot_nt(qsb, ksb), 0.0).astype(BF16)
                datt = jnp.where(mask, _dot_nt(dov, v), 0.0).astype(BF16)
                dqs = _dot(datt, ksb)
                dks = _dot_tn(datt, qsb)
                dv_intra = _dot_tn(att, dov)
                g_t = _dot_tn(dov, _chunked(kmask, qin_s[:, ksl], nc))
                ds = ds_scr[d * HEADS + h]
                for c in order:
                    rows = slice(c * CHUNK, (c + 1) * CHUNK)
                    dsb = ds.astype(BF16)
                    s_prev = st_r[c, h]
                    dk_out = _dot(v[rows], dsb)
                    dq_in = _dot(dov[rows], s_prev)
                    dv = dv_intra[rows] + _dot_nt(kout_s[rows, ksl], dsb)
                    dqkv_r[rows, OFF_V + h * DV:OFF_V + (h + 1) * DV] = dv.astype(BF16)
                    dec = decs[c][:, ksl]
                    ddec = jnp.sum(ds * s_prev.astype(F32), axis=0, keepdims=True)
                    e_out = eout_s[rows, ksl]
                    qc = q_r[rows, ksl].astype(F32) * QSCALE
                    kc = k_r[rows, ksl].astype(F32)
                    dq = dqs[rows] * eq_s[rows, ksl] + dq_in * ein_s[rows, ksl]
                    dk = dks[rows] * ek_s[rows, ksl] + dk_out * e_out
                    dqkv_r[rows, OFF_Q + h * DK:OFF_Q + (h + 1) * DK] = (dq * QSCALE).astype(BF16)
                    dqkv_r[rows, OFF_K + h * DK:OFF_K + (h + 1) * DK] = dk.astype(BF16)
                    tail = jnp.sum(dk_out * (kc * e_out), axis=0, keepdims=True) + ddec * dec
                    db_s[rows, ksl] = (qc * dq - kc * dk) + jnp.where(row == last, tail, 0.0)
                    ds = ds * dec + g_t[:, c * DK:(c + 1) * DK]
                ds_scr[d * HEADS + h] = ds
            dg = _dot_split3(cum_t.astype(BF16), db_s[...])
            dlogit = (dg * GATE_SCALE) * _sigmoid(-lg_s[...])
            dlb = dlogit.astype(BF16)
            dlr_r[...] = _dot_nt(dlb, wv)
            dw_r[...] += _dot_tn(lrv, dlb)
            db_r[...] += jnp.sum(dlogit, axis=0, keepdims=True)

    fw = lambda i: (nb - 1 - i, 0)
    bw = lambda i: (i, 0)
    const = lambda i: (0, 0)

    def tok_specs(m):
        return [pl.BlockSpec((tt, QK_W), lambda i: (m(i)[0], OFF_Q // QK_W)),
                pl.BlockSpec((tt, QK_W), lambda i: (m(i)[0], OFF_K // QK_W)),
                pl.BlockSpec((tt, V_W), lambda i: (m(i)[0], OFF_V // V_W)),
                pl.BlockSpec((tt, LR_W), m),
                pl.BlockSpec((tt, V_W), m),
                pl.BlockSpec((nc, HEADS, DV, DK), lambda i: (m(i)[0], 0, 0, 0))]

    dqkv = jax.ShapeDtypeStruct((seq, QK_W + QK_W + V_W), BF16)
    dlr = jax.ShapeDtypeStruct((seq, LR_W), F32)
    dw = jax.ShapeDtypeStruct((LR_W, QK_W), F32)
    dbias = jax.ShapeDtypeStruct((1, QK_W), F32)
    return pl.pallas_call(
        body, name="gla_bwd",
        out_shape=(dqkv, dlr, dqkv, dlr, dw, dw, dbias, dbias),
        grid=(nb,),
        in_specs=tok_specs(fw) + tok_specs(bw) + [
            pl.BlockSpec((LR_W, QK_W), const), pl.BlockSpec((LR_W, QK_W), const),
            pl.BlockSpec((1, QK_W), const), pl.BlockSpec((1, QK_W), const)],
        out_specs=(pl.BlockSpec((tt, QK_W + QK_W + V_W), fw), pl.BlockSpec((tt, LR_W), fw),
                   pl.BlockSpec((tt, QK_W + QK_W + V_W), bw), pl.BlockSpec((tt, LR_W), bw),
                   pl.BlockSpec((LR_W, QK_W), const), pl.BlockSpec((LR_W, QK_W), const),
                   pl.BlockSpec((1, QK_W), const), pl.BlockSpec((1, QK_W), const)),
        scratch_shapes=[pltpu.VMEM((2 * HEADS, DV, DK), F32)] + [pltpu.VMEM((tt, QK_W), F32)] * 4
        + [pltpu.VMEM((tt, QK_W), BF16)] * 4 + [pltpu.VMEM((tt, QK_W), F32)] * 2,
        compiler_params=_cparams("arbitrary"),
    )(proj, proj, proj, lr, do, st_f, proj, proj, proj, lr, do, st_b, wgk_f, wgk_b, bgk_f, bgk_b)


def _both_directions(f_ref, b_ref):
    return (_f32(f_ref) + _f32(b_ref)).astype(BF16)


def _input_grad(dqkv_f, dqkv_b, dp_gates, dp_ch, dlr_f, dlr_b, w_nat, x2d, norm_g, dx2, sums, tm):
    seq = x2d.shape[0]
    nt, n = seq // tm, len(sums)
    relay_step = (3 * nt) // 8

    def body(dqf, dqb, dg, dc, dlf, dlb, w, x_ref, g_ref, dx2_ref, *rest):
        ins, (gx_ref, dng_ref), outs = rest[:n], rest[n:n + 2], rest[n + 2:2 * n + 2]
        passing, joined = rest[2 * n + 2:3 * n + 2], rest[3 * n + 2:4 * n + 2]
        send_sems, recv_sems, local_sems = rest[4 * n + 2:]
        i = pl.program_id(0)
        c = lax.axis_index("c")
        first, second, diagonal = _route_chips()
        slot = lambda chip: 2 * chip[0] + chip[1]

        def remote(a, k, src, dst, to):
            return pltpu.make_async_remote_copy(src_ref=src, dst_ref=dst, send_sem=send_sems.at[3 * a + k],
                                                recv_sem=recv_sems.at[3 * a + k], device_id=(*to, c),
                                                device_id_type=MESH)

        direct = lambda a: remote(a, 0, ins[a].at[slot(first)], outs[a].at[0], first)
        for_second = lambda a: remote(a, 1, ins[a].at[slot(diagonal)], passing[a], first)
        joint = lambda a: remote(a, 2, joined[a], outs[a].at[1], second)
        own = lambda a: pltpu.make_async_copy(ins[a].at[slot(second)], joined[a], local_sems.at[a])

        @pl.when(i == 0)
        def _():
            _start_all([for_second(a) for a in range(n)] + [own(a) for a in range(n)] + [direct(a) for a in range(n)])
            dng_ref[...] = jnp.zeros(dng_ref.shape, F32)

        @pl.when(i == relay_step)
        def _():
            for a in range(n):
                for_second(a).wait_recv()
                own(a).wait()
                joined[a][...] = (joined[a][...].astype(F32) + passing[a][...].astype(F32)).astype(BF16)
                joint(a).start()

        dh = (_dot((dlf[...] + dlb[...]).astype(BF16), w[NAT_LR:NAT_LR + LR_W, :])
              + _dot(_both_directions(dqf, dqb), w[0:NAT_ZA, :])
              + _dot(dg[:, 0:CONV_W], w[NAT_ZA:NAT_LR, :]) + _dot(dg[:, CONV_W:2 * CONV_W], w[NAT_B:NAT_C, :])
              + _dot(dg[:, 2 * CONV_W:], w[NAT_ZC:IN_W, :]) + _dot(dc[...], w[NAT_C:NAT_ZC, :]))
        xv = x_ref[...]
        r = lax.rsqrt(jnp.mean(xv * xv, axis=-1, keepdims=True) + EPS)
        xn = xv * r
        dng_ref[...] += jnp.sum(dh * xn, axis=0, keepdims=True)
        dn = dh * g_ref[...]
        gx_ref[...] = (r * dn - xn * (r * jnp.mean(dn * xn, axis=-1, keepdims=True))) + dx2_ref[...]

        @pl.when(i == nt - 1)
        def _():
            for a in range(n):
                direct(a).wait_recv()
                joint(a).wait_recv()
            for a in range(n):
                for cp in (direct(a), for_second(a), joint(a)):
                    cp.wait_send()

    rowt = pl.BlockSpec((tm, D_MODEL), lambda i: (i, 0))
    seg = lambda width: pl.BlockSpec((tm, width), lambda i: (i, 0))
    resident = lambda rows: pl.BlockSpec((rows, D_MODEL), lambda i: (0, 0), pipeline_mode=pl.Buffered(1))
    hbm = pl.BlockSpec(memory_space=pl.ANY)
    blocks = [pltpu.VMEM(s.shape[1:], s.dtype) for s in sums]
    return pl.pallas_call(
        body, name="input_grad",
        out_shape=(jax.ShapeDtypeStruct((seq, D_MODEL), F32), jax.ShapeDtypeStruct((1, D_MODEL), F32))
        + tuple(jax.ShapeDtypeStruct((2,) + s.shape[1:], s.dtype) for s in sums),
        grid=(nt,),
        in_specs=[seg(QKV_W), seg(QKV_W), seg(GATES_W), seg(CH_W), seg(LR_W), seg(LR_W), resident(IN_W),
                  rowt, pl.BlockSpec((1, D_MODEL), lambda i: (0, 0)), rowt] + [hbm] * n,
        out_specs=(rowt, pl.BlockSpec((1, D_MODEL), lambda i: (0, 0))) + (hbm,) * n,
        scratch_shapes=blocks + blocks + [pltpu.SemaphoreType.DMA((3 * n,)), pltpu.SemaphoreType.DMA((3 * n,)),
                                          pltpu.SemaphoreType.DMA((n,))],
        compiler_params=_cparams("arbitrary"),
    )(dqkv_f, dqkv_b, dp_gates, dp_ch, dlr_f, dlr_b, w_nat, x2d, norm_g, dx2, *sums)


def _weight_grad_out(y_t, dx2b, tk, riding):
    m, seq = y_t.shape
    n = dx2b.shape[1]
    nk = seq // tk

    def body(a_ref, b_ref, ride_in, o_ref, ride_out, send_sems, recv_sems):
        k = pl.program_id(0)

        @pl.when(k == 0)
        def _():
            _start_all(_sibling_copies(ride_in, ride_out, send_sems, recv_sems))
            o_ref[...] = jnp.zeros(o_ref.shape, F32)

        o_ref[...] += _dot(a_ref[...], b_ref[...])

        @pl.when(k == nk - 1)
        def _():
            _wait_all(_sibling_copies(ride_in, ride_out, send_sems, recv_sems))

    hbm = pl.BlockSpec(memory_space=pl.ANY)
    return pl.pallas_call(
        body, name="wgrad_out",
        out_shape=(jax.ShapeDtypeStruct((m, n), F32), jax.ShapeDtypeStruct((4,) + _block_shape(riding), F32)),
        grid=(nk,),
        in_specs=[pl.BlockSpec((m, tk), lambda k: (0, k)), pl.BlockSpec((tk, n), lambda k: (k, 0)), hbm],
        out_specs=(pl.BlockSpec((m, n), lambda k: (0, 0)), hbm),
        scratch_shapes=[pltpu.SemaphoreType.DMA((4,)), pltpu.SemaphoreType.DMA((4,))],
        compiler_params=_cparams("arbitrary"),
    )(y_t, dx2b, riding)


def _weight_grad_in(h_t, dqkv_f, dqkv_b, dp_gates, dp_ch, dlr_f, dlr_b):
    m, seq = h_t.shape
    tn = 512
    n_qkv, n_gates, n_ch = QKV_W // tn, GATES_W // tn, CH_W // tn
    starts = ([k * tn for k in range(n_qkv)] + [NAT_ZA, NAT_ZA + tn, NAT_B, NAT_B + tn, NAT_ZC, NAT_ZC + tn]
              + [NAT_C + k * tn for k in range(n_ch)])

    def out_row(j):
        row = 0
        for k, start in enumerate(starts):
            row = row + jnp.where(j == k, start // 32, 0)
        return pl.multiple_of(row * 32, 32), 0

    def body(a_ref, bqf, bqb, bg, bc, o_ref, acc, bq):
        j = pl.program_id(0)

        @pl.when(j < n_qkv)
        def _():
            bq[...] = _both_directions(bqf, bqb)
            acc[...] = _dot(a_ref[...], bq[...])

        @pl.when(jnp.logical_and(j >= n_qkv, j < n_qkv + n_gates))
        def _():
            acc[...] = _dot(a_ref[...], bg[...])

        @pl.when(j >= n_qkv + n_gates)
        def _():
            acc[...] = _dot(a_ref[...], bc[...])

        o_ref[...] = acc[...].T

    resident = pl.BlockSpec((m, seq), lambda j: (0, 0), pipeline_mode=pl.Buffered(1))
    seg = lambda first, count: pl.BlockSpec((seq, tn), lambda j: (0, jnp.clip(j - first, 0, count - 1)))
    main = pl.pallas_call(
        body, name="wgrad_in",
        out_shape=jax.ShapeDtypeStruct((IN_W, m), F32),
        grid=(n_qkv + n_gates + n_ch,),
        in_specs=[resident, seg(0, n_qkv), seg(0, n_qkv), seg(n_qkv, n_gates), seg(n_qkv + n_gates, n_ch)],
        out_specs=pl.BlockSpec((pl.Element(tn), pl.Element(m)), out_row),
        scratch_shapes=[pltpu.VMEM((m, tn), F32), pltpu.VMEM((seq, tn), BF16)],
        compiler_params=_cparams("arbitrary"),
    )(h_t, dqkv_f, dqkv_b, dp_gates, dp_ch)

    def lr_body(a_ref, bf_ref, bb_ref, full_ref, o_ref, acc):
        acc[...] = _dot(a_ref[...], (bf_ref[...] + bb_ref[...]).astype(BF16))
        o_ref[...] = acc[...].T[0:2 * RANK, :]

    whole = lambda shape: pl.BlockSpec(shape, lambda j: (0, 0))
    return pl.pallas_call(
        lr_body, name="wgrad_lr",
        out_shape=jax.ShapeDtypeStruct((IN_W, m), F32),
        grid=(1,),
        in_specs=[whole((m, seq)), whole((seq, LR_W)), whole((seq, LR_W)), pl.BlockSpec(memory_space=pl.ANY)],
        out_specs=pl.BlockSpec((pl.Element(2 * RANK), pl.Element(m)), lambda j: (NAT_LR, 0)),
        scratch_shapes=[pltpu.VMEM((m, LR_W), F32)],
        input_output_aliases={3: 0},
        compiler_params=_cparams("arbitrary"),
    )(h_t, dlr_f, dlr_b, main)


def _pad_rows(a, rows):
    return jnp.pad(a, ((0, rows - a.shape[0]), (0, 0)))


def _rows128(a):
    a = a.reshape(-1, 128)
    return _pad_rows(a, -(-a.shape[0] // 8) * 8)


def _pack(arrs):
    return jnp.concatenate([_rows128(a) for a in arrs], axis=0)


def _unpack(buf, like):
    out, start = [], 0
    for a in like:
        rows = a.size // 128
        out.append(buf[start:start + rows].reshape(a.shape))
        start += -(-rows // 8) * 8
    return out


def kernel(x, norm_g, w_in, w_gk_f, b_gk_f, w_gk_b, b_gk_b, gla_norm_g, conv_w, conv_b, w_out, final_g, loss_target, m_norm_g, m_w_in, m_w_gk_f, m_b_gk_f, m_w_gk_b, m_b_gk_b, m_gla_norm_g, m_conv_w, m_conv_b, m_w_out, m_final_g, v_norm_g, v_w_in, v_w_gk_f, v_b_gk_f, v_w_gk_b, v_b_gk_b, v_gla_norm_g, v_conv_w, v_conv_b, v_w_out, v_final_g):
    px, py, pc = _position()
    me = _blk(px, py, pc)
    seq = x.shape[1]
    x2d, tgt = x[0], loss_target[0]
    tt = min(256, seq)

    small_s = jnp.concatenate([jnp.concatenate([w_gk_f[0], w_gk_b[0]], axis=1), _pad_rows(conv_w[0], 8)], axis=0)
    shifted = lax.dynamic_update_slice(jnp.zeros((SHIFTED_ROWS, D_MODEL), F32), w_in[0].T, (4 * (me % 4), 0))
    order = sum(jnp.where(2 * px + py == k, jnp.asarray(tiles + (0,), jnp.int32), 0) for k, tiles in enumerate(TILE_ORDER))
    proj, lr, h_t, w_nat, wout_all, small_all = _gather_inproj(x2d, norm_g, shifted, w_out[0], small_s, order,
                                                               min(1024, seq))
    w_out_full = wout_all.reshape(MIX_W, D_MODEL)
    wgk_cols = 512 // N_DEV
    wgk_f_full = small_all[:, 0:RANK, 0:wgk_cols].transpose(1, 0, 2).reshape(RANK, QK_W)
    wgk_b_full = small_all[:, 0:RANK, wgk_cols:2 * wgk_cols].transpose(1, 0, 2).reshape(RANK, QK_W)
    conv_w_full = _pad_rows(small_all[:, RANK:RANK + 3, :].transpose(1, 0, 2).reshape(3, CONV_W), 8)
    zr = lambda n: jnp.zeros((n, QK_W), F32)
    wgk_f_pad = jnp.concatenate([wgk_f_full, zr(LR_W - RANK)], axis=0).astype(BF16)
    wgk_b_pad = jnp.concatenate([zr(RANK), wgk_b_full, zr(LR_W - 2 * RANK)], axis=0).astype(BF16)

    o_f, o_b, st_f, st_b = _gla_fwd(proj, lr, wgk_f_pad, wgk_b_pad, b_gk_f, b_gk_b, tt)
    tmix = min(256, seq)
    y_t, conv, dx2, dx2b, loss_p, dfg_p = _mix_out_loss(o_f, o_b, proj, x2d, tgt, gla_norm_g, conv_w_full, conv_b,
                                                        w_out_full, final_g.reshape(1, D_MODEL), tmix)

    dp_gates, do, dconv, dgg_p, dcb_p = _mix_bwd(dx2b, o_f, o_b, proj, conv, gla_norm_g, w_out_full, tmix)
    dp_ch, dcw_p = _conv_bwd(dconv, proj, conv_w_full, tmix)
    dqkv_f, dlr_f, dqkv_b, dlr_b, dwf_p, dwb_p, dbf_p, dbb_p = _gla_bwd(
        proj, lr, do, st_f, st_b, wgk_f_pad, wgk_b_pad, b_gk_f, b_gk_b, tt)
    dw_nat = _weight_grad_in(h_t, dqkv_f, dqkv_b, dp_gates, dp_ch, dlr_f, dlr_b)

    dw_out, sib_in = _weight_grad_out(y_t, dx2b, min(1024, seq), dw_nat)
    part_out = dw_out.reshape(N_DEV, MIX_W // N_DEV, D_MODEL)
    core = jnp.reshape(pc, (1,)).astype(jnp.int32)
    chip = jnp.reshape(2 * px + py, (1,)).astype(jnp.int32)
    sums_in, sib_out = _chip_sums(dw_nat, sib_in, core, 512, "chip_sums_in", riding=part_out)
    sums_out = _chip_sums(part_out, sib_out, core, D_MODEL, "chip_sums_out")
    grad_x2d, dng_p, far_in, far_out = _input_grad(dqkv_f, dqkv_b, dp_gates, dp_ch, dlr_f, dlr_b, w_nat, x2d, norm_g, dx2,
                                                   [sums_in, sums_out], tmix)
    pieces = [dng_p, dbf_p, dbb_p, dgg_p, dcb_p, dfg_p[0], dwf_p[0:RANK], dwb_p[RANK:2 * RANK], dcw_p[0:3], loss_p[0]]
    g_window, small_tot = _final_sum(sums_in, far_in, chip, _pack(pieces), 512, "final_sum_in")
    g_in_t = lax.dynamic_slice_in_dim(g_window, 4 * pc, SHARD_W, axis=0)
    g_w_out, d_w_out, nm_w_out, nv_w_out = _final_sum_adamw(sums_out, far_out, chip, w_out[0], m_w_out[0], v_w_out[0],
                                                            256, "adamw_out")
    flat = lambda a: a[0].T.reshape(SHARD_W, D_MODEL // 128, 128)
    unflat = lambda a: a.reshape(SHARD_W, D_MODEL).T
    d_flat, m_flat, v_flat = _adamw_rows(g_in_t.reshape(SHARD_W, D_MODEL // 128, 128), flat(w_in), flat(m_w_in),
                                         flat(v_w_in), 180, "adamw_in")
    g_w_in, d_w_in, nm_w_in, nv_w_in = g_in_t.T, unflat(d_flat), unflat(m_flat), unflat(v_flat)

    tot = _unpack(small_tot, pieces)
    g_norm_g, g_b_gk_f, g_b_gk_b, g_gla, g_conv_b, g_final = tot[:6]
    g_wgk_f = lax.dynamic_slice_in_dim(tot[6], me * wgk_cols, wgk_cols, axis=1)[None]
    g_wgk_b = lax.dynamic_slice_in_dim(tot[7], me * wgk_cols, wgk_cols, axis=1)[None]
    g_conv_w = lax.dynamic_slice_in_dim(tot[8], me * 128, 128, axis=1)[None]
    loss = tot[9][0]

    small_g = [g_norm_g, g_b_gk_f, g_b_gk_b, g_gla, g_conv_b, g_final, g_wgk_f, g_wgk_b, g_conv_w]
    small_w = [norm_g, b_gk_f, b_gk_b, gla_norm_g, conv_b, final_g, w_gk_f, w_gk_b, conv_w]
    small_m = [m_norm_g, m_b_gk_f, m_b_gk_b, m_gla_norm_g, m_conv_b, m_final_g, m_w_gk_f, m_w_gk_b, m_conv_w]
    small_v = [v_norm_g, v_b_gk_f, v_b_gk_b, v_gla_norm_g, v_conv_b, v_final_g, v_w_gk_f, v_w_gk_b, v_conv_w]
    d_s, m_s, v_s = _adamw_small(_pack(small_g), _pack(small_w), _pack(small_m), _pack(small_v))
    d_l, m_l, v_l = _unpack(d_s, small_w), _unpack(m_s, small_w), _unpack(v_s, small_w)

    def ordered(sm, big_in, big_out):
        return [sm[0], big_in[None], sm[6], sm[1], sm[7], sm[2], sm[3], sm[8], sm[4], big_out[None], sm[5]]

    grads = ordered(small_g, g_w_in, g_w_out)
    deltas = ordered(d_l, d_w_in, d_w_out)
    new_m = ordered(m_l, nm_w_in, nm_w_out)
    new_v = ordered(v_l, nv_w_in, nv_w_out)
    return (loss, grad_x2d[None], *grads, *deltas, *new_m, *new_v)
```

```python
import jax
import jax.numpy as jnp
from jax import lax
from jax.experimental import pallas as pl
from jax.experimental.pallas import tpu as pltpu

F32 = jnp.float32
BF16 = jnp.bfloat16
MESH = pl.DeviceIdType.MESH

N_DEV = 8
D_MODEL = 1024
HEADS = 4
DK = 128
DV = 256
QK_W = HEADS * DK
V_W = HEADS * DV
CONV_W = 1024
MIX_W = V_W + CONV_W
CHUNK = 64
RANK = 16
IN_W = 7200
SHARD_W = IN_W // N_DEV
MAIN_W = 7168
LR_W = 128
OFF_Q, OFF_K, OFF_V, OFF_ZA, OFF_B, OFF_ZC, OFF_C, OFF_H = 0, 512, 1024, 2048, 3072, 4096, 5120, 6144
QKV_W, GATES_W, CH_W = 2048, 3072, 2048
NAT_ZA, NAT_LR, NAT_B, NAT_C, NAT_ZC = 2048, 3072, 3104, 4128, 6176
EPS = 1e-6
GATE_SCALE = 1.0 / 16.0
QSCALE = DK ** -0.5
REF_F, LAST_F = CHUNK // 2, CHUNK - 1
REF_B, LAST_B = CHUNK - 1 - CHUNK // 2, 0

ADAM_LR = 0.001
ADAM_B1 = 0.9
ADAM_B2 = 0.999
ADAM_EPS = 1e-08
ADAM_WD = 0.01
ADAM_STEP = 10

VMEM_LIMIT = 56 * 1024 * 1024


def _cparams(*sem):
    return pltpu.CompilerParams(dimension_semantics=sem, vmem_limit_bytes=VMEM_LIMIT)


def _dot(a, b):
    return jnp.dot(a, b, preferred_element_type=F32)


def _dot_nt(a, b):
    return lax.dot_general(a, b, (((1,), (1,)), ((), ())), preferred_element_type=F32)


def _dot_tn(a, b):
    return lax.dot_general(a, b, (((0,), (0,)), ((), ())), preferred_element_type=F32)


def _sigmoid(z):
    return jax.nn.sigmoid(z)


def _position():
    return lax.axis_index("x"), lax.axis_index("y"), lax.axis_index("c")


def _blk(px, py, pc):
    return 4 * px + 2 * py + pc


EDGE = 16
SHIFTED_ROWS = 912
BODY_ROWS = SHIFTED_ROWS - 2 * EDGE


def _first_tile_row(blk, px):
    return EDGE * (56 * blk + px)


def _edge_tiles():
    tiles = {}
    for blk in range(N_DEV):
        first = _first_tile_row(blk, blk // 4)
        tiles.setdefault(first, []).append((blk, 0))
        tiles.setdefault(first + EDGE + BODY_ROWS, []).append((blk, 1))
    return tiles


def _peer_copies(srcs, outs, send_sems, recv_sems):
    x, y, c = _position()
    me = _blk(x, y, c)
    copies = []
    for a, (src, out) in enumerate(zip(srcs, outs)):
        k = 0
        for dx in (0, 1):
            for dy in (0, 1):
                for dc in (0, 1):
                    if dx + dy + dc == 0:
                        continue
                    peer = (1 - x if dx else x, 1 - y if dy else y, 1 - c if dc else c)
                    copies.append(pltpu.make_async_remote_copy(
                        src_ref=src, dst_ref=out.at[me], send_sem=send_sems.at[a * 7 + k],
                        recv_sem=recv_sems.at[a * 7 + k], device_id=peer, device_id_type=MESH))
                    k += 1
    return copies


def _route_chips():
    x, y, c = _position()
    along_x = c == 0
    return [(jnp.where(along_x, 1 - x, x), jnp.where(along_x, y, 1 - y)),
            (jnp.where(along_x, x, 1 - x), jnp.where(along_x, 1 - y, y)), (1 - x, 1 - y)]


WINDOW_ROWS = SHARD_W + 4


def _window_start(k, parity):
    return 2 * SHARD_W * k + (SHARD_W - 4) * parity


def _owner_block(part, k, parity):
    if part.ndim == 3:
        return part.at[2 * k + parity]
    return part.at[pl.ds(pl.multiple_of(_window_start(k, parity), 8), WINDOW_ROWS)]


def _block_shape(part):
    return part.shape[1:] if part.ndim == 3 else (WINDOW_ROWS, part.shape[1])


def _sibling_copies(part, out, send_sems, recv_sems):
    x, y, c = _position()
    return [pltpu.make_async_remote_copy(src_ref=_owner_block(part, k, 1 - c), dst_ref=out.at[k],
                                         send_sem=send_sems.at[k], recv_sem=recv_sems.at[k],
                                         device_id=(x, y, 1 - c), device_id_type=MESH)
            for k in range(4)]


def _start_all(copies):
    for cp in copies:
        cp.start()


def _wait_all(copies):
    for cp in copies:
        cp.wait_recv()
    for cp in copies:
        cp.wait_send()


def _chip_sums(part, from_sibling, core, tc, name, riding=None):
    rows, cols = _block_shape(part)
    nj = cols // tc

    def body(core_ref, p_ref, s_ref, *rest):
        if riding is None:
            (o_ref,) = rest
        else:
            ride_in, o_ref, ride_out, send_sems, recv_sems = rest
            k, j = pl.program_id(0), pl.program_id(1)

            @pl.when(jnp.logical_and(k == 0, j == 0))
            def _():
                _start_all(_sibling_copies(ride_in, ride_out, send_sems, recv_sems))

        o_ref[0] = (p_ref[...].reshape(rows, tc) + s_ref[0]).astype(BF16)

        if riding is not None:
            @pl.when(jnp.logical_and(k == 3, j == nj - 1))
            def _():
                _wait_all(_sibling_copies(ride_in, ride_out, send_sems, recv_sems))

    hbm = pl.BlockSpec(memory_space=pl.ANY)
    sums = jax.ShapeDtypeStruct((4, rows, cols), BF16)
    tile_out = pl.BlockSpec((1, rows, tc), lambda k, j, core_ref: (k, 0, j))
    if part.ndim == 3:
        mine = pl.BlockSpec((1, rows, tc), lambda k, j, core_ref: (2 * k + core_ref[0], 0, j))
    else:
        mine = pl.BlockSpec((pl.Element(rows), pl.Element(tc)),
                            lambda k, j, core_ref: (pl.multiple_of(_window_start(k, core_ref[0]), 8),
                                                    pl.multiple_of(j * tc, 128)))
    in_specs = [mine, pl.BlockSpec((1, rows, tc), lambda k, j, core_ref: (k, 0, j))]
    if riding is None:
        out_shape, out_specs, scratch, args = sums, tile_out, [], (core, part, from_sibling)
    else:
        out_shape = (sums, jax.ShapeDtypeStruct((4,) + _block_shape(riding), F32))
        out_specs, in_specs = (tile_out, hbm), in_specs + [hbm]
        scratch = [pltpu.SemaphoreType.DMA((4,)), pltpu.SemaphoreType.DMA((4,))]
        args = (core, part, from_sibling, riding)
    return pl.pallas_call(
        body, name=name, out_shape=out_shape,
        grid_spec=pltpu.PrefetchScalarGridSpec(num_scalar_prefetch=1, grid=(4, nj), in_specs=in_specs,
                                               out_specs=out_specs, scratch_shapes=scratch),
        compiler_params=_cparams("arbitrary", "arbitrary"),
    )(*args)


def _sum_chips(s_ref, r_ref):
    f = lambda a: a.astype(F32)
    return (f(s_ref[0]) + f(r_ref[0])) + f(r_ref[1])


def _final_sum(sums, from_chips, chip, small, tc, name):
    _, rows, cols = sums.shape
    nj = cols // tc

    def body(chip_ref, s_ref, r_ref, sm_ref, g_out, tot_ref, all_ref, send_sems, recv_sems):
        j = pl.program_id(0)
        me = _blk(*_position())

        @pl.when(j == 0)
        def _():
            all_ref[me] = sm_ref[...]
            _start_all(_peer_copies((all_ref.at[me],), (all_ref,), send_sems, recv_sems))

        g_out[...] = _sum_chips(s_ref, r_ref)

        @pl.when(j == nj - 1)
        def _():
            _wait_all(_peer_copies((all_ref.at[me],), (all_ref,), send_sems, recv_sems))
            acc = all_ref[0]
            for d in range(1, N_DEV):
                acc = acc + all_ref[d]
            tot_ref[...] = acc

    whole = pl.BlockSpec(small.shape, lambda j, chip_ref: (0, 0))
    return pl.pallas_call(
        body, name=name,
        out_shape=(jax.ShapeDtypeStruct((rows, cols), F32), jax.ShapeDtypeStruct(small.shape, F32)),
        grid_spec=pltpu.PrefetchScalarGridSpec(
            num_scalar_prefetch=1, grid=(nj,),
            in_specs=[pl.BlockSpec((1, rows, tc), lambda j, chip_ref: (chip_ref[0], 0, j)),
                      pl.BlockSpec((2, rows, tc), lambda j, chip_ref: (0, 0, j)), whole],
            out_specs=(pl.BlockSpec((rows, tc), lambda j, chip_ref: (0, j)), whole),
            scratch_shapes=[pltpu.VMEM((N_DEV,) + small.shape, F32), pltpu.SemaphoreType.DMA((7,)),
                            pltpu.SemaphoreType.DMA((7,))]),
        compiler_params=_cparams("arbitrary"),
    )(chip, sums, from_chips, small)


def _adamw_rows(g, w, m, v, tr, name):
    rows = g.shape[0]

    def body(g_ref, w_ref, m_ref, v_ref, d_out, m_out, v_out):
        delta, m_new, v_new = _adamw(w_ref[...], g_ref[...], m_ref[...], v_ref[...])
        d_out[...] = delta
        m_out[...] = m_new
        v_out[...] = v_new

    tile = pl.BlockSpec((tr,) + g.shape[1:], lambda r: (r, 0, 0))
    shp = jax.ShapeDtypeStruct(g.shape, F32)
    return pl.pallas_call(
        body, name=name, out_shape=(shp, shp, shp), grid=(rows // tr,),
        in_specs=[tile] * 4, out_specs=(tile, tile, tile),
        compiler_params=_cparams("arbitrary"),
    )(g, w, m, v)


def _adamw(w, g, m, v):
    m = ADAM_B1 * m + (1.0 - ADAM_B1) * g
    v = ADAM_B2 * v + (1.0 - ADAM_B2) * (g * g)
    m_hat = m / (1.0 - ADAM_B1 ** ADAM_STEP)
    v_hat = v / (1.0 - ADAM_B2 ** ADAM_STEP)
    delta = -ADAM_LR * (m_hat / (jnp.sqrt(v_hat) + ADAM_EPS) + ADAM_WD * w)
    return delta, m, v


def _final_sum_adamw(sums, from_chips, chip, w, m, v, tr, name):
    rows, cols = w.shape

    def body(chip_ref, s_ref, r_ref, w_ref, m_ref, v_ref, g_out, d_out, m_out, v_out):
        g = _sum_chips(s_ref, r_ref)
        delta, m_new, v_new = _adamw(w_ref[...], g, m_ref[...], v_ref[...])
        g_out[...] = g
        d_out[...] = delta
        m_out[...] = m_new
        v_out[...] = v_new

    tile = pl.BlockSpec((tr, cols), lambda r, chip_ref: (r, 0))
    shp = jax.ShapeDtypeStruct((rows, cols), F32)
    return pl.pallas_call(
        body, name=name,
        out_shape=(shp, shp, shp, shp),
        grid_spec=pltpu.PrefetchScalarGridSpec(
            num_scalar_prefetch=1, grid=(rows // tr,),
            in_specs=[pl.BlockSpec((1, tr, cols), lambda r, chip_ref: (chip_ref[0], r, 0)),
                      pl.BlockSpec((2, tr, cols), lambda r, chip_ref: (0, r, 0)),
                      tile, tile, tile],
            out_specs=(tile, tile, tile, tile)),
        compiler_params=_cparams("arbitrary"),
    )(chip, sums, from_chips, w, m, v)


def _adamw_small(g, w, m, v):
    def body(g_ref, w_ref, m_ref, v_ref, d_out, m_out, v_out):
        delta, m_new, v_new = _adamw(w_ref[...], g_ref[...], m_ref[...], v_ref[...])
        d_out[...] = delta
        m_out[...] = m_new
        v_out[...] = v_new

    vmem = pl.BlockSpec(memory_space=pltpu.VMEM)
    shp = jax.ShapeDtypeStruct(g.shape, F32)
    return pl.pallas_call(body, name="adamw_small", out_shape=(shp, shp, shp),
                          in_specs=[vmem] * 4, out_specs=(vmem, vmem, vmem))(g, w, m, v)


TILE_ROWS = (0, 1024, NAT_ZA, NAT_B, NAT_ZC, NAT_C, NAT_C + CONV_W)


TILE_ORDER = ((0, 1, 2, 3, 5, 6, 4), (2, 0, 1, 4, 3, 5, 6), (5, 0, 6, 4, 1, 2, 3), (4, 2, 3, 5, 6, 0, 1))
NEIGHBOUR_SWEEP, DIAGONAL_SWEEP = 1, 4
PIECES, W_IN_PIECES, OTHER_PIECES = 4, (0, 1), (2, 3)


def _gather_inproj(x2d, norm_g, shard_t, w_out_s, small_s, order, tm):
    seq = x2d.shape[0]
    tn = CONV_W
    ni, nj = seq // tm, MAIN_W // tn
    first_sweep = lambda j, i, order_ref: jnp.where(j == 0, i, ni - 1)
    last_sweep = lambda j, i, order_ref: jnp.where(j == nj - 1, i, 0)
    edge_tiles = _edge_tiles()

    def body(order_ref, x_ref, g_ref, shard_ref, wout_ref, sm_ref, proj_ref, lr_ref, ht_ref, w_nat, wout_all, sm_all,
             w_all, h_all, edges, stage, wout_b, sm_b, send_sems, recv_sems, local_sems):
        j, i = pl.program_id(0), pl.program_id(1)
        rows = pl.ds(pl.multiple_of(i * tm, tm), tm)
        x, y, c = _position()
        me, here, sibling = _blk(x, y, c), (x, y, c), (x, y, 1 - c)
        chips = _route_chips()
        sibling_chips = [chips[1], chips[0], chips[2]]

        def pieces(px, py, pc):
            blk = _blk(px, py, pc)
            body_rows = pl.ds(pl.multiple_of(_first_tile_row(blk, px) + EDGE, EDGE), BODY_ROWS)
            return [w_all.at[body_rows], edges.at[blk], wout_all.at[blk], sm_all.at[blk]]

        def copy(a, k, block, to, staged=None):
            ref = pieces(*block)[a]
            return pltpu.make_async_remote_copy(src_ref=ref if staged is None else staged, dst_ref=ref,
                                                send_sem=send_sems.at[a * 7 + k], recv_sem=recv_sems.at[a * 7 + k],
                                                device_id=to, device_id_type=MESH)

        def own_copies(group):
            targets = [(0, sibling)] + [(1 + n, (*chips[n], c)) for n in range(2)]
            staged = [None, None, wout_b, sm_b]
            return [copy(a, k, here, to, staged[a]) for k, to in targets for a in group]

        def relays(group):
            return [copy(a, 3, (*chips[0], c), (*chips[1], c)) for a in group]

        def forwards(n, group):
            return [copy(a, 4 + n, (*chips[n], c), sibling) for a in group]

        def keep_own():
            return [pltpu.make_async_copy(wout_b, wout_all.at[me], local_sems.at[0]),
                    pltpu.make_async_copy(sm_b, sm_all.at[me], local_sems.at[1])]

        def keep_weight():
            return pltpu.make_async_copy(w_all, w_nat, local_sems.at[2])

        def arrive(ns, group):
            for n in ns:
                for a in group:
                    copy(a, 1 + n, (*chips[n], c), here).wait_recv()
                _start_all((relays(group) if n == 0 else []) + forwards(n, group))
            for n in ns:
                for a in group:
                    copy(a, 4 + n, (*sibling_chips[n], 1 - c), here).wait_recv()

        def add_edge_tiles(stage):
            for row, parts in edge_tiles.items():
                ready = 0
                for blk, _ in parts:
                    away = (x != blk // 4).astype(jnp.int32) + (y != (blk // 2) % 2).astype(jnp.int32)
                    ready = jnp.maximum(ready, away)

                @pl.when(ready == stage)
                def _(row=row, parts=parts):
                    tile = edges[parts[0][0], parts[0][1]].astype(F32)
                    for blk, side in parts[1:]:
                        tile = tile + edges[blk, side].astype(F32)
                    w_all[row:row + EDGE, :] = tile.astype(BF16)

        @pl.when(jnp.logical_and(j == 0, i == 0))
        def _():
            last = SHARD_W // 8 * 8
            for col in range(0, D_MODEL, 128):
                cols = slice(col, col + 128)
                stage[0:last, :] = shard_ref[0:last, cols]
                stage[last:, :] = jnp.zeros((SHIFTED_ROWS - last, 128), F32)
                stage[last:SHARD_W, :] = shard_ref[last:SHARD_W, cols]
                for k in range(EDGE // 4):
                    @pl.when(me % 4 == k)
                    def _(k=k, cols=cols):
                        moved = pltpu.roll(stage[...], 4 * k, 0) if k else stage[...]
                        pieces(*here)[0][:, cols] = moved[EDGE:EDGE + BODY_ROWS].astype(BF16)
                        edges[me, 0, :, cols] = moved[0:EDGE].astype(BF16)
                        edges[me, 1, :, cols] = moved[EDGE + BODY_ROWS:].astype(BF16)
            _start_all(own_copies(W_IN_PIECES))
            wout_b[...] = wout_ref[...].astype(BF16)
            sm_b[...] = sm_ref[...]
            _start_all(own_copies(OTHER_PIECES) + keep_own())
            for a in W_IN_PIECES:
                copy(a, 0, sibling, here).wait_recv()
            add_edge_tiles(0)

        @pl.when(jnp.logical_and(j == NEIGHBOUR_SWEEP, i == 0))
        def _():
            arrive((0, 1), W_IN_PIECES)
            add_edge_tiles(1)

        @pl.when(jnp.logical_and(j == DIAGONAL_SWEEP, i == 0))
        def _():
            arrive((2,), W_IN_PIECES)
            add_edge_tiles(2)
            keep_weight().start()
            arrive((0, 1), OTHER_PIECES)

        @pl.when(jnp.logical_and(j == nj - 1, i == 0))
        def _():
            arrive((2,), OTHER_PIECES)

        @pl.when(j == 0)
        def _():
            xv = x_ref[...]
            r = lax.rsqrt(jnp.mean(xv * xv, axis=-1, keepdims=True) + EPS)
            h = (xv * r) * g_ref[...]
            h_all[rows, :] = h.astype(BF16)
            ht_ref[...] = h.T.astype(BF16)

        tile = order_ref[j]
        row = 0
        for k, start in enumerate(TILE_ROWS):
            row = row + jnp.where(tile == k, start // 32, 0)
        w_tile = w_all[pl.ds(pl.multiple_of(row * 32, 32), tn), :]
        proj_ref[...] = _dot_nt(h_all[rows, :], w_tile).astype(BF16)

        @pl.when(j == nj - 1)
        def _():
            lr_ref[...] = _dot_nt(h_all[rows, :], w_all[NAT_LR:NAT_LR + LR_W, :])

        @pl.when(jnp.logical_and(j == nj - 1, i == ni - 1))
        def _():
            everything = range(PIECES)
            passed_on = [cp for n in range(3) for cp in forwards(n, everything)]
            for cp in own_copies(everything) + relays(everything) + passed_on:
                cp.wait_send()
            for a in OTHER_PIECES:
                copy(a, 0, sibling, here).wait_recv()
            for cp in keep_own() + [keep_weight()]:
                cp.wait()

    const = lambda shape: pl.BlockSpec(shape, lambda j, i, order_ref: (0,) * len(shape))
    hbm = pl.BlockSpec(memory_space=pl.ANY)
    vmem = pl.BlockSpec(memory_space=pltpu.VMEM)
    return pl.pallas_call(
        body, name="gather_inproj",
        out_shape=(jax.ShapeDtypeStruct((seq, MAIN_W), BF16), jax.ShapeDtypeStruct((seq, LR_W), F32),
                   jax.ShapeDtypeStruct((D_MODEL, seq), BF16), jax.ShapeDtypeStruct((IN_W, D_MODEL), BF16),
                   jax.ShapeDtypeStruct((N_DEV,) + w_out_s.shape, BF16),
                   jax.ShapeDtypeStruct((N_DEV,) + small_s.shape, F32)),
        grid_spec=pltpu.PrefetchScalarGridSpec(
            num_scalar_prefetch=1, grid=(nj, ni),
            in_specs=[pl.BlockSpec((tm, D_MODEL), lambda j, i, order_ref: (first_sweep(j, i, order_ref), 0)),
                      const((1, D_MODEL)), vmem, vmem, const(small_s.shape)],
            out_specs=(pl.BlockSpec((tm, tn), lambda j, i, order_ref: (i, order_ref[j])),
                       pl.BlockSpec((tm, LR_W), lambda j, i, order_ref: (last_sweep(j, i, order_ref), 0)),
                       pl.BlockSpec((D_MODEL, tm), lambda j, i, order_ref: (0, first_sweep(j, i, order_ref))),
                       hbm, hbm, hbm),
            scratch_shapes=[pltpu.VMEM((IN_W, D_MODEL), BF16), pltpu.VMEM((seq, D_MODEL), BF16),
                            pltpu.VMEM((N_DEV, 2, EDGE, D_MODEL), BF16), pltpu.VMEM((SHIFTED_ROWS, 128), F32),
                            pltpu.VMEM(w_out_s.shape, BF16), pltpu.VMEM(small_s.shape, F32),
                            pltpu.SemaphoreType.DMA((7 * PIECES,)), pltpu.SemaphoreType.DMA((7 * PIECES,)),
                            pltpu.SemaphoreType.DMA((3,))]),
        compiler_params=_cparams("arbitrary", "arbitrary"),
    )(order, x2d, norm_g, shard_t, w_out_s, small_s)


def _block_masks(tt):
    row = lax.broadcasted_iota(jnp.int32, (tt, tt), 0)
    col = lax.broadcasted_iota(jnp.int32, (tt, tt), 1)
    same = jnp.right_shift(row, 6) == jnp.right_shift(col, 6)
    return (jnp.logical_and(same, col <= row), jnp.logical_and(same, col >= row), jnp.logical_and(same, col > row))


def _dot_split3(ones_mat, x):
    x1 = x.astype(BF16)
    r1 = x - x1.astype(F32)
    x2 = r1.astype(BF16)
    x3 = (r1 - x2.astype(F32)).astype(BF16)
    return (_dot(ones_mat, x3) + _dot(ones_mat, x2)) + _dot(ones_mat, x1)


def _log_gate(logits):
    return (jnp.minimum(logits, 0.0) - jnp.log(1.0 + jnp.exp(-jnp.abs(logits)))) * GATE_SCALE


def _chunk_column_mask(tt):
    nc = tt // CHUNK
    row = lax.broadcasted_iota(jnp.int32, (tt, nc * DK), 0)
    col = lax.broadcasted_iota(jnp.int32, (tt, nc * DK), 1)
    return jnp.right_shift(row, 6) == jnp.right_shift(col, 7)


def _chunked(mask, x, nc):
    wide = jnp.concatenate([x] * nc, axis=1)
    return jnp.where(mask, wide, jnp.zeros_like(wide))


def _gla_fwd(proj, lr, wgk_f, wgk_b, bgk_f, bgk_b, tt):
    seq = proj.shape[0]
    nb, nc, nch = seq // tt, tt // CHUNK, seq // CHUNK

    def body(qf, kf, vf, lrf, qb, kb, vb, lrb, wf, wb, bf, bb, of, ob, stf, stb, s_scr, qs_s, ks_s, qin_s, kout_s):
        @pl.when(pl.program_id(0) == 0)
        def _():
            s_scr[...] = jnp.zeros(s_scr.shape, F32)

        low, upp, sup = _block_masks(tt)
        dirs = ((qf, kf, vf, lrf, wf, bf, of, stf, low, low, REF_F, LAST_F, list(range(nc))),
                (qb, kb, vb, lrb, wb, bb, ob, stb, upp, sup, REF_B, LAST_B, list(reversed(range(nc)))))
        for d, (q_r, k_r, v_r, lr_r, w_r, b_r, o_r, st_r, cum, mask, ref, last, order) in enumerate(dirs):
            logits = _dot(lr_r[...].astype(BF16), w_r[...]) + b_r[...]
            b = _dot_split3(cum.astype(BF16), _log_gate(logits))
            decs = []
            for c in range(nc):
                rows = slice(c * CHUNK, (c + 1) * CHUNK)
                bc = b[rows]
                b_ref, b_last = bc[ref:ref + 1], bc[last:last + 1]
                qc = q_r[rows, :].astype(F32) * QSCALE
                kc = k_r[rows, :].astype(F32)
                qs_s[rows, :] = (qc * jnp.exp(bc - b_ref)).astype(BF16)
                ks_s[rows, :] = (kc * jnp.exp(b_ref - bc)).astype(BF16)
                qin_s[rows, :] = (qc * jnp.exp(bc)).astype(BF16)
                kout_s[rows, :] = (kc * jnp.exp(b_last - bc)).astype(BF16)
                decs.append(jnp.exp(b_last))
            for h in range(HEADS):
                ksl = slice(h * DK, (h + 1) * DK)
                vsl = slice(h * DV, (h + 1) * DV)
                v = v_r[:, vsl].astype(BF16)
                att = jnp.where(mask, _dot_nt(qs_s[:, ksl], ks_s[:, ksl]), 0.0).astype(BF16)
                o_intra = _dot(att, v)
                st = s_scr[d * HEADS + h]
                for c in order:
                    rows = slice(c * CHUNK, (c + 1) * CHUNK)
                    stb = st.astype(BF16)
                    st_r[c, h] = stb
                    o_r[rows, vsl] = (o_intra[rows] + _dot_nt(qin_s[rows, ksl], stb)).astype(BF16)
                    st = st * decs[c][:, ksl] + _dot_tn(v[rows], kout_s[rows, ksl])
                s_scr[d * HEADS + h] = st

    fw = lambda i: (i, 0)
    bw = lambda i: (nb - 1 - i, 0)
    const = lambda i: (0, 0)

    def tok_specs(m):
        return [pl.BlockSpec((tt, QK_W), lambda i: (m(i)[0], OFF_Q // QK_W)),
                pl.BlockSpec((tt, QK_W), lambda i: (m(i)[0], OFF_K // QK_W)),
                pl.BlockSpec((tt, V_W), lambda i: (m(i)[0], OFF_V // V_W)),
                pl.BlockSpec((tt, LR_W), m)]

    st_shape = jax.ShapeDtypeStruct((nch, HEADS, DV, DK), BF16)
    o_shape = jax.ShapeDtypeStruct((seq, V_W), BF16)
    operand = pltpu.VMEM((tt, QK_W), BF16)
    return pl.pallas_call(
        body, name="gla_fwd",
        out_shape=(o_shape, o_shape, st_shape, st_shape),
        grid=(nb,),
        in_specs=tok_specs(fw) + tok_specs(bw) + [
            pl.BlockSpec((LR_W, QK_W), const), pl.BlockSpec((LR_W, QK_W), const),
            pl.BlockSpec((1, QK_W), const), pl.BlockSpec((1, QK_W), const)],
        out_specs=(pl.BlockSpec((tt, V_W), fw), pl.BlockSpec((tt, V_W), bw),
                   pl.BlockSpec((nc, HEADS, DV, DK), lambda i: (i, 0, 0, 0)),
                   pl.BlockSpec((nc, HEADS, DV, DK), lambda i: (nb - 1 - i, 0, 0, 0))),
        scratch_shapes=[pltpu.VMEM((2 * HEADS, DV, DK), F32), operand, operand, operand, operand],
        compiler_params=_cparams("arbitrary"),
    )(proj, proj, proj, lr, proj, proj, proj, lr, wgk_f, wgk_b, bgk_f, bgk_b)


def _head_norm(o, gain):
    outs, rinv = [], []
    for h in range(HEADS):
        oh = o[:, h * DV:(h + 1) * DV]
        r = lax.rsqrt(jnp.mean(oh * oh, axis=-1, keepdims=True) + EPS)
        outs.append((oh * r) * gain)
        rinv.append(r)
    return jnp.concatenate(outs, axis=1), rinv


def _shift_rows(u, prev_row, next_row):
    n = u.shape[0]
    row = lax.broadcasted_iota(jnp.int32, (n, 1), 0)
    up = jnp.where(row == 0, prev_row, pltpu.roll(u, 1, 0))
    un = jnp.where(row == n - 1, next_row, pltpu.roll(u, n - 1, 0))
    return up, un


HALO = 16


def _halo_specs(tm, seq, col_block):
    per = tm // HALO
    last = seq // HALO - 1
    return [pl.BlockSpec((HALO, CONV_W), lambda i: (jnp.maximum(i * per - 1, 0), col_block)),
            pl.BlockSpec((HALO, CONV_W), lambda i: (jnp.minimum((i + 1) * per, last), col_block))]


def _f32(ref):
    return ref[...].astype(F32)


def _last_row(ref):
    return ref[HALO - 1:HALO, :].astype(F32)


def _first_row(ref):
    return ref[0:1, :].astype(F32)


def _mix_out_loss(o_f, o_b, proj, x2d, tgt, gla_g, conv_w, conv_b, w_out, final_g, tm):
    seq = x2d.shape[0]
    nt = seq // tm

    def body(of, ob, za, bg, cg, hc, zc, cprev, cnext, hprev, hnext, x_ref, t_ref, gg, cw, cb, wo, fg,
             yt_ref, conv_ref, dx2_ref, dx2b_ref, loss_ref, dfg_ref):
        i = pl.program_id(0)

        @pl.when(i == 0)
        def _():
            loss_ref[...] = jnp.zeros(loss_ref.shape, F32)
            dfg_ref[...] = jnp.zeros(dfg_ref.shape, F32)

        on, _ = _head_norm(_f32(of) + _f32(ob), gg[...])
        zav = _f32(za)
        y_a = on * (zav * _sigmoid(zav))
        u = _f32(cg) * _f32(hc)
        prev_row = jnp.where(i > 0, _last_row(cprev) * _last_row(hprev), 0.0)
        next_row = jnp.where(i < nt - 1, _first_row(cnext) * _first_row(hnext), 0.0)
        up, un = _shift_rows(u, prev_row, next_row)
        conv = (cw[0:1, :] * up + cw[1:2, :] * u + cw[2:3, :] * un) + cb[...]
        conv_ref[...] = conv.astype(BF16)
        zcv = _f32(zc)
        y_c = _f32(bg) * conv * (zcv * _sigmoid(zcv))
        y = jnp.concatenate([y_a, y_c], axis=1)
        yt_ref[...] = y.T.astype(BF16)
        x2 = x_ref[...] + _dot(y.astype(BF16), wo[...])
        r = lax.rsqrt(jnp.mean(x2 * x2, axis=-1, keepdims=True) + EPS)
        xn = x2 * r
        err = xn * fg[...] - t_ref[...]
        loss_ref[...] += 0.5 * jnp.sum(jnp.mean(err * err, axis=-1, keepdims=True))
        dyf = err * (1.0 / D_MODEL)
        dfg_ref[...] += jnp.sum(dyf * xn, axis=0, keepdims=True)
        dxn = dyf * fg[...]
        dx2 = r * dxn - xn * (r * jnp.mean(dxn * xn, axis=-1, keepdims=True))
        dx2_ref[...] = dx2
        dx2b_ref[...] = dx2.astype(BF16)

    def col(off):
        return pl.BlockSpec((tm, CONV_W), lambda i: (i, off // CONV_W))

    rowt = pl.BlockSpec((tm, D_MODEL), lambda i: (i, 0))
    const = lambda shape: pl.BlockSpec(shape, lambda i: (0, 0))
    return pl.pallas_call(
        body, name="mix_out_loss",
        out_shape=(jax.ShapeDtypeStruct((MIX_W, seq), BF16), jax.ShapeDtypeStruct((seq, CONV_W), BF16),
                   jax.ShapeDtypeStruct((seq, D_MODEL), F32), jax.ShapeDtypeStruct((seq, D_MODEL), BF16),
                   jax.ShapeDtypeStruct((8, 128), F32), jax.ShapeDtypeStruct((1, D_MODEL), F32)),
        grid=(nt,),
        in_specs=[rowt, rowt, col(OFF_ZA), col(OFF_B), col(OFF_C), col(OFF_H), col(OFF_ZC)]
        + _halo_specs(tm, seq, OFF_C // CONV_W) + _halo_specs(tm, seq, OFF_H // CONV_W)
        + [rowt, rowt, const((1, DV)), const((8, CONV_W)), const((1, CONV_W)), const((MIX_W, D_MODEL)),
           const((1, D_MODEL))],
        out_specs=(pl.BlockSpec((MIX_W, tm), lambda i: (0, i)), rowt, rowt, rowt, const((8, 128)),
                   const((1, D_MODEL))),
        compiler_params=_cparams("arbitrary"),
    )(o_f, o_b, proj, proj, proj, proj, proj, proj, proj, proj, proj, x2d, tgt, gla_g, conv_w, conv_b, w_out, final_g)


def _dsilu(z, s):
    return s * (1.0 + z * (1.0 - s))


def _mix_bwd(dx2b, o_f, o_b, proj, conv, gla_g, w_out, tm):
    seq = dx2b.shape[0]

    def body(dx, of, ob, za, bg, zc, cv, gg, wo, dg_ref, do_ref, dconv_ref, dgg_ref, dcb_ref):
        @pl.when(pl.program_id(0) == 0)
        def _():
            dgg_ref[...] = jnp.zeros(dgg_ref.shape, F32)
            dcb_ref[...] = jnp.zeros(dcb_ref.shape, F32)

        dy = _dot_nt(dx[...], wo[...])
        dy_a, dy_c = dy[:, :V_W], dy[:, V_W:]
        zcv, bgv, convv = _f32(zc), _f32(bg), _f32(cv)
        sc = _sigmoid(zcv)
        szc = zcv * sc
        dg_ref[:, CONV_W:2 * CONV_W] = (dy_c * convv * szc).astype(BF16)
        dconv = dy_c * bgv * szc
        dconv_ref[...] = dconv.astype(BF16)
        dcb_ref[...] += jnp.sum(dconv, axis=0, keepdims=True)
        dg_ref[:, 2 * CONV_W:] = (dy_c * bgv * convv * _dsilu(zcv, sc)).astype(BF16)

        o = _f32(of) + _f32(ob)
        gain = gg[...]
        on, rinv = _head_norm(o, gain)
        zav = _f32(za)
        sa = _sigmoid(zav)
        dg_ref[:, :CONV_W] = (dy_a * on * _dsilu(zav, sa)).astype(BF16)
        don = dy_a * (zav * sa)
        dgg = jnp.zeros((1, DV), F32)
        dos = []
        for h in range(HEADS):
            sl = slice(h * DV, (h + 1) * DV)
            oh, r, dh = o[:, sl], rinv[h], don[:, sl]
            ohn = oh * r
            dgg = dgg + jnp.sum(dh * ohn, axis=0, keepdims=True)
            dn = dh * gain
            dos.append(r * dn - ohn * (r * jnp.mean(dn * ohn, axis=-1, keepdims=True)))
        dgg_ref[...] += dgg
        do_ref[...] = jnp.concatenate(dos, axis=1).astype(BF16)

    def col(off):
        return pl.BlockSpec((tm, CONV_W), lambda i: (i, off // CONV_W))

    rowt = pl.BlockSpec((tm, D_MODEL), lambda i: (i, 0))
    const = lambda shape: pl.BlockSpec(shape, lambda i: (0, 0))
    return pl.pallas_call(
        body, name="mix_bwd",
        out_shape=(jax.ShapeDtypeStruct((seq, GATES_W), BF16), jax.ShapeDtypeStruct((seq, V_W), BF16),
                   jax.ShapeDtypeStruct((seq, CONV_W), BF16),
                   jax.ShapeDtypeStruct((1, DV), F32), jax.ShapeDtypeStruct((1, CONV_W), F32)),
        grid=(seq // tm,),
        in_specs=[rowt, rowt, rowt, col(OFF_ZA), col(OFF_B), col(OFF_ZC), rowt, const((1, DV)),
                  const((MIX_W, D_MODEL))],
        out_specs=(pl.BlockSpec((tm, GATES_W), lambda i: (i, 0)), rowt, rowt, const((1, DV)), const((1, CONV_W))),
        compiler_params=_cparams("arbitrary"),
    )(dx2b, o_f, o_b, proj, proj, proj, conv, gla_g, w_out)


def _conv_bwd(dconv, proj, conv_w, tm):
    seq = dconv.shape[0]
    nt = seq // tm

    def body(dc_in, dprev, dnext, cg, hc, cprev, cnext, hprev, hnext, cw, dch_ref, dcw_ref):
        i = pl.program_id(0)

        @pl.when(i == 0)
        def _():
            dcw_ref[...] = jnp.zeros(dcw_ref.shape, F32)

        first, lastt = i > 0, i < nt - 1
        dcv = _f32(dc_in)
        d_up, d_un = _shift_rows(dcv, jnp.where(first, _last_row(dprev), 0.0), jnp.where(lastt, _first_row(dnext), 0.0))
        cgv, hcv = _f32(cg), _f32(hc)
        u = cgv * hcv
        u_up, u_un = _shift_rows(u, jnp.where(first, _last_row(cprev) * _last_row(hprev), 0.0),
                                 jnp.where(lastt, _first_row(cnext) * _first_row(hnext), 0.0))
        du = cw[0:1, :] * d_un + cw[1:2, :] * dcv + cw[2:3, :] * d_up
        dch_ref[:, :CONV_W] = (du * hcv).astype(BF16)
        dch_ref[:, CONV_W:] = (du * cgv).astype(BF16)
        dcw_ref[0:1, :] += jnp.sum(dcv * u_up, axis=0, keepdims=True)
        dcw_ref[1:2, :] += jnp.sum(dcv * u, axis=0, keepdims=True)
        dcw_ref[2:3, :] += jnp.sum(dcv * u_un, axis=0, keepdims=True)

    def col(off):
        return pl.BlockSpec((tm, CONV_W), lambda i: (i, off // CONV_W))

    rowt = pl.BlockSpec((tm, CONV_W), lambda i: (i, 0))
    const = lambda shape: pl.BlockSpec(shape, lambda i: (0, 0))
    return pl.pallas_call(
        body, name="conv_bwd",
        out_shape=(jax.ShapeDtypeStruct((seq, CH_W), BF16), jax.ShapeDtypeStruct((8, CONV_W), F32)),
        grid=(nt,),
        in_specs=[rowt] + _halo_specs(tm, seq, 0) + [col(OFF_C), col(OFF_H)]
        + _halo_specs(tm, seq, OFF_C // CONV_W) + _halo_specs(tm, seq, OFF_H // CONV_W) + [const((8, CONV_W))],
        out_specs=(pl.BlockSpec((tm, CH_W), lambda i: (i, 0)), const((8, CONV_W))),
        compiler_params=_cparams("arbitrary"),
    )(dconv, dconv, dconv, proj, proj, proj, proj, proj, proj, conv_w)


def _gla_bwd(proj, lr, do, st_f, st_b, wgk_f, wgk_b, bgk_f, bgk_b, tt):
    seq = proj.shape[0]
    nb, nc = seq // tt, tt // CHUNK

    def body(qf, kf, vf, lrf, dof, stf, qb, kb, vb, lrb, dob, stb, wf, wb, bf, bb,
             dqkv_f, dlr_f, dqkv_b, dlr_b, dwf, dwb, dbf, dbb,
             ds_scr, eq_s, ek_s, ein_s, eout_s, qs_s, ks_s, qin_s, kout_s, db_s, lg_s):
        @pl.when(pl.program_id(0) == 0)
        def _():
            ds_scr[...] = jnp.zeros(ds_scr.shape, F32)
            for r in (dwf, dwb, dbf, dbb):
                r[...] = jnp.zeros(r.shape, F32)

        low, upp, sup = _block_masks(tt)
        row = lax.broadcasted_iota(jnp.int32, (CHUNK, 1), 0)
        kmask = _chunk_column_mask(tt)
        dirs = ((qf, kf, vf, lrf, dof, stf, wf, bf, dqkv_f, dlr_f, dwf, dbf,
                 low, upp, low, REF_F, LAST_F, list(reversed(range(nc)))),
                (qb, kb, vb, lrb, dob, stb, wb, bb, dqkv_b, dlr_b, dwb, dbb,
                 upp, low, sup, REF_B, LAST_B, list(range(nc))))
        for d, (q_r, k_r, v_r, lr_r, do_r, st_r, w_r, b_r, dqkv_r, dlr_r, dw_r, db_r,
                cum, cum_t, mask, ref, last, order) in enumerate(dirs):
            lrv = lr_r[...].astype(BF16)
            wv = w_r[...]
            logits = _dot(lrv, wv) + b_r[...]
            lg_s[...] = logits
            b = _dot_split3(cum.astype(BF16), _log_gate(logits))
            decs = []
            for c in range(nc):
                rows = slice(c * CHUNK, (c + 1) * CHUNK)
                bc = b[rows]
                b_ref, b_last = bc[ref:ref + 1], bc[last:last + 1]
                qc = q_r[rows, :].astype(F32) * QSCALE
                kc = k_r[rows, :].astype(F32)
                e_q, e_k, e_in, e_out = jnp.exp(bc - b_ref), jnp.exp(b_ref - bc), jnp.exp(bc), jnp.exp(b_last - bc)
                eq_s[rows, :], ek_s[rows, :], ein_s[rows, :], eout_s[rows, :] = e_q, e_k, e_in, e_out
                qs_s[rows, :] = (qc * e_q).astype(BF16)
                ks_s[rows, :] = (kc * e_k).astype(BF16)
                qin_s[rows, :] = (qc * e_in).astype(BF16)
                kout_s[rows, :] = (kc * e_out).astype(BF16)
                decs.append(jnp.exp(b_last))
            for h in range(HEADS):
                ksl = slice(h * DK, (h + 1) * DK)
                vsl = slice(h * DV, (h + 1) * DV)
                v = v_r[:, vsl].astype(BF16)
                dov = do_r[:, vsl].astype(BF16)
                qsb, ksb = qs_s[:, ksl], ks_s[:, ksl]
                att = jnp.where(mask, _dot_nt(qsb, ksb), 0.0).astype(BF16)
                datt = jnp.where(mask, _dot_nt(dov, v), 0.0).astype(BF16)
                dqs = _dot(datt, ksb)
                dks = _dot_tn(datt, qsb)
                dv_intra = _dot_tn(att, dov)
                g_t = _dot_tn(dov, _chunked(kmask, qin_s[:, ksl], nc))
                ds = ds_scr[d * HEADS + h]
                for c in order:
                    rows = slice(c * CHUNK, (c + 1) * CHUNK)
                    dsb = ds.astype(BF16)
                    s_prev = st_r[c, h]
                    dk_out = _dot(v[rows], dsb)
                    dq_in = _dot(dov[rows], s_prev)
                    dv = dv_intra[rows] + _dot_nt(kout_s[rows, ksl], dsb)
                    dqkv_r[rows, OFF_V + h * DV:OFF_V + (h + 1) * DV] = dv.astype(BF16)
                    dec = decs[c][:, ksl]
                    ddec = jnp.sum(ds * s_prev.astype(F32), axis=0, keepdims=True)
                    e_out = eout_s[rows, ksl]
                    qc = q_r[rows, ksl].astype(F32) * QSCALE
                    kc = k_r[rows, ksl].astype(F32)
                    dq = dqs[rows] * eq_s[rows, ksl] + dq_in * ein_s[rows, ksl]
                    dk = dks[rows] * ek_s[rows, ksl] + dk_out * e_out
                    dqkv_r[rows, OFF_Q + h * DK:OFF_Q + (h + 1) * DK] = (dq * QSCALE).astype(BF16)
                    dqkv_r[rows, OFF_K + h * DK:OFF_K + (h + 1) * DK] = dk.astype(BF16)
                    tail = jnp.sum(dk_out * (kc * e_out), axis=0, keepdims=True) + ddec * dec
                    db_s[rows, ksl] = (qc * dq - kc * dk) + jnp.where(row == last, tail, 0.0)
                    ds = ds * dec + g_t[:, c * DK:(c + 1) * DK]
                ds_scr[d * HEADS + h] = ds
            dg = _dot_split3(cum_t.astype(BF16), db_s[...])
            dlogit = (dg * GATE_SCALE) * _sigmoid(-lg_s[...])
            dlb = dlogit.astype(BF16)
            dlr_r[...] = _dot_nt(dlb, wv)
            dw_r[...] += _dot_tn(lrv, dlb)
            db_r[...] += jnp.sum(dlogit, axis=0, keepdims=True)

    fw = lambda i: (nb - 1 - i, 0)
    bw = lambda i: (i, 0)
    const = lambda i: (0, 0)

    def tok_specs(m):
        return [pl.BlockSpec((tt, QK_W), lambda i: (m(i)[0], OFF_Q // QK_W)),
                pl.BlockSpec((tt, QK_W), lambda i: (m(i)[0], OFF_K // QK_W)),
                pl.BlockSpec((tt, V_W), lambda i: (m(i)[0], OFF_V // V_W)),
                pl.BlockSpec((tt, LR_W), m),
                pl.BlockSpec((tt, V_W), m),
                pl.BlockSpec((nc, HEADS, DV, DK), lambda i: (m(i)[0], 0, 0, 0))]

    dqkv = jax.ShapeDtypeStruct((seq, QK_W + QK_W + V_W), BF16)
    dlr = jax.ShapeDtypeStruct((seq, LR_W), F32)
    dw = jax.ShapeDtypeStruct((LR_W, QK_W), F32)
    dbias = jax.ShapeDtypeStruct((1, QK_W), F32)
    return pl.pallas_call(
        body, name="gla_bwd",
        out_shape=(dqkv, dlr, dqkv, dlr, dw, dw, dbias, dbias),
        grid=(nb,),
        in_specs=tok_specs(fw) + tok_specs(bw) + [
            pl.BlockSpec((LR_W, QK_W), const), pl.BlockSpec((LR_W, QK_W), const),
            pl.BlockSpec((1, QK_W), const), pl.BlockSpec((1, QK_W), const)],
        out_specs=(pl.BlockSpec((tt, QK_W + QK_W + V_W), fw), pl.BlockSpec((tt, LR_W), fw),
                   pl.BlockSpec((tt, QK_W + QK_W + V_W), bw), pl.BlockSpec((tt, LR_W), bw),
                   pl.BlockSpec((LR_W, QK_W), const), pl.BlockSpec((LR_W, QK_W), const),
                   pl.BlockSpec((1, QK_W), const), pl.BlockSpec((1, QK_W), const)),
        scratch_shapes=[pltpu.VMEM((2 * HEADS, DV, DK), F32)] + [pltpu.VMEM((tt, QK_W), F32)] * 4
        + [pltpu.VMEM((tt, QK_W), BF16)] * 4 + [pltpu.VMEM((tt, QK_W), F32)] * 2,
        compiler_params=_cparams("arbitrary"),
    )(proj, proj, proj, lr, do, st_f, proj, proj, proj, lr, do, st_b, wgk_f, wgk_b, bgk_f, bgk_b)


def _both_directions(f_ref, b_ref):
    return (_f32(f_ref) + _f32(b_ref)).astype(BF16)


def _input_grad(dqkv_f, dqkv_b, dp_gates, dp_ch, dlr_f, dlr_b, w_nat, x2d, norm_g, dx2, sums, tm):
    seq = x2d.shape[0]
    nt, n = seq // tm, len(sums)
    relay_step = (3 * nt) // 8

    def body(dqf, dqb, dg, dc, dlf, dlb, w, x_ref, g_ref, dx2_ref, *rest):
        ins, (gx_ref, dng_ref), outs = rest[:n], rest[n:n + 2], rest[n + 2:2 * n + 2]
        passing, joined = rest[2 * n + 2:3 * n + 2], rest[3 * n + 2:4 * n + 2]
        send_sems, recv_sems, local_sems = rest[4 * n + 2:]
        i = pl.program_id(0)
        c = lax.axis_index("c")
        first, second, diagonal = _route_chips()
        slot = lambda chip: 2 * chip[0] + chip[1]

        def remote(a, k, src, dst, to):
            return pltpu.make_async_remote_copy(src_ref=src, dst_ref=dst, send_sem=send_sems.at[3 * a + k],
                                                recv_sem=recv_sems.at[3 * a + k], device_id=(*to, c),
                                                device_id_type=MESH)

        direct = lambda a: remote(a, 0, ins[a].at[slot(first)], outs[a].at[0], first)
        for_second = lambda a: remote(a, 1, ins[a].at[slot(diagonal)], passing[a], first)
        joint = lambda a: remote(a, 2, joined[a], outs[a].at[1], second)
        own = lambda a: pltpu.make_async_copy(ins[a].at[slot(second)], joined[a], local_sems.at[a])

        @pl.when(i == 0)
        def _():
            _start_all([for_second(a) for a in range(n)] + [own(a) for a in range(n)] + [direct(a) for a in range(n)])
            dng_ref[...] = jnp.zeros(dng_ref.shape, F32)

        @pl.when(i == relay_step)
        def _():
            for a in range(n):
                for_second(a).wait_recv()
                own(a).wait()
                joined[a][...] = (joined[a][...].astype(F32) + passing[a][...].astype(F32)).astype(BF16)
                joint(a).start()

        dh = (_dot((dlf[...] + dlb[...]).astype(BF16), w[NAT_LR:NAT_LR + LR_W, :])
              + _dot(_both_directions(dqf, dqb), w[0:NAT_ZA, :])
              + _dot(dg[:, 0:CONV_W], w[NAT_ZA:NAT_LR, :]) + _dot(dg[:, CONV_W:2 * CONV_W], w[NAT_B:NAT_C, :])
              + _dot(dg[:, 2 * CONV_W:], w[NAT_ZC:IN_W, :]) + _dot(dc[...], w[NAT_C:NAT_ZC, :]))
        xv = x_ref[...]
        r = lax.rsqrt(jnp.mean(xv * xv, axis=-1, keepdims=True) + EPS)
        xn = xv * r
        dng_ref[...] += jnp.sum(dh * xn, axis=0, keepdims=True)
        dn = dh * g_ref[...]
        gx_ref[...] = (r * dn - xn * (r * jnp.mean(dn * xn, axis=-1, keepdims=True))) + dx2_ref[...]

        @pl.when(i == nt - 1)
        def _():
            for a in range(n):
                direct(a).wait_recv()
                joint(a).wait_recv()
            for a in range(n):
                for cp in (direct(a), for_second(a), joint(a)):
                    cp.wait_send()

    rowt = pl.BlockSpec((tm, D_MODEL), lambda i: (i, 0))
    seg = lambda width: pl.BlockSpec((tm, width), lambda i: (i, 0))
    resident = lambda rows: pl.BlockSpec((rows, D_MODEL), lambda i: (0, 0), pipeline_mode=pl.Buffered(1))
    hbm = pl.BlockSpec(memory_space=pl.ANY)
    blocks = [pltpu.VMEM(s.shape[1:], s.dtype) for s in sums]
    return pl.pallas_call(
        body, name="input_grad",
        out_shape=(jax.ShapeDtypeStruct((seq, D_MODEL), F32), jax.ShapeDtypeStruct((1, D_MODEL), F32))
        + tuple(jax.ShapeDtypeStruct((2,) + s.shape[1:], s.dtype) for s in sums),
        grid=(nt,),
        in_specs=[seg(QKV_W), seg(QKV_W), seg(GATES_W), seg(CH_W), seg(LR_W), seg(LR_W), resident(IN_W),
                  rowt, pl.BlockSpec((1, D_MODEL), lambda i: (0, 0)), rowt] + [hbm] * n,
        out_specs=(rowt, pl.BlockSpec((1, D_MODEL), lambda i: (0, 0))) + (hbm,) * n,
        scratch_shapes=blocks + blocks + [pltpu.SemaphoreType.DMA((3 * n,)), pltpu.SemaphoreType.DMA((3 * n,)),
                                          pltpu.SemaphoreType.DMA((n,))],
        compiler_params=_cparams("arbitrary"),
    )(dqkv_f, dqkv_b, dp_gates, dp_ch, dlr_f, dlr_b, w_nat, x2d, norm_g, dx2, *sums)


def _weight_grad_out(y_t, dx2b, tk, riding):
    m, seq = y_t.shape
    n = dx2b.shape[1]
    nk = seq // tk

    def body(a_ref, b_ref, ride_in, o_ref, ride_out, send_sems, recv_sems):
        k = pl.program_id(0)

        @pl.when(k == 0)
        def _():
            _start_all(_sibling_copies(ride_in, ride_out, send_sems, recv_sems))
            o_ref[...] = jnp.zeros(o_ref.shape, F32)

        o_ref[...] += _dot(a_ref[...], b_ref[...])

        @pl.when(k == nk - 1)
        def _():
            _wait_all(_sibling_copies(ride_in, ride_out, send_sems, recv_sems))

    hbm = pl.BlockSpec(memory_space=pl.ANY)
    return pl.pallas_call(
        body, name="wgrad_out",
        out_shape=(jax.ShapeDtypeStruct((m, n), F32), jax.ShapeDtypeStruct((4,) + _block_shape(riding), F32)),
        grid=(nk,),
        in_specs=[pl.BlockSpec((m, tk), lambda k: (0, k)), pl.BlockSpec((tk, n), lambda k: (k, 0)), hbm],
        out_specs=(pl.BlockSpec((m, n), lambda k: (0, 0)), hbm),
        scratch_shapes=[pltpu.SemaphoreType.DMA((4,)), pltpu.SemaphoreType.DMA((4,))],
        compiler_params=_cparams("arbitrary"),
    )(y_t, dx2b, riding)


def _weight_grad_in(h_t, dqkv_f, dqkv_b, dp_gates, dp_ch, dlr_f, dlr_b):
    m, seq = h_t.shape
    tn = 512
    n_qkv, n_gates, n_ch = QKV_W // tn, GATES_W // tn, CH_W // tn
    starts = ([k * tn for k in range(n_qkv)] + [NAT_ZA, NAT_ZA + tn, NAT_B, NAT_B + tn, NAT_ZC, NAT_ZC + tn]
              + [NAT_C + k * tn for k in range(n_ch)])

    def out_row(j):
        row = 0
        for k, start in enumerate(starts):
            row = row + jnp.where(j == k, start // 32, 0)
        return pl.multiple_of(row * 32, 32), 0

    def body(a_ref, bqf, bqb, bg, bc, o_ref, acc, bq):
        j = pl.program_id(0)

        @pl.when(j < n_qkv)
        def _():
            bq[...] = _both_directions(bqf, bqb)
            acc[...] = _dot(a_ref[...], bq[...])

        @pl.when(jnp.logical_and(j >= n_qkv, j < n_qkv + n_gates))
        def _():
            acc[...] = _dot(a_ref[...], bg[...])

        @pl.when(j >= n_qkv + n_gates)
        def _():
            acc[...] = _dot(a_ref[...], bc[...])

        o_ref[...] = acc[...].T

    resident = pl.BlockSpec((m, seq), lambda j: (0, 0), pipeline_mode=pl.Buffered(1))
    seg = lambda first, count: pl.BlockSpec((seq, tn), lambda j: (0, jnp.clip(j - first, 0, count - 1)))
    main = pl.pallas_call(
        body, name="wgrad_in",
        out_shape=jax.ShapeDtypeStruct((IN_W, m), F32),
        grid=(n_qkv + n_gates + n_ch,),
        in_specs=[resident, seg(0, n_qkv), seg(0, n_qkv), seg(n_qkv, n_gates), seg(n_qkv + n_gates, n_ch)],
        out_specs=pl.BlockSpec((pl.Element(tn), pl.Element(m)), out_row),
        scratch_shapes=[pltpu.VMEM((m, tn), F32), pltpu.VMEM((seq, tn), BF16)],
        compiler_params=_cparams("arbitrary"),
    )(h_t, dqkv_f, dqkv_b, dp_gates, dp_ch)

    def lr_body(a_ref, bf_ref, bb_ref, full_ref, o_ref, acc):
        acc[...] = _dot(a_ref[...], (bf_ref[...] + bb_ref[...]).astype(BF16))
        o_ref[...] = acc[...].T[0:2 * RANK, :]

    whole = lambda shape: pl.BlockSpec(shape, lambda j: (0, 0))
    return pl.pallas_call(
        lr_body, name="wgrad_lr",
        out_shape=jax.ShapeDtypeStruct((IN_W, m), F32),
        grid=(1,),
        in_specs=[whole((m, seq)), whole((seq, LR_W)), whole((seq, LR_W)), pl.BlockSpec(memory_space=pl.ANY)],
        out_specs=pl.BlockSpec((pl.Element(2 * RANK), pl.Element(m)), lambda j: (NAT_LR, 0)),
        scratch_shapes=[pltpu.VMEM((m, LR_W), F32)],
        input_output_aliases={3: 0},
        compiler_params=_cparams("arbitrary"),
    )(h_t, dlr_f, dlr_b, main)


def _pad_rows(a, rows):
    return jnp.pad(a, ((0, rows - a.shape[0]), (0, 0)))


def _rows128(a):
    a = a.reshape(-1, 128)
    return _pad_rows(a, -(-a.shape[0] // 8) * 8)


def _pack(arrs):
    return jnp.concatenate([_rows128(a) for a in arrs], axis=0)


def _unpack(buf, like):
    out, start = [], 0
    for a in like:
        rows = a.size // 128
        out.append(buf[start:start + rows].reshape(a.shape))
        start += -(-rows // 8) * 8
    return out


def kernel(x, norm_g, w_in, w_gk_f, b_gk_f, w_gk_b, b_gk_b, gla_norm_g, conv_w, conv_b, w_out, final_g, loss_target, m_norm_g, m_w_in, m_w_gk_f, m_b_gk_f, m_w_gk_b, m_b_gk_b, m_gla_norm_g, m_conv_w, m_conv_b, m_w_out, m_final_g, v_norm_g, v_w_in, v_w_gk_f, v_b_gk_f, v_w_gk_b, v_b_gk_b, v_gla_norm_g, v_conv_w, v_conv_b, v_w_out, v_final_g):
    px, py, pc = _position()
    me = _blk(px, py, pc)
    seq = x.shape[1]
    x2d, tgt = x[0], loss_target[0]
    tt = min(256, seq)

    small_s = jnp.concatenate([jnp.concatenate([w_gk_f[0], w_gk_b[0]], axis=1), _pad_rows(conv_w[0], 8)], axis=0)
    order = sum(jnp.where(2 * px + py == k, jnp.asarray(tiles + (0,), jnp.int32), 0) for k, tiles in enumerate(TILE_ORDER))
    proj, lr, h_t, w_nat, wout_all, small_all = _gather_inproj(x2d, norm_g, w_in[0].T, w_out[0], small_s, order,
                                                               min(1024, seq))
    w_out_full = wout_all.reshape(MIX_W, D_MODEL)
    wgk_cols = 512 // N_DEV
    wgk_f_full = small_all[:, 0:RANK, 0:wgk_cols].transpose(1, 0, 2).reshape(RANK, QK_W)
    wgk_b_full = small_all[:, 0:RANK, wgk_cols:2 * wgk_cols].transpose(1, 0, 2).reshape(RANK, QK_W)
    conv_w_full = _pad_rows(small_all[:, RANK:RANK + 3, :].transpose(1, 0, 2).reshape(3, CONV_W), 8)
    zr = lambda n: jnp.zeros((n, QK_W), F32)
    wgk_f_pad = jnp.concatenate([wgk_f_full, zr(LR_W - RANK)], axis=0).astype(BF16)
    wgk_b_pad = jnp.concatenate([zr(RANK), wgk_b_full, zr(LR_W - 2 * RANK)], axis=0).astype(BF16)

    o_f, o_b, st_f, st_b = _gla_fwd(proj, lr, wgk_f_pad, wgk_b_pad, b_gk_f, b_gk_b, tt)
    tmix = min(256, seq)
    y_t, conv, dx2, dx2b, loss_p, dfg_p = _mix_out_loss(o_f, o_b, proj, x2d, tgt, gla_norm_g, conv_w_full, conv_b,
                                                        w_out_full, final_g.reshape(1, D_MODEL), tmix)

    dp_gates, do, dconv, dgg_p, dcb_p = _mix_bwd(dx2b, o_f, o_b, proj, conv, gla_norm_g, w_out_full, tmix)
    dp_ch, dcw_p = _conv_bwd(dconv, proj, conv_w_full, tmix)
    dqkv_f, dlr_f, dqkv_b, dlr_b, dwf_p, dwb_p, dbf_p, dbb_p = _gla_bwd(
        proj, lr, do, st_f, st_b, wgk_f_pad, wgk_b_pad, b_gk_f, b_gk_b, tt)
    dw_nat = _weight_grad_in(h_t, dqkv_f, dqkv_b, dp_gates, dp_ch, dlr_f, dlr_b)

    dw_out, sib_in = _weight_grad_out(y_t, dx2b, min(1024, seq), dw_nat)
    part_out = dw_out.reshape(N_DEV, MIX_W // N_DEV, D_MODEL)
    core = jnp.reshape(pc, (1,)).astype(jnp.int32)
    chip = jnp.reshape(2 * px + py, (1,)).astype(jnp.int32)
    sums_in, sib_out = _chip_sums(dw_nat, sib_in, core, 512, "chip_sums_in", riding=part_out)
    sums_out = _chip_sums(part_out, sib_out, core, D_MODEL, "chip_sums_out")
    grad_x2d, dng_p, far_in, far_out = _input_grad(dqkv_f, dqkv_b, dp_gates, dp_ch, dlr_f, dlr_b, w_nat, x2d, norm_g, dx2,
                                                   [sums_in, sums_out], tmix)
    pieces = [dng_p, dbf_p, dbb_p, dgg_p, dcb_p, dfg_p[0], dwf_p[0:RANK], dwb_p[RANK:2 * RANK], dcw_p[0:3], loss_p[0]]
    g_window, small_tot = _final_sum(sums_in, far_in, chip, _pack(pieces), 512, "final_sum_in")
    g_in_t = lax.dynamic_slice_in_dim(g_window, 4 * pc, SHARD_W, axis=0)
    g_w_out, d_w_out, nm_w_out, nv_w_out = _final_sum_adamw(sums_out, far_out, chip, w_out[0], m_w_out[0], v_w_out[0],
                                                            256, "adamw_out")
    flat = lambda a: a[0].T.reshape(SHARD_W, D_MODEL // 128, 128)
    unflat = lambda a: a.reshape(SHARD_W, D_MODEL).T
    d_flat, m_flat, v_flat = _adamw_rows(g_in_t.reshape(SHARD_W, D_MODEL // 128, 128), flat(w_in), flat(m_w_in),
                                         flat(v_w_in), 180, "adamw_in")
    g_w_in, d_w_in, nm_w_in, nv_w_in = g_in_t.T, unflat(d_flat), unflat(m_flat), unflat(v_flat)

    tot = _unpack(small_tot, pieces)
    g_norm_g, g_b_gk_f, g_b_gk_b, g_gla, g_conv_b, g_final = tot[:6]
    g_wgk_f = lax.dynamic_slice_in_dim(tot[6], me * wgk_cols, wgk_cols, axis=1)[None]
    g_wgk_b = lax.dynamic_slice_in_dim(tot[7], me * wgk_cols, wgk_cols, axis=1)[None]
    g_conv_w = lax.dynamic_slice_in_dim(tot[8], me * 128, 128, axis=1)[None]
    loss = tot[9][0]

    small_g = [g_norm_g, g_b_gk_f, g_b_gk_b, g_gla, g_conv_b, g_final, g_wgk_f, g_wgk_b, g_conv_w]
    small_w = [norm_g, b_gk_f, b_gk_b, gla_norm_g, conv_b, final_g, w_gk_f, w_gk_b, conv_w]
    small_m = [m_norm_g, m_b_gk_f, m_b_gk_b, m_gla_norm_g, m_conv_b, m_final_g, m_w_gk_f, m_w_gk_b, m_conv_w]
    small_v = [v_norm_g, v_b_gk_f, v_b_gk_b, v_gla_norm_g, v_conv_b, v_final_g, v_w_gk_f, v_w_gk_b, v_conv_w]
    d_s, m_s, v_s = _adamw_small(_pack(small_g), _pack(small_w), _pack(small_m), _pack(small_v))
    d_l, m_l, v_l = _unpack(d_s, small_w), _unpack(m_s, small_w), _unpack(v_s, small_w)

    def ordered(sm, big_in, big_out):
        return [sm[0], big_in[None], sm[6], sm[1], sm[7], sm[2], sm[3], sm[8], sm[4], big_out[None], sm[5]]

    grads = ordered(small_g, g_w_in, g_w_out)
    deltas = ordered(d_l, d_w_in, d_w_out)
    new_m = ordered(m_l, nm_w_in, nm_w_out)
    new_v = ordered(v_l, nv_w_in, nv_w_out)
    return (loss, grad_x2d[None], *grads, *deltas, *new_m, *new_v)
```

```python
import functools

import jax
import jax.numpy as jnp
from jax import lax
from jax.experimental import pallas as pl
from jax.experimental.pallas import tpu as pltpu

F32 = jnp.float32
BF16 = jnp.bfloat16
MESH = pl.DeviceIdType.MESH

N_DEV = 8
D_MODEL = 1024
HEADS = 4
DK = 128
DV = 256
QK_W = HEADS * DK
V_W = HEADS * DV
CONV_W = 1024
MIX_W = V_W + CONV_W
CHUNK = 64
RANK = 16
IN_W = 7200
SHARD_W = IN_W // N_DEV
MAIN_W = 7168
LR_W = 128
OFF_Q, OFF_K, OFF_V, OFF_ZA, OFF_B, OFF_ZC, OFF_C, OFF_H = 0, 512, 1024, 2048, 3072, 4096, 5120, 6144
QKV_W, GATES_W, CH_W = 2048, 3072, 2048
NAT_ZA, NAT_LR, NAT_B, NAT_C, NAT_ZC = 2048, 3072, 3104, 4128, 6176
EPS = 1e-6
GATE_SCALE = 1.0 / 16.0
QSCALE = DK ** -0.5
REF_F, LAST_F = CHUNK // 2, CHUNK - 1
REF_B, LAST_B = CHUNK - 1 - CHUNK // 2, 0

ADAM_LR = 0.001
ADAM_B1 = 0.9
ADAM_B2 = 0.999
ADAM_EPS = 1e-08
ADAM_WD = 0.01
ADAM_STEP = 10

VMEM_LIMIT = 56 * 1024 * 1024


def _cparams(*sem):
    return pltpu.CompilerParams(dimension_semantics=sem, vmem_limit_bytes=VMEM_LIMIT)


def _dot(a, b):
    return jnp.dot(a, b, preferred_element_type=F32)


def _dot_nt(a, b):
    return lax.dot_general(a, b, (((1,), (1,)), ((), ())), preferred_element_type=F32)


def _dot_tn(a, b):
    return lax.dot_general(a, b, (((0,), (0,)), ((), ())), preferred_element_type=F32)


def _sigmoid(z):
    return jax.nn.sigmoid(z)


def _position():
    return lax.axis_index("x"), lax.axis_index("y"), lax.axis_index("c")


def _blk(px, py, pc):
    return 4 * px + 2 * py + pc


EDGE = 16
SHIFTED_ROWS = 912
BODY_ROWS = SHIFTED_ROWS - 2 * EDGE


def _first_tile_row(blk, px):
    return EDGE * (56 * blk + px)


def _edge_tiles():
    tiles = {}
    for blk in range(N_DEV):
        first = _first_tile_row(blk, blk // 4)
        tiles.setdefault(first, []).append((blk, 0))
        tiles.setdefault(first + EDGE + BODY_ROWS, []).append((blk, 1))
    return tiles


def _peer_copies(srcs, outs, send_sems, recv_sems):
    x, y, c = _position()
    me = _blk(x, y, c)
    copies = []
    for a, (src, out) in enumerate(zip(srcs, outs)):
        k = 0
        for dx in (0, 1):
            for dy in (0, 1):
                for dc in (0, 1):
                    if dx + dy + dc == 0:
                        continue
                    peer = (1 - x if dx else x, 1 - y if dy else y, 1 - c if dc else c)
                    copies.append(pltpu.make_async_remote_copy(
                        src_ref=src, dst_ref=out.at[me], send_sem=send_sems.at[a * 7 + k],
                        recv_sem=recv_sems.at[a * 7 + k], device_id=peer, device_id_type=MESH))
                    k += 1
    return copies


def _route_chips():
    x, y, c = _position()
    along_x = c == 0
    return [(jnp.where(along_x, 1 - x, x), jnp.where(along_x, y, 1 - y)),
            (jnp.where(along_x, x, 1 - x), jnp.where(along_x, 1 - y, y)), (1 - x, 1 - y)]


WINDOW_ROWS = SHARD_W + 4


def _window_start(k, parity):
    return 2 * SHARD_W * k + (SHARD_W - 4) * parity


def _owner_block(part, k, parity):
    if part.ndim == 3:
        return part.at[2 * k + parity]
    return part.at[pl.ds(pl.multiple_of(_window_start(k, parity), 8), WINDOW_ROWS)]


def _block_shape(part):
    return part.shape[1:] if part.ndim == 3 else (WINDOW_ROWS, part.shape[1])


def _sibling_copies(part, out, send_sems, recv_sems):
    x, y, c = _position()
    return [pltpu.make_async_remote_copy(src_ref=_owner_block(part, k, 1 - c), dst_ref=out.at[k],
                                         send_sem=send_sems.at[k], recv_sem=recv_sems.at[k],
                                         device_id=(x, y, 1 - c), device_id_type=MESH)
            for k in range(4)]


def _start_all(copies):
    for cp in copies:
        cp.start()


def _wait_all(copies):
    for cp in copies:
        cp.wait_recv()
    for cp in copies:
        cp.wait_send()


def _chip_sums(part, from_sibling, core, tc, name, riding=None):
    rows, cols = _block_shape(part)
    nj = cols // tc

    def body(core_ref, p_ref, s_ref, *rest):
        if riding is None:
            (o_ref,) = rest
        else:
            ride_in, o_ref, ride_out, send_sems, recv_sems = rest
            k, j = pl.program_id(0), pl.program_id(1)

            @pl.when(jnp.logical_and(k == 0, j == 0))
            def _():
                _start_all(_sibling_copies(ride_in, ride_out, send_sems, recv_sems))

        o_ref[0] = (p_ref[...].reshape(rows, tc) + s_ref[0]).astype(BF16)

        if riding is not None:
            @pl.when(jnp.logical_and(k == 3, j == nj - 1))
            def _():
                _wait_all(_sibling_copies(ride_in, ride_out, send_sems, recv_sems))

    hbm = pl.BlockSpec(memory_space=pl.ANY)
    sums = jax.ShapeDtypeStruct((4, rows, cols), BF16)
    tile_out = pl.BlockSpec((1, rows, tc), lambda k, j, core_ref: (k, 0, j))
    if part.ndim == 3:
        mine = pl.BlockSpec((1, rows, tc), lambda k, j, core_ref: (2 * k + core_ref[0], 0, j))
    else:
        mine = pl.BlockSpec((pl.Element(rows), pl.Element(tc)),
                            lambda k, j, core_ref: (pl.multiple_of(_window_start(k, core_ref[0]), 8),
                                                    pl.multiple_of(j * tc, 128)))
    in_specs = [mine, pl.BlockSpec((1, rows, tc), lambda k, j, core_ref: (k, 0, j))]
    if riding is None:
        out_shape, out_specs, scratch, args = sums, tile_out, [], (core, part, from_sibling)
    else:
        out_shape = (sums, jax.ShapeDtypeStruct((4,) + _block_shape(riding), F32))
        out_specs, in_specs = (tile_out, hbm), in_specs + [hbm]
        scratch = [pltpu.SemaphoreType.DMA((4,)), pltpu.SemaphoreType.DMA((4,))]
        args = (core, part, from_sibling, riding)
    return pl.pallas_call(
        body, name=name, out_shape=out_shape,
        grid_spec=pltpu.PrefetchScalarGridSpec(num_scalar_prefetch=1, grid=(4, nj), in_specs=in_specs,
                                               out_specs=out_specs, scratch_shapes=scratch),
        compiler_params=_cparams("arbitrary", "arbitrary"),
    )(*args)


def _sum_chips(s_ref, r_ref):
    f = lambda a: a.astype(F32)
    return (f(s_ref[0]) + f(r_ref[0])) + f(r_ref[1])


def _final_sum(sums, from_chips, chip, small, tc, name):
    _, rows, cols = sums.shape
    nj = cols // tc

    def body(chip_ref, s_ref, r_ref, sm_ref, g_out, tot_ref, all_ref, send_sems, recv_sems):
        j = pl.program_id(0)
        me = _blk(*_position())

        @pl.when(j == 0)
        def _():
            all_ref[me] = sm_ref[...]
            _start_all(_peer_copies((all_ref.at[me],), (all_ref,), send_sems, recv_sems))

        g_out[...] = _sum_chips(s_ref, r_ref)

        @pl.when(j == nj - 1)
        def _():
            _wait_all(_peer_copies((all_ref.at[me],), (all_ref,), send_sems, recv_sems))
            acc = all_ref[0]
            for d in range(1, N_DEV):
                acc = acc + all_ref[d]
            tot_ref[...] = acc

    whole = pl.BlockSpec(small.shape, lambda j, chip_ref: (0, 0))
    return pl.pallas_call(
        body, name=name,
        out_shape=(jax.ShapeDtypeStruct((rows, cols), F32), jax.ShapeDtypeStruct(small.shape, F32)),
        grid_spec=pltpu.PrefetchScalarGridSpec(
            num_scalar_prefetch=1, grid=(nj,),
            in_specs=[pl.BlockSpec((1, rows, tc), lambda j, chip_ref: (chip_ref[0], 0, j)),
                      pl.BlockSpec((2, rows, tc), lambda j, chip_ref: (0, 0, j)), whole],
            out_specs=(pl.BlockSpec((rows, tc), lambda j, chip_ref: (0, j)), whole),
            scratch_shapes=[pltpu.VMEM((N_DEV,) + small.shape, F32), pltpu.SemaphoreType.DMA((7,)),
                            pltpu.SemaphoreType.DMA((7,))]),
        compiler_params=_cparams("arbitrary"),
    )(chip, sums, from_chips, small)


def _adamw_rows(g, w, m, v, tr, name):
    rows = g.shape[0]

    def body(g_ref, w_ref, m_ref, v_ref, d_out, m_out, v_out):
        delta, m_new, v_new = _adamw(w_ref[...], g_ref[...], m_ref[...], v_ref[...])
        d_out[...] = delta
        m_out[...] = m_new
        v_out[...] = v_new

    tile = pl.BlockSpec((tr,) + g.shape[1:], lambda r: (r, 0, 0))
    shp = jax.ShapeDtypeStruct(g.shape, F32)
    return pl.pallas_call(
        body, name=name, out_shape=(shp, shp, shp), grid=(rows // tr,),
        in_specs=[tile] * 4, out_specs=(tile, tile, tile),
        compiler_params=_cparams("arbitrary"),
    )(g, w, m, v)


def _adamw(w, g, m, v):
    m = ADAM_B1 * m + (1.0 - ADAM_B1) * g
    v = ADAM_B2 * v + (1.0 - ADAM_B2) * (g * g)
    m_hat = m / (1.0 - ADAM_B1 ** ADAM_STEP)
    v_hat = v / (1.0 - ADAM_B2 ** ADAM_STEP)
    delta = -ADAM_LR * (m_hat / (jnp.sqrt(v_hat) + ADAM_EPS) + ADAM_WD * w)
    return delta, m, v


def _final_sum_adamw(sums, from_chips, chip, w, m, v, tr, name):
    rows, cols = w.shape

    def body(chip_ref, s_ref, r_ref, w_ref, m_ref, v_ref, g_out, d_out, m_out, v_out):
        g = _sum_chips(s_ref, r_ref)
        delta, m_new, v_new = _adamw(w_ref[...], g, m_ref[...], v_ref[...])
        g_out[...] = g
        d_out[...] = delta
        m_out[...] = m_new
        v_out[...] = v_new

    tile = pl.BlockSpec((tr, cols), lambda r, chip_ref: (r, 0))
    shp = jax.ShapeDtypeStruct((rows, cols), F32)
    return pl.pallas_call(
        body, name=name,
        out_shape=(shp, shp, shp, shp),
        grid_spec=pltpu.PrefetchScalarGridSpec(
            num_scalar_prefetch=1, grid=(rows // tr,),
            in_specs=[pl.BlockSpec((1, tr, cols), lambda r, chip_ref: (chip_ref[0], r, 0)),
                      pl.BlockSpec((2, tr, cols), lambda r, chip_ref: (0, r, 0)),
                      tile, tile, tile],
            out_specs=(tile, tile, tile, tile)),
        compiler_params=_cparams("arbitrary"),
    )(chip, sums, from_chips, w, m, v)


def _adamw_small(g, w, m, v):
    def body(g_ref, w_ref, m_ref, v_ref, d_out, m_out, v_out):
        delta, m_new, v_new = _adamw(w_ref[...], g_ref[...], m_ref[...], v_ref[...])
        d_out[...] = delta
        m_out[...] = m_new
        v_out[...] = v_new

    vmem = pl.BlockSpec(memory_space=pltpu.VMEM)
    shp = jax.ShapeDtypeStruct(g.shape, F32)
    return pl.pallas_call(body, name="adamw_small", out_shape=(shp, shp, shp),
                          in_specs=[vmem] * 4, out_specs=(vmem, vmem, vmem))(g, w, m, v)


TILE_ROWS = (0, 1024, NAT_ZA, NAT_B, NAT_ZC, NAT_C, NAT_C + CONV_W)


TILE_ORDER = ((0, 1, 2, 3, 5, 6, 4), (2, 1, 0, 4, 3, 5, 6), (5, 6, 0, 4, 1, 2, 3), (4, 6, 2, 3, 5, 0, 1))
EARLY_SWEEP, NEIGHBOUR_SWEEP, DIAGONAL_SWEEP = 1, 2, 4
PIECES, W_IN_PIECES, OTHER_PIECES = 4, (0, 1), (2, 3)


def _gather_inproj(x2d, norm_g, shard_t, w_out_s, small_s, order, tm):
    seq = x2d.shape[0]
    tn = CONV_W
    ni, nj = seq // tm, MAIN_W // tn
    first_sweep = lambda j, i, order_ref: jnp.where(j == 0, i, ni - 1)
    last_sweep = lambda j, i, order_ref: jnp.where(j == nj - 1, i, 0)
    edge_tiles = _edge_tiles()

    def body(order_ref, x_ref, g_ref, shard_ref, wout_ref, sm_ref, proj_ref, lr_ref, ht_ref, w_nat, wout_all, sm_all,
             w_all, h_all, edges, stage, wout_b, sm_b, send_sems, recv_sems, local_sems):
        j, i = pl.program_id(0), pl.program_id(1)
        rows = pl.ds(pl.multiple_of(i * tm, tm), tm)
        x, y, c = _position()
        me, here, sibling = _blk(x, y, c), (x, y, c), (x, y, 1 - c)
        chips = _route_chips()
        sibling_chips = [chips[1], chips[0], chips[2]]

        def pieces(px, py, pc):
            blk = _blk(px, py, pc)
            body_rows = pl.ds(pl.multiple_of(_first_tile_row(blk, px) + EDGE, EDGE), BODY_ROWS)
            return [w_all.at[body_rows], edges.at[blk], wout_all.at[blk], sm_all.at[blk]]

        def copy(a, k, block, to, staged=None):
            ref = pieces(*block)[a]
            return pltpu.make_async_remote_copy(src_ref=ref if staged is None else staged, dst_ref=ref,
                                                send_sem=send_sems.at[a * 7 + k], recv_sem=recv_sems.at[a * 7 + k],
                                                device_id=to, device_id_type=MESH)

        def own_copies(group, slots=(0, 1, 2)):
            targets = [sibling] + [(*chips[n], c) for n in range(2)]
            staged = [None, None, wout_b, sm_b]
            return [copy(a, k, here, targets[k], staged[a]) for k in slots for a in group]

        def relays(group):
            return [copy(a, 3, (*chips[0], c), (*chips[1], c)) for a in group]

        def forwards(n, group):
            return [copy(a, 4 + n, (*chips[n], c), sibling) for a in group]

        def keep_own():
            return [pltpu.make_async_copy(wout_b, wout_all.at[me], local_sems.at[0]),
                    pltpu.make_async_copy(sm_b, sm_all.at[me], local_sems.at[1])]

        def keep_weight():
            return pltpu.make_async_copy(w_all, w_nat, local_sems.at[2])

        def take(ns, group):
            for n in ns:
                for a in group:
                    copy(a, 1 + n, (*chips[n], c), here).wait_recv()
                _start_all((relays(group) if n == 0 else []) + forwards(n, group))

        def take_passed_on(ns, group):
            for n in ns:
                for a in group:
                    copy(a, 4 + n, (*sibling_chips[n], 1 - c), here).wait_recv()

        def arrive(ns, group):
            take(ns, group)
            take_passed_on(ns, group)

        def per_core_and_row(step):
            for core in range(2):
                for row in range(2):
                    pl.when(jnp.logical_and(c == core, y == row))(functools.partial(step, core, row))

        def start_own(core, row):
            now = (0, 1 + core) if core == row else (0, 1, 2)
            _start_all(own_copies(W_IN_PIECES, now))
            wout_b[...] = wout_ref[...].astype(BF16)
            sm_b[...] = sm_ref[...]
            _start_all(own_copies(OTHER_PIECES, now) + keep_own())

        def take_early(core, row):
            if core == row:
                _start_all(own_copies(W_IN_PIECES, (2 - core,)) + own_copies(OTHER_PIECES, (2 - core,)))
                take((1 - core,), W_IN_PIECES)
            else:
                take_passed_on((core,), W_IN_PIECES)

        def take_neighbours(core, row):
            if core == row:
                take((core,), W_IN_PIECES)
                take_passed_on((0, 1), W_IN_PIECES)
            else:
                take((0, 1), W_IN_PIECES)
                take_passed_on((1 - core,), W_IN_PIECES)

        early_blk = _blk(x, 1 - y, y)

        def add_edge_tiles(stage):
            for row, parts in edge_tiles.items():
                ready = 0
                for blk, _ in parts:
                    away = (x != blk // 4).astype(jnp.int32) + (y != (blk // 2) % 2).astype(jnp.int32)
                    late = jnp.where(away == 1, jnp.where(early_blk == blk, 1, 2), jnp.where(away == 2, 3, 0))
                    ready = jnp.maximum(ready, late)

                @pl.when(ready == stage)
                def _(row=row, parts=parts):
                    tile = edges[parts[0][0], parts[0][1]].astype(F32)
                    for blk, side in parts[1:]:
                        tile = tile + edges[blk, side].astype(F32)
                    w_all[row:row + EDGE, :] = tile.astype(BF16)

        @pl.when(jnp.logical_and(j == 0, i == 0))
        def _():
            last = SHARD_W // 8 * 8
            for col in range(0, D_MODEL, 128):
                cols = slice(col, col + 128)
                stage[0:last, :] = shard_ref[0:last, cols]
                stage[last:, :] = jnp.zeros((SHIFTED_ROWS - last, 128), F32)
                stage[last:SHARD_W, :] = shard_ref[last:SHARD_W, cols]
                for k in range(EDGE // 4):
                    @pl.when(me % 4 == k)
                    def _(k=k, cols=cols):
                        moved = pltpu.roll(stage[...], 4 * k, 0) if k else stage[...]
                        pieces(*here)[0][:, cols] = moved[EDGE:EDGE + BODY_ROWS].astype(BF16)
                        edges[me, 0, :, cols] = moved[0:EDGE].astype(BF16)
                        edges[me, 1, :, cols] = moved[EDGE + BODY_ROWS:].astype(BF16)
            per_core_and_row(start_own)
            for a in W_IN_PIECES:
                copy(a, 0, sibling, here).wait_recv()
            add_edge_tiles(0)

        @pl.when(jnp.logical_and(j == EARLY_SWEEP, i == 0))
        def _():
            per_core_and_row(take_early)
            add_edge_tiles(1)

        @pl.when(jnp.logical_and(j == NEIGHBOUR_SWEEP, i == 0))
        def _():
            per_core_and_row(take_neighbours)
            add_edge_tiles(2)

        @pl.when(jnp.logical_and(j == DIAGONAL_SWEEP, i == 0))
        def _():
            arrive((2,), W_IN_PIECES)
            add_edge_tiles(3)
            keep_weight().start()
            arrive((0, 1), OTHER_PIECES)

        @pl.when(jnp.logical_and(j == nj - 1, i == 0))
        def _():
            arrive((2,), OTHER_PIECES)

        @pl.when(j == 0)
        def _():
            xv = x_ref[...]
            r = lax.rsqrt(jnp.mean(xv * xv, axis=-1, keepdims=True) + EPS)
            h = (xv * r) * g_ref[...]
            h_all[rows, :] = h.astype(BF16)
            ht_ref[...] = h.T.astype(BF16)

        tile = order_ref[j]
        row = 0
        for k, start in enumerate(TILE_ROWS):
            row = row + jnp.where(tile == k, start // 32, 0)
        w_tile = w_all[pl.ds(pl.multiple_of(row * 32, 32), tn), :]
        proj_ref[...] = _dot_nt(h_all[rows, :], w_tile).astype(BF16)

        @pl.when(j == nj - 1)
        def _():
            lr_ref[...] = _dot_nt(h_all[rows, :], w_all[NAT_LR:NAT_LR + LR_W, :])

        @pl.when(jnp.logical_and(j == nj - 1, i == ni - 1))
        def _():
            everything = range(PIECES)
            passed_on = [cp for n in range(3) for cp in forwards(n, everything)]
            for cp in own_copies(everything) + relays(everything) + passed_on:
                cp.wait_send()
            for a in OTHER_PIECES:
                copy(a, 0, sibling, here).wait_recv()
            for cp in keep_own() + [keep_weight()]:
                cp.wait()

    const = lambda shape: pl.BlockSpec(shape, lambda j, i, order_ref: (0,) * len(shape))
    hbm = pl.BlockSpec(memory_space=pl.ANY)
    vmem = pl.BlockSpec(memory_space=pltpu.VMEM)
    return pl.pallas_call(
        body, name="gather_inproj",
        out_shape=(jax.ShapeDtypeStruct((seq, MAIN_W), BF16), jax.ShapeDtypeStruct((seq, LR_W), F32),
                   jax.ShapeDtypeStruct((D_MODEL, seq), BF16), jax.ShapeDtypeStruct((IN_W, D_MODEL), BF16),
                   jax.ShapeDtypeStruct((N_DEV,) + w_out_s.shape, BF16),
                   jax.ShapeDtypeStruct((N_DEV,) + small_s.shape, F32)),
        grid_spec=pltpu.PrefetchScalarGridSpec(
            num_scalar_prefetch=1, grid=(nj, ni),
            in_specs=[pl.BlockSpec((tm, D_MODEL), lambda j, i, order_ref: (first_sweep(j, i, order_ref), 0)),
                      const((1, D_MODEL)), vmem, vmem, const(small_s.shape)],
            out_specs=(pl.BlockSpec((tm, tn), lambda j, i, order_ref: (i, order_ref[j])),
                       pl.BlockSpec((tm, LR_W), lambda j, i, order_ref: (last_sweep(j, i, order_ref), 0)),
                       pl.BlockSpec((D_MODEL, tm), lambda j, i, order_ref: (0, first_sweep(j, i, order_ref))),
                       hbm, hbm, hbm),
            scratch_shapes=[pltpu.VMEM((IN_W, D_MODEL), BF16), pltpu.VMEM((seq, D_MODEL), BF16),
                            pltpu.VMEM((N_DEV, 2, EDGE, D_MODEL), BF16), pltpu.VMEM((SHIFTED_ROWS, 128), F32),
                            pltpu.VMEM(w_out_s.shape, BF16), pltpu.VMEM(small_s.shape, F32),
                            pltpu.SemaphoreType.DMA((7 * PIECES,)), pltpu.SemaphoreType.DMA((7 * PIECES,)),
                            pltpu.SemaphoreType.DMA((3,))]),
        compiler_params=_cparams("arbitrary", "arbitrary"),
    )(order, x2d, norm_g, shard_t, w_out_s, small_s)


def _block_masks(tt):
    row = lax.broadcasted_iota(jnp.int32, (tt, tt), 0)
    col = lax.broadcasted_iota(jnp.int32, (tt, tt), 1)
    same = jnp.right_shift(row, 6) == jnp.right_shift(col, 6)
    return (jnp.logical_and(same, col <= row), jnp.logical_and(same, col >= row), jnp.logical_and(same, col > row))


def _dot_split3(ones_mat, x):
    x1 = x.astype(BF16)
    r1 = x - x1.astype(F32)
    x2 = r1.astype(BF16)
    x3 = (r1 - x2.astype(F32)).astype(BF16)
    return (_dot(ones_mat, x3) + _dot(ones_mat, x2)) + _dot(ones_mat, x1)


def _log_gate(logits):
    return (jnp.minimum(logits, 0.0) - jnp.log(1.0 + jnp.exp(-jnp.abs(logits)))) * GATE_SCALE


def _chunk_column_mask(tt):
    nc = tt // CHUNK
    row = lax.broadcasted_iota(jnp.int32, (tt, nc * DK), 0)
    col = lax.broadcasted_iota(jnp.int32, (tt, nc * DK), 1)
    return jnp.right_shift(row, 6) == jnp.right_shift(col, 7)


def _chunked(mask, x, nc):
    wide = jnp.concatenate([x] * nc, axis=1)
    return jnp.where(mask, wide, jnp.zeros_like(wide))


def _gla_fwd(proj, lr, wgk_f, wgk_b, bgk_f, bgk_b, tt):
    seq = proj.shape[0]
    nb, nc, nch = seq // tt, tt // CHUNK, seq // CHUNK

    def body(qf, kf, vf, lrf, qb, kb, vb, lrb, wf, wb, bf, bb, of, ob, stf, stb, s_scr, qs_s, ks_s, qin_s, kout_s):
        @pl.when(pl.program_id(0) == 0)
        def _():
            s_scr[...] = jnp.zeros(s_scr.shape, F32)

        low, upp, sup = _block_masks(tt)
        dirs = ((qf, kf, vf, lrf, wf, bf, of, stf, low, low, REF_F, LAST_F, list(range(nc))),
                (qb, kb, vb, lrb, wb, bb, ob, stb, upp, sup, REF_B, LAST_B, list(reversed(range(nc)))))
        for d, (q_r, k_r, v_r, lr_r, w_r, b_r, o_r, st_r, cum, mask, ref, last, order) in enumerate(dirs):
            logits = _dot(lr_r[...].astype(BF16), w_r[...]) + b_r[...]
            b = _dot_split3(cum.astype(BF16), _log_gate(logits))
            decs = []
            for c in range(nc):
                rows = slice(c * CHUNK, (c + 1) * CHUNK)
                bc = b[rows]
                b_ref, b_last = bc[ref:ref + 1], bc[last:last + 1]
                qc = q_r[rows, :].astype(F32) * QSCALE
                kc = k_r[rows, :].astype(F32)
                qs_s[rows, :] = (qc * jnp.exp(bc - b_ref)).astype(BF16)
                ks_s[rows, :] = (kc * jnp.exp(b_ref - bc)).astype(BF16)
                qin_s[rows, :] = (qc * jnp.exp(bc)).astype(BF16)
                kout_s[rows, :] = (kc * jnp.exp(b_last - bc)).astype(BF16)
                decs.append(jnp.exp(b_last))
            for h in range(HEADS):
                ksl = slice(h * DK, (h + 1) * DK)
                vsl = slice(h * DV, (h + 1) * DV)
                v = v_r[:, vsl].astype(BF16)
                att = jnp.where(mask, _dot_nt(qs_s[:, ksl], ks_s[:, ksl]), 0.0).astype(BF16)
                o_intra = _dot(att, v)
                st = s_scr[d * HEADS + h]
                for c in order:
                    rows = slice(c * CHUNK, (c + 1) * CHUNK)
                    stb = st.astype(BF16)
                    st_r[c, h] = stb
                    o_r[rows, vsl] = (o_intra[rows] + _dot_nt(qin_s[rows, ksl], stb)).astype(BF16)
                    st = st * decs[c][:, ksl] + _dot_tn(v[rows], kout_s[rows, ksl])
                s_scr[d * HEADS + h] = st

    fw = lambda i: (i, 0)
    bw = lambda i: (nb - 1 - i, 0)
    const = lambda i: (0, 0)

    def tok_specs(m):
        return [pl.BlockSpec((tt, QK_W), lambda i: (m(i)[0], OFF_Q // QK_W)),
                pl.BlockSpec((tt, QK_W), lambda i: (m(i)[0], OFF_K // QK_W)),
                pl.BlockSpec((tt, V_W), lambda i: (m(i)[0], OFF_V // V_W)),
                pl.BlockSpec((tt, LR_W), m)]

    st_shape = jax.ShapeDtypeStruct((nch, HEADS, DV, DK), BF16)
    o_shape = jax.ShapeDtypeStruct((seq, V_W), BF16)
    operand = pltpu.VMEM((tt, QK_W), BF16)
    return pl.pallas_call(
        body, name="gla_fwd",
        out_shape=(o_shape, o_shape, st_shape, st_shape),
        grid=(nb,),
        in_specs=tok_specs(fw) + tok_specs(bw) + [
            pl.BlockSpec((LR_W, QK_W), const), pl.BlockSpec((LR_W, QK_W), const),
            pl.BlockSpec((1, QK_W), const), pl.BlockSpec((1, QK_W), const)],
        out_specs=(pl.BlockSpec((tt, V_W), fw), pl.BlockSpec((tt, V_W), bw),
                   pl.BlockSpec((nc, HEADS, DV, DK), lambda i: (i, 0, 0, 0)),
                   pl.BlockSpec((nc, HEADS, DV, DK), lambda i: (nb - 1 - i, 0, 0, 0))),
        scratch_shapes=[pltpu.VMEM((2 * HEADS, DV, DK), F32), operand, operand, operand, operand],
        compiler_params=_cparams("arbitrary"),
    )(proj, proj, proj, lr, proj, proj, proj, lr, wgk_f, wgk_b, bgk_f, bgk_b)


def _head_norm(o, gain):
    outs, rinv = [], []
    for h in range(HEADS):
        oh = o[:, h * DV:(h + 1) * DV]
        r = lax.rsqrt(jnp.mean(oh * oh, axis=-1, keepdims=True) + EPS)
        outs.append((oh * r) * gain)
        rinv.append(r)
    return jnp.concatenate(outs, axis=1), rinv


def _shift_rows(u, prev_row, next_row):
    n = u.shape[0]
    row = lax.broadcasted_iota(jnp.int32, (n, 1), 0)
    up = jnp.where(row == 0, prev_row, pltpu.roll(u, 1, 0))
    un = jnp.where(row == n - 1, next_row, pltpu.roll(u, n - 1, 0))
    return up, un


HALO = 16


def _halo_specs(tm, seq, col_block):
    per = tm // HALO
    last = seq // HALO - 1
    return [pl.BlockSpec((HALO, CONV_W), lambda i: (jnp.maximum(i * per - 1, 0), col_block)),
            pl.BlockSpec((HALO, CONV_W), lambda i: (jnp.minimum((i + 1) * per, last), col_block))]


def _f32(ref):
    return ref[...].astype(F32)


def _last_row(ref):
    return ref[HALO - 1:HALO, :].astype(F32)


def _first_row(ref):
    return ref[0:1, :].astype(F32)


def _mix_out_loss(o_f, o_b, proj, x2d, tgt, gla_g, conv_w, conv_b, w_out, final_g, tm):
    seq = x2d.shape[0]
    nt = seq // tm

    def body(of, ob, za, bg, cg, hc, zc, cprev, cnext, hprev, hnext, x_ref, t_ref, gg, cw, cb, wo, fg,
             yt_ref, conv_ref, dx2_ref, dx2b_ref, loss_ref, dfg_ref):
        i = pl.program_id(0)

        @pl.when(i == 0)
        def _():
            loss_ref[...] = jnp.zeros(loss_ref.shape, F32)
            dfg_ref[...] = jnp.zeros(dfg_ref.shape, F32)

        on, _ = _head_norm(_f32(of) + _f32(ob), gg[...])
        zav = _f32(za)
        y_a = on * (zav * _sigmoid(zav))
        u = _f32(cg) * _f32(hc)
        prev_row = jnp.where(i > 0, _last_row(cprev) * _last_row(hprev), 0.0)
        next_row = jnp.where(i < nt - 1, _first_row(cnext) * _first_row(hnext), 0.0)
        up, un = _shift_rows(u, prev_row, next_row)
        conv = (cw[0:1, :] * up + cw[1:2, :] * u + cw[2:3, :] * un) + cb[...]
        conv_ref[...] = conv.astype(BF16)
        zcv = _f32(zc)
        y_c = _f32(bg) * conv * (zcv * _sigmoid(zcv))
        y = jnp.concatenate([y_a, y_c], axis=1)
        yt_ref[...] = y.T.astype(BF16)
        x2 = x_ref[...] + _dot(y.astype(BF16), wo[...])
        r = lax.rsqrt(jnp.mean(x2 * x2, axis=-1, keepdims=True) + EPS)
        xn = x2 * r
        err = xn * fg[...] - t_ref[...]
        loss_ref[...] += 0.5 * jnp.sum(jnp.mean(err * err, axis=-1, keepdims=True))
        dyf = err * (1.0 / D_MODEL)
        dfg_ref[...] += jnp.sum(dyf * xn, axis=0, keepdims=True)
        dxn = dyf * fg[...]
        dx2 = r * dxn - xn * (r * jnp.mean(dxn * xn, axis=-1, keepdims=True))
        dx2_ref[...] = dx2
        dx2b_ref[...] = dx2.astype(BF16)

    def col(off):
        return pl.BlockSpec((tm, CONV_W), lambda i: (i, off // CONV_W))

    rowt = pl.BlockSpec((tm, D_MODEL), lambda i: (i, 0))
    const = lambda shape: pl.BlockSpec(shape, lambda i: (0, 0))
    return pl.pallas_call(
        body, name="mix_out_loss",
        out_shape=(jax.ShapeDtypeStruct((MIX_W, seq), BF16), jax.ShapeDtypeStruct((seq, CONV_W), BF16),
                   jax.ShapeDtypeStruct((seq, D_MODEL), F32), jax.ShapeDtypeStruct((seq, D_MODEL), BF16),
                   jax.ShapeDtypeStruct((8, 128), F32), jax.ShapeDtypeStruct((1, D_MODEL), F32)),
        grid=(nt,),
        in_specs=[rowt, rowt, col(OFF_ZA), col(OFF_B), col(OFF_C), col(OFF_H), col(OFF_ZC)]
        + _halo_specs(tm, seq, OFF_C // CONV_W) + _halo_specs(tm, seq, OFF_H // CONV_W)
        + [rowt, rowt, const((1, DV)), const((8, CONV_W)), const((1, CONV_W)), const((MIX_W, D_MODEL)),
           const((1, D_MODEL))],
        out_specs=(pl.BlockSpec((MIX_W, tm), lambda i: (0, i)), rowt, rowt, rowt, const((8, 128)),
                   const((1, D_MODEL))),
        compiler_params=_cparams("arbitrary"),
    )(o_f, o_b, proj, proj, proj, proj, proj, proj, proj, proj, proj, x2d, tgt, gla_g, conv_w, conv_b, w_out, final_g)


def _dsilu(z, s):
    return s * (1.0 + z * (1.0 - s))


def _mix_bwd(dx2b, o_f, o_b, proj, conv, gla_g, w_out, tm):
    seq = dx2b.shape[0]

    def body(dx, of, ob, za, bg, zc, cv, gg, wo, dg_ref, do_ref, dconv_ref, dgg_ref, dcb_ref):
        @pl.when(pl.program_id(0) == 0)
        def _():
            dgg_ref[...] = jnp.zeros(dgg_ref.shape, F32)
            dcb_ref[...] = jnp.zeros(dcb_ref.shape, F32)

        dy = _dot_nt(dx[...], wo[...])
        dy_a, dy_c = dy[:, :V_W], dy[:, V_W:]
        zcv, bgv, convv = _f32(zc), _f32(bg), _f32(cv)
        sc = _sigmoid(zcv)
        szc = zcv * sc
        dg_ref[:, CONV_W:2 * CONV_W] = (dy_c * convv * szc).astype(BF16)
        dconv = dy_c * bgv * szc
        dconv_ref[...] = dconv.astype(BF16)
        dcb_ref[...] += jnp.sum(dconv, axis=0, keepdims=True)
        dg_ref[:, 2 * CONV_W:] = (dy_c * bgv * convv * _dsilu(zcv, sc)).astype(BF16)

        o = _f32(of) + _f32(ob)
        gain = gg[...]
        on, rinv = _head_norm(o, gain)
        zav = _f32(za)
        sa = _sigmoid(zav)
        dg_ref[:, :CONV_W] = (dy_a * on * _dsilu(zav, sa)).astype(BF16)
        don = dy_a * (zav * sa)
        dgg = jnp.zeros((1, DV), F32)
        dos = []
        for h in range(HEADS):
            sl = slice(h * DV, (h + 1) * DV)
            oh, r, dh = o[:, sl], rinv[h], don[:, sl]
            ohn = oh * r
            dgg = dgg + jnp.sum(dh * ohn, axis=0, keepdims=True)
            dn = dh * gain
            dos.append(r * dn - ohn * (r * jnp.mean(dn * ohn, axis=-1, keepdims=True)))
        dgg_ref[...] += dgg
        do_ref[...] = jnp.concatenate(dos, axis=1).astype(BF16)

    def col(off):
        return pl.BlockSpec((tm, CONV_W), lambda i: (i, off // CONV_W))

    rowt = pl.BlockSpec((tm, D_MODEL), lambda i: (i, 0))
    const = lambda shape: pl.BlockSpec(shape, lambda i: (0, 0))
    return pl.pallas_call(
        body, name="mix_bwd",
        out_shape=(jax.ShapeDtypeStruct((seq, GATES_W), BF16), jax.ShapeDtypeStruct((seq, V_W), BF16),
                   jax.ShapeDtypeStruct((seq, CONV_W), BF16),
                   jax.ShapeDtypeStruct((1, DV), F32), jax.ShapeDtypeStruct((1, CONV_W), F32)),
        grid=(seq // tm,),
        in_specs=[rowt, rowt, rowt, col(OFF_ZA), col(OFF_B), col(OFF_ZC), rowt, const((1, DV)),
                  const((MIX_W, D_MODEL))],
        out_specs=(pl.BlockSpec((tm, GATES_W), lambda i: (i, 0)), rowt, rowt, const((1, DV)), const((1, CONV_W))),
        compiler_params=_cparams("arbitrary"),
    )(dx2b, o_f, o_b, proj, proj, proj, conv, gla_g, w_out)


def _conv_bwd(dconv, proj, conv_w, tm):
    seq = dconv.shape[0]
    nt = seq // tm

    def body(dc_in, dprev, dnext, cg, hc, cprev, cnext, hprev, hnext, cw, dch_ref, dcw_ref):
        i = pl.program_id(0)

        @pl.when(i == 0)
        def _():
            dcw_ref[...] = jnp.zeros(dcw_ref.shape, F32)

        first, lastt = i > 0, i < nt - 1
        dcv = _f32(dc_in)
        d_up, d_un = _shift_rows(dcv, jnp.where(first, _last_row(dprev), 0.0), jnp.where(lastt, _first_row(dnext), 0.0))
        cgv, hcv = _f32(cg), _f32(hc)
        u = cgv * hcv
        u_up, u_un = _shift_rows(u, jnp.where(first, _last_row(cprev) * _last_row(hprev), 0.0),
                                 jnp.where(lastt, _first_row(cnext) * _first_row(hnext), 0.0))
        du = cw[0:1, :] * d_un + cw[1:2, :] * dcv + cw[2:3, :] * d_up
        dch_ref[:, :CONV_W] = (du * hcv).astype(BF16)
        dch_ref[:, CONV_W:] = (du * cgv).astype(BF16)
        dcw_ref[0:1, :] += jnp.sum(dcv * u_up, axis=0, keepdims=True)
        dcw_ref[1:2, :] += jnp.sum(dcv * u, axis=0, keepdims=True)
        dcw_ref[2:3, :] += jnp.sum(dcv * u_un, axis=0, keepdims=True)

    def col(off):
        return pl.BlockSpec((tm, CONV_W), lambda i: (i, off // CONV_W))

    rowt = pl.BlockSpec((tm, CONV_W), lambda i: (i, 0))
    const = lambda shape: pl.BlockSpec(shape, lambda i: (0, 0))
    return pl.pallas_call(
        body, name="conv_bwd",
        out_shape=(jax.ShapeDtypeStruct((seq, CH_W), BF16), jax.ShapeDtypeStruct((8, CONV_W), F32)),
        grid=(nt,),
        in_specs=[rowt] + _halo_specs(tm, seq, 0) + [col(OFF_C), col(OFF_H)]
        + _halo_specs(tm, seq, OFF_C // CONV_W) + _halo_specs(tm, seq, OFF_H // CONV_W) + [const((8, CONV_W))],
        out_specs=(pl.BlockSpec((tm, CH_W), lambda i: (i, 0)), const((8, CONV_W))),
        compiler_params=_cparams("arbitrary"),
    )(dconv, dconv, dconv, proj, proj, proj, proj, proj, proj, conv_w)


def _gla_bwd(proj, lr, do, st_f, st_b, wgk_f, wgk_b, bgk_f, bgk_b, tt):
    seq = proj.shape[0]
    nb, nc = seq // tt, tt // CHUNK

    def body(qf, kf, vf, lrf, dof, stf, qb, kb, vb, lrb, dob, stb, wf, wb, bf, bb,
             dqkv_f, dlr_f, dqkv_b, dlr_b, dwf, dwb, dbf, dbb,
             ds_scr, eq_s, ek_s, ein_s, eout_s, qs_s, ks_s, qin_s, kout_s, db_s, lg_s):
        @pl.when(pl.program_id(0) == 0)
        def _():
            ds_scr[...] = jnp.zeros(ds_scr.shape, F32)
            for r in (dwf, dwb, dbf, dbb):
                r[...] = jnp.zeros(r.shape, F32)

        low, upp, sup = _block_masks(tt)
        row = lax.broadcasted_iota(jnp.int32, (CHUNK, 1), 0)
        kmask = _chunk_column_mask(tt)
        dirs = ((qf, kf, vf, lrf, dof, stf, wf, bf, dqkv_f, dlr_f, dwf, dbf,
                 low, upp, low, REF_F, LAST_F, list(reversed(range(nc)))),
                (qb, kb, vb, lrb, dob, stb, wb, bb, dqkv_b, dlr_b, dwb, dbb,
                 upp, low, sup, REF_B, LAST_B, list(range(nc))))
        for d, (q_r, k_r, v_r, lr_r, do_r, st_r, w_r, b_r, dqkv_r, dlr_r, dw_r, db_r,
                cum, cum_t, mask, ref, last, order) in enumerate(dirs):
            lrv = lr_r[...].astype(BF16)
            wv = w_r[...]
            logits = _dot(lrv, wv) + b_r[...]
            lg_s[...] = logits
            b = _dot_split3(cum.astype(BF16), _log_gate(logits))
            decs = []
            for c in range(nc):
                rows = slice(c * CHUNK, (c + 1) * CHUNK)
                bc = b[rows]
                b_ref, b_last = bc[ref:ref + 1], bc[last:last + 1]
                qc = q_r[rows, :].astype(F32) * QSCALE
                kc = k_r[rows, :].astype(F32)
                e_q, e_k, e_in, e_out = jnp.exp(bc - b_ref), jnp.exp(b_ref - bc), jnp.exp(bc), jnp.exp(b_last - bc)
                eq_s[rows, :], ek_s[rows, :], ein_s[rows, :], eout_s[rows, :] = e_q, e_k, e_in, e_out
                qs_s[rows, :] = (qc * e_q).astype(BF16)
                ks_s[rows, :] = (kc * e_k).astype(BF16)
                qin_s[rows, :] = (qc * e_in).astype(BF16)
                kout_s[rows, :] = (kc * e_out).astype(BF16)
                decs.append(jnp.exp(b_last))
            for h in range(HEADS):
                ksl = slice(h * DK, (h + 1) * DK)
                vsl = slice(h * DV, (h + 1) * DV)
                v = v_r[:, vsl].astype(BF16)
                dov = do_r[:, vsl].astype(BF16)
                qsb, ksb = qs_s[:, ksl], ks_s[:, ksl]
                att = jnp.where(mask, _dot_nt(qsb, ksb), 0.0).astype(BF16)
                datt = jnp.where(mask, _dot_nt(dov, v), 0.0).astype(BF16)
                dqs = _dot(datt, ksb)
                dks = _dot_tn(datt, qsb)
                dv_intra = _dot_tn(att, dov)
                g_t = _dot_tn(dov, _chunked(kmask, qin_s[:, ksl], nc))
                ds = ds_scr[d * HEADS + h]
                for c in order:
                    rows = slice(c * CHUNK, (c + 1) * CHUNK)
                    dsb = ds.astype(BF16)
                    s_prev = st_r[c, h]
                    dk_out = _dot(v[rows], dsb)
                    dq_in = _dot(dov[rows], s_prev)
                    dv = dv_intra[rows] + _dot_nt(kout_s[rows, ksl], dsb)
                    dqkv_r[rows, OFF_V + h * DV:OFF_V + (h + 1) * DV] = dv.astype(BF16)
                    dec = decs[c][:, ksl]
                    ddec = jnp.sum(ds * s_prev.astype(F32), axis=0, keepdims=True)
                    e_out = eout_s[rows, ksl]
                    qc = q_r[rows, ksl].astype(F32) * QSCALE
                    kc = k_r[rows, ksl].astype(F32)
                    dq = dqs[rows] * eq_s[rows, ksl] + dq_in * ein_s[rows, ksl]
                    dk = dks[rows] * ek_s[rows, ksl] + dk_out * e_out
                    dqkv_r[rows, OFF_Q + h * DK:OFF_Q + (h + 1) * DK] = (dq * QSCALE).astype(BF16)
                    dqkv_r[rows, OFF_K + h * DK:OFF_K + (h + 1) * DK] = dk.astype(BF16)
                    tail = jnp.sum(dk_out * (kc * e_out), axis=0, keepdims=True) + ddec * dec
                    db_s[rows, ksl] = (qc * dq - kc * dk) + jnp.where(row == last, tail, 0.0)
                    ds = ds * dec + g_t[:, c * DK:(c + 1) * DK]
                ds_scr[d * HEADS + h] = ds
            dg = _dot_split3(cum_t.astype(BF16), db_s[...])
            dlogit = (dg * GATE_SCALE) * _sigmoid(-lg_s[...])
            dlb = dlogit.astype(BF16)
            dlr_r[...] = _dot_nt(dlb, wv)
            dw_r[...] += _dot_tn(lrv, dlb)
            db_r[...] += jnp.sum(dlogit, axis=0, keepdims=True)

    fw = lambda i: (nb - 1 - i, 0)
    bw = lambda i: (i, 0)
    const = lambda i: (0, 0)

    def tok_specs(m):
        return [pl.BlockSpec((tt, QK_W), lambda i: (m(i)[0], OFF_Q // QK_W)),
                pl.BlockSpec((tt, QK_W), lambda i: (m(i)[0], OFF_K // QK_W)),
                pl.BlockSpec((tt, V_W), lambda i: (m(i)[0], OFF_V // V_W)),
                pl.BlockSpec((tt, LR_W), m),
                pl.BlockSpec((tt, V_W), m),
                pl.BlockSpec((nc, HEADS, DV, DK), lambda i: (m(i)[0], 0, 0, 0))]

    dqkv = jax.ShapeDtypeStruct((seq, QK_W + QK_W + V_W), BF16)
    dlr = jax.ShapeDtypeStruct((seq, LR_W), F32)
    dw = jax.ShapeDtypeStruct((LR_W, QK_W), F32)
    dbias = jax.ShapeDtypeStruct((1, QK_W), F32)
    return pl.pallas_call(
        body, name="gla_bwd",
        out_shape=(dqkv, dlr, dqkv, dlr, dw, dw, dbias, dbias),
        grid=(nb,),
        in_specs=tok_specs(fw) + tok_specs(bw) + [
            pl.BlockSpec((LR_W, QK_W), const), pl.BlockSpec((LR_W, QK_W), const),
            pl.BlockSpec((1, QK_W), const), pl.BlockSpec((1, QK_W), const)],
        out_specs=(pl.BlockSpec((tt, QK_W + QK_W + V_W), fw), pl.BlockSpec((tt, LR_W), fw),
                   pl.BlockSpec((tt, QK_W + QK_W + V_W), bw), pl.BlockSpec((tt, LR_W), bw),
                   pl.BlockSpec((LR_W, QK_W), const), pl.BlockSpec((LR_W, QK_W), const),
                   pl.BlockSpec((1, QK_W), const), pl.BlockSpec((1, QK_W), const)),
        scratch_shapes=[pltpu.VMEM((2 * HEADS, DV, DK), F32)] + [pltpu.VMEM((tt, QK_W), F32)] * 4
        + [pltpu.VMEM((tt, QK_W), BF16)] * 4 + [pltpu.VMEM((tt, QK_W), F32)] * 2,
        compiler_params=_cparams("arbitrary"),
    )(proj, proj, proj, lr, do, st_f, proj, proj, proj, lr, do, st_b, wgk_f, wgk_b, bgk_f, bgk_b)


def _both_directions(f_ref, b_ref):
    return (_f32(f_ref) + _f32(b_ref)).astype(BF16)


def _input_grad(dqkv_f, dqkv_b, dp_gates, dp_ch, dlr_f, dlr_b, w_nat, x2d, norm_g, dx2, sums, tm):
    seq = x2d.shape[0]
    nt, n = seq // tm, len(sums)
    relay_step = (3 * nt) // 8

    def body(dqf, dqb, dg, dc, dlf, dlb, w, x_ref, g_ref, dx2_ref, *rest):
        ins, (gx_ref, dng_ref), outs = rest[:n], rest[n:n + 2], rest[n + 2:2 * n + 2]
        passing, joined = rest[2 * n + 2:3 * n + 2], rest[3 * n + 2:4 * n + 2]
        send_sems, recv_sems, local_sems = rest[4 * n + 2:]
        i = pl.program_id(0)
        c = lax.axis_index("c")
        first, second, diagonal = _route_chips()
        slot = lambda chip: 2 * chip[0] + chip[1]

        def remote(a, k, src, dst, to):
            return pltpu.make_async_remote_copy(src_ref=src, dst_ref=dst, send_sem=send_sems.at[3 * a + k],
                                                recv_sem=recv_sems.at[3 * a + k], device_id=(*to, c),
                                                device_id_type=MESH)

        direct = lambda a: remote(a, 0, ins[a].at[slot(first)], outs[a].at[0], first)
        for_second = lambda a: remote(a, 1, ins[a].at[slot(diagonal)], passing[a], first)
        joint = lambda a: remote(a, 2, joined[a], outs[a].at[1], second)
        own = lambda a: pltpu.make_async_copy(ins[a].at[slot(second)], joined[a], local_sems.at[a])

        @pl.when(i == 0)
        def _():
            _start_all([for_second(a) for a in range(n)] + [own(a) for a in range(n)] + [direct(a) for a in range(n)])
            dng_ref[...] = jnp.zeros(dng_ref.shape, F32)

        @pl.when(i == relay_step)
        def _():
            for a in range(n):
                for_second(a).wait_recv()
                own(a).wait()
                joined[a][...] = (joined[a][...].astype(F32) + passing[a][...].astype(F32)).astype(BF16)
                joint(a).start()

        dh = (_dot((dlf[...] + dlb[...]).astype(BF16), w[NAT_LR:NAT_LR + LR_W, :])
              + _dot(_both_directions(dqf, dqb), w[0:NAT_ZA, :])
              + _dot(dg[:, 0:CONV_W], w[NAT_ZA:NAT_LR, :]) + _dot(dg[:, CONV_W:2 * CONV_W], w[NAT_B:NAT_C, :])
              + _dot(dg[:, 2 * CONV_W:], w[NAT_ZC:IN_W, :]) + _dot(dc[...], w[NAT_C:NAT_ZC, :]))
        xv = x_ref[...]
        r = lax.rsqrt(jnp.mean(xv * xv, axis=-1, keepdims=True) + EPS)
        xn = xv * r
        dng_ref[...] += jnp.sum(dh * xn, axis=0, keepdims=True)
        dn = dh * g_ref[...]
        gx_ref[...] = (r * dn - xn * (r * jnp.mean(dn * xn, axis=-1, keepdims=True))) + dx2_ref[...]

        @pl.when(i == nt - 1)
        def _():
            for a in range(n):
                direct(a).wait_recv()
                joint(a).wait_recv()
            for a in range(n):
                for cp in (direct(a), for_second(a), joint(a)):
                    cp.wait_send()

    rowt = pl.BlockSpec((tm, D_MODEL), lambda i: (i, 0))
    seg = lambda width: pl.BlockSpec((tm, width), lambda i: (i, 0))
    resident = lambda rows: pl.BlockSpec((rows, D_MODEL), lambda i: (0, 0), pipeline_mode=pl.Buffered(1))
    hbm = pl.BlockSpec(memory_space=pl.ANY)
    blocks = [pltpu.VMEM(s.shape[1:], s.dtype) for s in sums]
    return pl.pallas_call(
        body, name="input_grad",
        out_shape=(jax.ShapeDtypeStruct((seq, D_MODEL), F32), jax.ShapeDtypeStruct((1, D_MODEL), F32))
        + tuple(jax.ShapeDtypeStruct((2,) + s.shape[1:], s.dtype) for s in sums),
        grid=(nt,),
        in_specs=[seg(QKV_W), seg(QKV_W), seg(GATES_W), seg(CH_W), seg(LR_W), seg(LR_W), resident(IN_W),
                  rowt, pl.BlockSpec((1, D_MODEL), lambda i: (0, 0)), rowt] + [hbm] * n,
        out_specs=(rowt, pl.BlockSpec((1, D_MODEL), lambda i: (0, 0))) + (hbm,) * n,
        scratch_shapes=blocks + blocks + [pltpu.SemaphoreType.DMA((3 * n,)), pltpu.SemaphoreType.DMA((3 * n,)),
                                          pltpu.SemaphoreType.DMA((n,))],
        compiler_params=_cparams("arbitrary"),
    )(dqkv_f, dqkv_b, dp_gates, dp_ch, dlr_f, dlr_b, w_nat, x2d, norm_g, dx2, *sums)


def _weight_grad_out(y_t, dx2b, tk, riding):
    m, seq = y_t.shape
    n = dx2b.shape[1]
    nk = seq // tk

    def body(a_ref, b_ref, ride_in, o_ref, ride_out, send_sems, recv_sems):
        k = pl.program_id(0)

        @pl.when(k == 0)
        def _():
            _start_all(_sibling_copies(ride_in, ride_out, send_sems, recv_sems))
            o_ref[...] = jnp.zeros(o_ref.shape, F32)

        o_ref[...] += _dot(a_ref[...], b_ref[...])

        @pl.when(k == nk - 1)
        def _():
            _wait_all(_sibling_copies(ride_in, ride_out, send_sems, recv_sems))

    hbm = pl.BlockSpec(memory_space=pl.ANY)
    return pl.pallas_call(
        body, name="wgrad_out",
        out_shape=(jax.ShapeDtypeStruct((m, n), F32), jax.ShapeDtypeStruct((4,) + _block_shape(riding), F32)),
        grid=(nk,),
        in_specs=[pl.BlockSpec((m, tk), lambda k: (0, k)), pl.BlockSpec((tk, n), lambda k: (k, 0)), hbm],
        out_specs=(pl.BlockSpec((m, n), lambda k: (0, 0)), hbm),
        scratch_shapes=[pltpu.SemaphoreType.DMA((4,)), pltpu.SemaphoreType.DMA((4,))],
        compiler_params=_cparams("arbitrary"),
    )(y_t, dx2b, riding)


def _weight_grad_in(h_t, dqkv_f, dqkv_b, dp_gates, dp_ch, dlr_f, dlr_b):
    m, seq = h_t.shape
    tn = 512
    n_qkv, n_gates, n_ch = QKV_W // tn, GATES_W // tn, CH_W // tn
    starts = ([k * tn for k in range(n_qkv)] + [NAT_ZA, NAT_ZA + tn, NAT_B, NAT_B + tn, NAT_ZC, NAT_ZC + tn]
              + [NAT_C + k * tn for k in range(n_ch)])

    def out_row(j):
        row = 0
        for k, start in enumerate(starts):
            row = row + jnp.where(j == k, start // 32, 0)
        return pl.multiple_of(row * 32, 32), 0

    def body(a_ref, bqf, bqb, bg, bc, o_ref, acc, bq):
        j = pl.program_id(0)

        @pl.when(j < n_qkv)
        def _():
            bq[...] = _both_directions(bqf, bqb)
            acc[...] = _dot(a_ref[...], bq[...])

        @pl.when(jnp.logical_and(j >= n_qkv, j < n_qkv + n_gates))
        def _():
            acc[...] = _dot(a_ref[...], bg[...])

        @pl.when(j >= n_qkv + n_gates)
        def _():
            acc[...] = _dot(a_ref[...], bc[...])

        o_ref[...] = acc[...].T

    resident = pl.BlockSpec((m, seq), lambda j: (0, 0), pipeline_mode=pl.Buffered(1))
    seg = lambda first, count: pl.BlockSpec((seq, tn), lambda j: (0, jnp.clip(j - first, 0, count - 1)))
    main = pl.pallas_call(
        body, name="wgrad_in",
        out_shape=jax.ShapeDtypeStruct((IN_W, m), F32),
        grid=(n_qkv + n_gates + n_ch,),
        in_specs=[resident, seg(0, n_qkv), seg(0, n_qkv), seg(n_qkv, n_gates), seg(n_qkv + n_gates, n_ch)],
        out_specs=pl.BlockSpec((pl.Element(tn), pl.Element(m)), out_row),
        scratch_shapes=[pltpu.VMEM((m, tn), F32), pltpu.VMEM((seq, tn), BF16)],
        compiler_params=_cparams("arbitrary"),
    )(h_t, dqkv_f, dqkv_b, dp_gates, dp_ch)

    def lr_body(a_ref, bf_ref, bb_ref, full_ref, o_ref, acc):
        acc[...] = _dot(a_ref[...], (bf_ref[...] + bb_ref[...]).astype(BF16))
        o_ref[...] = acc[...].T[0:2 * RANK, :]

    whole = lambda shape: pl.BlockSpec(shape, lambda j: (0, 0))
    return pl.pallas_call(
        lr_body, name="wgrad_lr",
        out_shape=jax.ShapeDtypeStruct((IN_W, m), F32),
        grid=(1,),
        in_specs=[whole((m, seq)), whole((seq, LR_W)), whole((seq, LR_W)), pl.BlockSpec(memory_space=pl.ANY)],
        out_specs=pl.BlockSpec((pl.Element(2 * RANK), pl.Element(m)), lambda j: (NAT_LR, 0)),
        scratch_shapes=[pltpu.VMEM((m, LR_W), F32)],
        input_output_aliases={3: 0},
        compiler_params=_cparams("arbitrary"),
    )(h_t, dlr_f, dlr_b, main)


def _pad_rows(a, rows):
    return jnp.pad(a, ((0, rows - a.shape[0]), (0, 0)))


def _rows128(a):
    a = a.reshape(-1, 128)
    return _pad_rows(a, -(-a.shape[0] // 8) * 8)


def _pack(arrs):
    return jnp.concatenate([_rows128(a) for a in arrs], axis=0)


def _unpack(buf, like):
    out, start = [], 0
    for a in like:
        rows = a.size // 128
        out.append(buf[start:start + rows].reshape(a.shape))
        start += -(-rows // 8) * 8
    return out


def kernel(x, norm_g, w_in, w_gk_f, b_gk_f, w_gk_b, b_gk_b, gla_norm_g, conv_w, conv_b, w_out, final_g, loss_target, m_norm_g, m_w_in, m_w_gk_f, m_b_gk_f, m_w_gk_b, m_b_gk_b, m_gla_norm_g, m_conv_w, m_conv_b, m_w_out, m_final_g, v_norm_g, v_w_in, v_w_gk_f, v_b_gk_f, v_w_gk_b, v_b_gk_b, v_gla_norm_g, v_conv_w, v_conv_b, v_w_out, v_final_g):
    px, py, pc = _position()
    me = _blk(px, py, pc)
    seq = x.shape[1]
    x2d, tgt = x[0], loss_target[0]
    tt = min(256, seq)

    small_s = jnp.concatenate([jnp.concatenate([w_gk_f[0], w_gk_b[0]], axis=1), _pad_rows(conv_w[0], 8)], axis=0)
    order = sum(jnp.where(2 * px + py == k, jnp.asarray(tiles + (0,), jnp.int32), 0) for k, tiles in enumerate(TILE_ORDER))
    proj, lr, h_t, w_nat, wout_all, small_all = _gather_inproj(x2d, norm_g, w_in[0].T, w_out[0], small_s, order,
                                                               min(1024, seq))
    w_out_full = wout_all.reshape(MIX_W, D_MODEL)
    wgk_cols = 512 // N_DEV
    wgk_f_full = small_all[:, 0:RANK, 0:wgk_cols].transpose(1, 0, 2).reshape(RANK, QK_W)
    wgk_b_full = small_all[:, 0:RANK, wgk_cols:2 * wgk_cols].transpose(1, 0, 2).reshape(RANK, QK_W)
    conv_w_full = _pad_rows(small_all[:, RANK:RANK + 3, :].transpose(1, 0, 2).reshape(3, CONV_W), 8)
    zr = lambda n: jnp.zeros((n, QK_W), F32)
    wgk_f_pad = jnp.concatenate([wgk_f_full, zr(LR_W - RANK)], axis=0).astype(BF16)
    wgk_b_pad = jnp.concatenate([zr(RANK), wgk_b_full, zr(LR_W - 2 * RANK)], axis=0).astype(BF16)

    o_f, o_b, st_f, st_b = _gla_fwd(proj, lr, wgk_f_pad, wgk_b_pad, b_gk_f, b_gk_b, tt)
    tmix = min(256, seq)
    y_t, conv, dx2, dx2b, loss_p, dfg_p = _mix_out_loss(o_f, o_b, proj, x2d, tgt, gla_norm_g, conv_w_full, conv_b,
                                                        w_out_full, final_g.reshape(1, D_MODEL), tmix)

    dp_gates, do, dconv, dgg_p, dcb_p = _mix_bwd(dx2b, o_f, o_b, proj, conv, gla_norm_g, w_out_full, tmix)
    dp_ch, dcw_p = _conv_bwd(dconv, proj, conv_w_full, tmix)
    dqkv_f, dlr_f, dqkv_b, dlr_b, dwf_p, dwb_p, dbf_p, dbb_p = _gla_bwd(
        proj, lr, do, st_f, st_b, wgk_f_pad, wgk_b_pad, b_gk_f, b_gk_b, tt)
    dw_nat = _weight_grad_in(h_t, dqkv_f, dqkv_b, dp_gates, dp_ch, dlr_f, dlr_b)

    dw_out, sib_in = _weight_grad_out(y_t, dx2b, min(1024, seq), dw_nat)
    part_out = dw_out.reshape(N_DEV, MIX_W // N_DEV, D_MODEL)
    core = jnp.reshape(pc, (1,)).astype(jnp.int32)
    chip = jnp.reshape(2 * px + py, (1,)).astype(jnp.int32)
    sums_in, sib_out = _chip_sums(dw_nat, sib_in, core, 512, "chip_sums_in", riding=part_out)
    sums_out = _chip_sums(part_out, sib_out, core, D_MODEL, "chip_sums_out")
    grad_x2d, dng_p, far_in, far_out = _input_grad(dqkv_f, dqkv_b, dp_gates, dp_ch, dlr_f, dlr_b, w_nat, x2d, norm_g, dx2,
                                                   [sums_in, sums_out], tmix)
    pieces = [dng_p, dbf_p, dbb_p, dgg_p, dcb_p, dfg_p[0], dwf_p[0:RANK], dwb_p[RANK:2 * RANK], dcw_p[0:3], loss_p[0]]
    g_window, small_tot = _final_sum(sums_in, far_in, chip, _pack(pieces), 512, "final_sum_in")
    g_in_t = lax.dynamic_slice_in_dim(g_window, 4 * pc, SHARD_W, axis=0)
    g_w_out, d_w_out, nm_w_out, nv_w_out = _final_sum_adamw(sums_out, far_out, chip, w_out[0], m_w_out[0], v_w_out[0],
                                                            256, "adamw_out")
    flat = lambda a: a[0].T.reshape(SHARD_W, D_MODEL // 128, 128)
    unflat = lambda a: a.reshape(SHARD_W, D_MODEL).T
    d_flat, m_flat, v_flat = _adamw_rows(g_in_t.reshape(SHARD_W, D_MODEL // 128, 128), flat(w_in), flat(m_w_in),
                                         flat(v_w_in), 180, "adamw_in")
    g_w_in, d_w_in, nm_w_in, nv_w_in = g_in_t.T, unflat(d_flat), unflat(m_flat), unflat(v_flat)

    tot = _unpack(small_tot, pieces)
    g_norm_g, g_b_gk_f, g_b_gk_b, g_gla, g_conv_b, g_final = tot[:6]
    g_wgk_f = lax.dynamic_slice_in_dim(tot[6], me * wgk_cols, wgk_cols, axis=1)[None]
    g_wgk_b = lax.dynamic_slice_in_dim(tot[7], me * wgk_cols, wgk_cols, axis=1)[None]
    g_conv_w = lax.dynamic_slice_in_dim(tot[8], me * 128, 128, axis=1)[None]
    loss = tot[9][0]

    small_g = [g_norm_g, g_b_gk_f, g_b_gk_b, g_gla, g_conv_b, g_final, g_wgk_f, g_wgk_b, g_conv_w]
    small_w = [norm_g, b_gk_f, b_gk_b, gla_norm_g, conv_b, final_g, w_gk_f, w_gk_b, conv_w]
    small_m = [m_norm_g, m_b_gk_f, m_b_gk_b, m_gla_norm_g, m_conv_b, m_final_g, m_w_gk_f, m_w_gk_b, m_conv_w]
    small_v = [v_norm_g, v_b_gk_f, v_b_gk_b, v_gla_norm_g, v_conv_b, v_final_g, v_w_gk_f, v_w_gk_b, v_conv_w]
    d_s, m_s, v_s = _adamw_small(_pack(small_g), _pack(small_w), _pack(small_m), _pack(small_v))
    d_l, m_l, v_l = _unpack(d_s, small_w), _unpack(m_s, small_w), _unpack(v_s, small_w)

    def ordered(sm, big_in, big_out):
        return [sm[0], big_in[None], sm[6], sm[1], sm[7], sm[2], sm[3], sm[8], sm[4], big_out[None], sm[5]]

    grads = ordered(small_g, g_w_in, g_w_out)
    deltas = ordered(d_l, d_w_in, d_w_out)
    new_m = ordered(m_l, nm_w_in, nm_w_out)
    new_v = ordered(v_l, nv_w_in, nv_w_out)
    return (loss, grad_x2d[None], *grads, *deltas, *new_m, *new_v)
```

```python
import functools

import jax
import jax.numpy as jnp
from jax import lax
from jax.experimental import pallas as pl
from jax.experimental.pallas import tpu as pltpu

F32 = jnp.float32
BF16 = jnp.bfloat16
MESH = pl.DeviceIdType.MESH

N_DEV = 8
D_MODEL = 1024
HEADS = 4
DK = 128
DV = 256
QK_W = HEADS * DK
V_W = HEADS * DV
CONV_W = 1024
MIX_W = V_W + CONV_W
CHUNK = 64
RANK = 16
IN_W = 7200
SHARD_W = IN_W // N_DEV
MAIN_W = 7168
LR_W = 128
OFF_Q, OFF_K, OFF_V, OFF_ZA, OFF_B, OFF_ZC, OFF_C, OFF_H = 0, 512, 1024, 2048, 3072, 4096, 5120, 6144
QKV_W, GATES_W, CH_W = 2048, 3072, 2048
NAT_ZA, NAT_LR, NAT_B, NAT_C, NAT_ZC = 2048, 3072, 3104, 4128, 6176
EPS = 1e-6
GATE_SCALE = 1.0 / 16.0
QSCALE = DK ** -0.5
REF_F, LAST_F = CHUNK // 2, CHUNK - 1
REF_B, LAST_B = CHUNK - 1 - CHUNK // 2, 0

ADAM_LR = 0.001
ADAM_B1 = 0.9
ADAM_B2 = 0.999
ADAM_EPS = 1e-08
ADAM_WD = 0.01
ADAM_STEP = 10

VMEM_LIMIT = 56 * 1024 * 1024


def _cparams(*sem):
    return pltpu.CompilerParams(dimension_semantics=sem, vmem_limit_bytes=VMEM_LIMIT)


def _dot(a, b):
    return jnp.dot(a, b, preferred_element_type=F32)


def _dot_nt(a, b):
    return lax.dot_general(a, b, (((1,), (1,)), ((), ())), preferred_element_type=F32)


def _dot_tn(a, b):
    return lax.dot_general(a, b, (((0,), (0,)), ((), ())), preferred_element_type=F32)


def _sigmoid(z):
    return jax.nn.sigmoid(z)


def _position():
    return lax.axis_index("x"), lax.axis_index("y"), lax.axis_index("c")


def _blk(px, py, pc):
    return 4 * px + 2 * py + pc


EDGE = 16
SHIFTED_ROWS = 912
BODY_ROWS = SHIFTED_ROWS - 2 * EDGE


def _first_tile_row(blk, px):
    return EDGE * (56 * blk + px)


def _edge_tiles():
    tiles = {}
    for blk in range(N_DEV):
        first = _first_tile_row(blk, blk // 4)
        tiles.setdefault(first, []).append((blk, 0))
        tiles.setdefault(first + EDGE + BODY_ROWS, []).append((blk, 1))
    return tiles


def _peer_copies(srcs, outs, send_sems, recv_sems):
    x, y, c = _position()
    me = _blk(x, y, c)
    copies = []
    for a, (src, out) in enumerate(zip(srcs, outs)):
        k = 0
        for dx in (0, 1):
            for dy in (0, 1):
                for dc in (0, 1):
                    if dx + dy + dc == 0:
                        continue
                    peer = (1 - x if dx else x, 1 - y if dy else y, 1 - c if dc else c)
                    copies.append(pltpu.make_async_remote_copy(
                        src_ref=src, dst_ref=out.at[me], send_sem=send_sems.at[a * 7 + k],
                        recv_sem=recv_sems.at[a * 7 + k], device_id=peer, device_id_type=MESH))
                    k += 1
    return copies


def _route_chips():
    x, y, c = _position()
    along_x = c == 0
    return [(jnp.where(along_x, 1 - x, x), jnp.where(along_x, y, 1 - y)),
            (jnp.where(along_x, x, 1 - x), jnp.where(along_x, 1 - y, y)), (1 - x, 1 - y)]


WINDOW_ROWS = SHARD_W + 4


def _window_start(k, parity):
    return 2 * SHARD_W * k + (SHARD_W - 4) * parity


def _owner_block(part, k, parity):
    if part.ndim == 3:
        return part.at[2 * k + parity]
    return part.at[pl.ds(pl.multiple_of(_window_start(k, parity), 8), WINDOW_ROWS)]


def _block_shape(part):
    return part.shape[1:] if part.ndim == 3 else (WINDOW_ROWS, part.shape[1])


def _sibling_copies(part, out, send_sems, recv_sems):
    x, y, c = _position()
    return [pltpu.make_async_remote_copy(src_ref=_owner_block(part, k, 1 - c), dst_ref=out.at[k],
                                         send_sem=send_sems.at[k], recv_sem=recv_sems.at[k],
                                         device_id=(x, y, 1 - c), device_id_type=MESH)
            for k in range(4)]


def _start_all(copies):
    for cp in copies:
        cp.start()


def _wait_all(copies):
    for cp in copies:
        cp.wait_recv()
    for cp in copies:
        cp.wait_send()


def _chip_sums(part, from_sibling, core, tc, name, riding=None):
    rows, cols = _block_shape(part)
    nj = cols // tc

    def body(core_ref, p_ref, s_ref, *rest):
        if riding is None:
            (o_ref,) = rest
        else:
            ride_in, o_ref, ride_out, send_sems, recv_sems = rest
            k, j = pl.program_id(0), pl.program_id(1)

            @pl.when(jnp.logical_and(k == 0, j == 0))
            def _():
                _start_all(_sibling_copies(ride_in, ride_out, send_sems, recv_sems))

        o_ref[0] = (p_ref[...].reshape(rows, tc) + s_ref[0]).astype(BF16)

        if riding is not None:
            @pl.when(jnp.logical_and(k == 3, j == nj - 1))
            def _():
                _wait_all(_sibling_copies(ride_in, ride_out, send_sems, recv_sems))

    hbm = pl.BlockSpec(memory_space=pl.ANY)
    sums = jax.ShapeDtypeStruct((4, rows, cols), BF16)
    tile_out = pl.BlockSpec((1, rows, tc), lambda k, j, core_ref: (k, 0, j))
    if part.ndim == 3:
        mine = pl.BlockSpec((1, rows, tc), lambda k, j, core_ref: (2 * k + core_ref[0], 0, j))
    else:
        mine = pl.BlockSpec((pl.Element(rows), pl.Element(tc)),
                            lambda k, j, core_ref: (pl.multiple_of(_window_start(k, core_ref[0]), 8),
                                                    pl.multiple_of(j * tc, 128)))
    in_specs = [mine, pl.BlockSpec((1, rows, tc), lambda k, j, core_ref: (k, 0, j))]
    if riding is None:
        out_shape, out_specs, scratch, args = sums, tile_out, [], (core, part, from_sibling)
    else:
        out_shape = (sums, jax.ShapeDtypeStruct((4,) + _block_shape(riding), F32))
        out_specs, in_specs = (tile_out, hbm), in_specs + [hbm]
        scratch = [pltpu.SemaphoreType.DMA((4,)), pltpu.SemaphoreType.DMA((4,))]
        args = (core, part, from_sibling, riding)
    return pl.pallas_call(
        body, name=name, out_shape=out_shape,
        grid_spec=pltpu.PrefetchScalarGridSpec(num_scalar_prefetch=1, grid=(4, nj), in_specs=in_specs,
                                               out_specs=out_specs, scratch_shapes=scratch),
        compiler_params=_cparams("arbitrary", "arbitrary"),
    )(*args)


def _sum_chips(s_ref, r_ref):
    f = lambda a: a.astype(F32)
    return (f(s_ref[0]) + f(r_ref[0])) + f(r_ref[1])


def _final_sum(sums, from_chips, chip, small, tc, name):
    _, rows, cols = sums.shape
    nj = cols // tc

    def body(chip_ref, s_ref, r_ref, sm_ref, g_out, tot_ref, all_ref, send_sems, recv_sems):
        j = pl.program_id(0)
        me = _blk(*_position())

        @pl.when(j == 0)
        def _():
            all_ref[me] = sm_ref[...]
            _start_all(_peer_copies((all_ref.at[me],), (all_ref,), send_sems, recv_sems))

        g_out[...] = _sum_chips(s_ref, r_ref)

        @pl.when(j == nj - 1)
        def _():
            _wait_all(_peer_copies((all_ref.at[me],), (all_ref,), send_sems, recv_sems))
            acc = all_ref[0]
            for d in range(1, N_DEV):
                acc = acc + all_ref[d]
            tot_ref[...] = acc

    whole = pl.BlockSpec(small.shape, lambda j, chip_ref: (0, 0))
    return pl.pallas_call(
        body, name=name,
        out_shape=(jax.ShapeDtypeStruct((rows, cols), F32), jax.ShapeDtypeStruct(small.shape, F32)),
        grid_spec=pltpu.PrefetchScalarGridSpec(
            num_scalar_prefetch=1, grid=(nj,),
            in_specs=[pl.BlockSpec((1, rows, tc), lambda j, chip_ref: (chip_ref[0], 0, j)),
                      pl.BlockSpec((2, rows, tc), lambda j, chip_ref: (0, 0, j)), whole],
            out_specs=(pl.BlockSpec((rows, tc), lambda j, chip_ref: (0, j)), whole),
            scratch_shapes=[pltpu.VMEM((N_DEV,) + small.shape, F32), pltpu.SemaphoreType.DMA((7,)),
                            pltpu.SemaphoreType.DMA((7,))]),
        compiler_params=_cparams("arbitrary"),
    )(chip, sums, from_chips, small)


def _adamw_rows(g, w, m, v, tr, name):
    rows = g.shape[0]

    def body(g_ref, w_ref, m_ref, v_ref, d_out, m_out, v_out):
        delta, m_new, v_new = _adamw(w_ref[...], g_ref[...], m_ref[...], v_ref[...])
        d_out[...] = delta
        m_out[...] = m_new
        v_out[...] = v_new

    tile = pl.BlockSpec((tr,) + g.shape[1:], lambda r: (r, 0, 0))
    shp = jax.ShapeDtypeStruct(g.shape, F32)
    return pl.pallas_call(
        body, name=name, out_shape=(shp, shp, shp), grid=(rows // tr,),
        in_specs=[tile] * 4, out_specs=(tile, tile, tile),
        compiler_params=_cparams("arbitrary"),
    )(g, w, m, v)


def _adamw(w, g, m, v):
    m = ADAM_B1 * m + (1.0 - ADAM_B1) * g
    v = ADAM_B2 * v + (1.0 - ADAM_B2) * (g * g)
    m_hat = m / (1.0 - ADAM_B1 ** ADAM_STEP)
    v_hat = v / (1.0 - ADAM_B2 ** ADAM_STEP)
    delta = -ADAM_LR * (m_hat / (jnp.sqrt(v_hat) + ADAM_EPS) + ADAM_WD * w)
    return delta, m, v


def _final_sum_adamw(sums, from_chips, chip, w, m, v, tr, name):
    rows, cols = w.shape

    def body(chip_ref, s_ref, r_ref, w_ref, m_ref, v_ref, g_out, d_out, m_out, v_out):
        g = _sum_chips(s_ref, r_ref)
        delta, m_new, v_new = _adamw(w_ref[...], g, m_ref[...], v_ref[...])
        g_out[...] = g
        d_out[...] = delta
        m_out[...] = m_new
        v_out[...] = v_new

    tile = pl.BlockSpec((tr, cols), lambda r, chip_ref: (r, 0))
    shp = jax.ShapeDtypeStruct((rows, cols), F32)
    return pl.pallas_call(
        body, name=name,
        out_shape=(shp, shp, shp, shp),
        grid_spec=pltpu.PrefetchScalarGridSpec(
            num_scalar_prefetch=1, grid=(rows // tr,),
            in_specs=[pl.BlockSpec((1, tr, cols), lambda r, chip_ref: (chip_ref[0], r, 0)),
                      pl.BlockSpec((2, tr, cols), lambda r, chip_ref: (0, r, 0)),
                      tile, tile, tile],
            out_specs=(tile, tile, tile, tile)),
        compiler_params=_cparams("arbitrary"),
    )(chip, sums, from_chips, w, m, v)


def _adamw_small(g, w, m, v):
    def body(g_ref, w_ref, m_ref, v_ref, d_out, m_out, v_out):
        delta, m_new, v_new = _adamw(w_ref[...], g_ref[...], m_ref[...], v_ref[...])
        d_out[...] = delta
        m_out[...] = m_new
        v_out[...] = v_new

    vmem = pl.BlockSpec(memory_space=pltpu.VMEM)
    shp = jax.ShapeDtypeStruct(g.shape, F32)
    return pl.pallas_call(body, name="adamw_small", out_shape=(shp, shp, shp),
                          in_specs=[vmem] * 4, out_specs=(vmem, vmem, vmem))(g, w, m, v)


TILE_ROWS = (0, 1024, NAT_ZA, NAT_B, NAT_ZC, NAT_C, NAT_C + CONV_W)


TILE_ORDER = ((0, 1, 2, 3, 5, 6, 4), (2, 1, 0, 4, 3, 5, 6), (5, 6, 0, 4, 1, 2, 3), (4, 6, 2, 3, 5, 0, 1))
EARLY_SWEEP, NEIGHBOUR_SWEEP, DIAGONAL_SWEEP = 1, 2, 4
PIECES, W_IN_PIECES, OTHER_PIECES = 4, (0, 1), (2, 3)


def _gather_inproj(x2d, norm_g, shard_t, w_out_s, small_s, order, tm):
    seq = x2d.shape[0]
    tn = CONV_W
    ni, nj = seq // tm, MAIN_W // tn
    first_sweep = lambda j, i, order_ref: jnp.where(j == 0, i, ni - 1)
    last_sweep = lambda j, i, order_ref: jnp.where(j == nj - 1, i, 0)
    edge_tiles = _edge_tiles()

    def body(order_ref, x_ref, g_ref, shard_ref, wout_ref, sm_ref, proj_ref, lr_ref, ht_ref, w_nat, wout_all, sm_all,
             w_all, h_all, edges, stage, wout_b, sm_b, send_sems, recv_sems, local_sems):
        j, i = pl.program_id(0), pl.program_id(1)
        rows = pl.ds(pl.multiple_of(i * tm, tm), tm)
        x, y, c = _position()
        me, here, sibling = _blk(x, y, c), (x, y, c), (x, y, 1 - c)
        chips = _route_chips()
        sibling_chips = [chips[1], chips[0], chips[2]]

        def pieces(px, py, pc):
            blk = _blk(px, py, pc)
            body_rows = pl.ds(pl.multiple_of(_first_tile_row(blk, px) + EDGE, EDGE), BODY_ROWS)
            return [w_all.at[body_rows], edges.at[blk], wout_all.at[blk], sm_all.at[blk]]

        def copy(a, k, block, to, staged=None):
            ref = pieces(*block)[a]
            return pltpu.make_async_remote_copy(src_ref=ref if staged is None else staged, dst_ref=ref,
                                                send_sem=send_sems.at[a * 7 + k], recv_sem=recv_sems.at[a * 7 + k],
                                                device_id=to, device_id_type=MESH)

        def own_copies(group, slots=(0, 1, 2)):
            targets = [sibling] + [(*chips[n], c) for n in range(2)]
            staged = [None, None, wout_b, sm_b]
            return [copy(a, k, here, targets[k], staged[a]) for k in slots for a in group]

        def relays(group):
            return [copy(a, 3, (*chips[0], c), (*chips[1], c)) for a in group]

        def forwards(n, group):
            return [copy(a, 4 + n, (*chips[n], c), sibling) for a in group]

        def keep_own():
            return [pltpu.make_async_copy(wout_b, wout_all.at[me], local_sems.at[0]),
                    pltpu.make_async_copy(sm_b, sm_all.at[me], local_sems.at[1])]

        def keep_weight():
            return pltpu.make_async_copy(w_all, w_nat, local_sems.at[2])

        def take(ns, group):
            for n in ns:
                for a in group:
                    copy(a, 1 + n, (*chips[n], c), here).wait_recv()
                _start_all((relays(group) if n == 0 else []) + forwards(n, group))

        def take_passed_on(ns, group):
            for n in ns:
                for a in group:
                    copy(a, 4 + n, (*sibling_chips[n], 1 - c), here).wait_recv()

        def arrive(ns, group):
            take(ns, group)
            take_passed_on(ns, group)

        def per_core_and_row(step):
            for core in range(2):
                for row in range(2):
                    pl.when(jnp.logical_and(c == core, y == row))(functools.partial(step, core, row))

        def start_own(core, row):
            now = (0, 1 + core) if core == row else (0, 1, 2)
            _start_all(own_copies(W_IN_PIECES, now))
            wout_b[...] = wout_ref[...].astype(BF16)
            sm_b[...] = sm_ref[...]
            _start_all(own_copies(OTHER_PIECES, now) + keep_own())

        def take_early(core, row):
            if core == row:
                _start_all(own_copies(W_IN_PIECES, (2 - core,)) + own_copies(OTHER_PIECES, (2 - core,)))
                take((1 - core,), W_IN_PIECES)
            else:
                take_passed_on((core,), W_IN_PIECES)

        def take_neighbours(core, row):
            if core == row:
                take((core,), W_IN_PIECES)
                take_passed_on((0, 1), W_IN_PIECES)
            else:
                take((0, 1), W_IN_PIECES)
                take_passed_on((1 - core,), W_IN_PIECES)

        early_blk = _blk(x, 1 - y, y)

        def add_edge_tiles(stage):
            for row, parts in edge_tiles.items():
                ready = 0
                for blk, _ in parts:
                    away = (x != blk // 4).astype(jnp.int32) + (y != (blk // 2) % 2).astype(jnp.int32)
                    late = jnp.where(away == 1, jnp.where(early_blk == blk, 1, 2), jnp.where(away == 2, 3 + blk % 2, 0))
                    ready = jnp.maximum(ready, late)

                @pl.when(ready == stage)
                def _(row=row, parts=parts):
                    tile = edges[parts[0][0], parts[0][1]].astype(F32)
                    for blk, side in parts[1:]:
                        tile = tile + edges[blk, side].astype(F32)
                    w_all[row:row + EDGE, :] = tile.astype(BF16)

        @pl.when(jnp.logical_and(j == 0, i == 0))
        def _():
            last = SHARD_W // 8 * 8
            for col in range(0, D_MODEL, 128):
                cols = slice(col, col + 128)
                stage[0:last, :] = shard_ref[0:last, cols]
                stage[last:, :] = jnp.zeros((SHIFTED_ROWS - last, 128), F32)
                stage[last:SHARD_W, :] = shard_ref[last:SHARD_W, cols]
                for k in range(EDGE // 4):
                    @pl.when(me % 4 == k)
                    def _(k=k, cols=cols):
                        moved = pltpu.roll(stage[...], 4 * k, 0) if k else stage[...]
                        pieces(*here)[0][:, cols] = moved[EDGE:EDGE + BODY_ROWS].astype(BF16)
                        edges[me, 0, :, cols] = moved[0:EDGE].astype(BF16)
                        edges[me, 1, :, cols] = moved[EDGE + BODY_ROWS:].astype(BF16)
            per_core_and_row(start_own)
            for a in W_IN_PIECES:
                copy(a, 0, sibling, here).wait_recv()
            add_edge_tiles(0)

        @pl.when(jnp.logical_and(j == EARLY_SWEEP, i == 0))
        def _():
            per_core_and_row(take_early)
            add_edge_tiles(1)

        @pl.when(jnp.logical_and(j == NEIGHBOUR_SWEEP, i == 0))
        def _():
            per_core_and_row(take_neighbours)
            add_edge_tiles(2)

        for core in range(2):
            @pl.when(jnp.logical_and(j == DIAGONAL_SWEEP + core, i == 0))
            def _(core=core):
                pl.when(c == core)(lambda: take((2,), W_IN_PIECES))
                pl.when(c != core)(lambda: take_passed_on((2,), W_IN_PIECES))
                add_edge_tiles(3 + core)
                if core == 0:
                    arrive((0, 1), OTHER_PIECES)
                else:
                    keep_weight().start()

        @pl.when(jnp.logical_and(j == nj - 1, i == 0))
        def _():
            arrive((2,), OTHER_PIECES)

        @pl.when(j == 0)
        def _():
            xv = x_ref[...]
            r = lax.rsqrt(jnp.mean(xv * xv, axis=-1, keepdims=True) + EPS)
            h = (xv * r) * g_ref[...]
            h_all[rows, :] = h.astype(BF16)
            ht_ref[...] = h.T.astype(BF16)

        tile = order_ref[j]
        row = 0
        for k, start in enumerate(TILE_ROWS):
            row = row + jnp.where(tile == k, start // 32, 0)
        w_tile = w_all[pl.ds(pl.multiple_of(row * 32, 32), tn), :]
        proj_ref[...] = _dot_nt(h_all[rows, :], w_tile).astype(BF16)

        @pl.when(j == nj - 1)
        def _():
            lr_ref[...] = _dot_nt(h_all[rows, :], w_all[NAT_LR:NAT_LR + LR_W, :])

        @pl.when(jnp.logical_and(j == nj - 1, i == ni - 1))
        def _():
            everything = range(PIECES)
            passed_on = [cp for n in range(3) for cp in forwards(n, everything)]
            for cp in own_copies(everything) + relays(everything) + passed_on:
                cp.wait_send()
            for a in OTHER_PIECES:
                copy(a, 0, sibling, here).wait_recv()
            for cp in keep_own() + [keep_weight()]:
                cp.wait()

    const = lambda shape: pl.BlockSpec(shape, lambda j, i, order_ref: (0,) * len(shape))
    hbm = pl.BlockSpec(memory_space=pl.ANY)
    vmem = pl.BlockSpec(memory_space=pltpu.VMEM)
    return pl.pallas_call(
        body, name="gather_inproj",
        out_shape=(jax.ShapeDtypeStruct((seq, MAIN_W), BF16), jax.ShapeDtypeStruct((seq, LR_W), F32),
                   jax.ShapeDtypeStruct((D_MODEL, seq), BF16), jax.ShapeDtypeStruct((IN_W, D_MODEL), BF16),
                   jax.ShapeDtypeStruct((N_DEV,) + w_out_s.shape, BF16),
                   jax.ShapeDtypeStruct((N_DEV,) + small_s.shape, F32)),
        grid_spec=pltpu.PrefetchScalarGridSpec(
            num_scalar_prefetch=1, grid=(nj, ni),
            in_specs=[pl.BlockSpec((tm, D_MODEL), lambda j, i, order_ref: (first_sweep(j, i, order_ref), 0)),
                      const((1, D_MODEL)), vmem, vmem, const(small_s.shape)],
            out_specs=(pl.BlockSpec((tm, tn), lambda j, i, order_ref: (i, order_ref[j])),
                       pl.BlockSpec((tm, LR_W), lambda j, i, order_ref: (last_sweep(j, i, order_ref), 0)),
                       pl.BlockSpec((D_MODEL, tm), lambda j, i, order_ref: (0, first_sweep(j, i, order_ref))),
                       hbm, hbm, hbm),
            scratch_shapes=[pltpu.VMEM((IN_W, D_MODEL), BF16), pltpu.VMEM((seq, D_MODEL), BF16),
                            pltpu.VMEM((N_DEV, 2, EDGE, D_MODEL), BF16), pltpu.VMEM((SHIFTED_ROWS, 128), F32),
                            pltpu.VMEM(w_out_s.shape, BF16), pltpu.VMEM(small_s.shape, F32),
                            pltpu.SemaphoreType.DMA((7 * PIECES,)), pltpu.SemaphoreType.DMA((7 * PIECES,)),
                            pltpu.SemaphoreType.DMA((3,))]),
        compiler_params=_cparams("arbitrary", "arbitrary"),
    )(order, x2d, norm_g, shard_t, w_out_s, small_s)


def _block_masks(tt):
    row = lax.broadcasted_iota(jnp.int32, (tt, tt), 0)
    col = lax.broadcasted_iota(jnp.int32, (tt, tt), 1)
    same = jnp.right_shift(row, 6) == jnp.right_shift(col, 6)
    return (jnp.logical_and(same, col <= row), jnp.logical_and(same, col >= row), jnp.logical_and(same, col > row))


def _dot_split3(ones_mat, x):
    x1 = x.astype(BF16)
    r1 = x - x1.astype(F32)
    x2 = r1.astype(BF16)
    x3 = (r1 - x2.astype(F32)).astype(BF16)
    return (_dot(ones_mat, x3) + _dot(ones_mat, x2)) + _dot(ones_mat, x1)


def _log_gate(logits):
    return (jnp.minimum(logits, 0.0) - jnp.log(1.0 + jnp.exp(-jnp.abs(logits)))) * GATE_SCALE


def _chunk_column_mask(tt):
    nc = tt // CHUNK
    row = lax.broadcasted_iota(jnp.int32, (tt, nc * DK), 0)
    col = lax.broadcasted_iota(jnp.int32, (tt, nc * DK), 1)
    return jnp.right_shift(row, 6) == jnp.right_shift(col, 7)


def _chunked(mask, x, nc):
    wide = jnp.concatenate([x] * nc, axis=1)
    return jnp.where(mask, wide, jnp.zeros_like(wide))


def _gla_fwd(proj, lr, wgk_f, wgk_b, bgk_f, bgk_b, tt):
    seq = proj.shape[0]
    nb, nc, nch = seq // tt, tt // CHUNK, seq // CHUNK

    def body(qf, kf, vf, lrf, qb, kb, vb, lrb, wf, wb, bf, bb, of, ob, stf, stb, s_scr, qs_s, ks_s, qin_s, kout_s):
        @pl.when(pl.program_id(0) == 0)
        def _():
            s_scr[...] = jnp.zeros(s_scr.shape, F32)

        low, upp, sup = _block_masks(tt)
        dirs = ((qf, kf, vf, lrf, wf, bf, of, stf, low, low, REF_F, LAST_F, list(range(nc))),
                (qb, kb, vb, lrb, wb, bb, ob, stb, upp, sup, REF_B, LAST_B, list(reversed(range(nc)))))
        for d, (q_r, k_r, v_r, lr_r, w_r, b_r, o_r, st_r, cum, mask, ref, last, order) in enumerate(dirs):
            logits = _dot(lr_r[...].astype(BF16), w_r[...]) + b_r[...]
            b = _dot_split3(cum.astype(BF16), _log_gate(logits))
            decs = []
            for c in range(nc):
                rows = slice(c * CHUNK, (c + 1) * CHUNK)
                bc = b[rows]
                b_ref, b_last = bc[ref:ref + 1], bc[last:last + 1]
                qc = q_r[rows, :].astype(F32) * QSCALE
                kc = k_r[rows, :].astype(F32)
                qs_s[rows, :] = (qc * jnp.exp(bc - b_ref)).astype(BF16)
                ks_s[rows, :] = (kc * jnp.exp(b_ref - bc)).astype(BF16)
                qin_s[rows, :] = (qc * jnp.exp(bc)).astype(BF16)
                kout_s[rows, :] = (kc * jnp.exp(b_last - bc)).astype(BF16)
                decs.append(jnp.exp(b_last))
            for h in range(HEADS):
                ksl = slice(h * DK, (h + 1) * DK)
                vsl = slice(h * DV, (h + 1) * DV)
                v = v_r[:, vsl].astype(BF16)
                att = jnp.where(mask, _dot_nt(qs_s[:, ksl], ks_s[:, ksl]), 0.0).astype(BF16)
                o_intra = _dot(att, v)
                st = s_scr[d * HEADS + h]
                for c in order:
                    rows = slice(c * CHUNK, (c + 1) * CHUNK)
                    stb = st.astype(BF16)
                    st_r[c, h] = stb
                    o_r[rows, vsl] = (o_intra[rows] + _dot_nt(qin_s[rows, ksl], stb)).astype(BF16)
                    st = st * decs[c][:, ksl] + _dot_tn(v[rows], kout_s[rows, ksl])
                s_scr[d * HEADS + h] = st

    fw = lambda i: (i, 0)
    bw = lambda i: (nb - 1 - i, 0)
    const = lambda i: (0, 0)

    def tok_specs(m):
        return [pl.BlockSpec((tt, QK_W), lambda i: (m(i)[0], OFF_Q // QK_W)),
                pl.BlockSpec((tt, QK_W), lambda i: (m(i)[0], OFF_K // QK_W)),
                pl.BlockSpec((tt, V_W), lambda i: (m(i)[0], OFF_V // V_W)),
                pl.BlockSpec((tt, LR_W), m)]

    st_shape = jax.ShapeDtypeStruct((nch, HEADS, DV, DK), BF16)
    o_shape = jax.ShapeDtypeStruct((seq, V_W), BF16)
    operand = pltpu.VMEM((tt, QK_W), BF16)
    return pl.pallas_call(
        body, name="gla_fwd",
        out_shape=(o_shape, o_shape, st_shape, st_shape),
        grid=(nb,),
        in_specs=tok_specs(fw) + tok_specs(bw) + [
            pl.BlockSpec((LR_W, QK_W), const), pl.BlockSpec((LR_W, QK_W), const),
            pl.BlockSpec((1, QK_W), const), pl.BlockSpec((1, QK_W), const)],
        out_specs=(pl.BlockSpec((tt, V_W), fw), pl.BlockSpec((tt, V_W), bw),
                   pl.BlockSpec((nc, HEADS, DV, DK), lambda i: (i, 0, 0, 0)),
                   pl.BlockSpec((nc, HEADS, DV, DK), lambda i: (nb - 1 - i, 0, 0, 0))),
        scratch_shapes=[pltpu.VMEM((2 * HEADS, DV, DK), F32), operand, operand, operand, operand],
        compiler_params=_cparams("arbitrary"),
    )(proj, proj, proj, lr, proj, proj, proj, lr, wgk_f, wgk_b, bgk_f, bgk_b)


def _head_norm(o, gain):
    outs, rinv = [], []
    for h in range(HEADS):
        oh = o[:, h * DV:(h + 1) * DV]
        r = lax.rsqrt(jnp.mean(oh * oh, axis=-1, keepdims=True) + EPS)
        outs.append((oh * r) * gain)
        rinv.append(r)
    return jnp.concatenate(outs, axis=1), rinv


def _shift_rows(u, prev_row, next_row):
    n = u.shape[0]
    row = lax.broadcasted_iota(jnp.int32, (n, 1), 0)
    up = jnp.where(row == 0, prev_row, pltpu.roll(u, 1, 0))
    un = jnp.where(row == n - 1, next_row, pltpu.roll(u, n - 1, 0))
    return up, un


HALO = 16


def _halo_specs(tm, seq, col_block):
    per = tm // HALO
    last = seq // HALO - 1
    return [pl.BlockSpec((HALO, CONV_W), lambda i: (jnp.maximum(i * per - 1, 0), col_block)),
            pl.BlockSpec((HALO, CONV_W), lambda i: (jnp.minimum((i + 1) * per, last), col_block))]


def _f32(ref):
    return ref[...].astype(F32)


def _last_row(ref):
    return ref[HALO - 1:HALO, :].astype(F32)


def _first_row(ref):
    return ref[0:1, :].astype(F32)


def _mix_out_loss(o_f, o_b, proj, x2d, tgt, gla_g, conv_w, conv_b, w_out, final_g, tm):
    seq = x2d.shape[0]
    nt = seq // tm

    def body(of, ob, za, bg, cg, hc, zc, cprev, cnext, hprev, hnext, x_ref, t_ref, gg, cw, cb, wo, fg,
             yt_ref, conv_ref, dx2_ref, dx2b_ref, loss_ref, dfg_ref):
        i = pl.program_id(0)

        @pl.when(i == 0)
        def _():
            loss_ref[...] = jnp.zeros(loss_ref.shape, F32)
            dfg_ref[...] = jnp.zeros(dfg_ref.shape, F32)

        on, _ = _head_norm(_f32(of) + _f32(ob), gg[...])
        zav = _f32(za)
        y_a = on * (zav * _sigmoid(zav))
        u = _f32(cg) * _f32(hc)
        prev_row = jnp.where(i > 0, _last_row(cprev) * _last_row(hprev), 0.0)
        next_row = jnp.where(i < nt - 1, _first_row(cnext) * _first_row(hnext), 0.0)
        up, un = _shift_rows(u, prev_row, next_row)
        conv = (cw[0:1, :] * up + cw[1:2, :] * u + cw[2:3, :] * un) + cb[...]
        conv_ref[...] = conv.astype(BF16)
        zcv = _f32(zc)
        y_c = _f32(bg) * conv * (zcv * _sigmoid(zcv))
        y = jnp.concatenate([y_a, y_c], axis=1)
        yt_ref[...] = y.T.astype(BF16)
        x2 = x_ref[...] + _dot(y.astype(BF16), wo[...])
        r = lax.rsqrt(jnp.mean(x2 * x2, axis=-1, keepdims=True) + EPS)
        xn = x2 * r
        err = xn * fg[...] - t_ref[...]
        loss_ref[...] += 0.5 * jnp.sum(jnp.mean(err * err, axis=-1, keepdims=True))
        dyf = err * (1.0 / D_MODEL)
        dfg_ref[...] += jnp.sum(dyf * xn, axis=0, keepdims=True)
        dxn = dyf * fg[...]
        dx2 = r * dxn - xn * (r * jnp.mean(dxn * xn, axis=-1, keepdims=True))
        dx2_ref[...] = dx2
        dx2b_ref[...] = dx2.astype(BF16)

    def col(off):
        return pl.BlockSpec((tm, CONV_W), lambda i: (i, off // CONV_W))

    rowt = pl.BlockSpec((tm, D_MODEL), lambda i: (i, 0))
    const = lambda shape: pl.BlockSpec(shape, lambda i: (0, 0))
    return pl.pallas_call(
        body, name="mix_out_loss",
        out_shape=(jax.ShapeDtypeStruct((MIX_W, seq), BF16), jax.ShapeDtypeStruct((seq, CONV_W), BF16),
                   jax.ShapeDtypeStruct((seq, D_MODEL), F32), jax.ShapeDtypeStruct((seq, D_MODEL), BF16),
                   jax.ShapeDtypeStruct((8, 128), F32), jax.ShapeDtypeStruct((1, D_MODEL), F32)),
        grid=(nt,),
        in_specs=[rowt, rowt, col(OFF_ZA), col(OFF_B), col(OFF_C), col(OFF_H), col(OFF_ZC)]
        + _halo_specs(tm, seq, OFF_C // CONV_W) + _halo_specs(tm, seq, OFF_H // CONV_W)
        + [rowt, rowt, const((1, DV)), const((8, CONV_W)), const((1, CONV_W)), const((MIX_W, D_MODEL)),
           const((1, D_MODEL))],
        out_specs=(pl.BlockSpec((MIX_W, tm), lambda i: (0, i)), rowt, rowt, rowt, const((8, 128)),
                   const((1, D_MODEL))),
        compiler_params=_cparams("arbitrary"),
    )(o_f, o_b, proj, proj, proj, proj, proj, proj, proj, proj, proj, x2d, tgt, gla_g, conv_w, conv_b, w_out, final_g)


def _dsilu(z, s):
    return s * (1.0 + z * (1.0 - s))


def _mix_bwd(dx2b, o_f, o_b, proj, conv, gla_g, w_out, tm):
    seq = dx2b.shape[0]

    def body(dx, of, ob, za, bg, zc, cv, gg, wo, dg_ref, do_ref, dconv_ref, dgg_ref, dcb_ref):
        @pl.when(pl.program_id(0) == 0)
        def _():
            dgg_ref[...] = jnp.zeros(dgg_ref.shape, F32)
            dcb_ref[...] = jnp.zeros(dcb_ref.shape, F32)

        dy = _dot_nt(dx[...], wo[...])
        dy_a, dy_c = dy[:, :V_W], dy[:, V_W:]
        zcv, bgv, convv = _f32(zc), _f32(bg), _f32(cv)
        sc = _sigmoid(zcv)
        szc = zcv * sc
        dg_ref[:, CONV_W:2 * CONV_W] = (dy_c * convv * szc).astype(BF16)
        dconv = dy_c * bgv * szc
        dconv_ref[...] = dconv.astype(BF16)
        dcb_ref[...] += jnp.sum(dconv, axis=0, keepdims=True)
        dg_ref[:, 2 * CONV_W:] = (dy_c * bgv * convv * _dsilu(zcv, sc)).astype(BF16)

        o = _f32(of) + _f32(ob)
        gain = gg[...]
        on, rinv = _head_norm(o, gain)
        zav = _f32(za)
        sa = _sigmoid(zav)
        dg_ref[:, :CONV_W] = (dy_a * on * _dsilu(zav, sa)).astype(BF16)
        don = dy_a * (zav * sa)
        dgg = jnp.zeros((1, DV), F32)
        dos = []
        for h in range(HEADS):
            sl = slice(h * DV, (h + 1) * DV)
            oh, r, dh = o[:, sl], rinv[h], don[:, sl]
            ohn = oh * r
            dgg = dgg + jnp.sum(dh * ohn, axis=0, keepdims=True)
            dn = dh * gain
            dos.append(r * dn - ohn * (r * jnp.mean(dn * ohn, axis=-1, keepdims=True)))
        dgg_ref[...] += dgg
        do_ref[...] = jnp.concatenate(dos, axis=1).astype(BF16)

    def col(off):
        return pl.BlockSpec((tm, CONV_W), lambda i: (i, off // CONV_W))

    rowt = pl.BlockSpec((tm, D_MODEL), lambda i: (i, 0))
    const = lambda shape: pl.BlockSpec(shape, lambda i: (0, 0))
    return pl.pallas_call(
        body, name="mix_bwd",
        out_shape=(jax.ShapeDtypeStruct((seq, GATES_W), BF16), jax.ShapeDtypeStruct((seq, V_W), BF16),
                   jax.ShapeDtypeStruct((seq, CONV_W), BF16),
                   jax.ShapeDtypeStruct((1, DV), F32), jax.ShapeDtypeStruct((1, CONV_W), F32)),
        grid=(seq // tm,),
        in_specs=[rowt, rowt, rowt, col(OFF_ZA), col(OFF_B), col(OFF_ZC), rowt, const((1, DV)),
                  const((MIX_W, D_MODEL))],
        out_specs=(pl.BlockSpec((tm, GATES_W), lambda i: (i, 0)), rowt, rowt, const((1, DV)), const((1, CONV_W))),
        compiler_params=_cparams("arbitrary"),
    )(dx2b, o_f, o_b, proj, proj, proj, conv, gla_g, w_out)


def _conv_bwd(dconv, proj, conv_w, tm):
    seq = dconv.shape[0]
    nt = seq // tm

    def body(dc_in, dprev, dnext, cg, hc, cprev, cnext, hprev, hnext, cw, dch_ref, dcw_ref):
        i = pl.program_id(0)

        @pl.when(i == 0)
        def _():
            dcw_ref[...] = jnp.zeros(dcw_ref.shape, F32)

        first, lastt = i > 0, i < nt - 1
        dcv = _f32(dc_in)
        d_up, d_un = _shift_rows(dcv, jnp.where(first, _last_row(dprev), 0.0), jnp.where(lastt, _first_row(dnext), 0.0))
        cgv, hcv = _f32(cg), _f32(hc)
        u = cgv * hcv
        u_up, u_un = _shift_rows(u, jnp.where(first, _last_row(cprev) * _last_row(hprev), 0.0),
                                 jnp.where(lastt, _first_row(cnext) * _first_row(hnext), 0.0))
        du = cw[0:1, :] * d_un + cw[1:2, :] * dcv + cw[2:3, :] * d_up
        dch_ref[:, :CONV_W] = (du * hcv).astype(BF16)
        dch_ref[:, CONV_W:] = (du * cgv).astype(BF16)
        dcw_ref[0:1, :] += jnp.sum(dcv * u_up, axis=0, keepdims=True)
        dcw_ref[1:2, :] += jnp.sum(dcv * u, axis=0, keepdims=True)
        dcw_ref[2:3, :] += jnp.sum(dcv * u_un, axis=0, keepdims=True)

    def col(off):
        return pl.BlockSpec((tm, CONV_W), lambda i: (i, off // CONV_W))

    rowt = pl.BlockSpec((tm, CONV_W), lambda i: (i, 0))
    const = lambda shape: pl.BlockSpec(shape, lambda i: (0, 0))
    return pl.pallas_call(
        body, name="conv_bwd",
        out_shape=(jax.ShapeDtypeStruct((seq, CH_W), BF16), jax.ShapeDtypeStruct((8, CONV_W), F32)),
        grid=(nt,),
        in_specs=[rowt] + _halo_specs(tm, seq, 0) + [col(OFF_C), col(OFF_H)]
        + _halo_specs(tm, seq, OFF_C // CONV_W) + _halo_specs(tm, seq, OFF_H // CONV_W) + [const((8, CONV_W))],
        out_specs=(pl.BlockSpec((tm, CH_W), lambda i: (i, 0)), const((8, CONV_W))),
        compiler_params=_cparams("arbitrary"),
    )(dconv, dconv, dconv, proj, proj, proj, proj, proj, proj, conv_w)


def _gla_bwd(proj, lr, do, st_f, st_b, wgk_f, wgk_b, bgk_f, bgk_b, tt):
    seq = proj.shape[0]
    nb, nc = seq // tt, tt // CHUNK

    def body(qf, kf, vf, lrf, dof, stf, qb, kb, vb, lrb, dob, stb, wf, wb, bf, bb,
             dqkv_f, dlr_f, dqkv_b, dlr_b, dwf, dwb, dbf, dbb,
             ds_scr, eq_s, ek_s, ein_s, eout_s, qs_s, ks_s, qin_s, kout_s, db_s, lg_s):
        @pl.when(pl.program_id(0) == 0)
        def _():
            ds_scr[...] = jnp.zeros(ds_scr.shape, F32)
            for r in (dwf, dwb, dbf, dbb):
                r[...] = jnp.zeros(r.shape, F32)

        low, upp, sup = _block_masks(tt)
        row = lax.broadcasted_iota(jnp.int32, (CHUNK, 1), 0)
        kmask = _chunk_column_mask(tt)
        dirs = ((qf, kf, vf, lrf, dof, stf, wf, bf, dqkv_f, dlr_f, dwf, dbf,
                 low, upp, low, REF_F, LAST_F, list(reversed(range(nc)))),
                (qb, kb, vb, lrb, dob, stb, wb, bb, dqkv_b, dlr_b, dwb, dbb,
                 upp, low, sup, REF_B, LAST_B, list(range(nc))))
        for d, (q_r, k_r, v_r, lr_r, do_r, st_r, w_r, b_r, dqkv_r, dlr_r, dw_r, db_r,
                cum, cum_t, mask, ref, last, order) in enumerate(dirs):
            lrv = lr_r[...].astype(BF16)
            wv = w_r[...]
            logits = _dot(lrv, wv) + b_r[...]
            lg_s[...] = logits
            b = _dot_split3(cum.astype(BF16), _log_gate(logits))
            decs = []
            for c in range(nc):
                rows = slice(c * CHUNK, (c + 1) * CHUNK)
                bc = b[rows]
                b_ref, b_last = bc[ref:ref + 1], bc[last:last + 1]
                qc = q_r[rows, :].astype(F32) * QSCALE
                kc = k_r[rows, :].astype(F32)
                e_q, e_k, e_in, e_out = jnp.exp(bc - b_ref), jnp.exp(b_ref - bc), jnp.exp(bc), jnp.exp(b_last - bc)
                eq_s[rows, :], ek_s[rows, :], ein_s[rows, :], eout_s[rows, :] = e_q, e_k, e_in, e_out
                qs_s[rows, :] = (qc * e_q).astype(BF16)
                ks_s[rows, :] = (kc * e_k).astype(BF16)
                qin_s[rows, :] = (qc * e_in).astype(BF16)
                kout_s[rows, :] = (kc * e_out).astype(BF16)
                decs.append(jnp.exp(b_last))
            for h in range(HEADS):
                ksl = slice(h * DK, (h + 1) * DK)
                vsl = slice(h * DV, (h + 1) * DV)
                v = v_r[:, vsl].astype(BF16)
                dov = do_r[:, vsl].astype(BF16)
                qsb, ksb = qs_s[:, ksl], ks_s[:, ksl]
                att = jnp.where(mask, _dot_nt(qsb, ksb), 0.0).astype(BF16)
                datt = jnp.where(mask, _dot_nt(dov, v), 0.0).astype(BF16)
                dqs = _dot(datt, ksb)
                dks = _dot_tn(datt, qsb)
                dv_intra = _dot_tn(att, dov)
                g_t = _dot_tn(dov, _chunked(kmask, qin_s[:, ksl], nc))
                ds = ds_scr[d * HEADS + h]
                for c in order:
                    rows = slice(c * CHUNK, (c + 1) * CHUNK)
                    dsb = ds.astype(BF16)
                    s_prev = st_r[c, h]
                    dk_out = _dot(v[rows], dsb)
                    dq_in = _dot(dov[rows], s_prev)
                    dv = dv_intra[rows] + _dot_nt(kout_s[rows, ksl], dsb)
                    dqkv_r[rows, OFF_V + h * DV:OFF_V + (h + 1) * DV] = dv.astype(BF16)
                    dec = decs[c][:, ksl]
                    ddec = jnp.sum(ds * s_prev.astype(F32), axis=0, keepdims=True)
                    e_out = eout_s[rows, ksl]
                    qc = q_r[rows, ksl].astype(F32) * QSCALE
                    kc = k_r[rows, ksl].astype(F32)
                    dq = dqs[rows] * eq_s[rows, ksl] + dq_in * ein_s[rows, ksl]
                    dk = dks[rows] * ek_s[rows, ksl] + dk_out * e_out
                    dqkv_r[rows, OFF_Q + h * DK:OFF_Q + (h + 1) * DK] = (dq * QSCALE).astype(BF16)
                    dqkv_r[rows, OFF_K + h * DK:OFF_K + (h + 1) * DK] = dk.astype(BF16)
                    tail = jnp.sum(dk_out * (kc * e_out), axis=0, keepdims=True) + ddec * dec
                    db_s[rows, ksl] = (qc * dq - kc * dk) + jnp.where(row == last, tail, 0.0)
                    ds = ds * dec + g_t[:, c * DK:(c + 1) * DK]
                ds_scr[d * HEADS + h] = ds
            dg = _dot_split3(cum_t.astype(BF16), db_s[...])
            dlogit = (dg * GATE_SCALE) * _sigmoid(-lg_s[...])
            dlb = dlogit.astype(BF16)
            dlr_r[...] = _dot_nt(dlb, wv)
            dw_r[...] += _dot_tn(lrv, dlb)
            db_r[...] += jnp.sum(dlogit, axis=0, keepdims=True)

    fw = lambda i: (nb - 1 - i, 0)
    bw = lambda i: (i, 0)
    const = lambda i: (0, 0)

    def tok_specs(m):
        return [pl.BlockSpec((tt, QK_W), lambda i: (m(i)[0], OFF_Q // QK_W)),
                pl.BlockSpec((tt, QK_W), lambda i: (m(i)[0], OFF_K // QK_W)),
                pl.BlockSpec((tt, V_W), lambda i: (m(i)[0], OFF_V // V_W)),
                pl.BlockSpec((tt, LR_W), m),
                pl.BlockSpec((tt, V_W), m),
                pl.BlockSpec((nc, HEADS, DV, DK), lambda i: (m(i)[0], 0, 0, 0))]

    dqkv = jax.ShapeDtypeStruct((seq, QK_W + QK_W + V_W), BF16)
    dlr = jax.ShapeDtypeStruct((seq, LR_W), F32)
    dw = jax.ShapeDtypeStruct((LR_W, QK_W), F32)
    dbias = jax.ShapeDtypeStruct((1, QK_W), F32)
    return pl.pallas_call(
        body, name="gla_bwd",
        out_shape=(dqkv, dlr, dqkv, dlr, dw, dw, dbias, dbias),
        grid=(nb,),
        in_specs=tok_specs(fw) + tok_specs(bw) + [
            pl.BlockSpec((LR_W, QK_W), const), pl.BlockSpec((LR_W, QK_W), const),
            pl.BlockSpec((1, QK_W), const), pl.BlockSpec((1, QK_W), const)],
        out_specs=(pl.BlockSpec((tt, QK_W + QK_W + V_W), fw), pl.BlockSpec((tt, LR_W), fw),
                   pl.BlockSpec((tt, QK_W + QK_W + V_W), bw), pl.BlockSpec((tt, LR_W), bw),
                   pl.BlockSpec((LR_W, QK_W), const), pl.BlockSpec((LR_W, QK_W), const),
                   pl.BlockSpec((1, QK_W), const), pl.BlockSpec((1, QK_W), const)),
        scratch_shapes=[pltpu.VMEM((2 * HEADS, DV, DK), F32)] + [pltpu.VMEM((tt, QK_W), F32)] * 4
        + [pltpu.VMEM((tt, QK_W), BF16)] * 4 + [pltpu.VMEM((tt, QK_W), F32)] * 2,
        compiler_params=_cparams("arbitrary"),
    )(proj, proj, proj, lr, do, st_f, proj, proj, proj, lr, do, st_b, wgk_f, wgk_b, bgk_f, bgk_b)


def _both_directions(f_ref, b_ref):
    return (_f32(f_ref) + _f32(b_ref)).astype(BF16)


def _input_grad(dqkv_f, dqkv_b, dp_gates, dp_ch, dlr_f, dlr_b, w_nat, x2d, norm_g, dx2, sums, tm):
    seq = x2d.shape[0]
    nt, n = seq // tm, len(sums)
    relay_step = (3 * nt) // 8

    def body(dqf, dqb, dg, dc, dlf, dlb, w, x_ref, g_ref, dx2_ref, *rest):
        ins, (gx_ref, dng_ref), outs = rest[:n], rest[n:n + 2], rest[n + 2:2 * n + 2]
        passing, joined = rest[2 * n + 2:3 * n + 2], rest[3 * n + 2:4 * n + 2]
        send_sems, recv_sems, local_sems = rest[4 * n + 2:]
        i = pl.program_id(0)
        c = lax.axis_index("c")
        first, second, diagonal = _route_chips()
        slot = lambda chip: 2 * chip[0] + chip[1]

        def remote(a, k, src, dst, to):
            return pltpu.make_async_remote_copy(src_ref=src, dst_ref=dst, send_sem=send_sems.at[3 * a + k],
                                                recv_sem=recv_sems.at[3 * a + k], device_id=(*to, c),
                                                device_id_type=MESH)

        direct = lambda a: remote(a, 0, ins[a].at[slot(first)], outs[a].at[0], first)
        for_second = lambda a: remote(a, 1, ins[a].at[slot(diagonal)], passing[a], first)
        joint = lambda a: remote(a, 2, joined[a], outs[a].at[1], second)
        own = lambda a: pltpu.make_async_copy(ins[a].at[slot(second)], joined[a], local_sems.at[a])

        @pl.when(i == 0)
        def _():
            _start_all([for_second(a) for a in range(n)] + [own(a) for a in range(n)] + [direct(a) for a in range(n)])
            dng_ref[...] = jnp.zeros(dng_ref.shape, F32)

        @pl.when(i == relay_step)
        def _():
            for a in range(n):
                for_second(a).wait_recv()
                own(a).wait()
                joined[a][...] = (joined[a][...].astype(F32) + passing[a][...].astype(F32)).astype(BF16)
                joint(a).start()

        dh = (_dot((dlf[...] + dlb[...]).astype(BF16), w[NAT_LR:NAT_LR + LR_W, :])
              + _dot(_both_directions(dqf, dqb), w[0:NAT_ZA, :])
              + _dot(dg[:, 0:CONV_W], w[NAT_ZA:NAT_LR, :]) + _dot(dg[:, CONV_W:2 * CONV_W], w[NAT_B:NAT_C, :])
              + _dot(dg[:, 2 * CONV_W:], w[NAT_ZC:IN_W, :]) + _dot(dc[...], w[NAT_C:NAT_ZC, :]))
        xv = x_ref[...]
        r = lax.rsqrt(jnp.mean(xv * xv, axis=-1, keepdims=True) + EPS)
        xn = xv * r
        dng_ref[...] += jnp.sum(dh * xn, axis=0, keepdims=True)
        dn = dh * g_ref[...]
        gx_ref[...] = (r * dn - xn * (r * jnp.mean(dn * xn, axis=-1, keepdims=True))) + dx2_ref[...]

        @pl.when(i == nt - 1)
        def _():
            for a in range(n):
                direct(a).wait_recv()
                joint(a).wait_recv()
            for a in range(n):
                for cp in (direct(a), for_second(a), joint(a)):
                    cp.wait_send()

    rowt = pl.BlockSpec((tm, D_MODEL), lambda i: (i, 0))
    seg = lambda width: pl.BlockSpec((tm, width), lambda i: (i, 0))
    resident = lambda rows: pl.BlockSpec((rows, D_MODEL), lambda i: (0, 0), pipeline_mode=pl.Buffered(1))
    hbm = pl.BlockSpec(memory_space=pl.ANY)
    blocks = [pltpu.VMEM(s.shape[1:], s.dtype) for s in sums]
    return pl.pallas_call(
        body, name="input_grad",
        out_shape=(jax.ShapeDtypeStruct((seq, D_MODEL), F32), jax.ShapeDtypeStruct((1, D_MODEL), F32))
        + tuple(jax.ShapeDtypeStruct((2,) + s.shape[1:], s.dtype) for s in sums),
        grid=(nt,),
        in_specs=[seg(QKV_W), seg(QKV_W), seg(GATES_W), seg(CH_W), seg(LR_W), seg(LR_W), resident(IN_W),
                  rowt, pl.BlockSpec((1, D_MODEL), lambda i: (0, 0)), rowt] + [hbm] * n,
        out_specs=(rowt, pl.BlockSpec((1, D_MODEL), lambda i: (0, 0))) + (hbm,) * n,
        scratch_shapes=blocks + blocks + [pltpu.SemaphoreType.DMA((3 * n,)), pltpu.SemaphoreType.DMA((3 * n,)),
                                          pltpu.SemaphoreType.DMA((n,))],
        compiler_params=_cparams("arbitrary"),
    )(dqkv_f, dqkv_b, dp_gates, dp_ch, dlr_f, dlr_b, w_nat, x2d, norm_g, dx2, *sums)


def _weight_grad_out(y_t, dx2b, tk, riding):
    m, seq = y_t.shape
    n = dx2b.shape[1]
    nk = seq // tk

    def body(a_ref, b_ref, ride_in, o_ref, ride_out, send_sems, recv_sems):
        k = pl.program_id(0)

        @pl.when(k == 0)
        def _():
            _start_all(_sibling_copies(ride_in, ride_out, send_sems, recv_sems))
            o_ref[...] = jnp.zeros(o_ref.shape, F32)

        o_ref[...] += _dot(a_ref[...], b_ref[...])

        @pl.when(k == nk - 1)
        def _():
            _wait_all(_sibling_copies(ride_in, ride_out, send_sems, recv_sems))

    hbm = pl.BlockSpec(memory_space=pl.ANY)
    return pl.pallas_call(
        body, name="wgrad_out",
        out_shape=(jax.ShapeDtypeStruct((m, n), F32), jax.ShapeDtypeStruct((4,) + _block_shape(riding), F32)),
        grid=(nk,),
        in_specs=[pl.BlockSpec((m, tk), lambda k: (0, k)), pl.BlockSpec((tk, n), lambda k: (k, 0)), hbm],
        out_specs=(pl.BlockSpec((m, n), lambda k: (0, 0)), hbm),
        scratch_shapes=[pltpu.SemaphoreType.DMA((4,)), pltpu.SemaphoreType.DMA((4,))],
        compiler_params=_cparams("arbitrary"),
    )(y_t, dx2b, riding)


def _weight_grad_in(h_t, dqkv_f, dqkv_b, dp_gates, dp_ch, dlr_f, dlr_b):
    m, seq = h_t.shape
    tn = 512
    n_qkv, n_gates, n_ch = QKV_W // tn, GATES_W // tn, CH_W // tn
    starts = ([k * tn for k in range(n_qkv)] + [NAT_ZA, NAT_ZA + tn, NAT_B, NAT_B + tn, NAT_ZC, NAT_ZC + tn]
              + [NAT_C + k * tn for k in range(n_ch)])

    def out_row(j):
        row = 0
        for k, start in enumerate(starts):
            row = row + jnp.where(j == k, start // 32, 0)
        return pl.multiple_of(row * 32, 32), 0

    def body(a_ref, bqf, bqb, bg, bc, o_ref, acc, bq):
        j = pl.program_id(0)

        @pl.when(j < n_qkv)
        def _():
            bq[...] = _both_directions(bqf, bqb)
            acc[...] = _dot(a_ref[...], bq[...])

        @pl.when(jnp.logical_and(j >= n_qkv, j < n_qkv + n_gates))
        def _():
            acc[...] = _dot(a_ref[...], bg[...])

        @pl.when(j >= n_qkv + n_gates)
        def _():
            acc[...] = _dot(a_ref[...], bc[...])

        o_ref[...] = acc[...].T

    resident = pl.BlockSpec((m, seq), lambda j: (0, 0), pipeline_mode=pl.Buffered(1))
    seg = lambda first, count: pl.BlockSpec((seq, tn), lambda j: (0, jnp.clip(j - first, 0, count - 1)))
    main = pl.pallas_call(
        body, name="wgrad_in",
        out_shape=jax.ShapeDtypeStruct((IN_W, m), F32),
        grid=(n_qkv + n_gates + n_ch,),
        in_specs=[resident, seg(0, n_qkv), seg(0, n_qkv), seg(n_qkv, n_gates), seg(n_qkv + n_gates, n_ch)],
        out_specs=pl.BlockSpec((pl.Element(tn), pl.Element(m)), out_row),
        scratch_shapes=[pltpu.VMEM((m, tn), F32), pltpu.VMEM((seq, tn), BF16)],
        compiler_params=_cparams("arbitrary"),
    )(h_t, dqkv_f, dqkv_b, dp_gates, dp_ch)

    def lr_body(a_ref, bf_ref, bb_ref, full_ref, o_ref, acc):
        acc[...] = _dot(a_ref[...], (bf_ref[...] + bb_ref[...]).astype(BF16))
        o_ref[...] = acc[...].T[0:2 * RANK, :]

    whole = lambda shape: pl.BlockSpec(shape, lambda j: (0, 0))
    return pl.pallas_call(
        lr_body, name="wgrad_lr",
        out_shape=jax.ShapeDtypeStruct((IN_W, m), F32),
        grid=(1,),
        in_specs=[whole((m, seq)), whole((seq, LR_W)), whole((seq, LR_W)), pl.BlockSpec(memory_space=pl.ANY)],
        out_specs=pl.BlockSpec((pl.Element(2 * RANK), pl.Element(m)), lambda j: (NAT_LR, 0)),
        scratch_shapes=[pltpu.VMEM((m, LR_W), F32)],
        input_output_aliases={3: 0},
        compiler_params=_cparams("arbitrary"),
    )(h_t, dlr_f, dlr_b, main)


def _pad_rows(a, rows):
    return jnp.pad(a, ((0, rows - a.shape[0]), (0, 0)))


def _rows128(a):
    a = a.reshape(-1, 128)
    return _pad_rows(a, -(-a.shape[0] // 8) * 8)


def _pack(arrs):
    return jnp.concatenate([_rows128(a) for a in arrs], axis=0)


def _unpack(buf, like):
    out, start = [], 0
    for a in like:
        rows = a.size // 128
        out.append(buf[start:start + rows].reshape(a.shape))
        start += -(-rows // 8) * 8
    return out


def kernel(x, norm_g, w_in, w_gk_f, b_gk_f, w_gk_b, b_gk_b, gla_norm_g, conv_w, conv_b, w_out, final_g, loss_target, m_norm_g, m_w_in, m_w_gk_f, m_b_gk_f, m_w_gk_b, m_b_gk_b, m_gla_norm_g, m_conv_w, m_conv_b, m_w_out, m_final_g, v_norm_g, v_w_in, v_w_gk_f, v_b_gk_f, v_w_gk_b, v_b_gk_b, v_gla_norm_g, v_conv_w, v_conv_b, v_w_out, v_final_g):
    px, py, pc = _position()
    me = _blk(px, py, pc)
    seq = x.shape[1]
    x2d, tgt = x[0], loss_target[0]
    tt = min(256, seq)

    small_s = jnp.concatenate([jnp.concatenate([w_gk_f[0], w_gk_b[0]], axis=1), _pad_rows(conv_w[0], 8)], axis=0)
    order = sum(jnp.where(2 * px + py == k, jnp.asarray(tiles + (0,), jnp.int32), 0) for k, tiles in enumerate(TILE_ORDER))
    proj, lr, h_t, w_nat, wout_all, small_all = _gather_inproj(x2d, norm_g, w_in[0].T, w_out[0], small_s, order,
                                                               min(1024, seq))
    w_out_full = wout_all.reshape(MIX_W, D_MODEL)
    wgk_cols = 512 // N_DEV
    wgk_f_full = small_all[:, 0:RANK, 0:wgk_cols].transpose(1, 0, 2).reshape(RANK, QK_W)
    wgk_b_full = small_all[:, 0:RANK, wgk_cols:2 * wgk_cols].transpose(1, 0, 2).reshape(RANK, QK_W)
    conv_w_full = _pad_rows(small_all[:, RANK:RANK + 3, :].transpose(1, 0, 2).reshape(3, CONV_W), 8)
    zr = lambda n: jnp.zeros((n, QK_W), F32)
    wgk_f_pad = jnp.concatenate([wgk_f_full, zr(LR_W - RANK)], axis=0).astype(BF16)
    wgk_b_pad = jnp.concatenate([zr(RANK), wgk_b_full, zr(LR_W - 2 * RANK)], axis=0).astype(BF16)

    o_f, o_b, st_f, st_b = _gla_fwd(proj, lr, wgk_f_pad, wgk_b_pad, b_gk_f, b_gk_b, tt)
    tmix = min(256, seq)
    y_t, conv, dx2, dx2b, loss_p, dfg_p = _mix_out_loss(o_f, o_b, proj, x2d, tgt, gla_norm_g, conv_w_full, conv_b,
                                                        w_out_full, final_g.reshape(1, D_MODEL), tmix)

    dp_gates, do, dconv, dgg_p, dcb_p = _mix_bwd(dx2b, o_f, o_b, proj, conv, gla_norm_g, w_out_full, tmix)
    dp_ch, dcw_p = _conv_bwd(dconv, proj, conv_w_full, tmix)
    dqkv_f, dlr_f, dqkv_b, dlr_b, dwf_p, dwb_p, dbf_p, dbb_p = _gla_bwd(
        proj, lr, do, st_f, st_b, wgk_f_pad, wgk_b_pad, b_gk_f, b_gk_b, tt)
    dw_nat = _weight_grad_in(h_t, dqkv_f, dqkv_b, dp_gates, dp_ch, dlr_f, dlr_b)

    dw_out, sib_in = _weight_grad_out(y_t, dx2b, min(1024, seq), dw_nat)
    part_out = dw_out.reshape(N_DEV, MIX_W // N_DEV, D_MODEL)
    core = jnp.reshape(pc, (1,)).astype(jnp.int32)
    chip = jnp.reshape(2 * px + py, (1,)).astype(jnp.int32)
    sums_in, sib_out = _chip_sums(dw_nat, sib_in, core, 512, "chip_sums_in", riding=part_out)
    sums_out = _chip_sums(part_out, sib_out, core, D_MODEL, "chip_sums_out")
    grad_x2d, dng_p, far_in, far_out = _input_grad(dqkv_f, dqkv_b, dp_gates, dp_ch, dlr_f, dlr_b, w_nat, x2d, norm_g, dx2,
                                                   [sums_in, sums_out], tmix)
    pieces = [dng_p, dbf_p, dbb_p, dgg_p, dcb_p, dfg_p[0], dwf_p[0:RANK], dwb_p[RANK:2 * RANK], dcw_p[0:3], loss_p[0]]
    g_window, small_tot = _final_sum(sums_in, far_in, chip, _pack(pieces), 512, "final_sum_in")
    g_in_t = lax.dynamic_slice_in_dim(g_window, 4 * pc, SHARD_W, axis=0)
    g_w_out, d_w_out, nm_w_out, nv_w_out = _final_sum_adamw(sums_out, far_out, chip, w_out[0], m_w_out[0], v_w_out[0],
                                                            256, "adamw_out")
    flat = lambda a: a[0].T.reshape(SHARD_W, D_MODEL // 128, 128)
    unflat = lambda a: a.reshape(SHARD_W, D_MODEL).T
    d_flat, m_flat, v_flat = _adamw_rows(g_in_t.reshape(SHARD_W, D_MODEL // 128, 128), flat(w_in), flat(m_w_in),
                                         flat(v_w_in), 180, "adamw_in")
    g_w_in, d_w_in, nm_w_in, nv_w_in = g_in_t.T, unflat(d_flat), unflat(m_flat), unflat(v_flat)

    tot = _unpack(small_tot, pieces)
    g_norm_g, g_b_gk_f, g_b_gk_b, g_gla, g_conv_b, g_final = tot[:6]
    g_wgk_f = lax.dynamic_slice_in_dim(tot[6], me * wgk_cols, wgk_cols, axis=1)[None]
    g_wgk_b = lax.dynamic_slice_in_dim(tot[7], me * wgk_cols, wgk_cols, axis=1)[None]
    g_conv_w = lax.dynamic_slice_in_dim(tot[8], me * 128, 128, axis=1)[None]
    loss = tot[9][0]

    small_g = [g_norm_g, g_b_gk_f, g_b_gk_b, g_gla, g_conv_b, g_final, g_wgk_f, g_wgk_b, g_conv_w]
    small_w = [norm_g, b_gk_f, b_gk_b, gla_norm_g, conv_b, final_g, w_gk_f, w_gk_b, conv_w]
    small_m = [m_norm_g, m_b_gk_f, m_b_gk_b, m_gla_norm_g, m_conv_b, m_final_g, m_w_gk_f, m_w_gk_b, m_conv_w]
    small_v = [v_norm_g, v_b_gk_f, v_b_gk_b, v_gla_norm_g, v_conv_b, v_final_g, v_w_gk_f, v_w_gk_b, v_conv_w]
    d_s, m_s, v_s = _adamw_small(_pack(small_g), _pack(small_w), _pack(small_m), _pack(small_v))
    d_l, m_l, v_l = _unpack(d_s, small_w), _unpack(m_s, small_w), _unpack(v_s, small_w)

    def ordered(sm, big_in, big_out):
        return [sm[0], big_in[None], sm[6], sm[1], sm[7], sm[2], sm[3], sm[8], sm[4], big_out[None], sm[5]]

    grads = ordered(small_g, g_w_in, g_w_out)
    deltas = ordered(d_l, d_w_in, d_w_out)
    new_m = ordered(m_l, nm_w_in, nm_w_out)
    new_v = ordered(v_l, nv_w_in, nv_w_out)
    return (loss, grad_x2d[None], *grads, *deltas, *new_m, *new_v)
```

```python
import functools

import jax
import jax.numpy as jnp
from jax import lax
from jax.experimental import pallas as pl
from jax.experimental.pallas import tpu as pltpu

F32 = jnp.float32
BF16 = jnp.bfloat16
MESH = pl.DeviceIdType.MESH

N_DEV = 8
D_MODEL = 1024
HEADS = 4
DK = 128
DV = 256
QK_W = HEADS * DK
V_W = HEADS * DV
CONV_W = 1024
MIX_W = V_W + CONV_W
CHUNK = 64
RANK = 16
IN_W = 7200
SHARD_W = IN_W // N_DEV
MAIN_W = 7168
LR_W = 128
OFF_Q, OFF_K, OFF_V, OFF_ZA, OFF_B, OFF_ZC, OFF_C, OFF_H = 0, 512, 1024, 2048, 3072, 4096, 5120, 6144
QKV_W, GATES_W, CH_W = 2048, 3072, 2048
NAT_ZA, NAT_LR, NAT_B, NAT_C, NAT_ZC = 2048, 3072, 3104, 4128, 6176
EPS = 1e-6
GATE_SCALE = 1.0 / 16.0
QSCALE = DK ** -0.5
REF_F, LAST_F = CHUNK // 2, CHUNK - 1
REF_B, LAST_B = CHUNK - 1 - CHUNK // 2, 0

ADAM_LR = 0.001
ADAM_B1 = 0.9
ADAM_B2 = 0.999
ADAM_EPS = 1e-08
ADAM_WD = 0.01
ADAM_STEP = 10

VMEM_LIMIT = 56 * 1024 * 1024


def _cparams(*sem):
    return pltpu.CompilerParams(dimension_semantics=sem, vmem_limit_bytes=VMEM_LIMIT)


def _dot(a, b):
    return jnp.dot(a, b, preferred_element_type=F32)


def _dot_nt(a, b):
    return lax.dot_general(a, b, (((1,), (1,)), ((), ())), preferred_element_type=F32)


def _dot_tn(a, b):
    return lax.dot_general(a, b, (((0,), (0,)), ((), ())), preferred_element_type=F32)


def _sigmoid(z):
    return jax.nn.sigmoid(z)


def _position():
    return lax.axis_index("x"), lax.axis_index("y"), lax.axis_index("c")


def _blk(px, py, pc):
    return 4 * px + 2 * py + pc


EDGE = 16
SHIFTED_ROWS = 912
BODY_ROWS = SHIFTED_ROWS - 2 * EDGE


def _first_tile_row(blk, px):
    return EDGE * (56 * blk + px)


def _edge_tiles():
    tiles = {}
    for blk in range(N_DEV):
        first = _first_tile_row(blk, blk // 4)
        tiles.setdefault(first, []).append((blk, 0))
        tiles.setdefault(first + EDGE + BODY_ROWS, []).append((blk, 1))
    return tiles


def _peer_copies(srcs, outs, send_sems, recv_sems):
    x, y, c = _position()
    me = _blk(x, y, c)
    copies = []
    for a, (src, out) in enumerate(zip(srcs, outs)):
        k = 0
        for dx in (0, 1):
            for dy in (0, 1):
                for dc in (0, 1):
                    if dx + dy + dc == 0:
                        continue
                    peer = (1 - x if dx else x, 1 - y if dy else y, 1 - c if dc else c)
                    copies.append(pltpu.make_async_remote_copy(
                        src_ref=src, dst_ref=out.at[me], send_sem=send_sems.at[a * 7 + k],
                        recv_sem=recv_sems.at[a * 7 + k], device_id=peer, device_id_type=MESH))
                    k += 1
    return copies


def _route_chips():
    x, y, c = _position()
    along_x = c == 0
    return [(jnp.where(along_x, 1 - x, x), jnp.where(along_x, y, 1 - y)),
            (jnp.where(along_x, x, 1 - x), jnp.where(along_x, 1 - y, y)), (1 - x, 1 - y)]


WINDOW_ROWS = SHARD_W + 4


def _window_start(k, parity):
    return 2 * SHARD_W * k + (SHARD_W - 4) * parity


def _owner_block(part, k, parity):
    if part.ndim == 3:
        return part.at[2 * k + parity]
    return part.at[pl.ds(pl.multiple_of(_window_start(k, parity), 8), WINDOW_ROWS)]


def _block_shape(part):
    return part.shape[1:] if part.ndim == 3 else (WINDOW_ROWS, part.shape[1])


def _sibling_copies(part, out, send_sems, recv_sems):
    x, y, c = _position()
    return [pltpu.make_async_remote_copy(src_ref=_owner_block(part, k, 1 - c), dst_ref=out.at[k],
                                         send_sem=send_sems.at[k], recv_sem=recv_sems.at[k],
                                         device_id=(x, y, 1 - c), device_id_type=MESH)
            for k in range(4)]


def _start_all(copies):
    for cp in copies:
        cp.start()


def _wait_all(copies):
    for cp in copies:
        cp.wait_recv()
    for cp in copies:
        cp.wait_send()


def _chip_sums(part, from_sibling, core, tc, name, riding=None):
    rows, cols = _block_shape(part)
    nj = cols // tc

    def body(core_ref, p_ref, s_ref, *rest):
        if riding is None:
            (o_ref,) = rest
        else:
            ride_in, o_ref, ride_out, send_sems, recv_sems = rest
            k, j = pl.program_id(0), pl.program_id(1)

            @pl.when(jnp.logical_and(k == 0, j == 0))
            def _():
                _start_all(_sibling_copies(ride_in, ride_out, send_sems, recv_sems))

        o_ref[0] = (p_ref[...].reshape(rows, tc) + s_ref[0]).astype(BF16)

        if riding is not None:
            @pl.when(jnp.logical_and(k == 3, j == nj - 1))
            def _():
                _wait_all(_sibling_copies(ride_in, ride_out, send_sems, recv_sems))

    hbm = pl.BlockSpec(memory_space=pl.ANY)
    sums = jax.ShapeDtypeStruct((4, rows, cols), BF16)
    tile_out = pl.BlockSpec((1, rows, tc), lambda k, j, core_ref: (k, 0, j))
    if part.ndim == 3:
        mine = pl.BlockSpec((1, rows, tc), lambda k, j, core_ref: (2 * k + core_ref[0], 0, j))
    else:
        mine = pl.BlockSpec((pl.Element(rows), pl.Element(tc)),
                            lambda k, j, core_ref: (pl.multiple_of(_window_start(k, core_ref[0]), 8),
                                                    pl.multiple_of(j * tc, 128)))
    in_specs = [mine, pl.BlockSpec((1, rows, tc), lambda k, j, core_ref: (k, 0, j))]
    if riding is None:
        out_shape, out_specs, scratch, args = sums, tile_out, [], (core, part, from_sibling)
    else:
        out_shape = (sums, jax.ShapeDtypeStruct((4,) + _block_shape(riding), F32))
        out_specs, in_specs = (tile_out, hbm), in_specs + [hbm]
        scratch = [pltpu.SemaphoreType.DMA((4,)), pltpu.SemaphoreType.DMA((4,))]
        args = (core, part, from_sibling, riding)
    return pl.pallas_call(
        body, name=name, out_shape=out_shape,
        grid_spec=pltpu.PrefetchScalarGridSpec(num_scalar_prefetch=1, grid=(4, nj), in_specs=in_specs,
                                               out_specs=out_specs, scratch_shapes=scratch),
        compiler_params=_cparams("arbitrary", "arbitrary"),
    )(*args)


def _sum_chips(s_ref, r_ref):
    f = lambda a: a.astype(F32)
    return (f(s_ref[0]) + f(r_ref[0])) + f(r_ref[1])


def _final_sum(sums, from_chips, chip, small, tc, name):
    _, rows, cols = sums.shape
    nj = cols // tc

    def body(chip_ref, s_ref, r_ref, sm_ref, g_out, tot_ref, all_ref, send_sems, recv_sems):
        j = pl.program_id(0)
        me = _blk(*_position())

        @pl.when(j == 0)
        def _():
            all_ref[me] = sm_ref[...]
            _start_all(_peer_copies((all_ref.at[me],), (all_ref,), send_sems, recv_sems))

        g_out[...] = _sum_chips(s_ref, r_ref)

        @pl.when(j == nj - 1)
        def _():
            _wait_all(_peer_copies((all_ref.at[me],), (all_ref,), send_sems, recv_sems))
            acc = all_ref[0]
            for d in range(1, N_DEV):
                acc = acc + all_ref[d]
            tot_ref[...] = acc

    whole = pl.BlockSpec(small.shape, lambda j, chip_ref: (0, 0))
    return pl.pallas_call(
        body, name=name,
        out_shape=(jax.ShapeDtypeStruct((rows, cols), F32), jax.ShapeDtypeStruct(small.shape, F32)),
        grid_spec=pltpu.PrefetchScalarGridSpec(
            num_scalar_prefetch=1, grid=(nj,),
            in_specs=[pl.BlockSpec((1, rows, tc), lambda j, chip_ref: (chip_ref[0], 0, j)),
                      pl.BlockSpec((2, rows, tc), lambda j, chip_ref: (0, 0, j)), whole],
            out_specs=(pl.BlockSpec((rows, tc), lambda j, chip_ref: (0, j)), whole),
            scratch_shapes=[pltpu.VMEM((N_DEV,) + small.shape, F32), pltpu.SemaphoreType.DMA((7,)),
                            pltpu.SemaphoreType.DMA((7,))]),
        compiler_params=_cparams("arbitrary"),
    )(chip, sums, from_chips, small)


def _adamw_rows(g, w, m, v, tr, name):
    rows = g.shape[0]

    def body(g_ref, w_ref, m_ref, v_ref, d_out, m_out, v_out):
        delta, m_new, v_new = _adamw(w_ref[...], g_ref[...], m_ref[...], v_ref[...])
        d_out[...] = delta
        m_out[...] = m_new
        v_out[...] = v_new

    tile = pl.BlockSpec((tr,) + g.shape[1:], lambda r: (r, 0, 0))
    shp = jax.ShapeDtypeStruct(g.shape, F32)
    return pl.pallas_call(
        body, name=name, out_shape=(shp, shp, shp), grid=(rows // tr,),
        in_specs=[tile] * 4, out_specs=(tile, tile, tile),
        compiler_params=_cparams("arbitrary"),
    )(g, w, m, v)


def _adamw(w, g, m, v):
    m = ADAM_B1 * m + (1.0 - ADAM_B1) * g
    v = ADAM_B2 * v + (1.0 - ADAM_B2) * (g * g)
    m_hat = m / (1.0 - ADAM_B1 ** ADAM_STEP)
    v_hat = v / (1.0 - ADAM_B2 ** ADAM_STEP)
    delta = -ADAM_LR * (m_hat / (jnp.sqrt(v_hat) + ADAM_EPS) + ADAM_WD * w)
    return delta, m, v


def _final_sum_adamw(sums, from_chips, chip, w, m, v, tr, name):
    rows, cols = w.shape

    def body(chip_ref, s_ref, r_ref, w_ref, m_ref, v_ref, g_out, d_out, m_out, v_out):
        g = _sum_chips(s_ref, r_ref)
        delta, m_new, v_new = _adamw(w_ref[...], g, m_ref[...], v_ref[...])
        g_out[...] = g
        d_out[...] = delta
        m_out[...] = m_new
        v_out[...] = v_new

    tile = pl.BlockSpec((tr, cols), lambda r, chip_ref: (r, 0))
    shp = jax.ShapeDtypeStruct((rows, cols), F32)
    return pl.pallas_call(
        body, name=name,
        out_shape=(shp, shp, shp, shp),
        grid_spec=pltpu.PrefetchScalarGridSpec(
            num_scalar_prefetch=1, grid=(rows // tr,),
            in_specs=[pl.BlockSpec((1, tr, cols), lambda r, chip_ref: (chip_ref[0], r, 0)),
                      pl.BlockSpec((2, tr, cols), lambda r, chip_ref: (0, r, 0)),
                      tile, tile, tile],
            out_specs=(tile, tile, tile, tile)),
        compiler_params=_cparams("arbitrary"),
    )(chip, sums, from_chips, w, m, v)


def _adamw_small(g, w, m, v):
    def body(g_ref, w_ref, m_ref, v_ref, d_out, m_out, v_out):
        delta, m_new, v_new = _adamw(w_ref[...], g_ref[...], m_ref[...], v_ref[...])
        d_out[...] = delta
        m_out[...] = m_new
        v_out[...] = v_new

    vmem = pl.BlockSpec(memory_space=pltpu.VMEM)
    shp = jax.ShapeDtypeStruct(g.shape, F32)
    return pl.pallas_call(body, name="adamw_small", out_shape=(shp, shp, shp),
                          in_specs=[vmem] * 4, out_specs=(vmem, vmem, vmem))(g, w, m, v)


TILE_ROWS = (0, 1024, NAT_ZA, NAT_B, NAT_ZC, NAT_C, NAT_C + CONV_W)


TILE_ORDER = ((0, 1, 2, 3, 5, 6, 4), (2, 1, 0, 4, 3, 5, 6), (5, 6, 0, 4, 1, 2, 3), (4, 6, 2, 3, 5, 0, 1))
EARLY_SWEEP, NEIGHBOUR_SWEEP, DIAGONAL_SWEEP = 1, 2, 4
PIECES, W_IN_PIECES, OTHER_PIECES = 4, (0, 1), (2, 3)


def _gather_inproj(x2d, norm_g, shard_t, w_out_s, small_s, order, tm):
    seq = x2d.shape[0]
    tn = CONV_W
    ni, nj = seq // tm, MAIN_W // tn
    first_sweep = lambda j, i, order_ref: jnp.where(j == 0, i, ni - 1)
    last_sweep = lambda j, i, order_ref: jnp.where(j == nj - 1, i, 0)
    edge_tiles = _edge_tiles()

    def body(order_ref, x_ref, g_ref, shard_ref, wout_ref, sm_ref, proj_ref, lr_ref, ht_ref, w_nat, wout_all, sm_all,
             w_all, h_all, edges, stage, wout_b, sm_b, send_sems, recv_sems, local_sems):
        j, i = pl.program_id(0), pl.program_id(1)
        rows = pl.ds(pl.multiple_of(i * tm, tm), tm)
        x, y, c = _position()
        me, here, sibling = _blk(x, y, c), (x, y, c), (x, y, 1 - c)
        chips = _route_chips()
        sibling_chips = [chips[1], chips[0], chips[2]]

        def pieces(px, py, pc):
            blk = _blk(px, py, pc)
            body_rows = pl.ds(pl.multiple_of(_first_tile_row(blk, px) + EDGE, EDGE), BODY_ROWS)
            return [w_all.at[body_rows], edges.at[blk], wout_all.at[blk], sm_all.at[blk]]

        def copy(a, k, block, to, staged=None):
            ref = pieces(*block)[a]
            return pltpu.make_async_remote_copy(src_ref=ref if staged is None else staged, dst_ref=ref,
                                                send_sem=send_sems.at[a * 7 + k], recv_sem=recv_sems.at[a * 7 + k],
                                                device_id=to, device_id_type=MESH)

        def own_copies(group, slots=(0, 1, 2)):
            targets = [sibling] + [(*chips[n], c) for n in range(2)]
            staged = [None, None, wout_b, sm_b]
            return [copy(a, k, here, targets[k], staged[a]) for k in slots for a in group]

        def relays(group):
            return [copy(a, 3, (*chips[0], c), (*chips[1], c)) for a in group]

        def forwards(n, group):
            return [copy(a, 4 + n, (*chips[n], c), sibling) for a in group]

        def keep_own():
            return [pltpu.make_async_copy(wout_b, wout_all.at[me], local_sems.at[0]),
                    pltpu.make_async_copy(sm_b, sm_all.at[me], local_sems.at[1])]

        def keep_weight():
            return pltpu.make_async_copy(w_all, w_nat, local_sems.at[2])

        def take(ns, group, relay=True):
            for n in ns:
                for a in group:
                    copy(a, 1 + n, (*chips[n], c), here).wait_recv()
                _start_all((relays(group) if n == 0 and relay else []) + forwards(n, group))

        def take_passed_on(ns, group):
            for n in ns:
                for a in group:
                    copy(a, 4 + n, (*sibling_chips[n], 1 - c), here).wait_recv()

        def arrive(ns, group):
            take(ns, group)
            take_passed_on(ns, group)

        def per_core_and_row(step):
            for core in range(2):
                for row in range(2):
                    pl.when(jnp.logical_and(c == core, y == row))(functools.partial(step, core, row))

        def start_own(core, row):
            now = (0, 1 + core) if core == row else (0, 1, 2)
            _start_all(own_copies(W_IN_PIECES, now))
            wout_b[...] = wout_ref[...].astype(BF16)
            sm_b[...] = sm_ref[...]
            _start_all(own_copies(OTHER_PIECES, now) + keep_own())

        def take_early(core, row):
            if core == row:
                _start_all(own_copies(W_IN_PIECES, (2 - core,)) + own_copies(OTHER_PIECES, (2 - core,)))
                take((1 - core,), W_IN_PIECES, relay=False)
            else:
                take_passed_on((core,), W_IN_PIECES)

        def take_neighbours(core, row):
            if core == row:
                _start_all(relays(W_IN_PIECES) if core == 1 else [])
                take((core,), W_IN_PIECES)
                take_passed_on((0, 1), W_IN_PIECES)
            else:
                take((0, 1), W_IN_PIECES)
                take_passed_on((1 - core,), W_IN_PIECES)

        early_blk = _blk(x, 1 - y, y)

        def add_edge_tiles(stage):
            for row, parts in edge_tiles.items():
                ready = 0
                for blk, _ in parts:
                    away = (x != blk // 4).astype(jnp.int32) + (y != (blk // 2) % 2).astype(jnp.int32)
                    late = jnp.where(away == 1, jnp.where(early_blk == blk, 1, 2), jnp.where(away == 2, 3 + blk % 2, 0))
                    ready = jnp.maximum(ready, late)

                @pl.when(ready == stage)
                def _(row=row, parts=parts):
                    tile = edges[parts[0][0], parts[0][1]].astype(F32)
                    for blk, side in parts[1:]:
                        tile = tile + edges[blk, side].astype(F32)
                    w_all[row:row + EDGE, :] = tile.astype(BF16)

        @pl.when(jnp.logical_and(j == 0, i == 0))
        def _():
            last = SHARD_W // 8 * 8
            for col in range(0, D_MODEL, 128):
                cols = slice(col, col + 128)
                stage[0:last, :] = shard_ref[0:last, cols]
                stage[last:, :] = jnp.zeros((SHIFTED_ROWS - last, 128), F32)
                stage[last:SHARD_W, :] = shard_ref[last:SHARD_W, cols]
                for k in range(EDGE // 4):
                    @pl.when(me % 4 == k)
                    def _(k=k, cols=cols):
                        moved = pltpu.roll(stage[...], 4 * k, 0) if k else stage[...]
                        pieces(*here)[0][:, cols] = moved[EDGE:EDGE + BODY_ROWS].astype(BF16)
                        edges[me, 0, :, cols] = moved[0:EDGE].astype(BF16)
                        edges[me, 1, :, cols] = moved[EDGE + BODY_ROWS:].astype(BF16)
            per_core_and_row(start_own)
            for a in W_IN_PIECES:
                copy(a, 0, sibling, here).wait_recv()
            add_edge_tiles(0)

        @pl.when(jnp.logical_and(j == EARLY_SWEEP, i == 0))
        def _():
            per_core_and_row(take_early)
            add_edge_tiles(1)

        @pl.when(jnp.logical_and(j == NEIGHBOUR_SWEEP, i == 0))
        def _():
            per_core_and_row(take_neighbours)
            add_edge_tiles(2)

        for core in range(2):
            @pl.when(jnp.logical_and(j == DIAGONAL_SWEEP + core, i == 0))
            def _(core=core):
                pl.when(c == core)(lambda: take((2,), W_IN_PIECES))
                pl.when(c != core)(lambda: take_passed_on((2,), W_IN_PIECES))
                add_edge_tiles(3 + core)
                if core == 0:
                    arrive((0, 1), OTHER_PIECES)
                else:
                    keep_weight().start()

        @pl.when(jnp.logical_and(j == nj - 1, i == 0))
        def _():
            arrive((2,), OTHER_PIECES)

        @pl.when(j == 0)
        def _():
            xv = x_ref[...]
            r = lax.rsqrt(jnp.mean(xv * xv, axis=-1, keepdims=True) + EPS)
            h = (xv * r) * g_ref[...]
            h_all[rows, :] = h.astype(BF16)
            ht_ref[...] = h.T.astype(BF16)

        tile = order_ref[j]
        row = 0
        for k, start in enumerate(TILE_ROWS):
            row = row + jnp.where(tile == k, start // 32, 0)
        w_tile = w_all[pl.ds(pl.multiple_of(row * 32, 32), tn), :]
        proj_ref[...] = _dot_nt(h_all[rows, :], w_tile).astype(BF16)

        @pl.when(j == nj - 1)
        def _():
            lr_ref[...] = _dot_nt(h_all[rows, :], w_all[NAT_LR:NAT_LR + LR_W, :])

        @pl.when(jnp.logical_and(j == nj - 1, i == ni - 1))
        def _():
            everything = range(PIECES)
            passed_on = [cp for n in range(3) for cp in forwards(n, everything)]
            for cp in own_copies(everything) + relays(everything) + passed_on:
                cp.wait_send()
            for a in OTHER_PIECES:
                copy(a, 0, sibling, here).wait_recv()
            for cp in keep_own() + [keep_weight()]:
                cp.wait()

    const = lambda shape: pl.BlockSpec(shape, lambda j, i, order_ref: (0,) * len(shape))
    hbm = pl.BlockSpec(memory_space=pl.ANY)
    vmem = pl.BlockSpec(memory_space=pltpu.VMEM)
    return pl.pallas_call(
        body, name="gather_inproj",
        out_shape=(jax.ShapeDtypeStruct((seq, MAIN_W), BF16), jax.ShapeDtypeStruct((seq, LR_W), F32),
                   jax.ShapeDtypeStruct((D_MODEL, seq), BF16), jax.ShapeDtypeStruct((IN_W, D_MODEL), BF16),
                   jax.ShapeDtypeStruct((N_DEV,) + w_out_s.shape, BF16),
                   jax.ShapeDtypeStruct((N_DEV,) + small_s.shape, F32)),
        grid_spec=pltpu.PrefetchScalarGridSpec(
            num_scalar_prefetch=1, grid=(nj, ni),
            in_specs=[pl.BlockSpec((tm, D_MODEL), lambda j, i, order_ref: (first_sweep(j, i, order_ref), 0)),
                      const((1, D_MODEL)), vmem, vmem, const(small_s.shape)],
            out_specs=(pl.BlockSpec((tm, tn), lambda j, i, order_ref: (i, order_ref[j])),
                       pl.BlockSpec((tm, LR_W), lambda j, i, order_ref: (last_sweep(j, i, order_ref), 0)),
                       pl.BlockSpec((D_MODEL, tm), lambda j, i, order_ref: (0, first_sweep(j, i, order_ref))),
                       hbm, hbm, hbm),
            scratch_shapes=[pltpu.VMEM((IN_W, D_MODEL), BF16), pltpu.VMEM((seq, D_MODEL), BF16),
                            pltpu.VMEM((N_DEV, 2, EDGE, D_MODEL), BF16), pltpu.VMEM((SHIFTED_ROWS, 128), F32),
                            pltpu.VMEM(w_out_s.shape, BF16), pltpu.VMEM(small_s.shape, F32),
                            pltpu.SemaphoreType.DMA((7 * PIECES,)), pltpu.SemaphoreType.DMA((7 * PIECES,)),
                            pltpu.SemaphoreType.DMA((3,))]),
        compiler_params=_cparams("arbitrary", "arbitrary"),
    )(order, x2d, norm_g, shard_t, w_out_s, small_s)


def _block_masks(tt):
    row = lax.broadcasted_iota(jnp.int32, (tt, tt), 0)
    col = lax.broadcasted_iota(jnp.int32, (tt, tt), 1)
    same = jnp.right_shift(row, 6) == jnp.right_shift(col, 6)
    return (jnp.logical_and(same, col <= row), jnp.logical_and(same, col >= row), jnp.logical_and(same, col > row))


def _dot_split3(ones_mat, x):
    x1 = x.astype(BF16)
    r1 = x - x1.astype(F32)
    x2 = r1.astype(BF16)
    x3 = (r1 - x2.astype(F32)).astype(BF16)
    return (_dot(ones_mat, x3) + _dot(ones_mat, x2)) + _dot(ones_mat, x1)


def _log_gate(logits):
    return (jnp.minimum(logits, 0.0) - jnp.log(1.0 + jnp.exp(-jnp.abs(logits)))) * GATE_SCALE


def _chunk_column_mask(tt):
    nc = tt // CHUNK
    row = lax.broadcasted_iota(jnp.int32, (tt, nc * DK), 0)
    col = lax.broadcasted_iota(jnp.int32, (tt, nc * DK), 1)
    return jnp.right_shift(row, 6) == jnp.right_shift(col, 7)


def _chunked(mask, x, nc):
    wide = jnp.concatenate([x] * nc, axis=1)
    return jnp.where(mask, wide, jnp.zeros_like(wide))


def _gla_fwd(proj, lr, wgk_f, wgk_b, bgk_f, bgk_b, tt):
    seq = proj.shape[0]
    nb, nc, nch = seq // tt, tt // CHUNK, seq // CHUNK

    def body(qf, kf, vf, lrf, qb, kb, vb, lrb, wf, wb, bf, bb, of, ob, stf, stb, s_scr, qs_s, ks_s, qin_s, kout_s):
        @pl.when(pl.program_id(0) == 0)
        def _():
            s_scr[...] = jnp.zeros(s_scr.shape, F32)

        low, upp, sup = _block_masks(tt)
        dirs = ((qf, kf, vf, lrf, wf, bf, of, stf, low, low, REF_F, LAST_F, list(range(nc))),
                (qb, kb, vb, lrb, wb, bb, ob, stb, upp, sup, REF_B, LAST_B, list(reversed(range(nc)))))
        for d, (q_r, k_r, v_r, lr_r, w_r, b_r, o_r, st_r, cum, mask, ref, last, order) in enumerate(dirs):
            logits = _dot(lr_r[...].astype(BF16), w_r[...]) + b_r[...]
            b = _dot_split3(cum.astype(BF16), _log_gate(logits))
            decs = []
            for c in range(nc):
                rows = slice(c * CHUNK, (c + 1) * CHUNK)
                bc = b[rows]
                b_ref, b_last = bc[ref:ref + 1], bc[last:last + 1]
                qc = q_r[rows, :].astype(F32) * QSCALE
                kc = k_r[rows, :].astype(F32)
                qs_s[rows, :] = (qc * jnp.exp(bc - b_ref)).astype(BF16)
                ks_s[rows, :] = (kc * jnp.exp(b_ref - bc)).astype(BF16)
                qin_s[rows, :] = (qc * jnp.exp(bc)).astype(BF16)
                kout_s[rows, :] = (kc * jnp.exp(b_last - bc)).astype(BF16)
                decs.append(jnp.exp(b_last))
            for h in range(HEADS):
                ksl = slice(h * DK, (h + 1) * DK)
                vsl = slice(h * DV, (h + 1) * DV)
                v = v_r[:, vsl].astype(BF16)
                att = jnp.where(mask, _dot_nt(qs_s[:, ksl], ks_s[:, ksl]), 0.0).astype(BF16)
                o_intra = _dot(att, v)
                st = s_scr[d * HEADS + h]
                for c in order:
                    rows = slice(c * CHUNK, (c + 1) * CHUNK)
                    stb = st.astype(BF16)
                    st_r[c, h] = stb
                    o_r[rows, vsl] = (o_intra[rows] + _dot_nt(qin_s[rows, ksl], stb)).astype(BF16)
                    st = st * decs[c][:, ksl] + _dot_tn(v[rows], kout_s[rows, ksl])
                s_scr[d * HEADS + h] = st

    fw = lambda i: (i, 0)
    bw = lambda i: (nb - 1 - i, 0)
    const = lambda i: (0, 0)

    def tok_specs(m):
        return [pl.BlockSpec((tt, QK_W), lambda i: (m(i)[0], OFF_Q // QK_W)),
                pl.BlockSpec((tt, QK_W), lambda i: (m(i)[0], OFF_K // QK_W)),
                pl.BlockSpec((tt, V_W), lambda i: (m(i)[0], OFF_V // V_W)),
                pl.BlockSpec((tt, LR_W), m)]

    st_shape = jax.ShapeDtypeStruct((nch, HEADS, DV, DK), BF16)
    o_shape = jax.ShapeDtypeStruct((seq, V_W), BF16)
    operand = pltpu.VMEM((tt, QK_W), BF16)
    return pl.pallas_call(
        body, name="gla_fwd",
        out_shape=(o_shape, o_shape, st_shape, st_shape),
        grid=(nb,),
        in_specs=tok_specs(fw) + tok_specs(bw) + [
            pl.BlockSpec((LR_W, QK_W), const), pl.BlockSpec((LR_W, QK_W), const),
            pl.BlockSpec((1, QK_W), const), pl.BlockSpec((1, QK_W), const)],
        out_specs=(pl.BlockSpec((tt, V_W), fw), pl.BlockSpec((tt, V_W), bw),
                   pl.BlockSpec((nc, HEADS, DV, DK), lambda i: (i, 0, 0, 0)),
                   pl.BlockSpec((nc, HEADS, DV, DK), lambda i: (nb - 1 - i, 0, 0, 0))),
        scratch_shapes=[pltpu.VMEM((2 * HEADS, DV, DK), F32), operand, operand, operand, operand],
        compiler_params=_cparams("arbitrary"),
    )(proj, proj, proj, lr, proj, proj, proj, lr, wgk_f, wgk_b, bgk_f, bgk_b)


def _head_norm(o, gain):
    outs, rinv = [], []
    for h in range(HEADS):
        oh = o[:, h * DV:(h + 1) * DV]
        r = lax.rsqrt(jnp.mean(oh * oh, axis=-1, keepdims=True) + EPS)
        outs.append((oh * r) * gain)
        rinv.append(r)
    return jnp.concatenate(outs, axis=1), rinv


def _shift_rows(u, prev_row, next_row):
    n = u.shape[0]
    row = lax.broadcasted_iota(jnp.int32, (n, 1), 0)
    up = jnp.where(row == 0, prev_row, pltpu.roll(u, 1, 0))
    un = jnp.where(row == n - 1, next_row, pltpu.roll(u, n - 1, 0))
    return up, un


HALO = 16


def _halo_specs(tm, seq, col_block):
    per = tm // HALO
    last = seq // HALO - 1
    return [pl.BlockSpec((HALO, CONV_W), lambda i: (jnp.maximum(i * per - 1, 0), col_block)),
            pl.BlockSpec((HALO, CONV_W), lambda i: (jnp.minimum((i + 1) * per, last), col_block))]


def _f32(ref):
    return ref[...].astype(F32)


def _last_row(ref):
    return ref[HALO - 1:HALO, :].astype(F32)


def _first_row(ref):
    return ref[0:1, :].astype(F32)


def _mix_out_loss(o_f, o_b, proj, x2d, tgt, gla_g, conv_w, conv_b, w_out, final_g, tm):
    seq = x2d.shape[0]
    nt = seq // tm

    def body(of, ob, za, bg, cg, hc, zc, cprev, cnext, hprev, hnext, x_ref, t_ref, gg, cw, cb, wo, fg,
             yt_ref, conv_ref, dx2_ref, dx2b_ref, loss_ref, dfg_ref):
        i = pl.program_id(0)

        @pl.when(i == 0)
        def _():
            loss_ref[...] = jnp.zeros(loss_ref.shape, F32)
            dfg_ref[...] = jnp.zeros(dfg_ref.shape, F32)

        on, _ = _head_norm(_f32(of) + _f32(ob), gg[...])
        zav = _f32(za)
        y_a = on * (zav * _sigmoid(zav))
        u = _f32(cg) * _f32(hc)
        prev_row = jnp.where(i > 0, _last_row(cprev) * _last_row(hprev), 0.0)
        next_row = jnp.where(i < nt - 1, _first_row(cnext) * _first_row(hnext), 0.0)
        up, un = _shift_rows(u, prev_row, next_row)
        conv = (cw[0:1, :] * up + cw[1:2, :] * u + cw[2:3, :] * un) + cb[...]
        conv_ref[...] = conv.astype(BF16)
        zcv = _f32(zc)
        y_c = _f32(bg) * conv * (zcv * _sigmoid(zcv))
        y = jnp.concatenate([y_a, y_c], axis=1)
        yt_ref[...] = y.T.astype(BF16)
        x2 = x_ref[...] + _dot(y.astype(BF16), wo[...])
        r = lax.rsqrt(jnp.mean(x2 * x2, axis=-1, keepdims=True) + EPS)
        xn = x2 * r
        err = xn * fg[...] - t_ref[...]
        loss_ref[...] += 0.5 * jnp.sum(jnp.mean(err * err, axis=-1, keepdims=True))
        dyf = err * (1.0 / D_MODEL)
        dfg_ref[...] += jnp.sum(dyf * xn, axis=0, keepdims=True)
        dxn = dyf * fg[...]
        dx2 = r * dxn - xn * (r * jnp.mean(dxn * xn, axis=-1, keepdims=True))
        dx2_ref[...] = dx2
        dx2b_ref[...] = dx2.astype(BF16)

    def col(off):
        return pl.BlockSpec((tm, CONV_W), lambda i: (i, off // CONV_W))

    rowt = pl.BlockSpec((tm, D_MODEL), lambda i: (i, 0))
    const = lambda shape: pl.BlockSpec(shape, lambda i: (0, 0))
    return pl.pallas_call(
        body, name="mix_out_loss",
        out_shape=(jax.ShapeDtypeStruct((MIX_W, seq), BF16), jax.ShapeDtypeStruct((seq, CONV_W), BF16),
                   jax.ShapeDtypeStruct((seq, D_MODEL), F32), jax.ShapeDtypeStruct((seq, D_MODEL), BF16),
                   jax.ShapeDtypeStruct((8, 128), F32), jax.ShapeDtypeStruct((1, D_MODEL), F32)),
        grid=(nt,),
        in_specs=[rowt, rowt, col(OFF_ZA), col(OFF_B), col(OFF_C), col(OFF_H), col(OFF_ZC)]
        + _halo_specs(tm, seq, OFF_C // CONV_W) + _halo_specs(tm, seq, OFF_H // CONV_W)
        + [rowt, rowt, const((1, DV)), const((8, CONV_W)), const((1, CONV_W)), const((MIX_W, D_MODEL)),
           const((1, D_MODEL))],
        out_specs=(pl.BlockSpec((MIX_W, tm), lambda i: (0, i)), rowt, rowt, rowt, const((8, 128)),
                   const((1, D_MODEL))),
        compiler_params=_cparams("arbitrary"),
    )(o_f, o_b, proj, proj, proj, proj, proj, proj, proj, proj, proj, x2d, tgt, gla_g, conv_w, conv_b, w_out, final_g)


def _dsilu(z, s):
    return s * (1.0 + z * (1.0 - s))


def _mix_bwd(dx2b, o_f, o_b, proj, conv, gla_g, w_out, tm):
    seq = dx2b.shape[0]

    def body(dx, of, ob, za, bg, zc, cv, gg, wo, dg_ref, do_ref, dconv_ref, dgg_ref, dcb_ref):
        @pl.when(pl.program_id(0) == 0)
        def _():
            dgg_ref[...] = jnp.zeros(dgg_ref.shape, F32)
            dcb_ref[...] = jnp.zeros(dcb_ref.shape, F32)

        dy = _dot_nt(dx[...], wo[...])
        dy_a, dy_c = dy[:, :V_W], dy[:, V_W:]
        zcv, bgv, convv = _f32(zc), _f32(bg), _f32(cv)
        sc = _sigmoid(zcv)
        szc = zcv * sc
        dg_ref[:, CONV_W:2 * CONV_W] = (dy_c * convv * szc).astype(BF16)
        dconv = dy_c * bgv * szc
        dconv_ref[...] = dconv.astype(BF16)
        dcb_ref[...] += jnp.sum(dconv, axis=0, keepdims=True)
        dg_ref[:, 2 * CONV_W:] = (dy_c * bgv * convv * _dsilu(zcv, sc)).astype(BF16)

        o = _f32(of) + _f32(ob)
        gain = gg[...]
        on, rinv = _head_norm(o, gain)
        zav = _f32(za)
        sa = _sigmoid(zav)
        dg_ref[:, :CONV_W] = (dy_a * on * _dsilu(zav, sa)).astype(BF16)
        don = dy_a * (zav * sa)
        dgg = jnp.zeros((1, DV), F32)
        dos = []
        for h in range(HEADS):
            sl = slice(h * DV, (h + 1) * DV)
            oh, r, dh = o[:, sl], rinv[h], don[:, sl]
            ohn = oh * r
            dgg = dgg + jnp.sum(dh * ohn, axis=0, keepdims=True)
            dn = dh * gain
            dos.append(r * dn - ohn * (r * jnp.mean(dn * ohn, axis=-1, keepdims=True)))
        dgg_ref[...] += dgg
        do_ref[...] = jnp.concatenate(dos, axis=1).astype(BF16)

    def col(off):
        return pl.BlockSpec((tm, CONV_W), lambda i: (i, off // CONV_W))

    rowt = pl.BlockSpec((tm, D_MODEL), lambda i: (i, 0))
    const = lambda shape: pl.BlockSpec(shape, lambda i: (0, 0))
    return pl.pallas_call(
        body, name="mix_bwd",
        out_shape=(jax.ShapeDtypeStruct((seq, GATES_W), BF16), jax.ShapeDtypeStruct((seq, V_W), BF16),
                   jax.ShapeDtypeStruct((seq, CONV_W), BF16),
                   jax.ShapeDtypeStruct((1, DV), F32), jax.ShapeDtypeStruct((1, CONV_W), F32)),
        grid=(seq // tm,),
        in_specs=[rowt, rowt, rowt, col(OFF_ZA), col(OFF_B), col(OFF_ZC), rowt, const((1, DV)),
                  const((MIX_W, D_MODEL))],
        out_specs=(pl.BlockSpec((tm, GATES_W), lambda i: (i, 0)), rowt, rowt, const((1, DV)), const((1, CONV_W))),
        compiler_params=_cparams("arbitrary"),
    )(dx2b, o_f, o_b, proj, proj, proj, conv, gla_g, w_out)


def _conv_bwd(dconv, proj, conv_w, tm):
    seq = dconv.shape[0]
    nt = seq // tm

    def body(dc_in, dprev, dnext, cg, hc, cprev, cnext, hprev, hnext, cw, dch_ref, dcw_ref):
        i = pl.program_id(0)

        @pl.when(i == 0)
        def _():
            dcw_ref[...] = jnp.zeros(dcw_ref.shape, F32)

        first, lastt = i > 0, i < nt - 1
        dcv = _f32(dc_in)
        d_up, d_un = _shift_rows(dcv, jnp.where(first, _last_row(dprev), 0.0), jnp.where(lastt, _first_row(dnext), 0.0))
        cgv, hcv = _f32(cg), _f32(hc)
        u = cgv * hcv
        u_up, u_un = _shift_rows(u, jnp.where(first, _last_row(cprev) * _last_row(hprev), 0.0),
                                 jnp.where(lastt, _first_row(cnext) * _first_row(hnext), 0.0))
        du = cw[0:1, :] * d_un + cw[1:2, :] * dcv + cw[2:3, :] * d_up
        dch_ref[:, :CONV_W] = (du * hcv).astype(BF16)
        dch_ref[:, CONV_W:] = (du * cgv).astype(BF16)
        dcw_ref[0:1, :] += jnp.sum(dcv * u_up, axis=0, keepdims=True)
        dcw_ref[1:2, :] += jnp.sum(dcv * u, axis=0, keepdims=True)
        dcw_ref[2:3, :] += jnp.sum(dcv * u_un, axis=0, keepdims=True)

    def col(off):
        return pl.BlockSpec((tm, CONV_W), lambda i: (i, off // CONV_W))

    rowt = pl.BlockSpec((tm, CONV_W), lambda i: (i, 0))
    const = lambda shape: pl.BlockSpec(shape, lambda i: (0, 0))
    return pl.pallas_call(
        body, name="conv_bwd",
        out_shape=(jax.ShapeDtypeStruct((seq, CH_W), BF16), jax.ShapeDtypeStruct((8, CONV_W), F32)),
        grid=(nt,),
        in_specs=[rowt] + _halo_specs(tm, seq, 0) + [col(OFF_C), col(OFF_H)]
        + _halo_specs(tm, seq, OFF_C // CONV_W) + _halo_specs(tm, seq, OFF_H // CONV_W) + [const((8, CONV_W))],
        out_specs=(pl.BlockSpec((tm, CH_W), lambda i: (i, 0)), const((8, CONV_W))),
        compiler_params=_cparams("arbitrary"),
    )(dconv, dconv, dconv, proj, proj, proj, proj, proj, proj, conv_w)


def _gla_bwd(proj, lr, do, st_f, st_b, wgk_f, wgk_b, bgk_f, bgk_b, tt):
    seq = proj.shape[0]
    nb, nc = seq // tt, tt // CHUNK

    def body(qf, kf, vf, lrf, dof, stf, qb, kb, vb, lrb, dob, stb, wf, wb, bf, bb,
             dqkv_f, dlr_f, dqkv_b, dlr_b, dwf, dwb, dbf, dbb,
             ds_scr, eq_s, ek_s, ein_s, eout_s, qs_s, ks_s, qin_s, kout_s, db_s, lg_s):
        @pl.when(pl.program_id(0) == 0)
        def _():
            ds_scr[...] = jnp.zeros(ds_scr.shape, F32)
            for r in (dwf, dwb, dbf, dbb):
                r[...] = jnp.zeros(r.shape, F32)

        low, upp, sup = _block_masks(tt)
        row = lax.broadcasted_iota(jnp.int32, (CHUNK, 1), 0)
        kmask = _chunk_column_mask(tt)
        dirs = ((qf, kf, vf, lrf, dof, stf, wf, bf, dqkv_f, dlr_f, dwf, dbf,
                 low, upp, low, REF_F, LAST_F, list(reversed(range(nc)))),
                (qb, kb, vb, lrb, dob, stb, wb, bb, dqkv_b, dlr_b, dwb, dbb,
                 upp, low, sup, REF_B, LAST_B, list(range(nc))))
        for d, (q_r, k_r, v_r, lr_r, do_r, st_r, w_r, b_r, dqkv_r, dlr_r, dw_r, db_r,
                cum, cum_t, mask, ref, last, order) in enumerate(dirs):
            lrv = lr_r[...].astype(BF16)
            wv = w_r[...]
            logits = _dot(lrv, wv) + b_r[...]
            lg_s[...] = logits
            b = _dot_split3(cum.astype(BF16), _log_gate(logits))
            decs = []
            for c in range(nc):
                rows = slice(c * CHUNK, (c + 1) * CHUNK)
                bc = b[rows]
                b_ref, b_last = bc[ref:ref + 1], bc[last:last + 1]
                qc = q_r[rows, :].astype(F32) * QSCALE
                kc = k_r[rows, :].astype(F32)
                e_q, e_k, e_in, e_out = jnp.exp(bc - b_ref), jnp.exp(b_ref - bc), jnp.exp(bc), jnp.exp(b_last - bc)
                eq_s[rows, :], ek_s[rows, :], ein_s[rows, :], eout_s[rows, :] = e_q, e_k, e_in, e_out
                qs_s[rows, :] = (qc * e_q).astype(BF16)
                ks_s[rows, :] = (kc * e_k).astype(BF16)
                qin_s[rows, :] = (qc * e_in).astype(BF16)
                kout_s[rows, :] = (kc * e_out).astype(BF16)
                decs.append(jnp.exp(b_last))
            for h in range(HEADS):
                ksl = slice(h * DK, (h + 1) * DK)
                vsl = slice(h * DV, (h + 1) * DV)
                v = v_r[:, vsl].astype(BF16)
                dov = do_r[:, vsl].astype(BF16)
                qsb, ksb = qs_s[:, ksl], ks_s[:, ksl]
                att = jnp.where(mask, _dot_nt(qsb, ksb), 0.0).astype(BF16)
                datt = jnp.where(mask, _dot_nt(dov, v), 0.0).astype(BF16)
                dqs = _dot(datt, ksb)
                dks = _dot_tn(datt, qsb)
                dv_intra = _dot_tn(att, dov)
                g_t = _dot_tn(dov, _chunked(kmask, qin_s[:, ksl], nc))
                ds = ds_scr[d * HEADS + h]
                for c in order:
                    rows = slice(c * CHUNK, (c + 1) * CHUNK)
                    dsb = ds.astype(BF16)
                    s_prev = st_r[c, h]
                    dk_out = _dot(v[rows], dsb)
                    dq_in = _dot(dov[rows], s_prev)
                    dv = dv_intra[rows] + _dot_nt(kout_s[rows, ksl], dsb)
                    dqkv_r[rows, OFF_V + h * DV:OFF_V + (h + 1) * DV] = dv.astype(BF16)
                    dec = decs[c][:, ksl]
                    ddec = jnp.sum(ds * s_prev.astype(F32), axis=0, keepdims=True)
                    e_out = eout_s[rows, ksl]
                    qc = q_r[rows, ksl].astype(F32) * QSCALE
                    kc = k_r[rows, ksl].astype(F32)
                    dq = dqs[rows] * eq_s[rows, ksl] + dq_in * ein_s[rows, ksl]
                    dk = dks[rows] * ek_s[rows, ksl] + dk_out * e_out
                    dqkv_r[rows, OFF_Q + h * DK:OFF_Q + (h + 1) * DK] = (dq * QSCALE).astype(BF16)
                    dqkv_r[rows, OFF_K + h * DK:OFF_K + (h + 1) * DK] = dk.astype(BF16)
                    tail = jnp.sum(dk_out * (kc * e_out), axis=0, keepdims=True) + ddec * dec
                    db_s[rows, ksl] = (qc * dq - kc * dk) + jnp.where(row == last, tail, 0.0)
                    ds = ds * dec + g_t[:, c * DK:(c + 1) * DK]
                ds_scr[d * HEADS + h] = ds
            dg = _dot_split3(cum_t.astype(BF16), db_s[...])
            dlogit = (dg * GATE_SCALE) * _sigmoid(-lg_s[...])
            dlb = dlogit.astype(BF16)
            dlr_r[...] = _dot_nt(dlb, wv)
            dw_r[...] += _dot_tn(lrv, dlb)
            db_r[...] += jnp.sum(dlogit, axis=0, keepdims=True)

    fw = lambda i: (nb - 1 - i, 0)
    bw = lambda i: (i, 0)
    const = lambda i: (0, 0)

    def tok_specs(m):
        return [pl.BlockSpec((tt, QK_W), lambda i: (m(i)[0], OFF_Q // QK_W)),
                pl.BlockSpec((tt, QK_W), lambda i: (m(i)[0], OFF_K // QK_W)),
                pl.BlockSpec((tt, V_W), lambda i: (m(i)[0], OFF_V // V_W)),
                pl.BlockSpec((tt, LR_W), m),
                pl.BlockSpec((tt, V_W), m),
                pl.BlockSpec((nc, HEADS, DV, DK), lambda i: (m(i)[0], 0, 0, 0))]

    dqkv = jax.ShapeDtypeStruct((seq, QK_W + QK_W + V_W), BF16)
    dlr = jax.ShapeDtypeStruct((seq, LR_W), F32)
    dw = jax.ShapeDtypeStruct((LR_W, QK_W), F32)
    dbias = jax.ShapeDtypeStruct((1, QK_W), F32)
    return pl.pallas_call(
        body, name="gla_bwd",
        out_shape=(dqkv, dlr, dqkv, dlr, dw, dw, dbias, dbias),
        grid=(nb,),
        in_specs=tok_specs(fw) + tok_specs(bw) + [
            pl.BlockSpec((LR_W, QK_W), const), pl.BlockSpec((LR_W, QK_W), const),
            pl.BlockSpec((1, QK_W), const), pl.BlockSpec((1, QK_W), const)],
        out_specs=(pl.BlockSpec((tt, QK_W + QK_W + V_W), fw), pl.BlockSpec((tt, LR_W), fw),
                   pl.BlockSpec((tt, QK_W + QK_W + V_W), bw), pl.BlockSpec((tt, LR_W), bw),
                   pl.BlockSpec((LR_W, QK_W), const), pl.BlockSpec((LR_W, QK_W), const),
                   pl.BlockSpec((1, QK_W), const), pl.BlockSpec((1, QK_W), const)),
        scratch_shapes=[pltpu.VMEM((2 * HEADS, DV, DK), F32)] + [pltpu.VMEM((tt, QK_W), F32)] * 4
        + [pltpu.VMEM((tt, QK_W), BF16)] * 4 + [pltpu.VMEM((tt, QK_W), F32)] * 2,
        compiler_params=_cparams("arbitrary"),
    )(proj, proj, proj, lr, do, st_f, proj, proj, proj, lr, do, st_b, wgk_f, wgk_b, bgk_f, bgk_b)


def _both_directions(f_ref, b_ref):
    return (_f32(f_ref) + _f32(b_ref)).astype(BF16)


def _input_grad(dqkv_f, dqkv_b, dp_gates, dp_ch, dlr_f, dlr_b, w_nat, x2d, norm_g, dx2, sums, tm):
    seq = x2d.shape[0]
    nt, n = seq // tm, len(sums)
    relay_step = (3 * nt) // 8

    def body(dqf, dqb, dg, dc, dlf, dlb, w, x_ref, g_ref, dx2_ref, *rest):
        ins, (gx_ref, dng_ref), outs = rest[:n], rest[n:n + 2], rest[n + 2:2 * n + 2]
        passing, joined = rest[2 * n + 2:3 * n + 2], rest[3 * n + 2:4 * n + 2]
        send_sems, recv_sems, local_sems = rest[4 * n + 2:]
        i = pl.program_id(0)
        c = lax.axis_index("c")
        first, second, diagonal = _route_chips()
        slot = lambda chip: 2 * chip[0] + chip[1]

        def remote(a, k, src, dst, to):
            return pltpu.make_async_remote_copy(src_ref=src, dst_ref=dst, send_sem=send_sems.at[3 * a + k],
                                                recv_sem=recv_sems.at[3 * a + k], device_id=(*to, c),
                                                device_id_type=MESH)

        direct = lambda a: remote(a, 0, ins[a].at[slot(first)], outs[a].at[0], first)
        for_second = lambda a: remote(a, 1, ins[a].at[slot(diagonal)], passing[a], first)
        joint = lambda a: remote(a, 2, joined[a], outs[a].at[1], second)
        own = lambda a: pltpu.make_async_copy(ins[a].at[slot(second)], joined[a], local_sems.at[a])

        @pl.when(i == 0)
        def _():
            _start_all([for_second(a) for a in range(n)] + [own(a) for a in range(n)] + [direct(a) for a in range(n)])
            dng_ref[...] = jnp.zeros(dng_ref.shape, F32)

        @pl.when(i == relay_step)
        def _():
            for a in range(n):
                for_second(a).wait_recv()
                own(a).wait()
                joined[a][...] = (joined[a][...].astype(F32) + passing[a][...].astype(F32)).astype(BF16)
                joint(a).start()

        dh = (_dot((dlf[...] + dlb[...]).astype(BF16), w[NAT_LR:NAT_LR + LR_W, :])
              + _dot(_both_directions(dqf, dqb), w[0:NAT_ZA, :])
              + _dot(dg[:, 0:CONV_W], w[NAT_ZA:NAT_LR, :]) + _dot(dg[:, CONV_W:2 * CONV_W], w[NAT_B:NAT_C, :])
              + _dot(dg[:, 2 * CONV_W:], w[NAT_ZC:IN_W, :]) + _dot(dc[...], w[NAT_C:NAT_ZC, :]))
        xv = x_ref[...]
        r = lax.rsqrt(jnp.mean(xv * xv, axis=-1, keepdims=True) + EPS)
        xn = xv * r
        dng_ref[...] += jnp.sum(dh * xn, axis=0, keepdims=True)
        dn = dh * g_ref[...]
        gx_ref[...] = (r * dn - xn * (r * jnp.mean(dn * xn, axis=-1, keepdims=True))) + dx2_ref[...]

        @pl.when(i == nt - 1)
        def _():
            for a in range(n):
                direct(a).wait_recv()
                joint(a).wait_recv()
            for a in range(n):
                for cp in (direct(a), for_second(a), joint(a)):
                    cp.wait_send()

    rowt = pl.BlockSpec((tm, D_MODEL), lambda i: (i, 0))
    seg = lambda width: pl.BlockSpec((tm, width), lambda i: (i, 0))
    resident = lambda rows: pl.BlockSpec((rows, D_MODEL), lambda i: (0, 0), pipeline_mode=pl.Buffered(1))
    hbm = pl.BlockSpec(memory_space=pl.ANY)
    blocks = [pltpu.VMEM(s.shape[1:], s.dtype) for s in sums]
    return pl.pallas_call(
        body, name="input_grad",
        out_shape=(jax.ShapeDtypeStruct((seq, D_MODEL), F32), jax.ShapeDtypeStruct((1, D_MODEL), F32))
        + tuple(jax.ShapeDtypeStruct((2,) + s.shape[1:], s.dtype) for s in sums),
        grid=(nt,),
        in_specs=[seg(QKV_W), seg(QKV_W), seg(GATES_W), seg(CH_W), seg(LR_W), seg(LR_W), resident(IN_W),
                  rowt, pl.BlockSpec((1, D_MODEL), lambda i: (0, 0)), rowt] + [hbm] * n,
        out_specs=(rowt, pl.BlockSpec((1, D_MODEL), lambda i: (0, 0))) + (hbm,) * n,
        scratch_shapes=blocks + blocks + [pltpu.SemaphoreType.DMA((3 * n,)), pltpu.SemaphoreType.DMA((3 * n,)),
                                          pltpu.SemaphoreType.DMA((n,))],
        compiler_params=_cparams("arbitrary"),
    )(dqkv_f, dqkv_b, dp_gates, dp_ch, dlr_f, dlr_b, w_nat, x2d, norm_g, dx2, *sums)


def _weight_grad_out(y_t, dx2b, tk, riding):
    m, seq = y_t.shape
    n = dx2b.shape[1]
    nk = seq // tk

    def body(a_ref, b_ref, ride_in, o_ref, ride_out, send_sems, recv_sems):
        k = pl.program_id(0)

        @pl.when(k == 0)
        def _():
            _start_all(_sibling_copies(ride_in, ride_out, send_sems, recv_sems))
            o_ref[...] = jnp.zeros(o_ref.shape, F32)

        o_ref[...] += _dot(a_ref[...], b_ref[...])

        @pl.when(k == nk - 1)
        def _():
            _wait_all(_sibling_copies(ride_in, ride_out, send_sems, recv_sems))

    hbm = pl.BlockSpec(memory_space=pl.ANY)
    return pl.pallas_call(
        body, name="wgrad_out",
        out_shape=(jax.ShapeDtypeStruct((m, n), F32), jax.ShapeDtypeStruct((4,) + _block_shape(riding), F32)),
        grid=(nk,),
        in_specs=[pl.BlockSpec((m, tk), lambda k: (0, k)), pl.BlockSpec((tk, n), lambda k: (k, 0)), hbm],
        out_specs=(pl.BlockSpec((m, n), lambda k: (0, 0)), hbm),
        scratch_shapes=[pltpu.SemaphoreType.DMA((4,)), pltpu.SemaphoreType.DMA((4,))],
        compiler_params=_cparams("arbitrary"),
    )(y_t, dx2b, riding)


def _weight_grad_in(h_t, dqkv_f, dqkv_b, dp_gates, dp_ch, dlr_f, dlr_b):
    m, seq = h_t.shape
    tn = 512
    n_qkv, n_gates, n_ch = QKV_W // tn, GATES_W // tn, CH_W // tn
    starts = ([k * tn for k in range(n_qkv)] + [NAT_ZA, NAT_ZA + tn, NAT_B, NAT_B + tn, NAT_ZC, NAT_ZC + tn]
              + [NAT_C + k * tn for k in range(n_ch)])

    def out_row(j):
        row = 0
        for k, start in enumerate(starts):
            row = row + jnp.where(j == k, start // 32, 0)
        return pl.multiple_of(row * 32, 32), 0

    def body(a_ref, bqf, bqb, bg, bc, o_ref, acc, bq):
        j = pl.program_id(0)

        @pl.when(j < n_qkv)
        def _():
            bq[...] = _both_directions(bqf, bqb)
            acc[...] = _dot(a_ref[...], bq[...])

        @pl.when(jnp.logical_and(j >= n_qkv, j < n_qkv + n_gates))
        def _():
            acc[...] = _dot(a_ref[...], bg[...])

        @pl.when(j >= n_qkv + n_gates)
        def _():
            acc[...] = _dot(a_ref[...], bc[...])

        o_ref[...] = acc[...].T

    resident = pl.BlockSpec((m, seq), lambda j: (0, 0), pipeline_mode=pl.Buffered(1))
    seg = lambda first, count: pl.BlockSpec((seq, tn), lambda j: (0, jnp.clip(j - first, 0, count - 1)))
    main = pl.pallas_call(
        body, name="wgrad_in",
        out_shape=jax.ShapeDtypeStruct((IN_W, m), F32),
        grid=(n_qkv + n_gates + n_ch,),
        in_specs=[resident, seg(0, n_qkv), seg(0, n_qkv), seg(n_qkv, n_gates), seg(n_qkv + n_gates, n_ch)],
        out_specs=pl.BlockSpec((pl.Element(tn), pl.Element(m)), out_row),
        scratch_shapes=[pltpu.VMEM((m, tn), F32), pltpu.VMEM((seq, tn), BF16)],
        compiler_params=_cparams("arbitrary"),
    )(h_t, dqkv_f, dqkv_b, dp_gates, dp_ch)

    def lr_body(a_ref, bf_ref, bb_ref, full_ref, o_ref, acc):
        acc[...] = _dot(a_ref[...], (bf_ref[...] + bb_ref[...]).astype(BF16))
        o_ref[...] = acc[...].T[0:2 * RANK, :]

    whole = lambda shape: pl.BlockSpec(shape, lambda j: (0, 0))
    return pl.pallas_call(
        lr_body, name="wgrad_lr",
        out_shape=jax.ShapeDtypeStruct((IN_W, m), F32),
        grid=(1,),
        in_specs=[whole((m, seq)), whole((seq, LR_W)), whole((seq, LR_W)), pl.BlockSpec(memory_space=pl.ANY)],
        out_specs=pl.BlockSpec((pl.Element(2 * RANK), pl.Element(m)), lambda j: (NAT_LR, 0)),
        scratch_shapes=[pltpu.VMEM((m, LR_W), F32)],
        input_output_aliases={3: 0},
        compiler_params=_cparams("arbitrary"),
    )(h_t, dlr_f, dlr_b, main)


def _pad_rows(a, rows):
    return jnp.pad(a, ((0, rows - a.shape[0]), (0, 0)))


def _rows128(a):
    a = a.reshape(-1, 128)
    return _pad_rows(a, -(-a.shape[0] // 8) * 8)


def _pack(arrs):
    return jnp.concatenate([_rows128(a) for a in arrs], axis=0)


def _unpack(buf, like):
    out, start = [], 0
    for a in like:
        rows = a.size // 128
        out.append(buf[start:start + rows].reshape(a.shape))
        start += -(-rows // 8) * 8
    return out


def kernel(x, norm_g, w_in, w_gk_f, b_gk_f, w_gk_b, b_gk_b, gla_norm_g, conv_w, conv_b, w_out, final_g, loss_target, m_norm_g, m_w_in, m_w_gk_f, m_b_gk_f, m_w_gk_b, m_b_gk_b, m_gla_norm_g, m_conv_w, m_conv_b, m_w_out, m_final_g, v_norm_g, v_w_in, v_w_gk_f, v_b_gk_f, v_w_gk_b, v_b_gk_b, v_gla_norm_g, v_conv_w, v_conv_b, v_w_out, v_final_g):
    px, py, pc = _position()
    me = _blk(px, py, pc)
    seq = x.shape[1]
    x2d, tgt = x[0], loss_target[0]
    tt = min(256, seq)

    small_s = jnp.concatenate([jnp.concatenate([w_gk_f[0], w_gk_b[0]], axis=1), _pad_rows(conv_w[0], 8)], axis=0)
    order = sum(jnp.where(2 * px + py == k, jnp.asarray(tiles + (0,), jnp.int32), 0) for k, tiles in enumerate(TILE_ORDER))
    proj, lr, h_t, w_nat, wout_all, small_all = _gather_inproj(x2d, norm_g, w_in[0].T, w_out[0], small_s, order,
                                                               min(1024, seq))
    w_out_full = wout_all.reshape(MIX_W, D_MODEL)
    wgk_cols = 512 // N_DEV
    wgk_f_full = small_all[:, 0:RANK, 0:wgk_cols].transpose(1, 0, 2).reshape(RANK, QK_W)
    wgk_b_full = small_all[:, 0:RANK, wgk_cols:2 * wgk_cols].transpose(1, 0, 2).reshape(RANK, QK_W)
    conv_w_full = _pad_rows(small_all[:, RANK:RANK + 3, :].transpose(1, 0, 2).reshape(3, CONV_W), 8)
    zr = lambda n: jnp.zeros((n, QK_W), F32)
    wgk_f_pad = jnp.concatenate([wgk_f_full, zr(LR_W - RANK)], axis=0).astype(BF16)
    wgk_b_pad = jnp.concatenate([zr(RANK), wgk_b_full, zr(LR_W - 2 * RANK)], axis=0).astype(BF16)

    o_f, o_b, st_f, st_b = _gla_fwd(proj, lr, wgk_f_pad, wgk_b_pad, b_gk_f, b_gk_b, tt)
    tmix = min(256, seq)
    y_t, conv, dx2, dx2b, loss_p, dfg_p = _mix_out_loss(o_f, o_b, proj, x2d, tgt, gla_norm_g, conv_w_full, conv_b,
                                                        w_out_full, final_g.reshape(1, D_MODEL), tmix)

    dp_gates, do, dconv, dgg_p, dcb_p = _mix_bwd(dx2b, o_f, o_b, proj, conv, gla_norm_g, w_out_full, tmix)
    dp_ch, dcw_p = _conv_bwd(dconv, proj, conv_w_full, tmix)
    dqkv_f, dlr_f, dqkv_b, dlr_b, dwf_p, dwb_p, dbf_p, dbb_p = _gla_bwd(
        proj, lr, do, st_f, st_b, wgk_f_pad, wgk_b_pad, b_gk_f, b_gk_b, tt)
    dw_nat = _weight_grad_in(h_t, dqkv_f, dqkv_b, dp_gates, dp_ch, dlr_f, dlr_b)

    dw_out, sib_in = _weight_grad_out(y_t, dx2b, min(1024, seq), dw_nat)
    part_out = dw_out.reshape(N_DEV, MIX_W // N_DEV, D_MODEL)
    core = jnp.reshape(pc, (1,)).astype(jnp.int32)
    chip = jnp.reshape(2 * px + py, (1,)).astype(jnp.int32)
    sums_in, sib_out = _chip_sums(dw_nat, sib_in, core, 512, "chip_sums_in", riding=part_out)
    sums_out = _chip_sums(part_out, sib_out, core, D_MODEL, "chip_sums_out")
    grad_x2d, dng_p, far_in, far_out = _input_grad(dqkv_f, dqkv_b, dp_gates, dp_ch, dlr_f, dlr_b, w_nat, x2d, norm_g, dx2,
                                                   [sums_in, sums_out], tmix)
    pieces = [dng_p, dbf_p, dbb_p, dgg_p, dcb_p, dfg_p[0], dwf_p[0:RANK], dwb_p[RANK:2 * RANK], dcw_p[0:3], loss_p[0]]
    g_window, small_tot = _final_sum(sums_in, far_in, chip, _pack(pieces), 512, "final_sum_in")
    g_in_t = lax.dynamic_slice_in_dim(g_window, 4 * pc, SHARD_W, axis=0)
    g_w_out, d_w_out, nm_w_out, nv_w_out = _final_sum_adamw(sums_out, far_out, chip, w_out[0], m_w_out[0], v_w_out[0],
                                                            256, "adamw_out")
    flat = lambda a: a[0].T.reshape(SHARD_W, D_MODEL // 128, 128)
    unflat = lambda a: a.reshape(SHARD_W, D_MODEL).T
    d_flat, m_flat, v_flat = _adamw_rows(g_in_t.reshape(SHARD_W, D_MODEL // 128, 128), flat(w_in), flat(m_w_in),
                                         flat(v_w_in), 180, "adamw_in")
    g_w_in, d_w_in, nm_w_in, nv_w_in = g_in_t.T, unflat(d_flat), unflat(m_flat), unflat(v_flat)

    tot = _unpack(small_tot, pieces)
    g_norm_g, g_b_gk_f, g_b_gk_b, g_gla, g_conv_b, g_final = tot[:6]
    g_wgk_f = lax.dynamic_slice_in_dim(tot[6], me * wgk_cols, wgk_cols, axis=1)[None]
    g_wgk_b = lax.dynamic_slice_in_dim(tot[7], me * wgk_cols, wgk_cols, axis=1)[None]
    g_conv_w = lax.dynamic_slice_in_dim(tot[8], me * 128, 128, axis=1)[None]
    loss = tot[9][0]

    small_g = [g_norm_g, g_b_gk_f, g_b_gk_b, g_gla, g_conv_b, g_final, g_wgk_f, g_wgk_b, g_conv_w]
    small_w = [norm_g, b_gk_f, b_gk_b, gla_norm_g, conv_b, final_g, w_gk_f, w_gk_b, conv_w]
    small_m = [m_norm_g, m_b_gk_f, m_b_gk_b, m_gla_norm_g, m_conv_b, m_final_g, m_w_gk_f, m_w_gk_b, m_conv_w]
    small_v = [v_norm_g, v_b_gk_f, v_b_gk_b, v_gla_norm_g, v_conv_b, v_final_g, v_w_gk_f, v_w_gk_b, v_conv_w]
    d_s, m_s, v_s = _adamw_small(_pack(small_g), _pack(small_w), _pack(small_m), _pack(small_v))
    d_l, m_l, v_l = _unpack(d_s, small_w), _unpack(m_s, small_w), _unpack(v_s, small_w)

    def ordered(sm, big_in, big_out):
        return [sm[0], big_in[None], sm[6], sm[1], sm[7], sm[2], sm[3], sm[8], sm[4], big_out[None], sm[5]]

    grads = ordered(small_g, g_w_in, g_w_out)
    deltas = ordered(d_l, d_w_in, d_w_out)
    new_m = ordered(m_l, nm_w_in, nm_w_out)
    new_v = ordered(v_l, nv_w_in, nv_w_out)
    return (loss, grad_x2d[None], *grads, *deltas, *new_m, *new_v)
```

```python
import functools

import jax
import jax.numpy as jnp
from jax import lax
from jax.experimental import pallas as pl
from jax.experimental.pallas import tpu as pltpu

F32 = jnp.float32
BF16 = jnp.bfloat16
MESH = pl.DeviceIdType.MESH

N_DEV = 8
D_MODEL = 1024
HEADS = 4
DK = 128
DV = 256
QK_W = HEADS * DK
V_W = HEADS * DV
CONV_W = 1024
MIX_W = V_W + CONV_W
CHUNK = 64
RANK = 16
IN_W = 7200
SHARD_W = IN_W // N_DEV
MAIN_W = 7168
LR_W = 128
OFF_Q, OFF_K, OFF_V, OFF_ZA, OFF_B, OFF_ZC, OFF_C, OFF_H = 0, 512, 1024, 2048, 3072, 4096, 5120, 6144
QKV_W, GATES_W, CH_W = 2048, 3072, 2048
NAT_ZA, NAT_LR, NAT_B, NAT_C, NAT_ZC = 2048, 3072, 3104, 4128, 6176
EPS = 1e-6
GATE_SCALE = 1.0 / 16.0
QSCALE = DK ** -0.5
REF_F, LAST_F = CHUNK // 2, CHUNK - 1
REF_B, LAST_B = CHUNK - 1 - CHUNK // 2, 0

ADAM_LR = 0.001
ADAM_B1 = 0.9
ADAM_B2 = 0.999
ADAM_EPS = 1e-08
ADAM_WD = 0.01
ADAM_STEP = 10

VMEM_LIMIT = 56 * 1024 * 1024


def _cparams(*sem):
    return pltpu.CompilerParams(dimension_semantics=sem, vmem_limit_bytes=VMEM_LIMIT)


def _dot(a, b):
    return jnp.dot(a, b, preferred_element_type=F32)


def _dot_nt(a, b):
    return lax.dot_general(a, b, (((1,), (1,)), ((), ())), preferred_element_type=F32)


def _dot_tn(a, b):
    return lax.dot_general(a, b, (((0,), (0,)), ((), ())), preferred_element_type=F32)


def _sigmoid(z):
    return jax.nn.sigmoid(z)


def _position():
    return lax.axis_index("x"), lax.axis_index("y"), lax.axis_index("c")


def _blk(px, py, pc):
    return 4 * px + 2 * py + pc


EDGE = 16
SHIFTED_ROWS = 912
BODY_ROWS = SHIFTED_ROWS - 2 * EDGE


def _first_tile_row(blk, px):
    return EDGE * (56 * blk + px)


def _edge_tiles():
    tiles = {}
    for blk in range(N_DEV):
        first = _first_tile_row(blk, blk // 4)
        tiles.setdefault(first, []).append((blk, 0))
        tiles.setdefault(first + EDGE + BODY_ROWS, []).append((blk, 1))
    return tiles


def _peer_copies(srcs, outs, send_sems, recv_sems):
    x, y, c = _position()
    me = _blk(x, y, c)
    copies = []
    for a, (src, out) in enumerate(zip(srcs, outs)):
        k = 0
        for dx in (0, 1):
            for dy in (0, 1):
                for dc in (0, 1):
                    if dx + dy + dc == 0:
                        continue
                    peer = (1 - x if dx else x, 1 - y if dy else y, 1 - c if dc else c)
                    copies.append(pltpu.make_async_remote_copy(
                        src_ref=src, dst_ref=out.at[me], send_sem=send_sems.at[a * 7 + k],
                        recv_sem=recv_sems.at[a * 7 + k], device_id=peer, device_id_type=MESH))
                    k += 1
    return copies


def _route_chips():
    x, y, c = _position()
    along_x = c == 0
    return [(jnp.where(along_x, 1 - x, x), jnp.where(along_x, y, 1 - y)),
            (jnp.where(along_x, x, 1 - x), jnp.where(along_x, 1 - y, y)), (1 - x, 1 - y)]


WINDOW_ROWS = SHARD_W + 4


def _window_start(k, parity):
    return 2 * SHARD_W * k + (SHARD_W - 4) * parity


def _owner_block(part, k, parity):
    if part.ndim == 3:
        return part.at[2 * k + parity]
    return part.at[pl.ds(pl.multiple_of(_window_start(k, parity), 8), WINDOW_ROWS)]


def _block_shape(part):
    return part.shape[1:] if part.ndim == 3 else (WINDOW_ROWS, part.shape[1])


def _sibling_copies(part, out, send_sems, recv_sems):
    x, y, c = _position()
    return [pltpu.make_async_remote_copy(src_ref=_owner_block(part, k, 1 - c), dst_ref=out.at[k],
                                         send_sem=send_sems.at[k], recv_sem=recv_sems.at[k],
                                         device_id=(x, y, 1 - c), device_id_type=MESH)
            for k in range(4)]


def _start_all(copies):
    for cp in copies:
        cp.start()


def _wait_all(copies):
    for cp in copies:
        cp.wait_recv()
    for cp in copies:
        cp.wait_send()


def _chip_sums(part, from_sibling, core, tc, name, riding=None):
    rows, cols = _block_shape(part)
    nj = cols // tc

    def body(core_ref, p_ref, s_ref, *rest):
        if riding is None:
            (o_ref,) = rest
        else:
            ride_in, o_ref, ride_out, send_sems, recv_sems = rest
            k, j = pl.program_id(0), pl.program_id(1)

            @pl.when(jnp.logical_and(k == 0, j == 0))
            def _():
                _start_all(_sibling_copies(ride_in, ride_out, send_sems, recv_sems))

        o_ref[0] = (p_ref[...].reshape(rows, tc) + s_ref[0]).astype(BF16)

        if riding is not None:
            @pl.when(jnp.logical_and(k == 3, j == nj - 1))
            def _():
                _wait_all(_sibling_copies(ride_in, ride_out, send_sems, recv_sems))

    hbm = pl.BlockSpec(memory_space=pl.ANY)
    sums = jax.ShapeDtypeStruct((4, rows, cols), BF16)
    tile_out = pl.BlockSpec((1, rows, tc), lambda k, j, core_ref: (k, 0, j))
    if part.ndim == 3:
        mine = pl.BlockSpec((1, rows, tc), lambda k, j, core_ref: (2 * k + core_ref[0], 0, j))
    else:
        mine = pl.BlockSpec((pl.Element(rows), pl.Element(tc)),
                            lambda k, j, core_ref: (pl.multiple_of(_window_start(k, core_ref[0]), 8),
                                                    pl.multiple_of(j * tc, 128)))
    in_specs = [mine, pl.BlockSpec((1, rows, tc), lambda k, j, core_ref: (k, 0, j))]
    if riding is None:
        out_shape, out_specs, scratch, args = sums, tile_out, [], (core, part, from_sibling)
    else:
        out_shape = (sums, jax.ShapeDtypeStruct((4,) + _block_shape(riding), F32))
        out_specs, in_specs = (tile_out, hbm), in_specs + [hbm]
        scratch = [pltpu.SemaphoreType.DMA((4,)), pltpu.SemaphoreType.DMA((4,))]
        args = (core, part, from_sibling, riding)
    return pl.pallas_call(
        body, name=name, out_shape=out_shape,
        grid_spec=pltpu.PrefetchScalarGridSpec(num_scalar_prefetch=1, grid=(4, nj), in_specs=in_specs,
                                               out_specs=out_specs, scratch_shapes=scratch),
        compiler_params=_cparams("arbitrary", "arbitrary"),
    )(*args)


def _sum_chips(s_ref, r_ref):
    f = lambda a: a.astype(F32)
    return (f(s_ref[0]) + f(r_ref[0])) + f(r_ref[1])


def _final_sum(sums, from_chips, chip, small, tc, name):
    _, rows, cols = sums.shape
    nj = cols // tc

    def body(chip_ref, s_ref, r_ref, sm_ref, g_out, tot_ref, all_ref, send_sems, recv_sems):
        j = pl.program_id(0)
        me = _blk(*_position())

        @pl.when(j == 0)
        def _():
            all_ref[me] = sm_ref[...]
            _start_all(_peer_copies((all_ref.at[me],), (all_ref,), send_sems, recv_sems))

        g_out[...] = _sum_chips(s_ref, r_ref)

        @pl.when(j == nj - 1)
        def _():
            _wait_all(_peer_copies((all_ref.at[me],), (all_ref,), send_sems, recv_sems))
            acc = all_ref[0]
            for d in range(1, N_DEV):
                acc = acc + all_ref[d]
            tot_ref[...] = acc

    whole = pl.BlockSpec(small.shape, lambda j, chip_ref: (0, 0))
    return pl.pallas_call(
        body, name=name,
        out_shape=(jax.ShapeDtypeStruct((rows, cols), F32), jax.ShapeDtypeStruct(small.shape, F32)),
        grid_spec=pltpu.PrefetchScalarGridSpec(
            num_scalar_prefetch=1, grid=(nj,),
            in_specs=[pl.BlockSpec((1, rows, tc), lambda j, chip_ref: (chip_ref[0], 0, j)),
                      pl.BlockSpec((2, rows, tc), lambda j, chip_ref: (0, 0, j)), whole],
            out_specs=(pl.BlockSpec((rows, tc), lambda j, chip_ref: (0, j)), whole),
            scratch_shapes=[pltpu.VMEM((N_DEV,) + small.shape, F32), pltpu.SemaphoreType.DMA((7,)),
                            pltpu.SemaphoreType.DMA((7,))]),
        compiler_params=_cparams("arbitrary"),
    )(chip, sums, from_chips, small)


def _adamw_rows(g, w, m, v, tr, name):
    rows = g.shape[0]

    def body(g_ref, w_ref, m_ref, v_ref, d_out, m_out, v_out):
        delta, m_new, v_new = _adamw(w_ref[...], g_ref[...], m_ref[...], v_ref[...])
        d_out[...] = delta
        m_out[...] = m_new
        v_out[...] = v_new

    tile = pl.BlockSpec((tr,) + g.shape[1:], lambda r: (r, 0, 0))
    shp = jax.ShapeDtypeStruct(g.shape, F32)
    return pl.pallas_call(
        body, name=name, out_shape=(shp, shp, shp), grid=(rows // tr,),
        in_specs=[tile] * 4, out_specs=(tile, tile, tile),
        compiler_params=_cparams("arbitrary"),
    )(g, w, m, v)


def _adamw(w, g, m, v):
    m = ADAM_B1 * m + (1.0 - ADAM_B1) * g
    v = ADAM_B2 * v + (1.0 - ADAM_B2) * (g * g)
    m_hat = m / (1.0 - ADAM_B1 ** ADAM_STEP)
    v_hat = v / (1.0 - ADAM_B2 ** ADAM_STEP)
    delta = -ADAM_LR * (m_hat / (jnp.sqrt(v_hat) + ADAM_EPS) + ADAM_WD * w)
    return delta, m, v


def _final_sum_adamw(sums, from_chips, chip, w, m, v, tr, name):
    rows, cols = w.shape

    def body(chip_ref, s_ref, r_ref, w_ref, m_ref, v_ref, g_out, d_out, m_out, v_out):
        g = _sum_chips(s_ref, r_ref)
        delta, m_new, v_new = _adamw(w_ref[...], g, m_ref[...], v_ref[...])
        g_out[...] = g
        d_out[...] = delta
        m_out[...] = m_new
        v_out[...] = v_new

    tile = pl.BlockSpec((tr, cols), lambda r, chip_ref: (r, 0))
    shp = jax.ShapeDtypeStruct((rows, cols), F32)
    return pl.pallas_call(
        body, name=name,
        out_shape=(shp, shp, shp, shp),
        grid_spec=pltpu.PrefetchScalarGridSpec(
            num_scalar_prefetch=1, grid=(rows // tr,),
            in_specs=[pl.BlockSpec((1, tr, cols), lambda r, chip_ref: (chip_ref[0], r, 0)),
                      pl.BlockSpec((2, tr, cols), lambda r, chip_ref: (0, r, 0)),
                      tile, tile, tile],
            out_specs=(tile, tile, tile, tile)),
        compiler_params=_cparams("arbitrary"),
    )(chip, sums, from_chips, w, m, v)


def _adamw_small(g, w, m, v):
    def body(g_ref, w_ref, m_ref, v_ref, d_out, m_out, v_out):
        delta, m_new, v_new = _adamw(w_ref[...], g_ref[...], m_ref[...], v_ref[...])
        d_out[...] = delta
        m_out[...] = m_new
        v_out[...] = v_new

    vmem = pl.BlockSpec(memory_space=pltpu.VMEM)
    shp = jax.ShapeDtypeStruct(g.shape, F32)
    return pl.pallas_call(body, name="adamw_small", out_shape=(shp, shp, shp),
                          in_specs=[vmem] * 4, out_specs=(vmem, vmem, vmem))(g, w, m, v)


TILE_ROWS = (0, 1024, NAT_ZA, NAT_B, NAT_ZC, NAT_C, NAT_C + CONV_W)


TILE_ORDER = ((0, 1, 2, 3, 5, 6, 4), (2, 1, 0, 4, 3, 5, 6), (5, 6, 0, 4, 1, 2, 3), (4, 6, 2, 3, 5, 0, 1))
EARLY_SWEEP, NEIGHBOUR_SWEEP, DIAGONAL_SWEEP = 1, 2, 4
PIECES, W_IN_PIECES, OTHER_PIECES = 4, (0, 1), (2, 3)


def _gather_inproj(x2d, norm_g, shard_t, w_out_s, small_s, order, tm):
    seq = x2d.shape[0]
    tn = CONV_W
    ni, nj = seq // tm, MAIN_W // tn
    first_sweep = lambda j, i, order_ref: jnp.where(j == 0, i, ni - 1)
    last_sweep = lambda j, i, order_ref: jnp.where(j == nj - 1, i, 0)
    edge_tiles = _edge_tiles()

    def body(order_ref, x_ref, g_ref, shard_ref, wout_ref, sm_ref, proj_ref, lr_ref, ht_ref, w_nat, wout_all, sm_all,
             w_all, h_all, edges, stage, wout_b, sm_b, send_sems, recv_sems, local_sems):
        j, i = pl.program_id(0), pl.program_id(1)
        rows = pl.ds(pl.multiple_of(i * tm, tm), tm)
        x, y, c = _position()
        me, here, sibling = _blk(x, y, c), (x, y, c), (x, y, 1 - c)
        chips = _route_chips()
        sibling_chips = [chips[1], chips[0], chips[2]]

        def pieces(px, py, pc):
            blk = _blk(px, py, pc)
            body_rows = pl.ds(pl.multiple_of(_first_tile_row(blk, px) + EDGE, EDGE), BODY_ROWS)
            return [w_all.at[body_rows], edges.at[blk], wout_all.at[blk], sm_all.at[blk]]

        def copy(a, k, block, to, staged=None):
            ref = pieces(*block)[a]
            return pltpu.make_async_remote_copy(src_ref=ref if staged is None else staged, dst_ref=ref,
                                                send_sem=send_sems.at[a * 7 + k], recv_sem=recv_sems.at[a * 7 + k],
                                                device_id=to, device_id_type=MESH)

        def own_copies(group, slots=(0, 1, 2)):
            targets = [sibling] + [(*chips[n], c) for n in range(2)]
            staged = [None, None, wout_b, sm_b]
            return [copy(a, k, here, targets[k], staged[a]) for k in slots for a in group]

        def relays(group):
            return [copy(a, 3, (*chips[0], c), (*chips[1], c)) for a in group]

        def forwards(n, group):
            return [copy(a, 4 + n, (*chips[n], c), sibling) for a in group]

        def keep_own():
            return [pltpu.make_async_copy(wout_b, wout_all.at[me], local_sems.at[0]),
                    pltpu.make_async_copy(sm_b, sm_all.at[me], local_sems.at[1])]

        def keep_weight():
            return pltpu.make_async_copy(w_all, w_nat, local_sems.at[2])

        def take(ns, group, relay=True):
            for n in ns:
                for a in group:
                    copy(a, 1 + n, (*chips[n], c), here).wait_recv()
                _start_all((relays(group) if n == 0 and relay else []) + forwards(n, group))

        def take_passed_on(ns, group):
            for n in ns:
                for a in group:
                    copy(a, 4 + n, (*sibling_chips[n], 1 - c), here).wait_recv()

        def arrive(ns, group):
            take(ns, group)
            take_passed_on(ns, group)

        def per_core_and_row(step):
            for core in range(2):
                for row in range(2):
                    pl.when(jnp.logical_and(c == core, y == row))(functools.partial(step, core, row))

        def start_own(core, row):
            now = (0, 1 + core) if core == row else (0, 1, 2)
            _start_all(own_copies(W_IN_PIECES, now))
            wout_b[...] = wout_ref[...].astype(BF16)
            sm_b[...] = sm_ref[...]
            _start_all(own_copies(OTHER_PIECES, now) + keep_own())

        def take_early(core, row):
            if core == row:
                _start_all(own_copies(W_IN_PIECES, (2 - core,)) + own_copies(OTHER_PIECES, (2 - core,)))
                take((1 - core,), W_IN_PIECES, relay=False)
            else:
                take_passed_on((core,), W_IN_PIECES)

        def take_neighbours(core, row):
            if core == row:
                _start_all(relays(W_IN_PIECES) if core == 1 else [])
                take((core,), W_IN_PIECES)
                take_passed_on((0, 1), W_IN_PIECES)
            else:
                take((0, 1), W_IN_PIECES)
                take_passed_on((1 - core,), W_IN_PIECES)

        early_blk = _blk(x, 1 - y, y)

        def add_edge_tiles(stage):
            for row, parts in edge_tiles.items():
                ready = 0
                for blk, _ in parts:
                    away = (x != blk // 4).astype(jnp.int32) + (y != (blk // 2) % 2).astype(jnp.int32)
                    late = jnp.where(away == 1, jnp.where(early_blk == blk, 1, 2), jnp.where(away == 2, 3 + blk % 2, 0))
                    ready = jnp.maximum(ready, late)

                @pl.when(ready == stage)
                def _(row=row, parts=parts):
                    tile = edges[parts[0][0], parts[0][1]].astype(F32)
                    for blk, side in parts[1:]:
                        tile = tile + edges[blk, side].astype(F32)
                    w_all[row:row + EDGE, :] = tile.astype(BF16)

        @pl.when(jnp.logical_and(j == 0, i == 0))
        def _():
            last = SHARD_W // 8 * 8
            for col in range(0, D_MODEL, 128):
                cols = slice(col, col + 128)
                stage[0:last, :] = shard_ref[0:last, cols]
                stage[last:, :] = jnp.zeros((SHIFTED_ROWS - last, 128), F32)
                stage[last:SHARD_W, :] = shard_ref[last:SHARD_W, cols]
                for k in range(EDGE // 4):
                    @pl.when(me % 4 == k)
                    def _(k=k, cols=cols):
                        moved = pltpu.roll(stage[...], 4 * k, 0) if k else stage[...]
                        pieces(*here)[0][:, cols] = moved[EDGE:EDGE + BODY_ROWS].astype(BF16)
                        edges[me, 0, :, cols] = moved[0:EDGE].astype(BF16)
                        edges[me, 1, :, cols] = moved[EDGE + BODY_ROWS:].astype(BF16)
            per_core_and_row(start_own)
            for a in W_IN_PIECES:
                copy(a, 0, sibling, here).wait_recv()
            add_edge_tiles(0)

        @pl.when(jnp.logical_and(j == EARLY_SWEEP, i == 0))
        def _():
            per_core_and_row(take_early)
            add_edge_tiles(1)

        @pl.when(jnp.logical_and(j == NEIGHBOUR_SWEEP, i == 0))
        def _():
            per_core_and_row(take_neighbours)
            add_edge_tiles(2)

        for core in range(2):
            @pl.when(jnp.logical_and(j == DIAGONAL_SWEEP + core, i == 0))
            def _(core=core):
                pl.when(c == core)(lambda: take((2,), W_IN_PIECES))
                pl.when(c != core)(lambda: take_passed_on((2,), W_IN_PIECES))
                add_edge_tiles(3 + core)
                if core == 0:
                    arrive((0, 1), OTHER_PIECES)
                else:
                    keep_weight().start()

        @pl.when(jnp.logical_and(j == nj - 1, i == 0))
        def _():
            arrive((2,), OTHER_PIECES)

        @pl.when(j == 0)
        def _():
            xv = x_ref[...]
            r = lax.rsqrt(jnp.mean(xv * xv, axis=-1, keepdims=True) + EPS)
            h = (xv * r) * g_ref[...]
            h_all[rows, :] = h.astype(BF16)
            ht_ref[...] = h.T.astype(BF16)

        tile = order_ref[j]
        row = 0
        for k, start in enumerate(TILE_ROWS):
            row = row + jnp.where(tile == k, start // 32, 0)
        w_tile = w_all[pl.ds(pl.multiple_of(row * 32, 32), tn), :]
        proj_ref[...] = _dot_nt(h_all[rows, :], w_tile).astype(BF16)

        @pl.when(j == nj - 1)
        def _():
            lr_ref[...] = _dot_nt(h_all[rows, :], w_all[NAT_LR:NAT_LR + LR_W, :])

        @pl.when(jnp.logical_and(j == nj - 1, i == ni - 1))
        def _():
            everything = range(PIECES)
            passed_on = [cp for n in range(3) for cp in forwards(n, everything)]
            for cp in own_copies(everything) + relays(everything) + passed_on:
                cp.wait_send()
            for a in OTHER_PIECES:
                copy(a, 0, sibling, here).wait_recv()
            for cp in keep_own() + [keep_weight()]:
                cp.wait()

    const = lambda shape: pl.BlockSpec(shape, lambda j, i, order_ref: (0,) * len(shape))
    hbm = pl.BlockSpec(memory_space=pl.ANY)
    vmem = pl.BlockSpec(memory_space=pltpu.VMEM)
    return pl.pallas_call(
        body, name="gather_inproj",
        out_shape=(jax.ShapeDtypeStruct((seq, MAIN_W), BF16), jax.ShapeDtypeStruct((seq, LR_W), F32),
                   jax.ShapeDtypeStruct((D_MODEL, seq), BF16), jax.ShapeDtypeStruct((IN_W, D_MODEL), BF16),
                   jax.ShapeDtypeStruct((N_DEV,) + w_out_s.shape, BF16),
                   jax.ShapeDtypeStruct((N_DEV,) + small_s.shape, F32)),
        grid_spec=pltpu.PrefetchScalarGridSpec(
            num_scalar_prefetch=1, grid=(nj, ni),
            in_specs=[pl.BlockSpec((tm, D_MODEL), lambda j, i, order_ref: (first_sweep(j, i, order_ref), 0)),
                      const((1, D_MODEL)), vmem, vmem, const(small_s.shape)],
            out_specs=(pl.BlockSpec((tm, tn), lambda j, i, order_ref: (i, order_ref[j])),
                       pl.BlockSpec((tm, LR_W), lambda j, i, order_ref: (last_sweep(j, i, order_ref), 0)),
                       pl.BlockSpec((D_MODEL, tm), lambda j, i, order_ref: (0, first_sweep(j, i, order_ref))),
                       hbm, hbm, hbm),
            scratch_shapes=[pltpu.VMEM((IN_W, D_MODEL), BF16), pltpu.VMEM((seq, D_MODEL), BF16),
                            pltpu.VMEM((N_DEV, 2, EDGE, D_MODEL), BF16), pltpu.VMEM((SHIFTED_ROWS, 128), F32),
                            pltpu.VMEM(w_out_s.shape, BF16), pltpu.VMEM(small_s.shape, F32),
                            pltpu.SemaphoreType.DMA((7 * PIECES,)), pltpu.SemaphoreType.DMA((7 * PIECES,)),
                            pltpu.SemaphoreType.DMA((3,))]),
        compiler_params=_cparams("arbitrary", "arbitrary"),
    )(order, x2d, norm_g, shard_t, w_out_s, small_s)


def _block_masks(tt):
    row = lax.broadcasted_iota(jnp.int32, (tt, tt), 0)
    col = lax.broadcasted_iota(jnp.int32, (tt, tt), 1)
    same = jnp.right_shift(row, 6) == jnp.right_shift(col, 6)
    return (jnp.logical_and(same, col <= row), jnp.logical_and(same, col >= row), jnp.logical_and(same, col > row))


def _dot_split3(ones_mat, x):
    x1 = x.astype(BF16)
    r1 = x - x1.astype(F32)
    x2 = r1.astype(BF16)
    x3 = (r1 - x2.astype(F32)).astype(BF16)
    return (_dot(ones_mat, x3) + _dot(ones_mat, x2)) + _dot(ones_mat, x1)


def _log_gate(logits):
    return (jnp.minimum(logits, 0.0) - jnp.log(1.0 + jnp.exp(-jnp.abs(logits)))) * GATE_SCALE


def _chunk_column_mask(tt):
    nc = tt // CHUNK
    row = lax.broadcasted_iota(jnp.int32, (tt, nc * DK), 0)
    col = lax.broadcasted_iota(jnp.int32, (tt, nc * DK), 1)
    return jnp.right_shift(row, 6) == jnp.right_shift(col, 7)


def _chunked(mask, x, nc):
    wide = jnp.concatenate([x] * nc, axis=1)
    return jnp.where(mask, wide, jnp.zeros_like(wide))


def _gla_fwd(proj, lr, wgk_f, wgk_b, bgk_f, bgk_b, tt):
    seq = proj.shape[0]
    nb, nc, nch = seq // tt, tt // CHUNK, seq // CHUNK

    def body(qf, kf, vf, lrf, qb, kb, vb, lrb, wf, wb, bf, bb, of, ob, stf, stb, s_scr, qs_s, ks_s, qin_s, kout_s):
        @pl.when(pl.program_id(0) == 0)
        def _():
            s_scr[...] = jnp.zeros(s_scr.shape, F32)

        low, upp, sup = _block_masks(tt)
        dirs = ((qf, kf, vf, lrf, wf, bf, of, stf, low, low, REF_F, LAST_F, list(range(nc))),
                (qb, kb, vb, lrb, wb, bb, ob, stb, upp, sup, REF_B, LAST_B, list(reversed(range(nc)))))
        for d, (q_r, k_r, v_r, lr_r, w_r, b_r, o_r, st_r, cum, mask, ref, last, order) in enumerate(dirs):
            logits = _dot(lr_r[...].astype(BF16), w_r[...]) + b_r[...]
            b = _dot_split3(cum.astype(BF16), _log_gate(logits))
            decs = []
            for c in range(nc):
                rows = slice(c * CHUNK, (c + 1) * CHUNK)
                bc = b[rows]
                b_ref, b_last = bc[ref:ref + 1], bc[last:last + 1]
                qc = q_r[rows, :].astype(F32) * QSCALE
                kc = k_r[rows, :].astype(F32)
                qs_s[rows, :] = (qc * jnp.exp(bc - b_ref)).astype(BF16)
                ks_s[rows, :] = (kc * jnp.exp(b_ref - bc)).astype(BF16)
                qin_s[rows, :] = (qc * jnp.exp(bc)).astype(BF16)
                kout_s[rows, :] = (kc * jnp.exp(b_last - bc)).astype(BF16)
                decs.append(jnp.exp(b_last))
            for h in range(HEADS):
                ksl = slice(h * DK, (h + 1) * DK)
                vsl = slice(h * DV, (h + 1) * DV)
                v = v_r[:, vsl].astype(BF16)
                att = jnp.where(mask, _dot_nt(qs_s[:, ksl], ks_s[:, ksl]), 0.0).astype(BF16)
                o_intra = _dot(att, v)
                st = s_scr[d * HEADS + h]
                for c in order:
                    rows = slice(c * CHUNK, (c + 1) * CHUNK)
                    stb = st.astype(BF16)
                    st_r[c, h] = stb
                    o_r[rows, vsl] = (o_intra[rows] + _dot_nt(qin_s[rows, ksl], stb)).astype(BF16)
                    st = st * decs[c][:, ksl] + _dot_tn(v[rows], kout_s[rows, ksl])
                s_scr[d * HEADS + h] = st

    fw = lambda i: (i, 0)
    bw = lambda i: (nb - 1 - i, 0)
    const = lambda i: (0, 0)

    def tok_specs(m):
        return [pl.BlockSpec((tt, QK_W), lambda i: (m(i)[0], OFF_Q // QK_W)),
                pl.BlockSpec((tt, QK_W), lambda i: (m(i)[0], OFF_K // QK_W)),
                pl.BlockSpec((tt, V_W), lambda i: (m(i)[0], OFF_V // V_W)),
                pl.BlockSpec((tt, LR_W), m)]

    st_shape = jax.ShapeDtypeStruct((nch, HEADS, DV, DK), BF16)
    o_shape = jax.ShapeDtypeStruct((seq, V_W), BF16)
    operand = pltpu.VMEM((tt, QK_W), BF16)
    return pl.pallas_call(
        body, name="gla_fwd",
        out_shape=(o_shape, o_shape, st_shape, st_shape),
        grid=(nb,),
        in_specs=tok_specs(fw) + tok_specs(bw) + [
            pl.BlockSpec((LR_W, QK_W), const), pl.BlockSpec((LR_W, QK_W), const),
            pl.BlockSpec((1, QK_W), const), pl.BlockSpec((1, QK_W), const)],
        out_specs=(pl.BlockSpec((tt, V_W), fw), pl.BlockSpec((tt, V_W), bw),
                   pl.BlockSpec((nc, HEADS, DV, DK), lambda i: (i, 0, 0, 0)),
                   pl.BlockSpec((nc, HEADS, DV, DK), lambda i: (nb - 1 - i, 0, 0, 0))),
        scratch_shapes=[pltpu.VMEM((2 * HEADS, DV, DK), F32), operand, operand, operand, operand],
        compiler_params=_cparams("arbitrary"),
    )(proj, proj, proj, lr, proj, proj, proj, lr, wgk_f, wgk_b, bgk_f, bgk_b)


def _head_norm(o, gain):
    outs, rinv = [], []
    for h in range(HEADS):
        oh = o[:, h * DV:(h + 1) * DV]
        r = lax.rsqrt(jnp.mean(oh * oh, axis=-1, keepdims=True) + EPS)
        outs.append((oh * r) * gain)
        rinv.append(r)
    return jnp.concatenate(outs, axis=1), rinv


def _shift_rows(u, prev_row, next_row):
    n = u.shape[0]
    row = lax.broadcasted_iota(jnp.int32, (n, 1), 0)
    up = jnp.where(row == 0, prev_row, pltpu.roll(u, 1, 0))
    un = jnp.where(row == n - 1, next_row, pltpu.roll(u, n - 1, 0))
    return up, un


HALO = 16


def _halo_specs(tm, seq, col_block):
    per = tm // HALO
    last = seq // HALO - 1
    return [pl.BlockSpec((HALO, CONV_W), lambda i: (jnp.maximum(i * per - 1, 0), col_block)),
            pl.BlockSpec((HALO, CONV_W), lambda i: (jnp.minimum((i + 1) * per, last), col_block))]


def _f32(ref):
    return ref[...].astype(F32)


def _last_row(ref):
    return ref[HALO - 1:HALO, :].astype(F32)


def _first_row(ref):
    return ref[0:1, :].astype(F32)


def _mix_out_loss(o_f, o_b, proj, x2d, tgt, gla_g, conv_w, conv_b, w_out, final_g, tm):
    seq = x2d.shape[0]
    nt = seq // tm

    def body(of, ob, za, bg, cg, hc, zc, cprev, cnext, hprev, hnext, x_ref, t_ref, gg, cw, cb, wo, fg,
             yt_ref, conv_ref, dx2_ref, dx2b_ref, loss_ref, dfg_ref):
        i = pl.program_id(0)

        @pl.when(i == 0)
        def _():
            loss_ref[...] = jnp.zeros(loss_ref.shape, F32)
            dfg_ref[...] = jnp.zeros(dfg_ref.shape, F32)

        on, _ = _head_norm(_f32(of) + _f32(ob), gg[...])
        zav = _f32(za)
        y_a = on * (zav * _sigmoid(zav))
        u = _f32(cg) * _f32(hc)
        prev_row = jnp.where(i > 0, _last_row(cprev) * _last_row(hprev), 0.0)
        next_row = jnp.where(i < nt - 1, _first_row(cnext) * _first_row(hnext), 0.0)
        up, un = _shift_rows(u, prev_row, next_row)
        conv = (cw[0:1, :] * up + cw[1:2, :] * u + cw[2:3, :] * un) + cb[...]
        conv_ref[...] = conv.astype(BF16)
        zcv = _f32(zc)
        y_c = _f32(bg) * conv * (zcv * _sigmoid(zcv))
        y = jnp.concatenate([y_a, y_c], axis=1)
        yt_ref[...] = y.T.astype(BF16)
        x2 = x_ref[...] + _dot(y.astype(BF16), wo[...])
        r = lax.rsqrt(jnp.mean(x2 * x2, axis=-1, keepdims=True) + EPS)
        xn = x2 * r
        err = xn * fg[...] - t_ref[...]
        loss_ref[...] += 0.5 * jnp.sum(jnp.mean(err * err, axis=-1, keepdims=True))
        dyf = err * (1.0 / D_MODEL)
        dfg_ref[...] += jnp.sum(dyf * xn, axis=0, keepdims=True)
        dxn = dyf * fg[...]
        dx2 = r * dxn - xn * (r * jnp.mean(dxn * xn, axis=-1, keepdims=True))
        dx2_ref[...] = dx2
        dx2b_ref[...] = dx2.astype(BF16)

    def col(off):
        return pl.BlockSpec((tm, CONV_W), lambda i: (i, off // CONV_W))

    rowt = pl.BlockSpec((tm, D_MODEL), lambda i: (i, 0))
    const = lambda shape: pl.BlockSpec(shape, lambda i: (0, 0))
    return pl.pallas_call(
        body, name="mix_out_loss",
        out_shape=(jax.ShapeDtypeStruct((MIX_W, seq), BF16), jax.ShapeDtypeStruct((seq, CONV_W), BF16),
                   jax.ShapeDtypeStruct((seq, D_MODEL), F32), jax.ShapeDtypeStruct((seq, D_MODEL), BF16),
                   jax.ShapeDtypeStruct((8, 128), F32), jax.ShapeDtypeStruct((1, D_MODEL), F32)),
        grid=(nt,),
        in_specs=[rowt, rowt, col(OFF_ZA), col(OFF_B), col(OFF_C), col(OFF_H), col(OFF_ZC)]
        + _halo_specs(tm, seq, OFF_C // CONV_W) + _halo_specs(tm, seq, OFF_H // CONV_W)
        + [rowt, rowt, const((1, DV)), const((8, CONV_W)), const((1, CONV_W)), const((MIX_W, D_MODEL)),
           const((1, D_MODEL))],
        out_specs=(pl.BlockSpec((MIX_W, tm), lambda i: (0, i)), rowt, rowt, rowt, const((8, 128)),
                   const((1, D_MODEL))),
        compiler_params=_cparams("arbitrary"),
    )(o_f, o_b, proj, proj, proj, proj, proj, proj, proj, proj, proj, x2d, tgt, gla_g, conv_w, conv_b, w_out, final_g)


def _dsilu(z, s):
    return s * (1.0 + z * (1.0 - s))


def _mix_bwd(dx2b, o_f, o_b, proj, conv, gla_g, w_out, tm):
    seq = dx2b.shape[0]

    def body(dx, of, ob, za, bg, zc, cv, gg, wo, dg_ref, do_ref, dconv_ref, dgg_ref, dcb_ref):
        @pl.when(pl.program_id(0) == 0)
        def _():
            dgg_ref[...] = jnp.zeros(dgg_ref.shape, F32)
            dcb_ref[...] = jnp.zeros(dcb_ref.shape, F32)

        dy = _dot_nt(dx[...], wo[...])
        dy_a, dy_c = dy[:, :V_W], dy[:, V_W:]
        zcv, bgv, convv = _f32(zc), _f32(bg), _f32(cv)
        sc = _sigmoid(zcv)
        szc = zcv * sc
        dg_ref[:, CONV_W:2 * CONV_W] = (dy_c * convv * szc).astype(BF16)
        dconv = dy_c * bgv * szc
        dconv_ref[...] = dconv.astype(BF16)
        dcb_ref[...] += jnp.sum(dconv, axis=0, keepdims=True)
        dg_ref[:, 2 * CONV_W:] = (dy_c * bgv * convv * _dsilu(zcv, sc)).astype(BF16)

        o = _f32(of) + _f32(ob)
        gain = gg[...]
        on, rinv = _head_norm(o, gain)
        zav = _f32(za)
        sa = _sigmoid(zav)
        dg_ref[:, :CONV_W] = (dy_a * on * _dsilu(zav, sa)).astype(BF16)
        don = dy_a * (zav * sa)
        dgg = jnp.zeros((1, DV), F32)
        dos = []
        for h in range(HEADS):
            sl = slice(h * DV, (h + 1) * DV)
            oh, r, dh = o[:, sl], rinv[h], don[:, sl]
            ohn = oh * r
            dgg = dgg + jnp.sum(dh * ohn, axis=0, keepdims=True)
            dn = dh * gain
            dos.append(r * dn - ohn * (r * jnp.mean(dn * ohn, axis=-1, keepdims=True)))
        dgg_ref[...] += dgg
        do_ref[...] = jnp.concatenate(dos, axis=1).astype(BF16)

    def col(off):
        return pl.BlockSpec((tm, CONV_W), lambda i: (i, off // CONV_W))

    rowt = pl.BlockSpec((tm, D_MODEL), lambda i: (i, 0))
    const = lambda shape: pl.BlockSpec(shape, lambda i: (0, 0))
    return pl.pallas_call(
        body, name="mix_bwd",
        out_shape=(jax.ShapeDtypeStruct((seq, GATES_W), BF16), jax.ShapeDtypeStruct((seq, V_W), BF16),
                   jax.ShapeDtypeStruct((seq, CONV_W), BF16),
                   jax.ShapeDtypeStruct((1, DV), F32), jax.ShapeDtypeStruct((1, CONV_W), F32)),
        grid=(seq // tm,),
        in_specs=[rowt, rowt, rowt, col(OFF_ZA), col(OFF_B), col(OFF_ZC), rowt, const((1, DV)),
                  const((MIX_W, D_MODEL))],
        out_specs=(pl.BlockSpec((tm, GATES_W), lambda i: (i, 0)), rowt, rowt, const((1, DV)), const((1, CONV_W))),
        compiler_params=_cparams("arbitrary"),
    )(dx2b, o_f, o_b, proj, proj, proj, conv, gla_g, w_out)


def _conv_bwd(dconv, proj, conv_w, tm):
    seq = dconv.shape[0]
    nt = seq // tm

    def body(dc_in, dprev, dnext, cg, hc, cprev, cnext, hprev, hnext, cw, dch_ref, dcw_ref):
        i = pl.program_id(0)

        @pl.when(i == 0)
        def _():
            dcw_ref[...] = jnp.zeros(dcw_ref.shape, F32)

        first, lastt = i > 0, i < nt - 1
        dcv = _f32(dc_in)
        d_up, d_un = _shift_rows(dcv, jnp.where(first, _last_row(dprev), 0.0), jnp.where(lastt, _first_row(dnext), 0.0))
        cgv, hcv = _f32(cg), _f32(hc)
        u = cgv * hcv
        u_up, u_un = _shift_rows(u, jnp.where(first, _last_row(cprev) * _last_row(hprev), 0.0),
                                 jnp.where(lastt, _first_row(cnext) * _first_row(hnext), 0.0))
        du = cw[0:1, :] * d_un + cw[1:2, :] * dcv + cw[2:3, :] * d_up
        dch_ref[:, :CONV_W] = (du * hcv).astype(BF16)
        dch_ref[:, CONV_W:] = (du * cgv).astype(BF16)
        dcw_ref[0:1, :] += jnp.sum(dcv * u_up, axis=0, keepdims=True)
        dcw_ref[1:2, :] += jnp.sum(dcv * u, axis=0, keepdims=True)
        dcw_ref[2:3, :] += jnp.sum(dcv * u_un, axis=0, keepdims=True)

    def col(off):
        return pl.BlockSpec((tm, CONV_W), lambda i: (i, off // CONV_W))

    rowt = pl.BlockSpec((tm, CONV_W), lambda i: (i, 0))
    const = lambda shape: pl.BlockSpec(shape, lambda i: (0, 0))
    return pl.pallas_call(
        body, name="conv_bwd",
        out_shape=(jax.ShapeDtypeStruct((seq, CH_W), BF16), jax.ShapeDtypeStruct((8, CONV_W), F32)),
        grid=(nt,),
        in_specs=[rowt] + _halo_specs(tm, seq, 0) + [col(OFF_C), col(OFF_H)]
        + _halo_specs(tm, seq, OFF_C // CONV_W) + _halo_specs(tm, seq, OFF_H // CONV_W) + [const((8, CONV_W))],
        out_specs=(pl.BlockSpec((tm, CH_W), lambda i: (i, 0)), const((8, CONV_W))),
        compiler_params=_cparams("arbitrary"),
    )(dconv, dconv, dconv, proj, proj, proj, proj, proj, proj, conv_w)


def _gla_bwd(proj, lr, do, st_f, st_b, wgk_f, wgk_b, bgk_f, bgk_b, tt):
    seq = proj.shape[0]
    nb, nc = seq // tt, tt // CHUNK

    def body(qf, kf, vf, lrf, dof, stf, qb, kb, vb, lrb, dob, stb, wf, wb, bf, bb,
             dqkv_f, dlr_f, dqkv_b, dlr_b, dwf, dwb, dbf, dbb,
             ds_scr, eq_s, ek_s, ein_s, eout_s, qs_s, ks_s, qin_s, kout_s, db_s, lg_s):
        @pl.when(pl.program_id(0) == 0)
        def _():
            ds_scr[...] = jnp.zeros(ds_scr.shape, F32)
            for r in (dwf, dwb, dbf, dbb):
                r[...] = jnp.zeros(r.shape, F32)

        low, upp, sup = _block_masks(tt)
        row = lax.broadcasted_iota(jnp.int32, (CHUNK, 1), 0)
        kmask = _chunk_column_mask(tt)
        dirs = ((qf, kf, vf, lrf, dof, stf, wf, bf, dqkv_f, dlr_f, dwf, dbf,
                 low, upp, low, REF_F, LAST_F, list(reversed(range(nc)))),
                (qb, kb, vb, lrb, dob, stb, wb, bb, dqkv_b, dlr_b, dwb, dbb,
                 upp, low, sup, REF_B, LAST_B, list(range(nc))))
        for d, (q_r, k_r, v_r, lr_r, do_r, st_r, w_r, b_r, dqkv_r, dlr_r, dw_r, db_r,
                cum, cum_t, mask, ref, last, order) in enumerate(dirs):
            lrv = lr_r[...].astype(BF16)
            wv = w_r[...]
            logits = _dot(lrv, wv) + b_r[...]
            lg_s[...] = logits
            b = _dot_split3(cum.astype(BF16), _log_gate(logits))
            decs = []
            for c in range(nc):
                rows = slice(c * CHUNK, (c + 1) * CHUNK)
                bc = b[rows]
                b_ref, b_last = bc[ref:ref + 1], bc[last:last + 1]
                qc = q_r[rows, :].astype(F32) * QSCALE
                kc = k_r[rows, :].astype(F32)
                e_q, e_k, e_in, e_out = jnp.exp(bc - b_ref), jnp.exp(b_ref - bc), jnp.exp(bc), jnp.exp(b_last - bc)
                eq_s[rows, :], ek_s[rows, :], ein_s[rows, :], eout_s[rows, :] = e_q, e_k, e_in, e_out
                qs_s[rows, :] = (qc * e_q).astype(BF16)
                ks_s[rows, :] = (kc * e_k).astype(BF16)
                qin_s[rows, :] = (qc * e_in).astype(BF16)
                kout_s[rows, :] = (kc * e_out).astype(BF16)
                decs.append(jnp.exp(b_last))
            for h in range(HEADS):
                ksl = slice(h * DK, (h + 1) * DK)
                vsl = slice(h * DV, (h + 1) * DV)
                v = v_r[:, vsl].astype(BF16)
                dov = do_r[:, vsl].astype(BF16)
                qsb, ksb = qs_s[:, ksl], ks_s[:, ksl]
                att = jnp.where(mask, _dot_nt(qsb, ksb), 0.0).astype(BF16)
                datt = jnp.where(mask, _dot_nt(dov, v), 0.0).astype(BF16)
                dqs = _dot(datt, ksb)
                dks = _dot_tn(datt, qsb)
                dv_intra = _dot_tn(att, dov)
                g_t = _dot_tn(dov, _chunked(kmask, qin_s[:, ksl], nc))
                ds = ds_scr[d * HEADS + h]
                for c in order:
                    rows = slice(c * CHUNK, (c + 1) * CHUNK)
                    dsb = ds.astype(BF16)
                    s_prev = st_r[c, h]
                    dk_out = _dot(v[rows], dsb)
                    dq_in = _dot(dov[rows], s_prev)
                    dv = dv_intra[rows] + _dot_nt(kout_s[rows, ksl], dsb)
                    dqkv_r[rows, OFF_V + h * DV:OFF_V + (h + 1) * DV] = dv.astype(BF16)
                    dec = decs[c][:, ksl]
                    ddec = jnp.sum(ds * s_prev.astype(F32), axis=0, keepdims=True)
                    e_out = eout_s[rows, ksl]
                    qc = q_r[rows, ksl].astype(F32) * QSCALE
                    kc = k_r[rows, ksl].astype(F32)
                    dq = dqs[rows] * eq_s[rows, ksl] + dq_in * ein_s[rows, ksl]
                    dk = dks[rows] * ek_s[rows, ksl] + dk_out * e_out
                    dqkv_r[rows, OFF_Q + h * DK:OFF_Q + (h + 1) * DK] = (dq * QSCALE).astype(BF16)
                    dqkv_r[rows, OFF_K + h * DK:OFF_K + (h + 1) * DK] = dk.astype(BF16)
                    tail = jnp.sum(dk_out * (kc * e_out), axis=0, keepdims=True) + ddec * dec
                    db_s[rows, ksl] = (qc * dq - kc * dk) + jnp.where(row == last, tail, 0.0)
                    ds = ds * dec + g_t[:, c * DK:(c + 1) * DK]
                ds_scr[d * HEADS + h] = ds
            dg = _dot_split3(cum_t.astype(BF16), db_s[...])
            dlogit = (dg * GATE_SCALE) * _sigmoid(-lg_s[...])
            dlb = dlogit.astype(BF16)
            dlr_r[...] = _dot_nt(dlb, wv)
            dw_r[...] += _dot_tn(lrv, dlb)
            db_r[...] += jnp.sum(dlogit, axis=0, keepdims=True)

    fw = lambda i: (nb - 1 - i, 0)
    bw = lambda i: (i, 0)
    const = lambda i: (0, 0)

    def tok_specs(m):
        return [pl.BlockSpec((tt, QK_W), lambda i: (m(i)[0], OFF_Q // QK_W)),
                pl.BlockSpec((tt, QK_W), lambda i: (m(i)[0], OFF_K // QK_W)),
                pl.BlockSpec((tt, V_W), lambda i: (m(i)[0], OFF_V // V_W)),
                pl.BlockSpec((tt, LR_W), m),
                pl.BlockSpec((tt, V_W), m),
                pl.BlockSpec((nc, HEADS, DV, DK), lambda i: (m(i)[0], 0, 0, 0))]

    dqkv = jax.ShapeDtypeStruct((seq, QK_W + QK_W + V_W), BF16)
    dlr = jax.ShapeDtypeStruct((seq, LR_W), F32)
    dw = jax.ShapeDtypeStruct((LR_W, QK_W), F32)
    dbias = jax.ShapeDtypeStruct((1, QK_W), F32)
    return pl.pallas_call(
        body, name="gla_bwd",
        out_shape=(dqkv, dlr, dqkv, dlr, dw, dw, dbias, dbias),
        grid=(nb,),
        in_specs=tok_specs(fw) + tok_specs(bw) + [
            pl.BlockSpec((LR_W, QK_W), const), pl.BlockSpec((LR_W, QK_W), const),
            pl.BlockSpec((1, QK_W), const), pl.BlockSpec((1, QK_W), const)],
        out_specs=(pl.BlockSpec((tt, QK_W + QK_W + V_W), fw), pl.BlockSpec((tt, LR_W), fw),
                   pl.BlockSpec((tt, QK_W + QK_W + V_W), bw), pl.BlockSpec((tt, LR_W), bw),
                   pl.BlockSpec((LR_W, QK_W), const), pl.BlockSpec((LR_W, QK_W), const),
                   pl.BlockSpec((1, QK_W), const), pl.BlockSpec((1, QK_W), const)),
        scratch_shapes=[pltpu.VMEM((2 * HEADS, DV, DK), F32)] + [pltpu.VMEM((tt, QK_W), F32)] * 4
        + [pltpu.VMEM((tt, QK_W), BF16)] * 4 + [pltpu.VMEM((tt, QK_W), F32)] * 2,
        compiler_params=_cparams("arbitrary"),
    )(proj, proj, proj, lr, do, st_f, proj, proj, proj, lr, do, st_b, wgk_f, wgk_b, bgk_f, bgk_b)


def _both_directions(f_ref, b_ref):
    return (_f32(f_ref) + _f32(b_ref)).astype(BF16)


def _input_grad(dqkv_f, dqkv_b, dp_gates, dp_ch, dlr_f, dlr_b, w_nat, x2d, norm_g, dx2, sums, tm):
    seq = x2d.shape[0]
    nt, n = seq // tm, len(sums)
    relay_step = (3 * nt) // 8

    def body(dqf, dqb, dg, dc, dlf, dlb, w, x_ref, g_ref, dx2_ref, *rest):
        ins, (gx_ref, dng_ref), outs = rest[:n], rest[n:n + 2], rest[n + 2:2 * n + 2]
        passing, joined = rest[2 * n + 2:3 * n + 2], rest[3 * n + 2:4 * n + 2]
        send_sems, recv_sems, local_sems = rest[4 * n + 2:]
        i = pl.program_id(0)
        c = lax.axis_index("c")
        first, second, diagonal = _route_chips()
        slot = lambda chip: 2 * chip[0] + chip[1]

        def remote(a, k, src, dst, to):
            return pltpu.make_async_remote_copy(src_ref=src, dst_ref=dst, send_sem=send_sems.at[3 * a + k],
                                                recv_sem=recv_sems.at[3 * a + k], device_id=(*to, c),
                                                device_id_type=MESH)

        direct = lambda a: remote(a, 0, ins[a].at[slot(first)], outs[a].at[0], first)
        for_second = lambda a: remote(a, 1, ins[a].at[slot(diagonal)], passing[a], first)
        joint = lambda a: remote(a, 2, joined[a], outs[a].at[1], second)
        own = lambda a: pltpu.make_async_copy(ins[a].at[slot(second)], joined[a], local_sems.at[a])

        @pl.when(i == 0)
        def _():
            _start_all([for_second(a) for a in range(n)] + [own(a) for a in range(n)] + [direct(a) for a in range(n)])
            dng_ref[...] = jnp.zeros(dng_ref.shape, F32)

        @pl.when(i == relay_step)
        def _():
            for a in range(n):
                for_second(a).wait_recv()
                own(a).wait()
                joined[a][...] = (joined[a][...].astype(F32) + passing[a][...].astype(F32)).astype(BF16)
                joint(a).start()

        dh = (_dot((dlf[...] + dlb[...]).astype(BF16), w[NAT_LR:NAT_LR + LR_W, :])
              + _dot(_both_directions(dqf, dqb), w[0:NAT_ZA, :])
              + _dot(dg[:, 0:CONV_W], w[NAT_ZA:NAT_LR, :]) + _dot(dg[:, CONV_W:2 * CONV_W], w[NAT_B:NAT_C, :])
              + _dot(dg[:, 2 * CONV_W:], w[NAT_ZC:IN_W, :]) + _dot(dc[...], w[NAT_C:NAT_ZC, :]))
        xv = x_ref[...]
        r = lax.rsqrt(jnp.mean(xv * xv, axis=-1, keepdims=True) + EPS)
        xn = xv * r
        dng_ref[...] += jnp.sum(dh * xn, axis=0, keepdims=True)
        dn = dh * g_ref[...]
        gx_ref[...] = (r * dn - xn * (r * jnp.mean(dn * xn, axis=-1, keepdims=True))) + dx2_ref[...]

        @pl.when(i == nt - 1)
        def _():
            for a in range(n):
                direct(a).wait_recv()
                joint(a).wait_recv()
            for a in range(n):
                for cp in (direct(a), for_second(a), joint(a)):
                    cp.wait_send()

    rowt = pl.BlockSpec((tm, D_MODEL), lambda i: (i, 0))
    seg = lambda width: pl.BlockSpec((tm, width), lambda i: (i, 0))
    resident = lambda rows: pl.BlockSpec((rows, D_MODEL), lambda i: (0, 0), pipeline_mode=pl.Buffered(1))
    hbm = pl.BlockSpec(memory_space=pl.ANY)
    blocks = [pltpu.VMEM(s.shape[1:], s.dtype) for s in sums]
    return pl.pallas_call(
        body, name="input_grad",
        out_shape=(jax.ShapeDtypeStruct((seq, D_MODEL), F32), jax.ShapeDtypeStruct((1, D_MODEL), F32))
        + tuple(jax.ShapeDtypeStruct((2,) + s.shape[1:], s.dtype) for s in sums),
        grid=(nt,),
        in_specs=[seg(QKV_W), seg(QKV_W), seg(GATES_W), seg(CH_W), seg(LR_W), seg(LR_W), resident(IN_W),
                  rowt, pl.BlockSpec((1, D_MODEL), lambda i: (0, 0)), rowt] + [hbm] * n,
        out_specs=(rowt, pl.BlockSpec((1, D_MODEL), lambda i: (0, 0))) + (hbm,) * n,
        scratch_shapes=blocks + blocks + [pltpu.SemaphoreType.DMA((3 * n,)), pltpu.SemaphoreType.DMA((3 * n,)),
                                          pltpu.SemaphoreType.DMA((n,))],
        compiler_params=_cparams("arbitrary"),
    )(dqkv_f, dqkv_b, dp_gates, dp_ch, dlr_f, dlr_b, w_nat, x2d, norm_g, dx2, *sums)


def _weight_grad_out(y_t, dx2b, tk, riding):
    m, seq = y_t.shape
    n = dx2b.shape[1]
    nk = seq // tk

    def body(a_ref, b_ref, ride_in, o_ref, ride_out, send_sems, recv_sems):
        k = pl.program_id(0)

        @pl.when(k == 0)
        def _():
            _start_all(_sibling_copies(ride_in, ride_out, send_sems, recv_sems))
            o_ref[...] = jnp.zeros(o_ref.shape, F32)

        o_ref[...] += _dot(a_ref[...], b_ref[...])

        @pl.when(k == nk - 1)
        def _():
            _wait_all(_sibling_copies(ride_in, ride_out, send_sems, recv_sems))

    hbm = pl.BlockSpec(memory_space=pl.ANY)
    return pl.pallas_call(
        body, name="wgrad_out",
        out_shape=(jax.ShapeDtypeStruct((m, n), F32), jax.ShapeDtypeStruct((4,) + _block_shape(riding), F32)),
        grid=(nk,),
        in_specs=[pl.BlockSpec((m, tk), lambda k: (0, k)), pl.BlockSpec((tk, n), lambda k: (k, 0)), hbm],
        out_specs=(pl.BlockSpec((m, n), lambda k: (0, 0)), hbm),
        scratch_shapes=[pltpu.SemaphoreType.DMA((4,)), pltpu.SemaphoreType.DMA((4,))],
        compiler_params=_cparams("arbitrary"),
    )(y_t, dx2b, riding)


def _weight_grad_in(h_t, dqkv_f, dqkv_b, dp_gates, dp_ch, dlr_f, dlr_b):
    m, seq = h_t.shape
    tn = 512
    n_qkv, n_gates, n_ch = QKV_W // tn, GATES_W // tn, CH_W // tn
    starts = ([k * tn for k in range(n_qkv)] + [NAT_ZA, NAT_ZA + tn, NAT_B, NAT_B + tn, NAT_ZC, NAT_ZC + tn]
              + [NAT_C + k * tn for k in range(n_ch)])

    def out_row(j):
        row = 0
        for k, start in enumerate(starts):
            row = row + jnp.where(j == k, start // 32, 0)
        return pl.multiple_of(row * 32, 32), 0

    def body(a_ref, bqf, bqb, bg, bc, o_ref, acc, bq):
        j = pl.program_id(0)

        @pl.when(j < n_qkv)
        def _():
            bq[...] = _both_directions(bqf, bqb)
            acc[...] = _dot(a_ref[...], bq[...])

        @pl.when(jnp.logical_and(j >= n_qkv, j < n_qkv + n_gates))
        def _():
            acc[...] = _dot(a_ref[...], bg[...])

        @pl.when(j >= n_qkv + n_gates)
        def _():
            acc[...] = _dot(a_ref[...], bc[...])

        o_ref[...] = acc[...].T

    resident = pl.BlockSpec((m, seq), lambda j: (0, 0), pipeline_mode=pl.Buffered(1))
    seg = lambda first, count: pl.BlockSpec((seq, tn), lambda j: (0, jnp.clip(j - first, 0, count - 1)))
    main = pl.pallas_call(
        body, name="wgrad_in",
        out_shape=jax.ShapeDtypeStruct((IN_W, m), F32),
        grid=(n_qkv + n_gates + n_ch,),
        in_specs=[resident, seg(0, n_qkv), seg(0, n_qkv), seg(n_qkv, n_gates), seg(n_qkv + n_gates, n_ch)],
        out_specs=pl.BlockSpec((pl.Element(tn), pl.Element(m)), out_row),
        scratch_shapes=[pltpu.VMEM((m, tn), F32), pltpu.VMEM((seq, tn), BF16)],
        compiler_params=_cparams("arbitrary"),
    )(h_t, dqkv_f, dqkv_b, dp_gates, dp_ch)

    def lr_body(a_ref, bf_ref, bb_ref, full_ref, o_ref, acc):
        acc[...] = _dot(a_ref[...], (bf_ref[...] + bb_ref[...]).astype(BF16))
        o_ref[...] = acc[...].T[0:2 * RANK, :]

    whole = lambda shape: pl.BlockSpec(shape, lambda j: (0, 0))
    return pl.pallas_call(
        lr_body, name="wgrad_lr",
        out_shape=jax.ShapeDtypeStruct((IN_W, m), F32),
        grid=(1,),
        in_specs=[whole((m, seq)), whole((seq, LR_W)), whole((seq, LR_W)), pl.BlockSpec(memory_space=pl.ANY)],
        out_specs=pl.BlockSpec((pl.Element(2 * RANK), pl.Element(m)), lambda j: (NAT_LR, 0)),
        scratch_shapes=[pltpu.VMEM((m, LR_W), F32)],
        input_output_aliases={3: 0},
        compiler_params=_cparams("arbitrary"),
    )(h_t, dlr_f, dlr_b, main)


def _pad_rows(a, rows):
    return jnp.pad(a, ((0, rows - a.shape[0]), (0, 0)))


def _rows128(a):
    a = a.reshape(-1, 128)
    return _pad_rows(a, -(-a.shape[0] // 8) * 8)


def _pack(arrs):
    return jnp.concatenate([_rows128(a) for a in arrs], axis=0)


def _unpack(buf, like):
    out, start = [], 0
    for a in like:
        rows = a.size // 128
        out.append(buf[start:start + rows].reshape(a.shape))
        start += -(-rows // 8) * 8
    return out


def kernel(x, norm_g, w_in, w_gk_f, b_gk_f, w_gk_b, b_gk_b, gla_norm_g, conv_w, conv_b, w_out, final_g, loss_target, m_norm_g, m_w_in, m_w_gk_f, m_b_gk_f, m_w_gk_b, m_b_gk_b, m_gla_norm_g, m_conv_w, m_conv_b, m_w_out, m_final_g, v_norm_g, v_w_in, v_w_gk_f, v_b_gk_f, v_w_gk_b, v_b_gk_b, v_gla_norm_g, v_conv_w, v_conv_b, v_w_out, v_final_g):
    px, py, pc = _position()
    me = _blk(px, py, pc)
    seq = x.shape[1]
    x2d, tgt = x[0], loss_target[0]
    tt = min(256, seq)

    small_s = jnp.concatenate([jnp.concatenate([w_gk_f[0], w_gk_b[0]], axis=1), _pad_rows(conv_w[0], 8)], axis=0)
    order = sum(jnp.where(2 * px + py == k, jnp.asarray(tiles + (0,), jnp.int32), 0) for k, tiles in enumerate(TILE_ORDER))
    proj, lr, h_t, w_nat, wout_all, small_all = _gather_inproj(x2d, norm_g, w_in[0].T, w_out[0], small_s, order,
                                                               min(1024, seq))
    w_out_full = wout_all.reshape(MIX_W, D_MODEL)
    wgk_cols = 512 // N_DEV
    wgk_f_full = small_all[:, 0:RANK, 0:wgk_cols].transpose(1, 0, 2).reshape(RANK, QK_W)
    wgk_b_full = small_all[:, 0:RANK, wgk_cols:2 * wgk_cols].transpose(1, 0, 2).reshape(RANK, QK_W)
    conv_w_full = _pad_rows(small_all[:, RANK:RANK + 3, :].transpose(1, 0, 2).reshape(3, CONV_W), 8)
    zr = lambda n: jnp.zeros((n, QK_W), F32)
    wgk_f_pad = jnp.concatenate([wgk_f_full, zr(LR_W - RANK)], axis=0).astype(BF16)
    wgk_b_pad = jnp.concatenate([zr(RANK), wgk_b_full, zr(LR_W - 2 * RANK)], axis=0).astype(BF16)

    o_f, o_b, st_f, st_b = _gla_fwd(proj, lr, wgk_f_pad, wgk_b_pad, b_gk_f, b_gk_b, tt)
    tmix = min(512, seq)
    y_t, conv, dx2, dx2b, loss_p, dfg_p = _mix_out_loss(o_f, o_b, proj, x2d, tgt, gla_norm_g, conv_w_full, conv_b,
                                                        w_out_full, final_g.reshape(1, D_MODEL), tmix)

    dp_gates, do, dconv, dgg_p, dcb_p = _mix_bwd(dx2b, o_f, o_b, proj, conv, gla_norm_g, w_out_full, tmix)
    dp_ch, dcw_p = _conv_bwd(dconv, proj, conv_w_full, tmix)
    dqkv_f, dlr_f, dqkv_b, dlr_b, dwf_p, dwb_p, dbf_p, dbb_p = _gla_bwd(
        proj, lr, do, st_f, st_b, wgk_f_pad, wgk_b_pad, b_gk_f, b_gk_b, tt)
    dw_nat = _weight_grad_in(h_t, dqkv_f, dqkv_b, dp_gates, dp_ch, dlr_f, dlr_b)

    dw_out, sib_in = _weight_grad_out(y_t, dx2b, min(1024, seq), dw_nat)
    part_out = dw_out.reshape(N_DEV, MIX_W // N_DEV, D_MODEL)
    core = jnp.reshape(pc, (1,)).astype(jnp.int32)
    chip = jnp.reshape(2 * px + py, (1,)).astype(jnp.int32)
    sums_in, sib_out = _chip_sums(dw_nat, sib_in, core, D_MODEL, "chip_sums_in", riding=part_out)
    sums_out = _chip_sums(part_out, sib_out, core, D_MODEL, "chip_sums_out")
    grad_x2d, dng_p, far_in, far_out = _input_grad(dqkv_f, dqkv_b, dp_gates, dp_ch, dlr_f, dlr_b, w_nat, x2d, norm_g, dx2,
                                                   [sums_in, sums_out], min(256, seq))
    pieces = [dng_p, dbf_p, dbb_p, dgg_p, dcb_p, dfg_p[0], dwf_p[0:RANK], dwb_p[RANK:2 * RANK], dcw_p[0:3], loss_p[0]]
    g_window, small_tot = _final_sum(sums_in, far_in, chip, _pack(pieces), 512, "final_sum_in")
    g_in_t = lax.dynamic_slice_in_dim(g_window, 4 * pc, SHARD_W, axis=0)
    g_w_out, d_w_out, nm_w_out, nv_w_out = _final_sum_adamw(sums_out, far_out, chip, w_out[0], m_w_out[0], v_w_out[0],
                                                            256, "adamw_out")
    flat = lambda a: a[0].T.reshape(SHARD_W, D_MODEL // 128, 128)
    unflat = lambda a: a.reshape(SHARD_W, D_MODEL).T
    d_flat, m_flat, v_flat = _adamw_rows(g_in_t.reshape(SHARD_W, D_MODEL // 128, 128), flat(w_in), flat(m_w_in),
                                         flat(v_w_in), 300, "adamw_in")
    g_w_in, d_w_in, nm_w_in, nv_w_in = g_in_t.T, unflat(d_flat), unflat(m_flat), unflat(v_flat)

    tot = _unpack(small_tot, pieces)
    g_norm_g, g_b_gk_f, g_b_gk_b, g_gla, g_conv_b, g_final = tot[:6]
    g_wgk_f = lax.dynamic_slice_in_dim(tot[6], me * wgk_cols, wgk_cols, axis=1)[None]
    g_wgk_b = lax.dynamic_slice_in_dim(tot[7], me * wgk_cols, wgk_cols, axis=1)[None]
    g_conv_w = lax.dynamic_slice_in_dim(tot[8], me * 128, 128, axis=1)[None]
    loss = tot[9][0]

    small_g = [g_norm_g, g_b_gk_f, g_b_gk_b, g_gla, g_conv_b, g_final, g_wgk_f, g_wgk_b, g_conv_w]
    small_w = [norm_g, b_gk_f, b_gk_b, gla_norm_g, conv_b, final_g, w_gk_f, w_gk_b, conv_w]
    small_m = [m_norm_g, m_b_gk_f, m_b_gk_b, m_gla_norm_g, m_conv_b, m_final_g, m_w_gk_f, m_w_gk_b, m_conv_w]
    small_v = [v_norm_g, v_b_gk_f, v_b_gk_b, v_gla_norm_g, v_conv_b, v_final_g, v_w_gk_f, v_w_gk_b, v_conv_w]
    d_s, m_s, v_s = _adamw_small(_pack(small_g), _pack(small_w), _pack(small_m), _pack(small_v))
    d_l, m_l, v_l = _unpack(d_s, small_w), _unpack(m_s, small_w), _unpack(v_s, small_w)

    def ordered(sm, big_in, big_out):
        return [sm[0], big_in[None], sm[6], sm[1], sm[7], sm[2], sm[3], sm[8], sm[4], big_out[None], sm[5]]

    grads = ordered(small_g, g_w_in, g_w_out)
    deltas = ordered(d_l, d_w_in, d_w_out)
    new_m = ordered(m_l, nm_w_in, nm_w_out)
    new_v = ordered(v_l, nv_w_in, nv_w_out)
    return (loss, grad_x2d[None], *grads, *deltas, *new_m, *new_v)
```

```python
import functools

import jax
import jax.numpy as jnp
from jax import lax
from jax.experimental import pallas as pl
from jax.experimental.pallas import tpu as pltpu

F32 = jnp.float32
BF16 = jnp.bfloat16
MESH = pl.DeviceIdType.MESH

N_DEV = 8
D_MODEL = 1024
HEADS = 4
DK = 128
DV = 256
QK_W = HEADS * DK
V_W = HEADS * DV
CONV_W = 1024
MIX_W = V_W + CONV_W
CHUNK = 64
RANK = 16
IN_W = 7200
SHARD_W = IN_W // N_DEV
MAIN_W = 7168
LR_W = 128
OFF_Q, OFF_K, OFF_V, OFF_ZA, OFF_B, OFF_ZC, OFF_C, OFF_H = 0, 512, 1024, 2048, 3072, 4096, 5120, 6144
QKV_W, GATES_W, CH_W = 2048, 3072, 2048
NAT_ZA, NAT_LR, NAT_B, NAT_C, NAT_ZC = 2048, 3072, 3104, 4128, 6176
EPS = 1e-6
GATE_SCALE = 1.0 / 16.0
QSCALE = DK ** -0.5
REF_F, LAST_F = CHUNK // 2, CHUNK - 1
REF_B, LAST_B = CHUNK - 1 - CHUNK // 2, 0

ADAM_LR = 0.001
ADAM_B1 = 0.9
ADAM_B2 = 0.999
ADAM_EPS = 1e-08
ADAM_WD = 0.01
ADAM_STEP = 10

VMEM_LIMIT = 56 * 1024 * 1024


def _cparams(*sem):
    return pltpu.CompilerParams(dimension_semantics=sem, vmem_limit_bytes=VMEM_LIMIT)


def _dot(a, b):
    return jnp.dot(a, b, preferred_element_type=F32)


def _dot_nt(a, b):
    return lax.dot_general(a, b, (((1,), (1,)), ((), ())), preferred_element_type=F32)


def _dot_tn(a, b):
    return lax.dot_general(a, b, (((0,), (0,)), ((), ())), preferred_element_type=F32)


def _sigmoid(z):
    return jax.nn.sigmoid(z)


def _position():
    return lax.axis_index("x"), lax.axis_index("y"), lax.axis_index("c")


def _blk(px, py, pc):
    return 4 * px + 2 * py + pc


EDGE = 16
SHIFTED_ROWS = 912
BODY_ROWS = SHIFTED_ROWS - 2 * EDGE


def _first_tile_row(blk, px):
    return EDGE * (56 * blk + px)


def _edge_tiles():
    tiles = {}
    for blk in range(N_DEV):
        first = _first_tile_row(blk, blk // 4)
        tiles.setdefault(first, []).append((blk, 0))
        tiles.setdefault(first + EDGE + BODY_ROWS, []).append((blk, 1))
    return tiles


def _peer_copies(srcs, outs, send_sems, recv_sems):
    x, y, c = _position()
    me = _blk(x, y, c)
    copies = []
    for a, (src, out) in enumerate(zip(srcs, outs)):
        k = 0
        for dx in (0, 1):
            for dy in (0, 1):
                for dc in (0, 1):
                    if dx + dy + dc == 0:
                        continue
                    peer = (1 - x if dx else x, 1 - y if dy else y, 1 - c if dc else c)
                    copies.append(pltpu.make_async_remote_copy(
                        src_ref=src, dst_ref=out.at[me], send_sem=send_sems.at[a * 7 + k],
                        recv_sem=recv_sems.at[a * 7 + k], device_id=peer, device_id_type=MESH))
                    k += 1
    return copies


def _route_chips():
    x, y, c = _position()
    along_x = c == 0
    return [(jnp.where(along_x, 1 - x, x), jnp.where(along_x, y, 1 - y)),
            (jnp.where(along_x, x, 1 - x), jnp.where(along_x, 1 - y, y)), (1 - x, 1 - y)]


WINDOW_ROWS = SHARD_W + 4


def _window_start(k, parity):
    return 2 * SHARD_W * k + (SHARD_W - 4) * parity


def _owner_block(part, k, parity):
    if part.ndim == 3:
        return part.at[2 * k + parity]
    return part.at[pl.ds(pl.multiple_of(_window_start(k, parity), 8), WINDOW_ROWS)]


def _block_shape(part):
    return part.shape[1:] if part.ndim == 3 else (WINDOW_ROWS, part.shape[1])


def _sibling_copies(part, out, send_sems, recv_sems):
    x, y, c = _position()
    return [pltpu.make_async_remote_copy(src_ref=_owner_block(part, k, 1 - c), dst_ref=out.at[k],
                                         send_sem=send_sems.at[k], recv_sem=recv_sems.at[k],
                                         device_id=(x, y, 1 - c), device_id_type=MESH)
            for k in range(4)]


def _start_all(copies):
    for cp in copies:
        cp.start()


def _wait_all(copies):
    for cp in copies:
        cp.wait_recv()
    for cp in copies:
        cp.wait_send()


def _chip_sums(part, from_sibling, core, tc, name, riding=None):
    rows, cols = _block_shape(part)
    nj = cols // tc

    def body(core_ref, p_ref, s_ref, *rest):
        if riding is None:
            (o_ref,) = rest
        else:
            ride_in, o_ref, ride_out, send_sems, recv_sems = rest
            k, j = pl.program_id(0), pl.program_id(1)

            @pl.when(jnp.logical_and(k == 0, j == 0))
            def _():
                _start_all(_sibling_copies(ride_in, ride_out, send_sems, recv_sems))

        o_ref[0] = (p_ref[...].reshape(rows, tc) + s_ref[0]).astype(BF16)

        if riding is not None:
            @pl.when(jnp.logical_and(k == 3, j == nj - 1))
            def _():
                _wait_all(_sibling_copies(ride_in, ride_out, send_sems, recv_sems))

    hbm = pl.BlockSpec(memory_space=pl.ANY)
    sums = jax.ShapeDtypeStruct((4, rows, cols), BF16)
    tile_out = pl.BlockSpec((1, rows, tc), lambda k, j, core_ref: (k, 0, j))
    if part.ndim == 3:
        mine = pl.BlockSpec((1, rows, tc), lambda k, j, core_ref: (2 * k + core_ref[0], 0, j))
    else:
        mine = pl.BlockSpec((pl.Element(rows), pl.Element(tc)),
                            lambda k, j, core_ref: (pl.multiple_of(_window_start(k, core_ref[0]), 8),
                                                    pl.multiple_of(j * tc, 128)))
    in_specs = [mine, pl.BlockSpec((1, rows, tc), lambda k, j, core_ref: (k, 0, j))]
    if riding is None:
        out_shape, out_specs, scratch, args = sums, tile_out, [], (core, part, from_sibling)
    else:
        out_shape = (sums, jax.ShapeDtypeStruct((4,) + _block_shape(riding), F32))
        out_specs, in_specs = (tile_out, hbm), in_specs + [hbm]
        scratch = [pltpu.SemaphoreType.DMA((4,)), pltpu.SemaphoreType.DMA((4,))]
        args = (core, part, from_sibling, riding)
    return pl.pallas_call(
        body, name=name, out_shape=out_shape,
        grid_spec=pltpu.PrefetchScalarGridSpec(num_scalar_prefetch=1, grid=(4, nj), in_specs=in_specs,
                                               out_specs=out_specs, scratch_shapes=scratch),
        compiler_params=_cparams("arbitrary", "arbitrary"),
    )(*args)


def _sum_chips(s_ref, r_ref):
    f = lambda a: a.astype(F32)
    return (f(s_ref[0]) + f(r_ref[0])) + f(r_ref[1])


def _final_sum(sums, from_chips, chip, small, tc, name):
    _, rows, cols = sums.shape
    nj = cols // tc

    def body(chip_ref, s_ref, r_ref, sm_ref, g_out, tot_ref, all_ref, send_sems, recv_sems):
        j = pl.program_id(0)
        me = _blk(*_position())

        @pl.when(j == 0)
        def _():
            all_ref[me] = sm_ref[...]
            _start_all(_peer_copies((all_ref.at[me],), (all_ref,), send_sems, recv_sems))

        g_out[...] = _sum_chips(s_ref, r_ref)

        @pl.when(j == nj - 1)
        def _():
            _wait_all(_peer_copies((all_ref.at[me],), (all_ref,), send_sems, recv_sems))
            acc = all_ref[0]
            for d in range(1, N_DEV):
                acc = acc + all_ref[d]
            tot_ref[...] = acc

    whole = pl.BlockSpec(small.shape, lambda j, chip_ref: (0, 0))
    return pl.pallas_call(
        body, name=name,
        out_shape=(jax.ShapeDtypeStruct((rows, cols), F32), jax.ShapeDtypeStruct(small.shape, F32)),
        grid_spec=pltpu.PrefetchScalarGridSpec(
            num_scalar_prefetch=1, grid=(nj,),
            in_specs=[pl.BlockSpec((1, rows, tc), lambda j, chip_ref: (chip_ref[0], 0, j)),
                      pl.BlockSpec((2, rows, tc), lambda j, chip_ref: (0, 0, j)), whole],
            out_specs=(pl.BlockSpec((rows, tc), lambda j, chip_ref: (0, j)), whole),
            scratch_shapes=[pltpu.VMEM((N_DEV,) + small.shape, F32), pltpu.SemaphoreType.DMA((7,)),
                            pltpu.SemaphoreType.DMA((7,))]),
        compiler_params=_cparams("arbitrary"),
    )(chip, sums, from_chips, small)


def _adamw_rows(g, w, m, v, tr, name):
    rows = g.shape[0]

    def body(g_ref, w_ref, m_ref, v_ref, d_out, m_out, v_out):
        delta, m_new, v_new = _adamw(w_ref[...], g_ref[...], m_ref[...], v_ref[...])
        d_out[...] = delta
        m_out[...] = m_new
        v_out[...] = v_new

    tile = pl.BlockSpec((tr,) + g.shape[1:], lambda r: (r, 0, 0))
    shp = jax.ShapeDtypeStruct(g.shape, F32)
    return pl.pallas_call(
        body, name=name, out_shape=(shp, shp, shp), grid=(rows // tr,),
        in_specs=[tile] * 4, out_specs=(tile, tile, tile),
        compiler_params=_cparams("arbitrary"),
    )(g, w, m, v)


def _adamw(w, g, m, v):
    m = ADAM_B1 * m + (1.0 - ADAM_B1) * g
    v = ADAM_B2 * v + (1.0 - ADAM_B2) * (g * g)
    m_hat = m / (1.0 - ADAM_B1 ** ADAM_STEP)
    v_hat = v / (1.0 - ADAM_B2 ** ADAM_STEP)
    delta = -ADAM_LR * (m_hat / (jnp.sqrt(v_hat) + ADAM_EPS) + ADAM_WD * w)
    return delta, m, v


def _final_sum_adamw(sums, from_chips, chip, w, m, v, tr, name):
    rows, cols = w.shape

    def body(chip_ref, s_ref, r_ref, w_ref, m_ref, v_ref, g_out, d_out, m_out, v_out):
        g = _sum_chips(s_ref, r_ref)
        delta, m_new, v_new = _adamw(w_ref[...], g, m_ref[...], v_ref[...])
        g_out[...] = g
        d_out[...] = delta
        m_out[...] = m_new
        v_out[...] = v_new

    tile = pl.BlockSpec((tr, cols), lambda r, chip_ref: (r, 0))
    shp = jax.ShapeDtypeStruct((rows, cols), F32)
    return pl.pallas_call(
        body, name=name,
        out_shape=(shp, shp, shp, shp),
        grid_spec=pltpu.PrefetchScalarGridSpec(
            num_scalar_prefetch=1, grid=(rows // tr,),
            in_specs=[pl.BlockSpec((1, tr, cols), lambda r, chip_ref: (chip_ref[0], r, 0)),
                      pl.BlockSpec((2, tr, cols), lambda r, chip_ref: (0, r, 0)),
                      tile, tile, tile],
            out_specs=(tile, tile, tile, tile)),
        compiler_params=_cparams("arbitrary"),
    )(chip, sums, from_chips, w, m, v)


def _adamw_small(g, w, m, v):
    def body(g_ref, w_ref, m_ref, v_ref, d_out, m_out, v_out):
        delta, m_new, v_new = _adamw(w_ref[...], g_ref[...], m_ref[...], v_ref[...])
        d_out[...] = delta
        m_out[...] = m_new
        v_out[...] = v_new

    vmem = pl.BlockSpec(memory_space=pltpu.VMEM)
    shp = jax.ShapeDtypeStruct(g.shape, F32)
    return pl.pallas_call(body, name="adamw_small", out_shape=(shp, shp, shp),
                          in_specs=[vmem] * 4, out_specs=(vmem, vmem, vmem))(g, w, m, v)


TILE_ROWS = (0, 1024, NAT_ZA, NAT_B, NAT_ZC, NAT_C, NAT_C + CONV_W)


TILE_ORDER = ((0, 1, 2, 3, 5, 6, 4), (2, 1, 0, 4, 3, 5, 6), (5, 6, 0, 4, 1, 2, 3), (4, 6, 2, 3, 5, 0, 1))
EARLY_SWEEP, NEIGHBOUR_SWEEP, DIAGONAL_SWEEP = 1, 2, 4
PIECES, W_IN_PIECES, OTHER_PIECES = 4, (0, 1), (2, 3)


def _gather_inproj(x2d, norm_g, shard_t, w_out_s, small_s, order, tm):
    seq = x2d.shape[0]
    tn = CONV_W
    ni, nj = seq // tm, MAIN_W // tn
    first_sweep = lambda j, i, order_ref: jnp.where(j == 0, i, ni - 1)
    last_sweep = lambda j, i, order_ref: jnp.where(j == nj - 1, i, 0)
    edge_tiles = _edge_tiles()

    def body(order_ref, x_ref, g_ref, shard_ref, wout_ref, sm_ref, proj_ref, lr_ref, ht_ref, w_nat, wout_all, sm_all,
             w_all, h_all, edges, stage, wout_b, sm_b, send_sems, recv_sems, local_sems):
        j, i = pl.program_id(0), pl.program_id(1)
        rows = pl.ds(pl.multiple_of(i * tm, tm), tm)
        x, y, c = _position()
        me, here, sibling = _blk(x, y, c), (x, y, c), (x, y, 1 - c)
        chips = _route_chips()
        sibling_chips = [chips[1], chips[0], chips[2]]

        def pieces(px, py, pc):
            blk = _blk(px, py, pc)
            body_rows = pl.ds(pl.multiple_of(_first_tile_row(blk, px) + EDGE, EDGE), BODY_ROWS)
            return [w_all.at[body_rows], edges.at[blk], wout_all.at[blk], sm_all.at[blk]]

        def copy(a, k, block, to, staged=None):
            ref = pieces(*block)[a]
            return pltpu.make_async_remote_copy(src_ref=ref if staged is None else staged, dst_ref=ref,
                                                send_sem=send_sems.at[a * 7 + k], recv_sem=recv_sems.at[a * 7 + k],
                                                device_id=to, device_id_type=MESH)

        def own_copies(group, slots=(0, 1, 2)):
            targets = [sibling] + [(*chips[n], c) for n in range(2)]
            staged = [None, None, wout_b, sm_b]
            return [copy(a, k, here, targets[k], staged[a]) for k in slots for a in group]

        def relays(group):
            return [copy(a, 3, (*chips[0], c), (*chips[1], c)) for a in group]

        def forwards(n, group):
            return [copy(a, 4 + n, (*chips[n], c), sibling) for a in group]

        def keep_own():
            return [pltpu.make_async_copy(wout_b, wout_all.at[me], local_sems.at[0]),
                    pltpu.make_async_copy(sm_b, sm_all.at[me], local_sems.at[1])]

        def keep_weight():
            return pltpu.make_async_copy(w_all, w_nat, local_sems.at[2])

        def take(ns, group, relay=True):
            for n in ns:
                for a in group:
                    copy(a, 1 + n, (*chips[n], c), here).wait_recv()
                _start_all((relays(group) if n == 0 and relay else []) + forwards(n, group))

        def take_passed_on(ns, group):
            for n in ns:
                for a in group:
                    copy(a, 4 + n, (*sibling_chips[n], 1 - c), here).wait_recv()

        def arrive(ns, group):
            take(ns, group)
            take_passed_on(ns, group)

        def per_core_and_row(step):
            for core in range(2):
                for row in range(2):
                    pl.when(jnp.logical_and(c == core, y == row))(functools.partial(step, core, row))

        def start_own(core, row):
            now = (0, 1 + core) if core == row else (0, 1, 2)
            _start_all(own_copies(W_IN_PIECES, now))
            wout_b[...] = wout_ref[...].astype(BF16)
            sm_b[...] = sm_ref[...]
            _start_all(own_copies(OTHER_PIECES, now) + keep_own())

        def take_early(core, row):
            if core == row:
                _start_all(own_copies(W_IN_PIECES, (2 - core,)) + own_copies(OTHER_PIECES, (2 - core,)))
                take((1 - core,), W_IN_PIECES, relay=False)
            else:
                take_passed_on((core,), W_IN_PIECES)

        def take_neighbours(core, row):
            if core == row:
                _start_all(relays(W_IN_PIECES) if core == 1 else [])
                take((core,), W_IN_PIECES)
                take_passed_on((0, 1), W_IN_PIECES)
            else:
                take((0, 1), W_IN_PIECES)
                take_passed_on((1 - core,), W_IN_PIECES)

        early_blk = _blk(x, 1 - y, y)

        def add_edge_tiles(stage):
            for row, parts in edge_tiles.items():
                ready = 0
                for blk, _ in parts:
                    away = (x != blk // 4).astype(jnp.int32) + (y != (blk // 2) % 2).astype(jnp.int32)
                    late = jnp.where(away == 1, jnp.where(early_blk == blk, 1, 2), jnp.where(away == 2, 3 + blk % 2, 0))
                    ready = jnp.maximum(ready, late)

                @pl.when(ready == stage)
                def _(row=row, parts=parts):
                    tile = edges[parts[0][0], parts[0][1]].astype(F32)
                    for blk, side in parts[1:]:
                        tile = tile + edges[blk, side].astype(F32)
                    w_all[row:row + EDGE, :] = tile.astype(BF16)

        @pl.when(jnp.logical_and(j == 0, i == 0))
        def _():
            last = SHARD_W // 8 * 8
            for col in range(0, D_MODEL, 128):
                cols = slice(col, col + 128)
                stage[0:last, :] = shard_ref[0:last, cols]
                stage[last:, :] = jnp.zeros((SHIFTED_ROWS - last, 128), F32)
                stage[last:SHARD_W, :] = shard_ref[last:SHARD_W, cols]
                for k in range(EDGE // 4):
                    @pl.when(me % 4 == k)
                    def _(k=k, cols=cols):
                        moved = pltpu.roll(stage[...], 4 * k, 0) if k else stage[...]
                        pieces(*here)[0][:, cols] = moved[EDGE:EDGE + BODY_ROWS].astype(BF16)
                        edges[me, 0, :, cols] = moved[0:EDGE].astype(BF16)
                        edges[me, 1, :, cols] = moved[EDGE + BODY_ROWS:].astype(BF16)
            per_core_and_row(start_own)
            for a in W_IN_PIECES:
                copy(a, 0, sibling, here).wait_recv()
            add_edge_tiles(0)

        @pl.when(jnp.logical_and(j == EARLY_SWEEP, i == 0))
        def _():
            per_core_and_row(take_early)
            add_edge_tiles(1)

        @pl.when(jnp.logical_and(j == NEIGHBOUR_SWEEP, i == 0))
        def _():
            per_core_and_row(take_neighbours)
            add_edge_tiles(2)

        for core in range(2):
            @pl.when(jnp.logical_and(j == DIAGONAL_SWEEP + core, i == 0))
            def _(core=core):
                pl.when(c == core)(lambda: take((2,), W_IN_PIECES))
                pl.when(c != core)(lambda: take_passed_on((2,), W_IN_PIECES))
                add_edge_tiles(3 + core)
                if core == 0:
                    arrive((0, 1), OTHER_PIECES)
                else:
                    keep_weight().start()

        @pl.when(jnp.logical_and(j == nj - 1, i == 0))
        def _():
            arrive((2,), OTHER_PIECES)

        @pl.when(j == 0)
        def _():
            xv = x_ref[...]
            r = lax.rsqrt(jnp.mean(xv * xv, axis=-1, keepdims=True) + EPS)
            h = (xv * r) * g_ref[...]
            h_all[rows, :] = h.astype(BF16)
            ht_ref[...] = h.T.astype(BF16)

        tile = order_ref[j]
        row = 0
        for k, start in enumerate(TILE_ROWS):
            row = row + jnp.where(tile == k, start // 32, 0)
        w_tile = w_all[pl.ds(pl.multiple_of(row * 32, 32), tn), :]
        proj_ref[...] = _dot_nt(h_all[rows, :], w_tile).astype(BF16)

        @pl.when(j == nj - 1)
        def _():
            lr_ref[...] = _dot_nt(h_all[rows, :], w_all[NAT_LR:NAT_LR + LR_W, :])

        @pl.when(jnp.logical_and(j == nj - 1, i == ni - 1))
        def _():
            everything = range(PIECES)
            passed_on = [cp for n in range(3) for cp in forwards(n, everything)]
            for cp in own_copies(everything) + relays(everything) + passed_on:
                cp.wait_send()
            for a in OTHER_PIECES:
                copy(a, 0, sibling, here).wait_recv()
            for cp in keep_own() + [keep_weight()]:
                cp.wait()

    const = lambda shape: pl.BlockSpec(shape, lambda j, i, order_ref: (0,) * len(shape))
    hbm = pl.BlockSpec(memory_space=pl.ANY)
    vmem = pl.BlockSpec(memory_space=pltpu.VMEM)
    return pl.pallas_call(
        body, name="gather_inproj",
        out_shape=(jax.ShapeDtypeStruct((seq, MAIN_W), BF16), jax.ShapeDtypeStruct((seq, LR_W), F32),
                   jax.ShapeDtypeStruct((D_MODEL, seq), BF16), jax.ShapeDtypeStruct((IN_W, D_MODEL), BF16),
                   jax.ShapeDtypeStruct((N_DEV,) + w_out_s.shape, BF16),
                   jax.ShapeDtypeStruct((N_DEV,) + small_s.shape, F32)),
        grid_spec=pltpu.PrefetchScalarGridSpec(
            num_scalar_prefetch=1, grid=(nj, ni),
            in_specs=[pl.BlockSpec((tm, D_MODEL), lambda j, i, order_ref: (first_sweep(j, i, order_ref), 0)),
                      const((1, D_MODEL)), vmem, vmem, const(small_s.shape)],
            out_specs=(pl.BlockSpec((tm, tn), lambda j, i, order_ref: (i, order_ref[j])),
                       pl.BlockSpec((tm, LR_W), lambda j, i, order_ref: (last_sweep(j, i, order_ref), 0)),
                       pl.BlockSpec((D_MODEL, tm), lambda j, i, order_ref: (0, first_sweep(j, i, order_ref))),
                       hbm, hbm, hbm),
            scratch_shapes=[pltpu.VMEM((IN_W, D_MODEL), BF16), pltpu.VMEM((seq, D_MODEL), BF16),
                            pltpu.VMEM((N_DEV, 2, EDGE, D_MODEL), BF16), pltpu.VMEM((SHIFTED_ROWS, 128), F32),
                            pltpu.VMEM(w_out_s.shape, BF16), pltpu.VMEM(small_s.shape, F32),
                            pltpu.SemaphoreType.DMA((7 * PIECES,)), pltpu.SemaphoreType.DMA((7 * PIECES,)),
                            pltpu.SemaphoreType.DMA((3,))]),
        compiler_params=_cparams("arbitrary", "arbitrary"),
    )(order, x2d, norm_g, shard_t, w_out_s, small_s)


def _block_masks(tt):
    row = lax.broadcasted_iota(jnp.int32, (tt, tt), 0)
    col = lax.broadcasted_iota(jnp.int32, (tt, tt), 1)
    same = jnp.right_shift(row, 6) == jnp.right_shift(col, 6)
    return (jnp.logical_and(same, col <= row), jnp.logical_and(same, col >= row), jnp.logical_and(same, col > row))


def _dot_split3(ones_mat, x):
    x1 = x.astype(BF16)
    r1 = x - x1.astype(F32)
    x2 = r1.astype(BF16)
    x3 = (r1 - x2.astype(F32)).astype(BF16)
    return (_dot(ones_mat, x3) + _dot(ones_mat, x2)) + _dot(ones_mat, x1)


def _log_gate(logits):
    return (jnp.minimum(logits, 0.0) - jnp.log(1.0 + jnp.exp(-jnp.abs(logits)))) * GATE_SCALE


def _chunk_column_mask(tt):
    nc = tt // CHUNK
    row = lax.broadcasted_iota(jnp.int32, (tt, nc * DK), 0)
    col = lax.broadcasted_iota(jnp.int32, (tt, nc * DK), 1)
    return jnp.right_shift(row, 6) == jnp.right_shift(col, 7)


def _chunked(mask, x, nc):
    wide = jnp.concatenate([x] * nc, axis=1)
    return jnp.where(mask, wide, jnp.zeros_like(wide))


def _gla_fwd(proj, lr, wgk_f, wgk_b, bgk_f, bgk_b, tt):
    seq = proj.shape[0]
    nb, nc, nch = seq // tt, tt // CHUNK, seq // CHUNK

    def body(qf, kf, vf, lrf, qb, kb, vb, lrb, wf, wb, bf, bb, of, ob, stf, stb, s_scr, qs_s, ks_s, qin_s, kout_s):
        @pl.when(pl.program_id(0) == 0)
        def _():
            s_scr[...] = jnp.zeros(s_scr.shape, F32)

        low, upp, sup = _block_masks(tt)
        dirs = ((qf, kf, vf, lrf, wf, bf, of, stf, low, low, REF_F, LAST_F, list(range(nc))),
                (qb, kb, vb, lrb, wb, bb, ob, stb, upp, sup, REF_B, LAST_B, list(reversed(range(nc)))))
        for d, (q_r, k_r, v_r, lr_r, w_r, b_r, o_r, st_r, cum, mask, ref, last, order) in enumerate(dirs):
            logits = _dot(lr_r[...].astype(BF16), w_r[...]) + b_r[...]
            b = _dot_split3(cum.astype(BF16), _log_gate(logits))
            decs = []
            for c in range(nc):
                rows = slice(c * CHUNK, (c + 1) * CHUNK)
                bc = b[rows]
                b_ref, b_last = bc[ref:ref + 1], bc[last:last + 1]
                qc = q_r[rows, :].astype(F32) * QSCALE
                kc = k_r[rows, :].astype(F32)
                qs_s[rows, :] = (qc * jnp.exp(bc - b_ref)).astype(BF16)
                ks_s[rows, :] = (kc * jnp.exp(b_ref - bc)).astype(BF16)
                qin_s[rows, :] = (qc * jnp.exp(bc)).astype(BF16)
                kout_s[rows, :] = (kc * jnp.exp(b_last - bc)).astype(BF16)
                decs.append(jnp.exp(b_last))
            for h in range(HEADS):
                ksl = slice(h * DK, (h + 1) * DK)
                vsl = slice(h * DV, (h + 1) * DV)
                v = v_r[:, vsl].astype(BF16)
                att = jnp.where(mask, _dot_nt(qs_s[:, ksl], ks_s[:, ksl]), 0.0).astype(BF16)
                o_intra = _dot(att, v)
                st = s_scr[d * HEADS + h]
                for c in order:
                    rows = slice(c * CHUNK, (c + 1) * CHUNK)
                    stb = st.astype(BF16)
                    st_r[c, h] = stb
                    o_r[rows, vsl] = (o_intra[rows] + _dot_nt(qin_s[rows, ksl], stb)).astype(BF16)
                    st = st * decs[c][:, ksl] + _dot_tn(v[rows], kout_s[rows, ksl])
                s_scr[d * HEADS + h] = st

    fw = lambda i: (i, 0)
    bw = lambda i: (nb - 1 - i, 0)
    const = lambda i: (0, 0)

    def tok_specs(m):
        return [pl.BlockSpec((tt, QK_W), lambda i: (m(i)[0], OFF_Q // QK_W)),
                pl.BlockSpec((tt, QK_W), lambda i: (m(i)[0], OFF_K // QK_W)),
                pl.BlockSpec((tt, V_W), lambda i: (m(i)[0], OFF_V // V_W)),
                pl.BlockSpec((tt, LR_W), m)]

    st_shape = jax.ShapeDtypeStruct((nch, HEADS, DV, DK), BF16)
    o_shape = jax.ShapeDtypeStruct((seq, V_W), BF16)
    operand = pltpu.VMEM((tt, QK_W), BF16)
    return pl.pallas_call(
        body, name="gla_fwd",
        out_shape=(o_shape, o_shape, st_shape, st_shape),
        grid=(nb,),
        in_specs=tok_specs(fw) + tok_specs(bw) + [
            pl.BlockSpec((LR_W, QK_W), const), pl.BlockSpec((LR_W, QK_W), const),
            pl.BlockSpec((1, QK_W), const), pl.BlockSpec((1, QK_W), const)],
        out_specs=(pl.BlockSpec((tt, V_W), fw), pl.BlockSpec((tt, V_W), bw),
                   pl.BlockSpec((nc, HEADS, DV, DK), lambda i: (i, 0, 0, 0)),
                   pl.BlockSpec((nc, HEADS, DV, DK), lambda i: (nb - 1 - i, 0, 0, 0))),
        scratch_shapes=[pltpu.VMEM((2 * HEADS, DV, DK), F32), operand, operand, operand, operand],
        compiler_params=_cparams("arbitrary"),
    )(proj, proj, proj, lr, proj, proj, proj, lr, wgk_f, wgk_b, bgk_f, bgk_b)


def _head_norm(o, gain):
    outs, rinv = [], []
    for h in range(HEADS):
        oh = o[:, h * DV:(h + 1) * DV]
        r = lax.rsqrt(jnp.mean(oh * oh, axis=-1, keepdims=True) + EPS)
        outs.append((oh * r) * gain)
        rinv.append(r)
    return jnp.concatenate(outs, axis=1), rinv


def _shift_rows(u, prev_row, next_row):
    n = u.shape[0]
    row = lax.broadcasted_iota(jnp.int32, (n, 1), 0)
    up = jnp.where(row == 0, prev_row, pltpu.roll(u, 1, 0))
    un = jnp.where(row == n - 1, next_row, pltpu.roll(u, n - 1, 0))
    return up, un


HALO = 16


def _halo_specs(tm, seq, col_block):
    per = tm // HALO
    last = seq // HALO - 1
    return [pl.BlockSpec((HALO, CONV_W), lambda i: (jnp.maximum(i * per - 1, 0), col_block)),
            pl.BlockSpec((HALO, CONV_W), lambda i: (jnp.minimum((i + 1) * per, last), col_block))]


def _f32(ref):
    return ref[...].astype(F32)


def _last_row(ref):
    return ref[HALO - 1:HALO, :].astype(F32)


def _first_row(ref):
    return ref[0:1, :].astype(F32)


def _mix_out_loss(o_f, o_b, proj, x2d, tgt, gla_g, conv_w, conv_b, w_out, final_g, tm):
    seq = x2d.shape[0]
    nt = seq // tm

    def body(of, ob, za, bg, cg, hc, zc, cprev, cnext, hprev, hnext, x_ref, t_ref, gg, cw, cb, wo, fg,
             yt_ref, conv_ref, dx2_ref, dx2b_ref, loss_ref, dfg_ref):
        i = pl.program_id(0)

        @pl.when(i == 0)
        def _():
            loss_ref[...] = jnp.zeros(loss_ref.shape, F32)
            dfg_ref[...] = jnp.zeros(dfg_ref.shape, F32)

        on, _ = _head_norm(_f32(of) + _f32(ob), gg[...])
        zav = _f32(za)
        y_a = on * (zav * _sigmoid(zav))
        u = _f32(cg) * _f32(hc)
        prev_row = jnp.where(i > 0, _last_row(cprev) * _last_row(hprev), 0.0)
        next_row = jnp.where(i < nt - 1, _first_row(cnext) * _first_row(hnext), 0.0)
        up, un = _shift_rows(u, prev_row, next_row)
        conv = (cw[0:1, :] * up + cw[1:2, :] * u + cw[2:3, :] * un) + cb[...]
        conv_ref[...] = conv.astype(BF16)
        zcv = _f32(zc)
        y_c = _f32(bg) * conv * (zcv * _sigmoid(zcv))
        y = jnp.concatenate([y_a, y_c], axis=1)
        yt_ref[...] = y.T.astype(BF16)
        x2 = x_ref[...] + _dot(y.astype(BF16), wo[...])
        r = lax.rsqrt(jnp.mean(x2 * x2, axis=-1, keepdims=True) + EPS)
        xn = x2 * r
        err = xn * fg[...] - t_ref[...]
        loss_ref[...] += 0.5 * jnp.sum(jnp.mean(err * err, axis=-1, keepdims=True))
        dyf = err * (1.0 / D_MODEL)
        dfg_ref[...] += jnp.sum(dyf * xn, axis=0, keepdims=True)
        dxn = dyf * fg[...]
        dx2 = r * dxn - xn * (r * jnp.mean(dxn * xn, axis=-1, keepdims=True))
        dx2_ref[...] = dx2
        dx2b_ref[...] = dx2.astype(BF16)

    def col(off):
        return pl.BlockSpec((tm, CONV_W), lambda i: (i, off // CONV_W))

    rowt = pl.BlockSpec((tm, D_MODEL), lambda i: (i, 0))
    const = lambda shape: pl.BlockSpec(shape, lambda i: (0, 0))
    return pl.pallas_call(
        body, name="mix_out_loss",
        out_shape=(jax.ShapeDtypeStruct((MIX_W, seq), BF16), jax.ShapeDtypeStruct((seq, CONV_W), BF16),
                   jax.ShapeDtypeStruct((seq, D_MODEL), F32), jax.ShapeDtypeStruct((seq, D_MODEL), BF16),
                   jax.ShapeDtypeStruct((8, 128), F32), jax.ShapeDtypeStruct((1, D_MODEL), F32)),
        grid=(nt,),
        in_specs=[rowt, rowt, col(OFF_ZA), col(OFF_B), col(OFF_C), col(OFF_H), col(OFF_ZC)]
        + _halo_specs(tm, seq, OFF_C // CONV_W) + _halo_specs(tm, seq, OFF_H // CONV_W)
        + [rowt, rowt, const((1, DV)), const((8, CONV_W)), const((1, CONV_W)), const((MIX_W, D_MODEL)),
           const((1, D_MODEL))],
        out_specs=(pl.BlockSpec((MIX_W, tm), lambda i: (0, i)), rowt, rowt, rowt, const((8, 128)),
                   const((1, D_MODEL))),
        compiler_params=_cparams("arbitrary"),
    )(o_f, o_b, proj, proj, proj, proj, proj, proj, proj, proj, proj, x2d, tgt, gla_g, conv_w, conv_b, w_out, final_g)


def _dsilu(z, s):
    return s * (1.0 + z * (1.0 - s))


def _mix_bwd(dx2b, o_f, o_b, proj, conv, gla_g, w_out, tm):
    seq = dx2b.shape[0]

    def body(dx, of, ob, za, bg, zc, cv, gg, wo, dg_ref, do_ref, dconv_ref, dgg_ref, dcb_ref):
        @pl.when(pl.program_id(0) == 0)
        def _():
            dgg_ref[...] = jnp.zeros(dgg_ref.shape, F32)
            dcb_ref[...] = jnp.zeros(dcb_ref.shape, F32)

        dy = _dot_nt(dx[...], wo[...])
        dy_a, dy_c = dy[:, :V_W], dy[:, V_W:]
        zcv, bgv, convv = _f32(zc), _f32(bg), _f32(cv)
        sc = _sigmoid(zcv)
        szc = zcv * sc
        dg_ref[:, CONV_W:2 * CONV_W] = (dy_c * convv * szc).astype(BF16)
        dconv = dy_c * bgv * szc
        dconv_ref[...] = dconv.astype(BF16)
        dcb_ref[...] += jnp.sum(dconv, axis=0, keepdims=True)
        dg_ref[:, 2 * CONV_W:] = (dy_c * bgv * convv * _dsilu(zcv, sc)).astype(BF16)

        o = _f32(of) + _f32(ob)
        gain = gg[...]
        on, rinv = _head_norm(o, gain)
        zav = _f32(za)
        sa = _sigmoid(zav)
        dg_ref[:, :CONV_W] = (dy_a * on * _dsilu(zav, sa)).astype(BF16)
        don = dy_a * (zav * sa)
        dgg = jnp.zeros((1, DV), F32)
        dos = []
        for h in range(HEADS):
            sl = slice(h * DV, (h + 1) * DV)
            oh, r, dh = o[:, sl], rinv[h], don[:, sl]
            ohn = oh * r
            dgg = dgg + jnp.sum(dh * ohn, axis=0, keepdims=True)
            dn = dh * gain
            dos.append(r * dn - ohn * (r * jnp.mean(dn * ohn, axis=-1, keepdims=True)))
        dgg_ref[...] += dgg
        do_ref[...] = jnp.concatenate(dos, axis=1).astype(BF16)

    def col(off):
        return pl.BlockSpec((tm, CONV_W), lambda i: (i, off // CONV_W))

    rowt = pl.BlockSpec((tm, D_MODEL), lambda i: (i, 0))
    const = lambda shape: pl.BlockSpec(shape, lambda i: (0, 0))
    return pl.pallas_call(
        body, name="mix_bwd",
        out_shape=(jax.ShapeDtypeStruct((seq, GATES_W), BF16), jax.ShapeDtypeStruct((seq, V_W), BF16),
                   jax.ShapeDtypeStruct((seq, CONV_W), BF16),
                   jax.ShapeDtypeStruct((1, DV), F32), jax.ShapeDtypeStruct((1, CONV_W), F32)),
        grid=(seq // tm,),
        in_specs=[rowt, rowt, rowt, col(OFF_ZA), col(OFF_B), col(OFF_ZC), rowt, const((1, DV)),
                  const((MIX_W, D_MODEL))],
        out_specs=(pl.BlockSpec((tm, GATES_W), lambda i: (i, 0)), rowt, rowt, const((1, DV)), const((1, CONV_W))),
        compiler_params=_cparams("arbitrary"),
    )(dx2b, o_f, o_b, proj, proj, proj, conv, gla_g, w_out)


def _conv_bwd(dconv, proj, conv_w, tm):
    seq = dconv.shape[0]
    nt = seq // tm

    def body(dc_in, dprev, dnext, cg, hc, cprev, cnext, hprev, hnext, cw, dch_ref, dcw_ref):
        i = pl.program_id(0)

        @pl.when(i == 0)
        def _():
            dcw_ref[...] = jnp.zeros(dcw_ref.shape, F32)

        first, lastt = i > 0, i < nt - 1
        dcv = _f32(dc_in)
        d_up, d_un = _shift_rows(dcv, jnp.where(first, _last_row(dprev), 0.0), jnp.where(lastt, _first_row(dnext), 0.0))
        cgv, hcv = _f32(cg), _f32(hc)
        u = cgv * hcv
        u_up, u_un = _shift_rows(u, jnp.where(first, _last_row(cprev) * _last_row(hprev), 0.0),
                                 jnp.where(lastt, _first_row(cnext) * _first_row(hnext), 0.0))
        du = cw[0:1, :] * d_un + cw[1:2, :] * dcv + cw[2:3, :] * d_up
        dch_ref[:, :CONV_W] = (du * hcv).astype(BF16)
        dch_ref[:, CONV_W:] = (du * cgv).astype(BF16)
        dcw_ref[0:1, :] += jnp.sum(dcv * u_up, axis=0, keepdims=True)
        dcw_ref[1:2, :] += jnp.sum(dcv * u, axis=0, keepdims=True)
        dcw_ref[2:3, :] += jnp.sum(dcv * u_un, axis=0, keepdims=True)

    def col(off):
        return pl.BlockSpec((tm, CONV_W), lambda i: (i, off // CONV_W))

    rowt = pl.BlockSpec((tm, CONV_W), lambda i: (i, 0))
    const = lambda shape: pl.BlockSpec(shape, lambda i: (0, 0))
    return pl.pallas_call(
        body, name="conv_bwd",
        out_shape=(jax.ShapeDtypeStruct((seq, CH_W), BF16), jax.ShapeDtypeStruct((8, CONV_W), F32)),
        grid=(nt,),
        in_specs=[rowt] + _halo_specs(tm, seq, 0) + [col(OFF_C), col(OFF_H)]
        + _halo_specs(tm, seq, OFF_C // CONV_W) + _halo_specs(tm, seq, OFF_H // CONV_W) + [const((8, CONV_W))],
        out_specs=(pl.BlockSpec((tm, CH_W), lambda i: (i, 0)), const((8, CONV_W))),
        compiler_params=_cparams("arbitrary"),
    )(dconv, dconv, dconv, proj, proj, proj, proj, proj, proj, conv_w)


def _gla_bwd(proj, lr, do, st_f, st_b, wgk_f, wgk_b, bgk_f, bgk_b, tt):
    seq = proj.shape[0]
    nb, nc = seq // tt, tt // CHUNK

    def body(qf, kf, vf, lrf, dof, stf, qb, kb, vb, lrb, dob, stb, wf, wb, bf, bb,
             dqkv_f, dlr_f, dqkv_b, dlr_b, dwf, dwb, dbf, dbb,
             ds_scr, eq_s, ek_s, ein_s, eout_s, qs_s, ks_s, qin_s, kout_s, db_s, lg_s):
        @pl.when(pl.program_id(0) == 0)
        def _():
            ds_scr[...] = jnp.zeros(ds_scr.shape, F32)
            for r in (dwf, dwb, dbf, dbb):
                r[...] = jnp.zeros(r.shape, F32)

        low, upp, sup = _block_masks(tt)
        row = lax.broadcasted_iota(jnp.int32, (CHUNK, 1), 0)
        kmask = _chunk_column_mask(tt)
        dirs = ((qf, kf, vf, lrf, dof, stf, wf, bf, dqkv_f, dlr_f, dwf, dbf,
                 low, upp, low, REF_F, LAST_F, list(reversed(range(nc)))),
                (qb, kb, vb, lrb, dob, stb, wb, bb, dqkv_b, dlr_b, dwb, dbb,
                 upp, low, sup, REF_B, LAST_B, list(range(nc))))
        for d, (q_r, k_r, v_r, lr_r, do_r, st_r, w_r, b_r, dqkv_r, dlr_r, dw_r, db_r,
                cum, cum_t, mask, ref, last, order) in enumerate(dirs):
            lrv = lr_r[...].astype(BF16)
            wv = w_r[...]
            logits = _dot(lrv, wv) + b_r[...]
            lg_s[...] = logits
            b = _dot_split3(cum.astype(BF16), _log_gate(logits))
            decs = []
            for c in range(nc):
                rows = slice(c * CHUNK, (c + 1) * CHUNK)
                bc = b[rows]
                b_ref, b_last = bc[ref:ref + 1], bc[last:last + 1]
                qc = q_r[rows, :].astype(F32) * QSCALE
                kc = k_r[rows, :].astype(F32)
                e_q, e_k, e_in, e_out = jnp.exp(bc - b_ref), jnp.exp(b_ref - bc), jnp.exp(bc), jnp.exp(b_last - bc)
                eq_s[rows, :], ek_s[rows, :], ein_s[rows, :], eout_s[rows, :] = e_q, e_k, e_in, e_out
                qs_s[rows, :] = (qc * e_q).astype(BF16)
                ks_s[rows, :] = (kc * e_k).astype(BF16)
                qin_s[rows, :] = (qc * e_in).astype(BF16)
                kout_s[rows, :] = (kc * e_out).astype(BF16)
                decs.append(jnp.exp(b_last))
            for h in range(HEADS):
                ksl = slice(h * DK, (h + 1) * DK)
                vsl = slice(h * DV, (h + 1) * DV)
                v = v_r[:, vsl].astype(BF16)
                dov = do_r[:, vsl].astype(BF16)
                qsb, ksb = qs_s[:, ksl], ks_s[:, ksl]
                att = jnp.where(mask, _dot_nt(qsb, ksb), 0.0).astype(BF16)
                datt = jnp.where(mask, _dot_nt(dov, v), 0.0).astype(BF16)
                dqs = _dot(datt, ksb)
                dks = _dot_tn(datt, qsb)
                dv_intra = _dot_tn(att, dov)
                g_t = _dot_tn(dov, _chunked(kmask, qin_s[:, ksl], nc))
                ds = ds_scr[d * HEADS + h]
                for c in order:
                    rows = slice(c * CHUNK, (c + 1) * CHUNK)
                    dsb = ds.astype(BF16)
                    s_prev = st_r[c, h]
                    dk_out = _dot(v[rows], dsb)
                    dq_in = _dot(dov[rows], s_prev)
                    dv = dv_intra[rows] + _dot_nt(kout_s[rows, ksl], dsb)
                    dqkv_r[rows, OFF_V + h * DV:OFF_V + (h + 1) * DV] = dv.astype(BF16)
                    dec = decs[c][:, ksl]
                    ddec = jnp.sum(ds * s_prev.astype(F32), axis=0, keepdims=True)
                    e_out = eout_s[rows, ksl]
                    qc = q_r[rows, ksl].astype(F32) * QSCALE
                    kc = k_r[rows, ksl].astype(F32)
                    dq = dqs[rows] * eq_s[rows, ksl] + dq_in * ein_s[rows, ksl]
                    dk = dks[rows] * ek_s[rows, ksl] + dk_out * e_out
                    dqkv_r[rows, OFF_Q + h * DK:OFF_Q + (h + 1) * DK] = (dq * QSCALE).astype(BF16)
                    dqkv_r[rows, OFF_K + h * DK:OFF_K + (h + 1) * DK] = dk.astype(BF16)
                    tail = jnp.sum(dk_out * (kc * e_out), axis=0, keepdims=True) + ddec * dec
                    db_s[rows, ksl] = (qc * dq - kc * dk) + jnp.where(row == last, tail, 0.0)
                    ds = ds * dec + g_t[:, c * DK:(c + 1) * DK]
                ds_scr[d * HEADS + h] = ds
            dg = _dot_split3(cum_t.astype(BF16), db_s[...])
            dlogit = (dg * GATE_SCALE) * _sigmoid(-lg_s[...])
            dlb = dlogit.astype(BF16)
            dlr_r[...] = _dot_nt(dlb, wv)
            dw_r[...] += _dot_tn(lrv, dlb)
            db_r[...] += jnp.sum(dlogit, axis=0, keepdims=True)

    fw = lambda i: (nb - 1 - i, 0)
    bw = lambda i: (i, 0)
    const = lambda i: (0, 0)

    def tok_specs(m):
        return [pl.BlockSpec((tt, QK_W), lambda i: (m(i)[0], OFF_Q // QK_W)),
                pl.BlockSpec((tt, QK_W), lambda i: (m(i)[0], OFF_K // QK_W)),
                pl.BlockSpec((tt, V_W), lambda i: (m(i)[0], OFF_V // V_W)),
                pl.BlockSpec((tt, LR_W), m),
                pl.BlockSpec((tt, V_W), m),
                pl.BlockSpec((nc, HEADS, DV, DK), lambda i: (m(i)[0], 0, 0, 0))]

    dqkv = jax.ShapeDtypeStruct((seq, QK_W + QK_W + V_W), BF16)
    dlr = jax.ShapeDtypeStruct((seq, LR_W), F32)
    dw = jax.ShapeDtypeStruct((LR_W, QK_W), F32)
    dbias = jax.ShapeDtypeStruct((1, QK_W), F32)
    return pl.pallas_call(
        body, name="gla_bwd",
        out_shape=(dqkv, dlr, dqkv, dlr, dw, dw, dbias, dbias),
        grid=(nb,),
        in_specs=tok_specs(fw) + tok_specs(bw) + [
            pl.BlockSpec((LR_W, QK_W), const), pl.BlockSpec((LR_W, QK_W), const),
            pl.BlockSpec((1, QK_W), const), pl.BlockSpec((1, QK_W), const)],
        out_specs=(pl.BlockSpec((tt, QK_W + QK_W + V_W), fw), pl.BlockSpec((tt, LR_W), fw),
                   pl.BlockSpec((tt, QK_W + QK_W + V_W), bw), pl.BlockSpec((tt, LR_W), bw),
                   pl.BlockSpec((LR_W, QK_W), const), pl.BlockSpec((LR_W, QK_W), const),
                   pl.BlockSpec((1, QK_W), const), pl.BlockSpec((1, QK_W), const)),
        scratch_shapes=[pltpu.VMEM((2 * HEADS, DV, DK), F32)] + [pltpu.VMEM((tt, QK_W), F32)] * 4
        + [pltpu.VMEM((tt, QK_W), BF16)] * 4 + [pltpu.VMEM((tt, QK_W), F32)] * 2,
        compiler_params=_cparams("arbitrary"),
    )(proj, proj, proj, lr, do, st_f, proj, proj, proj, lr, do, st_b, wgk_f, wgk_b, bgk_f, bgk_b)


def _both_directions(f_ref, b_ref):
    return (_f32(f_ref) + _f32(b_ref)).astype(BF16)


def _input_grad(dqkv_f, dqkv_b, dp_gates, dp_ch, dlr_f, dlr_b, w_nat, x2d, norm_g, dx2, sums, tm):
    seq = x2d.shape[0]
    nt, n = seq // tm, len(sums)
    relay_step = (3 * nt) // 8

    def body(dqf, dqb, dg, dc, dlf, dlb, w, x_ref, g_ref, dx2_ref, *rest):
        ins, (gx_ref, dng_ref), outs = rest[:n], rest[n:n + 2], rest[n + 2:2 * n + 2]
        passing, joined = rest[2 * n + 2:3 * n + 2], rest[3 * n + 2:4 * n + 2]
        send_sems, recv_sems, local_sems = rest[4 * n + 2:]
        i = pl.program_id(0)
        c = lax.axis_index("c")
        first, second, diagonal = _route_chips()
        slot = lambda chip: 2 * chip[0] + chip[1]

        def remote(a, k, src, dst, to):
            return pltpu.make_async_remote_copy(src_ref=src, dst_ref=dst, send_sem=send_sems.at[3 * a + k],
                                                recv_sem=recv_sems.at[3 * a + k], device_id=(*to, c),
                                                device_id_type=MESH)

        direct = lambda a: remote(a, 0, ins[a].at[slot(first)], outs[a].at[0], first)
        for_second = lambda a: remote(a, 1, ins[a].at[slot(diagonal)], passing[a], first)
        joint = lambda a: remote(a, 2, joined[a], outs[a].at[1], second)
        own = lambda a: pltpu.make_async_copy(ins[a].at[slot(second)], joined[a], local_sems.at[a])

        @pl.when(i == 0)
        def _():
            _start_all([for_second(a) for a in range(n)] + [own(a) for a in range(n)] + [direct(a) for a in range(n)])
            dng_ref[...] = jnp.zeros(dng_ref.shape, F32)

        @pl.when(i == relay_step)
        def _():
            for a in range(n):
                for_second(a).wait_recv()
                own(a).wait()
                joined[a][...] = (joined[a][...].astype(F32) + passing[a][...].astype(F32)).astype(BF16)
                joint(a).start()

        dh = (_dot((dlf[...] + dlb[...]).astype(BF16), w[NAT_LR:NAT_LR + LR_W, :])
              + _dot(_both_directions(dqf, dqb), w[0:NAT_ZA, :])
              + _dot(dg[:, 0:CONV_W], w[NAT_ZA:NAT_LR, :]) + _dot(dg[:, CONV_W:2 * CONV_W], w[NAT_B:NAT_C, :])
              + _dot(dg[:, 2 * CONV_W:], w[NAT_ZC:IN_W, :]) + _dot(dc[...], w[NAT_C:NAT_ZC, :]))
        xv = x_ref[...]
        r = lax.rsqrt(jnp.mean(xv * xv, axis=-1, keepdims=True) + EPS)
        xn = xv * r
        dng_ref[...] += jnp.sum(dh * xn, axis=0, keepdims=True)
        dn = dh * g_ref[...]
        gx_ref[...] = (r * dn - xn * (r * jnp.mean(dn * xn, axis=-1, keepdims=True))) + dx2_ref[...]

        @pl.when(i == nt - 1)
        def _():
            for a in range(n):
                direct(a).wait_recv()
                joint(a).wait_recv()
            for a in range(n):
                for cp in (direct(a), for_second(a), joint(a)):
                    cp.wait_send()

    rowt = pl.BlockSpec((tm, D_MODEL), lambda i: (i, 0))
    seg = lambda width: pl.BlockSpec((tm, width), lambda i: (i, 0))
    resident = lambda rows: pl.BlockSpec((rows, D_MODEL), lambda i: (0, 0), pipeline_mode=pl.Buffered(1))
    hbm = pl.BlockSpec(memory_space=pl.ANY)
    blocks = [pltpu.VMEM(s.shape[1:], s.dtype) for s in sums]
    return pl.pallas_call(
        body, name="input_grad",
        out_shape=(jax.ShapeDtypeStruct((seq, D_MODEL), F32), jax.ShapeDtypeStruct((1, D_MODEL), F32))
        + tuple(jax.ShapeDtypeStruct((2,) + s.shape[1:], s.dtype) for s in sums),
        grid=(nt,),
        in_specs=[seg(QKV_W), seg(QKV_W), seg(GATES_W), seg(CH_W), seg(LR_W), seg(LR_W), resident(IN_W),
                  rowt, pl.BlockSpec((1, D_MODEL), lambda i: (0, 0)), rowt] + [hbm] * n,
        out_specs=(rowt, pl.BlockSpec((1, D_MODEL), lambda i: (0, 0))) + (hbm,) * n,
        scratch_shapes=blocks + blocks + [pltpu.SemaphoreType.DMA((3 * n,)), pltpu.SemaphoreType.DMA((3 * n,)),
                                          pltpu.SemaphoreType.DMA((n,))],
        compiler_params=_cparams("arbitrary"),
    )(dqkv_f, dqkv_b, dp_gates, dp_ch, dlr_f, dlr_b, w_nat, x2d, norm_g, dx2, *sums)


def _weight_grad_out(y_t, dx2b, tk, riding):
    m, seq = y_t.shape
    n = dx2b.shape[1]
    nk = seq // tk

    def body(a_ref, b_ref, ride_in, o_ref, ride_out, send_sems, recv_sems):
        k = pl.program_id(0)

        @pl.when(k == 0)
        def _():
            _start_all(_sibling_copies(ride_in, ride_out, send_sems, recv_sems))
            o_ref[...] = jnp.zeros(o_ref.shape, F32)

        o_ref[...] += _dot(a_ref[...], b_ref[...])

        @pl.when(k == nk - 1)
        def _():
            _wait_all(_sibling_copies(ride_in, ride_out, send_sems, recv_sems))

    hbm = pl.BlockSpec(memory_space=pl.ANY)
    return pl.pallas_call(
        body, name="wgrad_out",
        out_shape=(jax.ShapeDtypeStruct((m, n), F32), jax.ShapeDtypeStruct((4,) + _block_shape(riding), F32)),
        grid=(nk,),
        in_specs=[pl.BlockSpec((m, tk), lambda k: (0, k)), pl.BlockSpec((tk, n), lambda k: (k, 0)), hbm],
        out_specs=(pl.BlockSpec((m, n), lambda k: (0, 0)), hbm),
        scratch_shapes=[pltpu.SemaphoreType.DMA((4,)), pltpu.SemaphoreType.DMA((4,))],
        compiler_params=_cparams("arbitrary"),
    )(y_t, dx2b, riding)


def _weight_grad_in(h_t, dqkv_f, dqkv_b, dp_gates, dp_ch, dlr_f, dlr_b):
    m, seq = h_t.shape
    tn = 512
    n_qkv, n_gates, n_ch = QKV_W // tn, GATES_W // tn, CH_W // tn
    starts = ([k * tn for k in range(n_qkv)] + [NAT_ZA, NAT_ZA + tn, NAT_B, NAT_B + tn, NAT_ZC, NAT_ZC + tn]
              + [NAT_C + k * tn for k in range(n_ch)])

    def out_row(j):
        row = 0
        for k, start in enumerate(starts):
            row = row + jnp.where(j == k, start // 32, 0)
        return pl.multiple_of(row * 32, 32), 0

    def body(a_ref, bqf, bqb, bg, bc, o_ref, acc, bq):
        j = pl.program_id(0)

        @pl.when(j < n_qkv)
        def _():
            bq[...] = _both_directions(bqf, bqb)
            acc[...] = _dot(a_ref[...], bq[...])

        @pl.when(jnp.logical_and(j >= n_qkv, j < n_qkv + n_gates))
        def _():
            acc[...] = _dot(a_ref[...], bg[...])

        @pl.when(j >= n_qkv + n_gates)
        def _():
            acc[...] = _dot(a_ref[...], bc[...])

        o_ref[...] = acc[...].T

    resident = pl.BlockSpec((m, seq), lambda j: (0, 0), pipeline_mode=pl.Buffered(1))
    seg = lambda first, count: pl.BlockSpec((seq, tn), lambda j: (0, jnp.clip(j - first, 0, count - 1)))
    main = pl.pallas_call(
        body, name="wgrad_in",
        out_shape=jax.ShapeDtypeStruct((IN_W, m), F32),
        grid=(n_qkv + n_gates + n_ch,),
        in_specs=[resident, seg(0, n_qkv), seg(0, n_qkv), seg(n_qkv, n_gates), seg(n_qkv + n_gates, n_ch)],
        out_specs=pl.BlockSpec((pl.Element(tn), pl.Element(m)), out_row),
        scratch_shapes=[pltpu.VMEM((m, tn), F32), pltpu.VMEM((seq, tn), BF16)],
        compiler_params=_cparams("arbitrary"),
    )(h_t, dqkv_f, dqkv_b, dp_gates, dp_ch)

    def lr_body(a_ref, bf_ref, bb_ref, full_ref, o_ref, acc):
        acc[...] = _dot(a_ref[...], (bf_ref[...] + bb_ref[...]).astype(BF16))
        o_ref[...] = acc[...].T[0:2 * RANK, :]

    whole = lambda shape: pl.BlockSpec(shape, lambda j: (0, 0))
    return pl.pallas_call(
        lr_body, name="wgrad_lr",
        out_shape=jax.ShapeDtypeStruct((IN_W, m), F32),
        grid=(1,),
        in_specs=[whole((m, seq)), whole((seq, LR_W)), whole((seq, LR_W)), pl.BlockSpec(memory_space=pl.ANY)],
        out_specs=pl.BlockSpec((pl.Element(2 * RANK), pl.Element(m)), lambda j: (NAT_LR, 0)),
        scratch_shapes=[pltpu.VMEM((m, LR_W), F32)],
        input_output_aliases={3: 0},
        compiler_params=_cparams("arbitrary"),
    )(h_t, dlr_f, dlr_b, main)


def _pad_rows(a, rows):
    return jnp.pad(a, ((0, rows - a.shape[0]), (0, 0)))


def _rows128(a):
    a = a.reshape(-1, 128)
    return _pad_rows(a, -(-a.shape[0] // 8) * 8)


def _pack(arrs):
    return jnp.concatenate([_rows128(a) for a in arrs], axis=0)


def _unpack(buf, like):
    out, start = [], 0
    for a in like:
        rows = a.size // 128
        out.append(buf[start:start + rows].reshape(a.shape))
        start += -(-rows // 8) * 8
    return out


def kernel(x, norm_g, w_in, w_gk_f, b_gk_f, w_gk_b, b_gk_b, gla_norm_g, conv_w, conv_b, w_out, final_g, loss_target, m_norm_g, m_w_in, m_w_gk_f, m_b_gk_f, m_w_gk_b, m_b_gk_b, m_gla_norm_g, m_conv_w, m_conv_b, m_w_out, m_final_g, v_norm_g, v_w_in, v_w_gk_f, v_b_gk_f, v_w_gk_b, v_b_gk_b, v_gla_norm_g, v_conv_w, v_conv_b, v_w_out, v_final_g):
    px, py, pc = _position()
    me = _blk(px, py, pc)
    seq = x.shape[1]
    x2d, tgt = x[0], loss_target[0]
    tt = min(256, seq)

    small_s = jnp.concatenate([jnp.concatenate([w_gk_f[0], w_gk_b[0]], axis=1), _pad_rows(conv_w[0], 8)], axis=0)
    order = sum(jnp.where(2 * px + py == k, jnp.asarray(tiles + (0,), jnp.int32), 0) for k, tiles in enumerate(TILE_ORDER))
    proj, lr, h_t, w_nat, wout_all, small_all = _gather_inproj(x2d, norm_g, w_in[0].T, w_out[0], small_s, order,
                                                               min(1024, seq))
    w_out_full = wout_all.reshape(MIX_W, D_MODEL)
    wgk_cols = 512 // N_DEV
    wgk_f_full = small_all[:, 0:RANK, 0:wgk_cols].transpose(1, 0, 2).reshape(RANK, QK_W)
    wgk_b_full = small_all[:, 0:RANK, wgk_cols:2 * wgk_cols].transpose(1, 0, 2).reshape(RANK, QK_W)
    conv_w_full = _pad_rows(small_all[:, RANK:RANK + 3, :].transpose(1, 0, 2).reshape(3, CONV_W), 8)
    zr = lambda n: jnp.zeros((n, QK_W), F32)
    wgk_f_pad = jnp.concatenate([wgk_f_full, zr(LR_W - RANK)], axis=0).astype(BF16)
    wgk_b_pad = jnp.concatenate([zr(RANK), wgk_b_full, zr(LR_W - 2 * RANK)], axis=0).astype(BF16)

    o_f, o_b, st_f, st_b = _gla_fwd(proj, lr, wgk_f_pad, wgk_b_pad, b_gk_f, b_gk_b, tt)
    tmix = min(512, seq)
    y_t, conv, dx2, dx2b, loss_p, dfg_p = _mix_out_loss(o_f, o_b, proj, x2d, tgt, gla_norm_g, conv_w_full, conv_b,
                                                        w_out_full, final_g.reshape(1, D_MODEL), tmix)

    dp_gates, do, dconv, dgg_p, dcb_p = _mix_bwd(dx2b, o_f, o_b, proj, conv, gla_norm_g, w_out_full, tmix)
    dp_ch, dcw_p = _conv_bwd(dconv, proj, conv_w_full, tmix)
    dqkv_f, dlr_f, dqkv_b, dlr_b, dwf_p, dwb_p, dbf_p, dbb_p = _gla_bwd(
        proj, lr, do, st_f, st_b, wgk_f_pad, wgk_b_pad, b_gk_f, b_gk_b, tt)
    dw_nat = _weight_grad_in(h_t, dqkv_f, dqkv_b, dp_gates, dp_ch, dlr_f, dlr_b)

    dw_out, sib_in = _weight_grad_out(y_t, dx2b, min(1024, seq), dw_nat)
    part_out = dw_out.reshape(N_DEV, MIX_W // N_DEV, D_MODEL)
    core = jnp.reshape(pc, (1,)).astype(jnp.int32)
    chip = jnp.reshape(2 * px + py, (1,)).astype(jnp.int32)
    sums_in, sib_out = _chip_sums(dw_nat, sib_in, core, D_MODEL, "chip_sums_in", riding=part_out)
    sums_out = _chip_sums(part_out, sib_out, core, D_MODEL, "chip_sums_out")
    grad_x2d, dng_p, far_in, far_out = _input_grad(dqkv_f, dqkv_b, dp_gates, dp_ch, dlr_f, dlr_b, w_nat, x2d, norm_g, dx2,
                                                   [sums_in, sums_out], tmix)
    pieces = [dng_p, dbf_p, dbb_p, dgg_p, dcb_p, dfg_p[0], dwf_p[0:RANK], dwb_p[RANK:2 * RANK], dcw_p[0:3], loss_p[0]]
    g_window, small_tot = _final_sum(sums_in, far_in, chip, _pack(pieces), 512, "final_sum_in")
    g_in_t = lax.dynamic_slice_in_dim(g_window, 4 * pc, SHARD_W, axis=0)
    g_w_out, d_w_out, nm_w_out, nv_w_out = _final_sum_adamw(sums_out, far_out, chip, w_out[0], m_w_out[0], v_w_out[0],
                                                            256, "adamw_out")
    flat = lambda a: a[0].T.reshape(SHARD_W, D_MODEL // 128, 128)
    unflat = lambda a: a.reshape(SHARD_W, D_MODEL).T
    d_flat, m_flat, v_flat = _adamw_rows(g_in_t.reshape(SHARD_W, D_MODEL // 128, 128), flat(w_in), flat(m_w_in),
                                         flat(v_w_in), 300, "adamw_in")
    g_w_in, d_w_in, nm_w_in, nv_w_in = g_in_t.T, unflat(d_flat), unflat(m_flat), unflat(v_flat)

    tot = _unpack(small_tot, pieces)
    g_norm_g, g_b_gk_f, g_b_gk_b, g_gla, g_conv_b, g_final = tot[:6]
    g_wgk_f = lax.dynamic_slice_in_dim(tot[6], me * wgk_cols, wgk_cols, axis=1)[None]
    g_wgk_b = lax.dynamic_slice_in_dim(tot[7], me * wgk_cols, wgk_cols, axis=1)[None]
    g_conv_w = lax.dynamic_slice_in_dim(tot[8], me * 128, 128, axis=1)[None]
    loss = tot[9][0]

    small_g = [g_norm_g, g_b_gk_f, g_b_gk_b, g_gla, g_conv_b, g_final, g_wgk_f, g_wgk_b, g_conv_w]
    small_w = [norm_g, b_gk_f, b_gk_b, gla_norm_g, conv_b, final_g, w_gk_f, w_gk_b, conv_w]
    small_m = [m_norm_g, m_b_gk_f, m_b_gk_b, m_gla_norm_g, m_conv_b, m_final_g, m_w_gk_f, m_w_gk_b, m_conv_w]
    small_v = [v_norm_g, v_b_gk_f, v_b_gk_b, v_gla_norm_g, v_conv_b, v_final_g, v_w_gk_f, v_w_gk_b, v_conv_w]
    d_s, m_s, v_s = _adamw_small(_pack(small_g), _pack(small_w), _pack(small_m), _pack(small_v))
    d_l, m_l, v_l = _unpack(d_s, small_w), _unpack(m_s, small_w), _unpack(v_s, small_w)

    def ordered(sm, big_in, big_out):
        return [sm[0], big_in[None], sm[6], sm[1], sm[7], sm[2], sm[3], sm[8], sm[4], big_out[None], sm[5]]

    grads = ordered(small_g, g_w_in, g_w_out)
    deltas = ordered(d_l, d_w_in, d_w_out)
    new_m = ordered(m_l, nm_w_in, nm_w_out)
    new_v = ordered(v_l, nv_w_in, nv_w_out)
    return (loss, grad_x2d[None], *grads, *deltas, *new_m, *new_v)
```

```python
import functools

import jax
import jax.numpy as jnp
from jax import lax
from jax.experimental import pallas as pl
from jax.experimental.pallas import tpu as pltpu

F32 = jnp.float32
BF16 = jnp.bfloat16
MESH = pl.DeviceIdType.MESH

N_DEV = 8
D_MODEL = 1024
HEADS = 4
DK = 128
DV = 256
QK_W = HEADS * DK
V_W = HEADS * DV
CONV_W = 1024
MIX_W = V_W + CONV_W
CHUNK = 64
RANK = 16
IN_W = 7200
SHARD_W = IN_W // N_DEV
MAIN_W = 7168
LR_W = 128
OFF_Q, OFF_K, OFF_V, OFF_ZA, OFF_B, OFF_ZC, OFF_C, OFF_H = 0, 512, 1024, 2048, 3072, 4096, 5120, 6144
QKV_W, GATES_W, CH_W = 2048, 3072, 2048
NAT_ZA, NAT_LR, NAT_B, NAT_C, NAT_ZC = 2048, 3072, 3104, 4128, 6176
EPS = 1e-6
GATE_SCALE = 1.0 / 16.0
QSCALE = DK ** -0.5
REF_F, LAST_F = CHUNK // 2, CHUNK - 1
REF_B, LAST_B = CHUNK - 1 - CHUNK // 2, 0

ADAM_LR = 0.001
ADAM_B1 = 0.9
ADAM_B2 = 0.999
ADAM_EPS = 1e-08
ADAM_WD = 0.01
ADAM_STEP = 10

VMEM_LIMIT = 56 * 1024 * 1024


def _cparams(*sem):
    return pltpu.CompilerParams(dimension_semantics=sem, vmem_limit_bytes=VMEM_LIMIT)


def _dot(a, b):
    return jnp.dot(a, b, preferred_element_type=F32)


def _dot_nt(a, b):
    return lax.dot_general(a, b, (((1,), (1,)), ((), ())), preferred_element_type=F32)


def _dot_tn(a, b):
    return lax.dot_general(a, b, (((0,), (0,)), ((), ())), preferred_element_type=F32)


def _sigmoid(z):
    return jax.nn.sigmoid(z)


def _position():
    return lax.axis_index("x"), lax.axis_index("y"), lax.axis_index("c")


def _blk(px, py, pc):
    return 4 * px + 2 * py + pc


EDGE = 16
SHIFTED_ROWS = 912
BODY_ROWS = SHIFTED_ROWS - 2 * EDGE


def _first_tile_row(blk, px):
    return EDGE * (56 * blk + px)


def _edge_tiles():
    tiles = {}
    for blk in range(N_DEV):
        first = _first_tile_row(blk, blk // 4)
        tiles.setdefault(first, []).append((blk, 0))
        tiles.setdefault(first + EDGE + BODY_ROWS, []).append((blk, 1))
    return tiles


def _peer_copies(srcs, outs, send_sems, recv_sems):
    x, y, c = _position()
    me = _blk(x, y, c)
    copies = []
    for a, (src, out) in enumerate(zip(srcs, outs)):
        k = 0
        for dx in (0, 1):
            for dy in (0, 1):
                for dc in (0, 1):
                    if dx + dy + dc == 0:
                        continue
                    peer = (1 - x if dx else x, 1 - y if dy else y, 1 - c if dc else c)
                    copies.append(pltpu.make_async_remote_copy(
                        src_ref=src, dst_ref=out.at[me], send_sem=send_sems.at[a * 7 + k],
                        recv_sem=recv_sems.at[a * 7 + k], device_id=peer, device_id_type=MESH))
                    k += 1
    return copies


def _route_chips():
    x, y, c = _position()
    along_x = c == 0
    return [(jnp.where(along_x, 1 - x, x), jnp.where(along_x, y, 1 - y)),
            (jnp.where(along_x, x, 1 - x), jnp.where(along_x, 1 - y, y)), (1 - x, 1 - y)]


WINDOW_ROWS = SHARD_W + 4


def _window_start(k, parity):
    return 2 * SHARD_W * k + (SHARD_W - 4) * parity


def _owner_block(part, k, parity):
    if part.ndim == 3:
        return part.at[2 * k + parity]
    return part.at[pl.ds(pl.multiple_of(_window_start(k, parity), 8), WINDOW_ROWS)]


def _block_shape(part):
    return part.shape[1:] if part.ndim == 3 else (WINDOW_ROWS, part.shape[1])


def _sibling_copies(part, out, send_sems, recv_sems):
    x, y, c = _position()
    return [pltpu.make_async_remote_copy(src_ref=_owner_block(part, k, 1 - c), dst_ref=out.at[k],
                                         send_sem=send_sems.at[k], recv_sem=recv_sems.at[k],
                                         device_id=(x, y, 1 - c), device_id_type=MESH)
            for k in range(4)]


def _start_all(copies):
    for cp in copies:
        cp.start()


def _wait_all(copies):
    for cp in copies:
        cp.wait_recv()
    for cp in copies:
        cp.wait_send()


def _chip_sums(part, from_sibling, core, tc, name, riding=None):
    rows, cols = _block_shape(part)
    nj = cols // tc

    def body(core_ref, p_ref, s_ref, *rest):
        if riding is None:
            (o_ref,) = rest
        else:
            ride_in, o_ref, ride_out, send_sems, recv_sems = rest
            k, j = pl.program_id(0), pl.program_id(1)

            @pl.when(jnp.logical_and(k == 0, j == 0))
            def _():
                _start_all(_sibling_copies(ride_in, ride_out, send_sems, recv_sems))

        o_ref[0] = (p_ref[...].reshape(rows, tc) + s_ref[0]).astype(BF16)

        if riding is not None:
            @pl.when(jnp.logical_and(k == 3, j == nj - 1))
            def _():
                _wait_all(_sibling_copies(ride_in, ride_out, send_sems, recv_sems))

    hbm = pl.BlockSpec(memory_space=pl.ANY)
    sums = jax.ShapeDtypeStruct((4, rows, cols), BF16)
    tile_out = pl.BlockSpec((1, rows, tc), lambda k, j, core_ref: (k, 0, j))
    if part.ndim == 3:
        mine = pl.BlockSpec((1, rows, tc), lambda k, j, core_ref: (2 * k + core_ref[0], 0, j))
    else:
        mine = pl.BlockSpec((pl.Element(rows), pl.Element(tc)),
                            lambda k, j, core_ref: (pl.multiple_of(_window_start(k, core_ref[0]), 8),
                                                    pl.multiple_of(j * tc, 128)))
    in_specs = [mine, pl.BlockSpec((1, rows, tc), lambda k, j, core_ref: (k, 0, j))]
    if riding is None:
        out_shape, out_specs, scratch, args = sums, tile_out, [], (core, part, from_sibling)
    else:
        out_shape = (sums, jax.ShapeDtypeStruct((4,) + _block_shape(riding), F32))
        out_specs, in_specs = (tile_out, hbm), in_specs + [hbm]
        scratch = [pltpu.SemaphoreType.DMA((4,)), pltpu.SemaphoreType.DMA((4,))]
        args = (core, part, from_sibling, riding)
    return pl.pallas_call(
        body, name=name, out_shape=out_shape,
        grid_spec=pltpu.PrefetchScalarGridSpec(num_scalar_prefetch=1, grid=(4, nj), in_specs=in_specs,
                                               out_specs=out_specs, scratch_shapes=scratch),
        compiler_params=_cparams("arbitrary", "arbitrary"),
    )(*args)


def _sum_chips(s_ref, r_ref):
    f = lambda a: a.astype(F32)
    return (f(s_ref[0]) + f(r_ref[0])) + f(r_ref[1])


def _final_sum(sums, from_chips, chip, small, tc, name):
    _, rows, cols = sums.shape
    nj = cols // tc

    def body(chip_ref, s_ref, r_ref, sm_ref, g_out, tot_ref, all_ref, send_sems, recv_sems):
        j = pl.program_id(0)
        me = _blk(*_position())

        @pl.when(j == 0)
        def _():
            all_ref[me] = sm_ref[...]
            _start_all(_peer_copies((all_ref.at[me],), (all_ref,), send_sems, recv_sems))

        g_out[...] = _sum_chips(s_ref, r_ref)

        @pl.when(j == nj - 1)
        def _():
            _wait_all(_peer_copies((all_ref.at[me],), (all_ref,), send_sems, recv_sems))
            acc = all_ref[0]
            for d in range(1, N_DEV):
                acc = acc + all_ref[d]
            tot_ref[...] = acc

    whole = pl.BlockSpec(small.shape, lambda j, chip_ref: (0, 0))
    return pl.pallas_call(
        body, name=name,
        out_shape=(jax.ShapeDtypeStruct((rows, cols), F32), jax.ShapeDtypeStruct(small.shape, F32)),
        grid_spec=pltpu.PrefetchScalarGridSpec(
            num_scalar_prefetch=1, grid=(nj,),
            in_specs=[pl.BlockSpec((1, rows, tc), lambda j, chip_ref: (chip_ref[0], 0, j)),
                      pl.BlockSpec((2, rows, tc), lambda j, chip_ref: (0, 0, j)), whole],
            out_specs=(pl.BlockSpec((rows, tc), lambda j, chip_ref: (0, j)), whole),
            scratch_shapes=[pltpu.VMEM((N_DEV,) + small.shape, F32), pltpu.SemaphoreType.DMA((7,)),
                            pltpu.SemaphoreType.DMA((7,))]),
        compiler_params=_cparams("arbitrary"),
    )(chip, sums, from_chips, small)


def _adamw_rows(g, w, m, v, tr, name):
    rows = g.shape[0]

    def body(g_ref, w_ref, m_ref, v_ref, d_out, m_out, v_out):
        delta, m_new, v_new = _adamw(w_ref[...], g_ref[...], m_ref[...], v_ref[...])
        d_out[...] = delta
        m_out[...] = m_new
        v_out[...] = v_new

    tile = pl.BlockSpec((tr,) + g.shape[1:], lambda r: (r, 0, 0))
    shp = jax.ShapeDtypeStruct(g.shape, F32)
    return pl.pallas_call(
        body, name=name, out_shape=(shp, shp, shp), grid=(rows // tr,),
        in_specs=[tile] * 4, out_specs=(tile, tile, tile),
        compiler_params=_cparams("arbitrary"),
    )(g, w, m, v)


def _adamw(w, g, m, v):
    m = ADAM_B1 * m + (1.0 - ADAM_B1) * g
    v = ADAM_B2 * v + (1.0 - ADAM_B2) * (g * g)
    m_hat = m / (1.0 - ADAM_B1 ** ADAM_STEP)
    v_hat = v / (1.0 - ADAM_B2 ** ADAM_STEP)
    delta = -ADAM_LR * (m_hat / (jnp.sqrt(v_hat) + ADAM_EPS) + ADAM_WD * w)
    return delta, m, v


def _final_sum_adamw(sums, from_chips, chip, w, m, v, tr, name):
    rows, cols = w.shape

    def body(chip_ref, s_ref, r_ref, w_ref, m_ref, v_ref, g_out, d_out, m_out, v_out):
        g = _sum_chips(s_ref, r_ref)
        delta, m_new, v_new = _adamw(w_ref[...], g, m_ref[...], v_ref[...])
        g_out[...] = g
        d_out[...] = delta
        m_out[...] = m_new
        v_out[...] = v_new

    tile = pl.BlockSpec((tr, cols), lambda r, chip_ref: (r, 0))
    shp = jax.ShapeDtypeStruct((rows, cols), F32)
    return pl.pallas_call(
        body, name=name,
        out_shape=(shp, shp, shp, shp),
        grid_spec=pltpu.PrefetchScalarGridSpec(
            num_scalar_prefetch=1, grid=(rows // tr,),
            in_specs=[pl.BlockSpec((1, tr, cols), lambda r, chip_ref: (chip_ref[0], r, 0)),
                      pl.BlockSpec((2, tr, cols), lambda r, chip_ref: (0, r, 0)),
                      tile, tile, tile],
            out_specs=(tile, tile, tile, tile)),
        compiler_params=_cparams("arbitrary"),
    )(chip, sums, from_chips, w, m, v)


def _adamw_small(g, w, m, v):
    def body(g_ref, w_ref, m_ref, v_ref, d_out, m_out, v_out):
        delta, m_new, v_new = _adamw(w_ref[...], g_ref[...], m_ref[...], v_ref[...])
        d_out[...] = delta
        m_out[...] = m_new
        v_out[...] = v_new

    vmem = pl.BlockSpec(memory_space=pltpu.VMEM)
    shp = jax.ShapeDtypeStruct(g.shape, F32)
    return pl.pallas_call(body, name="adamw_small", out_shape=(shp, shp, shp),
                          in_specs=[vmem] * 4, out_specs=(vmem, vmem, vmem))(g, w, m, v)


TILE_ROWS = (0, 1024, NAT_ZA, NAT_B, NAT_ZC, NAT_C, NAT_C + CONV_W)


TILE_ORDER = ((0, 1, 2, 3, 5, 6, 4), (2, 1, 0, 4, 3, 5, 6), (5, 6, 0, 4, 1, 2, 3), (4, 6, 2, 3, 5, 0, 1))
EARLY_SWEEP, NEIGHBOUR_SWEEP, DIAGONAL_SWEEP = 1, 2, 4
PIECES, W_IN_PIECES, OTHER_PIECES = 4, (0, 1), (2, 3)


def _gather_inproj(x2d, norm_g, shard_t, w_out_s, small_s, order, tm):
    seq = x2d.shape[0]
    tn = CONV_W
    ni, nj = seq // tm, MAIN_W // tn
    first_sweep = lambda j, i, order_ref: jnp.where(j == 0, i, ni - 1)
    last_sweep = lambda j, i, order_ref: jnp.where(j == nj - 1, i, 0)
    edge_tiles = _edge_tiles()

    def body(order_ref, x_ref, g_ref, shard_ref, wout_ref, sm_ref, proj_ref, lr_ref, ht_ref, w_nat, wout_all, sm_all,
             w_all, h_all, edges, stage, wout_b, sm_b, send_sems, recv_sems, local_sems):
        j, i = pl.program_id(0), pl.program_id(1)
        rows = pl.ds(pl.multiple_of(i * tm, tm), tm)
        x, y, c = _position()
        me, here, sibling = _blk(x, y, c), (x, y, c), (x, y, 1 - c)
        chips = _route_chips()
        sibling_chips = [chips[1], chips[0], chips[2]]

        def pieces(px, py, pc):
            blk = _blk(px, py, pc)
            body_rows = pl.ds(pl.multiple_of(_first_tile_row(blk, px) + EDGE, EDGE), BODY_ROWS)
            return [w_all.at[body_rows], edges.at[blk], wout_all.at[blk], sm_all.at[blk]]

        def copy(a, k, block, to, staged=None):
            ref = pieces(*block)[a]
            return pltpu.make_async_remote_copy(src_ref=ref if staged is None else staged, dst_ref=ref,
                                                send_sem=send_sems.at[a * 7 + k], recv_sem=recv_sems.at[a * 7 + k],
                                                device_id=to, device_id_type=MESH)

        def own_copies(group, slots=(0, 1, 2)):
            targets = [sibling] + [(*chips[n], c) for n in range(2)]
            staged = [None, None, wout_b, sm_b]
            return [copy(a, k, here, targets[k], staged[a]) for k in slots for a in group]

        def relays(group):
            return [copy(a, 3, (*chips[0], c), (*chips[1], c)) for a in group]

        def forwards(n, group):
            return [copy(a, 4 + n, (*chips[n], c), sibling) for a in group]

        def keep_own():
            return [pltpu.make_async_copy(wout_b, wout_all.at[me], local_sems.at[0]),
                    pltpu.make_async_copy(sm_b, sm_all.at[me], local_sems.at[1])]

        def keep_weight():
            return pltpu.make_async_copy(w_all, w_nat, local_sems.at[2])

        def take(ns, group, relay=True):
            for n in ns:
                for a in group:
                    copy(a, 1 + n, (*chips[n], c), here).wait_recv()
                _start_all((relays(group) if n == 0 and relay else []) + forwards(n, group))

        def take_passed_on(ns, group):
            for n in ns:
                for a in group:
                    copy(a, 4 + n, (*sibling_chips[n], 1 - c), here).wait_recv()

        def arrive(ns, group):
            take(ns, group)
            take_passed_on(ns, group)

        def per_core_and_row(step):
            for core in range(2):
                for row in range(2):
                    pl.when(jnp.logical_and(c == core, y == row))(functools.partial(step, core, row))

        def start_own(core, row):
            now = (0, 1 + core) if core == row else (0, 2 - core, 1 + core)
            _start_all(own_copies(W_IN_PIECES, now))
            wout_b[...] = wout_ref[...].astype(BF16)
            sm_b[...] = sm_ref[...]
            _start_all(own_copies(OTHER_PIECES, now) + keep_own())

        def take_early(core, row):
            if core == row:
                _start_all(own_copies(W_IN_PIECES, (2 - core,)) + own_copies(OTHER_PIECES, (2 - core,)))
                take((1 - core,), W_IN_PIECES, relay=False)
            else:
                take_passed_on((core,), W_IN_PIECES)

        def take_neighbours(core, row):
            if core == row:
                _start_all(relays(W_IN_PIECES) if core == 1 else [])
                take((core,), W_IN_PIECES)
                take_passed_on((0, 1), W_IN_PIECES)
            else:
                take((0, 1), W_IN_PIECES)
                take_passed_on((1 - core,), W_IN_PIECES)

        early_blk = _blk(x, 1 - y, y)

        def add_edge_tiles(stage):
            for row, parts in edge_tiles.items():
                ready = 0
                for blk, _ in parts:
                    away = (x != blk // 4).astype(jnp.int32) + (y != (blk // 2) % 2).astype(jnp.int32)
                    late = jnp.where(away == 1, jnp.where(early_blk == blk, 1, 2), jnp.where(away == 2, 3 + blk % 2, 0))
                    ready = jnp.maximum(ready, late)

                @pl.when(ready == stage)
                def _(row=row, parts=parts):
                    tile = edges[parts[0][0], parts[0][1]].astype(F32)
                    for blk, side in parts[1:]:
                        tile = tile + edges[blk, side].astype(F32)
                    w_all[row:row + EDGE, :] = tile.astype(BF16)

        @pl.when(jnp.logical_and(j == 0, i == 0))
        def _():
            last = SHARD_W // 8 * 8
            for col in range(0, D_MODEL, 128):
                cols = slice(col, col + 128)
                stage[0:last, :] = shard_ref[0:last, cols]
                stage[last:, :] = jnp.zeros((SHIFTED_ROWS - last, 128), F32)
                stage[last:SHARD_W, :] = shard_ref[last:SHARD_W, cols]
                for k in range(EDGE // 4):
                    @pl.when(me % 4 == k)
                    def _(k=k, cols=cols):
                        moved = pltpu.roll(stage[...], 4 * k, 0) if k else stage[...]
                        pieces(*here)[0][:, cols] = moved[EDGE:EDGE + BODY_ROWS].astype(BF16)
                        edges[me, 0, :, cols] = moved[0:EDGE].astype(BF16)
                        edges[me, 1, :, cols] = moved[EDGE + BODY_ROWS:].astype(BF16)
            per_core_and_row(start_own)
            for a in W_IN_PIECES:
                copy(a, 0, sibling, here).wait_recv()
            add_edge_tiles(0)

        @pl.when(jnp.logical_and(j == EARLY_SWEEP, i == 0))
        def _():
            per_core_and_row(take_early)
            add_edge_tiles(1)

        @pl.when(jnp.logical_and(j == NEIGHBOUR_SWEEP, i == 0))
        def _():
            per_core_and_row(take_neighbours)
            add_edge_tiles(2)

        for core in range(2):
            @pl.when(jnp.logical_and(j == DIAGONAL_SWEEP + core, i == 0))
            def _(core=core):
                pl.when(c == core)(lambda: take((2,), W_IN_PIECES))
                pl.when(c != core)(lambda: take_passed_on((2,), W_IN_PIECES))
                add_edge_tiles(3 + core)
                if core == 0:
                    arrive((0, 1), OTHER_PIECES)
                else:
                    keep_weight().start()

        @pl.when(jnp.logical_and(j == nj - 1, i == 0))
        def _():
            arrive((2,), OTHER_PIECES)

        @pl.when(j == 0)
        def _():
            xv = x_ref[...]
            r = lax.rsqrt(jnp.mean(xv * xv, axis=-1, keepdims=True) + EPS)
            h = (xv * r) * g_ref[...]
            h_all[rows, :] = h.astype(BF16)
            ht_ref[...] = h.T.astype(BF16)

        tile = order_ref[j]
        row = 0
        for k, start in enumerate(TILE_ROWS):
            row = row + jnp.where(tile == k, start // 32, 0)
        w_tile = w_all[pl.ds(pl.multiple_of(row * 32, 32), tn), :]
        proj_ref[...] = _dot_nt(h_all[rows, :], w_tile).astype(BF16)

        @pl.when(j == nj - 1)
        def _():
            lr_ref[...] = _dot_nt(h_all[rows, :], w_all[NAT_LR:NAT_LR + LR_W, :])

        @pl.when(jnp.logical_and(j == nj - 1, i == ni - 1))
        def _():
            everything = range(PIECES)
            passed_on = [cp for n in range(3) for cp in forwards(n, everything)]
            for cp in own_copies(everything) + relays(everything) + passed_on:
                cp.wait_send()
            for a in OTHER_PIECES:
                copy(a, 0, sibling, here).wait_recv()
            for cp in keep_own() + [keep_weight()]:
                cp.wait()

    const = lambda shape: pl.BlockSpec(shape, lambda j, i, order_ref: (0,) * len(shape))
    hbm = pl.BlockSpec(memory_space=pl.ANY)
    vmem = pl.BlockSpec(memory_space=pltpu.VMEM)
    return pl.pallas_call(
        body, name="gather_inproj",
        out_shape=(jax.ShapeDtypeStruct((seq, MAIN_W), BF16), jax.ShapeDtypeStruct((seq, LR_W), F32),
                   jax.ShapeDtypeStruct((D_MODEL, seq), BF16), jax.ShapeDtypeStruct((IN_W, D_MODEL), BF16),
                   jax.ShapeDtypeStruct((N_DEV,) + w_out_s.shape, BF16),
                   jax.ShapeDtypeStruct((N_DEV,) + small_s.shape, F32)),
        grid_spec=pltpu.PrefetchScalarGridSpec(
            num_scalar_prefetch=1, grid=(nj, ni),
            in_specs=[pl.BlockSpec((tm, D_MODEL), lambda j, i, order_ref: (first_sweep(j, i, order_ref), 0)),
                      const((1, D_MODEL)), vmem, vmem, const(small_s.shape)],
            out_specs=(pl.BlockSpec((tm, tn), lambda j, i, order_ref: (i, order_ref[j])),
                       pl.BlockSpec((tm, LR_W), lambda j, i, order_ref: (last_sweep(j, i, order_ref), 0)),
                       pl.BlockSpec((D_MODEL, tm), lambda j, i, order_ref: (0, first_sweep(j, i, order_ref))),
                       hbm, hbm, hbm),
            scratch_shapes=[pltpu.VMEM((IN_W, D_MODEL), BF16), pltpu.VMEM((seq, D_MODEL), BF16),
                            pltpu.VMEM((N_DEV, 2, EDGE, D_MODEL), BF16), pltpu.VMEM((SHIFTED_ROWS, 128), F32),
                            pltpu.VMEM(w_out_s.shape, BF16), pltpu.VMEM(small_s.shape, F32),
                            pltpu.SemaphoreType.DMA((7 * PIECES,)), pltpu.SemaphoreType.DMA((7 * PIECES,)),
                            pltpu.SemaphoreType.DMA((3,))]),
        compiler_params=_cparams("arbitrary", "arbitrary"),
    )(order, x2d, norm_g, shard_t, w_out_s, small_s)


def _block_masks(tt):
    row = lax.broadcasted_iota(jnp.int32, (tt, tt), 0)
    col = lax.broadcasted_iota(jnp.int32, (tt, tt), 1)
    same = jnp.right_shift(row, 6) == jnp.right_shift(col, 6)
    return (jnp.logical_and(same, col <= row), jnp.logical_and(same, col >= row), jnp.logical_and(same, col > row))


def _dot_split3(ones_mat, x):
    x1 = x.astype(BF16)
    r1 = x - x1.astype(F32)
    x2 = r1.astype(BF16)
    x3 = (r1 - x2.astype(F32)).astype(BF16)
    return (_dot(ones_mat, x3) + _dot(ones_mat, x2)) + _dot(ones_mat, x1)


def _log_gate(logits):
    return (jnp.minimum(logits, 0.0) - jnp.log(1.0 + jnp.exp(-jnp.abs(logits)))) * GATE_SCALE


def _chunk_column_mask(tt):
    nc = tt // CHUNK
    row = lax.broadcasted_iota(jnp.int32, (tt, nc * DK), 0)
    col = lax.broadcasted_iota(jnp.int32, (tt, nc * DK), 1)
    return jnp.right_shift(row, 6) == jnp.right_shift(col, 7)


def _chunked(mask, x, nc):
    wide = jnp.concatenate([x] * nc, axis=1)
    return jnp.where(mask, wide, jnp.zeros_like(wide))


def _gla_fwd(proj, lr, wgk_f, wgk_b, bgk_f, bgk_b, tt):
    seq = proj.shape[0]
    nb, nc, nch = seq // tt, tt // CHUNK, seq // CHUNK

    def body(qf, kf, vf, lrf, qb, kb, vb, lrb, wf, wb, bf, bb, of, ob, stf, stb, s_scr, qs_s, ks_s, qin_s, kout_s):
        @pl.when(pl.program_id(0) == 0)
        def _():
            s_scr[...] = jnp.zeros(s_scr.shape, F32)

        low, upp, sup = _block_masks(tt)
        dirs = ((qf, kf, vf, lrf, wf, bf, of, stf, low, low, REF_F, LAST_F, list(range(nc))),
                (qb, kb, vb, lrb, wb, bb, ob, stb, upp, sup, REF_B, LAST_B, list(reversed(range(nc)))))
        for d, (q_r, k_r, v_r, lr_r, w_r, b_r, o_r, st_r, cum, mask, ref, last, order) in enumerate(dirs):
            logits = _dot(lr_r[...].astype(BF16), w_r[...]) + b_r[...]
            b = _dot_split3(cum.astype(BF16), _log_gate(logits))
            decs = []
            for c in range(nc):
                rows = slice(c * CHUNK, (c + 1) * CHUNK)
                bc = b[rows]
                b_ref, b_last = bc[ref:ref + 1], bc[last:last + 1]
                qc = q_r[rows, :].astype(F32) * QSCALE
                kc = k_r[rows, :].astype(F32)
                qs_s[rows, :] = (qc * jnp.exp(bc - b_ref)).astype(BF16)
                ks_s[rows, :] = (kc * jnp.exp(b_ref - bc)).astype(BF16)
                qin_s[rows, :] = (qc * jnp.exp(bc)).astype(BF16)
                kout_s[rows, :] = (kc * jnp.exp(b_last - bc)).astype(BF16)
                decs.append(jnp.exp(b_last))
            for h in range(HEADS):
                ksl = slice(h * DK, (h + 1) * DK)
                vsl = slice(h * DV, (h + 1) * DV)
                v = v_r[:, vsl].astype(BF16)
                att = jnp.where(mask, _dot_nt(qs_s[:, ksl], ks_s[:, ksl]), 0.0).astype(BF16)
                o_intra = _dot(att, v)
                st = s_scr[d * HEADS + h]
                for c in order:
                    rows = slice(c * CHUNK, (c + 1) * CHUNK)
                    stb = st.astype(BF16)
                    st_r[c, h] = stb
                    o_r[rows, vsl] = (o_intra[rows] + _dot_nt(qin_s[rows, ksl], stb)).astype(BF16)
                    st = st * decs[c][:, ksl] + _dot_tn(v[rows], kout_s[rows, ksl])
                s_scr[d * HEADS + h] = st

    fw = lambda i: (i, 0)
    bw = lambda i: (nb - 1 - i, 0)
    const = lambda i: (0, 0)

    def tok_specs(m):
        return [pl.BlockSpec((tt, QK_W), lambda i: (m(i)[0], OFF_Q // QK_W)),
                pl.BlockSpec((tt, QK_W), lambda i: (m(i)[0], OFF_K // QK_W)),
                pl.BlockSpec((tt, V_W), lambda i: (m(i)[0], OFF_V // V_W)),
                pl.BlockSpec((tt, LR_W), m)]

    st_shape = jax.ShapeDtypeStruct((nch, HEADS, DV, DK), BF16)
    o_shape = jax.ShapeDtypeStruct((seq, V_W), BF16)
    operand = pltpu.VMEM((tt, QK_W), BF16)
    return pl.pallas_call(
        body, name="gla_fwd",
        out_shape=(o_shape, o_shape, st_shape, st_shape),
        grid=(nb,),
        in_specs=tok_specs(fw) + tok_specs(bw) + [
            pl.BlockSpec((LR_W, QK_W), const), pl.BlockSpec((LR_W, QK_W), const),
            pl.BlockSpec((1, QK_W), const), pl.BlockSpec((1, QK_W), const)],
        out_specs=(pl.BlockSpec((tt, V_W), fw), pl.BlockSpec((tt, V_W), bw),
                   pl.BlockSpec((nc, HEADS, DV, DK), lambda i: (i, 0, 0, 0)),
                   pl.BlockSpec((nc, HEADS, DV, DK), lambda i: (nb - 1 - i, 0, 0, 0))),
        scratch_shapes=[pltpu.VMEM((2 * HEADS, DV, DK), F32), operand, operand, operand, operand],
        compiler_params=_cparams("arbitrary"),
    )(proj, proj, proj, lr, proj, proj, proj, lr, wgk_f, wgk_b, bgk_f, bgk_b)


def _head_norm(o, gain):
    outs, rinv = [], []
    for h in range(HEADS):
        oh = o[:, h * DV:(h + 1) * DV]
        r = lax.rsqrt(jnp.mean(oh * oh, axis=-1, keepdims=True) + EPS)
        outs.append((oh * r) * gain)
        rinv.append(r)
    return jnp.concatenate(outs, axis=1), rinv


def _shift_rows(u, prev_row, next_row):
    n = u.shape[0]
    row = lax.broadcasted_iota(jnp.int32, (n, 1), 0)
    up = jnp.where(row == 0, prev_row, pltpu.roll(u, 1, 0))
    un = jnp.where(row == n - 1, next_row, pltpu.roll(u, n - 1, 0))
    return up, un


HALO = 16


def _halo_specs(tm, seq, col_block):
    per = tm // HALO
    last = seq // HALO - 1
    return [pl.BlockSpec((HALO, CONV_W), lambda i: (jnp.maximum(i * per - 1, 0), col_block)),
            pl.BlockSpec((HALO, CONV_W), lambda i: (jnp.minimum((i + 1) * per, last), col_block))]


def _f32(ref):
    return ref[...].astype(F32)


def _last_row(ref):
    return ref[HALO - 1:HALO, :].astype(F32)


def _first_row(ref):
    return ref[0:1, :].astype(F32)


def _mix_out_loss(o_f, o_b, proj, x2d, tgt, gla_g, conv_w, conv_b, w_out, final_g, tm):
    seq = x2d.shape[0]
    nt = seq // tm

    def body(of, ob, za, bg, cg, hc, zc, cprev, cnext, hprev, hnext, x_ref, t_ref, gg, cw, cb, wo, fg,
             yt_ref, conv_ref, dx2_ref, dx2b_ref, loss_ref, dfg_ref):
        i = pl.program_id(0)

        @pl.when(i == 0)
        def _():
            loss_ref[...] = jnp.zeros(loss_ref.shape, F32)
            dfg_ref[...] = jnp.zeros(dfg_ref.shape, F32)

        on, _ = _head_norm(_f32(of) + _f32(ob), gg[...])
        zav = _f32(za)
        y_a = on * (zav * _sigmoid(zav))
        u = _f32(cg) * _f32(hc)
        prev_row = jnp.where(i > 0, _last_row(cprev) * _last_row(hprev), 0.0)
        next_row = jnp.where(i < nt - 1, _first_row(cnext) * _first_row(hnext), 0.0)
        up, un = _shift_rows(u, prev_row, next_row)
        conv = (cw[0:1, :] * up + cw[1:2, :] * u + cw[2:3, :] * un) + cb[...]
        conv_ref[...] = conv.astype(BF16)
        zcv = _f32(zc)
        y_c = _f32(bg) * conv * (zcv * _sigmoid(zcv))
        y = jnp.concatenate([y_a, y_c], axis=1)
        yt_ref[...] = y.T.astype(BF16)
        x2 = x_ref[...] + _dot(y.astype(BF16), wo[...])
        r = lax.rsqrt(jnp.mean(x2 * x2, axis=-1, keepdims=True) + EPS)
        xn = x2 * r
        err = xn * fg[...] - t_ref[...]
        loss_ref[...] += 0.5 * jnp.sum(jnp.mean(err * err, axis=-1, keepdims=True))
        dyf = err * (1.0 / D_MODEL)
        dfg_ref[...] += jnp.sum(dyf * xn, axis=0, keepdims=True)
        dxn = dyf * fg[...]
        dx2 = r * dxn - xn * (r * jnp.mean(dxn * xn, axis=-1, keepdims=True))
        dx2_ref[...] = dx2
        dx2b_ref[...] = dx2.astype(BF16)

    def col(off):
        return pl.BlockSpec((tm, CONV_W), lambda i: (i, off // CONV_W))

    rowt = pl.BlockSpec((tm, D_MODEL), lambda i: (i, 0))
    const = lambda shape: pl.BlockSpec(shape, lambda i: (0, 0))
    return pl.pallas_call(
        body, name="mix_out_loss",
        out_shape=(jax.ShapeDtypeStruct((MIX_W, seq), BF16), jax.ShapeDtypeStruct((seq, CONV_W), BF16),
                   jax.ShapeDtypeStruct((seq, D_MODEL), F32), jax.ShapeDtypeStruct((seq, D_MODEL), BF16),
                   jax.ShapeDtypeStruct((8, 128), F32), jax.ShapeDtypeStruct((1, D_MODEL), F32)),
        grid=(nt,),
        in_specs=[rowt, rowt, col(OFF_ZA), col(OFF_B), col(OFF_C), col(OFF_H), col(OFF_ZC)]
        + _halo_specs(tm, seq, OFF_C // CONV_W) + _halo_specs(tm, seq, OFF_H // CONV_W)
        + [rowt, rowt, const((1, DV)), const((8, CONV_W)), const((1, CONV_W)), const((MIX_W, D_MODEL)),
           const((1, D_MODEL))],
        out_specs=(pl.BlockSpec((MIX_W, tm), lambda i: (0, i)), rowt, rowt, rowt, const((8, 128)),
                   const((1, D_MODEL))),
        compiler_params=_cparams("arbitrary"),
    )(o_f, o_b, proj, proj, proj, proj, proj, proj, proj, proj, proj, x2d, tgt, gla_g, conv_w, conv_b, w_out, final_g)


def _dsilu(z, s):
    return s * (1.0 + z * (1.0 - s))


def _mix_bwd(dx2b, o_f, o_b, proj, conv, gla_g, w_out, tm):
    seq = dx2b.shape[0]

    def body(dx, of, ob, za, bg, zc, cv, gg, wo, dg_ref, do_ref, dconv_ref, dgg_ref, dcb_ref):
        @pl.when(pl.program_id(0) == 0)
        def _():
            dgg_ref[...] = jnp.zeros(dgg_ref.shape, F32)
            dcb_ref[...] = jnp.zeros(dcb_ref.shape, F32)

        dy = _dot_nt(dx[...], wo[...])
        dy_a, dy_c = dy[:, :V_W], dy[:, V_W:]
        zcv, bgv, convv = _f32(zc), _f32(bg), _f32(cv)
        sc = _sigmoid(zcv)
        szc = zcv * sc
        dg_ref[:, CONV_W:2 * CONV_W] = (dy_c * convv * szc).astype(BF16)
        dconv = dy_c * bgv * szc
        dconv_ref[...] = dconv.astype(BF16)
        dcb_ref[...] += jnp.sum(dconv, axis=0, keepdims=True)
        dg_ref[:, 2 * CONV_W:] = (dy_c * bgv * convv * _dsilu(zcv, sc)).astype(BF16)

        o = _f32(of) + _f32(ob)
        gain = gg[...]
        on, rinv = _head_norm(o, gain)
        zav = _f32(za)
        sa = _sigmoid(zav)
        dg_ref[:, :CONV_W] = (dy_a * on * _dsilu(zav, sa)).astype(BF16)
        don = dy_a * (zav * sa)
        dgg = jnp.zeros((1, DV), F32)
        dos = []
        for h in range(HEADS):
            sl = slice(h * DV, (h + 1) * DV)
            oh, r, dh = o[:, sl], rinv[h], don[:, sl]
            ohn = oh * r
            dgg = dgg + jnp.sum(dh * ohn, axis=0, keepdims=True)
            dn = dh * gain
            dos.append(r * dn - ohn * (r * jnp.mean(dn * ohn, axis=-1, keepdims=True)))
        dgg_ref[...] += dgg
        do_ref[...] = jnp.concatenate(dos, axis=1).astype(BF16)

    def col(off):
        return pl.BlockSpec((tm, CONV_W), lambda i: (i, off // CONV_W))

    rowt = pl.BlockSpec((tm, D_MODEL), lambda i: (i, 0))
    const = lambda shape: pl.BlockSpec(shape, lambda i: (0, 0))
    return pl.pallas_call(
        body, name="mix_bwd",
        out_shape=(jax.ShapeDtypeStruct((seq, GATES_W), BF16), jax.ShapeDtypeStruct((seq, V_W), BF16),
                   jax.ShapeDtypeStruct((seq, CONV_W), BF16),
                   jax.ShapeDtypeStruct((1, DV), F32), jax.ShapeDtypeStruct((1, CONV_W), F32)),
        grid=(seq // tm,),
        in_specs=[rowt, rowt, rowt, col(OFF_ZA), col(OFF_B), col(OFF_ZC), rowt, const((1, DV)),
                  const((MIX_W, D_MODEL))],
        out_specs=(pl.BlockSpec((tm, GATES_W), lambda i: (i, 0)), rowt, rowt, const((1, DV)), const((1, CONV_W))),
        compiler_params=_cparams("arbitrary"),
    )(dx2b, o_f, o_b, proj, proj, proj, conv, gla_g, w_out)


def _conv_bwd(dconv, proj, conv_w, tm):
    seq = dconv.shape[0]
    nt = seq // tm

    def body(dc_in, dprev, dnext, cg, hc, cprev, cnext, hprev, hnext, cw, dch_ref, dcw_ref):
        i = pl.program_id(0)

        @pl.when(i == 0)
        def _():
            dcw_ref[...] = jnp.zeros(dcw_ref.shape, F32)

        first, lastt = i > 0, i < nt - 1
        dcv = _f32(dc_in)
        d_up, d_un = _shift_rows(dcv, jnp.where(first, _last_row(dprev), 0.0), jnp.where(lastt, _first_row(dnext), 0.0))
        cgv, hcv = _f32(cg), _f32(hc)
        u = cgv * hcv
        u_up, u_un = _shift_rows(u, jnp.where(first, _last_row(cprev) * _last_row(hprev), 0.0),
                                 jnp.where(lastt, _first_row(cnext) * _first_row(hnext), 0.0))
        du = cw[0:1, :] * d_un + cw[1:2, :] * dcv + cw[2:3, :] * d_up
        dch_ref[:, :CONV_W] = (du * hcv).astype(BF16)
        dch_ref[:, CONV_W:] = (du * cgv).astype(BF16)
        dcw_ref[0:1, :] += jnp.sum(dcv * u_up, axis=0, keepdims=True)
        dcw_ref[1:2, :] += jnp.sum(dcv * u, axis=0, keepdims=True)
        dcw_ref[2:3, :] += jnp.sum(dcv * u_un, axis=0, keepdims=True)

    def col(off):
        return pl.BlockSpec((tm, CONV_W), lambda i: (i, off // CONV_W))

    rowt = pl.BlockSpec((tm, CONV_W), lambda i: (i, 0))
    const = lambda shape: pl.BlockSpec(shape, lambda i: (0, 0))
    return pl.pallas_call(
        body, name="conv_bwd",
        out_shape=(jax.ShapeDtypeStruct((seq, CH_W), BF16), jax.ShapeDtypeStruct((8, CONV_W), F32)),
        grid=(nt,),
        in_specs=[rowt] + _halo_specs(tm, seq, 0) + [col(OFF_C), col(OFF_H)]
        + _halo_specs(tm, seq, OFF_C // CONV_W) + _halo_specs(tm, seq, OFF_H // CONV_W) + [const((8, CONV_W))],
        out_specs=(pl.BlockSpec((tm, CH_W), lambda i: (i, 0)), const((8, CONV_W))),
        compiler_params=_cparams("arbitrary"),
    )(dconv, dconv, dconv, proj, proj, proj, proj, proj, proj, conv_w)


def _gla_bwd(proj, lr, do, st_f, st_b, wgk_f, wgk_b, bgk_f, bgk_b, tt):
    seq = proj.shape[0]
    nb, nc = seq // tt, tt // CHUNK

    def body(qf, kf, vf, lrf, dof, stf, qb, kb, vb, lrb, dob, stb, wf, wb, bf, bb,
             dqkv_f, dlr_f, dqkv_b, dlr_b, dwf, dwb, dbf, dbb,
             ds_scr, eq_s, ek_s, ein_s, eout_s, qs_s, ks_s, qin_s, kout_s, db_s, lg_s):
        @pl.when(pl.program_id(0) == 0)
        def _():
            ds_scr[...] = jnp.zeros(ds_scr.shape, F32)
            for r in (dwf, dwb, dbf, dbb):
                r[...] = jnp.zeros(r.shape, F32)

        low, upp, sup = _block_masks(tt)
        row = lax.broadcasted_iota(jnp.int32, (CHUNK, 1), 0)
        kmask = _chunk_column_mask(tt)
        dirs = ((qf, kf, vf, lrf, dof, stf, wf, bf, dqkv_f, dlr_f, dwf, dbf,
                 low, upp, low, REF_F, LAST_F, list(reversed(range(nc)))),
                (qb, kb, vb, lrb, dob, stb, wb, bb, dqkv_b, dlr_b, dwb, dbb,
                 upp, low, sup, REF_B, LAST_B, list(range(nc))))
        for d, (q_r, k_r, v_r, lr_r, do_r, st_r, w_r, b_r, dqkv_r, dlr_r, dw_r, db_r,
                cum, cum_t, mask, ref, last, order) in enumerate(dirs):
            lrv = lr_r[...].astype(BF16)
            wv = w_r[...]
            logits = _dot(lrv, wv) + b_r[...]
            lg_s[...] = logits
            b = _dot_split3(cum.astype(BF16), _log_gate(logits))
            decs = []
            for c in range(nc):
                rows = slice(c * CHUNK, (c + 1) * CHUNK)
                bc = b[rows]
                b_ref, b_last = bc[ref:ref + 1], bc[last:last + 1]
                qc = q_r[rows, :].astype(F32) * QSCALE
                kc = k_r[rows, :].astype(F32)
                e_q, e_k, e_in, e_out = jnp.exp(bc - b_ref), jnp.exp(b_ref - bc), jnp.exp(bc), jnp.exp(b_last - bc)
                eq_s[rows, :], ek_s[rows, :], ein_s[rows, :], eout_s[rows, :] = e_q, e_k, e_in, e_out
                qs_s[rows, :] = (qc * e_q).astype(BF16)
                ks_s[rows, :] = (kc * e_k).astype(BF16)
                qin_s[rows, :] = (qc * e_in).astype(BF16)
                kout_s[rows, :] = (kc * e_out).astype(BF16)
                decs.append(jnp.exp(b_last))
            for h in range(HEADS):
                ksl = slice(h * DK, (h + 1) * DK)
                vsl = slice(h * DV, (h + 1) * DV)
                v = v_r[:, vsl].astype(BF16)
                dov = do_r[:, vsl].astype(BF16)
                qsb, ksb = qs_s[:, ksl], ks_s[:, ksl]
                att = jnp.where(mask, _dot_nt(qsb, ksb), 0.0).astype(BF16)
                datt = jnp.where(mask, _dot_nt(dov, v), 0.0).astype(BF16)
                dqs = _dot(datt, ksb)
                dks = _dot_tn(datt, qsb)
                dv_intra = _dot_tn(att, dov)
                g_t = _dot_tn(dov, _chunked(kmask, qin_s[:, ksl], nc))
                ds = ds_scr[d * HEADS + h]
                for c in order:
                    rows = slice(c * CHUNK, (c + 1) * CHUNK)
                    dsb = ds.astype(BF16)
                    s_prev = st_r[c, h]
                    dk_out = _dot(v[rows], dsb)
                    dq_in = _dot(dov[rows], s_prev)
                    dv = dv_intra[rows] + _dot_nt(kout_s[rows, ksl], dsb)
                    dqkv_r[rows, OFF_V + h * DV:OFF_V + (h + 1) * DV] = dv.astype(BF16)
                    dec = decs[c][:, ksl]
                    ddec = jnp.sum(ds * s_prev.astype(F32), axis=0, keepdims=True)
                    e_out = eout_s[rows, ksl]
                    qc = q_r[rows, ksl].astype(F32) * QSCALE
                    kc = k_r[rows, ksl].astype(F32)
                    dq = dqs[rows] * eq_s[rows, ksl] + dq_in * ein_s[rows, ksl]
                    dk = dks[rows] * ek_s[rows, ksl] + dk_out * e_out
                    dqkv_r[rows, OFF_Q + h * DK:OFF_Q + (h + 1) * DK] = (dq * QSCALE).astype(BF16)
                    dqkv_r[rows, OFF_K + h * DK:OFF_K + (h + 1) * DK] = dk.astype(BF16)
                    tail = jnp.sum(dk_out * (kc * e_out), axis=0, keepdims=True) + ddec * dec
                    db_s[rows, ksl] = (qc * dq - kc * dk) + jnp.where(row == last, tail, 0.0)
                    ds = ds * dec + g_t[:, c * DK:(c + 1) * DK]
                ds_scr[d * HEADS + h] = ds
            dg = _dot_split3(cum_t.astype(BF16), db_s[...])
            dlogit = (dg * GATE_SCALE) * _sigmoid(-lg_s[...])
            dlb = dlogit.astype(BF16)
            dlr_r[...] = _dot_nt(dlb, wv)
            dw_r[...] += _dot_tn(lrv, dlb)
            db_r[...] += jnp.sum(dlogit, axis=0, keepdims=True)

    fw = lambda i: (nb - 1 - i, 0)
    bw = lambda i: (i, 0)
    const = lambda i: (0, 0)

    def tok_specs(m):
        return [pl.BlockSpec((tt, QK_W), lambda i: (m(i)[0], OFF_Q // QK_W)),
                pl.BlockSpec((tt, QK_W), lambda i: (m(i)[0], OFF_K // QK_W)),
                pl.BlockSpec((tt, V_W), lambda i: (m(i)[0], OFF_V // V_W)),
                pl.BlockSpec((tt, LR_W), m),
                pl.BlockSpec((tt, V_W), m),
                pl.BlockSpec((nc, HEADS, DV, DK), lambda i: (m(i)[0], 0, 0, 0))]

    dqkv = jax.ShapeDtypeStruct((seq, QK_W + QK_W + V_W), BF16)
    dlr = jax.ShapeDtypeStruct((seq, LR_W), F32)
    dw = jax.ShapeDtypeStruct((LR_W, QK_W), F32)
    dbias = jax.ShapeDtypeStruct((1, QK_W), F32)
    return pl.pallas_call(
        body, name="gla_bwd",
        out_shape=(dqkv, dlr, dqkv, dlr, dw, dw, dbias, dbias),
        grid=(nb,),
        in_specs=tok_specs(fw) + tok_specs(bw) + [
            pl.BlockSpec((LR_W, QK_W), const), pl.BlockSpec((LR_W, QK_W), const),
            pl.BlockSpec((1, QK_W), const), pl.BlockSpec((1, QK_W), const)],
        out_specs=(pl.BlockSpec((tt, QK_W + QK_W + V_W), fw), pl.BlockSpec((tt, LR_W), fw),
                   pl.BlockSpec((tt, QK_W + QK_W + V_W), bw), pl.BlockSpec((tt, LR_W), bw),
                   pl.BlockSpec((LR_W, QK_W), const), pl.BlockSpec((LR_W, QK_W), const),
                   pl.BlockSpec((1, QK_W), const), pl.BlockSpec((1, QK_W), const)),
        scratch_shapes=[pltpu.VMEM((2 * HEADS, DV, DK), F32)] + [pltpu.VMEM((tt, QK_W), F32)] * 4
        + [pltpu.VMEM((tt, QK_W), BF16)] * 4 + [pltpu.VMEM((tt, QK_W), F32)] * 2,
        compiler_params=_cparams("arbitrary"),
    )(proj, proj, proj, lr, do, st_f, proj, proj, proj, lr, do, st_b, wgk_f, wgk_b, bgk_f, bgk_b)


def _both_directions(f_ref, b_ref):
    return (_f32(f_ref) + _f32(b_ref)).astype(BF16)


def _input_grad(dqkv_f, dqkv_b, dp_gates, dp_ch, dlr_f, dlr_b, w_nat, x2d, norm_g, dx2, sums, tm):
    seq = x2d.shape[0]
    nt, n = seq // tm, len(sums)
    relay_step = (3 * nt) // 8

    def body(dqf, dqb, dg, dc, dlf, dlb, w, x_ref, g_ref, dx2_ref, *rest):
        ins, (gx_ref, dng_ref), outs = rest[:n], rest[n:n + 2], rest[n + 2:2 * n + 2]
        passing, joined = rest[2 * n + 2:3 * n + 2], rest[3 * n + 2:4 * n + 2]
        send_sems, recv_sems, local_sems = rest[4 * n + 2:]
        i = pl.program_id(0)
        c = lax.axis_index("c")
        first, second, diagonal = _route_chips()
        slot = lambda chip: 2 * chip[0] + chip[1]

        def remote(a, k, src, dst, to):
            return pltpu.make_async_remote_copy(src_ref=src, dst_ref=dst, send_sem=send_sems.at[3 * a + k],
                                                recv_sem=recv_sems.at[3 * a + k], device_id=(*to, c),
                                                device_id_type=MESH)

        direct = lambda a: remote(a, 0, ins[a].at[slot(first)], outs[a].at[0], first)
        for_second = lambda a: remote(a, 1, ins[a].at[slot(diagonal)], passing[a], first)
        joint = lambda a: remote(a, 2, joined[a], outs[a].at[1], second)
        own = lambda a: pltpu.make_async_copy(ins[a].at[slot(second)], joined[a], local_sems.at[a])

        @pl.when(i == 0)
        def _():
            _start_all([for_second(a) for a in range(n)] + [own(a) for a in range(n)] + [direct(a) for a in range(n)])
            dng_ref[...] = jnp.zeros(dng_ref.shape, F32)

        @pl.when(i == relay_step)
        def _():
            for a in range(n):
                for_second(a).wait_recv()
                own(a).wait()
                joined[a][...] = (joined[a][...].astype(F32) + passing[a][...].astype(F32)).astype(BF16)
                joint(a).start()

        dh = (_dot((dlf[...] + dlb[...]).astype(BF16), w[NAT_LR:NAT_LR + LR_W, :])
              + _dot(_both_directions(dqf, dqb), w[0:NAT_ZA, :])
              + _dot(dg[:, 0:CONV_W], w[NAT_ZA:NAT_LR, :]) + _dot(dg[:, CONV_W:2 * CONV_W], w[NAT_B:NAT_C, :])
              + _dot(dg[:, 2 * CONV_W:], w[NAT_ZC:IN_W, :]) + _dot(dc[...], w[NAT_C:NAT_ZC, :]))
        xv = x_ref[...]
        r = lax.rsqrt(jnp.mean(xv * xv, axis=-1, keepdims=True) + EPS)
        xn = xv * r
        dng_ref[...] += jnp.sum(dh * xn, axis=0, keepdims=True)
        dn = dh * g_ref[...]
        gx_ref[...] = (r * dn - xn * (r * jnp.mean(dn * xn, axis=-1, keepdims=True))) + dx2_ref[...]

        @pl.when(i == nt - 1)
        def _():
            for a in range(n):
                direct(a).wait_recv()
                joint(a).wait_recv()
            for a in range(n):
                for cp in (direct(a), for_second(a), joint(a)):
                    cp.wait_send()

    rowt = pl.BlockSpec((tm, D_MODEL), lambda i: (i, 0))
    seg = lambda width: pl.BlockSpec((tm, width), lambda i: (i, 0))
    resident = lambda rows: pl.BlockSpec((rows, D_MODEL), lambda i: (0, 0), pipeline_mode=pl.Buffered(1))
    hbm = pl.BlockSpec(memory_space=pl.ANY)
    blocks = [pltpu.VMEM(s.shape[1:], s.dtype) for s in sums]
    return pl.pallas_call(
        body, name="input_grad",
        out_shape=(jax.ShapeDtypeStruct((seq, D_MODEL), F32), jax.ShapeDtypeStruct((1, D_MODEL), F32))
        + tuple(jax.ShapeDtypeStruct((2,) + s.shape[1:], s.dtype) for s in sums),
        grid=(nt,),
        in_specs=[seg(QKV_W), seg(QKV_W), seg(GATES_W), seg(CH_W), seg(LR_W), seg(LR_W), resident(IN_W),
                  rowt, pl.BlockSpec((1, D_MODEL), lambda i: (0, 0)), rowt] + [hbm] * n,
        out_specs=(rowt, pl.BlockSpec((1, D_MODEL), lambda i: (0, 0))) + (hbm,) * n,
        scratch_shapes=blocks + blocks + [pltpu.SemaphoreType.DMA((3 * n,)), pltpu.SemaphoreType.DMA((3 * n,)),
                                          pltpu.SemaphoreType.DMA((n,))],
        compiler_params=_cparams("arbitrary"),
    )(dqkv_f, dqkv_b, dp_gates, dp_ch, dlr_f, dlr_b, w_nat, x2d, norm_g, dx2, *sums)


def _weight_grad_out(y_t, dx2b, tk, riding):
    m, seq = y_t.shape
    n = dx2b.shape[1]
    nk = seq // tk

    def body(a_ref, b_ref, ride_in, o_ref, ride_out, send_sems, recv_sems):
        k = pl.program_id(0)

        @pl.when(k == 0)
        def _():
            _start_all(_sibling_copies(ride_in, ride_out, send_sems, recv_sems))
            o_ref[...] = jnp.zeros(o_ref.shape, F32)

        o_ref[...] += _dot(a_ref[...], b_ref[...])

        @pl.when(k == nk - 1)
        def _():
            _wait_all(_sibling_copies(ride_in, ride_out, send_sems, recv_sems))

    hbm = pl.BlockSpec(memory_space=pl.ANY)
    return pl.pallas_call(
        body, name="wgrad_out",
        out_shape=(jax.ShapeDtypeStruct((m, n), F32), jax.ShapeDtypeStruct((4,) + _block_shape(riding), F32)),
        grid=(nk,),
        in_specs=[pl.BlockSpec((m, tk), lambda k: (0, k)), pl.BlockSpec((tk, n), lambda k: (k, 0)), hbm],
        out_specs=(pl.BlockSpec((m, n), lambda k: (0, 0)), hbm),
        scratch_shapes=[pltpu.SemaphoreType.DMA((4,)), pltpu.SemaphoreType.DMA((4,))],
        compiler_params=_cparams("arbitrary"),
    )(y_t, dx2b, riding)


def _weight_grad_in(h_t, dqkv_f, dqkv_b, dp_gates, dp_ch, dlr_f, dlr_b):
    m, seq = h_t.shape
    tn = 512
    n_qkv, n_gates, n_ch = QKV_W // tn, GATES_W // tn, CH_W // tn
    starts = ([k * tn for k in range(n_qkv)] + [NAT_ZA, NAT_ZA + tn, NAT_B, NAT_B + tn, NAT_ZC, NAT_ZC + tn]
              + [NAT_C + k * tn for k in range(n_ch)])

    def out_row(j):
        row = 0
        for k, start in enumerate(starts):
            row = row + jnp.where(j == k, start // 32, 0)
        return pl.multiple_of(row * 32, 32), 0

    def body(a_ref, bqf, bqb, bg, bc, o_ref, acc, bq):
        j = pl.program_id(0)

        @pl.when(j < n_qkv)
        def _():
            bq[...] = _both_directions(bqf, bqb)
            acc[...] = _dot(a_ref[...], bq[...])

        @pl.when(jnp.logical_and(j >= n_qkv, j < n_qkv + n_gates))
        def _():
            acc[...] = _dot(a_ref[...], bg[...])

        @pl.when(j >= n_qkv + n_gates)
        def _():
            acc[...] = _dot(a_ref[...], bc[...])

        o_ref[...] = acc[...].T

    resident = pl.BlockSpec((m, seq), lambda j: (0, 0), pipeline_mode=pl.Buffered(1))
    seg = lambda first, count: pl.BlockSpec((seq, tn), lambda j: (0, jnp.clip(j - first, 0, count - 1)))
    main = pl.pallas_call(
        body, name="wgrad_in",
        out_shape=jax.ShapeDtypeStruct((IN_W, m), F32),
        grid=(n_qkv + n_gates + n_ch,),
        in_specs=[resident, seg(0, n_qkv), seg(0, n_qkv), seg(n_qkv, n_gates), seg(n_qkv + n_gates, n_ch)],
        out_specs=pl.BlockSpec((pl.Element(tn), pl.Element(m)), out_row),
        scratch_shapes=[pltpu.VMEM((m, tn), F32), pltpu.VMEM((seq, tn), BF16)],
        compiler_params=_cparams("arbitrary"),
    )(h_t, dqkv_f, dqkv_b, dp_gates, dp_ch)

    def lr_body(a_ref, bf_ref, bb_ref, full_ref, o_ref, acc):
        acc[...] = _dot(a_ref[...], (bf_ref[...] + bb_ref[...]).astype(BF16))
        o_ref[...] = acc[...].T[0:2 * RANK, :]

    whole = lambda shape: pl.BlockSpec(shape, lambda j: (0, 0))
    return pl.pallas_call(
        lr_body, name="wgrad_lr",
        out_shape=jax.ShapeDtypeStruct((IN_W, m), F32),
        grid=(1,),
        in_specs=[whole((m, seq)), whole((seq, LR_W)), whole((seq, LR_W)), pl.BlockSpec(memory_space=pl.ANY)],
        out_specs=pl.BlockSpec((pl.Element(2 * RANK), pl.Element(m)), lambda j: (NAT_LR, 0)),
        scratch_shapes=[pltpu.VMEM((m, LR_W), F32)],
        input_output_aliases={3: 0},
        compiler_params=_cparams("arbitrary"),
    )(h_t, dlr_f, dlr_b, main)


def _pad_rows(a, rows):
    return jnp.pad(a, ((0, rows - a.shape[0]), (0, 0)))


def _rows128(a):
    a = a.reshape(-1, 128)
    return _pad_rows(a, -(-a.shape[0] // 8) * 8)


def _pack(arrs):
    return jnp.concatenate([_rows128(a) for a in arrs], axis=0)


def _unpack(buf, like):
    out, start = [], 0
    for a in like:
        rows = a.size // 128
        out.append(buf[start:start + rows].reshape(a.shape))
        start += -(-rows // 8) * 8
    return out


def kernel(x, norm_g, w_in, w_gk_f, b_gk_f, w_gk_b, b_gk_b, gla_norm_g, conv_w, conv_b, w_out, final_g, loss_target, m_norm_g, m_w_in, m_w_gk_f, m_b_gk_f, m_w_gk_b, m_b_gk_b, m_gla_norm_g, m_conv_w, m_conv_b, m_w_out, m_final_g, v_norm_g, v_w_in, v_w_gk_f, v_b_gk_f, v_w_gk_b, v_b_gk_b, v_gla_norm_g, v_conv_w, v_conv_b, v_w_out, v_final_g):
    px, py, pc = _position()
    me = _blk(px, py, pc)
    seq = x.shape[1]
    x2d, tgt = x[0], loss_target[0]
    tt = min(256, seq)

    small_s = jnp.concatenate([jnp.concatenate([w_gk_f[0], w_gk_b[0]], axis=1), _pad_rows(conv_w[0], 8)], axis=0)
    order = sum(jnp.where(2 * px + py == k, jnp.asarray(tiles + (0,), jnp.int32), 0) for k, tiles in enumerate(TILE_ORDER))
    proj, lr, h_t, w_nat, wout_all, small_all = _gather_inproj(x2d, norm_g, w_in[0].T, w_out[0], small_s, order,
                                                               min(1024, seq))
    w_out_full = wout_all.reshape(MIX_W, D_MODEL)
    wgk_cols = 512 // N_DEV
    wgk_f_full = small_all[:, 0:RANK, 0:wgk_cols].transpose(1, 0, 2).reshape(RANK, QK_W)
    wgk_b_full = small_all[:, 0:RANK, wgk_cols:2 * wgk_cols].transpose(1, 0, 2).reshape(RANK, QK_W)
    conv_w_full = _pad_rows(small_all[:, RANK:RANK + 3, :].transpose(1, 0, 2).reshape(3, CONV_W), 8)
    zr = lambda n: jnp.zeros((n, QK_W), F32)
    wgk_f_pad = jnp.concatenate([wgk_f_full, zr(LR_W - RANK)], axis=0).astype(BF16)
    wgk_b_pad = jnp.concatenate([zr(RANK), wgk_b_full, zr(LR_W - 2 * RANK)], axis=0).astype(BF16)

    o_f, o_b, st_f, st_b = _gla_fwd(proj, lr, wgk_f_pad, wgk_b_pad, b_gk_f, b_gk_b, tt)
    tmix = min(512, seq)
    y_t, conv, dx2, dx2b, loss_p, dfg_p = _mix_out_loss(o_f, o_b, proj, x2d, tgt, gla_norm_g, conv_w_full, conv_b,
                                                        w_out_full, final_g.reshape(1, D_MODEL), tmix)

    dp_gates, do, dconv, dgg_p, dcb_p = _mix_bwd(dx2b, o_f, o_b, proj, conv, gla_norm_g, w_out_full, tmix)
    dp_ch, dcw_p = _conv_bwd(dconv, proj, conv_w_full, tmix)
    dqkv_f, dlr_f, dqkv_b, dlr_b, dwf_p, dwb_p, dbf_p, dbb_p = _gla_bwd(
        proj, lr, do, st_f, st_b, wgk_f_pad, wgk_b_pad, b_gk_f, b_gk_b, tt)
    dw_nat = _weight_grad_in(h_t, dqkv_f, dqkv_b, dp_gates, dp_ch, dlr_f, dlr_b)

    dw_out, sib_in = _weight_grad_out(y_t, dx2b, min(1024, seq), dw_nat)
    part_out = dw_out.reshape(N_DEV, MIX_W // N_DEV, D_MODEL)
    core = jnp.reshape(pc, (1,)).astype(jnp.int32)
    chip = jnp.reshape(2 * px + py, (1,)).astype(jnp.int32)
    sums_in, sib_out = _chip_sums(dw_nat, sib_in, core, D_MODEL, "chip_sums_in", riding=part_out)
    sums_out = _chip_sums(part_out, sib_out, core, D_MODEL, "chip_sums_out")
    grad_x2d, dng_p, far_in, far_out = _input_grad(dqkv_f, dqkv_b, dp_gates, dp_ch, dlr_f, dlr_b, w_nat, x2d, norm_g, dx2,
                                                   [sums_in, sums_out], min(256, seq))
    pieces = [dng_p, dbf_p, dbb_p, dgg_p, dcb_p, dfg_p[0], dwf_p[0:RANK], dwb_p[RANK:2 * RANK], dcw_p[0:3], loss_p[0]]
    g_window, small_tot = _final_sum(sums_in, far_in, chip, _pack(pieces), 512, "final_sum_in")
    g_in_t = lax.dynamic_slice_in_dim(g_window, 4 * pc, SHARD_W, axis=0)
    g_w_out, d_w_out, nm_w_out, nv_w_out = _final_sum_adamw(sums_out, far_out, chip, w_out[0], m_w_out[0], v_w_out[0],
                                                            256, "adamw_out")
    flat = lambda a: a[0].T.reshape(SHARD_W, D_MODEL // 128, 128)
    unflat = lambda a: a.reshape(SHARD_W, D_MODEL).T
    d_flat, m_flat, v_flat = _adamw_rows(g_in_t.reshape(SHARD_W, D_MODEL // 128, 128), flat(w_in), flat(m_w_in),
                                         flat(v_w_in), 300, "adamw_in")
    g_w_in, d_w_in, nm_w_in, nv_w_in = g_in_t.T, unflat(d_flat), unflat(m_flat), unflat(v_flat)

    tot = _unpack(small_tot, pieces)
    g_norm_g, g_b_gk_f, g_b_gk_b, g_gla, g_conv_b, g_final = tot[:6]
    g_wgk_f = lax.dynamic_slice_in_dim(tot[6], me * wgk_cols, wgk_cols, axis=1)[None]
    g_wgk_b = lax.dynamic_slice_in_dim(tot[7], me * wgk_cols, wgk_cols, axis=1)[None]
    g_conv_w = lax.dynamic_slice_in_dim(tot[8], me * 128, 128, axis=1)[None]
    loss = tot[9][0]

    small_g = [g_norm_g, g_b_gk_f, g_b_gk_b, g_gla, g_conv_b, g_final, g_wgk_f, g_wgk_b, g_conv_w]
    small_w = [norm_g, b_gk_f, b_gk_b, gla_norm_g, conv_b, final_g, w_gk_f, w_gk_b, conv_w]
    small_m = [m_norm_g, m_b_gk_f, m_b_gk_b, m_gla_norm_g, m_conv_b, m_final_g, m_w_gk_f, m_w_gk_b, m_conv_w]
    small_v = [v_norm_g, v_b_gk_f, v_b_gk_b, v_gla_norm_g, v_conv_b, v_final_g, v_w_gk_f, v_w_gk_b, v_conv_w]
    d_s, m_s, v_s = _adamw_small(_pack(small_g), _pack(small_w), _pack(small_m), _pack(small_v))
    d_l, m_l, v_l = _unpack(d_s, small_w), _unpack(m_s, small_w), _unpack(v_s, small_w)

    def ordered(sm, big_in, big_out):
        return [sm[0], big_in[None], sm[6], sm[1], sm[7], sm[2], sm[3], sm[8], sm[4], big_out[None], sm[5]]

    grads = ordered(small_g, g_w_in, g_w_out)
    deltas = ordered(d_l, d_w_in, d_w_out)
    new_m = ordered(m_l, nm_w_in, nm_w_out)
    new_v = ordered(v_l, nv_w_in, nv_w_out)
    return (loss, grad_x2d[None], *grads, *deltas, *new_m, *new_v)
```

```python
import functools

import jax
import jax.numpy as jnp
from jax import lax
from jax.experimental import pallas as pl
from jax.experimental.pallas import tpu as pltpu

F32 = jnp.float32
BF16 = jnp.bfloat16
MESH = pl.DeviceIdType.MESH

N_DEV = 8
D_MODEL = 1024
HEADS = 4
DK = 128
DV = 256
QK_W = HEADS * DK
V_W = HEADS * DV
CONV_W = 1024
MIX_W = V_W + CONV_W
CHUNK = 64
RANK = 16
IN_W = 7200
SHARD_W = IN_W // N_DEV
MAIN_W = 7168
LR_W = 128
OFF_Q, OFF_K, OFF_V, OFF_ZA, OFF_B, OFF_ZC, OFF_C, OFF_H = 0, 512, 1024, 2048, 3072, 4096, 5120, 6144
QKV_W, GATES_W, CH_W = 2048, 3072, 2048
NAT_ZA, NAT_LR, NAT_B, NAT_C, NAT_ZC = 2048, 3072, 3104, 4128, 6176
EPS = 1e-6
GATE_SCALE = 1.0 / 16.0
QSCALE = DK ** -0.5
REF_F, LAST_F = CHUNK // 2, CHUNK - 1
REF_B, LAST_B = CHUNK - 1 - CHUNK // 2, 0

ADAM_LR = 0.001
ADAM_B1 = 0.9
ADAM_B2 = 0.999
ADAM_EPS = 1e-08
ADAM_WD = 0.01
ADAM_STEP = 10

VMEM_LIMIT = 56 * 1024 * 1024


def _cparams(*sem):
    return pltpu.CompilerParams(dimension_semantics=sem, vmem_limit_bytes=VMEM_LIMIT)


def _dot(a, b):
    return jnp.dot(a, b, preferred_element_type=F32)


def _dot_nt(a, b):
    return lax.dot_general(a, b, (((1,), (1,)), ((), ())), preferred_element_type=F32)


def _dot_tn(a, b):
    return lax.dot_general(a, b, (((0,), (0,)), ((), ())), preferred_element_type=F32)


def _sigmoid(z):
    return jax.nn.sigmoid(z)


def _position():
    return lax.axis_index("x"), lax.axis_index("y"), lax.axis_index("c")


def _blk(px, py, pc):
    return 4 * px + 2 * py + pc


EDGE = 16
SHIFTED_ROWS = 912
BODY_ROWS = SHIFTED_ROWS - 2 * EDGE


def _first_tile_row(blk, px):
    return EDGE * (56 * blk + px)


def _edge_tiles():
    tiles = {}
    for blk in range(N_DEV):
        first = _first_tile_row(blk, blk // 4)
        tiles.setdefault(first, []).append((blk, 0))
        tiles.setdefault(first + EDGE + BODY_ROWS, []).append((blk, 1))
    return tiles


def _peer_copies(srcs, outs, send_sems, recv_sems):
    x, y, c = _position()
    me = _blk(x, y, c)
    copies = []
    for a, (src, out) in enumerate(zip(srcs, outs)):
        k = 0
        for dx in (0, 1):
            for dy in (0, 1):
                for dc in (0, 1):
                    if dx + dy + dc == 0:
                        continue
                    peer = (1 - x if dx else x, 1 - y if dy else y, 1 - c if dc else c)
                    copies.append(pltpu.make_async_remote_copy(
                        src_ref=src, dst_ref=out.at[me], send_sem=send_sems.at[a * 7 + k],
                        recv_sem=recv_sems.at[a * 7 + k], device_id=peer, device_id_type=MESH))
                    k += 1
    return copies


def _route_chips():
    x, y, c = _position()
    along_x = c == 0
    return [(jnp.where(along_x, 1 - x, x), jnp.where(along_x, y, 1 - y)),
            (jnp.where(along_x, x, 1 - x), jnp.where(along_x, 1 - y, y)), (1 - x, 1 - y)]


WINDOW_ROWS = SHARD_W + 4


def _window_start(k, parity):
    return 2 * SHARD_W * k + (SHARD_W - 4) * parity


def _owner_block(part, k, parity):
    if part.ndim == 3:
        return part.at[2 * k + parity]
    return part.at[pl.ds(pl.multiple_of(_window_start(k, parity), 8), WINDOW_ROWS)]


def _block_shape(part):
    return part.shape[1:] if part.ndim == 3 else (WINDOW_ROWS, part.shape[1])


def _sibling_copies(part, out, send_sems, recv_sems):
    x, y, c = _position()
    return [pltpu.make_async_remote_copy(src_ref=_owner_block(part, k, 1 - c), dst_ref=out.at[k],
                                         send_sem=send_sems.at[k], recv_sem=recv_sems.at[k],
                                         device_id=(x, y, 1 - c), device_id_type=MESH)
            for k in range(4)]


def _start_all(copies):
    for cp in copies:
        cp.start()


def _wait_all(copies):
    for cp in copies:
        cp.wait_recv()
    for cp in copies:
        cp.wait_send()


def _chip_sums(part, from_sibling, core, tc, name, riding=None):
    rows, cols = _block_shape(part)
    nj = cols // tc

    def body(core_ref, p_ref, s_ref, *rest):
        if riding is None:
            (o_ref,) = rest
        else:
            ride_in, o_ref, ride_out, send_sems, recv_sems = rest
            k, j = pl.program_id(0), pl.program_id(1)

            @pl.when(jnp.logical_and(k == 0, j == 0))
            def _():
                _start_all(_sibling_copies(ride_in, ride_out, send_sems, recv_sems))

        o_ref[0] = (p_ref[...].reshape(rows, tc) + s_ref[0]).astype(BF16)

        if riding is not None:
            @pl.when(jnp.logical_and(k == 3, j == nj - 1))
            def _():
                _wait_all(_sibling_copies(ride_in, ride_out, send_sems, recv_sems))

    hbm = pl.BlockSpec(memory_space=pl.ANY)
    sums = jax.ShapeDtypeStruct((4, rows, cols), BF16)
    tile_out = pl.BlockSpec((1, rows, tc), lambda k, j, core_ref: (k, 0, j))
    if part.ndim == 3:
        mine = pl.BlockSpec((1, rows, tc), lambda k, j, core_ref: (2 * k + core_ref[0], 0, j))
    else:
        mine = pl.BlockSpec((pl.Element(rows), pl.Element(tc)),
                            lambda k, j, core_ref: (pl.multiple_of(_window_start(k, core_ref[0]), 8),
                                                    pl.multiple_of(j * tc, 128)))
    in_specs = [mine, pl.BlockSpec((1, rows, tc), lambda k, j, core_ref: (k, 0, j))]
    if riding is None:
        out_shape, out_specs, scratch, args = sums, tile_out, [], (core, part, from_sibling)
    else:
        out_shape = (sums, jax.ShapeDtypeStruct((4,) + _block_shape(riding), F32))
        out_specs, in_specs = (tile_out, hbm), in_specs + [hbm]
        scratch = [pltpu.SemaphoreType.DMA((4,)), pltpu.SemaphoreType.DMA((4,))]
        args = (core, part, from_sibling, riding)
    return pl.pallas_call(
        body, name=name, out_shape=out_shape,
        grid_spec=pltpu.PrefetchScalarGridSpec(num_scalar_prefetch=1, grid=(4, nj), in_specs=in_specs,
                                               out_specs=out_specs, scratch_shapes=scratch),
        compiler_params=_cparams("arbitrary", "arbitrary"),
    )(*args)


def _sum_chips(s_ref, r_ref):
    f = lambda a: a.astype(F32)
    return (f(s_ref[0]) + f(r_ref[0])) + f(r_ref[1])


def _final_sum(sums, from_chips, chip, small, tc, name):
    _, rows, cols = sums.shape
    nj = cols // tc

    def body(chip_ref, s_ref, r_ref, sm_ref, g_out, tot_ref, all_ref, send_sems, recv_sems):
        j = pl.program_id(0)
        me = _blk(*_position())

        @pl.when(j == 0)
        def _():
            all_ref[me] = sm_ref[...]
            _start_all(_peer_copies((all_ref.at[me],), (all_ref,), send_sems, recv_sems))

        g_out[...] = _sum_chips(s_ref, r_ref)

        @pl.when(j == nj - 1)
        def _():
            _wait_all(_peer_copies((all_ref.at[me],), (all_ref,), send_sems, recv_sems))
            acc = all_ref[0]
            for d in range(1, N_DEV):
                acc = acc + all_ref[d]
            tot_ref[...] = acc

    whole = pl.BlockSpec(small.shape, lambda j, chip_ref: (0, 0))
    return pl.pallas_call(
        body, name=name,
        out_shape=(jax.ShapeDtypeStruct((rows, cols), F32), jax.ShapeDtypeStruct(small.shape, F32)),
        grid_spec=pltpu.PrefetchScalarGridSpec(
            num_scalar_prefetch=1, grid=(nj,),
            in_specs=[pl.BlockSpec((1, rows, tc), lambda j, chip_ref: (chip_ref[0], 0, j)),
                      pl.BlockSpec((2, rows, tc), lambda j, chip_ref: (0, 0, j)), whole],
            out_specs=(pl.BlockSpec((rows, tc), lambda j, chip_ref: (0, j)), whole),
            scratch_shapes=[pltpu.VMEM((N_DEV,) + small.shape, F32), pltpu.SemaphoreType.DMA((7,)),
                            pltpu.SemaphoreType.DMA((7,))]),
        compiler_params=_cparams("arbitrary"),
    )(chip, sums, from_chips, small)


def _adamw_rows(g, w, m, v, tr, name):
    rows = g.shape[0]

    def body(g_ref, w_ref, m_ref, v_ref, d_out, m_out, v_out):
        delta, m_new, v_new = _adamw(w_ref[...], g_ref[...], m_ref[...], v_ref[...])
        d_out[...] = delta
        m_out[...] = m_new
        v_out[...] = v_new

    tile = pl.BlockSpec((tr,) + g.shape[1:], lambda r: (r, 0, 0))
    shp = jax.ShapeDtypeStruct(g.shape, F32)
    return pl.pallas_call(
        body, name=name, out_shape=(shp, shp, shp), grid=(rows // tr,),
        in_specs=[tile] * 4, out_specs=(tile, tile, tile),
        compiler_params=_cparams("arbitrary"),
    )(g, w, m, v)


def _adamw(w, g, m, v):
    m = ADAM_B1 * m + (1.0 - ADAM_B1) * g
    v = ADAM_B2 * v + (1.0 - ADAM_B2) * (g * g)
    m_hat = m / (1.0 - ADAM_B1 ** ADAM_STEP)
    v_hat = v / (1.0 - ADAM_B2 ** ADAM_STEP)
    delta = -ADAM_LR * (m_hat / (jnp.sqrt(v_hat) + ADAM_EPS) + ADAM_WD * w)
    return delta, m, v


def _final_sum_adamw(sums, from_chips, chip, w, m, v, tr, name):
    rows, cols = w.shape

    def body(chip_ref, s_ref, r_ref, w_ref, m_ref, v_ref, g_out, d_out, m_out, v_out):
        g = _sum_chips(s_ref, r_ref)
        delta, m_new, v_new = _adamw(w_ref[...], g, m_ref[...], v_ref[...])
        g_out[...] = g
        d_out[...] = delta
        m_out[...] = m_new
        v_out[...] = v_new

    tile = pl.BlockSpec((tr, cols), lambda r, chip_ref: (r, 0))
    shp = jax.ShapeDtypeStruct((rows, cols), F32)
    return pl.pallas_call(
        body, name=name,
        out_shape=(shp, shp, shp, shp),
        grid_spec=pltpu.PrefetchScalarGridSpec(
            num_scalar_prefetch=1, grid=(rows // tr,),
            in_specs=[pl.BlockSpec((1, tr, cols), lambda r, chip_ref: (chip_ref[0], r, 0)),
                      pl.BlockSpec((2, tr, cols), lambda r, chip_ref: (0, r, 0)),
                      tile, tile, tile],
            out_specs=(tile, tile, tile, tile)),
        compiler_params=_cparams("arbitrary"),
    )(chip, sums, from_chips, w, m, v)


def _adamw_small(g, w, m, v):
    def body(g_ref, w_ref, m_ref, v_ref, d_out, m_out, v_out):
        delta, m_new, v_new = _adamw(w_ref[...], g_ref[...], m_ref[...], v_ref[...])
        d_out[...] = delta
        m_out[...] = m_new
        v_out[...] = v_new

    vmem = pl.BlockSpec(memory_space=pltpu.VMEM)
    shp = jax.ShapeDtypeStruct(g.shape, F32)
    return pl.pallas_call(body, name="adamw_small", out_shape=(shp, shp, shp),
                          in_specs=[vmem] * 4, out_specs=(vmem, vmem, vmem))(g, w, m, v)


TILE_ROWS = (0, 1024, NAT_ZA, NAT_B, NAT_ZC, NAT_C, NAT_C + CONV_W)


TILE_ORDER = ((0, 1, 2, 3, 5, 6, 4), (2, 1, 0, 4, 3, 5, 6), (5, 6, 0, 4, 1, 2, 3), (4, 6, 2, 3, 5, 0, 1))
EARLY_SWEEP, NEIGHBOUR_SWEEP, DIAGONAL_SWEEP = 1, 2, 4
PIECES, W_IN_PIECES, OTHER_PIECES = 4, (0, 1), (2, 3)


def _gather_inproj(x2d, norm_g, shard_t, w_out_s, small_s, order, tm):
    seq = x2d.shape[0]
    tn = CONV_W
    ni, nj = seq // tm, MAIN_W // tn
    first_sweep = lambda j, i, order_ref: jnp.where(j == 0, i, ni - 1)
    last_sweep = lambda j, i, order_ref: jnp.where(j == nj - 1, i, 0)
    edge_tiles = _edge_tiles()

    def body(order_ref, x_ref, g_ref, shard_ref, wout_ref, sm_ref, proj_ref, lr_ref, ht_ref, w_nat, wout_all, sm_all,
             w_all, h_all, edges, stage, wout_b, sm_b, send_sems, recv_sems, local_sems):
        j, i = pl.program_id(0), pl.program_id(1)
        rows = pl.ds(pl.multiple_of(i * tm, tm), tm)
        x, y, c = _position()
        me, here, sibling = _blk(x, y, c), (x, y, c), (x, y, 1 - c)
        chips = _route_chips()
        sibling_chips = [chips[1], chips[0], chips[2]]

        def pieces(px, py, pc):
            blk = _blk(px, py, pc)
            body_rows = pl.ds(pl.multiple_of(_first_tile_row(blk, px) + EDGE, EDGE), BODY_ROWS)
            return [w_all.at[body_rows], edges.at[blk], wout_all.at[blk], sm_all.at[blk]]

        def copy(a, k, block, to, staged=None):
            ref = pieces(*block)[a]
            return pltpu.make_async_remote_copy(src_ref=ref if staged is None else staged, dst_ref=ref,
                                                send_sem=send_sems.at[a * 7 + k], recv_sem=recv_sems.at[a * 7 + k],
                                                device_id=to, device_id_type=MESH)

        def own_copies(group, slots=(0, 1, 2)):
            targets = [sibling] + [(*chips[n], c) for n in range(2)]
            staged = [None, None, wout_b, sm_b]
            return [copy(a, k, here, targets[k], staged[a]) for k in slots for a in group]

        def relays(group):
            return [copy(a, 3, (*chips[0], c), (*chips[1], c)) for a in group]

        def forwards(n, group):
            return [copy(a, 4 + n, (*chips[n], c), sibling) for a in group]

        def keep_own():
            return [pltpu.make_async_copy(wout_b, wout_all.at[me], local_sems.at[0]),
                    pltpu.make_async_copy(sm_b, sm_all.at[me], local_sems.at[1])]

        def keep_weight():
            return pltpu.make_async_copy(w_all, w_nat, local_sems.at[2])

        def take(ns, group, relay=True):
            for n in ns:
                for a in group:
                    copy(a, 1 + n, (*chips[n], c), here).wait_recv()
                _start_all((relays(group) if n == 0 and relay else []) + forwards(n, group))

        def take_passed_on(ns, group):
            for n in ns:
                for a in group:
                    copy(a, 4 + n, (*sibling_chips[n], 1 - c), here).wait_recv()

        def arrive(ns, group):
            take(ns, group)
            take_passed_on(ns, group)

        def per_core_and_row(step):
            for core in range(2):
                for row in range(2):
                    pl.when(jnp.logical_and(c == core, y == row))(functools.partial(step, core, row))

        def start_own(core, row):
            now = (0, 1 + core) if core == row else (0, 2 - core, 1 + core)
            _start_all(own_copies(W_IN_PIECES, now))
            wout_b[...] = wout_ref[...].astype(BF16)
            sm_b[...] = sm_ref[...]
            _start_all(own_copies(OTHER_PIECES, now) + keep_own())

        def take_early(core, row):
            if core == row:
                _start_all(own_copies(W_IN_PIECES, (2 - core,)) + own_copies(OTHER_PIECES, (2 - core,)))
                take((1 - core,), W_IN_PIECES, relay=False)
            else:
                take_passed_on((core,), W_IN_PIECES)

        def take_neighbours(core, row):
            if core == row:
                _start_all(relays(W_IN_PIECES) if core == 1 else [])
                take((core,), W_IN_PIECES)
                take_passed_on((0, 1), W_IN_PIECES)
            else:
                take((0, 1), W_IN_PIECES)
                take_passed_on((1 - core,), W_IN_PIECES)

        early_blk = _blk(x, 1 - y, y)

        def add_edge_tiles(stage):
            for row, parts in edge_tiles.items():
                ready = 0
                for blk, _ in parts:
                    away = (x != blk // 4).astype(jnp.int32) + (y != (blk // 2) % 2).astype(jnp.int32)
                    late = jnp.where(away == 1, jnp.where(early_blk == blk, 1, 2), jnp.where(away == 2, 3 + blk % 2, 0))
                    ready = jnp.maximum(ready, late)

                @pl.when(ready == stage)
                def _(row=row, parts=parts):
                    tile = edges[parts[0][0], parts[0][1]].astype(F32)
                    for blk, side in parts[1:]:
                        tile = tile + edges[blk, side].astype(F32)
                    w_all[row:row + EDGE, :] = tile.astype(BF16)

        @pl.when(jnp.logical_and(j == 0, i == 0))
        def _():
            last = SHARD_W // 8 * 8
            for col in range(0, D_MODEL, 128):
                cols = slice(col, col + 128)
                stage[0:last, :] = shard_ref[0:last, cols]
                stage[last:, :] = jnp.zeros((SHIFTED_ROWS - last, 128), F32)
                stage[last:SHARD_W, :] = shard_ref[last:SHARD_W, cols]
                for k in range(EDGE // 4):
                    @pl.when(me % 4 == k)
                    def _(k=k, cols=cols):
                        moved = pltpu.roll(stage[...], 4 * k, 0) if k else stage[...]
                        pieces(*here)[0][:, cols] = moved[EDGE:EDGE + BODY_ROWS].astype(BF16)
                        edges[me, 0, :, cols] = moved[0:EDGE].astype(BF16)
                        edges[me, 1, :, cols] = moved[EDGE + BODY_ROWS:].astype(BF16)
            per_core_and_row(start_own)
            for a in W_IN_PIECES:
                copy(a, 0, sibling, here).wait_recv()
            add_edge_tiles(0)

        @pl.when(jnp.logical_and(j == EARLY_SWEEP, i == 0))
        def _():
            per_core_and_row(take_early)
            add_edge_tiles(1)

        @pl.when(jnp.logical_and(j == NEIGHBOUR_SWEEP, i == 0))
        def _():
            per_core_and_row(take_neighbours)
            add_edge_tiles(2)

        for core in range(2):
            @pl.when(jnp.logical_and(j == DIAGONAL_SWEEP + core, i == 0))
            def _(core=core):
                pl.when(c == core)(lambda: take((2,), W_IN_PIECES))
                pl.when(c != core)(lambda: take_passed_on((2,), W_IN_PIECES))
                add_edge_tiles(3 + core)
                if core == 0:
                    arrive((0, 1), OTHER_PIECES)
                else:
                    keep_weight().start()

        @pl.when(jnp.logical_and(j == nj - 1, i == 0))
        def _():
            arrive((2,), OTHER_PIECES)

        @pl.when(j == 0)
        def _():
            xv = x_ref[...]
            r = lax.rsqrt(jnp.mean(xv * xv, axis=-1, keepdims=True) + EPS)
            h = (xv * r) * g_ref[...]
            h_all[rows, :] = h.astype(BF16)
            ht_ref[...] = h.T.astype(BF16)

        tile = order_ref[j]
        row = 0
        for k, start in enumerate(TILE_ROWS):
            row = row + jnp.where(tile == k, start // 32, 0)
        w_tile = w_all[pl.ds(pl.multiple_of(row * 32, 32), tn), :]
        proj_ref[...] = _dot_nt(h_all[rows, :], w_tile).astype(BF16)

        @pl.when(j == nj - 1)
        def _():
            lr_ref[...] = _dot_nt(h_all[rows, :], w_all[NAT_LR:NAT_LR + LR_W, :])

        @pl.when(jnp.logical_and(j == nj - 1, i == ni - 1))
        def _():
            everything = range(PIECES)
            passed_on = [cp for n in range(3) for cp in forwards(n, everything)]
            for cp in own_copies(everything) + relays(everything) + passed_on:
                cp.wait_send()
            for a in OTHER_PIECES:
                copy(a, 0, sibling, here).wait_recv()
            for cp in keep_own() + [keep_weight()]:
                cp.wait()

    const = lambda shape: pl.BlockSpec(shape, lambda j, i, order_ref: (0,) * len(shape))
    hbm = pl.BlockSpec(memory_space=pl.ANY)
    vmem = pl.BlockSpec(memory_space=pltpu.VMEM)
    return pl.pallas_call(
        body, name="gather_inproj",
        out_shape=(jax.ShapeDtypeStruct((seq, MAIN_W), BF16), jax.ShapeDtypeStruct((seq, LR_W), F32),
                   jax.ShapeDtypeStruct((D_MODEL, seq), BF16), jax.ShapeDtypeStruct((IN_W, D_MODEL), BF16),
                   jax.ShapeDtypeStruct((N_DEV,) + w_out_s.shape, BF16),
                   jax.ShapeDtypeStruct((N_DEV,) + small_s.shape, F32)),
        grid_spec=pltpu.PrefetchScalarGridSpec(
            num_scalar_prefetch=1, grid=(nj, ni),
            in_specs=[pl.BlockSpec((tm, D_MODEL), lambda j, i, order_ref: (first_sweep(j, i, order_ref), 0)),
                      const((1, D_MODEL)), vmem, vmem, const(small_s.shape)],
            out_specs=(pl.BlockSpec((tm, tn), lambda j, i, order_ref: (i, order_ref[j])),
                       pl.BlockSpec((tm, LR_W), lambda j, i, order_ref: (last_sweep(j, i, order_ref), 0)),
                       pl.BlockSpec((D_MODEL, tm), lambda j, i, order_ref: (0, first_sweep(j, i, order_ref))),
                       hbm, hbm, hbm),
            scratch_shapes=[pltpu.VMEM((IN_W, D_MODEL), BF16), pltpu.VMEM((seq, D_MODEL), BF16),
                            pltpu.VMEM((N_DEV, 2, EDGE, D_MODEL), BF16), pltpu.VMEM((SHIFTED_ROWS, 128), F32),
                            pltpu.VMEM(w_out_s.shape, BF16), pltpu.VMEM(small_s.shape, F32),
                            pltpu.SemaphoreType.DMA((7 * PIECES,)), pltpu.SemaphoreType.DMA((7 * PIECES,)),
                            pltpu.SemaphoreType.DMA((3,))]),
        compiler_params=_cparams("arbitrary", "arbitrary"),
    )(order, x2d, norm_g, shard_t, w_out_s, small_s)


def _block_masks(tt):
    row = lax.broadcasted_iota(jnp.int32, (tt, tt), 0)
    col = lax.broadcasted_iota(jnp.int32, (tt, tt), 1)
    same = jnp.right_shift(row, 6) == jnp.right_shift(col, 6)
    return (jnp.logical_and(same, col <= row), jnp.logical_and(same, col >= row), jnp.logical_and(same, col > row))


def _dot_split3(ones_mat, x):
    x1 = x.astype(BF16)
    r1 = x - x1.astype(F32)
    x2 = r1.astype(BF16)
    x3 = (r1 - x2.astype(F32)).astype(BF16)
    return (_dot(ones_mat, x3) + _dot(ones_mat, x2)) + _dot(ones_mat, x1)


def _log_gate(logits):
    return (jnp.minimum(logits, 0.0) - jnp.log(1.0 + jnp.exp(-jnp.abs(logits)))) * GATE_SCALE


def _chunk_column_mask(tt):
    nc = tt // CHUNK
    row = lax.broadcasted_iota(jnp.int32, (tt, nc * DK), 0)
    col = lax.broadcasted_iota(jnp.int32, (tt, nc * DK), 1)
    return jnp.right_shift(row, 6) == jnp.right_shift(col, 7)


def _chunked(mask, x, nc):
    wide = jnp.concatenate([x] * nc, axis=1)
    return jnp.where(mask, wide, jnp.zeros_like(wide))


def _gla_fwd(proj, lr, wgk_f, wgk_b, bgk_f, bgk_b, tt):
    seq = proj.shape[0]
    nb, nc, nch = seq // tt, tt // CHUNK, seq // CHUNK

    def body(qf, kf, vf, lrf, qb, kb, vb, lrb, wf, wb, bf, bb, of, ob, stf, stb, s_scr, qs_s, ks_s, qin_s, kout_s):
        @pl.when(pl.program_id(0) == 0)
        def _():
            s_scr[...] = jnp.zeros(s_scr.shape, F32)

        low, upp, sup = _block_masks(tt)
        dirs = ((qf, kf, vf, lrf, wf, bf, of, stf, low, low, REF_F, LAST_F, list(range(nc))),
                (qb, kb, vb, lrb, wb, bb, ob, stb, upp, sup, REF_B, LAST_B, list(reversed(range(nc)))))
        for d, (q_r, k_r, v_r, lr_r, w_r, b_r, o_r, st_r, cum, mask, ref, last, order) in enumerate(dirs):
            logits = _dot(lr_r[...].astype(BF16), w_r[...]) + b_r[...]
            b = _dot_split3(cum.astype(BF16), _log_gate(logits))
            decs = []
            for c in range(nc):
                rows = slice(c * CHUNK, (c + 1) * CHUNK)
                bc = b[rows]
                b_ref, b_last = bc[ref:ref + 1], bc[last:last + 1]
                qc = q_r[rows, :].astype(F32) * QSCALE
                kc = k_r[rows, :].astype(F32)
                qs_s[rows, :] = (qc * jnp.exp(bc - b_ref)).astype(BF16)
                ks_s[rows, :] = (kc * jnp.exp(b_ref - bc)).astype(BF16)
                qin_s[rows, :] = (qc * jnp.exp(bc)).astype(BF16)
                kout_s[rows, :] = (kc * jnp.exp(b_last - bc)).astype(BF16)
                decs.append(jnp.exp(b_last))
            for h in range(HEADS):
                ksl = slice(h * DK, (h + 1) * DK)
                vsl = slice(h * DV, (h + 1) * DV)
                v = v_r[:, vsl].astype(BF16)
                att = jnp.where(mask, _dot_nt(qs_s[:, ksl], ks_s[:, ksl]), 0.0).astype(BF16)
                o_intra = _dot(att, v)
                st = s_scr[d * HEADS + h]
                for c in order:
                    rows = slice(c * CHUNK, (c + 1) * CHUNK)
                    stb = st.astype(BF16)
                    st_r[c, h] = stb
                    o_r[rows, vsl] = (o_intra[rows] + _dot_nt(qin_s[rows, ksl], stb)).astype(BF16)
                    st = st * decs[c][:, ksl] + _dot_tn(v[rows], kout_s[rows, ksl])
                s_scr[d * HEADS + h] = st

    fw = lambda i: (i, 0)
    bw = lambda i: (nb - 1 - i, 0)
    const = lambda i: (0, 0)

    def tok_specs(m):
        return [pl.BlockSpec((tt, QK_W), lambda i: (m(i)[0], OFF_Q // QK_W)),
                pl.BlockSpec((tt, QK_W), lambda i: (m(i)[0], OFF_K // QK_W)),
                pl.BlockSpec((tt, V_W), lambda i: (m(i)[0], OFF_V // V_W)),
                pl.BlockSpec((tt, LR_W), m)]

    st_shape = jax.ShapeDtypeStruct((nch, HEADS, DV, DK), BF16)
    o_shape = jax.ShapeDtypeStruct((seq, V_W), BF16)
    operand = pltpu.VMEM((tt, QK_W), BF16)
    return pl.pallas_call(
        body, name="gla_fwd",
        out_shape=(o_shape, o_shape, st_shape, st_shape),
        grid=(nb,),
        in_specs=tok_specs(fw) + tok_specs(bw) + [
            pl.BlockSpec((LR_W, QK_W), const), pl.BlockSpec((LR_W, QK_W), const),
            pl.BlockSpec((1, QK_W), const), pl.BlockSpec((1, QK_W), const)],
        out_specs=(pl.BlockSpec((tt, V_W), fw), pl.BlockSpec((tt, V_W), bw),
                   pl.BlockSpec((nc, HEADS, DV, DK), lambda i: (i, 0, 0, 0)),
                   pl.BlockSpec((nc, HEADS, DV, DK), lambda i: (nb - 1 - i, 0, 0, 0))),
        scratch_shapes=[pltpu.VMEM((2 * HEADS, DV, DK), F32), operand, operand, operand, operand],
        compiler_params=_cparams("arbitrary"),
    )(proj, proj, proj, lr, proj, proj, proj, lr, wgk_f, wgk_b, bgk_f, bgk_b)


def _head_norm(o, gain):
    outs, rinv = [], []
    for h in range(HEADS):
        oh = o[:, h * DV:(h + 1) * DV]
        r = lax.rsqrt(jnp.mean(oh * oh, axis=-1, keepdims=True) + EPS)
        outs.append((oh * r) * gain)
        rinv.append(r)
    return jnp.concatenate(outs, axis=1), rinv


def _shift_rows(u, prev_row, next_row):
    n = u.shape[0]
    row = lax.broadcasted_iota(jnp.int32, (n, 1), 0)
    up = jnp.where(row == 0, prev_row, pltpu.roll(u, 1, 0))
    un = jnp.where(row == n - 1, next_row, pltpu.roll(u, n - 1, 0))
    return up, un


HALO = 16


def _halo_specs(tm, seq, col_block):
    per = tm // HALO
    last = seq // HALO - 1
    return [pl.BlockSpec((HALO, CONV_W), lambda i: (jnp.maximum(i * per - 1, 0), col_block)),
            pl.BlockSpec((HALO, CONV_W), lambda i: (jnp.minimum((i + 1) * per, last), col_block))]


def _f32(ref):
    return ref[...].astype(F32)


def _last_row(ref):
    return ref[HALO - 1:HALO, :].astype(F32)


def _first_row(ref):
    return ref[0:1, :].astype(F32)


def _mix_out_loss(o_f, o_b, proj, x2d, tgt, gla_g, conv_w, conv_b, w_out, final_g, tm):
    seq = x2d.shape[0]
    nt = seq // tm

    def body(of, ob, za, bg, cg, hc, zc, cprev, cnext, hprev, hnext, x_ref, t_ref, gg, cw, cb, wo, fg,
             yt_ref, conv_ref, dx2_ref, dx2b_ref, loss_ref, dfg_ref):
        i = pl.program_id(0)

        @pl.when(i == 0)
        def _():
            loss_ref[...] = jnp.zeros(loss_ref.shape, F32)
            dfg_ref[...] = jnp.zeros(dfg_ref.shape, F32)

        on, _ = _head_norm(_f32(of) + _f32(ob), gg[...])
        zav = _f32(za)
        y_a = on * (zav * _sigmoid(zav))
        u = _f32(cg) * _f32(hc)
        prev_row = jnp.where(i > 0, _last_row(cprev) * _last_row(hprev), 0.0)
        next_row = jnp.where(i < nt - 1, _first_row(cnext) * _first_row(hnext), 0.0)
        up, un = _shift_rows(u, prev_row, next_row)
        conv = (cw[0:1, :] * up + cw[1:2, :] * u + cw[2:3, :] * un) + cb[...]
        conv_ref[...] = conv.astype(BF16)
        zcv = _f32(zc)
        y_c = _f32(bg) * conv * (zcv * _sigmoid(zcv))
        y = jnp.concatenate([y_a, y_c], axis=1)
        yt_ref[...] = y.T.astype(BF16)
        x2 = x_ref[...] + _dot(y.astype(BF16), wo[...])
        r = lax.rsqrt(jnp.mean(x2 * x2, axis=-1, keepdims=True) + EPS)
        xn = x2 * r
        err = xn * fg[...] - t_ref[...]
        loss_ref[...] += 0.5 * jnp.sum(jnp.mean(err * err, axis=-1, keepdims=True))
        dyf = err * (1.0 / D_MODEL)
        dfg_ref[...] += jnp.sum(dyf * xn, axis=0, keepdims=True)
        dxn = dyf * fg[...]
        dx2 = r * dxn - xn * (r * jnp.mean(dxn * xn, axis=-1, keepdims=True))
        dx2_ref[...] = dx2
        dx2b_ref[...] = dx2.astype(BF16)

    def col(off):
        return pl.BlockSpec((tm, CONV_W), lambda i: (i, off // CONV_W))

    rowt = pl.BlockSpec((tm, D_MODEL), lambda i: (i, 0))
    const = lambda shape: pl.BlockSpec(shape, lambda i: (0, 0))
    return pl.pallas_call(
        body, name="mix_out_loss",
        out_shape=(jax.ShapeDtypeStruct((MIX_W, seq), BF16), jax.ShapeDtypeStruct((seq, CONV_W), BF16),
                   jax.ShapeDtypeStruct((seq, D_MODEL), F32), jax.ShapeDtypeStruct((seq, D_MODEL), BF16),
                   jax.ShapeDtypeStruct((8, 128), F32), jax.ShapeDtypeStruct((1, D_MODEL), F32)),
        grid=(nt,),
        in_specs=[rowt, rowt, col(OFF_ZA), col(OFF_B), col(OFF_C), col(OFF_H), col(OFF_ZC)]
        + _halo_specs(tm, seq, OFF_C // CONV_W) + _halo_specs(tm, seq, OFF_H // CONV_W)
        + [rowt, rowt, const((1, DV)), const((8, CONV_W)), const((1, CONV_W)), const((MIX_W, D_MODEL)),
           const((1, D_MODEL))],
        out_specs=(pl.BlockSpec((MIX_W, tm), lambda i: (0, i)), rowt, rowt, rowt, const((8, 128)),
                   const((1, D_MODEL))),
        compiler_params=_cparams("arbitrary"),
    )(o_f, o_b, proj, proj, proj, proj, proj, proj, proj, proj, proj, x2d, tgt, gla_g, conv_w, conv_b, w_out, final_g)


def _dsilu(z, s):
    return s * (1.0 + z * (1.0 - s))


def _mix_bwd(dx2b, o_f, o_b, proj, conv, gla_g, w_out, tm):
    seq = dx2b.shape[0]

    def body(dx, of, ob, za, bg, zc, cv, gg, wo, dg_ref, do_ref, dconv_ref, dgg_ref, dcb_ref):
        @pl.when(pl.program_id(0) == 0)
        def _():
            dgg_ref[...] = jnp.zeros(dgg_ref.shape, F32)
            dcb_ref[...] = jnp.zeros(dcb_ref.shape, F32)

        dy = _dot_nt(dx[...], wo[...])
        dy_a, dy_c = dy[:, :V_W], dy[:, V_W:]
        zcv, bgv, convv = _f32(zc), _f32(bg), _f32(cv)
        sc = _sigmoid(zcv)
        szc = zcv * sc
        dg_ref[:, CONV_W:2 * CONV_W] = (dy_c * convv * szc).astype(BF16)
        dconv = dy_c * bgv * szc
        dconv_ref[...] = dconv.astype(BF16)
        dcb_ref[...] += jnp.sum(dconv, axis=0, keepdims=True)
        dg_ref[:, 2 * CONV_W:] = (dy_c * bgv * convv * _dsilu(zcv, sc)).astype(BF16)

        o = _f32(of) + _f32(ob)
        gain = gg[...]
        on, rinv = _head_norm(o, gain)
        zav = _f32(za)
        sa = _sigmoid(zav)
        dg_ref[:, :CONV_W] = (dy_a * on * _dsilu(zav, sa)).astype(BF16)
        don = dy_a * (zav * sa)
        dgg = jnp.zeros((1, DV), F32)
        dos = []
        for h in range(HEADS):
            sl = slice(h * DV, (h + 1) * DV)
            oh, r, dh = o[:, sl], rinv[h], don[:, sl]
            ohn = oh * r
            dgg = dgg + jnp.sum(dh * ohn, axis=0, keepdims=True)
            dn = dh * gain
            dos.append(r * dn - ohn * (r * jnp.mean(dn * ohn, axis=-1, keepdims=True)))
        dgg_ref[...] += dgg
        do_ref[...] = jnp.concatenate(dos, axis=1).astype(BF16)

    def col(off):
        return pl.BlockSpec((tm, CONV_W), lambda i: (i, off // CONV_W))

    rowt = pl.BlockSpec((tm, D_MODEL), lambda i: (i, 0))
    const = lambda shape: pl.BlockSpec(shape, lambda i: (0, 0))
    return pl.pallas_call(
        body, name="mix_bwd",
        out_shape=(jax.ShapeDtypeStruct((seq, GATES_W), BF16), jax.ShapeDtypeStruct((seq, V_W), BF16),
                   jax.ShapeDtypeStruct((seq, CONV_W), BF16),
                   jax.ShapeDtypeStruct((1, DV), F32), jax.ShapeDtypeStruct((1, CONV_W), F32)),
        grid=(seq // tm,),
        in_specs=[rowt, rowt, rowt, col(OFF_ZA), col(OFF_B), col(OFF_ZC), rowt, const((1, DV)),
                  const((MIX_W, D_MODEL))],
        out_specs=(pl.BlockSpec((tm, GATES_W), lambda i: (i, 0)), rowt, rowt, const((1, DV)), const((1, CONV_W))),
        compiler_params=_cparams("arbitrary"),
    )(dx2b, o_f, o_b, proj, proj, proj, conv, gla_g, w_out)


def _conv_bwd(dconv, proj, conv_w, tm):
    seq = dconv.shape[0]
    nt = seq // tm

    def body(dc_in, dprev, dnext, cg, hc, cprev, cnext, hprev, hnext, cw, dch_ref, dcw_ref):
        i = pl.program_id(0)

        @pl.when(i == 0)
        def _():
            dcw_ref[...] = jnp.zeros(dcw_ref.shape, F32)

        first, lastt = i > 0, i < nt - 1
        dcv = _f32(dc_in)
        d_up, d_un = _shift_rows(dcv, jnp.where(first, _last_row(dprev), 0.0), jnp.where(lastt, _first_row(dnext), 0.0))
        cgv, hcv = _f32(cg), _f32(hc)
        u = cgv * hcv
        u_up, u_un = _shift_rows(u, jnp.where(first, _last_row(cprev) * _last_row(hprev), 0.0),
                                 jnp.where(lastt, _first_row(cnext) * _first_row(hnext), 0.0))
        du = cw[0:1, :] * d_un + cw[1:2, :] * dcv + cw[2:3, :] * d_up
        dch_ref[:, :CONV_W] = (du * hcv).astype(BF16)
        dch_ref[:, CONV_W:] = (du * cgv).astype(BF16)
        dcw_ref[0:1, :] += jnp.sum(dcv * u_up, axis=0, keepdims=True)
        dcw_ref[1:2, :] += jnp.sum(dcv * u, axis=0, keepdims=True)
        dcw_ref[2:3, :] += jnp.sum(dcv * u_un, axis=0, keepdims=True)

    def col(off):
        return pl.BlockSpec((tm, CONV_W), lambda i: (i, off // CONV_W))

    rowt = pl.BlockSpec((tm, CONV_W), lambda i: (i, 0))
    const = lambda shape: pl.BlockSpec(shape, lambda i: (0, 0))
    return pl.pallas_call(
        body, name="conv_bwd",
        out_shape=(jax.ShapeDtypeStruct((seq, CH_W), BF16), jax.ShapeDtypeStruct((8, CONV_W), F32)),
        grid=(nt,),
        in_specs=[rowt] + _halo_specs(tm, seq, 0) + [col(OFF_C), col(OFF_H)]
        + _halo_specs(tm, seq, OFF_C // CONV_W) + _halo_specs(tm, seq, OFF_H // CONV_W) + [const((8, CONV_W))],
        out_specs=(pl.BlockSpec((tm, CH_W), lambda i: (i, 0)), const((8, CONV_W))),
        compiler_params=_cparams("arbitrary"),
    )(dconv, dconv, dconv, proj, proj, proj, proj, proj, proj, conv_w)


def _gla_bwd(proj, lr, do, st_f, st_b, wgk_f, wgk_b, bgk_f, bgk_b, tt):
    seq = proj.shape[0]
    nb, nc = seq // tt, tt // CHUNK

    def body(qf, kf, vf, lrf, dof, stf, qb, kb, vb, lrb, dob, stb, wf, wb, bf, bb,
             dqkv_f, dlr_f, dqkv_b, dlr_b, dwf, dwb, dbf, dbb,
             ds_scr, eq_s, ek_s, ein_s, eout_s, qs_s, ks_s, qin_s, kout_s, db_s, lg_s):
        @pl.when(pl.program_id(0) == 0)
        def _():
            ds_scr[...] = jnp.zeros(ds_scr.shape, F32)
            for r in (dwf, dwb, dbf, dbb):
                r[...] = jnp.zeros(r.shape, F32)

        low, upp, sup = _block_masks(tt)
        row = lax.broadcasted_iota(jnp.int32, (CHUNK, 1), 0)
        kmask = _chunk_column_mask(tt)
        dirs = ((qf, kf, vf, lrf, dof, stf, wf, bf, dqkv_f, dlr_f, dwf, dbf,
                 low, upp, low, REF_F, LAST_F, list(reversed(range(nc)))),
                (qb, kb, vb, lrb, dob, stb, wb, bb, dqkv_b, dlr_b, dwb, dbb,
                 upp, low, sup, REF_B, LAST_B, list(range(nc))))
        for d, (q_r, k_r, v_r, lr_r, do_r, st_r, w_r, b_r, dqkv_r, dlr_r, dw_r, db_r,
                cum, cum_t, mask, ref, last, order) in enumerate(dirs):
            lrv = lr_r[...].astype(BF16)
            wv = w_r[...]
            logits = _dot(lrv, wv) + b_r[...]
            lg_s[...] = logits
            b = _dot_split3(cum.astype(BF16), _log_gate(logits))
            decs = []
            for c in range(nc):
                rows = slice(c * CHUNK, (c + 1) * CHUNK)
                bc = b[rows]
                b_ref, b_last = bc[ref:ref + 1], bc[last:last + 1]
                qc = q_r[rows, :].astype(F32) * QSCALE
                kc = k_r[rows, :].astype(F32)
                e_q, e_k, e_in, e_out = jnp.exp(bc - b_ref), jnp.exp(b_ref - bc), jnp.exp(bc), jnp.exp(b_last - bc)
                eq_s[rows, :], ek_s[rows, :], ein_s[rows, :], eout_s[rows, :] = e_q, e_k, e_in, e_out
                qs_s[rows, :] = (qc * e_q).astype(BF16)
                ks_s[rows, :] = (kc * e_k).astype(BF16)
                qin_s[rows, :] = (qc * e_in).astype(BF16)
                kout_s[rows, :] = (kc * e_out).astype(BF16)
                decs.append(jnp.exp(b_last))
            for h in range(HEADS):
                ksl = slice(h * DK, (h + 1) * DK)
                vsl = slice(h * DV, (h + 1) * DV)
                v = v_r[:, vsl].astype(BF16)
                dov = do_r[:, vsl].astype(BF16)
                qsb, ksb = qs_s[:, ksl], ks_s[:, ksl]
                att = jnp.where(mask, _dot_nt(qsb, ksb), 0.0).astype(BF16)
                datt = jnp.where(mask, _dot_nt(dov, v), 0.0).astype(BF16)
                dqs = _dot(datt, ksb)
                dks = _dot_tn(datt, qsb)
                dv_intra = _dot_tn(att, dov)
                g_t = _dot_tn(dov, _chunked(kmask, qin_s[:, ksl], nc))
                ds = ds_scr[d * HEADS + h]
                for c in order:
                    rows = slice(c * CHUNK, (c + 1) * CHUNK)
                    dsb = ds.astype(BF16)
                    s_prev = st_r[c, h]
                    dk_out = _dot(v[rows], dsb)
                    dq_in = _dot(dov[rows], s_prev)
                    dv = dv_intra[rows] + _dot_nt(kout_s[rows, ksl], dsb)
                    dqkv_r[rows, OFF_V + h * DV:OFF_V + (h + 1) * DV] = dv.astype(BF16)
                    dec = decs[c][:, ksl]
                    ddec = jnp.sum(ds * s_prev.astype(F32), axis=0, keepdims=True)
                    e_out = eout_s[rows, ksl]
                    qc = q_r[rows, ksl].astype(F32) * QSCALE
                    kc = k_r[rows, ksl].astype(F32)
                    dq = dqs[rows] * eq_s[rows, ksl] + dq_in * ein_s[rows, ksl]
                    dk = dks[rows] * ek_s[rows, ksl] + dk_out * e_out
                    dqkv_r[rows, OFF_Q + h * DK:OFF_Q + (h + 1) * DK] = (dq * QSCALE).astype(BF16)
                    dqkv_r[rows, OFF_K + h * DK:OFF_K + (h + 1) * DK] = dk.astype(BF16)
                    tail = jnp.sum(dk_out * (kc * e_out), axis=0, keepdims=True) + ddec * dec
                    db_s[rows, ksl] = (qc * dq - kc * dk) + jnp.where(row == last, tail, 0.0)
                    ds = ds * dec + g_t[:, c * DK:(c + 1) * DK]
                ds_scr[d * HEADS + h] = ds
            dg = _dot_split3(cum_t.astype(BF16), db_s[...])
            dlogit = (dg * GATE_SCALE) * _sigmoid(-lg_s[...])
            dlb = dlogit.astype(BF16)
            dlr_r[...] = _dot_nt(dlb, wv)
            dw_r[...] += _dot_tn(lrv, dlb)
            db_r[...] += jnp.sum(dlogit, axis=0, keepdims=True)

    fw = lambda i: (nb - 1 - i, 0)
    bw = lambda i: (i, 0)
    const = lambda i: (0, 0)

    def tok_specs(m):
        return [pl.BlockSpec((tt, QK_W), lambda i: (m(i)[0], OFF_Q // QK_W)),
                pl.BlockSpec((tt, QK_W), lambda i: (m(i)[0], OFF_K // QK_W)),
                pl.BlockSpec((tt, V_W), lambda i: (m(i)[0], OFF_V // V_W)),
                pl.BlockSpec((tt, LR_W), m),
                pl.BlockSpec((tt, V_W), m),
                pl.BlockSpec((nc, HEADS, DV, DK), lambda i: (m(i)[0], 0, 0, 0))]

    dqkv = jax.ShapeDtypeStruct((seq, QK_W + QK_W + V_W), BF16)
    dlr = jax.ShapeDtypeStruct((seq, LR_W), F32)
    dw = jax.ShapeDtypeStruct((LR_W, QK_W), F32)
    dbias = jax.ShapeDtypeStruct((1, QK_W), F32)
    return pl.pallas_call(
        body, name="gla_bwd",
        out_shape=(dqkv, dlr, dqkv, dlr, dw, dw, dbias, dbias),
        grid=(nb,),
        in_specs=tok_specs(fw) + tok_specs(bw) + [
            pl.BlockSpec((LR_W, QK_W), const), pl.BlockSpec((LR_W, QK_W), const),
            pl.BlockSpec((1, QK_W), const), pl.BlockSpec((1, QK_W), const)],
        out_specs=(pl.BlockSpec((tt, QK_W + QK_W + V_W), fw), pl.BlockSpec((tt, LR_W), fw),
                   pl.BlockSpec((tt, QK_W + QK_W + V_W), bw), pl.BlockSpec((tt, LR_W), bw),
                   pl.BlockSpec((LR_W, QK_W), const), pl.BlockSpec((LR_W, QK_W), const),
                   pl.BlockSpec((1, QK_W), const), pl.BlockSpec((1, QK_W), const)),
        scratch_shapes=[pltpu.VMEM((2 * HEADS, DV, DK), F32)] + [pltpu.VMEM((tt, QK_W), F32)] * 4
        + [pltpu.VMEM((tt, QK_W), BF16)] * 4 + [pltpu.VMEM((tt, QK_W), F32)] * 2,
        compiler_params=_cparams("arbitrary"),
    )(proj, proj, proj, lr, do, st_f, proj, proj, proj, lr, do, st_b, wgk_f, wgk_b, bgk_f, bgk_b)


def _both_directions(f_ref, b_ref):
    return (_f32(f_ref) + _f32(b_ref)).astype(BF16)


def _input_grad(dqkv_f, dqkv_b, dp_gates, dp_ch, dlr_f, dlr_b, w_nat, x2d, norm_g, dx2, sums, tm):
    seq = x2d.shape[0]
    nt, n = seq // tm, len(sums)
    relay_step = (3 * nt) // 8

    def body(dqf, dqb, dg, dc, dlf, dlb, w, x_ref, g_ref, dx2_ref, *rest):
        ins, (gx_ref, dng_ref), outs = rest[:n], rest[n:n + 2], rest[n + 2:2 * n + 2]
        passing, joined = rest[2 * n + 2:3 * n + 2], rest[3 * n + 2:4 * n + 2]
        send_sems, recv_sems, local_sems = rest[4 * n + 2:]
        i = pl.program_id(0)
        c = lax.axis_index("c")
        first, second, diagonal = _route_chips()
        slot = lambda chip: 2 * chip[0] + chip[1]

        def remote(a, k, src, dst, to):
            return pltpu.make_async_remote_copy(src_ref=src, dst_ref=dst, send_sem=send_sems.at[3 * a + k],
                                                recv_sem=recv_sems.at[3 * a + k], device_id=(*to, c),
                                                device_id_type=MESH)

        direct = lambda a: remote(a, 0, ins[a].at[slot(first)], outs[a].at[0], first)
        for_second = lambda a: remote(a, 1, ins[a].at[slot(diagonal)], passing[a], first)
        joint = lambda a: remote(a, 2, joined[a], outs[a].at[1], second)
        own = lambda a: pltpu.make_async_copy(ins[a].at[slot(second)], joined[a], local_sems.at[a])

        @pl.when(i == 0)
        def _():
            _start_all([for_second(a) for a in range(n)] + [own(a) for a in range(n)] + [direct(a) for a in range(n)])
            dng_ref[...] = jnp.zeros(dng_ref.shape, F32)

        @pl.when(i == relay_step)
        def _():
            for a in range(n):
                for_second(a).wait_recv()
                own(a).wait()
                joined[a][...] = (joined[a][...].astype(F32) + passing[a][...].astype(F32)).astype(BF16)
                joint(a).start()

        dh = (_dot((dlf[...] + dlb[...]).astype(BF16), w[NAT_LR:NAT_LR + LR_W, :])
              + _dot(_both_directions(dqf, dqb), w[0:NAT_ZA, :])
              + _dot(dg[:, 0:CONV_W], w[NAT_ZA:NAT_LR, :]) + _dot(dg[:, CONV_W:2 * CONV_W], w[NAT_B:NAT_C, :])
              + _dot(dg[:, 2 * CONV_W:], w[NAT_ZC:IN_W, :]) + _dot(dc[...], w[NAT_C:NAT_ZC, :]))
        xv = x_ref[...]
        r = lax.rsqrt(jnp.mean(xv * xv, axis=-1, keepdims=True) + EPS)
        xn = xv * r
        dng_ref[...] += jnp.sum(dh * xn, axis=0, keepdims=True)
        dn = dh * g_ref[...]
        gx_ref[...] = (r * dn - xn * (r * jnp.mean(dn * xn, axis=-1, keepdims=True))) + dx2_ref[...]

        @pl.when(i == nt - 1)
        def _():
            for a in range(n):
                direct(a).wait_recv()
                joint(a).wait_recv()
            for a in range(n):
                for cp in (direct(a), for_second(a), joint(a)):
                    cp.wait_send()

    rowt = pl.BlockSpec((tm, D_MODEL), lambda i: (i, 0))
    seg = lambda width: pl.BlockSpec((tm, width), lambda i: (i, 0))
    resident = lambda rows: pl.BlockSpec((rows, D_MODEL), lambda i: (0, 0), pipeline_mode=pl.Buffered(1))
    hbm = pl.BlockSpec(memory_space=pl.ANY)
    blocks = [pltpu.VMEM(s.shape[1:], s.dtype) for s in sums]
    return pl.pallas_call(
        body, name="input_grad",
        out_shape=(jax.ShapeDtypeStruct((seq, D_MODEL), F32), jax.ShapeDtypeStruct((1, D_MODEL), F32))
        + tuple(jax.ShapeDtypeStruct((2,) + s.shape[1:], s.dtype) for s in sums),
        grid=(nt,),
        in_specs=[seg(QKV_W), seg(QKV_W), seg(GATES_W), seg(CH_W), seg(LR_W), seg(LR_W), resident(IN_W),
                  rowt, pl.BlockSpec((1, D_MODEL), lambda i: (0, 0)), rowt] + [hbm] * n,
        out_specs=(rowt, pl.BlockSpec((1, D_MODEL), lambda i: (0, 0))) + (hbm,) * n,
        scratch_shapes=blocks + blocks + [pltpu.SemaphoreType.DMA((3 * n,)), pltpu.SemaphoreType.DMA((3 * n,)),
                                          pltpu.SemaphoreType.DMA((n,))],
        compiler_params=_cparams("arbitrary"),
    )(dqkv_f, dqkv_b, dp_gates, dp_ch, dlr_f, dlr_b, w_nat, x2d, norm_g, dx2, *sums)


def _weight_grad_out(y_t, dx2b, tk, riding):
    m, seq = y_t.shape
    n = dx2b.shape[1]
    nk = seq // tk

    def body(a_ref, b_ref, ride_in, o_ref, ride_out, send_sems, recv_sems):
        k = pl.program_id(0)

        @pl.when(k == 0)
        def _():
            _start_all(_sibling_copies(ride_in, ride_out, send_sems, recv_sems))
            o_ref[...] = jnp.zeros(o_ref.shape, F32)

        o_ref[...] += _dot(a_ref[...], b_ref[...])

        @pl.when(k == nk - 1)
        def _():
            _wait_all(_sibling_copies(ride_in, ride_out, send_sems, recv_sems))

    hbm = pl.BlockSpec(memory_space=pl.ANY)
    return pl.pallas_call(
        body, name="wgrad_out",
        out_shape=(jax.ShapeDtypeStruct((m, n), F32), jax.ShapeDtypeStruct((4,) + _block_shape(riding), F32)),
        grid=(nk,),
        in_specs=[pl.BlockSpec((m, tk), lambda k: (0, k)), pl.BlockSpec((tk, n), lambda k: (k, 0)), hbm],
        out_specs=(pl.BlockSpec((m, n), lambda k: (0, 0)), hbm),
        scratch_shapes=[pltpu.SemaphoreType.DMA((4,)), pltpu.SemaphoreType.DMA((4,))],
        compiler_params=_cparams("arbitrary"),
    )(y_t, dx2b, riding)


def _weight_grad_in(h_t, dqkv_f, dqkv_b, dp_gates, dp_ch, dlr_f, dlr_b):
    m, seq = h_t.shape
    tn = 512
    n_qkv, n_gates, n_ch = QKV_W // tn, GATES_W // tn, CH_W // tn
    starts = ([k * tn for k in range(n_qkv)] + [NAT_ZA, NAT_ZA + tn, NAT_B, NAT_B + tn, NAT_ZC, NAT_ZC + tn]
              + [NAT_C + k * tn for k in range(n_ch)])

    def out_row(j):
        row = 0
        for k, start in enumerate(starts):
            row = row + jnp.where(j == k, start // 32, 0)
        return pl.multiple_of(row * 32, 32), 0

    def body(a_ref, bqf, bqb, bg, bc, dlf, dlb, o_ref, lr_ref, acc, bq):
        j = pl.program_id(0)

        @pl.when(j == 0)
        def _():
            acc[:, 0:LR_W] = _dot(a_ref[...], (dlf[...] + dlb[...]).astype(BF16))
            lr_ref[...] = acc[:, 0:LR_W].T[0:2 * RANK, :]

        @pl.when(j < n_qkv)
        def _():
            bq[...] = _both_directions(bqf, bqb)
            acc[...] = _dot(a_ref[...], bq[...])

        @pl.when(jnp.logical_and(j >= n_qkv, j < n_qkv + n_gates))
        def _():
            acc[...] = _dot(a_ref[...], bg[...])

        @pl.when(j >= n_qkv + n_gates)
        def _():
            acc[...] = _dot(a_ref[...], bc[...])

        o_ref[...] = acc[...].T

    resident = lambda shape: pl.BlockSpec(shape, lambda j: (0, 0), pipeline_mode=pl.Buffered(1))
    seg = lambda first, count: pl.BlockSpec((seq, tn), lambda j: (0, jnp.clip(j - first, 0, count - 1)))
    main, lr_rows = pl.pallas_call(
        body, name="wgrad_in",
        out_shape=(jax.ShapeDtypeStruct((IN_W, m), F32), jax.ShapeDtypeStruct((2 * RANK, m), F32)),
        grid=(n_qkv + n_gates + n_ch,),
        in_specs=[resident((m, seq)), seg(0, n_qkv), seg(0, n_qkv), seg(n_qkv, n_gates), seg(n_qkv + n_gates, n_ch),
                  resident((seq, LR_W)), resident((seq, LR_W))],
        out_specs=(pl.BlockSpec((pl.Element(tn), pl.Element(m)), out_row),
                   pl.BlockSpec((2 * RANK, m), lambda j: (0, 0))),
        scratch_shapes=[pltpu.VMEM((m, tn), F32), pltpu.VMEM((seq, tn), BF16)],
        compiler_params=_cparams("arbitrary"),
    )(h_t, dqkv_f, dqkv_b, dp_gates, dp_ch, dlr_f, dlr_b)
    return lax.dynamic_update_slice(main, lr_rows, (NAT_LR, 0))


def _pad_rows(a, rows):
    return jnp.pad(a, ((0, rows - a.shape[0]), (0, 0)))


def _rows128(a):
    a = a.reshape(-1, 128)
    return _pad_rows(a, -(-a.shape[0] // 8) * 8)


def _pack(arrs):
    return jnp.concatenate([_rows128(a) for a in arrs], axis=0)


def _unpack(buf, like):
    out, start = [], 0
    for a in like:
        rows = a.size // 128
        out.append(buf[start:start + rows].reshape(a.shape))
        start += -(-rows // 8) * 8
    return out


def kernel(x, norm_g, w_in, w_gk_f, b_gk_f, w_gk_b, b_gk_b, gla_norm_g, conv_w, conv_b, w_out, final_g, loss_target, m_norm_g, m_w_in, m_w_gk_f, m_b_gk_f, m_w_gk_b, m_b_gk_b, m_gla_norm_g, m_conv_w, m_conv_b, m_w_out, m_final_g, v_norm_g, v_w_in, v_w_gk_f, v_b_gk_f, v_w_gk_b, v_b_gk_b, v_gla_norm_g, v_conv_w, v_conv_b, v_w_out, v_final_g):
    px, py, pc = _position()
    me = _blk(px, py, pc)
    seq = x.shape[1]
    x2d, tgt = x[0], loss_target[0]
    tt = min(256, seq)

    small_s = jnp.concatenate([jnp.concatenate([w_gk_f[0], w_gk_b[0]], axis=1), _pad_rows(conv_w[0], 8)], axis=0)
    order = sum(jnp.where(2 * px + py == k, jnp.asarray(tiles + (0,), jnp.int32), 0) for k, tiles in enumerate(TILE_ORDER))
    proj, lr, h_t, w_nat, wout_all, small_all = _gather_inproj(x2d, norm_g, w_in[0].T, w_out[0], small_s, order,
                                                               min(1024, seq))
    w_out_full = wout_all.reshape(MIX_W, D_MODEL)
    wgk_cols = 512 // N_DEV
    wgk_f_full = small_all[:, 0:RANK, 0:wgk_cols].transpose(1, 0, 2).reshape(RANK, QK_W)
    wgk_b_full = small_all[:, 0:RANK, wgk_cols:2 * wgk_cols].transpose(1, 0, 2).reshape(RANK, QK_W)
    conv_w_full = _pad_rows(small_all[:, RANK:RANK + 3, :].transpose(1, 0, 2).reshape(3, CONV_W), 8)
    zr = lambda n: jnp.zeros((n, QK_W), F32)
    wgk_f_pad = jnp.concatenate([wgk_f_full, zr(LR_W - RANK)], axis=0).astype(BF16)
    wgk_b_pad = jnp.concatenate([zr(RANK), wgk_b_full, zr(LR_W - 2 * RANK)], axis=0).astype(BF16)

    o_f, o_b, st_f, st_b = _gla_fwd(proj, lr, wgk_f_pad, wgk_b_pad, b_gk_f, b_gk_b, tt)
    tmix = min(512, seq)
    y_t, conv, dx2, dx2b, loss_p, dfg_p = _mix_out_loss(o_f, o_b, proj, x2d, tgt, gla_norm_g, conv_w_full, conv_b,
                                                        w_out_full, final_g.reshape(1, D_MODEL), tmix)

    dp_gates, do, dconv, dgg_p, dcb_p = _mix_bwd(dx2b, o_f, o_b, proj, conv, gla_norm_g, w_out_full, tmix)
    dp_ch, dcw_p = _conv_bwd(dconv, proj, conv_w_full, tmix)
    dqkv_f, dlr_f, dqkv_b, dlr_b, dwf_p, dwb_p, dbf_p, dbb_p = _gla_bwd(
        proj, lr, do, st_f, st_b, wgk_f_pad, wgk_b_pad, b_gk_f, b_gk_b, tt)
    dw_nat = _weight_grad_in(h_t, dqkv_f, dqkv_b, dp_gates, dp_ch, dlr_f, dlr_b)

    dw_out, sib_in = _weight_grad_out(y_t, dx2b, min(1024, seq), dw_nat)
    part_out = dw_out.reshape(N_DEV, MIX_W // N_DEV, D_MODEL)
    core = jnp.reshape(pc, (1,)).astype(jnp.int32)
    chip = jnp.reshape(2 * px + py, (1,)).astype(jnp.int32)
    sums_in, sib_out = _chip_sums(dw_nat, sib_in, core, D_MODEL, "chip_sums_in", riding=part_out)
    sums_out = _chip_sums(part_out, sib_out, core, D_MODEL, "chip_sums_out")
    grad_x2d, dng_p, far_in, far_out = _input_grad(dqkv_f, dqkv_b, dp_gates, dp_ch, dlr_f, dlr_b, w_nat, x2d, norm_g, dx2,
                                                   [sums_in, sums_out], min(256, seq))
    pieces = [dng_p, dbf_p, dbb_p, dgg_p, dcb_p, dfg_p[0], dwf_p[0:RANK], dwb_p[RANK:2 * RANK], dcw_p[0:3], loss_p[0]]
    g_window, small_tot = _final_sum(sums_in, far_in, chip, _pack(pieces), 512, "final_sum_in")
    g_in_t = lax.dynamic_slice_in_dim(g_window, 4 * pc, SHARD_W, axis=0)
    g_w_out, d_w_out, nm_w_out, nv_w_out = _final_sum_adamw(sums_out, far_out, chip, w_out[0], m_w_out[0], v_w_out[0],
                                                            256, "adamw_out")
    flat = lambda a: a[0].T.reshape(SHARD_W, D_MODEL // 128, 128)
    unflat = lambda a: a.reshape(SHARD_W, D_MODEL).T
    d_flat, m_flat, v_flat = _adamw_rows(g_in_t.reshape(SHARD_W, D_MODEL // 128, 128), flat(w_in), flat(m_w_in),
                                         flat(v_w_in), 300, "adamw_in")
    g_w_in, d_w_in, nm_w_in, nv_w_in = g_in_t.T, unflat(d_flat), unflat(m_flat), unflat(v_flat)

    tot = _unpack(small_tot, pieces)
    g_norm_g, g_b_gk_f, g_b_gk_b, g_gla, g_conv_b, g_final = tot[:6]
    g_wgk_f = lax.dynamic_slice_in_dim(tot[6], me * wgk_cols, wgk_cols, axis=1)[None]
    g_wgk_b = lax.dynamic_slice_in_dim(tot[7], me * wgk_cols, wgk_cols, axis=1)[None]
    g_conv_w = lax.dynamic_slice_in_dim(tot[8], me * 128, 128, axis=1)[None]
    loss = tot[9][0]

    small_g = [g_norm_g, g_b_gk_f, g_b_gk_b, g_gla, g_conv_b, g_final, g_wgk_f, g_wgk_b, g_conv_w]
    small_w = [norm_g, b_gk_f, b_gk_b, gla_norm_g, conv_b, final_g, w_gk_f, w_gk_b, conv_w]
    small_m = [m_norm_g, m_b_gk_f, m_b_gk_b, m_gla_norm_g, m_conv_b, m_final_g, m_w_gk_f, m_w_gk_b, m_conv_w]
    small_v = [v_norm_g, v_b_gk_f, v_b_gk_b, v_gla_norm_g, v_conv_b, v_final_g, v_w_gk_f, v_w_gk_b, v_conv_w]
    d_s, m_s, v_s = _adamw_small(_pack(small_g), _pack(small_w), _pack(small_m), _pack(small_v))
    d_l, m_l, v_l = _unpack(d_s, small_w), _unpack(m_s, small_w), _unpack(v_s, small_w)

    def ordered(sm, big_in, big_out):
        return [sm[0], big_in[None], sm[6], sm[1], sm[7], sm[2], sm[3], sm[8], sm[4], big_out[None], sm[5]]

    grads = ordered(small_g, g_w_in, g_w_out)
    deltas = ordered(d_l, d_w_in, d_w_out)
    new_m = ordered(m_l, nm_w_in, nm_w_out)
    new_v = ordered(v_l, nv_w_in, nv_w_out)
    return (loss, grad_x2d[None], *grads, *deltas, *new_m, *new_v)
```

```python
import functools

import jax
import jax.numpy as jnp
from jax import lax
from jax.experimental import pallas as pl
from jax.experimental.pallas import tpu as pltpu

F32 = jnp.float32
BF16 = jnp.bfloat16
MESH = pl.DeviceIdType.MESH

N_DEV = 8
D_MODEL = 1024
HEADS = 4
DK = 128
DV = 256
QK_W = HEADS * DK
V_W = HEADS * DV
CONV_W = 1024
MIX_W = V_W + CONV_W
CHUNK = 64
RANK = 16
IN_W = 7200
SHARD_W = IN_W // N_DEV
MAIN_W = 7168
LR_W = 128
OFF_Q, OFF_K, OFF_V, OFF_ZA, OFF_B, OFF_ZC, OFF_C, OFF_H = 0, 512, 1024, 2048, 3072, 4096, 5120, 6144
QKV_W, GATES_W, CH_W = 2048, 3072, 2048
NAT_ZA, NAT_LR, NAT_B, NAT_C, NAT_ZC = 2048, 3072, 3104, 4128, 6176
EPS = 1e-6
GATE_SCALE = 1.0 / 16.0
QSCALE = DK ** -0.5
REF_F, LAST_F = CHUNK // 2, CHUNK - 1
REF_B, LAST_B = CHUNK - 1 - CHUNK // 2, 0

ADAM_LR = 0.001
ADAM_B1 = 0.9
ADAM_B2 = 0.999
ADAM_EPS = 1e-08
ADAM_WD = 0.01
ADAM_STEP = 10

VMEM_LIMIT = 56 * 1024 * 1024


def _cparams(*sem):
    return pltpu.CompilerParams(dimension_semantics=sem, vmem_limit_bytes=VMEM_LIMIT)


def _dot(a, b):
    return jnp.dot(a, b, preferred_element_type=F32)


def _dot_nt(a, b):
    return lax.dot_general(a, b, (((1,), (1,)), ((), ())), preferred_element_type=F32)


def _dot_tn(a, b):
    return lax.dot_general(a, b, (((0,), (0,)), ((), ())), preferred_element_type=F32)


def _sigmoid(z):
    return jax.nn.sigmoid(z)


def _position():
    return lax.axis_index("x"), lax.axis_index("y"), lax.axis_index("c")


def _blk(px, py, pc):
    return 4 * px + 2 * py + pc


EDGE = 16
SHIFTED_ROWS = 912
BODY_ROWS = SHIFTED_ROWS - 2 * EDGE


def _first_tile_row(blk, px):
    return EDGE * (56 * blk + px)


def _edge_tiles():
    tiles = {}
    for blk in range(N_DEV):
        first = _first_tile_row(blk, blk // 4)
        tiles.setdefault(first, []).append((blk, 0))
        tiles.setdefault(first + EDGE + BODY_ROWS, []).append((blk, 1))
    return tiles


def _peer_copies(srcs, outs, send_sems, recv_sems):
    x, y, c = _position()
    me = _blk(x, y, c)
    copies = []
    for a, (src, out) in enumerate(zip(srcs, outs)):
        k = 0
        for dx in (0, 1):
            for dy in (0, 1):
                for dc in (0, 1):
                    if dx + dy + dc == 0:
                        continue
                    peer = (1 - x if dx else x, 1 - y if dy else y, 1 - c if dc else c)
                    copies.append(pltpu.make_async_remote_copy(
                        src_ref=src, dst_ref=out.at[me], send_sem=send_sems.at[a * 7 + k],
                        recv_sem=recv_sems.at[a * 7 + k], device_id=peer, device_id_type=MESH))
                    k += 1
    return copies


def _route_chips():
    x, y, c = _position()
    along_x = c == 0
    return [(jnp.where(along_x, 1 - x, x), jnp.where(along_x, y, 1 - y)),
            (jnp.where(along_x, x, 1 - x), jnp.where(along_x, 1 - y, y)), (1 - x, 1 - y)]


WINDOW_ROWS = SHARD_W + 4


def _window_start(k, parity):
    return 2 * SHARD_W * k + (SHARD_W - 4) * parity


def _owner_block(part, k, parity):
    if part.ndim == 3:
        return part.at[2 * k + parity]
    return part.at[pl.ds(pl.multiple_of(_window_start(k, parity), 8), WINDOW_ROWS)]


def _block_shape(part):
    return part.shape[1:] if part.ndim == 3 else (WINDOW_ROWS, part.shape[1])


def _sibling_copies(part, out, send_sems, recv_sems):
    x, y, c = _position()
    return [pltpu.make_async_remote_copy(src_ref=_owner_block(part, k, 1 - c), dst_ref=out.at[k],
                                         send_sem=send_sems.at[k], recv_sem=recv_sems.at[k],
                                         device_id=(x, y, 1 - c), device_id_type=MESH)
            for k in range(4)]


def _start_all(copies):
    for cp in copies:
        cp.start()


def _wait_all(copies):
    for cp in copies:
        cp.wait_recv()
    for cp in copies:
        cp.wait_send()


def _chip_sums(part, from_sibling, core, tc, name, riding=None):
    rows, cols = _block_shape(part)
    nj = cols // tc

    def body(core_ref, p_ref, s_ref, *rest):
        if riding is None:
            (o_ref,) = rest
        else:
            ride_in, o_ref, ride_out, send_sems, recv_sems = rest
            k, j = pl.program_id(0), pl.program_id(1)

            @pl.when(jnp.logical_and(k == 0, j == 0))
            def _():
                _start_all(_sibling_copies(ride_in, ride_out, send_sems, recv_sems))

        o_ref[0] = (p_ref[...].reshape(rows, tc) + s_ref[0]).astype(BF16)

        if riding is not None:
            @pl.when(jnp.logical_and(k == 3, j == nj - 1))
            def _():
                _wait_all(_sibling_copies(ride_in, ride_out, send_sems, recv_sems))

    hbm = pl.BlockSpec(memory_space=pl.ANY)
    sums = jax.ShapeDtypeStruct((4, rows, cols), BF16)
    tile_out = pl.BlockSpec((1, rows, tc), lambda k, j, core_ref: (k, 0, j))
    if part.ndim == 3:
        mine = pl.BlockSpec((1, rows, tc), lambda k, j, core_ref: (2 * k + core_ref[0], 0, j))
    else:
        mine = pl.BlockSpec((pl.Element(rows), pl.Element(tc)),
                            lambda k, j, core_ref: (pl.multiple_of(_window_start(k, core_ref[0]), 8),
                                                    pl.multiple_of(j * tc, 128)))
    in_specs = [mine, pl.BlockSpec((1, rows, tc), lambda k, j, core_ref: (k, 0, j))]
    if riding is None:
        out_shape, out_specs, scratch, args = sums, tile_out, [], (core, part, from_sibling)
    else:
        out_shape = (sums, jax.ShapeDtypeStruct((4,) + _block_shape(riding), F32))
        out_specs, in_specs = (tile_out, hbm), in_specs + [hbm]
        scratch = [pltpu.SemaphoreType.DMA((4,)), pltpu.SemaphoreType.DMA((4,))]
        args = (core, part, from_sibling, riding)
    return pl.pallas_call(
        body, name=name, out_shape=out_shape,
        grid_spec=pltpu.PrefetchScalarGridSpec(num_scalar_prefetch=1, grid=(4, nj), in_specs=in_specs,
                                               out_specs=out_specs, scratch_shapes=scratch),
        compiler_params=_cparams("arbitrary", "arbitrary"),
    )(*args)


def _sum_chips(s_ref, r_ref):
    f = lambda a: a.astype(F32)
    return (f(s_ref[0]) + f(r_ref[0])) + f(r_ref[1])


def _final_sum(sums, from_chips, chip, small, tc, name):
    _, rows, cols = sums.shape
    nj = cols // tc

    def body(chip_ref, s_ref, r_ref, sm_ref, g_out, tot_ref, all_ref, send_sems, recv_sems):
        j = pl.program_id(0)
        me = _blk(*_position())

        @pl.when(j == 0)
        def _():
            all_ref[me] = sm_ref[...]
            _start_all(_peer_copies((all_ref.at[me],), (all_ref,), send_sems, recv_sems))

        g_out[...] = _sum_chips(s_ref, r_ref)

        @pl.when(j == nj - 1)
        def _():
            _wait_all(_peer_copies((all_ref.at[me],), (all_ref,), send_sems, recv_sems))
            acc = all_ref[0]
            for d in range(1, N_DEV):
                acc = acc + all_ref[d]
            tot_ref[...] = acc

    whole = pl.BlockSpec(small.shape, lambda j, chip_ref: (0, 0))
    return pl.pallas_call(
        body, name=name,
        out_shape=(jax.ShapeDtypeStruct((rows, cols), F32), jax.ShapeDtypeStruct(small.shape, F32)),
        grid_spec=pltpu.PrefetchScalarGridSpec(
            num_scalar_prefetch=1, grid=(nj,),
            in_specs=[pl.BlockSpec((1, rows, tc), lambda j, chip_ref: (chip_ref[0], 0, j)),
                      pl.BlockSpec((2, rows, tc), lambda j, chip_ref: (0, 0, j)), whole],
            out_specs=(pl.BlockSpec((rows, tc), lambda j, chip_ref: (0, j)), whole),
            scratch_shapes=[pltpu.VMEM((N_DEV,) + small.shape, F32), pltpu.SemaphoreType.DMA((7,)),
                            pltpu.SemaphoreType.DMA((7,))]),
        compiler_params=_cparams("arbitrary"),
    )(chip, sums, from_chips, small)


def _adamw_rows(g, w, m, v, tr, name):
    rows = g.shape[0]

    def body(g_ref, w_ref, m_ref, v_ref, d_out, m_out, v_out):
        delta, m_new, v_new = _adamw(w_ref[...], g_ref[...], m_ref[...], v_ref[...])
        d_out[...] = delta
        m_out[...] = m_new
        v_out[...] = v_new

    tile = pl.BlockSpec((tr,) + g.shape[1:], lambda r: (r, 0, 0))
    shp = jax.ShapeDtypeStruct(g.shape, F32)
    return pl.pallas_call(
        body, name=name, out_shape=(shp, shp, shp), grid=(rows // tr,),
        in_specs=[tile] * 4, out_specs=(tile, tile, tile),
        compiler_params=_cparams("arbitrary"),
    )(g, w, m, v)


def _adamw(w, g, m, v):
    m = ADAM_B1 * m + (1.0 - ADAM_B1) * g
    v = ADAM_B2 * v + (1.0 - ADAM_B2) * (g * g)
    m_hat = m / (1.0 - ADAM_B1 ** ADAM_STEP)
    v_hat = v / (1.0 - ADAM_B2 ** ADAM_STEP)
    delta = -ADAM_LR * (m_hat / (jnp.sqrt(v_hat) + ADAM_EPS) + ADAM_WD * w)
    return delta, m, v


def _final_sum_adamw(sums, from_chips, chip, w, m, v, tr, name):
    rows, cols = w.shape

    def body(chip_ref, s_ref, r_ref, w_ref, m_ref, v_ref, g_out, d_out, m_out, v_out):
        g = _sum_chips(s_ref, r_ref)
        delta, m_new, v_new = _adamw(w_ref[...], g, m_ref[...], v_ref[...])
        g_out[...] = g
        d_out[...] = delta
        m_out[...] = m_new
        v_out[...] = v_new

    tile = pl.BlockSpec((tr, cols), lambda r, chip_ref: (r, 0))
    shp = jax.ShapeDtypeStruct((rows, cols), F32)
    return pl.pallas_call(
        body, name=name,
        out_shape=(shp, shp, shp, shp),
        grid_spec=pltpu.PrefetchScalarGridSpec(
            num_scalar_prefetch=1, grid=(rows // tr,),
            in_specs=[pl.BlockSpec((1, tr, cols), lambda r, chip_ref: (chip_ref[0], r, 0)),
                      pl.BlockSpec((2, tr, cols), lambda r, chip_ref: (0, r, 0)),
                      tile, tile, tile],
            out_specs=(tile, tile, tile, tile)),
        compiler_params=_cparams("arbitrary"),
    )(chip, sums, from_chips, w, m, v)


def _adamw_small(g, w, m, v):
    def body(g_ref, w_ref, m_ref, v_ref, d_out, m_out, v_out):
        delta, m_new, v_new = _adamw(w_ref[...], g_ref[...], m_ref[...], v_ref[...])
        d_out[...] = delta
        m_out[...] = m_new
        v_out[...] = v_new

    vmem = pl.BlockSpec(memory_space=pltpu.VMEM)
    shp = jax.ShapeDtypeStruct(g.shape, F32)
    return pl.pallas_call(body, name="adamw_small", out_shape=(shp, shp, shp),
                          in_specs=[vmem] * 4, out_specs=(vmem, vmem, vmem))(g, w, m, v)


TILE_ROWS = (0, 1024, NAT_ZA, NAT_B, NAT_ZC, NAT_C, NAT_C + CONV_W)


TILE_ORDER = ((0, 1, 2, 3, 5, 6, 4), (2, 1, 0, 4, 3, 5, 6), (5, 6, 0, 4, 1, 2, 3), (4, 6, 2, 3, 5, 0, 1))
EARLY_SWEEP, NEIGHBOUR_SWEEP, DIAGONAL_SWEEP = 1, 2, 4
PIECES, W_IN_PIECES, OTHER_PIECES = 4, (0, 1), (2, 3)


def _gather_inproj(x2d, norm_g, shard_t, w_out_s, small_s, order, tm):
    seq = x2d.shape[0]
    tn = CONV_W
    ni, nj = seq // tm, MAIN_W // tn
    first_sweep = lambda j, i, order_ref: jnp.where(j == 0, i, ni - 1)
    last_sweep = lambda j, i, order_ref: jnp.where(j == nj - 1, i, 0)
    edge_tiles = _edge_tiles()

    def body(order_ref, x_ref, g_ref, shard_ref, wout_ref, sm_ref, proj_ref, lr_ref, ht_ref, w_nat, wout_all, sm_all,
             w_all, h_all, edges, stage, wout_b, sm_b, send_sems, recv_sems, local_sems):
        j, i = pl.program_id(0), pl.program_id(1)
        rows = pl.ds(pl.multiple_of(i * tm, tm), tm)
        x, y, c = _position()
        me, here, sibling = _blk(x, y, c), (x, y, c), (x, y, 1 - c)
        chips = _route_chips()
        sibling_chips = [chips[1], chips[0], chips[2]]

        def pieces(px, py, pc):
            blk = _blk(px, py, pc)
            body_rows = pl.ds(pl.multiple_of(_first_tile_row(blk, px) + EDGE, EDGE), BODY_ROWS)
            return [w_all.at[body_rows], edges.at[blk], wout_all.at[blk], sm_all.at[blk]]

        def copy(a, k, block, to, staged=None):
            ref = pieces(*block)[a]
            return pltpu.make_async_remote_copy(src_ref=ref if staged is None else staged, dst_ref=ref,
                                                send_sem=send_sems.at[a * 7 + k], recv_sem=recv_sems.at[a * 7 + k],
                                                device_id=to, device_id_type=MESH)

        def own_copies(group, slots=(0, 1, 2)):
            targets = [sibling] + [(*chips[n], c) for n in range(2)]
            staged = [None, None, wout_b, sm_b]
            return [copy(a, k, here, targets[k], staged[a]) for k in slots for a in group]

        def relays(group):
            return [copy(a, 3, (*chips[0], c), (*chips[1], c)) for a in group]

        def forwards(n, group):
            return [copy(a, 4 + n, (*chips[n], c), sibling) for a in group]

        def keep_own():
            return [pltpu.make_async_copy(wout_b, wout_all.at[me], local_sems.at[0]),
                    pltpu.make_async_copy(sm_b, sm_all.at[me], local_sems.at[1])]

        def keep_weight():
            return pltpu.make_async_copy(w_all, w_nat, local_sems.at[2])

        def take(ns, group, relay=True):
            for n in ns:
                for a in group:
                    copy(a, 1 + n, (*chips[n], c), here).wait_recv()
                _start_all((relays(group) if n == 0 and relay else []) + forwards(n, group))

        def take_passed_on(ns, group):
            for n in ns:
                for a in group:
                    copy(a, 4 + n, (*sibling_chips[n], 1 - c), here).wait_recv()

        def arrive(ns, group):
            take(ns, group)
            take_passed_on(ns, group)

        def per_core_and_row(step):
            for core in range(2):
                for row in range(2):
                    pl.when(jnp.logical_and(c == core, y == row))(functools.partial(step, core, row))

        def start_own(core, row):
            now = (0, 1 + core) if core == row else (0, 2 - core, 1 + core)
            _start_all(own_copies(W_IN_PIECES, now))
            wout_b[...] = wout_ref[...].astype(BF16)
            sm_b[...] = sm_ref[...]
            _start_all(own_copies(OTHER_PIECES, now) + keep_own())

        def take_early(core, row):
            if core == row:
                _start_all(own_copies(W_IN_PIECES, (2 - core,)) + own_copies(OTHER_PIECES, (2 - core,)))
                take((1 - core,), W_IN_PIECES, relay=False)
            else:
                take_passed_on((core,), W_IN_PIECES)

        def take_neighbours(core, row):
            if core == row:
                _start_all(relays(W_IN_PIECES) if core == 1 else [])
                take((core,), W_IN_PIECES)
                take_passed_on((0, 1), W_IN_PIECES)
            else:
                take((0, 1), W_IN_PIECES)
                take_passed_on((1 - core,), W_IN_PIECES)

        early_blk = _blk(x, 1 - y, y)

        def add_edge_tiles(stage):
            for row, parts in edge_tiles.items():
                ready = 0
                for blk, _ in parts:
                    away = (x != blk // 4).astype(jnp.int32) + (y != (blk // 2) % 2).astype(jnp.int32)
                    late = jnp.where(away == 1, jnp.where(early_blk == blk, 1, 2), jnp.where(away == 2, 3 + blk % 2, 0))
                    ready = jnp.maximum(ready, late)

                @pl.when(ready == stage)
                def _(row=row, parts=parts):
                    tile = edges[parts[0][0], parts[0][1]].astype(F32)
                    for blk, side in parts[1:]:
                        tile = tile + edges[blk, side].astype(F32)
                    w_all[row:row + EDGE, :] = tile.astype(BF16)

        @pl.when(jnp.logical_and(j == 0, i == 0))
        def _():
            last = SHARD_W // 8 * 8
            for col in range(0, D_MODEL, 128):
                cols = slice(col, col + 128)
                stage[0:last, :] = shard_ref[0:last, cols]
                stage[last:, :] = jnp.zeros((SHIFTED_ROWS - last, 128), F32)
                stage[last:SHARD_W, :] = shard_ref[last:SHARD_W, cols]
                for k in range(EDGE // 4):
                    @pl.when(me % 4 == k)
                    def _(k=k, cols=cols):
                        moved = pltpu.roll(stage[...], 4 * k, 0) if k else stage[...]
                        pieces(*here)[0][:, cols] = moved[EDGE:EDGE + BODY_ROWS].astype(BF16)
                        edges[me, 0, :, cols] = moved[0:EDGE].astype(BF16)
                        edges[me, 1, :, cols] = moved[EDGE + BODY_ROWS:].astype(BF16)
            per_core_and_row(start_own)
            for a in W_IN_PIECES:
                copy(a, 0, sibling, here).wait_recv()
            add_edge_tiles(0)

        @pl.when(jnp.logical_and(j == EARLY_SWEEP, i == 0))
        def _():
            per_core_and_row(take_early)
            add_edge_tiles(1)

        @pl.when(jnp.logical_and(j == NEIGHBOUR_SWEEP, i == 0))
        def _():
            per_core_and_row(take_neighbours)
            add_edge_tiles(2)

        for core in range(2):
            @pl.when(jnp.logical_and(j == DIAGONAL_SWEEP + core, i == 0))
            def _(core=core):
                pl.when(c == core)(lambda: take((2,), W_IN_PIECES))
                pl.when(c != core)(lambda: take_passed_on((2,), W_IN_PIECES))
                add_edge_tiles(3 + core)
                if core == 0:
                    arrive((0, 1), OTHER_PIECES)
                else:
                    keep_weight().start()

        @pl.when(jnp.logical_and(j == nj - 1, i == 0))
        def _():
            arrive((2,), OTHER_PIECES)

        @pl.when(j == 0)
        def _():
            xv = x_ref[...]
            r = lax.rsqrt(jnp.mean(xv * xv, axis=-1, keepdims=True) + EPS)
            h = (xv * r) * g_ref[...]
            h_all[rows, :] = h.astype(BF16)
            ht_ref[...] = h.T.astype(BF16)

        tile = order_ref[j]
        row = 0
        for k, start in enumerate(TILE_ROWS):
            row = row + jnp.where(tile == k, start // 32, 0)
        w_tile = w_all[pl.ds(pl.multiple_of(row * 32, 32), tn), :]
        proj_ref[...] = _dot_nt(h_all[rows, :], w_tile).astype(BF16)

        @pl.when(j == nj - 1)
        def _():
            lr_ref[...] = _dot_nt(h_all[rows, :], w_all[NAT_LR:NAT_LR + LR_W, :])

        @pl.when(jnp.logical_and(j == nj - 1, i == ni - 1))
        def _():
            everything = range(PIECES)
            passed_on = [cp for n in range(3) for cp in forwards(n, everything)]
            for cp in own_copies(everything) + relays(everything) + passed_on:
                cp.wait_send()
            for a in OTHER_PIECES:
                copy(a, 0, sibling, here).wait_recv()
            for cp in keep_own() + [keep_weight()]:
                cp.wait()

    const = lambda shape: pl.BlockSpec(shape, lambda j, i, order_ref: (0,) * len(shape))
    hbm = pl.BlockSpec(memory_space=pl.ANY)
    vmem = pl.BlockSpec(memory_space=pltpu.VMEM)
    return pl.pallas_call(
        body, name="gather_inproj",
        out_shape=(jax.ShapeDtypeStruct((seq, MAIN_W), BF16), jax.ShapeDtypeStruct((seq, LR_W), F32),
                   jax.ShapeDtypeStruct((D_MODEL, seq), BF16), jax.ShapeDtypeStruct((IN_W, D_MODEL), BF16),
                   jax.ShapeDtypeStruct((N_DEV,) + w_out_s.shape, BF16),
                   jax.ShapeDtypeStruct((N_DEV,) + small_s.shape, F32)),
        grid_spec=pltpu.PrefetchScalarGridSpec(
            num_scalar_prefetch=1, grid=(nj, ni),
            in_specs=[pl.BlockSpec((tm, D_MODEL), lambda j, i, order_ref: (first_sweep(j, i, order_ref), 0)),
                      const((1, D_MODEL)), vmem, vmem, const(small_s.shape)],
            out_specs=(pl.BlockSpec((tm, tn), lambda j, i, order_ref: (i, order_ref[j])),
                       pl.BlockSpec((tm, LR_W), lambda j, i, order_ref: (last_sweep(j, i, order_ref), 0)),
                       pl.BlockSpec((D_MODEL, tm), lambda j, i, order_ref: (0, first_sweep(j, i, order_ref))),
                       hbm, hbm, hbm),
            scratch_shapes=[pltpu.VMEM((IN_W, D_MODEL), BF16), pltpu.VMEM((seq, D_MODEL), BF16),
                            pltpu.VMEM((N_DEV, 2, EDGE, D_MODEL), BF16), pltpu.VMEM((SHIFTED_ROWS, 128), F32),
                            pltpu.VMEM(w_out_s.shape, BF16), pltpu.VMEM(small_s.shape, F32),
                            pltpu.SemaphoreType.DMA((7 * PIECES,)), pltpu.SemaphoreType.DMA((7 * PIECES,)),
                            pltpu.SemaphoreType.DMA((3,))]),
        compiler_params=_cparams("arbitrary", "arbitrary"),
    )(order, x2d, norm_g, shard_t, w_out_s, small_s)


def _block_masks(tt):
    row = lax.broadcasted_iota(jnp.int32, (tt, tt), 0)
    col = lax.broadcasted_iota(jnp.int32, (tt, tt), 1)
    same = jnp.right_shift(row, 6) == jnp.right_shift(col, 6)
    return (jnp.logical_and(same, col <= row), jnp.logical_and(same, col >= row), jnp.logical_and(same, col > row))


def _dot_split3(ones_mat, x):
    x1 = x.astype(BF16)
    r1 = x - x1.astype(F32)
    x2 = r1.astype(BF16)
    x3 = (r1 - x2.astype(F32)).astype(BF16)
    return (_dot(ones_mat, x3) + _dot(ones_mat, x2)) + _dot(ones_mat, x1)


def _log_gate(logits):
    return (jnp.minimum(logits, 0.0) - jnp.log(1.0 + jnp.exp(-jnp.abs(logits)))) * GATE_SCALE


def _chunk_column_mask(tt):
    nc = tt // CHUNK
    row = lax.broadcasted_iota(jnp.int32, (tt, nc * DK), 0)
    col = lax.broadcasted_iota(jnp.int32, (tt, nc * DK), 1)
    return jnp.right_shift(row, 6) == jnp.right_shift(col, 7)


def _chunked(mask, x, nc):
    wide = jnp.concatenate([x] * nc, axis=1)
    return jnp.where(mask, wide, jnp.zeros_like(wide))


def _gla_fwd(proj, lr, wgk_f, wgk_b, bgk_f, bgk_b, tt):
    seq = proj.shape[0]
    nb, nc, nch = seq // tt, tt // CHUNK, seq // CHUNK

    def body(qf, kf, vf, lrf, qb, kb, vb, lrb, wf, wb, bf, bb, of, ob, stf, stb, s_scr, qs_s, ks_s, qin_s, kout_s):
        @pl.when(pl.program_id(0) == 0)
        def _():
            s_scr[...] = jnp.zeros(s_scr.shape, F32)

        low, upp, sup = _block_masks(tt)
        dirs = ((qf, kf, vf, lrf, wf, bf, of, stf, low, low, REF_F, LAST_F, list(range(nc))),
                (qb, kb, vb, lrb, wb, bb, ob, stb, upp, sup, REF_B, LAST_B, list(reversed(range(nc)))))
        for d, (q_r, k_r, v_r, lr_r, w_r, b_r, o_r, st_r, cum, mask, ref, last, order) in enumerate(dirs):
            logits = _dot(lr_r[...].astype(BF16), w_r[...]) + b_r[...]
            b = _dot_split3(cum.astype(BF16), _log_gate(logits))
            decs = []
            for c in range(nc):
                rows = slice(c * CHUNK, (c + 1) * CHUNK)
                bc = b[rows]
                b_ref, b_last = bc[ref:ref + 1], bc[last:last + 1]
                qc = q_r[rows, :].astype(F32) * QSCALE
                kc = k_r[rows, :].astype(F32)
                qs_s[rows, :] = (qc * jnp.exp(bc - b_ref)).astype(BF16)
                ks_s[rows, :] = (kc * jnp.exp(b_ref - bc)).astype(BF16)
                qin_s[rows, :] = (qc * jnp.exp(bc)).astype(BF16)
                kout_s[rows, :] = (kc * jnp.exp(b_last - bc)).astype(BF16)
                decs.append(jnp.exp(b_last))
            for h in range(HEADS):
                ksl = slice(h * DK, (h + 1) * DK)
                vsl = slice(h * DV, (h + 1) * DV)
                v = v_r[:, vsl].astype(BF16)
                att = jnp.where(mask, _dot_nt(qs_s[:, ksl], ks_s[:, ksl]), 0.0).astype(BF16)
                o_intra = _dot(att, v)
                st = s_scr[d * HEADS + h]
                for c in order:
                    rows = slice(c * CHUNK, (c + 1) * CHUNK)
                    stb = st.astype(BF16)
                    st_r[c, h] = stb
                    o_r[rows, vsl] = (o_intra[rows] + _dot_nt(qin_s[rows, ksl], stb)).astype(BF16)
                    st = st * decs[c][:, ksl] + _dot_tn(v[rows], kout_s[rows, ksl])
                s_scr[d * HEADS + h] = st

    fw = lambda i: (i, 0)
    bw = lambda i: (nb - 1 - i, 0)
    const = lambda i: (0, 0)

    def tok_specs(m):
        return [pl.BlockSpec((tt, QK_W), lambda i: (m(i)[0], OFF_Q // QK_W)),
                pl.BlockSpec((tt, QK_W), lambda i: (m(i)[0], OFF_K // QK_W)),
                pl.BlockSpec((tt, V_W), lambda i: (m(i)[0], OFF_V // V_W)),
                pl.BlockSpec((tt, LR_W), m)]

    st_shape = jax.ShapeDtypeStruct((nch, HEADS, DV, DK), BF16)
    o_shape = jax.ShapeDtypeStruct((seq, V_W), BF16)
    operand = pltpu.VMEM((tt, QK_W), BF16)
    return pl.pallas_call(
        body, name="gla_fwd",
        out_shape=(o_shape, o_shape, st_shape, st_shape),
        grid=(nb,),
        in_specs=tok_specs(fw) + tok_specs(bw) + [
            pl.BlockSpec((LR_W, QK_W), const), pl.BlockSpec((LR_W, QK_W), const),
            pl.BlockSpec((1, QK_W), const), pl.BlockSpec((1, QK_W), const)],
        out_specs=(pl.BlockSpec((tt, V_W), fw), pl.BlockSpec((tt, V_W), bw),
                   pl.BlockSpec((nc, HEADS, DV, DK), lambda i: (i, 0, 0, 0)),
                   pl.BlockSpec((nc, HEADS, DV, DK), lambda i: (nb - 1 - i, 0, 0, 0))),
        scratch_shapes=[pltpu.VMEM((2 * HEADS, DV, DK), F32), operand, operand, operand, operand],
        compiler_params=_cparams("arbitrary"),
    )(proj, proj, proj, lr, proj, proj, proj, lr, wgk_f, wgk_b, bgk_f, bgk_b)


def _head_norm(o, gain):
    outs, rinv = [], []
    for h in range(HEADS):
        oh = o[:, h * DV:(h + 1) * DV]
        r = lax.rsqrt(jnp.mean(oh * oh, axis=-1, keepdims=True) + EPS)
        outs.append((oh * r) * gain)
        rinv.append(r)
    return jnp.concatenate(outs, axis=1), rinv


def _shift_rows(u, prev_row, next_row):
    n = u.shape[0]
    row = lax.broadcasted_iota(jnp.int32, (n, 1), 0)
    up = jnp.where(row == 0, prev_row, pltpu.roll(u, 1, 0))
    un = jnp.where(row == n - 1, next_row, pltpu.roll(u, n - 1, 0))
    return up, un


HALO = 16


def _halo_specs(tm, seq, col_block):
    per = tm // HALO
    last = seq // HALO - 1
    return [pl.BlockSpec((HALO, CONV_W), lambda i: (jnp.maximum(i * per - 1, 0), col_block)),
            pl.BlockSpec((HALO, CONV_W), lambda i: (jnp.minimum((i + 1) * per, last), col_block))]


def _f32(ref):
    return ref[...].astype(F32)


def _last_row(ref):
    return ref[HALO - 1:HALO, :].astype(F32)


def _first_row(ref):
    return ref[0:1, :].astype(F32)


def _mix_out_loss(o_f, o_b, proj, x2d, tgt, gla_g, conv_w, conv_b, w_out, final_g, tm):
    seq = x2d.shape[0]
    nt = seq // tm

    def body(of, ob, za, bg, cg, hc, zc, cprev, cnext, hprev, hnext, x_ref, t_ref, gg, cw, cb, wo, fg,
             yt_ref, conv_ref, dx2_ref, dx2b_ref, loss_ref, dfg_ref):
        i = pl.program_id(0)

        @pl.when(i == 0)
        def _():
            loss_ref[...] = jnp.zeros(loss_ref.shape, F32)
            dfg_ref[...] = jnp.zeros(dfg_ref.shape, F32)

        on, _ = _head_norm(_f32(of) + _f32(ob), gg[...])
        zav = _f32(za)
        y_a = on * (zav * _sigmoid(zav))
        u = _f32(cg) * _f32(hc)
        prev_row = jnp.where(i > 0, _last_row(cprev) * _last_row(hprev), 0.0)
        next_row = jnp.where(i < nt - 1, _first_row(cnext) * _first_row(hnext), 0.0)
        up, un = _shift_rows(u, prev_row, next_row)
        conv = (cw[0:1, :] * up + cw[1:2, :] * u + cw[2:3, :] * un) + cb[...]
        conv_ref[...] = conv.astype(BF16)
        zcv = _f32(zc)
        y_c = _f32(bg) * conv * (zcv * _sigmoid(zcv))
        y = jnp.concatenate([y_a, y_c], axis=1)
        yt_ref[...] = y.T.astype(BF16)
        x2 = x_ref[...] + _dot(y.astype(BF16), wo[...])
        r = lax.rsqrt(jnp.mean(x2 * x2, axis=-1, keepdims=True) + EPS)
        xn = x2 * r
        err = xn * fg[...] - t_ref[...]
        loss_ref[...] += 0.5 * jnp.sum(jnp.mean(err * err, axis=-1, keepdims=True))
        dyf = err * (1.0 / D_MODEL)
        dfg_ref[...] += jnp.sum(dyf * xn, axis=0, keepdims=True)
        dxn = dyf * fg[...]
        dx2 = r * dxn - xn * (r * jnp.mean(dxn * xn, axis=-1, keepdims=True))
        dx2_ref[...] = dx2
        dx2b_ref[...] = dx2.astype(BF16)

    def col(off):
        return pl.BlockSpec((tm, CONV_W), lambda i: (i, off // CONV_W))

    rowt = pl.BlockSpec((tm, D_MODEL), lambda i: (i, 0))
    const = lambda shape: pl.BlockSpec(shape, lambda i: (0, 0))
    return pl.pallas_call(
        body, name="mix_out_loss",
        out_shape=(jax.ShapeDtypeStruct((MIX_W, seq), BF16), jax.ShapeDtypeStruct((seq, CONV_W), BF16),
                   jax.ShapeDtypeStruct((seq, D_MODEL), F32), jax.ShapeDtypeStruct((seq, D_MODEL), BF16),
                   jax.ShapeDtypeStruct((8, 128), F32), jax.ShapeDtypeStruct((1, D_MODEL), F32)),
        grid=(nt,),
        in_specs=[rowt, rowt, col(OFF_ZA), col(OFF_B), col(OFF_C), col(OFF_H), col(OFF_ZC)]
        + _halo_specs(tm, seq, OFF_C // CONV_W) + _halo_specs(tm, seq, OFF_H // CONV_W)
        + [rowt, rowt, const((1, DV)), const((8, CONV_W)), const((1, CONV_W)), const((MIX_W, D_MODEL)),
           const((1, D_MODEL))],
        out_specs=(pl.BlockSpec((MIX_W, tm), lambda i: (0, i)), rowt, rowt, rowt, const((8, 128)),
                   const((1, D_MODEL))),
        compiler_params=_cparams("arbitrary"),
    )(o_f, o_b, proj, proj, proj, proj, proj, proj, proj, proj, proj, x2d, tgt, gla_g, conv_w, conv_b, w_out, final_g)


def _dsilu(z, s):
    return s * (1.0 + z * (1.0 - s))


def _mix_bwd(dx2b, o_f, o_b, proj, conv, gla_g, w_out, tm):
    seq = dx2b.shape[0]

    def body(dx, of, ob, za, bg, zc, cv, gg, wo, dg_ref, do_ref, dconv_ref, dgg_ref, dcb_ref):
        @pl.when(pl.program_id(0) == 0)
        def _():
            dgg_ref[...] = jnp.zeros(dgg_ref.shape, F32)
            dcb_ref[...] = jnp.zeros(dcb_ref.shape, F32)

        dy = _dot_nt(dx[...], wo[...])
        dy_a, dy_c = dy[:, :V_W], dy[:, V_W:]
        zcv, bgv, convv = _f32(zc), _f32(bg), _f32(cv)
        sc = _sigmoid(zcv)
        szc = zcv * sc
        dg_ref[:, CONV_W:2 * CONV_W] = (dy_c * convv * szc).astype(BF16)
        dconv = dy_c * bgv * szc
        dconv_ref[...] = dconv.astype(BF16)
        dcb_ref[...] += jnp.sum(dconv, axis=0, keepdims=True)
        dg_ref[:, 2 * CONV_W:] = (dy_c * bgv * convv * _dsilu(zcv, sc)).astype(BF16)

        o = _f32(of) + _f32(ob)
        gain = gg[...]
        on, rinv = _head_norm(o, gain)
        zav = _f32(za)
        sa = _sigmoid(zav)
        dg_ref[:, :CONV_W] = (dy_a * on * _dsilu(zav, sa)).astype(BF16)
        don = dy_a * (zav * sa)
        dgg = jnp.zeros((1, DV), F32)
        dos = []
        for h in range(HEADS):
            sl = slice(h * DV, (h + 1) * DV)
            oh, r, dh = o[:, sl], rinv[h], don[:, sl]
            ohn = oh * r
            dgg = dgg + jnp.sum(dh * ohn, axis=0, keepdims=True)
            dn = dh * gain
            dos.append(r * dn - ohn * (r * jnp.mean(dn * ohn, axis=-1, keepdims=True)))
        dgg_ref[...] += dgg
        do_ref[...] = jnp.concatenate(dos, axis=1).astype(BF16)

    def col(off):
        return pl.BlockSpec((tm, CONV_W), lambda i: (i, off // CONV_W))

    rowt = pl.BlockSpec((tm, D_MODEL), lambda i: (i, 0))
    const = lambda shape: pl.BlockSpec(shape, lambda i: (0, 0))
    return pl.pallas_call(
        body, name="mix_bwd",
        out_shape=(jax.ShapeDtypeStruct((seq, GATES_W), BF16), jax.ShapeDtypeStruct((seq, V_W), BF16),
                   jax.ShapeDtypeStruct((seq, CONV_W), BF16),
                   jax.ShapeDtypeStruct((1, DV), F32), jax.ShapeDtypeStruct((1, CONV_W), F32)),
        grid=(seq // tm,),
        in_specs=[rowt, rowt, rowt, col(OFF_ZA), col(OFF_B), col(OFF_ZC), rowt, const((1, DV)),
                  const((MIX_W, D_MODEL))],
        out_specs=(pl.BlockSpec((tm, GATES_W), lambda i: (i, 0)), rowt, rowt, const((1, DV)), const((1, CONV_W))),
        compiler_params=_cparams("arbitrary"),
    )(dx2b, o_f, o_b, proj, proj, proj, conv, gla_g, w_out)


def _conv_bwd(dconv, proj, conv_w, tm):
    seq = dconv.shape[0]
    nt = seq // tm

    def body(dc_in, dprev, dnext, cg, hc, cprev, cnext, hprev, hnext, cw, dch_ref, dcw_ref):
        i = pl.program_id(0)

        @pl.when(i == 0)
        def _():
            dcw_ref[...] = jnp.zeros(dcw_ref.shape, F32)

        first, lastt = i > 0, i < nt - 1
        dcv = _f32(dc_in)
        d_up, d_un = _shift_rows(dcv, jnp.where(first, _last_row(dprev), 0.0), jnp.where(lastt, _first_row(dnext), 0.0))
        cgv, hcv = _f32(cg), _f32(hc)
        u = cgv * hcv
        u_up, u_un = _shift_rows(u, jnp.where(first, _last_row(cprev) * _last_row(hprev), 0.0),
                                 jnp.where(lastt, _first_row(cnext) * _first_row(hnext), 0.0))
        du = cw[0:1, :] * d_un + cw[1:2, :] * dcv + cw[2:3, :] * d_up
        dch_ref[:, :CONV_W] = (du * hcv).astype(BF16)
        dch_ref[:, CONV_W:] = (du * cgv).astype(BF16)
        dcw_ref[0:1, :] += jnp.sum(dcv * u_up, axis=0, keepdims=True)
        dcw_ref[1:2, :] += jnp.sum(dcv * u, axis=0, keepdims=True)
        dcw_ref[2:3, :] += jnp.sum(dcv * u_un, axis=0, keepdims=True)

    def col(off):
        return pl.BlockSpec((tm, CONV_W), lambda i: (i, off // CONV_W))

    rowt = pl.BlockSpec((tm, CONV_W), lambda i: (i, 0))
    const = lambda shape: pl.BlockSpec(shape, lambda i: (0, 0))
    return pl.pallas_call(
        body, name="conv_bwd",
        out_shape=(jax.ShapeDtypeStruct((seq, CH_W), BF16), jax.ShapeDtypeStruct((8, CONV_W), F32)),
        grid=(nt,),
        in_specs=[rowt] + _halo_specs(tm, seq, 0) + [col(OFF_C), col(OFF_H)]
        + _halo_specs(tm, seq, OFF_C // CONV_W) + _halo_specs(tm, seq, OFF_H // CONV_W) + [const((8, CONV_W))],
        out_specs=(pl.BlockSpec((tm, CH_W), lambda i: (i, 0)), const((8, CONV_W))),
        compiler_params=_cparams("arbitrary"),
    )(dconv, dconv, dconv, proj, proj, proj, proj, proj, proj, conv_w)


def _gla_bwd(proj, lr, do, st_f, st_b, wgk_f, wgk_b, bgk_f, bgk_b, tt):
    seq = proj.shape[0]
    nb, nc = seq // tt, tt // CHUNK

    def body(qf, kf, vf, lrf, dof, stf, qb, kb, vb, lrb, dob, stb, wf, wb, bf, bb,
             dqkv_f, dlr_f, dqkv_b, dlr_b, dwf, dwb, dbf, dbb,
             ds_scr, eq_s, ek_s, ein_s, eout_s, qs_s, ks_s, qin_s, kout_s, db_s, lg_s):
        @pl.when(pl.program_id(0) == 0)
        def _():
            ds_scr[...] = jnp.zeros(ds_scr.shape, F32)
            for r in (dwf, dwb, dbf, dbb):
                r[...] = jnp.zeros(r.shape, F32)

        low, upp, sup = _block_masks(tt)
        row = lax.broadcasted_iota(jnp.int32, (CHUNK, 1), 0)
        kmask = _chunk_column_mask(tt)
        dirs = ((qf, kf, vf, lrf, dof, stf, wf, bf, dqkv_f, dlr_f, dwf, dbf,
                 low, upp, low, REF_F, LAST_F, list(reversed(range(nc)))),
                (qb, kb, vb, lrb, dob, stb, wb, bb, dqkv_b, dlr_b, dwb, dbb,
                 upp, low, sup, REF_B, LAST_B, list(range(nc))))
        for d, (q_r, k_r, v_r, lr_r, do_r, st_r, w_r, b_r, dqkv_r, dlr_r, dw_r, db_r,
                cum, cum_t, mask, ref, last, order) in enumerate(dirs):
            lrv = lr_r[...].astype(BF16)
            wv = w_r[...]
            logits = _dot(lrv, wv) + b_r[...]
            lg_s[...] = logits
            b = _dot_split3(cum.astype(BF16), _log_gate(logits))
            decs = []
            for c in range(nc):
                rows = slice(c * CHUNK, (c + 1) * CHUNK)
                bc = b[rows]
                b_ref, b_last = bc[ref:ref + 1], bc[last:last + 1]
                qc = q_r[rows, :].astype(F32) * QSCALE
                kc = k_r[rows, :].astype(F32)
                e_q, e_k, e_in, e_out = jnp.exp(bc - b_ref), jnp.exp(b_ref - bc), jnp.exp(bc), jnp.exp(b_last - bc)
                eq_s[rows, :], ek_s[rows, :], ein_s[rows, :], eout_s[rows, :] = e_q, e_k, e_in, e_out
                qs_s[rows, :] = (qc * e_q).astype(BF16)
                ks_s[rows, :] = (kc * e_k).astype(BF16)
                qin_s[rows, :] = (qc * e_in).astype(BF16)
                kout_s[rows, :] = (kc * e_out).astype(BF16)
                decs.append(jnp.exp(b_last))
            for h in range(HEADS):
                ksl = slice(h * DK, (h + 1) * DK)
                vsl = slice(h * DV, (h + 1) * DV)
                v = v_r[:, vsl].astype(BF16)
                dov = do_r[:, vsl].astype(BF16)
                qsb, ksb = qs_s[:, ksl], ks_s[:, ksl]
                att = jnp.where(mask, _dot_nt(qsb, ksb), 0.0).astype(BF16)
                datt = jnp.where(mask, _dot_nt(dov, v), 0.0).astype(BF16)
                dqs = _dot(datt, ksb)
                dks = _dot_tn(datt, qsb)
                dv_intra = _dot_tn(att, dov)
                g_t = _dot_tn(dov, _chunked(kmask, qin_s[:, ksl], nc))
                ds = ds_scr[d * HEADS + h]
                for c in order:
                    rows = slice(c * CHUNK, (c + 1) * CHUNK)
                    dsb = ds.astype(BF16)
                    s_prev = st_r[c, h]
                    dk_out = _dot(v[rows], dsb)
                    dq_in = _dot(dov[rows], s_prev)
                    dv = dv_intra[rows] + _dot_nt(kout_s[rows, ksl], dsb)
                    dqkv_r[rows, OFF_V + h * DV:OFF_V + (h + 1) * DV] = dv.astype(BF16)
                    dec = decs[c][:, ksl]
                    ddec = jnp.sum(ds * s_prev.astype(F32), axis=0, keepdims=True)
                    e_out = eout_s[rows, ksl]
                    qc = q_r[rows, ksl].astype(F32) * QSCALE
                    kc = k_r[rows, ksl].astype(F32)
                    dq = dqs[rows] * eq_s[rows, ksl] + dq_in * ein_s[rows, ksl]
                    dk = dks[rows] * ek_s[rows, ksl] + dk_out * e_out
                    dqkv_r[rows, OFF_Q + h * DK:OFF_Q + (h + 1) * DK] = (dq * QSCALE).astype(BF16)
                    dqkv_r[rows, OFF_K + h * DK:OFF_K + (h + 1) * DK] = dk.astype(BF16)
                    tail = jnp.sum(dk_out * (kc * e_out), axis=0, keepdims=True) + ddec * dec
                    db_s[rows, ksl] = (qc * dq - kc * dk) + jnp.where(row == last, tail, 0.0)
                    ds = ds * dec + g_t[:, c * DK:(c + 1) * DK]
                ds_scr[d * HEADS + h] = ds
            dg = _dot_split3(cum_t.astype(BF16), db_s[...])
            dlogit = (dg * GATE_SCALE) * _sigmoid(-lg_s[...])
            dlb = dlogit.astype(BF16)
            dlr_r[...] = _dot_nt(dlb, wv)
            dw_r[...] += _dot_tn(lrv, dlb)
            db_r[...] += jnp.sum(dlogit, axis=0, keepdims=True)

    fw = lambda i: (nb - 1 - i, 0)
    bw = lambda i: (i, 0)
    const = lambda i: (0, 0)

    def tok_specs(m):
        return [pl.BlockSpec((tt, QK_W), lambda i: (m(i)[0], OFF_Q // QK_W)),
                pl.BlockSpec((tt, QK_W), lambda i: (m(i)[0], OFF_K // QK_W)),
                pl.BlockSpec((tt, V_W), lambda i: (m(i)[0], OFF_V // V_W)),
                pl.BlockSpec((tt, LR_W), m),
                pl.BlockSpec((tt, V_W), m),
                pl.BlockSpec((nc, HEADS, DV, DK), lambda i: (m(i)[0], 0, 0, 0))]

    dqkv = jax.ShapeDtypeStruct((seq, QK_W + QK_W + V_W), BF16)
    dlr = jax.ShapeDtypeStruct((seq, LR_W), F32)
    dw = jax.ShapeDtypeStruct((LR_W, QK_W), F32)
    dbias = jax.ShapeDtypeStruct((1, QK_W), F32)
    return pl.pallas_call(
        body, name="gla_bwd",
        out_shape=(dqkv, dlr, dqkv, dlr, dw, dw, dbias, dbias),
        grid=(nb,),
        in_specs=tok_specs(fw) + tok_specs(bw) + [
            pl.BlockSpec((LR_W, QK_W), const), pl.BlockSpec((LR_W, QK_W), const),
            pl.BlockSpec((1, QK_W), const), pl.BlockSpec((1, QK_W), const)],
        out_specs=(pl.BlockSpec((tt, QK_W + QK_W + V_W), fw), pl.BlockSpec((tt, LR_W), fw),
                   pl.BlockSpec((tt, QK_W + QK_W + V_W), bw), pl.BlockSpec((tt, LR_W), bw),
                   pl.BlockSpec((LR_W, QK_W), const), pl.BlockSpec((LR_W, QK_W), const),
                   pl.BlockSpec((1, QK_W), const), pl.BlockSpec((1, QK_W), const)),
        scratch_shapes=[pltpu.VMEM((2 * HEADS, DV, DK), F32)] + [pltpu.VMEM((tt, QK_W), F32)] * 4
        + [pltpu.VMEM((tt, QK_W), BF16)] * 4 + [pltpu.VMEM((tt, QK_W), F32)] * 2,
        compiler_params=_cparams("arbitrary"),
    )(proj, proj, proj, lr, do, st_f, proj, proj, proj, lr, do, st_b, wgk_f, wgk_b, bgk_f, bgk_b)


def _both_directions(f_ref, b_ref):
    return (_f32(f_ref) + _f32(b_ref)).astype(BF16)


def _input_grad(dqkv_f, dqkv_b, dp_gates, dp_ch, dlr_f, dlr_b, w_nat, x2d, norm_g, dx2, sums, tm):
    seq = x2d.shape[0]
    nt, n = seq // tm, len(sums)
    relay_step = (3 * nt) // 8

    def body(dqf, dqb, dg, dc, dlf, dlb, w, x_ref, g_ref, dx2_ref, *rest):
        ins, (gx_ref, dng_ref), outs = rest[:n], rest[n:n + 2], rest[n + 2:2 * n + 2]
        passing, joined = rest[2 * n + 2:3 * n + 2], rest[3 * n + 2:4 * n + 2]
        send_sems, recv_sems, local_sems = rest[4 * n + 2:]
        i = pl.program_id(0)
        c = lax.axis_index("c")
        first, second, diagonal = _route_chips()
        slot = lambda chip: 2 * chip[0] + chip[1]

        def remote(a, k, src, dst, to):
            return pltpu.make_async_remote_copy(src_ref=src, dst_ref=dst, send_sem=send_sems.at[3 * a + k],
                                                recv_sem=recv_sems.at[3 * a + k], device_id=(*to, c),
                                                device_id_type=MESH)

        direct = lambda a: remote(a, 0, ins[a].at[slot(first)], outs[a].at[0], first)
        for_second = lambda a: remote(a, 1, ins[a].at[slot(diagonal)], passing[a], first)
        joint = lambda a: remote(a, 2, joined[a], outs[a].at[1], second)
        own = lambda a: pltpu.make_async_copy(ins[a].at[slot(second)], joined[a], local_sems.at[a])

        @pl.when(i == 0)
        def _():
            _start_all([for_second(a) for a in range(n)] + [own(a) for a in range(n)] + [direct(a) for a in range(n)])
            dng_ref[...] = jnp.zeros(dng_ref.shape, F32)

        @pl.when(i == relay_step)
        def _():
            for a in range(n):
                for_second(a).wait_recv()
                own(a).wait()
                joined[a][...] = (joined[a][...].astype(F32) + passing[a][...].astype(F32)).astype(BF16)
                joint(a).start()

        dh = (_dot((dlf[...] + dlb[...]).astype(BF16), w[NAT_LR:NAT_LR + LR_W, :])
              + _dot(_both_directions(dqf, dqb), w[0:NAT_ZA, :])
              + _dot(dg[:, 0:CONV_W], w[NAT_ZA:NAT_LR, :]) + _dot(dg[:, CONV_W:2 * CONV_W], w[NAT_B:NAT_C, :])
              + _dot(dg[:, 2 * CONV_W:], w[NAT_ZC:IN_W, :]) + _dot(dc[...], w[NAT_C:NAT_ZC, :]))
        xv = x_ref[...]
        r = lax.rsqrt(jnp.mean(xv * xv, axis=-1, keepdims=True) + EPS)
        xn = xv * r
        dng_ref[...] += jnp.sum(dh * xn, axis=0, keepdims=True)
        dn = dh * g_ref[...]
        gx_ref[...] = (r * dn - xn * (r * jnp.mean(dn * xn, axis=-1, keepdims=True))) + dx2_ref[...]

        @pl.when(i == nt - 1)
        def _():
            for a in range(n):
                direct(a).wait_recv()
                joint(a).wait_recv()
            for a in range(n):
                for cp in (direct(a), for_second(a), joint(a)):
                    cp.wait_send()

    rowt = pl.BlockSpec((tm, D_MODEL), lambda i: (i, 0))
    seg = lambda width: pl.BlockSpec((tm, width), lambda i: (i, 0))
    resident = lambda rows: pl.BlockSpec((rows, D_MODEL), lambda i: (0, 0), pipeline_mode=pl.Buffered(1))
    hbm = pl.BlockSpec(memory_space=pl.ANY)
    blocks = [pltpu.VMEM(s.shape[1:], s.dtype) for s in sums]
    return pl.pallas_call(
        body, name="input_grad",
        out_shape=(jax.ShapeDtypeStruct((seq, D_MODEL), F32), jax.ShapeDtypeStruct((1, D_MODEL), F32))
        + tuple(jax.ShapeDtypeStruct((2,) + s.shape[1:], s.dtype) for s in sums),
        grid=(nt,),
        in_specs=[seg(QKV_W), seg(QKV_W), seg(GATES_W), seg(CH_W), seg(LR_W), seg(LR_W), resident(IN_W),
                  rowt, pl.BlockSpec((1, D_MODEL), lambda i: (0, 0)), rowt] + [hbm] * n,
        out_specs=(rowt, pl.BlockSpec((1, D_MODEL), lambda i: (0, 0))) + (hbm,) * n,
        scratch_shapes=blocks + blocks + [pltpu.SemaphoreType.DMA((3 * n,)), pltpu.SemaphoreType.DMA((3 * n,)),
                                          pltpu.SemaphoreType.DMA((n,))],
        compiler_params=_cparams("arbitrary"),
    )(dqkv_f, dqkv_b, dp_gates, dp_ch, dlr_f, dlr_b, w_nat, x2d, norm_g, dx2, *sums)


def _weight_grad_out(y_t, dx2b, tk, riding):
    m, seq = y_t.shape
    n = dx2b.shape[1]
    nk = seq // tk

    def body(a_ref, b_ref, ride_in, o_ref, ride_out, send_sems, recv_sems):
        k = pl.program_id(0)

        @pl.when(k == 0)
        def _():
            _start_all(_sibling_copies(ride_in, ride_out, send_sems, recv_sems))
            o_ref[...] = jnp.zeros(o_ref.shape, F32)

        o_ref[...] += _dot(a_ref[...], b_ref[...])

        @pl.when(k == nk - 1)
        def _():
            _wait_all(_sibling_copies(ride_in, ride_out, send_sems, recv_sems))

    hbm = pl.BlockSpec(memory_space=pl.ANY)
    return pl.pallas_call(
        body, name="wgrad_out",
        out_shape=(jax.ShapeDtypeStruct((m, n), F32), jax.ShapeDtypeStruct((4,) + _block_shape(riding), F32)),
        grid=(nk,),
        in_specs=[pl.BlockSpec((m, tk), lambda k: (0, k)), pl.BlockSpec((tk, n), lambda k: (k, 0)), hbm],
        out_specs=(pl.BlockSpec((m, n), lambda k: (0, 0)), hbm),
        scratch_shapes=[pltpu.SemaphoreType.DMA((4,)), pltpu.SemaphoreType.DMA((4,))],
        compiler_params=_cparams("arbitrary"),
    )(y_t, dx2b, riding)


def _weight_grad_in(h_t, dqkv_f, dqkv_b, dp_gates, dp_ch, dlr_f, dlr_b):
    m, seq = h_t.shape
    tn = 512
    n_qkv, n_gates, n_ch = QKV_W // tn, GATES_W // tn, CH_W // tn
    starts = ([k * tn for k in range(n_qkv)] + [NAT_ZA, NAT_ZA + tn, NAT_B, NAT_B + tn, NAT_ZC, NAT_ZC + tn]
              + [NAT_C + k * tn for k in range(n_ch)])

    def out_row(j):
        row = 0
        for k, start in enumerate(starts):
            row = row + jnp.where(j == k, start // 32, 0)
        return pl.multiple_of(row * 32, 32), 0

    def body(a_ref, bqf, bqb, bg, bc, dlf, dlb, o_ref, lr_ref, acc, bq):
        j = pl.program_id(0)

        @pl.when(j == 0)
        def _():
            acc[:, 0:LR_W] = _dot(a_ref[...], (dlf[...] + dlb[...]).astype(BF16))
            lr_ref[...] = acc[:, 0:LR_W].T[0:2 * RANK, :]

        @pl.when(j < n_qkv)
        def _():
            bq[...] = _both_directions(bqf, bqb)
            acc[...] = _dot(a_ref[...], bq[...])

        @pl.when(jnp.logical_and(j >= n_qkv, j < n_qkv + n_gates))
        def _():
            acc[...] = _dot(a_ref[...], bg[...])

        @pl.when(j >= n_qkv + n_gates)
        def _():
            acc[...] = _dot(a_ref[...], bc[...])

        o_ref[...] = acc[...].T

    resident = lambda shape: pl.BlockSpec(shape, lambda j: (0, 0), pipeline_mode=pl.Buffered(1))
    seg = lambda first, count: pl.BlockSpec((seq, tn), lambda j: (0, jnp.clip(j - first, 0, count - 1)))
    main, lr_rows = pl.pallas_call(
        body, name="wgrad_in",
        out_shape=(jax.ShapeDtypeStruct((IN_W, m), F32), jax.ShapeDtypeStruct((2 * RANK, m), F32)),
        grid=(n_qkv + n_gates + n_ch,),
        in_specs=[resident((m, seq)), seg(0, n_qkv), seg(0, n_qkv), seg(n_qkv, n_gates), seg(n_qkv + n_gates, n_ch),
                  resident((seq, LR_W)), resident((seq, LR_W))],
        out_specs=(pl.BlockSpec((pl.Element(tn), pl.Element(m)), out_row),
                   pl.BlockSpec((2 * RANK, m), lambda j: (0, 0))),
        scratch_shapes=[pltpu.VMEM((m, tn), F32), pltpu.VMEM((seq, tn), BF16)],
        compiler_params=_cparams("arbitrary"),
    )(h_t, dqkv_f, dqkv_b, dp_gates, dp_ch, dlr_f, dlr_b)
    return lax.dynamic_update_slice(main, lr_rows, (NAT_LR, 0))


def _pad_rows(a, rows):
    return jnp.pad(a, ((0, rows - a.shape[0]), (0, 0)))


def _rows128(a):
    a = a.reshape(-1, 128)
    return _pad_rows(a, -(-a.shape[0] // 8) * 8)


def _pack(arrs):
    return jnp.concatenate([_rows128(a) for a in arrs], axis=0)


def _unpack(buf, like):
    out, start = [], 0
    for a in like:
        rows = a.size // 128
        out.append(buf[start:start + rows].reshape(a.shape))
        start += -(-rows // 8) * 8
    return out


def kernel(x, norm_g, w_in, w_gk_f, b_gk_f, w_gk_b, b_gk_b, gla_norm_g, conv_w, conv_b, w_out, final_g, loss_target, m_norm_g, m_w_in, m_w_gk_f, m_b_gk_f, m_w_gk_b, m_b_gk_b, m_gla_norm_g, m_conv_w, m_conv_b, m_w_out, m_final_g, v_norm_g, v_w_in, v_w_gk_f, v_b_gk_f, v_w_gk_b, v_b_gk_b, v_gla_norm_g, v_conv_w, v_conv_b, v_w_out, v_final_g):
    px, py, pc = _position()
    me = _blk(px, py, pc)
    seq = x.shape[1]
    x2d, tgt = x[0], loss_target[0]
    tt = min(256, seq)

    small_s = jnp.concatenate([jnp.concatenate([w_gk_f[0], w_gk_b[0]], axis=1), _pad_rows(conv_w[0], 8)], axis=0)
    order = sum(jnp.where(2 * px + py == k, jnp.asarray(tiles + (0,), jnp.int32), 0) for k, tiles in enumerate(TILE_ORDER))
    proj, lr, h_t, w_nat, wout_all, small_all = _gather_inproj(x2d, norm_g, w_in[0].T, w_out[0], small_s, order,
                                                               min(1024, seq))
    w_out_full = wout_all.reshape(MIX_W, D_MODEL)
    wgk_cols = 512 // N_DEV
    wgk_f_full = small_all[:, 0:RANK, 0:wgk_cols].transpose(1, 0, 2).reshape(RANK, QK_W)
    wgk_b_full = small_all[:, 0:RANK, wgk_cols:2 * wgk_cols].transpose(1, 0, 2).reshape(RANK, QK_W)
    conv_w_full = _pad_rows(small_all[:, RANK:RANK + 3, :].transpose(1, 0, 2).reshape(3, CONV_W), 8)
    zr = lambda n: jnp.zeros((n, QK_W), F32)
    wgk_f_pad = jnp.concatenate([wgk_f_full, zr(LR_W - RANK)], axis=0).astype(BF16)
    wgk_b_pad = jnp.concatenate([zr(RANK), wgk_b_full, zr(LR_W - 2 * RANK)], axis=0).astype(BF16)

    o_f, o_b, st_f, st_b = _gla_fwd(proj, lr, wgk_f_pad, wgk_b_pad, b_gk_f, b_gk_b, tt)
    tmix = min(512, seq)
    y_t, conv, dx2, dx2b, loss_p, dfg_p = _mix_out_loss(o_f, o_b, proj, x2d, tgt, gla_norm_g, conv_w_full, conv_b,
                                                        w_out_full, final_g.reshape(1, D_MODEL), tmix)

    dp_gates, do, dconv, dgg_p, dcb_p = _mix_bwd(dx2b, o_f, o_b, proj, conv, gla_norm_g, w_out_full, tmix)
    dp_ch, dcw_p = _conv_bwd(dconv, proj, conv_w_full, tmix)
    dqkv_f, dlr_f, dqkv_b, dlr_b, dwf_p, dwb_p, dbf_p, dbb_p = _gla_bwd(
        proj, lr, do, st_f, st_b, wgk_f_pad, wgk_b_pad, b_gk_f, b_gk_b, tt)
    dw_nat = _weight_grad_in(h_t, dqkv_f, dqkv_b, dp_gates, dp_ch, dlr_f, dlr_b)

    dw_out, sib_in = _weight_grad_out(y_t, dx2b, min(2048, seq), dw_nat)
    part_out = dw_out.reshape(N_DEV, MIX_W // N_DEV, D_MODEL)
    core = jnp.reshape(pc, (1,)).astype(jnp.int32)
    chip = jnp.reshape(2 * px + py, (1,)).astype(jnp.int32)
    sums_in, sib_out = _chip_sums(dw_nat, sib_in, core, D_MODEL, "chip_sums_in", riding=part_out)
    sums_out = _chip_sums(part_out, sib_out, core, D_MODEL, "chip_sums_out")
    grad_x2d, dng_p, far_in, far_out = _input_grad(dqkv_f, dqkv_b, dp_gates, dp_ch, dlr_f, dlr_b, w_nat, x2d, norm_g, dx2,
                                                   [sums_in, sums_out], min(256, seq))
    pieces = [dng_p, dbf_p, dbb_p, dgg_p, dcb_p, dfg_p[0], dwf_p[0:RANK], dwb_p[RANK:2 * RANK], dcw_p[0:3], loss_p[0]]
    g_window, small_tot = _final_sum(sums_in, far_in, chip, _pack(pieces), 512, "final_sum_in")
    g_in_t = lax.dynamic_slice_in_dim(g_window, 4 * pc, SHARD_W, axis=0)
    g_w_out, d_w_out, nm_w_out, nv_w_out = _final_sum_adamw(sums_out, far_out, chip, w_out[0], m_w_out[0], v_w_out[0],
                                                            256, "adamw_out")
    flat = lambda a: a[0].T.reshape(SHARD_W, D_MODEL // 128, 128)
    unflat = lambda a: a.reshape(SHARD_W, D_MODEL).T
    d_flat, m_flat, v_flat = _adamw_rows(g_in_t.reshape(SHARD_W, D_MODEL // 128, 128), flat(w_in), flat(m_w_in),
                                         flat(v_w_in), 450, "adamw_in")
    g_w_in, d_w_in, nm_w_in, nv_w_in = g_in_t.T, unflat(d_flat), unflat(m_flat), unflat(v_flat)

    tot = _unpack(small_tot, pieces)
    g_norm_g, g_b_gk_f, g_b_gk_b, g_gla, g_conv_b, g_final = tot[:6]
    g_wgk_f = lax.dynamic_slice_in_dim(tot[6], me * wgk_cols, wgk_cols, axis=1)[None]
    g_wgk_b = lax.dynamic_slice_in_dim(tot[7], me * wgk_cols, wgk_cols, axis=1)[None]
    g_conv_w = lax.dynamic_slice_in_dim(tot[8], me * 128, 128, axis=1)[None]
    loss = tot[9][0]

    small_g = [g_norm_g, g_b_gk_f, g_b_gk_b, g_gla, g_conv_b, g_final, g_wgk_f, g_wgk_b, g_conv_w]
    small_w = [norm_g, b_gk_f, b_gk_b, gla_norm_g, conv_b, final_g, w_gk_f, w_gk_b, conv_w]
    small_m = [m_norm_g, m_b_gk_f, m_b_gk_b, m_gla_norm_g, m_conv_b, m_final_g, m_w_gk_f, m_w_gk_b, m_conv_w]
    small_v = [v_norm_g, v_b_gk_f, v_b_gk_b, v_gla_norm_g, v_conv_b, v_final_g, v_w_gk_f, v_w_gk_b, v_conv_w]
    d_s, m_s, v_s = _adamw_small(_pack(small_g), _pack(small_w), _pack(small_m), _pack(small_v))
    d_l, m_l, v_l = _unpack(d_s, small_w), _unpack(m_s, small_w), _unpack(v_s, small_w)

    def ordered(sm, big_in, big_out):
        return [sm[0], big_in[None], sm[6], sm[1], sm[7], sm[2], sm[3], sm[8], sm[4], big_out[None], sm[5]]

    grads = ordered(small_g, g_w_in, g_w_out)
    deltas = ordered(d_l, d_w_in, d_w_out)
    new_m = ordered(m_l, nm_w_in, nm_w_out)
    new_v = ordered(v_l, nv_w_in, nv_w_out)
    return (loss, grad_x2d[None], *grads, *deltas, *new_m, *new_v)
```

```python
import functools

import jax
import jax.numpy as jnp
from jax import lax
from jax.experimental import pallas as pl
from jax.experimental.pallas import tpu as pltpu

F32 = jnp.float32
BF16 = jnp.bfloat16
MESH = pl.DeviceIdType.MESH

N_DEV = 8
D_MODEL = 1024
HEADS = 4
DK = 128
DV = 256
QK_W = HEADS * DK
V_W = HEADS * DV
CONV_W = 1024
MIX_W = V_W + CONV_W
CHUNK = 64
RANK = 16
IN_W = 7200
SHARD_W = IN_W // N_DEV
MAIN_W = 7168
LR_W = 128
OFF_Q, OFF_K, OFF_V, OFF_ZA, OFF_B, OFF_ZC, OFF_C, OFF_H = 0, 512, 1024, 2048, 3072, 4096, 5120, 6144
QKV_W, GATES_W, CH_W = 2048, 3072, 2048
NAT_ZA, NAT_LR, NAT_B, NAT_C, NAT_ZC = 2048, 3072, 3104, 4128, 6176
EPS = 1e-6
GATE_SCALE = 1.0 / 16.0
QSCALE = DK ** -0.5
REF_F, LAST_F = CHUNK // 2, CHUNK - 1
REF_B, LAST_B = CHUNK - 1 - CHUNK // 2, 0

ADAM_LR = 0.001
ADAM_B1 = 0.9
ADAM_B2 = 0.999
ADAM_EPS = 1e-08
ADAM_WD = 0.01
ADAM_STEP = 10

VMEM_LIMIT = 56 * 1024 * 1024


def _cparams(*sem):
    return pltpu.CompilerParams(dimension_semantics=sem, vmem_limit_bytes=VMEM_LIMIT)


def _dot(a, b):
    return jnp.dot(a, b, preferred_element_type=F32)


def _dot_nt(a, b):
    return lax.dot_general(a, b, (((1,), (1,)), ((), ())), preferred_element_type=F32)


def _dot_tn(a, b):
    return lax.dot_general(a, b, (((0,), (0,)), ((), ())), preferred_element_type=F32)


def _sigmoid(z):
    return jax.nn.sigmoid(z)


def _position():
    return lax.axis_index("x"), lax.axis_index("y"), lax.axis_index("c")


def _blk(px, py, pc):
    return 4 * px + 2 * py + pc


EDGE = 16
SHIFTED_ROWS = 912
BODY_ROWS = SHIFTED_ROWS - 2 * EDGE


def _first_tile_row(blk, px):
    return EDGE * (56 * blk + px)


def _edge_tiles():
    tiles = {}
    for blk in range(N_DEV):
        first = _first_tile_row(blk, blk // 4)
        tiles.setdefault(first, []).append((blk, 0))
        tiles.setdefault(first + EDGE + BODY_ROWS, []).append((blk, 1))
    return tiles


def _peer_copies(srcs, outs, send_sems, recv_sems):
    x, y, c = _position()
    me = _blk(x, y, c)
    copies = []
    for a, (src, out) in enumerate(zip(srcs, outs)):
        k = 0
        for dx in (0, 1):
            for dy in (0, 1):
                for dc in (0, 1):
                    if dx + dy + dc == 0:
                        continue
                    peer = (1 - x if dx else x, 1 - y if dy else y, 1 - c if dc else c)
                    copies.append(pltpu.make_async_remote_copy(
                        src_ref=src, dst_ref=out.at[me], send_sem=send_sems.at[a * 7 + k],
                        recv_sem=recv_sems.at[a * 7 + k], device_id=peer, device_id_type=MESH))
                    k += 1
    return copies


def _route_chips():
    x, y, c = _position()
    along_x = c == 0
    return [(jnp.where(along_x, 1 - x, x), jnp.where(along_x, y, 1 - y)),
            (jnp.where(along_x, x, 1 - x), jnp.where(along_x, 1 - y, y)), (1 - x, 1 - y)]


WINDOW_ROWS = SHARD_W + 4


def _window_start(k, parity):
    return 2 * SHARD_W * k + (SHARD_W - 4) * parity


def _owner_block(part, k, parity):
    if part.ndim == 3:
        return part.at[2 * k + parity]
    return part.at[pl.ds(pl.multiple_of(_window_start(k, parity), 8), WINDOW_ROWS)]


def _block_shape(part):
    return part.shape[1:] if part.ndim == 3 else (WINDOW_ROWS, part.shape[1])


def _sibling_copies(part, out, send_sems, recv_sems):
    x, y, c = _position()
    return [pltpu.make_async_remote_copy(src_ref=_owner_block(part, k, 1 - c), dst_ref=out.at[k],
                                         send_sem=send_sems.at[k], recv_sem=recv_sems.at[k],
                                         device_id=(x, y, 1 - c), device_id_type=MESH)
            for k in range(4)]


def _start_all(copies):
    for cp in copies:
        cp.start()


def _wait_all(copies):
    for cp in copies:
        cp.wait_recv()
    for cp in copies:
        cp.wait_send()


def _chip_sums(part, from_sibling, core, tc, name, riding=None):
    rows, cols = _block_shape(part)
    nj = cols // tc

    def body(core_ref, p_ref, s_ref, *rest):
        if riding is None:
            (o_ref,) = rest
        else:
            ride_in, o_ref, ride_out, send_sems, recv_sems = rest
            k, j = pl.program_id(0), pl.program_id(1)

            @pl.when(jnp.logical_and(k == 0, j == 0))
            def _():
                _start_all(_sibling_copies(ride_in, ride_out, send_sems, recv_sems))

        o_ref[0] = (p_ref[...].reshape(rows, tc) + s_ref[0]).astype(BF16)

        if riding is not None:
            @pl.when(jnp.logical_and(k == 3, j == nj - 1))
            def _():
                _wait_all(_sibling_copies(ride_in, ride_out, send_sems, recv_sems))

    hbm = pl.BlockSpec(memory_space=pl.ANY)
    sums = jax.ShapeDtypeStruct((4, rows, cols), BF16)
    tile_out = pl.BlockSpec((1, rows, tc), lambda k, j, core_ref: (k, 0, j))
    if part.ndim == 3:
        mine = pl.BlockSpec((1, rows, tc), lambda k, j, core_ref: (2 * k + core_ref[0], 0, j))
    else:
        mine = pl.BlockSpec((pl.Element(rows), pl.Element(tc)),
                            lambda k, j, core_ref: (pl.multiple_of(_window_start(k, core_ref[0]), 8),
                                                    pl.multiple_of(j * tc, 128)))
    in_specs = [mine, pl.BlockSpec((1, rows, tc), lambda k, j, core_ref: (k, 0, j))]
    if riding is None:
        out_shape, out_specs, scratch, args = sums, tile_out, [], (core, part, from_sibling)
    else:
        out_shape = (sums, jax.ShapeDtypeStruct((4,) + _block_shape(riding), F32))
        out_specs, in_specs = (tile_out, hbm), in_specs + [hbm]
        scratch = [pltpu.SemaphoreType.DMA((4,)), pltpu.SemaphoreType.DMA((4,))]
        args = (core, part, from_sibling, riding)
    return pl.pallas_call(
        body, name=name, out_shape=out_shape,
        grid_spec=pltpu.PrefetchScalarGridSpec(num_scalar_prefetch=1, grid=(4, nj), in_specs=in_specs,
                                               out_specs=out_specs, scratch_shapes=scratch),
        compiler_params=_cparams("arbitrary", "arbitrary"),
    )(*args)


def _sum_chips(s_ref, r_ref):
    f = lambda a: a.astype(F32)
    return (f(s_ref[0]) + f(r_ref[0])) + f(r_ref[1])


def _final_sum(sums, from_chips, chip, small, tc, name):
    _, rows, cols = sums.shape
    nj = cols // tc

    def body(chip_ref, s_ref, r_ref, sm_ref, g_out, tot_ref, all_ref, send_sems, recv_sems):
        j = pl.program_id(0)
        me = _blk(*_position())

        @pl.when(j == 0)
        def _():
            all_ref[me] = sm_ref[...]
            _start_all(_peer_copies((all_ref.at[me],), (all_ref,), send_sems, recv_sems))

        g_out[...] = _sum_chips(s_ref, r_ref)

        @pl.when(j == nj - 1)
        def _():
            _wait_all(_peer_copies((all_ref.at[me],), (all_ref,), send_sems, recv_sems))
            acc = all_ref[0]
            for d in range(1, N_DEV):
                acc = acc + all_ref[d]
            tot_ref[...] = acc

    whole = pl.BlockSpec(small.shape, lambda j, chip_ref: (0, 0))
    return pl.pallas_call(
        body, name=name,
        out_shape=(jax.ShapeDtypeStruct((rows, cols), F32), jax.ShapeDtypeStruct(small.shape, F32)),
        grid_spec=pltpu.PrefetchScalarGridSpec(
            num_scalar_prefetch=1, grid=(nj,),
            in_specs=[pl.BlockSpec((1, rows, tc), lambda j, chip_ref: (chip_ref[0], 0, j)),
                      pl.BlockSpec((2, rows, tc), lambda j, chip_ref: (0, 0, j)), whole],
            out_specs=(pl.BlockSpec((rows, tc), lambda j, chip_ref: (0, j)), whole),
            scratch_shapes=[pltpu.VMEM((N_DEV,) + small.shape, F32), pltpu.SemaphoreType.DMA((7,)),
                            pltpu.SemaphoreType.DMA((7,))]),
        compiler_params=_cparams("arbitrary"),
    )(chip, sums, from_chips, small)


def _adamw_rows(g, w, m, v, tr, name):
    rows = g.shape[0]

    def body(g_ref, w_ref, m_ref, v_ref, d_out, m_out, v_out):
        delta, m_new, v_new = _adamw(w_ref[...], g_ref[...], m_ref[...], v_ref[...])
        d_out[...] = delta
        m_out[...] = m_new
        v_out[...] = v_new

    tile = pl.BlockSpec((tr,) + g.shape[1:], lambda r: (r, 0, 0))
    shp = jax.ShapeDtypeStruct(g.shape, F32)
    return pl.pallas_call(
        body, name=name, out_shape=(shp, shp, shp), grid=(rows // tr,),
        in_specs=[tile] * 4, out_specs=(tile, tile, tile),
        compiler_params=_cparams("arbitrary"),
    )(g, w, m, v)


def _adamw(w, g, m, v):
    m = ADAM_B1 * m + (1.0 - ADAM_B1) * g
    v = ADAM_B2 * v + (1.0 - ADAM_B2) * (g * g)
    m_hat = m / (1.0 - ADAM_B1 ** ADAM_STEP)
    v_hat = v / (1.0 - ADAM_B2 ** ADAM_STEP)
    delta = -ADAM_LR * (m_hat / (jnp.sqrt(v_hat) + ADAM_EPS) + ADAM_WD * w)
    return delta, m, v


def _final_sum_adamw(sums, from_chips, chip, w, m, v, tr, name):
    rows, cols = w.shape

    def body(chip_ref, s_ref, r_ref, w_ref, m_ref, v_ref, g_out, d_out, m_out, v_out):
        g = _sum_chips(s_ref, r_ref)
        delta, m_new, v_new = _adamw(w_ref[...], g, m_ref[...], v_ref[...])
        g_out[...] = g
        d_out[...] = delta
        m_out[...] = m_new
        v_out[...] = v_new

    tile = pl.BlockSpec((tr, cols), lambda r, chip_ref: (r, 0))
    shp = jax.ShapeDtypeStruct((rows, cols), F32)
    return pl.pallas_call(
        body, name=name,
        out_shape=(shp, shp, shp, shp),
        grid_spec=pltpu.PrefetchScalarGridSpec(
            num_scalar_prefetch=1, grid=(rows // tr,),
            in_specs=[pl.BlockSpec((1, tr, cols), lambda r, chip_ref: (chip_ref[0], r, 0)),
                      pl.BlockSpec((2, tr, cols), lambda r, chip_ref: (0, r, 0)),
                      tile, tile, tile],
            out_specs=(tile, tile, tile, tile)),
        compiler_params=_cparams("arbitrary"),
    )(chip, sums, from_chips, w, m, v)


def _adamw_small(g, w, m, v):
    def body(g_ref, w_ref, m_ref, v_ref, d_out, m_out, v_out):
        delta, m_new, v_new = _adamw(w_ref[...], g_ref[...], m_ref[...], v_ref[...])
        d_out[...] = delta
        m_out[...] = m_new
        v_out[...] = v_new

    vmem = pl.BlockSpec(memory_space=pltpu.VMEM)
    shp = jax.ShapeDtypeStruct(g.shape, F32)
    return pl.pallas_call(body, name="adamw_small", out_shape=(shp, shp, shp),
                          in_specs=[vmem] * 4, out_specs=(vmem, vmem, vmem))(g, w, m, v)


TILE_ROWS = (0, 1024, NAT_ZA, NAT_B, NAT_ZC, NAT_C, NAT_C + CONV_W)


TILE_ORDER = ((0, 1, 2, 3, 5, 6, 4), (2, 1, 0, 4, 3, 5, 6), (5, 6, 0, 4, 1, 2, 3), (4, 6, 2, 3, 5, 0, 1))
EARLY_SWEEP, NEIGHBOUR_SWEEP, DIAGONAL_SWEEP = 1, 2, 4
PIECES, W_IN_PIECES, OTHER_PIECES = 4, (0, 1), (2, 3)


def _gather_inproj(x2d, norm_g, shard_t, w_out_s, small_s, order, tm):
    seq = x2d.shape[0]
    tn = CONV_W
    ni, nj = seq // tm, MAIN_W // tn
    first_sweep = lambda j, i, order_ref: jnp.where(j == 0, i, ni - 1)
    last_sweep = lambda j, i, order_ref: jnp.where(j == nj - 1, i, 0)
    edge_tiles = _edge_tiles()

    def body(order_ref, x_ref, g_ref, shard_ref, wout_ref, sm_ref, proj_ref, lr_ref, ht_ref, w_nat, wout_all, sm_all,
             w_all, h_all, edges, stage, wout_b, sm_b, send_sems, recv_sems, local_sems):
        j, i = pl.program_id(0), pl.program_id(1)
        rows = pl.ds(pl.multiple_of(i * tm, tm), tm)
        x, y, c = _position()
        me, here, sibling = _blk(x, y, c), (x, y, c), (x, y, 1 - c)
        chips = _route_chips()
        sibling_chips = [chips[1], chips[0], chips[2]]

        def pieces(px, py, pc):
            blk = _blk(px, py, pc)
            body_rows = pl.ds(pl.multiple_of(_first_tile_row(blk, px) + EDGE, EDGE), BODY_ROWS)
            return [w_all.at[body_rows], edges.at[blk], wout_all.at[blk], sm_all.at[blk]]

        def copy(a, k, block, to, staged=None):
            ref = pieces(*block)[a]
            return pltpu.make_async_remote_copy(src_ref=ref if staged is None else staged, dst_ref=ref,
                                                send_sem=send_sems.at[a * 7 + k], recv_sem=recv_sems.at[a * 7 + k],
                                                device_id=to, device_id_type=MESH)

        def own_copies(group, slots=(0, 1, 2)):
            targets = [sibling] + [(*chips[n], c) for n in range(2)]
            staged = [None, None, wout_b, sm_b]
            return [copy(a, k, here, targets[k], staged[a]) for k in slots for a in group]

        def relays(group):
            return [copy(a, 3, (*chips[0], c), (*chips[1], c)) for a in group]

        def forwards(n, group):
            return [copy(a, 4 + n, (*chips[n], c), sibling) for a in group]

        def keep_own():
            return [pltpu.make_async_copy(wout_b, wout_all.at[me], local_sems.at[0]),
                    pltpu.make_async_copy(sm_b, sm_all.at[me], local_sems.at[1])]

        def keep_weight():
            return pltpu.make_async_copy(w_all, w_nat, local_sems.at[2])

        def take(ns, group, relay=True):
            for n in ns:
                for a in group:
                    copy(a, 1 + n, (*chips[n], c), here).wait_recv()
                _start_all((relays(group) if n == 0 and relay else []) + forwards(n, group))

        def take_passed_on(ns, group):
            for n in ns:
                for a in group:
                    copy(a, 4 + n, (*sibling_chips[n], 1 - c), here).wait_recv()

        def arrive(ns, group):
            take(ns, group)
            take_passed_on(ns, group)

        def per_core_and_row(step):
            for core in range(2):
                for row in range(2):
                    pl.when(jnp.logical_and(c == core, y == row))(functools.partial(step, core, row))

        def start_own(core, row):
            now = (0, 1 + core) if core == row else (0, 2 - core, 1 + core)
            _start_all(own_copies(W_IN_PIECES, now))
            wout_b[...] = wout_ref[...].astype(BF16)
            sm_b[...] = sm_ref[...]
            _start_all(own_copies(OTHER_PIECES, now) + keep_own())

        def take_early(core, row):
            if core == row:
                _start_all(own_copies(W_IN_PIECES, (2 - core,)) + own_copies(OTHER_PIECES, (2 - core,)))
                take((1 - core,), W_IN_PIECES, relay=False)
            else:
                take_passed_on((core,), W_IN_PIECES)

        def take_neighbours(core, row):
            if core == row:
                _start_all(relays(W_IN_PIECES) if core == 1 else [])
                take((core,), W_IN_PIECES)
                take_passed_on((0, 1), W_IN_PIECES)
            else:
                take((0, 1), W_IN_PIECES)
                take_passed_on((1 - core,), W_IN_PIECES)

        early_blk = _blk(x, 1 - y, y)

        def add_edge_tiles(stage):
            for row, parts in edge_tiles.items():
                ready = 0
                for blk, _ in parts:
                    away = (x != blk // 4).astype(jnp.int32) + (y != (blk // 2) % 2).astype(jnp.int32)
                    late = jnp.where(away == 1, jnp.where(early_blk == blk, 1, 2), jnp.where(away == 2, 3 + blk % 2, 0))
                    ready = jnp.maximum(ready, late)

                @pl.when(ready == stage)
                def _(row=row, parts=parts):
                    tile = edges[parts[0][0], parts[0][1]].astype(F32)
                    for blk, side in parts[1:]:
                        tile = tile + edges[blk, side].astype(F32)
                    w_all[row:row + EDGE, :] = tile.astype(BF16)

        @pl.when(jnp.logical_and(j == 0, i == 0))
        def _():
            last = SHARD_W // 8 * 8
            for col in range(0, D_MODEL, 128):
                cols = slice(col, col + 128)
                stage[0:last, :] = shard_ref[0:last, cols]
                stage[last:, :] = jnp.zeros((SHIFTED_ROWS - last, 128), F32)
                stage[last:SHARD_W, :] = shard_ref[last:SHARD_W, cols]
                for k in range(EDGE // 4):
                    @pl.when(me % 4 == k)
                    def _(k=k, cols=cols):
                        moved = pltpu.roll(stage[...], 4 * k, 0) if k else stage[...]
                        pieces(*here)[0][:, cols] = moved[EDGE:EDGE + BODY_ROWS].astype(BF16)
                        edges[me, 0, :, cols] = moved[0:EDGE].astype(BF16)
                        edges[me, 1, :, cols] = moved[EDGE + BODY_ROWS:].astype(BF16)
            per_core_and_row(start_own)
            for a in W_IN_PIECES:
                copy(a, 0, sibling, here).wait_recv()
            add_edge_tiles(0)

        @pl.when(jnp.logical_and(j == EARLY_SWEEP, i == 0))
        def _():
            per_core_and_row(take_early)
            add_edge_tiles(1)

        @pl.when(jnp.logical_and(j == NEIGHBOUR_SWEEP, i == 0))
        def _():
            per_core_and_row(take_neighbours)
            add_edge_tiles(2)

        for core in range(2):
            @pl.when(jnp.logical_and(j == DIAGONAL_SWEEP + core, i == 0))
            def _(core=core):
                pl.when(c == core)(lambda: take((2,), W_IN_PIECES))
                pl.when(c != core)(lambda: take_passed_on((2,), W_IN_PIECES))
                add_edge_tiles(3 + core)
                if core == 0:
                    arrive((0, 1), OTHER_PIECES)
                else:
                    keep_weight().start()

        @pl.when(jnp.logical_and(j == nj - 1, i == 0))
        def _():
            arrive((2,), OTHER_PIECES)

        @pl.when(j == 0)
        def _():
            xv = x_ref[...]
            r = lax.rsqrt(jnp.mean(xv * xv, axis=-1, keepdims=True) + EPS)
            h = (xv * r) * g_ref[...]
            h_all[rows, :] = h.astype(BF16)
            ht_ref[...] = h.T.astype(BF16)

        tile = order_ref[j]
        row = 0
        for k, start in enumerate(TILE_ROWS):
            row = row + jnp.where(tile == k, start // 32, 0)
        w_tile = w_all[pl.ds(pl.multiple_of(row * 32, 32), tn), :]
        proj_ref[...] = _dot_nt(h_all[rows, :], w_tile).astype(BF16)

        @pl.when(j == nj - 1)
        def _():
            lr_ref[...] = _dot_nt(h_all[rows, :], w_all[NAT_LR:NAT_LR + LR_W, :])

        @pl.when(jnp.logical_and(j == nj - 1, i == ni - 1))
        def _():
            everything = range(PIECES)
            passed_on = [cp for n in range(3) for cp in forwards(n, everything)]
            for cp in own_copies(everything) + relays(everything) + passed_on:
                cp.wait_send()
            for a in OTHER_PIECES:
                copy(a, 0, sibling, here).wait_recv()
            for cp in keep_own() + [keep_weight()]:
                cp.wait()

    const = lambda shape: pl.BlockSpec(shape, lambda j, i, order_ref: (0,) * len(shape))
    hbm = pl.BlockSpec(memory_space=pl.ANY)
    vmem = pl.BlockSpec(memory_space=pltpu.VMEM)
    return pl.pallas_call(
        body, name="gather_inproj",
        out_shape=(jax.ShapeDtypeStruct((seq, MAIN_W), BF16), jax.ShapeDtypeStruct((seq, LR_W), F32),
                   jax.ShapeDtypeStruct((D_MODEL, seq), BF16), jax.ShapeDtypeStruct((IN_W, D_MODEL), BF16),
                   jax.ShapeDtypeStruct((N_DEV,) + w_out_s.shape, BF16),
                   jax.ShapeDtypeStruct((N_DEV,) + small_s.shape, F32)),
        grid_spec=pltpu.PrefetchScalarGridSpec(
            num_scalar_prefetch=1, grid=(nj, ni),
            in_specs=[pl.BlockSpec((tm, D_MODEL), lambda j, i, order_ref: (first_sweep(j, i, order_ref), 0)),
                      const((1, D_MODEL)), vmem, vmem, const(small_s.shape)],
            out_specs=(pl.BlockSpec((tm, tn), lambda j, i, order_ref: (i, order_ref[j])),
                       pl.BlockSpec((tm, LR_W), lambda j, i, order_ref: (last_sweep(j, i, order_ref), 0)),
                       pl.BlockSpec((D_MODEL, tm), lambda j, i, order_ref: (0, first_sweep(j, i, order_ref))),
                       hbm, hbm, hbm),
            scratch_shapes=[pltpu.VMEM((IN_W, D_MODEL), BF16), pltpu.VMEM((seq, D_MODEL), BF16),
                            pltpu.VMEM((N_DEV, 2, EDGE, D_MODEL), BF16), pltpu.VMEM((SHIFTED_ROWS, 128), F32),
                            pltpu.VMEM(w_out_s.shape, BF16), pltpu.VMEM(small_s.shape, F32),
                            pltpu.SemaphoreType.DMA((7 * PIECES,)), pltpu.SemaphoreType.DMA((7 * PIECES,)),
                            pltpu.SemaphoreType.DMA((3,))]),
        compiler_params=_cparams("arbitrary", "arbitrary"),
    )(order, x2d, norm_g, shard_t, w_out_s, small_s)


def _block_masks(tt):
    row = lax.broadcasted_iota(jnp.int32, (tt, tt), 0)
    col = lax.broadcasted_iota(jnp.int32, (tt, tt), 1)
    same = jnp.right_shift(row, 6) == jnp.right_shift(col, 6)
    return (jnp.logical_and(same, col <= row), jnp.logical_and(same, col >= row), jnp.logical_and(same, col > row))


def _dot_split3(ones_mat, x):
    x1 = x.astype(BF16)
    r1 = x - x1.astype(F32)
    x2 = r1.astype(BF16)
    x3 = (r1 - x2.astype(F32)).astype(BF16)
    return (_dot(ones_mat, x3) + _dot(ones_mat, x2)) + _dot(ones_mat, x1)


def _log_gate(logits):
    return (jnp.minimum(logits, 0.0) - jnp.log(1.0 + jnp.exp(-jnp.abs(logits)))) * GATE_SCALE


def _chunk_column_mask(tt):
    nc = tt // CHUNK
    row = lax.broadcasted_iota(jnp.int32, (tt, nc * DK), 0)
    col = lax.broadcasted_iota(jnp.int32, (tt, nc * DK), 1)
    return jnp.right_shift(row, 6) == jnp.right_shift(col, 7)


def _chunked(mask, x, nc):
    wide = jnp.concatenate([x] * nc, axis=1)
    return jnp.where(mask, wide, jnp.zeros_like(wide))


def _gla_fwd(proj, lr, wgk_f, wgk_b, bgk_f, bgk_b, tt):
    seq = proj.shape[0]
    nb, nc, nch = seq // tt, tt // CHUNK, seq // CHUNK

    def body(qf, kf, vf, lrf, qb, kb, vb, lrb, wf, wb, bf, bb, of, ob, stf, stb, s_scr, qs_s, ks_s, qin_s, kout_s):
        @pl.when(pl.program_id(0) == 0)
        def _():
            s_scr[...] = jnp.zeros(s_scr.shape, F32)

        low, upp, sup = _block_masks(tt)
        dirs = ((qf, kf, vf, lrf, wf, bf, of, stf, low, low, REF_F, LAST_F, list(range(nc))),
                (qb, kb, vb, lrb, wb, bb, ob, stb, upp, sup, REF_B, LAST_B, list(reversed(range(nc)))))
        for d, (q_r, k_r, v_r, lr_r, w_r, b_r, o_r, st_r, cum, mask, ref, last, order) in enumerate(dirs):
            logits = _dot(lr_r[...].astype(BF16), w_r[...]) + b_r[...]
            b = _dot_split3(cum.astype(BF16), _log_gate(logits))
            decs = []
            for c in range(nc):
                rows = slice(c * CHUNK, (c + 1) * CHUNK)
                bc = b[rows]
                b_ref, b_last = bc[ref:ref + 1], bc[last:last + 1]
                qc = q_r[rows, :].astype(F32) * QSCALE
                kc = k_r[rows, :].astype(F32)
                qs_s[rows, :] = (qc * jnp.exp(bc - b_ref)).astype(BF16)
                ks_s[rows, :] = (kc * jnp.exp(b_ref - bc)).astype(BF16)
                qin_s[rows, :] = (qc * jnp.exp(bc)).astype(BF16)
                kout_s[rows, :] = (kc * jnp.exp(b_last - bc)).astype(BF16)
                decs.append(jnp.exp(b_last))
            for h in range(HEADS):
                ksl = slice(h * DK, (h + 1) * DK)
                vsl = slice(h * DV, (h + 1) * DV)
                v = v_r[:, vsl].astype(BF16)
                att = jnp.where(mask, _dot_nt(qs_s[:, ksl], ks_s[:, ksl]), 0.0).astype(BF16)
                o_intra = _dot(att, v)
                st = s_scr[d * HEADS + h]
                for c in order:
                    rows = slice(c * CHUNK, (c + 1) * CHUNK)
                    stb = st.astype(BF16)
                    st_r[c, h] = stb
                    o_r[rows, vsl] = (o_intra[rows] + _dot_nt(qin_s[rows, ksl], stb)).astype(BF16)
                    st = st * decs[c][:, ksl] + _dot_tn(v[rows], kout_s[rows, ksl])
                s_scr[d * HEADS + h] = st

    fw = lambda i: (i, 0)
    bw = lambda i: (nb - 1 - i, 0)
    const = lambda i: (0, 0)

    def tok_specs(m):
        return [pl.BlockSpec((tt, QK_W), lambda i: (m(i)[0], OFF_Q // QK_W)),
                pl.BlockSpec((tt, QK_W), lambda i: (m(i)[0], OFF_K // QK_W)),
                pl.BlockSpec((tt, V_W), lambda i: (m(i)[0], OFF_V // V_W)),
                pl.BlockSpec((tt, LR_W), m)]

    st_shape = jax.ShapeDtypeStruct((nch, HEADS, DV, DK), BF16)
    o_shape = jax.ShapeDtypeStruct((seq, V_W), BF16)
    operand = pltpu.VMEM((tt, QK_W), BF16)
    return pl.pallas_call(
        body, name="gla_fwd",
        out_shape=(o_shape, o_shape, st_shape, st_shape),
        grid=(nb,),
        in_specs=tok_specs(fw) + tok_specs(bw) + [
            pl.BlockSpec((LR_W, QK_W), const), pl.BlockSpec((LR_W, QK_W), const),
            pl.BlockSpec((1, QK_W), const), pl.BlockSpec((1, QK_W), const)],
        out_specs=(pl.BlockSpec((tt, V_W), fw), pl.BlockSpec((tt, V_W), bw),
                   pl.BlockSpec((nc, HEADS, DV, DK), lambda i: (i, 0, 0, 0)),
                   pl.BlockSpec((nc, HEADS, DV, DK), lambda i: (nb - 1 - i, 0, 0, 0))),
        scratch_shapes=[pltpu.VMEM((2 * HEADS, DV, DK), F32), operand, operand, operand, operand],
        compiler_params=_cparams("arbitrary"),
    )(proj, proj, proj, lr, proj, proj, proj, lr, wgk_f, wgk_b, bgk_f, bgk_b)


def _head_norm(o, gain):
    outs, rinv = [], []
    for h in range(HEADS):
        oh = o[:, h * DV:(h + 1) * DV]
        r = lax.rsqrt(jnp.mean(oh * oh, axis=-1, keepdims=True) + EPS)
        outs.append((oh * r) * gain)
        rinv.append(r)
    return jnp.concatenate(outs, axis=1), rinv


def _shift_rows(u, prev_row, next_row):
    n = u.shape[0]
    row = lax.broadcasted_iota(jnp.int32, (n, 1), 0)
    up = jnp.where(row == 0, prev_row, pltpu.roll(u, 1, 0))
    un = jnp.where(row == n - 1, next_row, pltpu.roll(u, n - 1, 0))
    return up, un


HALO = 16


def _halo_specs(tm, seq, col_block):
    per = tm // HALO
    last = seq // HALO - 1
    return [pl.BlockSpec((HALO, CONV_W), lambda i: (jnp.maximum(i * per - 1, 0), col_block)),
            pl.BlockSpec((HALO, CONV_W), lambda i: (jnp.minimum((i + 1) * per, last), col_block))]


def _f32(ref):
    return ref[...].astype(F32)


def _last_row(ref):
    return ref[HALO - 1:HALO, :].astype(F32)


def _first_row(ref):
    return ref[0:1, :].astype(F32)


def _mix_out_loss(o_f, o_b, proj, x2d, tgt, gla_g, conv_w, conv_b, w_out, final_g, tm):
    seq = x2d.shape[0]
    nt = seq // tm

    def body(of, ob, za, bg, cg, hc, zc, cprev, cnext, hprev, hnext, x_ref, t_ref, gg, cw, cb, wo, fg,
             yt_ref, conv_ref, dx2_ref, dx2b_ref, loss_ref, dfg_ref):
        i = pl.program_id(0)

        @pl.when(i == 0)
        def _():
            loss_ref[...] = jnp.zeros(loss_ref.shape, F32)
            dfg_ref[...] = jnp.zeros(dfg_ref.shape, F32)

        on, _ = _head_norm(_f32(of) + _f32(ob), gg[...])
        zav = _f32(za)
        y_a = on * (zav * _sigmoid(zav))
        u = _f32(cg) * _f32(hc)
        prev_row = jnp.where(i > 0, _last_row(cprev) * _last_row(hprev), 0.0)
        next_row = jnp.where(i < nt - 1, _first_row(cnext) * _first_row(hnext), 0.0)
        up, un = _shift_rows(u, prev_row, next_row)
        conv = (cw[0:1, :] * up + cw[1:2, :] * u + cw[2:3, :] * un) + cb[...]
        conv_ref[...] = conv.astype(BF16)
        zcv = _f32(zc)
        y_c = _f32(bg) * conv * (zcv * _sigmoid(zcv))
        y = jnp.concatenate([y_a, y_c], axis=1)
        yt_ref[...] = y.T.astype(BF16)
        x2 = x_ref[...] + _dot(y.astype(BF16), wo[...])
        r = lax.rsqrt(jnp.mean(x2 * x2, axis=-1, keepdims=True) + EPS)
        xn = x2 * r
        err = xn * fg[...] - t_ref[...]
        loss_ref[...] += 0.5 * jnp.sum(jnp.mean(err * err, axis=-1, keepdims=True))
        dyf = err * (1.0 / D_MODEL)
        dfg_ref[...] += jnp.sum(dyf * xn, axis=0, keepdims=True)
        dxn = dyf * fg[...]
        dx2 = r * dxn - xn * (r * jnp.mean(dxn * xn, axis=-1, keepdims=True))
        dx2_ref[...] = dx2
        dx2b_ref[...] = dx2.astype(BF16)

    def col(off):
        return pl.BlockSpec((tm, CONV_W), lambda i: (i, off // CONV_W))

    rowt = pl.BlockSpec((tm, D_MODEL), lambda i: (i, 0))
    const = lambda shape: pl.BlockSpec(shape, lambda i: (0, 0))
    return pl.pallas_call(
        body, name="mix_out_loss",
        out_shape=(jax.ShapeDtypeStruct((MIX_W, seq), BF16), jax.ShapeDtypeStruct((seq, CONV_W), BF16),
                   jax.ShapeDtypeStruct((seq, D_MODEL), F32), jax.ShapeDtypeStruct((seq, D_MODEL), BF16),
                   jax.ShapeDtypeStruct((8, 128), F32), jax.ShapeDtypeStruct((1, D_MODEL), F32)),
        grid=(nt,),
        in_specs=[rowt, rowt, col(OFF_ZA), col(OFF_B), col(OFF_C), col(OFF_H), col(OFF_ZC)]
        + _halo_specs(tm, seq, OFF_C // CONV_W) + _halo_specs(tm, seq, OFF_H // CONV_W)
        + [rowt, rowt, const((1, DV)), const((8, CONV_W)), const((1, CONV_W)), const((MIX_W, D_MODEL)),
           const((1, D_MODEL))],
        out_specs=(pl.BlockSpec((MIX_W, tm), lambda i: (0, i)), rowt, rowt, rowt, const((8, 128)),
                   const((1, D_MODEL))),
        compiler_params=_cparams("arbitrary"),
    )(o_f, o_b, proj, proj, proj, proj, proj, proj, proj, proj, proj, x2d, tgt, gla_g, conv_w, conv_b, w_out, final_g)


def _dsilu(z, s):
    return s * (1.0 + z * (1.0 - s))


def _mix_bwd(dx2b, o_f, o_b, proj, conv, gla_g, w_out, tm):
    seq = dx2b.shape[0]

    def body(dx, of, ob, za, bg, zc, cv, gg, wo, dg_ref, do_ref, dconv_ref, dgg_ref, dcb_ref):
        @pl.when(pl.program_id(0) == 0)
        def _():
            dgg_ref[...] = jnp.zeros(dgg_ref.shape, F32)
            dcb_ref[...] = jnp.zeros(dcb_ref.shape, F32)

        dy = _dot_nt(dx[...], wo[...])
        dy_a, dy_c = dy[:, :V_W], dy[:, V_W:]
        zcv, bgv, convv = _f32(zc), _f32(bg), _f32(cv)
        sc = _sigmoid(zcv)
        szc = zcv * sc
        dg_ref[:, CONV_W:2 * CONV_W] = (dy_c * convv * szc).astype(BF16)
        dconv = dy_c * bgv * szc
        dconv_ref[...] = dconv.astype(BF16)
        dcb_ref[...] += jnp.sum(dconv, axis=0, keepdims=True)
        dg_ref[:, 2 * CONV_W:] = (dy_c * bgv * convv * _dsilu(zcv, sc)).astype(BF16)

        o = _f32(of) + _f32(ob)
        gain = gg[...]
        on, rinv = _head_norm(o, gain)
        zav = _f32(za)
        sa = _sigmoid(zav)
        dg_ref[:, :CONV_W] = (dy_a * on * _dsilu(zav, sa)).astype(BF16)
        don = dy_a * (zav * sa)
        dgg = jnp.zeros((1, DV), F32)
        dos = []
        for h in range(HEADS):
            sl = slice(h * DV, (h + 1) * DV)
            oh, r, dh = o[:, sl], rinv[h], don[:, sl]
            ohn = oh * r
            dgg = dgg + jnp.sum(dh * ohn, axis=0, keepdims=True)
            dn = dh * gain
            dos.append(r * dn - ohn * (r * jnp.mean(dn * ohn, axis=-1, keepdims=True)))
        dgg_ref[...] += dgg
        do_ref[...] = jnp.concatenate(dos, axis=1).astype(BF16)

    def col(off):
        return pl.BlockSpec((tm, CONV_W), lambda i: (i, off // CONV_W))

    rowt = pl.BlockSpec((tm, D_MODEL), lambda i: (i, 0))
    const = lambda shape: pl.BlockSpec(shape, lambda i: (0, 0))
    return pl.pallas_call(
        body, name="mix_bwd",
        out_shape=(jax.ShapeDtypeStruct((seq, GATES_W), BF16), jax.ShapeDtypeStruct((seq, V_W), BF16),
                   jax.ShapeDtypeStruct((seq, CONV_W), BF16),
                   jax.ShapeDtypeStruct((1, DV), F32), jax.ShapeDtypeStruct((1, CONV_W), F32)),
        grid=(seq // tm,),
        in_specs=[rowt, rowt, rowt, col(OFF_ZA), col(OFF_B), col(OFF_ZC), rowt, const((1, DV)),
                  const((MIX_W, D_MODEL))],
        out_specs=(pl.BlockSpec((tm, GATES_W), lambda i: (i, 0)), rowt, rowt, const((1, DV)), const((1, CONV_W))),
        compiler_params=_cparams("arbitrary"),
    )(dx2b, o_f, o_b, proj, proj, proj, conv, gla_g, w_out)


def _conv_bwd(dconv, proj, conv_w, tm):
    seq = dconv.shape[0]
    nt = seq // tm

    def body(dc_in, dprev, dnext, cg, hc, cprev, cnext, hprev, hnext, cw, dch_ref, dcw_ref):
        i = pl.program_id(0)

        @pl.when(i == 0)
        def _():
            dcw_ref[...] = jnp.zeros(dcw_ref.shape, F32)

        first, lastt = i > 0, i < nt - 1
        dcv = _f32(dc_in)
        d_up, d_un = _shift_rows(dcv, jnp.where(first, _last_row(dprev), 0.0), jnp.where(lastt, _first_row(dnext), 0.0))
        cgv, hcv = _f32(cg), _f32(hc)
        u = cgv * hcv
        u_up, u_un = _shift_rows(u, jnp.where(first, _last_row(cprev) * _last_row(hprev), 0.0),
                                 jnp.where(lastt, _first_row(cnext) * _first_row(hnext), 0.0))
        du = cw[0:1, :] * d_un + cw[1:2, :] * dcv + cw[2:3, :] * d_up
        dch_ref[:, :CONV_W] = (du * hcv).astype(BF16)
        dch_ref[:, CONV_W:] = (du * cgv).astype(BF16)
        dcw_ref[0:1, :] += jnp.sum(dcv * u_up, axis=0, keepdims=True)
        dcw_ref[1:2, :] += jnp.sum(dcv * u, axis=0, keepdims=True)
        dcw_ref[2:3, :] += jnp.sum(dcv * u_un, axis=0, keepdims=True)

    def col(off):
        return pl.BlockSpec((tm, CONV_W), lambda i: (i, off // CONV_W))

    rowt = pl.BlockSpec((tm, CONV_W), lambda i: (i, 0))
    const = lambda shape: pl.BlockSpec(shape, lambda i: (0, 0))
    return pl.pallas_call(
        body, name="conv_bwd",
        out_shape=(jax.ShapeDtypeStruct((seq, CH_W), BF16), jax.ShapeDtypeStruct((8, CONV_W), F32)),
        grid=(nt,),
        in_specs=[rowt] + _halo_specs(tm, seq, 0) + [col(OFF_C), col(OFF_H)]
        + _halo_specs(tm, seq, OFF_C // CONV_W) + _halo_specs(tm, seq, OFF_H // CONV_W) + [const((8, CONV_W))],
        out_specs=(pl.BlockSpec((tm, CH_W), lambda i: (i, 0)), const((8, CONV_W))),
        compiler_params=_cparams("arbitrary"),
    )(dconv, dconv, dconv, proj, proj, proj, proj, proj, proj, conv_w)


def _gla_bwd(proj, lr, do, st_f, st_b, wgk_f, wgk_b, bgk_f, bgk_b, tt):
    seq = proj.shape[0]
    nb, nc = seq // tt, tt // CHUNK

    def body(qf, kf, vf, lrf, dof, stf, qb, kb, vb, lrb, dob, stb, wf, wb, bf, bb,
             dqkv_f, dlr_f, dqkv_b, dlr_b, dwf, dwb, dbf, dbb,
             ds_scr, eq_s, ek_s, ein_s, eout_s, qs_s, ks_s, qin_s, kout_s, db_s, lg_s):
        @pl.when(pl.program_id(0) == 0)
        def _():
            ds_scr[...] = jnp.zeros(ds_scr.shape, F32)
            for r in (dwf, dwb, dbf, dbb):
                r[...] = jnp.zeros(r.shape, F32)

        low, upp, sup = _block_masks(tt)
        row = lax.broadcasted_iota(jnp.int32, (CHUNK, 1), 0)
        kmask = _chunk_column_mask(tt)
        dirs = ((qf, kf, vf, lrf, dof, stf, wf, bf, dqkv_f, dlr_f, dwf, dbf,
                 low, upp, low, REF_F, LAST_F, list(reversed(range(nc)))),
                (qb, kb, vb, lrb, dob, stb, wb, bb, dqkv_b, dlr_b, dwb, dbb,
                 upp, low, sup, REF_B, LAST_B, list(range(nc))))
        for d, (q_r, k_r, v_r, lr_r, do_r, st_r, w_r, b_r, dqkv_r, dlr_r, dw_r, db_r,
                cum, cum_t, mask, ref, last, order) in enumerate(dirs):
            lrv = lr_r[...].astype(BF16)
            wv = w_r[...]
            logits = _dot(lrv, wv) + b_r[...]
            lg_s[...] = logits
            b = _dot_split3(cum.astype(BF16), _log_gate(logits))
            decs = []
            for c in range(nc):
                rows = slice(c * CHUNK, (c + 1) * CHUNK)
                bc = b[rows]
                b_ref, b_last = bc[ref:ref + 1], bc[last:last + 1]
                qc = q_r[rows, :].astype(F32) * QSCALE
                kc = k_r[rows, :].astype(F32)
                e_q, e_k, e_in, e_out = jnp.exp(bc - b_ref), jnp.exp(b_ref - bc), jnp.exp(bc), jnp.exp(b_last - bc)
                eq_s[rows, :], ek_s[rows, :], ein_s[rows, :], eout_s[rows, :] = e_q, e_k, e_in, e_out
                qs_s[rows, :] = (qc * e_q).astype(BF16)
                ks_s[rows, :] = (kc * e_k).astype(BF16)
                qin_s[rows, :] = (qc * e_in).astype(BF16)
                kout_s[rows, :] = (kc * e_out).astype(BF16)
                decs.append(jnp.exp(b_last))
            for h in range(HEADS):
                ksl = slice(h * DK, (h + 1) * DK)
                vsl = slice(h * DV, (h + 1) * DV)
                v = v_r[:, vsl].astype(BF16)
                dov = do_r[:, vsl].astype(BF16)
                qsb, ksb = qs_s[:, ksl], ks_s[:, ksl]
                att = jnp.where(mask, _dot_nt(qsb, ksb), 0.0).astype(BF16)
                datt = jnp.where(mask, _dot_nt(dov, v), 0.0).astype(BF16)
                dqs = _dot(datt, ksb)
                dks = _dot_tn(datt, qsb)
                dv_intra = _dot_tn(att, dov)
                g_t = _dot_tn(dov, _chunked(kmask, qin_s[:, ksl], nc))
                ds = ds_scr[d * HEADS + h]
                for c in order:
                    rows = slice(c * CHUNK, (c + 1) * CHUNK)
                    dsb = ds.astype(BF16)
                    s_prev = st_r[c, h]
                    dk_out = _dot(v[rows], dsb)
                    dq_in = _dot(dov[rows], s_prev)
                    dv = dv_intra[rows] + _dot_nt(kout_s[rows, ksl], dsb)
                    dqkv_r[rows, OFF_V + h * DV:OFF_V + (h + 1) * DV] = dv.astype(BF16)
                    dec = decs[c][:, ksl]
                    ddec = jnp.sum(ds * s_prev.astype(F32), axis=0, keepdims=True)
                    e_out = eout_s[rows, ksl]
                    qc = q_r[rows, ksl].astype(F32) * QSCALE
                    kc = k_r[rows, ksl].astype(F32)
                    dq = dqs[rows] * eq_s[rows, ksl] + dq_in * ein_s[rows, ksl]
                    dk = dks[rows] * ek_s[rows, ksl] + dk_out * e_out
                    dqkv_r[rows, OFF_Q + h * DK:OFF_Q + (h + 1) * DK] = (dq * QSCALE).astype(BF16)
                    dqkv_r[rows, OFF_K + h * DK:OFF_K + (h + 1) * DK] = dk.astype(BF16)
                    tail = jnp.sum(dk_out * (kc * e_out), axis=0, keepdims=True) + ddec * dec
                    db_s[rows, ksl] = (qc * dq - kc * dk) + jnp.where(row == last, tail, 0.0)
                    ds = ds * dec + g_t[:, c * DK:(c + 1) * DK]
                ds_scr[d * HEADS + h] = ds
            dg = _dot_split3(cum_t.astype(BF16), db_s[...])
            dlogit = (dg * GATE_SCALE) * _sigmoid(-lg_s[...])
            dlb = dlogit.astype(BF16)
            dlr_r[...] = _dot_nt(dlb, wv)
            dw_r[...] += _dot_tn(lrv, dlb)
            db_r[...] += jnp.sum(dlogit, axis=0, keepdims=True)

    fw = lambda i: (nb - 1 - i, 0)
    bw = lambda i: (i, 0)
    const = lambda i: (0, 0)

    def tok_specs(m):
        return [pl.BlockSpec((tt, QK_W), lambda i: (m(i)[0], OFF_Q // QK_W)),
                pl.BlockSpec((tt, QK_W), lambda i: (m(i)[0], OFF_K // QK_W)),
                pl.BlockSpec((tt, V_W), lambda i: (m(i)[0], OFF_V // V_W)),
                pl.BlockSpec((tt, LR_W), m),
                pl.BlockSpec((tt, V_W), m),
                pl.BlockSpec((nc, HEADS, DV, DK), lambda i: (m(i)[0], 0, 0, 0))]

    dqkv = jax.ShapeDtypeStruct((seq, QK_W + QK_W + V_W), BF16)
    dlr = jax.ShapeDtypeStruct((seq, LR_W), F32)
    dw = jax.ShapeDtypeStruct((LR_W, QK_W), F32)
    dbias = jax.ShapeDtypeStruct((1, QK_W), F32)
    return pl.pallas_call(
        body, name="gla_bwd",
        out_shape=(dqkv, dlr, dqkv, dlr, dw, dw, dbias, dbias),
        grid=(nb,),
        in_specs=tok_specs(fw) + tok_specs(bw) + [
            pl.BlockSpec((LR_W, QK_W), const), pl.BlockSpec((LR_W, QK_W), const),
            pl.BlockSpec((1, QK_W), const), pl.BlockSpec((1, QK_W), const)],
        out_specs=(pl.BlockSpec((tt, QK_W + QK_W + V_W), fw), pl.BlockSpec((tt, LR_W), fw),
                   pl.BlockSpec((tt, QK_W + QK_W + V_W), bw), pl.BlockSpec((tt, LR_W), bw),
                   pl.BlockSpec((LR_W, QK_W), const), pl.BlockSpec((LR_W, QK_W), const),
                   pl.BlockSpec((1, QK_W), const), pl.BlockSpec((1, QK_W), const)),
        scratch_shapes=[pltpu.VMEM((2 * HEADS, DV, DK), F32)] + [pltpu.VMEM((tt, QK_W), F32)] * 4
        + [pltpu.VMEM((tt, QK_W), BF16)] * 4 + [pltpu.VMEM((tt, QK_W), F32)] * 2,
        compiler_params=_cparams("arbitrary"),
    )(proj, proj, proj, lr, do, st_f, proj, proj, proj, lr, do, st_b, wgk_f, wgk_b, bgk_f, bgk_b)


def _both_directions(f_ref, b_ref):
    return (_f32(f_ref) + _f32(b_ref)).astype(BF16)


def _input_grad(dqkv_f, dqkv_b, dp_gates, dp_ch, dlr_f, dlr_b, w_nat, x2d, norm_g, dx2, sums, tm):
    seq = x2d.shape[0]
    nt, n = seq // tm, len(sums)
    relay_step = (3 * nt) // 8

    def body(dqf, dqb, dg, dc, dlf, dlb, w, x_ref, g_ref, dx2_ref, *rest):
        ins, (gx_ref, dng_ref), outs = rest[:n], rest[n:n + 2], rest[n + 2:2 * n + 2]
        passing, joined = rest[2 * n + 2:3 * n + 2], rest[3 * n + 2:4 * n + 2]
        send_sems, recv_sems, local_sems = rest[4 * n + 2:]
        i = pl.program_id(0)
        c = lax.axis_index("c")
        first, second, diagonal = _route_chips()
        slot = lambda chip: 2 * chip[0] + chip[1]

        def remote(a, k, src, dst, to):
            return pltpu.make_async_remote_copy(src_ref=src, dst_ref=dst, send_sem=send_sems.at[3 * a + k],
                                                recv_sem=recv_sems.at[3 * a + k], device_id=(*to, c),
                                                device_id_type=MESH)

        direct = lambda a: remote(a, 0, ins[a].at[slot(first)], outs[a].at[0], first)
        for_second = lambda a: remote(a, 1, ins[a].at[slot(diagonal)], passing[a], first)
        joint = lambda a: remote(a, 2, joined[a], outs[a].at[1], second)
        own = lambda a: pltpu.make_async_copy(ins[a].at[slot(second)], joined[a], local_sems.at[a])

        @pl.when(i == 0)
        def _():
            _start_all([for_second(a) for a in range(n)] + [own(a) for a in range(n)] + [direct(a) for a in range(n)])
            dng_ref[...] = jnp.zeros(dng_ref.shape, F32)

        @pl.when(i == relay_step)
        def _():
            for a in range(n):
                for_second(a).wait_recv()
                own(a).wait()
                joined[a][...] = (joined[a][...].astype(F32) + passing[a][...].astype(F32)).astype(BF16)
                joint(a).start()

        dh = (_dot((dlf[...] + dlb[...]).astype(BF16), w[NAT_LR:NAT_LR + LR_W, :])
              + _dot(_both_directions(dqf, dqb), w[0:NAT_ZA, :])
              + _dot(dg[:, 0:CONV_W], w[NAT_ZA:NAT_LR, :]) + _dot(dg[:, CONV_W:2 * CONV_W], w[NAT_B:NAT_C, :])
              + _dot(dg[:, 2 * CONV_W:], w[NAT_ZC:IN_W, :]) + _dot(dc[...], w[NAT_C:NAT_ZC, :]))
        xv = x_ref[...]
        r = lax.rsqrt(jnp.mean(xv * xv, axis=-1, keepdims=True) + EPS)
        xn = xv * r
        dng_ref[...] += jnp.sum(dh * xn, axis=0, keepdims=True)
        dn = dh * g_ref[...]
        gx_ref[...] = (r * dn - xn * (r * jnp.mean(dn * xn, axis=-1, keepdims=True))) + dx2_ref[...]

        @pl.when(i == nt - 1)
        def _():
            for a in range(n):
                direct(a).wait_recv()
                joint(a).wait_recv()
            for a in range(n):
                for cp in (direct(a), for_second(a), joint(a)):
                    cp.wait_send()

    rowt = pl.BlockSpec((tm, D_MODEL), lambda i: (i, 0))
    seg = lambda width: pl.BlockSpec((tm, width), lambda i: (i, 0))
    resident = lambda rows: pl.BlockSpec((rows, D_MODEL), lambda i: (0, 0), pipeline_mode=pl.Buffered(1))
    hbm = pl.BlockSpec(memory_space=pl.ANY)
    blocks = [pltpu.VMEM(s.shape[1:], s.dtype) for s in sums]
    return pl.pallas_call(
        body, name="input_grad",
        out_shape=(jax.ShapeDtypeStruct((seq, D_MODEL), F32), jax.ShapeDtypeStruct((1, D_MODEL), F32))
        + tuple(jax.ShapeDtypeStruct((2,) + s.shape[1:], s.dtype) for s in sums),
        grid=(nt,),
        in_specs=[seg(QKV_W), seg(QKV_W), seg(GATES_W), seg(CH_W), seg(LR_W), seg(LR_W), resident(IN_W),
                  rowt, pl.BlockSpec((1, D_MODEL), lambda i: (0, 0)), rowt] + [hbm] * n,
        out_specs=(rowt, pl.BlockSpec((1, D_MODEL), lambda i: (0, 0))) + (hbm,) * n,
        scratch_shapes=blocks + blocks + [pltpu.SemaphoreType.DMA((3 * n,)), pltpu.SemaphoreType.DMA((3 * n,)),
                                          pltpu.SemaphoreType.DMA((n,))],
        compiler_params=_cparams("arbitrary"),
    )(dqkv_f, dqkv_b, dp_gates, dp_ch, dlr_f, dlr_b, w_nat, x2d, norm_g, dx2, *sums)


def _weight_grad_out(y_t, dx2b, tk, riding):
    m, seq = y_t.shape
    n = dx2b.shape[1]
    nk = seq // tk

    def body(a_ref, b_ref, ride_in, o_ref, ride_out, send_sems, recv_sems):
        k = pl.program_id(0)

        @pl.when(k == 0)
        def _():
            _start_all(_sibling_copies(ride_in, ride_out, send_sems, recv_sems))
            o_ref[...] = jnp.zeros(o_ref.shape, F32)

        o_ref[...] += _dot(a_ref[...], b_ref[...])

        @pl.when(k == nk - 1)
        def _():
            _wait_all(_sibling_copies(ride_in, ride_out, send_sems, recv_sems))

    hbm = pl.BlockSpec(memory_space=pl.ANY)
    return pl.pallas_call(
        body, name="wgrad_out",
        out_shape=(jax.ShapeDtypeStruct((m, n), F32), jax.ShapeDtypeStruct((4,) + _block_shape(riding), F32)),
        grid=(nk,),
        in_specs=[pl.BlockSpec((m, tk), lambda k: (0, k)), pl.BlockSpec((tk, n), lambda k: (k, 0)), hbm],
        out_specs=(pl.BlockSpec((m, n), lambda k: (0, 0)), hbm),
        scratch_shapes=[pltpu.SemaphoreType.DMA((4,)), pltpu.SemaphoreType.DMA((4,))],
        compiler_params=_cparams("arbitrary"),
    )(y_t, dx2b, riding)


def _weight_grad_in(h_t, dqkv_f, dqkv_b, dp_gates, dp_ch, dlr_f, dlr_b):
    m, seq = h_t.shape
    tn = 512
    n_qkv, n_gates, n_ch = QKV_W // tn, GATES_W // tn, CH_W // tn
    starts = ([k * tn for k in range(n_qkv)] + [NAT_ZA, NAT_ZA + tn, NAT_B, NAT_B + tn, NAT_ZC, NAT_ZC + tn]
              + [NAT_C + k * tn for k in range(n_ch)])

    def out_row(j):
        row = 0
        for k, start in enumerate(starts):
            row = row + jnp.where(j == k, start // 32, 0)
        return pl.multiple_of(row * 32, 32), 0

    def body(a_ref, bqf, bqb, bg, bc, dlf, dlb, o_ref, lr_ref, acc, bq):
        j = pl.program_id(0)

        @pl.when(j == 0)
        def _():
            acc[:, 0:LR_W] = _dot(a_ref[...], (dlf[...] + dlb[...]).astype(BF16))
            lr_ref[...] = acc[:, 0:LR_W].T[0:2 * RANK, :]

        @pl.when(j < n_qkv)
        def _():
            bq[...] = _both_directions(bqf, bqb)
            acc[...] = _dot(a_ref[...], bq[...])

        @pl.when(jnp.logical_and(j >= n_qkv, j < n_qkv + n_gates))
        def _():
            acc[...] = _dot(a_ref[...], bg[...])

        @pl.when(j >= n_qkv + n_gates)
        def _():
            acc[...] = _dot(a_ref[...], bc[...])

        o_ref[...] = acc[...].T

    resident = lambda shape: pl.BlockSpec(shape, lambda j: (0, 0), pipeline_mode=pl.Buffered(1))
    seg = lambda first, count: pl.BlockSpec((seq, tn), lambda j: (0, jnp.clip(j - first, 0, count - 1)))
    main, lr_rows = pl.pallas_call(
        body, name="wgrad_in",
        out_shape=(jax.ShapeDtypeStruct((IN_W, m), F32), jax.ShapeDtypeStruct((2 * RANK, m), F32)),
        grid=(n_qkv + n_gates + n_ch,),
        in_specs=[resident((m, seq)), seg(0, n_qkv), seg(0, n_qkv), seg(n_qkv, n_gates), seg(n_qkv + n_gates, n_ch),
                  resident((seq, LR_W)), resident((seq, LR_W))],
        out_specs=(pl.BlockSpec((pl.Element(tn), pl.Element(m)), out_row),
                   pl.BlockSpec((2 * RANK, m), lambda j: (0, 0))),
        scratch_shapes=[pltpu.VMEM((m, tn), F32), pltpu.VMEM((seq, tn), BF16)],
        compiler_params=_cparams("arbitrary"),
    )(h_t, dqkv_f, dqkv_b, dp_gates, dp_ch, dlr_f, dlr_b)
    return lax.dynamic_update_slice(main, lr_rows, (NAT_LR, 0))


def _pad_rows(a, rows):
    return jnp.pad(a, ((0, rows - a.shape[0]), (0, 0)))


def _rows128(a):
    a = a.reshape(-1, 128)
    return _pad_rows(a, -(-a.shape[0] // 8) * 8)


def _pack(arrs):
    return jnp.concatenate([_rows128(a) for a in arrs], axis=0)


def _unpack(buf, like):
    out, start = [], 0
    for a in like:
        rows = a.size // 128
        out.append(buf[start:start + rows].reshape(a.shape))
        start += -(-rows // 8) * 8
    return out


def kernel(x, norm_g, w_in, w_gk_f, b_gk_f, w_gk_b, b_gk_b, gla_norm_g, conv_w, conv_b, w_out, final_g, loss_target, m_norm_g, m_w_in, m_w_gk_f, m_b_gk_f, m_w_gk_b, m_b_gk_b, m_gla_norm_g, m_conv_w, m_conv_b, m_w_out, m_final_g, v_norm_g, v_w_in, v_w_gk_f, v_b_gk_f, v_w_gk_b, v_b_gk_b, v_gla_norm_g, v_conv_w, v_conv_b, v_w_out, v_final_g):
    px, py, pc = _position()
    me = _blk(px, py, pc)
    seq = x.shape[1]
    x2d, tgt = x[0], loss_target[0]
    tt = min(256, seq)

    small_s = jnp.concatenate([jnp.concatenate([w_gk_f[0], w_gk_b[0]], axis=1), _pad_rows(conv_w[0], 8)], axis=0)
    order = sum(jnp.where(2 * px + py == k, jnp.asarray(tiles + (0,), jnp.int32), 0) for k, tiles in enumerate(TILE_ORDER))
    proj, lr, h_t, w_nat, wout_all, small_all = _gather_inproj(x2d, norm_g, w_in[0].T, w_out[0], small_s, order,
                                                               min(1024, seq))
    w_out_full = wout_all.reshape(MIX_W, D_MODEL)
    wgk_cols = 512 // N_DEV
    wgk_f_full = small_all[:, 0:RANK, 0:wgk_cols].transpose(1, 0, 2).reshape(RANK, QK_W)
    wgk_b_full = small_all[:, 0:RANK, wgk_cols:2 * wgk_cols].transpose(1, 0, 2).reshape(RANK, QK_W)
    conv_w_full = _pad_rows(small_all[:, RANK:RANK + 3, :].transpose(1, 0, 2).reshape(3, CONV_W), 8)
    zr = lambda n: jnp.zeros((n, QK_W), F32)
    wgk_f_pad = jnp.concatenate([wgk_f_full, zr(LR_W - RANK)], axis=0).astype(BF16)
    wgk_b_pad = jnp.concatenate([zr(RANK), wgk_b_full, zr(LR_W - 2 * RANK)], axis=0).astype(BF16)

    o_f, o_b, st_f, st_b = _gla_fwd(proj, lr, wgk_f_pad, wgk_b_pad, b_gk_f, b_gk_b, tt)
    tmix = min(512, seq)
    y_t, conv, dx2, dx2b, loss_p, dfg_p = _mix_out_loss(o_f, o_b, proj, x2d, tgt, gla_norm_g, conv_w_full, conv_b,
                                                        w_out_full, final_g.reshape(1, D_MODEL), tmix)

    dp_gates, do, dconv, dgg_p, dcb_p = _mix_bwd(dx2b, o_f, o_b, proj, conv, gla_norm_g, w_out_full, tmix)
    dp_ch, dcw_p = _conv_bwd(dconv, proj, conv_w_full, tmix)
    dqkv_f, dlr_f, dqkv_b, dlr_b, dwf_p, dwb_p, dbf_p, dbb_p = _gla_bwd(
        proj, lr, do, st_f, st_b, wgk_f_pad, wgk_b_pad, b_gk_f, b_gk_b, tt)
    dw_nat = _weight_grad_in(h_t, dqkv_f, dqkv_b, dp_gates, dp_ch, dlr_f, dlr_b)

    dw_out, sib_in = _weight_grad_out(y_t, dx2b, min(1024, seq), dw_nat)
    part_out = dw_out.reshape(N_DEV, MIX_W // N_DEV, D_MODEL)
    core = jnp.reshape(pc, (1,)).astype(jnp.int32)
    chip = jnp.reshape(2 * px + py, (1,)).astype(jnp.int32)
    sums_in, sib_out = _chip_sums(dw_nat, sib_in, core, D_MODEL, "chip_sums_in", riding=part_out)
    sums_out = _chip_sums(part_out, sib_out, core, D_MODEL, "chip_sums_out")
    grad_x2d, dng_p, far_in, far_out = _input_grad(dqkv_f, dqkv_b, dp_gates, dp_ch, dlr_f, dlr_b, w_nat, x2d, norm_g, dx2,
                                                   [sums_in, sums_out], min(256, seq))
    pieces = [dng_p, dbf_p, dbb_p, dgg_p, dcb_p, dfg_p[0], dwf_p[0:RANK], dwb_p[RANK:2 * RANK], dcw_p[0:3], loss_p[0]]
    g_window, small_tot = _final_sum(sums_in, far_in, chip, _pack(pieces), 512, "final_sum_in")
    g_in_t = lax.dynamic_slice_in_dim(g_window, 4 * pc, SHARD_W, axis=0)
    g_w_out, d_w_out, nm_w_out, nv_w_out = _final_sum_adamw(sums_out, far_out, chip, w_out[0], m_w_out[0], v_w_out[0],
                                                            256, "adamw_out")
    flat = lambda a: a[0].T.reshape(SHARD_W, D_MODEL // 128, 128)
    unflat = lambda a: a.reshape(SHARD_W, D_MODEL).T
    d_flat, m_flat, v_flat = _adamw_rows(g_in_t.reshape(SHARD_W, D_MODEL // 128, 128), flat(w_in), flat(m_w_in),
                                         flat(v_w_in), 450, "adamw_in")
    g_w_in, d_w_in, nm_w_in, nv_w_in = g_in_t.T, unflat(d_flat), unflat(m_flat), unflat(v_flat)

    tot = _unpack(small_tot, pieces)
    g_norm_g, g_b_gk_f, g_b_gk_b, g_gla, g_conv_b, g_final = tot[:6]
    g_wgk_f = lax.dynamic_slice_in_dim(tot[6], me * wgk_cols, wgk_cols, axis=1)[None]
    g_wgk_b = lax.dynamic_slice_in_dim(tot[7], me * wgk_cols, wgk_cols, axis=1)[None]
    g_conv_w = lax.dynamic_slice_in_dim(tot[8], me * 128, 128, axis=1)[None]
    loss = tot[9][0]

    small_g = [g_norm_g, g_b_gk_f, g_b_gk_b, g_gla, g_conv_b, g_final, g_wgk_f, g_wgk_b, g_conv_w]
    small_w = [norm_g, b_gk_f, b_gk_b, gla_norm_g, conv_b, final_g, w_gk_f, w_gk_b, conv_w]
    small_m = [m_norm_g, m_b_gk_f, m_b_gk_b, m_gla_norm_g, m_conv_b, m_final_g, m_w_gk_f, m_w_gk_b, m_conv_w]
    small_v = [v_norm_g, v_b_gk_f, v_b_gk_b, v_gla_norm_g, v_conv_b, v_final_g, v_w_gk_f, v_w_gk_b, v_conv_w]
    d_s, m_s, v_s = _adamw_small(_pack(small_g), _pack(small_w), _pack(small_m), _pack(small_v))
    d_l, m_l, v_l = _unpack(d_s, small_w), _unpack(m_s, small_w), _unpack(v_s, small_w)

    def ordered(sm, big_in, big_out):
        return [sm[0], big_in[None], sm[6], sm[1], sm[7], sm[2], sm[3], sm[8], sm[4], big_out[None], sm[5]]

    grads = ordered(small_g, g_w_in, g_w_out)
    deltas = ordered(d_l, d_w_in, d_w_out)
    new_m = ordered(m_l, nm_w_in, nm_w_out)
    new_v = ordered(v_l, nv_w_in, nv_w_out)
    return (loss, grad_x2d[None], *grads, *deltas, *new_m, *new_v)
```

```python
import functools

import jax
import jax.numpy as jnp
from jax import lax
from jax.experimental import pallas as pl
from jax.experimental.pallas import tpu as pltpu

F32 = jnp.float32
BF16 = jnp.bfloat16
MESH = pl.DeviceIdType.MESH

N_DEV = 8
D_MODEL = 1024
HEADS = 4
DK = 128
DV = 256
QK_W = HEADS * DK
V_W = HEADS * DV
CONV_W = 1024
MIX_W = V_W + CONV_W
CHUNK = 64
RANK = 16
IN_W = 7200
SHARD_W = IN_W // N_DEV
MAIN_W = 7168
LR_W = 128
OFF_Q, OFF_K, OFF_V, OFF_ZA, OFF_B, OFF_ZC, OFF_C, OFF_H = 0, 512, 1024, 2048, 3072, 4096, 5120, 6144
QKV_W, GATES_W, CH_W = 2048, 3072, 2048
NAT_ZA, NAT_LR, NAT_B, NAT_C, NAT_ZC = 2048, 3072, 3104, 4128, 6176
EPS = 1e-6
GATE_SCALE = 1.0 / 16.0
QSCALE = DK ** -0.5
REF_F, LAST_F = CHUNK // 2, CHUNK - 1
REF_B, LAST_B = CHUNK - 1 - CHUNK // 2, 0

ADAM_LR = 0.001
ADAM_B1 = 0.9
ADAM_B2 = 0.999
ADAM_EPS = 1e-08
ADAM_WD = 0.01
ADAM_STEP = 10

VMEM_LIMIT = 56 * 1024 * 1024


def _cparams(*sem):
    return pltpu.CompilerParams(dimension_semantics=sem, vmem_limit_bytes=VMEM_LIMIT)


def _dot(a, b):
    return jnp.dot(a, b, preferred_element_type=F32)


def _dot_nt(a, b):
    return lax.dot_general(a, b, (((1,), (1,)), ((), ())), preferred_element_type=F32)


def _dot_tn(a, b):
    return lax.dot_general(a, b, (((0,), (0,)), ((), ())), preferred_element_type=F32)


def _sigmoid(z):
    return jax.nn.sigmoid(z)


def _position():
    return lax.axis_index("x"), lax.axis_index("y"), lax.axis_index("c")


def _blk(px, py, pc):
    return 4 * px + 2 * py + pc


EDGE = 16
SHIFTED_ROWS = 912
BODY_ROWS = SHIFTED_ROWS - 2 * EDGE


def _first_tile_row(blk, px):
    return EDGE * (56 * blk + px)


def _edge_tiles():
    tiles = {}
    for blk in range(N_DEV):
        first = _first_tile_row(blk, blk // 4)
        tiles.setdefault(first, []).append((blk, 0))
        tiles.setdefault(first + EDGE + BODY_ROWS, []).append((blk, 1))
    return tiles


def _peer_copies(srcs, outs, send_sems, recv_sems):
    x, y, c = _position()
    me = _blk(x, y, c)
    copies = []
    for a, (src, out) in enumerate(zip(srcs, outs)):
        k = 0
        for dx in (0, 1):
            for dy in (0, 1):
                for dc in (0, 1):
                    if dx + dy + dc == 0:
                        continue
                    peer = (1 - x if dx else x, 1 - y if dy else y, 1 - c if dc else c)
                    copies.append(pltpu.make_async_remote_copy(
                        src_ref=src, dst_ref=out.at[me], send_sem=send_sems.at[a * 7 + k],
                        recv_sem=recv_sems.at[a * 7 + k], device_id=peer, device_id_type=MESH))
                    k += 1
    return copies


def _route_chips():
    x, y, c = _position()
    along_x = c == 0
    return [(jnp.where(along_x, 1 - x, x), jnp.where(along_x, y, 1 - y)),
            (jnp.where(along_x, x, 1 - x), jnp.where(along_x, 1 - y, y)), (1 - x, 1 - y)]


WINDOW_ROWS = SHARD_W + 4


def _window_start(k, parity):
    return 2 * SHARD_W * k + (SHARD_W - 4) * parity


def _owner_block(part, k, parity):
    if part.ndim == 3:
        return part.at[2 * k + parity]
    return part.at[pl.ds(pl.multiple_of(_window_start(k, parity), 8), WINDOW_ROWS)]


def _block_shape(part):
    return part.shape[1:] if part.ndim == 3 else (WINDOW_ROWS, part.shape[1])


def _sibling_copies(part, out, send_sems, recv_sems):
    x, y, c = _position()
    return [pltpu.make_async_remote_copy(src_ref=_owner_block(part, k, 1 - c), dst_ref=out.at[k],
                                         send_sem=send_sems.at[k], recv_sem=recv_sems.at[k],
                                         device_id=(x, y, 1 - c), device_id_type=MESH)
            for k in range(4)]


def _start_all(copies):
    for cp in copies:
        cp.start()


def _wait_all(copies):
    for cp in copies:
        cp.wait_recv()
    for cp in copies:
        cp.wait_send()


def _chip_sums(part, from_sibling, core, tc, name, riding=None):
    rows, cols = _block_shape(part)
    nj = cols // tc

    def body(core_ref, p_ref, s_ref, *rest):
        if riding is None:
            (o_ref,) = rest
        else:
            ride_in, o_ref, ride_out, send_sems, recv_sems = rest
            k, j = pl.program_id(0), pl.program_id(1)

            @pl.when(jnp.logical_and(k == 0, j == 0))
            def _():
                _start_all(_sibling_copies(ride_in, ride_out, send_sems, recv_sems))

        o_ref[0] = (p_ref[...].reshape(rows, tc) + s_ref[0]).astype(BF16)

        if riding is not None:
            @pl.when(jnp.logical_and(k == 3, j == nj - 1))
            def _():
                _wait_all(_sibling_copies(ride_in, ride_out, send_sems, recv_sems))

    hbm = pl.BlockSpec(memory_space=pl.ANY)
    sums = jax.ShapeDtypeStruct((4, rows, cols), BF16)
    tile_out = pl.BlockSpec((1, rows, tc), lambda k, j, core_ref: (k, 0, j))
    if part.ndim == 3:
        mine = pl.BlockSpec((1, rows, tc), lambda k, j, core_ref: (2 * k + core_ref[0], 0, j))
    else:
        mine = pl.BlockSpec((pl.Element(rows), pl.Element(tc)),
                            lambda k, j, core_ref: (pl.multiple_of(_window_start(k, core_ref[0]), 8),
                                                    pl.multiple_of(j * tc, 128)))
    in_specs = [mine, pl.BlockSpec((1, rows, tc), lambda k, j, core_ref: (k, 0, j))]
    if riding is None:
        out_shape, out_specs, scratch, args = sums, tile_out, [], (core, part, from_sibling)
    else:
        out_shape = (sums, jax.ShapeDtypeStruct((4,) + _block_shape(riding), F32))
        out_specs, in_specs = (tile_out, hbm), in_specs + [hbm]
        scratch = [pltpu.SemaphoreType.DMA((4,)), pltpu.SemaphoreType.DMA((4,))]
        args = (core, part, from_sibling, riding)
    return pl.pallas_call(
        body, name=name, out_shape=out_shape,
        grid_spec=pltpu.PrefetchScalarGridSpec(num_scalar_prefetch=1, grid=(4, nj), in_specs=in_specs,
                                               out_specs=out_specs, scratch_shapes=scratch),
        compiler_params=_cparams("arbitrary", "arbitrary"),
    )(*args)


def _sum_chips(s_ref, r_ref):
    f = lambda a: a.astype(F32)
    return (f(s_ref[0]) + f(r_ref[0])) + f(r_ref[1])


def _final_sum(sums, from_chips, chip, small, tc, name):
    _, rows, cols = sums.shape
    nj = cols // tc

    def body(chip_ref, s_ref, r_ref, sm_ref, g_out, tot_ref, all_ref, send_sems, recv_sems):
        j = pl.program_id(0)
        me = _blk(*_position())

        @pl.when(j == 0)
        def _():
            all_ref[me] = sm_ref[...]
            _start_all(_peer_copies((all_ref.at[me],), (all_ref,), send_sems, recv_sems))

        g_out[...] = _sum_chips(s_ref, r_ref)

        @pl.when(j == nj - 1)
        def _():
            _wait_all(_peer_copies((all_ref.at[me],), (all_ref,), send_sems, recv_sems))
            acc = all_ref[0]
            for d in range(1, N_DEV):
                acc = acc + all_ref[d]
            tot_ref[...] = acc

    whole = pl.BlockSpec(small.shape, lambda j, chip_ref: (0, 0))
    return pl.pallas_call(
        body, name=name,
        out_shape=(jax.ShapeDtypeStruct((rows, cols), F32), jax.ShapeDtypeStruct(small.shape, F32)),
        grid_spec=pltpu.PrefetchScalarGridSpec(
            num_scalar_prefetch=1, grid=(nj,),
            in_specs=[pl.BlockSpec((1, rows, tc), lambda j, chip_ref: (chip_ref[0], 0, j)),
                      pl.BlockSpec((2, rows, tc), lambda j, chip_ref: (0, 0, j)), whole],
            out_specs=(pl.BlockSpec((rows, tc), lambda j, chip_ref: (0, j)), whole),
            scratch_shapes=[pltpu.VMEM((N_DEV,) + small.shape, F32), pltpu.SemaphoreType.DMA((7,)),
                            pltpu.SemaphoreType.DMA((7,))]),
        compiler_params=_cparams("arbitrary"),
    )(chip, sums, from_chips, small)


def _adamw_rows(g, w, m, v, tr, name):
    rows = g.shape[0]

    def body(g_ref, w_ref, m_ref, v_ref, d_out, m_out, v_out):
        delta, m_new, v_new = _adamw(w_ref[...], g_ref[...], m_ref[...], v_ref[...])
        d_out[...] = delta
        m_out[...] = m_new
        v_out[...] = v_new

    tile = pl.BlockSpec((tr,) + g.shape[1:], lambda r: (r, 0, 0))
    shp = jax.ShapeDtypeStruct(g.shape, F32)
    return pl.pallas_call(
        body, name=name, out_shape=(shp, shp, shp), grid=(rows // tr,),
        in_specs=[tile] * 4, out_specs=(tile, tile, tile),
        compiler_params=_cparams("arbitrary"),
    )(g, w, m, v)


def _adamw(w, g, m, v):
    m = ADAM_B1 * m + (1.0 - ADAM_B1) * g
    v = ADAM_B2 * v + (1.0 - ADAM_B2) * (g * g)
    m_hat = m / (1.0 - ADAM_B1 ** ADAM_STEP)
    v_hat = v / (1.0 - ADAM_B2 ** ADAM_STEP)
    delta = -ADAM_LR * (m_hat / (jnp.sqrt(v_hat) + ADAM_EPS) + ADAM_WD * w)
    return delta, m, v


def _final_sum_adamw(sums, from_chips, chip, w, m, v, tr, name):
    rows, cols = w.shape

    def body(chip_ref, s_ref, r_ref, w_ref, m_ref, v_ref, g_out, d_out, m_out, v_out):
        g = _sum_chips(s_ref, r_ref)
        delta, m_new, v_new = _adamw(w_ref[...], g, m_ref[...], v_ref[...])
        g_out[...] = g
        d_out[...] = delta
        m_out[...] = m_new
        v_out[...] = v_new

    tile = pl.BlockSpec((tr, cols), lambda r, chip_ref: (r, 0))
    shp = jax.ShapeDtypeStruct((rows, cols), F32)
    return pl.pallas_call(
        body, name=name,
        out_shape=(shp, shp, shp, shp),
        grid_spec=pltpu.PrefetchScalarGridSpec(
            num_scalar_prefetch=1, grid=(rows // tr,),
            in_specs=[pl.BlockSpec((1, tr, cols), lambda r, chip_ref: (chip_ref[0], r, 0)),
                      pl.BlockSpec((2, tr, cols), lambda r, chip_ref: (0, r, 0)),
                      tile, tile, tile],
            out_specs=(tile, tile, tile, tile)),
        compiler_params=_cparams("arbitrary"),
    )(chip, sums, from_chips, w, m, v)


def _adamw_small(g, w, m, v):
    def body(g_ref, w_ref, m_ref, v_ref, d_out, m_out, v_out):
        delta, m_new, v_new = _adamw(w_ref[...], g_ref[...], m_ref[...], v_ref[...])
        d_out[...] = delta
        m_out[...] = m_new
        v_out[...] = v_new

    vmem = pl.BlockSpec(memory_space=pltpu.VMEM)
    shp = jax.ShapeDtypeStruct(g.shape, F32)
    return pl.pallas_call(body, name="adamw_small", out_shape=(shp, shp, shp),
                          in_specs=[vmem] * 4, out_specs=(vmem, vmem, vmem))(g, w, m, v)


TILE_ROWS = (0, 1024, NAT_ZA, NAT_B, NAT_ZC, NAT_C, NAT_C + CONV_W)


TILE_ORDER = ((0, 1, 2, 3, 5, 6, 4), (2, 1, 0, 4, 3, 5, 6), (5, 6, 0, 4, 1, 2, 3), (4, 6, 2, 3, 5, 0, 1))
EARLY_SWEEP, NEIGHBOUR_SWEEP, DIAGONAL_SWEEP = 1, 2, 4
PIECES, W_IN_PIECES, OTHER_PIECES = 4, (0, 1), (2, 3)


def _gather_inproj(x2d, norm_g, shard_t, w_out_s, small_s, order, tm):
    seq = x2d.shape[0]
    tn = CONV_W
    ni, nj = seq // tm, MAIN_W // tn
    first_sweep = lambda j, i, order_ref: jnp.where(j == 0, i, ni - 1)
    last_sweep = lambda j, i, order_ref: jnp.where(j == nj - 1, i, 0)
    edge_tiles = _edge_tiles()

    def body(order_ref, x_ref, g_ref, shard_ref, wout_ref, sm_ref, proj_ref, lr_ref, ht_ref, w_nat, wout_all, sm_all,
             w_all, h_all, edges, stage, wout_b, sm_b, send_sems, recv_sems, local_sems):
        j, i = pl.program_id(0), pl.program_id(1)
        rows = pl.ds(pl.multiple_of(i * tm, tm), tm)
        x, y, c = _position()
        me, here, sibling = _blk(x, y, c), (x, y, c), (x, y, 1 - c)
        chips = _route_chips()
        sibling_chips = [chips[1], chips[0], chips[2]]

        def pieces(px, py, pc):
            blk = _blk(px, py, pc)
            body_rows = pl.ds(pl.multiple_of(_first_tile_row(blk, px) + EDGE, EDGE), BODY_ROWS)
            return [w_all.at[body_rows], edges.at[blk], wout_all.at[blk], sm_all.at[blk]]

        def copy(a, k, block, to, staged=None):
            ref = pieces(*block)[a]
            return pltpu.make_async_remote_copy(src_ref=ref if staged is None else staged, dst_ref=ref,
                                                send_sem=send_sems.at[a * 7 + k], recv_sem=recv_sems.at[a * 7 + k],
                                                device_id=to, device_id_type=MESH)

        def own_copies(group, slots=(0, 1, 2)):
            targets = [sibling] + [(*chips[n], c) for n in range(2)]
            staged = [None, None, wout_b, sm_b]
            return [copy(a, k, here, targets[k], staged[a]) for k in slots for a in group]

        def relays(group):
            return [copy(a, 3, (*chips[0], c), (*chips[1], c)) for a in group]

        def forwards(n, group):
            return [copy(a, 4 + n, (*chips[n], c), sibling) for a in group]

        def keep_own():
            return [pltpu.make_async_copy(wout_b, wout_all.at[me], local_sems.at[0]),
                    pltpu.make_async_copy(sm_b, sm_all.at[me], local_sems.at[1])]

        def keep_weight():
            return pltpu.make_async_copy(w_all, w_nat, local_sems.at[2])

        def take(ns, group, relay=True):
            for n in ns:
                for a in group:
                    copy(a, 1 + n, (*chips[n], c), here).wait_recv()
                _start_all((relays(group) if n == 0 and relay else []) + forwards(n, group))

        def take_passed_on(ns, group):
            for n in ns:
                for a in group:
                    copy(a, 4 + n, (*sibling_chips[n], 1 - c), here).wait_recv()

        def arrive(ns, group):
            take(ns, group)
            take_passed_on(ns, group)

        def per_core_and_row(step):
            for core in range(2):
                for row in range(2):
                    pl.when(jnp.logical_and(c == core, y == row))(functools.partial(step, core, row))

        def start_own(core, row):
            now = (0, 1 + core) if core == row else (0, 2 - core, 1 + core)
            _start_all(own_copies(W_IN_PIECES, now))
            wout_b[...] = wout_ref[...].astype(BF16)
            sm_b[...] = sm_ref[...]
            _start_all(own_copies(OTHER_PIECES, now) + keep_own())

        def take_early(core, row):
            if core == row:
                _start_all(own_copies(W_IN_PIECES, (2 - core,)) + own_copies(OTHER_PIECES, (2 - core,)))
                take((1 - core,), W_IN_PIECES, relay=False)
            else:
                take_passed_on((core,), W_IN_PIECES)

        def take_neighbours(core, row):
            if core == row:
                _start_all(relays(W_IN_PIECES) if core == 1 else [])
                take((core,), W_IN_PIECES)
                take_passed_on((0, 1), W_IN_PIECES)
            else:
                take((0, 1), W_IN_PIECES)
                take_passed_on((1 - core,), W_IN_PIECES)

        early_blk = _blk(x, 1 - y, y)

        def add_edge_tiles(stage):
            for row, parts in edge_tiles.items():
                ready = 0
                for blk, _ in parts:
                    away = (x != blk // 4).astype(jnp.int32) + (y != (blk // 2) % 2).astype(jnp.int32)
                    late = jnp.where(away == 1, jnp.where(early_blk == blk, 1, 2), jnp.where(away == 2, 3 + blk % 2, 0))
                    ready = jnp.maximum(ready, late)

                @pl.when(ready == stage)
                def _(row=row, parts=parts):
                    tile = edges[parts[0][0], parts[0][1]].astype(F32)
                    for blk, side in parts[1:]:
                        tile = tile + edges[blk, side].astype(F32)
                    w_all[row:row + EDGE, :] = tile.astype(BF16)

        @pl.when(jnp.logical_and(j == 0, i == 0))
        def _():
            last = SHARD_W // 8 * 8
            for col in range(0, D_MODEL, 128):
                cols = slice(col, col + 128)
                stage[0:last, :] = shard_ref[0:last, cols]
                stage[last:, :] = jnp.zeros((SHIFTED_ROWS - last, 128), F32)
                stage[last:SHARD_W, :] = shard_ref[last:SHARD_W, cols]
                for k in range(EDGE // 4):
                    @pl.when(me % 4 == k)
                    def _(k=k, cols=cols):
                        moved = pltpu.roll(stage[...], 4 * k, 0) if k else stage[...]
                        pieces(*here)[0][:, cols] = moved[EDGE:EDGE + BODY_ROWS].astype(BF16)
                        edges[me, 0, :, cols] = moved[0:EDGE].astype(BF16)
                        edges[me, 1, :, cols] = moved[EDGE + BODY_ROWS:].astype(BF16)
            per_core_and_row(start_own)
            for a in W_IN_PIECES:
                copy(a, 0, sibling, here).wait_recv()
            add_edge_tiles(0)

        @pl.when(jnp.logical_and(j == EARLY_SWEEP, i == 0))
        def _():
            per_core_and_row(take_early)
            add_edge_tiles(1)

        @pl.when(jnp.logical_and(j == NEIGHBOUR_SWEEP, i == 0))
        def _():
            per_core_and_row(take_neighbours)
            add_edge_tiles(2)

        for core in range(2):
            @pl.when(jnp.logical_and(j == DIAGONAL_SWEEP + core, i == 0))
            def _(core=core):
                pl.when(c == core)(lambda: take((2,), W_IN_PIECES))
                pl.when(c != core)(lambda: take_passed_on((2,), W_IN_PIECES))
                add_edge_tiles(3 + core)
                if core == 0:
                    arrive((0, 1), OTHER_PIECES)
                else:
                    keep_weight().start()

        @pl.when(jnp.logical_and(j == nj - 1, i == 0))
        def _():
            arrive((2,), OTHER_PIECES)

        @pl.when(j == 0)
        def _():
            xv = x_ref[...]
            r = lax.rsqrt(jnp.mean(xv * xv, axis=-1, keepdims=True) + EPS)
            h = (xv * r) * g_ref[...]
            h_all[rows, :] = h.astype(BF16)
            ht_ref[...] = h.T.astype(BF16)

        tile = order_ref[j]
        row = 0
        for k, start in enumerate(TILE_ROWS):
            row = row + jnp.where(tile == k, start // 32, 0)
        w_tile = w_all[pl.ds(pl.multiple_of(row * 32, 32), tn), :]
        proj_ref[...] = _dot_nt(h_all[rows, :], w_tile).astype(BF16)

        @pl.when(j == nj - 1)
        def _():
            lr_ref[...] = _dot_nt(h_all[rows, :], w_all[NAT_LR:NAT_LR + LR_W, :])

        @pl.when(jnp.logical_and(j == nj - 1, i == ni - 1))
        def _():
            everything = range(PIECES)
            passed_on = [cp for n in range(3) for cp in forwards(n, everything)]
            for cp in own_copies(everything) + relays(everything) + passed_on:
                cp.wait_send()
            for a in OTHER_PIECES:
                copy(a, 0, sibling, here).wait_recv()
            for cp in keep_own() + [keep_weight()]:
                cp.wait()

    const = lambda shape: pl.BlockSpec(shape, lambda j, i, order_ref: (0,) * len(shape))
    hbm = pl.BlockSpec(memory_space=pl.ANY)
    vmem = pl.BlockSpec(memory_space=pltpu.VMEM)
    return pl.pallas_call(
        body, name="gather_inproj",
        out_shape=(jax.ShapeDtypeStruct((seq, MAIN_W), BF16), jax.ShapeDtypeStruct((seq, LR_W), F32),
                   jax.ShapeDtypeStruct((D_MODEL, seq), BF16), jax.ShapeDtypeStruct((IN_W, D_MODEL), BF16),
                   jax.ShapeDtypeStruct((N_DEV,) + w_out_s.shape, BF16),
                   jax.ShapeDtypeStruct((N_DEV,) + small_s.shape, F32)),
        grid_spec=pltpu.PrefetchScalarGridSpec(
            num_scalar_prefetch=1, grid=(nj, ni),
            in_specs=[pl.BlockSpec((tm, D_MODEL), lambda j, i, order_ref: (first_sweep(j, i, order_ref), 0)),
                      const((1, D_MODEL)), vmem, vmem, const(small_s.shape)],
            out_specs=(pl.BlockSpec((tm, tn), lambda j, i, order_ref: (i, order_ref[j])),
                       pl.BlockSpec((tm, LR_W), lambda j, i, order_ref: (last_sweep(j, i, order_ref), 0)),
                       pl.BlockSpec((D_MODEL, tm), lambda j, i, order_ref: (0, first_sweep(j, i, order_ref))),
                       hbm, hbm, hbm),
            scratch_shapes=[pltpu.VMEM((IN_W, D_MODEL), BF16), pltpu.VMEM((seq, D_MODEL), BF16),
                            pltpu.VMEM((N_DEV, 2, EDGE, D_MODEL), BF16), pltpu.VMEM((SHIFTED_ROWS, 128), F32),
                            pltpu.VMEM(w_out_s.shape, BF16), pltpu.VMEM(small_s.shape, F32),
                            pltpu.SemaphoreType.DMA((7 * PIECES,)), pltpu.SemaphoreType.DMA((7 * PIECES,)),
                            pltpu.SemaphoreType.DMA((3,))]),
        compiler_params=_cparams("arbitrary", "arbitrary"),
    )(order, x2d, norm_g, shard_t, w_out_s, small_s)


def _block_masks(tt):
    row = lax.broadcasted_iota(jnp.int32, (tt, tt), 0)
    col = lax.broadcasted_iota(jnp.int32, (tt, tt), 1)
    same = jnp.right_shift(row, 6) == jnp.right_shift(col, 6)
    return (jnp.logical_and(same, col <= row), jnp.logical_and(same, col >= row), jnp.logical_and(same, col > row))


def _dot_split3(ones_mat, x):
    x1 = x.astype(BF16)
    r1 = x - x1.astype(F32)
    x2 = r1.astype(BF16)
    x3 = (r1 - x2.astype(F32)).astype(BF16)
    return (_dot(ones_mat, x3) + _dot(ones_mat, x2)) + _dot(ones_mat, x1)


def _log_gate(logits):
    return (jnp.minimum(logits, 0.0) - jnp.log(1.0 + jnp.exp(-jnp.abs(logits)))) * GATE_SCALE


def _chunk_column_mask(tt):
    nc = tt // CHUNK
    row = lax.broadcasted_iota(jnp.int32, (tt, nc * DK), 0)
    col = lax.broadcasted_iota(jnp.int32, (tt, nc * DK), 1)
    return jnp.right_shift(row, 6) == jnp.right_shift(col, 7)


def _chunked(mask, x, nc):
    wide = jnp.concatenate([x] * nc, axis=1)
    return jnp.where(mask, wide, jnp.zeros_like(wide))


def _gla_fwd(proj, lr, wgk_f, wgk_b, bgk_f, bgk_b, tt):
    seq = proj.shape[0]
    nb, nc, nch = seq // tt, tt // CHUNK, seq // CHUNK

    def body(qf, kf, vf, lrf, qb, kb, vb, lrb, wf, wb, bf, bb, of, ob, stf, stb, s_scr, qs_s, ks_s, qin_s, kout_s):
        @pl.when(pl.program_id(0) == 0)
        def _():
            s_scr[...] = jnp.zeros(s_scr.shape, F32)

        low, upp, sup = _block_masks(tt)
        dirs = ((qf, kf, vf, lrf, wf, bf, of, stf, low, low, REF_F, LAST_F, list(range(nc))),
                (qb, kb, vb, lrb, wb, bb, ob, stb, upp, sup, REF_B, LAST_B, list(reversed(range(nc)))))
        for d, (q_r, k_r, v_r, lr_r, w_r, b_r, o_r, st_r, cum, mask, ref, last, order) in enumerate(dirs):
            logits = _dot(lr_r[...].astype(BF16), w_r[...]) + b_r[...]
            b = _dot_split3(cum.astype(BF16), _log_gate(logits))
            decs = []
            for c in range(nc):
                rows = slice(c * CHUNK, (c + 1) * CHUNK)
                bc = b[rows]
                b_ref, b_last = bc[ref:ref + 1], bc[last:last + 1]
                qc = q_r[rows, :].astype(F32) * QSCALE
                kc = k_r[rows, :].astype(F32)
                qs_s[rows, :] = (qc * jnp.exp(bc - b_ref)).astype(BF16)
                ks_s[rows, :] = (kc * jnp.exp(b_ref - bc)).astype(BF16)
                qin_s[rows, :] = (qc * jnp.exp(bc)).astype(BF16)
                kout_s[rows, :] = (kc * jnp.exp(b_last - bc)).astype(BF16)
                decs.append(jnp.exp(b_last))
            for h in range(HEADS):
                ksl = slice(h * DK, (h + 1) * DK)
                vsl = slice(h * DV, (h + 1) * DV)
                v = v_r[:, vsl].astype(BF16)
                att = jnp.where(mask, _dot_nt(qs_s[:, ksl], ks_s[:, ksl]), 0.0).astype(BF16)
                o_intra = _dot(att, v)
                st = s_scr[d * HEADS + h]
                for c in order:
                    rows = slice(c * CHUNK, (c + 1) * CHUNK)
                    stb = st.astype(BF16)
                    st_r[c, h] = stb
                    o_r[rows, vsl] = (o_intra[rows] + _dot_nt(qin_s[rows, ksl], stb)).astype(BF16)
                    st = st * decs[c][:, ksl] + _dot_tn(v[rows], kout_s[rows, ksl])
                s_scr[d * HEADS + h] = st

    fw = lambda i: (i, 0)
    bw = lambda i: (nb - 1 - i, 0)
    const = lambda i: (0, 0)

    def tok_specs(m):
        return [pl.BlockSpec((tt, QK_W), lambda i: (m(i)[0], OFF_Q // QK_W)),
                pl.BlockSpec((tt, QK_W), lambda i: (m(i)[0], OFF_K // QK_W)),
                pl.BlockSpec((tt, V_W), lambda i: (m(i)[0], OFF_V // V_W)),
                pl.BlockSpec((tt, LR_W), m)]

    st_shape = jax.ShapeDtypeStruct((nch, HEADS, DV, DK), BF16)
    o_shape = jax.ShapeDtypeStruct((seq, V_W), BF16)
    operand = pltpu.VMEM((tt, QK_W), BF16)
    return pl.pallas_call(
        body, name="gla_fwd",
        out_shape=(o_shape, o_shape, st_shape, st_shape),
        grid=(nb,),
        in_specs=tok_specs(fw) + tok_specs(bw) + [
            pl.BlockSpec((LR_W, QK_W), const), pl.BlockSpec((LR_W, QK_W), const),
            pl.BlockSpec((1, QK_W), const), pl.BlockSpec((1, QK_W), const)],
        out_specs=(pl.BlockSpec((tt, V_W), fw), pl.BlockSpec((tt, V_W), bw),
                   pl.BlockSpec((nc, HEADS, DV, DK), lambda i: (i, 0, 0, 0)),
                   pl.BlockSpec((nc, HEADS, DV, DK), lambda i: (nb - 1 - i, 0, 0, 0))),
        scratch_shapes=[pltpu.VMEM((2 * HEADS, DV, DK), F32), operand, operand, operand, operand],
        compiler_params=_cparams("arbitrary"),
    )(proj, proj, proj, lr, proj, proj, proj, lr, wgk_f, wgk_b, bgk_f, bgk_b)


def _head_norm(o, gain):
    outs, rinv = [], []
    for h in range(HEADS):
        oh = o[:, h * DV:(h + 1) * DV]
        r = lax.rsqrt(jnp.mean(oh * oh, axis=-1, keepdims=True) + EPS)
        outs.append((oh * r) * gain)
        rinv.append(r)
    return jnp.concatenate(outs, axis=1), rinv


def _shift_rows(u, prev_row, next_row):
    n = u.shape[0]
    row = lax.broadcasted_iota(jnp.int32, (n, 1), 0)
    up = jnp.where(row == 0, prev_row, pltpu.roll(u, 1, 0))
    un = jnp.where(row == n - 1, next_row, pltpu.roll(u, n - 1, 0))
    return up, un


HALO = 16


def _halo_specs(tm, seq, col_block):
    per = tm // HALO
    last = seq // HALO - 1
    return [pl.BlockSpec((HALO, CONV_W), lambda i: (jnp.maximum(i * per - 1, 0), col_block)),
            pl.BlockSpec((HALO, CONV_W), lambda i: (jnp.minimum((i + 1) * per, last), col_block))]


def _f32(ref):
    return ref[...].astype(F32)


def _last_row(ref):
    return ref[HALO - 1:HALO, :].astype(F32)


def _first_row(ref):
    return ref[0:1, :].astype(F32)


def _mix_out_loss(o_f, o_b, proj, x2d, tgt, gla_g, conv_w, conv_b, w_out, final_g, tm):
    seq = x2d.shape[0]
    nt = seq // tm

    def body(of, ob, za, bg, cg, hc, zc, cprev, cnext, hprev, hnext, x_ref, t_ref, gg, cw, cb, wo, fg,
             yt_ref, conv_ref, dx2_ref, dx2b_ref, loss_ref, dfg_ref):
        i = pl.program_id(0)

        @pl.when(i == 0)
        def _():
            loss_ref[...] = jnp.zeros(loss_ref.shape, F32)
            dfg_ref[...] = jnp.zeros(dfg_ref.shape, F32)

        on, _ = _head_norm(_f32(of) + _f32(ob), gg[...])
        zav = _f32(za)
        y_a = on * (zav * _sigmoid(zav))
        u = _f32(cg) * _f32(hc)
        prev_row = jnp.where(i > 0, _last_row(cprev) * _last_row(hprev), 0.0)
        next_row = jnp.where(i < nt - 1, _first_row(cnext) * _first_row(hnext), 0.0)
        up, un = _shift_rows(u, prev_row, next_row)
        conv = (cw[0:1, :] * up + cw[1:2, :] * u + cw[2:3, :] * un) + cb[...]
        conv_ref[...] = conv.astype(BF16)
        zcv = _f32(zc)
        y_c = _f32(bg) * conv * (zcv * _sigmoid(zcv))
        y = jnp.concatenate([y_a, y_c], axis=1)
        yt_ref[...] = y.T.astype(BF16)
        x2 = x_ref[...] + _dot(y.astype(BF16), wo[...])
        r = lax.rsqrt(jnp.mean(x2 * x2, axis=-1, keepdims=True) + EPS)
        xn = x2 * r
        err = xn * fg[...] - t_ref[...]
        loss_ref[...] += 0.5 * jnp.sum(jnp.mean(err * err, axis=-1, keepdims=True))
        dyf = err * (1.0 / D_MODEL)
        dfg_ref[...] += jnp.sum(dyf * xn, axis=0, keepdims=True)
        dxn = dyf * fg[...]
        dx2 = r * dxn - xn * (r * jnp.mean(dxn * xn, axis=-1, keepdims=True))
        dx2_ref[...] = dx2
        dx2b_ref[...] = dx2.astype(BF16)

    def col(off):
        return pl.BlockSpec((tm, CONV_W), lambda i: (i, off // CONV_W))

    rowt = pl.BlockSpec((tm, D_MODEL), lambda i: (i, 0))
    const = lambda shape: pl.BlockSpec(shape, lambda i: (0, 0))
    return pl.pallas_call(
        body, name="mix_out_loss",
        out_shape=(jax.ShapeDtypeStruct((MIX_W, seq), BF16), jax.ShapeDtypeStruct((seq, CONV_W), BF16),
                   jax.ShapeDtypeStruct((seq, D_MODEL), F32), jax.ShapeDtypeStruct((seq, D_MODEL), BF16),
                   jax.ShapeDtypeStruct((8, 128), F32), jax.ShapeDtypeStruct((1, D_MODEL), F32)),
        grid=(nt,),
        in_specs=[rowt, rowt, col(OFF_ZA), col(OFF_B), col(OFF_C), col(OFF_H), col(OFF_ZC)]
        + _halo_specs(tm, seq, OFF_C // CONV_W) + _halo_specs(tm, seq, OFF_H // CONV_W)
        + [rowt, rowt, const((1, DV)), const((8, CONV_W)), const((1, CONV_W)), const((MIX_W, D_MODEL)),
           const((1, D_MODEL))],
        out_specs=(pl.BlockSpec((MIX_W, tm), lambda i: (0, i)), rowt, rowt, rowt, const((8, 128)),
                   const((1, D_MODEL))),
        compiler_params=_cparams("arbitrary"),
    )(o_f, o_b, proj, proj, proj, proj, proj, proj, proj, proj, proj, x2d, tgt, gla_g, conv_w, conv_b, w_out, final_g)


def _dsilu(z, s):
    return s * (1.0 + z * (1.0 - s))


def _mix_bwd(dx2b, o_f, o_b, proj, conv, gla_g, w_out, tm):
    seq = dx2b.shape[0]

    def body(dx, of, ob, za, bg, zc, cv, gg, wo, dg_ref, do_ref, dconv_ref, dgg_ref, dcb_ref):
        @pl.when(pl.program_id(0) == 0)
        def _():
            dgg_ref[...] = jnp.zeros(dgg_ref.shape, F32)
            dcb_ref[...] = jnp.zeros(dcb_ref.shape, F32)

        dy = _dot_nt(dx[...], wo[...])
        dy_a, dy_c = dy[:, :V_W], dy[:, V_W:]
        zcv, bgv, convv = _f32(zc), _f32(bg), _f32(cv)
        sc = _sigmoid(zcv)
        szc = zcv * sc
        dg_ref[:, CONV_W:2 * CONV_W] = (dy_c * convv * szc).astype(BF16)
        dconv = dy_c * bgv * szc
        dconv_ref[...] = dconv.astype(BF16)
        dcb_ref[...] += jnp.sum(dconv, axis=0, keepdims=True)
        dg_ref[:, 2 * CONV_W:] = (dy_c * bgv * convv * _dsilu(zcv, sc)).astype(BF16)

        o = _f32(of) + _f32(ob)
        gain = gg[...]
        on, rinv = _head_norm(o, gain)
        zav = _f32(za)
        sa = _sigmoid(zav)
        dg_ref[:, :CONV_W] = (dy_a * on * _dsilu(zav, sa)).astype(BF16)
        don = dy_a * (zav * sa)
        dgg = jnp.zeros((1, DV), F32)
        dos = []
        for h in range(HEADS):
            sl = slice(h * DV, (h + 1) * DV)
            oh, r, dh = o[:, sl], rinv[h], don[:, sl]
            ohn = oh * r
            dgg = dgg + jnp.sum(dh * ohn, axis=0, keepdims=True)
            dn = dh * gain
            dos.append(r * dn - ohn * (r * jnp.mean(dn * ohn, axis=-1, keepdims=True)))
        dgg_ref[...] += dgg
        do_ref[...] = jnp.concatenate(dos, axis=1).astype(BF16)

    def col(off):
        return pl.BlockSpec((tm, CONV_W), lambda i: (i, off // CONV_W))

    rowt = pl.BlockSpec((tm, D_MODEL), lambda i: (i, 0))
    const = lambda shape: pl.BlockSpec(shape, lambda i: (0, 0))
    return pl.pallas_call(
        body, name="mix_bwd",
        out_shape=(jax.ShapeDtypeStruct((seq, GATES_W), BF16), jax.ShapeDtypeStruct((seq, V_W), BF16),
                   jax.ShapeDtypeStruct((seq, CONV_W), BF16),
                   jax.ShapeDtypeStruct((1, DV), F32), jax.ShapeDtypeStruct((1, CONV_W), F32)),
        grid=(seq // tm,),
        in_specs=[rowt, rowt, rowt, col(OFF_ZA), col(OFF_B), col(OFF_ZC), rowt, const((1, DV)),
                  const((MIX_W, D_MODEL))],
        out_specs=(pl.BlockSpec((tm, GATES_W), lambda i: (i, 0)), rowt, rowt, const((1, DV)), const((1, CONV_W))),
        compiler_params=_cparams("arbitrary"),
    )(dx2b, o_f, o_b, proj, proj, proj, conv, gla_g, w_out)


def _conv_bwd(dconv, proj, conv_w, tm):
    seq = dconv.shape[0]
    nt = seq // tm

    def body(dc_in, dprev, dnext, cg, hc, cprev, cnext, hprev, hnext, cw, dch_ref, dcw_ref):
        i = pl.program_id(0)

        @pl.when(i == 0)
        def _():
            dcw_ref[...] = jnp.zeros(dcw_ref.shape, F32)

        first, lastt = i > 0, i < nt - 1
        dcv = _f32(dc_in)
        d_up, d_un = _shift_rows(dcv, jnp.where(first, _last_row(dprev), 0.0), jnp.where(lastt, _first_row(dnext), 0.0))
        cgv, hcv = _f32(cg), _f32(hc)
        u = cgv * hcv
        u_up, u_un = _shift_rows(u, jnp.where(first, _last_row(cprev) * _last_row(hprev), 0.0),
                                 jnp.where(lastt, _first_row(cnext) * _first_row(hnext), 0.0))
        du = cw[0:1, :] * d_un + cw[1:2, :] * dcv + cw[2:3, :] * d_up
        dch_ref[:, :CONV_W] = (du * hcv).astype(BF16)
        dch_ref[:, CONV_W:] = (du * cgv).astype(BF16)
        dcw_ref[0:1, :] += jnp.sum(dcv * u_up, axis=0, keepdims=True)
        dcw_ref[1:2, :] += jnp.sum(dcv * u, axis=0, keepdims=True)
        dcw_ref[2:3, :] += jnp.sum(dcv * u_un, axis=0, keepdims=True)

    def col(off):
        return pl.BlockSpec((tm, CONV_W), lambda i: (i, off // CONV_W))

    rowt = pl.BlockSpec((tm, CONV_W), lambda i: (i, 0))
    const = lambda shape: pl.BlockSpec(shape, lambda i: (0, 0))
    return pl.pallas_call(
        body, name="conv_bwd",
        out_shape=(jax.ShapeDtypeStruct((seq, CH_W), BF16), jax.ShapeDtypeStruct((8, CONV_W), F32)),
        grid=(nt,),
        in_specs=[rowt] + _halo_specs(tm, seq, 0) + [col(OFF_C), col(OFF_H)]
        + _halo_specs(tm, seq, OFF_C // CONV_W) + _halo_specs(tm, seq, OFF_H // CONV_W) + [const((8, CONV_W))],
        out_specs=(pl.BlockSpec((tm, CH_W), lambda i: (i, 0)), const((8, CONV_W))),
        compiler_params=_cparams("arbitrary"),
    )(dconv, dconv, dconv, proj, proj, proj, proj, proj, proj, conv_w)


def _gla_bwd(proj, lr, do, st_f, st_b, wgk_f, wgk_b, bgk_f, bgk_b, tt):
    seq = proj.shape[0]
    nb, nc = seq // tt, tt // CHUNK

    def body(qf, kf, vf, lrf, dof, stf, qb, kb, vb, lrb, dob, stb, wf, wb, bf, bb,
             dqkv_f, dlr_f, dqkv_b, dlr_b, dwf, dwb, dbf, dbb,
             ds_scr, eq_s, ek_s, ein_s, eout_s, qs_s, ks_s, qin_s, kout_s, db_s, lg_s, dqs_s, dks_s, dvi_s, gt_s):
        @pl.when(pl.program_id(0) == 0)
        def _():
            ds_scr[...] = jnp.zeros(ds_scr.shape, F32)
            for r in (dwf, dwb, dbf, dbb):
                r[...] = jnp.zeros(r.shape, F32)

        low, upp, sup = _block_masks(tt)
        row = lax.broadcasted_iota(jnp.int32, (CHUNK, 1), 0)
        kmask = _chunk_column_mask(tt)
        dirs = ((qf, kf, vf, lrf, dof, stf, wf, bf, dqkv_f, dlr_f, dwf, dbf,
                 low, upp, low, REF_F, LAST_F, list(reversed(range(nc)))),
                (qb, kb, vb, lrb, dob, stb, wb, bb, dqkv_b, dlr_b, dwb, dbb,
                 upp, low, sup, REF_B, LAST_B, list(range(nc))))
        for d, (q_r, k_r, v_r, lr_r, do_r, st_r, w_r, b_r, dqkv_r, dlr_r, dw_r, db_r,
                cum, cum_t, mask, ref, last, order) in enumerate(dirs):
            lrv = lr_r[...].astype(BF16)
            wv = w_r[...]
            logits = _dot(lrv, wv) + b_r[...]
            lg_s[...] = logits
            b = _dot_split3(cum.astype(BF16), _log_gate(logits))
            decs = []
            for c in range(nc):
                rows = slice(c * CHUNK, (c + 1) * CHUNK)
                bc = b[rows]
                b_ref, b_last = bc[ref:ref + 1], bc[last:last + 1]
                qc = q_r[rows, :].astype(F32) * QSCALE
                kc = k_r[rows, :].astype(F32)
                e_q, e_k, e_in, e_out = jnp.exp(bc - b_ref), jnp.exp(b_ref - bc), jnp.exp(bc), jnp.exp(b_last - bc)
                eq_s[rows, :], ek_s[rows, :], ein_s[rows, :], eout_s[rows, :] = e_q, e_k, e_in, e_out
                qs_s[rows, :] = (qc * e_q).astype(BF16)
                ks_s[rows, :] = (kc * e_k).astype(BF16)
                qin_s[rows, :] = (qc * e_in).astype(BF16)
                kout_s[rows, :] = (kc * e_out).astype(BF16)
                decs.append(jnp.exp(b_last))
            for h in range(HEADS):
                ksl = slice(h * DK, (h + 1) * DK)
                vsl = slice(h * DV, (h + 1) * DV)
                v = v_r[:, vsl].astype(BF16)
                dov = do_r[:, vsl].astype(BF16)
                qsb, ksb = qs_s[:, ksl], ks_s[:, ksl]
                att = jnp.where(mask, _dot_nt(qsb, ksb), 0.0).astype(BF16)
                datt = jnp.where(mask, _dot_nt(dov, v), 0.0).astype(BF16)
                dqs, dks, dv_intra, g_t = dqs_s, dks_s, dvi_s, gt_s
                dqs[...] = _dot(datt, ksb)
                dks[...] = _dot_tn(datt, qsb)
                dv_intra[...] = _dot_tn(att, dov)
                g_t[...] = _dot_tn(dov, _chunked(kmask, qin_s[:, ksl], nc))
                ds = ds_scr[d * HEADS + h]
                for c in order:
                    rows = slice(c * CHUNK, (c + 1) * CHUNK)
                    dsb = ds.astype(BF16)
                    s_prev = st_r[c, h]
                    dk_out = _dot(v[rows], dsb)
                    dq_in = _dot(dov[rows], s_prev)
                    dv = dv_intra[rows] + _dot_nt(kout_s[rows, ksl], dsb)
                    dqkv_r[rows, OFF_V + h * DV:OFF_V + (h + 1) * DV] = dv.astype(BF16)
                    dec = decs[c][:, ksl]
                    ddec = jnp.sum(ds * s_prev.astype(F32), axis=0, keepdims=True)
                    e_out = eout_s[rows, ksl]
                    qc = q_r[rows, ksl].astype(F32) * QSCALE
                    kc = k_r[rows, ksl].astype(F32)
                    dq = dqs[rows] * eq_s[rows, ksl] + dq_in * ein_s[rows, ksl]
                    dk = dks[rows] * ek_s[rows, ksl] + dk_out * e_out
                    dqkv_r[rows, OFF_Q + h * DK:OFF_Q + (h + 1) * DK] = (dq * QSCALE).astype(BF16)
                    dqkv_r[rows, OFF_K + h * DK:OFF_K + (h + 1) * DK] = dk.astype(BF16)
                    tail = jnp.sum(dk_out * (kc * e_out), axis=0, keepdims=True) + ddec * dec
                    db_s[rows, ksl] = (qc * dq - kc * dk) + jnp.where(row == last, tail, 0.0)
                    ds = ds * dec + g_t[:, c * DK:(c + 1) * DK]
                ds_scr[d * HEADS + h] = ds
            dg = _dot_split3(cum_t.astype(BF16), db_s[...])
            dlogit = (dg * GATE_SCALE) * _sigmoid(-lg_s[...])
            dlb = dlogit.astype(BF16)
            dlr_r[...] = _dot_nt(dlb, wv)
            dw_r[...] += _dot_tn(lrv, dlb)
            db_r[...] += jnp.sum(dlogit, axis=0, keepdims=True)

    fw = lambda i: (nb - 1 - i, 0)
    bw = lambda i: (i, 0)
    const = lambda i: (0, 0)

    def tok_specs(m):
        return [pl.BlockSpec((tt, QK_W), lambda i: (m(i)[0], OFF_Q // QK_W)),
                pl.BlockSpec((tt, QK_W), lambda i: (m(i)[0], OFF_K // QK_W)),
                pl.BlockSpec((tt, V_W), lambda i: (m(i)[0], OFF_V // V_W)),
                pl.BlockSpec((tt, LR_W), m),
                pl.BlockSpec((tt, V_W), m),
                pl.BlockSpec((nc, HEADS, DV, DK), lambda i: (m(i)[0], 0, 0, 0))]

    dqkv = jax.ShapeDtypeStruct((seq, QK_W + QK_W + V_W), BF16)
    dlr = jax.ShapeDtypeStruct((seq, LR_W), F32)
    dw = jax.ShapeDtypeStruct((LR_W, QK_W), F32)
    dbias = jax.ShapeDtypeStruct((1, QK_W), F32)
    return pl.pallas_call(
        body, name="gla_bwd",
        out_shape=(dqkv, dlr, dqkv, dlr, dw, dw, dbias, dbias),
        grid=(nb,),
        in_specs=tok_specs(fw) + tok_specs(bw) + [
            pl.BlockSpec((LR_W, QK_W), const), pl.BlockSpec((LR_W, QK_W), const),
            pl.BlockSpec((1, QK_W), const), pl.BlockSpec((1, QK_W), const)],
        out_specs=(pl.BlockSpec((tt, QK_W + QK_W + V_W), fw), pl.BlockSpec((tt, LR_W), fw),
                   pl.BlockSpec((tt, QK_W + QK_W + V_W), bw), pl.BlockSpec((tt, LR_W), bw),
                   pl.BlockSpec((LR_W, QK_W), const), pl.BlockSpec((LR_W, QK_W), const),
                   pl.BlockSpec((1, QK_W), const), pl.BlockSpec((1, QK_W), const)),
        scratch_shapes=[pltpu.VMEM((2 * HEADS, DV, DK), F32)] + [pltpu.VMEM((tt, QK_W), F32)] * 4
        + [pltpu.VMEM((tt, QK_W), BF16)] * 4 + [pltpu.VMEM((tt, QK_W), F32)] * 2
        + [pltpu.VMEM((tt, DK), F32)] * 2 + [pltpu.VMEM((tt, DV), F32), pltpu.VMEM((DV, nc * DK), F32)],
        compiler_params=_cparams("arbitrary"),
    )(proj, proj, proj, lr, do, st_f, proj, proj, proj, lr, do, st_b, wgk_f, wgk_b, bgk_f, bgk_b)


def _both_directions(f_ref, b_ref):
    return (_f32(f_ref) + _f32(b_ref)).astype(BF16)


def _input_grad(dqkv_f, dqkv_b, dp_gates, dp_ch, dlr_f, dlr_b, w_nat, x2d, norm_g, dx2, sums, tm):
    seq = x2d.shape[0]
    nt, n = seq // tm, len(sums)
    relay_step = (3 * nt) // 8

    def body(dqf, dqb, dg, dc, dlf, dlb, w, x_ref, g_ref, dx2_ref, *rest):
        ins, (gx_ref, dng_ref), outs = rest[:n], rest[n:n + 2], rest[n + 2:2 * n + 2]
        passing, joined = rest[2 * n + 2:3 * n + 2], rest[3 * n + 2:4 * n + 2]
        send_sems, recv_sems, local_sems = rest[4 * n + 2:]
        i = pl.program_id(0)
        c = lax.axis_index("c")
        first, second, diagonal = _route_chips()
        slot = lambda chip: 2 * chip[0] + chip[1]

        def remote(a, k, src, dst, to):
            return pltpu.make_async_remote_copy(src_ref=src, dst_ref=dst, send_sem=send_sems.at[3 * a + k],
                                                recv_sem=recv_sems.at[3 * a + k], device_id=(*to, c),
                                                device_id_type=MESH)

        direct = lambda a: remote(a, 0, ins[a].at[slot(first)], outs[a].at[0], first)
        for_second = lambda a: remote(a, 1, ins[a].at[slot(diagonal)], passing[a], first)
        joint = lambda a: remote(a, 2, joined[a], outs[a].at[1], second)
        own = lambda a: pltpu.make_async_copy(ins[a].at[slot(second)], joined[a], local_sems.at[a])

        @pl.when(i == 0)
        def _():
            _start_all([for_second(a) for a in range(n)] + [own(a) for a in range(n)] + [direct(a) for a in range(n)])
            dng_ref[...] = jnp.zeros(dng_ref.shape, F32)

        @pl.when(i == relay_step)
        def _():
            for a in range(n):
                for_second(a).wait_recv()
                own(a).wait()
                joined[a][...] = (joined[a][...].astype(F32) + passing[a][...].astype(F32)).astype(BF16)
                joint(a).start()

        dh = (_dot((dlf[...] + dlb[...]).astype(BF16), w[NAT_LR:NAT_LR + LR_W, :])
              + _dot(_both_directions(dqf, dqb), w[0:NAT_ZA, :])
              + _dot(dg[:, 0:CONV_W], w[NAT_ZA:NAT_LR, :]) + _dot(dg[:, CONV_W:2 * CONV_W], w[NAT_B:NAT_C, :])
              + _dot(dg[:, 2 * CONV_W:], w[NAT_ZC:IN_W, :]) + _dot(dc[...], w[NAT_C:NAT_ZC, :]))
        xv = x_ref[...]
        r = lax.rsqrt(jnp.mean(xv * xv, axis=-1, keepdims=True) + EPS)
        xn = xv * r
        dng_ref[...] += jnp.sum(dh * xn, axis=0, keepdims=True)
        dn = dh * g_ref[...]
        gx_ref[...] = (r * dn - xn * (r * jnp.mean(dn * xn, axis=-1, keepdims=True))) + dx2_ref[...]

        @pl.when(i == nt - 1)
        def _():
            for a in range(n):
                direct(a).wait_recv()
                joint(a).wait_recv()
            for a in range(n):
                for cp in (direct(a), for_second(a), joint(a)):
                    cp.wait_send()

    rowt = pl.BlockSpec((tm, D_MODEL), lambda i: (i, 0))
    seg = lambda width: pl.BlockSpec((tm, width), lambda i: (i, 0))
    resident = lambda rows: pl.BlockSpec((rows, D_MODEL), lambda i: (0, 0), pipeline_mode=pl.Buffered(1))
    hbm = pl.BlockSpec(memory_space=pl.ANY)
    blocks = [pltpu.VMEM(s.shape[1:], s.dtype) for s in sums]
    return pl.pallas_call(
        body, name="input_grad",
        out_shape=(jax.ShapeDtypeStruct((seq, D_MODEL), F32), jax.ShapeDtypeStruct((1, D_MODEL), F32))
        + tuple(jax.ShapeDtypeStruct((2,) + s.shape[1:], s.dtype) for s in sums),
        grid=(nt,),
        in_specs=[seg(QKV_W), seg(QKV_W), seg(GATES_W), seg(CH_W), seg(LR_W), seg(LR_W), resident(IN_W),
                  rowt, pl.BlockSpec((1, D_MODEL), lambda i: (0, 0)), rowt] + [hbm] * n,
        out_specs=(rowt, pl.BlockSpec((1, D_MODEL), lambda i: (0, 0))) + (hbm,) * n,
        scratch_shapes=blocks + blocks + [pltpu.SemaphoreType.DMA((3 * n,)), pltpu.SemaphoreType.DMA((3 * n,)),
                                          pltpu.SemaphoreType.DMA((n,))],
        compiler_params=_cparams("arbitrary"),
    )(dqkv_f, dqkv_b, dp_gates, dp_ch, dlr_f, dlr_b, w_nat, x2d, norm_g, dx2, *sums)


def _weight_grad_out(y_t, dx2b, tk, riding):
    m, seq = y_t.shape
    n = dx2b.shape[1]
    nk = seq // tk

    def body(a_ref, b_ref, ride_in, o_ref, ride_out, send_sems, recv_sems):
        k = pl.program_id(0)

        @pl.when(k == 0)
        def _():
            _start_all(_sibling_copies(ride_in, ride_out, send_sems, recv_sems))
            o_ref[...] = jnp.zeros(o_ref.shape, F32)

        o_ref[...] += _dot(a_ref[...], b_ref[...])

        @pl.when(k == nk - 1)
        def _():
            _wait_all(_sibling_copies(ride_in, ride_out, send_sems, recv_sems))

    hbm = pl.BlockSpec(memory_space=pl.ANY)
    return pl.pallas_call(
        body, name="wgrad_out",
        out_shape=(jax.ShapeDtypeStruct((m, n), F32), jax.ShapeDtypeStruct((4,) + _block_shape(riding), F32)),
        grid=(nk,),
        in_specs=[pl.BlockSpec((m, tk), lambda k: (0, k)), pl.BlockSpec((tk, n), lambda k: (k, 0)), hbm],
        out_specs=(pl.BlockSpec((m, n), lambda k: (0, 0)), hbm),
        scratch_shapes=[pltpu.SemaphoreType.DMA((4,)), pltpu.SemaphoreType.DMA((4,))],
        compiler_params=_cparams("arbitrary"),
    )(y_t, dx2b, riding)


def _weight_grad_in(h_t, dqkv_f, dqkv_b, dp_gates, dp_ch, dlr_f, dlr_b):
    m, seq = h_t.shape
    tn = 512
    n_qkv, n_gates, n_ch = QKV_W // tn, GATES_W // tn, CH_W // tn
    starts = ([k * tn for k in range(n_qkv)] + [NAT_ZA, NAT_ZA + tn, NAT_B, NAT_B + tn, NAT_ZC, NAT_ZC + tn]
              + [NAT_C + k * tn for k in range(n_ch)])

    def out_row(j):
        row = 0
        for k, start in enumerate(starts):
            row = row + jnp.where(j == k, start // 32, 0)
        return pl.multiple_of(row * 32, 32), 0

    def body(a_ref, bqf, bqb, bg, bc, dlf, dlb, o_ref, lr_ref, acc, bq):
        j = pl.program_id(0)

        @pl.when(j == 0)
        def _():
            acc[:, 0:LR_W] = _dot(a_ref[...], (dlf[...] + dlb[...]).astype(BF16))
            lr_ref[...] = acc[:, 0:LR_W].T[0:2 * RANK, :]

        @pl.when(j < n_qkv)
        def _():
            bq[...] = _both_directions(bqf, bqb)
            acc[...] = _dot(a_ref[...], bq[...])

        @pl.when(jnp.logical_and(j >= n_qkv, j < n_qkv + n_gates))
        def _():
            acc[...] = _dot(a_ref[...], bg[...])

        @pl.when(j >= n_qkv + n_gates)
        def _():
            acc[...] = _dot(a_ref[...], bc[...])

        o_ref[...] = acc[...].T

    resident = lambda shape: pl.BlockSpec(shape, lambda j: (0, 0), pipeline_mode=pl.Buffered(1))
    seg = lambda first, count: pl.BlockSpec((seq, tn), lambda j: (0, jnp.clip(j - first, 0, count - 1)))
    main, lr_rows = pl.pallas_call(
        body, name="wgrad_in",
        out_shape=(jax.ShapeDtypeStruct((IN_W, m), F32), jax.ShapeDtypeStruct((2 * RANK, m), F32)),
        grid=(n_qkv + n_gates + n_ch,),
        in_specs=[resident((m, seq)), seg(0, n_qkv), seg(0, n_qkv), seg(n_qkv, n_gates), seg(n_qkv + n_gates, n_ch),
                  resident((seq, LR_W)), resident((seq, LR_W))],
        out_specs=(pl.BlockSpec((pl.Element(tn), pl.Element(m)), out_row),
                   pl.BlockSpec((2 * RANK, m), lambda j: (0, 0))),
        scratch_shapes=[pltpu.VMEM((m, tn), F32), pltpu.VMEM((seq, tn), BF16)],
        compiler_params=_cparams("arbitrary"),
    )(h_t, dqkv_f, dqkv_b, dp_gates, dp_ch, dlr_f, dlr_b)
    return lax.dynamic_update_slice(main, lr_rows, (NAT_LR, 0))


def _pad_rows(a, rows):
    return jnp.pad(a, ((0, rows - a.shape[0]), (0, 0)))


def _rows128(a):
    a = a.reshape(-1, 128)
    return _pad_rows(a, -(-a.shape[0] // 8) * 8)


def _pack(arrs):
    return jnp.concatenate([_rows128(a) for a in arrs], axis=0)


def _unpack(buf, like):
    out, start = [], 0
    for a in like:
        rows = a.size // 128
        out.append(buf[start:start + rows].reshape(a.shape))
        start += -(-rows // 8) * 8
    return out


def kernel(x, norm_g, w_in, w_gk_f, b_gk_f, w_gk_b, b_gk_b, gla_norm_g, conv_w, conv_b, w_out, final_g, loss_target, m_norm_g, m_w_in, m_w_gk_f, m_b_gk_f, m_w_gk_b, m_b_gk_b, m_gla_norm_g, m_conv_w, m_conv_b, m_w_out, m_final_g, v_norm_g, v_w_in, v_w_gk_f, v_b_gk_f, v_w_gk_b, v_b_gk_b, v_gla_norm_g, v_conv_w, v_conv_b, v_w_out, v_final_g):
    px, py, pc = _position()
    me = _blk(px, py, pc)
    seq = x.shape[1]
    x2d, tgt = x[0], loss_target[0]
    tt = min(256, seq)

    small_s = jnp.concatenate([jnp.concatenate([w_gk_f[0], w_gk_b[0]], axis=1), _pad_rows(conv_w[0], 8)], axis=0)
    order = sum(jnp.where(2 * px + py == k, jnp.asarray(tiles + (0,), jnp.int32), 0) for k, tiles in enumerate(TILE_ORDER))
    proj, lr, h_t, w_nat, wout_all, small_all = _gather_inproj(x2d, norm_g, w_in[0].T, w_out[0], small_s, order,
                                                               min(1024, seq))
    w_out_full = wout_all.reshape(MIX_W, D_MODEL)
    wgk_cols = 512 // N_DEV
    wgk_f_full = small_all[:, 0:RANK, 0:wgk_cols].transpose(1, 0, 2).reshape(RANK, QK_W)
    wgk_b_full = small_all[:, 0:RANK, wgk_cols:2 * wgk_cols].transpose(1, 0, 2).reshape(RANK, QK_W)
    conv_w_full = _pad_rows(small_all[:, RANK:RANK + 3, :].transpose(1, 0, 2).reshape(3, CONV_W), 8)
    zr = lambda n: jnp.zeros((n, QK_W), F32)
    wgk_f_pad = jnp.concatenate([wgk_f_full, zr(LR_W - RANK)], axis=0).astype(BF16)
    wgk_b_pad = jnp.concatenate([zr(RANK), wgk_b_full, zr(LR_W - 2 * RANK)], axis=0).astype(BF16)

    o_f, o_b, st_f, st_b = _gla_fwd(proj, lr, wgk_f_pad, wgk_b_pad, b_gk_f, b_gk_b, tt)
    tmix = min(512, seq)
    y_t, conv, dx2, dx2b, loss_p, dfg_p = _mix_out_loss(o_f, o_b, proj, x2d, tgt, gla_norm_g, conv_w_full, conv_b,
                                                        w_out_full, final_g.reshape(1, D_MODEL), tmix)

    dp_gates, do, dconv, dgg_p, dcb_p = _mix_bwd(dx2b, o_f, o_b, proj, conv, gla_norm_g, w_out_full, tmix)
    dp_ch, dcw_p = _conv_bwd(dconv, proj, conv_w_full, tmix)
    dqkv_f, dlr_f, dqkv_b, dlr_b, dwf_p, dwb_p, dbf_p, dbb_p = _gla_bwd(
        proj, lr, do, st_f, st_b, wgk_f_pad, wgk_b_pad, b_gk_f, b_gk_b, tt)
    dw_nat = _weight_grad_in(h_t, dqkv_f, dqkv_b, dp_gates, dp_ch, dlr_f, dlr_b)

    dw_out, sib_in = _weight_grad_out(y_t, dx2b, min(1024, seq), dw_nat)
    part_out = dw_out.reshape(N_DEV, MIX_W // N_DEV, D_MODEL)
    core = jnp.reshape(pc, (1,)).astype(jnp.int32)
    chip = jnp.reshape(2 * px + py, (1,)).astype(jnp.int32)
    sums_in, sib_out = _chip_sums(dw_nat, sib_in, core, D_MODEL, "chip_sums_in", riding=part_out)
    sums_out = _chip_sums(part_out, sib_out, core, D_MODEL, "chip_sums_out")
    grad_x2d, dng_p, far_in, far_out = _input_grad(dqkv_f, dqkv_b, dp_gates, dp_ch, dlr_f, dlr_b, w_nat, x2d, norm_g, dx2,
                                                   [sums_in, sums_out], min(256, seq))
    pieces = [dng_p, dbf_p, dbb_p, dgg_p, dcb_p, dfg_p[0], dwf_p[0:RANK], dwb_p[RANK:2 * RANK], dcw_p[0:3], loss_p[0]]
    g_window, small_tot = _final_sum(sums_in, far_in, chip, _pack(pieces), 512, "final_sum_in")
    g_in_t = lax.dynamic_slice_in_dim(g_window, 4 * pc, SHARD_W, axis=0)
    g_w_out, d_w_out, nm_w_out, nv_w_out = _final_sum_adamw(sums_out, far_out, chip, w_out[0], m_w_out[0], v_w_out[0],
                                                            256, "adamw_out")
    flat = lambda a: a[0].T.reshape(SHARD_W, D_MODEL // 128, 128)
    unflat = lambda a: a.reshape(SHARD_W, D_MODEL).T
    d_flat, m_flat, v_flat = _adamw_rows(g_in_t.reshape(SHARD_W, D_MODEL // 128, 128), flat(w_in), flat(m_w_in),
                                         flat(v_w_in), 450, "adamw_in")
    g_w_in, d_w_in, nm_w_in, nv_w_in = g_in_t.T, unflat(d_flat), unflat(m_flat), unflat(v_flat)

    tot = _unpack(small_tot, pieces)
    g_norm_g, g_b_gk_f, g_b_gk_b, g_gla, g_conv_b, g_final = tot[:6]
    g_wgk_f = lax.dynamic_slice_in_dim(tot[6], me * wgk_cols, wgk_cols, axis=1)[None]
    g_wgk_b = lax.dynamic_slice_in_dim(tot[7], me * wgk_cols, wgk_cols, axis=1)[None]
    g_conv_w = lax.dynamic_slice_in_dim(tot[8], me * 128, 128, axis=1)[None]
    loss = tot[9][0]

    small_g = [g_norm_g, g_b_gk_f, g_b_gk_b, g_gla, g_conv_b, g_final, g_wgk_f, g_wgk_b, g_conv_w]
    small_w = [norm_g, b_gk_f, b_gk_b, gla_norm_g, conv_b, final_g, w_gk_f, w_gk_b, conv_w]
    small_m = [m_norm_g, m_b_gk_f, m_b_gk_b, m_gla_norm_g, m_conv_b, m_final_g, m_w_gk_f, m_w_gk_b, m_conv_w]
    small_v = [v_norm_g, v_b_gk_f, v_b_gk_b, v_gla_norm_g, v_conv_b, v_final_g, v_w_gk_f, v_w_gk_b, v_conv_w]
    d_s, m_s, v_s = _adamw_small(_pack(small_g), _pack(small_w), _pack(small_m), _pack(small_v))
    d_l, m_l, v_l = _unpack(d_s, small_w), _unpack(m_s, small_w), _unpack(v_s, small_w)

    def ordered(sm, big_in, big_out):
        return [sm[0], big_in[None], sm[6], sm[1], sm[7], sm[2], sm[3], sm[8], sm[4], big_out[None], sm[5]]

    grads = ordered(small_g, g_w_in, g_w_out)
    deltas = ordered(d_l, d_w_in, d_w_out)
    new_m = ordered(m_l, nm_w_in, nm_w_out)
    new_v = ordered(v_l, nv_w_in, nv_w_out)
    return (loss, grad_x2d[None], *grads, *deltas, *new_m, *new_v)
```

```python
import functools

import jax
import jax.numpy as jnp
from jax import lax
from jax.experimental import pallas as pl
from jax.experimental.pallas import tpu as pltpu

F32 = jnp.float32
BF16 = jnp.bfloat16
MESH = pl.DeviceIdType.MESH

N_DEV = 8
D_MODEL = 1024
HEADS = 4
DK = 128
DV = 256
QK_W = HEADS * DK
V_W = HEADS * DV
CONV_W = 1024
MIX_W = V_W + CONV_W
CHUNK = 64
RANK = 16
IN_W = 7200
SHARD_W = IN_W // N_DEV
MAIN_W = 7168
LR_W = 128
OFF_Q, OFF_K, OFF_V, OFF_ZA, OFF_B, OFF_ZC, OFF_C, OFF_H = 0, 512, 1024, 2048, 3072, 4096, 5120, 6144
QKV_W, GATES_W, CH_W = 2048, 3072, 2048
NAT_ZA, NAT_LR, NAT_B, NAT_C, NAT_ZC = 2048, 3072, 3104, 4128, 6176
EPS = 1e-6
GATE_SCALE = 1.0 / 16.0
QSCALE = DK ** -0.5
REF_F, LAST_F = CHUNK // 2, CHUNK - 1
REF_B, LAST_B = CHUNK - 1 - CHUNK // 2, 0

ADAM_LR = 0.001
ADAM_B1 = 0.9
ADAM_B2 = 0.999
ADAM_EPS = 1e-08
ADAM_WD = 0.01
ADAM_STEP = 10

VMEM_LIMIT = 56 * 1024 * 1024


def _cparams(*sem):
    return pltpu.CompilerParams(dimension_semantics=sem, vmem_limit_bytes=VMEM_LIMIT)


def _dot(a, b):
    return jnp.dot(a, b, preferred_element_type=F32)


def _dot_nt(a, b):
    return lax.dot_general(a, b, (((1,), (1,)), ((), ())), preferred_element_type=F32)


def _dot_tn(a, b):
    return lax.dot_general(a, b, (((0,), (0,)), ((), ())), preferred_element_type=F32)


def _sigmoid(z):
    return jax.nn.sigmoid(z)


def _position():
    return lax.axis_index("x"), lax.axis_index("y"), lax.axis_index("c")


def _blk(px, py, pc):
    return 4 * px + 2 * py + pc


EDGE = 16
SHIFTED_ROWS = 912
BODY_ROWS = SHIFTED_ROWS - 2 * EDGE


def _first_tile_row(blk, px):
    return EDGE * (56 * blk + px)


def _edge_tiles():
    tiles = {}
    for blk in range(N_DEV):
        first = _first_tile_row(blk, blk // 4)
        tiles.setdefault(first, []).append((blk, 0))
        tiles.setdefault(first + EDGE + BODY_ROWS, []).append((blk, 1))
    return tiles


def _peer_copies(srcs, outs, send_sems, recv_sems):
    x, y, c = _position()
    me = _blk(x, y, c)
    copies = []
    for a, (src, out) in enumerate(zip(srcs, outs)):
        k = 0
        for dx in (0, 1):
            for dy in (0, 1):
                for dc in (0, 1):
                    if dx + dy + dc == 0:
                        continue
                    peer = (1 - x if dx else x, 1 - y if dy else y, 1 - c if dc else c)
                    copies.append(pltpu.make_async_remote_copy(
                        src_ref=src, dst_ref=out.at[me], send_sem=send_sems.at[a * 7 + k],
                        recv_sem=recv_sems.at[a * 7 + k], device_id=peer, device_id_type=MESH))
                    k += 1
    return copies


def _route_chips():
    x, y, c = _position()
    along_x = c == 0
    return [(jnp.where(along_x, 1 - x, x), jnp.where(along_x, y, 1 - y)),
            (jnp.where(along_x, x, 1 - x), jnp.where(along_x, 1 - y, y)), (1 - x, 1 - y)]


WINDOW_ROWS = SHARD_W + 4


def _window_start(k, parity):
    return 2 * SHARD_W * k + (SHARD_W - 4) * parity


def _owner_block(part, k, parity):
    if part.ndim == 3:
        return part.at[2 * k + parity]
    return part.at[pl.ds(pl.multiple_of(_window_start(k, parity), 8), WINDOW_ROWS)]


def _block_shape(part):
    return part.shape[1:] if part.ndim == 3 else (WINDOW_ROWS, part.shape[1])


def _sibling_copies(part, out, send_sems, recv_sems):
    x, y, c = _position()
    return [pltpu.make_async_remote_copy(src_ref=_owner_block(part, k, 1 - c), dst_ref=out.at[k],
                                         send_sem=send_sems.at[k], recv_sem=recv_sems.at[k],
                                         device_id=(x, y, 1 - c), device_id_type=MESH)
            for k in range(4)]


def _start_all(copies):
    for cp in copies:
        cp.start()


def _wait_all(copies):
    for cp in copies:
        cp.wait_recv()
    for cp in copies:
        cp.wait_send()


def _chip_sums(part, from_sibling, core, tc, name, riding=None):
    rows, cols = _block_shape(part)
    nj = cols // tc

    def body(core_ref, p_ref, s_ref, *rest):
        if riding is None:
            (o_ref,) = rest
        else:
            ride_in, o_ref, ride_out, send_sems, recv_sems = rest
            k, j = pl.program_id(0), pl.program_id(1)

            @pl.when(jnp.logical_and(k == 0, j == 0))
            def _():
                _start_all(_sibling_copies(ride_in, ride_out, send_sems, recv_sems))

        o_ref[0] = (p_ref[...].reshape(rows, tc) + s_ref[0]).astype(BF16)

        if riding is not None:
            @pl.when(jnp.logical_and(k == 3, j == nj - 1))
            def _():
                _wait_all(_sibling_copies(ride_in, ride_out, send_sems, recv_sems))

    hbm = pl.BlockSpec(memory_space=pl.ANY)
    sums = jax.ShapeDtypeStruct((4, rows, cols), BF16)
    tile_out = pl.BlockSpec((1, rows, tc), lambda k, j, core_ref: (k, 0, j))
    if part.ndim == 3:
        mine = pl.BlockSpec((1, rows, tc), lambda k, j, core_ref: (2 * k + core_ref[0], 0, j))
    else:
        mine = pl.BlockSpec((pl.Element(rows), pl.Element(tc)),
                            lambda k, j, core_ref: (pl.multiple_of(_window_start(k, core_ref[0]), 8),
                                                    pl.multiple_of(j * tc, 128)))
    in_specs = [mine, pl.BlockSpec((1, rows, tc), lambda k, j, core_ref: (k, 0, j))]
    if riding is None:
        out_shape, out_specs, scratch, args = sums, tile_out, [], (core, part, from_sibling)
    else:
        out_shape = (sums, jax.ShapeDtypeStruct((4,) + _block_shape(riding), F32))
        out_specs, in_specs = (tile_out, hbm), in_specs + [hbm]
        scratch = [pltpu.SemaphoreType.DMA((4,)), pltpu.SemaphoreType.DMA((4,))]
        args = (core, part, from_sibling, riding)
    return pl.pallas_call(
        body, name=name, out_shape=out_shape,
        grid_spec=pltpu.PrefetchScalarGridSpec(num_scalar_prefetch=1, grid=(4, nj), in_specs=in_specs,
                                               out_specs=out_specs, scratch_shapes=scratch),
        compiler_params=_cparams("arbitrary", "arbitrary"),
    )(*args)


def _sum_chips(s_ref, r_ref):
    f = lambda a: a.astype(F32)
    return (f(s_ref[0]) + f(r_ref[0])) + f(r_ref[1])


def _final_sum(sums, from_chips, chip, small, tc, name):
    _, rows, cols = sums.shape
    nj = cols // tc

    def body(chip_ref, s_ref, r_ref, sm_ref, g_out, tot_ref, all_ref, send_sems, recv_sems):
        j = pl.program_id(0)
        me = _blk(*_position())

        @pl.when(j == 0)
        def _():
            all_ref[me] = sm_ref[...]
            _start_all(_peer_copies((all_ref.at[me],), (all_ref,), send_sems, recv_sems))

        g_out[...] = _sum_chips(s_ref, r_ref)

        @pl.when(j == nj - 1)
        def _():
            _wait_all(_peer_copies((all_ref.at[me],), (all_ref,), send_sems, recv_sems))
            acc = all_ref[0]
            for d in range(1, N_DEV):
                acc = acc + all_ref[d]
            tot_ref[...] = acc

    whole = pl.BlockSpec(small.shape, lambda j, chip_ref: (0, 0))
    return pl.pallas_call(
        body, name=name,
        out_shape=(jax.ShapeDtypeStruct((rows, cols), F32), jax.ShapeDtypeStruct(small.shape, F32)),
        grid_spec=pltpu.PrefetchScalarGridSpec(
            num_scalar_prefetch=1, grid=(nj,),
            in_specs=[pl.BlockSpec((1, rows, tc), lambda j, chip_ref: (chip_ref[0], 0, j)),
                      pl.BlockSpec((2, rows, tc), lambda j, chip_ref: (0, 0, j)), whole],
            out_specs=(pl.BlockSpec((rows, tc), lambda j, chip_ref: (0, j)), whole),
            scratch_shapes=[pltpu.VMEM((N_DEV,) + small.shape, F32), pltpu.SemaphoreType.DMA((7,)),
                            pltpu.SemaphoreType.DMA((7,))]),
        compiler_params=_cparams("arbitrary"),
    )(chip, sums, from_chips, small)


def _adamw_rows(g, w, m, v, tr, name):
    rows = g.shape[0]

    def body(g_ref, w_ref, m_ref, v_ref, d_out, m_out, v_out):
        delta, m_new, v_new = _adamw(w_ref[...], g_ref[...], m_ref[...], v_ref[...])
        d_out[...] = delta
        m_out[...] = m_new
        v_out[...] = v_new

    tile = pl.BlockSpec((tr,) + g.shape[1:], lambda r: (r, 0, 0))
    shp = jax.ShapeDtypeStruct(g.shape, F32)
    return pl.pallas_call(
        body, name=name, out_shape=(shp, shp, shp), grid=(rows // tr,),
        in_specs=[tile] * 4, out_specs=(tile, tile, tile),
        compiler_params=_cparams("arbitrary"),
    )(g, w, m, v)


def _adamw(w, g, m, v):
    m = ADAM_B1 * m + (1.0 - ADAM_B1) * g
    v = ADAM_B2 * v + (1.0 - ADAM_B2) * (g * g)
    m_hat = m / (1.0 - ADAM_B1 ** ADAM_STEP)
    v_hat = v / (1.0 - ADAM_B2 ** ADAM_STEP)
    delta = -ADAM_LR * (m_hat / (jnp.sqrt(v_hat) + ADAM_EPS) + ADAM_WD * w)
    return delta, m, v


def _final_sum_adamw(sums, from_chips, chip, w, m, v, tr, name):
    rows, cols = w.shape

    def body(chip_ref, s_ref, r_ref, w_ref, m_ref, v_ref, g_out, d_out, m_out, v_out):
        g = _sum_chips(s_ref, r_ref)
        delta, m_new, v_new = _adamw(w_ref[...], g, m_ref[...], v_ref[...])
        g_out[...] = g
        d_out[...] = delta
        m_out[...] = m_new
        v_out[...] = v_new

    tile = pl.BlockSpec((tr, cols), lambda r, chip_ref: (r, 0))
    shp = jax.ShapeDtypeStruct((rows, cols), F32)
    return pl.pallas_call(
        body, name=name,
        out_shape=(shp, shp, shp, shp),
        grid_spec=pltpu.PrefetchScalarGridSpec(
            num_scalar_prefetch=1, grid=(rows // tr,),
            in_specs=[pl.BlockSpec((1, tr, cols), lambda r, chip_ref: (chip_ref[0], r, 0)),
                      pl.BlockSpec((2, tr, cols), lambda r, chip_ref: (0, r, 0)),
                      tile, tile, tile],
            out_specs=(tile, tile, tile, tile)),
        compiler_params=_cparams("arbitrary"),
    )(chip, sums, from_chips, w, m, v)


def _adamw_small(g, w, m, v):
    def body(g_ref, w_ref, m_ref, v_ref, d_out, m_out, v_out):
        delta, m_new, v_new = _adamw(w_ref[...], g_ref[...], m_ref[...], v_ref[...])
        d_out[...] = delta
        m_out[...] = m_new
        v_out[...] = v_new

    vmem = pl.BlockSpec(memory_space=pltpu.VMEM)
    shp = jax.ShapeDtypeStruct(g.shape, F32)
    return pl.pallas_call(body, name="adamw_small", out_shape=(shp, shp, shp),
                          in_specs=[vmem] * 4, out_specs=(vmem, vmem, vmem))(g, w, m, v)


TILE_ROWS = (0, 1024, NAT_ZA, NAT_B, NAT_ZC, NAT_C, NAT_C + CONV_W)


TILE_ORDER = ((0, 1, 2, 3, 5, 6, 4), (2, 1, 0, 4, 3, 5, 6), (5, 6, 0, 4, 1, 2, 3), (4, 6, 2, 3, 5, 0, 1))
EARLY_SWEEP, NEIGHBOUR_SWEEP, DIAGONAL_SWEEP = 1, 2, 4
PIECES, W_IN_PIECES, OTHER_PIECES = 4, (0, 1), (2, 3)


def _gather_inproj(x2d, norm_g, shard_t, w_out_s, small_s, order, tm):
    seq = x2d.shape[0]
    tn = CONV_W
    ni, nj = seq // tm, MAIN_W // tn
    first_sweep = lambda j, i, order_ref: jnp.where(j == 0, i, ni - 1)
    last_sweep = lambda j, i, order_ref: jnp.where(j == nj - 1, i, 0)
    edge_tiles = _edge_tiles()

    def body(order_ref, x_ref, g_ref, shard_ref, wout_ref, sm_ref, proj_ref, lr_ref, ht_ref, w_nat, wout_all, sm_all,
             w_all, h_all, edges, stage, wout_b, sm_b, send_sems, recv_sems, local_sems):
        j, i = pl.program_id(0), pl.program_id(1)
        rows = pl.ds(pl.multiple_of(i * tm, tm), tm)
        x, y, c = _position()
        me, here, sibling = _blk(x, y, c), (x, y, c), (x, y, 1 - c)
        chips = _route_chips()
        sibling_chips = [chips[1], chips[0], chips[2]]

        def pieces(px, py, pc):
            blk = _blk(px, py, pc)
            body_rows = pl.ds(pl.multiple_of(_first_tile_row(blk, px) + EDGE, EDGE), BODY_ROWS)
            return [w_all.at[body_rows], edges.at[blk], wout_all.at[blk], sm_all.at[blk]]

        def copy(a, k, block, to, staged=None):
            ref = pieces(*block)[a]
            return pltpu.make_async_remote_copy(src_ref=ref if staged is None else staged, dst_ref=ref,
                                                send_sem=send_sems.at[a * 7 + k], recv_sem=recv_sems.at[a * 7 + k],
                                                device_id=to, device_id_type=MESH)

        def own_copies(group, slots=(0, 1, 2)):
            targets = [sibling] + [(*chips[n], c) for n in range(2)]
            staged = [None, None, wout_b, sm_b]
            return [copy(a, k, here, targets[k], staged[a]) for k in slots for a in group]

        def relays(group):
            return [copy(a, 3, (*chips[0], c), (*chips[1], c)) for a in group]

        def forwards(n, group):
            return [copy(a, 4 + n, (*chips[n], c), sibling) for a in group]

        def keep_own():
            return [pltpu.make_async_copy(wout_b, wout_all.at[me], local_sems.at[0]),
                    pltpu.make_async_copy(sm_b, sm_all.at[me], local_sems.at[1])]

        def keep_weight():
            return pltpu.make_async_copy(w_all, w_nat, local_sems.at[2])

        def take(ns, group, relay=True):
            for n in ns:
                for a in group:
                    copy(a, 1 + n, (*chips[n], c), here).wait_recv()
                _start_all((relays(group) if n == 0 and relay else []) + forwards(n, group))

        def take_passed_on(ns, group):
            for n in ns:
                for a in group:
                    copy(a, 4 + n, (*sibling_chips[n], 1 - c), here).wait_recv()

        def arrive(ns, group):
            take(ns, group)
            take_passed_on(ns, group)

        def per_core_and_row(step):
            for core in range(2):
                for row in range(2):
                    pl.when(jnp.logical_and(c == core, y == row))(functools.partial(step, core, row))

        def start_own(core, row):
            now = (0, 1 + core) if core == row else (0, 2 - core, 1 + core)
            _start_all(own_copies(W_IN_PIECES, now))
            wout_b[...] = wout_ref[...].astype(BF16)
            sm_b[...] = sm_ref[...]
            _start_all(own_copies(OTHER_PIECES, now) + keep_own())

        def take_early(core, row):
            if core == row:
                _start_all(own_copies(W_IN_PIECES, (2 - core,)) + own_copies(OTHER_PIECES, (2 - core,)))
                take((1 - core,), W_IN_PIECES, relay=False)
            else:
                take_passed_on((core,), W_IN_PIECES)

        def take_neighbours(core, row):
            if core == row:
                _start_all(relays(W_IN_PIECES) if core == 1 else [])
                take((core,), W_IN_PIECES)
                take_passed_on((0, 1), W_IN_PIECES)
            else:
                take((0, 1), W_IN_PIECES)
                take_passed_on((1 - core,), W_IN_PIECES)

        early_blk = _blk(x, 1 - y, y)

        def add_edge_tiles(stage):
            for row, parts in edge_tiles.items():
                ready = 0
                for blk, _ in parts:
                    away = (x != blk // 4).astype(jnp.int32) + (y != (blk // 2) % 2).astype(jnp.int32)
                    late = jnp.where(away == 1, jnp.where(early_blk == blk, 1, 2), jnp.where(away == 2, 3 + blk % 2, 0))
                    ready = jnp.maximum(ready, late)

                @pl.when(ready == stage)
                def _(row=row, parts=parts):
                    tile = edges[parts[0][0], parts[0][1]].astype(F32)
                    for blk, side in parts[1:]:
                        tile = tile + edges[blk, side].astype(F32)
                    w_all[row:row + EDGE, :] = tile.astype(BF16)

        @pl.when(jnp.logical_and(j == 0, i == 0))
        def _():
            last = SHARD_W // 8 * 8
            for col in range(0, D_MODEL, 128):
                cols = slice(col, col + 128)
                stage[0:last, :] = shard_ref[0:last, cols]
                stage[last:, :] = jnp.zeros((SHIFTED_ROWS - last, 128), F32)
                stage[last:SHARD_W, :] = shard_ref[last:SHARD_W, cols]
                for k in range(EDGE // 4):
                    @pl.when(me % 4 == k)
                    def _(k=k, cols=cols):
                        moved = pltpu.roll(stage[...], 4 * k, 0) if k else stage[...]
                        pieces(*here)[0][:, cols] = moved[EDGE:EDGE + BODY_ROWS].astype(BF16)
                        edges[me, 0, :, cols] = moved[0:EDGE].astype(BF16)
                        edges[me, 1, :, cols] = moved[EDGE + BODY_ROWS:].astype(BF16)
            per_core_and_row(start_own)
            for a in W_IN_PIECES:
                copy(a, 0, sibling, here).wait_recv()
            add_edge_tiles(0)

        @pl.when(jnp.logical_and(j == EARLY_SWEEP, i == 0))
        def _():
            per_core_and_row(take_early)
            add_edge_tiles(1)

        @pl.when(jnp.logical_and(j == NEIGHBOUR_SWEEP, i == 0))
        def _():
            per_core_and_row(take_neighbours)
            add_edge_tiles(2)

        for core in range(2):
            @pl.when(jnp.logical_and(j == DIAGONAL_SWEEP + core, i == 0))
            def _(core=core):
                pl.when(c == core)(lambda: take((2,), W_IN_PIECES))
                pl.when(c != core)(lambda: take_passed_on((2,), W_IN_PIECES))
                add_edge_tiles(3 + core)
                if core == 0:
                    arrive((0, 1), OTHER_PIECES)
                else:
                    keep_weight().start()

        @pl.when(jnp.logical_and(j == nj - 1, i == 0))
        def _():
            arrive((2,), OTHER_PIECES)

        @pl.when(j == 0)
        def _():
            xv = x_ref[...]
            r = lax.rsqrt(jnp.mean(xv * xv, axis=-1, keepdims=True) + EPS)
            h = (xv * r) * g_ref[...]
            h_all[rows, :] = h.astype(BF16)
            ht_ref[...] = h.T.astype(BF16)

        tile = order_ref[j]
        row = 0
        for k, start in enumerate(TILE_ROWS):
            row = row + jnp.where(tile == k, start // 32, 0)
        w_tile = w_all[pl.ds(pl.multiple_of(row * 32, 32), tn), :]
        proj_ref[...] = _dot_nt(h_all[rows, :], w_tile).astype(BF16)

        @pl.when(j == nj - 1)
        def _():
            lr_ref[...] = _dot_nt(h_all[rows, :], w_all[NAT_LR:NAT_LR + LR_W, :])

        @pl.when(jnp.logical_and(j == nj - 1, i == ni - 1))
        def _():
            everything = range(PIECES)
            passed_on = [cp for n in range(3) for cp in forwards(n, everything)]
            for cp in own_copies(everything) + relays(everything) + passed_on:
                cp.wait_send()
            for a in OTHER_PIECES:
                copy(a, 0, sibling, here).wait_recv()
            for cp in keep_own() + [keep_weight()]:
                cp.wait()

    const = lambda shape: pl.BlockSpec(shape, lambda j, i, order_ref: (0,) * len(shape))
    hbm = pl.BlockSpec(memory_space=pl.ANY)
    vmem = pl.BlockSpec(memory_space=pltpu.VMEM)
    return pl.pallas_call(
        body, name="gather_inproj",
        out_shape=(jax.ShapeDtypeStruct((seq, MAIN_W), BF16), jax.ShapeDtypeStruct((seq, LR_W), F32),
                   jax.ShapeDtypeStruct((D_MODEL, seq), BF16), jax.ShapeDtypeStruct((IN_W, D_MODEL), BF16),
                   jax.ShapeDtypeStruct((N_DEV,) + w_out_s.shape, BF16),
                   jax.ShapeDtypeStruct((N_DEV,) + small_s.shape, F32)),
        grid_spec=pltpu.PrefetchScalarGridSpec(
            num_scalar_prefetch=1, grid=(nj, ni),
            in_specs=[pl.BlockSpec((tm, D_MODEL), lambda j, i, order_ref: (first_sweep(j, i, order_ref), 0)),
                      const((1, D_MODEL)), vmem, vmem, const(small_s.shape)],
            out_specs=(pl.BlockSpec((tm, tn), lambda j, i, order_ref: (i, order_ref[j])),
                       pl.BlockSpec((tm, LR_W), lambda j, i, order_ref: (last_sweep(j, i, order_ref), 0)),
                       pl.BlockSpec((D_MODEL, tm), lambda j, i, order_ref: (0, first_sweep(j, i, order_ref))),
                       hbm, hbm, hbm),
            scratch_shapes=[pltpu.VMEM((IN_W, D_MODEL), BF16), pltpu.VMEM((seq, D_MODEL), BF16),
                            pltpu.VMEM((N_DEV, 2, EDGE, D_MODEL), BF16), pltpu.VMEM((SHIFTED_ROWS, 128), F32),
                            pltpu.VMEM(w_out_s.shape, BF16), pltpu.VMEM(small_s.shape, F32),
                            pltpu.SemaphoreType.DMA((7 * PIECES,)), pltpu.SemaphoreType.DMA((7 * PIECES,)),
                            pltpu.SemaphoreType.DMA((3,))]),
        compiler_params=_cparams("arbitrary", "arbitrary"),
    )(order, x2d, norm_g, shard_t, w_out_s, small_s)


def _block_masks(tt):
    row = lax.broadcasted_iota(jnp.int32, (tt, tt), 0)
    col = lax.broadcasted_iota(jnp.int32, (tt, tt), 1)
    same = jnp.right_shift(row, 6) == jnp.right_shift(col, 6)
    return (jnp.logical_and(same, col <= row), jnp.logical_and(same, col >= row), jnp.logical_and(same, col > row))


def _dot_split3(ones_mat, x):
    x1 = x.astype(BF16)
    r1 = x - x1.astype(F32)
    x2 = r1.astype(BF16)
    x3 = (r1 - x2.astype(F32)).astype(BF16)
    return (_dot(ones_mat, x3) + _dot(ones_mat, x2)) + _dot(ones_mat, x1)


def _log_gate(logits):
    return (jnp.minimum(logits, 0.0) - jnp.log(1.0 + jnp.exp(-jnp.abs(logits)))) * GATE_SCALE


def _chunk_column_mask(tt):
    nc = tt // CHUNK
    row = lax.broadcasted_iota(jnp.int32, (tt, nc * DK), 0)
    col = lax.broadcasted_iota(jnp.int32, (tt, nc * DK), 1)
    return jnp.right_shift(row, 6) == jnp.right_shift(col, 7)


def _chunked(mask, x, nc):
    wide = jnp.concatenate([x] * nc, axis=1)
    return jnp.where(mask, wide, jnp.zeros_like(wide))


def _gla_fwd(proj, lr, wgk_f, wgk_b, bgk_f, bgk_b, tt):
    seq = proj.shape[0]
    nb, nc, nch = seq // tt, tt // CHUNK, seq // CHUNK

    def body(qf, kf, vf, lrf, qb, kb, vb, lrb, wf, wb, bf, bb, of, ob, stf, stb, s_scr, qs_s, ks_s, qin_s, kout_s):
        @pl.when(pl.program_id(0) == 0)
        def _():
            s_scr[...] = jnp.zeros(s_scr.shape, F32)

        low, upp, sup = _block_masks(tt)
        dirs = ((qf, kf, vf, lrf, wf, bf, of, stf, low, low, REF_F, LAST_F, list(range(nc))),
                (qb, kb, vb, lrb, wb, bb, ob, stb, upp, sup, REF_B, LAST_B, list(reversed(range(nc)))))
        for d, (q_r, k_r, v_r, lr_r, w_r, b_r, o_r, st_r, cum, mask, ref, last, order) in enumerate(dirs):
            logits = _dot(lr_r[...].astype(BF16), w_r[...]) + b_r[...]
            b = _dot_split3(cum.astype(BF16), _log_gate(logits))
            decs = []
            for c in range(nc):
                rows = slice(c * CHUNK, (c + 1) * CHUNK)
                bc = b[rows]
                b_ref, b_last = bc[ref:ref + 1], bc[last:last + 1]
                qc = q_r[rows, :].astype(F32) * QSCALE
                kc = k_r[rows, :].astype(F32)
                qs_s[rows, :] = (qc * jnp.exp(bc - b_ref)).astype(BF16)
                ks_s[rows, :] = (kc * jnp.exp(b_ref - bc)).astype(BF16)
                qin_s[rows, :] = (qc * jnp.exp(bc)).astype(BF16)
                kout_s[rows, :] = (kc * jnp.exp(b_last - bc)).astype(BF16)
                decs.append(jnp.exp(b_last))
            for h in range(HEADS):
                ksl = slice(h * DK, (h + 1) * DK)
                vsl = slice(h * DV, (h + 1) * DV)
                v = v_r[:, vsl].astype(BF16)
                att = jnp.where(mask, _dot_nt(qs_s[:, ksl], ks_s[:, ksl]), 0.0).astype(BF16)
                o_intra = _dot(att, v)
                st = s_scr[d * HEADS + h]
                for c in order:
                    rows = slice(c * CHUNK, (c + 1) * CHUNK)
                    stb = st.astype(BF16)
                    st_r[c, h] = stb
                    o_r[rows, vsl] = (o_intra[rows] + _dot_nt(qin_s[rows, ksl], stb)).astype(BF16)
                    st = st * decs[c][:, ksl] + _dot_tn(v[rows], kout_s[rows, ksl])
                s_scr[d * HEADS + h] = st

    fw = lambda i: (i, 0)
    bw = lambda i: (nb - 1 - i, 0)
    const = lambda i: (0, 0)

    def tok_specs(m):
        return [pl.BlockSpec((tt, QK_W), lambda i: (m(i)[0], OFF_Q // QK_W)),
                pl.BlockSpec((tt, QK_W), lambda i: (m(i)[0], OFF_K // QK_W)),
                pl.BlockSpec((tt, V_W), lambda i: (m(i)[0], OFF_V // V_W)),
                pl.BlockSpec((tt, LR_W), m)]

    st_shape = jax.ShapeDtypeStruct((nch, HEADS, DV, DK), BF16)
    o_shape = jax.ShapeDtypeStruct((seq, V_W), BF16)
    operand = pltpu.VMEM((tt, QK_W), BF16)
    return pl.pallas_call(
        body, name="gla_fwd",
        out_shape=(o_shape, o_shape, st_shape, st_shape),
        grid=(nb,),
        in_specs=tok_specs(fw) + tok_specs(bw) + [
            pl.BlockSpec((LR_W, QK_W), const), pl.BlockSpec((LR_W, QK_W), const),
            pl.BlockSpec((1, QK_W), const), pl.BlockSpec((1, QK_W), const)],
        out_specs=(pl.BlockSpec((tt, V_W), fw), pl.BlockSpec((tt, V_W), bw),
                   pl.BlockSpec((nc, HEADS, DV, DK), lambda i: (i, 0, 0, 0)),
                   pl.BlockSpec((nc, HEADS, DV, DK), lambda i: (nb - 1 - i, 0, 0, 0))),
        scratch_shapes=[pltpu.VMEM((2 * HEADS, DV, DK), F32), operand, operand, operand, operand],
        compiler_params=_cparams("arbitrary"),
    )(proj, proj, proj, lr, proj, proj, proj, lr, wgk_f, wgk_b, bgk_f, bgk_b)


def _head_norm(o, gain):
    outs, rinv = [], []
    for h in range(HEADS):
        oh = o[:, h * DV:(h + 1) * DV]
        r = lax.rsqrt(jnp.mean(oh * oh, axis=-1, keepdims=True) + EPS)
        outs.append((oh * r) * gain)
        rinv.append(r)
    return jnp.concatenate(outs, axis=1), rinv


def _shift_rows(u, prev_row, next_row):
    n = u.shape[0]
    row = lax.broadcasted_iota(jnp.int32, (n, 1), 0)
    up = jnp.where(row == 0, prev_row, pltpu.roll(u, 1, 0))
    un = jnp.where(row == n - 1, next_row, pltpu.roll(u, n - 1, 0))
    return up, un


HALO = 16


def _halo_specs(tm, seq, col_block):
    per = tm // HALO
    last = seq // HALO - 1
    return [pl.BlockSpec((HALO, CONV_W), lambda i: (jnp.maximum(i * per - 1, 0), col_block)),
            pl.BlockSpec((HALO, CONV_W), lambda i: (jnp.minimum((i + 1) * per, last), col_block))]


def _f32(ref):
    return ref[...].astype(F32)


def _last_row(ref):
    return ref[HALO - 1:HALO, :].astype(F32)


def _first_row(ref):
    return ref[0:1, :].astype(F32)


def _mix_out_loss(o_f, o_b, proj, x2d, tgt, gla_g, conv_w, conv_b, w_out, final_g, tm):
    seq = x2d.shape[0]
    nt = seq // tm

    def body(of, ob, za, bg, cg, hc, zc, cprev, cnext, hprev, hnext, x_ref, t_ref, gg, cw, cb, wo, fg,
             yt_ref, conv_ref, dx2_ref, dx2b_ref, loss_ref, dfg_ref):
        i = pl.program_id(0)

        @pl.when(i == 0)
        def _():
            loss_ref[...] = jnp.zeros(loss_ref.shape, F32)
            dfg_ref[...] = jnp.zeros(dfg_ref.shape, F32)

        on, _ = _head_norm(_f32(of) + _f32(ob), gg[...])
        zav = _f32(za)
        y_a = on * (zav * _sigmoid(zav))
        u = _f32(cg) * _f32(hc)
        prev_row = jnp.where(i > 0, _last_row(cprev) * _last_row(hprev), 0.0)
        next_row = jnp.where(i < nt - 1, _first_row(cnext) * _first_row(hnext), 0.0)
        up, un = _shift_rows(u, prev_row, next_row)
        conv = (cw[0:1, :] * up + cw[1:2, :] * u + cw[2:3, :] * un) + cb[...]
        conv_ref[...] = conv.astype(BF16)
        zcv = _f32(zc)
        y_c = _f32(bg) * conv * (zcv * _sigmoid(zcv))
        y = jnp.concatenate([y_a, y_c], axis=1)
        yt_ref[...] = y.T.astype(BF16)
        x2 = x_ref[...] + _dot(y.astype(BF16), wo[...])
        r = lax.rsqrt(jnp.mean(x2 * x2, axis=-1, keepdims=True) + EPS)
        xn = x2 * r
        err = xn * fg[...] - t_ref[...]
        loss_ref[...] += 0.5 * jnp.sum(jnp.mean(err * err, axis=-1, keepdims=True))
        dyf = err * (1.0 / D_MODEL)
        dfg_ref[...] += jnp.sum(dyf * xn, axis=0, keepdims=True)
        dxn = dyf * fg[...]
        dx2 = r * dxn - xn * (r * jnp.mean(dxn * xn, axis=-1, keepdims=True))
        dx2_ref[...] = dx2
        dx2b_ref[...] = dx2.astype(BF16)

    def col(off):
        return pl.BlockSpec((tm, CONV_W), lambda i: (i, off // CONV_W))

    rowt = pl.BlockSpec((tm, D_MODEL), lambda i: (i, 0))
    const = lambda shape: pl.BlockSpec(shape, lambda i: (0, 0))
    return pl.pallas_call(
        body, name="mix_out_loss",
        out_shape=(jax.ShapeDtypeStruct((MIX_W, seq), BF16), jax.ShapeDtypeStruct((seq, CONV_W), BF16),
                   jax.ShapeDtypeStruct((seq, D_MODEL), F32), jax.ShapeDtypeStruct((seq, D_MODEL), BF16),
                   jax.ShapeDtypeStruct((8, 128), F32), jax.ShapeDtypeStruct((1, D_MODEL), F32)),
        grid=(nt,),
        in_specs=[rowt, rowt, col(OFF_ZA), col(OFF_B), col(OFF_C), col(OFF_H), col(OFF_ZC)]
        + _halo_specs(tm, seq, OFF_C // CONV_W) + _halo_specs(tm, seq, OFF_H // CONV_W)
        + [rowt, rowt, const((1, DV)), const((8, CONV_W)), const((1, CONV_W)), const((MIX_W, D_MODEL)),
           const((1, D_MODEL))],
        out_specs=(pl.BlockSpec((MIX_W, tm), lambda i: (0, i)), rowt, rowt, rowt, const((8, 128)),
                   const((1, D_MODEL))),
        compiler_params=_cparams("arbitrary"),
    )(o_f, o_b, proj, proj, proj, proj, proj, proj, proj, proj, proj, x2d, tgt, gla_g, conv_w, conv_b, w_out, final_g)


def _dsilu(z, s):
    return s * (1.0 + z * (1.0 - s))


def _mix_bwd(dx2b, o_f, o_b, proj, conv, gla_g, w_out, tm):
    seq = dx2b.shape[0]

    def body(dx, of, ob, za, bg, zc, cv, gg, wo, dg_ref, do_ref, dconv_ref, dgg_ref, dcb_ref):
        @pl.when(pl.program_id(0) == 0)
        def _():
            dgg_ref[...] = jnp.zeros(dgg_ref.shape, F32)
            dcb_ref[...] = jnp.zeros(dcb_ref.shape, F32)

        dy = _dot_nt(dx[...], wo[...])
        dy_a, dy_c = dy[:, :V_W], dy[:, V_W:]
        zcv, bgv, convv = _f32(zc), _f32(bg), _f32(cv)
        sc = _sigmoid(zcv)
        szc = zcv * sc
        dg_ref[:, CONV_W:2 * CONV_W] = (dy_c * convv * szc).astype(BF16)
        dconv = dy_c * bgv * szc
        dconv_ref[...] = dconv.astype(BF16)
        dcb_ref[...] += jnp.sum(dconv, axis=0, keepdims=True)
        dg_ref[:, 2 * CONV_W:] = (dy_c * bgv * convv * _dsilu(zcv, sc)).astype(BF16)

        o = _f32(of) + _f32(ob)
        gain = gg[...]
        on, rinv = _head_norm(o, gain)
        zav = _f32(za)
        sa = _sigmoid(zav)
        dg_ref[:, :CONV_W] = (dy_a * on * _dsilu(zav, sa)).astype(BF16)
        don = dy_a * (zav * sa)
        dgg = jnp.zeros((1, DV), F32)
        dos = []
        for h in range(HEADS):
            sl = slice(h * DV, (h + 1) * DV)
            oh, r, dh = o[:, sl], rinv[h], don[:, sl]
            ohn = oh * r
            dgg = dgg + jnp.sum(dh * ohn, axis=0, keepdims=True)
            dn = dh * gain
            dos.append(r * dn - ohn * (r * jnp.mean(dn * ohn, axis=-1, keepdims=True)))
        dgg_ref[...] += dgg
        do_ref[...] = jnp.concatenate(dos, axis=1).astype(BF16)

    def col(off):
        return pl.BlockSpec((tm, CONV_W), lambda i: (i, off // CONV_W))

    rowt = pl.BlockSpec((tm, D_MODEL), lambda i: (i, 0))
    const = lambda shape: pl.BlockSpec(shape, lambda i: (0, 0))
    return pl.pallas_call(
        body, name="mix_bwd",
        out_shape=(jax.ShapeDtypeStruct((seq, GATES_W), BF16), jax.ShapeDtypeStruct((seq, V_W), BF16),
                   jax.ShapeDtypeStruct((seq, CONV_W), BF16),
                   jax.ShapeDtypeStruct((1, DV), F32), jax.ShapeDtypeStruct((1, CONV_W), F32)),
        grid=(seq // tm,),
        in_specs=[rowt, rowt, rowt, col(OFF_ZA), col(OFF_B), col(OFF_ZC), rowt, const((1, DV)),
                  const((MIX_W, D_MODEL))],
        out_specs=(pl.BlockSpec((tm, GATES_W), lambda i: (i, 0)), rowt, rowt, const((1, DV)), const((1, CONV_W))),
        compiler_params=_cparams("arbitrary"),
    )(dx2b, o_f, o_b, proj, proj, proj, conv, gla_g, w_out)


def _conv_bwd(dconv, proj, conv_w, tm):
    seq = dconv.shape[0]
    nt = seq // tm

    def body(dc_in, dprev, dnext, cg, hc, cprev, cnext, hprev, hnext, cw, dch_ref, dcw_ref):
        i = pl.program_id(0)

        @pl.when(i == 0)
        def _():
            dcw_ref[...] = jnp.zeros(dcw_ref.shape, F32)

        first, lastt = i > 0, i < nt - 1
        dcv = _f32(dc_in)
        d_up, d_un = _shift_rows(dcv, jnp.where(first, _last_row(dprev), 0.0), jnp.where(lastt, _first_row(dnext), 0.0))
        cgv, hcv = _f32(cg), _f32(hc)
        u = cgv * hcv
        u_up, u_un = _shift_rows(u, jnp.where(first, _last_row(cprev) * _last_row(hprev), 0.0),
                                 jnp.where(lastt, _first_row(cnext) * _first_row(hnext), 0.0))
        du = cw[0:1, :] * d_un + cw[1:2, :] * dcv + cw[2:3, :] * d_up
        dch_ref[:, :CONV_W] = (du * hcv).astype(BF16)
        dch_ref[:, CONV_W:] = (du * cgv).astype(BF16)
        dcw_ref[0:1, :] += jnp.sum(dcv * u_up, axis=0, keepdims=True)
        dcw_ref[1:2, :] += jnp.sum(dcv * u, axis=0, keepdims=True)
        dcw_ref[2:3, :] += jnp.sum(dcv * u_un, axis=0, keepdims=True)

    def col(off):
        return pl.BlockSpec((tm, CONV_W), lambda i: (i, off // CONV_W))

    rowt = pl.BlockSpec((tm, CONV_W), lambda i: (i, 0))
    const = lambda shape: pl.BlockSpec(shape, lambda i: (0, 0))
    return pl.pallas_call(
        body, name="conv_bwd",
        out_shape=(jax.ShapeDtypeStruct((seq, CH_W), BF16), jax.ShapeDtypeStruct((8, CONV_W), F32)),
        grid=(nt,),
        in_specs=[rowt] + _halo_specs(tm, seq, 0) + [col(OFF_C), col(OFF_H)]
        + _halo_specs(tm, seq, OFF_C // CONV_W) + _halo_specs(tm, seq, OFF_H // CONV_W) + [const((8, CONV_W))],
        out_specs=(pl.BlockSpec((tm, CH_W), lambda i: (i, 0)), const((8, CONV_W))),
        compiler_params=_cparams("arbitrary"),
    )(dconv, dconv, dconv, proj, proj, proj, proj, proj, proj, conv_w)


def _gla_bwd(proj, lr, do, st_f, st_b, wgk_f, wgk_b, bgk_f, bgk_b, tt):
    seq = proj.shape[0]
    nb, nc = seq // tt, tt // CHUNK

    def body(qf, kf, vf, lrf, dof, stf, qb, kb, vb, lrb, dob, stb, wf, wb, bf, bb,
             dqkv_f, dlr_f, dqkv_b, dlr_b, dwf, dwb, dbf, dbb,
             ds_scr, eq_s, ek_s, ein_s, eout_s, qs_s, ks_s, qin_s, kout_s, db_s, lg_s, dqs_s, dks_s, dvi_s, gt_s):
        @pl.when(pl.program_id(0) == 0)
        def _():
            ds_scr[...] = jnp.zeros(ds_scr.shape, F32)
            for r in (dwf, dwb, dbf, dbb):
                r[...] = jnp.zeros(r.shape, F32)

        low, upp, sup = _block_masks(tt)
        row = lax.broadcasted_iota(jnp.int32, (CHUNK, 1), 0)
        kmask = _chunk_column_mask(tt)
        dirs = ((qf, kf, vf, lrf, dof, stf, wf, bf, dqkv_f, dlr_f, dwf, dbf,
                 low, upp, low, REF_F, LAST_F, list(reversed(range(nc)))),
                (qb, kb, vb, lrb, dob, stb, wb, bb, dqkv_b, dlr_b, dwb, dbb,
                 upp, low, sup, REF_B, LAST_B, list(range(nc))))
        for d, (q_r, k_r, v_r, lr_r, do_r, st_r, w_r, b_r, dqkv_r, dlr_r, dw_r, db_r,
                cum, cum_t, mask, ref, last, order) in enumerate(dirs):
            lrv = lr_r[...].astype(BF16)
            wv = w_r[...]
            logits = _dot(lrv, wv) + b_r[...]
            lg_s[...] = logits
            b = _dot_split3(cum.astype(BF16), _log_gate(logits))
            decs = []
            for c in range(nc):
                rows = slice(c * CHUNK, (c + 1) * CHUNK)
                bc = b[rows]
                b_ref, b_last = bc[ref:ref + 1], bc[last:last + 1]
                qc = q_r[rows, :].astype(F32) * QSCALE
                kc = k_r[rows, :].astype(F32)
                e_q, e_k, e_in, e_out = jnp.exp(bc - b_ref), jnp.exp(b_ref - bc), jnp.exp(bc), jnp.exp(b_last - bc)
                eq_s[rows, :], ek_s[rows, :], ein_s[rows, :], eout_s[rows, :] = e_q, e_k, e_in, e_out
                qs_s[rows, :] = (qc * e_q).astype(BF16)
                ks_s[rows, :] = (kc * e_k).astype(BF16)
                qin_s[rows, :] = (qc * e_in).astype(BF16)
                kout_s[rows, :] = (kc * e_out).astype(BF16)
                decs.append(jnp.exp(b_last))
            for h in range(HEADS):
                ksl = slice(h * DK, (h + 1) * DK)
                vsl = slice(h * DV, (h + 1) * DV)
                v = v_r[:, vsl].astype(BF16)
                dov = do_r[:, vsl].astype(BF16)
                qsb, ksb = qs_s[:, ksl], ks_s[:, ksl]
                att = jnp.where(mask, _dot_nt(qsb, ksb), 0.0).astype(BF16)
                datt = jnp.where(mask, _dot_nt(dov, v), 0.0).astype(BF16)
                dqs, dks, dv_intra, g_t = dqs_s, dks_s, dvi_s, gt_s
                dqs[...] = _dot(datt, ksb)
                dks[...] = _dot_tn(datt, qsb)
                dv_intra[...] = _dot_tn(att, dov)
                g_t[...] = _dot_tn(dov, _chunked(kmask, qin_s[:, ksl], nc))
                ds = ds_scr[d * HEADS + h]
                for c in order:
                    rows = slice(c * CHUNK, (c + 1) * CHUNK)
                    dsb = ds.astype(BF16)
                    s_prev = st_r[c, h]
                    dk_out = _dot(v_r[rows, vsl].astype(BF16), dsb)
                    dq_in = _dot(do_r[rows, vsl].astype(BF16), s_prev)
                    dv = dv_intra[rows] + _dot_nt(kout_s[rows, ksl], dsb)
                    dqkv_r[rows, OFF_V + h * DV:OFF_V + (h + 1) * DV] = dv.astype(BF16)
                    dec = decs[c][:, ksl]
                    ddec = jnp.sum(ds * s_prev.astype(F32), axis=0, keepdims=True)
                    e_out = eout_s[rows, ksl]
                    qc = q_r[rows, ksl].astype(F32) * QSCALE
                    kc = k_r[rows, ksl].astype(F32)
                    dq = dqs[rows] * eq_s[rows, ksl] + dq_in * ein_s[rows, ksl]
                    dk = dks[rows] * ek_s[rows, ksl] + dk_out * e_out
                    dqkv_r[rows, OFF_Q + h * DK:OFF_Q + (h + 1) * DK] = (dq * QSCALE).astype(BF16)
                    dqkv_r[rows, OFF_K + h * DK:OFF_K + (h + 1) * DK] = dk.astype(BF16)
                    tail = jnp.sum(dk_out * (kc * e_out), axis=0, keepdims=True) + ddec * dec
                    db_s[rows, ksl] = (qc * dq - kc * dk) + jnp.where(row == last, tail, 0.0)
                    ds = ds * dec + g_t[:, c * DK:(c + 1) * DK]
                ds_scr[d * HEADS + h] = ds
            dg = _dot_split3(cum_t.astype(BF16), db_s[...])
            dlogit = (dg * GATE_SCALE) * _sigmoid(-lg_s[...])
            dlb = dlogit.astype(BF16)
            dlr_r[...] = _dot_nt(dlb, wv)
            dw_r[...] += _dot_tn(lrv, dlb)
            db_r[...] += jnp.sum(dlogit, axis=0, keepdims=True)

    fw = lambda i: (nb - 1 - i, 0)
    bw = lambda i: (i, 0)
    const = lambda i: (0, 0)

    def tok_specs(m):
        return [pl.BlockSpec((tt, QK_W), lambda i: (m(i)[0], OFF_Q // QK_W)),
                pl.BlockSpec((tt, QK_W), lambda i: (m(i)[0], OFF_K // QK_W)),
                pl.BlockSpec((tt, V_W), lambda i: (m(i)[0], OFF_V // V_W)),
                pl.BlockSpec((tt, LR_W), m),
                pl.BlockSpec((tt, V_W), m),
                pl.BlockSpec((nc, HEADS, DV, DK), lambda i: (m(i)[0], 0, 0, 0))]

    dqkv = jax.ShapeDtypeStruct((seq, QK_W + QK_W + V_W), BF16)
    dlr = jax.ShapeDtypeStruct((seq, LR_W), F32)
    dw = jax.ShapeDtypeStruct((LR_W, QK_W), F32)
    dbias = jax.ShapeDtypeStruct((1, QK_W), F32)
    return pl.pallas_call(
        body, name="gla_bwd",
        out_shape=(dqkv, dlr, dqkv, dlr, dw, dw, dbias, dbias),
        grid=(nb,),
        in_specs=tok_specs(fw) + tok_specs(bw) + [
            pl.BlockSpec((LR_W, QK_W), const), pl.BlockSpec((LR_W, QK_W), const),
            pl.BlockSpec((1, QK_W), const), pl.BlockSpec((1, QK_W), const)],
        out_specs=(pl.BlockSpec((tt, QK_W + QK_W + V_W), fw), pl.BlockSpec((tt, LR_W), fw),
                   pl.BlockSpec((tt, QK_W + QK_W + V_W), bw), pl.BlockSpec((tt, LR_W), bw),
                   pl.BlockSpec((LR_W, QK_W), const), pl.BlockSpec((LR_W, QK_W), const),
                   pl.BlockSpec((1, QK_W), const), pl.BlockSpec((1, QK_W), const)),
        scratch_shapes=[pltpu.VMEM((2 * HEADS, DV, DK), F32)] + [pltpu.VMEM((tt, QK_W), F32)] * 4
        + [pltpu.VMEM((tt, QK_W), BF16)] * 4 + [pltpu.VMEM((tt, QK_W), F32)] * 2
        + [pltpu.VMEM((tt, DK), F32)] * 2 + [pltpu.VMEM((tt, DV), F32), pltpu.VMEM((DV, nc * DK), F32)],
        compiler_params=_cparams("arbitrary"),
    )(proj, proj, proj, lr, do, st_f, proj, proj, proj, lr, do, st_b, wgk_f, wgk_b, bgk_f, bgk_b)


def _both_directions(f_ref, b_ref):
    return (_f32(f_ref) + _f32(b_ref)).astype(BF16)


def _input_grad(dqkv_f, dqkv_b, dp_gates, dp_ch, dlr_f, dlr_b, w_nat, x2d, norm_g, dx2, sums, tm):
    seq = x2d.shape[0]
    nt, n = seq // tm, len(sums)
    relay_step = (3 * nt) // 8

    def body(dqf, dqb, dg, dc, dlf, dlb, w, x_ref, g_ref, dx2_ref, *rest):
        ins, (gx_ref, dng_ref), outs = rest[:n], rest[n:n + 2], rest[n + 2:2 * n + 2]
        passing, joined = rest[2 * n + 2:3 * n + 2], rest[3 * n + 2:4 * n + 2]
        send_sems, recv_sems, local_sems = rest[4 * n + 2:]
        i = pl.program_id(0)
        c = lax.axis_index("c")
        first, second, diagonal = _route_chips()
        slot = lambda chip: 2 * chip[0] + chip[1]

        def remote(a, k, src, dst, to):
            return pltpu.make_async_remote_copy(src_ref=src, dst_ref=dst, send_sem=send_sems.at[3 * a + k],
                                                recv_sem=recv_sems.at[3 * a + k], device_id=(*to, c),
                                                device_id_type=MESH)

        direct = lambda a: remote(a, 0, ins[a].at[slot(first)], outs[a].at[0], first)
        for_second = lambda a: remote(a, 1, ins[a].at[slot(diagonal)], passing[a], first)
        joint = lambda a: remote(a, 2, joined[a], outs[a].at[1], second)
        own = lambda a: pltpu.make_async_copy(ins[a].at[slot(second)], joined[a], local_sems.at[a])

        @pl.when(i == 0)
        def _():
            _start_all([for_second(a) for a in range(n)] + [own(a) for a in range(n)] + [direct(a) for a in range(n)])
            dng_ref[...] = jnp.zeros(dng_ref.shape, F32)

        @pl.when(i == relay_step)
        def _():
            for a in range(n):
                for_second(a).wait_recv()
                own(a).wait()
                joined[a][...] = (joined[a][...].astype(F32) + passing[a][...].astype(F32)).astype(BF16)
                joint(a).start()

        dh = (_dot((dlf[...] + dlb[...]).astype(BF16), w[NAT_LR:NAT_LR + LR_W, :])
              + _dot(_both_directions(dqf, dqb), w[0:NAT_ZA, :])
              + _dot(dg[:, 0:CONV_W], w[NAT_ZA:NAT_LR, :]) + _dot(dg[:, CONV_W:2 * CONV_W], w[NAT_B:NAT_C, :])
              + _dot(dg[:, 2 * CONV_W:], w[NAT_ZC:IN_W, :]) + _dot(dc[...], w[NAT_C:NAT_ZC, :]))
        xv = x_ref[...]
        r = lax.rsqrt(jnp.mean(xv * xv, axis=-1, keepdims=True) + EPS)
        xn = xv * r
        dng_ref[...] += jnp.sum(dh * xn, axis=0, keepdims=True)
        dn = dh * g_ref[...]
        gx_ref[...] = (r * dn - xn * (r * jnp.mean(dn * xn, axis=-1, keepdims=True))) + dx2_ref[...]

        @pl.when(i == nt - 1)
        def _():
            for a in range(n):
                direct(a).wait_recv()
                joint(a).wait_recv()
            for a in range(n):
                for cp in (direct(a), for_second(a), joint(a)):
                    cp.wait_send()

    rowt = pl.BlockSpec((tm, D_MODEL), lambda i: (i, 0))
    seg = lambda width: pl.BlockSpec((tm, width), lambda i: (i, 0))
    resident = lambda rows: pl.BlockSpec((rows, D_MODEL), lambda i: (0, 0), pipeline_mode=pl.Buffered(1))
    hbm = pl.BlockSpec(memory_space=pl.ANY)
    blocks = [pltpu.VMEM(s.shape[1:], s.dtype) for s in sums]
    return pl.pallas_call(
        body, name="input_grad",
        out_shape=(jax.ShapeDtypeStruct((seq, D_MODEL), F32), jax.ShapeDtypeStruct((1, D_MODEL), F32))
        + tuple(jax.ShapeDtypeStruct((2,) + s.shape[1:], s.dtype) for s in sums),
        grid=(nt,),
        in_specs=[seg(QKV_W), seg(QKV_W), seg(GATES_W), seg(CH_W), seg(LR_W), seg(LR_W), resident(IN_W),
                  rowt, pl.BlockSpec((1, D_MODEL), lambda i: (0, 0)), rowt] + [hbm] * n,
        out_specs=(rowt, pl.BlockSpec((1, D_MODEL), lambda i: (0, 0))) + (hbm,) * n,
        scratch_shapes=blocks + blocks + [pltpu.SemaphoreType.DMA((3 * n,)), pltpu.SemaphoreType.DMA((3 * n,)),
                                          pltpu.SemaphoreType.DMA((n,))],
        compiler_params=_cparams("arbitrary"),
    )(dqkv_f, dqkv_b, dp_gates, dp_ch, dlr_f, dlr_b, w_nat, x2d, norm_g, dx2, *sums)


def _weight_grad_out(y_t, dx2b, tk, riding):
    m, seq = y_t.shape
    n = dx2b.shape[1]
    nk = seq // tk

    def body(a_ref, b_ref, ride_in, o_ref, ride_out, send_sems, recv_sems):
        k = pl.program_id(0)

        @pl.when(k == 0)
        def _():
            _start_all(_sibling_copies(ride_in, ride_out, send_sems, recv_sems))
            o_ref[...] = jnp.zeros(o_ref.shape, F32)

        o_ref[...] += _dot(a_ref[...], b_ref[...])

        @pl.when(k == nk - 1)
        def _():
            _wait_all(_sibling_copies(ride_in, ride_out, send_sems, recv_sems))

    hbm = pl.BlockSpec(memory_space=pl.ANY)
    return pl.pallas_call(
        body, name="wgrad_out",
        out_shape=(jax.ShapeDtypeStruct((m, n), F32), jax.ShapeDtypeStruct((4,) + _block_shape(riding), F32)),
        grid=(nk,),
        in_specs=[pl.BlockSpec((m, tk), lambda k: (0, k)), pl.BlockSpec((tk, n), lambda k: (k, 0)), hbm],
        out_specs=(pl.BlockSpec((m, n), lambda k: (0, 0)), hbm),
        scratch_shapes=[pltpu.SemaphoreType.DMA((4,)), pltpu.SemaphoreType.DMA((4,))],
        compiler_params=_cparams("arbitrary"),
    )(y_t, dx2b, riding)


def _weight_grad_in(h_t, dqkv_f, dqkv_b, dp_gates, dp_ch, dlr_f, dlr_b):
    m, seq = h_t.shape
    tn = 512
    n_qkv, n_gates, n_ch = QKV_W // tn, GATES_W // tn, CH_W // tn
    starts = ([k * tn for k in range(n_qkv)] + [NAT_ZA, NAT_ZA + tn, NAT_B, NAT_B + tn, NAT_ZC, NAT_ZC + tn]
              + [NAT_C + k * tn for k in range(n_ch)])

    def out_row(j):
        row = 0
        for k, start in enumerate(starts):
            row = row + jnp.where(j == k, start // 32, 0)
        return pl.multiple_of(row * 32, 32), 0

    def body(a_ref, bqf, bqb, bg, bc, dlf, dlb, o_ref, lr_ref, acc, bq):
        j = pl.program_id(0)

        @pl.when(j == 0)
        def _():
            acc[:, 0:LR_W] = _dot(a_ref[...], (dlf[...] + dlb[...]).astype(BF16))
            lr_ref[...] = acc[:, 0:LR_W].T[0:2 * RANK, :]

        @pl.when(j < n_qkv)
        def _():
            bq[...] = _both_directions(bqf, bqb)
            acc[...] = _dot(a_ref[...], bq[...])

        @pl.when(jnp.logical_and(j >= n_qkv, j < n_qkv + n_gates))
        def _():
            acc[...] = _dot(a_ref[...], bg[...])

        @pl.when(j >= n_qkv + n_gates)
        def _():
            acc[...] = _dot(a_ref[...], bc[...])

        o_ref[...] = acc[...].T

    resident = lambda shape: pl.BlockSpec(shape, lambda j: (0, 0), pipeline_mode=pl.Buffered(1))
    seg = lambda first, count: pl.BlockSpec((seq, tn), lambda j: (0, jnp.clip(j - first, 0, count - 1)))
    main, lr_rows = pl.pallas_call(
        body, name="wgrad_in",
        out_shape=(jax.ShapeDtypeStruct((IN_W, m), F32), jax.ShapeDtypeStruct((2 * RANK, m), F32)),
        grid=(n_qkv + n_gates + n_ch,),
        in_specs=[resident((m, seq)), seg(0, n_qkv), seg(0, n_qkv), seg(n_qkv, n_gates), seg(n_qkv + n_gates, n_ch),
                  resident((seq, LR_W)), resident((seq, LR_W))],
        out_specs=(pl.BlockSpec((pl.Element(tn), pl.Element(m)), out_row),
                   pl.BlockSpec((2 * RANK, m), lambda j: (0, 0))),
        scratch_shapes=[pltpu.VMEM((m, tn), F32), pltpu.VMEM((seq, tn), BF16)],
        compiler_params=_cparams("arbitrary"),
    )(h_t, dqkv_f, dqkv_b, dp_gates, dp_ch, dlr_f, dlr_b)
    return lax.dynamic_update_slice(main, lr_rows, (NAT_LR, 0))


def _pad_rows(a, rows):
    return jnp.pad(a, ((0, rows - a.shape[0]), (0, 0)))


def _rows128(a):
    a = a.reshape(-1, 128)
    return _pad_rows(a, -(-a.shape[0] // 8) * 8)


def _pack(arrs):
    return jnp.concatenate([_rows128(a) for a in arrs], axis=0)


def _unpack(buf, like):
    out, start = [], 0
    for a in like:
        rows = a.size // 128
        out.append(buf[start:start + rows].reshape(a.shape))
        start += -(-rows // 8) * 8
    return out


def kernel(x, norm_g, w_in, w_gk_f, b_gk_f, w_gk_b, b_gk_b, gla_norm_g, conv_w, conv_b, w_out, final_g, loss_target, m_norm_g, m_w_in, m_w_gk_f, m_b_gk_f, m_w_gk_b, m_b_gk_b, m_gla_norm_g, m_conv_w, m_conv_b, m_w_out, m_final_g, v_norm_g, v_w_in, v_w_gk_f, v_b_gk_f, v_w_gk_b, v_b_gk_b, v_gla_norm_g, v_conv_w, v_conv_b, v_w_out, v_final_g):
    px, py, pc = _position()
    me = _blk(px, py, pc)
    seq = x.shape[1]
    x2d, tgt = x[0], loss_target[0]
    tt = min(256, seq)

    small_s = jnp.concatenate([jnp.concatenate([w_gk_f[0], w_gk_b[0]], axis=1), _pad_rows(conv_w[0], 8)], axis=0)
    order = sum(jnp.where(2 * px + py == k, jnp.asarray(tiles + (0,), jnp.int32), 0) for k, tiles in enumerate(TILE_ORDER))
    proj, lr, h_t, w_nat, wout_all, small_all = _gather_inproj(x2d, norm_g, w_in[0].T, w_out[0], small_s, order,
                                                               min(1024, seq))
    w_out_full = wout_all.reshape(MIX_W, D_MODEL)
    wgk_cols = 512 // N_DEV
    wgk_f_full = small_all[:, 0:RANK, 0:wgk_cols].transpose(1, 0, 2).reshape(RANK, QK_W)
    wgk_b_full = small_all[:, 0:RANK, wgk_cols:2 * wgk_cols].transpose(1, 0, 2).reshape(RANK, QK_W)
    conv_w_full = _pad_rows(small_all[:, RANK:RANK + 3, :].transpose(1, 0, 2).reshape(3, CONV_W), 8)
    zr = lambda n: jnp.zeros((n, QK_W), F32)
    wgk_f_pad = jnp.concatenate([wgk_f_full, zr(LR_W - RANK)], axis=0).astype(BF16)
    wgk_b_pad = jnp.concatenate([zr(RANK), wgk_b_full, zr(LR_W - 2 * RANK)], axis=0).astype(BF16)

    o_f, o_b, st_f, st_b = _gla_fwd(proj, lr, wgk_f_pad, wgk_b_pad, b_gk_f, b_gk_b, tt)
    tmix = min(512, seq)
    y_t, conv, dx2, dx2b, loss_p, dfg_p = _mix_out_loss(o_f, o_b, proj, x2d, tgt, gla_norm_g, conv_w_full, conv_b,
                                                        w_out_full, final_g.reshape(1, D_MODEL), tmix)

    dp_gates, do, dconv, dgg_p, dcb_p = _mix_bwd(dx2b, o_f, o_b, proj, conv, gla_norm_g, w_out_full, tmix)
    dp_ch, dcw_p = _conv_bwd(dconv, proj, conv_w_full, tmix)
    dqkv_f, dlr_f, dqkv_b, dlr_b, dwf_p, dwb_p, dbf_p, dbb_p = _gla_bwd(
        proj, lr, do, st_f, st_b, wgk_f_pad, wgk_b_pad, b_gk_f, b_gk_b, tt)
    dw_nat = _weight_grad_in(h_t, dqkv_f, dqkv_b, dp_gates, dp_ch, dlr_f, dlr_b)

    dw_out, sib_in = _weight_grad_out(y_t, dx2b, min(1024, seq), dw_nat)
    part_out = dw_out.reshape(N_DEV, MIX_W // N_DEV, D_MODEL)
    core = jnp.reshape(pc, (1,)).astype(jnp.int32)
    chip = jnp.reshape(2 * px + py, (1,)).astype(jnp.int32)
    sums_in, sib_out = _chip_sums(dw_nat, sib_in, core, D_MODEL, "chip_sums_in", riding=part_out)
    sums_out = _chip_sums(part_out, sib_out, core, D_MODEL, "chip_sums_out")
    grad_x2d, dng_p, far_in, far_out = _input_grad(dqkv_f, dqkv_b, dp_gates, dp_ch, dlr_f, dlr_b, w_nat, x2d, norm_g, dx2,
                                                   [sums_in, sums_out], min(256, seq))
    pieces = [dng_p, dbf_p, dbb_p, dgg_p, dcb_p, dfg_p[0], dwf_p[0:RANK], dwb_p[RANK:2 * RANK], dcw_p[0:3], loss_p[0]]
    g_window, small_tot = _final_sum(sums_in, far_in, chip, _pack(pieces), 512, "final_sum_in")
    g_in_t = lax.dynamic_slice_in_dim(g_window, 4 * pc, SHARD_W, axis=0)
    g_w_out, d_w_out, nm_w_out, nv_w_out = _final_sum_adamw(sums_out, far_out, chip, w_out[0], m_w_out[0], v_w_out[0],
                                                            256, "adamw_out")
    flat = lambda a: a[0].T.reshape(SHARD_W, D_MODEL // 128, 128)
    unflat = lambda a: a.reshape(SHARD_W, D_MODEL).T
    d_flat, m_flat, v_flat = _adamw_rows(g_in_t.reshape(SHARD_W, D_MODEL // 128, 128), flat(w_in), flat(m_w_in),
                                         flat(v_w_in), 450, "adamw_in")
    g_w_in, d_w_in, nm_w_in, nv_w_in = g_in_t.T, unflat(d_flat), unflat(m_flat), unflat(v_flat)

    tot = _unpack(small_tot, pieces)
    g_norm_g, g_b_gk_f, g_b_gk_b, g_gla, g_conv_b, g_final = tot[:6]
    g_wgk_f = lax.dynamic_slice_in_dim(tot[6], me * wgk_cols, wgk_cols, axis=1)[None]
    g_wgk_b = lax.dynamic_slice_in_dim(tot[7], me * wgk_cols, wgk_cols, axis=1)[None]
    g_conv_w = lax.dynamic_slice_in_dim(tot[8], me * 128, 128, axis=1)[None]
    loss = tot[9][0]

    small_g = [g_norm_g, g_b_gk_f, g_b_gk_b, g_gla, g_conv_b, g_final, g_wgk_f, g_wgk_b, g_conv_w]
    small_w = [norm_g, b_gk_f, b_gk_b, gla_norm_g, conv_b, final_g, w_gk_f, w_gk_b, conv_w]
    small_m = [m_norm_g, m_b_gk_f, m_b_gk_b, m_gla_norm_g, m_conv_b, m_final_g, m_w_gk_f, m_w_gk_b, m_conv_w]
    small_v = [v_norm_g, v_b_gk_f, v_b_gk_b, v_gla_norm_g, v_conv_b, v_final_g, v_w_gk_f, v_w_gk_b, v_conv_w]
    d_s, m_s, v_s = _adamw_small(_pack(small_g), _pack(small_w), _pack(small_m), _pack(small_v))
    d_l, m_l, v_l = _unpack(d_s, small_w), _unpack(m_s, small_w), _unpack(v_s, small_w)

    def ordered(sm, big_in, big_out):
        return [sm[0], big_in[None], sm[6], sm[1], sm[7], sm[2], sm[3], sm[8], sm[4], big_out[None], sm[5]]

    grads = ordered(small_g, g_w_in, g_w_out)
    deltas = ordered(d_l, d_w_in, d_w_out)
    new_m = ordered(m_l, nm_w_in, nm_w_out)
    new_v = ordered(v_l, nv_w_in, nv_w_out)
    return (loss, grad_x2d[None], *grads, *deltas, *new_m, *new_v)
```

```python
import functools

import jax
import jax.numpy as jnp
from jax import lax
from jax.experimental import pallas as pl
from jax.experimental.pallas import tpu as pltpu

F32 = jnp.float32
BF16 = jnp.bfloat16
MESH = pl.DeviceIdType.MESH

N_DEV = 8
D_MODEL = 1024
HEADS = 4
DK = 128
DV = 256
QK_W = HEADS * DK
V_W = HEADS * DV
CONV_W = 1024
MIX_W = V_W + CONV_W
CHUNK = 64
RANK = 16
IN_W = 7200
SHARD_W = IN_W // N_DEV
MAIN_W = 7168
LR_W = 128
OFF_Q, OFF_K, OFF_V, OFF_ZA, OFF_B, OFF_ZC, OFF_C, OFF_H = 0, 512, 1024, 2048, 3072, 4096, 5120, 6144
QKV_W, GATES_W, CH_W = 2048, 3072, 2048
NAT_ZA, NAT_LR, NAT_B, NAT_C, NAT_ZC = 2048, 3072, 3104, 4128, 6176
EPS = 1e-6
GATE_SCALE = 1.0 / 16.0
QSCALE = DK ** -0.5
REF_F, LAST_F = CHUNK // 2, CHUNK - 1
REF_B, LAST_B = CHUNK - 1 - CHUNK // 2, 0

ADAM_LR = 0.001
ADAM_B1 = 0.9
ADAM_B2 = 0.999
ADAM_EPS = 1e-08
ADAM_WD = 0.01
ADAM_STEP = 10

VMEM_LIMIT = 56 * 1024 * 1024


def _cparams(*sem):
    return pltpu.CompilerParams(dimension_semantics=sem, vmem_limit_bytes=VMEM_LIMIT)


def _dot(a, b):
    return jnp.dot(a, b, preferred_element_type=F32)


def _dot_nt(a, b):
    return lax.dot_general(a, b, (((1,), (1,)), ((), ())), preferred_element_type=F32)


def _dot_tn(a, b):
    return lax.dot_general(a, b, (((0,), (0,)), ((), ())), preferred_element_type=F32)


def _sigmoid(z):
    return jax.nn.sigmoid(z)


def _position():
    return lax.axis_index("x"), lax.axis_index("y"), lax.axis_index("c")


def _blk(px, py, pc):
    return 4 * px + 2 * py + pc


EDGE = 16
SHIFTED_ROWS = 912
BODY_ROWS = SHIFTED_ROWS - 2 * EDGE


def _first_tile_row(blk, px):
    return EDGE * (56 * blk + px)


def _edge_tiles():
    tiles = {}
    for blk in range(N_DEV):
        first = _first_tile_row(blk, blk // 4)
        tiles.setdefault(first, []).append((blk, 0))
        tiles.setdefault(first + EDGE + BODY_ROWS, []).append((blk, 1))
    return tiles


def _peer_copies(srcs, outs, send_sems, recv_sems):
    x, y, c = _position()
    me = _blk(x, y, c)
    copies = []
    for a, (src, out) in enumerate(zip(srcs, outs)):
        k = 0
        for dx in (0, 1):
            for dy in (0, 1):
                for dc in (0, 1):
                    if dx + dy + dc == 0:
                        continue
                    peer = (1 - x if dx else x, 1 - y if dy else y, 1 - c if dc else c)
                    copies.append(pltpu.make_async_remote_copy(
                        src_ref=src, dst_ref=out.at[me], send_sem=send_sems.at[a * 7 + k],
                        recv_sem=recv_sems.at[a * 7 + k], device_id=peer, device_id_type=MESH))
                    k += 1
    return copies


def _route_chips():
    x, y, c = _position()
    along_x = c == 0
    return [(jnp.where(along_x, 1 - x, x), jnp.where(along_x, y, 1 - y)),
            (jnp.where(along_x, x, 1 - x), jnp.where(along_x, 1 - y, y)), (1 - x, 1 - y)]


WINDOW_ROWS = SHARD_W + 4


def _window_start(k, parity):
    return 2 * SHARD_W * k + (SHARD_W - 4) * parity


def _owner_block(part, k, parity):
    if part.ndim == 3:
        return part.at[2 * k + parity]
    return part.at[pl.ds(pl.multiple_of(_window_start(k, parity), 8), WINDOW_ROWS)]


def _block_shape(part):
    return part.shape[1:] if part.ndim == 3 else (WINDOW_ROWS, part.shape[1])


def _sibling_copies(part, out, send_sems, recv_sems):
    x, y, c = _position()
    return [pltpu.make_async_remote_copy(src_ref=_owner_block(part, k, 1 - c), dst_ref=out.at[k],
                                         send_sem=send_sems.at[k], recv_sem=recv_sems.at[k],
                                         device_id=(x, y, 1 - c), device_id_type=MESH)
            for k in range(4)]


def _start_all(copies):
    for cp in copies:
        cp.start()


def _wait_all(copies):
    for cp in copies:
        cp.wait_recv()
    for cp in copies:
        cp.wait_send()


def _chip_sums(part, from_sibling, core, tc, name, riding=None):
    rows, cols = _block_shape(part)
    nj = cols // tc

    def body(core_ref, p_ref, s_ref, *rest):
        if riding is None:
            (o_ref,) = rest
        else:
            ride_in, o_ref, ride_out, send_sems, recv_sems = rest
            k, j = pl.program_id(0), pl.program_id(1)

            @pl.when(jnp.logical_and(k == 0, j == 0))
            def _():
                _start_all(_sibling_copies(ride_in, ride_out, send_sems, recv_sems))

        o_ref[0] = (p_ref[...].reshape(rows, tc) + s_ref[0]).astype(BF16)

        if riding is not None:
            @pl.when(jnp.logical_and(k == 3, j == nj - 1))
            def _():
                _wait_all(_sibling_copies(ride_in, ride_out, send_sems, recv_sems))

    hbm = pl.BlockSpec(memory_space=pl.ANY)
    sums = jax.ShapeDtypeStruct((4, rows, cols), BF16)
    tile_out = pl.BlockSpec((1, rows, tc), lambda k, j, core_ref: (k, 0, j))
    if part.ndim == 3:
        mine = pl.BlockSpec((1, rows, tc), lambda k, j, core_ref: (2 * k + core_ref[0], 0, j))
    else:
        mine = pl.BlockSpec((pl.Element(rows), pl.Element(tc)),
                            lambda k, j, core_ref: (pl.multiple_of(_window_start(k, core_ref[0]), 8),
                                                    pl.multiple_of(j * tc, 128)))
    in_specs = [mine, pl.BlockSpec((1, rows, tc), lambda k, j, core_ref: (k, 0, j))]
    if riding is None:
        out_shape, out_specs, scratch, args = sums, tile_out, [], (core, part, from_sibling)
    else:
        out_shape = (sums, jax.ShapeDtypeStruct((4,) + _block_shape(riding), F32))
        out_specs, in_specs = (tile_out, hbm), in_specs + [hbm]
        scratch = [pltpu.SemaphoreType.DMA((4,)), pltpu.SemaphoreType.DMA((4,))]
        args = (core, part, from_sibling, riding)
    return pl.pallas_call(
        body, name=name, out_shape=out_shape,
        grid_spec=pltpu.PrefetchScalarGridSpec(num_scalar_prefetch=1, grid=(4, nj), in_specs=in_specs,
                                               out_specs=out_specs, scratch_shapes=scratch),
        compiler_params=_cparams("arbitrary", "arbitrary"),
    )(*args)


def _sum_chips(s_ref, r_ref):
    f = lambda a: a.astype(F32)
    return (f(s_ref[0]) + f(r_ref[0])) + f(r_ref[1])


def _final_sum(sums, from_chips, chip, small, tc, name):
    _, rows, cols = sums.shape
    nj = cols // tc

    def body(chip_ref, s_ref, r_ref, sm_ref, g_out, tot_ref, all_ref, send_sems, recv_sems):
        j = pl.program_id(0)
        me = _blk(*_position())

        @pl.when(j == 0)
        def _():
            all_ref[me] = sm_ref[...]
            _start_all(_peer_copies((all_ref.at[me],), (all_ref,), send_sems, recv_sems))

        g_out[...] = _sum_chips(s_ref, r_ref)

        @pl.when(j == nj - 1)
        def _():
            _wait_all(_peer_copies((all_ref.at[me],), (all_ref,), send_sems, recv_sems))
            acc = all_ref[0]
            for d in range(1, N_DEV):
                acc = acc + all_ref[d]
            tot_ref[...] = acc

    whole = pl.BlockSpec(small.shape, lambda j, chip_ref: (0, 0))
    return pl.pallas_call(
        body, name=name,
        out_shape=(jax.ShapeDtypeStruct((rows, cols), F32), jax.ShapeDtypeStruct(small.shape, F32)),
        grid_spec=pltpu.PrefetchScalarGridSpec(
            num_scalar_prefetch=1, grid=(nj,),
            in_specs=[pl.BlockSpec((1, rows, tc), lambda j, chip_ref: (chip_ref[0], 0, j)),
                      pl.BlockSpec((2, rows, tc), lambda j, chip_ref: (0, 0, j)), whole],
            out_specs=(pl.BlockSpec((rows, tc), lambda j, chip_ref: (0, j)), whole),
            scratch_shapes=[pltpu.VMEM((N_DEV,) + small.shape, F32), pltpu.SemaphoreType.DMA((7,)),
                            pltpu.SemaphoreType.DMA((7,))]),
        compiler_params=_cparams("arbitrary"),
    )(chip, sums, from_chips, small)


def _adamw_rows(g, w, m, v, tr, name):
    rows = g.shape[0]

    def body(g_ref, w_ref, m_ref, v_ref, d_out, m_out, v_out):
        delta, m_new, v_new = _adamw(w_ref[...], g_ref[...], m_ref[...], v_ref[...])
        d_out[...] = delta
        m_out[...] = m_new
        v_out[...] = v_new

    tile = pl.BlockSpec((tr,) + g.shape[1:], lambda r: (r, 0, 0))
    shp = jax.ShapeDtypeStruct(g.shape, F32)
    return pl.pallas_call(
        body, name=name, out_shape=(shp, shp, shp), grid=(rows // tr,),
        in_specs=[tile] * 4, out_specs=(tile, tile, tile),
        compiler_params=_cparams("arbitrary"),
    )(g, w, m, v)


def _adamw(w, g, m, v):
    m = ADAM_B1 * m + (1.0 - ADAM_B1) * g
    v = ADAM_B2 * v + (1.0 - ADAM_B2) * (g * g)
    m_hat = m / (1.0 - ADAM_B1 ** ADAM_STEP)
    v_hat = v / (1.0 - ADAM_B2 ** ADAM_STEP)
    delta = -ADAM_LR * (m_hat / (jnp.sqrt(v_hat) + ADAM_EPS) + ADAM_WD * w)
    return delta, m, v


def _final_sum_adamw(sums, from_chips, chip, w, m, v, tr, name):
    rows, cols = w.shape

    def body(chip_ref, s_ref, r_ref, w_ref, m_ref, v_ref, g_out, d_out, m_out, v_out):
        g = _sum_chips(s_ref, r_ref)
        delta, m_new, v_new = _adamw(w_ref[...], g, m_ref[...], v_ref[...])
        g_out[...] = g
        d_out[...] = delta
        m_out[...] = m_new
        v_out[...] = v_new

    tile = pl.BlockSpec((tr, cols), lambda r, chip_ref: (r, 0))
    shp = jax.ShapeDtypeStruct((rows, cols), F32)
    return pl.pallas_call(
        body, name=name,
        out_shape=(shp, shp, shp, shp),
        grid_spec=pltpu.PrefetchScalarGridSpec(
            num_scalar_prefetch=1, grid=(rows // tr,),
            in_specs=[pl.BlockSpec((1, tr, cols), lambda r, chip_ref: (chip_ref[0], r, 0)),
                      pl.BlockSpec((2, tr, cols), lambda r, chip_ref: (0, r, 0)),
                      tile, tile, tile],
            out_specs=(tile, tile, tile, tile)),
        compiler_params=_cparams("arbitrary"),
    )(chip, sums, from_chips, w, m, v)


def _adamw_small(g, w, m, v):
    def body(g_ref, w_ref, m_ref, v_ref, d_out, m_out, v_out):
        delta, m_new, v_new = _adamw(w_ref[...], g_ref[...], m_ref[...], v_ref[...])
        d_out[...] = delta
        m_out[...] = m_new
        v_out[...] = v_new

    vmem = pl.BlockSpec(memory_space=pltpu.VMEM)
    shp = jax.ShapeDtypeStruct(g.shape, F32)
    return pl.pallas_call(body, name="adamw_small", out_shape=(shp, shp, shp),
                          in_specs=[vmem] * 4, out_specs=(vmem, vmem, vmem))(g, w, m, v)


TILE_ROWS = (0, 1024, NAT_ZA, NAT_B, NAT_ZC, NAT_C, NAT_C + CONV_W)


TILE_ORDER = ((0, 1, 2, 3, 5, 6, 4), (2, 1, 0, 4, 3, 5, 6), (5, 6, 0, 4, 1, 2, 3), (4, 6, 2, 3, 5, 0, 1))
EARLY_SWEEP, NEIGHBOUR_SWEEP, DIAGONAL_SWEEP = 1, 2, 4
PIECES, W_IN_PIECES, OTHER_PIECES = 4, (0, 1), (2, 3)


def _gather_inproj(x2d, norm_g, shard_t, w_out_s, small_s, order, tm):
    seq = x2d.shape[0]
    tn = CONV_W
    ni, nj = seq // tm, MAIN_W // tn
    first_sweep = lambda j, i, order_ref: jnp.where(j == 0, i, ni - 1)
    last_sweep = lambda j, i, order_ref: jnp.where(j == nj - 1, i, 0)
    edge_tiles = _edge_tiles()

    def body(order_ref, x_ref, g_ref, shard_ref, wout_ref, sm_ref, proj_ref, lr_ref, ht_ref, w_nat, wout_all, sm_all,
             w_all, h_all, edges, stage, wout_b, sm_b, send_sems, recv_sems, local_sems):
        j, i = pl.program_id(0), pl.program_id(1)
        rows = pl.ds(pl.multiple_of(i * tm, tm), tm)
        x, y, c = _position()
        me, here, sibling = _blk(x, y, c), (x, y, c), (x, y, 1 - c)
        chips = _route_chips()
        sibling_chips = [chips[1], chips[0], chips[2]]

        def pieces(px, py, pc):
            blk = _blk(px, py, pc)
            body_rows = pl.ds(pl.multiple_of(_first_tile_row(blk, px) + EDGE, EDGE), BODY_ROWS)
            return [w_all.at[body_rows], edges.at[blk], wout_all.at[blk], sm_all.at[blk]]

        def copy(a, k, block, to, staged=None):
            ref = pieces(*block)[a]
            return pltpu.make_async_remote_copy(src_ref=ref if staged is None else staged, dst_ref=ref,
                                                send_sem=send_sems.at[a * 7 + k], recv_sem=recv_sems.at[a * 7 + k],
                                                device_id=to, device_id_type=MESH)

        def own_copies(group, slots=(0, 1, 2)):
            targets = [sibling] + [(*chips[n], c) for n in range(2)]
            staged = [None, None, wout_b, sm_b]
            return [copy(a, k, here, targets[k], staged[a]) for k in slots for a in group]

        def relays(group):
            return [copy(a, 3, (*chips[0], c), (*chips[1], c)) for a in group]

        def forwards(n, group):
            return [copy(a, 4 + n, (*chips[n], c), sibling) for a in group]

        def keep_own():
            return [pltpu.make_async_copy(wout_b, wout_all.at[me], local_sems.at[0]),
                    pltpu.make_async_copy(sm_b, sm_all.at[me], local_sems.at[1])]

        def keep_weight():
            return pltpu.make_async_copy(w_all, w_nat, local_sems.at[2])

        def take(ns, group, relay=True):
            for n in ns:
                for a in group:
                    copy(a, 1 + n, (*chips[n], c), here).wait_recv()
                _start_all((relays(group) if n == 0 and relay else []) + forwards(n, group))

        def take_passed_on(ns, group):
            for n in ns:
                for a in group:
                    copy(a, 4 + n, (*sibling_chips[n], 1 - c), here).wait_recv()

        def arrive(ns, group):
            take(ns, group)
            take_passed_on(ns, group)

        def per_core_and_row(step):
            for core in range(2):
                for row in range(2):
                    pl.when(jnp.logical_and(c == core, y == row))(functools.partial(step, core, row))

        def start_own(core, row):
            now = (0, 1 + core) if core == row else (0, 2 - core, 1 + core)
            _start_all(own_copies(W_IN_PIECES, now))
            wout_b[...] = wout_ref[...].astype(BF16)
            sm_b[...] = sm_ref[...]
            _start_all(own_copies(OTHER_PIECES, now) + keep_own())

        def take_early(core, row):
            if core == row:
                _start_all(own_copies(W_IN_PIECES, (2 - core,)) + own_copies(OTHER_PIECES, (2 - core,)))
                take((1 - core,), W_IN_PIECES, relay=False)
            else:
                take_passed_on((core,), W_IN_PIECES)

        def take_neighbours(core, row):
            if core == row:
                _start_all(relays(W_IN_PIECES) if core == 1 else [])
                take((core,), W_IN_PIECES)
                take_passed_on((0, 1), W_IN_PIECES)
            else:
                take((0, 1), W_IN_PIECES)
                take_passed_on((1 - core,), W_IN_PIECES)

        early_blk = _blk(x, 1 - y, y)

        def add_edge_tiles(stage):
            for row, parts in edge_tiles.items():
                ready = 0
                for blk, _ in parts:
                    away = (x != blk // 4).astype(jnp.int32) + (y != (blk // 2) % 2).astype(jnp.int32)
                    late = jnp.where(away == 1, jnp.where(early_blk == blk, 1, 2), jnp.where(away == 2, 3 + blk % 2, 0))
                    ready = jnp.maximum(ready, late)

                @pl.when(ready == stage)
                def _(row=row, parts=parts):
                    tile = edges[parts[0][0], parts[0][1]].astype(F32)
                    for blk, side in parts[1:]:
                        tile = tile + edges[blk, side].astype(F32)
                    w_all[row:row + EDGE, :] = tile.astype(BF16)

        @pl.when(jnp.logical_and(j == 0, i == 0))
        def _():
            last = SHARD_W // 8 * 8
            for col in range(0, D_MODEL, 128):
                cols = slice(col, col + 128)
                stage[0:last, :] = shard_ref[0:last, cols]
                stage[last:, :] = jnp.zeros((SHIFTED_ROWS - last, 128), F32)
                stage[last:SHARD_W, :] = shard_ref[last:SHARD_W, cols]
                for k in range(EDGE // 4):
                    @pl.when(me % 4 == k)
                    def _(k=k, cols=cols):
                        moved = pltpu.roll(stage[...], 4 * k, 0) if k else stage[...]
                        pieces(*here)[0][:, cols] = moved[EDGE:EDGE + BODY_ROWS].astype(BF16)
                        edges[me, 0, :, cols] = moved[0:EDGE].astype(BF16)
                        edges[me, 1, :, cols] = moved[EDGE + BODY_ROWS:].astype(BF16)
            per_core_and_row(start_own)
            for a in W_IN_PIECES:
                copy(a, 0, sibling, here).wait_recv()
            add_edge_tiles(0)

        @pl.when(jnp.logical_and(j == EARLY_SWEEP, i == 0))
        def _():
            per_core_and_row(take_early)
            add_edge_tiles(1)

        @pl.when(jnp.logical_and(j == NEIGHBOUR_SWEEP, i == 0))
        def _():
            per_core_and_row(take_neighbours)
            add_edge_tiles(2)

        for core in range(2):
            @pl.when(jnp.logical_and(j == DIAGONAL_SWEEP + core, i == 0))
            def _(core=core):
                pl.when(c == core)(lambda: take((2,), W_IN_PIECES))
                pl.when(c != core)(lambda: take_passed_on((2,), W_IN_PIECES))
                add_edge_tiles(3 + core)
                if core == 0:
                    arrive((0, 1), OTHER_PIECES)
                else:
                    keep_weight().start()

        @pl.when(jnp.logical_and(j == nj - 1, i == 0))
        def _():
            arrive((2,), OTHER_PIECES)

        @pl.when(j == 0)
        def _():
            xv = x_ref[...]
            r = lax.rsqrt(jnp.mean(xv * xv, axis=-1, keepdims=True) + EPS)
            h = (xv * r) * g_ref[...]
            h_all[rows, :] = h.astype(BF16)
            ht_ref[...] = h.T.astype(BF16)

        tile = order_ref[j]
        row = 0
        for k, start in enumerate(TILE_ROWS):
            row = row + jnp.where(tile == k, start // 32, 0)
        w_tile = w_all[pl.ds(pl.multiple_of(row * 32, 32), tn), :]
        proj_ref[...] = _dot_nt(h_all[rows, :], w_tile).astype(BF16)

        @pl.when(j == nj - 1)
        def _():
            lr_ref[...] = _dot_nt(h_all[rows, :], w_all[NAT_LR:NAT_LR + LR_W, :])

        @pl.when(jnp.logical_and(j == nj - 1, i == ni - 1))
        def _():
            everything = range(PIECES)
            passed_on = [cp for n in range(3) for cp in forwards(n, everything)]
            for cp in own_copies(everything) + relays(everything) + passed_on:
                cp.wait_send()
            for a in OTHER_PIECES:
                copy(a, 0, sibling, here).wait_recv()
            for cp in keep_own() + [keep_weight()]:
                cp.wait()

    const = lambda shape: pl.BlockSpec(shape, lambda j, i, order_ref: (0,) * len(shape))
    hbm = pl.BlockSpec(memory_space=pl.ANY)
    vmem = pl.BlockSpec(memory_space=pltpu.VMEM)
    return pl.pallas_call(
        body, name="gather_inproj",
        out_shape=(jax.ShapeDtypeStruct((seq, MAIN_W), BF16), jax.ShapeDtypeStruct((seq, LR_W), F32),
                   jax.ShapeDtypeStruct((D_MODEL, seq), BF16), jax.ShapeDtypeStruct((IN_W, D_MODEL), BF16),
                   jax.ShapeDtypeStruct((N_DEV,) + w_out_s.shape, BF16),
                   jax.ShapeDtypeStruct((N_DEV,) + small_s.shape, F32)),
        grid_spec=pltpu.PrefetchScalarGridSpec(
            num_scalar_prefetch=1, grid=(nj, ni),
            in_specs=[pl.BlockSpec((tm, D_MODEL), lambda j, i, order_ref: (first_sweep(j, i, order_ref), 0)),
                      const((1, D_MODEL)), vmem, vmem, const(small_s.shape)],
            out_specs=(pl.BlockSpec((tm, tn), lambda j, i, order_ref: (i, order_ref[j])),
                       pl.BlockSpec((tm, LR_W), lambda j, i, order_ref: (last_sweep(j, i, order_ref), 0)),
                       pl.BlockSpec((D_MODEL, tm), lambda j, i, order_ref: (0, first_sweep(j, i, order_ref))),
                       hbm, hbm, hbm),
            scratch_shapes=[pltpu.VMEM((IN_W, D_MODEL), BF16), pltpu.VMEM((seq, D_MODEL), BF16),
                            pltpu.VMEM((N_DEV, 2, EDGE, D_MODEL), BF16), pltpu.VMEM((SHIFTED_ROWS, 128), F32),
                            pltpu.VMEM(w_out_s.shape, BF16), pltpu.VMEM(small_s.shape, F32),
                            pltpu.SemaphoreType.DMA((7 * PIECES,)), pltpu.SemaphoreType.DMA((7 * PIECES,)),
                            pltpu.SemaphoreType.DMA((3,))]),
        compiler_params=_cparams("arbitrary", "arbitrary"),
    )(order, x2d, norm_g, shard_t, w_out_s, small_s)


def _block_masks(tt):
    row = lax.broadcasted_iota(jnp.int32, (tt, tt), 0)
    col = lax.broadcasted_iota(jnp.int32, (tt, tt), 1)
    same = jnp.right_shift(row, 6) == jnp.right_shift(col, 6)
    return (jnp.logical_and(same, col <= row), jnp.logical_and(same, col >= row), jnp.logical_and(same, col > row))


def _dot_split3(ones_mat, x):
    x1 = x.astype(BF16)
    r1 = x - x1.astype(F32)
    x2 = r1.astype(BF16)
    x3 = (r1 - x2.astype(F32)).astype(BF16)
    return (_dot(ones_mat, x3) + _dot(ones_mat, x2)) + _dot(ones_mat, x1)


def _log_gate(logits):
    return (jnp.minimum(logits, 0.0) - jnp.log(1.0 + jnp.exp(-jnp.abs(logits)))) * GATE_SCALE


def _chunk_column_mask(tt):
    nc = tt // CHUNK
    row = lax.broadcasted_iota(jnp.int32, (tt, nc * DK), 0)
    col = lax.broadcasted_iota(jnp.int32, (tt, nc * DK), 1)
    return jnp.right_shift(row, 6) == jnp.right_shift(col, 7)


def _chunked(mask, x, nc):
    wide = jnp.concatenate([x] * nc, axis=1)
    return jnp.where(mask, wide, jnp.zeros_like(wide))


def _gla_fwd(proj, lr, wgk_f, wgk_b, bgk_f, bgk_b, tt):
    seq = proj.shape[0]
    nb, nc, nch = seq // tt, tt // CHUNK, seq // CHUNK

    def body(qf, kf, vf, lrf, qb, kb, vb, lrb, wf, wb, bf, bb, of, ob, stf, stb, s_scr, qs_s, ks_s, qin_s, kout_s):
        @pl.when(pl.program_id(0) == 0)
        def _():
            s_scr[...] = jnp.zeros(s_scr.shape, F32)

        low, upp, sup = _block_masks(tt)
        dirs = ((qf, kf, vf, lrf, wf, bf, of, stf, low, low, REF_F, LAST_F, list(range(nc))),
                (qb, kb, vb, lrb, wb, bb, ob, stb, upp, sup, REF_B, LAST_B, list(reversed(range(nc)))))
        for d, (q_r, k_r, v_r, lr_r, w_r, b_r, o_r, st_r, cum, mask, ref, last, order) in enumerate(dirs):
            logits = _dot(lr_r[...].astype(BF16), w_r[...]) + b_r[...]
            b = _dot_split3(cum.astype(BF16), _log_gate(logits))
            decs = []
            for c in range(nc):
                rows = slice(c * CHUNK, (c + 1) * CHUNK)
                bc = b[rows]
                b_ref, b_last = bc[ref:ref + 1], bc[last:last + 1]
                qc = q_r[rows, :].astype(F32) * QSCALE
                kc = k_r[rows, :].astype(F32)
                qs_s[rows, :] = (qc * jnp.exp(bc - b_ref)).astype(BF16)
                ks_s[rows, :] = (kc * jnp.exp(b_ref - bc)).astype(BF16)
                qin_s[rows, :] = (qc * jnp.exp(bc)).astype(BF16)
                kout_s[rows, :] = (kc * jnp.exp(b_last - bc)).astype(BF16)
                decs.append(jnp.exp(b_last))
            for h in range(HEADS):
                ksl = slice(h * DK, (h + 1) * DK)
                vsl = slice(h * DV, (h + 1) * DV)
                v = v_r[:, vsl].astype(BF16)
                att = jnp.where(mask, _dot_nt(qs_s[:, ksl], ks_s[:, ksl]), 0.0).astype(BF16)
                o_intra = _dot(att, v)
                st = s_scr[d * HEADS + h]
                for c in order:
                    rows = slice(c * CHUNK, (c + 1) * CHUNK)
                    stb = st.astype(BF16)
                    st_r[c, h] = stb
                    o_r[rows, vsl] = (o_intra[rows] + _dot_nt(qin_s[rows, ksl], stb)).astype(BF16)
                    st = st * decs[c][:, ksl] + _dot_tn(v_r[rows, vsl].astype(BF16), kout_s[rows, ksl])
                s_scr[d * HEADS + h] = st

    fw = lambda i: (i, 0)
    bw = lambda i: (nb - 1 - i, 0)
    const = lambda i: (0, 0)

    def tok_specs(m):
        return [pl.BlockSpec((tt, QK_W), lambda i: (m(i)[0], OFF_Q // QK_W)),
                pl.BlockSpec((tt, QK_W), lambda i: (m(i)[0], OFF_K // QK_W)),
                pl.BlockSpec((tt, V_W), lambda i: (m(i)[0], OFF_V // V_W)),
                pl.BlockSpec((tt, LR_W), m)]

    st_shape = jax.ShapeDtypeStruct((nch, HEADS, DV, DK), BF16)
    o_shape = jax.ShapeDtypeStruct((seq, V_W), BF16)
    operand = pltpu.VMEM((tt, QK_W), BF16)
    return pl.pallas_call(
        body, name="gla_fwd",
        out_shape=(o_shape, o_shape, st_shape, st_shape),
        grid=(nb,),
        in_specs=tok_specs(fw) + tok_specs(bw) + [
            pl.BlockSpec((LR_W, QK_W), const), pl.BlockSpec((LR_W, QK_W), const),
            pl.BlockSpec((1, QK_W), const), pl.BlockSpec((1, QK_W), const)],
        out_specs=(pl.BlockSpec((tt, V_W), fw), pl.BlockSpec((tt, V_W), bw),
                   pl.BlockSpec((nc, HEADS, DV, DK), lambda i: (i, 0, 0, 0)),
                   pl.BlockSpec((nc, HEADS, DV, DK), lambda i: (nb - 1 - i, 0, 0, 0))),
        scratch_shapes=[pltpu.VMEM((2 * HEADS, DV, DK), F32), operand, operand, operand, operand],
        compiler_params=_cparams("arbitrary"),
    )(proj, proj, proj, lr, proj, proj, proj, lr, wgk_f, wgk_b, bgk_f, bgk_b)


def _head_norm(o, gain):
    outs, rinv = [], []
    for h in range(HEADS):
        oh = o[:, h * DV:(h + 1) * DV]
        r = lax.rsqrt(jnp.mean(oh * oh, axis=-1, keepdims=True) + EPS)
        outs.append((oh * r) * gain)
        rinv.append(r)
    return jnp.concatenate(outs, axis=1), rinv


def _shift_rows(u, prev_row, next_row):
    n = u.shape[0]
    row = lax.broadcasted_iota(jnp.int32, (n, 1), 0)
    up = jnp.where(row == 0, prev_row, pltpu.roll(u, 1, 0))
    un = jnp.where(row == n - 1, next_row, pltpu.roll(u, n - 1, 0))
    return up, un


HALO = 16


def _halo_specs(tm, seq, col_block):
    per = tm // HALO
    last = seq // HALO - 1
    return [pl.BlockSpec((HALO, CONV_W), lambda i: (jnp.maximum(i * per - 1, 0), col_block)),
            pl.BlockSpec((HALO, CONV_W), lambda i: (jnp.minimum((i + 1) * per, last), col_block))]


def _f32(ref):
    return ref[...].astype(F32)


def _last_row(ref):
    return ref[HALO - 1:HALO, :].astype(F32)


def _first_row(ref):
    return ref[0:1, :].astype(F32)


def _mix_out_loss(o_f, o_b, proj, x2d, tgt, gla_g, conv_w, conv_b, w_out, final_g, tm):
    seq = x2d.shape[0]
    nt = seq // tm

    def body(of, ob, za, bg, cg, hc, zc, cprev, cnext, hprev, hnext, x_ref, t_ref, gg, cw, cb, wo, fg,
             yt_ref, conv_ref, dx2_ref, dx2b_ref, loss_ref, dfg_ref):
        i = pl.program_id(0)

        @pl.when(i == 0)
        def _():
            loss_ref[...] = jnp.zeros(loss_ref.shape, F32)
            dfg_ref[...] = jnp.zeros(dfg_ref.shape, F32)

        on, _ = _head_norm(_f32(of) + _f32(ob), gg[...])
        zav = _f32(za)
        y_a = on * (zav * _sigmoid(zav))
        u = _f32(cg) * _f32(hc)
        prev_row = jnp.where(i > 0, _last_row(cprev) * _last_row(hprev), 0.0)
        next_row = jnp.where(i < nt - 1, _first_row(cnext) * _first_row(hnext), 0.0)
        up, un = _shift_rows(u, prev_row, next_row)
        conv = (cw[0:1, :] * up + cw[1:2, :] * u + cw[2:3, :] * un) + cb[...]
        conv_ref[...] = conv.astype(BF16)
        zcv = _f32(zc)
        y_c = _f32(bg) * conv * (zcv * _sigmoid(zcv))
        y = jnp.concatenate([y_a, y_c], axis=1)
        yt_ref[...] = y.T.astype(BF16)
        x2 = x_ref[...] + _dot(y.astype(BF16), wo[...])
        r = lax.rsqrt(jnp.mean(x2 * x2, axis=-1, keepdims=True) + EPS)
        xn = x2 * r
        err = xn * fg[...] - t_ref[...]
        loss_ref[...] += 0.5 * jnp.sum(jnp.mean(err * err, axis=-1, keepdims=True))
        dyf = err * (1.0 / D_MODEL)
        dfg_ref[...] += jnp.sum(dyf * xn, axis=0, keepdims=True)
        dxn = dyf * fg[...]
        dx2 = r * dxn - xn * (r * jnp.mean(dxn * xn, axis=-1, keepdims=True))
        dx2_ref[...] = dx2
        dx2b_ref[...] = dx2.astype(BF16)

    def col(off):
        return pl.BlockSpec((tm, CONV_W), lambda i: (i, off // CONV_W))

    rowt = pl.BlockSpec((tm, D_MODEL), lambda i: (i, 0))
    const = lambda shape: pl.BlockSpec(shape, lambda i: (0, 0))
    return pl.pallas_call(
        body, name="mix_out_loss",
        out_shape=(jax.ShapeDtypeStruct((MIX_W, seq), BF16), jax.ShapeDtypeStruct((seq, CONV_W), BF16),
                   jax.ShapeDtypeStruct((seq, D_MODEL), F32), jax.ShapeDtypeStruct((seq, D_MODEL), BF16),
                   jax.ShapeDtypeStruct((8, 128), F32), jax.ShapeDtypeStruct((1, D_MODEL), F32)),
        grid=(nt,),
        in_specs=[rowt, rowt, col(OFF_ZA), col(OFF_B), col(OFF_C), col(OFF_H), col(OFF_ZC)]
        + _halo_specs(tm, seq, OFF_C // CONV_W) + _halo_specs(tm, seq, OFF_H // CONV_W)
        + [rowt, rowt, const((1, DV)), const((8, CONV_W)), const((1, CONV_W)), const((MIX_W, D_MODEL)),
           const((1, D_MODEL))],
        out_specs=(pl.BlockSpec((MIX_W, tm), lambda i: (0, i)), rowt, rowt, rowt, const((8, 128)),
                   const((1, D_MODEL))),
        compiler_params=_cparams("arbitrary"),
    )(o_f, o_b, proj, proj, proj, proj, proj, proj, proj, proj, proj, x2d, tgt, gla_g, conv_w, conv_b, w_out, final_g)


def _dsilu(z, s):
    return s * (1.0 + z * (1.0 - s))


def _mix_bwd(dx2b, o_f, o_b, proj, conv, gla_g, w_out, tm):
    seq = dx2b.shape[0]

    def body(dx, of, ob, za, bg, zc, cv, gg, wo, dg_ref, do_ref, dconv_ref, dgg_ref, dcb_ref):
        @pl.when(pl.program_id(0) == 0)
        def _():
            dgg_ref[...] = jnp.zeros(dgg_ref.shape, F32)
            dcb_ref[...] = jnp.zeros(dcb_ref.shape, F32)

        dy = _dot_nt(dx[...], wo[...])
        dy_a, dy_c = dy[:, :V_W], dy[:, V_W:]
        zcv, bgv, convv = _f32(zc), _f32(bg), _f32(cv)
        sc = _sigmoid(zcv)
        szc = zcv * sc
        dg_ref[:, CONV_W:2 * CONV_W] = (dy_c * convv * szc).astype(BF16)
        dconv = dy_c * bgv * szc
        dconv_ref[...] = dconv.astype(BF16)
        dcb_ref[...] += jnp.sum(dconv, axis=0, keepdims=True)
        dg_ref[:, 2 * CONV_W:] = (dy_c * bgv * convv * _dsilu(zcv, sc)).astype(BF16)

        o = _f32(of) + _f32(ob)
        gain = gg[...]
        on, rinv = _head_norm(o, gain)
        zav = _f32(za)
        sa = _sigmoid(zav)
        dg_ref[:, :CONV_W] = (dy_a * on * _dsilu(zav, sa)).astype(BF16)
        don = dy_a * (zav * sa)
        dgg = jnp.zeros((1, DV), F32)
        dos = []
        for h in range(HEADS):
            sl = slice(h * DV, (h + 1) * DV)
            oh, r, dh = o[:, sl], rinv[h], don[:, sl]
            ohn = oh * r
            dgg = dgg + jnp.sum(dh * ohn, axis=0, keepdims=True)
            dn = dh * gain
            dos.append(r * dn - ohn * (r * jnp.mean(dn * ohn, axis=-1, keepdims=True)))
        dgg_ref[...] += dgg
        do_ref[...] = jnp.concatenate(dos, axis=1).astype(BF16)

    def col(off):
        return pl.BlockSpec((tm, CONV_W), lambda i: (i, off // CONV_W))

    rowt = pl.BlockSpec((tm, D_MODEL), lambda i: (i, 0))
    const = lambda shape: pl.BlockSpec(shape, lambda i: (0, 0))
    return pl.pallas_call(
        body, name="mix_bwd",
        out_shape=(jax.ShapeDtypeStruct((seq, GATES_W), BF16), jax.ShapeDtypeStruct((seq, V_W), BF16),
                   jax.ShapeDtypeStruct((seq, CONV_W), BF16),
                   jax.ShapeDtypeStruct((1, DV), F32), jax.ShapeDtypeStruct((1, CONV_W), F32)),
        grid=(seq // tm,),
        in_specs=[rowt, rowt, rowt, col(OFF_ZA), col(OFF_B), col(OFF_ZC), rowt, const((1, DV)),
                  const((MIX_W, D_MODEL))],
        out_specs=(pl.BlockSpec((tm, GATES_W), lambda i: (i, 0)), rowt, rowt, const((1, DV)), const((1, CONV_W))),
        compiler_params=_cparams("arbitrary"),
    )(dx2b, o_f, o_b, proj, proj, proj, conv, gla_g, w_out)


def _conv_bwd(dconv, proj, conv_w, tm):
    seq = dconv.shape[0]
    nt = seq // tm

    def body(dc_in, dprev, dnext, cg, hc, cprev, cnext, hprev, hnext, cw, dch_ref, dcw_ref):
        i = pl.program_id(0)

        @pl.when(i == 0)
        def _():
            dcw_ref[...] = jnp.zeros(dcw_ref.shape, F32)

        first, lastt = i > 0, i < nt - 1
        dcv = _f32(dc_in)
        d_up, d_un = _shift_rows(dcv, jnp.where(first, _last_row(dprev), 0.0), jnp.where(lastt, _first_row(dnext), 0.0))
        cgv, hcv = _f32(cg), _f32(hc)
        u = cgv * hcv
        u_up, u_un = _shift_rows(u, jnp.where(first, _last_row(cprev) * _last_row(hprev), 0.0),
                                 jnp.where(lastt, _first_row(cnext) * _first_row(hnext), 0.0))
        du = cw[0:1, :] * d_un + cw[1:2, :] * dcv + cw[2:3, :] * d_up
        dch_ref[:, :CONV_W] = (du * hcv).astype(BF16)
        dch_ref[:, CONV_W:] = (du * cgv).astype(BF16)
        dcw_ref[0:1, :] += jnp.sum(dcv * u_up, axis=0, keepdims=True)
        dcw_ref[1:2, :] += jnp.sum(dcv * u, axis=0, keepdims=True)
        dcw_ref[2:3, :] += jnp.sum(dcv * u_un, axis=0, keepdims=True)

    def col(off):
        return pl.BlockSpec((tm, CONV_W), lambda i: (i, off // CONV_W))

    rowt = pl.BlockSpec((tm, CONV_W), lambda i: (i, 0))
    const = lambda shape: pl.BlockSpec(shape, lambda i: (0, 0))
    return pl.pallas_call(
        body, name="conv_bwd",
        out_shape=(jax.ShapeDtypeStruct((seq, CH_W), BF16), jax.ShapeDtypeStruct((8, CONV_W), F32)),
        grid=(nt,),
        in_specs=[rowt] + _halo_specs(tm, seq, 0) + [col(OFF_C), col(OFF_H)]
        + _halo_specs(tm, seq, OFF_C // CONV_W) + _halo_specs(tm, seq, OFF_H // CONV_W) + [const((8, CONV_W))],
        out_specs=(pl.BlockSpec((tm, CH_W), lambda i: (i, 0)), const((8, CONV_W))),
        compiler_params=_cparams("arbitrary"),
    )(dconv, dconv, dconv, proj, proj, proj, proj, proj, proj, conv_w)


def _gla_bwd(proj, lr, do, st_f, st_b, wgk_f, wgk_b, bgk_f, bgk_b, tt):
    seq = proj.shape[0]
    nb, nc = seq // tt, tt // CHUNK

    def body(qf, kf, vf, lrf, dof, stf, qb, kb, vb, lrb, dob, stb, wf, wb, bf, bb,
             dqkv_f, dlr_f, dqkv_b, dlr_b, dwf, dwb, dbf, dbb,
             ds_scr, eq_s, ek_s, ein_s, eout_s, qs_s, ks_s, qin_s, kout_s, db_s, lg_s, dqs_s, dks_s, dvi_s, gt_s):
        @pl.when(pl.program_id(0) == 0)
        def _():
            ds_scr[...] = jnp.zeros(ds_scr.shape, F32)
            for r in (dwf, dwb, dbf, dbb):
                r[...] = jnp.zeros(r.shape, F32)

        low, upp, sup = _block_masks(tt)
        row = lax.broadcasted_iota(jnp.int32, (CHUNK, 1), 0)
        kmask = _chunk_column_mask(tt)
        dirs = ((qf, kf, vf, lrf, dof, stf, wf, bf, dqkv_f, dlr_f, dwf, dbf,
                 low, upp, low, REF_F, LAST_F, list(reversed(range(nc)))),
                (qb, kb, vb, lrb, dob, stb, wb, bb, dqkv_b, dlr_b, dwb, dbb,
                 upp, low, sup, REF_B, LAST_B, list(range(nc))))
        for d, (q_r, k_r, v_r, lr_r, do_r, st_r, w_r, b_r, dqkv_r, dlr_r, dw_r, db_r,
                cum, cum_t, mask, ref, last, order) in enumerate(dirs):
            lrv = lr_r[...].astype(BF16)
            wv = w_r[...]
            logits = _dot(lrv, wv) + b_r[...]
            lg_s[...] = logits
            b = _dot_split3(cum.astype(BF16), _log_gate(logits))
            decs = []
            for c in range(nc):
                rows = slice(c * CHUNK, (c + 1) * CHUNK)
                bc = b[rows]
                b_ref, b_last = bc[ref:ref + 1], bc[last:last + 1]
                qc = q_r[rows, :].astype(F32) * QSCALE
                kc = k_r[rows, :].astype(F32)
                e_q, e_k, e_in, e_out = jnp.exp(bc - b_ref), jnp.exp(b_ref - bc), jnp.exp(bc), jnp.exp(b_last - bc)
                eq_s[rows, :], ek_s[rows, :], ein_s[rows, :], eout_s[rows, :] = e_q, e_k, e_in, e_out
                qs_s[rows, :] = (qc * e_q).astype(BF16)
                ks_s[rows, :] = (kc * e_k).astype(BF16)
                qin_s[rows, :] = (qc * e_in).astype(BF16)
                kout_s[rows, :] = (kc * e_out).astype(BF16)
                decs.append(jnp.exp(b_last))
            for h in range(HEADS):
                ksl = slice(h * DK, (h + 1) * DK)
                vsl = slice(h * DV, (h + 1) * DV)
                v = v_r[:, vsl].astype(BF16)
                dov = do_r[:, vsl].astype(BF16)
                qsb, ksb = qs_s[:, ksl], ks_s[:, ksl]
                att = jnp.where(mask, _dot_nt(qsb, ksb), 0.0).astype(BF16)
                datt = jnp.where(mask, _dot_nt(dov, v), 0.0).astype(BF16)
                dqs, dks, dv_intra, g_t = dqs_s, dks_s, dvi_s, gt_s
                dqs[...] = _dot(datt, ksb)
                dks[...] = _dot_tn(datt, qsb)
                dv_intra[...] = _dot_tn(att, dov)
                g_t[...] = _dot_tn(dov, _chunked(kmask, qin_s[:, ksl], nc))
                ds = ds_scr[d * HEADS + h]
                for c in order:
                    rows = slice(c * CHUNK, (c + 1) * CHUNK)
                    dsb = ds.astype(BF16)
                    s_prev = st_r[c, h]
                    dk_out = _dot(v_r[rows, vsl].astype(BF16), dsb)
                    dq_in = _dot(do_r[rows, vsl].astype(BF16), s_prev)
                    dv = dv_intra[rows] + _dot_nt(kout_s[rows, ksl], dsb)
                    dqkv_r[rows, OFF_V + h * DV:OFF_V + (h + 1) * DV] = dv.astype(BF16)
                    dec = decs[c][:, ksl]
                    ddec = jnp.sum(ds * s_prev.astype(F32), axis=0, keepdims=True)
                    e_out = eout_s[rows, ksl]
                    qc = q_r[rows, ksl].astype(F32) * QSCALE
                    kc = k_r[rows, ksl].astype(F32)
                    dq = dqs[rows] * eq_s[rows, ksl] + dq_in * ein_s[rows, ksl]
                    dk = dks[rows] * ek_s[rows, ksl] + dk_out * e_out
                    dqkv_r[rows, OFF_Q + h * DK:OFF_Q + (h + 1) * DK] = (dq * QSCALE).astype(BF16)
                    dqkv_r[rows, OFF_K + h * DK:OFF_K + (h + 1) * DK] = dk.astype(BF16)
                    tail = jnp.sum(dk_out * (kc * e_out), axis=0, keepdims=True) + ddec * dec
                    db_s[rows, ksl] = (qc * dq - kc * dk) + jnp.where(row == last, tail, 0.0)
                    ds = ds * dec + g_t[:, c * DK:(c + 1) * DK]
                ds_scr[d * HEADS + h] = ds
            dg = _dot_split3(cum_t.astype(BF16), db_s[...])
            dlogit = (dg * GATE_SCALE) * _sigmoid(-lg_s[...])
            dlb = dlogit.astype(BF16)
            dlr_r[...] = _dot_nt(dlb, wv)
            dw_r[...] += _dot_tn(lrv, dlb)
            db_r[...] += jnp.sum(dlogit, axis=0, keepdims=True)

    fw = lambda i: (nb - 1 - i, 0)
    bw = lambda i: (i, 0)
    const = lambda i: (0, 0)

    def tok_specs(m):
        return [pl.BlockSpec((tt, QK_W), lambda i: (m(i)[0], OFF_Q // QK_W)),
                pl.BlockSpec((tt, QK_W), lambda i: (m(i)[0], OFF_K // QK_W)),
                pl.BlockSpec((tt, V_W), lambda i: (m(i)[0], OFF_V // V_W)),
                pl.BlockSpec((tt, LR_W), m),
                pl.BlockSpec((tt, V_W), m),
                pl.BlockSpec((nc, HEADS, DV, DK), lambda i: (m(i)[0], 0, 0, 0))]

    dqkv = jax.ShapeDtypeStruct((seq, QK_W + QK_W + V_W), BF16)
    dlr = jax.ShapeDtypeStruct((seq, LR_W), F32)
    dw = jax.ShapeDtypeStruct((LR_W, QK_W), F32)
    dbias = jax.ShapeDtypeStruct((1, QK_W), F32)
    return pl.pallas_call(
        body, name="gla_bwd",
        out_shape=(dqkv, dlr, dqkv, dlr, dw, dw, dbias, dbias),
        grid=(nb,),
        in_specs=tok_specs(fw) + tok_specs(bw) + [
            pl.BlockSpec((LR_W, QK_W), const), pl.BlockSpec((LR_W, QK_W), const),
            pl.BlockSpec((1, QK_W), const), pl.BlockSpec((1, QK_W), const)],
        out_specs=(pl.BlockSpec((tt, QK_W + QK_W + V_W), fw), pl.BlockSpec((tt, LR_W), fw),
                   pl.BlockSpec((tt, QK_W + QK_W + V_W), bw), pl.BlockSpec((tt, LR_W), bw),
                   pl.BlockSpec((LR_W, QK_W), const), pl.BlockSpec((LR_W, QK_W), const),
                   pl.BlockSpec((1, QK_W), const), pl.BlockSpec((1, QK_W), const)),
        scratch_shapes=[pltpu.VMEM((2 * HEADS, DV, DK), F32)] + [pltpu.VMEM((tt, QK_W), F32)] * 4
        + [pltpu.VMEM((tt, QK_W), BF16)] * 4 + [pltpu.VMEM((tt, QK_W), F32)] * 2
        + [pltpu.VMEM((tt, DK), F32)] * 2 + [pltpu.VMEM((tt, DV), F32), pltpu.VMEM((DV, nc * DK), F32)],
        compiler_params=_cparams("arbitrary"),
    )(proj, proj, proj, lr, do, st_f, proj, proj, proj, lr, do, st_b, wgk_f, wgk_b, bgk_f, bgk_b)


def _both_directions(f_ref, b_ref):
    return (_f32(f_ref) + _f32(b_ref)).astype(BF16)


def _input_grad(dqkv_f, dqkv_b, dp_gates, dp_ch, dlr_f, dlr_b, w_nat, x2d, norm_g, dx2, sums, tm):
    seq = x2d.shape[0]
    nt, n = seq // tm, len(sums)
    relay_step = (3 * nt) // 8

    def body(dqf, dqb, dg, dc, dlf, dlb, w, x_ref, g_ref, dx2_ref, *rest):
        ins, (gx_ref, dng_ref), outs = rest[:n], rest[n:n + 2], rest[n + 2:2 * n + 2]
        passing, joined = rest[2 * n + 2:3 * n + 2], rest[3 * n + 2:4 * n + 2]
        send_sems, recv_sems, local_sems = rest[4 * n + 2:]
        i = pl.program_id(0)
        c = lax.axis_index("c")
        first, second, diagonal = _route_chips()
        slot = lambda chip: 2 * chip[0] + chip[1]

        def remote(a, k, src, dst, to):
            return pltpu.make_async_remote_copy(src_ref=src, dst_ref=dst, send_sem=send_sems.at[3 * a + k],
                                                recv_sem=recv_sems.at[3 * a + k], device_id=(*to, c),
                                                device_id_type=MESH)

        direct = lambda a: remote(a, 0, ins[a].at[slot(first)], outs[a].at[0], first)
        for_second = lambda a: remote(a, 1, ins[a].at[slot(diagonal)], passing[a], first)
        joint = lambda a: remote(a, 2, joined[a], outs[a].at[1], second)
        own = lambda a: pltpu.make_async_copy(ins[a].at[slot(second)], joined[a], local_sems.at[a])

        @pl.when(i == 0)
        def _():
            _start_all([for_second(a) for a in range(n)] + [own(a) for a in range(n)] + [direct(a) for a in range(n)])
            dng_ref[...] = jnp.zeros(dng_ref.shape, F32)

        @pl.when(i == relay_step)
        def _():
            for a in range(n):
                for_second(a).wait_recv()
                own(a).wait()
                joined[a][...] = (joined[a][...].astype(F32) + passing[a][...].astype(F32)).astype(BF16)
                joint(a).start()

        dh = (_dot((dlf[...] + dlb[...]).astype(BF16), w[NAT_LR:NAT_LR + LR_W, :])
              + _dot(_both_directions(dqf, dqb), w[0:NAT_ZA, :])
              + _dot(dg[:, 0:CONV_W], w[NAT_ZA:NAT_LR, :]) + _dot(dg[:, CONV_W:2 * CONV_W], w[NAT_B:NAT_C, :])
              + _dot(dg[:, 2 * CONV_W:], w[NAT_ZC:IN_W, :]) + _dot(dc[...], w[NAT_C:NAT_ZC, :]))
        xv = x_ref[...]
        r = lax.rsqrt(jnp.mean(xv * xv, axis=-1, keepdims=True) + EPS)
        xn = xv * r
        dng_ref[...] += jnp.sum(dh * xn, axis=0, keepdims=True)
        dn = dh * g_ref[...]
        gx_ref[...] = (r * dn - xn * (r * jnp.mean(dn * xn, axis=-1, keepdims=True))) + dx2_ref[...]

        @pl.when(i == nt - 1)
        def _():
            for a in range(n):
                direct(a).wait_recv()
                joint(a).wait_recv()
            for a in range(n):
                for cp in (direct(a), for_second(a), joint(a)):
                    cp.wait_send()

    rowt = pl.BlockSpec((tm, D_MODEL), lambda i: (i, 0))
    seg = lambda width: pl.BlockSpec((tm, width), lambda i: (i, 0))
    resident = lambda rows: pl.BlockSpec((rows, D_MODEL), lambda i: (0, 0), pipeline_mode=pl.Buffered(1))
    hbm = pl.BlockSpec(memory_space=pl.ANY)
    blocks = [pltpu.VMEM(s.shape[1:], s.dtype) for s in sums]
    return pl.pallas_call(
        body, name="input_grad",
        out_shape=(jax.ShapeDtypeStruct((seq, D_MODEL), F32), jax.ShapeDtypeStruct((1, D_MODEL), F32))
        + tuple(jax.ShapeDtypeStruct((2,) + s.shape[1:], s.dtype) for s in sums),
        grid=(nt,),
        in_specs=[seg(QKV_W), seg(QKV_W), seg(GATES_W), seg(CH_W), seg(LR_W), seg(LR_W), resident(IN_W),
                  rowt, pl.BlockSpec((1, D_MODEL), lambda i: (0, 0)), rowt] + [hbm] * n,
        out_specs=(rowt, pl.BlockSpec((1, D_MODEL), lambda i: (0, 0))) + (hbm,) * n,
        scratch_shapes=blocks + blocks + [pltpu.SemaphoreType.DMA((3 * n,)), pltpu.SemaphoreType.DMA((3 * n,)),
                                          pltpu.SemaphoreType.DMA((n,))],
        compiler_params=_cparams("arbitrary"),
    )(dqkv_f, dqkv_b, dp_gates, dp_ch, dlr_f, dlr_b, w_nat, x2d, norm_g, dx2, *sums)


def _weight_grad_out(y_t, dx2b, tk, riding):
    m, seq = y_t.shape
    n = dx2b.shape[1]
    nk = seq // tk

    def body(a_ref, b_ref, ride_in, o_ref, ride_out, send_sems, recv_sems):
        k = pl.program_id(0)

        @pl.when(k == 0)
        def _():
            _start_all(_sibling_copies(ride_in, ride_out, send_sems, recv_sems))
            o_ref[...] = jnp.zeros(o_ref.shape, F32)

        o_ref[...] += _dot(a_ref[...], b_ref[...])

        @pl.when(k == nk - 1)
        def _():
            _wait_all(_sibling_copies(ride_in, ride_out, send_sems, recv_sems))

    hbm = pl.BlockSpec(memory_space=pl.ANY)
    return pl.pallas_call(
        body, name="wgrad_out",
        out_shape=(jax.ShapeDtypeStruct((m, n), F32), jax.ShapeDtypeStruct((4,) + _block_shape(riding), F32)),
        grid=(nk,),
        in_specs=[pl.BlockSpec((m, tk), lambda k: (0, k)), pl.BlockSpec((tk, n), lambda k: (k, 0)), hbm],
        out_specs=(pl.BlockSpec((m, n), lambda k: (0, 0)), hbm),
        scratch_shapes=[pltpu.SemaphoreType.DMA((4,)), pltpu.SemaphoreType.DMA((4,))],
        compiler_params=_cparams("arbitrary"),
    )(y_t, dx2b, riding)


def _weight_grad_in(h_t, dqkv_f, dqkv_b, dp_gates, dp_ch, dlr_f, dlr_b):
    m, seq = h_t.shape
    tn = 512
    n_qkv, n_gates, n_ch = QKV_W // tn, GATES_W // tn, CH_W // tn
    starts = ([k * tn for k in range(n_qkv)] + [NAT_ZA, NAT_ZA + tn, NAT_B, NAT_B + tn, NAT_ZC, NAT_ZC + tn]
              + [NAT_C + k * tn for k in range(n_ch)])

    def out_row(j):
        row = 0
        for k, start in enumerate(starts):
            row = row + jnp.where(j == k, start // 32, 0)
        return pl.multiple_of(row * 32, 32), 0

    def body(a_ref, bqf, bqb, bg, bc, dlf, dlb, o_ref, lr_ref, acc, bq):
        j = pl.program_id(0)

        @pl.when(j == 0)
        def _():
            acc[:, 0:LR_W] = _dot(a_ref[...], (dlf[...] + dlb[...]).astype(BF16))
            lr_ref[...] = acc[:, 0:LR_W].T[0:2 * RANK, :]

        @pl.when(j < n_qkv)
        def _():
            bq[...] = _both_directions(bqf, bqb)
            acc[...] = _dot(a_ref[...], bq[...])

        @pl.when(jnp.logical_and(j >= n_qkv, j < n_qkv + n_gates))
        def _():
            acc[...] = _dot(a_ref[...], bg[...])

        @pl.when(j >= n_qkv + n_gates)
        def _():
            acc[...] = _dot(a_ref[...], bc[...])

        o_ref[...] = acc[...].T

    resident = lambda shape: pl.BlockSpec(shape, lambda j: (0, 0), pipeline_mode=pl.Buffered(1))
    seg = lambda first, count: pl.BlockSpec((seq, tn), lambda j: (0, jnp.clip(j - first, 0, count - 1)))
    main, lr_rows = pl.pallas_call(
        body, name="wgrad_in",
        out_shape=(jax.ShapeDtypeStruct((IN_W, m), F32), jax.ShapeDtypeStruct((2 * RANK, m), F32)),
        grid=(n_qkv + n_gates + n_ch,),
        in_specs=[resident((m, seq)), seg(0, n_qkv), seg(0, n_qkv), seg(n_qkv, n_gates), seg(n_qkv + n_gates, n_ch),
                  resident((seq, LR_W)), resident((seq, LR_W))],
        out_specs=(pl.BlockSpec((pl.Element(tn), pl.Element(m)), out_row),
                   pl.BlockSpec((2 * RANK, m), lambda j: (0, 0))),
        scratch_shapes=[pltpu.VMEM((m, tn), F32), pltpu.VMEM((seq, tn), BF16)],
        compiler_params=_cparams("arbitrary"),
    )(h_t, dqkv_f, dqkv_b, dp_gates, dp_ch, dlr_f, dlr_b)
    return lax.dynamic_update_slice(main, lr_rows, (NAT_LR, 0))


def _pad_rows(a, rows):
    return jnp.pad(a, ((0, rows - a.shape[0]), (0, 0)))


def _rows128(a):
    a = a.reshape(-1, 128)
    return _pad_rows(a, -(-a.shape[0] // 8) * 8)


def _pack(arrs):
    return jnp.concatenate([_rows128(a) for a in arrs], axis=0)


def _unpack(buf, like):
    out, start = [], 0
    for a in like:
        rows = a.size // 128
        out.append(buf[start:start + rows].reshape(a.shape))
        start += -(-rows // 8) * 8
    return out


def kernel(x, norm_g, w_in, w_gk_f, b_gk_f, w_gk_b, b_gk_b, gla_norm_g, conv_w, conv_b, w_out, final_g, loss_target, m_norm_g, m_w_in, m_w_gk_f, m_b_gk_f, m_w_gk_b, m_b_gk_b, m_gla_norm_g, m_conv_w, m_conv_b, m_w_out, m_final_g, v_norm_g, v_w_in, v_w_gk_f, v_b_gk_f, v_w_gk_b, v_b_gk_b, v_gla_norm_g, v_conv_w, v_conv_b, v_w_out, v_final_g):
    px, py, pc = _position()
    me = _blk(px, py, pc)
    seq = x.shape[1]
    x2d, tgt = x[0], loss_target[0]
    tt = min(256, seq)

    small_s = jnp.concatenate([jnp.concatenate([w_gk_f[0], w_gk_b[0]], axis=1), _pad_rows(conv_w[0], 8)], axis=0)
    order = sum(jnp.where(2 * px + py == k, jnp.asarray(tiles + (0,), jnp.int32), 0) for k, tiles in enumerate(TILE_ORDER))
    proj, lr, h_t, w_nat, wout_all, small_all = _gather_inproj(x2d, norm_g, w_in[0].T, w_out[0], small_s, order,
                                                               min(1024, seq))
    w_out_full = wout_all.reshape(MIX_W, D_MODEL)
    wgk_cols = 512 // N_DEV
    wgk_f_full = small_all[:, 0:RANK, 0:wgk_cols].transpose(1, 0, 2).reshape(RANK, QK_W)
    wgk_b_full = small_all[:, 0:RANK, wgk_cols:2 * wgk_cols].transpose(1, 0, 2).reshape(RANK, QK_W)
    conv_w_full = _pad_rows(small_all[:, RANK:RANK + 3, :].transpose(1, 0, 2).reshape(3, CONV_W), 8)
    zr = lambda n: jnp.zeros((n, QK_W), F32)
    wgk_f_pad = jnp.concatenate([wgk_f_full, zr(LR_W - RANK)], axis=0).astype(BF16)
    wgk_b_pad = jnp.concatenate([zr(RANK), wgk_b_full, zr(LR_W - 2 * RANK)], axis=0).astype(BF16)

    o_f, o_b, st_f, st_b = _gla_fwd(proj, lr, wgk_f_pad, wgk_b_pad, b_gk_f, b_gk_b, tt)
    tmix = min(512, seq)
    y_t, conv, dx2, dx2b, loss_p, dfg_p = _mix_out_loss(o_f, o_b, proj, x2d, tgt, gla_norm_g, conv_w_full, conv_b,
                                                        w_out_full, final_g.reshape(1, D_MODEL), tmix)

    dp_gates, do, dconv, dgg_p, dcb_p = _mix_bwd(dx2b, o_f, o_b, proj, conv, gla_norm_g, w_out_full, tmix)
    dp_ch, dcw_p = _conv_bwd(dconv, proj, conv_w_full, tmix)
    dqkv_f, dlr_f, dqkv_b, dlr_b, dwf_p, dwb_p, dbf_p, dbb_p = _gla_bwd(
        proj, lr, do, st_f, st_b, wgk_f_pad, wgk_b_pad, b_gk_f, b_gk_b, tt)
    dw_nat = _weight_grad_in(h_t, dqkv_f, dqkv_b, dp_gates, dp_ch, dlr_f, dlr_b)

    dw_out, sib_in = _weight_grad_out(y_t, dx2b, min(1024, seq), dw_nat)
    part_out = dw_out.reshape(N_DEV, MIX_W // N_DEV, D_MODEL)
    core = jnp.reshape(pc, (1,)).astype(jnp.int32)
    chip = jnp.reshape(2 * px + py, (1,)).astype(jnp.int32)
    sums_in, sib_out = _chip_sums(dw_nat, sib_in, core, D_MODEL, "chip_sums_in", riding=part_out)
    sums_out = _chip_sums(part_out, sib_out, core, D_MODEL, "chip_sums_out")
    grad_x2d, dng_p, far_in, far_out = _input_grad(dqkv_f, dqkv_b, dp_gates, dp_ch, dlr_f, dlr_b, w_nat, x2d, norm_g, dx2,
                                                   [sums_in, sums_out], min(256, seq))
    pieces = [dng_p, dbf_p, dbb_p, dgg_p, dcb_p, dfg_p[0], dwf_p[0:RANK], dwb_p[RANK:2 * RANK], dcw_p[0:3], loss_p[0]]
    g_window, small_tot = _final_sum(sums_in, far_in, chip, _pack(pieces), 512, "final_sum_in")
    g_in_t = lax.dynamic_slice_in_dim(g_window, 4 * pc, SHARD_W, axis=0)
    g_w_out, d_w_out, nm_w_out, nv_w_out = _final_sum_adamw(sums_out, far_out, chip, w_out[0], m_w_out[0], v_w_out[0],
                                                            256, "adamw_out")
    flat = lambda a: a[0].T.reshape(SHARD_W, D_MODEL // 128, 128)
    unflat = lambda a: a.reshape(SHARD_W, D_MODEL).T
    d_flat, m_flat, v_flat = _adamw_rows(g_in_t.reshape(SHARD_W, D_MODEL // 128, 128), flat(w_in), flat(m_w_in),
                                         flat(v_w_in), 450, "adamw_in")
    g_w_in, d_w_in, nm_w_in, nv_w_in = g_in_t.T, unflat(d_flat), unflat(m_flat), unflat(v_flat)

    tot = _unpack(small_tot, pieces)
    g_norm_g, g_b_gk_f, g_b_gk_b, g_gla, g_conv_b, g_final = tot[:6]
    g_wgk_f = lax.dynamic_slice_in_dim(tot[6], me * wgk_cols, wgk_cols, axis=1)[None]
    g_wgk_b = lax.dynamic_slice_in_dim(tot[7], me * wgk_cols, wgk_cols, axis=1)[None]
    g_conv_w = lax.dynamic_slice_in_dim(tot[8], me * 128, 128, axis=1)[None]
    loss = tot[9][0]

    small_g = [g_norm_g, g_b_gk_f, g_b_gk_b, g_gla, g_conv_b, g_final, g_wgk_f, g_wgk_b, g_conv_w]
    small_w = [norm_g, b_gk_f, b_gk_b, gla_norm_g, conv_b, final_g, w_gk_f, w_gk_b, conv_w]
    small_m = [m_norm_g, m_b_gk_f, m_b_gk_b, m_gla_norm_g, m_conv_b, m_final_g, m_w_gk_f, m_w_gk_b, m_conv_w]
    small_v = [v_norm_g, v_b_gk_f, v_b_gk_b, v_gla_norm_g, v_conv_b, v_final_g, v_w_gk_f, v_w_gk_b, v_conv_w]
    d_s, m_s, v_s = _adamw_small(_pack(small_g), _pack(small_w), _pack(small_m), _pack(small_v))
    d_l, m_l, v_l = _unpack(d_s, small_w), _unpack(m_s, small_w), _unpack(v_s, small_w)

    def ordered(sm, big_in, big_out):
        return [sm[0], big_in[None], sm[6], sm[1], sm[7], sm[2], sm[3], sm[8], sm[4], big_out[None], sm[5]]

    grads = ordered(small_g, g_w_in, g_w_out)
    deltas = ordered(d_l, d_w_in, d_w_out)
    new_m = ordered(m_l, nm_w_in, nm_w_out)
    new_v = ordered(v_l, nv_w_in, nv_w_out)
    return (loss, grad_x2d[None], *grads, *deltas, *new_m, *new_v)
```
